```python
import math
import jax, jax.numpy as jnp
from jax import lax
import numpy as np

D_MODEL = 1024
BATCH = 8
SEQ = 4096
DEPTH = 2

MIX_WIDTH = D_MODEL
SGU_WIDTH = MIX_WIDTH // 2
SGU_HEADS = 4
SGU_HEAD_DIM = SGU_WIDTH // SGU_HEADS
CHUNK = 128
ATT_WIDTH = MIX_WIDTH - SGU_WIDTH
ATT_HEADS = 8
ATT_HEAD_DIM = ATT_WIDTH // ATT_HEADS
DILATED_PATTERNS = ((128, 1), (512, 4), (2048, 16))
ATT_BLOCK = 128
ROPE_THETA = 10000.0
D_FF = 2816
IN_WIDTH = 2 * SGU_WIDTH + 3 * ATT_WIDTH
N_ADA = 9
EPS = 1e-6

kernel_name = "hybrid_sgu_dilated_macaron_adaln"


def rmsnorm(x, g):
    xf = x.astype(jnp.float32)
    y = xf * lax.rsqrt(jnp.mean(xf * xf, axis=-1, keepdims=True) + EPS)
    return (y * g.astype(jnp.float32)).astype(x.dtype)


def modulate(h, shift, scale):
    return h * (1.0 + scale[:, None, :]) + shift[:, None, :]


def swiglu(y, w_gate, w_up, w_down):
    return (jax.nn.silu(y @ w_gate) * (y @ w_up)) @ w_down


def rope_tables(S, dh, dtype):
    inv = ROPE_THETA ** (-jnp.arange(0, dh, 2, dtype=jnp.float32) / dh)
    ang = jnp.arange(S, dtype=jnp.float32)[:, None] * inv[None, :]
    ang = jnp.concatenate([ang, ang], axis=-1)
    return jnp.cos(ang)[:, None, :].astype(dtype), jnp.sin(ang)[:, None, :].astype(dtype)


def apply_rope(t, cos, sin):
    half = t.shape[-1] // 2
    rot = jnp.concatenate([-t[..., half:], t[..., :half]], axis=-1)
    return t * cos + rot * sin


def spatial_gating(u, v, ln_g, ln_b, w_s, b_s):
    B, S, H, dh = u.shape
    u = jax.nn.gelu(u)
    v = jax.nn.gelu(v)
    vf = v.astype(jnp.float32)
    mu = jnp.mean(vf, axis=-1, keepdims=True)
    var = jnp.mean(jnp.square(vf - mu), axis=-1, keepdims=True)
    vn = ((vf - mu) * lax.rsqrt(var + EPS)).astype(v.dtype) * ln_g + ln_b
    vc = vn.reshape(B, S // CHUNK, CHUNK, H, dh)
    causal = jnp.tril(jnp.ones((CHUNK, CHUNK), dtype=bool))
    ws = jnp.where(causal[None], w_s, jnp.zeros_like(w_s))
    z = jnp.einsum('hij,bnjhc->bnihc', ws, vc) + b_s.T[None, None, :, :, None]
    return u * z.reshape(B, S, H, dh)


def dilated_branch(q, k, v, window, dil):
    B, S, H, dh = q.shape
    n_back = window // dil
    span = dil * ATT_BLOCK
    S_pad = -(-S // span) * span
    L = S_pad // dil
    nb = L // ATT_BLOCK

    def to_blocks(t):
        t = jnp.pad(t, ((0, 0), (0, S_pad - S), (0, 0), (0, 0)))
        t = t.reshape(B, L, dil, H, dh).transpose(0, 2, 3, 1, 4)
        return t.reshape(B, dil, H, nb, ATT_BLOCK, dh)

    def with_prev(t):
        prev = jnp.concatenate([jnp.zeros_like(t[:, :, :, :1]), t[:, :, :, :-1]], axis=3)
        return jnp.concatenate([prev, t], axis=4)

    qb = to_blocks(q)
    kk = with_prev(to_blocks(k))
    vv = with_prev(to_blocks(v))
    s = jnp.einsum('brhnqd,brhnkd->brhnqk', qb, kk,
                   preferred_element_type=jnp.float32) * (1.0 / math.sqrt(dh))
    qi = jnp.arange(ATT_BLOCK)[:, None]
    ki = jnp.arange(2 * ATT_BLOCK)[None, :]
    diff = qi + ATT_BLOCK - ki
    band = (diff >= 0) & (diff <= n_back)
    valid = (jnp.arange(nb)[:, None, None] > 0) | (ki[None] >= ATT_BLOCK)
    mask = band[None] & valid
    s = jnp.where(mask, s, -jnp.inf)
    m = jnp.max(s, axis=-1, keepdims=True)
    p = jnp.exp(s - m)
    den = jnp.sum(p, axis=-1, keepdims=True)
    o = jnp.einsum('brhnqk,brhnkd->brhnqd', p, vv.astype(jnp.float32)) / den
    lse = (m + jnp.log(den))[..., 0]
    o = o.reshape(B, dil, H, L, dh).transpose(0, 3, 1, 2, 4).reshape(B, S_pad, H, dh)[:, :S]
    lse = lse.reshape(B, dil, H, L).transpose(0, 3, 1, 2).reshape(B, S_pad, H)[:, :S]
    return o, lse


def dilated_mixture(q, k, v):
    outs, lses = [], []
    for window, dil in DILATED_PATTERNS:
        o, lse = dilated_branch(q, k, v, window, dil)
        outs.append(o)
        lses.append(lse)
    w = jax.nn.softmax(jnp.stack(lses, axis=0), axis=0)
    o = jnp.sum(w[..., None] * jnp.stack(outs, axis=0), axis=0)
    return o.astype(q.dtype)


def _fwd_setup_inputs(seed: int = 0) -> dict:
    key = jax.random.key(seed)
    ks = jax.random.split(key, 20)
    f32 = jnp.float32
    nrm = lambda k, shape, scale: jax.random.normal(k, shape, f32) * scale
    return {
        "x": nrm(ks[0], (BATCH, SEQ, D_MODEL), 1.0),
        "c": nrm(ks[1], (BATCH, D_MODEL), 1.0),
        "ada_w": nrm(ks[2], (DEPTH, D_MODEL, N_ADA * D_MODEL), D_MODEL ** -0.5),
        "ada_b": nrm(ks[3], (DEPTH, N_ADA * D_MODEL), 0.02),
        "norm_g": 1.0 + nrm(ks[4], (DEPTH, 3, D_MODEL), 0.02),
        "ffn1_wg": nrm(ks[5], (DEPTH, D_MODEL, D_FF), D_MODEL ** -0.5),
        "ffn1_wu": nrm(ks[6], (DEPTH, D_MODEL, D_FF), D_MODEL ** -0.5),
        "ffn1_wd": nrm(ks[7], (DEPTH, D_FF, D_MODEL), D_FF ** -0.5),
        "ffn2_wg": nrm(ks[8], (DEPTH, D_MODEL, D_FF), D_MODEL ** -0.5),
        "ffn2_wu": nrm(ks[9], (DEPTH, D_MODEL, D_FF), D_MODEL ** -0.5),
        "ffn2_wd": nrm(ks[10], (DEPTH, D_FF, D_MODEL), D_FF ** -0.5),
        "w_in": nrm(ks[11], (DEPTH, D_MODEL, IN_WIDTH), D_MODEL ** -0.5),
        "sgu_ln_g": 1.0 + nrm(ks[12], (DEPTH, SGU_HEADS, SGU_HEAD_DIM), 0.02),
        "sgu_ln_b": nrm(ks[13], (DEPTH, SGU_HEADS, SGU_HEAD_DIM), 0.02),
        "sgu_w": nrm(ks[14], (DEPTH, SGU_HEADS, CHUNK, CHUNK), CHUNK ** -0.5),
        "sgu_b": 1.0 + nrm(ks[15], (DEPTH, SGU_HEADS, CHUNK), 0.02),
        "w_out": nrm(ks[16], (DEPTH, MIX_WIDTH, D_MODEL), MIX_WIDTH ** -0.5),
        "final_g": 1.0 + nrm(ks[17], (D_MODEL,), 0.02),
    }


def _fwd_reference(x, c, ada_w, ada_b, norm_g, ffn1_wg, ffn1_wu, ffn1_wd, ffn2_wg, ffn2_wu, ffn2_wd,
              w_in, sgu_ln_g, sgu_ln_b, sgu_w, sgu_b, w_out, final_g):
    B, S, D = x.shape
    cos, sin = rope_tables(S, ATT_HEAD_DIM, x.dtype)
    c_act = jax.nn.silu(c)
    h = x
    for l in range(DEPTH):
        mod = c_act @ ada_w[l] + ada_b[l]
        sh1, sc1, g1, sh2, sc2, g2, sh3, sc3, g3 = jnp.split(mod, N_ADA, axis=-1)

        y = modulate(rmsnorm(h, norm_g[l, 0]), sh1, sc1)
        h = h + 0.5 * g1[:, None, :] * swiglu(y, ffn1_wg[l], ffn1_wu[l], ffn1_wd[l])

        y = modulate(rmsnorm(h, norm_g[l, 1]), sh2, sc2)
        proj = y @ w_in[l]
        u_a, v_a, q_b, k_b, v_b = jnp.split(
            proj, [SGU_WIDTH, 2 * SGU_WIDTH, 2 * SGU_WIDTH + ATT_WIDTH, 2 * SGU_WIDTH + 2 * ATT_WIDTH], axis=-1)
        u_a = u_a.reshape(B, S, SGU_HEADS, SGU_HEAD_DIM)
        v_a = v_a.reshape(B, S, SGU_HEADS, SGU_HEAD_DIM)
        out_a = spatial_gating(u_a, v_a, sgu_ln_g[l], sgu_ln_b[l], sgu_w[l], sgu_b[l])
        q_b = apply_rope(q_b.reshape(B, S, ATT_HEADS, ATT_HEAD_DIM), cos, sin)
        k_b = apply_rope(k_b.reshape(B, S, ATT_HEADS, ATT_HEAD_DIM), cos, sin)
        v_b = v_b.reshape(B, S, ATT_HEADS, ATT_HEAD_DIM)
        out_b = dilated_mixture(q_b, k_b, v_b)
        mixed = jnp.concatenate([out_a.reshape(B, S, SGU_WIDTH), out_b.reshape(B, S, ATT_WIDTH)], axis=-1)
        h = h + g2[:, None, :] * (mixed @ w_out[l])

        y = modulate(rmsnorm(h, norm_g[l, 2]), sh3, sc3)
        h = h + 0.5 * g3[:, None, :] * swiglu(y, ffn2_wg[l], ffn2_wu[l], ffn2_wd[l])
    return rmsnorm(h, final_g)


import jax as _jax
import jax.numpy as _jnp

TWIN_FORMAT = 'train_step'
FWD_PARAMS = ['x', 'c', 'ada_w', 'ada_b', 'norm_g', 'ffn1_wg', 'ffn1_wu', 'ffn1_wd', 'ffn2_wg', 'ffn2_wu', 'ffn2_wd', 'w_in', 'sgu_ln_g', 'sgu_ln_b', 'sgu_w', 'sgu_b', 'w_out', 'final_g']
TWIN_WEIGHTS = ['ada_w', 'ada_b', 'norm_g', 'ffn1_wg', 'ffn1_wu', 'ffn1_wd', 'ffn2_wg', 'ffn2_wu', 'ffn2_wd', 'w_in', 'sgu_ln_g', 'sgu_ln_b', 'sgu_w', 'sgu_b', 'w_out', 'final_g']
TWIN_DIFF_INPUT = 'x'
TWIN_INPUTS = ['x', 'c', 'ada_w', 'ada_b', 'norm_g', 'ffn1_wg', 'ffn1_wu', 'ffn1_wd', 'ffn2_wg', 'ffn2_wu', 'ffn2_wd', 'w_in', 'sgu_ln_g', 'sgu_ln_b', 'sgu_w', 'sgu_b', 'w_out', 'final_g', 'loss_target', 'm_ada_w', 'm_ada_b', 'm_norm_g', 'm_ffn1_wg', 'm_ffn1_wu', 'm_ffn1_wd', 'm_ffn2_wg', 'm_ffn2_wu', 'm_ffn2_wd', 'm_w_in', 'm_sgu_ln_g', 'm_sgu_ln_b', 'm_sgu_w', 'm_sgu_b', 'm_w_out', 'm_final_g', 'v_ada_w', 'v_ada_b', 'v_norm_g', 'v_ffn1_wg', 'v_ffn1_wu', 'v_ffn1_wd', 'v_ffn2_wg', 'v_ffn2_wu', 'v_ffn2_wd', 'v_w_in', 'v_sgu_ln_g', 'v_sgu_ln_b', 'v_sgu_w', 'v_sgu_b', 'v_w_out', 'v_final_g']
TWIN_OUTPUTS = ['loss', 'grad_x', 'grad_ada_w', 'grad_ada_b', 'grad_norm_g', 'grad_ffn1_wg', 'grad_ffn1_wu', 'grad_ffn1_wd', 'grad_ffn2_wg', 'grad_ffn2_wu', 'grad_ffn2_wd', 'grad_w_in', 'grad_sgu_ln_g', 'grad_sgu_ln_b', 'grad_sgu_w', 'grad_sgu_b', 'grad_w_out', 'grad_final_g', 'delta_ada_w', 'delta_ada_b', 'delta_norm_g', 'delta_ffn1_wg', 'delta_ffn1_wu', 'delta_ffn1_wd', 'delta_ffn2_wg', 'delta_ffn2_wu', 'delta_ffn2_wd', 'delta_w_in', 'delta_sgu_ln_g', 'delta_sgu_ln_b', 'delta_sgu_w', 'delta_sgu_b', 'delta_w_out', 'delta_final_g', 'new_m_ada_w', 'new_m_ada_b', 'new_m_norm_g', 'new_m_ffn1_wg', 'new_m_ffn1_wu', 'new_m_ffn1_wd', 'new_m_ffn2_wg', 'new_m_ffn2_wu', 'new_m_ffn2_wd', 'new_m_w_in', 'new_m_sgu_ln_g', 'new_m_sgu_ln_b', 'new_m_sgu_w', 'new_m_sgu_b', 'new_m_w_out', 'new_m_final_g', 'new_v_ada_w', 'new_v_ada_b', 'new_v_norm_g', 'new_v_ffn1_wg', 'new_v_ffn1_wu', 'new_v_ffn1_wd', 'new_v_ffn2_wg', 'new_v_ffn2_wu', 'new_v_ffn2_wd', 'new_v_w_in', 'new_v_sgu_ln_g', 'new_v_sgu_ln_b', 'new_v_sgu_w', 'new_v_sgu_b', 'new_v_w_out', 'new_v_final_g']
TWIN_LEAF_KINDS = {'loss': 'loss', 'grad_x': 'grad_x', 'grad_ada_w': 'grad_w', 'grad_ada_b': 'grad_w', 'grad_norm_g': 'grad_w', 'grad_ffn1_wg': 'grad_w', 'grad_ffn1_wu': 'grad_w', 'grad_ffn1_wd': 'grad_w', 'grad_ffn2_wg': 'grad_w', 'grad_ffn2_wu': 'grad_w', 'grad_ffn2_wd': 'grad_w', 'grad_w_in': 'grad_w', 'grad_sgu_ln_g': 'grad_w', 'grad_sgu_ln_b': 'grad_w', 'grad_sgu_w': 'grad_w', 'grad_sgu_b': 'grad_w', 'grad_w_out': 'grad_w', 'grad_final_g': 'grad_w', 'delta_ada_w': 'delta_w', 'delta_ada_b': 'delta_w', 'delta_norm_g': 'delta_w', 'delta_ffn1_wg': 'delta_w', 'delta_ffn1_wu': 'delta_w', 'delta_ffn1_wd': 'delta_w', 'delta_ffn2_wg': 'delta_w', 'delta_ffn2_wu': 'delta_w', 'delta_ffn2_wd': 'delta_w', 'delta_w_in': 'delta_w', 'delta_sgu_ln_g': 'delta_w', 'delta_sgu_ln_b': 'delta_w', 'delta_sgu_w': 'delta_w', 'delta_sgu_b': 'delta_w', 'delta_w_out': 'delta_w', 'delta_final_g': 'delta_w', 'new_m_ada_w': 'new_m', 'new_m_ada_b': 'new_m', 'new_m_norm_g': 'new_m', 'new_m_ffn1_wg': 'new_m', 'new_m_ffn1_wu': 'new_m', 'new_m_ffn1_wd': 'new_m', 'new_m_ffn2_wg': 'new_m', 'new_m_ffn2_wu': 'new_m', 'new_m_ffn2_wd': 'new_m', 'new_m_w_in': 'new_m', 'new_m_sgu_ln_g': 'new_m', 'new_m_sgu_ln_b': 'new_m', 'new_m_sgu_w': 'new_m', 'new_m_sgu_b': 'new_m', 'new_m_w_out': 'new_m', 'new_m_final_g': 'new_m', 'new_v_ada_w': 'new_v', 'new_v_ada_b': 'new_v', 'new_v_norm_g': 'new_v', 'new_v_ffn1_wg': 'new_v', 'new_v_ffn1_wu': 'new_v', 'new_v_ffn1_wd': 'new_v', 'new_v_ffn2_wg': 'new_v', 'new_v_ffn2_wu': 'new_v', 'new_v_ffn2_wd': 'new_v', 'new_v_w_in': 'new_v', 'new_v_sgu_ln_g': 'new_v', 'new_v_sgu_ln_b': 'new_v', 'new_v_sgu_w': 'new_v', 'new_v_sgu_b': 'new_v', 'new_v_w_out': 'new_v', 'new_v_final_g': 'new_v'}


def _forward(args):
    return _fwd_reference(*[args[k] for k in FWD_PARAMS])


def _output_shape():
    out = _jax.eval_shape(lambda: _forward(_fwd_setup_inputs(0)))
    return out.shape, out.dtype

N_MICROBATCH = 1
ADAM_LR = 0.001
ADAM_B1 = 0.9
ADAM_B2 = 0.999
ADAM_EPS = 1e-08
ADAM_WD = 0.01
ADAM_STEP = 10
PER_EXAMPLE_BATCH_AXIS = {'x': 0, 'c': 0, 'loss_target': 0}
SHARED_INPUTS = []
_WEIGHT_DTYPES = {'ada_w': _jnp.float32, 'ada_b': _jnp.float32, 'norm_g': _jnp.float32, 'ffn1_wg': _jnp.float32, 'ffn1_wu': _jnp.float32, 'ffn1_wd': _jnp.float32, 'ffn2_wg': _jnp.float32, 'ffn2_wu': _jnp.float32, 'ffn2_wd': _jnp.float32, 'w_in': _jnp.float32, 'sgu_ln_g': _jnp.float32, 'sgu_ln_b': _jnp.float32, 'sgu_w': _jnp.float32, 'sgu_b': _jnp.float32, 'w_out': _jnp.float32, 'final_g': _jnp.float32}
MOMENT_SCALE = {'ada_w': 5.851183e-02, 'ada_b': 9.958459e-02, 'norm_g': 7.205603e-02, 'ffn1_wg': 3.426566e-02, 'ffn1_wu': 3.348080e-02, 'ffn1_wd': 5.536314e-02, 'ffn2_wg': 2.987171e-02, 'ffn2_wu': 2.940203e-02, 'ffn2_wd': 4.848697e-02, 'w_in': 6.856377e-02, 'sgu_ln_g': 5.215788e-02, 'sgu_ln_b': 5.187924e-02, 'sgu_w': 5.570354e-02, 'sgu_b': 8.348843e-02, 'w_out': 9.796163e-02, 'final_g': 3.291741e+01}


def _to_microbatches(a, axis):
    t = _jnp.moveaxis(a, axis, 0)
    t = t.reshape((N_MICROBATCH, t.shape[0] // N_MICROBATCH) + t.shape[1:])
    return _jnp.moveaxis(t, 1, axis + 1)


def setup_inputs(seed: int = 0) -> dict:
    inp = _fwd_setup_inputs(seed)
    key = _jax.random.fold_in(_jax.random.key(seed), 7919)
    shape, _ = _output_shape()
    out = dict(inp)
    out["loss_target"] = _jax.random.normal(_jax.random.fold_in(key, 0), shape, _jnp.float32)
    for i, name in enumerate(TWIN_WEIGHTS):
        w = inp[name].astype(_jnp.float32)
        if MOMENT_SCALE is None:
            s = _jnp.sqrt(_jnp.mean(_jnp.square(w)) + 1e-30)
        else:
            s = MOMENT_SCALE[name]
        km, kv = _jax.random.split(_jax.random.fold_in(key, i + 1))
        out[name] = w
        out["m_" + name] = s * _jax.random.normal(km, w.shape, _jnp.float32)
        out["v_" + name] = (s * s) * _jax.random.uniform(kv, w.shape, _jnp.float32, 0.5, 1.5)
    if N_MICROBATCH > 1:
        for name, axis in PER_EXAMPLE_BATCH_AXIS.items():
            out[name] = _to_microbatches(out[name], axis)
    return {'x': out['x'], 'c': out['c'], 'ada_w': out['ada_w'], 'ada_b': out['ada_b'], 'norm_g': out['norm_g'], 'ffn1_wg': out['ffn1_wg'], 'ffn1_wu': out['ffn1_wu'], 'ffn1_wd': out['ffn1_wd'], 'ffn2_wg': out['ffn2_wg'], 'ffn2_wu': out['ffn2_wu'], 'ffn2_wd': out['ffn2_wd'], 'w_in': out['w_in'], 'sgu_ln_g': out['sgu_ln_g'], 'sgu_ln_b': out['sgu_ln_b'], 'sgu_w': out['sgu_w'], 'sgu_b': out['sgu_b'], 'w_out': out['w_out'], 'final_g': out['final_g'], 'loss_target': out['loss_target'], 'm_ada_w': out['m_ada_w'], 'm_ada_b': out['m_ada_b'], 'm_norm_g': out['m_norm_g'], 'm_ffn1_wg': out['m_ffn1_wg'], 'm_ffn1_wu': out['m_ffn1_wu'], 'm_ffn1_wd': out['m_ffn1_wd'], 'm_ffn2_wg': out['m_ffn2_wg'], 'm_ffn2_wu': out['m_ffn2_wu'], 'm_ffn2_wd': out['m_ffn2_wd'], 'm_w_in': out['m_w_in'], 'm_sgu_ln_g': out['m_sgu_ln_g'], 'm_sgu_ln_b': out['m_sgu_ln_b'], 'm_sgu_w': out['m_sgu_w'], 'm_sgu_b': out['m_sgu_b'], 'm_w_out': out['m_w_out'], 'm_final_g': out['m_final_g'], 'v_ada_w': out['v_ada_w'], 'v_ada_b': out['v_ada_b'], 'v_norm_g': out['v_norm_g'], 'v_ffn1_wg': out['v_ffn1_wg'], 'v_ffn1_wu': out['v_ffn1_wu'], 'v_ffn1_wd': out['v_ffn1_wd'], 'v_ffn2_wg': out['v_ffn2_wg'], 'v_ffn2_wu': out['v_ffn2_wu'], 'v_ffn2_wd': out['v_ffn2_wd'], 'v_w_in': out['v_w_in'], 'v_sgu_ln_g': out['v_sgu_ln_g'], 'v_sgu_ln_b': out['v_sgu_ln_b'], 'v_sgu_w': out['v_sgu_w'], 'v_sgu_b': out['v_sgu_b'], 'v_w_out': out['v_w_out'], 'v_final_g': out['v_final_g']}


def _loss(weights, diff, rest, loss_target):
    with _jax.named_scope("forward"):
        args = {**rest, TWIN_DIFF_INPUT: diff, **{k: w.astype(_WEIGHT_DTYPES[k]) for k, w in weights.items()}}
        y = _forward(args)
    with _jax.named_scope("loss_head"):
        err = _jnp.square(y.astype(_jnp.float32) - loss_target)
        return 0.5 * _jnp.sum(_jnp.mean(err, axis=-1)) if err.ndim else 0.5 * err


def _adamw(w, g, m, v):
    m = ADAM_B1 * m + (1.0 - ADAM_B1) * g
    v = ADAM_B2 * v + (1.0 - ADAM_B2) * _jnp.square(g)
    m_hat = m / (1.0 - ADAM_B1 ** ADAM_STEP)
    v_hat = v / (1.0 - ADAM_B2 ** ADAM_STEP)
    delta = -ADAM_LR * (m_hat / (_jnp.sqrt(v_hat) + ADAM_EPS) + ADAM_WD * w)
    return delta, m, v


def reference(x, c, ada_w, ada_b, norm_g, ffn1_wg, ffn1_wu, ffn1_wd, ffn2_wg, ffn2_wu, ffn2_wd, w_in, sgu_ln_g, sgu_ln_b, sgu_w, sgu_b, w_out, final_g, loss_target, m_ada_w, m_ada_b, m_norm_g, m_ffn1_wg, m_ffn1_wu, m_ffn1_wd, m_ffn2_wg, m_ffn2_wu, m_ffn2_wd, m_w_in, m_sgu_ln_g, m_sgu_ln_b, m_sgu_w, m_sgu_b, m_w_out, m_final_g, v_ada_w, v_ada_b, v_norm_g, v_ffn1_wg, v_ffn1_wu, v_ffn1_wd, v_ffn2_wg, v_ffn2_wu, v_ffn2_wd, v_w_in, v_sgu_ln_g, v_sgu_ln_b, v_sgu_w, v_sgu_b, v_w_out, v_final_g):
    given = dict(x=x, c=c, ada_w=ada_w, ada_b=ada_b, norm_g=norm_g, ffn1_wg=ffn1_wg, ffn1_wu=ffn1_wu, ffn1_wd=ffn1_wd, ffn2_wg=ffn2_wg, ffn2_wu=ffn2_wu, ffn2_wd=ffn2_wd, w_in=w_in, sgu_ln_g=sgu_ln_g, sgu_ln_b=sgu_ln_b, sgu_w=sgu_w, sgu_b=sgu_b, w_out=w_out, final_g=final_g, loss_target=loss_target, m_ada_w=m_ada_w, m_ada_b=m_ada_b, m_norm_g=m_norm_g, m_ffn1_wg=m_ffn1_wg, m_ffn1_wu=m_ffn1_wu, m_ffn1_wd=m_ffn1_wd, m_ffn2_wg=m_ffn2_wg, m_ffn2_wu=m_ffn2_wu, m_ffn2_wd=m_ffn2_wd, m_w_in=m_w_in, m_sgu_ln_g=m_sgu_ln_g, m_sgu_ln_b=m_sgu_ln_b, m_sgu_w=m_sgu_w, m_sgu_b=m_sgu_b, m_w_out=m_w_out, m_final_g=m_final_g, v_ada_w=v_ada_w, v_ada_b=v_ada_b, v_norm_g=v_norm_g, v_ffn1_wg=v_ffn1_wg, v_ffn1_wu=v_ffn1_wu, v_ffn1_wd=v_ffn1_wd, v_ffn2_wg=v_ffn2_wg, v_ffn2_wu=v_ffn2_wu, v_ffn2_wd=v_ffn2_wd, v_w_in=v_w_in, v_sgu_ln_g=v_sgu_ln_g, v_sgu_ln_b=v_sgu_ln_b, v_sgu_w=v_sgu_w, v_sgu_b=v_sgu_b, v_w_out=v_w_out, v_final_g=v_final_g)
    weights = {n: given[n] for n in TWIN_WEIGHTS}
    shared = {n: given[n] for n in SHARED_INPUTS}
    per_example = {n: given[n] for n in ['x', 'c']}
    grad_fn = _jax.value_and_grad(_loss, argnums=(0, 1))

    def one_microbatch(ex, loss_target):
        ex = dict(ex)
        diff = ex.pop(TWIN_DIFF_INPUT)
        return grad_fn(weights, diff, {**shared, **ex}, loss_target)

    if N_MICROBATCH == 1:
        loss, (grad_w, grad_x) = one_microbatch(per_example, given["loss_target"])
    else:
        def body(carry, xs):
            loss_sum, grad_sum = carry
            l_k, (gw_k, gx_k) = one_microbatch(xs[0], xs[1])
            with _jax.named_scope("update"):
                return (loss_sum + l_k, _jax.tree.map(_jnp.add, grad_sum, gw_k)), gx_k

        init = (_jnp.zeros((), _jnp.float32), _jax.tree.map(_jnp.zeros_like, weights))
        (loss, grad_w), grad_x = _jax.lax.scan(body, init, (per_example, given["loss_target"]))
    with _jax.named_scope("update"):
        delta_w, new_m, new_v = {}, {}, {}
        for n in TWIN_WEIGHTS:
            delta_w[n], new_m[n], new_v[n] = _adamw(weights[n], grad_w[n], given["m_" + n], given["v_" + n])
    return (loss, grad_x, *[grad_w[n] for n in TWIN_WEIGHTS], *[delta_w[n] for n in TWIN_WEIGHTS],
            *[new_m[n] for n in TWIN_WEIGHTS], *[new_v[n] for n in TWIN_WEIGHTS])
```

```python
import math

import jax
import jax.numpy as jnp
from jax import lax
from jax.experimental import pallas as pl
from jax.experimental.pallas import tpu as pltpu

F32 = jnp.float32
BF16 = jnp.bfloat16
EPS = 1e-6
SGU_HEADS = 4
HEAD_LANES = 128
ATT_DH = 64
ATT_BLOCK = 128
MIX_HALF = SGU_HEADS * HEAD_LANES
DILATIONS = (1, 4, 16)
ROPE_THETA = 10000.0
N_ADA = 9
ADAM_LR, ADAM_B1, ADAM_B2, ADAM_EPS, ADAM_WD, ADAM_STEP = 0.001, 0.9, 0.999, 1e-08, 0.01, 10
NEG = -1e30
V7X_VMEM_BYTES = 64 * 1024 * 1024
VMEM_LIMIT = V7X_VMEM_BYTES * 7 // 8
MESH = pl.DeviceIdType.MESH
N_DEV = 8
N_CHIP = 4


def _tile(n, cap, mult):
    if n <= cap:
        return n
    t = (cap // mult) * mult
    while t >= mult:
        if n % t == 0:
            return t
        t -= mult
    raise ValueError((n, cap, mult))


def _params(dims=None):
    return pltpu.CompilerParams(dimension_semantics=dims, vmem_limit_bytes=VMEM_LIMIT)


def _wspec(w, rows, idx):
    arr, lead = w
    return pl.BlockSpec((None,) * len(lead) + (rows, arr.shape[-1]), lambda *g: tuple(lead) + (idx(*g), 0))


def _wrows(w):
    return w[0].shape[-2]


def _nt(a, b):
    return lax.dot_general(a, b, (((1,), (1,)), ((), ())), preferred_element_type=F32)


def _tn(a, b):
    return lax.dot_general(a, b, (((0,), (0,)), ((), ())), preferred_element_type=F32)


def _nn(a, b):
    return jnp.dot(a, b, preferred_element_type=F32)


def _sigmoid(x):
    return 1.0 / (1.0 + jnp.exp(-x))


_GELU_K = math.sqrt(2.0 / math.pi)
_GELU_C = 0.044715


def _gelu(x):
    t = jnp.tanh(_GELU_K * (x + _GELU_C * x * x * x))
    return 0.5 * x * (1.0 + t)


def _gelu_and_grad(x):
    x2 = x * x
    t = jnp.tanh(_GELU_K * (x + _GELU_C * x * x2))
    g = 0.5 * x * (1.0 + t)
    dg = 0.5 * (1.0 + t) + 0.5 * x * (1.0 - t * t) * (_GELU_K * (1.0 + 3.0 * _GELU_C * x2))
    return g, dg


def normmod_fwd(h, ng, i_n, mod, i_sh, i_sc, name):
    T, D = h.shape
    tm = _tile(T, 512, 8)

    def body(h_ref, ng_ref, mod_ref, y_ref):
        x = h_ref[...]
        r = lax.rsqrt(jnp.mean(x * x, axis=-1, keepdims=True) + EPS)
        y = (x * r) * ng_ref[i_n:i_n + 1, :]
        y_ref[...] = (y * (1.0 + mod_ref[i_sc:i_sc + 1, :]) + mod_ref[i_sh:i_sh + 1, :]).astype(BF16)

    return pl.pallas_call(
        body, name=name, grid=(T // tm,),
        in_specs=[pl.BlockSpec((tm, D), lambda i: (i, 0)),
                  pl.BlockSpec(ng.shape, lambda i: (0, 0)),
                  pl.BlockSpec(mod.shape, lambda i: (0, 0))],
        out_specs=pl.BlockSpec((tm, D), lambda i: (i, 0)),
        out_shape=jax.ShapeDtypeStruct((T, D), BF16),
        compiler_params=_params(("parallel",)),
    )(h, ng, mod)


def ffn_up(y, wgT, wuT, name):
    T, D = y.shape
    F = _wrows(wgT)
    tm = _tile(T, 512, 16)
    tf = _tile(F, 1408, 128)

    def body(y_ref, wg_ref, wu_ref, a_ref, b_ref, s_ref):
        yv = y_ref[...]
        a = _nt(yv, wg_ref[...])
        b = _nt(yv, wu_ref[...])
        a_ref[...] = a.astype(BF16)
        b_ref[...] = b.astype(BF16)
        s_ref[...] = (a * _sigmoid(a) * b).astype(BF16)

    act = jax.ShapeDtypeStruct((T, F), BF16)
    return pl.pallas_call(
        body, name=name, grid=(F // tf, T // tm),
        in_specs=[pl.BlockSpec((tm, D), lambda j, i: (i, 0)),
                  _wspec(wgT, tf, lambda j, i: j),
                  _wspec(wuT, tf, lambda j, i: j)],
        out_specs=[pl.BlockSpec((tm, tf), lambda j, i: (i, j))] * 3,
        out_shape=[act, act, act],
        compiler_params=_params(("parallel", "parallel")),
    )(y, wgT[0], wuT[0])


def resid_matmul(xs, w, h, mod, i_g, coef, name):
    T, D = h.shape
    kb = xs[0].shape[1]
    assert all(x.shape == (T, kb) for x in xs) and _wrows(w) == kb * len(xs)
    tm = _tile(T, 512, 16)
    nx = len(xs)

    def body(*refs):
        x_refs, w_refs = refs[:nx], refs[nx:2 * nx]
        h_ref, mod_ref, hn_ref, o_ref = refs[2 * nx:]
        o = _nn(x_refs[0][...], w_refs[0][...])
        for xr, wr in zip(x_refs[1:], w_refs[1:]):
            o = o + _nn(xr[...], wr[...])
        o_ref[...] = o.astype(BF16)
        hn_ref[...] = h_ref[...] + (coef * mod_ref[i_g:i_g + 1, :]) * o

    return pl.pallas_call(
        body, name=name, grid=(T // tm,),
        in_specs=([pl.BlockSpec((tm, kb), lambda i: (i, 0))] * nx
                  + [_wspec(w, kb, lambda i, p=p: p) for p in range(nx)]
                  + [pl.BlockSpec((tm, D), lambda i: (i, 0)),
                     pl.BlockSpec(mod.shape, lambda i: (0, 0))]),
        out_specs=[pl.BlockSpec((tm, D), lambda i: (i, 0))] * 2,
        out_shape=[jax.ShapeDtypeStruct((T, D), F32), jax.ShapeDtypeStruct((T, D), BF16)],
        compiler_params=_params(("parallel",)),
    )(*xs, *([w[0]] * nx), h, mod)


def gate_bwd(dh, o, mod, i_g, coef, name):
    T, D = dh.shape
    tm = _tile(T, 512, 16)

    def body(dh_ref, o_ref, mod_ref, do_ref, red_ref):
        d = dh_ref[...]
        do_ref[...] = (d * (coef * mod_ref[i_g:i_g + 1, :])).astype(BF16)

        @pl.when(pl.program_id(0) == 0)
        def _():
            red_ref[...] = jnp.zeros_like(red_ref)

        red_ref[0:1, :] += coef * jnp.sum(d * o_ref[...].astype(F32), axis=0, keepdims=True)

    return pl.pallas_call(
        body, name=name, grid=(T // tm,),
        in_specs=[pl.BlockSpec((tm, D), lambda i: (i, 0)),
                  pl.BlockSpec((tm, D), lambda i: (i, 0)),
                  pl.BlockSpec(mod.shape, lambda i: (0, 0))],
        out_specs=[pl.BlockSpec((tm, D), lambda i: (i, 0)), pl.BlockSpec((8, D), lambda i: (0, 0))],
        out_shape=[jax.ShapeDtypeStruct((T, D), BF16), jax.ShapeDtypeStruct((8, D), F32)],
        compiler_params=_params(("arbitrary",)),
    )(dh, o, mod)


def ffn_bwd_mid(do, wd, a, b, name):
    T, D = do.shape
    F = _wrows(wd)
    tm = _tile(T, 512, 16)
    tf = _tile(F, 1408, 128)

    def body(do_ref, wd_ref, a_ref, b_ref, da_ref, db_ref):
        ds = _nt(do_ref[...], wd_ref[...])
        av = a_ref[...].astype(F32)
        bv = b_ref[...].astype(F32)
        sig = _sigmoid(av)
        da_ref[...] = (ds * bv * (sig * (1.0 + av * (1.0 - sig)))).astype(BF16)
        db_ref[...] = (ds * (av * sig)).astype(BF16)

    act = jax.ShapeDtypeStruct((T, F), BF16)
    return pl.pallas_call(
        body, name=name, grid=(F // tf, T // tm),
        in_specs=[pl.BlockSpec((tm, D), lambda j, i: (i, 0)),
                  _wspec(wd, tf, lambda j, i: j),
                  pl.BlockSpec((tm, tf), lambda j, i: (i, j)),
                  pl.BlockSpec((tm, tf), lambda j, i: (i, j))],
        out_specs=[pl.BlockSpec((tm, tf), lambda j, i: (i, j))] * 2,
        out_shape=[act, act],
        compiler_params=_params(("parallel", "parallel")),
    )(do, wd[0], a, b)


def dy_normbwd(pairs, h, dhp, ng, i_n, mod, i_sc, name):
    T, D = h.shape
    tm = _tile(T, 256, 16)
    npair = len(pairs)

    def body(*refs):
        x_refs, w_refs = refs[:npair], refs[npair:2 * npair]
        h_ref, dhp_ref, ng_ref, mod_ref, dh_ref, red_ref = refs[2 * npair:]
        dy = _nn(x_refs[0][...], w_refs[0][...])
        for xr, wr in zip(x_refs[1:], w_refs[1:]):
            dy = dy + _nn(xr[...], wr[...])
        x = h_ref[...]
        r = lax.rsqrt(jnp.mean(x * x, axis=-1, keepdims=True) + EPS)
        n = x * r
        gn = ng_ref[i_n:i_n + 1, :]
        dnh = dy * (1.0 + mod_ref[i_sc:i_sc + 1, :])

        @pl.when(pl.program_id(0) == 0)
        def _():
            red_ref[...] = jnp.zeros_like(red_ref)

        red_ref[0:1, :] += jnp.sum(dy, axis=0, keepdims=True)
        red_ref[1:2, :] += jnp.sum(dy * (n * gn), axis=0, keepdims=True)
        red_ref[2:3, :] += jnp.sum(dnh * n, axis=0, keepdims=True)
        dn = dnh * gn
        dh_ref[...] = dhp_ref[...] + r * (dn - n * jnp.mean(dn * n, axis=-1, keepdims=True))

    in_specs = ([pl.BlockSpec((tm, kb), lambda i, c=c: (i, c)) for (_, c, _, _, kb) in pairs]
                + [_wspec(w, kb, lambda i, r=r: r) for (_, _, w, r, kb) in pairs]
                + [pl.BlockSpec((tm, D), lambda i: (i, 0)),
                   pl.BlockSpec((tm, D), lambda i: (i, 0)),
                   pl.BlockSpec(ng.shape, lambda i: (0, 0)),
                   pl.BlockSpec(mod.shape, lambda i: (0, 0))])
    return pl.pallas_call(
        body, name=name, grid=(T // tm,), in_specs=in_specs,
        out_specs=[pl.BlockSpec((tm, D), lambda i: (i, 0)), pl.BlockSpec((8, D), lambda i: (0, 0))],
        out_shape=[jax.ShapeDtypeStruct((T, D), F32), jax.ShapeDtypeStruct((8, D), F32)],
        compiler_params=_params(("arbitrary",)),
    )(*[p[0] for p in pairs], *[p[2][0] for p in pairs], h, dhp, ng, mod)


def matmul_tn(a, a_col, ma, b, name):
    T, N = b.shape
    tmo = _tile(ma, 1408, 128)
    nmo = ma // tmo
    tk = _tile(T, 512, 16)
    nk = T // tk

    def body(a_ref, b_ref, o_ref, acc_ref):
        k = pl.program_id(1)

        @pl.when(k == 0)
        def _():
            acc_ref[...] = jnp.zeros_like(acc_ref)

        acc_ref[...] += _tn(a_ref[...], b_ref[...])

        @pl.when(k == nk - 1)
        def _():
            o_ref[...] = acc_ref[...].astype(BF16)

    return pl.pallas_call(
        body, name=name, grid=(nmo, nk),
        in_specs=[pl.BlockSpec((tk, tmo), lambda j, k: (k, a_col * nmo + j)),
                  pl.BlockSpec((tk, N), lambda j, k: (k, 0))],
        out_specs=pl.BlockSpec((tmo, N), lambda j, k: (j, 0)),
        out_shape=jax.ShapeDtypeStruct((ma, N), BF16),
        scratch_shapes=[pltpu.VMEM((tmo, N), F32)],
        compiler_params=_params(("parallel", "arbitrary")),
    )(a, b)


def matmul_nt(x, w, name):
    T, K = x.shape
    N = _wrows(w)
    tm = _tile(T, 512, 16)
    tn = _tile(N, 1280, 128)

    def body(x_ref, w_ref, o_ref):
        o_ref[...] = _nt(x_ref[...], w_ref[...]).astype(BF16)

    return pl.pallas_call(
        body, name=name, grid=(N // tn, T // tm),
        in_specs=[pl.BlockSpec((tm, K), lambda j, i: (i, 0)), _wspec(w, tn, lambda j, i: j)],
        out_specs=pl.BlockSpec((tm, tn), lambda j, i: (i, j)),
        out_shape=jax.ShapeDtypeStruct((T, N), BF16),
        compiler_params=_params(("parallel", "parallel")),
    )(x, w[0])


def _sgu_head_fwd(u, v, lng, lnb):
    gu, dgu = _gelu_and_grad(u)
    gv, dgv = _gelu_and_grad(v)
    mu = jnp.mean(gv, axis=-1, keepdims=True)
    xc = gv - mu
    rstd = lax.rsqrt(jnp.mean(xc * xc, axis=-1, keepdims=True) + EPS)
    xhat = xc * rstd
    vn = xhat * lng + lnb
    return gu, dgu, dgv, rstd, xhat, vn


def _tril_mask():
    r = lax.broadcasted_iota(jnp.int32, (ATT_BLOCK, ATT_BLOCK), 0)
    c = lax.broadcasted_iota(jnp.int32, (ATT_BLOCK, ATT_BLOCK), 1)
    return c <= r


def _triu_mask():
    r = lax.broadcasted_iota(jnp.int32, (ATT_BLOCK, ATT_BLOCK), 0)
    c = lax.broadcasted_iota(jnp.int32, (ATT_BLOCK, ATT_BLOCK), 1)
    return r <= c


def sgu_fwd(proj, lng, lnb, w, bcol, name):
    T = proj.shape[0]
    tm = _tile(T, 512, 128)
    nch = tm // ATT_BLOCK

    def body(u_ref, v_ref, lng_ref, lnb_ref, w_ref, b_ref, o_ref):
        tril = _tril_mask()
        for hd in range(SGU_HEADS):
            sl = slice(hd * HEAD_LANES, (hd + 1) * HEAD_LANES)
            u = u_ref[:, sl].astype(F32)
            v = v_ref[:, sl].astype(F32)
            gu, _, _, _, _, vn = _sgu_head_fwd(u, v, lng_ref[:, sl], lnb_ref[:, sl])
            wm = jnp.where(tril, w_ref[hd], 0.0).astype(BF16)
            vnb = vn.astype(BF16)
            bc = b_ref[:, hd:hd + 1]
            for ch in range(nch):
                rs = slice(ch * ATT_BLOCK, (ch + 1) * ATT_BLOCK)
                z = _nn(wm, vnb[rs, :]) + bc
                o_ref[rs, sl] = (gu[rs, :] * z).astype(BF16)

    return pl.pallas_call(
        body, name=name, grid=(T // tm,),
        in_specs=[pl.BlockSpec((tm, MIX_HALF), lambda i: (i, 0)),
                  pl.BlockSpec((tm, MIX_HALF), lambda i: (i, 1)),
                  pl.BlockSpec((1, MIX_HALF), lambda i: (0, 0)),
                  pl.BlockSpec((1, MIX_HALF), lambda i: (0, 0)),
                  pl.BlockSpec(w.shape, lambda i: (0, 0, 0)),
                  pl.BlockSpec(bcol.shape, lambda i: (0, 0))],
        out_specs=pl.BlockSpec((tm, MIX_HALF), lambda i: (i, 0)),
        out_shape=jax.ShapeDtypeStruct((T, MIX_HALF), BF16),
        compiler_params=_params(("parallel",)),
    )(proj, proj, lng, lnb, w, bcol)


def sgu_bwd(proj, dmixed, lng, lnb, w, wt, bcol, name):
    T = proj.shape[0]
    tm = _tile(T, 512, 128)
    nch = tm // ATT_BLOCK
    nsteps = T // tm

    def body(u_ref, v_ref, g_ref, lng_ref, lnb_ref, w_ref, wt_ref, b_ref, duv_ref, dw_ref, dvec_ref, bacc_ref):
        step = pl.program_id(0)

        @pl.when(step == 0)
        def _():
            dw_ref[...] = jnp.zeros_like(dw_ref)
            dvec_ref[...] = jnp.zeros_like(dvec_ref)
            bacc_ref[...] = jnp.zeros_like(bacc_ref)

        tril = _tril_mask()
        triu = _triu_mask()
        for hd in range(SGU_HEADS):
            sl = slice(hd * HEAD_LANES, (hd + 1) * HEAD_LANES)
            u = u_ref[:, sl].astype(F32)
            v = v_ref[:, sl].astype(F32)
            lng_h = lng_ref[:, sl]
            gu, dgu, dgv, rstd, xhat, vn = _sgu_head_fwd(u, v, lng_h, lnb_ref[:, sl])
            wm = jnp.where(tril, w_ref[hd], 0.0).astype(BF16)
            wmt = jnp.where(triu, wt_ref[hd], 0.0).astype(BF16)
            vnb = vn.astype(BF16)
            bc = b_ref[:, hd:hd + 1]
            g = g_ref[:, sl].astype(F32)
            dw_acc = jnp.zeros((ATT_BLOCK, ATT_BLOCK), F32)
            b_acc = jnp.zeros((ATT_BLOCK, HEAD_LANES), F32)
            dvn_parts = []
            for ch in range(nch):
                rs = slice(ch * ATT_BLOCK, (ch + 1) * ATT_BLOCK)
                z = _nn(wm, vnb[rs, :]) + bc
                duv_ref[rs, sl] = (g[rs, :] * z * dgu[rs, :]).astype(BF16)
                dz = g[rs, :] * gu[rs, :]
                dzb = dz.astype(BF16)
                dvn_parts.append(_nn(wmt, dzb))
                dw_acc = dw_acc + _nt(dzb, vnb[rs, :])
                b_acc = b_acc + dz
            dvn = jnp.concatenate(dvn_parts, axis=0)
            dw_ref[hd] += jnp.where(tril, dw_acc, 0.0)
            bacc_ref[hd] += b_acc
            dvec_ref[0:1, sl] += jnp.sum(dvn * xhat, axis=0, keepdims=True)
            dvec_ref[1:2, sl] += jnp.sum(dvn, axis=0, keepdims=True)
            dxh = dvn * lng_h
            dgv_in = rstd * (dxh - jnp.mean(dxh, axis=-1, keepdims=True)
                             - xhat * jnp.mean(dxh * xhat, axis=-1, keepdims=True))
            duv_ref[:, MIX_HALF + hd * HEAD_LANES:MIX_HALF + (hd + 1) * HEAD_LANES] = (dgv_in * dgv).astype(BF16)

        @pl.when(step == nsteps - 1)
        def _():
            for hd in range(SGU_HEADS):
                sl = slice(hd * HEAD_LANES, (hd + 1) * HEAD_LANES)
                dvec_ref[2:3, sl] = jnp.sum(bacc_ref[hd].T, axis=0, keepdims=True)

    return pl.pallas_call(
        body, name=name, grid=(nsteps,),
        in_specs=[pl.BlockSpec((tm, MIX_HALF), lambda i: (i, 0)),
                  pl.BlockSpec((tm, MIX_HALF), lambda i: (i, 1)),
                  pl.BlockSpec((tm, MIX_HALF), lambda i: (i, 0)),
                  pl.BlockSpec((1, MIX_HALF), lambda i: (0, 0)),
                  pl.BlockSpec((1, MIX_HALF), lambda i: (0, 0)),
                  pl.BlockSpec(w.shape, lambda i: (0, 0, 0)),
                  pl.BlockSpec(w.shape, lambda i: (0, 0, 0)),
                  pl.BlockSpec(bcol.shape, lambda i: (0, 0))],
        out_specs=[pl.BlockSpec((tm, 2 * MIX_HALF), lambda i: (i, 0)),
                   pl.BlockSpec(w.shape, lambda i: (0, 0, 0)),
                   pl.BlockSpec((8, MIX_HALF), lambda i: (0, 0))],
        out_shape=[jax.ShapeDtypeStruct((T, 2 * MIX_HALF), BF16),
                   jax.ShapeDtypeStruct(w.shape, F32),
                   jax.ShapeDtypeStruct((8, MIX_HALF), F32)],
        scratch_shapes=[pltpu.VMEM((SGU_HEADS, ATT_BLOCK, HEAD_LANES), F32)],
        compiler_params=_params(("arbitrary",)),
    )(proj, proj, dmixed, lng, lnb, w, wt, bcol)


def _rot_half(t):
    lane = lax.broadcasted_iota(jnp.int32, t.shape, 1)
    first = (lane % ATT_DH) < (ATT_DH // 2)
    return jnp.where(first, -pltpu.roll(t, HEAD_LANES - ATT_DH // 2, 1), pltpu.roll(t, ATT_DH // 2, 1))


def rope_fwd(proj, cos, sin, name):
    T = proj.shape[0]
    tm = _tile(T, 512, 16)
    scale = 1.0 / math.sqrt(ATT_DH)

    def body(q_ref, k_ref, cos_ref, sin_ref, qo_ref, ko_ref):
        c = cos_ref[...]
        s = sin_ref[...]
        for hp in range(MIX_HALF // HEAD_LANES):
            sl = slice(hp * HEAD_LANES, (hp + 1) * HEAD_LANES)
            q = q_ref[:, sl].astype(F32)
            k = k_ref[:, sl].astype(F32)
            qo_ref[:, sl] = (scale * (q * c + _rot_half(q) * s)).astype(BF16)
            ko_ref[:, sl] = (k * c + _rot_half(k) * s).astype(BF16)

    out = jax.ShapeDtypeStruct((T, MIX_HALF), BF16)
    return pl.pallas_call(
        body, name=name, grid=(T // tm,),
        in_specs=[pl.BlockSpec((tm, MIX_HALF), lambda i: (i, 2)),
                  pl.BlockSpec((tm, MIX_HALF), lambda i: (i, 3)),
                  pl.BlockSpec((tm, HEAD_LANES), lambda i: (i, 0)),
                  pl.BlockSpec((tm, HEAD_LANES), lambda i: (i, 0))],
        out_specs=[pl.BlockSpec((tm, MIX_HALF), lambda i: (i, 0))] * 2,
        out_shape=[out, out],
        compiler_params=_params(("parallel",)),
    )(proj, proj, cos, sin)


def rope_bwd(dqs, dks, dvs, cos, sin, name):
    T = dqs[0].shape[0]
    tm = _tile(T, 512, 16)
    scale = 1.0 / math.sqrt(ATT_DH)
    npat = len(dqs)

    def body(*refs):
        dq_refs, dk_refs, dv_refs = refs[:npat], refs[npat:2 * npat], refs[2 * npat:3 * npat]
        cos_ref, sin_ref, o_ref = refs[3 * npat:]
        c = cos_ref[...]
        s = sin_ref[...]
        for hp in range(MIX_HALF // HEAD_LANES):
            sl = slice(hp * HEAD_LANES, (hp + 1) * HEAD_LANES)
            gq = scale * sum(r[:, sl] for r in dq_refs)
            gk = sum(r[:, sl] for r in dk_refs)
            gv = sum(r[:, sl] for r in dv_refs)
            o_ref[:, sl] = (gq * c - _rot_half(gq * s)).astype(BF16)
            o_ref[:, MIX_HALF + hp * HEAD_LANES:MIX_HALF + (hp + 1) * HEAD_LANES] = (
                gk * c - _rot_half(gk * s)).astype(BF16)
            o_ref[:, 2 * MIX_HALF + hp * HEAD_LANES:2 * MIX_HALF + (hp + 1) * HEAD_LANES] = gv.astype(BF16)

    return pl.pallas_call(
        body, name=name, grid=(T // tm,),
        in_specs=([pl.BlockSpec((tm, MIX_HALF), lambda i: (i, 0))] * (3 * npat)
                  + [pl.BlockSpec((tm, HEAD_LANES), lambda i: (i, 0))] * 2),
        out_specs=pl.BlockSpec((tm, 3 * MIX_HALF), lambda i: (i, 0)),
        out_shape=jax.ShapeDtypeStruct((T, 3 * MIX_HALF), BF16),
        compiler_params=_params(("parallel",)),
    )(*dqs, *dks, *dvs, cos, sin)


def _band_masks(n):
    r = lax.broadcasted_iota(jnp.int32, (ATT_BLOCK, ATT_BLOCK), 0)
    c = lax.broadcasted_iota(jnp.int32, (ATT_BLOCK, ATT_BLOCK), 1)
    return (c >= r) & (n > 0), c <= r, c < ATT_DH


def attn_fwd(q, k, v, name):
    d, L, W = q.shape
    nb = L // ATT_BLOCK

    def body(q_ref, kp_ref, kc_ref, vp_ref, vc_ref, o_ref, lse_ref):
        mask_p, mask_c, head0 = _band_masks(pl.program_id(1))
        for hp in range(W // HEAD_LANES):
            sl = slice(hp * HEAD_LANES, (hp + 1) * HEAD_LANES)
            qq, kp, kc, vp, vc = q_ref[0, :, sl], kp_ref[0, :, sl], kc_ref[0, :, sl], vp_ref[0, :, sl], vc_ref[0, :, sl]
            outs, lses = [], []
            for hh in range(2):
                hm = head0 if hh == 0 else jnp.logical_not(head0)
                qm = jnp.where(hm, qq, jnp.zeros_like(qq))
                sp = jnp.where(mask_p, _nt(qm, kp), NEG)
                sc = jnp.where(mask_c, _nt(qm, kc), NEG)
                m = jnp.maximum(jnp.max(sp, axis=1, keepdims=True), jnp.max(sc, axis=1, keepdims=True))
                pp = jnp.exp(sp - m)
                pc = jnp.exp(sc - m)
                den = jnp.sum(pp, axis=1, keepdims=True) + jnp.sum(pc, axis=1, keepdims=True)
                o = _nn(pp.astype(BF16), vp) + _nn(pc.astype(BF16), vc)
                outs.append(o / den)
                lses.append(m + jnp.log(den))
            o_ref[0, :, sl] = jnp.where(head0, outs[0], outs[1])
            lse_ref[0, :, sl] = jnp.where(head0, lses[0], lses[1])

    cur = pl.BlockSpec((1, ATT_BLOCK, W), lambda r, n: (r, n, 0))
    prev = pl.BlockSpec((1, ATT_BLOCK, W), lambda r, n: (r, jnp.maximum(n - 1, 0), 0))
    out = jax.ShapeDtypeStruct((d, L, W), F32)
    return pl.pallas_call(
        body, name=name, grid=(d, nb),
        in_specs=[cur, prev, cur, prev, cur],
        out_specs=[cur, cur], out_shape=[out, out],
        compiler_params=_params(("parallel", "parallel")),
    )(q, k, k, v, v)


def attn_combine(os_, lses, name):
    T, W = os_[0].shape
    tm = _tile(T, 512, 16)
    npat = len(os_)

    def body(*refs):
        o_refs, l_refs = refs[:npat], refs[npat:2 * npat]
        out_ref, lse_ref = refs[2 * npat:]
        ls = [r[...] for r in l_refs]
        m = ls[0]
        for l in ls[1:]:
            m = jnp.maximum(m, l)
        es = [jnp.exp(l - m) for l in ls]
        z = es[0]
        for e in es[1:]:
            z = z + e
        acc = es[0] * o_refs[0][...]
        for e, r in zip(es[1:], o_refs[1:]):
            acc = acc + e * r[...]
        out_ref[...] = (acc / z).astype(BF16)
        lse_ref[...] = m + jnp.log(z)

    blk = pl.BlockSpec((tm, W), lambda i: (i, 0))
    return pl.pallas_call(
        body, name=name, grid=(T // tm,),
        in_specs=[blk] * (2 * npat), out_specs=[blk, blk],
        out_shape=[jax.ShapeDtypeStruct((T, W), BF16), jax.ShapeDtypeStruct((T, W), F32)],
        compiler_params=_params(("parallel",)),
    )(*os_, *lses)


def attn_bwd(q, k, v, do, o, lse, name):
    d, L, W = q.shape
    nb = L // ATT_BLOCK

    def body(q_ref, kp_ref, kc_ref, vp_ref, vc_ref, do_ref, o_ref, lse_ref, dq_ref, dk_ref, dv_ref, kkeep, vkeep):
        n = pl.program_id(1)

        @pl.when(n < nb)
        def _():
            mask_p, mask_c, head0 = _band_masks(n)
            for hp in range(W // HEAD_LANES):
                sl = slice(hp * HEAD_LANES, (hp + 1) * HEAD_LANES)
                qq, kp, kc, vp, vc = q_ref[0, :, sl], kp_ref[0, :, sl], kc_ref[0, :, sl], vp_ref[0, :, sl], vc_ref[0, :, sl]
                dout = do_ref[0, :, sl]
                lse_v = lse_ref[0, :, sl]
                prod = dout.astype(F32) * o_ref[0, :, sl].astype(F32)
                dq_acc = jnp.zeros((ATT_BLOCK, HEAD_LANES), F32)
                kprev = jnp.zeros((ATT_BLOCK, HEAD_LANES), F32)
                vprev = jnp.zeros((ATT_BLOCK, HEAD_LANES), F32)
                kcur = jnp.zeros((ATT_BLOCK, HEAD_LANES), F32)
                vcur = jnp.zeros((ATT_BLOCK, HEAD_LANES), F32)
                for hh in range(2):
                    hm = head0 if hh == 0 else jnp.logical_not(head0)
                    qm = jnp.where(hm, qq, jnp.zeros_like(qq))
                    dom = jnp.where(hm, dout, jnp.zeros_like(dout))
                    lse_c = jnp.max(jnp.where(hm, lse_v, NEG), axis=1, keepdims=True)
                    delta = jnp.sum(jnp.where(hm, prod, 0.0), axis=1, keepdims=True)
                    pp = jnp.exp(jnp.where(mask_p, _nt(qm, kp), NEG) - lse_c)
                    pc = jnp.exp(jnp.where(mask_c, _nt(qm, kc), NEG) - lse_c)
                    dsp = (pp * (_nt(dom, vp) - delta)).astype(BF16)
                    dsc = (pc * (_nt(dom, vc) - delta)).astype(BF16)
                    dq_acc = dq_acc + _nn(dsp, jnp.where(hm, kp, jnp.zeros_like(kp))) + _nn(dsc, jnp.where(hm, kc, jnp.zeros_like(kc)))
                    kprev = kprev + _tn(dsp, qm)
                    vprev = vprev + _tn(pp.astype(BF16), dom)
                    kcur = kcur + _tn(dsc, qm)
                    vcur = vcur + _tn(pc.astype(BF16), dom)
                dq_ref[0, :, sl] = dq_acc

                @pl.when(n > 0)
                def _():
                    dk_ref[0, :, sl] = kkeep[:, sl] + kprev
                    dv_ref[0, :, sl] = vkeep[:, sl] + vprev

                kkeep[:, sl] = kcur
                vkeep[:, sl] = vcur

        @pl.when(n == nb)
        def _():
            dk_ref[0] = kkeep[...]
            dv_ref[0] = vkeep[...]

    cur = pl.BlockSpec((1, ATT_BLOCK, W), lambda r, n: (r, jnp.minimum(n, nb - 1), 0))
    prev = pl.BlockSpec((1, ATT_BLOCK, W), lambda r, n: (r, jnp.clip(n - 1, 0, nb - 1), 0))
    out = jax.ShapeDtypeStruct((d, L, W), F32)
    return pl.pallas_call(
        body, name=name, grid=(d, nb + 1),
        in_specs=[cur, prev, cur, prev, cur, cur, cur, cur],
        out_specs=[cur, prev, prev], out_shape=[out, out, out],
        scratch_shapes=[pltpu.VMEM((ATT_BLOCK, W), F32), pltpu.VMEM((ATT_BLOCK, W), F32)],
        compiler_params=_params(("parallel", "arbitrary")),
    )(q, k, k, v, v, do, o, lse)


def final_loss_bwd(h, gf, tgt, name):
    T, D = h.shape
    tm = _tile(T, 512, 8)

    def body(h_ref, g_ref, t_ref, dh_ref, red_ref):
        x = h_ref[...]
        r = lax.rsqrt(jnp.mean(x * x, axis=-1, keepdims=True) + EPS)
        n = x * r
        g = g_ref[...]
        err = n * g - t_ref[...]
        dy = err * (1.0 / D)

        @pl.when(pl.program_id(0) == 0)
        def _():
            red_ref[...] = jnp.zeros_like(red_ref)

        red_ref[0:1, :] += jnp.sum(dy * n, axis=0, keepdims=True)
        red_ref[1:2, :] += jnp.zeros((1, D), F32) + (0.5 / D) * jnp.sum(err * err, keepdims=True)
        dn = dy * g
        dh_ref[...] = r * (dn - n * jnp.mean(dn * n, axis=-1, keepdims=True))

    return pl.pallas_call(
        body, name=name, grid=(T // tm,),
        in_specs=[pl.BlockSpec((tm, D), lambda i: (i, 0)),
                  pl.BlockSpec((1, D), lambda i: (0, 0)),
                  pl.BlockSpec((tm, D), lambda i: (i, 0))],
        out_specs=[pl.BlockSpec((tm, D), lambda i: (i, 0)), pl.BlockSpec((8, D), lambda i: (0, 0))],
        out_shape=[jax.ShapeDtypeStruct((T, D), F32), jax.ShapeDtypeStruct((8, D), F32)],
        compiler_params=_params(("arbitrary",)),
    )(h, gf, tgt)


def ada_fwd(c_all, ada_w, ada_b, name):
    nl, D, N = ada_w.shape

    def body(c_ref, w_ref, b_ref, o_ref):
        c = c_ref[...]
        o_ref[0] = _nn(c * _sigmoid(c), w_ref[0]) + b_ref[0]

    return pl.pallas_call(
        body, name=name, grid=(nl,),
        in_specs=[pl.BlockSpec((N_DEV, D), lambda l: (0, 0)),
                  pl.BlockSpec((1, D, N), lambda l: (l, 0, 0)),
                  pl.BlockSpec((1, 1, N), lambda l: (l, 0, 0))],
        out_specs=pl.BlockSpec((1, N_DEV, N), lambda l: (l, 0, 0)),
        out_shape=jax.ShapeDtypeStruct((nl, N_DEV, N), F32),
        compiler_params=_params(("parallel",)),
    )(c_all, ada_w, ada_b)


def ada_bwd(c_allT, dmod, name):
    nl, _, N = dmod.shape
    D = c_allT.shape[0]

    def body(c_ref, g_ref, o_ref):
        c = c_ref[...]
        ca = c * _sigmoid(c)
        acc = ca[:, 0:1] * g_ref[0, 0:1, :]
        for b in range(1, N_DEV):
            acc = acc + ca[:, b:b + 1] * g_ref[0, b:b + 1, :]
        o_ref[0] = acc

    return pl.pallas_call(
        body, name=name, grid=(nl,),
        in_specs=[pl.BlockSpec((D, N_DEV), lambda l: (0, 0)),
                  pl.BlockSpec((1, N_DEV, N), lambda l: (l, 0, 0))],
        out_specs=pl.BlockSpec((1, D, N), lambda l: (l, 0, 0)),
        out_shape=jax.ShapeDtypeStruct((nl, D, N), F32),
        compiler_params=_params(("parallel",)),
    )(c_allT, dmod)


def adamw(w, g, m, v, name):
    R, C = w.shape
    tr = _tile(R, max(8, (1 << 19) // C // 8 * 8), 8)
    c1 = 1.0 - ADAM_B1 ** ADAM_STEP
    c2 = 1.0 - ADAM_B2 ** ADAM_STEP

    def body(w_ref, g_ref, m_ref, v_ref, d_ref, mo_ref, vo_ref):
        gv = g_ref[...]
        mn = ADAM_B1 * m_ref[...] + (1.0 - ADAM_B1) * gv
        vn = ADAM_B2 * v_ref[...] + (1.0 - ADAM_B2) * (gv * gv)
        mo_ref[...] = mn
        vo_ref[...] = vn
        d_ref[...] = -ADAM_LR * ((mn / c1) / (jnp.sqrt(vn / c2) + ADAM_EPS) + ADAM_WD * w_ref[...])

    blk = pl.BlockSpec((tr, C), lambda i: (i, 0))
    out = jax.ShapeDtypeStruct((R, C), F32)
    return pl.pallas_call(
        body, name=name, grid=(R // tr,),
        in_specs=[blk] * 4, out_specs=[blk] * 3, out_shape=[out] * 3,
        compiler_params=_params(("parallel",)),
    )(w, g, m, v)


def sum_halves(g, lands, c_idx, name):
    n, ns, _, rh, D = g.shape

    def body(c_ref, g_ref, l_ref, o_ref):
        o_ref[0, 0] = (g_ref[0, 0, 0].astype(F32) + l_ref[0, 0].astype(F32)).astype(BF16)

    return pl.pallas_call(
        body, name=name,
        grid_spec=pltpu.PrefetchScalarGridSpec(
            num_scalar_prefetch=1, grid=(n, ns),
            in_specs=[pl.BlockSpec((1, 1, 1, rh, D), lambda i, j, c: (i, j, c[0], 0, 0)),
                      pl.BlockSpec((1, 1, rh, D), lambda i, j, c: (i, j, 0, 0))],
            out_specs=pl.BlockSpec((1, 1, rh, D), lambda i, j, c: (i, j, 0, 0))),
        out_shape=jax.ShapeDtypeStruct((n, ns, rh, D), BF16),
        compiler_params=_params(("parallel", "parallel")),
    )(c_idx, g, lands)


def sum_chips(p, lands, chip_idx, name):
    n, ns, rh, D = p.shape

    def body(c_ref, p_ref, l_ref, o_ref):
        acc = p_ref[0, 0].astype(F32)
        for j in range(N_CHIP - 1):
            acc = acc + l_ref[j, 0].astype(F32)
        o_ref[0] = acc

    return pl.pallas_call(
        body, name=name,
        grid_spec=pltpu.PrefetchScalarGridSpec(
            num_scalar_prefetch=1, grid=(n,),
            in_specs=[pl.BlockSpec((1, 1, rh, D), lambda i, c: (i, c[0], 0, 0)),
                      pl.BlockSpec((N_CHIP - 1, 1, rh, D), lambda i, c: (0, i, 0, 0))],
            out_specs=pl.BlockSpec((1, rh, D), lambda i, c: (i, 0, 0))),
        out_shape=jax.ShapeDtypeStruct((n, rh, D), F32),
        compiler_params=_params(("parallel",)),
    )(chip_idx, p, lands)


def _my_place():
    return lax.axis_index("x"), lax.axis_index("y"), lax.axis_index("c")


def _other_chips(mx, my):
    return [(1 - mx, my), (mx, 1 - my), (1 - mx, 1 - my)]


def gather_small(x, name):
    def body(x_ref, out_ref, sum_ref, send_sems, recv_sems):
        mx, my, mc = _my_place()
        me = 4 * mx + 2 * my + mc
        out_ref[me] = x_ref[...]
        sends = []
        for k in range(1, N_DEV):
            kx, ky, kc = (k >> 2) & 1, (k >> 1) & 1, k & 1
            peer = (1 - mx if kx else mx, 1 - my if ky else my, 1 - mc if kc else mc)
            cp = pltpu.make_async_remote_copy(
                src_ref=x_ref, dst_ref=out_ref.at[me], send_sem=send_sems.at[k - 1], recv_sem=recv_sems.at[k - 1],
                device_id=peer, device_id_type=MESH)
            cp.start()
            sends.append((cp, 4 * peer[0] + 2 * peer[1] + peer[2], peer))
        for k, (cp, peer_slot, peer) in enumerate(sends):
            pltpu.make_async_remote_copy(
                src_ref=x_ref, dst_ref=out_ref.at[peer_slot], send_sem=send_sems.at[k], recv_sem=recv_sems.at[k],
                device_id=peer, device_id_type=MESH).wait_recv()
        for cp, _, _ in sends:
            cp.wait_send()
        acc = out_ref[0]
        for s in range(1, N_DEV):
            acc = acc + out_ref[s]
        sum_ref[...] = acc

    vmem = pl.BlockSpec(memory_space=pltpu.VMEM)
    return pl.pallas_call(
        body, name=name,
        in_specs=[vmem], out_specs=[vmem, vmem],
        out_shape=[jax.ShapeDtypeStruct((N_DEV,) + x.shape, x.dtype), jax.ShapeDtypeStruct(x.shape, x.dtype)],
        scratch_shapes=[pltpu.SemaphoreType.DMA((N_DEV - 1,)), pltpu.SemaphoreType.DMA((N_DEV - 1,))],
        compiler_params=pltpu.CompilerParams(vmem_limit_bytes=VMEM_LIMIT),
    )(x)


def gather_weights(shards, name):
    K = len(shards)

    def body(*refs):
        ins, outs = refs[:K], refs[K:2 * K]
        send1, recv1, send2, recv2, lsem = refs[2 * K:]
        mx, my, mc = _my_place()
        ci = 2 * mx + my
        chips = _other_chips(mx, my)
        local = [pltpu.make_async_copy(ins[k], outs[k].at[:, ci], lsem.at[k]) for k in range(K)]
        for cp in local:
            cp.start()

        def half(k, chip_idx, hc):
            return outs[k].at[:, chip_idx, hc]

        first, passed = [], []
        for j, (cx, cy) in enumerate(chips):
            for k in range(K):
                cp = pltpu.make_async_remote_copy(
                    src_ref=ins[k].at[:, mc], dst_ref=half(k, ci, mc),
                    send_sem=send1.at[k * 3 + j], recv_sem=recv1.at[k * 3 + j],
                    device_id=(cx, cy, mc), device_id_type=MESH)
                cp.start()
                first.append(cp)
        for j, (cx, cy) in enumerate(chips):
            cj = 2 * cx + cy
            for k in range(K):
                pltpu.make_async_remote_copy(
                    src_ref=ins[k].at[:, mc], dst_ref=half(k, cj, mc),
                    send_sem=send1.at[k * 3 + j], recv_sem=recv1.at[k * 3 + j],
                    device_id=(cx, cy, mc), device_id_type=MESH).wait_recv()
                cp = pltpu.make_async_remote_copy(
                    src_ref=half(k, cj, mc), dst_ref=half(k, cj, mc),
                    send_sem=send2.at[k * 3 + j], recv_sem=recv2.at[k * 3 + j],
                    device_id=(mx, my, 1 - mc), device_id_type=MESH)
                cp.start()
                passed.append(cp)
        for j, (cx, cy) in enumerate(chips):
            cj = 2 * cx + cy
            for k in range(K):
                pltpu.make_async_remote_copy(
                    src_ref=half(k, cj, 1 - mc), dst_ref=half(k, cj, 1 - mc),
                    send_sem=send2.at[k * 3 + j], recv_sem=recv2.at[k * 3 + j],
                    device_id=(mx, my, 1 - mc), device_id_type=MESH).wait_recv()
        for cp in first + passed:
            cp.wait_send()
        for cp in local:
            cp.wait()

    hbm = pl.BlockSpec(memory_space=pl.ANY)
    return pl.pallas_call(
        body, name=name,
        in_specs=[hbm] * K, out_specs=[hbm] * K,
        out_shape=[jax.ShapeDtypeStruct((s.shape[0], N_CHIP) + s.shape[1:], s.dtype) for s in shards],
        scratch_shapes=[pltpu.SemaphoreType.DMA((3 * K,))] * 4 + [pltpu.SemaphoreType.DMA((K,))],
    )(*shards)


def sibling_send_half(gs, name):
    K = len(gs)

    def body(*refs):
        ins, outs = refs[:K], refs[K:2 * K]
        send, recv = refs[2 * K:]
        mx, my, mc = _my_place()
        cps = []
        for k in range(K):
            cp = pltpu.make_async_remote_copy(
                src_ref=ins[k].at[:, :, 1 - mc], dst_ref=outs[k], send_sem=send.at[k], recv_sem=recv.at[k],
                device_id=(mx, my, 1 - mc), device_id_type=MESH)
            cp.start()
            cps.append(cp)
        for cp in cps:
            cp.wait()

    hbm = pl.BlockSpec(memory_space=pl.ANY)
    return pl.pallas_call(
        body, name=name,
        in_specs=[hbm] * K, out_specs=[hbm] * K,
        out_shape=[jax.ShapeDtypeStruct(g.shape[:2] + g.shape[3:], g.dtype) for g in gs],
        scratch_shapes=[pltpu.SemaphoreType.DMA((K,)), pltpu.SemaphoreType.DMA((K,))],
    )(*gs)


def scatter_to_chips(ps, name):
    K = len(ps)

    def body(*refs):
        ins, outs = refs[:K], refs[K:2 * K]
        send, recv = refs[2 * K:]
        mx, my, mc = _my_place()
        cps = []
        for j, (cx, cy) in enumerate(_other_chips(mx, my)):
            for k in range(K):
                cp = pltpu.make_async_remote_copy(
                    src_ref=ins[k].at[:, 2 * cx + cy], dst_ref=outs[k].at[j],
                    send_sem=send.at[k * 3 + j], recv_sem=recv.at[k * 3 + j],
                    device_id=(cx, cy, mc), device_id_type=MESH)
                cp.start()
                cps.append(cp)
        for cp in cps:
            cp.wait()

    hbm = pl.BlockSpec(memory_space=pl.ANY)
    return pl.pallas_call(
        body, name=name,
        in_specs=[hbm] * K, out_specs=[hbm] * K,
        out_shape=[jax.ShapeDtypeStruct((N_CHIP - 1, p.shape[0]) + p.shape[2:], p.dtype) for p in ps],
        scratch_shapes=[pltpu.SemaphoreType.DMA((3 * K,)), pltpu.SemaphoreType.DMA((3 * K,))],
    )(*ps)


def sibling_complete(ss, name):
    K = len(ss)

    def body(*refs):
        ins, outs = refs[:K], refs[K:2 * K]
        send, recv, lsem = refs[2 * K:]
        mx, my, mc = _my_place()
        local = [pltpu.make_async_copy(ins[k], outs[k].at[:, mc], lsem.at[k]) for k in range(K)]
        for cp in local:
            cp.start()
        cps = []
        for k in range(K):
            cp = pltpu.make_async_remote_copy(
                src_ref=ins[k], dst_ref=outs[k].at[:, mc], send_sem=send.at[k], recv_sem=recv.at[k],
                device_id=(mx, my, 1 - mc), device_id_type=MESH)
            cp.start()
            cps.append(cp)
        for k in range(K):
            pltpu.make_async_remote_copy(
                src_ref=ins[k], dst_ref=outs[k].at[:, 1 - mc], send_sem=send.at[k], recv_sem=recv.at[k],
                device_id=(mx, my, 1 - mc), device_id_type=MESH).wait_recv()
        for cp in cps:
            cp.wait_send()
        for cp in local:
            cp.wait()

    hbm = pl.BlockSpec(memory_space=pl.ANY)
    return pl.pallas_call(
        body, name=name,
        in_specs=[hbm] * K, out_specs=[hbm] * K,
        out_shape=[jax.ShapeDtypeStruct((s.shape[0], 2) + s.shape[1:], s.dtype) for s in ss],
        scratch_shapes=[pltpu.SemaphoreType.DMA((K,)), pltpu.SemaphoreType.DMA((K,)), pltpu.SemaphoreType.DMA((K,))],
    )(*ss)


def _rope_tables(T):
    inv = ROPE_THETA ** (-jnp.arange(0, ATT_DH, 2, dtype=F32) / ATT_DH)
    ang = jnp.arange(T, dtype=F32)[:, None] * inv[None, :]
    ang = jnp.concatenate([ang, ang, ang, ang], axis=-1)
    return jnp.cos(ang), jnp.sin(ang)


def _to_residues(x, d):
    T, W = x.shape
    if d == 1:
        return x.reshape(1, T, W)
    return x.reshape(T // d, d, W).transpose(1, 0, 2)


def _from_residues(x):
    d, L, W = x.shape
    if d == 1:
        return x.reshape(L, W)
    return x.transpose(1, 0, 2).reshape(L * d, W)


def _ffn_fwd(h, ng, i_n, mod, i0, wgT, wuT, wd, tag):
    y = normmod_fwd(h, ng, i_n, mod, i0, i0 + 1, f"normmod_{tag}")
    a, b, s = ffn_up(y, wgT, wuT, f"ffn_up_{tag}")
    hn, o = resid_matmul([s], wd, h, mod, i0 + 2, 0.5, f"ffn_down_{tag}")
    return hn, (h, y, a, b, s, o)


def _ffn_bwd(dh, res, ng, i_n, mod, i0, wgT, wuT, wd, tag):
    h, y, a, b, s, o = res
    F = _wrows(wgT)
    do, red_g = gate_bwd(dh, o, mod, i0 + 2, 0.5, f"gate_bwd_{tag}")
    da, db = ffn_bwd_mid(do, wd, a, b, f"ffn_bwd_mid_{tag}")
    dh_new, red_n = dy_normbwd([(da, 0, wgT, 0, F), (db, 0, wuT, 0, F)], h, dh, ng, i_n, mod, i0 + 1,
                               f"ffn_bwd_dy_{tag}")
    g_wgT = matmul_tn(da, 0, F, y, f"dwg_{tag}")
    g_wuT = matmul_tn(db, 0, F, y, f"dwu_{tag}")
    g_wd = matmul_tn(s, 0, F, do, f"dwd_{tag}")
    return dh_new, (g_wgT, g_wuT, g_wd), red_n, red_g


def _mixer_fwd(h, ng, mod, w_inT, w_out, sgu, cos, sin, tag):
    lng, lnb, sw, swt, bcol = sgu
    y = normmod_fwd(h, ng, 1, mod, 3, 4, f"normmod_{tag}")
    proj = matmul_nt(y, w_inT, f"proj_{tag}")
    out_a = sgu_fwd(proj, lng, lnb, sw, bcol, f"sgu_fwd_{tag}")
    qr, kr = rope_fwd(proj, cos, sin, f"rope_fwd_{tag}")
    vv = proj[:, 4 * MIX_HALF:]
    os_, lses, qkv_res = [], [], []
    for d in DILATIONS:
        qd, kd, vd = _to_residues(qr, d), _to_residues(kr, d), _to_residues(vv, d)
        o_d, lse_d = attn_fwd(qd, kd, vd, f"attn_fwd_d{d}_{tag}")
        os_.append(_from_residues(o_d))
        lses.append(_from_residues(lse_d))
        qkv_res.append((qd, kd, vd))
    out_b, lse = attn_combine(os_, lses, f"attn_combine_{tag}")
    hn, om = resid_matmul([out_a, out_b], w_out, h, mod, 5, 1.0, f"mix_out_{tag}")
    return hn, (h, y, proj, out_a, out_b, lse, qkv_res, om)


def _mixer_bwd(dh, res, ng, mod, w_inT, w_out, sgu, cos, sin, tag):
    lng, lnb, sw, swt, bcol = sgu
    h, y, proj, out_a, out_b, lse, qkv_res, om = res
    dom, red_g = gate_bwd(dh, om, mod, 5, 1.0, f"gate_bwd_{tag}")
    dmixed = matmul_nt(dom, w_out, f"dmixed_{tag}")
    g_wout = jnp.concatenate([matmul_tn(out_a, 0, MIX_HALF, dom, f"dwout_a_{tag}"),
                              matmul_tn(out_b, 0, MIX_HALF, dom, f"dwout_b_{tag}")], axis=0)
    d_uv, d_sw, d_svec = sgu_bwd(proj, dmixed, lng, lnb, sw, swt, bcol, f"sgu_bwd_{tag}")
    dob = dmixed[:, MIX_HALF:]
    dqs, dks, dvs = [], [], []
    for d, (qd, kd, vd) in zip(DILATIONS, qkv_res):
        dq, dk, dv = attn_bwd(qd, kd, vd, _to_residues(dob, d), _to_residues(out_b, d), _to_residues(lse, d),
                              f"attn_bwd_d{d}_{tag}")
        dqs.append(_from_residues(dq))
        dks.append(_from_residues(dk))
        dvs.append(_from_residues(dv))
    d_qkv = rope_bwd(dqs, dks, dvs, cos, sin, f"rope_bwd_{tag}")
    pairs = ([(d_uv, p, w_inT, p, MIX_HALF) for p in range(2)]
             + [(d_qkv, p, w_inT, 2 + p, MIX_HALF) for p in range(3)])
    dh_new, red_n = dy_normbwd(pairs, h, dh, ng, 1, mod, 4, f"mix_bwd_dy_{tag}")
    g_winT = jnp.concatenate([matmul_tn(d_uv, 0, 2 * MIX_HALF, y, f"dwin_uv_{tag}"),
                              matmul_tn(d_qkv, 0, 3 * MIX_HALF, y, f"dwin_qkv_{tag}")], axis=0)
    return dh_new, g_winT, g_wout, d_sw, d_svec, red_n, red_g


def _local_step(x, tgt, mods, ngs, ffn_w, w_inT, w_out, sgus, gf):
    T, D = x.shape
    cos, sin = _rope_tables(T)
    h = x
    saved = []
    for l in range(2):
        h, r1 = _ffn_fwd(h, ngs[l], 0, mods[l], 0, (ffn_w, (l, 0)), (ffn_w, (l, 1)), (ffn_w, (l, 2)), f"l{l}f1")
        h, r2 = _mixer_fwd(h, ngs[l], mods[l], (w_inT, (l,)), (w_out, (l,)), sgus[l], cos, sin, f"l{l}mx")
        h, r3 = _ffn_fwd(h, ngs[l], 2, mods[l], 6, (ffn_w, (l, 3)), (ffn_w, (l, 4)), (ffn_w, (l, 5)), f"l{l}f2")
        saved.append((r1, r2, r3))
    dh, red_final = final_loss_bwd(h, gf, tgt, "final_loss_bwd")
    layer_grads = [None, None]
    for l in (1, 0):
        r1, r2, r3 = saved[l]
        dh, g_f2, rn3, rg3 = _ffn_bwd(dh, r3, ngs[l], 2, mods[l], 6, (ffn_w, (l, 3)), (ffn_w, (l, 4)), (ffn_w, (l, 5)),
                                      f"l{l}f2")
        dh, g_winT, g_wout, d_sw, d_svec, rn2, rg2 = _mixer_bwd(dh, r2, ngs[l], mods[l], (w_inT, (l,)), (w_out, (l,)),
                                                                sgus[l], cos, sin, f"l{l}mx")
        dh, g_f1, rn1, rg1 = _ffn_bwd(dh, r1, ngs[l], 0, mods[l], 0, (ffn_w, (l, 0)), (ffn_w, (l, 1)), (ffn_w, (l, 2)),
                                      f"l{l}f1")
        layer_grads[l] = dict(ffn=g_f1 + g_f2, w_inT=g_winT, w_out=g_wout, sgu_w=d_sw, sgu_vec=d_svec,
                              red_n=(rn1, rn2, rn3), red_g=(rg1, rg2, rg3))
    return dh, layer_grads, red_final


def _adam_out(w, g, m, v, name):
    shp = w.shape
    two_d = (-1, shp[-1])
    d, mn, vn = adamw(w.reshape(two_d), g.reshape(two_d), m.reshape(two_d), v.reshape(two_d), name)
    return g, d.reshape(shp), mn.reshape(shp), vn.reshape(shp)


def kernel(x, c, ada_w, ada_b, norm_g, ffn1_wg, ffn1_wu, ffn1_wd, ffn2_wg, ffn2_wu, ffn2_wd, w_in, sgu_ln_g, sgu_ln_b, sgu_w, sgu_b, w_out, final_g, loss_target, m_ada_w, m_ada_b, m_norm_g, m_ffn1_wg, m_ffn1_wu, m_ffn1_wd, m_ffn2_wg, m_ffn2_wu, m_ffn2_wd, m_w_in, m_sgu_ln_g, m_sgu_ln_b, m_sgu_w, m_sgu_b, m_w_out, m_final_g, v_ada_w, v_ada_b, v_norm_g, v_ffn1_wg, v_ffn1_wu, v_ffn1_wd, v_ffn2_wg, v_ffn2_wu, v_ffn2_wd, v_w_in, v_sgu_ln_g, v_sgu_ln_b, v_sgu_w, v_sgu_b, v_w_out, v_final_g):
    T, D = x.shape[1], x.shape[2]
    NL = ada_w.shape[0]
    mx, my, mc = _my_place()
    me = 4 * mx + 2 * my + mc
    ci = 2 * mx + my
    c_idx = jnp.reshape(mc, (1,)).astype(jnp.int32)
    chip_idx = jnp.reshape(ci, (1,)).astype(jnp.int32)

    def halves(a):
        n, r, _ = a.shape
        return a.reshape(n, 2, r // 2, D)

    ffn_shard = jnp.stack([jnp.swapaxes(ffn1_wg, 1, 2), jnp.swapaxes(ffn1_wu, 1, 2), ffn1_wd,
                           jnp.swapaxes(ffn2_wg, 1, 2), jnp.swapaxes(ffn2_wu, 1, 2), ffn2_wd], axis=1).astype(BF16)
    Fs = ffn_shard.shape[2]
    win_shard = jnp.swapaxes(w_in, 1, 2).astype(BF16)
    wout_shard = w_out.astype(BF16)
    ffn_full, win_full, wout_full = gather_weights(
        [halves(ffn_shard.reshape(NL * 6, Fs, D)), halves(win_shard), halves(wout_shard)], "gather_weights")
    ffn_w = ffn_full.reshape(NL, 6, N_CHIP * Fs, D)
    w_inT = win_full.reshape(NL, N_CHIP * win_shard.shape[1], D)
    w_outf = wout_full.reshape(NL, N_CHIP * wout_shard.shape[1], D)

    ngw = norm_g.shape[2]
    small_in = jnp.concatenate([jnp.pad(c, ((0, 7), (0, 0))),
                                jnp.pad(norm_g.reshape(NL * 3, ngw), ((0, 8 - NL * 3), (0, D - ngw)))], axis=0)
    small_all, _ = gather_small(small_in, "gather_c_normg")
    c_all = small_all[:, 0, :]
    ng_parts = small_all[0::2, 8:8 + NL * 3, :ngw]
    ngs = jnp.transpose(ng_parts, (1, 0, 2)).reshape(NL, 3, N_CHIP * ngw)

    nmod = ada_w.shape[2]
    ada_b_mine = lax.dynamic_slice_in_dim(ada_b, ci * nmod, nmod, axis=1).reshape(NL, 1, nmod)
    mod_part = ada_fwd(c_all, ada_w, ada_b_mine, "ada_fwd")
    mod_all, _ = gather_small(mod_part.reshape(NL * N_DEV, nmod), "gather_mod")
    mod_rows = lax.dynamic_index_in_dim(mod_all.reshape(N_DEV, NL, N_DEV, nmod), me, axis=2, keepdims=False)
    mods = jnp.transpose(mod_rows[0::2], (1, 0, 2)).reshape(NL, N_ADA, D)

    sgus = []
    for l in range(NL):
        sgus.append((sgu_ln_g[l].reshape(1, MIX_HALF), sgu_ln_b[l].reshape(1, MIX_HALF), sgu_w[l],
                     jnp.swapaxes(sgu_w[l], 1, 2), jnp.transpose(sgu_b[l])))

    grad_x, lg, red_final = _local_step(x[0], loss_target[0], mods, ngs, ffn_w, w_inT, w_outf, sgus,
                                        final_g.reshape(1, D))

    blocks = []
    for l in range(NL):
        blocks += list(lg[l]["red_n"]) + list(lg[l]["red_g"])
        blocks.append(jnp.pad(lg[l]["sgu_vec"], ((0, 0), (0, D - MIX_HALF))))
        blocks.append(lg[l]["sgu_w"].reshape(-1, D))
    blocks.append(red_final)
    offs, o = [], 0
    for b in blocks:
        offs.append(o)
        o += b.shape[0]
    small_all, small_sum = gather_small(jnp.concatenate(blocks, axis=0), "gather_small_grads")
    per_layer = 8
    loss = small_sum[offs[-1] + 1, 0]
    g_final_g = small_sum[offs[-1], :]
    g_norm_g, g_ada_b, g_lng, g_lnb, g_sb, g_sw, dmod_all = [], [], [], [], [], [], []
    for l in range(NL):
        b0 = per_layer * l
        rn = [small_sum[offs[b0 + i]:offs[b0 + i] + 8] for i in range(3)]
        rg = [small_sum[offs[b0 + 3 + i]:offs[b0 + 3 + i] + 8] for i in range(3)]
        g_norm_g.append(jnp.stack([rn[i][2] for i in range(3)], axis=0))
        g_ada_b.append(jnp.concatenate([jnp.stack([rn[i][0], rn[i][1], rg[i][0]], axis=0) for i in range(3)],
                                       axis=0).reshape(N_ADA * D))
        sv = small_sum[offs[b0 + 6]:offs[b0 + 6] + 8, :MIX_HALF]
        g_lng.append(sv[0].reshape(SGU_HEADS, HEAD_LANES))
        g_lnb.append(sv[1].reshape(SGU_HEADS, HEAD_LANES))
        g_sb.append(sv[2].reshape(SGU_HEADS, ATT_BLOCK))
        g_sw.append(small_sum[offs[b0 + 7]:offs[b0 + 7] + SGU_HEADS * ATT_BLOCK * HEAD_LANES // D].reshape(sgu_w.shape[1:]))
        rows = []
        for i in range(3):
            an = small_all[:, offs[b0 + i]:offs[b0 + i] + 2]
            ag = small_all[:, offs[b0 + 3 + i]:offs[b0 + 3 + i] + 1]
            rows += [an[:, 0], an[:, 1], ag[:, 0]]
        dmod_all.append(jnp.stack(rows, axis=1).reshape(N_DEV, N_ADA * D))
    dmod_all = jnp.stack(dmod_all, axis=0)
    dmod_mine = lax.dynamic_slice_in_dim(dmod_all, ci * nmod, nmod, axis=2)
    g_ada_w = ada_bwd(jnp.transpose(c_all), dmod_mine, "ada_bwd")
    g_ada_b = jnp.stack(g_ada_b, axis=0)
    g_norm_g_full = jnp.stack(g_norm_g, axis=0)
    g_norm_g_mine = lax.dynamic_slice_in_dim(g_norm_g_full, ci * ngw, ngw, axis=2)

    def split(a):
        n, r4, _ = a.shape
        return a.reshape(n, N_CHIP, 2, r4 // N_CHIP // 2, D)

    g_ffn = split(jnp.stack([g for l in range(NL) for g in lg[l]["ffn"]], axis=0))
    g_win = split(jnp.stack([lg[l]["w_inT"] for l in range(NL)], axis=0))
    g_wout = split(jnp.stack([lg[l]["w_out"] for l in range(NL)], axis=0))
    parts = [g_ffn, g_win, g_wout]
    lands = sibling_send_half(parts, "rs_sibling")
    psums = [sum_halves(g, ld, c_idx, f"rs_sum_halves_{i}") for i, (g, ld) in enumerate(zip(parts, lands))]
    lands2 = scatter_to_chips(psums, "rs_chips")
    ssums = [sum_chips(p, ld, chip_idx, f"rs_sum_chips_{i}") for i, (p, ld) in enumerate(zip(psums, lands2))]
    fin = sibling_complete(ssums, "rs_complete")
    gT_ffn = fin[0].reshape(NL, 6, Fs, D)
    g_w_in = jnp.swapaxes(fin[1].reshape(NL, -1, D), 1, 2)
    g_w_out = fin[2].reshape(NL, -1, D)
    g_ffn1_wg = jnp.swapaxes(gT_ffn[:, 0], 1, 2)
    g_ffn1_wu = jnp.swapaxes(gT_ffn[:, 1], 1, 2)
    g_ffn1_wd = gT_ffn[:, 2]
    g_ffn2_wg = jnp.swapaxes(gT_ffn[:, 3], 1, 2)
    g_ffn2_wu = jnp.swapaxes(gT_ffn[:, 4], 1, 2)
    g_ffn2_wd = gT_ffn[:, 5]

    triples = [
        ("ada_w", ada_w, g_ada_w, m_ada_w, v_ada_w),
        ("ada_b", ada_b, g_ada_b, m_ada_b, v_ada_b),
        ("norm_g", norm_g, g_norm_g_mine, m_norm_g, v_norm_g),
        ("ffn1_wg", ffn1_wg, g_ffn1_wg, m_ffn1_wg, v_ffn1_wg),
        ("ffn1_wu", ffn1_wu, g_ffn1_wu, m_ffn1_wu, v_ffn1_wu),
        ("ffn1_wd", ffn1_wd, g_ffn1_wd, m_ffn1_wd, v_ffn1_wd),
        ("ffn2_wg", ffn2_wg, g_ffn2_wg, m_ffn2_wg, v_ffn2_wg),
        ("ffn2_wu", ffn2_wu, g_ffn2_wu, m_ffn2_wu, v_ffn2_wu),
        ("ffn2_wd", ffn2_wd, g_ffn2_wd, m_ffn2_wd, v_ffn2_wd),
        ("w_in", w_in, g_w_in, m_w_in, v_w_in),
        ("sgu_ln_g", sgu_ln_g, jnp.stack(g_lng, axis=0), m_sgu_ln_g, v_sgu_ln_g),
        ("sgu_ln_b", sgu_ln_b, jnp.stack(g_lnb, axis=0), m_sgu_ln_b, v_sgu_ln_b),
        ("sgu_w", sgu_w, jnp.stack(g_sw, axis=0), m_sgu_w, v_sgu_w),
        ("sgu_b", sgu_b, jnp.stack(g_sb, axis=0), m_sgu_b, v_sgu_b),
        ("w_out", w_out, g_w_out, m_w_out, v_w_out),
        ("final_g", final_g.reshape(1, D), g_final_g.reshape(1, D), m_final_g.reshape(1, D), v_final_g.reshape(1, D)),
    ]
    grads, deltas, new_ms, new_vs = [], [], [], []
    for nm, w, g, m, v in triples:
        g, dlt, mn, vn = _adam_out(w, g, m, v, f"adamw_{nm}")
        if nm == "final_g":
            g, dlt, mn, vn = (t.reshape(D) for t in (g, dlt, mn, vn))
        grads.append(g)
        deltas.append(dlt)
        new_ms.append(mn)
        new_vs.append(vn)
    return (loss, grad_x[None], *grads, *deltas, *new_ms, *new_vs)
```

```python
import math

import jax
import jax.numpy as jnp
from jax import lax
from jax.experimental import pallas as pl
from jax.experimental.pallas import tpu as pltpu

F32 = jnp.float32
BF16 = jnp.bfloat16
EPS = 1e-6
SGU_HEADS = 4
HEAD_LANES = 128
ATT_DH = 64
ATT_BLOCK = 128
MIX_HALF = SGU_HEADS * HEAD_LANES
DILATIONS = (1, 4, 16)
ROPE_THETA = 10000.0
N_ADA = 9
ADAM_LR, ADAM_B1, ADAM_B2, ADAM_EPS, ADAM_WD, ADAM_STEP = 0.001, 0.9, 0.999, 1e-08, 0.01, 10
NEG = -1e30
V7X_VMEM_BYTES = 64 * 1024 * 1024
VMEM_LIMIT = V7X_VMEM_BYTES * 7 // 8
MESH = pl.DeviceIdType.MESH
N_DEV = 8
N_CHIP = 4


def _tile(n, cap, mult):
    if n <= cap:
        return n
    t = (cap // mult) * mult
    while t >= mult:
        if n % t == 0:
            return t
        t -= mult
    raise ValueError((n, cap, mult))


def _params(dims=None):
    return pltpu.CompilerParams(dimension_semantics=dims, vmem_limit_bytes=VMEM_LIMIT)


def _wspec(w, rows, idx):
    arr, lead = w
    return pl.BlockSpec((None,) * len(lead) + (rows, arr.shape[-1]), lambda *g: tuple(lead) + (idx(*g), 0))


def _wrows(w):
    return w[0].shape[-2]


def _nt(a, b):
    return lax.dot_general(a, b, (((1,), (1,)), ((), ())), preferred_element_type=F32)


def _tn(a, b):
    return lax.dot_general(a, b, (((0,), (0,)), ((), ())), preferred_element_type=F32)


def _nn(a, b):
    return jnp.dot(a, b, preferred_element_type=F32)


def _sigmoid(x):
    return 1.0 / (1.0 + jnp.exp(-x))


_GELU_K = math.sqrt(2.0 / math.pi)
_GELU_C = 0.044715


def _gelu(x):
    t = jnp.tanh(_GELU_K * (x + _GELU_C * x * x * x))
    return 0.5 * x * (1.0 + t)


def _gelu_and_grad(x):
    x2 = x * x
    t = jnp.tanh(_GELU_K * (x + _GELU_C * x * x2))
    g = 0.5 * x * (1.0 + t)
    dg = 0.5 * (1.0 + t) + 0.5 * x * (1.0 - t * t) * (_GELU_K * (1.0 + 3.0 * _GELU_C * x2))
    return g, dg


def normmod_fwd(h, ng, i_n, mod, i_sh, i_sc, name):
    T, D = h.shape
    tm = _tile(T, 512, 8)

    def body(h_ref, ng_ref, mod_ref, y_ref):
        x = h_ref[...]
        r = lax.rsqrt(jnp.mean(x * x, axis=-1, keepdims=True) + EPS)
        y = (x * r) * ng_ref[i_n:i_n + 1, :]
        y_ref[...] = (y * (1.0 + mod_ref[i_sc:i_sc + 1, :]) + mod_ref[i_sh:i_sh + 1, :]).astype(BF16)

    return pl.pallas_call(
        body, name=name, grid=(T // tm,),
        in_specs=[pl.BlockSpec((tm, D), lambda i: (i, 0)),
                  pl.BlockSpec(ng.shape, lambda i: (0, 0)),
                  pl.BlockSpec(mod.shape, lambda i: (0, 0))],
        out_specs=pl.BlockSpec((tm, D), lambda i: (i, 0)),
        out_shape=jax.ShapeDtypeStruct((T, D), BF16),
        compiler_params=_params(("parallel",)),
    )(h, ng, mod)


def ffn_up(y, wgT, wuT, name):
    T, D = y.shape
    F = _wrows(wgT)
    tm = _tile(T, 512, 16)
    tf = _tile(F, 1408, 128)

    def body(y_ref, wg_ref, wu_ref, a_ref, b_ref, s_ref):
        yv = y_ref[...]
        a = _nt(yv, wg_ref[...])
        b = _nt(yv, wu_ref[...])
        a_ref[...] = a.astype(BF16)
        b_ref[...] = b.astype(BF16)
        s_ref[...] = (a * _sigmoid(a) * b).astype(BF16)

    act = jax.ShapeDtypeStruct((T, F), BF16)
    return pl.pallas_call(
        body, name=name, grid=(F // tf, T // tm),
        in_specs=[pl.BlockSpec((tm, D), lambda j, i: (i, 0)),
                  _wspec(wgT, tf, lambda j, i: j),
                  _wspec(wuT, tf, lambda j, i: j)],
        out_specs=[pl.BlockSpec((tm, tf), lambda j, i: (i, j))] * 3,
        out_shape=[act, act, act],
        compiler_params=_params(("parallel", "parallel")),
    )(y, wgT[0], wuT[0])


def resid_matmul(xs, w, h, mod, i_g, coef, name):
    T, D = h.shape
    kb = xs[0].shape[1]
    assert all(x.shape == (T, kb) for x in xs) and _wrows(w) == kb * len(xs)
    tm = _tile(T, 512, 16)
    nx = len(xs)

    def body(*refs):
        x_refs, w_refs = refs[:nx], refs[nx:2 * nx]
        h_ref, mod_ref, hn_ref, o_ref = refs[2 * nx:]
        o = _nn(x_refs[0][...], w_refs[0][...])
        for xr, wr in zip(x_refs[1:], w_refs[1:]):
            o = o + _nn(xr[...], wr[...])
        o_ref[...] = o.astype(BF16)
        hn_ref[...] = h_ref[...] + (coef * mod_ref[i_g:i_g + 1, :]) * o

    return pl.pallas_call(
        body, name=name, grid=(T // tm,),
        in_specs=([pl.BlockSpec((tm, kb), lambda i: (i, 0))] * nx
                  + [_wspec(w, kb, lambda i, p=p: p) for p in range(nx)]
                  + [pl.BlockSpec((tm, D), lambda i: (i, 0)),
                     pl.BlockSpec(mod.shape, lambda i: (0, 0))]),
        out_specs=[pl.BlockSpec((tm, D), lambda i: (i, 0))] * 2,
        out_shape=[jax.ShapeDtypeStruct((T, D), F32), jax.ShapeDtypeStruct((T, D), BF16)],
        compiler_params=_params(("parallel",)),
    )(*xs, *([w[0]] * nx), h, mod)


def gate_bwd(dh, o, mod, i_g, coef, name):
    T, D = dh.shape
    tm = _tile(T, 512, 16)

    def body(dh_ref, o_ref, mod_ref, do_ref, red_ref):
        d = dh_ref[...]
        do_ref[...] = (d * (coef * mod_ref[i_g:i_g + 1, :])).astype(BF16)

        @pl.when(pl.program_id(0) == 0)
        def _():
            red_ref[...] = jnp.zeros_like(red_ref)

        red_ref[0:1, :] += coef * jnp.sum(d * o_ref[...].astype(F32), axis=0, keepdims=True)

    return pl.pallas_call(
        body, name=name, grid=(T // tm,),
        in_specs=[pl.BlockSpec((tm, D), lambda i: (i, 0)),
                  pl.BlockSpec((tm, D), lambda i: (i, 0)),
                  pl.BlockSpec(mod.shape, lambda i: (0, 0))],
        out_specs=[pl.BlockSpec((tm, D), lambda i: (i, 0)), pl.BlockSpec((8, D), lambda i: (0, 0))],
        out_shape=[jax.ShapeDtypeStruct((T, D), BF16), jax.ShapeDtypeStruct((8, D), F32)],
        compiler_params=_params(("arbitrary",)),
    )(dh, o, mod)


def ffn_bwd_mid(do, wd, a, b, name):
    T, D = do.shape
    F = _wrows(wd)
    tm = _tile(T, 512, 16)
    tf = _tile(F, 1408, 128)

    def body(do_ref, wd_ref, a_ref, b_ref, da_ref, db_ref):
        ds = _nt(do_ref[...], wd_ref[...])
        av = a_ref[...].astype(F32)
        bv = b_ref[...].astype(F32)
        sig = _sigmoid(av)
        da_ref[...] = (ds * bv * (sig * (1.0 + av * (1.0 - sig)))).astype(BF16)
        db_ref[...] = (ds * (av * sig)).astype(BF16)

    act = jax.ShapeDtypeStruct((T, F), BF16)
    return pl.pallas_call(
        body, name=name, grid=(F // tf, T // tm),
        in_specs=[pl.BlockSpec((tm, D), lambda j, i: (i, 0)),
                  _wspec(wd, tf, lambda j, i: j),
                  pl.BlockSpec((tm, tf), lambda j, i: (i, j)),
                  pl.BlockSpec((tm, tf), lambda j, i: (i, j))],
        out_specs=[pl.BlockSpec((tm, tf), lambda j, i: (i, j))] * 2,
        out_shape=[act, act],
        compiler_params=_params(("parallel", "parallel")),
    )(do, wd[0], a, b)


def dy_normbwd(pairs, h, dhp, ng, i_n, mod, i_sc, name):
    T, D = h.shape
    tm = _tile(T, 256, 16)
    npair = len(pairs)

    def body(*refs):
        x_refs, w_refs = refs[:npair], refs[npair:2 * npair]
        h_ref, dhp_ref, ng_ref, mod_ref, dh_ref, red_ref = refs[2 * npair:]
        dy = _nn(x_refs[0][...], w_refs[0][...])
        for xr, wr in zip(x_refs[1:], w_refs[1:]):
            dy = dy + _nn(xr[...], wr[...])
        x = h_ref[...]
        r = lax.rsqrt(jnp.mean(x * x, axis=-1, keepdims=True) + EPS)
        n = x * r
        gn = ng_ref[i_n:i_n + 1, :]
        dnh = dy * (1.0 + mod_ref[i_sc:i_sc + 1, :])

        @pl.when(pl.program_id(0) == 0)
        def _():
            red_ref[...] = jnp.zeros_like(red_ref)

        red_ref[0:1, :] += jnp.sum(dy, axis=0, keepdims=True)
        red_ref[1:2, :] += jnp.sum(dy * (n * gn), axis=0, keepdims=True)
        red_ref[2:3, :] += jnp.sum(dnh * n, axis=0, keepdims=True)
        dn = dnh * gn
        dh_ref[...] = dhp_ref[...] + r * (dn - n * jnp.mean(dn * n, axis=-1, keepdims=True))

    in_specs = ([pl.BlockSpec((tm, kb), lambda i, c=c: (i, c)) for (_, c, _, _, kb) in pairs]
                + [_wspec(w, kb, lambda i, r=r: r) for (_, _, w, r, kb) in pairs]
                + [pl.BlockSpec((tm, D), lambda i: (i, 0)),
                   pl.BlockSpec((tm, D), lambda i: (i, 0)),
                   pl.BlockSpec(ng.shape, lambda i: (0, 0)),
                   pl.BlockSpec(mod.shape, lambda i: (0, 0))])
    return pl.pallas_call(
        body, name=name, grid=(T // tm,), in_specs=in_specs,
        out_specs=[pl.BlockSpec((tm, D), lambda i: (i, 0)), pl.BlockSpec((8, D), lambda i: (0, 0))],
        out_shape=[jax.ShapeDtypeStruct((T, D), F32), jax.ShapeDtypeStruct((8, D), F32)],
        compiler_params=_params(("arbitrary",)),
    )(*[p[0] for p in pairs], *[p[2][0] for p in pairs], h, dhp, ng, mod)


def matmul_tn(a, b, buf, slot, row0, name, tmo_cap=1408):
    T, N = b.shape
    ma = a.shape[1]
    tmo = _tile(ma, tmo_cap, 128)
    assert row0 % tmo == 0
    nmo = ma // tmo
    tk = _tile(T, 512, 16)
    nk = T // tk

    def body(a_ref, b_ref, buf_ref, o_ref, acc_ref):
        k = pl.program_id(1)

        @pl.when(k == 0)
        def _():
            acc_ref[...] = jnp.zeros_like(acc_ref)

        acc_ref[...] += _tn(a_ref[...], b_ref[...])

        @pl.when(k == nk - 1)
        def _():
            o_ref[...] = acc_ref[...].astype(BF16)

    return pl.pallas_call(
        body, name=name, grid=(nmo, nk),
        in_specs=[pl.BlockSpec((tk, tmo), lambda j, k: (k, j)),
                  pl.BlockSpec((tk, N), lambda j, k: (k, 0)),
                  pl.BlockSpec(memory_space=pl.ANY)],
        out_specs=pl.BlockSpec((None, tmo, N), lambda j, k: (slot, row0 // tmo + j, 0)),
        out_shape=jax.ShapeDtypeStruct(buf.shape, BF16),
        scratch_shapes=[pltpu.VMEM((tmo, N), F32)],
        input_output_aliases={2: 0},
        compiler_params=_params(("parallel", "arbitrary")),
    )(a, b, buf)


def matmul_nt(x, w, name):
    T, K = x.shape
    N = _wrows(w)
    tm = _tile(T, 512, 16)
    tn = _tile(N, 1280, 128)

    def body(x_ref, w_ref, o_ref):
        o_ref[...] = _nt(x_ref[...], w_ref[...]).astype(BF16)

    return pl.pallas_call(
        body, name=name, grid=(N // tn, T // tm),
        in_specs=[pl.BlockSpec((tm, K), lambda j, i: (i, 0)), _wspec(w, tn, lambda j, i: j)],
        out_specs=pl.BlockSpec((tm, tn), lambda j, i: (i, j)),
        out_shape=jax.ShapeDtypeStruct((T, N), BF16),
        compiler_params=_params(("parallel", "parallel")),
    )(x, w[0])


def _sgu_head_fwd(u, v, lng, lnb):
    gu, dgu = _gelu_and_grad(u)
    gv, dgv = _gelu_and_grad(v)
    mu = jnp.mean(gv, axis=-1, keepdims=True)
    xc = gv - mu
    rstd = lax.rsqrt(jnp.mean(xc * xc, axis=-1, keepdims=True) + EPS)
    xhat = xc * rstd
    vn = xhat * lng + lnb
    return gu, dgu, dgv, rstd, xhat, vn


def _tril_mask():
    r = lax.broadcasted_iota(jnp.int32, (ATT_BLOCK, ATT_BLOCK), 0)
    c = lax.broadcasted_iota(jnp.int32, (ATT_BLOCK, ATT_BLOCK), 1)
    return c <= r


def _triu_mask():
    r = lax.broadcasted_iota(jnp.int32, (ATT_BLOCK, ATT_BLOCK), 0)
    c = lax.broadcasted_iota(jnp.int32, (ATT_BLOCK, ATT_BLOCK), 1)
    return r <= c


def sgu_fwd(proj, lng, lnb, w, bcol, name):
    T = proj.shape[0]
    tm = _tile(T, 512, 128)
    nch = tm // ATT_BLOCK

    def body(u_ref, v_ref, lng_ref, lnb_ref, w_ref, b_ref, o_ref):
        tril = _tril_mask()
        for hd in range(SGU_HEADS):
            sl = slice(hd * HEAD_LANES, (hd + 1) * HEAD_LANES)
            u = u_ref[:, sl].astype(F32)
            v = v_ref[:, sl].astype(F32)
            gu, _, _, _, _, vn = _sgu_head_fwd(u, v, lng_ref[:, sl], lnb_ref[:, sl])
            wm = jnp.where(tril, w_ref[hd], 0.0).astype(BF16)
            vnb = vn.astype(BF16)
            bc = b_ref[:, hd:hd + 1]
            for ch in range(nch):
                rs = slice(ch * ATT_BLOCK, (ch + 1) * ATT_BLOCK)
                z = _nn(wm, vnb[rs, :]) + bc
                o_ref[rs, sl] = (gu[rs, :] * z).astype(BF16)

    return pl.pallas_call(
        body, name=name, grid=(T // tm,),
        in_specs=[pl.BlockSpec((tm, MIX_HALF), lambda i: (i, 0)),
                  pl.BlockSpec((tm, MIX_HALF), lambda i: (i, 1)),
                  pl.BlockSpec((1, MIX_HALF), lambda i: (0, 0)),
                  pl.BlockSpec((1, MIX_HALF), lambda i: (0, 0)),
                  pl.BlockSpec(w.shape, lambda i: (0, 0, 0)),
                  pl.BlockSpec(bcol.shape, lambda i: (0, 0))],
        out_specs=pl.BlockSpec((tm, MIX_HALF), lambda i: (i, 0)),
        out_shape=jax.ShapeDtypeStruct((T, MIX_HALF), BF16),
        compiler_params=_params(("parallel",)),
    )(proj, proj, lng, lnb, w, bcol)


def sgu_bwd(proj, dmixed, lng, lnb, w, wt, bcol, name):
    T = proj.shape[0]
    tm = _tile(T, 512, 128)
    nch = tm // ATT_BLOCK
    nsteps = T // tm

    def body(u_ref, v_ref, g_ref, lng_ref, lnb_ref, w_ref, wt_ref, b_ref, duv_ref, dw_ref, dvec_ref, bacc_ref):
        step = pl.program_id(0)

        @pl.when(step == 0)
        def _():
            dw_ref[...] = jnp.zeros_like(dw_ref)
            dvec_ref[...] = jnp.zeros_like(dvec_ref)
            bacc_ref[...] = jnp.zeros_like(bacc_ref)

        tril = _tril_mask()
        triu = _triu_mask()
        for hd in range(SGU_HEADS):
            sl = slice(hd * HEAD_LANES, (hd + 1) * HEAD_LANES)
            u = u_ref[:, sl].astype(F32)
            v = v_ref[:, sl].astype(F32)
            lng_h = lng_ref[:, sl]
            gu, dgu, dgv, rstd, xhat, vn = _sgu_head_fwd(u, v, lng_h, lnb_ref[:, sl])
            wm = jnp.where(tril, w_ref[hd], 0.0).astype(BF16)
            wmt = jnp.where(triu, wt_ref[hd], 0.0).astype(BF16)
            vnb = vn.astype(BF16)
            bc = b_ref[:, hd:hd + 1]
            g = g_ref[:, sl].astype(F32)
            dw_acc = jnp.zeros((ATT_BLOCK, ATT_BLOCK), F32)
            b_acc = jnp.zeros((ATT_BLOCK, HEAD_LANES), F32)
            dvn_parts = []
            for ch in range(nch):
                rs = slice(ch * ATT_BLOCK, (ch + 1) * ATT_BLOCK)
                z = _nn(wm, vnb[rs, :]) + bc
                duv_ref[rs, sl] = (g[rs, :] * z * dgu[rs, :]).astype(BF16)
                dz = g[rs, :] * gu[rs, :]
                dzb = dz.astype(BF16)
                dvn_parts.append(_nn(wmt, dzb))
                dw_acc = dw_acc + _nt(dzb, vnb[rs, :])
                b_acc = b_acc + dz
            dvn = jnp.concatenate(dvn_parts, axis=0)
            dw_ref[hd] += jnp.where(tril, dw_acc, 0.0)
            bacc_ref[hd] += b_acc
            dvec_ref[0:1, sl] += jnp.sum(dvn * xhat, axis=0, keepdims=True)
            dvec_ref[1:2, sl] += jnp.sum(dvn, axis=0, keepdims=True)
            dxh = dvn * lng_h
            dgv_in = rstd * (dxh - jnp.mean(dxh, axis=-1, keepdims=True)
                             - xhat * jnp.mean(dxh * xhat, axis=-1, keepdims=True))
            duv_ref[:, MIX_HALF + hd * HEAD_LANES:MIX_HALF + (hd + 1) * HEAD_LANES] = (dgv_in * dgv).astype(BF16)

        @pl.when(step == nsteps - 1)
        def _():
            for hd in range(SGU_HEADS):
                sl = slice(hd * HEAD_LANES, (hd + 1) * HEAD_LANES)
                dvec_ref[2:3, sl] = jnp.sum(bacc_ref[hd].T, axis=0, keepdims=True)

    return pl.pallas_call(
        body, name=name, grid=(nsteps,),
        in_specs=[pl.BlockSpec((tm, MIX_HALF), lambda i: (i, 0)),
                  pl.BlockSpec((tm, MIX_HALF), lambda i: (i, 1)),
                  pl.BlockSpec((tm, MIX_HALF), lambda i: (i, 0)),
                  pl.BlockSpec((1, MIX_HALF), lambda i: (0, 0)),
                  pl.BlockSpec((1, MIX_HALF), lambda i: (0, 0)),
                  pl.BlockSpec(w.shape, lambda i: (0, 0, 0)),
                  pl.BlockSpec(w.shape, lambda i: (0, 0, 0)),
                  pl.BlockSpec(bcol.shape, lambda i: (0, 0))],
        out_specs=[pl.BlockSpec((tm, 2 * MIX_HALF), lambda i: (i, 0)),
                   pl.BlockSpec(w.shape, lambda i: (0, 0, 0)),
                   pl.BlockSpec((8, MIX_HALF), lambda i: (0, 0))],
        out_shape=[jax.ShapeDtypeStruct((T, 2 * MIX_HALF), BF16),
                   jax.ShapeDtypeStruct(w.shape, F32),
                   jax.ShapeDtypeStruct((8, MIX_HALF), F32)],
        scratch_shapes=[pltpu.VMEM((SGU_HEADS, ATT_BLOCK, HEAD_LANES), F32)],
        compiler_params=_params(("arbitrary",)),
    )(proj, proj, dmixed, lng, lnb, w, wt, bcol)


def _rot_half(t):
    lane = lax.broadcasted_iota(jnp.int32, t.shape, 1)
    first = (lane % ATT_DH) < (ATT_DH // 2)
    return jnp.where(first, -pltpu.roll(t, HEAD_LANES - ATT_DH // 2, 1), pltpu.roll(t, ATT_DH // 2, 1))


def rope_fwd(proj, cos, sin, name):
    T = proj.shape[0]
    tm = _tile(T, 512, 16)
    scale = 1.0 / math.sqrt(ATT_DH)

    def body(q_ref, k_ref, cos_ref, sin_ref, qo_ref, ko_ref):
        c = cos_ref[...]
        s = sin_ref[...]
        for hp in range(MIX_HALF // HEAD_LANES):
            sl = slice(hp * HEAD_LANES, (hp + 1) * HEAD_LANES)
            q = q_ref[:, sl].astype(F32)
            k = k_ref[:, sl].astype(F32)
            qo_ref[:, sl] = (scale * (q * c + _rot_half(q) * s)).astype(BF16)
            ko_ref[:, sl] = (k * c + _rot_half(k) * s).astype(BF16)

    out = jax.ShapeDtypeStruct((T, MIX_HALF), BF16)
    return pl.pallas_call(
        body, name=name, grid=(T // tm,),
        in_specs=[pl.BlockSpec((tm, MIX_HALF), lambda i: (i, 2)),
                  pl.BlockSpec((tm, MIX_HALF), lambda i: (i, 3)),
                  pl.BlockSpec((tm, HEAD_LANES), lambda i: (i, 0)),
                  pl.BlockSpec((tm, HEAD_LANES), lambda i: (i, 0))],
        out_specs=[pl.BlockSpec((tm, MIX_HALF), lambda i: (i, 0))] * 2,
        out_shape=[out, out],
        compiler_params=_params(("parallel",)),
    )(proj, proj, cos, sin)


def rope_bwd(dqs, dks, dvs, cos, sin, name):
    T = dqs[0].shape[0]
    tm = _tile(T, 512, 16)
    scale = 1.0 / math.sqrt(ATT_DH)
    npat = len(dqs)

    def body(*refs):
        dq_refs, dk_refs, dv_refs = refs[:npat], refs[npat:2 * npat], refs[2 * npat:3 * npat]
        cos_ref, sin_ref, o_ref = refs[3 * npat:]
        c = cos_ref[...]
        s = sin_ref[...]
        for hp in range(MIX_HALF // HEAD_LANES):
            sl = slice(hp * HEAD_LANES, (hp + 1) * HEAD_LANES)
            gq = scale * sum(r[:, sl] for r in dq_refs)
            gk = sum(r[:, sl] for r in dk_refs)
            gv = sum(r[:, sl] for r in dv_refs)
            o_ref[:, sl] = (gq * c - _rot_half(gq * s)).astype(BF16)
            o_ref[:, MIX_HALF + hp * HEAD_LANES:MIX_HALF + (hp + 1) * HEAD_LANES] = (
                gk * c - _rot_half(gk * s)).astype(BF16)
            o_ref[:, 2 * MIX_HALF + hp * HEAD_LANES:2 * MIX_HALF + (hp + 1) * HEAD_LANES] = gv.astype(BF16)

    return pl.pallas_call(
        body, name=name, grid=(T // tm,),
        in_specs=([pl.BlockSpec((tm, MIX_HALF), lambda i: (i, 0))] * (3 * npat)
                  + [pl.BlockSpec((tm, HEAD_LANES), lambda i: (i, 0))] * 2),
        out_specs=pl.BlockSpec((tm, 3 * MIX_HALF), lambda i: (i, 0)),
        out_shape=jax.ShapeDtypeStruct((T, 3 * MIX_HALF), BF16),
        compiler_params=_params(("parallel",)),
    )(*dqs, *dks, *dvs, cos, sin)


def _band_masks(n):
    r = lax.broadcasted_iota(jnp.int32, (2 * ATT_BLOCK, ATT_BLOCK), 0)
    c = lax.broadcasted_iota(jnp.int32, (2 * ATT_BLOCK, ATT_BLOCK), 1)
    qi = r % ATT_BLOCK
    head = (c < ATT_DH) == (r < ATT_BLOCK)
    return (c >= qi) & (n > 0), c <= qi, head, c[:ATT_BLOCK] < ATT_DH


def _stack_heads(x, head):
    x2 = jnp.concatenate([x, x], axis=0)
    return jnp.where(head, x2, jnp.zeros_like(x2))


def attn_fwd(q, k, v, name):
    d, L, W = q.shape
    nb = L // ATT_BLOCK

    def body(q_ref, kp_ref, kc_ref, vp_ref, vc_ref, o_ref, lse_ref):
        mask_p, mask_c, head, head0 = _band_masks(pl.program_id(1))
        for hp in range(W // HEAD_LANES):
            sl = slice(hp * HEAD_LANES, (hp + 1) * HEAD_LANES)
            kp, kc, vp, vc = kp_ref[0, :, sl], kc_ref[0, :, sl], vp_ref[0, :, sl], vc_ref[0, :, sl]
            qs = _stack_heads(q_ref[0, :, sl], head)
            sp = jnp.where(mask_p, _nt(qs, kp), NEG)
            sc = jnp.where(mask_c, _nt(qs, kc), NEG)
            m = jnp.maximum(jnp.max(sp, axis=1, keepdims=True), jnp.max(sc, axis=1, keepdims=True))
            pp = jnp.exp(sp - m)
            pc = jnp.exp(sc - m)
            den = jnp.sum(pp, axis=1, keepdims=True) + jnp.sum(pc, axis=1, keepdims=True)
            o = (_nn(pp.astype(BF16), vp) + _nn(pc.astype(BF16), vc)) / den
            lse = m + jnp.log(den)
            o_ref[0, :, sl] = jnp.where(head0, o[:ATT_BLOCK], o[ATT_BLOCK:])
            lse_ref[0, :, sl] = jnp.where(head0, lse[:ATT_BLOCK], lse[ATT_BLOCK:])

    cur = pl.BlockSpec((1, ATT_BLOCK, W), lambda r, n: (r, n, 0))
    prev = pl.BlockSpec((1, ATT_BLOCK, W), lambda r, n: (r, jnp.maximum(n - 1, 0), 0))
    out = jax.ShapeDtypeStruct((d, L, W), F32)
    return pl.pallas_call(
        body, name=name, grid=(d, nb),
        in_specs=[cur, prev, cur, prev, cur],
        out_specs=[cur, cur], out_shape=[out, out],
        compiler_params=_params(("parallel", "parallel")),
    )(q, k, k, v, v)


def attn_combine(os_, lses, name):
    T, W = os_[0].shape
    tm = _tile(T, 512, 16)
    npat = len(os_)

    def body(*refs):
        o_refs, l_refs = refs[:npat], refs[npat:2 * npat]
        out_ref, lse_ref = refs[2 * npat:]
        ls = [r[...] for r in l_refs]
        m = ls[0]
        for l in ls[1:]:
            m = jnp.maximum(m, l)
        es = [jnp.exp(l - m) for l in ls]
        z = es[0]
        for e in es[1:]:
            z = z + e
        acc = es[0] * o_refs[0][...]
        for e, r in zip(es[1:], o_refs[1:]):
            acc = acc + e * r[...]
        out_ref[...] = (acc / z).astype(BF16)
        lse_ref[...] = m + jnp.log(z)

    blk = pl.BlockSpec((tm, W), lambda i: (i, 0))
    return pl.pallas_call(
        body, name=name, grid=(T // tm,),
        in_specs=[blk] * (2 * npat), out_specs=[blk, blk],
        out_shape=[jax.ShapeDtypeStruct((T, W), BF16), jax.ShapeDtypeStruct((T, W), F32)],
        compiler_params=_params(("parallel",)),
    )(*os_, *lses)


def attn_bwd(q, k, v, do, o, lse, name):
    d, L, W = q.shape
    nb = L // ATT_BLOCK

    def body(q_ref, kp_ref, kc_ref, vp_ref, vc_ref, do_ref, o_ref, lse_ref, dq_ref, dk_ref, dv_ref, kkeep, vkeep):
        n = pl.program_id(1)

        @pl.when(n < nb)
        def _():
            mask_p, mask_c, head, head0 = _band_masks(n)
            for hp in range(W // HEAD_LANES):
                sl = slice(hp * HEAD_LANES, (hp + 1) * HEAD_LANES)
                kp, kc, vp, vc = kp_ref[0, :, sl], kc_ref[0, :, sl], vp_ref[0, :, sl], vc_ref[0, :, sl]
                dout = do_ref[0, :, sl]
                qs = _stack_heads(q_ref[0, :, sl], head)
                dos = _stack_heads(dout, head)
                lse_v = lse_ref[0, :, sl]
                lse_c = jnp.max(jnp.where(head, jnp.concatenate([lse_v, lse_v], axis=0), NEG), axis=1, keepdims=True)
                delta = jnp.sum(_stack_heads(dout.astype(F32) * o_ref[0, :, sl].astype(F32), head), axis=1, keepdims=True)
                pp = jnp.exp(jnp.where(mask_p, _nt(qs, kp), NEG) - lse_c)
                pc = jnp.exp(jnp.where(mask_c, _nt(qs, kc), NEG) - lse_c)
                dsp = (pp * (_nt(dos, vp) - delta)).astype(BF16)
                dsc = (pc * (_nt(dos, vc) - delta)).astype(BF16)
                dq2 = _nn(dsp, kp) + _nn(dsc, kc)
                dq_ref[0, :, sl] = jnp.where(head0, dq2[:ATT_BLOCK], dq2[ATT_BLOCK:])
                kprev = _tn(dsp, qs)
                vprev = _tn(pp.astype(BF16), dos)

                @pl.when(n > 0)
                def _():
                    dk_ref[0, :, sl] = kkeep[:, sl] + kprev
                    dv_ref[0, :, sl] = vkeep[:, sl] + vprev

                kkeep[:, sl] = _tn(dsc, qs)
                vkeep[:, sl] = _tn(pc.astype(BF16), dos)

        @pl.when(n == nb)
        def _():
            dk_ref[0] = kkeep[...]
            dv_ref[0] = vkeep[...]

    cur = pl.BlockSpec((1, ATT_BLOCK, W), lambda r, n: (r, jnp.minimum(n, nb - 1), 0))
    prev = pl.BlockSpec((1, ATT_BLOCK, W), lambda r, n: (r, jnp.clip(n - 1, 0, nb - 1), 0))
    out = jax.ShapeDtypeStruct((d, L, W), F32)
    return pl.pallas_call(
        body, name=name, grid=(d, nb + 1),
        in_specs=[cur, prev, cur, prev, cur, cur, cur, cur],
        out_specs=[cur, prev, prev], out_shape=[out, out, out],
        scratch_shapes=[pltpu.VMEM((ATT_BLOCK, W), F32), pltpu.VMEM((ATT_BLOCK, W), F32)],
        compiler_params=_params(("parallel", "arbitrary")),
    )(q, k, k, v, v, do, o, lse)


def final_loss_bwd(h, gf, tgt, name):
    T, D = h.shape
    tm = _tile(T, 512, 8)

    def body(h_ref, g_ref, t_ref, dh_ref, red_ref):
        x = h_ref[...]
        r = lax.rsqrt(jnp.mean(x * x, axis=-1, keepdims=True) + EPS)
        n = x * r
        g = g_ref[...]
        err = n * g - t_ref[...]
        dy = err * (1.0 / D)

        @pl.when(pl.program_id(0) == 0)
        def _():
            red_ref[...] = jnp.zeros_like(red_ref)

        red_ref[0:1, :] += jnp.sum(dy * n, axis=0, keepdims=True)
        red_ref[1:2, :] += jnp.zeros((1, D), F32) + (0.5 / D) * jnp.sum(err * err, keepdims=True)
        dn = dy * g
        dh_ref[...] = r * (dn - n * jnp.mean(dn * n, axis=-1, keepdims=True))

    return pl.pallas_call(
        body, name=name, grid=(T // tm,),
        in_specs=[pl.BlockSpec((tm, D), lambda i: (i, 0)),
                  pl.BlockSpec((1, D), lambda i: (0, 0)),
                  pl.BlockSpec((tm, D), lambda i: (i, 0))],
        out_specs=[pl.BlockSpec((tm, D), lambda i: (i, 0)), pl.BlockSpec((8, D), lambda i: (0, 0))],
        out_shape=[jax.ShapeDtypeStruct((T, D), F32), jax.ShapeDtypeStruct((8, D), F32)],
        compiler_params=_params(("arbitrary",)),
    )(h, gf, tgt)


def ada_fwd(c_all, ada_w, ada_b, name):
    nl, D, N = ada_w.shape

    def body(c_ref, w_ref, b_ref, o_ref):
        c = c_ref[...]
        o_ref[0] = _nn(c * _sigmoid(c), w_ref[0]) + b_ref[0]

    return pl.pallas_call(
        body, name=name, grid=(nl,),
        in_specs=[pl.BlockSpec((N_DEV, D), lambda l: (0, 0)),
                  pl.BlockSpec((1, D, N), lambda l: (l, 0, 0)),
                  pl.BlockSpec((1, 1, N), lambda l: (l, 0, 0))],
        out_specs=pl.BlockSpec((1, N_DEV, N), lambda l: (l, 0, 0)),
        out_shape=jax.ShapeDtypeStruct((nl, N_DEV, N), F32),
        compiler_params=_params(("parallel",)),
    )(c_all, ada_w, ada_b)


def ada_bwd(c_allT, dmod, name):
    nl, _, N = dmod.shape
    D = c_allT.shape[0]

    def body(c_ref, g_ref, o_ref):
        c = c_ref[...]
        ca = c * _sigmoid(c)
        acc = ca[:, 0:1] * g_ref[0, 0:1, :]
        for b in range(1, N_DEV):
            acc = acc + ca[:, b:b + 1] * g_ref[0, b:b + 1, :]
        o_ref[0] = acc

    return pl.pallas_call(
        body, name=name, grid=(nl,),
        in_specs=[pl.BlockSpec((D, N_DEV), lambda l: (0, 0)),
                  pl.BlockSpec((1, N_DEV, N), lambda l: (l, 0, 0))],
        out_specs=pl.BlockSpec((1, D, N), lambda l: (l, 0, 0)),
        out_shape=jax.ShapeDtypeStruct((nl, D, N), F32),
        compiler_params=_params(("parallel",)),
    )(c_allT, dmod)


def adamw(w, g, m, v, name):
    R, C = w.shape
    tr = _tile(R, max(8, (1 << 19) // C // 8 * 8), 8)
    c1 = 1.0 - ADAM_B1 ** ADAM_STEP
    c2 = 1.0 - ADAM_B2 ** ADAM_STEP

    def body(w_ref, g_ref, m_ref, v_ref, d_ref, mo_ref, vo_ref):
        gv = g_ref[...]
        mn = ADAM_B1 * m_ref[...] + (1.0 - ADAM_B1) * gv
        vn = ADAM_B2 * v_ref[...] + (1.0 - ADAM_B2) * (gv * gv)
        mo_ref[...] = mn
        vo_ref[...] = vn
        d_ref[...] = -ADAM_LR * ((mn / c1) / (jnp.sqrt(vn / c2) + ADAM_EPS) + ADAM_WD * w_ref[...])

    blk = pl.BlockSpec((tr, C), lambda i: (i, 0))
    out = jax.ShapeDtypeStruct((R, C), F32)
    return pl.pallas_call(
        body, name=name, grid=(R // tr,),
        in_specs=[blk] * 4, out_specs=[blk] * 3, out_shape=[out] * 3,
        compiler_params=_params(("parallel",)),
    )(w, g, m, v)


def sum_halves(g, lands, c_idx, name):
    n, ns, _, rh, D = g.shape

    def body(c_ref, g_ref, l_ref, o_ref):
        o_ref[0, 0] = (g_ref[0, 0, 0].astype(F32) + l_ref[0, 0].astype(F32)).astype(BF16)

    return pl.pallas_call(
        body, name=name,
        grid_spec=pltpu.PrefetchScalarGridSpec(
            num_scalar_prefetch=1, grid=(n, ns),
            in_specs=[pl.BlockSpec((1, 1, 1, rh, D), lambda i, j, c: (i, j, c[0], 0, 0)),
                      pl.BlockSpec((1, 1, rh, D), lambda i, j, c: (i, j, 0, 0))],
            out_specs=pl.BlockSpec((1, 1, rh, D), lambda i, j, c: (i, j, 0, 0))),
        out_shape=jax.ShapeDtypeStruct((n, ns, rh, D), BF16),
        compiler_params=_params(("parallel", "parallel")),
    )(c_idx, g, lands)


def sum_chips(p, lands, place, name):
    n, ns, rh, D = p.shape

    def body(c_ref, p_ref, l_ref, o_ref):
        acc = p_ref[0, 0].astype(F32)
        for j in range(N_CHIP - 1):
            acc = acc + l_ref[j, 0].astype(F32)
        o_ref[0, 0] = acc

    return pl.pallas_call(
        body, name=name,
        grid_spec=pltpu.PrefetchScalarGridSpec(
            num_scalar_prefetch=1, grid=(n,),
            in_specs=[pl.BlockSpec((1, 1, rh, D), lambda i, c: (i, c[0], 0, 0)),
                      pl.BlockSpec((N_CHIP - 1, 1, rh, D), lambda i, c: (0, i, 0, 0))],
            out_specs=pl.BlockSpec((1, 1, rh, D), lambda i, c: (i, c[1], 0, 0))),
        out_shape=jax.ShapeDtypeStruct((n, 2, rh, D), F32),
        compiler_params=_params(("parallel",)),
    )(place, p, lands)


def _my_place():
    return lax.axis_index("x"), lax.axis_index("y"), lax.axis_index("c")


def _other_chips(mx, my):
    return [(1 - mx, my), (mx, 1 - my), (1 - mx, 1 - my)]


def gather_small(x, name):
    def body(x_ref, out_ref, sum_ref, send_sems, recv_sems):
        mx, my, mc = _my_place()
        me = 4 * mx + 2 * my + mc
        out_ref[me] = x_ref[...]
        sends = []
        for k in range(1, N_DEV):
            kx, ky, kc = (k >> 2) & 1, (k >> 1) & 1, k & 1
            peer = (1 - mx if kx else mx, 1 - my if ky else my, 1 - mc if kc else mc)
            cp = pltpu.make_async_remote_copy(
                src_ref=x_ref, dst_ref=out_ref.at[me], send_sem=send_sems.at[k - 1], recv_sem=recv_sems.at[k - 1],
                device_id=peer, device_id_type=MESH)
            cp.start()
            sends.append((cp, 4 * peer[0] + 2 * peer[1] + peer[2], peer))
        for k, (cp, peer_slot, peer) in enumerate(sends):
            pltpu.make_async_remote_copy(
                src_ref=x_ref, dst_ref=out_ref.at[peer_slot], send_sem=send_sems.at[k], recv_sem=recv_sems.at[k],
                device_id=peer, device_id_type=MESH).wait_recv()
        for cp, _, _ in sends:
            cp.wait_send()
        acc = out_ref[0]
        for s in range(1, N_DEV):
            acc = acc + out_ref[s]
        sum_ref[...] = acc

    vmem = pl.BlockSpec(memory_space=pltpu.VMEM)
    return pl.pallas_call(
        body, name=name,
        in_specs=[vmem], out_specs=[vmem, vmem],
        out_shape=[jax.ShapeDtypeStruct((N_DEV,) + x.shape, x.dtype), jax.ShapeDtypeStruct(x.shape, x.dtype)],
        scratch_shapes=[pltpu.SemaphoreType.DMA((N_DEV - 1,)), pltpu.SemaphoreType.DMA((N_DEV - 1,))],
        compiler_params=pltpu.CompilerParams(vmem_limit_bytes=VMEM_LIMIT),
    )(x)


def gather_weights(shards, name):
    K = len(shards)

    def body(*refs):
        ins, outs = refs[:K], refs[K:2 * K]
        send1, recv1, send2, recv2, send0, recv0 = refs[2 * K:]
        mx, my, mc = _my_place()
        ci = 2 * mx + my
        chips = _other_chips(mx, my)
        local = [pltpu.make_async_remote_copy(
            src_ref=ins[k], dst_ref=outs[k].at[:, ci], send_sem=send0.at[k], recv_sem=recv0.at[k],
            device_id=(mx, my, 1 - mc), device_id_type=MESH) for k in range(K)]
        for cp in local:
            cp.start()

        def half(k, chip_idx, hc):
            return outs[k].at[:, chip_idx, hc]

        first, passed = [], []
        for j, (cx, cy) in enumerate(chips):
            for k in range(K):
                cp = pltpu.make_async_remote_copy(
                    src_ref=ins[k].at[:, mc], dst_ref=half(k, ci, mc),
                    send_sem=send1.at[k * 3 + j], recv_sem=recv1.at[k * 3 + j],
                    device_id=(cx, cy, mc), device_id_type=MESH)
                cp.start()
                first.append(cp)
        for j, (cx, cy) in enumerate(chips):
            cj = 2 * cx + cy
            for k in range(K):
                pltpu.make_async_remote_copy(
                    src_ref=ins[k].at[:, mc], dst_ref=half(k, cj, mc),
                    send_sem=send1.at[k * 3 + j], recv_sem=recv1.at[k * 3 + j],
                    device_id=(cx, cy, mc), device_id_type=MESH).wait_recv()
                cp = pltpu.make_async_remote_copy(
                    src_ref=half(k, cj, mc), dst_ref=half(k, cj, mc),
                    send_sem=send2.at[k * 3 + j], recv_sem=recv2.at[k * 3 + j],
                    device_id=(mx, my, 1 - mc), device_id_type=MESH)
                cp.start()
                passed.append(cp)
        for j, (cx, cy) in enumerate(chips):
            cj = 2 * cx + cy
            for k in range(K):
                pltpu.make_async_remote_copy(
                    src_ref=half(k, cj, 1 - mc), dst_ref=half(k, cj, 1 - mc),
                    send_sem=send2.at[k * 3 + j], recv_sem=recv2.at[k * 3 + j],
                    device_id=(mx, my, 1 - mc), device_id_type=MESH).wait_recv()
        for cp in first + passed:
            cp.wait_send()
        for cp in local:
            cp.wait()

    hbm = pl.BlockSpec(memory_space=pl.ANY)
    return pl.pallas_call(
        body, name=name,
        in_specs=[hbm] * K, out_specs=[hbm] * K,
        out_shape=[jax.ShapeDtypeStruct((s.shape[0], N_CHIP) + s.shape[1:], s.dtype) for s in shards],
        scratch_shapes=[pltpu.SemaphoreType.DMA((3 * K,))] * 4 + [pltpu.SemaphoreType.DMA((K,))] * 2,
    )(*shards)


def sibling_send_half(gs, name):
    K = len(gs)

    def body(*refs):
        ins, outs = refs[:K], refs[K:2 * K]
        send, recv = refs[2 * K:]
        mx, my, mc = _my_place()
        cps = []
        for k in range(K):
            cp = pltpu.make_async_remote_copy(
                src_ref=ins[k].at[:, :, 1 - mc], dst_ref=outs[k], send_sem=send.at[k], recv_sem=recv.at[k],
                device_id=(mx, my, 1 - mc), device_id_type=MESH)
            cp.start()
            cps.append(cp)
        for cp in cps:
            cp.wait()

    hbm = pl.BlockSpec(memory_space=pl.ANY)
    return pl.pallas_call(
        body, name=name,
        in_specs=[hbm] * K, out_specs=[hbm] * K,
        out_shape=[jax.ShapeDtypeStruct(g.shape[:2] + g.shape[3:], g.dtype) for g in gs],
        scratch_shapes=[pltpu.SemaphoreType.DMA((K,)), pltpu.SemaphoreType.DMA((K,))],
    )(*gs)


def scatter_to_chips(ps, name):
    K = len(ps)

    def body(*refs):
        ins, outs = refs[:K], refs[K:2 * K]
        send, recv = refs[2 * K:]
        mx, my, mc = _my_place()
        cps = []
        for j, (cx, cy) in enumerate(_other_chips(mx, my)):
            for k in range(K):
                cp = pltpu.make_async_remote_copy(
                    src_ref=ins[k].at[:, 2 * cx + cy], dst_ref=outs[k].at[j],
                    send_sem=send.at[k * 3 + j], recv_sem=recv.at[k * 3 + j],
                    device_id=(cx, cy, mc), device_id_type=MESH)
                cp.start()
                cps.append(cp)
        for cp in cps:
            cp.wait()

    hbm = pl.BlockSpec(memory_space=pl.ANY)
    return pl.pallas_call(
        body, name=name,
        in_specs=[hbm] * K, out_specs=[hbm] * K,
        out_shape=[jax.ShapeDtypeStruct((N_CHIP - 1, p.shape[0]) + p.shape[2:], p.dtype) for p in ps],
        scratch_shapes=[pltpu.SemaphoreType.DMA((3 * K,)), pltpu.SemaphoreType.DMA((3 * K,))],
    )(*ps)


def sibling_complete(ss, name):
    K = len(ss)

    def body(*refs):
        ins, outs = refs[:K], refs[K:2 * K]
        send, recv = refs[2 * K:]
        mx, my, mc = _my_place()
        cps = []
        for k in range(K):
            cp = pltpu.make_async_remote_copy(
                src_ref=ins[k].at[:, mc], dst_ref=outs[k].at[:, mc], send_sem=send.at[k], recv_sem=recv.at[k],
                device_id=(mx, my, 1 - mc), device_id_type=MESH)
            cp.start()
            cps.append(cp)
        for k in range(K):
            pltpu.make_async_remote_copy(
                src_ref=ins[k].at[:, mc], dst_ref=outs[k].at[:, 1 - mc], send_sem=send.at[k], recv_sem=recv.at[k],
                device_id=(mx, my, 1 - mc), device_id_type=MESH).wait_recv()
        for cp in cps:
            cp.wait_send()

    hbm = pl.BlockSpec(memory_space=pl.ANY)
    return pl.pallas_call(
        body, name=name,
        in_specs=[hbm] * K, out_specs=[hbm] * K,
        out_shape=[jax.ShapeDtypeStruct(s.shape, s.dtype) for s in ss],
        scratch_shapes=[pltpu.SemaphoreType.DMA((K,)), pltpu.SemaphoreType.DMA((K,))],
        input_output_aliases={k: k for k in range(K)},
    )(*ss)


def _rope_tables(T):
    inv = ROPE_THETA ** (-jnp.arange(0, ATT_DH, 2, dtype=F32) / ATT_DH)
    ang = jnp.arange(T, dtype=F32)[:, None] * inv[None, :]
    ang = jnp.concatenate([ang, ang, ang, ang], axis=-1)
    return jnp.cos(ang), jnp.sin(ang)


def _to_residues(x, d):
    T, W = x.shape
    if d == 1:
        return x.reshape(1, T, W)
    return x.reshape(T // d, d, W).transpose(1, 0, 2)


def _from_residues(x):
    d, L, W = x.shape
    if d == 1:
        return x.reshape(L, W)
    return x.transpose(1, 0, 2).reshape(L * d, W)


def _ffn_fwd(h, ng, i_n, mod, i0, wgT, wuT, wd, tag):
    y = normmod_fwd(h, ng, i_n, mod, i0, i0 + 1, f"normmod_{tag}")
    a, b, s = ffn_up(y, wgT, wuT, f"ffn_up_{tag}")
    hn, o = resid_matmul([s], wd, h, mod, i0 + 2, 0.5, f"ffn_down_{tag}")
    return hn, (h, y, a, b, s, o)


def _ffn_bwd(dh, res, ng, i_n, mod, i0, wgT, wuT, wd, gbuf, slot0, tag):
    h, y, a, b, s, o = res
    F = _wrows(wgT)
    do, red_g = gate_bwd(dh, o, mod, i0 + 2, 0.5, f"gate_bwd_{tag}")
    da, db = ffn_bwd_mid(do, wd, a, b, f"ffn_bwd_mid_{tag}")
    dh_new, red_n = dy_normbwd([(da, 0, wgT, 0, F), (db, 0, wuT, 0, F)], h, dh, ng, i_n, mod, i0 + 1,
                               f"ffn_bwd_dy_{tag}")
    gbuf = matmul_tn(da, y, gbuf, slot0, 0, f"dwg_{tag}")
    gbuf = matmul_tn(db, y, gbuf, slot0 + 1, 0, f"dwu_{tag}")
    gbuf = matmul_tn(s, do, gbuf, slot0 + 2, 0, f"dwd_{tag}")
    return dh_new, gbuf, red_n, red_g


def _mixer_fwd(h, ng, mod, w_inT, w_out, sgu, cos, sin, tag):
    lng, lnb, sw, swt, bcol = sgu
    y = normmod_fwd(h, ng, 1, mod, 3, 4, f"normmod_{tag}")
    proj = matmul_nt(y, w_inT, f"proj_{tag}")
    out_a = sgu_fwd(proj, lng, lnb, sw, bcol, f"sgu_fwd_{tag}")
    qr, kr = rope_fwd(proj, cos, sin, f"rope_fwd_{tag}")
    vv = proj[:, 4 * MIX_HALF:]
    os_, lses, qkv_res = [], [], []
    for d in DILATIONS:
        qd, kd, vd = _to_residues(qr, d), _to_residues(kr, d), _to_residues(vv, d)
        o_d, lse_d = attn_fwd(qd, kd, vd, f"attn_fwd_d{d}_{tag}")
        os_.append(_from_residues(o_d))
        lses.append(_from_residues(lse_d))
        qkv_res.append((qd, kd, vd))
    out_b, lse = attn_combine(os_, lses, f"attn_combine_{tag}")
    hn, om = resid_matmul([out_a, out_b], w_out, h, mod, 5, 1.0, f"mix_out_{tag}")
    return hn, (h, y, proj, out_a, out_b, lse, qkv_res, om)


def _mixer_bwd(dh, res, ng, mod, w_inT, w_out, sgu, cos, sin, winbuf, woutbuf, slot, tag):
    lng, lnb, sw, swt, bcol = sgu
    h, y, proj, out_a, out_b, lse, qkv_res, om = res
    dom, red_g = gate_bwd(dh, om, mod, 5, 1.0, f"gate_bwd_{tag}")
    dmixed = matmul_nt(dom, w_out, f"dmixed_{tag}")
    woutbuf = matmul_tn(out_a, dom, woutbuf, slot, 0, f"dwout_a_{tag}", tmo_cap=MIX_HALF)
    woutbuf = matmul_tn(out_b, dom, woutbuf, slot, MIX_HALF, f"dwout_b_{tag}", tmo_cap=MIX_HALF)
    d_uv, d_sw, d_svec = sgu_bwd(proj, dmixed, lng, lnb, sw, swt, bcol, f"sgu_bwd_{tag}")
    dob = dmixed[:, MIX_HALF:]
    dqs, dks, dvs = [], [], []
    for d, (qd, kd, vd) in zip(DILATIONS, qkv_res):
        dq, dk, dv = attn_bwd(qd, kd, vd, _to_residues(dob, d), _to_residues(out_b, d), _to_residues(lse, d),
                              f"attn_bwd_d{d}_{tag}")
        dqs.append(_from_residues(dq))
        dks.append(_from_residues(dk))
        dvs.append(_from_residues(dv))
    d_qkv = rope_bwd(dqs, dks, dvs, cos, sin, f"rope_bwd_{tag}")
    pairs = ([(d_uv, p, w_inT, p, MIX_HALF) for p in range(2)]
             + [(d_qkv, p, w_inT, 2 + p, MIX_HALF) for p in range(3)])
    dh_new, red_n = dy_normbwd(pairs, h, dh, ng, 1, mod, 4, f"mix_bwd_dy_{tag}")
    winbuf = matmul_tn(d_uv, y, winbuf, slot, 0, f"dwin_uv_{tag}", tmo_cap=MIX_HALF)
    winbuf = matmul_tn(d_qkv, y, winbuf, slot, 2 * MIX_HALF, f"dwin_qkv_{tag}", tmo_cap=MIX_HALF)
    return dh_new, winbuf, woutbuf, d_sw, d_svec, red_n, red_g


def _local_step(x, tgt, mods, ngs, ffn_w, w_inT, w_out, sgus, gf):
    T, D = x.shape
    cos, sin = _rope_tables(T)
    gbuf = lax.empty((ffn_w.shape[0] * ffn_w.shape[1],) + ffn_w.shape[2:], BF16)
    winbuf = lax.empty(w_inT.shape, BF16)
    woutbuf = lax.empty(w_out.shape, BF16)
    h = x
    saved = []
    for l in range(2):
        h, r1 = _ffn_fwd(h, ngs[l], 0, mods[l], 0, (ffn_w, (l, 0)), (ffn_w, (l, 1)), (ffn_w, (l, 2)), f"l{l}f1")
        h, r2 = _mixer_fwd(h, ngs[l], mods[l], (w_inT, (l,)), (w_out, (l,)), sgus[l], cos, sin, f"l{l}mx")
        h, r3 = _ffn_fwd(h, ngs[l], 2, mods[l], 6, (ffn_w, (l, 3)), (ffn_w, (l, 4)), (ffn_w, (l, 5)), f"l{l}f2")
        saved.append((r1, r2, r3))
    dh, red_final = final_loss_bwd(h, gf, tgt, "final_loss_bwd")
    layer_grads = [None, None]
    for l in (1, 0):
        r1, r2, r3 = saved[l]
        dh, gbuf, rn3, rg3 = _ffn_bwd(dh, r3, ngs[l], 2, mods[l], 6, (ffn_w, (l, 3)), (ffn_w, (l, 4)), (ffn_w, (l, 5)),
                                      gbuf, 6 * l + 3, f"l{l}f2")
        dh, winbuf, woutbuf, d_sw, d_svec, rn2, rg2 = _mixer_bwd(
            dh, r2, ngs[l], mods[l], (w_inT, (l,)), (w_out, (l,)), sgus[l], cos, sin, winbuf, woutbuf, l, f"l{l}mx")
        dh, gbuf, rn1, rg1 = _ffn_bwd(dh, r1, ngs[l], 0, mods[l], 0, (ffn_w, (l, 0)), (ffn_w, (l, 1)), (ffn_w, (l, 2)),
                                      gbuf, 6 * l, f"l{l}f1")
        layer_grads[l] = dict(sgu_w=d_sw, sgu_vec=d_svec, red_n=(rn1, rn2, rn3), red_g=(rg1, rg2, rg3))
    return dh, (gbuf, winbuf, woutbuf), layer_grads, red_final


def _adam_out(w, g, m, v, name):
    shp = w.shape
    two_d = (-1, shp[-1])
    d, mn, vn = adamw(w.reshape(two_d), g.reshape(two_d), m.reshape(two_d), v.reshape(two_d), name)
    return g, d.reshape(shp), mn.reshape(shp), vn.reshape(shp)


def kernel(x, c, ada_w, ada_b, norm_g, ffn1_wg, ffn1_wu, ffn1_wd, ffn2_wg, ffn2_wu, ffn2_wd, w_in, sgu_ln_g, sgu_ln_b, sgu_w, sgu_b, w_out, final_g, loss_target, m_ada_w, m_ada_b, m_norm_g, m_ffn1_wg, m_ffn1_wu, m_ffn1_wd, m_ffn2_wg, m_ffn2_wu, m_ffn2_wd, m_w_in, m_sgu_ln_g, m_sgu_ln_b, m_sgu_w, m_sgu_b, m_w_out, m_final_g, v_ada_w, v_ada_b, v_norm_g, v_ffn1_wg, v_ffn1_wu, v_ffn1_wd, v_ffn2_wg, v_ffn2_wu, v_ffn2_wd, v_w_in, v_sgu_ln_g, v_sgu_ln_b, v_sgu_w, v_sgu_b, v_w_out, v_final_g):
    T, D = x.shape[1], x.shape[2]
    NL = ada_w.shape[0]
    mx, my, mc = _my_place()
    me = 4 * mx + 2 * my + mc
    ci = 2 * mx + my
    c_idx = jnp.reshape(mc, (1,)).astype(jnp.int32)
    place = jnp.stack([ci, mc]).astype(jnp.int32)

    def halves(a):
        n, r, _ = a.shape
        return a.reshape(n, 2, r // 2, D)

    ffn_shard = jnp.stack([jnp.swapaxes(ffn1_wg, 1, 2), jnp.swapaxes(ffn1_wu, 1, 2), ffn1_wd,
                           jnp.swapaxes(ffn2_wg, 1, 2), jnp.swapaxes(ffn2_wu, 1, 2), ffn2_wd], axis=1).astype(BF16)
    Fs = ffn_shard.shape[2]
    win_shard = jnp.swapaxes(w_in, 1, 2).astype(BF16)
    wout_shard = w_out.astype(BF16)
    ffn_full, win_full, wout_full = gather_weights(
        [halves(ffn_shard.reshape(NL * 6, Fs, D)), halves(win_shard), halves(wout_shard)], "gather_weights")
    ffn_w = ffn_full.reshape(NL, 6, N_CHIP * Fs, D)
    w_inT = win_full.reshape(NL, N_CHIP * win_shard.shape[1], D)
    w_outf = wout_full.reshape(NL, N_CHIP * wout_shard.shape[1], D)

    ngw = norm_g.shape[2]
    small_in = jnp.concatenate([jnp.pad(c, ((0, 7), (0, 0))),
                                jnp.pad(norm_g.reshape(NL * 3, ngw), ((0, 8 - NL * 3), (0, D - ngw)))], axis=0)
    small_all, _ = gather_small(small_in, "gather_c_normg")
    c_all = small_all[:, 0, :]
    ng_parts = small_all[0::2, 8:8 + NL * 3, :ngw]
    ngs = jnp.transpose(ng_parts, (1, 0, 2)).reshape(NL, 3, N_CHIP * ngw)

    nmod = ada_w.shape[2]
    ada_b_mine = lax.dynamic_slice_in_dim(ada_b, ci * nmod, nmod, axis=1).reshape(NL, 1, nmod)
    mod_part = ada_fwd(c_all, ada_w, ada_b_mine, "ada_fwd")
    mod_all, _ = gather_small(mod_part.reshape(NL * N_DEV, nmod), "gather_mod")
    mod_rows = lax.dynamic_index_in_dim(mod_all.reshape(N_DEV, NL, N_DEV, nmod), me, axis=2, keepdims=False)
    mods = jnp.transpose(mod_rows[0::2], (1, 0, 2)).reshape(NL, N_ADA, D)

    sgus = []
    for l in range(NL):
        sgus.append((sgu_ln_g[l].reshape(1, MIX_HALF), sgu_ln_b[l].reshape(1, MIX_HALF), sgu_w[l],
                     jnp.swapaxes(sgu_w[l], 1, 2), jnp.transpose(sgu_b[l])))

    grad_x, big, lg, red_final = _local_step(x[0], loss_target[0], mods, ngs, ffn_w, w_inT, w_outf, sgus,
                                             final_g.reshape(1, D))

    blocks = []
    for l in range(NL):
        blocks += list(lg[l]["red_n"]) + list(lg[l]["red_g"])
        blocks.append(jnp.pad(lg[l]["sgu_vec"], ((0, 0), (0, D - MIX_HALF))))
        blocks.append(lg[l]["sgu_w"].reshape(-1, D))
    blocks.append(red_final)
    offs, o = [], 0
    for b in blocks:
        offs.append(o)
        o += b.shape[0]
    small_all, small_sum = gather_small(jnp.concatenate(blocks, axis=0), "gather_small_grads")
    per_layer = 8
    loss = small_sum[offs[-1] + 1, 0]
    g_final_g = small_sum[offs[-1], :]
    g_norm_g, g_ada_b, g_lng, g_lnb, g_sb, g_sw, dmod_all = [], [], [], [], [], [], []
    for l in range(NL):
        b0 = per_layer * l
        rn = [small_sum[offs[b0 + i]:offs[b0 + i] + 8] for i in range(3)]
        rg = [small_sum[offs[b0 + 3 + i]:offs[b0 + 3 + i] + 8] for i in range(3)]
        g_norm_g.append(jnp.stack([rn[i][2] for i in range(3)], axis=0))
        g_ada_b.append(jnp.concatenate([jnp.stack([rn[i][0], rn[i][1], rg[i][0]], axis=0) for i in range(3)],
                                       axis=0).reshape(N_ADA * D))
        sv = small_sum[offs[b0 + 6]:offs[b0 + 6] + 8, :MIX_HALF]
        g_lng.append(sv[0].reshape(SGU_HEADS, HEAD_LANES))
        g_lnb.append(sv[1].reshape(SGU_HEADS, HEAD_LANES))
        g_sb.append(sv[2].reshape(SGU_HEADS, ATT_BLOCK))
        g_sw.append(small_sum[offs[b0 + 7]:offs[b0 + 7] + SGU_HEADS * ATT_BLOCK * HEAD_LANES // D].reshape(sgu_w.shape[1:]))
        rows = []
        for i in range(3):
            an = small_all[:, offs[b0 + i]:offs[b0 + i] + 2]
            ag = small_all[:, offs[b0 + 3 + i]:offs[b0 + 3 + i] + 1]
            rows += [an[:, 0], an[:, 1], ag[:, 0]]
        dmod_all.append(jnp.stack(rows, axis=1).reshape(N_DEV, N_ADA * D))
    dmod_all = jnp.stack(dmod_all, axis=0)
    dmod_mine = lax.dynamic_slice_in_dim(dmod_all, ci * nmod, nmod, axis=2)
    g_ada_w = ada_bwd(jnp.transpose(c_all), dmod_mine, "ada_bwd")
    g_ada_b = jnp.stack(g_ada_b, axis=0)
    g_norm_g_full = jnp.stack(g_norm_g, axis=0)
    g_norm_g_mine = lax.dynamic_slice_in_dim(g_norm_g_full, ci * ngw, ngw, axis=2)

    def split(a):
        n, r4, _ = a.shape
        return a.reshape(n, N_CHIP, 2, r4 // N_CHIP // 2, D)

    parts = [split(g) for g in big]
    lands = sibling_send_half(parts, "rs_sibling")
    psums = [sum_halves(g, ld, c_idx, f"rs_sum_halves_{i}") for i, (g, ld) in enumerate(zip(parts, lands))]
    lands2 = scatter_to_chips(psums, "rs_chips")
    ssums = [sum_chips(p, ld, place, f"rs_sum_chips_{i}") for i, (p, ld) in enumerate(zip(psums, lands2))]
    fin = sibling_complete(ssums, "rs_complete")
    gT_ffn = fin[0].reshape(NL, 6, Fs, D)
    g_w_in = jnp.swapaxes(fin[1].reshape(NL, -1, D), 1, 2)
    g_w_out = fin[2].reshape(NL, -1, D)
    g_ffn1_wg = jnp.swapaxes(gT_ffn[:, 0], 1, 2)
    g_ffn1_wu = jnp.swapaxes(gT_ffn[:, 1], 1, 2)
    g_ffn1_wd = gT_ffn[:, 2]
    g_ffn2_wg = jnp.swapaxes(gT_ffn[:, 3], 1, 2)
    g_ffn2_wu = jnp.swapaxes(gT_ffn[:, 4], 1, 2)
    g_ffn2_wd = gT_ffn[:, 5]

    triples = [
        ("ada_w", ada_w, g_ada_w, m_ada_w, v_ada_w),
        ("ada_b", ada_b, g_ada_b, m_ada_b, v_ada_b),
        ("norm_g", norm_g, g_norm_g_mine, m_norm_g, v_norm_g),
        ("ffn1_wg", ffn1_wg, g_ffn1_wg, m_ffn1_wg, v_ffn1_wg),
        ("ffn1_wu", ffn1_wu, g_ffn1_wu, m_ffn1_wu, v_ffn1_wu),
        ("ffn1_wd", ffn1_wd, g_ffn1_wd, m_ffn1_wd, v_ffn1_wd),
        ("ffn2_wg", ffn2_wg, g_ffn2_wg, m_ffn2_wg, v_ffn2_wg),
        ("ffn2_wu", ffn2_wu, g_ffn2_wu, m_ffn2_wu, v_ffn2_wu),
        ("ffn2_wd", ffn2_wd, g_ffn2_wd, m_ffn2_wd, v_ffn2_wd),
        ("w_in", w_in, g_w_in, m_w_in, v_w_in),
        ("sgu_ln_g", sgu_ln_g, jnp.stack(g_lng, axis=0), m_sgu_ln_g, v_sgu_ln_g),
        ("sgu_ln_b", sgu_ln_b, jnp.stack(g_lnb, axis=0), m_sgu_ln_b, v_sgu_ln_b),
        ("sgu_w", sgu_w, jnp.stack(g_sw, axis=0), m_sgu_w, v_sgu_w),
        ("sgu_b", sgu_b, jnp.stack(g_sb, axis=0), m_sgu_b, v_sgu_b),
        ("w_out", w_out, g_w_out, m_w_out, v_w_out),
        ("final_g", final_g.reshape(1, D), g_final_g.reshape(1, D), m_final_g.reshape(1, D), v_final_g.reshape(1, D)),
    ]
    grads, deltas, new_ms, new_vs = [], [], [], []
    for nm, w, g, m, v in triples:
        g, dlt, mn, vn = _adam_out(w, g, m, v, f"adamw_{nm}")
        if nm == "final_g":
            g, dlt, mn, vn = (t.reshape(D) for t in (g, dlt, mn, vn))
        grads.append(g)
        deltas.append(dlt)
        new_ms.append(mn)
        new_vs.append(vn)
    return (loss, grad_x[None], *grads, *deltas, *new_ms, *new_vs)
```

```python
import math

import jax
import jax.numpy as jnp
from jax import lax
from jax.experimental import pallas as pl
from jax.experimental.pallas import tpu as pltpu

F32 = jnp.float32
BF16 = jnp.bfloat16
EPS = 1e-6
SGU_HEADS = 4
HEAD_LANES = 128
ATT_DH = 64
ATT_BLOCK = 128
MIX_HALF = SGU_HEADS * HEAD_LANES
DILATIONS = (1, 4, 16)
ROPE_THETA = 10000.0
N_ADA = 9
ADAM_LR, ADAM_B1, ADAM_B2, ADAM_EPS, ADAM_WD, ADAM_STEP = 0.001, 0.9, 0.999, 1e-08, 0.01, 10
NEG = -1e30
V7X_VMEM_BYTES = 64 * 1024 * 1024
VMEM_LIMIT = V7X_VMEM_BYTES * 7 // 8
MESH = pl.DeviceIdType.MESH
N_DEV = 8
N_CHIP = 4


def _tile(n, cap, mult):
    if n <= cap:
        return n
    t = (cap // mult) * mult
    while t >= mult:
        if n % t == 0:
            return t
        t -= mult
    raise ValueError((n, cap, mult))


def _params(dims=None):
    return pltpu.CompilerParams(dimension_semantics=dims, vmem_limit_bytes=VMEM_LIMIT)


def _wspec(w, rows, idx):
    arr, lead = w
    return pl.BlockSpec((None,) * len(lead) + (rows, arr.shape[-1]), lambda *g: tuple(lead) + (idx(*g), 0))


def _wrows(w):
    return w[0].shape[-2]


def _nt(a, b):
    return lax.dot_general(a, b, (((1,), (1,)), ((), ())), preferred_element_type=F32)


def _tn(a, b):
    return lax.dot_general(a, b, (((0,), (0,)), ((), ())), preferred_element_type=F32)


def _nn(a, b):
    return jnp.dot(a, b, preferred_element_type=F32)


def _sigmoid(x):
    return 1.0 / (1.0 + jnp.exp(-x))


_GELU_K = math.sqrt(2.0 / math.pi)
_GELU_C = 0.044715


def _gelu(x):
    t = jnp.tanh(_GELU_K * (x + _GELU_C * x * x * x))
    return 0.5 * x * (1.0 + t)


def _gelu_and_grad(x):
    x2 = x * x
    t = jnp.tanh(_GELU_K * (x + _GELU_C * x * x2))
    g = 0.5 * x * (1.0 + t)
    dg = 0.5 * (1.0 + t) + 0.5 * x * (1.0 - t * t) * (_GELU_K * (1.0 + 3.0 * _GELU_C * x2))
    return g, dg


def normmod_fwd(h, ng, i_n, mod, i_sh, i_sc, name):
    T, D = h.shape
    tm = _tile(T, 512, 8)

    def body(h_ref, ng_ref, mod_ref, y_ref):
        x = h_ref[...]
        r = lax.rsqrt(jnp.mean(x * x, axis=-1, keepdims=True) + EPS)
        y = (x * r) * ng_ref[i_n:i_n + 1, :]
        y_ref[...] = (y * (1.0 + mod_ref[i_sc:i_sc + 1, :]) + mod_ref[i_sh:i_sh + 1, :]).astype(BF16)

    return pl.pallas_call(
        body, name=name, grid=(T // tm,),
        in_specs=[pl.BlockSpec((tm, D), lambda i: (i, 0)),
                  pl.BlockSpec(ng.shape, lambda i: (0, 0)),
                  pl.BlockSpec(mod.shape, lambda i: (0, 0))],
        out_specs=pl.BlockSpec((tm, D), lambda i: (i, 0)),
        out_shape=jax.ShapeDtypeStruct((T, D), BF16),
        compiler_params=_params(("parallel",)),
    )(h, ng, mod)


def ffn_up(y, wgT, wuT, name):
    T, D = y.shape
    F = _wrows(wgT)
    tm = _tile(T, 512, 16)
    tf = _tile(F, 1408, 128)

    def body(y_ref, wg_ref, wu_ref, a_ref, b_ref, s_ref):
        yv = y_ref[...]
        a = _nt(yv, wg_ref[...])
        b = _nt(yv, wu_ref[...])
        a_ref[...] = a.astype(BF16)
        b_ref[...] = b.astype(BF16)
        s_ref[...] = (a * _sigmoid(a) * b).astype(BF16)

    act = jax.ShapeDtypeStruct((T, F), BF16)
    return pl.pallas_call(
        body, name=name, grid=(F // tf, T // tm),
        in_specs=[pl.BlockSpec((tm, D), lambda j, i: (i, 0)),
                  _wspec(wgT, tf, lambda j, i: j),
                  _wspec(wuT, tf, lambda j, i: j)],
        out_specs=[pl.BlockSpec((tm, tf), lambda j, i: (i, j))] * 3,
        out_shape=[act, act, act],
        compiler_params=_params(("parallel", "parallel")),
    )(y, wgT[0], wuT[0])


def resid_matmul(xs, w, h, mod, i_g, coef, name):
    T, D = h.shape
    kb = xs[0].shape[1]
    assert all(x.shape == (T, kb) for x in xs) and _wrows(w) == kb * len(xs)
    tm = _tile(T, 512, 16)
    nx = len(xs)

    def body(*refs):
        x_refs, w_refs = refs[:nx], refs[nx:2 * nx]
        h_ref, mod_ref, hn_ref, o_ref = refs[2 * nx:]
        o = _nn(x_refs[0][...], w_refs[0][...])
        for xr, wr in zip(x_refs[1:], w_refs[1:]):
            o = o + _nn(xr[...], wr[...])
        o_ref[...] = o.astype(BF16)
        hn_ref[...] = h_ref[...] + (coef * mod_ref[i_g:i_g + 1, :]) * o

    return pl.pallas_call(
        body, name=name, grid=(T // tm,),
        in_specs=([pl.BlockSpec((tm, kb), lambda i: (i, 0))] * nx
                  + [_wspec(w, kb, lambda i, p=p: p) for p in range(nx)]
                  + [pl.BlockSpec((tm, D), lambda i: (i, 0)),
                     pl.BlockSpec(mod.shape, lambda i: (0, 0))]),
        out_specs=[pl.BlockSpec((tm, D), lambda i: (i, 0))] * 2,
        out_shape=[jax.ShapeDtypeStruct((T, D), F32), jax.ShapeDtypeStruct((T, D), BF16)],
        compiler_params=_params(("parallel",)),
    )(*xs, *([w[0]] * nx), h, mod)


def gate_bwd(dh, o, mod, i_g, coef, name):
    T, D = dh.shape
    tm = _tile(T, 512, 16)

    def body(dh_ref, o_ref, mod_ref, do_ref, red_ref):
        d = dh_ref[...]
        do_ref[...] = (d * (coef * mod_ref[i_g:i_g + 1, :])).astype(BF16)

        @pl.when(pl.program_id(0) == 0)
        def _():
            red_ref[...] = jnp.zeros_like(red_ref)

        red_ref[0:1, :] += coef * jnp.sum(d * o_ref[...].astype(F32), axis=0, keepdims=True)

    return pl.pallas_call(
        body, name=name, grid=(T // tm,),
        in_specs=[pl.BlockSpec((tm, D), lambda i: (i, 0)),
                  pl.BlockSpec((tm, D), lambda i: (i, 0)),
                  pl.BlockSpec(mod.shape, lambda i: (0, 0))],
        out_specs=[pl.BlockSpec((tm, D), lambda i: (i, 0)), pl.BlockSpec((8, D), lambda i: (0, 0))],
        out_shape=[jax.ShapeDtypeStruct((T, D), BF16), jax.ShapeDtypeStruct((8, D), F32)],
        compiler_params=_params(("arbitrary",)),
    )(dh, o, mod)


def ffn_bwd_mid(do, wd, a, b, name):
    T, D = do.shape
    F = _wrows(wd)
    tm = _tile(T, 512, 16)
    tf = _tile(F, 1408, 128)

    def body(do_ref, wd_ref, a_ref, b_ref, da_ref, db_ref):
        ds = _nt(do_ref[...], wd_ref[...])
        av = a_ref[...].astype(F32)
        bv = b_ref[...].astype(F32)
        sig = _sigmoid(av)
        da_ref[...] = (ds * bv * (sig * (1.0 + av * (1.0 - sig)))).astype(BF16)
        db_ref[...] = (ds * (av * sig)).astype(BF16)

    act = jax.ShapeDtypeStruct((T, F), BF16)
    return pl.pallas_call(
        body, name=name, grid=(F // tf, T // tm),
        in_specs=[pl.BlockSpec((tm, D), lambda j, i: (i, 0)),
                  _wspec(wd, tf, lambda j, i: j),
                  pl.BlockSpec((tm, tf), lambda j, i: (i, j)),
                  pl.BlockSpec((tm, tf), lambda j, i: (i, j))],
        out_specs=[pl.BlockSpec((tm, tf), lambda j, i: (i, j))] * 2,
        out_shape=[act, act],
        compiler_params=_params(("parallel", "parallel")),
    )(do, wd[0], a, b)


def dy_normbwd(pairs, h, dhp, ng, i_n, mod, i_sc, name):
    T, D = h.shape
    tm = _tile(T, 256, 16)
    npair = len(pairs)

    def body(*refs):
        x_refs, w_refs = refs[:npair], refs[npair:2 * npair]
        h_ref, dhp_ref, ng_ref, mod_ref, dh_ref, red_ref = refs[2 * npair:]
        dy = _nn(x_refs[0][...], w_refs[0][...])
        for xr, wr in zip(x_refs[1:], w_refs[1:]):
            dy = dy + _nn(xr[...], wr[...])
        x = h_ref[...]
        r = lax.rsqrt(jnp.mean(x * x, axis=-1, keepdims=True) + EPS)
        n = x * r
        gn = ng_ref[i_n:i_n + 1, :]
        dnh = dy * (1.0 + mod_ref[i_sc:i_sc + 1, :])

        @pl.when(pl.program_id(0) == 0)
        def _():
            red_ref[...] = jnp.zeros_like(red_ref)

        red_ref[0:1, :] += jnp.sum(dy, axis=0, keepdims=True)
        red_ref[1:2, :] += jnp.sum(dy * (n * gn), axis=0, keepdims=True)
        red_ref[2:3, :] += jnp.sum(dnh * n, axis=0, keepdims=True)
        dn = dnh * gn
        dh_ref[...] = dhp_ref[...] + r * (dn - n * jnp.mean(dn * n, axis=-1, keepdims=True))

    in_specs = ([pl.BlockSpec((tm, kb), lambda i, c=c: (i, c)) for (_, c, _, _, kb) in pairs]
                + [_wspec(w, kb, lambda i, r=r: r) for (_, _, w, r, kb) in pairs]
                + [pl.BlockSpec((tm, D), lambda i: (i, 0)),
                   pl.BlockSpec((tm, D), lambda i: (i, 0)),
                   pl.BlockSpec(ng.shape, lambda i: (0, 0)),
                   pl.BlockSpec(mod.shape, lambda i: (0, 0))])
    return pl.pallas_call(
        body, name=name, grid=(T // tm,), in_specs=in_specs,
        out_specs=[pl.BlockSpec((tm, D), lambda i: (i, 0)), pl.BlockSpec((8, D), lambda i: (0, 0))],
        out_shape=[jax.ShapeDtypeStruct((T, D), F32), jax.ShapeDtypeStruct((8, D), F32)],
        compiler_params=_params(("arbitrary",)),
    )(*[p[0] for p in pairs], *[p[2][0] for p in pairs], h, dhp, ng, mod)


def matmul_tn(a, b, buf, slot, row0, name, tmo_cap=1408):
    T, N = b.shape
    ma = a.shape[1]
    tmo = _tile(ma, tmo_cap, 128)
    assert row0 % tmo == 0
    nmo = ma // tmo
    tk = _tile(T, 512, 16)
    nk = T // tk

    def body(a_ref, b_ref, buf_ref, o_ref, acc_ref):
        k = pl.program_id(1)

        @pl.when(k == 0)
        def _():
            acc_ref[...] = jnp.zeros_like(acc_ref)

        acc_ref[...] += _tn(a_ref[...], b_ref[...])

        @pl.when(k == nk - 1)
        def _():
            o_ref[...] = acc_ref[...].astype(BF16)

    return pl.pallas_call(
        body, name=name, grid=(nmo, nk),
        in_specs=[pl.BlockSpec((tk, tmo), lambda j, k: (k, j)),
                  pl.BlockSpec((tk, N), lambda j, k: (k, 0)),
                  pl.BlockSpec(memory_space=pl.ANY)],
        out_specs=pl.BlockSpec((None, tmo, N), lambda j, k: (slot, row0 // tmo + j, 0)),
        out_shape=jax.ShapeDtypeStruct(buf.shape, BF16),
        scratch_shapes=[pltpu.VMEM((tmo, N), F32)],
        input_output_aliases={2: 0},
        compiler_params=_params(("parallel", "arbitrary")),
    )(a, b, buf)


def matmul_nt(x, w, name):
    T, K = x.shape
    N = _wrows(w)
    tm = _tile(T, 512, 16)
    tn = _tile(N, 1280, 128)

    def body(x_ref, w_ref, o_ref):
        o_ref[...] = _nt(x_ref[...], w_ref[...]).astype(BF16)

    return pl.pallas_call(
        body, name=name, grid=(N // tn, T // tm),
        in_specs=[pl.BlockSpec((tm, K), lambda j, i: (i, 0)), _wspec(w, tn, lambda j, i: j)],
        out_specs=pl.BlockSpec((tm, tn), lambda j, i: (i, j)),
        out_shape=jax.ShapeDtypeStruct((T, N), BF16),
        compiler_params=_params(("parallel", "parallel")),
    )(x, w[0])


def _sgu_head_fwd(u, v, lng, lnb):
    gu, dgu = _gelu_and_grad(u)
    gv, dgv = _gelu_and_grad(v)
    mu = jnp.mean(gv, axis=-1, keepdims=True)
    xc = gv - mu
    rstd = lax.rsqrt(jnp.mean(xc * xc, axis=-1, keepdims=True) + EPS)
    xhat = xc * rstd
    vn = xhat * lng + lnb
    return gu, dgu, dgv, rstd, xhat, vn


def _tril_mask():
    r = lax.broadcasted_iota(jnp.int32, (ATT_BLOCK, ATT_BLOCK), 0)
    c = lax.broadcasted_iota(jnp.int32, (ATT_BLOCK, ATT_BLOCK), 1)
    return c <= r


def _triu_mask():
    r = lax.broadcasted_iota(jnp.int32, (ATT_BLOCK, ATT_BLOCK), 0)
    c = lax.broadcasted_iota(jnp.int32, (ATT_BLOCK, ATT_BLOCK), 1)
    return r <= c


def sgu_fwd(proj, lng, lnb, w, bcol, name):
    T = proj.shape[0]
    tm = _tile(T, 512, 128)
    nch = tm // ATT_BLOCK

    def body(u_ref, v_ref, lng_ref, lnb_ref, w_ref, b_ref, o_ref):
        tril = _tril_mask()
        for hd in range(SGU_HEADS):
            sl = slice(hd * HEAD_LANES, (hd + 1) * HEAD_LANES)
            u = u_ref[:, sl].astype(F32)
            v = v_ref[:, sl].astype(F32)
            gu, _, _, _, _, vn = _sgu_head_fwd(u, v, lng_ref[:, sl], lnb_ref[:, sl])
            wm = jnp.where(tril, w_ref[hd], 0.0).astype(BF16)
            vnb = vn.astype(BF16)
            bc = b_ref[:, hd:hd + 1]
            for ch in range(nch):
                rs = slice(ch * ATT_BLOCK, (ch + 1) * ATT_BLOCK)
                z = _nn(wm, vnb[rs, :]) + bc
                o_ref[rs, sl] = (gu[rs, :] * z).astype(BF16)

    return pl.pallas_call(
        body, name=name, grid=(T // tm,),
        in_specs=[pl.BlockSpec((tm, MIX_HALF), lambda i: (i, 0)),
                  pl.BlockSpec((tm, MIX_HALF), lambda i: (i, 1)),
                  pl.BlockSpec((1, MIX_HALF), lambda i: (0, 0)),
                  pl.BlockSpec((1, MIX_HALF), lambda i: (0, 0)),
                  pl.BlockSpec(w.shape, lambda i: (0, 0, 0)),
                  pl.BlockSpec(bcol.shape, lambda i: (0, 0))],
        out_specs=pl.BlockSpec((tm, MIX_HALF), lambda i: (i, 0)),
        out_shape=jax.ShapeDtypeStruct((T, MIX_HALF), BF16),
        compiler_params=_params(("parallel",)),
    )(proj, proj, lng, lnb, w, bcol)


def sgu_bwd(proj, dmixed, lng, lnb, w, wt, bcol, name):
    T = proj.shape[0]
    tm = _tile(T, 512, 128)
    nch = tm // ATT_BLOCK
    nsteps = T // tm

    def body(u_ref, v_ref, g_ref, lng_ref, lnb_ref, w_ref, wt_ref, b_ref, duv_ref, dw_ref, dvec_ref, bacc_ref):
        step = pl.program_id(0)

        @pl.when(step == 0)
        def _():
            dw_ref[...] = jnp.zeros_like(dw_ref)
            dvec_ref[...] = jnp.zeros_like(dvec_ref)
            bacc_ref[...] = jnp.zeros_like(bacc_ref)

        tril = _tril_mask()
        triu = _triu_mask()
        for hd in range(SGU_HEADS):
            sl = slice(hd * HEAD_LANES, (hd + 1) * HEAD_LANES)
            u = u_ref[:, sl].astype(F32)
            v = v_ref[:, sl].astype(F32)
            lng_h = lng_ref[:, sl]
            gu, dgu, dgv, rstd, xhat, vn = _sgu_head_fwd(u, v, lng_h, lnb_ref[:, sl])
            wm = jnp.where(tril, w_ref[hd], 0.0).astype(BF16)
            wmt = jnp.where(triu, wt_ref[hd], 0.0).astype(BF16)
            vnb = vn.astype(BF16)
            bc = b_ref[:, hd:hd + 1]
            g = g_ref[:, sl].astype(F32)
            dw_acc = jnp.zeros((ATT_BLOCK, ATT_BLOCK), F32)
            b_acc = jnp.zeros((ATT_BLOCK, HEAD_LANES), F32)
            dvn_parts = []
            for ch in range(nch):
                rs = slice(ch * ATT_BLOCK, (ch + 1) * ATT_BLOCK)
                z = _nn(wm, vnb[rs, :]) + bc
                duv_ref[rs, sl] = (g[rs, :] * z * dgu[rs, :]).astype(BF16)
                dz = g[rs, :] * gu[rs, :]
                dzb = dz.astype(BF16)
                dvn_parts.append(_nn(wmt, dzb))
                dw_acc = dw_acc + _nt(dzb, vnb[rs, :])
                b_acc = b_acc + dz
            dvn = jnp.concatenate(dvn_parts, axis=0)
            dw_ref[hd] += jnp.where(tril, dw_acc, 0.0)
            bacc_ref[hd] += b_acc
            dvec_ref[0:1, sl] += jnp.sum(dvn * xhat, axis=0, keepdims=True)
            dvec_ref[1:2, sl] += jnp.sum(dvn, axis=0, keepdims=True)
            dxh = dvn * lng_h
            dgv_in = rstd * (dxh - jnp.mean(dxh, axis=-1, keepdims=True)
                             - xhat * jnp.mean(dxh * xhat, axis=-1, keepdims=True))
            duv_ref[:, MIX_HALF + hd * HEAD_LANES:MIX_HALF + (hd + 1) * HEAD_LANES] = (dgv_in * dgv).astype(BF16)

        @pl.when(step == nsteps - 1)
        def _():
            for hd in range(SGU_HEADS):
                sl = slice(hd * HEAD_LANES, (hd + 1) * HEAD_LANES)
                dvec_ref[2:3, sl] = jnp.sum(bacc_ref[hd].T, axis=0, keepdims=True)

    return pl.pallas_call(
        body, name=name, grid=(nsteps,),
        in_specs=[pl.BlockSpec((tm, MIX_HALF), lambda i: (i, 0)),
                  pl.BlockSpec((tm, MIX_HALF), lambda i: (i, 1)),
                  pl.BlockSpec((tm, MIX_HALF), lambda i: (i, 0)),
                  pl.BlockSpec((1, MIX_HALF), lambda i: (0, 0)),
                  pl.BlockSpec((1, MIX_HALF), lambda i: (0, 0)),
                  pl.BlockSpec(w.shape, lambda i: (0, 0, 0)),
                  pl.BlockSpec(w.shape, lambda i: (0, 0, 0)),
                  pl.BlockSpec(bcol.shape, lambda i: (0, 0))],
        out_specs=[pl.BlockSpec((tm, 2 * MIX_HALF), lambda i: (i, 0)),
                   pl.BlockSpec(w.shape, lambda i: (0, 0, 0)),
                   pl.BlockSpec((8, MIX_HALF), lambda i: (0, 0))],
        out_shape=[jax.ShapeDtypeStruct((T, 2 * MIX_HALF), BF16),
                   jax.ShapeDtypeStruct(w.shape, F32),
                   jax.ShapeDtypeStruct((8, MIX_HALF), F32)],
        scratch_shapes=[pltpu.VMEM((SGU_HEADS, ATT_BLOCK, HEAD_LANES), F32)],
        compiler_params=_params(("arbitrary",)),
    )(proj, proj, dmixed, lng, lnb, w, wt, bcol)


def _rot_half(t):
    lane = lax.broadcasted_iota(jnp.int32, t.shape, 1)
    first = (lane % ATT_DH) < (ATT_DH // 2)
    return jnp.where(first, -pltpu.roll(t, HEAD_LANES - ATT_DH // 2, 1), pltpu.roll(t, ATT_DH // 2, 1))


def rope_fwd(proj, cos, sin, name):
    T = proj.shape[0]
    tm = _tile(T, 512, 16)
    scale = 1.0 / math.sqrt(ATT_DH)

    def body(q_ref, k_ref, cos_ref, sin_ref, qo_ref, ko_ref):
        c = cos_ref[...]
        s = sin_ref[...]
        for hp in range(MIX_HALF // HEAD_LANES):
            sl = slice(hp * HEAD_LANES, (hp + 1) * HEAD_LANES)
            q = q_ref[:, sl].astype(F32)
            k = k_ref[:, sl].astype(F32)
            qo_ref[:, sl] = (scale * (q * c + _rot_half(q) * s)).astype(BF16)
            ko_ref[:, sl] = (k * c + _rot_half(k) * s).astype(BF16)

    out = jax.ShapeDtypeStruct((T, MIX_HALF), BF16)
    return pl.pallas_call(
        body, name=name, grid=(T // tm,),
        in_specs=[pl.BlockSpec((tm, MIX_HALF), lambda i: (i, 2)),
                  pl.BlockSpec((tm, MIX_HALF), lambda i: (i, 3)),
                  pl.BlockSpec((tm, HEAD_LANES), lambda i: (i, 0)),
                  pl.BlockSpec((tm, HEAD_LANES), lambda i: (i, 0))],
        out_specs=[pl.BlockSpec((tm, MIX_HALF), lambda i: (i, 0))] * 2,
        out_shape=[out, out],
        compiler_params=_params(("parallel",)),
    )(proj, proj, cos, sin)


def rope_bwd(dqs, dks, dvs, cos, sin, name):
    T = dqs[0].shape[0]
    tm = _tile(T, 512, 16)
    scale = 1.0 / math.sqrt(ATT_DH)
    npat = len(dqs)

    def body(*refs):
        dq_refs, dk_refs, dv_refs = refs[:npat], refs[npat:2 * npat], refs[2 * npat:3 * npat]
        cos_ref, sin_ref, o_ref = refs[3 * npat:]
        c = cos_ref[...]
        s = sin_ref[...]
        for hp in range(MIX_HALF // HEAD_LANES):
            sl = slice(hp * HEAD_LANES, (hp + 1) * HEAD_LANES)
            gq = scale * sum(r[:, sl] for r in dq_refs)
            gk = sum(r[:, sl] for r in dk_refs)
            gv = sum(r[:, sl] for r in dv_refs)
            o_ref[:, sl] = (gq * c - _rot_half(gq * s)).astype(BF16)
            o_ref[:, MIX_HALF + hp * HEAD_LANES:MIX_HALF + (hp + 1) * HEAD_LANES] = (
                gk * c - _rot_half(gk * s)).astype(BF16)
            o_ref[:, 2 * MIX_HALF + hp * HEAD_LANES:2 * MIX_HALF + (hp + 1) * HEAD_LANES] = gv.astype(BF16)

    return pl.pallas_call(
        body, name=name, grid=(T // tm,),
        in_specs=([pl.BlockSpec((tm, MIX_HALF), lambda i: (i, 0))] * (3 * npat)
                  + [pl.BlockSpec((tm, HEAD_LANES), lambda i: (i, 0))] * 2),
        out_specs=pl.BlockSpec((tm, 3 * MIX_HALF), lambda i: (i, 0)),
        out_shape=jax.ShapeDtypeStruct((T, 3 * MIX_HALF), BF16),
        compiler_params=_params(("parallel",)),
    )(*dqs, *dks, *dvs, cos, sin)


def _band_masks(n):
    r = lax.broadcasted_iota(jnp.int32, (2 * ATT_BLOCK, ATT_BLOCK), 0)
    c = lax.broadcasted_iota(jnp.int32, (2 * ATT_BLOCK, ATT_BLOCK), 1)
    qi = r % ATT_BLOCK
    head = (c < ATT_DH) == (r < ATT_BLOCK)
    return (c >= qi) & (n > 0), c <= qi, head, c[:ATT_BLOCK] < ATT_DH


def _stack_heads(x, head):
    x2 = jnp.concatenate([x, x], axis=0)
    return jnp.where(head, x2, jnp.zeros_like(x2))


def attn_fwd(q, k, v, name):
    d, L, W = q.shape
    nb = L // ATT_BLOCK

    def body(q_ref, kp_ref, kc_ref, vp_ref, vc_ref, o_ref, lse_ref):
        mask_p, mask_c, head, head0 = _band_masks(pl.program_id(1))
        for hp in range(W // HEAD_LANES):
            sl = slice(hp * HEAD_LANES, (hp + 1) * HEAD_LANES)
            kp, kc, vp, vc = kp_ref[0, :, sl], kc_ref[0, :, sl], vp_ref[0, :, sl], vc_ref[0, :, sl]
            qs = _stack_heads(q_ref[0, :, sl], head)
            sp = jnp.where(mask_p, _nt(qs, kp), NEG)
            sc = jnp.where(mask_c, _nt(qs, kc), NEG)
            m = jnp.maximum(jnp.max(sp, axis=1, keepdims=True), jnp.max(sc, axis=1, keepdims=True))
            pp = jnp.exp(sp - m)
            pc = jnp.exp(sc - m)
            den = jnp.sum(pp, axis=1, keepdims=True) + jnp.sum(pc, axis=1, keepdims=True)
            o = (_nn(pp.astype(BF16), vp) + _nn(pc.astype(BF16), vc)) / den
            lse = m + jnp.log(den)
            o_ref[0, :, sl] = jnp.where(head0, o[:ATT_BLOCK], o[ATT_BLOCK:])
            lse_ref[0, :, sl] = jnp.where(head0, lse[:ATT_BLOCK], lse[ATT_BLOCK:])

    cur = pl.BlockSpec((1, ATT_BLOCK, W), lambda r, n: (r, n, 0))
    prev = pl.BlockSpec((1, ATT_BLOCK, W), lambda r, n: (r, jnp.maximum(n - 1, 0), 0))
    out = jax.ShapeDtypeStruct((d, L, W), F32)
    return pl.pallas_call(
        body, name=name, grid=(d, nb),
        in_specs=[cur, prev, cur, prev, cur],
        out_specs=[cur, cur], out_shape=[out, out],
        compiler_params=_params(("parallel", "parallel")),
    )(q, k, k, v, v)


def attn_combine(os_, lses, name):
    T, W = os_[0].shape
    tm = _tile(T, 512, 16)
    npat = len(os_)

    def body(*refs):
        o_refs, l_refs = refs[:npat], refs[npat:2 * npat]
        out_ref, lse_ref = refs[2 * npat:]
        ls = [r[...] for r in l_refs]
        m = ls[0]
        for l in ls[1:]:
            m = jnp.maximum(m, l)
        es = [jnp.exp(l - m) for l in ls]
        z = es[0]
        for e in es[1:]:
            z = z + e
        acc = es[0] * o_refs[0][...]
        for e, r in zip(es[1:], o_refs[1:]):
            acc = acc + e * r[...]
        out_ref[...] = (acc / z).astype(BF16)
        lse_ref[...] = m + jnp.log(z)

    blk = pl.BlockSpec((tm, W), lambda i: (i, 0))
    return pl.pallas_call(
        body, name=name, grid=(T // tm,),
        in_specs=[blk] * (2 * npat), out_specs=[blk, blk],
        out_shape=[jax.ShapeDtypeStruct((T, W), BF16), jax.ShapeDtypeStruct((T, W), F32)],
        compiler_params=_params(("parallel",)),
    )(*os_, *lses)


def attn_bwd(q, k, v, do, o, lse, name):
    d, L, W = q.shape
    nb = L // ATT_BLOCK

    def body(q_ref, kp_ref, kc_ref, vp_ref, vc_ref, do_ref, o_ref, lse_ref, dq_ref, dk_ref, dv_ref, kkeep, vkeep):
        n = pl.program_id(1)

        @pl.when(n < nb)
        def _():
            mask_p, mask_c, head, head0 = _band_masks(n)
            for hp in range(W // HEAD_LANES):
                sl = slice(hp * HEAD_LANES, (hp + 1) * HEAD_LANES)
                kp, kc, vp, vc = kp_ref[0, :, sl], kc_ref[0, :, sl], vp_ref[0, :, sl], vc_ref[0, :, sl]
                dout = do_ref[0, :, sl]
                qs = _stack_heads(q_ref[0, :, sl], head)
                dos = _stack_heads(dout, head)
                lse_v = lse_ref[0, :, sl]
                lse_c = jnp.max(jnp.where(head, jnp.concatenate([lse_v, lse_v], axis=0), NEG), axis=1, keepdims=True)
                delta = jnp.sum(_stack_heads(dout.astype(F32) * o_ref[0, :, sl].astype(F32), head), axis=1, keepdims=True)
                pp = jnp.exp(jnp.where(mask_p, _nt(qs, kp), NEG) - lse_c)
                pc = jnp.exp(jnp.where(mask_c, _nt(qs, kc), NEG) - lse_c)
                dsp = (pp * (_nt(dos, vp) - delta)).astype(BF16)
                dsc = (pc * (_nt(dos, vc) - delta)).astype(BF16)
                dq2 = _nn(dsp, kp) + _nn(dsc, kc)
                dq_ref[0, :, sl] = jnp.where(head0, dq2[:ATT_BLOCK], dq2[ATT_BLOCK:])
                kprev = _tn(dsp, qs)
                vprev = _tn(pp.astype(BF16), dos)

                @pl.when(n > 0)
                def _():
                    dk_ref[0, :, sl] = kkeep[:, sl] + kprev
                    dv_ref[0, :, sl] = vkeep[:, sl] + vprev

                kkeep[:, sl] = _tn(dsc, qs)
                vkeep[:, sl] = _tn(pc.astype(BF16), dos)

        @pl.when(n == nb)
        def _():
            dk_ref[0] = kkeep[...]
            dv_ref[0] = vkeep[...]

    cur = pl.BlockSpec((1, ATT_BLOCK, W), lambda r, n: (r, jnp.minimum(n, nb - 1), 0))
    prev = pl.BlockSpec((1, ATT_BLOCK, W), lambda r, n: (r, jnp.clip(n - 1, 0, nb - 1), 0))
    out = jax.ShapeDtypeStruct((d, L, W), F32)
    return pl.pallas_call(
        body, name=name, grid=(d, nb + 1),
        in_specs=[cur, prev, cur, prev, cur, cur, cur, cur],
        out_specs=[cur, prev, prev], out_shape=[out, out, out],
        scratch_shapes=[pltpu.VMEM((ATT_BLOCK, W), F32), pltpu.VMEM((ATT_BLOCK, W), F32)],
        compiler_params=_params(("parallel", "arbitrary")),
    )(q, k, k, v, v, do, o, lse)


def final_loss_bwd(h, gf, tgt, name):
    T, D = h.shape
    tm = _tile(T, 512, 8)

    def body(h_ref, g_ref, t_ref, dh_ref, red_ref):
        x = h_ref[...]
        r = lax.rsqrt(jnp.mean(x * x, axis=-1, keepdims=True) + EPS)
        n = x * r
        g = g_ref[...]
        err = n * g - t_ref[...]
        dy = err * (1.0 / D)

        @pl.when(pl.program_id(0) == 0)
        def _():
            red_ref[...] = jnp.zeros_like(red_ref)

        red_ref[0:1, :] += jnp.sum(dy * n, axis=0, keepdims=True)
        red_ref[1:2, :] += jnp.zeros((1, D), F32) + (0.5 / D) * jnp.sum(err * err, keepdims=True)
        dn = dy * g
        dh_ref[...] = r * (dn - n * jnp.mean(dn * n, axis=-1, keepdims=True))

    return pl.pallas_call(
        body, name=name, grid=(T // tm,),
        in_specs=[pl.BlockSpec((tm, D), lambda i: (i, 0)),
                  pl.BlockSpec((1, D), lambda i: (0, 0)),
                  pl.BlockSpec((tm, D), lambda i: (i, 0))],
        out_specs=[pl.BlockSpec((tm, D), lambda i: (i, 0)), pl.BlockSpec((8, D), lambda i: (0, 0))],
        out_shape=[jax.ShapeDtypeStruct((T, D), F32), jax.ShapeDtypeStruct((8, D), F32)],
        compiler_params=_params(("arbitrary",)),
    )(h, gf, tgt)


def ada_fwd(c_all, ada_w, ada_b, name):
    nl, D, N = ada_w.shape

    def body(c_ref, w_ref, b_ref, o_ref):
        c = c_ref[...]
        o_ref[0] = _nn(c * _sigmoid(c), w_ref[0]) + b_ref[0]

    return pl.pallas_call(
        body, name=name, grid=(nl,),
        in_specs=[pl.BlockSpec((N_DEV, D), lambda l: (0, 0)),
                  pl.BlockSpec((1, D, N), lambda l: (l, 0, 0)),
                  pl.BlockSpec((1, 1, N), lambda l: (l, 0, 0))],
        out_specs=pl.BlockSpec((1, N_DEV, N), lambda l: (l, 0, 0)),
        out_shape=jax.ShapeDtypeStruct((nl, N_DEV, N), F32),
        compiler_params=_params(("parallel",)),
    )(c_all, ada_w, ada_b)


def ada_bwd(c_allT, dmod, name):
    nl, _, N = dmod.shape
    D = c_allT.shape[0]

    def body(c_ref, g_ref, o_ref):
        c = c_ref[...]
        ca = c * _sigmoid(c)
        acc = ca[:, 0:1] * g_ref[0, 0:1, :]
        for b in range(1, N_DEV):
            acc = acc + ca[:, b:b + 1] * g_ref[0, b:b + 1, :]
        o_ref[0] = acc

    return pl.pallas_call(
        body, name=name, grid=(nl,),
        in_specs=[pl.BlockSpec((D, N_DEV), lambda l: (0, 0)),
                  pl.BlockSpec((1, N_DEV, N), lambda l: (l, 0, 0))],
        out_specs=pl.BlockSpec((1, D, N), lambda l: (l, 0, 0)),
        out_shape=jax.ShapeDtypeStruct((nl, D, N), F32),
        compiler_params=_params(("parallel",)),
    )(c_allT, dmod)


def adamw(w, g, m, v, name):
    R, C = w.shape
    tr = _tile(R, max(8, (1 << 19) // C // 8 * 8), 8)
    c1 = 1.0 - ADAM_B1 ** ADAM_STEP
    c2 = 1.0 - ADAM_B2 ** ADAM_STEP

    def body(w_ref, g_ref, m_ref, v_ref, d_ref, mo_ref, vo_ref):
        gv = g_ref[...]
        mn = ADAM_B1 * m_ref[...] + (1.0 - ADAM_B1) * gv
        vn = ADAM_B2 * v_ref[...] + (1.0 - ADAM_B2) * (gv * gv)
        mo_ref[...] = mn
        vo_ref[...] = vn
        d_ref[...] = -ADAM_LR * ((mn / c1) / (jnp.sqrt(vn / c2) + ADAM_EPS) + ADAM_WD * w_ref[...])

    blk = pl.BlockSpec((tr, C), lambda i: (i, 0))
    out = jax.ShapeDtypeStruct((R, C), F32)
    return pl.pallas_call(
        body, name=name, grid=(R // tr,),
        in_specs=[blk] * 4, out_specs=[blk] * 3, out_shape=[out] * 3,
        compiler_params=_params(("parallel",)),
    )(w, g, m, v)


def sum_halves(g, lands, c_idx, name):
    n, ns, _, rh, D = g.shape

    def body(c_ref, g_ref, l_ref, o_ref):
        o_ref[0, 0] = (g_ref[0, 0, 0].astype(F32) + l_ref[0, 0].astype(F32)).astype(BF16)

    return pl.pallas_call(
        body, name=name,
        grid_spec=pltpu.PrefetchScalarGridSpec(
            num_scalar_prefetch=1, grid=(n, ns),
            in_specs=[pl.BlockSpec((1, 1, 1, rh, D), lambda i, j, c: (i, j, c[0], 0, 0)),
                      pl.BlockSpec((1, 1, rh, D), lambda i, j, c: (i, j, 0, 0))],
            out_specs=pl.BlockSpec((1, 1, rh, D), lambda i, j, c: (i, j, 0, 0))),
        out_shape=jax.ShapeDtypeStruct((n, ns, rh, D), BF16),
        compiler_params=_params(("parallel", "parallel")),
    )(c_idx, g, lands)


def sum_chips(p, lands, place, name):
    n, ns, rh, D = p.shape

    def body(c_ref, p_ref, l_ref, o_ref):
        acc = p_ref[0, 0].astype(F32)
        for j in range(N_CHIP - 1):
            acc = acc + l_ref[j, 0].astype(F32)
        o_ref[0, 0] = acc

    return pl.pallas_call(
        body, name=name,
        grid_spec=pltpu.PrefetchScalarGridSpec(
            num_scalar_prefetch=1, grid=(n,),
            in_specs=[pl.BlockSpec((1, 1, rh, D), lambda i, c: (i, c[0], 0, 0)),
                      pl.BlockSpec((N_CHIP - 1, 1, rh, D), lambda i, c: (0, i, 0, 0))],
            out_specs=pl.BlockSpec((1, 1, rh, D), lambda i, c: (i, c[1], 0, 0))),
        out_shape=jax.ShapeDtypeStruct((n, 2, rh, D), F32),
        compiler_params=_params(("parallel",)),
    )(place, p, lands)


def _my_place():
    return lax.axis_index("x"), lax.axis_index("y"), lax.axis_index("c")


def _other_chips(mx, my):
    return [(1 - mx, my), (mx, 1 - my), (1 - mx, 1 - my)]


def gather_small(x, name):
    def body(x_ref, out_ref, sum_ref, send_sems, recv_sems):
        mx, my, mc = _my_place()
        me = 4 * mx + 2 * my + mc
        out_ref[me] = x_ref[...]
        sends = []
        for k in range(1, N_DEV):
            kx, ky, kc = (k >> 2) & 1, (k >> 1) & 1, k & 1
            peer = (1 - mx if kx else mx, 1 - my if ky else my, 1 - mc if kc else mc)
            cp = pltpu.make_async_remote_copy(
                src_ref=x_ref, dst_ref=out_ref.at[me], send_sem=send_sems.at[k - 1], recv_sem=recv_sems.at[k - 1],
                device_id=peer, device_id_type=MESH)
            cp.start()
            sends.append((cp, 4 * peer[0] + 2 * peer[1] + peer[2], peer))
        for k, (cp, peer_slot, peer) in enumerate(sends):
            pltpu.make_async_remote_copy(
                src_ref=x_ref, dst_ref=out_ref.at[peer_slot], send_sem=send_sems.at[k], recv_sem=recv_sems.at[k],
                device_id=peer, device_id_type=MESH).wait_recv()
        for cp, _, _ in sends:
            cp.wait_send()
        acc = out_ref[0]
        for s in range(1, N_DEV):
            acc = acc + out_ref[s]
        sum_ref[...] = acc

    vmem = pl.BlockSpec(memory_space=pltpu.VMEM)
    return pl.pallas_call(
        body, name=name,
        in_specs=[vmem], out_specs=[vmem, vmem],
        out_shape=[jax.ShapeDtypeStruct((N_DEV,) + x.shape, x.dtype), jax.ShapeDtypeStruct(x.shape, x.dtype)],
        scratch_shapes=[pltpu.SemaphoreType.DMA((N_DEV - 1,)), pltpu.SemaphoreType.DMA((N_DEV - 1,))],
        compiler_params=pltpu.CompilerParams(vmem_limit_bytes=VMEM_LIMIT),
    )(x)


def gather_weights(shards, name):
    K = len(shards)

    def body(*refs):
        ins, outs = refs[:K], refs[K:2 * K]
        send1, recv1, send2, recv2, send0, recv0 = refs[2 * K:]
        mx, my, mc = _my_place()
        ci = 2 * mx + my
        chips = _other_chips(mx, my)
        local = [pltpu.make_async_remote_copy(
            src_ref=ins[k], dst_ref=outs[k].at[:, ci], send_sem=send0.at[k], recv_sem=recv0.at[k],
            device_id=(mx, my, 1 - mc), device_id_type=MESH) for k in range(K)]
        for cp in local:
            cp.start()

        def half(k, chip_idx, hc):
            return outs[k].at[:, chip_idx, hc]

        first, passed = [], []
        for j, (cx, cy) in enumerate(chips):
            for k in range(K):
                cp = pltpu.make_async_remote_copy(
                    src_ref=ins[k].at[:, mc], dst_ref=half(k, ci, mc),
                    send_sem=send1.at[k * 3 + j], recv_sem=recv1.at[k * 3 + j],
                    device_id=(cx, cy, mc), device_id_type=MESH)
                cp.start()
                first.append(cp)
        for j, (cx, cy) in enumerate(chips):
            cj = 2 * cx + cy
            for k in range(K):
                pltpu.make_async_remote_copy(
                    src_ref=ins[k].at[:, mc], dst_ref=half(k, cj, mc),
                    send_sem=send1.at[k * 3 + j], recv_sem=recv1.at[k * 3 + j],
                    device_id=(cx, cy, mc), device_id_type=MESH).wait_recv()
                cp = pltpu.make_async_remote_copy(
                    src_ref=half(k, cj, mc), dst_ref=half(k, cj, mc),
                    send_sem=send2.at[k * 3 + j], recv_sem=recv2.at[k * 3 + j],
                    device_id=(mx, my, 1 - mc), device_id_type=MESH)
                cp.start()
                passed.append(cp)
        for j, (cx, cy) in enumerate(chips):
            cj = 2 * cx + cy
            for k in range(K):
                pltpu.make_async_remote_copy(
                    src_ref=half(k, cj, 1 - mc), dst_ref=half(k, cj, 1 - mc),
                    send_sem=send2.at[k * 3 + j], recv_sem=recv2.at[k * 3 + j],
                    device_id=(mx, my, 1 - mc), device_id_type=MESH).wait_recv()
        for cp in first + passed:
            cp.wait_send()
        for cp in local:
            cp.wait()

    hbm = pl.BlockSpec(memory_space=pl.ANY)
    return pl.pallas_call(
        body, name=name,
        in_specs=[hbm] * K, out_specs=[hbm] * K,
        out_shape=[jax.ShapeDtypeStruct((s.shape[0], N_CHIP) + s.shape[1:], s.dtype) for s in shards],
        scratch_shapes=[pltpu.SemaphoreType.DMA((3 * K,))] * 4 + [pltpu.SemaphoreType.DMA((K,))] * 2,
    )(*shards)


_HBM = pl.BlockSpec(memory_space=pltpu.HBM)
_SEM = pl.BlockSpec(memory_space=pltpu.SEMAPHORE)
_DATAFLOW = pltpu.SideEffectType.DATAFLOW_SIDE_EFFECTING


def _gather_copies(shard, land, send, recv, base):
    mx, my, mc = _my_place()
    ci = 2 * mx + my
    peers = [((cx, cy, mc), 2 * cx + cy) for cx, cy in _other_chips(mx, my)] + [((mx, my, 1 - mc), ci)]
    out = []
    for q, (dev, src_slot) in enumerate(peers):
        out.append((
            pltpu.make_async_remote_copy(src_ref=shard, dst_ref=land.at[:, ci], send_sem=send.at[base + q],
                                         recv_sem=recv.at[base + q], device_id=dev, device_id_type=MESH),
            pltpu.make_async_remote_copy(src_ref=shard, dst_ref=land.at[:, src_slot], send_sem=send.at[base + q],
                                         recv_sem=recv.at[base + q], device_id=dev, device_id_type=MESH)))
    return out


def gather_start(groups, after, name):
    items = [s for g in groups for s in g]
    ni, ng = len(items), len(groups)

    def body(*refs):
        shards, lands = refs[:ni], refs[ni:2 * ni]
        sems = refs[2 * ni + 1:2 * ni + 1 + 2 * ng]
        token = refs[-1]
        i = 0
        for g, grp in enumerate(groups):
            for p in range(len(grp)):
                for start_cp, _ in _gather_copies(shards[i], lands[i], sems[2 * g], sems[2 * g + 1], 4 * p):
                    start_cp.start()
                i += 1
        token[...] = jnp.zeros_like(token)

    sem_shapes = []
    for grp in groups:
        sem_shapes += [pltpu.SemaphoreType.DMA((4 * len(grp),))] * 2
    land_shapes = [(s.shape[0], N_CHIP) + s.shape[1:] for s in items]
    outs = pl.pallas_call(
        body, name=name,
        in_specs=[_HBM] * (2 * ni) + [pl.BlockSpec(memory_space=pl.ANY)],
        out_specs=[_SEM] * (2 * ng) + [_HBM] * (2 * ni) + [pl.BlockSpec(memory_space=pltpu.VMEM)],
        out_shape=(sem_shapes + [pltpu.HBM(s.shape, s.dtype) for s in items]
                   + [pltpu.HBM(ls, s.dtype) for ls, s in zip(land_shapes, items)]
                   + [jax.ShapeDtypeStruct((8, 128), F32)]),
        input_output_aliases={i: 2 * ng + i for i in range(2 * ni)},
        compiler_params=pltpu.CompilerParams(has_side_effects=_DATAFLOW),
    )(*[pltpu.with_memory_space_constraint(s, pltpu.HBM) for s in items],
      *[pltpu.with_memory_space_constraint(lax.empty(ls, s.dtype), pltpu.HBM) for ls, s in zip(land_shapes, items)],
      after)
    sems, thru, token = outs[:2 * ng], outs[2 * ng:2 * ng + 2 * ni], outs[-1]
    handles, i = [], 0
    for g, grp in enumerate(groups):
        n = len(grp)
        handles.append((sems[2 * g], sems[2 * g + 1], thru[i:i + n], thru[ni + i:ni + i + n]))
        i += n
    return handles, token


def gather_wait(handle, after, name):
    send, recv, shards, lands = handle
    n = len(shards)

    def body(*refs):
        shard_refs, land_refs = refs[:n], refs[n:2 * n]
        send_ref, recv_ref = refs[2 * n], refs[2 * n + 1]
        for p in range(n):
            for start_cp, recv_cp in _gather_copies(shard_refs[p], land_refs[p], send_ref, recv_ref, 4 * p):
                start_cp.wait_send()
                recv_cp.wait_recv()

    outs = pl.pallas_call(
        body, name=name,
        in_specs=[_HBM] * (2 * n) + [_SEM, _SEM, pl.BlockSpec(memory_space=pl.ANY)],
        out_specs=[_HBM] * (2 * n),
        out_shape=[pltpu.HBM(s.shape, s.dtype) for s in shards] + [pltpu.HBM(l.shape, l.dtype) for l in lands],
        input_output_aliases={i: i for i in range(2 * n)},
        compiler_params=pltpu.CompilerParams(has_side_effects=_DATAFLOW),
    )(*shards, *lands, send, recv, after)
    return outs[n:]


def sibling_send_half(gs, name):
    K = len(gs)

    def body(*refs):
        ins, outs = refs[:K], refs[K:2 * K]
        send, recv = refs[2 * K:]
        mx, my, mc = _my_place()
        cps = []
        for k in range(K):
            cp = pltpu.make_async_remote_copy(
                src_ref=ins[k].at[:, :, 1 - mc], dst_ref=outs[k], send_sem=send.at[k], recv_sem=recv.at[k],
                device_id=(mx, my, 1 - mc), device_id_type=MESH)
            cp.start()
            cps.append(cp)
        for cp in cps:
            cp.wait()

    hbm = pl.BlockSpec(memory_space=pl.ANY)
    return pl.pallas_call(
        body, name=name,
        in_specs=[hbm] * K, out_specs=[hbm] * K,
        out_shape=[jax.ShapeDtypeStruct(g.shape[:2] + g.shape[3:], g.dtype) for g in gs],
        scratch_shapes=[pltpu.SemaphoreType.DMA((K,)), pltpu.SemaphoreType.DMA((K,))],
    )(*gs)


def scatter_to_chips(ps, name):
    K = len(ps)

    def body(*refs):
        ins, outs = refs[:K], refs[K:2 * K]
        send, recv = refs[2 * K:]
        mx, my, mc = _my_place()
        cps = []
        for j, (cx, cy) in enumerate(_other_chips(mx, my)):
            for k in range(K):
                cp = pltpu.make_async_remote_copy(
                    src_ref=ins[k].at[:, 2 * cx + cy], dst_ref=outs[k].at[j],
                    send_sem=send.at[k * 3 + j], recv_sem=recv.at[k * 3 + j],
                    device_id=(cx, cy, mc), device_id_type=MESH)
                cp.start()
                cps.append(cp)
        for cp in cps:
            cp.wait()

    hbm = pl.BlockSpec(memory_space=pl.ANY)
    return pl.pallas_call(
        body, name=name,
        in_specs=[hbm] * K, out_specs=[hbm] * K,
        out_shape=[jax.ShapeDtypeStruct((N_CHIP - 1, p.shape[0]) + p.shape[2:], p.dtype) for p in ps],
        scratch_shapes=[pltpu.SemaphoreType.DMA((3 * K,)), pltpu.SemaphoreType.DMA((3 * K,))],
    )(*ps)


def _scatter_copies(ps, lands, send, recv):
    mx, my, mc = _my_place()
    cps = []
    for j, (cx, cy) in enumerate(_other_chips(mx, my)):
        for k in range(len(ps)):
            cps.append(pltpu.make_async_remote_copy(
                src_ref=ps[k].at[:, 2 * cx + cy], dst_ref=lands[k].at[j],
                send_sem=send.at[k * 3 + j], recv_sem=recv.at[k * 3 + j],
                device_id=(cx, cy, mc), device_id_type=MESH))
    return cps


def scatter_start(ps, after, name):
    K = len(ps)

    def body(*refs):
        ins, lands = refs[:K], refs[K:2 * K]
        send, recv = refs[2 * K + 1], refs[2 * K + 2]
        for cp in _scatter_copies(ins, lands, send, recv):
            cp.start()
        refs[-1][...] = jnp.zeros_like(refs[-1])

    land_shapes = [(N_CHIP - 1, p.shape[0]) + p.shape[2:] for p in ps]
    outs = pl.pallas_call(
        body, name=name,
        in_specs=[_HBM] * (2 * K) + [pl.BlockSpec(memory_space=pl.ANY)],
        out_specs=[_SEM, _SEM] + [_HBM] * (2 * K) + [pl.BlockSpec(memory_space=pltpu.VMEM)],
        out_shape=([pltpu.SemaphoreType.DMA((3 * K,))] * 2 + [pltpu.HBM(p.shape, p.dtype) for p in ps]
                   + [pltpu.HBM(ls, p.dtype) for ls, p in zip(land_shapes, ps)] + [jax.ShapeDtypeStruct((8, 128), F32)]),
        input_output_aliases={i: 2 + i for i in range(2 * K)},
        compiler_params=pltpu.CompilerParams(has_side_effects=_DATAFLOW),
    )(*[pltpu.with_memory_space_constraint(p, pltpu.HBM) for p in ps],
      *[pltpu.with_memory_space_constraint(lax.empty(ls, p.dtype), pltpu.HBM) for ls, p in zip(land_shapes, ps)],
      after)
    return (outs[0], outs[1], outs[2:2 + K], outs[2 + K:2 + 2 * K]), outs[-1]


def scatter_wait(handle, after, name):
    send, recv, ps, lands = handle
    K = len(ps)

    def body(*refs):
        ins, land_refs = refs[:K], refs[K:2 * K]
        send_ref, recv_ref = refs[2 * K], refs[2 * K + 1]
        for cp in _scatter_copies(ins, land_refs, send_ref, recv_ref):
            cp.wait_send()
            cp.wait_recv()

    outs = pl.pallas_call(
        body, name=name,
        in_specs=[_HBM] * (2 * K) + [_SEM, _SEM, pl.BlockSpec(memory_space=pl.ANY)],
        out_specs=[_HBM] * (2 * K),
        out_shape=[pltpu.HBM(p.shape, p.dtype) for p in ps] + [pltpu.HBM(l.shape, l.dtype) for l in lands],
        input_output_aliases={i: i for i in range(2 * K)},
        compiler_params=pltpu.CompilerParams(has_side_effects=_DATAFLOW),
    )(*ps, *lands, send, recv, after)
    return outs[:K], outs[K:]


def sibling_complete(ss, name):
    K = len(ss)

    def body(*refs):
        ins, outs = refs[:K], refs[K:2 * K]
        send, recv = refs[2 * K:]
        mx, my, mc = _my_place()
        cps = []
        for k in range(K):
            cp = pltpu.make_async_remote_copy(
                src_ref=ins[k].at[:, mc], dst_ref=outs[k].at[:, mc], send_sem=send.at[k], recv_sem=recv.at[k],
                device_id=(mx, my, 1 - mc), device_id_type=MESH)
            cp.start()
            cps.append(cp)
        for k in range(K):
            pltpu.make_async_remote_copy(
                src_ref=ins[k].at[:, mc], dst_ref=outs[k].at[:, 1 - mc], send_sem=send.at[k], recv_sem=recv.at[k],
                device_id=(mx, my, 1 - mc), device_id_type=MESH).wait_recv()
        for cp in cps:
            cp.wait_send()

    hbm = pl.BlockSpec(memory_space=pl.ANY)
    return pl.pallas_call(
        body, name=name,
        in_specs=[hbm] * K, out_specs=[hbm] * K,
        out_shape=[jax.ShapeDtypeStruct(s.shape, s.dtype) for s in ss],
        scratch_shapes=[pltpu.SemaphoreType.DMA((K,)), pltpu.SemaphoreType.DMA((K,))],
        input_output_aliases={k: k for k in range(K)},
    )(*ss)


def _rope_tables(T):
    inv = ROPE_THETA ** (-jnp.arange(0, ATT_DH, 2, dtype=F32) / ATT_DH)
    ang = jnp.arange(T, dtype=F32)[:, None] * inv[None, :]
    ang = jnp.concatenate([ang, ang, ang, ang], axis=-1)
    return jnp.cos(ang), jnp.sin(ang)


def _to_residues(x, d):
    T, W = x.shape
    if d == 1:
        return x.reshape(1, T, W)
    return x.reshape(T // d, d, W).transpose(1, 0, 2)


def _from_residues(x):
    d, L, W = x.shape
    if d == 1:
        return x.reshape(L, W)
    return x.transpose(1, 0, 2).reshape(L * d, W)


def _ffn_fwd(h, ng, i_n, mod, i0, wgT, wuT, wd, tag):
    y = normmod_fwd(h, ng, i_n, mod, i0, i0 + 1, f"normmod_{tag}")
    a, b, s = ffn_up(y, wgT, wuT, f"ffn_up_{tag}")
    hn, o = resid_matmul([s], wd, h, mod, i0 + 2, 0.5, f"ffn_down_{tag}")
    return hn, (h, y, a, b, s, o)


def _ffn_bwd(dh, res, ng, i_n, mod, i0, wgT, wuT, wd, gbuf, slot0, tag):
    h, y, a, b, s, o = res
    F = _wrows(wgT)
    do, red_g = gate_bwd(dh, o, mod, i0 + 2, 0.5, f"gate_bwd_{tag}")
    da, db = ffn_bwd_mid(do, wd, a, b, f"ffn_bwd_mid_{tag}")
    dh_new, red_n = dy_normbwd([(da, 0, wgT, 0, F), (db, 0, wuT, 0, F)], h, dh, ng, i_n, mod, i0 + 1,
                               f"ffn_bwd_dy_{tag}")
    gbuf = matmul_tn(da, y, gbuf, slot0, 0, f"dwg_{tag}")
    gbuf = matmul_tn(db, y, gbuf, slot0 + 1, 0, f"dwu_{tag}")
    gbuf = matmul_tn(s, do, gbuf, slot0 + 2, 0, f"dwd_{tag}")
    return dh_new, gbuf, red_n, red_g


def _mixer_fwd(h, ng, mod, w_inT, w_out, sgu, cos, sin, tag):
    lng, lnb, sw, swt, bcol = sgu
    y = normmod_fwd(h, ng, 1, mod, 3, 4, f"normmod_{tag}")
    proj = matmul_nt(y, w_inT, f"proj_{tag}")
    out_a = sgu_fwd(proj, lng, lnb, sw, bcol, f"sgu_fwd_{tag}")
    qr, kr = rope_fwd(proj, cos, sin, f"rope_fwd_{tag}")
    vv = proj[:, 4 * MIX_HALF:]
    os_, lses, qkv_res = [], [], []
    for d in DILATIONS:
        qd, kd, vd = _to_residues(qr, d), _to_residues(kr, d), _to_residues(vv, d)
        o_d, lse_d = attn_fwd(qd, kd, vd, f"attn_fwd_d{d}_{tag}")
        os_.append(_from_residues(o_d))
        lses.append(_from_residues(lse_d))
        qkv_res.append((qd, kd, vd))
    out_b, lse = attn_combine(os_, lses, f"attn_combine_{tag}")
    hn, om = resid_matmul([out_a, out_b], w_out, h, mod, 5, 1.0, f"mix_out_{tag}")
    return hn, (h, y, proj, out_a, out_b, lse, qkv_res, om)


def _mixer_bwd(dh, res, ng, mod, w_inT, w_out, sgu, cos, sin, winbuf, woutbuf, slot, tag):
    lng, lnb, sw, swt, bcol = sgu
    h, y, proj, out_a, out_b, lse, qkv_res, om = res
    dom, red_g = gate_bwd(dh, om, mod, 5, 1.0, f"gate_bwd_{tag}")
    dmixed = matmul_nt(dom, w_out, f"dmixed_{tag}")
    woutbuf = matmul_tn(out_a, dom, woutbuf, slot, 0, f"dwout_a_{tag}", tmo_cap=MIX_HALF)
    woutbuf = matmul_tn(out_b, dom, woutbuf, slot, MIX_HALF, f"dwout_b_{tag}", tmo_cap=MIX_HALF)
    d_uv, d_sw, d_svec = sgu_bwd(proj, dmixed, lng, lnb, sw, swt, bcol, f"sgu_bwd_{tag}")
    dob = dmixed[:, MIX_HALF:]
    dqs, dks, dvs = [], [], []
    for d, (qd, kd, vd) in zip(DILATIONS, qkv_res):
        dq, dk, dv = attn_bwd(qd, kd, vd, _to_residues(dob, d), _to_residues(out_b, d), _to_residues(lse, d),
                              f"attn_bwd_d{d}_{tag}")
        dqs.append(_from_residues(dq))
        dks.append(_from_residues(dk))
        dvs.append(_from_residues(dv))
    d_qkv = rope_bwd(dqs, dks, dvs, cos, sin, f"rope_bwd_{tag}")
    pairs = ([(d_uv, p, w_inT, p, MIX_HALF) for p in range(2)]
             + [(d_qkv, p, w_inT, 2 + p, MIX_HALF) for p in range(3)])
    dh_new, red_n = dy_normbwd(pairs, h, dh, ng, 1, mod, 4, f"mix_bwd_dy_{tag}")
    winbuf = matmul_tn(d_uv, y, winbuf, slot, 0, f"dwin_uv_{tag}", tmo_cap=MIX_HALF)
    winbuf = matmul_tn(d_qkv, y, winbuf, slot, 2 * MIX_HALF, f"dwin_qkv_{tag}", tmo_cap=MIX_HALF)
    return dh_new, winbuf, woutbuf, d_sw, d_svec, red_n, red_g


def _local_step(x, tgt, mods, ngs, get_w, sgus, gf, on_layer_grads):
    T, D = x.shape
    cos, sin = _rope_tables(T)
    h = x
    saved, weights = [], []
    for l in range(2):
        wf1 = get_w(l, "f1", h)
        h, r1 = _ffn_fwd(h, ngs[l], 0, mods[l], 0, (wf1, (0,)), (wf1, (1,)), (wf1, (2,)), f"l{l}f1")
        w_inT, w_out = get_w(l, "mx", h)
        h, r2 = _mixer_fwd(h, ngs[l], mods[l], (w_inT, (0,)), (w_out, (0,)), sgus[l], cos, sin, f"l{l}mx")
        wf2 = get_w(l, "f2", h)
        h, r3 = _ffn_fwd(h, ngs[l], 2, mods[l], 6, (wf2, (0,)), (wf2, (1,)), (wf2, (2,)), f"l{l}f2")
        saved.append((r1, r2, r3))
        weights.append((wf1, w_inT, w_out, wf2))
    dh, red_final = final_loss_bwd(h, gf, tgt, "final_loss_bwd")
    layer_grads = [None, None]
    for l in (1, 0):
        r1, r2, r3 = saved[l]
        wf1, w_inT, w_out, wf2 = weights[l]
        gbuf = lax.empty((6,) + wf1.shape[1:], BF16)
        winbuf = lax.empty(w_inT.shape, BF16)
        woutbuf = lax.empty(w_out.shape, BF16)
        dh, gbuf, rn3, rg3 = _ffn_bwd(dh, r3, ngs[l], 2, mods[l], 6, (wf2, (0,)), (wf2, (1,)), (wf2, (2,)),
                                      gbuf, 3, f"l{l}f2")
        dh, winbuf, woutbuf, d_sw, d_svec, rn2, rg2 = _mixer_bwd(
            dh, r2, ngs[l], mods[l], (w_inT, (0,)), (w_out, (0,)), sgus[l], cos, sin, winbuf, woutbuf, 0, f"l{l}mx")
        dh, gbuf, rn1, rg1 = _ffn_bwd(dh, r1, ngs[l], 0, mods[l], 0, (wf1, (0,)), (wf1, (1,)), (wf1, (2,)),
                                      gbuf, 0, f"l{l}f1")
        layer_grads[l] = dict(sgu_w=d_sw, sgu_vec=d_svec, red_n=(rn1, rn2, rn3), red_g=(rg1, rg2, rg3))
        token = on_layer_grads(l, (gbuf, winbuf, woutbuf))
        if token is not None:
            mods = mods + token[0, 0]
    return dh, layer_grads, red_final


def _adam_out(w, g, m, v, name):
    shp = w.shape
    two_d = (-1, shp[-1])
    d, mn, vn = adamw(w.reshape(two_d), g.reshape(two_d), m.reshape(two_d), v.reshape(two_d), name)
    return g, d.reshape(shp), mn.reshape(shp), vn.reshape(shp)


def kernel(x, c, ada_w, ada_b, norm_g, ffn1_wg, ffn1_wu, ffn1_wd, ffn2_wg, ffn2_wu, ffn2_wd, w_in, sgu_ln_g, sgu_ln_b, sgu_w, sgu_b, w_out, final_g, loss_target, m_ada_w, m_ada_b, m_norm_g, m_ffn1_wg, m_ffn1_wu, m_ffn1_wd, m_ffn2_wg, m_ffn2_wu, m_ffn2_wd, m_w_in, m_sgu_ln_g, m_sgu_ln_b, m_sgu_w, m_sgu_b, m_w_out, m_final_g, v_ada_w, v_ada_b, v_norm_g, v_ffn1_wg, v_ffn1_wu, v_ffn1_wd, v_ffn2_wg, v_ffn2_wu, v_ffn2_wd, v_w_in, v_sgu_ln_g, v_sgu_ln_b, v_sgu_w, v_sgu_b, v_w_out, v_final_g):
    T, D = x.shape[1], x.shape[2]
    NL = ada_w.shape[0]
    mx, my, mc = _my_place()
    me = 4 * mx + 2 * my + mc
    ci = 2 * mx + my
    c_idx = jnp.reshape(mc, (1,)).astype(jnp.int32)
    place = jnp.stack([ci, mc]).astype(jnp.int32)

    ngw = norm_g.shape[2]
    small_in = jnp.concatenate([jnp.pad(c, ((0, 7), (0, 0))),
                                jnp.pad(norm_g.reshape(NL * 3, ngw), ((0, 8 - NL * 3), (0, D - ngw)))], axis=0)
    small_all, _ = gather_small(small_in, "gather_c_normg")
    c_all = small_all[:, 0, :]
    ng_parts = small_all[0::2, 8:8 + NL * 3, :ngw]
    ngs = jnp.transpose(ng_parts, (1, 0, 2)).reshape(NL, 3, N_CHIP * ngw)

    nmod = ada_w.shape[2]
    ada_b_mine = lax.dynamic_slice_in_dim(ada_b, ci * nmod, nmod, axis=1).reshape(NL, 1, nmod)
    mod_part = ada_fwd(c_all, ada_w, ada_b_mine, "ada_fwd")
    mod_all, _ = gather_small(mod_part.reshape(NL * N_DEV, nmod), "gather_mod")
    mod_rows = lax.dynamic_index_in_dim(mod_all.reshape(N_DEV, NL, N_DEV, nmod), me, axis=2, keepdims=False)
    mods = jnp.transpose(mod_rows[0::2], (1, 0, 2)).reshape(NL, N_ADA, D)

    sgus = []
    for l in range(NL):
        sgus.append((sgu_ln_g[l].reshape(1, MIX_HALF), sgu_ln_b[l].reshape(1, MIX_HALF), sgu_w[l],
                     jnp.swapaxes(sgu_w[l], 1, 2), jnp.transpose(sgu_b[l])))

    def halves(a):
        n, r, _ = a.shape
        return a.reshape(n, 2, r // 2, D)

    Fs = ffn1_wd.shape[1]
    groups = []
    for l in range(NL):
        f1 = jnp.stack([ffn1_wg[l].T, ffn1_wu[l].T, ffn1_wd[l]], axis=0).astype(BF16)
        f2 = jnp.stack([ffn2_wg[l].T, ffn2_wu[l].T, ffn2_wd[l]], axis=0).astype(BF16)
        groups += [[halves(f1)], [halves(w_in[l].T.astype(BF16)[None]), halves(w_out[l].astype(BF16)[None])],
                   [halves(f2)]]
    handles, token = gather_start(groups, mods, "gather_start")
    mods = mods + token[0, 0]
    block_no = {"f1": 0, "mx": 1, "f2": 2}

    def get_w(l, blk, after):
        g = 3 * l + block_no[blk]
        full = gather_wait(handles[g], after, f"gather_wait_l{l}{blk}")
        full = [a.reshape(a.shape[0], N_CHIP * 2 * a.shape[3], D) for a in full]
        return full[0] if blk != "mx" else tuple(full)

    def split(a):
        n, r4, _ = a.shape
        return a.reshape(n, N_CHIP, 2, r4 // N_CHIP // 2, D)

    pending, stash = {}, {}

    def reduce_start(l, bufs, after=None):
        parts = [split(g) for g in bufs]
        lands = sibling_send_half(parts, f"rs_sibling_l{l}")
        psums = [sum_halves(g, ld, c_idx, f"rs_sum_halves_l{l}_{i}") for i, (g, ld) in enumerate(zip(parts, lands))]
        handle, tok = scatter_start(psums, psums[0] if after is None else after, f"rs_chips_start_l{l}")
        pending[l] = handle
        return tok

    def reduce_finish(l, after):
        psums, lands2 = scatter_wait(pending.pop(l), after, f"rs_chips_wait_l{l}")
        ssums = [sum_chips(p, ld, place, f"rs_sum_chips_l{l}_{i}") for i, (p, ld) in enumerate(zip(psums, lands2))]
        return sibling_complete(ssums, f"rs_complete_l{l}")

    def on_layer_grads(l, bufs):
        if l == NL - 1:
            return reduce_start(l, bufs)
        stash[l] = bufs
        return None

    grad_x, lg, red_final = _local_step(x[0], loss_target[0], mods, ngs, get_w, sgus, final_g.reshape(1, D),
                                        on_layer_grads)

    blocks = []
    for l in range(NL):
        blocks += list(lg[l]["red_n"]) + list(lg[l]["red_g"])
        blocks.append(jnp.pad(lg[l]["sgu_vec"], ((0, 0), (0, D - MIX_HALF))))
        blocks.append(lg[l]["sgu_w"].reshape(-1, D))
    blocks.append(red_final)
    offs, o = [], 0
    for b in blocks:
        offs.append(o)
        o += b.shape[0]
    small_all, small_sum = gather_small(jnp.concatenate(blocks, axis=0), "gather_small_grads")
    per_layer = 8
    loss = small_sum[offs[-1] + 1, 0]
    g_final_g = small_sum[offs[-1], :]
    g_norm_g, g_ada_b, g_lng, g_lnb, g_sb, g_sw, dmod_all = [], [], [], [], [], [], []
    for l in range(NL):
        b0 = per_layer * l
        rn = [small_sum[offs[b0 + i]:offs[b0 + i] + 8] for i in range(3)]
        rg = [small_sum[offs[b0 + 3 + i]:offs[b0 + 3 + i] + 8] for i in range(3)]
        g_norm_g.append(jnp.stack([rn[i][2] for i in range(3)], axis=0))
        g_ada_b.append(jnp.concatenate([jnp.stack([rn[i][0], rn[i][1], rg[i][0]], axis=0) for i in range(3)],
                                       axis=0).reshape(N_ADA * D))
        sv = small_sum[offs[b0 + 6]:offs[b0 + 6] + 8, :MIX_HALF]
        g_lng.append(sv[0].reshape(SGU_HEADS, HEAD_LANES))
        g_lnb.append(sv[1].reshape(SGU_HEADS, HEAD_LANES))
        g_sb.append(sv[2].reshape(SGU_HEADS, ATT_BLOCK))
        g_sw.append(small_sum[offs[b0 + 7]:offs[b0 + 7] + SGU_HEADS * ATT_BLOCK * HEAD_LANES // D].reshape(sgu_w.shape[1:]))
        rows = []
        for i in range(3):
            an = small_all[:, offs[b0 + i]:offs[b0 + i] + 2]
            ag = small_all[:, offs[b0 + 3 + i]:offs[b0 + 3 + i] + 1]
            rows += [an[:, 0], an[:, 1], ag[:, 0]]
        dmod_all.append(jnp.stack(rows, axis=1).reshape(N_DEV, N_ADA * D))
    dmod_all = jnp.stack(dmod_all, axis=0)
    dmod_mine = lax.dynamic_slice_in_dim(dmod_all, ci * nmod, nmod, axis=2)
    g_ada_w = ada_bwd(jnp.transpose(c_all), dmod_mine, "ada_bwd")
    g_ada_b = jnp.stack(g_ada_b, axis=0)
    g_norm_g_full = jnp.stack(g_norm_g, axis=0)
    g_norm_g_mine = lax.dynamic_slice_in_dim(g_norm_g_full, ci * ngw, ngw, axis=2)

    fins = [None] * NL
    for l in range(NL - 1):
        reduce_start(l, stash.pop(l), after=small_sum)
    for l in range(NL - 1, -1, -1):
        fins[l] = reduce_finish(l, g_ada_w)

    def per_layer(i, k, transpose):
        gs = [fins[l][i].reshape(fins[l][i].shape[0], -1, D)[k] for l in range(NL)]
        return jnp.stack([g.T if transpose else g for g in gs], axis=0)

    g_ffn1_wg, g_ffn1_wu, g_ffn1_wd = per_layer(0, 0, True), per_layer(0, 1, True), per_layer(0, 2, False)
    g_ffn2_wg, g_ffn2_wu, g_ffn2_wd = per_layer(0, 3, True), per_layer(0, 4, True), per_layer(0, 5, False)
    g_w_in = per_layer(1, 0, True)
    g_w_out = per_layer(2, 0, False)

    triples = [
        ("ada_w", ada_w, g_ada_w, m_ada_w, v_ada_w),
        ("ada_b", ada_b, g_ada_b, m_ada_b, v_ada_b),
        ("norm_g", norm_g, g_norm_g_mine, m_norm_g, v_norm_g),
        ("ffn1_wg", ffn1_wg, g_ffn1_wg, m_ffn1_wg, v_ffn1_wg),
        ("ffn1_wu", ffn1_wu, g_ffn1_wu, m_ffn1_wu, v_ffn1_wu),
        ("ffn1_wd", ffn1_wd, g_ffn1_wd, m_ffn1_wd, v_ffn1_wd),
        ("ffn2_wg", ffn2_wg, g_ffn2_wg, m_ffn2_wg, v_ffn2_wg),
        ("ffn2_wu", ffn2_wu, g_ffn2_wu, m_ffn2_wu, v_ffn2_wu),
        ("ffn2_wd", ffn2_wd, g_ffn2_wd, m_ffn2_wd, v_ffn2_wd),
        ("w_in", w_in, g_w_in, m_w_in, v_w_in),
        ("sgu_ln_g", sgu_ln_g, jnp.stack(g_lng, axis=0), m_sgu_ln_g, v_sgu_ln_g),
        ("sgu_ln_b", sgu_ln_b, jnp.stack(g_lnb, axis=0), m_sgu_ln_b, v_sgu_ln_b),
        ("sgu_w", sgu_w, jnp.stack(g_sw, axis=0), m_sgu_w, v_sgu_w),
        ("sgu_b", sgu_b, jnp.stack(g_sb, axis=0), m_sgu_b, v_sgu_b),
        ("w_out", w_out, g_w_out, m_w_out, v_w_out),
        ("final_g", final_g.reshape(1, D), g_final_g.reshape(1, D), m_final_g.reshape(1, D), v_final_g.reshape(1, D)),
    ]
    grads, deltas, new_ms, new_vs = [], [], [], []
    for nm, w, g, m, v in triples:
        g, dlt, mn, vn = _adam_out(w, g, m, v, f"adamw_{nm}")
        if nm == "final_g":
            g, dlt, mn, vn = (t.reshape(D) for t in (g, dlt, mn, vn))
        grads.append(g)
        deltas.append(dlt)
        new_ms.append(mn)
        new_vs.append(vn)
    return (loss, grad_x[None], *grads, *deltas, *new_ms, *new_vs)
```

```python
import math

import jax
import jax.numpy as jnp
from jax import lax
from jax.experimental import pallas as pl
from jax.experimental.pallas import tpu as pltpu

F32 = jnp.float32
BF16 = jnp.bfloat16
EPS = 1e-6
SGU_HEADS = 4
HEAD_LANES = 128
ATT_DH = 64
ATT_BLOCK = 128
MIX_HALF = SGU_HEADS * HEAD_LANES
DILATIONS = (1, 4, 16)
ROPE_THETA = 10000.0
N_ADA = 9
ADAM_LR, ADAM_B1, ADAM_B2, ADAM_EPS, ADAM_WD, ADAM_STEP = 0.001, 0.9, 0.999, 1e-08, 0.01, 10
NEG = -1e30
V7X_VMEM_BYTES = 64 * 1024 * 1024
VMEM_LIMIT = V7X_VMEM_BYTES * 7 // 8
MESH = pl.DeviceIdType.MESH
N_DEV = 8
N_CHIP = 4


def _tile(n, cap, mult):
    if n <= cap:
        return n
    t = (cap // mult) * mult
    while t >= mult:
        if n % t == 0:
            return t
        t -= mult
    raise ValueError((n, cap, mult))


def _params(dims=None):
    return pltpu.CompilerParams(dimension_semantics=dims, vmem_limit_bytes=VMEM_LIMIT)


def _wspec(w, rows, idx):
    arr, lead = w
    return pl.BlockSpec((None,) * len(lead) + (rows, arr.shape[-1]), lambda *g: tuple(lead) + (idx(*g), 0))


def _wrows(w):
    return w[0].shape[-2]


def _nt(a, b):
    return lax.dot_general(a, b, (((1,), (1,)), ((), ())), preferred_element_type=F32)


def _tn(a, b):
    return lax.dot_general(a, b, (((0,), (0,)), ((), ())), preferred_element_type=F32)


def _nn(a, b):
    return jnp.dot(a, b, preferred_element_type=F32)


def _sigmoid(x):
    return 1.0 / (1.0 + jnp.exp(-x))


_GELU_K = math.sqrt(2.0 / math.pi)
_GELU_C = 0.044715


def _gelu(x):
    t = jnp.tanh(_GELU_K * (x + _GELU_C * x * x * x))
    return 0.5 * x * (1.0 + t)


def _gelu_and_grad(x):
    x2 = x * x
    t = jnp.tanh(_GELU_K * (x + _GELU_C * x * x2))
    g = 0.5 * x * (1.0 + t)
    dg = 0.5 * (1.0 + t) + 0.5 * x * (1.0 - t * t) * (_GELU_K * (1.0 + 3.0 * _GELU_C * x2))
    return g, dg


def normmod_fwd(h, ng, i_n, mod, i_sh, i_sc, name):
    T, D = h.shape
    tm = _tile(T, 512, 8)

    def body(h_ref, ng_ref, mod_ref, y_ref):
        x = h_ref[...]
        r = lax.rsqrt(jnp.mean(x * x, axis=-1, keepdims=True) + EPS)
        y = (x * r) * ng_ref[i_n:i_n + 1, :]
        y_ref[...] = (y * (1.0 + mod_ref[i_sc:i_sc + 1, :]) + mod_ref[i_sh:i_sh + 1, :]).astype(BF16)

    return pl.pallas_call(
        body, name=name, grid=(T // tm,),
        in_specs=[pl.BlockSpec((tm, D), lambda i: (i, 0)),
                  pl.BlockSpec(ng.shape, lambda i: (0, 0)),
                  pl.BlockSpec(mod.shape, lambda i: (0, 0))],
        out_specs=pl.BlockSpec((tm, D), lambda i: (i, 0)),
        out_shape=jax.ShapeDtypeStruct((T, D), BF16),
        compiler_params=_params(("parallel",)),
    )(h, ng, mod)


def ffn_up(y, wgT, wuT, name):
    T, D = y.shape
    F = _wrows(wgT)
    tm = _tile(T, 512, 16)
    tf = _tile(F, 1408, 128)

    def body(y_ref, wg_ref, wu_ref, a_ref, b_ref, s_ref):
        yv = y_ref[...]
        a = _nt(yv, wg_ref[...])
        b = _nt(yv, wu_ref[...])
        a_ref[...] = a.astype(BF16)
        b_ref[...] = b.astype(BF16)
        s_ref[...] = (a * _sigmoid(a) * b).astype(BF16)

    act = jax.ShapeDtypeStruct((T, F), BF16)
    return pl.pallas_call(
        body, name=name, grid=(F // tf, T // tm),
        in_specs=[pl.BlockSpec((tm, D), lambda j, i: (i, 0)),
                  _wspec(wgT, tf, lambda j, i: j),
                  _wspec(wuT, tf, lambda j, i: j)],
        out_specs=[pl.BlockSpec((tm, tf), lambda j, i: (i, j))] * 3,
        out_shape=[act, act, act],
        compiler_params=_params(("parallel", "parallel")),
    )(y, wgT[0], wuT[0])


def resid_matmul(xs, w, h, mod, i_g, coef, name):
    T, D = h.shape
    kb = xs[0].shape[1]
    assert all(x.shape == (T, kb) for x in xs) and _wrows(w) == kb * len(xs)
    tm = _tile(T, 512, 16)
    nx = len(xs)

    def body(*refs):
        x_refs, w_refs = refs[:nx], refs[nx:2 * nx]
        h_ref, mod_ref, hn_ref, o_ref = refs[2 * nx:]
        o = _nn(x_refs[0][...], w_refs[0][...])
        for xr, wr in zip(x_refs[1:], w_refs[1:]):
            o = o + _nn(xr[...], wr[...])
        o_ref[...] = o.astype(BF16)
        hn_ref[...] = h_ref[...] + (coef * mod_ref[i_g:i_g + 1, :]) * o

    return pl.pallas_call(
        body, name=name, grid=(T // tm,),
        in_specs=([pl.BlockSpec((tm, kb), lambda i: (i, 0))] * nx
                  + [_wspec(w, kb, lambda i, p=p: p) for p in range(nx)]
                  + [pl.BlockSpec((tm, D), lambda i: (i, 0)),
                     pl.BlockSpec(mod.shape, lambda i: (0, 0))]),
        out_specs=[pl.BlockSpec((tm, D), lambda i: (i, 0))] * 2,
        out_shape=[jax.ShapeDtypeStruct((T, D), F32), jax.ShapeDtypeStruct((T, D), BF16)],
        compiler_params=_params(("parallel",)),
    )(*xs, *([w[0]] * nx), h, mod)


def gate_bwd(dh, o, mod, i_g, coef, name):
    T, D = dh.shape
    tm = _tile(T, 512, 16)

    def body(dh_ref, o_ref, mod_ref, do_ref, red_ref):
        d = dh_ref[...]
        do_ref[...] = (d * (coef * mod_ref[i_g:i_g + 1, :])).astype(BF16)

        @pl.when(pl.program_id(0) == 0)
        def _():
            red_ref[...] = jnp.zeros_like(red_ref)

        red_ref[0:1, :] += coef * jnp.sum(d * o_ref[...].astype(F32), axis=0, keepdims=True)

    return pl.pallas_call(
        body, name=name, grid=(T // tm,),
        in_specs=[pl.BlockSpec((tm, D), lambda i: (i, 0)),
                  pl.BlockSpec((tm, D), lambda i: (i, 0)),
                  pl.BlockSpec(mod.shape, lambda i: (0, 0))],
        out_specs=[pl.BlockSpec((tm, D), lambda i: (i, 0)), pl.BlockSpec((8, D), lambda i: (0, 0))],
        out_shape=[jax.ShapeDtypeStruct((T, D), BF16), jax.ShapeDtypeStruct((8, D), F32)],
        compiler_params=_params(("arbitrary",)),
    )(dh, o, mod)


def ffn_bwd_mid(do, wd, a, b, name):
    T, D = do.shape
    F = _wrows(wd)
    tm = _tile(T, 512, 16)
    tf = _tile(F, 1408, 128)

    def body(do_ref, wd_ref, a_ref, b_ref, da_ref, db_ref):
        ds = _nt(do_ref[...], wd_ref[...])
        av = a_ref[...].astype(F32)
        bv = b_ref[...].astype(F32)
        sig = _sigmoid(av)
        da_ref[...] = (ds * bv * (sig * (1.0 + av * (1.0 - sig)))).astype(BF16)
        db_ref[...] = (ds * (av * sig)).astype(BF16)

    act = jax.ShapeDtypeStruct((T, F), BF16)
    return pl.pallas_call(
        body, name=name, grid=(F // tf, T // tm),
        in_specs=[pl.BlockSpec((tm, D), lambda j, i: (i, 0)),
                  _wspec(wd, tf, lambda j, i: j),
                  pl.BlockSpec((tm, tf), lambda j, i: (i, j)),
                  pl.BlockSpec((tm, tf), lambda j, i: (i, j))],
        out_specs=[pl.BlockSpec((tm, tf), lambda j, i: (i, j))] * 2,
        out_shape=[act, act],
        compiler_params=_params(("parallel", "parallel")),
    )(do, wd[0], a, b)


def dy_normbwd(pairs, h, dhp, ng, i_n, mod, i_sc, name):
    T, D = h.shape
    tm = _tile(T, 256, 16)
    npair = len(pairs)

    def body(*refs):
        x_refs, w_refs = refs[:npair], refs[npair:2 * npair]
        h_ref, dhp_ref, ng_ref, mod_ref, dh_ref, red_ref = refs[2 * npair:]
        dy = _nn(x_refs[0][...], w_refs[0][...])
        for xr, wr in zip(x_refs[1:], w_refs[1:]):
            dy = dy + _nn(xr[...], wr[...])
        x = h_ref[...]
        r = lax.rsqrt(jnp.mean(x * x, axis=-1, keepdims=True) + EPS)
        n = x * r
        gn = ng_ref[i_n:i_n + 1, :]
        dnh = dy * (1.0 + mod_ref[i_sc:i_sc + 1, :])

        @pl.when(pl.program_id(0) == 0)
        def _():
            red_ref[...] = jnp.zeros_like(red_ref)

        red_ref[0:1, :] += jnp.sum(dy, axis=0, keepdims=True)
        red_ref[1:2, :] += jnp.sum(dy * (n * gn), axis=0, keepdims=True)
        red_ref[2:3, :] += jnp.sum(dnh * n, axis=0, keepdims=True)
        dn = dnh * gn
        dh_ref[...] = dhp_ref[...] + r * (dn - n * jnp.mean(dn * n, axis=-1, keepdims=True))

    in_specs = ([pl.BlockSpec((tm, kb), lambda i, c=c: (i, c)) for (_, c, _, _, kb) in pairs]
                + [_wspec(w, kb, lambda i, r=r: r) for (_, _, w, r, kb) in pairs]
                + [pl.BlockSpec((tm, D), lambda i: (i, 0)),
                   pl.BlockSpec((tm, D), lambda i: (i, 0)),
                   pl.BlockSpec(ng.shape, lambda i: (0, 0)),
                   pl.BlockSpec(mod.shape, lambda i: (0, 0))])
    return pl.pallas_call(
        body, name=name, grid=(T // tm,), in_specs=in_specs,
        out_specs=[pl.BlockSpec((tm, D), lambda i: (i, 0)), pl.BlockSpec((8, D), lambda i: (0, 0))],
        out_shape=[jax.ShapeDtypeStruct((T, D), F32), jax.ShapeDtypeStruct((8, D), F32)],
        compiler_params=_params(("arbitrary",)),
    )(*[p[0] for p in pairs], *[p[2][0] for p in pairs], h, dhp, ng, mod)


def matmul_tn(a, b, buf, slot, row0, name, tmo_cap=1408):
    T, N = b.shape
    ma = a.shape[1]
    tmo = _tile(ma, tmo_cap, 128)
    assert row0 % tmo == 0
    nmo = ma // tmo
    tk = _tile(T, 512, 16)
    nk = T // tk

    def body(a_ref, b_ref, buf_ref, o_ref, acc_ref):
        k = pl.program_id(1)

        @pl.when(k == 0)
        def _():
            acc_ref[...] = jnp.zeros_like(acc_ref)

        acc_ref[...] += _tn(a_ref[...], b_ref[...])

        @pl.when(k == nk - 1)
        def _():
            o_ref[...] = acc_ref[...].astype(BF16)

    return pl.pallas_call(
        body, name=name, grid=(nmo, nk),
        in_specs=[pl.BlockSpec((tk, tmo), lambda j, k: (k, j)),
                  pl.BlockSpec((tk, N), lambda j, k: (k, 0)),
                  pl.BlockSpec(memory_space=pl.ANY)],
        out_specs=pl.BlockSpec((None, tmo, N), lambda j, k: (slot, row0 // tmo + j, 0)),
        out_shape=jax.ShapeDtypeStruct(buf.shape, BF16),
        scratch_shapes=[pltpu.VMEM((tmo, N), F32)],
        input_output_aliases={2: 0},
        compiler_params=_params(("parallel", "arbitrary")),
    )(a, b, buf)


def matmul_nt(x, w, name):
    T, K = x.shape
    N = _wrows(w)
    tm = _tile(T, 512, 16)
    tn = _tile(N, 1280, 128)

    def body(x_ref, w_ref, o_ref):
        o_ref[...] = _nt(x_ref[...], w_ref[...]).astype(BF16)

    return pl.pallas_call(
        body, name=name, grid=(N // tn, T // tm),
        in_specs=[pl.BlockSpec((tm, K), lambda j, i: (i, 0)), _wspec(w, tn, lambda j, i: j)],
        out_specs=pl.BlockSpec((tm, tn), lambda j, i: (i, j)),
        out_shape=jax.ShapeDtypeStruct((T, N), BF16),
        compiler_params=_params(("parallel", "parallel")),
    )(x, w[0])


def _sgu_head_fwd(u, v, lng, lnb):
    gu, dgu = _gelu_and_grad(u)
    gv, dgv = _gelu_and_grad(v)
    mu = jnp.mean(gv, axis=-1, keepdims=True)
    xc = gv - mu
    rstd = lax.rsqrt(jnp.mean(xc * xc, axis=-1, keepdims=True) + EPS)
    xhat = xc * rstd
    vn = xhat * lng + lnb
    return gu, dgu, dgv, rstd, xhat, vn


def _tril_mask():
    r = lax.broadcasted_iota(jnp.int32, (ATT_BLOCK, ATT_BLOCK), 0)
    c = lax.broadcasted_iota(jnp.int32, (ATT_BLOCK, ATT_BLOCK), 1)
    return c <= r


def _triu_mask():
    r = lax.broadcasted_iota(jnp.int32, (ATT_BLOCK, ATT_BLOCK), 0)
    c = lax.broadcasted_iota(jnp.int32, (ATT_BLOCK, ATT_BLOCK), 1)
    return r <= c


def sgu_fwd(proj, lng, lnb, w, bcol, name):
    T = proj.shape[0]
    tm = _tile(T, 512, 128)
    nch = tm // ATT_BLOCK

    def body(u_ref, v_ref, lng_ref, lnb_ref, w_ref, b_ref, o_ref):
        tril = _tril_mask()
        for hd in range(SGU_HEADS):
            sl = slice(hd * HEAD_LANES, (hd + 1) * HEAD_LANES)
            u = u_ref[:, sl].astype(F32)
            v = v_ref[:, sl].astype(F32)
            gu, _, _, _, _, vn = _sgu_head_fwd(u, v, lng_ref[:, sl], lnb_ref[:, sl])
            wm = jnp.where(tril, w_ref[hd], 0.0).astype(BF16)
            vnb = vn.astype(BF16)
            bc = b_ref[:, hd:hd + 1]
            for ch in range(nch):
                rs = slice(ch * ATT_BLOCK, (ch + 1) * ATT_BLOCK)
                z = _nn(wm, vnb[rs, :]) + bc
                o_ref[rs, sl] = (gu[rs, :] * z).astype(BF16)

    return pl.pallas_call(
        body, name=name, grid=(T // tm,),
        in_specs=[pl.BlockSpec((tm, MIX_HALF), lambda i: (i, 0)),
                  pl.BlockSpec((tm, MIX_HALF), lambda i: (i, 1)),
                  pl.BlockSpec((1, MIX_HALF), lambda i: (0, 0)),
                  pl.BlockSpec((1, MIX_HALF), lambda i: (0, 0)),
                  pl.BlockSpec(w.shape, lambda i: (0, 0, 0)),
                  pl.BlockSpec(bcol.shape, lambda i: (0, 0))],
        out_specs=pl.BlockSpec((tm, MIX_HALF), lambda i: (i, 0)),
        out_shape=jax.ShapeDtypeStruct((T, MIX_HALF), BF16),
        compiler_params=_params(("parallel",)),
    )(proj, proj, lng, lnb, w, bcol)


def sgu_bwd(proj, dmixed, lng, lnb, w, wt, bcol, name):
    T = proj.shape[0]
    tm = _tile(T, 512, 128)
    nch = tm // ATT_BLOCK
    nsteps = T // tm

    def body(u_ref, v_ref, g_ref, lng_ref, lnb_ref, w_ref, wt_ref, b_ref, duv_ref, dw_ref, dvec_ref, bacc_ref):
        step = pl.program_id(0)

        @pl.when(step == 0)
        def _():
            dw_ref[...] = jnp.zeros_like(dw_ref)
            dvec_ref[...] = jnp.zeros_like(dvec_ref)
            bacc_ref[...] = jnp.zeros_like(bacc_ref)

        tril = _tril_mask()
        triu = _triu_mask()
        for hd in range(SGU_HEADS):
            sl = slice(hd * HEAD_LANES, (hd + 1) * HEAD_LANES)
            u = u_ref[:, sl].astype(F32)
            v = v_ref[:, sl].astype(F32)
            lng_h = lng_ref[:, sl]
            gu, dgu, dgv, rstd, xhat, vn = _sgu_head_fwd(u, v, lng_h, lnb_ref[:, sl])
            wm = jnp.where(tril, w_ref[hd], 0.0).astype(BF16)
            wmt = jnp.where(triu, wt_ref[hd], 0.0).astype(BF16)
            vnb = vn.astype(BF16)
            bc = b_ref[:, hd:hd + 1]
            g = g_ref[:, sl].astype(F32)
            dw_acc = jnp.zeros((ATT_BLOCK, ATT_BLOCK), F32)
            b_acc = jnp.zeros((ATT_BLOCK, HEAD_LANES), F32)
            dvn_parts = []
            for ch in range(nch):
                rs = slice(ch * ATT_BLOCK, (ch + 1) * ATT_BLOCK)
                z = _nn(wm, vnb[rs, :]) + bc
                duv_ref[rs, sl] = (g[rs, :] * z * dgu[rs, :]).astype(BF16)
                dz = g[rs, :] * gu[rs, :]
                dzb = dz.astype(BF16)
                dvn_parts.append(_nn(wmt, dzb))
                dw_acc = dw_acc + _nt(dzb, vnb[rs, :])
                b_acc = b_acc + dz
            dvn = jnp.concatenate(dvn_parts, axis=0)
            dw_ref[hd] += jnp.where(tril, dw_acc, 0.0)
            bacc_ref[hd] += b_acc
            dvec_ref[0:1, sl] += jnp.sum(dvn * xhat, axis=0, keepdims=True)
            dvec_ref[1:2, sl] += jnp.sum(dvn, axis=0, keepdims=True)
            dxh = dvn * lng_h
            dgv_in = rstd * (dxh - jnp.mean(dxh, axis=-1, keepdims=True)
                             - xhat * jnp.mean(dxh * xhat, axis=-1, keepdims=True))
            duv_ref[:, MIX_HALF + hd * HEAD_LANES:MIX_HALF + (hd + 1) * HEAD_LANES] = (dgv_in * dgv).astype(BF16)

        @pl.when(step == nsteps - 1)
        def _():
            for hd in range(SGU_HEADS):
                sl = slice(hd * HEAD_LANES, (hd + 1) * HEAD_LANES)
                dvec_ref[2:3, sl] = jnp.sum(bacc_ref[hd].T, axis=0, keepdims=True)

    return pl.pallas_call(
        body, name=name, grid=(nsteps,),
        in_specs=[pl.BlockSpec((tm, MIX_HALF), lambda i: (i, 0)),
                  pl.BlockSpec((tm, MIX_HALF), lambda i: (i, 1)),
                  pl.BlockSpec((tm, MIX_HALF), lambda i: (i, 0)),
                  pl.BlockSpec((1, MIX_HALF), lambda i: (0, 0)),
                  pl.BlockSpec((1, MIX_HALF), lambda i: (0, 0)),
                  pl.BlockSpec(w.shape, lambda i: (0, 0, 0)),
                  pl.BlockSpec(w.shape, lambda i: (0, 0, 0)),
                  pl.BlockSpec(bcol.shape, lambda i: (0, 0))],
        out_specs=[pl.BlockSpec((tm, 2 * MIX_HALF), lambda i: (i, 0)),
                   pl.BlockSpec(w.shape, lambda i: (0, 0, 0)),
                   pl.BlockSpec((8, MIX_HALF), lambda i: (0, 0))],
        out_shape=[jax.ShapeDtypeStruct((T, 2 * MIX_HALF), BF16),
                   jax.ShapeDtypeStruct(w.shape, F32),
                   jax.ShapeDtypeStruct((8, MIX_HALF), F32)],
        scratch_shapes=[pltpu.VMEM((SGU_HEADS, ATT_BLOCK, HEAD_LANES), F32)],
        compiler_params=_params(("arbitrary",)),
    )(proj, proj, dmixed, lng, lnb, w, wt, bcol)


def _rot_half(t):
    lane = lax.broadcasted_iota(jnp.int32, t.shape, 1)
    first = (lane % ATT_DH) < (ATT_DH // 2)
    return jnp.where(first, -pltpu.roll(t, HEAD_LANES - ATT_DH // 2, 1), pltpu.roll(t, ATT_DH // 2, 1))


def rope_fwd(proj, cos, sin, name):
    T = proj.shape[0]
    tm = _tile(T, 512, 16)
    scale = 1.0 / math.sqrt(ATT_DH)

    def body(q_ref, k_ref, cos_ref, sin_ref, qo_ref, ko_ref):
        c = cos_ref[...]
        s = sin_ref[...]
        for hp in range(MIX_HALF // HEAD_LANES):
            sl = slice(hp * HEAD_LANES, (hp + 1) * HEAD_LANES)
            q = q_ref[:, sl].astype(F32)
            k = k_ref[:, sl].astype(F32)
            qo_ref[:, sl] = (scale * (q * c + _rot_half(q) * s)).astype(BF16)
            ko_ref[:, sl] = (k * c + _rot_half(k) * s).astype(BF16)

    out = jax.ShapeDtypeStruct((T, MIX_HALF), BF16)
    return pl.pallas_call(
        body, name=name, grid=(T // tm,),
        in_specs=[pl.BlockSpec((tm, MIX_HALF), lambda i: (i, 2)),
                  pl.BlockSpec((tm, MIX_HALF), lambda i: (i, 3)),
                  pl.BlockSpec((tm, HEAD_LANES), lambda i: (i, 0)),
                  pl.BlockSpec((tm, HEAD_LANES), lambda i: (i, 0))],
        out_specs=[pl.BlockSpec((tm, MIX_HALF), lambda i: (i, 0))] * 2,
        out_shape=[out, out],
        compiler_params=_params(("parallel",)),
    )(proj, proj, cos, sin)


def rope_bwd(dqs, dks, dvs, cos, sin, name):
    T = dqs[0].shape[0]
    tm = _tile(T, 512, 16)
    scale = 1.0 / math.sqrt(ATT_DH)
    npat = len(dqs)

    def body(*refs):
        dq_refs, dk_refs, dv_refs = refs[:npat], refs[npat:2 * npat], refs[2 * npat:3 * npat]
        cos_ref, sin_ref, o_ref = refs[3 * npat:]
        c = cos_ref[...]
        s = sin_ref[...]
        for hp in range(MIX_HALF // HEAD_LANES):
            sl = slice(hp * HEAD_LANES, (hp + 1) * HEAD_LANES)
            gq = scale * sum(r[:, sl] for r in dq_refs)
            gk = sum(r[:, sl] for r in dk_refs)
            gv = sum(r[:, sl] for r in dv_refs)
            o_ref[:, sl] = (gq * c - _rot_half(gq * s)).astype(BF16)
            o_ref[:, MIX_HALF + hp * HEAD_LANES:MIX_HALF + (hp + 1) * HEAD_LANES] = (
                gk * c - _rot_half(gk * s)).astype(BF16)
            o_ref[:, 2 * MIX_HALF + hp * HEAD_LANES:2 * MIX_HALF + (hp + 1) * HEAD_LANES] = gv.astype(BF16)

    return pl.pallas_call(
        body, name=name, grid=(T // tm,),
        in_specs=([pl.BlockSpec((tm, MIX_HALF), lambda i: (i, 0))] * (3 * npat)
                  + [pl.BlockSpec((tm, HEAD_LANES), lambda i: (i, 0))] * 2),
        out_specs=pl.BlockSpec((tm, 3 * MIX_HALF), lambda i: (i, 0)),
        out_shape=jax.ShapeDtypeStruct((T, 3 * MIX_HALF), BF16),
        compiler_params=_params(("parallel",)),
    )(*dqs, *dks, *dvs, cos, sin)


def _band_masks(n):
    r = lax.broadcasted_iota(jnp.int32, (2 * ATT_BLOCK, ATT_BLOCK), 0)
    c = lax.broadcasted_iota(jnp.int32, (2 * ATT_BLOCK, ATT_BLOCK), 1)
    qi = r % ATT_BLOCK
    head = (c < ATT_DH) == (r < ATT_BLOCK)
    return (c >= qi) & (n > 0), c <= qi, head, c[:ATT_BLOCK] < ATT_DH


def _stack_heads(x, head):
    x2 = jnp.concatenate([x, x], axis=0)
    return jnp.where(head, x2, jnp.zeros_like(x2))


def attn_fwd(q, k, v, name):
    d, L, W = q.shape
    nb = L // ATT_BLOCK

    def body(q_ref, kp_ref, kc_ref, vp_ref, vc_ref, o_ref, lse_ref):
        mask_p, mask_c, head, head0 = _band_masks(pl.program_id(1))
        for hp in range(W // HEAD_LANES):
            sl = slice(hp * HEAD_LANES, (hp + 1) * HEAD_LANES)
            kp, kc, vp, vc = kp_ref[0, :, sl], kc_ref[0, :, sl], vp_ref[0, :, sl], vc_ref[0, :, sl]
            qs = _stack_heads(q_ref[0, :, sl], head)
            sp = jnp.where(mask_p, _nt(qs, kp), NEG)
            sc = jnp.where(mask_c, _nt(qs, kc), NEG)
            m = jnp.maximum(jnp.max(sp, axis=1, keepdims=True), jnp.max(sc, axis=1, keepdims=True))
            pp = jnp.exp(sp - m)
            pc = jnp.exp(sc - m)
            den = jnp.sum(pp, axis=1, keepdims=True) + jnp.sum(pc, axis=1, keepdims=True)
            o = (_nn(pp.astype(BF16), vp) + _nn(pc.astype(BF16), vc)) / den
            lse = m + jnp.log(den)
            o_ref[0, :, sl] = jnp.where(head0, o[:ATT_BLOCK], o[ATT_BLOCK:])
            lse_ref[0, :, sl] = jnp.where(head0, lse[:ATT_BLOCK], lse[ATT_BLOCK:])

    cur = pl.BlockSpec((1, ATT_BLOCK, W), lambda r, n: (r, n, 0))
    prev = pl.BlockSpec((1, ATT_BLOCK, W), lambda r, n: (r, jnp.maximum(n - 1, 0), 0))
    out = jax.ShapeDtypeStruct((d, L, W), F32)
    return pl.pallas_call(
        body, name=name, grid=(d, nb),
        in_specs=[cur, prev, cur, prev, cur],
        out_specs=[cur, cur], out_shape=[out, out],
        compiler_params=_params(("parallel", "parallel")),
    )(q, k, k, v, v)


def attn_combine(os_, lses, name):
    T, W = os_[0].shape
    tm = _tile(T, 512, 16)
    npat = len(os_)

    def body(*refs):
        o_refs, l_refs = refs[:npat], refs[npat:2 * npat]
        out_ref, lse_ref = refs[2 * npat:]
        ls = [r[...] for r in l_refs]
        m = ls[0]
        for l in ls[1:]:
            m = jnp.maximum(m, l)
        es = [jnp.exp(l - m) for l in ls]
        z = es[0]
        for e in es[1:]:
            z = z + e
        acc = es[0] * o_refs[0][...]
        for e, r in zip(es[1:], o_refs[1:]):
            acc = acc + e * r[...]
        out_ref[...] = (acc / z).astype(BF16)
        lse_ref[...] = m + jnp.log(z)

    blk = pl.BlockSpec((tm, W), lambda i: (i, 0))
    return pl.pallas_call(
        body, name=name, grid=(T // tm,),
        in_specs=[blk] * (2 * npat), out_specs=[blk, blk],
        out_shape=[jax.ShapeDtypeStruct((T, W), BF16), jax.ShapeDtypeStruct((T, W), F32)],
        compiler_params=_params(("parallel",)),
    )(*os_, *lses)


def attn_bwd(q, k, v, do, o, lse, name):
    d, L, W = q.shape
    nb = L // ATT_BLOCK

    def body(q_ref, kp_ref, kc_ref, vp_ref, vc_ref, do_ref, o_ref, lse_ref, dq_ref, dk_ref, dv_ref, kkeep, vkeep):
        n = pl.program_id(1)

        @pl.when(n < nb)
        def _():
            mask_p, mask_c, head, head0 = _band_masks(n)
            for hp in range(W // HEAD_LANES):
                sl = slice(hp * HEAD_LANES, (hp + 1) * HEAD_LANES)
                kp, kc, vp, vc = kp_ref[0, :, sl], kc_ref[0, :, sl], vp_ref[0, :, sl], vc_ref[0, :, sl]
                dout = do_ref[0, :, sl]
                qs = _stack_heads(q_ref[0, :, sl], head)
                dos = _stack_heads(dout, head)
                lse_v = lse_ref[0, :, sl]
                lse_c = jnp.max(jnp.where(head, jnp.concatenate([lse_v, lse_v], axis=0), NEG), axis=1, keepdims=True)
                delta = jnp.sum(_stack_heads(dout.astype(F32) * o_ref[0, :, sl].astype(F32), head), axis=1, keepdims=True)
                pp = jnp.exp(jnp.where(mask_p, _nt(qs, kp), NEG) - lse_c)
                pc = jnp.exp(jnp.where(mask_c, _nt(qs, kc), NEG) - lse_c)
                dsp = (pp * (_nt(dos, vp) - delta)).astype(BF16)
                dsc = (pc * (_nt(dos, vc) - delta)).astype(BF16)
                dq2 = _nn(dsp, kp) + _nn(dsc, kc)
                dq_ref[0, :, sl] = jnp.where(head0, dq2[:ATT_BLOCK], dq2[ATT_BLOCK:])
                kprev = _tn(dsp, qs)
                vprev = _tn(pp.astype(BF16), dos)

                @pl.when(n > 0)
                def _():
                    dk_ref[0, :, sl] = kkeep[:, sl] + kprev
                    dv_ref[0, :, sl] = vkeep[:, sl] + vprev

                kkeep[:, sl] = _tn(dsc, qs)
                vkeep[:, sl] = _tn(pc.astype(BF16), dos)

        @pl.when(n == nb)
        def _():
            dk_ref[0] = kkeep[...]
            dv_ref[0] = vkeep[...]

    cur = pl.BlockSpec((1, ATT_BLOCK, W), lambda r, n: (r, jnp.minimum(n, nb - 1), 0))
    prev = pl.BlockSpec((1, ATT_BLOCK, W), lambda r, n: (r, jnp.clip(n - 1, 0, nb - 1), 0))
    out = jax.ShapeDtypeStruct((d, L, W), F32)
    return pl.pallas_call(
        body, name=name, grid=(d, nb + 1),
        in_specs=[cur, prev, cur, prev, cur, cur, cur, cur],
        out_specs=[cur, prev, prev], out_shape=[out, out, out],
        scratch_shapes=[pltpu.VMEM((ATT_BLOCK, W), F32), pltpu.VMEM((ATT_BLOCK, W), F32)],
        compiler_params=_params(("parallel", "arbitrary")),
    )(q, k, k, v, v, do, o, lse)


def final_loss_bwd(h, gf, tgt, name):
    T, D = h.shape
    tm = _tile(T, 512, 8)

    def body(h_ref, g_ref, t_ref, dh_ref, red_ref):
        x = h_ref[...]
        r = lax.rsqrt(jnp.mean(x * x, axis=-1, keepdims=True) + EPS)
        n = x * r
        g = g_ref[...]
        err = n * g - t_ref[...]
        dy = err * (1.0 / D)

        @pl.when(pl.program_id(0) == 0)
        def _():
            red_ref[...] = jnp.zeros_like(red_ref)

        red_ref[0:1, :] += jnp.sum(dy * n, axis=0, keepdims=True)
        red_ref[1:2, :] += jnp.zeros((1, D), F32) + (0.5 / D) * jnp.sum(err * err, keepdims=True)
        dn = dy * g
        dh_ref[...] = r * (dn - n * jnp.mean(dn * n, axis=-1, keepdims=True))

    return pl.pallas_call(
        body, name=name, grid=(T // tm,),
        in_specs=[pl.BlockSpec((tm, D), lambda i: (i, 0)),
                  pl.BlockSpec((1, D), lambda i: (0, 0)),
                  pl.BlockSpec((tm, D), lambda i: (i, 0))],
        out_specs=[pl.BlockSpec((tm, D), lambda i: (i, 0)), pl.BlockSpec((8, D), lambda i: (0, 0))],
        out_shape=[jax.ShapeDtypeStruct((T, D), F32), jax.ShapeDtypeStruct((8, D), F32)],
        compiler_params=_params(("arbitrary",)),
    )(h, gf, tgt)


def ada_fwd(c_all, ada_w, ada_b, name):
    nl, D, N = ada_w.shape

    def body(c_ref, w_ref, b_ref, o_ref):
        c = c_ref[...]
        o_ref[0] = _nn(c * _sigmoid(c), w_ref[0]) + b_ref[0]

    return pl.pallas_call(
        body, name=name, grid=(nl,),
        in_specs=[pl.BlockSpec((N_DEV, D), lambda l: (0, 0)),
                  pl.BlockSpec((1, D, N), lambda l: (l, 0, 0)),
                  pl.BlockSpec((1, 1, N), lambda l: (l, 0, 0))],
        out_specs=pl.BlockSpec((1, N_DEV, N), lambda l: (l, 0, 0)),
        out_shape=jax.ShapeDtypeStruct((nl, N_DEV, N), F32),
        compiler_params=_params(("parallel",)),
    )(c_all, ada_w, ada_b)


def ada_bwd(c_allT, dmod, name):
    nl, _, N = dmod.shape
    D = c_allT.shape[0]

    def body(c_ref, g_ref, o_ref):
        c = c_ref[...]
        ca = c * _sigmoid(c)
        acc = ca[:, 0:1] * g_ref[0, 0:1, :]
        for b in range(1, N_DEV):
            acc = acc + ca[:, b:b + 1] * g_ref[0, b:b + 1, :]
        o_ref[0] = acc

    return pl.pallas_call(
        body, name=name, grid=(nl,),
        in_specs=[pl.BlockSpec((D, N_DEV), lambda l: (0, 0)),
                  pl.BlockSpec((1, N_DEV, N), lambda l: (l, 0, 0))],
        out_specs=pl.BlockSpec((1, D, N), lambda l: (l, 0, 0)),
        out_shape=jax.ShapeDtypeStruct((nl, D, N), F32),
        compiler_params=_params(("parallel",)),
    )(c_allT, dmod)


def adamw(w, g, m, v, name):
    R, C = w.shape
    tr = _tile(R, max(8, (1 << 19) // C // 8 * 8), 8)
    c1 = 1.0 - ADAM_B1 ** ADAM_STEP
    c2 = 1.0 - ADAM_B2 ** ADAM_STEP

    def body(w_ref, g_ref, m_ref, v_ref, d_ref, mo_ref, vo_ref):
        gv = g_ref[...]
        mn = ADAM_B1 * m_ref[...] + (1.0 - ADAM_B1) * gv
        vn = ADAM_B2 * v_ref[...] + (1.0 - ADAM_B2) * (gv * gv)
        mo_ref[...] = mn
        vo_ref[...] = vn
        d_ref[...] = -ADAM_LR * ((mn / c1) / (jnp.sqrt(vn / c2) + ADAM_EPS) + ADAM_WD * w_ref[...])

    blk = pl.BlockSpec((tr, C), lambda i: (i, 0))
    out = jax.ShapeDtypeStruct((R, C), F32)
    return pl.pallas_call(
        body, name=name, grid=(R // tr,),
        in_specs=[blk] * 4, out_specs=[blk] * 3, out_shape=[out] * 3,
        compiler_params=_params(("parallel",)),
    )(w, g, m, v)


def adamw_layer(w, g, m, v, l, prev, name):
    NLw, R, C = w.shape
    tr = _tile(R, max(8, (1 << 19) // C // 8 * 8), 8)
    c1 = 1.0 - ADAM_B1 ** ADAM_STEP
    c2 = 1.0 - ADAM_B2 ** ADAM_STEP

    def body(w_ref, g_ref, m_ref, v_ref, *rest):
        go_ref, d_ref, mo_ref, vo_ref = rest[-4:]
        gv = g_ref[...]
        mn = ADAM_B1 * m_ref[...] + (1.0 - ADAM_B1) * gv
        vn = ADAM_B2 * v_ref[...] + (1.0 - ADAM_B2) * (gv * gv)
        go_ref[...] = gv
        mo_ref[...] = mn
        vo_ref[...] = vn
        d_ref[...] = -ADAM_LR * ((mn / c1) / (jnp.sqrt(vn / c2) + ADAM_EPS) + ADAM_WD * w_ref[...])

    lay = pl.BlockSpec((None, tr, C), lambda i: (l, i, 0))
    out = jax.ShapeDtypeStruct((NLw, R, C), F32)
    n_prev = 0 if prev is None else 4
    return pl.pallas_call(
        body, name=name, grid=(R // tr,),
        in_specs=[lay, pl.BlockSpec((tr, C), lambda i: (i, 0)), lay, lay] + [pl.BlockSpec(memory_space=pl.ANY)] * n_prev,
        out_specs=[lay] * 4, out_shape=[out] * 4,
        input_output_aliases={4 + i: i for i in range(n_prev)},
        compiler_params=_params(("parallel",)),
    )(w, g, m, v, *(prev or ()))


def sum_slots(x, name):
    S, R, C = x.shape
    tr = _tile(R, 128, 8)

    def body(x_ref, o_ref):
        acc = x_ref[0]
        for s in range(1, S):
            acc = acc + x_ref[s]
        o_ref[...] = acc

    return pl.pallas_call(
        body, name=name, grid=(R // tr,),
        in_specs=[pl.BlockSpec((S, tr, C), lambda i: (0, i, 0))],
        out_specs=pl.BlockSpec((tr, C), lambda i: (i, 0)),
        out_shape=jax.ShapeDtypeStruct((R, C), F32),
        compiler_params=_params(("parallel",)),
    )(x)


def sum_halves(g, lands, c_idx, name):
    n, ns, _, rh, D = g.shape

    def body(c_ref, g_ref, l_ref, o_ref):
        o_ref[0, 0] = (g_ref[0, 0, 0].astype(F32) + l_ref[0, 0].astype(F32)).astype(BF16)

    return pl.pallas_call(
        body, name=name,
        grid_spec=pltpu.PrefetchScalarGridSpec(
            num_scalar_prefetch=1, grid=(n, ns),
            in_specs=[pl.BlockSpec((1, 1, 1, rh, D), lambda i, j, c: (i, j, c[0], 0, 0)),
                      pl.BlockSpec((1, 1, rh, D), lambda i, j, c: (i, j, 0, 0))],
            out_specs=pl.BlockSpec((1, 1, rh, D), lambda i, j, c: (i, j, 0, 0))),
        out_shape=jax.ShapeDtypeStruct((n, ns, rh, D), BF16),
        compiler_params=_params(("parallel", "parallel")),
    )(c_idx, g, lands)


def sum_chips(p, lands, place, name):
    n, ns, rh, D = p.shape

    def body(c_ref, p_ref, l_ref, o_ref):
        acc = p_ref[0, 0].astype(F32)
        for j in range(N_CHIP - 1):
            acc = acc + l_ref[j, 0].astype(F32)
        o_ref[0, 0] = acc

    return pl.pallas_call(
        body, name=name,
        grid_spec=pltpu.PrefetchScalarGridSpec(
            num_scalar_prefetch=1, grid=(n,),
            in_specs=[pl.BlockSpec((1, 1, rh, D), lambda i, c: (i, c[0], 0, 0)),
                      pl.BlockSpec((N_CHIP - 1, 1, rh, D), lambda i, c: (0, i, 0, 0))],
            out_specs=pl.BlockSpec((1, 1, rh, D), lambda i, c: (i, c[1], 0, 0))),
        out_shape=jax.ShapeDtypeStruct((n, 2, rh, D), F32),
        compiler_params=_params(("parallel",)),
    )(place, p, lands)


def _my_place():
    return lax.axis_index("x"), lax.axis_index("y"), lax.axis_index("c")


def _other_chips(mx, my):
    return [(1 - mx, my), (mx, 1 - my), (1 - mx, 1 - my)]


def gather_small(x, name):
    def body(x_ref, out_ref, sum_ref, send_sems, recv_sems):
        mx, my, mc = _my_place()
        me = 4 * mx + 2 * my + mc
        out_ref[me] = x_ref[...]
        sends = []
        for k in range(1, N_DEV):
            kx, ky, kc = (k >> 2) & 1, (k >> 1) & 1, k & 1
            peer = (1 - mx if kx else mx, 1 - my if ky else my, 1 - mc if kc else mc)
            cp = pltpu.make_async_remote_copy(
                src_ref=x_ref, dst_ref=out_ref.at[me], send_sem=send_sems.at[k - 1], recv_sem=recv_sems.at[k - 1],
                device_id=peer, device_id_type=MESH)
            cp.start()
            sends.append((cp, 4 * peer[0] + 2 * peer[1] + peer[2], peer))
        for k, (cp, peer_slot, peer) in enumerate(sends):
            pltpu.make_async_remote_copy(
                src_ref=x_ref, dst_ref=out_ref.at[peer_slot], send_sem=send_sems.at[k], recv_sem=recv_sems.at[k],
                device_id=peer, device_id_type=MESH).wait_recv()
        for cp, _, _ in sends:
            cp.wait_send()
        acc = out_ref[0]
        for s in range(1, N_DEV):
            acc = acc + out_ref[s]
        sum_ref[...] = acc

    vmem = pl.BlockSpec(memory_space=pltpu.VMEM)
    return pl.pallas_call(
        body, name=name,
        in_specs=[vmem], out_specs=[vmem, vmem],
        out_shape=[jax.ShapeDtypeStruct((N_DEV,) + x.shape, x.dtype), jax.ShapeDtypeStruct(x.shape, x.dtype)],
        scratch_shapes=[pltpu.SemaphoreType.DMA((N_DEV - 1,)), pltpu.SemaphoreType.DMA((N_DEV - 1,))],
        compiler_params=pltpu.CompilerParams(vmem_limit_bytes=VMEM_LIMIT),
    )(x)


_HBM =pl.BlockSpec(memory_space=pltpu.HBM)
_SEM = pl.BlockSpec(memory_space=pltpu.SEMAPHORE)
_DATAFLOW = pltpu.SideEffectType.DATAFLOW_SIDE_EFFECTING


def _gather_copies(shard, land, send, recv, base):
    mx, my, mc = _my_place()
    ci = 2 * mx + my
    peers = [((cx, cy, mc), 2 * cx + cy) for cx, cy in _other_chips(mx, my)] + [((mx, my, 1 - mc), ci)]
    out = []
    for q, (dev, src_slot) in enumerate(peers):
        out.append((
            pltpu.make_async_remote_copy(src_ref=shard, dst_ref=land.at[:, ci], send_sem=send.at[base + q],
                                         recv_sem=recv.at[base + q], device_id=dev, device_id_type=MESH),
            pltpu.make_async_remote_copy(src_ref=shard, dst_ref=land.at[:, src_slot], send_sem=send.at[base + q],
                                         recv_sem=recv.at[base + q], device_id=dev, device_id_type=MESH)))
    return out


def gather_start(groups, after, name):
    items = [s for g in groups for s in g]
    ni, ng = len(items), len(groups)

    def body(*refs):
        shards, lands = refs[:ni], refs[ni:2 * ni]
        sems = refs[2 * ni + 1:2 * ni + 1 + 2 * ng]
        token = refs[-1]
        i = 0
        for g, grp in enumerate(groups):
            for p in range(len(grp)):
                for start_cp, _ in _gather_copies(shards[i], lands[i], sems[2 * g], sems[2 * g + 1], 4 * p):
                    start_cp.start()
                i += 1
        token[...] = jnp.zeros_like(token)

    sem_shapes = []
    for grp in groups:
        sem_shapes += [pltpu.SemaphoreType.DMA((4 * len(grp),))] * 2
    land_shapes = [(s.shape[0], N_CHIP) + s.shape[1:] for s in items]
    outs = pl.pallas_call(
        body, name=name,
        in_specs=[_HBM] * (2 * ni) + [pl.BlockSpec(memory_space=pl.ANY)],
        out_specs=[_SEM] * (2 * ng) + [_HBM] * (2 * ni) + [pl.BlockSpec(memory_space=pltpu.VMEM)],
        out_shape=(sem_shapes + [pltpu.HBM(s.shape, s.dtype) for s in items]
                   + [pltpu.HBM(ls, s.dtype) for ls, s in zip(land_shapes, items)]
                   + [jax.ShapeDtypeStruct((8, 128), F32)]),
        input_output_aliases={i: 2 * ng + i for i in range(2 * ni)},
        compiler_params=pltpu.CompilerParams(has_side_effects=_DATAFLOW),
    )(*[pltpu.with_memory_space_constraint(s, pltpu.HBM) for s in items],
      *[pltpu.with_memory_space_constraint(lax.empty(ls, s.dtype), pltpu.HBM) for ls, s in zip(land_shapes, items)],
      after)
    sems, thru, token = outs[:2 * ng], outs[2 * ng:2 * ng + 2 * ni], outs[-1]
    handles, i = [], 0
    for g, grp in enumerate(groups):
        n = len(grp)
        handles.append((sems[2 * g], sems[2 * g + 1], thru[i:i + n], thru[ni + i:ni + i + n]))
        i += n
    return handles, token


def gather_wait(handle, after, name):
    send, recv, shards, lands = handle
    n = len(shards)

    def body(*refs):
        shard_refs, land_refs = refs[:n], refs[n:2 * n]
        send_ref, recv_ref = refs[2 * n], refs[2 * n + 1]
        for p in range(n):
            for start_cp, recv_cp in _gather_copies(shard_refs[p], land_refs[p], send_ref, recv_ref, 4 * p):
                start_cp.wait_send()
                recv_cp.wait_recv()

    outs = pl.pallas_call(
        body, name=name,
        in_specs=[_HBM] * (2 * n) + [_SEM, _SEM, pl.BlockSpec(memory_space=pl.ANY)],
        out_specs=[_HBM] * (2 * n),
        out_shape=[pltpu.HBM(s.shape, s.dtype) for s in shards] + [pltpu.HBM(l.shape, l.dtype) for l in lands],
        input_output_aliases={i: i for i in range(2 * n)},
        compiler_params=pltpu.CompilerParams(has_side_effects=_DATAFLOW),
    )(*shards, *lands, send, recv, after)
    return outs[n:]


def sibling_send_half(gs, name):
    K = len(gs)

    def body(*refs):
        ins, outs = refs[:K], refs[K:2 * K]
        send, recv = refs[2 * K:]
        mx, my, mc = _my_place()
        cps = []
        for k in range(K):
            cp = pltpu.make_async_remote_copy(
                src_ref=ins[k].at[:, :, 1 - mc], dst_ref=outs[k], send_sem=send.at[k], recv_sem=recv.at[k],
                device_id=(mx, my, 1 - mc), device_id_type=MESH)
            cp.start()
            cps.append(cp)
        for cp in cps:
            cp.wait()

    hbm = pl.BlockSpec(memory_space=pl.ANY)
    return pl.pallas_call(
        body, name=name,
        in_specs=[hbm] * K, out_specs=[hbm] * K,
        out_shape=[jax.ShapeDtypeStruct(g.shape[:2] + g.shape[3:], g.dtype) for g in gs],
        scratch_shapes=[pltpu.SemaphoreType.DMA((K,)), pltpu.SemaphoreType.DMA((K,))],
    )(*gs)


def _small_copies(x, land, send, recv):
    mx, my, mc = _my_place()
    me = 4 * mx + 2 * my + mc
    out = []
    for k in range(1, N_DEV):
        peer = (1 - mx if k & 4 else mx, 1 - my if k & 2 else my, 1 - mc if k & 1 else mc)
        slot = 4 * peer[0] + 2 * peer[1] + peer[2]
        out.append(tuple(pltpu.make_async_remote_copy(
            src_ref=x, dst_ref=land.at[s], send_sem=send.at[k - 1], recv_sem=recv.at[k - 1],
            device_id=peer, device_id_type=MESH) for s in (me, slot)))
    return out


def small_start(x, after, name):
    def body(x_ref, land_ref, after_ref, send, recv, x_thru, land_thru, token):
        for mine, _ in _small_copies(x_ref, land_ref, send, recv):
            mine.start()
        token[...] = jnp.zeros_like(token)

    land_shape = (N_DEV,) + x.shape
    outs = pl.pallas_call(
        body, name=name,
        in_specs=[_HBM, _HBM, pl.BlockSpec(memory_space=pl.ANY)],
        out_specs=[_SEM, _SEM, _HBM, _HBM, pl.BlockSpec(memory_space=pltpu.VMEM)],
        out_shape=[pltpu.SemaphoreType.DMA((N_DEV - 1,))] * 2 + [pltpu.HBM(x.shape, x.dtype), pltpu.HBM(land_shape, x.dtype),
                                                                 jax.ShapeDtypeStruct((8, 128), F32)],
        input_output_aliases={0: 2, 1: 3},
        compiler_params=pltpu.CompilerParams(has_side_effects=_DATAFLOW),
    )(pltpu.with_memory_space_constraint(x, pltpu.HBM),
      pltpu.with_memory_space_constraint(lax.empty(land_shape, x.dtype), pltpu.HBM), after)
    return outs[:4], outs[4]


def small_wait(handle, after, name):
    send, recv, x, land = handle

    def body(x_ref, land_ref, send_ref, recv_ref, after_ref, x_out, land_out):
        for mine, theirs in _small_copies(x_ref, land_ref, send_ref, recv_ref):
            mine.wait_send()
            theirs.wait_recv()

    return pl.pallas_call(
        body, name=name,
        in_specs=[_HBM, _HBM, _SEM, _SEM, pl.BlockSpec(memory_space=pl.ANY)],
        out_specs=[_HBM, _HBM],
        out_shape=[pltpu.HBM(x.shape, x.dtype), pltpu.HBM(land.shape, land.dtype)],
        input_output_aliases={0: 0, 1: 1},
        compiler_params=pltpu.CompilerParams(has_side_effects=_DATAFLOW),
    )(x, land, send, recv, after)


def _scatter_copies(ps, lands, send, recv):
    mx, my, mc = _my_place()
    cps = []
    for j, (cx, cy) in enumerate(_other_chips(mx, my)):
        for k in range(len(ps)):
            cps.append(pltpu.make_async_remote_copy(
                src_ref=ps[k].at[:, 2 * cx + cy], dst_ref=lands[k].at[j],
                send_sem=send.at[k * 3 + j], recv_sem=recv.at[k * 3 + j],
                device_id=(cx, cy, mc), device_id_type=MESH))
    return cps


def scatter_start(ps, after, name):
    K = len(ps)

    def body(*refs):
        ins, lands = refs[:K], refs[K:2 * K]
        send, recv = refs[2 * K + 1], refs[2 * K + 2]
        for cp in _scatter_copies(ins, lands, send, recv):
            cp.start()
        refs[-1][...] = jnp.zeros_like(refs[-1])

    land_shapes = [(N_CHIP - 1, p.shape[0]) + p.shape[2:] for p in ps]
    outs = pl.pallas_call(
        body, name=name,
        in_specs=[_HBM] * (2 * K) + [pl.BlockSpec(memory_space=pl.ANY)],
        out_specs=[_SEM, _SEM] + [_HBM] * (2 * K) + [pl.BlockSpec(memory_space=pltpu.VMEM)],
        out_shape=([pltpu.SemaphoreType.DMA((3 * K,))] * 2 + [pltpu.HBM(p.shape, p.dtype) for p in ps]
                   + [pltpu.HBM(ls, p.dtype) for ls, p in zip(land_shapes, ps)] + [jax.ShapeDtypeStruct((8, 128), F32)]),
        input_output_aliases={i: 2 + i for i in range(2 * K)},
        compiler_params=pltpu.CompilerParams(has_side_effects=_DATAFLOW),
    )(*[pltpu.with_memory_space_constraint(p, pltpu.HBM) for p in ps],
      *[pltpu.with_memory_space_constraint(lax.empty(ls, p.dtype), pltpu.HBM) for ls, p in zip(land_shapes, ps)],
      after)
    return (outs[0], outs[1], outs[2:2 + K], outs[2 + K:2 + 2 * K]), outs[-1]


def scatter_wait(handle, after, name):
    send, recv, ps, lands = handle
    K = len(ps)

    def body(*refs):
        ins, land_refs = refs[:K], refs[K:2 * K]
        send_ref, recv_ref = refs[2 * K], refs[2 * K + 1]
        for cp in _scatter_copies(ins, land_refs, send_ref, recv_ref):
            cp.wait_send()
            cp.wait_recv()

    outs = pl.pallas_call(
        body, name=name,
        in_specs=[_HBM] * (2 * K) + [_SEM, _SEM, pl.BlockSpec(memory_space=pl.ANY)],
        out_specs=[_HBM] * (2 * K),
        out_shape=[pltpu.HBM(p.shape, p.dtype) for p in ps] + [pltpu.HBM(l.shape, l.dtype) for l in lands],
        input_output_aliases={i: i for i in range(2 * K)},
        compiler_params=pltpu.CompilerParams(has_side_effects=_DATAFLOW),
    )(*ps, *lands, send, recv, after)
    return outs[:K], outs[K:]


def sibling_complete(ss, name):
    K = len(ss)

    def body(*refs):
        ins, outs = refs[:K], refs[K:2 * K]
        send, recv = refs[2 * K:]
        mx, my, mc = _my_place()
        cps = []
        for k in range(K):
            cp = pltpu.make_async_remote_copy(
                src_ref=ins[k].at[:, mc], dst_ref=outs[k].at[:, mc], send_sem=send.at[k], recv_sem=recv.at[k],
                device_id=(mx, my, 1 - mc), device_id_type=MESH)
            cp.start()
            cps.append(cp)
        for k in range(K):
            pltpu.make_async_remote_copy(
                src_ref=ins[k].at[:, mc], dst_ref=outs[k].at[:, 1 - mc], send_sem=send.at[k], recv_sem=recv.at[k],
                device_id=(mx, my, 1 - mc), device_id_type=MESH).wait_recv()
        for cp in cps:
            cp.wait_send()

    hbm = pl.BlockSpec(memory_space=pl.ANY)
    return pl.pallas_call(
        body, name=name,
        in_specs=[hbm] * K, out_specs=[hbm] * K,
        out_shape=[jax.ShapeDtypeStruct(s.shape, s.dtype) for s in ss],
        scratch_shapes=[pltpu.SemaphoreType.DMA((K,)), pltpu.SemaphoreType.DMA((K,))],
        input_output_aliases={k: k for k in range(K)},
    )(*ss)


def _rope_tables(T):
    inv = ROPE_THETA ** (-jnp.arange(0, ATT_DH, 2, dtype=F32) / ATT_DH)
    ang = jnp.arange(T, dtype=F32)[:, None] * inv[None, :]
    ang = jnp.concatenate([ang, ang, ang, ang], axis=-1)
    return jnp.cos(ang), jnp.sin(ang)


def _to_residues(x, d):
    T, W = x.shape
    if d == 1:
        return x.reshape(1, T, W)
    return x.reshape(T // d, d, W).transpose(1, 0, 2)


def _from_residues(x):
    d, L, W = x.shape
    if d == 1:
        return x.reshape(L, W)
    return x.transpose(1, 0, 2).reshape(L * d, W)


def _ffn_fwd(h, ng, i_n, mod, i0, wgT, wuT, wd, tag):
    y = normmod_fwd(h, ng, i_n, mod, i0, i0 + 1, f"normmod_{tag}")
    a, b, s = ffn_up(y, wgT, wuT, f"ffn_up_{tag}")
    hn, o = resid_matmul([s], wd, h, mod, i0 + 2, 0.5, f"ffn_down_{tag}")
    return hn, (h, y, a, b, s, o)


def _ffn_bwd(dh, res, ng, i_n, mod, i0, wgT, wuT, wd, on_grads, tag):
    h, y, a, b, s, o = res
    F = _wrows(wgT)
    do, red_g = gate_bwd(dh, o, mod, i0 + 2, 0.5, f"gate_bwd_{tag}")
    da, db = ffn_bwd_mid(do, wd, a, b, f"ffn_bwd_mid_{tag}")
    gbuf = lax.empty((3, F, h.shape[1]), BF16)
    gbuf = matmul_tn(da, y, gbuf, 0, 0, f"dwg_{tag}")
    gbuf = matmul_tn(db, y, gbuf, 1, 0, f"dwu_{tag}")
    gbuf = matmul_tn(s, do, gbuf, 2, 0, f"dwd_{tag}")
    mod = mod + on_grads([gbuf])
    dh_new, red_n = dy_normbwd([(da, 0, wgT, 0, F), (db, 0, wuT, 0, F)], h, dh, ng, i_n, mod, i0 + 1,
                               f"ffn_bwd_dy_{tag}")
    return dh_new, red_n, red_g


def _mixer_fwd(h, ng, mod, w_inT, w_out, sgu, cos, sin, tag):
    lng, lnb, sw, swt, bcol = sgu
    y = normmod_fwd(h, ng, 1, mod, 3, 4, f"normmod_{tag}")
    proj = matmul_nt(y, w_inT, f"proj_{tag}")
    out_a = sgu_fwd(proj, lng, lnb, sw, bcol, f"sgu_fwd_{tag}")
    qr, kr = rope_fwd(proj, cos, sin, f"rope_fwd_{tag}")
    vv = proj[:, 4 * MIX_HALF:]
    os_, lses, qkv_res = [], [], []
    for d in DILATIONS:
        qd, kd, vd = _to_residues(qr, d), _to_residues(kr, d), _to_residues(vv, d)
        o_d, lse_d = attn_fwd(qd, kd, vd, f"attn_fwd_d{d}_{tag}")
        os_.append(_from_residues(o_d))
        lses.append(_from_residues(lse_d))
        qkv_res.append((qd, kd, vd))
    out_b, lse = attn_combine(os_, lses, f"attn_combine_{tag}")
    hn, om = resid_matmul([out_a, out_b], w_out, h, mod, 5, 1.0, f"mix_out_{tag}")
    return hn, (h, y, proj, out_a, out_b, lse, qkv_res, om)


def _mixer_bwd(dh, res, ng, mod, w_inT, w_out, sgu, cos, sin, on_grads, tag):
    lng, lnb, sw, swt, bcol = sgu
    h, y, proj, out_a, out_b, lse, qkv_res, om = res
    D = h.shape[1]
    dom, red_g = gate_bwd(dh, om, mod, 5, 1.0, f"gate_bwd_{tag}")
    dmixed = matmul_nt(dom, w_out, f"dmixed_{tag}")
    woutbuf = lax.empty((1, 2 * MIX_HALF, D), BF16)
    woutbuf = matmul_tn(out_a, dom, woutbuf, 0, 0, f"dwout_a_{tag}", tmo_cap=MIX_HALF)
    woutbuf = matmul_tn(out_b, dom, woutbuf, 0, MIX_HALF, f"dwout_b_{tag}", tmo_cap=MIX_HALF)
    d_uv, d_sw, d_svec = sgu_bwd(proj, dmixed, lng, lnb, sw, swt, bcol, f"sgu_bwd_{tag}")
    dob = dmixed[:, MIX_HALF:]
    dqs, dks, dvs = [], [], []
    for d, (qd, kd, vd) in zip(DILATIONS, qkv_res):
        dq, dk, dv = attn_bwd(qd, kd, vd, _to_residues(dob, d), _to_residues(out_b, d), _to_residues(lse, d),
                              f"attn_bwd_d{d}_{tag}")
        dqs.append(_from_residues(dq))
        dks.append(_from_residues(dk))
        dvs.append(_from_residues(dv))
    d_qkv = rope_bwd(dqs, dks, dvs, cos, sin, f"rope_bwd_{tag}")
    winbuf = lax.empty((1, 5 * MIX_HALF, D), BF16)
    winbuf = matmul_tn(d_uv, y, winbuf, 0, 0, f"dwin_uv_{tag}", tmo_cap=MIX_HALF)
    winbuf = matmul_tn(d_qkv, y, winbuf, 0, 2 * MIX_HALF, f"dwin_qkv_{tag}", tmo_cap=MIX_HALF)
    mod = mod + on_grads([winbuf, woutbuf])
    pairs = ([(d_uv, p, w_inT, p, MIX_HALF) for p in range(2)]
             + [(d_qkv, p, w_inT, 2 + p, MIX_HALF) for p in range(3)])
    dh_new, red_n = dy_normbwd(pairs, h, dh, ng, 1, mod, 4, f"mix_bwd_dy_{tag}")
    return dh_new, d_sw, d_svec, red_n, red_g


def _local_step(x, tgt, mods, ngs, get_w, sgus, gf, on_block_grads, on_layer_small):
    T, D = x.shape
    cos, sin = _rope_tables(T)
    h = x
    saved, weights = [], []
    for l in range(2):
        wf1 = get_w(l, "f1", h)
        h, r1 = _ffn_fwd(h, ngs[l], 0, mods[l], 0, (wf1, (0,)), (wf1, (1,)), (wf1, (2,)), f"l{l}f1")
        w_inT, w_out = get_w(l, "mx", h)
        h, r2 = _mixer_fwd(h, ngs[l], mods[l], (w_inT, (0,)), (w_out, (0,)), sgus[l], cos, sin, f"l{l}mx")
        wf2 = get_w(l, "f2", h)
        h, r3 = _ffn_fwd(h, ngs[l], 2, mods[l], 6, (wf2, (0,)), (wf2, (1,)), (wf2, (2,)), f"l{l}f2")
        saved.append((r1, r2, r3))
        weights.append((wf1, w_inT, w_out, wf2))
    dh, red_final = final_loss_bwd(h, gf, tgt, "final_loss_bwd")
    for l in (1, 0):
        r1, r2, r3 = saved[l]
        wf1, w_inT, w_out, wf2 = weights[l]

        def on(blk, l=l):
            return lambda arrays: on_block_grads(l, blk, arrays)

        dh, rn3, rg3 = _ffn_bwd(dh, r3, ngs[l], 2, mods[l], 6, (wf2, (0,)), (wf2, (1,)), (wf2, (2,)), on("f2"),
                                f"l{l}f2")
        dh, d_sw, d_svec, rn2, rg2 = _mixer_bwd(dh, r2, ngs[l], mods[l], (w_inT, (0,)), (w_out, (0,)), sgus[l],
                                                cos, sin, on("mx"), f"l{l}mx")
        dh, rn1, rg1 = _ffn_bwd(dh, r1, ngs[l], 0, mods[l], 0, (wf1, (0,)), (wf1, (1,)), (wf1, (2,)), on("f1"),
                                f"l{l}f1")
        mods = mods + on_layer_small(l, dict(sgu_w=d_sw, sgu_vec=d_svec, red_n=(rn1, rn2, rn3), red_g=(rg1, rg2, rg3)),
                                     red_final if l == 0 else None)
    return dh


def _adam_out(w, g, m, v, name):
    shp = w.shape
    two_d = (-1, shp[-1])
    d, mn, vn = adamw(w.reshape(two_d), g.reshape(two_d), m.reshape(two_d), v.reshape(two_d), name)
    return g, d.reshape(shp), mn.reshape(shp), vn.reshape(shp)


def kernel(x, c, ada_w, ada_b, norm_g, ffn1_wg, ffn1_wu, ffn1_wd, ffn2_wg, ffn2_wu, ffn2_wd, w_in, sgu_ln_g, sgu_ln_b, sgu_w, sgu_b, w_out, final_g, loss_target, m_ada_w, m_ada_b, m_norm_g, m_ffn1_wg, m_ffn1_wu, m_ffn1_wd, m_ffn2_wg, m_ffn2_wu, m_ffn2_wd, m_w_in, m_sgu_ln_g, m_sgu_ln_b, m_sgu_w, m_sgu_b, m_w_out, m_final_g, v_ada_w, v_ada_b, v_norm_g, v_ffn1_wg, v_ffn1_wu, v_ffn1_wd, v_ffn2_wg, v_ffn2_wu, v_ffn2_wd, v_w_in, v_sgu_ln_g, v_sgu_ln_b, v_sgu_w, v_sgu_b, v_w_out, v_final_g):
    T, D = x.shape[1], x.shape[2]
    NL = ada_w.shape[0]
    mx, my, mc = _my_place()
    me = 4 * mx + 2 * my + mc
    ci = 2 * mx + my
    c_idx = jnp.reshape(mc, (1,)).astype(jnp.int32)
    place = jnp.stack([ci, mc]).astype(jnp.int32)

    ngw = norm_g.shape[2]
    small_in = jnp.concatenate([jnp.pad(c, ((0, 7), (0, 0))),
                                jnp.pad(norm_g.reshape(NL * 3, ngw), ((0, 8 - NL * 3), (0, D - ngw)))], axis=0)
    small_all, _ = gather_small(small_in, "gather_c_normg")
    c_all = small_all[:, 0, :]
    ng_parts = small_all[0::2, 8:8 + NL * 3, :ngw]
    ngs = jnp.transpose(ng_parts, (1, 0, 2)).reshape(NL, 3, N_CHIP * ngw)

    nmod = ada_w.shape[2]
    ada_b_mine = lax.dynamic_slice_in_dim(ada_b, ci * nmod, nmod, axis=1).reshape(NL, 1, nmod)
    mod_part = ada_fwd(c_all, ada_w, ada_b_mine, "ada_fwd")
    mod_all, _ = gather_small(mod_part.reshape(NL * N_DEV, nmod), "gather_mod")
    mod_rows = lax.dynamic_index_in_dim(mod_all.reshape(N_DEV, NL, N_DEV, nmod), me, axis=2, keepdims=False)
    mods = jnp.transpose(mod_rows[0::2], (1, 0, 2)).reshape(NL, N_ADA, D)

    sgus = []
    for l in range(NL):
        sgus.append((sgu_ln_g[l].reshape(1, MIX_HALF), sgu_ln_b[l].reshape(1, MIX_HALF), sgu_w[l],
                     jnp.swapaxes(sgu_w[l], 1, 2), jnp.transpose(sgu_b[l])))

    def halves(a):
        n, r, _ = a.shape
        return a.reshape(n, 2, r // 2, D)

    Fs = ffn1_wd.shape[1]
    groups = []
    for l in range(NL):
        f1 = jnp.stack([ffn1_wg[l].T, ffn1_wu[l].T, ffn1_wd[l]], axis=0).astype(BF16)
        f2 = jnp.stack([ffn2_wg[l].T, ffn2_wu[l].T, ffn2_wd[l]], axis=0).astype(BF16)
        groups += [[halves(f1)], [halves(w_in[l].T.astype(BF16)[None]), halves(w_out[l].astype(BF16)[None])],
                   [halves(f2)]]
    handles, token = gather_start(groups, mods, "gather_start")
    mods = mods + token[0, 0]
    block_no = {"f1": 0, "mx": 1, "f2": 2}

    def get_w(l, blk, after):
        g = 3 * l + block_no[blk]
        full = gather_wait(handles[g], after, f"gather_wait_l{l}{blk}")
        full = [a.reshape(a.shape[0], N_CHIP * 2 * a.shape[3], D) for a in full]
        return full[0] if blk != "mx" else tuple(full)

    def split(a):
        n, r4, _ = a.shape
        return a.reshape(n, N_CHIP, 2, r4 // N_CHIP // 2, D)

    pending, small_pending, small_tokens = {}, {}, {}

    def on_block_grads(l, blk, bufs):
        tag = f"l{l}{blk}"
        parts = [split(g) for g in bufs]
        lands = sibling_send_half(parts, f"rs_sibling_{tag}")
        psums = [sum_halves(g, ld, c_idx, f"rs_sum_halves_{tag}_{i}") for i, (g, ld) in enumerate(zip(parts, lands))]
        handle, tok = scatter_start(psums, psums[0], f"rs_chips_start_{tag}")
        pending[(l, blk)] = handle
        return tok[0, 0]

    def block_finish(l, blk, after):
        tag = f"l{l}{blk}"
        psums, lands2 = scatter_wait(pending.pop((l, blk)), after, f"rs_chips_wait_{tag}")
        ssums = [sum_chips(p, ld, place, f"rs_sum_chips_{tag}_{i}") for i, (p, ld) in enumerate(zip(psums, lands2))]
        return [f.reshape(f.shape[0], -1, D) for f in sibling_complete(ssums, f"rs_complete_{tag}")]

    def on_layer_small(l, grads, red_final):
        blocks = list(grads["red_n"]) + list(grads["red_g"])
        blocks.append(jnp.pad(grads["sgu_vec"], ((0, 0), (0, D - MIX_HALF))))
        blocks.append(grads["sgu_w"].reshape(-1, D))
        if red_final is not None:
            blocks.append(red_final)
        xs = jnp.concatenate(blocks, axis=0)
        small_pending[l], small_tokens[l] = small_start(xs, xs, f"small_start_l{l}")
        return small_tokens[l][0, 0]

    grad_x = _local_step(x[0], loss_target[0], mods, ngs, get_w, sgus, final_g.reshape(1, D),
                         on_block_grads, on_layer_small)

    adam_state = {}

    def adam_big(nm, l, g, w, m, v):
        adam_state[nm] = adamw_layer(w, g, m, v, l, adam_state.get(nm), f"adamw_{nm}_l{l}")

    def adam_block(l, blk, fin):
        if blk == "mx":
            adam_big("w_in", l, fin[0][0].T, w_in, m_w_in, v_w_in)
            adam_big("w_out", l, fin[1][0], w_out, m_w_out, v_w_out)
        else:
            ws = ((ffn1_wg, m_ffn1_wg, v_ffn1_wg), (ffn1_wu, m_ffn1_wu, v_ffn1_wu), (ffn1_wd, m_ffn1_wd, v_ffn1_wd)) \
                if blk == "f1" else \
                ((ffn2_wg, m_ffn2_wg, v_ffn2_wg), (ffn2_wu, m_ffn2_wu, v_ffn2_wu), (ffn2_wd, m_ffn2_wd, v_ffn2_wd))
            pre = "ffn1" if blk == "f1" else "ffn2"
            for k, (nm, tr) in enumerate((("wg", True), ("wu", True), ("wd", False))):
                adam_big(f"{pre}_{nm}", l, fin[0][k].T if tr else fin[0][k], *ws[k])

    done_order = [(l, blk) for l in range(NL - 1, -1, -1) for blk in ("f2", "mx", "f1")]
    for l, blk in done_order[:-1]:
        adam_block(l, blk, block_finish(l, blk, small_tokens[0]))
    last_big = adam_state["w_out"][1]

    small_sum, small_all = [], []
    for l in range(NL):
        xs, land = small_wait(small_pending[l], last_big, f"small_wait_l{l}")
        full = lax.dynamic_update_slice(land, xs[None], (me, 0, 0))
        small_all.append(full)
        small_sum.append(sum_slots(full, f"small_sum_l{l}"))
    offs = [8 * i for i in range(8)]
    off_final = offs[7] + SGU_HEADS * ATT_BLOCK * HEAD_LANES // D
    loss = small_sum[0][off_final + 1, 0]
    g_final_g = small_sum[0][off_final, :]
    g_norm_g, g_ada_b, g_lng, g_lnb, g_sb, g_sw, dmod_all = [], [], [], [], [], [], []
    for l in range(NL):
        rn = [small_sum[l][offs[i]:offs[i] + 8] for i in range(3)]
        rg = [small_sum[l][offs[3 + i]:offs[3 + i] + 8] for i in range(3)]
        g_norm_g.append(jnp.stack([rn[i][2] for i in range(3)], axis=0))
        g_ada_b.append(jnp.concatenate([jnp.stack([rn[i][0], rn[i][1], rg[i][0]], axis=0) for i in range(3)],
                                       axis=0).reshape(N_ADA * D))
        sv = small_sum[l][offs[6]:offs[6] + 8, :MIX_HALF]
        g_lng.append(sv[0].reshape(SGU_HEADS, HEAD_LANES))
        g_lnb.append(sv[1].reshape(SGU_HEADS, HEAD_LANES))
        g_sb.append(sv[2].reshape(SGU_HEADS, ATT_BLOCK))
        g_sw.append(small_sum[l][offs[7]:off_final].reshape(sgu_w.shape[1:]))
        rows = []
        for i in range(3):
            an = small_all[l][:, offs[i]:offs[i] + 2]
            ag = small_all[l][:, offs[3 + i]:offs[3 + i] + 1]
            rows += [an[:, 0], an[:, 1], ag[:, 0]]
        dmod_all.append(jnp.stack(rows, axis=1).reshape(N_DEV, N_ADA * D))
    dmod_all = jnp.stack(dmod_all, axis=0)
    dmod_mine = lax.dynamic_slice_in_dim(dmod_all, ci * nmod, nmod, axis=2)
    g_ada_w = ada_bwd(jnp.transpose(c_all), dmod_mine, "ada_bwd")
    g_ada_b = jnp.stack(g_ada_b, axis=0)
    g_norm_g_full = jnp.stack(g_norm_g, axis=0)
    g_norm_g_mine = lax.dynamic_slice_in_dim(g_norm_g_full, ci * ngw, ngw, axis=2)

    small_params = [
        ("ada_w", ada_w, g_ada_w, m_ada_w, v_ada_w),
        ("ada_b", ada_b, g_ada_b, m_ada_b, v_ada_b),
        ("norm_g", norm_g, g_norm_g_mine, m_norm_g, v_norm_g),
        ("sgu_ln_g", sgu_ln_g, jnp.stack(g_lng, axis=0), m_sgu_ln_g, v_sgu_ln_g),
        ("sgu_ln_b", sgu_ln_b, jnp.stack(g_lnb, axis=0), m_sgu_ln_b, v_sgu_ln_b),
        ("sgu_w", sgu_w, jnp.stack(g_sw, axis=0), m_sgu_w, v_sgu_w),
        ("sgu_b", sgu_b, jnp.stack(g_sb, axis=0), m_sgu_b, v_sgu_b),
        ("final_g", final_g.reshape(1, D), g_final_g.reshape(1, D), m_final_g.reshape(1, D), v_final_g.reshape(1, D)),
    ]
    for nm, w, g, m, v in small_params:
        res = _adam_out(w, g, m, v, f"adamw_{nm}")
        adam_state[nm] = tuple(t.reshape(D) for t in res) if nm == "final_g" else res

    l, blk = done_order[-1]
    adam_block(l, blk, block_finish(l, blk, adam_state["ada_w"][1]))

    names = ["ada_w", "ada_b", "norm_g", "ffn1_wg", "ffn1_wu", "ffn1_wd", "ffn2_wg", "ffn2_wu", "ffn2_wd", "w_in",
             "sgu_ln_g", "sgu_ln_b", "sgu_w", "sgu_b", "w_out", "final_g"]
    return (loss, grad_x[None], *[adam_state[nm][i] for i in range(4) for nm in names])
```

```python
import math

import jax
import jax.numpy as jnp
from jax import lax
from jax.experimental import pallas as pl
from jax.experimental.pallas import tpu as pltpu

F32 = jnp.float32
BF16 = jnp.bfloat16
EPS = 1e-6
SGU_HEADS = 4
HEAD_LANES = 128
ATT_DH = 64
ATT_BLOCK = 128
MIX_HALF = SGU_HEADS * HEAD_LANES
DILATIONS = (1, 4, 16)
ROPE_THETA = 10000.0
N_ADA = 9
ADAM_LR, ADAM_B1, ADAM_B2, ADAM_EPS, ADAM_WD, ADAM_STEP = 0.001, 0.9, 0.999, 1e-08, 0.01, 10
NEG = -1e30
V7X_VMEM_BYTES = 64 * 1024 * 1024
VMEM_LIMIT = V7X_VMEM_BYTES * 7 // 8
MESH = pl.DeviceIdType.MESH
N_DEV = 8
N_CHIP = 4


def _tile(n, cap, mult):
    if n <= cap:
        return n
    t = (cap // mult) * mult
    while t >= mult:
        if n % t == 0:
            return t
        t -= mult
    raise ValueError((n, cap, mult))


def _params(dims=None):
    return pltpu.CompilerParams(dimension_semantics=dims, vmem_limit_bytes=VMEM_LIMIT)


def _wspec(w, rows, idx):
    arr, lead = w
    return pl.BlockSpec((None,) * len(lead) + (rows, arr.shape[-1]), lambda *g: tuple(lead) + (idx(*g), 0))


def _wrows(w):
    return w[0].shape[-2]


def _nt(a, b):
    return lax.dot_general(a, b, (((1,), (1,)), ((), ())), preferred_element_type=F32)


def _tn(a, b):
    return lax.dot_general(a, b, (((0,), (0,)), ((), ())), preferred_element_type=F32)


def _nn(a, b):
    return jnp.dot(a, b, preferred_element_type=F32)


def _sigmoid(x):
    return 1.0 / (1.0 + jnp.exp(-x))


_GELU_K = math.sqrt(2.0 / math.pi)
_GELU_C = 0.044715


def _gelu(x):
    t = jnp.tanh(_GELU_K * (x + _GELU_C * x * x * x))
    return 0.5 * x * (1.0 + t)


def _gelu_and_grad(x):
    x2 = x * x
    t = jnp.tanh(_GELU_K * (x + _GELU_C * x * x2))
    g = 0.5 * x * (1.0 + t)
    dg = 0.5 * (1.0 + t) + 0.5 * x * (1.0 - t * t) * (_GELU_K * (1.0 + 3.0 * _GELU_C * x2))
    return g, dg


def normmod_fwd(h, ng, i_n, mod, i_sh, i_sc, name):
    T, D = h.shape
    tm = _tile(T, 512, 8)

    def body(h_ref, ng_ref, mod_ref, y_ref):
        x = h_ref[...]
        r = lax.rsqrt(jnp.mean(x * x, axis=-1, keepdims=True) + EPS)
        y = (x * r) * ng_ref[i_n:i_n + 1, :]
        y_ref[...] = (y * (1.0 + mod_ref[i_sc:i_sc + 1, :]) + mod_ref[i_sh:i_sh + 1, :]).astype(BF16)

    return pl.pallas_call(
        body, name=name, grid=(T // tm,),
        in_specs=[pl.BlockSpec((tm, D), lambda i: (i, 0)),
                  pl.BlockSpec(ng.shape, lambda i: (0, 0)),
                  pl.BlockSpec(mod.shape, lambda i: (0, 0))],
        out_specs=pl.BlockSpec((tm, D), lambda i: (i, 0)),
        out_shape=jax.ShapeDtypeStruct((T, D), BF16),
        compiler_params=_params(("parallel",)),
    )(h, ng, mod)


def ffn_up(y, wgT, wuT, name):
    T, D = y.shape
    F = _wrows(wgT)
    tm = _tile(T, 512, 16)
    tf = _tile(F, 1408, 128)

    def body(y_ref, wg_ref, wu_ref, a_ref, b_ref, s_ref):
        yv = y_ref[...]
        a = _nt(yv, wg_ref[...])
        b = _nt(yv, wu_ref[...])
        a_ref[...] = a.astype(BF16)
        b_ref[...] = b.astype(BF16)
        s_ref[...] = (a * _sigmoid(a) * b).astype(BF16)

    act = jax.ShapeDtypeStruct((T, F), BF16)
    return pl.pallas_call(
        body, name=name, grid=(F // tf, T // tm),
        in_specs=[pl.BlockSpec((tm, D), lambda j, i: (i, 0)),
                  _wspec(wgT, tf, lambda j, i: j),
                  _wspec(wuT, tf, lambda j, i: j)],
        out_specs=[pl.BlockSpec((tm, tf), lambda j, i: (i, j))] * 3,
        out_shape=[act, act, act],
        compiler_params=_params(("parallel", "parallel")),
    )(y, wgT[0], wuT[0])


def resid_matmul(xs, w, h, mod, i_g, coef, name):
    T, D = h.shape
    kb = xs[0].shape[1]
    assert all(x.shape == (T, kb) for x in xs) and _wrows(w) == kb * len(xs)
    tm = _tile(T, 512, 16)
    nx = len(xs)

    def body(*refs):
        x_refs, w_refs = refs[:nx], refs[nx:2 * nx]
        h_ref, mod_ref, hn_ref, o_ref = refs[2 * nx:]
        o = _nn(x_refs[0][...], w_refs[0][...])
        for xr, wr in zip(x_refs[1:], w_refs[1:]):
            o = o + _nn(xr[...], wr[...])
        o_ref[...] = o.astype(BF16)
        hn_ref[...] = h_ref[...] + (coef * mod_ref[i_g:i_g + 1, :]) * o

    return pl.pallas_call(
        body, name=name, grid=(T // tm,),
        in_specs=([pl.BlockSpec((tm, kb), lambda i: (i, 0))] * nx
                  + [_wspec(w, kb, lambda i, p=p: p) for p in range(nx)]
                  + [pl.BlockSpec((tm, D), lambda i: (i, 0)),
                     pl.BlockSpec(mod.shape, lambda i: (0, 0))]),
        out_specs=[pl.BlockSpec((tm, D), lambda i: (i, 0))] * 2,
        out_shape=[jax.ShapeDtypeStruct((T, D), F32), jax.ShapeDtypeStruct((T, D), BF16)],
        compiler_params=_params(("parallel",)),
    )(*xs, *([w[0]] * nx), h, mod)


def gate_bwd(dh, o, mod, i_g, coef, name):
    T, D = dh.shape
    tm = _tile(T, 512, 16)

    def body(dh_ref, o_ref, mod_ref, do_ref, red_ref):
        d = dh_ref[...]
        do_ref[...] = (d * (coef * mod_ref[i_g:i_g + 1, :])).astype(BF16)

        @pl.when(pl.program_id(0) == 0)
        def _():
            red_ref[...] = jnp.zeros_like(red_ref)

        red_ref[0:1, :] += coef * jnp.sum(d * o_ref[...].astype(F32), axis=0, keepdims=True)

    return pl.pallas_call(
        body, name=name, grid=(T // tm,),
        in_specs=[pl.BlockSpec((tm, D), lambda i: (i, 0)),
                  pl.BlockSpec((tm, D), lambda i: (i, 0)),
                  pl.BlockSpec(mod.shape, lambda i: (0, 0))],
        out_specs=[pl.BlockSpec((tm, D), lambda i: (i, 0)), pl.BlockSpec((8, D), lambda i: (0, 0))],
        out_shape=[jax.ShapeDtypeStruct((T, D), BF16), jax.ShapeDtypeStruct((8, D), F32)],
        compiler_params=_params(("arbitrary",)),
    )(dh, o, mod)


def ffn_bwd_mid(do, wd, a, b, name):
    T, D = do.shape
    F = _wrows(wd)
    tm = _tile(T, 512, 16)
    tf = _tile(F, 1408, 128)

    def body(do_ref, wd_ref, a_ref, b_ref, da_ref, db_ref):
        ds = _nt(do_ref[...], wd_ref[...])
        av = a_ref[...].astype(F32)
        bv = b_ref[...].astype(F32)
        sig = _sigmoid(av)
        da_ref[...] = (ds * bv * (sig * (1.0 + av * (1.0 - sig)))).astype(BF16)
        db_ref[...] = (ds * (av * sig)).astype(BF16)

    act = jax.ShapeDtypeStruct((T, F), BF16)
    return pl.pallas_call(
        body, name=name, grid=(F // tf, T // tm),
        in_specs=[pl.BlockSpec((tm, D), lambda j, i: (i, 0)),
                  _wspec(wd, tf, lambda j, i: j),
                  pl.BlockSpec((tm, tf), lambda j, i: (i, j)),
                  pl.BlockSpec((tm, tf), lambda j, i: (i, j))],
        out_specs=[pl.BlockSpec((tm, tf), lambda j, i: (i, j))] * 2,
        out_shape=[act, act],
        compiler_params=_params(("parallel", "parallel")),
    )(do, wd[0], a, b)


def dy_normbwd(pairs, h, dhp, ng, i_n, mod, i_sc, name):
    T, D = h.shape
    tm = _tile(T, 256, 16)
    npair = len(pairs)

    def body(*refs):
        x_refs, w_refs = refs[:npair], refs[npair:2 * npair]
        h_ref, dhp_ref, ng_ref, mod_ref, dh_ref, red_ref = refs[2 * npair:]
        dy = _nn(x_refs[0][...], w_refs[0][...])
        for xr, wr in zip(x_refs[1:], w_refs[1:]):
            dy = dy + _nn(xr[...], wr[...])
        x = h_ref[...]
        r = lax.rsqrt(jnp.mean(x * x, axis=-1, keepdims=True) + EPS)
        n = x * r
        gn = ng_ref[i_n:i_n + 1, :]
        dnh = dy * (1.0 + mod_ref[i_sc:i_sc + 1, :])

        @pl.when(pl.program_id(0) == 0)
        def _():
            red_ref[...] = jnp.zeros_like(red_ref)

        red_ref[0:1, :] += jnp.sum(dy, axis=0, keepdims=True)
        red_ref[1:2, :] += jnp.sum(dy * (n * gn), axis=0, keepdims=True)
        red_ref[2:3, :] += jnp.sum(dnh * n, axis=0, keepdims=True)
        dn = dnh * gn
        dh_ref[...] = dhp_ref[...] + r * (dn - n * jnp.mean(dn * n, axis=-1, keepdims=True))

    in_specs = ([pl.BlockSpec((tm, kb), lambda i, c=c: (i, c)) for (_, c, _, _, kb) in pairs]
                + [_wspec(w, kb, lambda i, r=r: r) for (_, _, w, r, kb) in pairs]
                + [pl.BlockSpec((tm, D), lambda i: (i, 0)),
                   pl.BlockSpec((tm, D), lambda i: (i, 0)),
                   pl.BlockSpec(ng.shape, lambda i: (0, 0)),
                   pl.BlockSpec(mod.shape, lambda i: (0, 0))])
    return pl.pallas_call(
        body, name=name, grid=(T // tm,), in_specs=in_specs,
        out_specs=[pl.BlockSpec((tm, D), lambda i: (i, 0)), pl.BlockSpec((8, D), lambda i: (0, 0))],
        out_shape=[jax.ShapeDtypeStruct((T, D), F32), jax.ShapeDtypeStruct((8, D), F32)],
        compiler_params=_params(("arbitrary",)),
    )(*[p[0] for p in pairs], *[p[2][0] for p in pairs], h, dhp, ng, mod)


def matmul_tn(a, b, buf, slot, row0, name, tmo_cap=1408):
    T, N = b.shape
    ma = a.shape[1]
    tmo = _tile(ma, tmo_cap, 128)
    assert row0 % tmo == 0
    nmo = ma // tmo
    tk = _tile(T, 512, 16)
    nk = T // tk

    def body(a_ref, b_ref, buf_ref, o_ref, acc_ref):
        k = pl.program_id(1)

        @pl.when(k == 0)
        def _():
            acc_ref[...] = jnp.zeros_like(acc_ref)

        acc_ref[...] += _tn(a_ref[...], b_ref[...])

        @pl.when(k == nk - 1)
        def _():
            o_ref[...] = acc_ref[...].astype(BF16)

    return pl.pallas_call(
        body, name=name, grid=(nmo, nk),
        in_specs=[pl.BlockSpec((tk, tmo), lambda j, k: (k, j)),
                  pl.BlockSpec((tk, N), lambda j, k: (k, 0)),
                  pl.BlockSpec(memory_space=pl.ANY)],
        out_specs=pl.BlockSpec((None, tmo, N), lambda j, k: (slot, row0 // tmo + j, 0)),
        out_shape=jax.ShapeDtypeStruct(buf.shape, BF16),
        scratch_shapes=[pltpu.VMEM((tmo, N), F32)],
        input_output_aliases={2: 0},
        compiler_params=_params(("parallel", "arbitrary")),
    )(a, b, buf)


def matmul_nt(x, w, name):
    T, K = x.shape
    N = _wrows(w)
    tm = _tile(T, 512, 16)
    tn = _tile(N, 1280, 128)

    def body(x_ref, w_ref, o_ref):
        o_ref[...] = _nt(x_ref[...], w_ref[...]).astype(BF16)

    return pl.pallas_call(
        body, name=name, grid=(N // tn, T // tm),
        in_specs=[pl.BlockSpec((tm, K), lambda j, i: (i, 0)), _wspec(w, tn, lambda j, i: j)],
        out_specs=pl.BlockSpec((tm, tn), lambda j, i: (i, j)),
        out_shape=jax.ShapeDtypeStruct((T, N), BF16),
        compiler_params=_params(("parallel", "parallel")),
    )(x, w[0])


def _sgu_head_fwd(u, v, lng, lnb):
    gu, dgu = _gelu_and_grad(u)
    gv, dgv = _gelu_and_grad(v)
    mu = jnp.mean(gv, axis=-1, keepdims=True)
    xc = gv - mu
    rstd = lax.rsqrt(jnp.mean(xc * xc, axis=-1, keepdims=True) + EPS)
    xhat = xc * rstd
    vn = xhat * lng + lnb
    return gu, dgu, dgv, rstd, xhat, vn


def _tril_mask():
    r = lax.broadcasted_iota(jnp.int32, (ATT_BLOCK, ATT_BLOCK), 0)
    c = lax.broadcasted_iota(jnp.int32, (ATT_BLOCK, ATT_BLOCK), 1)
    return c <= r


def _triu_mask():
    r = lax.broadcasted_iota(jnp.int32, (ATT_BLOCK, ATT_BLOCK), 0)
    c = lax.broadcasted_iota(jnp.int32, (ATT_BLOCK, ATT_BLOCK), 1)
    return r <= c


def sgu_fwd(proj, lng, lnb, w, bcol, name):
    T = proj.shape[0]
    tm = _tile(T, 512, 128)
    nch = tm // ATT_BLOCK

    def body(u_ref, v_ref, lng_ref, lnb_ref, w_ref, b_ref, o_ref):
        tril = _tril_mask()
        for hd in range(SGU_HEADS):
            sl = slice(hd * HEAD_LANES, (hd + 1) * HEAD_LANES)
            u = u_ref[:, sl].astype(F32)
            v = v_ref[:, sl].astype(F32)
            gu, _, _, _, _, vn = _sgu_head_fwd(u, v, lng_ref[:, sl], lnb_ref[:, sl])
            wm = jnp.where(tril, w_ref[hd], 0.0).astype(BF16)
            vnb = vn.astype(BF16)
            bc = b_ref[:, hd:hd + 1]
            for ch in range(nch):
                rs = slice(ch * ATT_BLOCK, (ch + 1) * ATT_BLOCK)
                z = _nn(wm, vnb[rs, :]) + bc
                o_ref[rs, sl] = (gu[rs, :] * z).astype(BF16)

    return pl.pallas_call(
        body, name=name, grid=(T // tm,),
        in_specs=[pl.BlockSpec((tm, MIX_HALF), lambda i: (i, 0)),
                  pl.BlockSpec((tm, MIX_HALF), lambda i: (i, 1)),
                  pl.BlockSpec((1, MIX_HALF), lambda i: (0, 0)),
                  pl.BlockSpec((1, MIX_HALF), lambda i: (0, 0)),
                  pl.BlockSpec(w.shape, lambda i: (0, 0, 0)),
                  pl.BlockSpec(bcol.shape, lambda i: (0, 0))],
        out_specs=pl.BlockSpec((tm, MIX_HALF), lambda i: (i, 0)),
        out_shape=jax.ShapeDtypeStruct((T, MIX_HALF), BF16),
        compiler_params=_params(("parallel",)),
    )(proj, proj, lng, lnb, w, bcol)


def sgu_bwd(proj, dmixed, lng, lnb, w, wt, bcol, name):
    T = proj.shape[0]
    tm = _tile(T, 512, 128)
    nch = tm // ATT_BLOCK
    nsteps = T // tm

    def body(u_ref, v_ref, g_ref, lng_ref, lnb_ref, w_ref, wt_ref, b_ref, duv_ref, dw_ref, dvec_ref, bacc_ref):
        step = pl.program_id(0)

        @pl.when(step == 0)
        def _():
            dw_ref[...] = jnp.zeros_like(dw_ref)
            dvec_ref[...] = jnp.zeros_like(dvec_ref)
            bacc_ref[...] = jnp.zeros_like(bacc_ref)

        tril = _tril_mask()
        triu = _triu_mask()
        for hd in range(SGU_HEADS):
            sl = slice(hd * HEAD_LANES, (hd + 1) * HEAD_LANES)
            u = u_ref[:, sl].astype(F32)
            v = v_ref[:, sl].astype(F32)
            lng_h = lng_ref[:, sl]
            gu, dgu, dgv, rstd, xhat, vn = _sgu_head_fwd(u, v, lng_h, lnb_ref[:, sl])
            wm = jnp.where(tril, w_ref[hd], 0.0).astype(BF16)
            wmt = jnp.where(triu, wt_ref[hd], 0.0).astype(BF16)
            vnb = vn.astype(BF16)
            bc = b_ref[:, hd:hd + 1]
            g = g_ref[:, sl].astype(F32)
            dw_acc = jnp.zeros((ATT_BLOCK, ATT_BLOCK), F32)
            b_acc = jnp.zeros((ATT_BLOCK, HEAD_LANES), F32)
            dvn_parts = []
            for ch in range(nch):
                rs = slice(ch * ATT_BLOCK, (ch + 1) * ATT_BLOCK)
                z = _nn(wm, vnb[rs, :]) + bc
                duv_ref[rs, sl] = (g[rs, :] * z * dgu[rs, :]).astype(BF16)
                dz = g[rs, :] * gu[rs, :]
                dzb = dz.astype(BF16)
                dvn_parts.append(_nn(wmt, dzb))
                dw_acc = dw_acc + _nt(dzb, vnb[rs, :])
                b_acc = b_acc + dz
            dvn = jnp.concatenate(dvn_parts, axis=0)
            dw_ref[hd] += jnp.where(tril, dw_acc, 0.0)
            bacc_ref[hd] += b_acc
            dvec_ref[0:1, sl] += jnp.sum(dvn * xhat, axis=0, keepdims=True)
            dvec_ref[1:2, sl] += jnp.sum(dvn, axis=0, keepdims=True)
            dxh = dvn * lng_h
            dgv_in = rstd * (dxh - jnp.mean(dxh, axis=-1, keepdims=True)
                             - xhat * jnp.mean(dxh * xhat, axis=-1, keepdims=True))
            duv_ref[:, MIX_HALF + hd * HEAD_LANES:MIX_HALF + (hd + 1) * HEAD_LANES] = (dgv_in * dgv).astype(BF16)

        @pl.when(step == nsteps - 1)
        def _():
            for hd in range(SGU_HEADS):
                sl = slice(hd * HEAD_LANES, (hd + 1) * HEAD_LANES)
                dvec_ref[2:3, sl] = jnp.sum(bacc_ref[hd].T, axis=0, keepdims=True)

    return pl.pallas_call(
        body, name=name, grid=(nsteps,),
        in_specs=[pl.BlockSpec((tm, MIX_HALF), lambda i: (i, 0)),
                  pl.BlockSpec((tm, MIX_HALF), lambda i: (i, 1)),
                  pl.BlockSpec((tm, MIX_HALF), lambda i: (i, 0)),
                  pl.BlockSpec((1, MIX_HALF), lambda i: (0, 0)),
                  pl.BlockSpec((1, MIX_HALF), lambda i: (0, 0)),
                  pl.BlockSpec(w.shape, lambda i: (0, 0, 0)),
                  pl.BlockSpec(w.shape, lambda i: (0, 0, 0)),
                  pl.BlockSpec(bcol.shape, lambda i: (0, 0))],
        out_specs=[pl.BlockSpec((tm, 2 * MIX_HALF), lambda i: (i, 0)),
                   pl.BlockSpec(w.shape, lambda i: (0, 0, 0)),
                   pl.BlockSpec((8, MIX_HALF), lambda i: (0, 0))],
        out_shape=[jax.ShapeDtypeStruct((T, 2 * MIX_HALF), BF16),
                   jax.ShapeDtypeStruct(w.shape, F32),
                   jax.ShapeDtypeStruct((8, MIX_HALF), F32)],
        scratch_shapes=[pltpu.VMEM((SGU_HEADS, ATT_BLOCK, HEAD_LANES), F32)],
        compiler_params=_params(("arbitrary",)),
    )(proj, proj, dmixed, lng, lnb, w, wt, bcol)


def _rot_half(t):
    lane = lax.broadcasted_iota(jnp.int32, t.shape, 1)
    first = (lane % ATT_DH) < (ATT_DH // 2)
    return jnp.where(first, -pltpu.roll(t, HEAD_LANES - ATT_DH // 2, 1), pltpu.roll(t, ATT_DH // 2, 1))


LAYOUT_ROWS = 512


def _res_spec(d, tm, W):
    return pl.BlockSpec((d, tm // d, W), lambda i: (0, i, 0))


def _res_shape(d, T, W, dtype):
    return jax.ShapeDtypeStruct((d, T // d, W), dtype)


def _slab_buf(tm, W):
    return pltpu.VMEM((W // HEAD_LANES, tm, HEAD_LANES), F32)


def _lanes(hp):
    return slice(hp * HEAD_LANES, (hp + 1) * HEAD_LANES)


def _to_res(buf, out_ref, d, dtype):
    nslab, tm, _ = buf.shape
    for hp in range(nslab):
        if d == 1:
            out_ref[0, :, _lanes(hp)] = buf[hp].astype(dtype)
        else:
            for r in range(d):
                out_ref[r, :, _lanes(hp)] = buf.at[hp][pl.ds(r, tm // d, stride=d), :].astype(dtype)


def _from_res(in_ref, buf, d):
    nslab, tm, _ = buf.shape
    for hp in range(nslab):
        if d == 1:
            buf[hp] = in_ref[0, :, _lanes(hp)]
        else:
            for r in range(d):
                buf.at[hp][pl.ds(r, tm // d, stride=d), :] = in_ref[r, :, _lanes(hp)]


def rope_fwd(proj, cos, sin, name):
    T = proj.shape[0]
    tm = LAYOUT_ROWS
    scale = 1.0 / math.sqrt(ATT_DH)
    nd = len(DILATIONS)

    def body(q_ref, k_ref, v_ref, cos_ref, sin_ref, *rest):
        outs, buf = rest[:3 * nd], rest[3 * nd]
        c = cos_ref[...]
        s = sin_ref[...]
        for which, src in enumerate((q_ref, k_ref, v_ref)):
            for hp in range(MIX_HALF // HEAD_LANES):
                t = src[:, _lanes(hp)].astype(F32)
                if which == 0:
                    t = scale * (t * c + _rot_half(t) * s)
                elif which == 1:
                    t = t * c + _rot_half(t) * s
                buf[hp] = t
            for di, d in enumerate(DILATIONS):
                _to_res(buf, outs[3 * di + which], d, BF16)

    return pl.pallas_call(
        body, name=name, grid=(T // tm,),
        in_specs=[pl.BlockSpec((tm, MIX_HALF), lambda i: (i, 2)),
                  pl.BlockSpec((tm, MIX_HALF), lambda i: (i, 3)),
                  pl.BlockSpec((tm, MIX_HALF), lambda i: (i, 4)),
                  pl.BlockSpec((tm, HEAD_LANES), lambda i: (i, 0)),
                  pl.BlockSpec((tm, HEAD_LANES), lambda i: (i, 0))],
        out_specs=[_res_spec(d, tm, MIX_HALF) for d in DILATIONS for _ in range(3)],
        out_shape=[_res_shape(d, T, MIX_HALF, BF16) for d in DILATIONS for _ in range(3)],
        scratch_shapes=[_slab_buf(tm, MIX_HALF)],
        compiler_params=_params(("parallel",)),
    )(proj, proj, proj, cos, sin)


def to_residues(x, col, name):
    T = x.shape[0]
    tm = LAYOUT_ROWS

    def body(x_ref, *rest):
        outs, buf = rest[:-1], rest[-1]
        for hp in range(MIX_HALF // HEAD_LANES):
            buf[hp] = x_ref[:, _lanes(hp)].astype(F32)
        for o_ref, d in zip(outs, DILATIONS):
            _to_res(buf, o_ref, d, BF16)

    return pl.pallas_call(
        body, name=name, grid=(T // tm,),
        in_specs=[pl.BlockSpec((tm, MIX_HALF), lambda i: (i, col))],
        out_specs=[_res_spec(d, tm, MIX_HALF) for d in DILATIONS],
        out_shape=[_res_shape(d, T, MIX_HALF, BF16) for d in DILATIONS],
        scratch_shapes=[_slab_buf(tm, MIX_HALF)],
        compiler_params=_params(("parallel",)),
    )(x)


def rope_bwd(dqs, dks, dvs, cos, sin, name):
    T = dqs[0].shape[0] * dqs[0].shape[1]
    tm = LAYOUT_ROWS
    scale = 1.0 / math.sqrt(ATT_DH)
    npat = len(dqs)

    def body(*refs):
        groups = refs[:npat], refs[npat:2 * npat], refs[2 * npat:3 * npat]
        cos_ref, sin_ref, o_ref, buf, acc = refs[3 * npat:]
        c = cos_ref[...]
        s = sin_ref[...]
        for which, g_refs in enumerate(groups):
            _from_res(g_refs[0], acc, DILATIONS[0])
            for g_ref, d in zip(g_refs[1:], DILATIONS[1:]):
                _from_res(g_ref, buf, d)
                acc[...] += buf[...]
            for hp in range(MIX_HALF // HEAD_LANES):
                g = acc[hp]
                if which == 0:
                    g = scale * g
                if which < 2:
                    g = g * c - _rot_half(g * s)
                o_ref[:, which * MIX_HALF + hp * HEAD_LANES:which * MIX_HALF + (hp + 1) * HEAD_LANES] = g.astype(BF16)

    return pl.pallas_call(
        body, name=name, grid=(T // tm,),
        in_specs=([_res_spec(d, tm, MIX_HALF) for _ in range(3) for d in DILATIONS]
                  + [pl.BlockSpec((tm, HEAD_LANES), lambda i: (i, 0))] * 2),
        out_specs=pl.BlockSpec((tm, 3 * MIX_HALF), lambda i: (i, 0)),
        out_shape=jax.ShapeDtypeStruct((T, 3 * MIX_HALF), BF16),
        scratch_shapes=[_slab_buf(tm, MIX_HALF), _slab_buf(tm, MIX_HALF)],
        compiler_params=_params(("parallel",)),
    )(*dqs, *dks, *dvs, cos, sin)


def _band_masks(n):
    r = lax.broadcasted_iota(jnp.int32, (2 * ATT_BLOCK, ATT_BLOCK), 0)
    c = lax.broadcasted_iota(jnp.int32, (2 * ATT_BLOCK, ATT_BLOCK), 1)
    qi = r % ATT_BLOCK
    head = (c < ATT_DH) == (r < ATT_BLOCK)
    return (c >= qi) & (n > 0), c <= qi, head, c[:ATT_BLOCK] < ATT_DH


def _stack_heads(x, head):
    x2 = jnp.concatenate([x, x], axis=0)
    return jnp.where(head, x2, jnp.zeros_like(x2))


def attn_fwd(q, k, v, name):
    d, L, W = q.shape
    nb = L // ATT_BLOCK

    def body(q_ref, kp_ref, kc_ref, vp_ref, vc_ref, o_ref, lse_ref):
        mask_p, mask_c, head, head0 = _band_masks(pl.program_id(1))
        for hp in range(W // HEAD_LANES):
            sl = slice(hp * HEAD_LANES, (hp + 1) * HEAD_LANES)
            kp, kc, vp, vc = kp_ref[0, :, sl], kc_ref[0, :, sl], vp_ref[0, :, sl], vc_ref[0, :, sl]
            qs = _stack_heads(q_ref[0, :, sl], head)
            sp = jnp.where(mask_p, _nt(qs, kp), NEG)
            sc = jnp.where(mask_c, _nt(qs, kc), NEG)
            m = jnp.maximum(jnp.max(sp, axis=1, keepdims=True), jnp.max(sc, axis=1, keepdims=True))
            pp = jnp.exp(sp - m)
            pc = jnp.exp(sc - m)
            den = jnp.sum(pp, axis=1, keepdims=True) + jnp.sum(pc, axis=1, keepdims=True)
            o = (_nn(pp.astype(BF16), vp) + _nn(pc.astype(BF16), vc)) / den
            lse = m + jnp.log(den)
            o_ref[0, :, sl] = jnp.where(head0, o[:ATT_BLOCK], o[ATT_BLOCK:])
            lse_ref[0, :, sl] = jnp.where(head0, lse[:ATT_BLOCK], lse[ATT_BLOCK:])

    cur = pl.BlockSpec((1, ATT_BLOCK, W), lambda r, n: (r, n, 0))
    prev = pl.BlockSpec((1, ATT_BLOCK, W), lambda r, n: (r, jnp.maximum(n - 1, 0), 0))
    out = jax.ShapeDtypeStruct((d, L, W), F32)
    return pl.pallas_call(
        body, name=name, grid=(d, nb),
        in_specs=[cur, prev, cur, prev, cur],
        out_specs=[cur, cur], out_shape=[out, out],
        compiler_params=_params(("parallel", "parallel")),
    )(q, k, k, v, v)


def attn_combine(os_, lses, name):
    T = os_[0].shape[0] * os_[0].shape[1]
    W = os_[0].shape[2]
    tm = LAYOUT_ROWS
    npat = len(os_)

    def body(*refs):
        o_refs, l_refs = refs[:npat], refs[npat:2 * npat]
        out_ref = refs[2 * npat]
        ores, lres = refs[2 * npat + 1:3 * npat + 1], refs[3 * npat + 1:4 * npat + 1]
        bufs = refs[4 * npat + 1:]
        lbufs, obufs, out_buf, lse_buf = bufs[:npat], bufs[npat:2 * npat], bufs[2 * npat], bufs[2 * npat + 1]
        for p, d in enumerate(DILATIONS):
            _from_res(l_refs[p], lbufs[p], d)
            _from_res(o_refs[p], obufs[p], d)
        for hp in range(W // HEAD_LANES):
            ls = [b[hp] for b in lbufs]
            m = ls[0]
            for l in ls[1:]:
                m = jnp.maximum(m, l)
            es = [jnp.exp(l - m) for l in ls]
            z = es[0]
            for e in es[1:]:
                z = z + e
            acc = es[0] * obufs[0][hp]
            for p in range(1, npat):
                acc = acc + es[p] * obufs[p][hp]
            out = acc / z
            out_ref[:, _lanes(hp)] = out.astype(BF16)
            out_buf[hp] = out
            lse_buf[hp] = m + jnp.log(z)
        for p, d in enumerate(DILATIONS):
            _to_res(out_buf, ores[p], d, BF16)
            _to_res(lse_buf, lres[p], d, F32)

    return pl.pallas_call(
        body, name=name, grid=(T // tm,),
        in_specs=[_res_spec(d, tm, W) for _ in range(2) for d in DILATIONS],
        out_specs=([pl.BlockSpec((tm, W), lambda i: (i, 0))] + [_res_spec(d, tm, W) for _ in range(2) for d in DILATIONS]),
        out_shape=([jax.ShapeDtypeStruct((T, W), BF16)] + [_res_shape(d, T, W, BF16) for d in DILATIONS]
                   + [_res_shape(d, T, W, F32) for d in DILATIONS]),
        scratch_shapes=[_slab_buf(tm, W)] * (2 * npat + 2),
        compiler_params=_params(("parallel",)),
    )(*os_, *lses)


def attn_bwd(q, k, v, do, o, lse, name):
    d, L, W = q.shape
    nb = L // ATT_BLOCK

    def body(q_ref, kp_ref, kc_ref, vp_ref, vc_ref, do_ref, o_ref, lse_ref, dq_ref, dk_ref, dv_ref, kkeep, vkeep):
        n = pl.program_id(1)

        @pl.when(n < nb)
        def _():
            mask_p, mask_c, head, head0 = _band_masks(n)
            for hp in range(W // HEAD_LANES):
                sl = slice(hp * HEAD_LANES, (hp + 1) * HEAD_LANES)
                kp, kc, vp, vc = kp_ref[0, :, sl], kc_ref[0, :, sl], vp_ref[0, :, sl], vc_ref[0, :, sl]
                dout = do_ref[0, :, sl]
                qs = _stack_heads(q_ref[0, :, sl], head)
                dos = _stack_heads(dout, head)
                lse_v = lse_ref[0, :, sl]
                lse_c = jnp.max(jnp.where(head, jnp.concatenate([lse_v, lse_v], axis=0), NEG), axis=1, keepdims=True)
                delta = jnp.sum(_stack_heads(dout.astype(F32) * o_ref[0, :, sl].astype(F32), head), axis=1, keepdims=True)
                pp = jnp.exp(jnp.where(mask_p, _nt(qs, kp), NEG) - lse_c)
                pc = jnp.exp(jnp.where(mask_c, _nt(qs, kc), NEG) - lse_c)
                dsp = (pp * (_nt(dos, vp) - delta)).astype(BF16)
                dsc = (pc * (_nt(dos, vc) - delta)).astype(BF16)
                dq2 = _nn(dsp, kp) + _nn(dsc, kc)
                dq_ref[0, :, sl] = jnp.where(head0, dq2[:ATT_BLOCK], dq2[ATT_BLOCK:])
                kprev = _tn(dsp, qs)
                vprev = _tn(pp.astype(BF16), dos)

                @pl.when(n > 0)
                def _():
                    dk_ref[0, :, sl] = kkeep[:, sl] + kprev
                    dv_ref[0, :, sl] = vkeep[:, sl] + vprev

                kkeep[:, sl] = _tn(dsc, qs)
                vkeep[:, sl] = _tn(pc.astype(BF16), dos)

        @pl.when(n == nb)
        def _():
            dk_ref[0] = kkeep[...]
            dv_ref[0] = vkeep[...]

    cur = pl.BlockSpec((1, ATT_BLOCK, W), lambda r, n: (r, jnp.minimum(n, nb - 1), 0))
    prev = pl.BlockSpec((1, ATT_BLOCK, W), lambda r, n: (r, jnp.clip(n - 1, 0, nb - 1), 0))
    out = jax.ShapeDtypeStruct((d, L, W), F32)
    return pl.pallas_call(
        body, name=name, grid=(d, nb + 1),
        in_specs=[cur, prev, cur, prev, cur, cur, cur, cur],
        out_specs=[cur, prev, prev], out_shape=[out, out, out],
        scratch_shapes=[pltpu.VMEM((ATT_BLOCK, W), F32), pltpu.VMEM((ATT_BLOCK, W), F32)],
        compiler_params=_params(("parallel", "arbitrary")),
    )(q, k, k, v, v, do, o, lse)


def final_loss_bwd(h, gf, tgt, name):
    T, D = h.shape
    tm = _tile(T, 512, 8)

    def body(h_ref, g_ref, t_ref, dh_ref, red_ref):
        x = h_ref[...]
        r = lax.rsqrt(jnp.mean(x * x, axis=-1, keepdims=True) + EPS)
        n = x * r
        g = g_ref[...]
        err = n * g - t_ref[...]
        dy = err * (1.0 / D)

        @pl.when(pl.program_id(0) == 0)
        def _():
            red_ref[...] = jnp.zeros_like(red_ref)

        red_ref[0:1, :] += jnp.sum(dy * n, axis=0, keepdims=True)
        red_ref[1:2, :] += jnp.zeros((1, D), F32) + (0.5 / D) * jnp.sum(err * err, keepdims=True)
        dn = dy * g
        dh_ref[...] = r * (dn - n * jnp.mean(dn * n, axis=-1, keepdims=True))

    return pl.pallas_call(
        body, name=name, grid=(T // tm,),
        in_specs=[pl.BlockSpec((tm, D), lambda i: (i, 0)),
                  pl.BlockSpec((1, D), lambda i: (0, 0)),
                  pl.BlockSpec((tm, D), lambda i: (i, 0))],
        out_specs=[pl.BlockSpec((tm, D), lambda i: (i, 0)), pl.BlockSpec((8, D), lambda i: (0, 0))],
        out_shape=[jax.ShapeDtypeStruct((T, D), F32), jax.ShapeDtypeStruct((8, D), F32)],
        compiler_params=_params(("arbitrary",)),
    )(h, gf, tgt)


def ada_fwd(c_all, ada_w, ada_b, name):
    nl, D, N = ada_w.shape

    def body(c_ref, w_ref, b_ref, o_ref):
        c = c_ref[...]
        o_ref[0] = _nn(c * _sigmoid(c), w_ref[0]) + b_ref[0]

    return pl.pallas_call(
        body, name=name, grid=(nl,),
        in_specs=[pl.BlockSpec((N_DEV, D), lambda l: (0, 0)),
                  pl.BlockSpec((1, D, N), lambda l: (l, 0, 0)),
                  pl.BlockSpec((1, 1, N), lambda l: (l, 0, 0))],
        out_specs=pl.BlockSpec((1, N_DEV, N), lambda l: (l, 0, 0)),
        out_shape=jax.ShapeDtypeStruct((nl, N_DEV, N), F32),
        compiler_params=_params(("parallel",)),
    )(c_all, ada_w, ada_b)


def ada_bwd(c_allT, dmod, name):
    nl, _, N = dmod.shape
    D = c_allT.shape[0]

    def body(c_ref, g_ref, o_ref):
        c = c_ref[...]
        ca = c * _sigmoid(c)
        acc = ca[:, 0:1] * g_ref[0, 0:1, :]
        for b in range(1, N_DEV):
            acc = acc + ca[:, b:b + 1] * g_ref[0, b:b + 1, :]
        o_ref[0] = acc

    return pl.pallas_call(
        body, name=name, grid=(nl,),
        in_specs=[pl.BlockSpec((D, N_DEV), lambda l: (0, 0)),
                  pl.BlockSpec((1, N_DEV, N), lambda l: (l, 0, 0))],
        out_specs=pl.BlockSpec((1, D, N), lambda l: (l, 0, 0)),
        out_shape=jax.ShapeDtypeStruct((nl, D, N), F32),
        compiler_params=_params(("parallel",)),
    )(c_allT, dmod)


def adamw(w, g, m, v, name):
    R, C = w.shape
    tr = _tile(R, max(8, (1 << 19) // C // 8 * 8), 8)
    c1 = 1.0 - ADAM_B1 ** ADAM_STEP
    c2 = 1.0 - ADAM_B2 ** ADAM_STEP

    def body(w_ref, g_ref, m_ref, v_ref, d_ref, mo_ref, vo_ref):
        gv = g_ref[...]
        mn = ADAM_B1 * m_ref[...] + (1.0 - ADAM_B1) * gv
        vn = ADAM_B2 * v_ref[...] + (1.0 - ADAM_B2) * (gv * gv)
        mo_ref[...] = mn
        vo_ref[...] = vn
        d_ref[...] = -ADAM_LR * ((mn / c1) / (jnp.sqrt(vn / c2) + ADAM_EPS) + ADAM_WD * w_ref[...])

    blk = pl.BlockSpec((tr, C), lambda i: (i, 0))
    out = jax.ShapeDtypeStruct((R, C), F32)
    return pl.pallas_call(
        body, name=name, grid=(R // tr,),
        in_specs=[blk] * 4, out_specs=[blk] * 3, out_shape=[out] * 3,
        compiler_params=_params(("parallel",)),
    )(w, g, m, v)


def adamw_layer(w, g, m, v, l, prev, name):
    NLw, R, C = w.shape
    tr = _tile(R, max(8, (1 << 19) // C // 8 * 8), 8)
    nrb = R // tr
    c1 = 1.0 - ADAM_B1 ** ADAM_STEP
    c2 = 1.0 - ADAM_B2 ** ADAM_STEP
    w, m, v = (t.reshape(NLw * R, C) for t in (w, m, v))

    def body(w_ref, g_ref, m_ref, v_ref, *rest):
        go_ref, d_ref, mo_ref, vo_ref = rest[-4:]
        gv = g_ref[...]
        mn = ADAM_B1 * m_ref[...] + (1.0 - ADAM_B1) * gv
        vn = ADAM_B2 * v_ref[...] + (1.0 - ADAM_B2) * (gv * gv)
        go_ref[...] = gv
        mo_ref[...] = mn
        vo_ref[...] = vn
        d_ref[...] = -ADAM_LR * ((mn / c1) / (jnp.sqrt(vn / c2) + ADAM_EPS) + ADAM_WD * w_ref[...])

    lay = pl.BlockSpec((tr, C), lambda i: (l * nrb + i, 0))
    out = jax.ShapeDtypeStruct((NLw * R, C), F32)
    n_prev = 0 if prev is None else 4
    return pl.pallas_call(
        body, name=name, grid=(nrb,),
        in_specs=[lay, pl.BlockSpec((tr, C), lambda i: (i, 0)), lay, lay] + [pl.BlockSpec(memory_space=pl.ANY)] * n_prev,
        out_specs=[lay] * 4, out_shape=[out] * 4,
        input_output_aliases={4 + i: i for i in range(n_prev)},
        compiler_params=_params(("parallel",)),
    )(w, g, m, v, *(prev or ()))


def sum_slots(x, name):
    S, R, C = x.shape
    tr = _tile(R, 128, 8)

    def body(x_ref, o_ref):
        acc = x_ref[0]
        for s in range(1, S):
            acc = acc + x_ref[s]
        o_ref[...] = acc

    return pl.pallas_call(
        body, name=name, grid=(R // tr,),
        in_specs=[pl.BlockSpec((S, tr, C), lambda i: (0, i, 0))],
        out_specs=pl.BlockSpec((tr, C), lambda i: (i, 0)),
        out_shape=jax.ShapeDtypeStruct((R, C), F32),
        compiler_params=_params(("parallel",)),
    )(x)


def sum_halves(g, lands, c_idx, name):
    n, ns, _, rh, D = g.shape

    def body(c_ref, g_ref, l_ref, o_ref):
        o_ref[0, 0] = (g_ref[0, 0, 0].astype(F32) + l_ref[0, 0].astype(F32)).astype(BF16)

    return pl.pallas_call(
        body, name=name,
        grid_spec=pltpu.PrefetchScalarGridSpec(
            num_scalar_prefetch=1, grid=(n, ns),
            in_specs=[pl.BlockSpec((1, 1, 1, rh, D), lambda i, j, c: (i, j, c[0], 0, 0)),
                      pl.BlockSpec((1, 1, rh, D), lambda i, j, c: (i, j, 0, 0))],
            out_specs=pl.BlockSpec((1, 1, rh, D), lambda i, j, c: (i, j, 0, 0))),
        out_shape=jax.ShapeDtypeStruct((n, ns, rh, D), BF16),
        compiler_params=_params(("parallel", "parallel")),
    )(c_idx, g, lands)


def sum_chips(p, lands, place, name):
    n, ns, rh, D = p.shape

    def body(c_ref, p_ref, l_ref, o_ref):
        acc = p_ref[0, 0].astype(F32)
        for j in range(N_CHIP - 1):
            acc = acc + l_ref[j, 0].astype(F32)
        o_ref[0, 0] = acc

    return pl.pallas_call(
        body, name=name,
        grid_spec=pltpu.PrefetchScalarGridSpec(
            num_scalar_prefetch=1, grid=(n,),
            in_specs=[pl.BlockSpec((1, 1, rh, D), lambda i, c: (i, c[0], 0, 0)),
                      pl.BlockSpec((N_CHIP - 1, 1, rh, D), lambda i, c: (0, i, 0, 0))],
            out_specs=pl.BlockSpec((1, 1, rh, D), lambda i, c: (i, c[1], 0, 0))),
        out_shape=jax.ShapeDtypeStruct((n, 2, rh, D), F32),
        compiler_params=_params(("parallel",)),
    )(place, p, lands)


def _my_place():
    return lax.axis_index("x"), lax.axis_index("y"), lax.axis_index("c")


def _other_chips(mx, my):
    return [(1 - mx, my), (mx, 1 - my), (1 - mx, 1 - my)]


def gather_small(x, name):
    def body(x_ref, out_ref, sum_ref, send_sems, recv_sems):
        mx, my, mc = _my_place()
        me = 4 * mx + 2 * my + mc
        out_ref[me] = x_ref[...]
        sends = []
        for k in range(1, N_DEV):
            kx, ky, kc = (k >> 2) & 1, (k >> 1) & 1, k & 1
            peer = (1 - mx if kx else mx, 1 - my if ky else my, 1 - mc if kc else mc)
            cp = pltpu.make_async_remote_copy(
                src_ref=x_ref, dst_ref=out_ref.at[me], send_sem=send_sems.at[k - 1], recv_sem=recv_sems.at[k - 1],
                device_id=peer, device_id_type=MESH)
            cp.start()
            sends.append((cp, 4 * peer[0] + 2 * peer[1] + peer[2], peer))
        for k, (cp, peer_slot, peer) in enumerate(sends):
            pltpu.make_async_remote_copy(
                src_ref=x_ref, dst_ref=out_ref.at[peer_slot], send_sem=send_sems.at[k], recv_sem=recv_sems.at[k],
                device_id=peer, device_id_type=MESH).wait_recv()
        for cp, _, _ in sends:
            cp.wait_send()
        acc = out_ref[0]
        for s in range(1, N_DEV):
            acc = acc + out_ref[s]
        sum_ref[...] = acc

    vmem = pl.BlockSpec(memory_space=pltpu.VMEM)
    return pl.pallas_call(
        body, name=name,
        in_specs=[vmem], out_specs=[vmem, vmem],
        out_shape=[jax.ShapeDtypeStruct((N_DEV,) + x.shape, x.dtype), jax.ShapeDtypeStruct(x.shape, x.dtype)],
        scratch_shapes=[pltpu.SemaphoreType.DMA((N_DEV - 1,)), pltpu.SemaphoreType.DMA((N_DEV - 1,))],
        compiler_params=pltpu.CompilerParams(vmem_limit_bytes=VMEM_LIMIT),
    )(x)


_HBM =pl.BlockSpec(memory_space=pltpu.HBM)
_SEM = pl.BlockSpec(memory_space=pltpu.SEMAPHORE)
_DATAFLOW = pltpu.SideEffectType.DATAFLOW_SIDE_EFFECTING


def _gather_copies(shard, land, send, recv, base):
    mx, my, mc = _my_place()
    ci = 2 * mx + my
    peers = [((cx, cy, mc), 2 * cx + cy) for cx, cy in _other_chips(mx, my)] + [((mx, my, 1 - mc), ci)]
    out = []
    for q, (dev, src_slot) in enumerate(peers):
        out.append((
            pltpu.make_async_remote_copy(src_ref=shard, dst_ref=land.at[:, ci], send_sem=send.at[base + q],
                                         recv_sem=recv.at[base + q], device_id=dev, device_id_type=MESH),
            pltpu.make_async_remote_copy(src_ref=shard, dst_ref=land.at[:, src_slot], send_sem=send.at[base + q],
                                         recv_sem=recv.at[base + q], device_id=dev, device_id_type=MESH)))
    return out


def gather_start(groups, after, name):
    items = [s for g in groups for s in g]
    ni, ng = len(items), len(groups)

    def body(*refs):
        shards, lands = refs[:ni], refs[ni:2 * ni]
        sems = refs[2 * ni + 1:2 * ni + 1 + 2 * ng]
        token = refs[-1]
        i = 0
        for g, grp in enumerate(groups):
            for p in range(len(grp)):
                for start_cp, _ in _gather_copies(shards[i], lands[i], sems[2 * g], sems[2 * g + 1], 4 * p):
                    start_cp.start()
                i += 1
        token[...] = jnp.zeros_like(token)

    sem_shapes = []
    for grp in groups:
        sem_shapes += [pltpu.SemaphoreType.DMA((4 * len(grp),))] * 2
    land_shapes = [(s.shape[0], N_CHIP) + s.shape[1:] for s in items]
    outs = pl.pallas_call(
        body, name=name,
        in_specs=[_HBM] * (2 * ni) + [pl.BlockSpec(memory_space=pl.ANY)],
        out_specs=[_SEM] * (2 * ng) + [_HBM] * (2 * ni) + [pl.BlockSpec(memory_space=pltpu.VMEM)],
        out_shape=(sem_shapes + [pltpu.HBM(s.shape, s.dtype) for s in items]
                   + [pltpu.HBM(ls, s.dtype) for ls, s in zip(land_shapes, items)]
                   + [jax.ShapeDtypeStruct((8, 128), F32)]),
        input_output_aliases={i: 2 * ng + i for i in range(2 * ni)},
        compiler_params=pltpu.CompilerParams(has_side_effects=_DATAFLOW),
    )(*[pltpu.with_memory_space_constraint(s, pltpu.HBM) for s in items],
      *[pltpu.with_memory_space_constraint(lax.empty(ls, s.dtype), pltpu.HBM) for ls, s in zip(land_shapes, items)],
      after)
    sems, thru, token = outs[:2 * ng], outs[2 * ng:2 * ng + 2 * ni], outs[-1]
    handles, i = [], 0
    for g, grp in enumerate(groups):
        n = len(grp)
        handles.append((sems[2 * g], sems[2 * g + 1], thru[i:i + n], thru[ni + i:ni + i + n]))
        i += n
    return handles, token


def gather_wait(handle, after, name):
    send, recv, shards, lands = handle
    n = len(shards)

    def body(*refs):
        shard_refs, land_refs = refs[:n], refs[n:2 * n]
        send_ref, recv_ref = refs[2 * n], refs[2 * n + 1]
        for p in range(n):
            for start_cp, recv_cp in _gather_copies(shard_refs[p], land_refs[p], send_ref, recv_ref, 4 * p):
                start_cp.wait_send()
                recv_cp.wait_recv()

    outs = pl.pallas_call(
        body, name=name,
        in_specs=[_HBM] * (2 * n) + [_SEM, _SEM, pl.BlockSpec(memory_space=pl.ANY)],
        out_specs=[_HBM] * (2 * n),
        out_shape=[pltpu.HBM(s.shape, s.dtype) for s in shards] + [pltpu.HBM(l.shape, l.dtype) for l in lands],
        input_output_aliases={i: i for i in range(2 * n)},
        compiler_params=pltpu.CompilerParams(has_side_effects=_DATAFLOW),
    )(*shards, *lands, send, recv, after)
    return outs[n:]


def sibling_send_half(gs, name):
    K = len(gs)

    def body(*refs):
        ins, outs = refs[:K], refs[K:2 * K]
        send, recv = refs[2 * K:]
        mx, my, mc = _my_place()
        cps = []
        for k in range(K):
            cp = pltpu.make_async_remote_copy(
                src_ref=ins[k].at[:, :, 1 - mc], dst_ref=outs[k], send_sem=send.at[k], recv_sem=recv.at[k],
                device_id=(mx, my, 1 - mc), device_id_type=MESH)
            cp.start()
            cps.append(cp)
        for cp in cps:
            cp.wait()

    hbm = pl.BlockSpec(memory_space=pl.ANY)
    return pl.pallas_call(
        body, name=name,
        in_specs=[hbm] * K, out_specs=[hbm] * K,
        out_shape=[jax.ShapeDtypeStruct(g.shape[:2] + g.shape[3:], g.dtype) for g in gs],
        scratch_shapes=[pltpu.SemaphoreType.DMA((K,)), pltpu.SemaphoreType.DMA((K,))],
    )(*gs)


def _small_copies(x, land, send, recv):
    mx, my, mc = _my_place()
    me = 4 * mx + 2 * my + mc
    out = []
    for k in range(1, N_DEV):
        peer = (1 - mx if k & 4 else mx, 1 - my if k & 2 else my, 1 - mc if k & 1 else mc)
        slot = 4 * peer[0] + 2 * peer[1] + peer[2]
        out.append(tuple(pltpu.make_async_remote_copy(
            src_ref=x, dst_ref=land.at[s], send_sem=send.at[k - 1], recv_sem=recv.at[k - 1],
            device_id=peer, device_id_type=MESH) for s in (me, slot)))
    return out


def small_start(x, after, name):
    def body(x_ref, land_ref, after_ref, send, recv, x_thru, land_thru, token):
        for mine, _ in _small_copies(x_ref, land_ref, send, recv):
            mine.start()
        token[...] = jnp.zeros_like(token)

    land_shape = (N_DEV,) + x.shape
    outs = pl.pallas_call(
        body, name=name,
        in_specs=[_HBM, _HBM, pl.BlockSpec(memory_space=pl.ANY)],
        out_specs=[_SEM, _SEM, _HBM, _HBM, pl.BlockSpec(memory_space=pltpu.VMEM)],
        out_shape=[pltpu.SemaphoreType.DMA((N_DEV - 1,))] * 2 + [pltpu.HBM(x.shape, x.dtype), pltpu.HBM(land_shape, x.dtype),
                                                                 jax.ShapeDtypeStruct((8, 128), F32)],
        input_output_aliases={0: 2, 1: 3},
        compiler_params=pltpu.CompilerParams(has_side_effects=_DATAFLOW),
    )(pltpu.with_memory_space_constraint(x, pltpu.HBM),
      pltpu.with_memory_space_constraint(lax.empty(land_shape, x.dtype), pltpu.HBM), after)
    return outs[:4], outs[4]


def small_wait(handle, after, name):
    send, recv, x, land = handle

    def body(x_ref, land_ref, send_ref, recv_ref, after_ref, x_out, land_out):
        for mine, theirs in _small_copies(x_ref, land_ref, send_ref, recv_ref):
            mine.wait_send()
            theirs.wait_recv()

    return pl.pallas_call(
        body, name=name,
        in_specs=[_HBM, _HBM, _SEM, _SEM, pl.BlockSpec(memory_space=pl.ANY)],
        out_specs=[_HBM, _HBM],
        out_shape=[pltpu.HBM(x.shape, x.dtype), pltpu.HBM(land.shape, land.dtype)],
        input_output_aliases={0: 0, 1: 1},
        compiler_params=pltpu.CompilerParams(has_side_effects=_DATAFLOW),
    )(x, land, send, recv, after)


def _scatter_copies(ps, lands, send, recv):
    mx, my, mc = _my_place()
    cps = []
    for j, (cx, cy) in enumerate(_other_chips(mx, my)):
        for k in range(len(ps)):
            cps.append(pltpu.make_async_remote_copy(
                src_ref=ps[k].at[:, 2 * cx + cy], dst_ref=lands[k].at[j],
                send_sem=send.at[k * 3 + j], recv_sem=recv.at[k * 3 + j],
                device_id=(cx, cy, mc), device_id_type=MESH))
    return cps


def scatter_start(ps, after, name):
    K = len(ps)

    def body(*refs):
        ins, lands = refs[:K], refs[K:2 * K]
        send, recv = refs[2 * K + 1], refs[2 * K + 2]
        for cp in _scatter_copies(ins, lands, send, recv):
            cp.start()
        refs[-1][...] = jnp.zeros_like(refs[-1])

    land_shapes = [(N_CHIP - 1, p.shape[0]) + p.shape[2:] for p in ps]
    outs = pl.pallas_call(
        body, name=name,
        in_specs=[_HBM] * (2 * K) + [pl.BlockSpec(memory_space=pl.ANY)],
        out_specs=[_SEM, _SEM] + [_HBM] * (2 * K) + [pl.BlockSpec(memory_space=pltpu.VMEM)],
        out_shape=([pltpu.SemaphoreType.DMA((3 * K,))] * 2 + [pltpu.HBM(p.shape, p.dtype) for p in ps]
                   + [pltpu.HBM(ls, p.dtype) for ls, p in zip(land_shapes, ps)] + [jax.ShapeDtypeStruct((8, 128), F32)]),
        input_output_aliases={i: 2 + i for i in range(2 * K)},
        compiler_params=pltpu.CompilerParams(has_side_effects=_DATAFLOW),
    )(*[pltpu.with_memory_space_constraint(p, pltpu.HBM) for p in ps],
      *[pltpu.with_memory_space_constraint(lax.empty(ls, p.dtype), pltpu.HBM) for ls, p in zip(land_shapes, ps)],
      after)
    return (outs[0], outs[1], outs[2:2 + K], outs[2 + K:2 + 2 * K]), outs[-1]


def scatter_wait(handle, after, name):
    send, recv, ps, lands = handle
    K = len(ps)

    def body(*refs):
        ins, land_refs = refs[:K], refs[K:2 * K]
        send_ref, recv_ref = refs[2 * K], refs[2 * K + 1]
        for cp in _scatter_copies(ins, land_refs, send_ref, recv_ref):
            cp.wait_send()
            cp.wait_recv()

    outs = pl.pallas_call(
        body, name=name,
        in_specs=[_HBM] * (2 * K) + [_SEM, _SEM, pl.BlockSpec(memory_space=pl.ANY)],
        out_specs=[_HBM] * (2 * K),
        out_shape=[pltpu.HBM(p.shape, p.dtype) for p in ps] + [pltpu.HBM(l.shape, l.dtype) for l in lands],
        input_output_aliases={i: i for i in range(2 * K)},
        compiler_params=pltpu.CompilerParams(has_side_effects=_DATAFLOW),
    )(*ps, *lands, send, recv, after)
    return outs[:K], outs[K:]


def sibling_complete(ss, name):
    K = len(ss)

    def body(*refs):
        ins, outs = refs[:K], refs[K:2 * K]
        send, recv = refs[2 * K:]
        mx, my, mc = _my_place()
        cps = []
        for k in range(K):
            cp = pltpu.make_async_remote_copy(
                src_ref=ins[k].at[:, mc], dst_ref=outs[k].at[:, mc], send_sem=send.at[k], recv_sem=recv.at[k],
                device_id=(mx, my, 1 - mc), device_id_type=MESH)
            cp.start()
            cps.append(cp)
        for k in range(K):
            pltpu.make_async_remote_copy(
                src_ref=ins[k].at[:, mc], dst_ref=outs[k].at[:, 1 - mc], send_sem=send.at[k], recv_sem=recv.at[k],
                device_id=(mx, my, 1 - mc), device_id_type=MESH).wait_recv()
        for cp in cps:
            cp.wait_send()

    hbm = pl.BlockSpec(memory_space=pl.ANY)
    return pl.pallas_call(
        body, name=name,
        in_specs=[hbm] * K, out_specs=[hbm] * K,
        out_shape=[jax.ShapeDtypeStruct(s.shape, s.dtype) for s in ss],
        scratch_shapes=[pltpu.SemaphoreType.DMA((K,)), pltpu.SemaphoreType.DMA((K,))],
        input_output_aliases={k: k for k in range(K)},
    )(*ss)


def _rope_tables(T):
    inv = ROPE_THETA ** (-jnp.arange(0, ATT_DH, 2, dtype=F32) / ATT_DH)
    ang = jnp.arange(T, dtype=F32)[:, None] * inv[None, :]
    ang = jnp.concatenate([ang, ang, ang, ang], axis=-1)
    return jnp.cos(ang), jnp.sin(ang)


def _ffn_fwd(h, ng, i_n, mod, i0, wgT, wuT, wd, tag):
    y = normmod_fwd(h, ng, i_n, mod, i0, i0 + 1, f"normmod_{tag}")
    a, b, s = ffn_up(y, wgT, wuT, f"ffn_up_{tag}")
    hn, o = resid_matmul([s], wd, h, mod, i0 + 2, 0.5, f"ffn_down_{tag}")
    return hn, (h, y, a, b, s, o)


def _ffn_bwd(dh, res, ng, i_n, mod, i0, wgT, wuT, wd, on_grads, tag):
    h, y, a, b, s, o = res
    F = _wrows(wgT)
    do, red_g = gate_bwd(dh, o, mod, i0 + 2, 0.5, f"gate_bwd_{tag}")
    da, db = ffn_bwd_mid(do, wd, a, b, f"ffn_bwd_mid_{tag}")
    gbuf = lax.empty((3, F, h.shape[1]), BF16)
    gbuf = matmul_tn(da, y, gbuf, 0, 0, f"dwg_{tag}")
    gbuf = matmul_tn(db, y, gbuf, 1, 0, f"dwu_{tag}")
    gbuf = matmul_tn(s, do, gbuf, 2, 0, f"dwd_{tag}")
    mod = mod + on_grads([gbuf])
    dh_new, red_n = dy_normbwd([(da, 0, wgT, 0, F), (db, 0, wuT, 0, F)], h, dh, ng, i_n, mod, i0 + 1,
                               f"ffn_bwd_dy_{tag}")
    return dh_new, red_n, red_g


def _mixer_fwd(h, ng, mod, w_inT, w_out, sgu, cos, sin, tag):
    lng, lnb, sw, swt, bcol = sgu
    y = normmod_fwd(h, ng, 1, mod, 3, 4, f"normmod_{tag}")
    proj = matmul_nt(y, w_inT, f"proj_{tag}")
    out_a = sgu_fwd(proj, lng, lnb, sw, bcol, f"sgu_fwd_{tag}")
    qkv = rope_fwd(proj, cos, sin, f"rope_fwd_{tag}")
    npat = len(DILATIONS)
    qkv_res = [tuple(qkv[3 * p:3 * p + 3]) for p in range(npat)]
    os_, lses = [], []
    for d, (qd, kd, vd) in zip(DILATIONS, qkv_res):
        o_d, lse_d = attn_fwd(qd, kd, vd, f"attn_fwd_d{d}_{tag}")
        os_.append(o_d)
        lses.append(lse_d)
    comb = attn_combine(os_, lses, f"attn_combine_{tag}")
    out_b, o_res, lse_res = comb[0], comb[1:1 + npat], comb[1 + npat:]
    hn, om = resid_matmul([out_a, out_b], w_out, h, mod, 5, 1.0, f"mix_out_{tag}")
    return hn, (h, y, proj, out_a, out_b, o_res, lse_res, qkv_res, om)


def _mixer_bwd(dh, res, ng, mod, w_inT, w_out, sgu, cos, sin, on_grads, tag):
    lng, lnb, sw, swt, bcol = sgu
    h, y, proj, out_a, out_b, o_res, lse_res, qkv_res, om = res
    D = h.shape[1]
    dom, red_g = gate_bwd(dh, om, mod, 5, 1.0, f"gate_bwd_{tag}")
    dmixed = matmul_nt(dom, w_out, f"dmixed_{tag}")
    woutbuf = lax.empty((1, 2 * MIX_HALF, D), BF16)
    woutbuf = matmul_tn(out_a, dom, woutbuf, 0, 0, f"dwout_a_{tag}", tmo_cap=MIX_HALF)
    woutbuf = matmul_tn(out_b, dom, woutbuf, 0, MIX_HALF, f"dwout_b_{tag}", tmo_cap=MIX_HALF)
    d_uv, d_sw, d_svec = sgu_bwd(proj, dmixed, lng, lnb, sw, swt, bcol, f"sgu_bwd_{tag}")
    do_res = to_residues(dmixed, 1, f"dout_res_{tag}")
    dqs, dks, dvs = [], [], []
    for p, (d, (qd, kd, vd)) in enumerate(zip(DILATIONS, qkv_res)):
        dq, dk, dv = attn_bwd(qd, kd, vd, do_res[p], o_res[p], lse_res[p], f"attn_bwd_d{d}_{tag}")
        dqs.append(dq)
        dks.append(dk)
        dvs.append(dv)
    d_qkv = rope_bwd(dqs, dks, dvs, cos, sin, f"rope_bwd_{tag}")
    winbuf = lax.empty((1, 5 * MIX_HALF, D), BF16)
    winbuf = matmul_tn(d_uv, y, winbuf, 0, 0, f"dwin_uv_{tag}", tmo_cap=MIX_HALF)
    winbuf = matmul_tn(d_qkv, y, winbuf, 0, 2 * MIX_HALF, f"dwin_qkv_{tag}", tmo_cap=MIX_HALF)
    mod = mod + on_grads([winbuf, woutbuf])
    pairs = ([(d_uv, p, w_inT, p, MIX_HALF) for p in range(2)]
             + [(d_qkv, p, w_inT, 2 + p, MIX_HALF) for p in range(3)])
    dh_new, red_n = dy_normbwd(pairs, h, dh, ng, 1, mod, 4, f"mix_bwd_dy_{tag}")
    return dh_new, d_sw, d_svec, red_n, red_g


def _local_step(x, tgt, mods, ngs, get_w, sgus, gf, on_block_grads, on_layer_small):
    T, D = x.shape
    cos, sin = _rope_tables(T)
    h = x
    saved, weights = [], []
    for l in range(2):
        wf1 = get_w(l, "f1", h)
        h, r1 = _ffn_fwd(h, ngs[l], 0, mods[l], 0, (wf1, (0,)), (wf1, (1,)), (wf1, (2,)), f"l{l}f1")
        w_inT, w_out = get_w(l, "mx", h)
        h, r2 = _mixer_fwd(h, ngs[l], mods[l], (w_inT, (0,)), (w_out, (0,)), sgus[l], cos, sin, f"l{l}mx")
        wf2 = get_w(l, "f2", h)
        h, r3 = _ffn_fwd(h, ngs[l], 2, mods[l], 6, (wf2, (0,)), (wf2, (1,)), (wf2, (2,)), f"l{l}f2")
        saved.append((r1, r2, r3))
        weights.append((wf1, w_inT, w_out, wf2))
    dh, red_final = final_loss_bwd(h, gf, tgt, "final_loss_bwd")
    for l in (1, 0):
        r1, r2, r3 = saved[l]
        wf1, w_inT, w_out, wf2 = weights[l]

        def on(blk, l=l):
            return lambda arrays: on_block_grads(l, blk, arrays)

        dh, rn3, rg3 = _ffn_bwd(dh, r3, ngs[l], 2, mods[l], 6, (wf2, (0,)), (wf2, (1,)), (wf2, (2,)), on("f2"),
                                f"l{l}f2")
        dh, d_sw, d_svec, rn2, rg2 = _mixer_bwd(dh, r2, ngs[l], mods[l], (w_inT, (0,)), (w_out, (0,)), sgus[l],
                                                cos, sin, on("mx"), f"l{l}mx")
        dh, rn1, rg1 = _ffn_bwd(dh, r1, ngs[l], 0, mods[l], 0, (wf1, (0,)), (wf1, (1,)), (wf1, (2,)), on("f1"),
                                f"l{l}f1")
        mods = mods + on_layer_small(l, dict(sgu_w=d_sw, sgu_vec=d_svec, red_n=(rn1, rn2, rn3), red_g=(rg1, rg2, rg3)),
                                     red_final if l == 0 else None)
    return dh


def _adam_out(w, g, m, v, name):
    shp = w.shape
    two_d = (-1, shp[-1])
    d, mn, vn = adamw(w.reshape(two_d), g.reshape(two_d), m.reshape(two_d), v.reshape(two_d), name)
    return g, d.reshape(shp), mn.reshape(shp), vn.reshape(shp)


def kernel(x, c, ada_w, ada_b, norm_g, ffn1_wg, ffn1_wu, ffn1_wd, ffn2_wg, ffn2_wu, ffn2_wd, w_in, sgu_ln_g, sgu_ln_b, sgu_w, sgu_b, w_out, final_g, loss_target, m_ada_w, m_ada_b, m_norm_g, m_ffn1_wg, m_ffn1_wu, m_ffn1_wd, m_ffn2_wg, m_ffn2_wu, m_ffn2_wd, m_w_in, m_sgu_ln_g, m_sgu_ln_b, m_sgu_w, m_sgu_b, m_w_out, m_final_g, v_ada_w, v_ada_b, v_norm_g, v_ffn1_wg, v_ffn1_wu, v_ffn1_wd, v_ffn2_wg, v_ffn2_wu, v_ffn2_wd, v_w_in, v_sgu_ln_g, v_sgu_ln_b, v_sgu_w, v_sgu_b, v_w_out, v_final_g):
    T, D = x.shape[1], x.shape[2]
    NL = ada_w.shape[0]
    mx, my, mc = _my_place()
    me = 4 * mx + 2 * my + mc
    ci = 2 * mx + my
    c_idx = jnp.reshape(mc, (1,)).astype(jnp.int32)
    place = jnp.stack([ci, mc]).astype(jnp.int32)

    ngw = norm_g.shape[2]
    small_in = jnp.concatenate([jnp.pad(c, ((0, 7), (0, 0))),
                                jnp.pad(norm_g.reshape(NL * 3, ngw), ((0, 8 - NL * 3), (0, D - ngw)))], axis=0)
    small_all, _ = gather_small(small_in, "gather_c_normg")
    c_all = small_all[:, 0, :]
    ng_parts = small_all[0::2, 8:8 + NL * 3, :ngw]
    ngs = jnp.transpose(ng_parts, (1, 0, 2)).reshape(NL, 3, N_CHIP * ngw)

    nmod = ada_w.shape[2]
    ada_b_mine = lax.dynamic_slice_in_dim(ada_b, ci * nmod, nmod, axis=1).reshape(NL, 1, nmod)
    mod_part = ada_fwd(c_all, ada_w, ada_b_mine, "ada_fwd")
    mod_all, _ = gather_small(mod_part.reshape(NL * N_DEV, nmod), "gather_mod")
    mod_rows = lax.dynamic_index_in_dim(mod_all.reshape(N_DEV, NL, N_DEV, nmod), me, axis=2, keepdims=False)
    mods = jnp.transpose(mod_rows[0::2], (1, 0, 2)).reshape(NL, N_ADA, D)

    sgus = []
    for l in range(NL):
        sgus.append((sgu_ln_g[l].reshape(1, MIX_HALF), sgu_ln_b[l].reshape(1, MIX_HALF), sgu_w[l],
                     jnp.swapaxes(sgu_w[l], 1, 2), jnp.transpose(sgu_b[l])))

    def halves(a):
        n, r, _ = a.shape
        return a.reshape(n, 2, r // 2, D)

    Fs = ffn1_wd.shape[1]
    groups = []
    for l in range(NL):
        f1 = jnp.stack([ffn1_wg[l].T, ffn1_wu[l].T, ffn1_wd[l]], axis=0).astype(BF16)
        f2 = jnp.stack([ffn2_wg[l].T, ffn2_wu[l].T, ffn2_wd[l]], axis=0).astype(BF16)
        groups += [[halves(f1)], [halves(w_in[l].T.astype(BF16)[None]), halves(w_out[l].astype(BF16)[None])],
                   [halves(f2)]]
    handles, token = gather_start(groups, mods, "gather_start")
    mods = mods + token[0, 0]
    block_no = {"f1": 0, "mx": 1, "f2": 2}

    def get_w(l, blk, after):
        g = 3 * l + block_no[blk]
        full = gather_wait(handles[g], after, f"gather_wait_l{l}{blk}")
        full = [a.reshape(a.shape[0], N_CHIP * 2 * a.shape[3], D) for a in full]
        return full[0] if blk != "mx" else tuple(full)

    def split(a):
        n, r4, _ = a.shape
        return a.reshape(n, N_CHIP, 2, r4 // N_CHIP // 2, D)

    pending, small_pending, small_tokens = {}, {}, {}

    def on_block_grads(l, blk, bufs):
        tag = f"l{l}{blk}"
        parts = [split(g) for g in bufs]
        lands = sibling_send_half(parts, f"rs_sibling_{tag}")
        psums = [sum_halves(g, ld, c_idx, f"rs_sum_halves_{tag}_{i}") for i, (g, ld) in enumerate(zip(parts, lands))]
        handle, tok = scatter_start(psums, psums[0], f"rs_chips_start_{tag}")
        pending[(l, blk)] = handle
        return tok[0, 0]

    def block_finish(l, blk, after):
        tag = f"l{l}{blk}"
        psums, lands2 = scatter_wait(pending.pop((l, blk)), after, f"rs_chips_wait_{tag}")
        ssums = [sum_chips(p, ld, place, f"rs_sum_chips_{tag}_{i}") for i, (p, ld) in enumerate(zip(psums, lands2))]
        return [f.reshape(f.shape[0], -1, D) for f in sibling_complete(ssums, f"rs_complete_{tag}")]

    def on_layer_small(l, grads, red_final):
        blocks = list(grads["red_n"]) + list(grads["red_g"])
        blocks.append(jnp.pad(grads["sgu_vec"], ((0, 0), (0, D - MIX_HALF))))
        blocks.append(grads["sgu_w"].reshape(-1, D))
        if red_final is not None:
            blocks.append(red_final)
        xs = jnp.concatenate(blocks, axis=0)
        small_pending[l], small_tokens[l] = small_start(xs, xs, f"small_start_l{l}")
        return small_tokens[l][0, 0]

    grad_x = _local_step(x[0], loss_target[0], mods, ngs, get_w, sgus, final_g.reshape(1, D),
                         on_block_grads, on_layer_small)

    adam_state = {}

    def adam_big(nm, l, g, w, m, v):
        adam_state[nm] = adamw_layer(w, g, m, v, l, adam_state.get(nm), f"adamw_{nm}_l{l}")

    def adam_block(l, blk, fin):
        if blk == "mx":
            adam_big("w_in", l, fin[0][0].T, w_in, m_w_in, v_w_in)
            adam_big("w_out", l, fin[1][0], w_out, m_w_out, v_w_out)
        else:
            ws = ((ffn1_wg, m_ffn1_wg, v_ffn1_wg), (ffn1_wu, m_ffn1_wu, v_ffn1_wu), (ffn1_wd, m_ffn1_wd, v_ffn1_wd)) \
                if blk == "f1" else \
                ((ffn2_wg, m_ffn2_wg, v_ffn2_wg), (ffn2_wu, m_ffn2_wu, v_ffn2_wu), (ffn2_wd, m_ffn2_wd, v_ffn2_wd))
            pre = "ffn1" if blk == "f1" else "ffn2"
            for k, (nm, tr) in enumerate((("wg", True), ("wu", True), ("wd", False))):
                adam_big(f"{pre}_{nm}", l, fin[0][k].T if tr else fin[0][k], *ws[k])

    done_order = [(l, blk) for l in range(NL - 1, -1, -1) for blk in ("f2", "mx", "f1")]
    for l, blk in done_order[:-1]:
        adam_block(l, blk, block_finish(l, blk, small_tokens[0]))
    last_big = adam_state["w_out"][1]

    small_sum, small_all = [], []
    for l in range(NL):
        xs, land = small_wait(small_pending[l], last_big, f"small_wait_l{l}")
        full = lax.dynamic_update_slice(land, xs[None], (me, 0, 0))
        small_all.append(full)
        small_sum.append(sum_slots(full, f"small_sum_l{l}"))
    offs = [8 * i for i in range(8)]
    off_final = offs[7] + SGU_HEADS * ATT_BLOCK * HEAD_LANES // D
    loss = small_sum[0][off_final + 1, 0]
    g_final_g = small_sum[0][off_final, :]
    g_norm_g, g_ada_b, g_lng, g_lnb, g_sb, g_sw, dmod_all = [], [], [], [], [], [], []
    for l in range(NL):
        rn = [small_sum[l][offs[i]:offs[i] + 8] for i in range(3)]
        rg = [small_sum[l][offs[3 + i]:offs[3 + i] + 8] for i in range(3)]
        g_norm_g.append(jnp.stack([rn[i][2] for i in range(3)], axis=0))
        g_ada_b.append(jnp.concatenate([jnp.stack([rn[i][0], rn[i][1], rg[i][0]], axis=0) for i in range(3)],
                                       axis=0).reshape(N_ADA * D))
        sv = small_sum[l][offs[6]:offs[6] + 8, :MIX_HALF]
        g_lng.append(sv[0].reshape(SGU_HEADS, HEAD_LANES))
        g_lnb.append(sv[1].reshape(SGU_HEADS, HEAD_LANES))
        g_sb.append(sv[2].reshape(SGU_HEADS, ATT_BLOCK))
        g_sw.append(small_sum[l][offs[7]:off_final].reshape(sgu_w.shape[1:]))
        rows = []
        for i in range(3):
            an = small_all[l][:, offs[i]:offs[i] + 2]
            ag = small_all[l][:, offs[3 + i]:offs[3 + i] + 1]
            rows += [an[:, 0], an[:, 1], ag[:, 0]]
        dmod_all.append(jnp.stack(rows, axis=1).reshape(N_DEV, N_ADA * D))
    dmod_all = jnp.stack(dmod_all, axis=0)
    dmod_mine = lax.dynamic_slice_in_dim(dmod_all, ci * nmod, nmod, axis=2)
    g_ada_w = ada_bwd(jnp.transpose(c_all), dmod_mine, "ada_bwd")
    g_ada_b = jnp.stack(g_ada_b, axis=0)
    g_norm_g_full = jnp.stack(g_norm_g, axis=0)
    g_norm_g_mine = lax.dynamic_slice_in_dim(g_norm_g_full, ci * ngw, ngw, axis=2)

    small_params = [
        ("ada_w", ada_w, g_ada_w, m_ada_w, v_ada_w),
        ("ada_b", ada_b, g_ada_b, m_ada_b, v_ada_b),
        ("norm_g", norm_g, g_norm_g_mine, m_norm_g, v_norm_g),
        ("sgu_ln_g", sgu_ln_g, jnp.stack(g_lng, axis=0), m_sgu_ln_g, v_sgu_ln_g),
        ("sgu_ln_b", sgu_ln_b, jnp.stack(g_lnb, axis=0), m_sgu_ln_b, v_sgu_ln_b),
        ("sgu_w", sgu_w, jnp.stack(g_sw, axis=0), m_sgu_w, v_sgu_w),
        ("sgu_b", sgu_b, jnp.stack(g_sb, axis=0), m_sgu_b, v_sgu_b),
        ("final_g", final_g.reshape(1, D), g_final_g.reshape(1, D), m_final_g.reshape(1, D), v_final_g.reshape(1, D)),
    ]
    for nm, w, g, m, v in small_params:
        res = _adam_out(w, g, m, v, f"adamw_{nm}")
        adam_state[nm] = tuple(t.reshape(D) for t in res) if nm == "final_g" else res

    l, blk = done_order[-1]
    adam_block(l, blk, block_finish(l, blk, adam_state["ada_w"][1]))

    names = ["ada_w", "ada_b", "norm_g", "ffn1_wg", "ffn1_wu", "ffn1_wd", "ffn2_wg", "ffn2_wu", "ffn2_wd", "w_in",
             "sgu_ln_g", "sgu_ln_b", "sgu_w", "sgu_b", "w_out", "final_g"]
    shapes = [t.shape for t in (ada_w, ada_b, norm_g, ffn1_wg, ffn1_wu, ffn1_wd, ffn2_wg, ffn2_wu, ffn2_wd, w_in,
                                sgu_ln_g, sgu_ln_b, sgu_w, sgu_b, w_out, final_g)]
    return (loss, grad_x[None], *[adam_state[nm][i].reshape(s) for i in range(4) for nm, s in zip(names, shapes)])
```

```python
import math

import jax
import jax.numpy as jnp
from jax import lax
from jax.experimental import pallas as pl
from jax.experimental.pallas import tpu as pltpu

F32 = jnp.float32
BF16 = jnp.bfloat16
EPS = 1e-6
SGU_HEADS = 4
HEAD_LANES = 128
ATT_DH = 64
ATT_BLOCK = 128
MIX_HALF = SGU_HEADS * HEAD_LANES
DILATIONS = (1, 4, 16)
ROPE_THETA = 10000.0
N_ADA = 9
ADAM_LR, ADAM_B1, ADAM_B2, ADAM_EPS, ADAM_WD, ADAM_STEP = 0.001, 0.9, 0.999, 1e-08, 0.01, 10
NEG = -1e30
V7X_VMEM_BYTES = 64 * 1024 * 1024
VMEM_LIMIT = V7X_VMEM_BYTES * 7 // 8
MESH = pl.DeviceIdType.MESH
N_DEV = 8
N_CHIP = 4


def _tile(n, cap, mult):
    if n <= cap:
        return n
    t = (cap // mult) * mult
    while t >= mult:
        if n % t == 0:
            return t
        t -= mult
    raise ValueError((n, cap, mult))


def _params(dims=None):
    return pltpu.CompilerParams(dimension_semantics=dims, vmem_limit_bytes=VMEM_LIMIT)


def _wspec(w, rows, idx):
    arr, lead = w
    return pl.BlockSpec((None,) * len(lead) + (rows, arr.shape[-1]), lambda *g: tuple(lead) + (idx(*g), 0))


def _wrows(w):
    return w[0].shape[-2]


def _nt(a, b):
    return lax.dot_general(a, b, (((1,), (1,)), ((), ())), preferred_element_type=F32)


def _tn(a, b):
    return lax.dot_general(a, b, (((0,), (0,)), ((), ())), preferred_element_type=F32)


def _nn(a, b):
    return jnp.dot(a, b, preferred_element_type=F32)


def _sigmoid(x):
    return 1.0 / (1.0 + jnp.exp(-x))


_GELU_K = math.sqrt(2.0 / math.pi)
_GELU_C = 0.044715


def _gelu(x):
    t = jnp.tanh(_GELU_K * (x + _GELU_C * x * x * x))
    return 0.5 * x * (1.0 + t)


def _gelu_and_grad(x):
    x2 = x * x
    t = jnp.tanh(_GELU_K * (x + _GELU_C * x * x2))
    g = 0.5 * x * (1.0 + t)
    dg = 0.5 * (1.0 + t) + 0.5 * x * (1.0 - t * t) * (_GELU_K * (1.0 + 3.0 * _GELU_C * x2))
    return g, dg


def normmod_fwd(h, ng, i_n, mod, i_sh, i_sc, name):
    T, D = h.shape
    tm = _tile(T, 512, 8)

    def body(h_ref, ng_ref, mod_ref, y_ref):
        x = h_ref[...]
        r = lax.rsqrt(jnp.mean(x * x, axis=-1, keepdims=True) + EPS)
        y = (x * r) * ng_ref[i_n:i_n + 1, :]
        y_ref[...] = (y * (1.0 + mod_ref[i_sc:i_sc + 1, :]) + mod_ref[i_sh:i_sh + 1, :]).astype(BF16)

    return pl.pallas_call(
        body, name=name, grid=(T // tm,),
        in_specs=[pl.BlockSpec((tm, D), lambda i: (i, 0)),
                  pl.BlockSpec(ng.shape, lambda i: (0, 0)),
                  pl.BlockSpec(mod.shape, lambda i: (0, 0))],
        out_specs=pl.BlockSpec((tm, D), lambda i: (i, 0)),
        out_shape=jax.ShapeDtypeStruct((T, D), BF16),
        compiler_params=_params(("parallel",)),
    )(h, ng, mod)


def ffn_up(y, wgT, wuT, name):
    T, D = y.shape
    F = _wrows(wgT)
    tm = _tile(T, 512, 16)
    tf = _tile(F, 1408, 128)

    def body(y_ref, wg_ref, wu_ref, a_ref, b_ref, s_ref):
        yv = y_ref[...]
        a = _nt(yv, wg_ref[...])
        b = _nt(yv, wu_ref[...])
        a_ref[...] = a.astype(BF16)
        b_ref[...] = b.astype(BF16)
        s_ref[...] = (a * _sigmoid(a) * b).astype(BF16)

    act = jax.ShapeDtypeStruct((T, F), BF16)
    return pl.pallas_call(
        body, name=name, grid=(F // tf, T // tm),
        in_specs=[pl.BlockSpec((tm, D), lambda j, i: (i, 0)),
                  _wspec(wgT, tf, lambda j, i: j),
                  _wspec(wuT, tf, lambda j, i: j)],
        out_specs=[pl.BlockSpec((tm, tf), lambda j, i: (i, j))] * 3,
        out_shape=[act, act, act],
        compiler_params=_params(("parallel", "parallel")),
    )(y, wgT[0], wuT[0])


def resid_matmul(xs, w, h, mod, i_g, coef, name):
    T, D = h.shape
    kb = xs[0].shape[1]
    assert all(x.shape == (T, kb) for x in xs) and _wrows(w) == kb * len(xs)
    tm = _tile(T, 512, 16)
    nx = len(xs)

    def body(*refs):
        x_refs, w_refs = refs[:nx], refs[nx:2 * nx]
        h_ref, mod_ref, hn_ref, o_ref = refs[2 * nx:]
        o = _nn(x_refs[0][...], w_refs[0][...])
        for xr, wr in zip(x_refs[1:], w_refs[1:]):
            o = o + _nn(xr[...], wr[...])
        o_ref[...] = o.astype(BF16)
        hn_ref[...] = h_ref[...] + (coef * mod_ref[i_g:i_g + 1, :]) * o

    return pl.pallas_call(
        body, name=name, grid=(T // tm,),
        in_specs=([pl.BlockSpec((tm, kb), lambda i: (i, 0))] * nx
                  + [_wspec(w, kb, lambda i, p=p: p) for p in range(nx)]
                  + [pl.BlockSpec((tm, D), lambda i: (i, 0)),
                     pl.BlockSpec(mod.shape, lambda i: (0, 0))]),
        out_specs=[pl.BlockSpec((tm, D), lambda i: (i, 0))] * 2,
        out_shape=[jax.ShapeDtypeStruct((T, D), F32), jax.ShapeDtypeStruct((T, D), BF16)],
        compiler_params=_params(("parallel",)),
    )(*xs, *([w[0]] * nx), h, mod)


def gate_bwd(dh, o, mod, i_g, coef, name):
    T, D = dh.shape
    tm = _tile(T, 512, 16)

    def body(dh_ref, o_ref, mod_ref, do_ref, red_ref):
        d = dh_ref[...]
        do_ref[...] = (d * (coef * mod_ref[i_g:i_g + 1, :])).astype(BF16)

        @pl.when(pl.program_id(0) == 0)
        def _():
            red_ref[...] = jnp.zeros_like(red_ref)

        red_ref[0:1, :] += coef * jnp.sum(d * o_ref[...].astype(F32), axis=0, keepdims=True)

    return pl.pallas_call(
        body, name=name, grid=(T // tm,),
        in_specs=[pl.BlockSpec((tm, D), lambda i: (i, 0)),
                  pl.BlockSpec((tm, D), lambda i: (i, 0)),
                  pl.BlockSpec(mod.shape, lambda i: (0, 0))],
        out_specs=[pl.BlockSpec((tm, D), lambda i: (i, 0)), pl.BlockSpec((8, D), lambda i: (0, 0))],
        out_shape=[jax.ShapeDtypeStruct((T, D), BF16), jax.ShapeDtypeStruct((8, D), F32)],
        compiler_params=_params(("arbitrary",)),
    )(dh, o, mod)


def ffn_bwd_mid(do, wd, a, b, name):
    T, D = do.shape
    F = _wrows(wd)
    tm = _tile(T, 512, 16)
    tf = _tile(F, 1408, 128)

    def body(do_ref, wd_ref, a_ref, b_ref, da_ref, db_ref):
        ds = _nt(do_ref[...], wd_ref[...])
        av = a_ref[...].astype(F32)
        bv = b_ref[...].astype(F32)
        sig = _sigmoid(av)
        da_ref[...] = (ds * bv * (sig * (1.0 + av * (1.0 - sig)))).astype(BF16)
        db_ref[...] = (ds * (av * sig)).astype(BF16)

    act = jax.ShapeDtypeStruct((T, F), BF16)
    return pl.pallas_call(
        body, name=name, grid=(F // tf, T // tm),
        in_specs=[pl.BlockSpec((tm, D), lambda j, i: (i, 0)),
                  _wspec(wd, tf, lambda j, i: j),
                  pl.BlockSpec((tm, tf), lambda j, i: (i, j)),
                  pl.BlockSpec((tm, tf), lambda j, i: (i, j))],
        out_specs=[pl.BlockSpec((tm, tf), lambda j, i: (i, j))] * 2,
        out_shape=[act, act],
        compiler_params=_params(("parallel", "parallel")),
    )(do, wd[0], a, b)


def dy_normbwd(pairs, h, dhp, ng, i_n, mod, i_sc, name):
    T, D = h.shape
    tm = _tile(T, 256, 16)
    npair = len(pairs)

    def body(*refs):
        x_refs, w_refs = refs[:npair], refs[npair:2 * npair]
        h_ref, dhp_ref, ng_ref, mod_ref, dh_ref, red_ref = refs[2 * npair:]
        dy = _nn(x_refs[0][...], w_refs[0][...])
        for xr, wr in zip(x_refs[1:], w_refs[1:]):
            dy = dy + _nn(xr[...], wr[...])
        x = h_ref[...]
        r = lax.rsqrt(jnp.mean(x * x, axis=-1, keepdims=True) + EPS)
        n = x * r
        gn = ng_ref[i_n:i_n + 1, :]
        dnh = dy * (1.0 + mod_ref[i_sc:i_sc + 1, :])

        @pl.when(pl.program_id(0) == 0)
        def _():
            red_ref[...] = jnp.zeros_like(red_ref)

        red_ref[0:1, :] += jnp.sum(dy, axis=0, keepdims=True)
        red_ref[1:2, :] += jnp.sum(dy * (n * gn), axis=0, keepdims=True)
        red_ref[2:3, :] += jnp.sum(dnh * n, axis=0, keepdims=True)
        dn = dnh * gn
        dh_ref[...] = dhp_ref[...] + r * (dn - n * jnp.mean(dn * n, axis=-1, keepdims=True))

    in_specs = ([pl.BlockSpec((tm, kb), lambda i, c=c: (i, c)) for (_, c, _, _, kb) in pairs]
                + [_wspec(w, kb, lambda i, r=r: r) for (_, _, w, r, kb) in pairs]
                + [pl.BlockSpec((tm, D), lambda i: (i, 0)),
                   pl.BlockSpec((tm, D), lambda i: (i, 0)),
                   pl.BlockSpec(ng.shape, lambda i: (0, 0)),
                   pl.BlockSpec(mod.shape, lambda i: (0, 0))])
    return pl.pallas_call(
        body, name=name, grid=(T // tm,), in_specs=in_specs,
        out_specs=[pl.BlockSpec((tm, D), lambda i: (i, 0)), pl.BlockSpec((8, D), lambda i: (0, 0))],
        out_shape=[jax.ShapeDtypeStruct((T, D), F32), jax.ShapeDtypeStruct((8, D), F32)],
        compiler_params=_params(("arbitrary",)),
    )(*[p[0] for p in pairs], *[p[2][0] for p in pairs], h, dhp, ng, mod)


def matmul_tn(a, b, buf, slot, row0, name, tmo_cap=1408):
    T, N = b.shape
    ma = a.shape[1]
    tmo = _tile(ma, tmo_cap, 128)
    assert row0 % tmo == 0
    nmo = ma // tmo
    tk = _tile(T, 512, 16)
    nk = T // tk

    def body(a_ref, b_ref, buf_ref, o_ref, acc_ref):
        k = pl.program_id(1)

        @pl.when(k == 0)
        def _():
            acc_ref[...] = jnp.zeros_like(acc_ref)

        acc_ref[...] += _tn(a_ref[...], b_ref[...])

        @pl.when(k == nk - 1)
        def _():
            o_ref[...] = acc_ref[...].astype(BF16)

    return pl.pallas_call(
        body, name=name, grid=(nmo, nk),
        in_specs=[pl.BlockSpec((tk, tmo), lambda j, k: (k, j)),
                  pl.BlockSpec((tk, N), lambda j, k: (k, 0)),
                  pl.BlockSpec(memory_space=pl.ANY)],
        out_specs=pl.BlockSpec((None, tmo, N), lambda j, k: (slot, row0 // tmo + j, 0)),
        out_shape=jax.ShapeDtypeStruct(buf.shape, BF16),
        scratch_shapes=[pltpu.VMEM((tmo, N), F32)],
        input_output_aliases={2: 0},
        compiler_params=_params(("parallel", "arbitrary")),
    )(a, b, buf)


def matmul_nt(x, w, name):
    T, K = x.shape
    N = _wrows(w)
    tm = _tile(T, 512, 16)
    tn = _tile(N, 1280, 128)

    def body(x_ref, w_ref, o_ref):
        o_ref[...] = _nt(x_ref[...], w_ref[...]).astype(BF16)

    return pl.pallas_call(
        body, name=name, grid=(N // tn, T // tm),
        in_specs=[pl.BlockSpec((tm, K), lambda j, i: (i, 0)), _wspec(w, tn, lambda j, i: j)],
        out_specs=pl.BlockSpec((tm, tn), lambda j, i: (i, j)),
        out_shape=jax.ShapeDtypeStruct((T, N), BF16),
        compiler_params=_params(("parallel", "parallel")),
    )(x, w[0])


def _sgu_head_fwd(u, v, lng, lnb):
    gu, dgu = _gelu_and_grad(u)
    gv, dgv = _gelu_and_grad(v)
    mu = jnp.mean(gv, axis=-1, keepdims=True)
    xc = gv - mu
    rstd = lax.rsqrt(jnp.mean(xc * xc, axis=-1, keepdims=True) + EPS)
    xhat = xc * rstd
    vn = xhat * lng + lnb
    return gu, dgu, dgv, rstd, xhat, vn


def _tril_mask():
    r = lax.broadcasted_iota(jnp.int32, (ATT_BLOCK, ATT_BLOCK), 0)
    c = lax.broadcasted_iota(jnp.int32, (ATT_BLOCK, ATT_BLOCK), 1)
    return c <= r


def _triu_mask():
    r = lax.broadcasted_iota(jnp.int32, (ATT_BLOCK, ATT_BLOCK), 0)
    c = lax.broadcasted_iota(jnp.int32, (ATT_BLOCK, ATT_BLOCK), 1)
    return r <= c


def sgu_fwd(proj, lng, lnb, w, bcol, name):
    T = proj.shape[0]
    tm = _tile(T, 512, 128)
    nch = tm // ATT_BLOCK

    def body(u_ref, v_ref, lng_ref, lnb_ref, w_ref, b_ref, o_ref):
        tril = _tril_mask()
        for hd in range(SGU_HEADS):
            sl = slice(hd * HEAD_LANES, (hd + 1) * HEAD_LANES)
            u = u_ref[:, sl].astype(F32)
            v = v_ref[:, sl].astype(F32)
            gu, _, _, _, _, vn = _sgu_head_fwd(u, v, lng_ref[:, sl], lnb_ref[:, sl])
            wm = jnp.where(tril, w_ref[hd], 0.0).astype(BF16)
            vnb = vn.astype(BF16)
            bc = b_ref[:, hd:hd + 1]
            for ch in range(nch):
                rs = slice(ch * ATT_BLOCK, (ch + 1) * ATT_BLOCK)
                z = _nn(wm, vnb[rs, :]) + bc
                o_ref[rs, sl] = (gu[rs, :] * z).astype(BF16)

    return pl.pallas_call(
        body, name=name, grid=(T // tm,),
        in_specs=[pl.BlockSpec((tm, MIX_HALF), lambda i: (i, 0)),
                  pl.BlockSpec((tm, MIX_HALF), lambda i: (i, 1)),
                  pl.BlockSpec((1, MIX_HALF), lambda i: (0, 0)),
                  pl.BlockSpec((1, MIX_HALF), lambda i: (0, 0)),
                  pl.BlockSpec(w.shape, lambda i: (0, 0, 0)),
                  pl.BlockSpec(bcol.shape, lambda i: (0, 0))],
        out_specs=pl.BlockSpec((tm, MIX_HALF), lambda i: (i, 0)),
        out_shape=jax.ShapeDtypeStruct((T, MIX_HALF), BF16),
        compiler_params=_params(("parallel",)),
    )(proj, proj, lng, lnb, w, bcol)


def sgu_bwd(proj, dmixed, lng, lnb, w, wt, bcol, name):
    T = proj.shape[0]
    tm = _tile(T, 512, 128)
    nch = tm // ATT_BLOCK
    nsteps = T // tm

    def body(u_ref, v_ref, g_ref, lng_ref, lnb_ref, w_ref, wt_ref, b_ref, duv_ref, dw_ref, dvec_ref, bacc_ref):
        step = pl.program_id(0)

        @pl.when(step == 0)
        def _():
            dw_ref[...] = jnp.zeros_like(dw_ref)
            dvec_ref[...] = jnp.zeros_like(dvec_ref)
            bacc_ref[...] = jnp.zeros_like(bacc_ref)

        tril = _tril_mask()
        triu = _triu_mask()
        for hd in range(SGU_HEADS):
            sl = slice(hd * HEAD_LANES, (hd + 1) * HEAD_LANES)
            u = u_ref[:, sl].astype(F32)
            v = v_ref[:, sl].astype(F32)
            lng_h = lng_ref[:, sl]
            gu, dgu, dgv, rstd, xhat, vn = _sgu_head_fwd(u, v, lng_h, lnb_ref[:, sl])
            wm = jnp.where(tril, w_ref[hd], 0.0).astype(BF16)
            wmt = jnp.where(triu, wt_ref[hd], 0.0).astype(BF16)
            vnb = vn.astype(BF16)
            bc = b_ref[:, hd:hd + 1]
            g = g_ref[:, sl].astype(F32)
            dw_acc = jnp.zeros((ATT_BLOCK, ATT_BLOCK), F32)
            b_acc = jnp.zeros((ATT_BLOCK, HEAD_LANES), F32)
            dvn_parts = []
            for ch in range(nch):
                rs = slice(ch * ATT_BLOCK, (ch + 1) * ATT_BLOCK)
                z = _nn(wm, vnb[rs, :]) + bc
                duv_ref[rs, sl] = (g[rs, :] * z * dgu[rs, :]).astype(BF16)
                dz = g[rs, :] * gu[rs, :]
                dzb = dz.astype(BF16)
                dvn_parts.append(_nn(wmt, dzb))
                dw_acc = dw_acc + _nt(dzb, vnb[rs, :])
                b_acc = b_acc + dz
            dvn = jnp.concatenate(dvn_parts, axis=0)
            dw_ref[hd] += jnp.where(tril, dw_acc, 0.0)
            bacc_ref[hd] += b_acc
            dvec_ref[0:1, sl] += jnp.sum(dvn * xhat, axis=0, keepdims=True)
            dvec_ref[1:2, sl] += jnp.sum(dvn, axis=0, keepdims=True)
            dxh = dvn * lng_h
            dgv_in = rstd * (dxh - jnp.mean(dxh, axis=-1, keepdims=True)
                             - xhat * jnp.mean(dxh * xhat, axis=-1, keepdims=True))
            duv_ref[:, MIX_HALF + hd * HEAD_LANES:MIX_HALF + (hd + 1) * HEAD_LANES] = (dgv_in * dgv).astype(BF16)

        @pl.when(step == nsteps - 1)
        def _():
            for hd in range(SGU_HEADS):
                sl = slice(hd * HEAD_LANES, (hd + 1) * HEAD_LANES)
                dvec_ref[2:3, sl] = jnp.sum(bacc_ref[hd].T, axis=0, keepdims=True)

    return pl.pallas_call(
        body, name=name, grid=(nsteps,),
        in_specs=[pl.BlockSpec((tm, MIX_HALF), lambda i: (i, 0)),
                  pl.BlockSpec((tm, MIX_HALF), lambda i: (i, 1)),
                  pl.BlockSpec((tm, MIX_HALF), lambda i: (i, 0)),
                  pl.BlockSpec((1, MIX_HALF), lambda i: (0, 0)),
                  pl.BlockSpec((1, MIX_HALF), lambda i: (0, 0)),
                  pl.BlockSpec(w.shape, lambda i: (0, 0, 0)),
                  pl.BlockSpec(w.shape, lambda i: (0, 0, 0)),
                  pl.BlockSpec(bcol.shape, lambda i: (0, 0))],
        out_specs=[pl.BlockSpec((tm, 2 * MIX_HALF), lambda i: (i, 0)),
                   pl.BlockSpec(w.shape, lambda i: (0, 0, 0)),
                   pl.BlockSpec((8, MIX_HALF), lambda i: (0, 0))],
        out_shape=[jax.ShapeDtypeStruct((T, 2 * MIX_HALF), BF16),
                   jax.ShapeDtypeStruct(w.shape, F32),
                   jax.ShapeDtypeStruct((8, MIX_HALF), F32)],
        scratch_shapes=[pltpu.VMEM((SGU_HEADS, ATT_BLOCK, HEAD_LANES), F32)],
        compiler_params=_params(("arbitrary",)),
    )(proj, proj, dmixed, lng, lnb, w, wt, bcol)


def _rot_half(t):
    lane = lax.broadcasted_iota(jnp.int32, t.shape, 1)
    first = (lane % ATT_DH) < (ATT_DH // 2)
    return jnp.where(first, -pltpu.roll(t, HEAD_LANES - ATT_DH // 2, 1), pltpu.roll(t, ATT_DH // 2, 1))


LAYOUT_ROWS = 512


def _res_spec(d, tm, W):
    return pl.BlockSpec((d, tm // d, W), lambda i: (0, i, 0))


def _res_shape(d, T, W, dtype):
    return jax.ShapeDtypeStruct((d, T // d, W), dtype)


def _slab_buf(tm, W):
    return pltpu.VMEM((W // HEAD_LANES, tm, HEAD_LANES), F32)


def _lanes(hp):
    return slice(hp * HEAD_LANES, (hp + 1) * HEAD_LANES)


def _to_res(buf, out_ref, d, dtype):
    nslab, tm, _ = buf.shape
    for hp in range(nslab):
        if d == 1:
            out_ref[0, :, _lanes(hp)] = buf[hp].astype(dtype)
        else:
            for r in range(d):
                out_ref[r, :, _lanes(hp)] = buf.at[hp][pl.ds(r, tm // d, stride=d), :].astype(dtype)


def _from_res(in_ref, buf, d):
    nslab, tm, _ = buf.shape
    for hp in range(nslab):
        if d == 1:
            buf[hp] = in_ref[0, :, _lanes(hp)]
        else:
            for r in range(d):
                buf.at[hp][pl.ds(r, tm // d, stride=d), :] = in_ref[r, :, _lanes(hp)]


def rope_fwd(proj, cos, sin, name):
    T = proj.shape[0]
    tm = LAYOUT_ROWS
    scale = 1.0 / math.sqrt(ATT_DH)
    nd = len(DILATIONS)

    def body(q_ref, k_ref, v_ref, cos_ref, sin_ref, *rest):
        outs, buf = rest[:3 * nd], rest[3 * nd]
        c = cos_ref[...]
        s = sin_ref[...]
        for which, src in enumerate((q_ref, k_ref, v_ref)):
            for hp in range(MIX_HALF // HEAD_LANES):
                t = src[:, _lanes(hp)].astype(F32)
                if which == 0:
                    t = scale * (t * c + _rot_half(t) * s)
                elif which == 1:
                    t = t * c + _rot_half(t) * s
                buf[hp] = t
            for di, d in enumerate(DILATIONS):
                _to_res(buf, outs[3 * di + which], d, BF16)

    return pl.pallas_call(
        body, name=name, grid=(T // tm,),
        in_specs=[pl.BlockSpec((tm, MIX_HALF), lambda i: (i, 2)),
                  pl.BlockSpec((tm, MIX_HALF), lambda i: (i, 3)),
                  pl.BlockSpec((tm, MIX_HALF), lambda i: (i, 4)),
                  pl.BlockSpec((tm, HEAD_LANES), lambda i: (i, 0)),
                  pl.BlockSpec((tm, HEAD_LANES), lambda i: (i, 0))],
        out_specs=[_res_spec(d, tm, MIX_HALF) for d in DILATIONS for _ in range(3)],
        out_shape=[_res_shape(d, T, MIX_HALF, BF16) for d in DILATIONS for _ in range(3)],
        scratch_shapes=[_slab_buf(tm, MIX_HALF)],
        compiler_params=_params(("parallel",)),
    )(proj, proj, proj, cos, sin)


def to_residues(x, col, name):
    T = x.shape[0]
    tm = LAYOUT_ROWS

    def body(x_ref, *rest):
        outs, buf = rest[:-1], rest[-1]
        for hp in range(MIX_HALF // HEAD_LANES):
            buf[hp] = x_ref[:, _lanes(hp)].astype(F32)
        for o_ref, d in zip(outs, DILATIONS):
            _to_res(buf, o_ref, d, BF16)

    return pl.pallas_call(
        body, name=name, grid=(T // tm,),
        in_specs=[pl.BlockSpec((tm, MIX_HALF), lambda i: (i, col))],
        out_specs=[_res_spec(d, tm, MIX_HALF) for d in DILATIONS],
        out_shape=[_res_shape(d, T, MIX_HALF, BF16) for d in DILATIONS],
        scratch_shapes=[_slab_buf(tm, MIX_HALF)],
        compiler_params=_params(("parallel",)),
    )(x)


def rope_bwd(dqs, dks, dvs, cos, sin, name):
    T = dqs[0].shape[0] * dqs[0].shape[1]
    tm = LAYOUT_ROWS
    scale = 1.0 / math.sqrt(ATT_DH)
    npat = len(dqs)

    def body(*refs):
        groups = refs[:npat], refs[npat:2 * npat], refs[2 * npat:3 * npat]
        cos_ref, sin_ref, o_ref, buf, acc = refs[3 * npat:]
        c = cos_ref[...]
        s = sin_ref[...]
        for which, g_refs in enumerate(groups):
            _from_res(g_refs[0], acc, DILATIONS[0])
            for g_ref, d in zip(g_refs[1:], DILATIONS[1:]):
                _from_res(g_ref, buf, d)
                acc[...] += buf[...]
            for hp in range(MIX_HALF // HEAD_LANES):
                g = acc[hp]
                if which == 0:
                    g = scale * g
                if which < 2:
                    g = g * c - _rot_half(g * s)
                o_ref[:, which * MIX_HALF + hp * HEAD_LANES:which * MIX_HALF + (hp + 1) * HEAD_LANES] = g.astype(BF16)

    return pl.pallas_call(
        body, name=name, grid=(T // tm,),
        in_specs=([_res_spec(d, tm, MIX_HALF) for _ in range(3) for d in DILATIONS]
                  + [pl.BlockSpec((tm, HEAD_LANES), lambda i: (i, 0))] * 2),
        out_specs=pl.BlockSpec((tm, 3 * MIX_HALF), lambda i: (i, 0)),
        out_shape=jax.ShapeDtypeStruct((T, 3 * MIX_HALF), BF16),
        scratch_shapes=[_slab_buf(tm, MIX_HALF), _slab_buf(tm, MIX_HALF)],
        compiler_params=_params(("parallel",)),
    )(*dqs, *dks, *dvs, cos, sin)


def _band_masks(n):
    r = lax.broadcasted_iota(jnp.int32, (2 * ATT_BLOCK, ATT_BLOCK), 0)
    c = lax.broadcasted_iota(jnp.int32, (2 * ATT_BLOCK, ATT_BLOCK), 1)
    qi = r % ATT_BLOCK
    head = (c < ATT_DH) == (r < ATT_BLOCK)
    return (c >= qi) & (n > 0), c <= qi, head, c[:ATT_BLOCK] < ATT_DH


def _stack_heads(x, head):
    x2 = jnp.concatenate([x, x], axis=0)
    return jnp.where(head, x2, jnp.zeros_like(x2))


def attn_fwd(q, k, v, name):
    d, L, W = q.shape
    nb = L // ATT_BLOCK

    def body(q_ref, kp_ref, kc_ref, vp_ref, vc_ref, o_ref, lse_ref):
        mask_p, mask_c, head, head0 = _band_masks(pl.program_id(1))
        for hp in range(W // HEAD_LANES):
            sl = slice(hp * HEAD_LANES, (hp + 1) * HEAD_LANES)
            kp, kc, vp, vc = kp_ref[0, :, sl], kc_ref[0, :, sl], vp_ref[0, :, sl], vc_ref[0, :, sl]
            qs = _stack_heads(q_ref[0, :, sl], head)
            sp = jnp.where(mask_p, _nt(qs, kp), NEG)
            sc = jnp.where(mask_c, _nt(qs, kc), NEG)
            m = jnp.maximum(jnp.max(sp, axis=1, keepdims=True), jnp.max(sc, axis=1, keepdims=True))
            pp = jnp.exp(sp - m)
            pc = jnp.exp(sc - m)
            den = jnp.sum(pp, axis=1, keepdims=True) + jnp.sum(pc, axis=1, keepdims=True)
            o = (_nn(pp.astype(BF16), vp) + _nn(pc.astype(BF16), vc)) / den
            lse = m + jnp.log(den)
            o_ref[0, :, sl] = jnp.where(head0, o[:ATT_BLOCK], o[ATT_BLOCK:])
            lse_ref[0, :, sl] = jnp.where(head0, lse[:ATT_BLOCK], lse[ATT_BLOCK:])

    cur = pl.BlockSpec((1, ATT_BLOCK, W), lambda r, n: (r, n, 0))
    prev = pl.BlockSpec((1, ATT_BLOCK, W), lambda r, n: (r, jnp.maximum(n - 1, 0), 0))
    out = jax.ShapeDtypeStruct((d, L, W), F32)
    return pl.pallas_call(
        body, name=name, grid=(d, nb),
        in_specs=[cur, prev, cur, prev, cur],
        out_specs=[cur, cur], out_shape=[out, out],
        compiler_params=_params(("parallel", "parallel")),
    )(q, k, k, v, v)


def attn_combine(os_, lses, name):
    T = os_[0].shape[0] * os_[0].shape[1]
    W = os_[0].shape[2]
    tm = LAYOUT_ROWS
    npat = len(os_)

    def body(*refs):
        o_refs, l_refs = refs[:npat], refs[npat:2 * npat]
        out_ref = refs[2 * npat]
        ores, lres = refs[2 * npat + 1:3 * npat + 1], refs[3 * npat + 1:4 * npat + 1]
        bufs = refs[4 * npat + 1:]
        lbufs, obufs, out_buf, lse_buf = bufs[:npat], bufs[npat:2 * npat], bufs[2 * npat], bufs[2 * npat + 1]
        for p, d in enumerate(DILATIONS):
            _from_res(l_refs[p], lbufs[p], d)
            _from_res(o_refs[p], obufs[p], d)
        for hp in range(W // HEAD_LANES):
            ls = [b[hp] for b in lbufs]
            m = ls[0]
            for l in ls[1:]:
                m = jnp.maximum(m, l)
            es = [jnp.exp(l - m) for l in ls]
            z = es[0]
            for e in es[1:]:
                z = z + e
            acc = es[0] * obufs[0][hp]
            for p in range(1, npat):
                acc = acc + es[p] * obufs[p][hp]
            out = acc / z
            out_ref[:, _lanes(hp)] = out.astype(BF16)
            out_buf[hp] = out
            lse_buf[hp] = m + jnp.log(z)
        for p, d in enumerate(DILATIONS):
            _to_res(out_buf, ores[p], d, BF16)
            _to_res(lse_buf, lres[p], d, F32)

    return pl.pallas_call(
        body, name=name, grid=(T // tm,),
        in_specs=[_res_spec(d, tm, W) for _ in range(2) for d in DILATIONS],
        out_specs=([pl.BlockSpec((tm, W), lambda i: (i, 0))] + [_res_spec(d, tm, W) for _ in range(2) for d in DILATIONS]),
        out_shape=([jax.ShapeDtypeStruct((T, W), BF16)] + [_res_shape(d, T, W, BF16) for d in DILATIONS]
                   + [_res_shape(d, T, W, F32) for d in DILATIONS]),
        scratch_shapes=[_slab_buf(tm, W)] * (2 * npat + 2),
        compiler_params=_params(("parallel",)),
    )(*os_, *lses)


def attn_bwd(q, k, v, do, o, lse, name):
    d, L, W = q.shape
    nb = L // ATT_BLOCK

    def body(q_ref, kp_ref, kc_ref, vp_ref, vc_ref, do_ref, o_ref, lse_ref, dq_ref, dk_ref, dv_ref, kkeep, vkeep):
        n = pl.program_id(1)

        @pl.when(n == 0)
        def _():
            kkeep[...] = jnp.zeros_like(kkeep)
            vkeep[...] = jnp.zeros_like(vkeep)

        @pl.when(n < nb)
        def _():
            mask_p, mask_c, head, head0 = _band_masks(n)
            for hp in range(W // HEAD_LANES):
                sl = slice(hp * HEAD_LANES, (hp + 1) * HEAD_LANES)
                kp, kc, vp, vc = kp_ref[0, :, sl], kc_ref[0, :, sl], vp_ref[0, :, sl], vc_ref[0, :, sl]
                dout = do_ref[0, :, sl]
                qs = _stack_heads(q_ref[0, :, sl], head)
                dos = _stack_heads(dout, head)
                lse_v = lse_ref[0, :, sl]
                lse_c = jnp.max(jnp.where(head, jnp.concatenate([lse_v, lse_v], axis=0), NEG), axis=1, keepdims=True)
                delta = jnp.sum(_stack_heads(dout.astype(F32) * o_ref[0, :, sl].astype(F32), head), axis=1, keepdims=True)
                pp = jnp.exp(jnp.where(mask_p, _nt(qs, kp), NEG) - lse_c)
                pc = jnp.exp(jnp.where(mask_c, _nt(qs, kc), NEG) - lse_c)
                dsp = (pp * (_nt(dos, vp) - delta)).astype(BF16)
                dsc = (pc * (_nt(dos, vc) - delta)).astype(BF16)
                dq2 = _nn(dsp, kp) + _nn(dsc, kc)
                dq_ref[0, :, sl] = jnp.where(head0, dq2[:ATT_BLOCK], dq2[ATT_BLOCK:])
                dk_ref[0, :, sl] = kkeep[:, sl] + _tn(dsp, qs)
                dv_ref[0, :, sl] = vkeep[:, sl] + _tn(pp.astype(BF16), dos)
                kkeep[:, sl] = _tn(dsc, qs)
                vkeep[:, sl] = _tn(pc.astype(BF16), dos)

        @pl.when(n == nb)
        def _():
            dk_ref[0] = kkeep[...]
            dv_ref[0] = vkeep[...]

    cur = pl.BlockSpec((1, ATT_BLOCK, W), lambda r, n: (r, jnp.minimum(n, nb - 1), 0))
    prev = pl.BlockSpec((1, ATT_BLOCK, W), lambda r, n: (r, jnp.clip(n - 1, 0, nb - 1), 0))
    out = jax.ShapeDtypeStruct((d, L, W), F32)
    return pl.pallas_call(
        body, name=name, grid=(d, nb + 1),
        in_specs=[cur, prev, cur, prev, cur, cur, cur, cur],
        out_specs=[cur, prev, prev], out_shape=[out, out, out],
        scratch_shapes=[pltpu.VMEM((ATT_BLOCK, W), F32), pltpu.VMEM((ATT_BLOCK, W), F32)],
        compiler_params=_params(("parallel", "arbitrary")),
    )(q, k, k, v, v, do, o, lse)


def final_loss_bwd(h, gf, tgt, name):
    T, D = h.shape
    tm = _tile(T, 512, 8)

    def body(h_ref, g_ref, t_ref, dh_ref, red_ref):
        x = h_ref[...]
        r = lax.rsqrt(jnp.mean(x * x, axis=-1, keepdims=True) + EPS)
        n = x * r
        g = g_ref[...]
        err = n * g - t_ref[...]
        dy = err * (1.0 / D)

        @pl.when(pl.program_id(0) == 0)
        def _():
            red_ref[...] = jnp.zeros_like(red_ref)

        red_ref[0:1, :] += jnp.sum(dy * n, axis=0, keepdims=True)
        red_ref[1:2, :] += jnp.zeros((1, D), F32) + (0.5 / D) * jnp.sum(err * err, keepdims=True)
        dn = dy * g
        dh_ref[...] = r * (dn - n * jnp.mean(dn * n, axis=-1, keepdims=True))

    return pl.pallas_call(
        body, name=name, grid=(T // tm,),
        in_specs=[pl.BlockSpec((tm, D), lambda i: (i, 0)),
                  pl.BlockSpec((1, D), lambda i: (0, 0)),
                  pl.BlockSpec((tm, D), lambda i: (i, 0))],
        out_specs=[pl.BlockSpec((tm, D), lambda i: (i, 0)), pl.BlockSpec((8, D), lambda i: (0, 0))],
        out_shape=[jax.ShapeDtypeStruct((T, D), F32), jax.ShapeDtypeStruct((8, D), F32)],
        compiler_params=_params(("arbitrary",)),
    )(h, gf, tgt)


def ada_fwd(c_all, ada_w, ada_b, name):
    nl, D, N = ada_w.shape

    def body(c_ref, w_ref, b_ref, o_ref):
        c = c_ref[...]
        o_ref[0] = _nn(c * _sigmoid(c), w_ref[0]) + b_ref[0]

    return pl.pallas_call(
        body, name=name, grid=(nl,),
        in_specs=[pl.BlockSpec((N_DEV, D), lambda l: (0, 0)),
                  pl.BlockSpec((1, D, N), lambda l: (l, 0, 0)),
                  pl.BlockSpec((1, 1, N), lambda l: (l, 0, 0))],
        out_specs=pl.BlockSpec((1, N_DEV, N), lambda l: (l, 0, 0)),
        out_shape=jax.ShapeDtypeStruct((nl, N_DEV, N), F32),
        compiler_params=_params(("parallel",)),
    )(c_all, ada_w, ada_b)


def ada_bwd(c_allT, dmod, name):
    nl, _, N = dmod.shape
    D = c_allT.shape[0]

    def body(c_ref, g_ref, o_ref):
        c = c_ref[...]
        ca = c * _sigmoid(c)
        acc = ca[:, 0:1] * g_ref[0, 0:1, :]
        for b in range(1, N_DEV):
            acc = acc + ca[:, b:b + 1] * g_ref[0, b:b + 1, :]
        o_ref[0] = acc

    return pl.pallas_call(
        body, name=name, grid=(nl,),
        in_specs=[pl.BlockSpec((D, N_DEV), lambda l: (0, 0)),
                  pl.BlockSpec((1, N_DEV, N), lambda l: (l, 0, 0))],
        out_specs=pl.BlockSpec((1, D, N), lambda l: (l, 0, 0)),
        out_shape=jax.ShapeDtypeStruct((nl, D, N), F32),
        compiler_params=_params(("parallel",)),
    )(c_allT, dmod)


def adamw(w, g, m, v, name):
    R, C = w.shape
    tr = _tile(R, max(8, (1 << 19) // C // 8 * 8), 8)
    c1 = 1.0 - ADAM_B1 ** ADAM_STEP
    c2 = 1.0 - ADAM_B2 ** ADAM_STEP

    def body(w_ref, g_ref, m_ref, v_ref, d_ref, mo_ref, vo_ref):
        gv = g_ref[...]
        mn = ADAM_B1 * m_ref[...] + (1.0 - ADAM_B1) * gv
        vn = ADAM_B2 * v_ref[...] + (1.0 - ADAM_B2) * (gv * gv)
        mo_ref[...] = mn
        vo_ref[...] = vn
        d_ref[...] = -ADAM_LR * ((mn / c1) / (jnp.sqrt(vn / c2) + ADAM_EPS) + ADAM_WD * w_ref[...])

    blk = pl.BlockSpec((tr, C), lambda i: (i, 0))
    out = jax.ShapeDtypeStruct((R, C), F32)
    return pl.pallas_call(
        body, name=name, grid=(R // tr,),
        in_specs=[blk] * 4, out_specs=[blk] * 3, out_shape=[out] * 3,
        compiler_params=_params(("parallel",)),
    )(w, g, m, v)


def adamw_layer(w, g, m, v, l, prev, name):
    NLw, R, C = w.shape
    tr = _tile(R, max(8, (1 << 19) // C // 8 * 8), 8)
    nrb = R // tr
    c1 = 1.0 - ADAM_B1 ** ADAM_STEP
    c2 = 1.0 - ADAM_B2 ** ADAM_STEP
    w, m, v = (t.reshape(NLw * R, C) for t in (w, m, v))

    def body(w_ref, g_ref, m_ref, v_ref, *rest):
        go_ref, d_ref, mo_ref, vo_ref = rest[-4:]
        gv = g_ref[...]
        mn = ADAM_B1 * m_ref[...] + (1.0 - ADAM_B1) * gv
        vn = ADAM_B2 * v_ref[...] + (1.0 - ADAM_B2) * (gv * gv)
        go_ref[...] = gv
        mo_ref[...] = mn
        vo_ref[...] = vn
        d_ref[...] = -ADAM_LR * ((mn / c1) / (jnp.sqrt(vn / c2) + ADAM_EPS) + ADAM_WD * w_ref[...])

    lay = pl.BlockSpec((tr, C), lambda i: (l * nrb + i, 0))
    out = jax.ShapeDtypeStruct((NLw * R, C), F32)
    n_prev = 0 if prev is None else 4
    return pl.pallas_call(
        body, name=name, grid=(nrb,),
        in_specs=[lay, pl.BlockSpec((tr, C), lambda i: (i, 0)), lay, lay] + [pl.BlockSpec(memory_space=pl.ANY)] * n_prev,
        out_specs=[lay] * 4, out_shape=[out] * 4,
        input_output_aliases={4 + i: i for i in range(n_prev)},
        compiler_params=_params(("parallel",)),
    )(w, g, m, v, *(prev or ()))


def sum_slots(x, name):
    S, R, C = x.shape
    tr = _tile(R, 128, 8)

    def body(x_ref, o_ref):
        acc = x_ref[0]
        for s in range(1, S):
            acc = acc + x_ref[s]
        o_ref[...] = acc

    return pl.pallas_call(
        body, name=name, grid=(R // tr,),
        in_specs=[pl.BlockSpec((S, tr, C), lambda i: (0, i, 0))],
        out_specs=pl.BlockSpec((tr, C), lambda i: (i, 0)),
        out_shape=jax.ShapeDtypeStruct((R, C), F32),
        compiler_params=_params(("parallel",)),
    )(x)


def sum_halves(g, lands, c_idx, name):
    n, ns, _, rh, D = g.shape

    def body(c_ref, g_ref, l_ref, o_ref):
        o_ref[0, 0] = (g_ref[0, 0, 0].astype(F32) + l_ref[0, 0].astype(F32)).astype(BF16)

    return pl.pallas_call(
        body, name=name,
        grid_spec=pltpu.PrefetchScalarGridSpec(
            num_scalar_prefetch=1, grid=(n, ns),
            in_specs=[pl.BlockSpec((1, 1, 1, rh, D), lambda i, j, c: (i, j, c[0], 0, 0)),
                      pl.BlockSpec((1, 1, rh, D), lambda i, j, c: (i, j, 0, 0))],
            out_specs=pl.BlockSpec((1, 1, rh, D), lambda i, j, c: (i, j, 0, 0))),
        out_shape=jax.ShapeDtypeStruct((n, ns, rh, D), BF16),
        compiler_params=_params(("parallel", "parallel")),
    )(c_idx, g, lands)


def sum_chips(p, lands, place, name):
    n, ns, rh, D = p.shape

    def body(c_ref, p_ref, l_ref, o_ref):
        acc = p_ref[0, 0].astype(F32)
        for j in range(N_CHIP - 1):
            acc = acc + l_ref[j, 0].astype(F32)
        o_ref[0, 0] = acc

    return pl.pallas_call(
        body, name=name,
        grid_spec=pltpu.PrefetchScalarGridSpec(
            num_scalar_prefetch=1, grid=(n,),
            in_specs=[pl.BlockSpec((1, 1, rh, D), lambda i, c: (i, c[0], 0, 0)),
                      pl.BlockSpec((N_CHIP - 1, 1, rh, D), lambda i, c: (0, i, 0, 0))],
            out_specs=pl.BlockSpec((1, 1, rh, D), lambda i, c: (i, c[1], 0, 0))),
        out_shape=jax.ShapeDtypeStruct((n, 2, rh, D), F32),
        compiler_params=_params(("parallel",)),
    )(place, p, lands)


def _my_place():
    return lax.axis_index("x"), lax.axis_index("y"), lax.axis_index("c")


def _other_chips(mx, my):
    return [(1 - mx, my), (mx, 1 - my), (1 - mx, 1 - my)]


def gather_small(x, name):
    def body(x_ref, out_ref, sum_ref, send_sems, recv_sems):
        mx, my, mc = _my_place()
        me = 4 * mx + 2 * my + mc
        out_ref[me] = x_ref[...]
        sends = []
        for k in range(1, N_DEV):
            kx, ky, kc = (k >> 2) & 1, (k >> 1) & 1, k & 1
            peer = (1 - mx if kx else mx, 1 - my if ky else my, 1 - mc if kc else mc)
            cp = pltpu.make_async_remote_copy(
                src_ref=x_ref, dst_ref=out_ref.at[me], send_sem=send_sems.at[k - 1], recv_sem=recv_sems.at[k - 1],
                device_id=peer, device_id_type=MESH)
            cp.start()
            sends.append((cp, 4 * peer[0] + 2 * peer[1] + peer[2], peer))
        for k, (cp, peer_slot, peer) in enumerate(sends):
            pltpu.make_async_remote_copy(
                src_ref=x_ref, dst_ref=out_ref.at[peer_slot], send_sem=send_sems.at[k], recv_sem=recv_sems.at[k],
                device_id=peer, device_id_type=MESH).wait_recv()
        for cp, _, _ in sends:
            cp.wait_send()
        acc = out_ref[0]
        for s in range(1, N_DEV):
            acc = acc + out_ref[s]
        sum_ref[...] = acc

    vmem = pl.BlockSpec(memory_space=pltpu.VMEM)
    return pl.pallas_call(
        body, name=name,
        in_specs=[vmem], out_specs=[vmem, vmem],
        out_shape=[jax.ShapeDtypeStruct((N_DEV,) + x.shape, x.dtype), jax.ShapeDtypeStruct(x.shape, x.dtype)],
        scratch_shapes=[pltpu.SemaphoreType.DMA((N_DEV - 1,)), pltpu.SemaphoreType.DMA((N_DEV - 1,))],
        compiler_params=pltpu.CompilerParams(vmem_limit_bytes=VMEM_LIMIT),
    )(x)


_HBM =pl.BlockSpec(memory_space=pltpu.HBM)
_SEM = pl.BlockSpec(memory_space=pltpu.SEMAPHORE)
_DATAFLOW = pltpu.SideEffectType.DATAFLOW_SIDE_EFFECTING


def _gather_copies(shard, land, send, recv, base):
    mx, my, mc = _my_place()
    ci = 2 * mx + my
    peers = [((cx, cy, mc), 2 * cx + cy) for cx, cy in _other_chips(mx, my)] + [((mx, my, 1 - mc), ci)]
    out = []
    for q, (dev, src_slot) in enumerate(peers):
        out.append((
            pltpu.make_async_remote_copy(src_ref=shard, dst_ref=land.at[:, ci], send_sem=send.at[base + q],
                                         recv_sem=recv.at[base + q], device_id=dev, device_id_type=MESH),
            pltpu.make_async_remote_copy(src_ref=shard, dst_ref=land.at[:, src_slot], send_sem=send.at[base + q],
                                         recv_sem=recv.at[base + q], device_id=dev, device_id_type=MESH)))
    return out


def gather_start(groups, after, name):
    items = [s for g in groups for s in g]
    ni, ng = len(items), len(groups)

    def body(*refs):
        shards, lands = refs[:ni], refs[ni:2 * ni]
        sems = refs[2 * ni + 1:2 * ni + 1 + 2 * ng]
        token = refs[-1]
        i = 0
        for g, grp in enumerate(groups):
            for p in range(len(grp)):
                for start_cp, _ in _gather_copies(shards[i], lands[i], sems[2 * g], sems[2 * g + 1], 4 * p):
                    start_cp.start()
                i += 1
        token[...] = jnp.zeros_like(token)

    sem_shapes = []
    for grp in groups:
        sem_shapes += [pltpu.SemaphoreType.DMA((4 * len(grp),))] * 2
    land_shapes = [(s.shape[0], N_CHIP) + s.shape[1:] for s in items]
    outs = pl.pallas_call(
        body, name=name,
        in_specs=[_HBM] * (2 * ni) + [pl.BlockSpec(memory_space=pl.ANY)],
        out_specs=[_SEM] * (2 * ng) + [_HBM] * (2 * ni) + [pl.BlockSpec(memory_space=pltpu.VMEM)],
        out_shape=(sem_shapes + [pltpu.HBM(s.shape, s.dtype) for s in items]
                   + [pltpu.HBM(ls, s.dtype) for ls, s in zip(land_shapes, items)]
                   + [jax.ShapeDtypeStruct((8, 128), F32)]),
        input_output_aliases={i: 2 * ng + i for i in range(2 * ni)},
        compiler_params=pltpu.CompilerParams(has_side_effects=_DATAFLOW),
    )(*[pltpu.with_memory_space_constraint(s, pltpu.HBM) for s in items],
      *[pltpu.with_memory_space_constraint(lax.empty(ls, s.dtype), pltpu.HBM) for ls, s in zip(land_shapes, items)],
      after)
    sems, thru, token = outs[:2 * ng], outs[2 * ng:2 * ng + 2 * ni], outs[-1]
    handles, i = [], 0
    for g, grp in enumerate(groups):
        n = len(grp)
        handles.append((sems[2 * g], sems[2 * g + 1], thru[i:i + n], thru[ni + i:ni + i + n]))
        i += n
    return handles, token


def gather_wait(handle, after, name):
    send, recv, shards, lands = handle
    n = len(shards)

    def body(*refs):
        shard_refs, land_refs = refs[:n], refs[n:2 * n]
        send_ref, recv_ref = refs[2 * n], refs[2 * n + 1]
        for p in range(n):
            for start_cp, recv_cp in _gather_copies(shard_refs[p], land_refs[p], send_ref, recv_ref, 4 * p):
                start_cp.wait_send()
                recv_cp.wait_recv()

    outs = pl.pallas_call(
        body, name=name,
        in_specs=[_HBM] * (2 * n) + [_SEM, _SEM, pl.BlockSpec(memory_space=pl.ANY)],
        out_specs=[_HBM] * (2 * n),
        out_shape=[pltpu.HBM(s.shape, s.dtype) for s in shards] + [pltpu.HBM(l.shape, l.dtype) for l in lands],
        input_output_aliases={i: i for i in range(2 * n)},
        compiler_params=pltpu.CompilerParams(has_side_effects=_DATAFLOW),
    )(*shards, *lands, send, recv, after)
    return outs[n:]


def _sibling_copies(gs, lands, send, recv):
    mx, my, mc = _my_place()
    return [pltpu.make_async_remote_copy(
        src_ref=gs[k].at[:, :, 1 - mc], dst_ref=lands[k], send_sem=send.at[k], recv_sem=recv.at[k],
        device_id=(mx, my, 1 - mc), device_id_type=MESH) for k in range(len(gs))]


def sibling_start(gs, after, name):
    K = len(gs)

    def body(*refs):
        ins, lands = refs[:K], refs[K:2 * K]
        send, recv = refs[2 * K + 1], refs[2 * K + 2]
        for cp in _sibling_copies(ins, lands, send, recv):
            cp.start()
        refs[-1][...] = jnp.zeros_like(refs[-1])

    land_shapes = [g.shape[:2] + g.shape[3:] for g in gs]
    outs = pl.pallas_call(
        body, name=name,
        in_specs=[_HBM] * (2 * K) + [pl.BlockSpec(memory_space=pl.ANY)],
        out_specs=[_SEM, _SEM] + [_HBM] * (2 * K) + [pl.BlockSpec(memory_space=pltpu.VMEM)],
        out_shape=([pltpu.SemaphoreType.DMA((K,))] * 2 + [pltpu.HBM(g.shape, g.dtype) for g in gs]
                   + [pltpu.HBM(ls, g.dtype) for ls, g in zip(land_shapes, gs)] + [jax.ShapeDtypeStruct((8, 128), F32)]),
        input_output_aliases={i: 2 + i for i in range(2 * K)},
        compiler_params=pltpu.CompilerParams(has_side_effects=_DATAFLOW),
    )(*[pltpu.with_memory_space_constraint(g, pltpu.HBM) for g in gs],
      *[pltpu.with_memory_space_constraint(lax.empty(ls, g.dtype), pltpu.HBM) for ls, g in zip(land_shapes, gs)],
      after)
    return (outs[0], outs[1], outs[2:2 + K], outs[2 + K:2 + 2 * K]), outs[-1]


def sibling_wait(handle, after, name):
    send, recv, gs, lands = handle
    K = len(gs)

    def body(*refs):
        ins, land_refs = refs[:K], refs[K:2 * K]
        for cp in _sibling_copies(ins, land_refs, refs[2 * K], refs[2 * K + 1]):
            cp.wait_send()
            cp.wait_recv()

    outs = pl.pallas_call(
        body, name=name,
        in_specs=[_HBM] * (2 * K) + [_SEM, _SEM, pl.BlockSpec(memory_space=pl.ANY)],
        out_specs=[_HBM] * (2 * K),
        out_shape=[pltpu.HBM(g.shape, g.dtype) for g in gs] + [pltpu.HBM(l.shape, l.dtype) for l in lands],
        input_output_aliases={i: i for i in range(2 * K)},
        compiler_params=pltpu.CompilerParams(has_side_effects=_DATAFLOW),
    )(*gs, *lands, send, recv, after)
    return outs[:K], outs[K:]


def _small_copies(x, land, send, recv):
    mx, my, mc = _my_place()
    me = 4 * mx + 2 * my + mc
    out = []
    for k in range(1, N_DEV):
        peer = (1 - mx if k & 4 else mx, 1 - my if k & 2 else my, 1 - mc if k & 1 else mc)
        slot = 4 * peer[0] + 2 * peer[1] + peer[2]
        out.append(tuple(pltpu.make_async_remote_copy(
            src_ref=x, dst_ref=land.at[s], send_sem=send.at[k - 1], recv_sem=recv.at[k - 1],
            device_id=peer, device_id_type=MESH) for s in (me, slot)))
    return out


def small_start(x, after, name):
    def body(x_ref, land_ref, after_ref, send, recv, x_thru, land_thru, token):
        for mine, _ in _small_copies(x_ref, land_ref, send, recv):
            mine.start()
        token[...] = jnp.zeros_like(token)

    land_shape = (N_DEV,) + x.shape
    outs = pl.pallas_call(
        body, name=name,
        in_specs=[_HBM, _HBM, pl.BlockSpec(memory_space=pl.ANY)],
        out_specs=[_SEM, _SEM, _HBM, _HBM, pl.BlockSpec(memory_space=pltpu.VMEM)],
        out_shape=[pltpu.SemaphoreType.DMA((N_DEV - 1,))] * 2 + [pltpu.HBM(x.shape, x.dtype), pltpu.HBM(land_shape, x.dtype),
                                                                 jax.ShapeDtypeStruct((8, 128), F32)],
        input_output_aliases={0: 2, 1: 3},
        compiler_params=pltpu.CompilerParams(has_side_effects=_DATAFLOW),
    )(pltpu.with_memory_space_constraint(x, pltpu.HBM),
      pltpu.with_memory_space_constraint(lax.empty(land_shape, x.dtype), pltpu.HBM), after)
    return outs[:4], outs[4]


def small_wait(handle, after, name):
    send, recv, x, land = handle

    def body(x_ref, land_ref, send_ref, recv_ref, after_ref, x_out, land_out):
        for mine, theirs in _small_copies(x_ref, land_ref, send_ref, recv_ref):
            mine.wait_send()
            theirs.wait_recv()

    return pl.pallas_call(
        body, name=name,
        in_specs=[_HBM, _HBM, _SEM, _SEM, pl.BlockSpec(memory_space=pl.ANY)],
        out_specs=[_HBM, _HBM],
        out_shape=[pltpu.HBM(x.shape, x.dtype), pltpu.HBM(land.shape, land.dtype)],
        input_output_aliases={0: 0, 1: 1},
        compiler_params=pltpu.CompilerParams(has_side_effects=_DATAFLOW),
    )(x, land, send, recv, after)


def _scatter_copies(ps, lands, send, recv):
    mx, my, mc = _my_place()
    cps = []
    for j, (cx, cy) in enumerate(_other_chips(mx, my)):
        for k in range(len(ps)):
            cps.append(pltpu.make_async_remote_copy(
                src_ref=ps[k].at[:, 2 * cx + cy], dst_ref=lands[k].at[j],
                send_sem=send.at[k * 3 + j], recv_sem=recv.at[k * 3 + j],
                device_id=(cx, cy, mc), device_id_type=MESH))
    return cps


def scatter_start(ps, after, name):
    K = len(ps)

    def body(*refs):
        ins, lands = refs[:K], refs[K:2 * K]
        send, recv = refs[2 * K + 1], refs[2 * K + 2]
        for cp in _scatter_copies(ins, lands, send, recv):
            cp.start()
        refs[-1][...] = jnp.zeros_like(refs[-1])

    land_shapes = [(N_CHIP - 1, p.shape[0]) + p.shape[2:] for p in ps]
    outs = pl.pallas_call(
        body, name=name,
        in_specs=[_HBM] * (2 * K) + [pl.BlockSpec(memory_space=pl.ANY)],
        out_specs=[_SEM, _SEM] + [_HBM] * (2 * K) + [pl.BlockSpec(memory_space=pltpu.VMEM)],
        out_shape=([pltpu.SemaphoreType.DMA((3 * K,))] * 2 + [pltpu.HBM(p.shape, p.dtype) for p in ps]
                   + [pltpu.HBM(ls, p.dtype) for ls, p in zip(land_shapes, ps)] + [jax.ShapeDtypeStruct((8, 128), F32)]),
        input_output_aliases={i: 2 + i for i in range(2 * K)},
        compiler_params=pltpu.CompilerParams(has_side_effects=_DATAFLOW),
    )(*[pltpu.with_memory_space_constraint(p, pltpu.HBM) for p in ps],
      *[pltpu.with_memory_space_constraint(lax.empty(ls, p.dtype), pltpu.HBM) for ls, p in zip(land_shapes, ps)],
      after)
    return (outs[0], outs[1], outs[2:2 + K], outs[2 + K:2 + 2 * K]), outs[-1]


def scatter_wait(handle, after, name):
    send, recv, ps, lands = handle
    K = len(ps)

    def body(*refs):
        ins, land_refs = refs[:K], refs[K:2 * K]
        send_ref, recv_ref = refs[2 * K], refs[2 * K + 1]
        for cp in _scatter_copies(ins, land_refs, send_ref, recv_ref):
            cp.wait_send()
            cp.wait_recv()

    outs = pl.pallas_call(
        body, name=name,
        in_specs=[_HBM] * (2 * K) + [_SEM, _SEM, pl.BlockSpec(memory_space=pl.ANY)],
        out_specs=[_HBM] * (2 * K),
        out_shape=[pltpu.HBM(p.shape, p.dtype) for p in ps] + [pltpu.HBM(l.shape, l.dtype) for l in lands],
        input_output_aliases={i: i for i in range(2 * K)},
        compiler_params=pltpu.CompilerParams(has_side_effects=_DATAFLOW),
    )(*ps, *lands, send, recv, after)
    return outs[:K], outs[K:]


def sibling_complete(ss, name):
    K = len(ss)

    def body(*refs):
        ins, outs = refs[:K], refs[K:2 * K]
        send, recv = refs[2 * K:]
        mx, my, mc = _my_place()
        cps = []
        for k in range(K):
            cp = pltpu.make_async_remote_copy(
                src_ref=ins[k].at[:, mc], dst_ref=outs[k].at[:, mc], send_sem=send.at[k], recv_sem=recv.at[k],
                device_id=(mx, my, 1 - mc), device_id_type=MESH)
            cp.start()
            cps.append(cp)
        for k in range(K):
            pltpu.make_async_remote_copy(
                src_ref=ins[k].at[:, mc], dst_ref=outs[k].at[:, 1 - mc], send_sem=send.at[k], recv_sem=recv.at[k],
                device_id=(mx, my, 1 - mc), device_id_type=MESH).wait_recv()
        for cp in cps:
            cp.wait_send()

    hbm = pl.BlockSpec(memory_space=pl.ANY)
    return pl.pallas_call(
        body, name=name,
        in_specs=[hbm] * K, out_specs=[hbm] * K,
        out_shape=[jax.ShapeDtypeStruct(s.shape, s.dtype) for s in ss],
        scratch_shapes=[pltpu.SemaphoreType.DMA((K,)), pltpu.SemaphoreType.DMA((K,))],
        input_output_aliases={k: k for k in range(K)},
    )(*ss)


def _rope_tables(T):
    inv = ROPE_THETA ** (-jnp.arange(0, ATT_DH, 2, dtype=F32) / ATT_DH)
    ang = jnp.arange(T, dtype=F32)[:, None] * inv[None, :]
    ang = jnp.concatenate([ang, ang, ang, ang], axis=-1)
    return jnp.cos(ang), jnp.sin(ang)


def _ffn_fwd(h, ng, i_n, mod, i0, get_up, get_down, tag):
    wgu = get_up(h)
    y = normmod_fwd(h, ng, i_n, mod, i0, i0 + 1, f"normmod_{tag}")
    a, b, s = ffn_up(y, (wgu, (0,)), (wgu, (1,)), f"ffn_up_{tag}")
    wd = get_down(s)
    hn, o = resid_matmul([s], (wd, (0,)), h, mod, i0 + 2, 0.5, f"ffn_down_{tag}")
    return hn, (h, y, a, b, s, o), ((wgu, (0,)), (wgu, (1,)), (wd, (0,)))


def _ffn_bwd(dh, res, ng, i_n, mod, i0, wgT, wuT, wd, on_grads, tag):
    h, y, a, b, s, o = res
    F = _wrows(wgT)
    do, red_g = gate_bwd(dh, o, mod, i0 + 2, 0.5, f"gate_bwd_{tag}")
    da, db = ffn_bwd_mid(do, wd, a, b, f"ffn_bwd_mid_{tag}")
    gbuf = lax.empty((3, F, h.shape[1]), BF16)
    gbuf = matmul_tn(da, y, gbuf, 0, 0, f"dwg_{tag}")
    gbuf = matmul_tn(db, y, gbuf, 1, 0, f"dwu_{tag}")
    gbuf = matmul_tn(s, do, gbuf, 2, 0, f"dwd_{tag}")
    token, then = on_grads([gbuf])
    dh_new, red_n = dy_normbwd([(da, 0, wgT, 0, F), (db, 0, wuT, 0, F)], h, dh, ng, i_n, mod + token, i0 + 1,
                               f"ffn_bwd_dy_{tag}")
    return dh_new, red_n, red_g, then


def _mixer_fwd(h, ng, mod, w_inT, w_out, sgu, cos, sin, tag):
    lng, lnb, sw, swt, bcol = sgu
    y = normmod_fwd(h, ng, 1, mod, 3, 4, f"normmod_{tag}")
    proj = matmul_nt(y, w_inT, f"proj_{tag}")
    out_a = sgu_fwd(proj, lng, lnb, sw, bcol, f"sgu_fwd_{tag}")
    qkv = rope_fwd(proj, cos, sin, f"rope_fwd_{tag}")
    npat = len(DILATIONS)
    qkv_res = [tuple(qkv[3 * p:3 * p + 3]) for p in range(npat)]
    os_, lses = [], []
    for d, (qd, kd, vd) in zip(DILATIONS, qkv_res):
        o_d, lse_d = attn_fwd(qd, kd, vd, f"attn_fwd_d{d}_{tag}")
        os_.append(o_d)
        lses.append(lse_d)
    comb = attn_combine(os_, lses, f"attn_combine_{tag}")
    out_b, o_res, lse_res = comb[0], comb[1:1 + npat], comb[1 + npat:]
    hn, om = resid_matmul([out_a, out_b], w_out, h, mod, 5, 1.0, f"mix_out_{tag}")
    return hn, (h, y, proj, out_a, out_b, o_res, lse_res, qkv_res, om)


def _mixer_bwd(dh, res, ng, mod, w_inT, w_out, sgu, cos, sin, on_grads, tag):
    lng, lnb, sw, swt, bcol = sgu
    h, y, proj, out_a, out_b, o_res, lse_res, qkv_res, om = res
    D = h.shape[1]
    dom, red_g = gate_bwd(dh, om, mod, 5, 1.0, f"gate_bwd_{tag}")
    dmixed = matmul_nt(dom, w_out, f"dmixed_{tag}")
    woutbuf = lax.empty((1, 2 * MIX_HALF, D), BF16)
    woutbuf = matmul_tn(out_a, dom, woutbuf, 0, 0, f"dwout_a_{tag}", tmo_cap=MIX_HALF)
    woutbuf = matmul_tn(out_b, dom, woutbuf, 0, MIX_HALF, f"dwout_b_{tag}", tmo_cap=MIX_HALF)
    d_uv, d_sw, d_svec = sgu_bwd(proj, dmixed, lng, lnb, sw, swt, bcol, f"sgu_bwd_{tag}")
    do_res = to_residues(dmixed, 1, f"dout_res_{tag}")
    dqs, dks, dvs = [], [], []
    for p, (d, (qd, kd, vd)) in enumerate(zip(DILATIONS, qkv_res)):
        dq, dk, dv = attn_bwd(qd, kd, vd, do_res[p], o_res[p], lse_res[p], f"attn_bwd_d{d}_{tag}")
        dqs.append(dq)
        dks.append(dk)
        dvs.append(dv)
    d_qkv = rope_bwd(dqs, dks, dvs, cos, sin, f"rope_bwd_{tag}")
    winbuf = lax.empty((1, 5 * MIX_HALF, D), BF16)
    winbuf = matmul_tn(d_uv, y, winbuf, 0, 0, f"dwin_uv_{tag}", tmo_cap=MIX_HALF)
    winbuf = matmul_tn(d_qkv, y, winbuf, 0, 2 * MIX_HALF, f"dwin_qkv_{tag}", tmo_cap=MIX_HALF)
    token, then = on_grads([winbuf, woutbuf])
    pairs = ([(d_uv, p, w_inT, p, MIX_HALF) for p in range(2)]
             + [(d_qkv, p, w_inT, 2 + p, MIX_HALF) for p in range(3)])
    dh_new, red_n = dy_normbwd(pairs, h, dh, ng, 1, mod + token, 4, f"mix_bwd_dy_{tag}")
    return dh_new, d_sw, d_svec, red_n, red_g, then


def _local_step(x, tgt, mods, ngs, get_w, sgus, gf, on_block_grads, on_layer_small):
    T, D = x.shape
    cos, sin = _rope_tables(T)
    h = x
    saved, weights = [], []
    for l in range(2):
        def getter(blk, l=l):
            return lambda after: get_w(l, blk, after)

        h, r1, wf1 = _ffn_fwd(h, ngs[l], 0, mods[l], 0, getter("f1u"), getter("f1d"), f"l{l}f1")
        w_inT, w_out = get_w(l, "mx", h)
        h, r2 = _mixer_fwd(h, ngs[l], mods[l], (w_inT, (0,)), (w_out, (0,)), sgus[l], cos, sin, f"l{l}mx")
        h, r3, wf2 = _ffn_fwd(h, ngs[l], 2, mods[l], 6, getter("f2u"), getter("f2d"), f"l{l}f2")
        saved.append((r1, r2, r3))
        weights.append((wf1, w_inT, w_out, wf2))
    dh, red_final = final_loss_bwd(h, gf, tgt, "final_loss_bwd")
    for l in (1, 0):
        r1, r2, r3 = saved[l]
        wf1, w_inT, w_out, wf2 = weights[l]

        def on(blk, l=l):
            return lambda arrays: on_block_grads(l, blk, arrays)

        dh, rn3, rg3, then = _ffn_bwd(dh, r3, ngs[l], 2, mods[l], 6, *wf2, on("f2"), f"l{l}f2")
        mods = mods + then(dh)
        dh, d_sw, d_svec, rn2, rg2, then = _mixer_bwd(dh, r2, ngs[l], mods[l], (w_inT, (0,)), (w_out, (0,)),
                                                      sgus[l], cos, sin, on("mx"), f"l{l}mx")
        mods = mods + then(dh)
        dh, rn1, rg1, then = _ffn_bwd(dh, r1, ngs[l], 0, mods[l], 0, *wf1, on("f1"), f"l{l}f1")
        mods = mods + on_layer_small(l, dict(sgu_w=d_sw, sgu_vec=d_svec, red_n=(rn1, rn2, rn3), red_g=(rg1, rg2, rg3)),
                                     red_final if l == 0 else None)
        mods = mods + then(mods)
    return dh


def _adam_out(w, g, m, v, name):
    shp = w.shape
    two_d = (-1, shp[-1])
    d, mn, vn = adamw(w.reshape(two_d), g.reshape(two_d), m.reshape(two_d), v.reshape(two_d), name)
    return g, d.reshape(shp), mn.reshape(shp), vn.reshape(shp)


def kernel(x, c, ada_w, ada_b, norm_g, ffn1_wg, ffn1_wu, ffn1_wd, ffn2_wg, ffn2_wu, ffn2_wd, w_in, sgu_ln_g, sgu_ln_b, sgu_w, sgu_b, w_out, final_g, loss_target, m_ada_w, m_ada_b, m_norm_g, m_ffn1_wg, m_ffn1_wu, m_ffn1_wd, m_ffn2_wg, m_ffn2_wu, m_ffn2_wd, m_w_in, m_sgu_ln_g, m_sgu_ln_b, m_sgu_w, m_sgu_b, m_w_out, m_final_g, v_ada_w, v_ada_b, v_norm_g, v_ffn1_wg, v_ffn1_wu, v_ffn1_wd, v_ffn2_wg, v_ffn2_wu, v_ffn2_wd, v_w_in, v_sgu_ln_g, v_sgu_ln_b, v_sgu_w, v_sgu_b, v_w_out, v_final_g):
    T, D = x.shape[1], x.shape[2]
    NL = ada_w.shape[0]
    mx, my, mc = _my_place()
    me = 4 * mx + 2 * my + mc
    ci = 2 * mx + my
    c_idx = jnp.reshape(mc, (1,)).astype(jnp.int32)
    place = jnp.stack([ci, mc]).astype(jnp.int32)

    ngw = norm_g.shape[2]
    small_in = jnp.concatenate([jnp.pad(c, ((0, 7), (0, 0))),
                                jnp.pad(norm_g.reshape(NL * 3, ngw), ((0, 8 - NL * 3), (0, D - ngw)))], axis=0)
    small_all, _ = gather_small(small_in, "gather_c_normg")
    c_all = small_all[:, 0, :]
    ng_parts = small_all[0::2, 8:8 + NL * 3, :ngw]
    ngs = jnp.transpose(ng_parts, (1, 0, 2)).reshape(NL, 3, N_CHIP * ngw)

    nmod = ada_w.shape[2]
    ada_b_mine = lax.dynamic_slice_in_dim(ada_b, ci * nmod, nmod, axis=1).reshape(NL, 1, nmod)
    mod_part = ada_fwd(c_all, ada_w, ada_b_mine, "ada_fwd")
    mod_all, _ = gather_small(mod_part.reshape(NL * N_DEV, nmod), "gather_mod")
    mod_rows = lax.dynamic_index_in_dim(mod_all.reshape(N_DEV, NL, N_DEV, nmod), me, axis=2, keepdims=False)
    mods = jnp.transpose(mod_rows[0::2], (1, 0, 2)).reshape(NL, N_ADA, D)

    sgus = []
    for l in range(NL):
        sgus.append((sgu_ln_g[l].reshape(1, MIX_HALF), sgu_ln_b[l].reshape(1, MIX_HALF), sgu_w[l],
                     jnp.swapaxes(sgu_w[l], 1, 2), jnp.transpose(sgu_b[l])))

    def halves(a):
        n, r, _ = a.shape
        return a.reshape(n, 2, r // 2, D)

    Fs = ffn1_wd.shape[1]
    groups = []
    for l in range(NL):
        groups += [[halves(jnp.stack([ffn1_wg[l].T, ffn1_wu[l].T], axis=0).astype(BF16))],
                   [halves(ffn1_wd[l].astype(BF16)[None])],
                   [halves(w_in[l].T.astype(BF16)[None]), halves(w_out[l].astype(BF16)[None])],
                   [halves(jnp.stack([ffn2_wg[l].T, ffn2_wu[l].T], axis=0).astype(BF16))],
                   [halves(ffn2_wd[l].astype(BF16)[None])]]
    handles, token = gather_start(groups, mods, "gather_start")
    mods = mods + token[0, 0]
    group_no = {"f1u": 0, "f1d": 1, "mx": 2, "f2u": 3, "f2d": 4}

    def get_w(l, key, after):
        full = gather_wait(handles[len(group_no) * l + group_no[key]], after, f"gather_wait_l{l}{key}")
        full = [a.reshape(a.shape[0], N_CHIP * 2 * a.shape[3], D) for a in full]
        return full[0] if key != "mx" else tuple(full)

    def split(a):
        n, r4, _ = a.shape
        return a.reshape(n, N_CHIP, 2, r4 // N_CHIP // 2, D)

    pending, small_pending, small_tokens = {}, {}, {}

    def on_block_grads(l, blk, bufs):
        tag = f"l{l}{blk}"
        sib, tok1 = sibling_start([split(g) for g in bufs], place, f"rs_sibling_start_{tag}")

        def then(after):
            parts, lands = sibling_wait(sib, after, f"rs_sibling_wait_{tag}")
            psums = [sum_halves(g, ld, c_idx, f"rs_sum_halves_{tag}_{i}") for i, (g, ld) in enumerate(zip(parts, lands))]
            pending[(l, blk)], tok2 = scatter_start(psums, lands[0], f"rs_chips_start_{tag}")
            return tok2[0, 0]

        return tok1[0, 0], then

    def block_finish(l, blk, after):
        tag = f"l{l}{blk}"
        psums, lands2 = scatter_wait(pending.pop((l, blk)), after, f"rs_chips_wait_{tag}")
        ssums = [sum_chips(p, ld, place, f"rs_sum_chips_{tag}_{i}") for i, (p, ld) in enumerate(zip(psums, lands2))]
        return [f.reshape(f.shape[0], -1, D) for f in sibling_complete(ssums, f"rs_complete_{tag}")]

    def on_layer_small(l, grads, red_final):
        blocks = list(grads["red_n"]) + list(grads["red_g"])
        blocks.append(jnp.pad(grads["sgu_vec"], ((0, 0), (0, D - MIX_HALF))))
        blocks.append(grads["sgu_w"].reshape(-1, D))
        if red_final is not None:
            blocks.append(red_final)
        xs = jnp.concatenate(blocks, axis=0)
        small_pending[l], small_tokens[l] = small_start(xs, place, f"small_start_l{l}")
        return small_tokens[l][0, 0]

    grad_x = _local_step(x[0], loss_target[0], mods, ngs, get_w, sgus, final_g.reshape(1, D),
                         on_block_grads, on_layer_small)

    adam_state = {}

    def adam_big(nm, l, g, w, m, v):
        adam_state[nm] = adamw_layer(w, g, m, v, l, adam_state.get(nm), f"adamw_{nm}_l{l}")

    def adam_block(l, blk, fin):
        if blk == "mx":
            adam_big("w_in", l, fin[0][0].T, w_in, m_w_in, v_w_in)
            adam_big("w_out", l, fin[1][0], w_out, m_w_out, v_w_out)
        else:
            ws = ((ffn1_wg, m_ffn1_wg, v_ffn1_wg), (ffn1_wu, m_ffn1_wu, v_ffn1_wu), (ffn1_wd, m_ffn1_wd, v_ffn1_wd)) \
                if blk == "f1" else \
                ((ffn2_wg, m_ffn2_wg, v_ffn2_wg), (ffn2_wu, m_ffn2_wu, v_ffn2_wu), (ffn2_wd, m_ffn2_wd, v_ffn2_wd))
            pre = "ffn1" if blk == "f1" else "ffn2"
            for k, (nm, tr) in enumerate((("wg", True), ("wu", True), ("wd", False))):
                adam_big(f"{pre}_{nm}", l, fin[0][k].T if tr else fin[0][k], *ws[k])

    done_order = [(l, blk) for l in range(NL - 1, -1, -1) for blk in ("f2", "mx", "f1")]
    for l, blk in done_order[:-1]:
        adam_block(l, blk, block_finish(l, blk, small_tokens[0]))
    last_big = adam_state["w_out"][1]

    small_sum, small_all = [], []
    for l in range(NL):
        xs, land = small_wait(small_pending[l], last_big, f"small_wait_l{l}")
        full = lax.dynamic_update_slice(land, xs[None], (me, 0, 0))
        small_all.append(full)
        small_sum.append(sum_slots(full, f"small_sum_l{l}"))
    offs = [8 * i for i in range(8)]
    off_final = offs[7] + SGU_HEADS * ATT_BLOCK * HEAD_LANES // D
    loss = small_sum[0][off_final + 1, 0]
    g_final_g = small_sum[0][off_final, :]
    g_norm_g, g_ada_b, g_lng, g_lnb, g_sb, g_sw, dmod_all = [], [], [], [], [], [], []
    for l in range(NL):
        rn = [small_sum[l][offs[i]:offs[i] + 8] for i in range(3)]
        rg = [small_sum[l][offs[3 + i]:offs[3 + i] + 8] for i in range(3)]
        g_norm_g.append(jnp.stack([rn[i][2] for i in range(3)], axis=0))
        g_ada_b.append(jnp.concatenate([jnp.stack([rn[i][0], rn[i][1], rg[i][0]], axis=0) for i in range(3)],
                                       axis=0).reshape(N_ADA * D))
        sv = small_sum[l][offs[6]:offs[6] + 8, :MIX_HALF]
        g_lng.append(sv[0].reshape(SGU_HEADS, HEAD_LANES))
        g_lnb.append(sv[1].reshape(SGU_HEADS, HEAD_LANES))
        g_sb.append(sv[2].reshape(SGU_HEADS, ATT_BLOCK))
        g_sw.append(small_sum[l][offs[7]:off_final].reshape(sgu_w.shape[1:]))
        rows = []
        for i in range(3):
            an = small_all[l][:, offs[i]:offs[i] + 2]
            ag = small_all[l][:, offs[3 + i]:offs[3 + i] + 1]
            rows += [an[:, 0], an[:, 1], ag[:, 0]]
        dmod_all.append(jnp.stack(rows, axis=1).reshape(N_DEV, N_ADA * D))
    dmod_all = jnp.stack(dmod_all, axis=0)
    dmod_mine = lax.dynamic_slice_in_dim(dmod_all, ci * nmod, nmod, axis=2)
    g_ada_w = ada_bwd(jnp.transpose(c_all), dmod_mine, "ada_bwd")
    g_ada_b = jnp.stack(g_ada_b, axis=0)
    g_norm_g_full = jnp.stack(g_norm_g, axis=0)
    g_norm_g_mine = lax.dynamic_slice_in_dim(g_norm_g_full, ci * ngw, ngw, axis=2)

    small_params = [
        ("ada_w", ada_w, g_ada_w, m_ada_w, v_ada_w),
        ("ada_b", ada_b, g_ada_b, m_ada_b, v_ada_b),
        ("norm_g", norm_g, g_norm_g_mine, m_norm_g, v_norm_g),
        ("sgu_ln_g", sgu_ln_g, jnp.stack(g_lng, axis=0), m_sgu_ln_g, v_sgu_ln_g),
        ("sgu_ln_b", sgu_ln_b, jnp.stack(g_lnb, axis=0), m_sgu_ln_b, v_sgu_ln_b),
        ("sgu_w", sgu_w, jnp.stack(g_sw, axis=0), m_sgu_w, v_sgu_w),
        ("sgu_b", sgu_b, jnp.stack(g_sb, axis=0), m_sgu_b, v_sgu_b),
        ("final_g", final_g.reshape(1, D), g_final_g.reshape(1, D), m_final_g.reshape(1, D), v_final_g.reshape(1, D)),
    ]
    for nm, w, g, m, v in small_params:
        res = _adam_out(w, g, m, v, f"adamw_{nm}")
        adam_state[nm] = tuple(t.reshape(D) for t in res) if nm == "final_g" else res

    l, blk = done_order[-1]
    adam_block(l, blk, block_finish(l, blk, adam_state["ada_w"][1]))

    names = ["ada_w", "ada_b", "norm_g", "ffn1_wg", "ffn1_wu", "ffn1_wd", "ffn2_wg", "ffn2_wu", "ffn2_wd", "w_in",
             "sgu_ln_g", "sgu_ln_b", "sgu_w", "sgu_b", "w_out", "final_g"]
    shapes = [t.shape for t in (ada_w, ada_b, norm_g, ffn1_wg, ffn1_wu, ffn1_wd, ffn2_wg, ffn2_wu, ffn2_wd, w_in,
                                sgu_ln_g, sgu_ln_b, sgu_w, sgu_b, w_out, final_g)]
    return (loss, grad_x[None], *[adam_state[nm][i].reshape(s) for i in range(4) for nm, s in zip(names, shapes)])
```

```python
import math

import jax
import jax.numpy as jnp
from jax import lax
from jax.experimental import pallas as pl
from jax.experimental.pallas import tpu as pltpu

F32 = jnp.float32
BF16 = jnp.bfloat16
EPS = 1e-6
SGU_HEADS = 4
HEAD_LANES = 128
ATT_DH = 64
ATT_BLOCK = 128
MIX_HALF = SGU_HEADS * HEAD_LANES
DILATIONS = (1, 4, 16)
ROPE_THETA = 10000.0
N_ADA = 9
ADAM_LR, ADAM_B1, ADAM_B2, ADAM_EPS, ADAM_WD, ADAM_STEP = 0.001, 0.9, 0.999, 1e-08, 0.01, 10
NEG = -1e30
V7X_VMEM_BYTES = 64 * 1024 * 1024
VMEM_LIMIT = V7X_VMEM_BYTES * 7 // 8
MESH = pl.DeviceIdType.MESH
N_DEV = 8
N_CHIP = 4


def _tile(n, cap, mult):
    if n <= cap:
        return n
    t = (cap // mult) * mult
    while t >= mult:
        if n % t == 0:
            return t
        t -= mult
    raise ValueError((n, cap, mult))


def _params(dims=None):
    return pltpu.CompilerParams(dimension_semantics=dims, vmem_limit_bytes=VMEM_LIMIT)


def _wspec(w, rows, idx, resident=False):
    arr, lead = w
    kw = dict(pipeline_mode=pl.Buffered(1)) if resident else {}
    return pl.BlockSpec((None,) * len(lead) + (rows, arr.shape[-1]), lambda *g: tuple(lead) + (idx(*g), 0), **kw)


def _wrows(w):
    return w[0].shape[-2]


def _nt(a, b):
    return lax.dot_general(a, b, (((1,), (1,)), ((), ())), preferred_element_type=F32)


def _tn(a, b):
    return lax.dot_general(a, b, (((0,), (0,)), ((), ())), preferred_element_type=F32)


def _nn(a, b):
    return jnp.dot(a, b, preferred_element_type=F32)


def _sigmoid(x):
    return 0.5 * jnp.tanh(0.5 * x) + 0.5


_GELU_K = math.sqrt(2.0 / math.pi)
_GELU_C = 0.044715


def _gelu(x):
    t = jnp.tanh(_GELU_K * (x + _GELU_C * x * x * x))
    return 0.5 * x * (1.0 + t)


def _gelu_and_grad(x):
    x2 = x * x
    t = jnp.tanh(_GELU_K * (x + _GELU_C * x * x2))
    g = 0.5 * x * (1.0 + t)
    dg = 0.5 * (1.0 + t) + 0.5 * x * (1.0 - t * t) * (_GELU_K * (1.0 + 3.0 * _GELU_C * x2))
    return g, dg


def normmod_fwd(h, ng, i_n, mod, i_sh, i_sc, name):
    T, D = h.shape
    tm = _tile(T, 512, 8)

    def body(h_ref, ng_ref, mod_ref, y_ref):
        x = h_ref[...]
        r = lax.rsqrt(jnp.mean(x * x, axis=-1, keepdims=True) + EPS)
        y = (x * r) * ng_ref[i_n:i_n + 1, :]
        y_ref[...] = (y * (1.0 + mod_ref[i_sc:i_sc + 1, :]) + mod_ref[i_sh:i_sh + 1, :]).astype(BF16)

    return pl.pallas_call(
        body, name=name, grid=(T // tm,),
        in_specs=[pl.BlockSpec((tm, D), lambda i: (i, 0)),
                  pl.BlockSpec(ng.shape, lambda i: (0, 0)),
                  pl.BlockSpec(mod.shape, lambda i: (0, 0))],
        out_specs=pl.BlockSpec((tm, D), lambda i: (i, 0)),
        out_shape=jax.ShapeDtypeStruct((T, D), BF16),
        compiler_params=_params(("parallel",)),
    )(h, ng, mod)


def ffn_up(y, wgT, wuT, name):
    T, D = y.shape
    F = _wrows(wgT)
    tm = _tile(T, 512, 16)
    tf = _tile(F, 1408, 128)

    def body(y_ref, wg_ref, wu_ref, a_ref, b_ref, s_ref):
        yv = y_ref[...]
        a = _nt(yv, wg_ref[...])
        b = _nt(yv, wu_ref[...])
        a_ref[...] = a.astype(BF16)
        b_ref[...] = b.astype(BF16)
        s_ref[...] = (a * _sigmoid(a) * b).astype(BF16)

    act = jax.ShapeDtypeStruct((T, F), BF16)
    return pl.pallas_call(
        body, name=name, grid=(F // tf, T // tm),
        in_specs=[pl.BlockSpec((tm, D), lambda j, i: (i, 0)),
                  _wspec(wgT, tf, lambda j, i: j),
                  _wspec(wuT, tf, lambda j, i: j)],
        out_specs=[pl.BlockSpec((tm, tf), lambda j, i: (i, j))] * 3,
        out_shape=[act, act, act],
        compiler_params=_params(("parallel", "parallel")),
    )(y, wgT[0], wuT[0])


def resid_matmul(xs, w, h, mod, i_g, coef, name):
    T, D = h.shape
    kb = xs[0].shape[1]
    assert all(x.shape == (T, kb) for x in xs) and _wrows(w) == kb * len(xs)
    tm = _tile(T, 512, 16)
    nx = len(xs)

    def body(*refs):
        x_refs, w_refs = refs[:nx], refs[nx:2 * nx]
        h_ref, mod_ref, hn_ref, o_ref = refs[2 * nx:]
        o = _nn(x_refs[0][...], w_refs[0][...])
        for xr, wr in zip(x_refs[1:], w_refs[1:]):
            o = o + _nn(xr[...], wr[...])
        o_ref[...] = o.astype(BF16)
        hn_ref[...] = h_ref[...] + (coef * mod_ref[i_g:i_g + 1, :]) * o

    return pl.pallas_call(
        body, name=name, grid=(T // tm,),
        in_specs=([pl.BlockSpec((tm, kb), lambda i: (i, 0))] * nx
                  + [_wspec(w, kb, lambda i, p=p: p, resident=True) for p in range(nx)]
                  + [pl.BlockSpec((tm, D), lambda i: (i, 0)),
                     pl.BlockSpec(mod.shape, lambda i: (0, 0))]),
        out_specs=[pl.BlockSpec((tm, D), lambda i: (i, 0))] * 2,
        out_shape=[jax.ShapeDtypeStruct((T, D), F32), jax.ShapeDtypeStruct((T, D), BF16)],
        compiler_params=_params(("parallel",)),
    )(*xs, *([w[0]] * nx), h, mod)


def _gate_specs(gate, tm, D):
    o, mod, _, _ = gate
    T = o.shape[0]
    return ([pl.BlockSpec((tm, D), lambda i: (i, 0)), pl.BlockSpec(mod.shape, lambda i: (0, 0))],
            [pl.BlockSpec((tm, D), lambda i: (i, 0)), pl.BlockSpec((8, D), lambda i: (0, 0))],
            [jax.ShapeDtypeStruct((T, D), BF16), jax.ShapeDtypeStruct((8, D), F32)],
            [o, mod])


def _gate_emit(d, gate, o_ref, mod_ref, do_ref, red_ref):
    _, _, i_g, coef = gate
    do_ref[...] = (d * (coef * mod_ref[i_g:i_g + 1, :])).astype(BF16)

    @pl.when(pl.program_id(0) == 0)
    def _():
        red_ref[...] = jnp.zeros_like(red_ref)

    red_ref[0:1, :] += coef * jnp.sum(d * o_ref[...].astype(F32), axis=0, keepdims=True)


def ffn_bwd_mid(do, wd, a, b, name):
    T, D = do.shape
    F = _wrows(wd)
    tm = _tile(T, 512, 16)
    tf = _tile(F, 1408, 128)

    def body(do_ref, wd_ref, a_ref, b_ref, da_ref, db_ref):
        ds = _nt(do_ref[...], wd_ref[...])
        av = a_ref[...].astype(F32)
        bv = b_ref[...].astype(F32)
        sig = _sigmoid(av)
        da_ref[...] = (ds * bv * (sig * (1.0 + av * (1.0 - sig)))).astype(BF16)
        db_ref[...] = (ds * (av * sig)).astype(BF16)

    act = jax.ShapeDtypeStruct((T, F), BF16)
    return pl.pallas_call(
        body, name=name, grid=(F // tf, T // tm),
        in_specs=[pl.BlockSpec((tm, D), lambda j, i: (i, 0)),
                  _wspec(wd, tf, lambda j, i: j),
                  pl.BlockSpec((tm, tf), lambda j, i: (i, j)),
                  pl.BlockSpec((tm, tf), lambda j, i: (i, j))],
        out_specs=[pl.BlockSpec((tm, tf), lambda j, i: (i, j))] * 2,
        out_shape=[act, act],
        compiler_params=_params(("parallel", "parallel")),
    )(do, wd[0], a, b)


def dy_normbwd(pairs, h, dhp, ng, i_n, mod, i_sc, name, gate=None):
    T, D = h.shape
    tm = _tile(T, 512, 16)
    npair = len(pairs)
    g_in, g_out, g_shape, g_ops = _gate_specs(gate, tm, D) if gate else ([], [], [], [])

    def body(*refs):
        x_refs, w_refs = refs[:npair], refs[npair:2 * npair]
        h_ref, dhp_ref, ng_ref, mod_ref = refs[2 * npair:2 * npair + 4]
        dh_ref, red_ref = refs[2 * npair + 4 + len(g_in):2 * npair + 6 + len(g_in)]
        dy = _nn(x_refs[0][...], w_refs[0][...])
        for xr, wr in zip(x_refs[1:], w_refs[1:]):
            dy = dy + _nn(xr[...], wr[...])
        x = h_ref[...]
        r = lax.rsqrt(jnp.mean(x * x, axis=-1, keepdims=True) + EPS)
        n = x * r
        gn = ng_ref[i_n:i_n + 1, :]
        dnh = dy * (1.0 + mod_ref[i_sc:i_sc + 1, :])

        @pl.when(pl.program_id(0) == 0)
        def _():
            red_ref[...] = jnp.zeros_like(red_ref)

        red_ref[0:1, :] += jnp.sum(dy, axis=0, keepdims=True)
        red_ref[1:2, :] += jnp.sum(dy * (n * gn), axis=0, keepdims=True)
        red_ref[2:3, :] += jnp.sum(dnh * n, axis=0, keepdims=True)
        dn = dnh * gn
        dh_new = dhp_ref[...] + r * (dn - n * jnp.mean(dn * n, axis=-1, keepdims=True))
        dh_ref[...] = dh_new
        if gate:
            _gate_emit(dh_new, gate, refs[2 * npair + 4], refs[2 * npair + 5], refs[-2], refs[-1])

    in_specs = ([pl.BlockSpec((tm, kb), lambda i, c=c: (i, c)) for (_, c, _, _, kb) in pairs]
                + [_wspec(w, kb, lambda i, r=r: r, resident=True) for (_, _, w, r, kb) in pairs]
                + [pl.BlockSpec((tm, D), lambda i: (i, 0)),
                   pl.BlockSpec((tm, D), lambda i: (i, 0)),
                   pl.BlockSpec(ng.shape, lambda i: (0, 0)),
                   pl.BlockSpec(mod.shape, lambda i: (0, 0))] + g_in)
    return pl.pallas_call(
        body, name=name, grid=(T // tm,), in_specs=in_specs,
        out_specs=[pl.BlockSpec((tm, D), lambda i: (i, 0)), pl.BlockSpec((8, D), lambda i: (0, 0))] + g_out,
        out_shape=[jax.ShapeDtypeStruct((T, D), F32), jax.ShapeDtypeStruct((8, D), F32)] + g_shape,
        compiler_params=_params(("arbitrary",)),
    )(*[p[0] for p in pairs], *[p[2][0] for p in pairs], h, dhp, ng, mod, *g_ops)


def matmul_tn(a, b, buf, slot, row0, name, tmo_cap=1408):
    T, N = b.shape
    ma = a.shape[1]
    tmo = _tile(ma, tmo_cap, 128)
    assert row0 % tmo == 0
    nmo = ma // tmo
    tk = _tile(T, 2048, 16)
    nk = T // tk

    def body(a_ref, b_ref, buf_ref, o_ref, acc_ref):
        k = pl.program_id(1)

        @pl.when(k == 0)
        def _():
            acc_ref[...] = jnp.zeros_like(acc_ref)

        acc_ref[...] += _tn(a_ref[...], b_ref[...])

        @pl.when(k == nk - 1)
        def _():
            o_ref[...] = acc_ref[...].astype(BF16)

    return pl.pallas_call(
        body, name=name, grid=(nmo, nk),
        in_specs=[pl.BlockSpec((tk, tmo), lambda j, k: (k, j)),
                  pl.BlockSpec((tk, N), lambda j, k: (k, 0)),
                  pl.BlockSpec(memory_space=pl.ANY)],
        out_specs=pl.BlockSpec((None, tmo, N), lambda j, k: (slot, row0 // tmo + j, 0)),
        out_shape=jax.ShapeDtypeStruct(buf.shape, BF16),
        scratch_shapes=[pltpu.VMEM((tmo, N), F32)],
        input_output_aliases={2: 0},
        compiler_params=_params(("parallel", "arbitrary")),
    )(a, b, buf)


def matmul_nt(x, w, name):
    T, K = x.shape
    N = _wrows(w)
    tm = _tile(T, 512, 16)
    tn = _tile(N, 1280, 128)

    def body(x_ref, w_ref, o_ref):
        o_ref[...] = _nt(x_ref[...], w_ref[...]).astype(BF16)

    return pl.pallas_call(
        body, name=name, grid=(N // tn, T // tm),
        in_specs=[pl.BlockSpec((tm, K), lambda j, i: (i, 0)), _wspec(w, tn, lambda j, i: j)],
        out_specs=pl.BlockSpec((tm, tn), lambda j, i: (i, j)),
        out_shape=jax.ShapeDtypeStruct((T, N), BF16),
        compiler_params=_params(("parallel", "parallel")),
    )(x, w[0])


def _sgu_head_fwd(u, v, lng, lnb):
    gu, dgu = _gelu_and_grad(u)
    gv, dgv = _gelu_and_grad(v)
    mu = jnp.mean(gv, axis=-1, keepdims=True)
    xc = gv - mu
    rstd = lax.rsqrt(jnp.mean(xc * xc, axis=-1, keepdims=True) + EPS)
    xhat = xc * rstd
    vn = xhat * lng + lnb
    return gu, dgu, dgv, rstd, xhat, vn


def _tril_mask():
    r = lax.broadcasted_iota(jnp.int32, (ATT_BLOCK, ATT_BLOCK), 0)
    c = lax.broadcasted_iota(jnp.int32, (ATT_BLOCK, ATT_BLOCK), 1)
    return c <= r


def _triu_mask():
    r = lax.broadcasted_iota(jnp.int32, (ATT_BLOCK, ATT_BLOCK), 0)
    c = lax.broadcasted_iota(jnp.int32, (ATT_BLOCK, ATT_BLOCK), 1)
    return r <= c


def sgu_fwd(proj, lng, lnb, w, bcol, name):
    T = proj.shape[0]
    tm = _tile(T, 512, 128)
    nch = tm // ATT_BLOCK

    def body(u_ref, v_ref, lng_ref, lnb_ref, w_ref, b_ref, o_ref):
        tril = _tril_mask()
        for hd in range(SGU_HEADS):
            sl = slice(hd * HEAD_LANES, (hd + 1) * HEAD_LANES)
            u = u_ref[:, sl].astype(F32)
            v = v_ref[:, sl].astype(F32)
            gu, _, _, _, _, vn = _sgu_head_fwd(u, v, lng_ref[:, sl], lnb_ref[:, sl])
            wm = jnp.where(tril, w_ref[hd], 0.0).astype(BF16)
            vnb = vn.astype(BF16)
            bc = b_ref[:, hd:hd + 1]
            for ch in range(nch):
                rs = slice(ch * ATT_BLOCK, (ch + 1) * ATT_BLOCK)
                z = _nn(wm, vnb[rs, :]) + bc
                o_ref[rs, sl] = (gu[rs, :] * z).astype(BF16)

    return pl.pallas_call(
        body, name=name, grid=(T // tm,),
        in_specs=[pl.BlockSpec((tm, MIX_HALF), lambda i: (i, 0)),
                  pl.BlockSpec((tm, MIX_HALF), lambda i: (i, 1)),
                  pl.BlockSpec((1, MIX_HALF), lambda i: (0, 0)),
                  pl.BlockSpec((1, MIX_HALF), lambda i: (0, 0)),
                  pl.BlockSpec(w.shape, lambda i: (0, 0, 0)),
                  pl.BlockSpec(bcol.shape, lambda i: (0, 0))],
        out_specs=pl.BlockSpec((tm, MIX_HALF), lambda i: (i, 0)),
        out_shape=jax.ShapeDtypeStruct((T, MIX_HALF), BF16),
        compiler_params=_params(("parallel",)),
    )(proj, proj, lng, lnb, w, bcol)


def sgu_bwd(proj, dmixed, lng, lnb, w, wt, bcol, name):
    T = proj.shape[0]
    tm = _tile(T, 512, 128)
    nch = tm // ATT_BLOCK
    nsteps = T // tm

    def body(u_ref, v_ref, g_ref, lng_ref, lnb_ref, w_ref, wt_ref, b_ref, duv_ref, dw_ref, dvec_ref, bacc_ref):
        step = pl.program_id(0)

        @pl.when(step == 0)
        def _():
            dw_ref[...] = jnp.zeros_like(dw_ref)
            dvec_ref[...] = jnp.zeros_like(dvec_ref)
            bacc_ref[...] = jnp.zeros_like(bacc_ref)

        tril = _tril_mask()
        triu = _triu_mask()
        for hd in range(SGU_HEADS):
            sl = slice(hd * HEAD_LANES, (hd + 1) * HEAD_LANES)
            u = u_ref[:, sl].astype(F32)
            v = v_ref[:, sl].astype(F32)
            lng_h = lng_ref[:, sl]
            gu, dgu, dgv, rstd, xhat, vn = _sgu_head_fwd(u, v, lng_h, lnb_ref[:, sl])
            wm = jnp.where(tril, w_ref[hd], 0.0).astype(BF16)
            wmt = jnp.where(triu, wt_ref[hd], 0.0).astype(BF16)
            vnb = vn.astype(BF16)
            bc = b_ref[:, hd:hd + 1]
            g = g_ref[:, sl].astype(F32)
            dw_acc = jnp.zeros((ATT_BLOCK, ATT_BLOCK), F32)
            b_acc = jnp.zeros((ATT_BLOCK, HEAD_LANES), F32)
            dvn_parts = []
            for ch in range(nch):
                rs = slice(ch * ATT_BLOCK, (ch + 1) * ATT_BLOCK)
                z = _nn(wm, vnb[rs, :]) + bc
                duv_ref[rs, sl] = (g[rs, :] * z * dgu[rs, :]).astype(BF16)
                dz = g[rs, :] * gu[rs, :]
                dzb = dz.astype(BF16)
                dvn_parts.append(_nn(wmt, dzb))
                dw_acc = dw_acc + _nt(dzb, vnb[rs, :])
                b_acc = b_acc + dz
            dvn = jnp.concatenate(dvn_parts, axis=0)
            dw_ref[hd] += jnp.where(tril, dw_acc, 0.0)
            bacc_ref[hd] += b_acc
            dvec_ref[0:1, sl] += jnp.sum(dvn * xhat, axis=0, keepdims=True)
            dvec_ref[1:2, sl] += jnp.sum(dvn, axis=0, keepdims=True)
            dxh = dvn * lng_h
            dgv_in = rstd * (dxh - jnp.mean(dxh, axis=-1, keepdims=True)
                             - xhat * jnp.mean(dxh * xhat, axis=-1, keepdims=True))
            duv_ref[:, MIX_HALF + hd * HEAD_LANES:MIX_HALF + (hd + 1) * HEAD_LANES] = (dgv_in * dgv).astype(BF16)

        @pl.when(step == nsteps - 1)
        def _():
            for hd in range(SGU_HEADS):
                sl = slice(hd * HEAD_LANES, (hd + 1) * HEAD_LANES)
                dvec_ref[2:3, sl] = jnp.sum(bacc_ref[hd].T, axis=0, keepdims=True)

    return pl.pallas_call(
        body, name=name, grid=(nsteps,),
        in_specs=[pl.BlockSpec((tm, MIX_HALF), lambda i: (i, 0)),
                  pl.BlockSpec((tm, MIX_HALF), lambda i: (i, 1)),
                  pl.BlockSpec((tm, MIX_HALF), lambda i: (i, 0)),
                  pl.BlockSpec((1, MIX_HALF), lambda i: (0, 0)),
                  pl.BlockSpec((1, MIX_HALF), lambda i: (0, 0)),
                  pl.BlockSpec(w.shape, lambda i: (0, 0, 0)),
                  pl.BlockSpec(w.shape, lambda i: (0, 0, 0)),
                  pl.BlockSpec(bcol.shape, lambda i: (0, 0))],
        out_specs=[pl.BlockSpec((tm, 2 * MIX_HALF), lambda i: (i, 0)),
                   pl.BlockSpec(w.shape, lambda i: (0, 0, 0)),
                   pl.BlockSpec((8, MIX_HALF), lambda i: (0, 0))],
        out_shape=[jax.ShapeDtypeStruct((T, 2 * MIX_HALF), BF16),
                   jax.ShapeDtypeStruct(w.shape, F32),
                   jax.ShapeDtypeStruct((8, MIX_HALF), F32)],
        scratch_shapes=[pltpu.VMEM((SGU_HEADS, ATT_BLOCK, HEAD_LANES), F32)],
        compiler_params=_params(("arbitrary",)),
    )(proj, proj, dmixed, lng, lnb, w, wt, bcol)


def _rot_half(t):
    lane = lax.broadcasted_iota(jnp.int32, t.shape, 1)
    first = (lane % ATT_DH) < (ATT_DH // 2)
    return jnp.where(first, -pltpu.roll(t, HEAD_LANES - ATT_DH // 2, 1), pltpu.roll(t, ATT_DH // 2, 1))


LAYOUT_ROWS = 512


def _res_spec(d, tm, W):
    return pl.BlockSpec((d, tm // d, W), lambda i: (0, i, 0))


def _res_shape(d, T, W, dtype):
    return jax.ShapeDtypeStruct((d, T // d, W), dtype)


def _slab_buf(tm, W):
    return pltpu.VMEM((W // HEAD_LANES, tm, HEAD_LANES), F32)


def _lanes(hp):
    return slice(hp * HEAD_LANES, (hp + 1) * HEAD_LANES)


def _to_res(buf, out_ref, d, dtype):
    nslab, tm, _ = buf.shape
    for hp in range(nslab):
        if d == 1:
            out_ref[0, :, _lanes(hp)] = buf[hp].astype(dtype)
        else:
            for r in range(d):
                out_ref[r, :, _lanes(hp)] = buf.at[hp][pl.ds(r, tm // d, stride=d), :].astype(dtype)


def _from_res(in_ref, buf, d):
    nslab, tm, _ = buf.shape
    for hp in range(nslab):
        if d == 1:
            buf[hp] = in_ref[0, :, _lanes(hp)]
        else:
            for r in range(d):
                buf.at[hp][pl.ds(r, tm // d, stride=d), :] = in_ref[r, :, _lanes(hp)]


def rope_fwd(proj, cos, sin, name):
    T = proj.shape[0]
    tm = LAYOUT_ROWS
    scale = 1.0 / math.sqrt(ATT_DH)
    nd = len(DILATIONS)

    def body(q_ref, k_ref, v_ref, cos_ref, sin_ref, *rest):
        outs, buf = rest[:3 * nd], rest[3 * nd]
        c = cos_ref[...]
        s = sin_ref[...]
        for which, src in enumerate((q_ref, k_ref, v_ref)):
            for hp in range(MIX_HALF // HEAD_LANES):
                t = src[:, _lanes(hp)].astype(F32)
                if which == 0:
                    t = scale * (t * c + _rot_half(t) * s)
                elif which == 1:
                    t = t * c + _rot_half(t) * s
                buf[hp] = t
            for di, d in enumerate(DILATIONS):
                _to_res(buf, outs[3 * di + which], d, BF16)

    return pl.pallas_call(
        body, name=name, grid=(T // tm,),
        in_specs=[pl.BlockSpec((tm, MIX_HALF), lambda i: (i, 2)),
                  pl.BlockSpec((tm, MIX_HALF), lambda i: (i, 3)),
                  pl.BlockSpec((tm, MIX_HALF), lambda i: (i, 4)),
                  pl.BlockSpec((tm, HEAD_LANES), lambda i: (i, 0)),
                  pl.BlockSpec((tm, HEAD_LANES), lambda i: (i, 0))],
        out_specs=[_res_spec(d, tm, MIX_HALF) for d in DILATIONS for _ in range(3)],
        out_shape=[_res_shape(d, T, MIX_HALF, BF16) for d in DILATIONS for _ in range(3)],
        scratch_shapes=[_slab_buf(tm, MIX_HALF)],
        compiler_params=_params(("parallel",)),
    )(proj, proj, proj, cos, sin)


def to_residues(x, col, name):
    T = x.shape[0]
    tm = LAYOUT_ROWS

    def body(x_ref, *rest):
        outs, buf = rest[:-1], rest[-1]
        for hp in range(MIX_HALF // HEAD_LANES):
            buf[hp] = x_ref[:, _lanes(hp)].astype(F32)
        for o_ref, d in zip(outs, DILATIONS):
            _to_res(buf, o_ref, d, BF16)

    return pl.pallas_call(
        body, name=name, grid=(T // tm,),
        in_specs=[pl.BlockSpec((tm, MIX_HALF), lambda i: (i, col))],
        out_specs=[_res_spec(d, tm, MIX_HALF) for d in DILATIONS],
        out_shape=[_res_shape(d, T, MIX_HALF, BF16) for d in DILATIONS],
        scratch_shapes=[_slab_buf(tm, MIX_HALF)],
        compiler_params=_params(("parallel",)),
    )(x)


def rope_bwd(dqs, dks, dvs, cos, sin, name):
    T = dqs[0].shape[0] * dqs[0].shape[1]
    tm = LAYOUT_ROWS
    scale = 1.0 / math.sqrt(ATT_DH)
    npat = len(dqs)

    def body(*refs):
        groups = refs[:npat], refs[npat:2 * npat], refs[2 * npat:3 * npat]
        cos_ref, sin_ref, o_ref, buf, acc = refs[3 * npat:]
        c = cos_ref[...]
        s = sin_ref[...]
        for which, g_refs in enumerate(groups):
            _from_res(g_refs[0], acc, DILATIONS[0])
            for g_ref, d in zip(g_refs[1:], DILATIONS[1:]):
                _from_res(g_ref, buf, d)
                acc[...] += buf[...]
            for hp in range(MIX_HALF // HEAD_LANES):
                g = acc[hp]
                if which == 0:
                    g = scale * g
                if which < 2:
                    g = g * c - _rot_half(g * s)
                o_ref[:, which * MIX_HALF + hp * HEAD_LANES:which * MIX_HALF + (hp + 1) * HEAD_LANES] = g.astype(BF16)

    return pl.pallas_call(
        body, name=name, grid=(T // tm,),
        in_specs=([_res_spec(d, tm, MIX_HALF) for _ in range(3) for d in DILATIONS]
                  + [pl.BlockSpec((tm, HEAD_LANES), lambda i: (i, 0))] * 2),
        out_specs=pl.BlockSpec((tm, 3 * MIX_HALF), lambda i: (i, 0)),
        out_shape=jax.ShapeDtypeStruct((T, 3 * MIX_HALF), BF16),
        scratch_shapes=[_slab_buf(tm, MIX_HALF), _slab_buf(tm, MIX_HALF)],
        compiler_params=_params(("parallel",)),
    )(*dqs, *dks, *dvs, cos, sin)


def _band_masks(n):
    r = lax.broadcasted_iota(jnp.int32, (2 * ATT_BLOCK, ATT_BLOCK), 0)
    c = lax.broadcasted_iota(jnp.int32, (2 * ATT_BLOCK, ATT_BLOCK), 1)
    qi = r % ATT_BLOCK
    head = (c < ATT_DH) == (r < ATT_BLOCK)
    return (c >= qi) & (n > 0), c <= qi, head, c[:ATT_BLOCK] < ATT_DH


def _stack_heads(x, head):
    x2 = jnp.concatenate([x, x], axis=0)
    return jnp.where(head, x2, jnp.zeros_like(x2))


def attn_fwd(q, k, v, name):
    d, L, W = q.shape
    nb = L // ATT_BLOCK

    def body(q_ref, kp_ref, kc_ref, vp_ref, vc_ref, o_ref, lse_ref):
        mask_p, mask_c, head, head0 = _band_masks(pl.program_id(1))
        for hp in range(W // HEAD_LANES):
            sl = slice(hp * HEAD_LANES, (hp + 1) * HEAD_LANES)
            kp, kc, vp, vc = kp_ref[0, :, sl], kc_ref[0, :, sl], vp_ref[0, :, sl], vc_ref[0, :, sl]
            qs = _stack_heads(q_ref[0, :, sl], head)
            sp = jnp.where(mask_p, _nt(qs, kp), NEG)
            sc = jnp.where(mask_c, _nt(qs, kc), NEG)
            m = jnp.maximum(jnp.max(sp, axis=1, keepdims=True), jnp.max(sc, axis=1, keepdims=True))
            pp = jnp.exp(sp - m)
            pc = jnp.exp(sc - m)
            den = jnp.sum(pp, axis=1, keepdims=True) + jnp.sum(pc, axis=1, keepdims=True)
            o = (_nn(pp.astype(BF16), vp) + _nn(pc.astype(BF16), vc)) / den
            lse = m + jnp.log(den)
            o_ref[0, :, sl] = jnp.where(head0, o[:ATT_BLOCK], o[ATT_BLOCK:])
            lse_ref[0, :, sl] = jnp.where(head0, lse[:ATT_BLOCK], lse[ATT_BLOCK:])

    cur = pl.BlockSpec((1, ATT_BLOCK, W), lambda r, n: (r, n, 0))
    prev = pl.BlockSpec((1, ATT_BLOCK, W), lambda r, n: (r, jnp.maximum(n - 1, 0), 0))
    out = jax.ShapeDtypeStruct((d, L, W), F32)
    return pl.pallas_call(
        body, name=name, grid=(d, nb),
        in_specs=[cur, prev, cur, prev, cur],
        out_specs=[cur, cur], out_shape=[out, out],
        compiler_params=_params(("parallel", "parallel")),
    )(q, k, k, v, v)


def attn_combine(os_, lses, name):
    T = os_[0].shape[0] * os_[0].shape[1]
    W = os_[0].shape[2]
    tm = LAYOUT_ROWS
    npat = len(os_)

    def body(*refs):
        o_refs, l_refs = refs[:npat], refs[npat:2 * npat]
        out_ref = refs[2 * npat]
        ores, lres = refs[2 * npat + 1:3 * npat + 1], refs[3 * npat + 1:4 * npat + 1]
        bufs = refs[4 * npat + 1:]
        lbufs, obufs, out_buf, lse_buf = bufs[:npat], bufs[npat:2 * npat], bufs[2 * npat], bufs[2 * npat + 1]
        for p, d in enumerate(DILATIONS):
            _from_res(l_refs[p], lbufs[p], d)
            _from_res(o_refs[p], obufs[p], d)
        for hp in range(W // HEAD_LANES):
            ls = [b[hp] for b in lbufs]
            m = ls[0]
            for l in ls[1:]:
                m = jnp.maximum(m, l)
            es = [jnp.exp(l - m) for l in ls]
            z = es[0]
            for e in es[1:]:
                z = z + e
            acc = es[0] * obufs[0][hp]
            for p in range(1, npat):
                acc = acc + es[p] * obufs[p][hp]
            out = acc / z
            out_ref[:, _lanes(hp)] = out.astype(BF16)
            out_buf[hp] = out
            lse_buf[hp] = m + jnp.log(z)
        for p, d in enumerate(DILATIONS):
            _to_res(out_buf, ores[p], d, BF16)
            _to_res(lse_buf, lres[p], d, F32)

    return pl.pallas_call(
        body, name=name, grid=(T // tm,),
        in_specs=[_res_spec(d, tm, W) for _ in range(2) for d in DILATIONS],
        out_specs=([pl.BlockSpec((tm, W), lambda i: (i, 0))] + [_res_spec(d, tm, W) for _ in range(2) for d in DILATIONS]),
        out_shape=([jax.ShapeDtypeStruct((T, W), BF16)] + [_res_shape(d, T, W, BF16) for d in DILATIONS]
                   + [_res_shape(d, T, W, F32) for d in DILATIONS]),
        scratch_shapes=[_slab_buf(tm, W)] * (2 * npat + 2),
        compiler_params=_params(("parallel",)),
    )(*os_, *lses)


def attn_bwd(q, k, v, do, o, lse, name):
    d, L, W = q.shape
    nb = L // ATT_BLOCK

    def body(q_ref, kp_ref, kc_ref, vp_ref, vc_ref, do_ref, o_ref, lse_ref, dq_ref, dk_ref, dv_ref, kkeep, vkeep):
        n = pl.program_id(1)

        @pl.when(n == 0)
        def _():
            kkeep[...] = jnp.zeros_like(kkeep)
            vkeep[...] = jnp.zeros_like(vkeep)

        @pl.when(n < nb)
        def _():
            mask_p, mask_c, head, head0 = _band_masks(n)
            for hp in range(W // HEAD_LANES):
                sl = slice(hp * HEAD_LANES, (hp + 1) * HEAD_LANES)
                kp, kc, vp, vc = kp_ref[0, :, sl], kc_ref[0, :, sl], vp_ref[0, :, sl], vc_ref[0, :, sl]
                dout = do_ref[0, :, sl]
                qs = _stack_heads(q_ref[0, :, sl], head)
                dos = _stack_heads(dout, head)
                lse_v = lse_ref[0, :, sl]
                lse_c = jnp.max(jnp.where(head, jnp.concatenate([lse_v, lse_v], axis=0), NEG), axis=1, keepdims=True)
                delta = jnp.sum(_stack_heads(dout.astype(F32) * o_ref[0, :, sl].astype(F32), head), axis=1, keepdims=True)
                pp = jnp.exp(jnp.where(mask_p, _nt(qs, kp), NEG) - lse_c)
                pc = jnp.exp(jnp.where(mask_c, _nt(qs, kc), NEG) - lse_c)
                dsp = (pp * (_nt(dos, vp) - delta)).astype(BF16)
                dsc = (pc * (_nt(dos, vc) - delta)).astype(BF16)
                dq2 = _nn(dsp, kp) + _nn(dsc, kc)
                dq_ref[0, :, sl] = jnp.where(head0, dq2[:ATT_BLOCK], dq2[ATT_BLOCK:])
                dk_ref[0, :, sl] = kkeep[:, sl] + _tn(dsp, qs)
                dv_ref[0, :, sl] = vkeep[:, sl] + _tn(pp.astype(BF16), dos)
                kkeep[:, sl] = _tn(dsc, qs)
                vkeep[:, sl] = _tn(pc.astype(BF16), dos)

        @pl.when(n == nb)
        def _():
            dk_ref[0] = kkeep[...]
            dv_ref[0] = vkeep[...]

    cur = pl.BlockSpec((1, ATT_BLOCK, W), lambda r, n: (r, jnp.minimum(n, nb - 1), 0))
    prev = pl.BlockSpec((1, ATT_BLOCK, W), lambda r, n: (r, jnp.clip(n - 1, 0, nb - 1), 0))
    out = jax.ShapeDtypeStruct((d, L, W), F32)
    return pl.pallas_call(
        body, name=name, grid=(d, nb + 1),
        in_specs=[cur, prev, cur, prev, cur, cur, cur, cur],
        out_specs=[cur, prev, prev], out_shape=[out, out, out],
        scratch_shapes=[pltpu.VMEM((ATT_BLOCK, W), F32), pltpu.VMEM((ATT_BLOCK, W), F32)],
        compiler_params=_params(("parallel", "arbitrary")),
    )(q, k, k, v, v, do, o, lse)


def final_loss_bwd(h, gf, tgt, gate, name):
    T, D = h.shape
    tm = _tile(T, 512, 16)
    g_in, g_out, g_shape, g_ops = _gate_specs(gate, tm, D)

    def body(h_ref, g_ref, t_ref, o_ref, modg_ref, dh_ref, red_ref, do_ref, redg_ref):
        x = h_ref[...]
        r = lax.rsqrt(jnp.mean(x * x, axis=-1, keepdims=True) + EPS)
        n = x * r
        g = g_ref[...]
        err = n * g - t_ref[...]
        dy = err * (1.0 / D)

        @pl.when(pl.program_id(0) == 0)
        def _():
            red_ref[...] = jnp.zeros_like(red_ref)

        red_ref[0:1, :] += jnp.sum(dy * n, axis=0, keepdims=True)
        red_ref[1:2, :] += jnp.zeros((1, D), F32) + (0.5 / D) * jnp.sum(err * err, keepdims=True)
        dn = dy * g
        dh = r * (dn - n * jnp.mean(dn * n, axis=-1, keepdims=True))
        dh_ref[...] = dh
        _gate_emit(dh, gate, o_ref, modg_ref, do_ref, redg_ref)

    return pl.pallas_call(
        body, name=name, grid=(T // tm,),
        in_specs=[pl.BlockSpec((tm, D), lambda i: (i, 0)),
                  pl.BlockSpec((1, D), lambda i: (0, 0)),
                  pl.BlockSpec((tm, D), lambda i: (i, 0))] + g_in,
        out_specs=[pl.BlockSpec((tm, D), lambda i: (i, 0)), pl.BlockSpec((8, D), lambda i: (0, 0))] + g_out,
        out_shape=[jax.ShapeDtypeStruct((T, D), F32), jax.ShapeDtypeStruct((8, D), F32)] + g_shape,
        compiler_params=_params(("arbitrary",)),
    )(h, gf, tgt, *g_ops)


def ada_fwd(c_all, ada_w, ada_b, name):
    nl, D, N = ada_w.shape

    def body(c_ref, w_ref, b_ref, o_ref):
        c = c_ref[...]
        o_ref[0] = _nn(c * _sigmoid(c), w_ref[0]) + b_ref[0]

    return pl.pallas_call(
        body, name=name, grid=(nl,),
        in_specs=[pl.BlockSpec((N_DEV, D), lambda l: (0, 0)),
                  pl.BlockSpec((1, D, N), lambda l: (l, 0, 0)),
                  pl.BlockSpec((1, 1, N), lambda l: (l, 0, 0))],
        out_specs=pl.BlockSpec((1, N_DEV, N), lambda l: (l, 0, 0)),
        out_shape=jax.ShapeDtypeStruct((nl, N_DEV, N), F32),
        compiler_params=_params(("parallel",)),
    )(c_all, ada_w, ada_b)


def ada_bwd(c_allT, dmod, name):
    nl, _, N = dmod.shape
    D = c_allT.shape[0]

    def body(c_ref, g_ref, o_ref):
        c = c_ref[...]
        ca = c * _sigmoid(c)
        acc = ca[:, 0:1] * g_ref[0, 0:1, :]
        for b in range(1, N_DEV):
            acc = acc + ca[:, b:b + 1] * g_ref[0, b:b + 1, :]
        o_ref[0] = acc

    return pl.pallas_call(
        body, name=name, grid=(nl,),
        in_specs=[pl.BlockSpec((D, N_DEV), lambda l: (0, 0)),
                  pl.BlockSpec((1, N_DEV, N), lambda l: (l, 0, 0))],
        out_specs=pl.BlockSpec((1, D, N), lambda l: (l, 0, 0)),
        out_shape=jax.ShapeDtypeStruct((nl, D, N), F32),
        compiler_params=_params(("parallel",)),
    )(c_allT, dmod)


def adamw(w, g, m, v, name):
    R, C = w.shape
    tr = _tile(R, max(8, (1 << 19) // C // 8 * 8), 8)
    c1 = 1.0 - ADAM_B1 ** ADAM_STEP
    c2 = 1.0 - ADAM_B2 ** ADAM_STEP

    def body(w_ref, g_ref, m_ref, v_ref, d_ref, mo_ref, vo_ref):
        gv = g_ref[...]
        mn = ADAM_B1 * m_ref[...] + (1.0 - ADAM_B1) * gv
        vn = ADAM_B2 * v_ref[...] + (1.0 - ADAM_B2) * (gv * gv)
        mo_ref[...] = mn
        vo_ref[...] = vn
        d_ref[...] = -ADAM_LR * ((mn / c1) / (jnp.sqrt(vn / c2) + ADAM_EPS) + ADAM_WD * w_ref[...])

    blk = pl.BlockSpec((tr, C), lambda i: (i, 0))
    out = jax.ShapeDtypeStruct((R, C), F32)
    return pl.pallas_call(
        body, name=name, grid=(R // tr,),
        in_specs=[blk] * 4, out_specs=[blk] * 3, out_shape=[out] * 3,
        compiler_params=_params(("parallel",)),
    )(w, g, m, v)


def adamw_layer(w, g, m, v, l, prev, name):
    NLw, R, C = w.shape
    tr = _tile(R, max(8, (1 << 19) // C // 8 * 8), 8)
    nrb = R // tr
    c1 = 1.0 - ADAM_B1 ** ADAM_STEP
    c2 = 1.0 - ADAM_B2 ** ADAM_STEP
    w, m, v = (t.reshape(NLw * R, C) for t in (w, m, v))

    def body(w_ref, g_ref, m_ref, v_ref, *rest):
        go_ref, d_ref, mo_ref, vo_ref = rest[-4:]
        gv = g_ref[...]
        mn = ADAM_B1 * m_ref[...] + (1.0 - ADAM_B1) * gv
        vn = ADAM_B2 * v_ref[...] + (1.0 - ADAM_B2) * (gv * gv)
        go_ref[...] = gv
        mo_ref[...] = mn
        vo_ref[...] = vn
        d_ref[...] = -ADAM_LR * ((mn / c1) / (jnp.sqrt(vn / c2) + ADAM_EPS) + ADAM_WD * w_ref[...])

    lay = pl.BlockSpec((tr, C), lambda i: (l * nrb + i, 0))
    out = jax.ShapeDtypeStruct((NLw * R, C), F32)
    n_prev = 0 if prev is None else 4
    return pl.pallas_call(
        body, name=name, grid=(nrb,),
        in_specs=[lay, pl.BlockSpec((tr, C), lambda i: (i, 0)), lay, lay] + [pl.BlockSpec(memory_space=pl.ANY)] * n_prev,
        out_specs=[lay] * 4, out_shape=[out] * 4,
        input_output_aliases={4 + i: i for i in range(n_prev)},
        compiler_params=_params(("parallel",)),
    )(w, g, m, v, *(prev or ()))


def sum_slots(x, name):
    S, R, C = x.shape
    tr = _tile(R, 128, 8)

    def body(x_ref, o_ref):
        acc = x_ref[0]
        for s in range(1, S):
            acc = acc + x_ref[s]
        o_ref[...] = acc

    return pl.pallas_call(
        body, name=name, grid=(R // tr,),
        in_specs=[pl.BlockSpec((S, tr, C), lambda i: (0, i, 0))],
        out_specs=pl.BlockSpec((tr, C), lambda i: (i, 0)),
        out_shape=jax.ShapeDtypeStruct((R, C), F32),
        compiler_params=_params(("parallel",)),
    )(x)


def sum_halves(g, lands, c_idx, name):
    n, ns, _, rh, D = g.shape

    def body(c_ref, g_ref, l_ref, o_ref):
        o_ref[0, 0] = (g_ref[0, 0, 0].astype(F32) + l_ref[0, 0].astype(F32)).astype(BF16)

    return pl.pallas_call(
        body, name=name,
        grid_spec=pltpu.PrefetchScalarGridSpec(
            num_scalar_prefetch=1, grid=(n, ns),
            in_specs=[pl.BlockSpec((1, 1, 1, rh, D), lambda i, j, c: (i, j, c[0], 0, 0)),
                      pl.BlockSpec((1, 1, rh, D), lambda i, j, c: (i, j, 0, 0))],
            out_specs=pl.BlockSpec((1, 1, rh, D), lambda i, j, c: (i, j, 0, 0))),
        out_shape=jax.ShapeDtypeStruct((n, ns, rh, D), BF16),
        compiler_params=_params(("parallel", "parallel")),
    )(c_idx, g, lands)


def sum_chips(p, lands, place, name):
    n, ns, rh, D = p.shape

    def body(c_ref, p_ref, l_ref, o_ref):
        acc = p_ref[0, 0].astype(F32)
        for j in range(N_CHIP - 1):
            acc = acc + l_ref[j, 0].astype(F32)
        o_ref[0, 0] = acc

    return pl.pallas_call(
        body, name=name,
        grid_spec=pltpu.PrefetchScalarGridSpec(
            num_scalar_prefetch=1, grid=(n,),
            in_specs=[pl.BlockSpec((1, 1, rh, D), lambda i, c: (i, c[0], 0, 0)),
                      pl.BlockSpec((N_CHIP - 1, 1, rh, D), lambda i, c: (0, i, 0, 0))],
            out_specs=pl.BlockSpec((1, 1, rh, D), lambda i, c: (i, c[1], 0, 0))),
        out_shape=jax.ShapeDtypeStruct((n, 2, rh, D), F32),
        compiler_params=_params(("parallel",)),
    )(place, p, lands)


def _my_place():
    return lax.axis_index("x"), lax.axis_index("y"), lax.axis_index("c")


def _other_chips(mx, my):
    return [(1 - mx, my), (mx, 1 - my), (1 - mx, 1 - my)]


def gather_small(x, name):
    def body(x_ref, out_ref, sum_ref, send_sems, recv_sems):
        mx, my, mc = _my_place()
        me = 4 * mx + 2 * my + mc
        out_ref[me] = x_ref[...]
        sends = []
        for k in range(1, N_DEV):
            kx, ky, kc = (k >> 2) & 1, (k >> 1) & 1, k & 1
            peer = (1 - mx if kx else mx, 1 - my if ky else my, 1 - mc if kc else mc)
            cp = pltpu.make_async_remote_copy(
                src_ref=x_ref, dst_ref=out_ref.at[me], send_sem=send_sems.at[k - 1], recv_sem=recv_sems.at[k - 1],
                device_id=peer, device_id_type=MESH)
            cp.start()
            sends.append((cp, 4 * peer[0] + 2 * peer[1] + peer[2], peer))
        for k, (cp, peer_slot, peer) in enumerate(sends):
            pltpu.make_async_remote_copy(
                src_ref=x_ref, dst_ref=out_ref.at[peer_slot], send_sem=send_sems.at[k], recv_sem=recv_sems.at[k],
                device_id=peer, device_id_type=MESH).wait_recv()
        for cp, _, _ in sends:
            cp.wait_send()
        acc = out_ref[0]
        for s in range(1, N_DEV):
            acc = acc + out_ref[s]
        sum_ref[...] = acc

    vmem = pl.BlockSpec(memory_space=pltpu.VMEM)
    return pl.pallas_call(
        body, name=name,
        in_specs=[vmem], out_specs=[vmem, vmem],
        out_shape=[jax.ShapeDtypeStruct((N_DEV,) + x.shape, x.dtype), jax.ShapeDtypeStruct(x.shape, x.dtype)],
        scratch_shapes=[pltpu.SemaphoreType.DMA((N_DEV - 1,)), pltpu.SemaphoreType.DMA((N_DEV - 1,))],
        compiler_params=pltpu.CompilerParams(vmem_limit_bytes=VMEM_LIMIT),
    )(x)


_HBM =pl.BlockSpec(memory_space=pltpu.HBM)
_SEM = pl.BlockSpec(memory_space=pltpu.SEMAPHORE)
_DATAFLOW = pltpu.SideEffectType.DATAFLOW_SIDE_EFFECTING


def _gather_copies(shard, land, send, recv, base):
    mx, my, mc = _my_place()
    ci = 2 * mx + my
    peers = [((cx, cy, mc), 2 * cx + cy) for cx, cy in _other_chips(mx, my)] + [((mx, my, 1 - mc), ci)]
    out = []
    for q, (dev, src_slot) in enumerate(peers):
        out.append((
            pltpu.make_async_remote_copy(src_ref=shard, dst_ref=land.at[:, ci], send_sem=send.at[base + q],
                                         recv_sem=recv.at[base + q], device_id=dev, device_id_type=MESH),
            pltpu.make_async_remote_copy(src_ref=shard, dst_ref=land.at[:, src_slot], send_sem=send.at[base + q],
                                         recv_sem=recv.at[base + q], device_id=dev, device_id_type=MESH)))
    return out


def gather_start(groups, after, name):
    items = [s for g in groups for s in g]
    ni, ng = len(items), len(groups)

    def body(*refs):
        shards, lands = refs[:ni], refs[ni:2 * ni]
        sems = refs[2 * ni + 1:2 * ni + 1 + 2 * ng]
        token = refs[-1]
        i = 0
        for g, grp in enumerate(groups):
            for p in range(len(grp)):
                for start_cp, _ in _gather_copies(shards[i], lands[i], sems[2 * g], sems[2 * g + 1], 4 * p):
                    start_cp.start()
                i += 1
        token[...] = jnp.zeros_like(token)

    sem_shapes = []
    for grp in groups:
        sem_shapes += [pltpu.SemaphoreType.DMA((4 * len(grp),))] * 2
    land_shapes = [(s.shape[0], N_CHIP) + s.shape[1:] for s in items]
    outs = pl.pallas_call(
        body, name=name,
        in_specs=[_HBM] * (2 * ni) + [pl.BlockSpec(memory_space=pl.ANY)],
        out_specs=[_SEM] * (2 * ng) + [_HBM] * (2 * ni) + [pl.BlockSpec(memory_space=pltpu.VMEM)],
        out_shape=(sem_shapes + [pltpu.HBM(s.shape, s.dtype) for s in items]
                   + [pltpu.HBM(ls, s.dtype) for ls, s in zip(land_shapes, items)]
                   + [jax.ShapeDtypeStruct((8, 128), F32)]),
        input_output_aliases={i: 2 * ng + i for i in range(2 * ni)},
        compiler_params=pltpu.CompilerParams(has_side_effects=_DATAFLOW),
    )(*[pltpu.with_memory_space_constraint(s, pltpu.HBM) for s in items],
      *[pltpu.with_memory_space_constraint(lax.empty(ls, s.dtype), pltpu.HBM) for ls, s in zip(land_shapes, items)],
      after)
    sems, thru, token = outs[:2 * ng], outs[2 * ng:2 * ng + 2 * ni], outs[-1]
    handles, i = [], 0
    for g, grp in enumerate(groups):
        n = len(grp)
        handles.append((sems[2 * g], sems[2 * g + 1], thru[i:i + n], thru[ni + i:ni + i + n]))
        i += n
    return handles, token


def gather_wait(handle, after, name):
    send, recv, shards, lands = handle
    n = len(shards)

    def body(*refs):
        shard_refs, land_refs = refs[:n], refs[n:2 * n]
        send_ref, recv_ref = refs[2 * n], refs[2 * n + 1]
        for p in range(n):
            for start_cp, recv_cp in _gather_copies(shard_refs[p], land_refs[p], send_ref, recv_ref, 4 * p):
                start_cp.wait_send()
                recv_cp.wait_recv()

    outs = pl.pallas_call(
        body, name=name,
        in_specs=[_HBM] * (2 * n) + [_SEM, _SEM, pl.BlockSpec(memory_space=pl.ANY)],
        out_specs=[_HBM] * (2 * n),
        out_shape=[pltpu.HBM(s.shape, s.dtype) for s in shards] + [pltpu.HBM(l.shape, l.dtype) for l in lands],
        input_output_aliases={i: i for i in range(2 * n)},
        compiler_params=pltpu.CompilerParams(has_side_effects=_DATAFLOW),
    )(*shards, *lands, send, recv, after)
    return outs[n:]


def _sibling_copies(gs, lands, send, recv):
    mx, my, mc = _my_place()
    return [pltpu.make_async_remote_copy(
        src_ref=gs[k].at[:, :, 1 - mc], dst_ref=lands[k], send_sem=send.at[k], recv_sem=recv.at[k],
        device_id=(mx, my, 1 - mc), device_id_type=MESH) for k in range(len(gs))]


def sibling_start(gs, after, name):
    K = len(gs)

    def body(*refs):
        ins, lands = refs[:K], refs[K:2 * K]
        send, recv = refs[2 * K + 1], refs[2 * K + 2]
        for cp in _sibling_copies(ins, lands, send, recv):
            cp.start()
        refs[-1][...] = jnp.zeros_like(refs[-1])

    land_shapes = [g.shape[:2] + g.shape[3:] for g in gs]
    outs = pl.pallas_call(
        body, name=name,
        in_specs=[_HBM] * (2 * K) + [pl.BlockSpec(memory_space=pl.ANY)],
        out_specs=[_SEM, _SEM] + [_HBM] * (2 * K) + [pl.BlockSpec(memory_space=pltpu.VMEM)],
        out_shape=([pltpu.SemaphoreType.DMA((K,))] * 2 + [pltpu.HBM(g.shape, g.dtype) for g in gs]
                   + [pltpu.HBM(ls, g.dtype) for ls, g in zip(land_shapes, gs)] + [jax.ShapeDtypeStruct((8, 128), F32)]),
        input_output_aliases={i: 2 + i for i in range(2 * K)},
        compiler_params=pltpu.CompilerParams(has_side_effects=_DATAFLOW),
    )(*[pltpu.with_memory_space_constraint(g, pltpu.HBM) for g in gs],
      *[pltpu.with_memory_space_constraint(lax.empty(ls, g.dtype), pltpu.HBM) for ls, g in zip(land_shapes, gs)],
      after)
    return (outs[0], outs[1], outs[2:2 + K], outs[2 + K:2 + 2 * K]), outs[-1]


def sibling_wait(handle, after, name):
    send, recv, gs, lands = handle
    K = len(gs)

    def body(*refs):
        ins, land_refs = refs[:K], refs[K:2 * K]
        for cp in _sibling_copies(ins, land_refs, refs[2 * K], refs[2 * K + 1]):
            cp.wait_send()
            cp.wait_recv()

    outs = pl.pallas_call(
        body, name=name,
        in_specs=[_HBM] * (2 * K) + [_SEM, _SEM, pl.BlockSpec(memory_space=pl.ANY)],
        out_specs=[_HBM] * (2 * K),
        out_shape=[pltpu.HBM(g.shape, g.dtype) for g in gs] + [pltpu.HBM(l.shape, l.dtype) for l in lands],
        input_output_aliases={i: i for i in range(2 * K)},
        compiler_params=pltpu.CompilerParams(has_side_effects=_DATAFLOW),
    )(*gs, *lands, send, recv, after)
    return outs[:K], outs[K:]


def _small_copies(x, land, send, recv):
    mx, my, mc = _my_place()
    me = 4 * mx + 2 * my + mc
    out = []
    for k in range(1, N_DEV):
        peer = (1 - mx if k & 4 else mx, 1 - my if k & 2 else my, 1 - mc if k & 1 else mc)
        slot = 4 * peer[0] + 2 * peer[1] + peer[2]
        out.append(tuple(pltpu.make_async_remote_copy(
            src_ref=x, dst_ref=land.at[s], send_sem=send.at[k - 1], recv_sem=recv.at[k - 1],
            device_id=peer, device_id_type=MESH) for s in (me, slot)))
    return out


def small_start(x, after, name):
    def body(x_ref, land_ref, after_ref, send, recv, x_thru, land_thru, token):
        for mine, _ in _small_copies(x_ref, land_ref, send, recv):
            mine.start()
        token[...] = jnp.zeros_like(token)

    land_shape = (N_DEV,) + x.shape
    outs = pl.pallas_call(
        body, name=name,
        in_specs=[_HBM, _HBM, pl.BlockSpec(memory_space=pl.ANY)],
        out_specs=[_SEM, _SEM, _HBM, _HBM, pl.BlockSpec(memory_space=pltpu.VMEM)],
        out_shape=[pltpu.SemaphoreType.DMA((N_DEV - 1,))] * 2 + [pltpu.HBM(x.shape, x.dtype), pltpu.HBM(land_shape, x.dtype),
                                                                 jax.ShapeDtypeStruct((8, 128), F32)],
        input_output_aliases={0: 2, 1: 3},
        compiler_params=pltpu.CompilerParams(has_side_effects=_DATAFLOW),
    )(pltpu.with_memory_space_constraint(x, pltpu.HBM),
      pltpu.with_memory_space_constraint(lax.empty(land_shape, x.dtype), pltpu.HBM), after)
    return outs[:4], outs[4]


def small_wait(handle, after, name):
    send, recv, x, land = handle

    def body(x_ref, land_ref, send_ref, recv_ref, after_ref, x_out, land_out):
        for mine, theirs in _small_copies(x_ref, land_ref, send_ref, recv_ref):
            mine.wait_send()
            theirs.wait_recv()

    return pl.pallas_call(
        body, name=name,
        in_specs=[_HBM, _HBM, _SEM, _SEM, pl.BlockSpec(memory_space=pl.ANY)],
        out_specs=[_HBM, _HBM],
        out_shape=[pltpu.HBM(x.shape, x.dtype), pltpu.HBM(land.shape, land.dtype)],
        input_output_aliases={0: 0, 1: 1},
        compiler_params=pltpu.CompilerParams(has_side_effects=_DATAFLOW),
    )(x, land, send, recv, after)


def _scatter_copies(ps, lands, send, recv):
    mx, my, mc = _my_place()
    cps = []
    for j, (cx, cy) in enumerate(_other_chips(mx, my)):
        for k in range(len(ps)):
            cps.append(pltpu.make_async_remote_copy(
                src_ref=ps[k].at[:, 2 * cx + cy], dst_ref=lands[k].at[j],
                send_sem=send.at[k * 3 + j], recv_sem=recv.at[k * 3 + j],
                device_id=(cx, cy, mc), device_id_type=MESH))
    return cps


def scatter_start(ps, after, name):
    K = len(ps)

    def body(*refs):
        ins, lands = refs[:K], refs[K:2 * K]
        send, recv = refs[2 * K + 1], refs[2 * K + 2]
        for cp in _scatter_copies(ins, lands, send, recv):
            cp.start()
        refs[-1][...] = jnp.zeros_like(refs[-1])

    land_shapes = [(N_CHIP - 1, p.shape[0]) + p.shape[2:] for p in ps]
    outs = pl.pallas_call(
        body, name=name,
        in_specs=[_HBM] * (2 * K) + [pl.BlockSpec(memory_space=pl.ANY)],
        out_specs=[_SEM, _SEM] + [_HBM] * (2 * K) + [pl.BlockSpec(memory_space=pltpu.VMEM)],
        out_shape=([pltpu.SemaphoreType.DMA((3 * K,))] * 2 + [pltpu.HBM(p.shape, p.dtype) for p in ps]
                   + [pltpu.HBM(ls, p.dtype) for ls, p in zip(land_shapes, ps)] + [jax.ShapeDtypeStruct((8, 128), F32)]),
        input_output_aliases={i: 2 + i for i in range(2 * K)},
        compiler_params=pltpu.CompilerParams(has_side_effects=_DATAFLOW),
    )(*[pltpu.with_memory_space_constraint(p, pltpu.HBM) for p in ps],
      *[pltpu.with_memory_space_constraint(lax.empty(ls, p.dtype), pltpu.HBM) for ls, p in zip(land_shapes, ps)],
      after)
    return (outs[0], outs[1], outs[2:2 + K], outs[2 + K:2 + 2 * K]), outs[-1]


def scatter_wait(handle, after, name):
    send, recv, ps, lands = handle
    K = len(ps)

    def body(*refs):
        ins, land_refs = refs[:K], refs[K:2 * K]
        send_ref, recv_ref = refs[2 * K], refs[2 * K + 1]
        for cp in _scatter_copies(ins, land_refs, send_ref, recv_ref):
            cp.wait_send()
            cp.wait_recv()

    outs = pl.pallas_call(
        body, name=name,
        in_specs=[_HBM] * (2 * K) + [_SEM, _SEM, pl.BlockSpec(memory_space=pl.ANY)],
        out_specs=[_HBM] * (2 * K),
        out_shape=[pltpu.HBM(p.shape, p.dtype) for p in ps] + [pltpu.HBM(l.shape, l.dtype) for l in lands],
        input_output_aliases={i: i for i in range(2 * K)},
        compiler_params=pltpu.CompilerParams(has_side_effects=_DATAFLOW),
    )(*ps, *lands, send, recv, after)
    return outs[:K], outs[K:]


def sibling_complete(ss, name):
    K = len(ss)

    def body(*refs):
        ins, outs = refs[:K], refs[K:2 * K]
        send, recv = refs[2 * K:]
        mx, my, mc = _my_place()
        cps = []
        for k in range(K):
            cp = pltpu.make_async_remote_copy(
                src_ref=ins[k].at[:, mc], dst_ref=outs[k].at[:, mc], send_sem=send.at[k], recv_sem=recv.at[k],
                device_id=(mx, my, 1 - mc), device_id_type=MESH)
            cp.start()
            cps.append(cp)
        for k in range(K):
            pltpu.make_async_remote_copy(
                src_ref=ins[k].at[:, mc], dst_ref=outs[k].at[:, 1 - mc], send_sem=send.at[k], recv_sem=recv.at[k],
                device_id=(mx, my, 1 - mc), device_id_type=MESH).wait_recv()
        for cp in cps:
            cp.wait_send()

    hbm = pl.BlockSpec(memory_space=pl.ANY)
    return pl.pallas_call(
        body, name=name,
        in_specs=[hbm] * K, out_specs=[hbm] * K,
        out_shape=[jax.ShapeDtypeStruct(s.shape, s.dtype) for s in ss],
        scratch_shapes=[pltpu.SemaphoreType.DMA((K,)), pltpu.SemaphoreType.DMA((K,))],
        input_output_aliases={k: k for k in range(K)},
    )(*ss)


def _rope_tables(T):
    inv = ROPE_THETA ** (-jnp.arange(0, ATT_DH, 2, dtype=F32) / ATT_DH)
    ang = jnp.arange(T, dtype=F32)[:, None] * inv[None, :]
    ang = jnp.concatenate([ang, ang, ang, ang], axis=-1)
    return jnp.cos(ang), jnp.sin(ang)


def _ffn_fwd(h, ng, i_n, mod, i0, get_up, get_down, tag):
    wgu = get_up(h)
    y = normmod_fwd(h, ng, i_n, mod, i0, i0 + 1, f"normmod_{tag}")
    a, b, s = ffn_up(y, (wgu, (0,)), (wgu, (1,)), f"ffn_up_{tag}")
    wd = get_down(s)
    hn, o = resid_matmul([s], (wd, (0,)), h, mod, i0 + 2, 0.5, f"ffn_down_{tag}")
    return hn, (h, y, a, b, s, o), ((wgu, (0,)), (wgu, (1,)), (wd, (0,)))


def _ffn_bwd(dh, do, res, ng, i_n, mod, i0, wgT, wuT, wd, on_grads, next_gate, tag):
    h, y, a, b, s, o = res
    F = _wrows(wgT)
    da, db = ffn_bwd_mid(do, wd, a, b, f"ffn_bwd_mid_{tag}")
    gbuf = lax.empty((3, F, h.shape[1]), BF16)
    gbuf = matmul_tn(da, y, gbuf, 0, 0, f"dwg_{tag}")
    gbuf = matmul_tn(db, y, gbuf, 1, 0, f"dwu_{tag}")
    gbuf = matmul_tn(s, do, gbuf, 2, 0, f"dwd_{tag}")
    token, then = on_grads([gbuf])
    outs = dy_normbwd([(da, 0, wgT, 0, F), (db, 0, wuT, 0, F)], h, dh, ng, i_n, mod + token, i0 + 1,
                      f"ffn_bwd_dy_{tag}", next_gate)
    return outs, then


def _mixer_fwd(h, ng, mod, w_inT, w_out, sgu, cos, sin, tag):
    lng, lnb, sw, swt, bcol = sgu
    y = normmod_fwd(h, ng, 1, mod, 3, 4, f"normmod_{tag}")
    proj = matmul_nt(y, w_inT, f"proj_{tag}")
    out_a = sgu_fwd(proj, lng, lnb, sw, bcol, f"sgu_fwd_{tag}")
    qkv = rope_fwd(proj, cos, sin, f"rope_fwd_{tag}")
    npat = len(DILATIONS)
    qkv_res = [tuple(qkv[3 * p:3 * p + 3]) for p in range(npat)]
    os_, lses = [], []
    for d, (qd, kd, vd) in zip(DILATIONS, qkv_res):
        o_d, lse_d = attn_fwd(qd, kd, vd, f"attn_fwd_d{d}_{tag}")
        os_.append(o_d)
        lses.append(lse_d)
    comb = attn_combine(os_, lses, f"attn_combine_{tag}")
    out_b, o_res, lse_res = comb[0], comb[1:1 + npat], comb[1 + npat:]
    hn, om = resid_matmul([out_a, out_b], w_out, h, mod, 5, 1.0, f"mix_out_{tag}")
    return hn, (h, y, proj, out_a, out_b, o_res, lse_res, qkv_res, om)


def _mixer_bwd(dh, dom, res, ng, mod, w_inT, w_out, sgu, cos, sin, on_grads, next_gate, tag):
    lng, lnb, sw, swt, bcol = sgu
    h, y, proj, out_a, out_b, o_res, lse_res, qkv_res, om = res
    D = h.shape[1]
    dmixed = matmul_nt(dom, w_out, f"dmixed_{tag}")
    woutbuf = lax.empty((1, 2 * MIX_HALF, D), BF16)
    woutbuf = matmul_tn(out_a, dom, woutbuf, 0, 0, f"dwout_a_{tag}", tmo_cap=MIX_HALF)
    woutbuf = matmul_tn(out_b, dom, woutbuf, 0, MIX_HALF, f"dwout_b_{tag}", tmo_cap=MIX_HALF)
    d_uv, d_sw, d_svec = sgu_bwd(proj, dmixed, lng, lnb, sw, swt, bcol, f"sgu_bwd_{tag}")
    do_res = to_residues(dmixed, 1, f"dout_res_{tag}")
    dqs, dks, dvs = [], [], []
    for p, (d, (qd, kd, vd)) in enumerate(zip(DILATIONS, qkv_res)):
        dq, dk, dv = attn_bwd(qd, kd, vd, do_res[p], o_res[p], lse_res[p], f"attn_bwd_d{d}_{tag}")
        dqs.append(dq)
        dks.append(dk)
        dvs.append(dv)
    d_qkv = rope_bwd(dqs, dks, dvs, cos, sin, f"rope_bwd_{tag}")
    winbuf = lax.empty((1, 5 * MIX_HALF, D), BF16)
    winbuf = matmul_tn(d_uv, y, winbuf, 0, 0, f"dwin_uv_{tag}", tmo_cap=MIX_HALF)
    winbuf = matmul_tn(d_qkv, y, winbuf, 0, 2 * MIX_HALF, f"dwin_qkv_{tag}", tmo_cap=MIX_HALF)
    token, then = on_grads([winbuf, woutbuf])
    pairs = ([(d_uv, p, w_inT, p, MIX_HALF) for p in range(2)]
             + [(d_qkv, p, w_inT, 2 + p, MIX_HALF) for p in range(3)])
    outs = dy_normbwd(pairs, h, dh, ng, 1, mod + token, 4, f"mix_bwd_dy_{tag}", next_gate)
    return outs, d_sw, d_svec, then


def _local_step(x, tgt, mods, ngs, get_w, sgus, gf, on_block_grads, on_layer_small):
    T, D = x.shape
    cos, sin = _rope_tables(T)
    h = x
    saved, weights = [], []
    for l in range(2):
        def getter(blk, l=l):
            return lambda after: get_w(l, blk, after)

        h, r1, wf1 = _ffn_fwd(h, ngs[l], 0, mods[l], 0, getter("f1u"), getter("f1d"), f"l{l}f1")
        w_inT, w_out = get_w(l, "mx", h)
        h, r2 = _mixer_fwd(h, ngs[l], mods[l], (w_inT, (0,)), (w_out, (0,)), sgus[l], cos, sin, f"l{l}mx")
        h, r3, wf2 = _ffn_fwd(h, ngs[l], 2, mods[l], 6, getter("f2u"), getter("f2d"), f"l{l}f2")
        saved.append((r1, r2, r3))
        weights.append((wf1, w_inT, w_out, wf2))
    def gate_of(l, blk):
        r1, r2, r3 = saved[l]
        o, i_g, coef = {"f2": (r3[5], 8, 0.5), "mx": (r2[-1], 5, 1.0), "f1": (r1[5], 2, 0.5)}[blk]
        return o, mods[l], i_g, coef

    seq = [(l, blk) for l in (1, 0) for blk in ("f2", "mx", "f1")]
    dh, red_final, do, red_g = final_loss_bwd(h, gf, tgt, gate_of(*seq[0]), "final_loss_bwd")
    rn, rg = {}, {}
    for idx, (l, blk) in enumerate(seq):
        r1, r2, r3 = saved[l]
        wf1, w_inT, w_out, wf2 = weights[l]
        nxt = gate_of(*seq[idx + 1]) if idx + 1 < len(seq) else None
        rg[blk] = red_g
        tag = f"l{l}{blk}"

        def on(arrays, l=l, blk=blk):
            return on_block_grads(l, blk, arrays)

        if blk == "f2":
            outs, then = _ffn_bwd(dh, do, r3, ngs[l], 2, mods[l], 6, *wf2, on, nxt, tag)
        elif blk == "mx":
            outs, d_sw, d_svec, then = _mixer_bwd(dh, do, r2, ngs[l], mods[l], (w_inT, (0,)), (w_out, (0,)), sgus[l],
                                                  cos, sin, on, nxt, tag)
        else:
            outs, then = _ffn_bwd(dh, do, r1, ngs[l], 0, mods[l], 0, *wf1, on, nxt, tag)
        dh, rn[blk] = outs[0], outs[1]
        if nxt is not None:
            do, red_g = outs[2], outs[3]
        if blk == "f1":
            mods = mods + on_layer_small(l, dict(sgu_w=d_sw, sgu_vec=d_svec, red_n=(rn["f1"], rn["mx"], rn["f2"]),
                                                 red_g=(rg["f1"], rg["mx"], rg["f2"])),
                                         red_final if l == 0 else None)
            mods = mods + then(mods)
        else:
            mods = mods + then(dh)
    return dh


def _adam_out(w, g, m, v, name):
    shp = w.shape
    two_d = (-1, shp[-1])
    d, mn, vn = adamw(w.reshape(two_d), g.reshape(two_d), m.reshape(two_d), v.reshape(two_d), name)
    return g, d.reshape(shp), mn.reshape(shp), vn.reshape(shp)


def kernel(x, c, ada_w, ada_b, norm_g, ffn1_wg, ffn1_wu, ffn1_wd, ffn2_wg, ffn2_wu, ffn2_wd, w_in, sgu_ln_g, sgu_ln_b, sgu_w, sgu_b, w_out, final_g, loss_target, m_ada_w, m_ada_b, m_norm_g, m_ffn1_wg, m_ffn1_wu, m_ffn1_wd, m_ffn2_wg, m_ffn2_wu, m_ffn2_wd, m_w_in, m_sgu_ln_g, m_sgu_ln_b, m_sgu_w, m_sgu_b, m_w_out, m_final_g, v_ada_w, v_ada_b, v_norm_g, v_ffn1_wg, v_ffn1_wu, v_ffn1_wd, v_ffn2_wg, v_ffn2_wu, v_ffn2_wd, v_w_in, v_sgu_ln_g, v_sgu_ln_b, v_sgu_w, v_sgu_b, v_w_out, v_final_g):
    T, D = x.shape[1], x.shape[2]
    NL = ada_w.shape[0]
    mx, my, mc = _my_place()
    me = 4 * mx + 2 * my + mc
    ci = 2 * mx + my
    c_idx = jnp.reshape(mc, (1,)).astype(jnp.int32)
    place = jnp.stack([ci, mc]).astype(jnp.int32)

    ngw = norm_g.shape[2]
    small_in = jnp.concatenate([jnp.pad(c, ((0, 7), (0, 0))),
                                jnp.pad(norm_g.reshape(NL * 3, ngw), ((0, 8 - NL * 3), (0, D - ngw)))], axis=0)
    small_all, _ = gather_small(small_in, "gather_c_normg")
    c_all = small_all[:, 0, :]
    ng_parts = small_all[0::2, 8:8 + NL * 3, :ngw]
    ngs = jnp.transpose(ng_parts, (1, 0, 2)).reshape(NL, 3, N_CHIP * ngw)

    nmod = ada_w.shape[2]
    ada_b_mine = lax.dynamic_slice_in_dim(ada_b, ci * nmod, nmod, axis=1).reshape(NL, 1, nmod)
    mod_part = ada_fwd(c_all, ada_w, ada_b_mine, "ada_fwd")
    mod_all, _ = gather_small(mod_part.reshape(NL * N_DEV, nmod), "gather_mod")
    mod_rows = lax.dynamic_index_in_dim(mod_all.reshape(N_DEV, NL, N_DEV, nmod), me, axis=2, keepdims=False)
    mods = jnp.transpose(mod_rows[0::2], (1, 0, 2)).reshape(NL, N_ADA, D)

    sgus = []
    for l in range(NL):
        sgus.append((sgu_ln_g[l].reshape(1, MIX_HALF), sgu_ln_b[l].reshape(1, MIX_HALF), sgu_w[l],
                     jnp.swapaxes(sgu_w[l], 1, 2), jnp.transpose(sgu_b[l])))

    def halves(a):
        n, r, _ = a.shape
        return a.reshape(n, 2, r // 2, D)

    Fs = ffn1_wd.shape[1]
    groups = []
    for l in range(NL):
        groups += [[halves(jnp.stack([ffn1_wg[l].T, ffn1_wu[l].T], axis=0).astype(BF16))],
                   [halves(ffn1_wd[l].astype(BF16)[None])],
                   [halves(w_in[l].T.astype(BF16)[None]), halves(w_out[l].astype(BF16)[None])],
                   [halves(jnp.stack([ffn2_wg[l].T, ffn2_wu[l].T], axis=0).astype(BF16))],
                   [halves(ffn2_wd[l].astype(BF16)[None])]]
    handles, token = gather_start(groups, mods, "gather_start")
    mods = mods + token[0, 0]
    group_no = {"f1u": 0, "f1d": 1, "mx": 2, "f2u": 3, "f2d": 4}

    def get_w(l, key, after):
        full = gather_wait(handles[len(group_no) * l + group_no[key]], after, f"gather_wait_l{l}{key}")
        full = [a.reshape(a.shape[0], N_CHIP * 2 * a.shape[3], D) for a in full]
        return full[0] if key != "mx" else tuple(full)

    def split(a):
        n, r4, _ = a.shape
        return a.reshape(n, N_CHIP, 2, r4 // N_CHIP // 2, D)

    pending, small_pending, small_tokens = {}, {}, {}

    def on_block_grads(l, blk, bufs):
        tag = f"l{l}{blk}"
        sib, tok1 = sibling_start([split(g) for g in bufs], place, f"rs_sibling_start_{tag}")

        def then(after):
            parts, lands = sibling_wait(sib, after, f"rs_sibling_wait_{tag}")
            psums = [sum_halves(g, ld, c_idx, f"rs_sum_halves_{tag}_{i}") for i, (g, ld) in enumerate(zip(parts, lands))]
            pending[(l, blk)], tok2 = scatter_start(psums, lands[0], f"rs_chips_start_{tag}")
            return tok2[0, 0]

        return tok1[0, 0], then

    def blocks_finish(blocks, after, tag):
        ssums, counts = [], []
        for l, blk in blocks:
            psums, lands2 = scatter_wait(pending.pop((l, blk)), after, f"rs_chips_wait_l{l}{blk}")
            ssums += [sum_chips(p, ld, place, f"rs_sum_chips_l{l}{blk}_{i}") for i, (p, ld) in enumerate(zip(psums, lands2))]
            counts.append(len(psums))
        fins = [f.reshape(f.shape[0], -1, D) for f in sibling_complete(ssums, f"rs_complete_{tag}")]
        out, i = [], 0
        for n in counts:
            out.append(fins[i:i + n])
            i += n
        return out

    def on_layer_small(l, grads, red_final):
        blocks = list(grads["red_n"]) + list(grads["red_g"])
        blocks.append(jnp.pad(grads["sgu_vec"], ((0, 0), (0, D - MIX_HALF))))
        blocks.append(grads["sgu_w"].reshape(-1, D))
        if red_final is not None:
            blocks.append(red_final)
        xs = jnp.concatenate(blocks, axis=0)
        small_pending[l], small_tokens[l] = small_start(xs, place, f"small_start_l{l}")
        return small_tokens[l][0, 0]

    grad_x = _local_step(x[0], loss_target[0], mods, ngs, get_w, sgus, final_g.reshape(1, D),
                         on_block_grads, on_layer_small)

    adam_state = {}

    def adam_big(nm, l, g, w, m, v):
        adam_state[nm] = adamw_layer(w, g, m, v, l, adam_state.get(nm), f"adamw_{nm}_l{l}")

    def adam_block(l, blk, fin):
        if blk == "mx":
            adam_big("w_in", l, fin[0][0].T, w_in, m_w_in, v_w_in)
            adam_big("w_out", l, fin[1][0], w_out, m_w_out, v_w_out)
        else:
            ws = ((ffn1_wg, m_ffn1_wg, v_ffn1_wg), (ffn1_wu, m_ffn1_wu, v_ffn1_wu), (ffn1_wd, m_ffn1_wd, v_ffn1_wd)) \
                if blk == "f1" else \
                ((ffn2_wg, m_ffn2_wg, v_ffn2_wg), (ffn2_wu, m_ffn2_wu, v_ffn2_wu), (ffn2_wd, m_ffn2_wd, v_ffn2_wd))
            pre = "ffn1" if blk == "f1" else "ffn2"
            for k, (nm, tr) in enumerate((("wg", True), ("wu", True), ("wd", False))):
                adam_big(f"{pre}_{nm}", l, fin[0][k].T if tr else fin[0][k], *ws[k])

    done_order = [(l, blk) for l in range(NL - 1, -1, -1) for blk in ("f2", "mx", "f1")]
    for (l, blk), fin in zip(done_order[:-1], blocks_finish(done_order[:-1], small_tokens[0], "early")):
        adam_block(l, blk, fin)
    last_big = adam_state["w_out"][1]

    small_sum, small_all = [], []
    for l in range(NL):
        xs, land = small_wait(small_pending[l], last_big, f"small_wait_l{l}")
        full = lax.dynamic_update_slice(land, xs[None], (me, 0, 0))
        small_all.append(full)
        small_sum.append(sum_slots(full, f"small_sum_l{l}"))
    offs = [8 * i for i in range(8)]
    off_final = offs[7] + SGU_HEADS * ATT_BLOCK * HEAD_LANES // D
    loss = small_sum[0][off_final + 1, 0]
    g_final_g = small_sum[0][off_final, :]
    g_norm_g, g_ada_b, g_lng, g_lnb, g_sb, g_sw, dmod_all = [], [], [], [], [], [], []
    for l in range(NL):
        rn = [small_sum[l][offs[i]:offs[i] + 8] for i in range(3)]
        rg = [small_sum[l][offs[3 + i]:offs[3 + i] + 8] for i in range(3)]
        g_norm_g.append(jnp.stack([rn[i][2] for i in range(3)], axis=0))
        g_ada_b.append(jnp.concatenate([jnp.stack([rn[i][0], rn[i][1], rg[i][0]], axis=0) for i in range(3)],
                                       axis=0).reshape(N_ADA * D))
        sv = small_sum[l][offs[6]:offs[6] + 8, :MIX_HALF]
        g_lng.append(sv[0].reshape(SGU_HEADS, HEAD_LANES))
        g_lnb.append(sv[1].reshape(SGU_HEADS, HEAD_LANES))
        g_sb.append(sv[2].reshape(SGU_HEADS, ATT_BLOCK))
        g_sw.append(small_sum[l][offs[7]:off_final].reshape(sgu_w.shape[1:]))
        rows = []
        for i in range(3):
            an = small_all[l][:, offs[i]:offs[i] + 2]
            ag = small_all[l][:, offs[3 + i]:offs[3 + i] + 1]
            rows += [an[:, 0], an[:, 1], ag[:, 0]]
        dmod_all.append(jnp.stack(rows, axis=1).reshape(N_DEV, N_ADA * D))
    dmod_all = jnp.stack(dmod_all, axis=0)
    dmod_mine = lax.dynamic_slice_in_dim(dmod_all, ci * nmod, nmod, axis=2)
    g_ada_w = ada_bwd(jnp.transpose(c_all), dmod_mine, "ada_bwd")
    g_ada_b = jnp.stack(g_ada_b, axis=0)
    g_norm_g_full = jnp.stack(g_norm_g, axis=0)
    g_norm_g_mine = lax.dynamic_slice_in_dim(g_norm_g_full, ci * ngw, ngw, axis=2)

    small_params = [
        ("ada_w", ada_w, g_ada_w, m_ada_w, v_ada_w),
        ("ada_b", ada_b, g_ada_b, m_ada_b, v_ada_b),
        ("norm_g", norm_g, g_norm_g_mine, m_norm_g, v_norm_g),
        ("sgu_ln_g", sgu_ln_g, jnp.stack(g_lng, axis=0), m_sgu_ln_g, v_sgu_ln_g),
        ("sgu_ln_b", sgu_ln_b, jnp.stack(g_lnb, axis=0), m_sgu_ln_b, v_sgu_ln_b),
        ("sgu_w", sgu_w, jnp.stack(g_sw, axis=0), m_sgu_w, v_sgu_w),
        ("sgu_b", sgu_b, jnp.stack(g_sb, axis=0), m_sgu_b, v_sgu_b),
        ("final_g", final_g.reshape(1, D), g_final_g.reshape(1, D), m_final_g.reshape(1, D), v_final_g.reshape(1, D)),
    ]
    for nm, w, g, m, v in small_params:
        res = _adam_out(w, g, m, v, f"adamw_{nm}")
        adam_state[nm] = tuple(t.reshape(D) for t in res) if nm == "final_g" else res

    l, blk = done_order[-1]
    adam_block(l, blk, blocks_finish([(l, blk)], adam_state["ada_w"][1], "last")[0])

    names = ["ada_w", "ada_b", "norm_g", "ffn1_wg", "ffn1_wu", "ffn1_wd", "ffn2_wg", "ffn2_wu", "ffn2_wd", "w_in",
             "sgu_ln_g", "sgu_ln_b", "sgu_w", "sgu_b", "w_out", "final_g"]
    shapes = [t.shape for t in (ada_w, ada_b, norm_g, ffn1_wg, ffn1_wu, ffn1_wd, ffn2_wg, ffn2_wu, ffn2_wd, w_in,
                                sgu_ln_g, sgu_ln_b, sgu_w, sgu_b, w_out, final_g)]
    return (loss, grad_x[None], *[adam_state[nm][i].reshape(s) for i in range(4) for nm, s in zip(names, shapes)])
```

```python
import math

import jax
import jax.numpy as jnp
from jax import lax
from jax.experimental import pallas as pl
from jax.experimental.pallas import tpu as pltpu

F32 = jnp.float32
BF16 = jnp.bfloat16
EPS = 1e-6
SGU_HEADS = 4
HEAD_LANES = 128
ATT_DH = 64
ATT_BLOCK = 128
MIX_HALF = SGU_HEADS * HEAD_LANES
DILATIONS = (1, 4, 16)
ROPE_THETA = 10000.0
N_ADA = 9
ADAM_LR, ADAM_B1, ADAM_B2, ADAM_EPS, ADAM_WD, ADAM_STEP = 0.001, 0.9, 0.999, 1e-08, 0.01, 10
NEG = -1e30
V7X_VMEM_BYTES = 64 * 1024 * 1024
VMEM_LIMIT = V7X_VMEM_BYTES * 7 // 8
MESH = pl.DeviceIdType.MESH
N_DEV = 8
N_CHIP = 4


def _tile(n, cap, mult):
    if n <= cap:
        return n
    t = (cap // mult) * mult
    while t >= mult:
        if n % t == 0:
            return t
        t -= mult
    raise ValueError((n, cap, mult))


def _params(dims=None):
    return pltpu.CompilerParams(dimension_semantics=dims, vmem_limit_bytes=VMEM_LIMIT)


def _wspec(w, rows, idx, resident=False):
    arr, lead = w
    kw = dict(pipeline_mode=pl.Buffered(1)) if resident else {}
    return pl.BlockSpec((None,) * len(lead) + (rows, arr.shape[-1]), lambda *g: tuple(lead) + (idx(*g), 0), **kw)


def _wrows(w):
    return w[0].shape[-2]


def _nt(a, b):
    return lax.dot_general(a, b, (((1,), (1,)), ((), ())), preferred_element_type=F32)


def _tn(a, b):
    return lax.dot_general(a, b, (((0,), (0,)), ((), ())), preferred_element_type=F32)


def _nn(a, b):
    return jnp.dot(a, b, preferred_element_type=F32)


def _sigmoid(x):
    return 0.5 * jnp.tanh(0.5 * x) + 0.5


_GELU_K = math.sqrt(2.0 / math.pi)
_GELU_C = 0.044715


def _gelu(x):
    t = jnp.tanh(_GELU_K * (x + _GELU_C * x * x * x))
    return 0.5 * x * (1.0 + t)


def _gelu_and_grad(x):
    x2 = x * x
    t = jnp.tanh(_GELU_K * (x + _GELU_C * x * x2))
    g = 0.5 * x * (1.0 + t)
    dg = 0.5 * (1.0 + t) + 0.5 * x * (1.0 - t * t) * (_GELU_K * (1.0 + 3.0 * _GELU_C * x2))
    return g, dg


def normmod_fwd(h, ng, i_n, mod, i_sh, i_sc, name):
    T, D = h.shape
    tm = _tile(T, 512, 8)

    def body(h_ref, ng_ref, mod_ref, y_ref):
        y_ref[...] = _normmod(h_ref[...], ng_ref[i_n:i_n + 1, :], mod_ref[i_sh:i_sh + 1, :],
                              mod_ref[i_sc:i_sc + 1, :]).astype(BF16)

    return pl.pallas_call(
        body, name=name, grid=(T // tm,),
        in_specs=[pl.BlockSpec((tm, D), lambda i: (i, 0)),
                  pl.BlockSpec(ng.shape, lambda i: (0, 0)),
                  pl.BlockSpec(mod.shape, lambda i: (0, 0))],
        out_specs=pl.BlockSpec((tm, D), lambda i: (i, 0)),
        out_shape=jax.ShapeDtypeStruct((T, D), BF16),
        compiler_params=_params(("parallel",)),
    )(h, ng, mod)


def ffn_up(y, wgT, wuT, name):
    T, D = y.shape
    F = _wrows(wgT)
    tm = _tile(T, 1024, 16)
    tf = _tile(F, 1408, 128)
    cuts = list(range(0, tf, 768)) + [tf]

    def body(y_ref, wg_ref, wu_ref, a_ref, b_ref, s_ref):
        yv = y_ref[...]
        for c0, c1 in zip(cuts[:-1], cuts[1:]):
            a = _nt(yv, wg_ref[c0:c1, :])
            b = _nt(yv, wu_ref[c0:c1, :])
            a_ref[:, c0:c1] = a.astype(BF16)
            b_ref[:, c0:c1] = b.astype(BF16)
            s_ref[:, c0:c1] = (a * _sigmoid(a) * b).astype(BF16)

    act = jax.ShapeDtypeStruct((T, F), BF16)
    return pl.pallas_call(
        body, name=name, grid=(F // tf, T // tm),
        in_specs=[pl.BlockSpec((tm, D), lambda j, i: (i, 0)),
                  _wspec(wgT, tf, lambda j, i: j, resident=True),
                  _wspec(wuT, tf, lambda j, i: j, resident=True)],
        out_specs=[pl.BlockSpec((tm, tf), lambda j, i: (i, j))] * 3,
        out_shape=[act, act, act],
        compiler_params=_params(("parallel", "parallel")),
    )(y, wgT[0], wuT[0])


def _normmod(x, gn, sh, sc):
    r = lax.rsqrt(jnp.mean(x * x, axis=-1, keepdims=True) + EPS)
    return ((x * r) * gn) * (1.0 + sc) + sh


def resid_matmul(xs, w, h, mod, i_g, coef, name, norm_next=None):
    T, D = h.shape
    kb = xs[0].shape[1]
    assert all(x.shape == (T, kb) for x in xs) and _wrows(w) == kb * len(xs)
    tm = _tile(T, 1024, 16)
    nx = len(xs)
    n_in, n_out, n_shape, n_ops = [], [], [], []
    if norm_next:
        ng_n, i_n, mod_n, i_sh, i_sc = norm_next
        n_in = [pl.BlockSpec(ng_n.shape, lambda i: (0, 0)), pl.BlockSpec(mod_n.shape, lambda i: (0, 0))]
        n_out = [pl.BlockSpec((tm, D), lambda i: (i, 0))]
        n_shape = [jax.ShapeDtypeStruct((T, D), BF16)]
        n_ops = [ng_n, mod_n]

    def body(*refs):
        x_refs, w_refs = refs[:nx], refs[nx:2 * nx]
        h_ref, mod_ref = refs[2 * nx:2 * nx + 2]
        hn_ref, o_ref = refs[2 * nx + 2 + len(n_in):2 * nx + 4 + len(n_in)]
        o = _nn(x_refs[0][...], w_refs[0][...])
        for xr, wr in zip(x_refs[1:], w_refs[1:]):
            o = o + _nn(xr[...], wr[...])
        o_ref[...] = o.astype(BF16)
        hn = h_ref[...] + (coef * mod_ref[i_g:i_g + 1, :]) * o
        hn_ref[...] = hn
        if norm_next:
            ng_ref, modn_ref = refs[2 * nx + 2], refs[2 * nx + 3]
            refs[-1][...] = _normmod(hn, ng_ref[i_n:i_n + 1, :], modn_ref[i_sh:i_sh + 1, :],
                                     modn_ref[i_sc:i_sc + 1, :]).astype(BF16)

    return pl.pallas_call(
        body, name=name, grid=(T // tm,),
        in_specs=([pl.BlockSpec((tm, kb), lambda i: (i, 0))] * nx
                  + [_wspec(w, kb, lambda i, p=p: p, resident=True) for p in range(nx)]
                  + [pl.BlockSpec((tm, D), lambda i: (i, 0)),
                     pl.BlockSpec(mod.shape, lambda i: (0, 0))] + n_in),
        out_specs=[pl.BlockSpec((tm, D), lambda i: (i, 0))] * 2 + n_out,
        out_shape=[jax.ShapeDtypeStruct((T, D), F32), jax.ShapeDtypeStruct((T, D), BF16)] + n_shape,
        compiler_params=_params(("parallel",)),
    )(*xs, *([w[0]] * nx), h, mod, *n_ops)


def _gate_specs(gate, tm, D):
    o, mod, _, _ = gate
    T = o.shape[0]
    return ([pl.BlockSpec((tm, D), lambda i: (i, 0)), pl.BlockSpec(mod.shape, lambda i: (0, 0))],
            [pl.BlockSpec((tm, D), lambda i: (i, 0)), pl.BlockSpec((8, D), lambda i: (0, 0))],
            [jax.ShapeDtypeStruct((T, D), BF16), jax.ShapeDtypeStruct((8, D), F32)],
            [o, mod])


def _gate_emit(d, gate, o_ref, mod_ref, do_ref, red_ref):
    _, _, i_g, coef = gate
    do_ref[...] = (d * (coef * mod_ref[i_g:i_g + 1, :])).astype(BF16)

    @pl.when(pl.program_id(0) == 0)
    def _():
        red_ref[...] = jnp.zeros_like(red_ref)

    red_ref[0:1, :] += coef * jnp.sum(d * o_ref[...].astype(F32), axis=0, keepdims=True)


def ffn_bwd_mid(do, wd, a, b, name):
    T, D = do.shape
    F = _wrows(wd)
    tm = _tile(T, 512, 16)
    tf = _tile(F, 1408, 128)

    def body(do_ref, wd_ref, a_ref, b_ref, da_ref, db_ref):
        ds = _nt(do_ref[...], wd_ref[...])
        av = a_ref[...].astype(F32)
        bv = b_ref[...].astype(F32)
        sig = _sigmoid(av)
        da_ref[...] = (ds * bv * (sig * (1.0 + av * (1.0 - sig)))).astype(BF16)
        db_ref[...] = (ds * (av * sig)).astype(BF16)

    act = jax.ShapeDtypeStruct((T, F), BF16)
    return pl.pallas_call(
        body, name=name, grid=(F // tf, T // tm),
        in_specs=[pl.BlockSpec((tm, D), lambda j, i: (i, 0)),
                  _wspec(wd, tf, lambda j, i: j),
                  pl.BlockSpec((tm, tf), lambda j, i: (i, j)),
                  pl.BlockSpec((tm, tf), lambda j, i: (i, j))],
        out_specs=[pl.BlockSpec((tm, tf), lambda j, i: (i, j))] * 2,
        out_shape=[act, act],
        compiler_params=_params(("parallel", "parallel")),
    )(do, wd[0], a, b)


def dy_normbwd(pairs, h, dhp, ng, i_n, mod, i_sc, name, gate=None):
    T, D = h.shape
    tm = _tile(T, 512, 16)
    npair = len(pairs)
    g_in, g_out, g_shape, g_ops = _gate_specs(gate, tm, D) if gate else ([], [], [], [])

    def body(*refs):
        x_refs, w_refs = refs[:npair], refs[npair:2 * npair]
        h_ref, dhp_ref, ng_ref, mod_ref = refs[2 * npair:2 * npair + 4]
        dh_ref, red_ref = refs[2 * npair + 4 + len(g_in):2 * npair + 6 + len(g_in)]
        dy = _nn(x_refs[0][...], w_refs[0][...])
        for xr, wr in zip(x_refs[1:], w_refs[1:]):
            dy = dy + _nn(xr[...], wr[...])
        x = h_ref[...]
        r = lax.rsqrt(jnp.mean(x * x, axis=-1, keepdims=True) + EPS)
        n = x * r
        gn = ng_ref[i_n:i_n + 1, :]
        dnh = dy * (1.0 + mod_ref[i_sc:i_sc + 1, :])

        @pl.when(pl.program_id(0) == 0)
        def _():
            red_ref[...] = jnp.zeros_like(red_ref)

        red_ref[0:1, :] += jnp.sum(dy, axis=0, keepdims=True)
        red_ref[1:2, :] += jnp.sum(dy * (n * gn), axis=0, keepdims=True)
        red_ref[2:3, :] += jnp.sum(dnh * n, axis=0, keepdims=True)
        dn = dnh * gn
        dh_new = dhp_ref[...] + r * (dn - n * jnp.mean(dn * n, axis=-1, keepdims=True))
        dh_ref[...] = dh_new
        if gate:
            _gate_emit(dh_new, gate, refs[2 * npair + 4], refs[2 * npair + 5], refs[-2], refs[-1])

    in_specs = ([pl.BlockSpec((tm, kb), lambda i, c=c: (i, c)) for (_, c, _, _, kb) in pairs]
                + [_wspec(w, kb, lambda i, r=r: r, resident=True) for (_, _, w, r, kb) in pairs]
                + [pl.BlockSpec((tm, D), lambda i: (i, 0)),
                   pl.BlockSpec((tm, D), lambda i: (i, 0)),
                   pl.BlockSpec(ng.shape, lambda i: (0, 0)),
                   pl.BlockSpec(mod.shape, lambda i: (0, 0))] + g_in)
    return pl.pallas_call(
        body, name=name, grid=(T // tm,), in_specs=in_specs,
        out_specs=[pl.BlockSpec((tm, D), lambda i: (i, 0)), pl.BlockSpec((8, D), lambda i: (0, 0))] + g_out,
        out_shape=[jax.ShapeDtypeStruct((T, D), F32), jax.ShapeDtypeStruct((8, D), F32)] + g_shape,
        compiler_params=_params(("arbitrary",)),
    )(*[p[0] for p in pairs], *[p[2][0] for p in pairs], h, dhp, ng, mod, *g_ops)


def matmul_tn(a, b, buf, slot, row0, name, tmo_cap=1408):
    T, N = b.shape
    ma = a.shape[1]
    tmo = _tile(ma, tmo_cap, 128)
    assert row0 % tmo == 0
    nmo = ma // tmo
    tk = _tile(T, 2048, 16)
    nk = T // tk

    def body(a_ref, b_ref, buf_ref, o_ref, acc_ref):
        k = pl.program_id(1)

        @pl.when(k == 0)
        def _():
            acc_ref[...] = jnp.zeros_like(acc_ref)

        acc_ref[...] += _tn(a_ref[...], b_ref[...])

        @pl.when(k == nk - 1)
        def _():
            o_ref[...] = acc_ref[...].astype(BF16)

    return pl.pallas_call(
        body, name=name, grid=(nmo, nk),
        in_specs=[pl.BlockSpec((tk, tmo), lambda j, k: (k, j)),
                  pl.BlockSpec((tk, N), lambda j, k: (k, 0)),
                  pl.BlockSpec(memory_space=pl.ANY)],
        out_specs=pl.BlockSpec((None, tmo, N), lambda j, k: (slot, row0 // tmo + j, 0)),
        out_shape=jax.ShapeDtypeStruct(buf.shape, BF16),
        scratch_shapes=[pltpu.VMEM((tmo, N), F32)],
        input_output_aliases={2: 0},
        compiler_params=_params(("parallel", "arbitrary")),
    )(a, b, buf)


def matmul_nt(x, w, name):
    T, K = x.shape
    N = _wrows(w)
    tm = _tile(T, 1024, 16)
    tn = _tile(N, 1280, 128)

    def body(x_ref, w_ref, o_ref):
        o_ref[...] = _nt(x_ref[...], w_ref[...]).astype(BF16)

    return pl.pallas_call(
        body, name=name, grid=(N // tn, T // tm),
        in_specs=[pl.BlockSpec((tm, K), lambda j, i: (i, 0)), _wspec(w, tn, lambda j, i: j)],
        out_specs=pl.BlockSpec((tm, tn), lambda j, i: (i, j)),
        out_shape=jax.ShapeDtypeStruct((T, N), BF16),
        compiler_params=_params(("parallel", "parallel")),
    )(x, w[0])


def _sgu_head_fwd(u, v, lng, lnb):
    gu, dgu = _gelu_and_grad(u)
    gv, dgv = _gelu_and_grad(v)
    mu = jnp.mean(gv, axis=-1, keepdims=True)
    xc = gv - mu
    rstd = lax.rsqrt(jnp.mean(xc * xc, axis=-1, keepdims=True) + EPS)
    xhat = xc * rstd
    vn = xhat * lng + lnb
    return gu, dgu, dgv, rstd, xhat, vn


def _tril_mask():
    r = lax.broadcasted_iota(jnp.int32, (ATT_BLOCK, ATT_BLOCK), 0)
    c = lax.broadcasted_iota(jnp.int32, (ATT_BLOCK, ATT_BLOCK), 1)
    return c <= r


def _triu_mask():
    r = lax.broadcasted_iota(jnp.int32, (ATT_BLOCK, ATT_BLOCK), 0)
    c = lax.broadcasted_iota(jnp.int32, (ATT_BLOCK, ATT_BLOCK), 1)
    return r <= c


def sgu_fwd(proj, lng, lnb, w, bcol, name):
    T = proj.shape[0]
    tm = _tile(T, 512, 128)
    nch = tm // ATT_BLOCK

    def body(u_ref, v_ref, lng_ref, lnb_ref, w_ref, b_ref, o_ref):
        tril = _tril_mask()
        for hd in range(SGU_HEADS):
            sl = slice(hd * HEAD_LANES, (hd + 1) * HEAD_LANES)
            u = u_ref[:, sl].astype(F32)
            v = v_ref[:, sl].astype(F32)
            gu, _, _, _, _, vn = _sgu_head_fwd(u, v, lng_ref[:, sl], lnb_ref[:, sl])
            wm = jnp.where(tril, w_ref[hd], 0.0).astype(BF16)
            vnb = vn.astype(BF16)
            bc = b_ref[:, hd:hd + 1]
            for ch in range(nch):
                rs = slice(ch * ATT_BLOCK, (ch + 1) * ATT_BLOCK)
                z = _nn(wm, vnb[rs, :]) + bc
                o_ref[rs, sl] = (gu[rs, :] * z).astype(BF16)

    return pl.pallas_call(
        body, name=name, grid=(T // tm,),
        in_specs=[pl.BlockSpec((tm, MIX_HALF), lambda i: (i, 0)),
                  pl.BlockSpec((tm, MIX_HALF), lambda i: (i, 1)),
                  pl.BlockSpec((1, MIX_HALF), lambda i: (0, 0)),
                  pl.BlockSpec((1, MIX_HALF), lambda i: (0, 0)),
                  pl.BlockSpec(w.shape, lambda i: (0, 0, 0)),
                  pl.BlockSpec(bcol.shape, lambda i: (0, 0))],
        out_specs=pl.BlockSpec((tm, MIX_HALF), lambda i: (i, 0)),
        out_shape=jax.ShapeDtypeStruct((T, MIX_HALF), BF16),
        compiler_params=_params(("parallel",)),
    )(proj, proj, lng, lnb, w, bcol)


def sgu_bwd(proj, dmixed, lng, lnb, w, wt, bcol, name):
    T = proj.shape[0]
    tm = _tile(T, 512, 128)
    nch = tm // ATT_BLOCK
    nsteps = T // tm

    def body(u_ref, v_ref, g_ref, lng_ref, lnb_ref, w_ref, wt_ref, b_ref, duv_ref, dw_ref, dvec_ref, bacc_ref):
        step = pl.program_id(0)

        @pl.when(step == 0)
        def _():
            dw_ref[...] = jnp.zeros_like(dw_ref)
            dvec_ref[...] = jnp.zeros_like(dvec_ref)
            bacc_ref[...] = jnp.zeros_like(bacc_ref)

        tril = _tril_mask()
        triu = _triu_mask()
        for hd in range(SGU_HEADS):
            sl = slice(hd * HEAD_LANES, (hd + 1) * HEAD_LANES)
            u = u_ref[:, sl].astype(F32)
            v = v_ref[:, sl].astype(F32)
            lng_h = lng_ref[:, sl]
            gu, dgu, dgv, rstd, xhat, vn = _sgu_head_fwd(u, v, lng_h, lnb_ref[:, sl])
            wm = jnp.where(tril, w_ref[hd], 0.0).astype(BF16)
            wmt = jnp.where(triu, wt_ref[hd], 0.0).astype(BF16)
            vnb = vn.astype(BF16)
            bc = b_ref[:, hd:hd + 1]
            g = g_ref[:, sl].astype(F32)
            dw_acc = jnp.zeros((ATT_BLOCK, ATT_BLOCK), F32)
            b_acc = jnp.zeros((ATT_BLOCK, HEAD_LANES), F32)
            dvn_parts = []
            for ch in range(nch):
                rs = slice(ch * ATT_BLOCK, (ch + 1) * ATT_BLOCK)
                z = _nn(wm, vnb[rs, :]) + bc
                duv_ref[rs, sl] = (g[rs, :] * z * dgu[rs, :]).astype(BF16)
                dz = g[rs, :] * gu[rs, :]
                dzb = dz.astype(BF16)
                dvn_parts.append(_nn(wmt, dzb))
                dw_acc = dw_acc + _nt(dzb, vnb[rs, :])
                b_acc = b_acc + dz
            dvn = jnp.concatenate(dvn_parts, axis=0)
            dw_ref[hd] += jnp.where(tril, dw_acc, 0.0)
            bacc_ref[hd] += b_acc
            dvec_ref[0:1, sl] += jnp.sum(dvn * xhat, axis=0, keepdims=True)
            dvec_ref[1:2, sl] += jnp.sum(dvn, axis=0, keepdims=True)
            dxh = dvn * lng_h
            dgv_in = rstd * (dxh - jnp.mean(dxh, axis=-1, keepdims=True)
                             - xhat * jnp.mean(dxh * xhat, axis=-1, keepdims=True))
            duv_ref[:, MIX_HALF + hd * HEAD_LANES:MIX_HALF + (hd + 1) * HEAD_LANES] = (dgv_in * dgv).astype(BF16)

        @pl.when(step == nsteps - 1)
        def _():
            for hd in range(SGU_HEADS):
                sl = slice(hd * HEAD_LANES, (hd + 1) * HEAD_LANES)
                dvec_ref[2:3, sl] = jnp.sum(bacc_ref[hd].T, axis=0, keepdims=True)

    return pl.pallas_call(
        body, name=name, grid=(nsteps,),
        in_specs=[pl.BlockSpec((tm, MIX_HALF), lambda i: (i, 0)),
                  pl.BlockSpec((tm, MIX_HALF), lambda i: (i, 1)),
                  pl.BlockSpec((tm, MIX_HALF), lambda i: (i, 0)),
                  pl.BlockSpec((1, MIX_HALF), lambda i: (0, 0)),
                  pl.BlockSpec((1, MIX_HALF), lambda i: (0, 0)),
                  pl.BlockSpec(w.shape, lambda i: (0, 0, 0)),
                  pl.BlockSpec(w.shape, lambda i: (0, 0, 0)),
                  pl.BlockSpec(bcol.shape, lambda i: (0, 0))],
        out_specs=[pl.BlockSpec((tm, 2 * MIX_HALF), lambda i: (i, 0)),
                   pl.BlockSpec(w.shape, lambda i: (0, 0, 0)),
                   pl.BlockSpec((8, MIX_HALF), lambda i: (0, 0))],
        out_shape=[jax.ShapeDtypeStruct((T, 2 * MIX_HALF), BF16),
                   jax.ShapeDtypeStruct(w.shape, F32),
                   jax.ShapeDtypeStruct((8, MIX_HALF), F32)],
        scratch_shapes=[pltpu.VMEM((SGU_HEADS, ATT_BLOCK, HEAD_LANES), F32)],
        compiler_params=_params(("arbitrary",)),
    )(proj, proj, dmixed, lng, lnb, w, wt, bcol)


def _rot_half(t):
    lane = lax.broadcasted_iota(jnp.int32, t.shape, 1)
    first = (lane % ATT_DH) < (ATT_DH // 2)
    return jnp.where(first, -pltpu.roll(t, HEAD_LANES - ATT_DH // 2, 1), pltpu.roll(t, ATT_DH // 2, 1))


LAYOUT_ROWS = 512


def _res_spec(d, tm, W):
    return pl.BlockSpec((d, tm // d, W), lambda i: (0, i, 0))


def _res_shape(d, T, W, dtype):
    return jax.ShapeDtypeStruct((d, T // d, W), dtype)


def _slab_buf(tm, W):
    return pltpu.VMEM((W // HEAD_LANES, tm, HEAD_LANES), F32)


def _lanes(hp):
    return slice(hp * HEAD_LANES, (hp + 1) * HEAD_LANES)


def _to_res(buf, out_ref, d, dtype):
    nslab, tm, _ = buf.shape
    for hp in range(nslab):
        if d == 1:
            out_ref[0, :, _lanes(hp)] = buf[hp].astype(dtype)
        else:
            for r in range(d):
                out_ref[r, :, _lanes(hp)] = buf.at[hp][pl.ds(r, tm // d, stride=d), :].astype(dtype)


def _from_res(in_ref, buf, d):
    nslab, tm, _ = buf.shape
    for hp in range(nslab):
        if d == 1:
            buf[hp] = in_ref[0, :, _lanes(hp)]
        else:
            for r in range(d):
                buf.at[hp][pl.ds(r, tm // d, stride=d), :] = in_ref[r, :, _lanes(hp)]


def rope_fwd(proj, cos, sin, name):
    T = proj.shape[0]
    tm = LAYOUT_ROWS
    scale = 1.0 / math.sqrt(ATT_DH)
    nd = len(DILATIONS)

    def body(q_ref, k_ref, v_ref, cos_ref, sin_ref, *rest):
        outs, buf = rest[:3 * nd], rest[3 * nd]
        c = cos_ref[...]
        s = sin_ref[...]
        for which, src in enumerate((q_ref, k_ref, v_ref)):
            for hp in range(MIX_HALF // HEAD_LANES):
                t = src[:, _lanes(hp)].astype(F32)
                if which == 0:
                    t = scale * (t * c + _rot_half(t) * s)
                elif which == 1:
                    t = t * c + _rot_half(t) * s
                buf[hp] = t
            for di, d in enumerate(DILATIONS):
                _to_res(buf, outs[3 * di + which], d, BF16)

    return pl.pallas_call(
        body, name=name, grid=(T // tm,),
        in_specs=[pl.BlockSpec((tm, MIX_HALF), lambda i: (i, 2)),
                  pl.BlockSpec((tm, MIX_HALF), lambda i: (i, 3)),
                  pl.BlockSpec((tm, MIX_HALF), lambda i: (i, 4)),
                  pl.BlockSpec((tm, HEAD_LANES), lambda i: (i, 0)),
                  pl.BlockSpec((tm, HEAD_LANES), lambda i: (i, 0))],
        out_specs=[_res_spec(d, tm, MIX_HALF) for d in DILATIONS for _ in range(3)],
        out_shape=[_res_shape(d, T, MIX_HALF, BF16) for d in DILATIONS for _ in range(3)],
        scratch_shapes=[_slab_buf(tm, MIX_HALF)],
        compiler_params=_params(("parallel",)),
    )(proj, proj, proj, cos, sin)


def to_residues(x, col, name):
    T = x.shape[0]
    tm = LAYOUT_ROWS

    def body(x_ref, *rest):
        outs, buf = rest[:-1], rest[-1]
        for hp in range(MIX_HALF // HEAD_LANES):
            buf[hp] = x_ref[:, _lanes(hp)].astype(F32)
        for o_ref, d in zip(outs, DILATIONS):
            _to_res(buf, o_ref, d, BF16)

    return pl.pallas_call(
        body, name=name, grid=(T // tm,),
        in_specs=[pl.BlockSpec((tm, MIX_HALF), lambda i: (i, col))],
        out_specs=[_res_spec(d, tm, MIX_HALF) for d in DILATIONS],
        out_shape=[_res_shape(d, T, MIX_HALF, BF16) for d in DILATIONS],
        scratch_shapes=[_slab_buf(tm, MIX_HALF)],
        compiler_params=_params(("parallel",)),
    )(x)


def rope_bwd(dqs, dks, dvs, cos, sin, name):
    T = dqs[0].shape[0] * dqs[0].shape[1]
    tm = LAYOUT_ROWS
    scale = 1.0 / math.sqrt(ATT_DH)
    npat = len(dqs)

    def body(*refs):
        groups = refs[:npat], refs[npat:2 * npat], refs[2 * npat:3 * npat]
        cos_ref, sin_ref, o_ref, buf, acc = refs[3 * npat:]
        c = cos_ref[...]
        s = sin_ref[...]
        for which, g_refs in enumerate(groups):
            _from_res(g_refs[0], acc, DILATIONS[0])
            for g_ref, d in zip(g_refs[1:], DILATIONS[1:]):
                _from_res(g_ref, buf, d)
                acc[...] += buf[...]
            for hp in range(MIX_HALF // HEAD_LANES):
                g = acc[hp]
                if which == 0:
                    g = scale * g
                if which < 2:
                    g = g * c - _rot_half(g * s)
                o_ref[:, which * MIX_HALF + hp * HEAD_LANES:which * MIX_HALF + (hp + 1) * HEAD_LANES] = g.astype(BF16)

    return pl.pallas_call(
        body, name=name, grid=(T // tm,),
        in_specs=([_res_spec(d, tm, MIX_HALF) for _ in range(3) for d in DILATIONS]
                  + [pl.BlockSpec((tm, HEAD_LANES), lambda i: (i, 0))] * 2),
        out_specs=pl.BlockSpec((tm, 3 * MIX_HALF), lambda i: (i, 0)),
        out_shape=jax.ShapeDtypeStruct((T, 3 * MIX_HALF), BF16),
        scratch_shapes=[_slab_buf(tm, MIX_HALF), _slab_buf(tm, MIX_HALF)],
        compiler_params=_params(("parallel",)),
    )(*dqs, *dks, *dvs, cos, sin)


def _band_masks(n):
    r = lax.broadcasted_iota(jnp.int32, (2 * ATT_BLOCK, ATT_BLOCK), 0)
    c = lax.broadcasted_iota(jnp.int32, (2 * ATT_BLOCK, ATT_BLOCK), 1)
    qi = r % ATT_BLOCK
    head = (c < ATT_DH) == (r < ATT_BLOCK)
    return (c >= qi) & (n > 0), c <= qi, head, c[:ATT_BLOCK] < ATT_DH


def _stack_heads(x, head):
    x2 = jnp.concatenate([x, x], axis=0)
    return jnp.where(head, x2, jnp.zeros_like(x2))


def attn_fwd(q, k, v, name):
    d, L, W = q.shape
    nb = L // ATT_BLOCK

    def body(q_ref, kp_ref, kc_ref, vp_ref, vc_ref, o_ref, lse_ref):
        mask_p, mask_c, head, head0 = _band_masks(pl.program_id(1))
        for hp in range(W // HEAD_LANES):
            sl = slice(hp * HEAD_LANES, (hp + 1) * HEAD_LANES)
            kp, kc, vp, vc = kp_ref[0, :, sl], kc_ref[0, :, sl], vp_ref[0, :, sl], vc_ref[0, :, sl]
            qs = _stack_heads(q_ref[0, :, sl], head)
            sp = jnp.where(mask_p, _nt(qs, kp), NEG)
            sc = jnp.where(mask_c, _nt(qs, kc), NEG)
            m = jnp.maximum(jnp.max(sp, axis=1, keepdims=True), jnp.max(sc, axis=1, keepdims=True))
            pp = jnp.exp(sp - m)
            pc = jnp.exp(sc - m)
            den = jnp.sum(pp, axis=1, keepdims=True) + jnp.sum(pc, axis=1, keepdims=True)
            o = (_nn(pp.astype(BF16), vp) + _nn(pc.astype(BF16), vc)) / den
            lse = m + jnp.log(den)
            o_ref[0, :, sl] = jnp.where(head0, o[:ATT_BLOCK], o[ATT_BLOCK:])
            lse_ref[0, :, sl] = jnp.where(head0, lse[:ATT_BLOCK], lse[ATT_BLOCK:])

    cur = pl.BlockSpec((1, ATT_BLOCK, W), lambda r, n: (r, n, 0))
    prev = pl.BlockSpec((1, ATT_BLOCK, W), lambda r, n: (r, jnp.maximum(n - 1, 0), 0))
    out = jax.ShapeDtypeStruct((d, L, W), F32)
    return pl.pallas_call(
        body, name=name, grid=(d, nb),
        in_specs=[cur, prev, cur, prev, cur],
        out_specs=[cur, cur], out_shape=[out, out],
        compiler_params=_params(("parallel", "parallel")),
    )(q, k, k, v, v)


def attn_combine(os_, lses, name):
    T = os_[0].shape[0] * os_[0].shape[1]
    W = os_[0].shape[2]
    tm = LAYOUT_ROWS
    npat = len(os_)

    def body(*refs):
        o_refs, l_refs = refs[:npat], refs[npat:2 * npat]
        out_ref = refs[2 * npat]
        ores, lres = refs[2 * npat + 1:3 * npat + 1], refs[3 * npat + 1:4 * npat + 1]
        bufs = refs[4 * npat + 1:]
        lbufs, obufs, out_buf, lse_buf = bufs[:npat], bufs[npat:2 * npat], bufs[2 * npat], bufs[2 * npat + 1]
        for p, d in enumerate(DILATIONS):
            _from_res(l_refs[p], lbufs[p], d)
            _from_res(o_refs[p], obufs[p], d)
        for hp in range(W // HEAD_LANES):
            ls = [b[hp] for b in lbufs]
            m = ls[0]
            for l in ls[1:]:
                m = jnp.maximum(m, l)
            es = [jnp.exp(l - m) for l in ls]
            z = es[0]
            for e in es[1:]:
                z = z + e
            acc = es[0] * obufs[0][hp]
            for p in range(1, npat):
                acc = acc + es[p] * obufs[p][hp]
            out = acc / z
            out_ref[:, _lanes(hp)] = out.astype(BF16)
            out_buf[hp] = out
            lse_buf[hp] = m + jnp.log(z)
        for p, d in enumerate(DILATIONS):
            _to_res(out_buf, ores[p], d, BF16)
            _to_res(lse_buf, lres[p], d, F32)

    return pl.pallas_call(
        body, name=name, grid=(T // tm,),
        in_specs=[_res_spec(d, tm, W) for _ in range(2) for d in DILATIONS],
        out_specs=([pl.BlockSpec((tm, W), lambda i: (i, 0))] + [_res_spec(d, tm, W) for _ in range(2) for d in DILATIONS]),
        out_shape=([jax.ShapeDtypeStruct((T, W), BF16)] + [_res_shape(d, T, W, BF16) for d in DILATIONS]
                   + [_res_shape(d, T, W, F32) for d in DILATIONS]),
        scratch_shapes=[_slab_buf(tm, W)] * (2 * npat + 2),
        compiler_params=_params(("parallel",)),
    )(*os_, *lses)


def attn_bwd(q, k, v, do, o, lse, name):
    d, L, W = q.shape
    nb = L // ATT_BLOCK

    def body(q_ref, kp_ref, kc_ref, vp_ref, vc_ref, do_ref, o_ref, lse_ref, dq_ref, dk_ref, dv_ref, kkeep, vkeep):
        n = pl.program_id(1)

        @pl.when(n == 0)
        def _():
            kkeep[...] = jnp.zeros_like(kkeep)
            vkeep[...] = jnp.zeros_like(vkeep)

        @pl.when(n < nb)
        def _():
            mask_p, mask_c, head, head0 = _band_masks(n)
            for hp in range(W // HEAD_LANES):
                sl = slice(hp * HEAD_LANES, (hp + 1) * HEAD_LANES)
                kp, kc, vp, vc = kp_ref[0, :, sl], kc_ref[0, :, sl], vp_ref[0, :, sl], vc_ref[0, :, sl]
                dout = do_ref[0, :, sl]
                qs = _stack_heads(q_ref[0, :, sl], head)
                dos = _stack_heads(dout, head)
                lse_v = lse_ref[0, :, sl]
                lse_c = jnp.max(jnp.where(head, jnp.concatenate([lse_v, lse_v], axis=0), NEG), axis=1, keepdims=True)
                delta = jnp.sum(_stack_heads(dout.astype(F32) * o_ref[0, :, sl].astype(F32), head), axis=1, keepdims=True)
                pp = jnp.exp(jnp.where(mask_p, _nt(qs, kp), NEG) - lse_c)
                pc = jnp.exp(jnp.where(mask_c, _nt(qs, kc), NEG) - lse_c)
                dsp = (pp * (_nt(dos, vp) - delta)).astype(BF16)
                dsc = (pc * (_nt(dos, vc) - delta)).astype(BF16)
                dq2 = _nn(dsp, kp) + _nn(dsc, kc)
                dq_ref[0, :, sl] = jnp.where(head0, dq2[:ATT_BLOCK], dq2[ATT_BLOCK:])
                dk_ref[0, :, sl] = kkeep[:, sl] + _tn(dsp, qs)
                dv_ref[0, :, sl] = vkeep[:, sl] + _tn(pp.astype(BF16), dos)
                kkeep[:, sl] = _tn(dsc, qs)
                vkeep[:, sl] = _tn(pc.astype(BF16), dos)

        @pl.when(n == nb)
        def _():
            dk_ref[0] = kkeep[...]
            dv_ref[0] = vkeep[...]

    cur = pl.BlockSpec((1, ATT_BLOCK, W), lambda r, n: (r, jnp.minimum(n, nb - 1), 0))
    prev = pl.BlockSpec((1, ATT_BLOCK, W), lambda r, n: (r, jnp.clip(n - 1, 0, nb - 1), 0))
    out = jax.ShapeDtypeStruct((d, L, W), F32)
    return pl.pallas_call(
        body, name=name, grid=(d, nb + 1),
        in_specs=[cur, prev, cur, prev, cur, cur, cur, cur],
        out_specs=[cur, prev, prev], out_shape=[out, out, out],
        scratch_shapes=[pltpu.VMEM((ATT_BLOCK, W), F32), pltpu.VMEM((ATT_BLOCK, W), F32)],
        compiler_params=_params(("parallel", "arbitrary")),
    )(q, k, k, v, v, do, o, lse)


def final_loss_bwd(h, gf, tgt, gate, name):
    T, D = h.shape
    tm = _tile(T, 512, 16)
    g_in, g_out, g_shape, g_ops = _gate_specs(gate, tm, D)

    def body(h_ref, g_ref, t_ref, o_ref, modg_ref, dh_ref, red_ref, do_ref, redg_ref):
        x = h_ref[...]
        r = lax.rsqrt(jnp.mean(x * x, axis=-1, keepdims=True) + EPS)
        n = x * r
        g = g_ref[...]
        err = n * g - t_ref[...]
        dy = err * (1.0 / D)

        @pl.when(pl.program_id(0) == 0)
        def _():
            red_ref[...] = jnp.zeros_like(red_ref)

        red_ref[0:1, :] += jnp.sum(dy * n, axis=0, keepdims=True)
        red_ref[1:2, :] += jnp.zeros((1, D), F32) + (0.5 / D) * jnp.sum(err * err, keepdims=True)
        dn = dy * g
        dh = r * (dn - n * jnp.mean(dn * n, axis=-1, keepdims=True))
        dh_ref[...] = dh
        _gate_emit(dh, gate, o_ref, modg_ref, do_ref, redg_ref)

    return pl.pallas_call(
        body, name=name, grid=(T // tm,),
        in_specs=[pl.BlockSpec((tm, D), lambda i: (i, 0)),
                  pl.BlockSpec((1, D), lambda i: (0, 0)),
                  pl.BlockSpec((tm, D), lambda i: (i, 0))] + g_in,
        out_specs=[pl.BlockSpec((tm, D), lambda i: (i, 0)), pl.BlockSpec((8, D), lambda i: (0, 0))] + g_out,
        out_shape=[jax.ShapeDtypeStruct((T, D), F32), jax.ShapeDtypeStruct((8, D), F32)] + g_shape,
        compiler_params=_params(("arbitrary",)),
    )(h, gf, tgt, *g_ops)


def ada_fwd(c_all, ada_w, ada_b, name):
    nl, D, N = ada_w.shape

    def body(c_ref, w_ref, b_ref, o_ref):
        c = c_ref[...]
        o_ref[0] = _nn(c * _sigmoid(c), w_ref[0]) + b_ref[0]

    return pl.pallas_call(
        body, name=name, grid=(nl,),
        in_specs=[pl.BlockSpec((N_DEV, D), lambda l: (0, 0)),
                  pl.BlockSpec((1, D, N), lambda l: (l, 0, 0)),
                  pl.BlockSpec((1, 1, N), lambda l: (l, 0, 0))],
        out_specs=pl.BlockSpec((1, N_DEV, N), lambda l: (l, 0, 0)),
        out_shape=jax.ShapeDtypeStruct((nl, N_DEV, N), F32),
        compiler_params=_params(("parallel",)),
    )(c_all, ada_w, ada_b)


def ada_bwd(c_allT, dmod, name):
    nl, _, N = dmod.shape
    D = c_allT.shape[0]

    def body(c_ref, g_ref, o_ref):
        c = c_ref[...]
        ca = c * _sigmoid(c)
        acc = ca[:, 0:1] * g_ref[0, 0:1, :]
        for b in range(1, N_DEV):
            acc = acc + ca[:, b:b + 1] * g_ref[0, b:b + 1, :]
        o_ref[0] = acc

    return pl.pallas_call(
        body, name=name, grid=(nl,),
        in_specs=[pl.BlockSpec((D, N_DEV), lambda l: (0, 0)),
                  pl.BlockSpec((1, N_DEV, N), lambda l: (l, 0, 0))],
        out_specs=pl.BlockSpec((1, D, N), lambda l: (l, 0, 0)),
        out_shape=jax.ShapeDtypeStruct((nl, D, N), F32),
        compiler_params=_params(("parallel",)),
    )(c_allT, dmod)


def adamw(w, g, m, v, name):
    R, C = w.shape
    tr = _tile(R, max(8, (1 << 19) // C // 8 * 8), 8)
    c1 = 1.0 - ADAM_B1 ** ADAM_STEP
    c2 = 1.0 - ADAM_B2 ** ADAM_STEP

    def body(w_ref, g_ref, m_ref, v_ref, d_ref, mo_ref, vo_ref):
        gv = g_ref[...]
        mn = ADAM_B1 * m_ref[...] + (1.0 - ADAM_B1) * gv
        vn = ADAM_B2 * v_ref[...] + (1.0 - ADAM_B2) * (gv * gv)
        mo_ref[...] = mn
        vo_ref[...] = vn
        d_ref[...] = -ADAM_LR * ((mn / c1) / (jnp.sqrt(vn / c2) + ADAM_EPS) + ADAM_WD * w_ref[...])

    blk = pl.BlockSpec((tr, C), lambda i: (i, 0))
    out = jax.ShapeDtypeStruct((R, C), F32)
    return pl.pallas_call(
        body, name=name, grid=(R // tr,),
        in_specs=[blk] * 4, out_specs=[blk] * 3, out_shape=[out] * 3,
        compiler_params=_params(("parallel",)),
    )(w, g, m, v)


def adamw_layer(w, g, m, v, l, prev, name):
    NLw, R, C = w.shape
    tr = _tile(R, max(8, (1 << 19) // C // 8 * 8), 8)
    nrb = R // tr
    c1 = 1.0 - ADAM_B1 ** ADAM_STEP
    c2 = 1.0 - ADAM_B2 ** ADAM_STEP
    w, m, v = (t.reshape(NLw * R, C) for t in (w, m, v))

    def body(w_ref, g_ref, m_ref, v_ref, *rest):
        go_ref, d_ref, mo_ref, vo_ref = rest[-4:]
        gv = g_ref[...]
        mn = ADAM_B1 * m_ref[...] + (1.0 - ADAM_B1) * gv
        vn = ADAM_B2 * v_ref[...] + (1.0 - ADAM_B2) * (gv * gv)
        go_ref[...] = gv
        mo_ref[...] = mn
        vo_ref[...] = vn
        d_ref[...] = -ADAM_LR * ((mn / c1) / (jnp.sqrt(vn / c2) + ADAM_EPS) + ADAM_WD * w_ref[...])

    lay = pl.BlockSpec((tr, C), lambda i: (l * nrb + i, 0))
    out = jax.ShapeDtypeStruct((NLw * R, C), F32)
    n_prev = 0 if prev is None else 4
    return pl.pallas_call(
        body, name=name, grid=(nrb,),
        in_specs=[lay, pl.BlockSpec((tr, C), lambda i: (i, 0)), lay, lay] + [pl.BlockSpec(memory_space=pl.ANY)] * n_prev,
        out_specs=[lay] * 4, out_shape=[out] * 4,
        input_output_aliases={4 + i: i for i in range(n_prev)},
        compiler_params=_params(("parallel",)),
    )(w, g, m, v, *(prev or ()))


def sum_slots(x, name):
    S, R, C = x.shape
    tr = _tile(R, 128, 8)

    def body(x_ref, o_ref):
        acc = x_ref[0]
        for s in range(1, S):
            acc = acc + x_ref[s]
        o_ref[...] = acc

    return pl.pallas_call(
        body, name=name, grid=(R // tr,),
        in_specs=[pl.BlockSpec((S, tr, C), lambda i: (0, i, 0))],
        out_specs=pl.BlockSpec((tr, C), lambda i: (i, 0)),
        out_shape=jax.ShapeDtypeStruct((R, C), F32),
        compiler_params=_params(("parallel",)),
    )(x)


def sum_halves(g, lands, c_idx, name):
    n, ns, _, rh, D = g.shape

    def body(c_ref, g_ref, l_ref, o_ref):
        o_ref[0, 0] = (g_ref[0, 0, 0].astype(F32) + l_ref[0, 0].astype(F32)).astype(BF16)

    return pl.pallas_call(
        body, name=name,
        grid_spec=pltpu.PrefetchScalarGridSpec(
            num_scalar_prefetch=1, grid=(n, ns),
            in_specs=[pl.BlockSpec((1, 1, 1, rh, D), lambda i, j, c: (i, j, c[0], 0, 0)),
                      pl.BlockSpec((1, 1, rh, D), lambda i, j, c: (i, j, 0, 0))],
            out_specs=pl.BlockSpec((1, 1, rh, D), lambda i, j, c: (i, j, 0, 0))),
        out_shape=jax.ShapeDtypeStruct((n, ns, rh, D), BF16),
        compiler_params=_params(("parallel", "parallel")),
    )(c_idx, g, lands)


def sum_chips(p, lands, place, name):
    n, ns, rh, D = p.shape

    def body(c_ref, p_ref, l_ref, o_ref):
        acc = p_ref[0, 0].astype(F32)
        for j in range(N_CHIP - 1):
            acc = acc + l_ref[j, 0].astype(F32)
        o_ref[0, 0] = acc

    return pl.pallas_call(
        body, name=name,
        grid_spec=pltpu.PrefetchScalarGridSpec(
            num_scalar_prefetch=1, grid=(n,),
            in_specs=[pl.BlockSpec((1, 1, rh, D), lambda i, c: (i, c[0], 0, 0)),
                      pl.BlockSpec((N_CHIP - 1, 1, rh, D), lambda i, c: (0, i, 0, 0))],
            out_specs=pl.BlockSpec((1, 1, rh, D), lambda i, c: (i, c[1], 0, 0))),
        out_shape=jax.ShapeDtypeStruct((n, 2, rh, D), F32),
        compiler_params=_params(("parallel",)),
    )(place, p, lands)


def _my_place():
    return lax.axis_index("x"), lax.axis_index("y"), lax.axis_index("c")


def _other_chips(mx, my):
    return [(1 - mx, my), (mx, 1 - my), (1 - mx, 1 - my)]


def gather_small(x, name):
    def body(x_ref, out_ref, sum_ref, send_sems, recv_sems):
        mx, my, mc = _my_place()
        me = 4 * mx + 2 * my + mc
        out_ref[me] = x_ref[...]
        sends = []
        for k in range(1, N_DEV):
            kx, ky, kc = (k >> 2) & 1, (k >> 1) & 1, k & 1
            peer = (1 - mx if kx else mx, 1 - my if ky else my, 1 - mc if kc else mc)
            cp = pltpu.make_async_remote_copy(
                src_ref=x_ref, dst_ref=out_ref.at[me], send_sem=send_sems.at[k - 1], recv_sem=recv_sems.at[k - 1],
                device_id=peer, device_id_type=MESH)
            cp.start()
            sends.append((cp, 4 * peer[0] + 2 * peer[1] + peer[2], peer))
        for k, (cp, peer_slot, peer) in enumerate(sends):
            pltpu.make_async_remote_copy(
                src_ref=x_ref, dst_ref=out_ref.at[peer_slot], send_sem=send_sems.at[k], recv_sem=recv_sems.at[k],
                device_id=peer, device_id_type=MESH).wait_recv()
        for cp, _, _ in sends:
            cp.wait_send()
        acc = out_ref[0]
        for s in range(1, N_DEV):
            acc = acc + out_ref[s]
        sum_ref[...] = acc

    vmem = pl.BlockSpec(memory_space=pltpu.VMEM)
    return pl.pallas_call(
        body, name=name,
        in_specs=[vmem], out_specs=[vmem, vmem],
        out_shape=[jax.ShapeDtypeStruct((N_DEV,) + x.shape, x.dtype), jax.ShapeDtypeStruct(x.shape, x.dtype)],
        scratch_shapes=[pltpu.SemaphoreType.DMA((N_DEV - 1,)), pltpu.SemaphoreType.DMA((N_DEV - 1,))],
        compiler_params=pltpu.CompilerParams(vmem_limit_bytes=VMEM_LIMIT),
    )(x)


_HBM =pl.BlockSpec(memory_space=pltpu.HBM)
_SEM = pl.BlockSpec(memory_space=pltpu.SEMAPHORE)
_DATAFLOW = pltpu.SideEffectType.DATAFLOW_SIDE_EFFECTING


def _gather_copies(shard, land, send, recv, base):
    mx, my, mc = _my_place()
    ci = 2 * mx + my
    peers = [((cx, cy, mc), 2 * cx + cy) for cx, cy in _other_chips(mx, my)] + [((mx, my, 1 - mc), ci)]
    out = []
    for q, (dev, src_slot) in enumerate(peers):
        out.append((
            pltpu.make_async_remote_copy(src_ref=shard, dst_ref=land.at[:, ci], send_sem=send.at[base + q],
                                         recv_sem=recv.at[base + q], device_id=dev, device_id_type=MESH),
            pltpu.make_async_remote_copy(src_ref=shard, dst_ref=land.at[:, src_slot], send_sem=send.at[base + q],
                                         recv_sem=recv.at[base + q], device_id=dev, device_id_type=MESH)))
    return out


def gather_start(groups, after, name):
    items = [s for g in groups for s in g]
    ni, ng = len(items), len(groups)

    def body(*refs):
        shards, lands = refs[:ni], refs[ni:2 * ni]
        sems = refs[2 * ni + 1:2 * ni + 1 + 2 * ng]
        token = refs[-1]
        i = 0
        for g, grp in enumerate(groups):
            for p in range(len(grp)):
                for start_cp, _ in _gather_copies(shards[i], lands[i], sems[2 * g], sems[2 * g + 1], 4 * p):
                    start_cp.start()
                i += 1
        token[...] = jnp.zeros_like(token)

    sem_shapes = []
    for grp in groups:
        sem_shapes += [pltpu.SemaphoreType.DMA((4 * len(grp),))] * 2
    land_shapes = [(s.shape[0], N_CHIP) + s.shape[1:] for s in items]
    outs = pl.pallas_call(
        body, name=name,
        in_specs=[_HBM] * (2 * ni) + [pl.BlockSpec(memory_space=pl.ANY)],
        out_specs=[_SEM] * (2 * ng) + [_HBM] * (2 * ni) + [pl.BlockSpec(memory_space=pltpu.VMEM)],
        out_shape=(sem_shapes + [pltpu.HBM(s.shape, s.dtype) for s in items]
                   + [pltpu.HBM(ls, s.dtype) for ls, s in zip(land_shapes, items)]
                   + [jax.ShapeDtypeStruct((8, 128), F32)]),
        input_output_aliases={i: 2 * ng + i for i in range(2 * ni)},
        compiler_params=pltpu.CompilerParams(has_side_effects=_DATAFLOW),
    )(*[pltpu.with_memory_space_constraint(s, pltpu.HBM) for s in items],
      *[pltpu.with_memory_space_constraint(lax.empty(ls, s.dtype), pltpu.HBM) for ls, s in zip(land_shapes, items)],
      after)
    sems, thru, token = outs[:2 * ng], outs[2 * ng:2 * ng + 2 * ni], outs[-1]
    handles, i = [], 0
    for g, grp in enumerate(groups):
        n = len(grp)
        handles.append((sems[2 * g], sems[2 * g + 1], thru[i:i + n], thru[ni + i:ni + i + n]))
        i += n
    return handles, token


def gather_wait(handle, after, name):
    send, recv, shards, lands = handle
    n = len(shards)

    def body(*refs):
        shard_refs, land_refs = refs[:n], refs[n:2 * n]
        send_ref, recv_ref = refs[2 * n], refs[2 * n + 1]
        for p in range(n):
            for start_cp, recv_cp in _gather_copies(shard_refs[p], land_refs[p], send_ref, recv_ref, 4 * p):
                start_cp.wait_send()
                recv_cp.wait_recv()

    outs = pl.pallas_call(
        body, name=name,
        in_specs=[_HBM] * (2 * n) + [_SEM, _SEM, pl.BlockSpec(memory_space=pl.ANY)],
        out_specs=[_HBM] * (2 * n),
        out_shape=[pltpu.HBM(s.shape, s.dtype) for s in shards] + [pltpu.HBM(l.shape, l.dtype) for l in lands],
        input_output_aliases={i: i for i in range(2 * n)},
        compiler_params=pltpu.CompilerParams(has_side_effects=_DATAFLOW),
    )(*shards, *lands, send, recv, after)
    return outs[n:]


def _sibling_copies(gs, lands, send, recv):
    mx, my, mc = _my_place()
    return [pltpu.make_async_remote_copy(
        src_ref=gs[k].at[:, :, 1 - mc], dst_ref=lands[k], send_sem=send.at[k], recv_sem=recv.at[k],
        device_id=(mx, my, 1 - mc), device_id_type=MESH) for k in range(len(gs))]


def sibling_start(gs, after, name):
    K = len(gs)

    def body(*refs):
        ins, lands = refs[:K], refs[K:2 * K]
        send, recv = refs[2 * K + 1], refs[2 * K + 2]
        for cp in _sibling_copies(ins, lands, send, recv):
            cp.start()
        refs[-1][...] = jnp.zeros_like(refs[-1])

    land_shapes = [g.shape[:2] + g.shape[3:] for g in gs]
    outs = pl.pallas_call(
        body, name=name,
        in_specs=[_HBM] * (2 * K) + [pl.BlockSpec(memory_space=pl.ANY)],
        out_specs=[_SEM, _SEM] + [_HBM] * (2 * K) + [pl.BlockSpec(memory_space=pltpu.VMEM)],
        out_shape=([pltpu.SemaphoreType.DMA((K,))] * 2 + [pltpu.HBM(g.shape, g.dtype) for g in gs]
                   + [pltpu.HBM(ls, g.dtype) for ls, g in zip(land_shapes, gs)] + [jax.ShapeDtypeStruct((8, 128), F32)]),
        input_output_aliases={i: 2 + i for i in range(2 * K)},
        compiler_params=pltpu.CompilerParams(has_side_effects=_DATAFLOW),
    )(*[pltpu.with_memory_space_constraint(g, pltpu.HBM) for g in gs],
      *[pltpu.with_memory_space_constraint(lax.empty(ls, g.dtype), pltpu.HBM) for ls, g in zip(land_shapes, gs)],
      after)
    return (outs[0], outs[1], outs[2:2 + K], outs[2 + K:2 + 2 * K]), outs[-1]


def sibling_wait(handle, after, name):
    send, recv, gs, lands = handle
    K = len(gs)

    def body(*refs):
        ins, land_refs = refs[:K], refs[K:2 * K]
        for cp in _sibling_copies(ins, land_refs, refs[2 * K], refs[2 * K + 1]):
            cp.wait_send()
            cp.wait_recv()

    outs = pl.pallas_call(
        body, name=name,
        in_specs=[_HBM] * (2 * K) + [_SEM, _SEM, pl.BlockSpec(memory_space=pl.ANY)],
        out_specs=[_HBM] * (2 * K),
        out_shape=[pltpu.HBM(g.shape, g.dtype) for g in gs] + [pltpu.HBM(l.shape, l.dtype) for l in lands],
        input_output_aliases={i: i for i in range(2 * K)},
        compiler_params=pltpu.CompilerParams(has_side_effects=_DATAFLOW),
    )(*gs, *lands, send, recv, after)
    return outs[:K], outs[K:]


def _small_copies(x, land, send, recv):
    mx, my, mc = _my_place()
    me = 4 * mx + 2 * my + mc
    out = []
    for k in range(1, N_DEV):
        peer = (1 - mx if k & 4 else mx, 1 - my if k & 2 else my, 1 - mc if k & 1 else mc)
        slot = 4 * peer[0] + 2 * peer[1] + peer[2]
        out.append(tuple(pltpu.make_async_remote_copy(
            src_ref=x, dst_ref=land.at[s], send_sem=send.at[k - 1], recv_sem=recv.at[k - 1],
            device_id=peer, device_id_type=MESH) for s in (me, slot)))
    return out


def small_start(x, after, name):
    def body(x_ref, land_ref, after_ref, send, recv, x_thru, land_thru, token):
        for mine, _ in _small_copies(x_ref, land_ref, send, recv):
            mine.start()
        token[...] = jnp.zeros_like(token)

    land_shape = (N_DEV,) + x.shape
    outs = pl.pallas_call(
        body, name=name,
        in_specs=[_HBM, _HBM, pl.BlockSpec(memory_space=pl.ANY)],
        out_specs=[_SEM, _SEM, _HBM, _HBM, pl.BlockSpec(memory_space=pltpu.VMEM)],
        out_shape=[pltpu.SemaphoreType.DMA((N_DEV - 1,))] * 2 + [pltpu.HBM(x.shape, x.dtype), pltpu.HBM(land_shape, x.dtype),
                                                                 jax.ShapeDtypeStruct((8, 128), F32)],
        input_output_aliases={0: 2, 1: 3},
        compiler_params=pltpu.CompilerParams(has_side_effects=_DATAFLOW),
    )(pltpu.with_memory_space_constraint(x, pltpu.HBM),
      pltpu.with_memory_space_constraint(lax.empty(land_shape, x.dtype), pltpu.HBM), after)
    return outs[:4], outs[4]


def small_wait(handle, after, name):
    send, recv, x, land = handle

    def body(x_ref, land_ref, send_ref, recv_ref, after_ref, x_out, land_out):
        for mine, theirs in _small_copies(x_ref, land_ref, send_ref, recv_ref):
            mine.wait_send()
            theirs.wait_recv()

    return pl.pallas_call(
        body, name=name,
        in_specs=[_HBM, _HBM, _SEM, _SEM, pl.BlockSpec(memory_space=pl.ANY)],
        out_specs=[_HBM, _HBM],
        out_shape=[pltpu.HBM(x.shape, x.dtype), pltpu.HBM(land.shape, land.dtype)],
        input_output_aliases={0: 0, 1: 1},
        compiler_params=pltpu.CompilerParams(has_side_effects=_DATAFLOW),
    )(x, land, send, recv, after)


def _scatter_copies(ps, lands, send, recv):
    mx, my, mc = _my_place()
    cps = []
    for j, (cx, cy) in enumerate(_other_chips(mx, my)):
        for k in range(len(ps)):
            cps.append(pltpu.make_async_remote_copy(
                src_ref=ps[k].at[:, 2 * cx + cy], dst_ref=lands[k].at[j],
                send_sem=send.at[k * 3 + j], recv_sem=recv.at[k * 3 + j],
                device_id=(cx, cy, mc), device_id_type=MESH))
    return cps


def scatter_start(ps, after, name):
    K = len(ps)

    def body(*refs):
        ins, lands = refs[:K], refs[K:2 * K]
        send, recv = refs[2 * K + 1], refs[2 * K + 2]
        for cp in _scatter_copies(ins, lands, send, recv):
            cp.start()
        refs[-1][...] = jnp.zeros_like(refs[-1])

    land_shapes = [(N_CHIP - 1, p.shape[0]) + p.shape[2:] for p in ps]
    outs = pl.pallas_call(
        body, name=name,
        in_specs=[_HBM] * (2 * K) + [pl.BlockSpec(memory_space=pl.ANY)],
        out_specs=[_SEM, _SEM] + [_HBM] * (2 * K) + [pl.BlockSpec(memory_space=pltpu.VMEM)],
        out_shape=([pltpu.SemaphoreType.DMA((3 * K,))] * 2 + [pltpu.HBM(p.shape, p.dtype) for p in ps]
                   + [pltpu.HBM(ls, p.dtype) for ls, p in zip(land_shapes, ps)] + [jax.ShapeDtypeStruct((8, 128), F32)]),
        input_output_aliases={i: 2 + i for i in range(2 * K)},
        compiler_params=pltpu.CompilerParams(has_side_effects=_DATAFLOW),
    )(*[pltpu.with_memory_space_constraint(p, pltpu.HBM) for p in ps],
      *[pltpu.with_memory_space_constraint(lax.empty(ls, p.dtype), pltpu.HBM) for ls, p in zip(land_shapes, ps)],
      after)
    return (outs[0], outs[1], outs[2:2 + K], outs[2 + K:2 + 2 * K]), outs[-1]


def scatter_wait(handle, after, name):
    send, recv, ps, lands = handle
    K = len(ps)

    def body(*refs):
        ins, land_refs = refs[:K], refs[K:2 * K]
        send_ref, recv_ref = refs[2 * K], refs[2 * K + 1]
        for cp in _scatter_copies(ins, land_refs, send_ref, recv_ref):
            cp.wait_send()
            cp.wait_recv()

    outs = pl.pallas_call(
        body, name=name,
        in_specs=[_HBM] * (2 * K) + [_SEM, _SEM, pl.BlockSpec(memory_space=pl.ANY)],
        out_specs=[_HBM] * (2 * K),
        out_shape=[pltpu.HBM(p.shape, p.dtype) for p in ps] + [pltpu.HBM(l.shape, l.dtype) for l in lands],
        input_output_aliases={i: i for i in range(2 * K)},
        compiler_params=pltpu.CompilerParams(has_side_effects=_DATAFLOW),
    )(*ps, *lands, send, recv, after)
    return outs[:K], outs[K:]


def sibling_complete(ss, name):
    K = len(ss)

    def body(*refs):
        ins, outs = refs[:K], refs[K:2 * K]
        send, recv = refs[2 * K:]
        mx, my, mc = _my_place()
        cps = []
        for k in range(K):
            cp = pltpu.make_async_remote_copy(
                src_ref=ins[k].at[:, mc], dst_ref=outs[k].at[:, mc], send_sem=send.at[k], recv_sem=recv.at[k],
                device_id=(mx, my, 1 - mc), device_id_type=MESH)
            cp.start()
            cps.append(cp)
        for k in range(K):
            pltpu.make_async_remote_copy(
                src_ref=ins[k].at[:, mc], dst_ref=outs[k].at[:, 1 - mc], send_sem=send.at[k], recv_sem=recv.at[k],
                device_id=(mx, my, 1 - mc), device_id_type=MESH).wait_recv()
        for cp in cps:
            cp.wait_send()

    hbm = pl.BlockSpec(memory_space=pl.ANY)
    return pl.pallas_call(
        body, name=name,
        in_specs=[hbm] * K, out_specs=[hbm] * K,
        out_shape=[jax.ShapeDtypeStruct(s.shape, s.dtype) for s in ss],
        scratch_shapes=[pltpu.SemaphoreType.DMA((K,)), pltpu.SemaphoreType.DMA((K,))],
        input_output_aliases={k: k for k in range(K)},
    )(*ss)


def _rope_tables(T):
    inv = ROPE_THETA ** (-jnp.arange(0, ATT_DH, 2, dtype=F32) / ATT_DH)
    ang = jnp.arange(T, dtype=F32)[:, None] * inv[None, :]
    ang = jnp.concatenate([ang, ang, ang, ang], axis=-1)
    return jnp.cos(ang), jnp.sin(ang)


def _ffn_fwd(h, y, mod, i0, get_up, get_down, norm_next, tag):
    wgu = get_up(h)
    a, b, s = ffn_up(y, (wgu, (0,)), (wgu, (1,)), f"ffn_up_{tag}")
    wd = get_down(s)
    outs = resid_matmul([s], (wd, (0,)), h, mod, i0 + 2, 0.5, f"ffn_down_{tag}", norm_next)
    hn, o = outs[0], outs[1]
    return hn, (outs[2] if norm_next else None), (h, y, a, b, s, o), ((wgu, (0,)), (wgu, (1,)), (wd, (0,)))


def _ffn_bwd(dh, do, res, ng, i_n, mod, i0, wgT, wuT, wd, on_grads, next_gate, tag):
    h, y, a, b, s, o = res
    F = _wrows(wgT)
    da, db = ffn_bwd_mid(do, wd, a, b, f"ffn_bwd_mid_{tag}")
    gbuf = lax.empty((3, F, h.shape[1]), BF16)
    gbuf = matmul_tn(da, y, gbuf, 0, 0, f"dwg_{tag}")
    gbuf = matmul_tn(db, y, gbuf, 1, 0, f"dwu_{tag}")
    gbuf = matmul_tn(s, do, gbuf, 2, 0, f"dwd_{tag}")
    token, then = on_grads([gbuf])
    outs = dy_normbwd([(da, 0, wgT, 0, F), (db, 0, wuT, 0, F)], h, dh, ng, i_n, mod + token, i0 + 1,
                      f"ffn_bwd_dy_{tag}", next_gate)
    return outs, then


def _mixer_fwd(h, y, mod, w_inT, w_out, sgu, cos, sin, norm_next, tag):
    lng, lnb, sw, swt, bcol = sgu
    proj = matmul_nt(y, w_inT, f"proj_{tag}")
    out_a = sgu_fwd(proj, lng, lnb, sw, bcol, f"sgu_fwd_{tag}")
    qkv = rope_fwd(proj, cos, sin, f"rope_fwd_{tag}")
    npat = len(DILATIONS)
    qkv_res = [tuple(qkv[3 * p:3 * p + 3]) for p in range(npat)]
    os_, lses = [], []
    for d, (qd, kd, vd) in zip(DILATIONS, qkv_res):
        o_d, lse_d = attn_fwd(qd, kd, vd, f"attn_fwd_d{d}_{tag}")
        os_.append(o_d)
        lses.append(lse_d)
    comb = attn_combine(os_, lses, f"attn_combine_{tag}")
    out_b, o_res, lse_res = comb[0], comb[1:1 + npat], comb[1 + npat:]
    outs = resid_matmul([out_a, out_b], w_out, h, mod, 5, 1.0, f"mix_out_{tag}", norm_next)
    hn, om = outs[0], outs[1]
    return hn, (outs[2] if norm_next else None), (h, y, proj, out_a, out_b, o_res, lse_res, qkv_res, om)


def _mixer_bwd(dh, dom, res, ng, mod, w_inT, w_out, sgu, cos, sin, on_grads, next_gate, tag):
    lng, lnb, sw, swt, bcol = sgu
    h, y, proj, out_a, out_b, o_res, lse_res, qkv_res, om = res
    D = h.shape[1]
    dmixed = matmul_nt(dom, w_out, f"dmixed_{tag}")
    woutbuf = lax.empty((1, 2 * MIX_HALF, D), BF16)
    woutbuf = matmul_tn(out_a, dom, woutbuf, 0, 0, f"dwout_a_{tag}", tmo_cap=MIX_HALF)
    woutbuf = matmul_tn(out_b, dom, woutbuf, 0, MIX_HALF, f"dwout_b_{tag}", tmo_cap=MIX_HALF)
    d_uv, d_sw, d_svec = sgu_bwd(proj, dmixed, lng, lnb, sw, swt, bcol, f"sgu_bwd_{tag}")
    do_res = to_residues(dmixed, 1, f"dout_res_{tag}")
    dqs, dks, dvs = [], [], []
    for p, (d, (qd, kd, vd)) in enumerate(zip(DILATIONS, qkv_res)):
        dq, dk, dv = attn_bwd(qd, kd, vd, do_res[p], o_res[p], lse_res[p], f"attn_bwd_d{d}_{tag}")
        dqs.append(dq)
        dks.append(dk)
        dvs.append(dv)
    d_qkv = rope_bwd(dqs, dks, dvs, cos, sin, f"rope_bwd_{tag}")
    winbuf = lax.empty((1, 5 * MIX_HALF, D), BF16)
    winbuf = matmul_tn(d_uv, y, winbuf, 0, 0, f"dwin_uv_{tag}", tmo_cap=MIX_HALF)
    winbuf = matmul_tn(d_qkv, y, winbuf, 0, 2 * MIX_HALF, f"dwin_qkv_{tag}", tmo_cap=MIX_HALF)
    token, then = on_grads([winbuf, woutbuf])
    pairs = [(d_uv, 0, w_inT, 0, 2 * MIX_HALF), (d_qkv, 0, w_inT, 1, 2 * MIX_HALF), (d_qkv, 2, w_inT, 4, MIX_HALF)]
    outs = dy_normbwd(pairs, h, dh, ng, 1, mod + token, 4, f"mix_bwd_dy_{tag}", next_gate)
    return outs, d_sw, d_svec, then


def _local_step(x, tgt, mods, ngs, get_w, sgus, gf, on_block_grads, on_layer_small):
    T, D = x.shape
    cos, sin = _rope_tables(T)
    h = x
    saved, weights = [], []
    for l in range(2):
        def getter(blk, l=l):
            return lambda after: get_w(l, blk, after)

        if l == 0:
            y = normmod_fwd(h, ngs[0], 0, mods[0], 0, 1, "normmod_l0f1")
        h, y, r1, wf1 = _ffn_fwd(h, y, mods[l], 0, getter("f1u"), getter("f1d"), (ngs[l], 1, mods[l], 3, 4), f"l{l}f1")
        w_inT, w_out = get_w(l, "mx", h)
        h, y, r2 = _mixer_fwd(h, y, mods[l], (w_inT, (0,)), (w_out, (0,)), sgus[l], cos, sin,
                              (ngs[l], 2, mods[l], 6, 7), f"l{l}mx")
        h, y, r3, wf2 = _ffn_fwd(h, y, mods[l], 6, getter("f2u"), getter("f2d"),
                                 (ngs[l + 1], 0, mods[l + 1], 0, 1) if l + 1 < 2 else None, f"l{l}f2")
        saved.append((r1, r2, r3))
        weights.append((wf1, w_inT, w_out, wf2))
    def gate_of(l, blk):
        r1, r2, r3 = saved[l]
        o, i_g, coef = {"f2": (r3[5], 8, 0.5), "mx": (r2[-1], 5, 1.0), "f1": (r1[5], 2, 0.5)}[blk]
        return o, mods[l], i_g, coef

    seq = [(l, blk) for l in (1, 0) for blk in ("f2", "mx", "f1")]
    dh, red_final, do, red_g = final_loss_bwd(h, gf, tgt, gate_of(*seq[0]), "final_loss_bwd")
    rn, rg = {}, {}
    for idx, (l, blk) in enumerate(seq):
        r1, r2, r3 = saved[l]
        wf1, w_inT, w_out, wf2 = weights[l]
        nxt = gate_of(*seq[idx + 1]) if idx + 1 < len(seq) else None
        rg[blk] = red_g
        tag = f"l{l}{blk}"

        def on(arrays, l=l, blk=blk):
            return on_block_grads(l, blk, arrays)

        if blk == "f2":
            outs, then = _ffn_bwd(dh, do, r3, ngs[l], 2, mods[l], 6, *wf2, on, nxt, tag)
        elif blk == "mx":
            outs, d_sw, d_svec, then = _mixer_bwd(dh, do, r2, ngs[l], mods[l], (w_inT, (0,)), (w_out, (0,)), sgus[l],
                                                  cos, sin, on, nxt, tag)
        else:
            outs, then = _ffn_bwd(dh, do, r1, ngs[l], 0, mods[l], 0, *wf1, on, nxt, tag)
        dh, rn[blk] = outs[0], outs[1]
        if nxt is not None:
            do, red_g = outs[2], outs[3]
        if blk == "f1":
            mods = mods + on_layer_small(l, dict(sgu_w=d_sw, sgu_vec=d_svec, red_n=(rn["f1"], rn["mx"], rn["f2"]),
                                                 red_g=(rg["f1"], rg["mx"], rg["f2"])),
                                         red_final if l == 0 else None)
            mods = mods + then(mods)
        else:
            mods = mods + then(dh)
    return dh


def _adam_out(w, g, m, v, name):
    shp = w.shape
    two_d = (-1, shp[-1])
    d, mn, vn = adamw(w.reshape(two_d), g.reshape(two_d), m.reshape(two_d), v.reshape(two_d), name)
    return g, d.reshape(shp), mn.reshape(shp), vn.reshape(shp)


def kernel(x, c, ada_w, ada_b, norm_g, ffn1_wg, ffn1_wu, ffn1_wd, ffn2_wg, ffn2_wu, ffn2_wd, w_in, sgu_ln_g, sgu_ln_b, sgu_w, sgu_b, w_out, final_g, loss_target, m_ada_w, m_ada_b, m_norm_g, m_ffn1_wg, m_ffn1_wu, m_ffn1_wd, m_ffn2_wg, m_ffn2_wu, m_ffn2_wd, m_w_in, m_sgu_ln_g, m_sgu_ln_b, m_sgu_w, m_sgu_b, m_w_out, m_final_g, v_ada_w, v_ada_b, v_norm_g, v_ffn1_wg, v_ffn1_wu, v_ffn1_wd, v_ffn2_wg, v_ffn2_wu, v_ffn2_wd, v_w_in, v_sgu_ln_g, v_sgu_ln_b, v_sgu_w, v_sgu_b, v_w_out, v_final_g):
    T, D = x.shape[1], x.shape[2]
    NL = ada_w.shape[0]
    mx, my, mc = _my_place()
    me = 4 * mx + 2 * my + mc
    ci = 2 * mx + my
    c_idx = jnp.reshape(mc, (1,)).astype(jnp.int32)
    place = jnp.stack([ci, mc]).astype(jnp.int32)

    ngw = norm_g.shape[2]
    small_in = jnp.concatenate([jnp.pad(c, ((0, 7), (0, 0))),
                                jnp.pad(norm_g.reshape(NL * 3, ngw), ((0, 8 - NL * 3), (0, D - ngw)))], axis=0)
    small_all, _ = gather_small(small_in, "gather_c_normg")
    c_all = small_all[:, 0, :]
    ng_parts = small_all[0::2, 8:8 + NL * 3, :ngw]
    ngs = jnp.transpose(ng_parts, (1, 0, 2)).reshape(NL, 3, N_CHIP * ngw)

    nmod = ada_w.shape[2]
    ada_b_mine = lax.dynamic_slice_in_dim(ada_b, ci * nmod, nmod, axis=1).reshape(NL, 1, nmod)
    mod_part = ada_fwd(c_all, ada_w, ada_b_mine, "ada_fwd")
    mod_all, _ = gather_small(mod_part.reshape(NL * N_DEV, nmod), "gather_mod")
    mod_rows = lax.dynamic_index_in_dim(mod_all.reshape(N_DEV, NL, N_DEV, nmod), me, axis=2, keepdims=False)
    mods = jnp.transpose(mod_rows[0::2], (1, 0, 2)).reshape(NL, N_ADA, D)

    sgus = []
    for l in range(NL):
        sgus.append((sgu_ln_g[l].reshape(1, MIX_HALF), sgu_ln_b[l].reshape(1, MIX_HALF), sgu_w[l],
                     jnp.swapaxes(sgu_w[l], 1, 2), jnp.transpose(sgu_b[l])))

    def halves(a):
        n, r, _ = a.shape
        return a.reshape(n, 2, r // 2, D)

    Fs = ffn1_wd.shape[1]
    groups = []
    for l in range(NL):
        groups += [[halves(jnp.stack([ffn1_wg[l].T, ffn1_wu[l].T], axis=0).astype(BF16))],
                   [halves(ffn1_wd[l].astype(BF16)[None])],
                   [halves(w_in[l].T.astype(BF16)[None]), halves(w_out[l].astype(BF16)[None])],
                   [halves(jnp.stack([ffn2_wg[l].T, ffn2_wu[l].T], axis=0).astype(BF16))],
                   [halves(ffn2_wd[l].astype(BF16)[None])]]
    handles, token = gather_start(groups, mods, "gather_start")
    mods = mods + token[0, 0]
    group_no = {"f1u": 0, "f1d": 1, "mx": 2, "f2u": 3, "f2d": 4}

    def get_w(l, key, after):
        full = gather_wait(handles[len(group_no) * l + group_no[key]], after, f"gather_wait_l{l}{key}")
        full = [a.reshape(a.shape[0], N_CHIP * 2 * a.shape[3], D) for a in full]
        return full[0] if key != "mx" else tuple(full)

    def split(a):
        n, r4, _ = a.shape
        return a.reshape(n, N_CHIP, 2, r4 // N_CHIP // 2, D)

    pending, small_pending, small_tokens = {}, {}, {}

    def on_block_grads(l, blk, bufs):
        tag = f"l{l}{blk}"
        sib, tok1 = sibling_start([split(g) for g in bufs], place, f"rs_sibling_start_{tag}")

        def then(after):
            parts, lands = sibling_wait(sib, after, f"rs_sibling_wait_{tag}")
            psums = [sum_halves(g, ld, c_idx, f"rs_sum_halves_{tag}_{i}") for i, (g, ld) in enumerate(zip(parts, lands))]
            pending[(l, blk)], tok2 = scatter_start(psums, lands[0], f"rs_chips_start_{tag}")
            return tok2[0, 0]

        return tok1[0, 0], then

    def blocks_finish(blocks, after, tag):
        ssums, counts = [], []
        for l, blk in blocks:
            psums, lands2 = scatter_wait(pending.pop((l, blk)), after, f"rs_chips_wait_l{l}{blk}")
            ssums += [sum_chips(p, ld, place, f"rs_sum_chips_l{l}{blk}_{i}") for i, (p, ld) in enumerate(zip(psums, lands2))]
            counts.append(len(psums))
        fins = [f.reshape(f.shape[0], -1, D) for f in sibling_complete(ssums, f"rs_complete_{tag}")]
        out, i = [], 0
        for n in counts:
            out.append(fins[i:i + n])
            i += n
        return out

    def on_layer_small(l, grads, red_final):
        blocks = list(grads["red_n"]) + list(grads["red_g"])
        blocks.append(jnp.pad(grads["sgu_vec"], ((0, 0), (0, D - MIX_HALF))))
        blocks.append(grads["sgu_w"].reshape(-1, D))
        if red_final is not None:
            blocks.append(red_final)
        xs = jnp.concatenate(blocks, axis=0)
        small_pending[l], small_tokens[l] = small_start(xs, place, f"small_start_l{l}")
        return small_tokens[l][0, 0]

    grad_x = _local_step(x[0], loss_target[0], mods, ngs, get_w, sgus, final_g.reshape(1, D),
                         on_block_grads, on_layer_small)

    adam_state = {}

    def adam_big(nm, l, g, w, m, v):
        adam_state[nm] = adamw_layer(w, g, m, v, l, adam_state.get(nm), f"adamw_{nm}_l{l}")

    def adam_block(l, blk, fin):
        if blk == "mx":
            adam_big("w_in", l, fin[0][0].T, w_in, m_w_in, v_w_in)
            adam_big("w_out", l, fin[1][0], w_out, m_w_out, v_w_out)
        else:
            ws = ((ffn1_wg, m_ffn1_wg, v_ffn1_wg), (ffn1_wu, m_ffn1_wu, v_ffn1_wu), (ffn1_wd, m_ffn1_wd, v_ffn1_wd)) \
                if blk == "f1" else \
                ((ffn2_wg, m_ffn2_wg, v_ffn2_wg), (ffn2_wu, m_ffn2_wu, v_ffn2_wu), (ffn2_wd, m_ffn2_wd, v_ffn2_wd))
            pre = "ffn1" if blk == "f1" else "ffn2"
            for k, (nm, tr) in enumerate((("wg", True), ("wu", True), ("wd", False))):
                adam_big(f"{pre}_{nm}", l, fin[0][k].T if tr else fin[0][k], *ws[k])

    done_order = [(l, blk) for l in range(NL - 1, -1, -1) for blk in ("f2", "mx", "f1")]
    for (l, blk), fin in zip(done_order[:-1], blocks_finish(done_order[:-1], small_tokens[0], "early")):
        adam_block(l, blk, fin)
    last_big = adam_state["w_out"][1]

    small_sum, small_all = [], []
    for l in range(NL):
        xs, land = small_wait(small_pending[l], last_big, f"small_wait_l{l}")
        full = lax.dynamic_update_slice(land, xs[None], (me, 0, 0))
        small_all.append(full)
        small_sum.append(sum_slots(full, f"small_sum_l{l}"))
    offs = [8 * i for i in range(8)]
    off_final = offs[7] + SGU_HEADS * ATT_BLOCK * HEAD_LANES // D
    loss = small_sum[0][off_final + 1, 0]
    g_final_g = small_sum[0][off_final, :]
    g_norm_g, g_ada_b, g_lng, g_lnb, g_sb, g_sw, dmod_all = [], [], [], [], [], [], []
    for l in range(NL):
        rn = [small_sum[l][offs[i]:offs[i] + 8] for i in range(3)]
        rg = [small_sum[l][offs[3 + i]:offs[3 + i] + 8] for i in range(3)]
        g_norm_g.append(jnp.stack([rn[i][2] for i in range(3)], axis=0))
        g_ada_b.append(jnp.concatenate([jnp.stack([rn[i][0], rn[i][1], rg[i][0]], axis=0) for i in range(3)],
                                       axis=0).reshape(N_ADA * D))
        sv = small_sum[l][offs[6]:offs[6] + 8, :MIX_HALF]
        g_lng.append(sv[0].reshape(SGU_HEADS, HEAD_LANES))
        g_lnb.append(sv[1].reshape(SGU_HEADS, HEAD_LANES))
        g_sb.append(sv[2].reshape(SGU_HEADS, ATT_BLOCK))
        g_sw.append(small_sum[l][offs[7]:off_final].reshape(sgu_w.shape[1:]))
        rows = []
        for i in range(3):
            an = small_all[l][:, offs[i]:offs[i] + 2]
            ag = small_all[l][:, offs[3 + i]:offs[3 + i] + 1]
            rows += [an[:, 0], an[:, 1], ag[:, 0]]
        dmod_all.append(jnp.stack(rows, axis=1).reshape(N_DEV, N_ADA * D))
    dmod_all = jnp.stack(dmod_all, axis=0)
    dmod_mine = lax.dynamic_slice_in_dim(dmod_all, ci * nmod, nmod, axis=2)
    g_ada_w = ada_bwd(jnp.transpose(c_all), dmod_mine, "ada_bwd")
    g_ada_b = jnp.stack(g_ada_b, axis=0)
    g_norm_g_full = jnp.stack(g_norm_g, axis=0)
    g_norm_g_mine = lax.dynamic_slice_in_dim(g_norm_g_full, ci * ngw, ngw, axis=2)

    small_params = [
        ("ada_w", ada_w, g_ada_w, m_ada_w, v_ada_w),
        ("ada_b", ada_b, g_ada_b, m_ada_b, v_ada_b),
        ("norm_g", norm_g, g_norm_g_mine, m_norm_g, v_norm_g),
        ("sgu_ln_g", sgu_ln_g, jnp.stack(g_lng, axis=0), m_sgu_ln_g, v_sgu_ln_g),
        ("sgu_ln_b", sgu_ln_b, jnp.stack(g_lnb, axis=0), m_sgu_ln_b, v_sgu_ln_b),
        ("sgu_w", sgu_w, jnp.stack(g_sw, axis=0), m_sgu_w, v_sgu_w),
        ("sgu_b", sgu_b, jnp.stack(g_sb, axis=0), m_sgu_b, v_sgu_b),
        ("final_g", final_g.reshape(1, D), g_final_g.reshape(1, D), m_final_g.reshape(1, D), v_final_g.reshape(1, D)),
    ]
    for nm, w, g, m, v in small_params:
        res = _adam_out(w, g, m, v, f"adamw_{nm}")
        adam_state[nm] = tuple(t.reshape(D) for t in res) if nm == "final_g" else res

    l, blk = done_order[-1]
    adam_block(l, blk, blocks_finish([(l, blk)], adam_state["ada_w"][1], "last")[0])

    names = ["ada_w", "ada_b", "norm_g", "ffn1_wg", "ffn1_wu", "ffn1_wd", "ffn2_wg", "ffn2_wu", "ffn2_wd", "w_in",
             "sgu_ln_g", "sgu_ln_b", "sgu_w", "sgu_b", "w_out", "final_g"]
    shapes = [t.shape for t in (ada_w, ada_b, norm_g, ffn1_wg, ffn1_wu, ffn1_wd, ffn2_wg, ffn2_wu, ffn2_wd, w_in,
                                sgu_ln_g, sgu_ln_b, sgu_w, sgu_b, w_out, final_g)]
    return (loss, grad_x[None], *[adam_state[nm][i].reshape(s) for i in range(4) for nm, s in zip(names, shapes)])
```

```python
import math

import jax
import jax.numpy as jnp
from jax import lax
from jax.experimental import pallas as pl
from jax.experimental.pallas import tpu as pltpu

F32 = jnp.float32
BF16 = jnp.bfloat16
EPS = 1e-6
SGU_HEADS = 4
HEAD_LANES = 128
ATT_DH = 64
ATT_BLOCK = 128
MIX_HALF = SGU_HEADS * HEAD_LANES
DILATIONS = (1, 4, 16)
ROPE_THETA = 10000.0
N_ADA = 9
ADAM_LR, ADAM_B1, ADAM_B2, ADAM_EPS, ADAM_WD, ADAM_STEP = 0.001, 0.9, 0.999, 1e-08, 0.01, 10
NEG = -1e30
V7X_VMEM_BYTES = 64 * 1024 * 1024
VMEM_LIMIT = V7X_VMEM_BYTES * 7 // 8
MESH = pl.DeviceIdType.MESH
N_DEV = 8
N_CHIP = 4


def _tile(n, cap, mult):
    if n <= cap:
        return n
    t = (cap // mult) * mult
    while t >= mult:
        if n % t == 0:
            return t
        t -= mult
    raise ValueError((n, cap, mult))


def _params(dims=None):
    return pltpu.CompilerParams(dimension_semantics=dims, vmem_limit_bytes=VMEM_LIMIT)


def _wspec(w, rows, idx, resident=False):
    arr, lead = w
    kw = dict(pipeline_mode=pl.Buffered(1)) if resident else {}
    return pl.BlockSpec((None,) * len(lead) + (rows, arr.shape[-1]), lambda *g: tuple(lead) + (idx(*g), 0), **kw)


def _wrows(w):
    return w[0].shape[-2]


def _nt(a, b):
    return lax.dot_general(a, b, (((1,), (1,)), ((), ())), preferred_element_type=F32)


def _tn(a, b):
    return lax.dot_general(a, b, (((0,), (0,)), ((), ())), preferred_element_type=F32)


def _nn(a, b):
    return jnp.dot(a, b, preferred_element_type=F32)


def _sigmoid(x):
    return 0.5 * jnp.tanh(0.5 * x) + 0.5


_GELU_K = math.sqrt(2.0 / math.pi)
_GELU_C = 0.044715


def _gelu(x):
    t = jnp.tanh(_GELU_K * (x + _GELU_C * x * x * x))
    return 0.5 * x * (1.0 + t)


def _gelu_and_grad(x):
    x2 = x * x
    t = jnp.tanh(_GELU_K * (x + _GELU_C * x * x2))
    g = 0.5 * x * (1.0 + t)
    dg = 0.5 * (1.0 + t) + 0.5 * x * (1.0 - t * t) * (_GELU_K * (1.0 + 3.0 * _GELU_C * x2))
    return g, dg


def normmod_fwd(h, ng, i_n, mod, i_sh, i_sc, name):
    T, D = h.shape
    tm = _tile(T, 512, 8)

    def body(h_ref, ng_ref, mod_ref, y_ref):
        y_ref[...] = _normmod(h_ref[...], ng_ref[i_n:i_n + 1, :], mod_ref[i_sh:i_sh + 1, :],
                              mod_ref[i_sc:i_sc + 1, :]).astype(BF16)

    return pl.pallas_call(
        body, name=name, grid=(T // tm,),
        in_specs=[pl.BlockSpec((tm, D), lambda i: (i, 0)),
                  pl.BlockSpec(ng.shape, lambda i: (0, 0)),
                  pl.BlockSpec(mod.shape, lambda i: (0, 0))],
        out_specs=pl.BlockSpec((tm, D), lambda i: (i, 0)),
        out_shape=jax.ShapeDtypeStruct((T, D), BF16),
        compiler_params=_params(("parallel",)),
    )(h, ng, mod)


def ffn_up(y, wgT, wuT, name):
    T, D = y.shape
    F = _wrows(wgT)
    tm = _tile(T, 512, 16)
    tf = _tile(F, 2816, 256)
    cuts = list(range(0, tf, 768)) + [tf]

    def body(y_ref, wg_ref, wu_ref, a_ref, b_ref, s_ref):
        yv = y_ref[...]
        for c0, c1 in zip(cuts[:-1], cuts[1:]):
            a = _nt(yv, wg_ref[c0:c1, :])
            b = _nt(yv, wu_ref[c0:c1, :])
            a_ref[:, c0:c1] = a.astype(BF16)
            b_ref[:, c0:c1] = b.astype(BF16)
            s_ref[:, c0:c1] = (a * _sigmoid(a) * b).astype(BF16)

    act = jax.ShapeDtypeStruct((T, F), BF16)
    return pl.pallas_call(
        body, name=name, grid=(F // tf, T // tm),
        in_specs=[pl.BlockSpec((tm, D), lambda j, i: (i, 0)),
                  _wspec(wgT, tf, lambda j, i: j, resident=True),
                  _wspec(wuT, tf, lambda j, i: j, resident=True)],
        out_specs=[pl.BlockSpec((tm, tf), lambda j, i: (i, j))] * 3,
        out_shape=[act, act, act],
        compiler_params=_params(("parallel", "parallel")),
    )(y, wgT[0], wuT[0])


def _normmod(x, gn, sh, sc):
    r = lax.rsqrt(jnp.mean(x * x, axis=-1, keepdims=True) + EPS)
    return ((x * r) * gn) * (1.0 + sc) + sh


def resid_matmul(xs, w, h, mod, i_g, coef, name, norm_next=None):
    T, D = h.shape
    kb = xs[0].shape[1]
    assert all(x.shape == (T, kb) for x in xs) and _wrows(w) == kb * len(xs)
    tm = _tile(T, 1024, 16)
    nx = len(xs)
    n_in, n_out, n_shape, n_ops = [], [], [], []
    if norm_next:
        ng_n, i_n, mod_n, i_sh, i_sc = norm_next
        n_in = [pl.BlockSpec(ng_n.shape, lambda i: (0, 0)), pl.BlockSpec(mod_n.shape, lambda i: (0, 0))]
        n_out = [pl.BlockSpec((tm, D), lambda i: (i, 0))]
        n_shape = [jax.ShapeDtypeStruct((T, D), BF16)]
        n_ops = [ng_n, mod_n]

    def body(*refs):
        x_refs, w_refs = refs[:nx], refs[nx:2 * nx]
        h_ref, mod_ref = refs[2 * nx:2 * nx + 2]
        hn_ref, o_ref = refs[2 * nx + 2 + len(n_in):2 * nx + 4 + len(n_in)]
        o = _nn(x_refs[0][...], w_refs[0][...])
        for xr, wr in zip(x_refs[1:], w_refs[1:]):
            o = o + _nn(xr[...], wr[...])
        o_ref[...] = o.astype(BF16)
        hn = h_ref[...] + (coef * mod_ref[i_g:i_g + 1, :]) * o
        hn_ref[...] = hn
        if norm_next:
            ng_ref, modn_ref = refs[2 * nx + 2], refs[2 * nx + 3]
            refs[-1][...] = _normmod(hn, ng_ref[i_n:i_n + 1, :], modn_ref[i_sh:i_sh + 1, :],
                                     modn_ref[i_sc:i_sc + 1, :]).astype(BF16)

    return pl.pallas_call(
        body, name=name, grid=(T // tm,),
        in_specs=([pl.BlockSpec((tm, kb), lambda i: (i, 0))] * nx
                  + [_wspec(w, kb, lambda i, p=p: p, resident=True) for p in range(nx)]
                  + [pl.BlockSpec((tm, D), lambda i: (i, 0)),
                     pl.BlockSpec(mod.shape, lambda i: (0, 0))] + n_in),
        out_specs=[pl.BlockSpec((tm, D), lambda i: (i, 0))] * 2 + n_out,
        out_shape=[jax.ShapeDtypeStruct((T, D), F32), jax.ShapeDtypeStruct((T, D), BF16)] + n_shape,
        compiler_params=_params(("parallel",)),
    )(*xs, *([w[0]] * nx), h, mod, *n_ops)


def _gate_specs(gate, tm, D):
    o, mod, _, _ = gate
    T = o.shape[0]
    return ([pl.BlockSpec((tm, D), lambda i: (i, 0)), pl.BlockSpec(mod.shape, lambda i: (0, 0))],
            [pl.BlockSpec((tm, D), lambda i: (i, 0)), pl.BlockSpec((8, D), lambda i: (0, 0))],
            [jax.ShapeDtypeStruct((T, D), BF16), jax.ShapeDtypeStruct((8, D), F32)],
            [o, mod])


def _gate_emit(d, gate, o_ref, mod_ref, do_ref, red_ref):
    _, _, i_g, coef = gate
    do_ref[...] = (d * (coef * mod_ref[i_g:i_g + 1, :])).astype(BF16)

    @pl.when(pl.program_id(0) == 0)
    def _():
        red_ref[...] = jnp.zeros_like(red_ref)

    red_ref[0:1, :] += coef * jnp.sum(d * o_ref[...].astype(F32), axis=0, keepdims=True)


def ffn_bwd_mid(do, wd, a, b, name):
    T, D = do.shape
    F = _wrows(wd)
    tm = _tile(T, 512, 16)
    tf = _tile(F, 2816, 256)
    cuts = list(range(0, tf, 256)) + [tf]

    def body(do_ref, wd_ref, a_ref, b_ref, da_ref, db_ref):
        dov = do_ref[...]
        for c0, c1 in zip(cuts[:-1], cuts[1:]):
            ds = _nt(dov, wd_ref[c0:c1, :])
            av = a_ref[:, c0:c1].astype(F32)
            bv = b_ref[:, c0:c1].astype(F32)
            sig = _sigmoid(av)
            da_ref[:, c0:c1] = (ds * bv * (sig * (1.0 + av * (1.0 - sig)))).astype(BF16)
            db_ref[:, c0:c1] = (ds * (av * sig)).astype(BF16)

    act = jax.ShapeDtypeStruct((T, F), BF16)
    return pl.pallas_call(
        body, name=name, grid=(F // tf, T // tm),
        in_specs=[pl.BlockSpec((tm, D), lambda j, i: (i, 0)),
                  _wspec(wd, tf, lambda j, i: j, resident=True),
                  pl.BlockSpec((tm, tf), lambda j, i: (i, j)),
                  pl.BlockSpec((tm, tf), lambda j, i: (i, j))],
        out_specs=[pl.BlockSpec((tm, tf), lambda j, i: (i, j))] * 2,
        out_shape=[act, act],
        compiler_params=_params(("parallel", "parallel")),
    )(do, wd[0], a, b)


def dy_normbwd(pairs, h, dhp, ng, i_n, mod, i_sc, name, gate=None):
    T, D = h.shape
    tm = _tile(T, 512, 16)
    npair = len(pairs)
    g_in, g_out, g_shape, g_ops = _gate_specs(gate, tm, D) if gate else ([], [], [], [])

    def body(*refs):
        x_refs, w_refs = refs[:npair], refs[npair:2 * npair]
        h_ref, dhp_ref, ng_ref, mod_ref = refs[2 * npair:2 * npair + 4]
        dh_ref, red_ref = refs[2 * npair + 4 + len(g_in):2 * npair + 6 + len(g_in)]
        dy = _nn(x_refs[0][...], w_refs[0][...])
        for xr, wr in zip(x_refs[1:], w_refs[1:]):
            dy = dy + _nn(xr[...], wr[...])
        x = h_ref[...]
        r = lax.rsqrt(jnp.mean(x * x, axis=-1, keepdims=True) + EPS)
        n = x * r
        gn = ng_ref[i_n:i_n + 1, :]
        dnh = dy * (1.0 + mod_ref[i_sc:i_sc + 1, :])

        @pl.when(pl.program_id(0) == 0)
        def _():
            red_ref[...] = jnp.zeros_like(red_ref)

        red_ref[0:1, :] += jnp.sum(dy, axis=0, keepdims=True)
        red_ref[1:2, :] += jnp.sum(dy * (n * gn), axis=0, keepdims=True)
        red_ref[2:3, :] += jnp.sum(dnh * n, axis=0, keepdims=True)
        dn = dnh * gn
        dh_new = dhp_ref[...] + r * (dn - n * jnp.mean(dn * n, axis=-1, keepdims=True))
        dh_ref[...] = dh_new
        if gate:
            _gate_emit(dh_new, gate, refs[2 * npair + 4], refs[2 * npair + 5], refs[-2], refs[-1])

    in_specs = ([pl.BlockSpec((tm, kb), lambda i, c=c: (i, c)) for (_, c, _, _, kb) in pairs]
                + [_wspec(w, kb, lambda i, r=r: r, resident=True) for (_, _, w, r, kb) in pairs]
                + [pl.BlockSpec((tm, D), lambda i: (i, 0)),
                   pl.BlockSpec((tm, D), lambda i: (i, 0)),
                   pl.BlockSpec(ng.shape, lambda i: (0, 0)),
                   pl.BlockSpec(mod.shape, lambda i: (0, 0))] + g_in)
    return pl.pallas_call(
        body, name=name, grid=(T // tm,), in_specs=in_specs,
        out_specs=[pl.BlockSpec((tm, D), lambda i: (i, 0)), pl.BlockSpec((8, D), lambda i: (0, 0))] + g_out,
        out_shape=[jax.ShapeDtypeStruct((T, D), F32), jax.ShapeDtypeStruct((8, D), F32)] + g_shape,
        compiler_params=_params(("arbitrary",)),
    )(*[p[0] for p in pairs], *[p[2][0] for p in pairs], h, dhp, ng, mod, *g_ops)


def matmul_tn(a, b, buf, slot, row0, name, tmo_cap=1408):
    T, N = b.shape
    ma = a.shape[1]
    tmo = _tile(ma, tmo_cap, 128)
    assert row0 % tmo == 0
    nmo = ma // tmo
    tk = _tile(T, 2048, 16)
    nk = T // tk

    def body(a_ref, b_ref, buf_ref, o_ref, acc_ref):
        k = pl.program_id(1)

        @pl.when(k == 0)
        def _():
            acc_ref[...] = jnp.zeros_like(acc_ref)

        acc_ref[...] += _tn(a_ref[...], b_ref[...])

        @pl.when(k == nk - 1)
        def _():
            o_ref[...] = acc_ref[...].astype(BF16)

    return pl.pallas_call(
        body, name=name, grid=(nmo, nk),
        in_specs=[pl.BlockSpec((tk, tmo), lambda j, k: (k, j)),
                  pl.BlockSpec((tk, N), lambda j, k: (k, 0)),
                  pl.BlockSpec(memory_space=pl.ANY)],
        out_specs=pl.BlockSpec((None, tmo, N), lambda j, k: (slot, row0 // tmo + j, 0)),
        out_shape=jax.ShapeDtypeStruct(buf.shape, BF16),
        scratch_shapes=[pltpu.VMEM((tmo, N), F32)],
        input_output_aliases={2: 0},
        compiler_params=_params(("parallel", "arbitrary")),
    )(a, b, buf)


def matmul_nt(x, w, name):
    T, K = x.shape
    N = _wrows(w)
    tm = _tile(T, 1024, 16)
    tn = _tile(N, 1280, 128)

    def body(x_ref, w_ref, o_ref):
        o_ref[...] = _nt(x_ref[...], w_ref[...]).astype(BF16)

    return pl.pallas_call(
        body, name=name, grid=(N // tn, T // tm),
        in_specs=[pl.BlockSpec((tm, K), lambda j, i: (i, 0)), _wspec(w, tn, lambda j, i: j)],
        out_specs=pl.BlockSpec((tm, tn), lambda j, i: (i, j)),
        out_shape=jax.ShapeDtypeStruct((T, N), BF16),
        compiler_params=_params(("parallel", "parallel")),
    )(x, w[0])


def _sgu_head_fwd(u, v, lng, lnb):
    gu, dgu = _gelu_and_grad(u)
    gv, dgv = _gelu_and_grad(v)
    mu = jnp.mean(gv, axis=-1, keepdims=True)
    xc = gv - mu
    rstd = lax.rsqrt(jnp.mean(xc * xc, axis=-1, keepdims=True) + EPS)
    xhat = xc * rstd
    vn = xhat * lng + lnb
    return gu, dgu, dgv, rstd, xhat, vn


def _tril_mask():
    r = lax.broadcasted_iota(jnp.int32, (ATT_BLOCK, ATT_BLOCK), 0)
    c = lax.broadcasted_iota(jnp.int32, (ATT_BLOCK, ATT_BLOCK), 1)
    return c <= r


def _triu_mask():
    r = lax.broadcasted_iota(jnp.int32, (ATT_BLOCK, ATT_BLOCK), 0)
    c = lax.broadcasted_iota(jnp.int32, (ATT_BLOCK, ATT_BLOCK), 1)
    return r <= c


def sgu_fwd(proj, lng, lnb, w, bcol, name):
    T = proj.shape[0]
    tm = _tile(T, 512, 128)
    nch = tm // ATT_BLOCK

    def body(u_ref, v_ref, lng_ref, lnb_ref, w_ref, b_ref, o_ref):
        tril = _tril_mask()
        for hd in range(SGU_HEADS):
            sl = slice(hd * HEAD_LANES, (hd + 1) * HEAD_LANES)
            u = u_ref[:, sl].astype(F32)
            v = v_ref[:, sl].astype(F32)
            gu, _, _, _, _, vn = _sgu_head_fwd(u, v, lng_ref[:, sl], lnb_ref[:, sl])
            wm = jnp.where(tril, w_ref[hd], 0.0).astype(BF16)
            vnb = vn.astype(BF16)
            bc = b_ref[:, hd:hd + 1]
            for ch in range(nch):
                rs = slice(ch * ATT_BLOCK, (ch + 1) * ATT_BLOCK)
                z = _nn(wm, vnb[rs, :]) + bc
                o_ref[rs, sl] = (gu[rs, :] * z).astype(BF16)

    return pl.pallas_call(
        body, name=name, grid=(T // tm,),
        in_specs=[pl.BlockSpec((tm, MIX_HALF), lambda i: (i, 0)),
                  pl.BlockSpec((tm, MIX_HALF), lambda i: (i, 1)),
                  pl.BlockSpec((1, MIX_HALF), lambda i: (0, 0)),
                  pl.BlockSpec((1, MIX_HALF), lambda i: (0, 0)),
                  pl.BlockSpec(w.shape, lambda i: (0, 0, 0)),
                  pl.BlockSpec(bcol.shape, lambda i: (0, 0))],
        out_specs=pl.BlockSpec((tm, MIX_HALF), lambda i: (i, 0)),
        out_shape=jax.ShapeDtypeStruct((T, MIX_HALF), BF16),
        compiler_params=_params(("parallel",)),
    )(proj, proj, lng, lnb, w, bcol)


def sgu_bwd(proj, dmixed, lng, lnb, w, wt, bcol, name):
    T = proj.shape[0]
    tm = _tile(T, 512, 128)
    nch = tm // ATT_BLOCK
    nsteps = T // tm

    def body(u_ref, v_ref, g_ref, lng_ref, lnb_ref, w_ref, wt_ref, b_ref, duv_ref, dw_ref, dvec_ref, bacc_ref):
        step = pl.program_id(0)

        @pl.when(step == 0)
        def _():
            dw_ref[...] = jnp.zeros_like(dw_ref)
            dvec_ref[...] = jnp.zeros_like(dvec_ref)
            bacc_ref[...] = jnp.zeros_like(bacc_ref)

        tril = _tril_mask()
        triu = _triu_mask()
        for hd in range(SGU_HEADS):
            sl = slice(hd * HEAD_LANES, (hd + 1) * HEAD_LANES)
            u = u_ref[:, sl].astype(F32)
            v = v_ref[:, sl].astype(F32)
            lng_h = lng_ref[:, sl]
            gu, dgu, dgv, rstd, xhat, vn = _sgu_head_fwd(u, v, lng_h, lnb_ref[:, sl])
            wm = jnp.where(tril, w_ref[hd], 0.0).astype(BF16)
            wmt = jnp.where(triu, wt_ref[hd], 0.0).astype(BF16)
            vnb = vn.astype(BF16)
            bc = b_ref[:, hd:hd + 1]
            g = g_ref[:, sl].astype(F32)
            dw_acc = jnp.zeros((ATT_BLOCK, ATT_BLOCK), F32)
            b_acc = jnp.zeros((ATT_BLOCK, HEAD_LANES), F32)
            dvn_parts = []
            for ch in range(nch):
                rs = slice(ch * ATT_BLOCK, (ch + 1) * ATT_BLOCK)
                z = _nn(wm, vnb[rs, :]) + bc
                duv_ref[rs, sl] = (g[rs, :] * z * dgu[rs, :]).astype(BF16)
                dz = g[rs, :] * gu[rs, :]
                dzb = dz.astype(BF16)
                dvn_parts.append(_nn(wmt, dzb))
                dw_acc = dw_acc + _nt(dzb, vnb[rs, :])
                b_acc = b_acc + dz
            dvn = jnp.concatenate(dvn_parts, axis=0)
            dw_ref[hd] += jnp.where(tril, dw_acc, 0.0)
            bacc_ref[hd] += b_acc
            dvec_ref[0:1, sl] += jnp.sum(dvn * xhat, axis=0, keepdims=True)
            dvec_ref[1:2, sl] += jnp.sum(dvn, axis=0, keepdims=True)
            dxh = dvn * lng_h
            dgv_in = rstd * (dxh - jnp.mean(dxh, axis=-1, keepdims=True)
                             - xhat * jnp.mean(dxh * xhat, axis=-1, keepdims=True))
            duv_ref[:, MIX_HALF + hd * HEAD_LANES:MIX_HALF + (hd + 1) * HEAD_LANES] = (dgv_in * dgv).astype(BF16)

        @pl.when(step == nsteps - 1)
        def _():
            for hd in range(SGU_HEADS):
                sl = slice(hd * HEAD_LANES, (hd + 1) * HEAD_LANES)
                dvec_ref[2:3, sl] = jnp.sum(bacc_ref[hd].T, axis=0, keepdims=True)

    return pl.pallas_call(
        body, name=name, grid=(nsteps,),
        in_specs=[pl.BlockSpec((tm, MIX_HALF), lambda i: (i, 0)),
                  pl.BlockSpec((tm, MIX_HALF), lambda i: (i, 1)),
                  pl.BlockSpec((tm, MIX_HALF), lambda i: (i, 0)),
                  pl.BlockSpec((1, MIX_HALF), lambda i: (0, 0)),
                  pl.BlockSpec((1, MIX_HALF), lambda i: (0, 0)),
                  pl.BlockSpec(w.shape, lambda i: (0, 0, 0)),
                  pl.BlockSpec(w.shape, lambda i: (0, 0, 0)),
                  pl.BlockSpec(bcol.shape, lambda i: (0, 0))],
        out_specs=[pl.BlockSpec((tm, 2 * MIX_HALF), lambda i: (i, 0)),
                   pl.BlockSpec(w.shape, lambda i: (0, 0, 0)),
                   pl.BlockSpec((8, MIX_HALF), lambda i: (0, 0))],
        out_shape=[jax.ShapeDtypeStruct((T, 2 * MIX_HALF), BF16),
                   jax.ShapeDtypeStruct(w.shape, F32),
                   jax.ShapeDtypeStruct((8, MIX_HALF), F32)],
        scratch_shapes=[pltpu.VMEM((SGU_HEADS, ATT_BLOCK, HEAD_LANES), F32)],
        compiler_params=_params(("arbitrary",)),
    )(proj, proj, dmixed, lng, lnb, w, wt, bcol)


def _rot_half(t):
    lane = lax.broadcasted_iota(jnp.int32, t.shape, 1)
    first = (lane % ATT_DH) < (ATT_DH // 2)
    return jnp.where(first, -pltpu.roll(t, HEAD_LANES - ATT_DH // 2, 1), pltpu.roll(t, ATT_DH // 2, 1))


LAYOUT_ROWS = 512


def _res_spec(d, tm, W):
    return pl.BlockSpec((d, tm // d, W), lambda i: (0, i, 0))


def _res_shape(d, T, W, dtype):
    return jax.ShapeDtypeStruct((d, T // d, W), dtype)


def _slab_buf(tm, W):
    return pltpu.VMEM((W // HEAD_LANES, tm, HEAD_LANES), F32)


def _lanes(hp):
    return slice(hp * HEAD_LANES, (hp + 1) * HEAD_LANES)


def _to_res(buf, out_ref, d, dtype):
    nslab, tm, _ = buf.shape
    for hp in range(nslab):
        if d == 1:
            out_ref[0, :, _lanes(hp)] = buf[hp].astype(dtype)
        else:
            for r in range(d):
                out_ref[r, :, _lanes(hp)] = buf.at[hp][pl.ds(r, tm // d, stride=d), :].astype(dtype)


def _from_res(in_ref, buf, d):
    nslab, tm, _ = buf.shape
    for hp in range(nslab):
        if d == 1:
            buf[hp] = in_ref[0, :, _lanes(hp)]
        else:
            for r in range(d):
                buf.at[hp][pl.ds(r, tm // d, stride=d), :] = in_ref[r, :, _lanes(hp)]


def rope_fwd(proj, cos, sin, name):
    T = proj.shape[0]
    tm = LAYOUT_ROWS
    scale = 1.0 / math.sqrt(ATT_DH)
    nd = len(DILATIONS)

    def body(q_ref, k_ref, v_ref, cos_ref, sin_ref, *rest):
        outs, buf = rest[:3 * nd], rest[3 * nd]
        c = cos_ref[...]
        s = sin_ref[...]
        for which, src in enumerate((q_ref, k_ref, v_ref)):
            for hp in range(MIX_HALF // HEAD_LANES):
                t = src[:, _lanes(hp)].astype(F32)
                if which == 0:
                    t = scale * (t * c + _rot_half(t) * s)
                elif which == 1:
                    t = t * c + _rot_half(t) * s
                buf[hp] = t
            for di, d in enumerate(DILATIONS):
                _to_res(buf, outs[3 * di + which], d, BF16)

    return pl.pallas_call(
        body, name=name, grid=(T // tm,),
        in_specs=[pl.BlockSpec((tm, MIX_HALF), lambda i: (i, 2)),
                  pl.BlockSpec((tm, MIX_HALF), lambda i: (i, 3)),
                  pl.BlockSpec((tm, MIX_HALF), lambda i: (i, 4)),
                  pl.BlockSpec((tm, HEAD_LANES), lambda i: (i, 0)),
                  pl.BlockSpec((tm, HEAD_LANES), lambda i: (i, 0))],
        out_specs=[_res_spec(d, tm, MIX_HALF) for d in DILATIONS for _ in range(3)],
        out_shape=[_res_shape(d, T, MIX_HALF, BF16) for d in DILATIONS for _ in range(3)],
        scratch_shapes=[_slab_buf(tm, MIX_HALF)],
        compiler_params=_params(("parallel",)),
    )(proj, proj, proj, cos, sin)


def to_residues(x, col, name):
    T = x.shape[0]
    tm = LAYOUT_ROWS

    def body(x_ref, *rest):
        outs, buf = rest[:-1], rest[-1]
        for hp in range(MIX_HALF // HEAD_LANES):
            buf[hp] = x_ref[:, _lanes(hp)].astype(F32)
        for o_ref, d in zip(outs, DILATIONS):
            _to_res(buf, o_ref, d, BF16)

    return pl.pallas_call(
        body, name=name, grid=(T // tm,),
        in_specs=[pl.BlockSpec((tm, MIX_HALF), lambda i: (i, col))],
        out_specs=[_res_spec(d, tm, MIX_HALF) for d in DILATIONS],
        out_shape=[_res_shape(d, T, MIX_HALF, BF16) for d in DILATIONS],
        scratch_shapes=[_slab_buf(tm, MIX_HALF)],
        compiler_params=_params(("parallel",)),
    )(x)


def rope_bwd(dqs, dks, dvs, cos, sin, name):
    T = dqs[0].shape[0] * dqs[0].shape[1]
    tm = LAYOUT_ROWS
    scale = 1.0 / math.sqrt(ATT_DH)
    npat = len(dqs)

    def body(*refs):
        groups = refs[:npat], refs[npat:2 * npat], refs[2 * npat:3 * npat]
        cos_ref, sin_ref, o_ref, buf, acc = refs[3 * npat:]
        c = cos_ref[...]
        s = sin_ref[...]
        for which, g_refs in enumerate(groups):
            _from_res(g_refs[0], acc, DILATIONS[0])
            for g_ref, d in zip(g_refs[1:], DILATIONS[1:]):
                _from_res(g_ref, buf, d)
                acc[...] += buf[...]
            for hp in range(MIX_HALF // HEAD_LANES):
                g = acc[hp]
                if which == 0:
                    g = scale * g
                if which < 2:
                    g = g * c - _rot_half(g * s)
                o_ref[:, which * MIX_HALF + hp * HEAD_LANES:which * MIX_HALF + (hp + 1) * HEAD_LANES] = g.astype(BF16)

    return pl.pallas_call(
        body, name=name, grid=(T // tm,),
        in_specs=([_res_spec(d, tm, MIX_HALF) for _ in range(3) for d in DILATIONS]
                  + [pl.BlockSpec((tm, HEAD_LANES), lambda i: (i, 0))] * 2),
        out_specs=pl.BlockSpec((tm, 3 * MIX_HALF), lambda i: (i, 0)),
        out_shape=jax.ShapeDtypeStruct((T, 3 * MIX_HALF), BF16),
        scratch_shapes=[_slab_buf(tm, MIX_HALF), _slab_buf(tm, MIX_HALF)],
        compiler_params=_params(("parallel",)),
    )(*dqs, *dks, *dvs, cos, sin)


def _band_masks(n):
    r = lax.broadcasted_iota(jnp.int32, (2 * ATT_BLOCK, ATT_BLOCK), 0)
    c = lax.broadcasted_iota(jnp.int32, (2 * ATT_BLOCK, ATT_BLOCK), 1)
    qi = r % ATT_BLOCK
    head = (c < ATT_DH) == (r < ATT_BLOCK)
    return (c >= qi) & (n > 0), c <= qi, head, c[:ATT_BLOCK] < ATT_DH


def _stack_heads(x, head):
    x2 = jnp.concatenate([x, x], axis=0)
    return jnp.where(head, x2, jnp.zeros_like(x2))


def attn_fwd(q, k, v, name):
    d, L, W = q.shape
    nb = L // ATT_BLOCK

    def body(q_ref, kp_ref, kc_ref, vp_ref, vc_ref, o_ref, lse_ref):
        mask_p, mask_c, head, head0 = _band_masks(pl.program_id(1))
        for hp in range(W // HEAD_LANES):
            sl = slice(hp * HEAD_LANES, (hp + 1) * HEAD_LANES)
            kp, kc, vp, vc = kp_ref[0, :, sl], kc_ref[0, :, sl], vp_ref[0, :, sl], vc_ref[0, :, sl]
            qs = _stack_heads(q_ref[0, :, sl], head)
            sp = jnp.where(mask_p, _nt(qs, kp), NEG)
            sc = jnp.where(mask_c, _nt(qs, kc), NEG)
            m = jnp.maximum(jnp.max(sp, axis=1, keepdims=True), jnp.max(sc, axis=1, keepdims=True))
            pp = jnp.exp(sp - m)
            pc = jnp.exp(sc - m)
            den = jnp.sum(pp, axis=1, keepdims=True) + jnp.sum(pc, axis=1, keepdims=True)
            o = (_nn(pp.astype(BF16), vp) + _nn(pc.astype(BF16), vc)) / den
            lse = m + jnp.log(den)
            o_ref[0, :, sl] = jnp.where(head0, o[:ATT_BLOCK], o[ATT_BLOCK:])
            lse_ref[0, :, sl] = jnp.where(head0, lse[:ATT_BLOCK], lse[ATT_BLOCK:])

    cur = pl.BlockSpec((1, ATT_BLOCK, W), lambda r, n: (r, n, 0))
    prev = pl.BlockSpec((1, ATT_BLOCK, W), lambda r, n: (r, jnp.maximum(n - 1, 0), 0))
    out = jax.ShapeDtypeStruct((d, L, W), F32)
    return pl.pallas_call(
        body, name=name, grid=(d, nb),
        in_specs=[cur, prev, cur, prev, cur],
        out_specs=[cur, cur], out_shape=[out, out],
        compiler_params=_params(("parallel", "parallel")),
    )(q, k, k, v, v)


def attn_combine(os_, lses, name):
    T = os_[0].shape[0] * os_[0].shape[1]
    W = os_[0].shape[2]
    tm = LAYOUT_ROWS
    npat = len(os_)

    def body(*refs):
        o_refs, l_refs = refs[:npat], refs[npat:2 * npat]
        out_ref = refs[2 * npat]
        ores, lres = refs[2 * npat + 1:3 * npat + 1], refs[3 * npat + 1:4 * npat + 1]
        bufs = refs[4 * npat + 1:]
        lbufs, obufs, out_buf, lse_buf = bufs[:npat], bufs[npat:2 * npat], bufs[2 * npat], bufs[2 * npat + 1]
        for p, d in enumerate(DILATIONS):
            _from_res(l_refs[p], lbufs[p], d)
            _from_res(o_refs[p], obufs[p], d)
        for hp in range(W // HEAD_LANES):
            ls = [b[hp] for b in lbufs]
            m = ls[0]
            for l in ls[1:]:
                m = jnp.maximum(m, l)
            es = [jnp.exp(l - m) for l in ls]
            z = es[0]
            for e in es[1:]:
                z = z + e
            acc = es[0] * obufs[0][hp]
            for p in range(1, npat):
                acc = acc + es[p] * obufs[p][hp]
            out = acc / z
            out_ref[:, _lanes(hp)] = out.astype(BF16)
            out_buf[hp] = out
            lse_buf[hp] = m + jnp.log(z)
        for p, d in enumerate(DILATIONS):
            _to_res(out_buf, ores[p], d, BF16)
            _to_res(lse_buf, lres[p], d, F32)

    return pl.pallas_call(
        body, name=name, grid=(T // tm,),
        in_specs=[_res_spec(d, tm, W) for _ in range(2) for d in DILATIONS],
        out_specs=([pl.BlockSpec((tm, W), lambda i: (i, 0))] + [_res_spec(d, tm, W) for _ in range(2) for d in DILATIONS]),
        out_shape=([jax.ShapeDtypeStruct((T, W), BF16)] + [_res_shape(d, T, W, BF16) for d in DILATIONS]
                   + [_res_shape(d, T, W, F32) for d in DILATIONS]),
        scratch_shapes=[_slab_buf(tm, W)] * (2 * npat + 2),
        compiler_params=_params(("parallel",)),
    )(*os_, *lses)


def attn_bwd(q, k, v, do, o, lse, name):
    d, L, W = q.shape
    nb = L // ATT_BLOCK

    def body(q_ref, kp_ref, kc_ref, vp_ref, vc_ref, do_ref, o_ref, lse_ref, dq_ref, dk_ref, dv_ref, kkeep, vkeep):
        n = pl.program_id(1)

        @pl.when(n == 0)
        def _():
            kkeep[...] = jnp.zeros_like(kkeep)
            vkeep[...] = jnp.zeros_like(vkeep)

        @pl.when(n < nb)
        def _():
            mask_p, mask_c, head, head0 = _band_masks(n)
            for hp in range(W // HEAD_LANES):
                sl = slice(hp * HEAD_LANES, (hp + 1) * HEAD_LANES)
                kp, kc, vp, vc = kp_ref[0, :, sl], kc_ref[0, :, sl], vp_ref[0, :, sl], vc_ref[0, :, sl]
                dout = do_ref[0, :, sl]
                qs = _stack_heads(q_ref[0, :, sl], head)
                dos = _stack_heads(dout, head)
                lse_v = lse_ref[0, :, sl]
                lse_c = jnp.max(jnp.where(head, jnp.concatenate([lse_v, lse_v], axis=0), NEG), axis=1, keepdims=True)
                delta = jnp.sum(_stack_heads(dout.astype(F32) * o_ref[0, :, sl].astype(F32), head), axis=1, keepdims=True)
                pp = jnp.exp(jnp.where(mask_p, _nt(qs, kp), NEG) - lse_c)
                pc = jnp.exp(jnp.where(mask_c, _nt(qs, kc), NEG) - lse_c)
                dsp = (pp * (_nt(dos, vp) - delta)).astype(BF16)
                dsc = (pc * (_nt(dos, vc) - delta)).astype(BF16)
                dq2 = _nn(dsp, kp) + _nn(dsc, kc)
                dq_ref[0, :, sl] = jnp.where(head0, dq2[:ATT_BLOCK], dq2[ATT_BLOCK:])
                dk_ref[0, :, sl] = kkeep[:, sl] + _tn(dsp, qs)
                dv_ref[0, :, sl] = vkeep[:, sl] + _tn(pp.astype(BF16), dos)
                kkeep[:, sl] = _tn(dsc, qs)
                vkeep[:, sl] = _tn(pc.astype(BF16), dos)

        @pl.when(n == nb)
        def _():
            dk_ref[0] = kkeep[...]
            dv_ref[0] = vkeep[...]

    cur = pl.BlockSpec((1, ATT_BLOCK, W), lambda r, n: (r, jnp.minimum(n, nb - 1), 0))
    prev = pl.BlockSpec((1, ATT_BLOCK, W), lambda r, n: (r, jnp.clip(n - 1, 0, nb - 1), 0))
    out = jax.ShapeDtypeStruct((d, L, W), F32)
    return pl.pallas_call(
        body, name=name, grid=(d, nb + 1),
        in_specs=[cur, prev, cur, prev, cur, cur, cur, cur],
        out_specs=[cur, prev, prev], out_shape=[out, out, out],
        scratch_shapes=[pltpu.VMEM((ATT_BLOCK, W), F32), pltpu.VMEM((ATT_BLOCK, W), F32)],
        compiler_params=_params(("parallel", "arbitrary")),
    )(q, k, k, v, v, do, o, lse)


def final_loss_bwd(h, gf, tgt, gate, name):
    T, D = h.shape
    tm = _tile(T, 512, 16)
    g_in, g_out, g_shape, g_ops = _gate_specs(gate, tm, D)

    def body(h_ref, g_ref, t_ref, o_ref, modg_ref, dh_ref, red_ref, do_ref, redg_ref):
        x = h_ref[...]
        r = lax.rsqrt(jnp.mean(x * x, axis=-1, keepdims=True) + EPS)
        n = x * r
        g = g_ref[...]
        err = n * g - t_ref[...]
        dy = err * (1.0 / D)

        @pl.when(pl.program_id(0) == 0)
        def _():
            red_ref[...] = jnp.zeros_like(red_ref)

        red_ref[0:1, :] += jnp.sum(dy * n, axis=0, keepdims=True)
        red_ref[1:2, :] += jnp.zeros((1, D), F32) + (0.5 / D) * jnp.sum(err * err, keepdims=True)
        dn = dy * g
        dh = r * (dn - n * jnp.mean(dn * n, axis=-1, keepdims=True))
        dh_ref[...] = dh
        _gate_emit(dh, gate, o_ref, modg_ref, do_ref, redg_ref)

    return pl.pallas_call(
        body, name=name, grid=(T // tm,),
        in_specs=[pl.BlockSpec((tm, D), lambda i: (i, 0)),
                  pl.BlockSpec((1, D), lambda i: (0, 0)),
                  pl.BlockSpec((tm, D), lambda i: (i, 0))] + g_in,
        out_specs=[pl.BlockSpec((tm, D), lambda i: (i, 0)), pl.BlockSpec((8, D), lambda i: (0, 0))] + g_out,
        out_shape=[jax.ShapeDtypeStruct((T, D), F32), jax.ShapeDtypeStruct((8, D), F32)] + g_shape,
        compiler_params=_params(("arbitrary",)),
    )(h, gf, tgt, *g_ops)


def ada_fwd(c_all, ada_w, ada_b, name):
    nl, D, N = ada_w.shape

    def body(c_ref, w_ref, b_ref, o_ref):
        c = c_ref[...]
        o_ref[0] = _nn(c * _sigmoid(c), w_ref[0]) + b_ref[0]

    return pl.pallas_call(
        body, name=name, grid=(nl,),
        in_specs=[pl.BlockSpec((N_DEV, D), lambda l: (0, 0)),
                  pl.BlockSpec((1, D, N), lambda l: (l, 0, 0)),
                  pl.BlockSpec((1, 1, N), lambda l: (l, 0, 0))],
        out_specs=pl.BlockSpec((1, N_DEV, N), lambda l: (l, 0, 0)),
        out_shape=jax.ShapeDtypeStruct((nl, N_DEV, N), F32),
        compiler_params=_params(("parallel",)),
    )(c_all, ada_w, ada_b)


def ada_bwd(c_allT, dmod, name):
    nl, _, N = dmod.shape
    D = c_allT.shape[0]

    def body(c_ref, g_ref, o_ref):
        c = c_ref[...]
        ca = c * _sigmoid(c)
        acc = ca[:, 0:1] * g_ref[0, 0:1, :]
        for b in range(1, N_DEV):
            acc = acc + ca[:, b:b + 1] * g_ref[0, b:b + 1, :]
        o_ref[0] = acc

    return pl.pallas_call(
        body, name=name, grid=(nl,),
        in_specs=[pl.BlockSpec((D, N_DEV), lambda l: (0, 0)),
                  pl.BlockSpec((1, N_DEV, N), lambda l: (l, 0, 0))],
        out_specs=pl.BlockSpec((1, D, N), lambda l: (l, 0, 0)),
        out_shape=jax.ShapeDtypeStruct((nl, D, N), F32),
        compiler_params=_params(("parallel",)),
    )(c_allT, dmod)


def adamw(w, g, m, v, name):
    R, C = w.shape
    tr = _tile(R, max(8, (1 << 19) // C // 8 * 8), 8)
    c1 = 1.0 - ADAM_B1 ** ADAM_STEP
    c2 = 1.0 - ADAM_B2 ** ADAM_STEP

    def body(w_ref, g_ref, m_ref, v_ref, d_ref, mo_ref, vo_ref):
        gv = g_ref[...]
        mn = ADAM_B1 * m_ref[...] + (1.0 - ADAM_B1) * gv
        vn = ADAM_B2 * v_ref[...] + (1.0 - ADAM_B2) * (gv * gv)
        mo_ref[...] = mn
        vo_ref[...] = vn
        d_ref[...] = -ADAM_LR * ((mn / c1) / (jnp.sqrt(vn / c2) + ADAM_EPS) + ADAM_WD * w_ref[...])

    blk = pl.BlockSpec((tr, C), lambda i: (i, 0))
    out = jax.ShapeDtypeStruct((R, C), F32)
    return pl.pallas_call(
        body, name=name, grid=(R // tr,),
        in_specs=[blk] * 4, out_specs=[blk] * 3, out_shape=[out] * 3,
        compiler_params=_params(("parallel",)),
    )(w, g, m, v)


def adamw_layer(w, g, m, v, l, prev, name):
    NLw, R, C = w.shape
    tr = _tile(R, max(8, (1 << 19) // C // 8 * 8), 8)
    nrb = R // tr
    c1 = 1.0 - ADAM_B1 ** ADAM_STEP
    c2 = 1.0 - ADAM_B2 ** ADAM_STEP
    w, m, v = (t.reshape(NLw * R, C) for t in (w, m, v))

    def body(w_ref, g_ref, m_ref, v_ref, *rest):
        go_ref, d_ref, mo_ref, vo_ref = rest[-4:]
        gv = g_ref[...]
        mn = ADAM_B1 * m_ref[...] + (1.0 - ADAM_B1) * gv
        vn = ADAM_B2 * v_ref[...] + (1.0 - ADAM_B2) * (gv * gv)
        go_ref[...] = gv
        mo_ref[...] = mn
        vo_ref[...] = vn
        d_ref[...] = -ADAM_LR * ((mn / c1) / (jnp.sqrt(vn / c2) + ADAM_EPS) + ADAM_WD * w_ref[...])

    lay = pl.BlockSpec((tr, C), lambda i: (l * nrb + i, 0))
    out = jax.ShapeDtypeStruct((NLw * R, C), F32)
    n_prev = 0 if prev is None else 4
    return pl.pallas_call(
        body, name=name, grid=(nrb,),
        in_specs=[lay, pl.BlockSpec((tr, C), lambda i: (i, 0)), lay, lay] + [pl.BlockSpec(memory_space=pl.ANY)] * n_prev,
        out_specs=[lay] * 4, out_shape=[out] * 4,
        input_output_aliases={4 + i: i for i in range(n_prev)},
        compiler_params=_params(("parallel",)),
    )(w, g, m, v, *(prev or ()))


def sum_slots(x, name):
    S, R, C = x.shape
    tr = _tile(R, 128, 8)

    def body(x_ref, o_ref):
        acc = x_ref[0]
        for s in range(1, S):
            acc = acc + x_ref[s]
        o_ref[...] = acc

    return pl.pallas_call(
        body, name=name, grid=(R // tr,),
        in_specs=[pl.BlockSpec((S, tr, C), lambda i: (0, i, 0))],
        out_specs=pl.BlockSpec((tr, C), lambda i: (i, 0)),
        out_shape=jax.ShapeDtypeStruct((R, C), F32),
        compiler_params=_params(("parallel",)),
    )(x)


def sum_halves(g, lands, c_idx, name):
    n, ns, _, rh, D = g.shape

    def body(c_ref, g_ref, l_ref, o_ref):
        o_ref[0, 0] = (g_ref[0, 0, 0].astype(F32) + l_ref[0, 0].astype(F32)).astype(BF16)

    return pl.pallas_call(
        body, name=name,
        grid_spec=pltpu.PrefetchScalarGridSpec(
            num_scalar_prefetch=1, grid=(n, ns),
            in_specs=[pl.BlockSpec((1, 1, 1, rh, D), lambda i, j, c: (i, j, c[0], 0, 0)),
                      pl.BlockSpec((1, 1, rh, D), lambda i, j, c: (i, j, 0, 0))],
            out_specs=pl.BlockSpec((1, 1, rh, D), lambda i, j, c: (i, j, 0, 0))),
        out_shape=jax.ShapeDtypeStruct((n, ns, rh, D), BF16),
        compiler_params=_params(("parallel", "parallel")),
    )(c_idx, g, lands)


def sum_chips(p, lands, place, name):
    n, ns, rh, D = p.shape

    def body(c_ref, p_ref, l_ref, o_ref):
        acc = p_ref[0, 0].astype(F32)
        for j in range(N_CHIP - 1):
            acc = acc + l_ref[j, 0].astype(F32)
        o_ref[0, 0] = acc

    return pl.pallas_call(
        body, name=name,
        grid_spec=pltpu.PrefetchScalarGridSpec(
            num_scalar_prefetch=1, grid=(n,),
            in_specs=[pl.BlockSpec((1, 1, rh, D), lambda i, c: (i, c[0], 0, 0)),
                      pl.BlockSpec((N_CHIP - 1, 1, rh, D), lambda i, c: (0, i, 0, 0))],
            out_specs=pl.BlockSpec((1, 1, rh, D), lambda i, c: (i, c[1], 0, 0))),
        out_shape=jax.ShapeDtypeStruct((n, 2, rh, D), F32),
        compiler_params=_params(("parallel",)),
    )(place, p, lands)


def _my_place():
    return lax.axis_index("x"), lax.axis_index("y"), lax.axis_index("c")


def _other_chips(mx, my):
    return [(1 - mx, my), (mx, 1 - my), (1 - mx, 1 - my)]


def gather_small(x, after, name):
    def body(x_ref, after_ref, out_ref, sum_ref, send_sems, recv_sems):
        mx, my, mc = _my_place()
        me = 4 * mx + 2 * my + mc
        out_ref[me] = x_ref[...]
        sends = []
        for k in range(1, N_DEV):
            kx, ky, kc = (k >> 2) & 1, (k >> 1) & 1, k & 1
            peer = (1 - mx if kx else mx, 1 - my if ky else my, 1 - mc if kc else mc)
            cp = pltpu.make_async_remote_copy(
                src_ref=x_ref, dst_ref=out_ref.at[me], send_sem=send_sems.at[k - 1], recv_sem=recv_sems.at[k - 1],
                device_id=peer, device_id_type=MESH)
            cp.start()
            sends.append((cp, 4 * peer[0] + 2 * peer[1] + peer[2], peer))
        for k, (cp, peer_slot, peer) in enumerate(sends):
            pltpu.make_async_remote_copy(
                src_ref=x_ref, dst_ref=out_ref.at[peer_slot], send_sem=send_sems.at[k], recv_sem=recv_sems.at[k],
                device_id=peer, device_id_type=MESH).wait_recv()
        for cp, _, _ in sends:
            cp.wait_send()
        acc = out_ref[0]
        for s in range(1, N_DEV):
            acc = acc + out_ref[s]
        sum_ref[...] = acc

    vmem = pl.BlockSpec(memory_space=pltpu.VMEM)
    return pl.pallas_call(
        body, name=name,
        in_specs=[vmem, pl.BlockSpec(memory_space=pl.ANY)], out_specs=[vmem, vmem],
        out_shape=[jax.ShapeDtypeStruct((N_DEV,) + x.shape, x.dtype), jax.ShapeDtypeStruct(x.shape, x.dtype)],
        scratch_shapes=[pltpu.SemaphoreType.DMA((N_DEV - 1,)), pltpu.SemaphoreType.DMA((N_DEV - 1,))],
        compiler_params=pltpu.CompilerParams(vmem_limit_bytes=VMEM_LIMIT),
    )(x, after)


_HBM =pl.BlockSpec(memory_space=pltpu.HBM)
_SEM = pl.BlockSpec(memory_space=pltpu.SEMAPHORE)
_DATAFLOW = pltpu.SideEffectType.DATAFLOW_SIDE_EFFECTING


def _gather_copies(shard, land, send, recv, base):
    mx, my, mc = _my_place()
    ci = 2 * mx + my
    peers = [((cx, cy, mc), 2 * cx + cy) for cx, cy in _other_chips(mx, my)] + [((mx, my, 1 - mc), ci)]
    out = []
    for q, (dev, src_slot) in enumerate(peers):
        out.append((
            pltpu.make_async_remote_copy(src_ref=shard, dst_ref=land.at[:, ci], send_sem=send.at[base + q],
                                         recv_sem=recv.at[base + q], device_id=dev, device_id_type=MESH),
            pltpu.make_async_remote_copy(src_ref=shard, dst_ref=land.at[:, src_slot], send_sem=send.at[base + q],
                                         recv_sem=recv.at[base + q], device_id=dev, device_id_type=MESH)))
    return out


def gather_start(groups, after, name):
    items = [s for g in groups for s in g]
    ni, ng = len(items), len(groups)

    def body(*refs):
        shards, lands = refs[:ni], refs[ni:2 * ni]
        sems = refs[2 * ni + 1:2 * ni + 1 + 2 * ng]
        token = refs[-1]
        i = 0
        for g, grp in enumerate(groups):
            for p in range(len(grp)):
                for start_cp, _ in _gather_copies(shards[i], lands[i], sems[2 * g], sems[2 * g + 1], 4 * p):
                    start_cp.start()
                i += 1
        token[...] = jnp.zeros_like(token)

    sem_shapes = []
    for grp in groups:
        sem_shapes += [pltpu.SemaphoreType.DMA((4 * len(grp),))] * 2
    land_shapes = [(s.shape[0], N_CHIP) + s.shape[1:] for s in items]
    outs = pl.pallas_call(
        body, name=name,
        in_specs=[_HBM] * (2 * ni) + [pl.BlockSpec(memory_space=pl.ANY)],
        out_specs=[_SEM] * (2 * ng) + [_HBM] * (2 * ni) + [pl.BlockSpec(memory_space=pltpu.VMEM)],
        out_shape=(sem_shapes + [pltpu.HBM(s.shape, s.dtype) for s in items]
                   + [pltpu.HBM(ls, s.dtype) for ls, s in zip(land_shapes, items)]
                   + [jax.ShapeDtypeStruct((8, 128), F32)]),
        input_output_aliases={i: 2 * ng + i for i in range(2 * ni)},
        compiler_params=pltpu.CompilerParams(has_side_effects=_DATAFLOW),
    )(*[pltpu.with_memory_space_constraint(s, pltpu.HBM) for s in items],
      *[pltpu.with_memory_space_constraint(lax.empty(ls, s.dtype), pltpu.HBM) for ls, s in zip(land_shapes, items)],
      after)
    sems, thru, token = outs[:2 * ng], outs[2 * ng:2 * ng + 2 * ni], outs[-1]
    handles, i = [], 0
    for g, grp in enumerate(groups):
        n = len(grp)
        handles.append((sems[2 * g], sems[2 * g + 1], thru[i:i + n], thru[ni + i:ni + i + n]))
        i += n
    return handles, token


def gather_wait(handle, after, name):
    send, recv, shards, lands = handle
    n = len(shards)

    def body(*refs):
        shard_refs, land_refs = refs[:n], refs[n:2 * n]
        send_ref, recv_ref = refs[2 * n], refs[2 * n + 1]
        for p in range(n):
            for start_cp, recv_cp in _gather_copies(shard_refs[p], land_refs[p], send_ref, recv_ref, 4 * p):
                start_cp.wait_send()
                recv_cp.wait_recv()

    outs = pl.pallas_call(
        body, name=name,
        in_specs=[_HBM] * (2 * n) + [_SEM, _SEM, pl.BlockSpec(memory_space=pl.ANY)],
        out_specs=[_HBM] * (2 * n),
        out_shape=[pltpu.HBM(s.shape, s.dtype) for s in shards] + [pltpu.HBM(l.shape, l.dtype) for l in lands],
        input_output_aliases={i: i for i in range(2 * n)},
        compiler_params=pltpu.CompilerParams(has_side_effects=_DATAFLOW),
    )(*shards, *lands, send, recv, after)
    return outs[n:]


def _sibling_copies(gs, lands, send, recv):
    mx, my, mc = _my_place()
    return [pltpu.make_async_remote_copy(
        src_ref=gs[k].at[:, :, 1 - mc], dst_ref=lands[k], send_sem=send.at[k], recv_sem=recv.at[k],
        device_id=(mx, my, 1 - mc), device_id_type=MESH) for k in range(len(gs))]


def sibling_start(gs, after, name):
    K = len(gs)

    def body(*refs):
        ins, lands = refs[:K], refs[K:2 * K]
        send, recv = refs[2 * K + 1], refs[2 * K + 2]
        for cp in _sibling_copies(ins, lands, send, recv):
            cp.start()
        refs[-1][...] = jnp.zeros_like(refs[-1])

    land_shapes = [g.shape[:2] + g.shape[3:] for g in gs]
    outs = pl.pallas_call(
        body, name=name,
        in_specs=[_HBM] * (2 * K) + [pl.BlockSpec(memory_space=pl.ANY)],
        out_specs=[_SEM, _SEM] + [_HBM] * (2 * K) + [pl.BlockSpec(memory_space=pltpu.VMEM)],
        out_shape=([pltpu.SemaphoreType.DMA((K,))] * 2 + [pltpu.HBM(g.shape, g.dtype) for g in gs]
                   + [pltpu.HBM(ls, g.dtype) for ls, g in zip(land_shapes, gs)] + [jax.ShapeDtypeStruct((8, 128), F32)]),
        input_output_aliases={i: 2 + i for i in range(2 * K)},
        compiler_params=pltpu.CompilerParams(has_side_effects=_DATAFLOW),
    )(*[pltpu.with_memory_space_constraint(g, pltpu.HBM) for g in gs],
      *[pltpu.with_memory_space_constraint(lax.empty(ls, g.dtype), pltpu.HBM) for ls, g in zip(land_shapes, gs)],
      after)
    return (outs[0], outs[1], outs[2:2 + K], outs[2 + K:2 + 2 * K]), outs[-1]


def sibling_wait(handle, after, name):
    send, recv, gs, lands = handle
    K = len(gs)

    def body(*refs):
        ins, land_refs = refs[:K], refs[K:2 * K]
        for cp in _sibling_copies(ins, land_refs, refs[2 * K], refs[2 * K + 1]):
            cp.wait_send()
            cp.wait_recv()

    outs = pl.pallas_call(
        body, name=name,
        in_specs=[_HBM] * (2 * K) + [_SEM, _SEM, pl.BlockSpec(memory_space=pl.ANY)],
        out_specs=[_HBM] * (2 * K),
        out_shape=[pltpu.HBM(g.shape, g.dtype) for g in gs] + [pltpu.HBM(l.shape, l.dtype) for l in lands],
        input_output_aliases={i: i for i in range(2 * K)},
        compiler_params=pltpu.CompilerParams(has_side_effects=_DATAFLOW),
    )(*gs, *lands, send, recv, after)
    return outs[:K], outs[K:]


def _small_copies(x, land, send, recv):
    mx, my, mc = _my_place()
    me = 4 * mx + 2 * my + mc
    out = []
    for k in range(1, N_DEV):
        peer = (1 - mx if k & 4 else mx, 1 - my if k & 2 else my, 1 - mc if k & 1 else mc)
        slot = 4 * peer[0] + 2 * peer[1] + peer[2]
        out.append(tuple(pltpu.make_async_remote_copy(
            src_ref=x, dst_ref=land.at[s], send_sem=send.at[k - 1], recv_sem=recv.at[k - 1],
            device_id=peer, device_id_type=MESH) for s in (me, slot)))
    return out


def small_start(x, after, name):
    def body(x_ref, land_ref, after_ref, send, recv, x_thru, land_thru, token):
        for mine, _ in _small_copies(x_ref, land_ref, send, recv):
            mine.start()
        token[...] = jnp.zeros_like(token)

    land_shape = (N_DEV,) + x.shape
    outs = pl.pallas_call(
        body, name=name,
        in_specs=[_HBM, _HBM, pl.BlockSpec(memory_space=pl.ANY)],
        out_specs=[_SEM, _SEM, _HBM, _HBM, pl.BlockSpec(memory_space=pltpu.VMEM)],
        out_shape=[pltpu.SemaphoreType.DMA((N_DEV - 1,))] * 2 + [pltpu.HBM(x.shape, x.dtype), pltpu.HBM(land_shape, x.dtype),
                                                                 jax.ShapeDtypeStruct((8, 128), F32)],
        input_output_aliases={0: 2, 1: 3},
        compiler_params=pltpu.CompilerParams(has_side_effects=_DATAFLOW),
    )(pltpu.with_memory_space_constraint(x, pltpu.HBM),
      pltpu.with_memory_space_constraint(lax.empty(land_shape, x.dtype), pltpu.HBM), after)
    return outs[:4], outs[4]


def small_wait(handle, after, name):
    send, recv, x, land = handle

    def body(x_ref, land_ref, send_ref, recv_ref, after_ref, x_out, land_out):
        for mine, theirs in _small_copies(x_ref, land_ref, send_ref, recv_ref):
            mine.wait_send()
            theirs.wait_recv()

    return pl.pallas_call(
        body, name=name,
        in_specs=[_HBM, _HBM, _SEM, _SEM, pl.BlockSpec(memory_space=pl.ANY)],
        out_specs=[_HBM, _HBM],
        out_shape=[pltpu.HBM(x.shape, x.dtype), pltpu.HBM(land.shape, land.dtype)],
        input_output_aliases={0: 0, 1: 1},
        compiler_params=pltpu.CompilerParams(has_side_effects=_DATAFLOW),
    )(x, land, send, recv, after)


def _scatter_copies(ps, lands, send, recv):
    mx, my, mc = _my_place()
    cps = []
    for j, (cx, cy) in enumerate(_other_chips(mx, my)):
        for k in range(len(ps)):
            cps.append(pltpu.make_async_remote_copy(
                src_ref=ps[k].at[:, 2 * cx + cy], dst_ref=lands[k].at[j],
                send_sem=send.at[k * 3 + j], recv_sem=recv.at[k * 3 + j],
                device_id=(cx, cy, mc), device_id_type=MESH))
    return cps


def scatter_start(ps, after, name):
    K = len(ps)

    def body(*refs):
        ins, lands = refs[:K], refs[K:2 * K]
        send, recv = refs[2 * K + 1], refs[2 * K + 2]
        for cp in _scatter_copies(ins, lands, send, recv):
            cp.start()
        refs[-1][...] = jnp.zeros_like(refs[-1])

    land_shapes = [(N_CHIP - 1, p.shape[0]) + p.shape[2:] for p in ps]
    outs = pl.pallas_call(
        body, name=name,
        in_specs=[_HBM] * (2 * K) + [pl.BlockSpec(memory_space=pl.ANY)],
        out_specs=[_SEM, _SEM] + [_HBM] * (2 * K) + [pl.BlockSpec(memory_space=pltpu.VMEM)],
        out_shape=([pltpu.SemaphoreType.DMA((3 * K,))] * 2 + [pltpu.HBM(p.shape, p.dtype) for p in ps]
                   + [pltpu.HBM(ls, p.dtype) for ls, p in zip(land_shapes, ps)] + [jax.ShapeDtypeStruct((8, 128), F32)]),
        input_output_aliases={i: 2 + i for i in range(2 * K)},
        compiler_params=pltpu.CompilerParams(has_side_effects=_DATAFLOW),
    )(*[pltpu.with_memory_space_constraint(p, pltpu.HBM) for p in ps],
      *[pltpu.with_memory_space_constraint(lax.empty(ls, p.dtype), pltpu.HBM) for ls, p in zip(land_shapes, ps)],
      after)
    return (outs[0], outs[1], outs[2:2 + K], outs[2 + K:2 + 2 * K]), outs[-1]


def scatter_wait(handle, after, name):
    send, recv, ps, lands = handle
    K = len(ps)

    def body(*refs):
        ins, land_refs = refs[:K], refs[K:2 * K]
        send_ref, recv_ref = refs[2 * K], refs[2 * K + 1]
        for cp in _scatter_copies(ins, land_refs, send_ref, recv_ref):
            cp.wait_send()
            cp.wait_recv()

    outs = pl.pallas_call(
        body, name=name,
        in_specs=[_HBM] * (2 * K) + [_SEM, _SEM, pl.BlockSpec(memory_space=pl.ANY)],
        out_specs=[_HBM] * (2 * K),
        out_shape=[pltpu.HBM(p.shape, p.dtype) for p in ps] + [pltpu.HBM(l.shape, l.dtype) for l in lands],
        input_output_aliases={i: i for i in range(2 * K)},
        compiler_params=pltpu.CompilerParams(has_side_effects=_DATAFLOW),
    )(*ps, *lands, send, recv, after)
    return outs[:K], outs[K:]


def sibling_complete(ss, name):
    K = len(ss)

    def body(*refs):
        ins, outs = refs[:K], refs[K:2 * K]
        send, recv = refs[2 * K:]
        mx, my, mc = _my_place()
        cps = []
        for k in range(K):
            cp = pltpu.make_async_remote_copy(
                src_ref=ins[k].at[:, mc], dst_ref=outs[k].at[:, mc], send_sem=send.at[k], recv_sem=recv.at[k],
                device_id=(mx, my, 1 - mc), device_id_type=MESH)
            cp.start()
            cps.append(cp)
        for k in range(K):
            pltpu.make_async_remote_copy(
                src_ref=ins[k].at[:, mc], dst_ref=outs[k].at[:, 1 - mc], send_sem=send.at[k], recv_sem=recv.at[k],
                device_id=(mx, my, 1 - mc), device_id_type=MESH).wait_recv()
        for cp in cps:
            cp.wait_send()

    hbm = pl.BlockSpec(memory_space=pl.ANY)
    return pl.pallas_call(
        body, name=name,
        in_specs=[hbm] * K, out_specs=[hbm] * K,
        out_shape=[jax.ShapeDtypeStruct(s.shape, s.dtype) for s in ss],
        scratch_shapes=[pltpu.SemaphoreType.DMA((K,)), pltpu.SemaphoreType.DMA((K,))],
        input_output_aliases={k: k for k in range(K)},
    )(*ss)


def _rope_tables(T):
    inv = ROPE_THETA ** (-jnp.arange(0, ATT_DH, 2, dtype=F32) / ATT_DH)
    ang = jnp.arange(T, dtype=F32)[:, None] * inv[None, :]
    ang = jnp.concatenate([ang, ang, ang, ang], axis=-1)
    return jnp.cos(ang), jnp.sin(ang)


def _ffn_fwd(h, y, mod, i0, get_up, get_down, norm_next, tag):
    wgu = get_up(h)
    a, b, s = ffn_up(y, (wgu, (0,)), (wgu, (1,)), f"ffn_up_{tag}")
    wd = get_down(s)
    outs = resid_matmul([s], (wd, (0,)), h, mod, i0 + 2, 0.5, f"ffn_down_{tag}", norm_next)
    hn, o = outs[0], outs[1]
    return hn, (outs[2] if norm_next else None), (h, y, a, b, s, o), ((wgu, (0,)), (wgu, (1,)), (wd, (0,)))


def _ffn_bwd(dh, do, res, ng, i_n, mod, i0, wgT, wuT, wd, on_grads, next_gate, tag):
    h, y, a, b, s, o = res
    F = _wrows(wgT)
    da, db = ffn_bwd_mid(do, wd, a, b, f"ffn_bwd_mid_{tag}")
    gbuf = lax.empty((3, F, h.shape[1]), BF16)
    gbuf = matmul_tn(da, y, gbuf, 0, 0, f"dwg_{tag}")
    gbuf = matmul_tn(db, y, gbuf, 1, 0, f"dwu_{tag}")
    gbuf = matmul_tn(s, do, gbuf, 2, 0, f"dwd_{tag}")
    token, then = on_grads([gbuf])
    outs = dy_normbwd([(da, 0, wgT, 0, F), (db, 0, wuT, 0, F)], h, dh, ng, i_n, mod + token, i0 + 1,
                      f"ffn_bwd_dy_{tag}", next_gate)
    return outs, then


def _mixer_fwd(h, y, mod, w_inT, w_out, sgu, cos, sin, norm_next, tag):
    lng, lnb, sw, swt, bcol = sgu
    proj = matmul_nt(y, w_inT, f"proj_{tag}")
    out_a = sgu_fwd(proj, lng, lnb, sw, bcol, f"sgu_fwd_{tag}")
    qkv = rope_fwd(proj, cos, sin, f"rope_fwd_{tag}")
    npat = len(DILATIONS)
    qkv_res = [tuple(qkv[3 * p:3 * p + 3]) for p in range(npat)]
    os_, lses = [], []
    for d, (qd, kd, vd) in zip(DILATIONS, qkv_res):
        o_d, lse_d = attn_fwd(qd, kd, vd, f"attn_fwd_d{d}_{tag}")
        os_.append(o_d)
        lses.append(lse_d)
    comb = attn_combine(os_, lses, f"attn_combine_{tag}")
    out_b, o_res, lse_res = comb[0], comb[1:1 + npat], comb[1 + npat:]
    outs = resid_matmul([out_a, out_b], w_out, h, mod, 5, 1.0, f"mix_out_{tag}", norm_next)
    hn, om = outs[0], outs[1]
    return hn, (outs[2] if norm_next else None), (h, y, proj, out_a, out_b, o_res, lse_res, qkv_res, om)


def _mixer_bwd(dh, dom, res, ng, mod, w_inT, w_out, sgu, cos, sin, on_grads, next_gate, tag):
    lng, lnb, sw, swt, bcol = sgu
    h, y, proj, out_a, out_b, o_res, lse_res, qkv_res, om = res
    D = h.shape[1]
    dmixed = matmul_nt(dom, w_out, f"dmixed_{tag}")
    woutbuf = lax.empty((1, 2 * MIX_HALF, D), BF16)
    woutbuf = matmul_tn(out_a, dom, woutbuf, 0, 0, f"dwout_a_{tag}", tmo_cap=MIX_HALF)
    woutbuf = matmul_tn(out_b, dom, woutbuf, 0, MIX_HALF, f"dwout_b_{tag}", tmo_cap=MIX_HALF)
    d_uv, d_sw, d_svec = sgu_bwd(proj, dmixed, lng, lnb, sw, swt, bcol, f"sgu_bwd_{tag}")
    do_res = to_residues(dmixed, 1, f"dout_res_{tag}")
    dqs, dks, dvs = [], [], []
    for p, (d, (qd, kd, vd)) in enumerate(zip(DILATIONS, qkv_res)):
        dq, dk, dv = attn_bwd(qd, kd, vd, do_res[p], o_res[p], lse_res[p], f"attn_bwd_d{d}_{tag}")
        dqs.append(dq)
        dks.append(dk)
        dvs.append(dv)
    d_qkv = rope_bwd(dqs, dks, dvs, cos, sin, f"rope_bwd_{tag}")
    winbuf = lax.empty((1, 5 * MIX_HALF, D), BF16)
    winbuf = matmul_tn(d_uv, y, winbuf, 0, 0, f"dwin_uv_{tag}", tmo_cap=MIX_HALF)
    winbuf = matmul_tn(d_qkv, y, winbuf, 0, 2 * MIX_HALF, f"dwin_qkv_{tag}", tmo_cap=MIX_HALF)
    token, then = on_grads([winbuf, woutbuf])
    pairs = [(d_uv, 0, w_inT, 0, 2 * MIX_HALF), (d_qkv, 0, w_inT, 1, 2 * MIX_HALF), (d_qkv, 2, w_inT, 4, MIX_HALF)]
    outs = dy_normbwd(pairs, h, dh, ng, 1, mod + token, 4, f"mix_bwd_dy_{tag}", next_gate)
    return outs, d_sw, d_svec, then


def _local_step(x, tgt, mods, ngs, get_w, sgus, gf, on_block_grads, on_layer_small):
    T, D = x.shape
    cos, sin = _rope_tables(T)
    h = x
    saved, weights = [], []
    for l in range(2):
        def getter(blk, l=l):
            return lambda after: get_w(l, blk, after)

        if l == 0:
            y = normmod_fwd(h, ngs[0], 0, mods[0], 0, 1, "normmod_l0f1")
        h, y, r1, wf1 = _ffn_fwd(h, y, mods[l], 0, getter("f1u"), getter("f1d"), (ngs[l], 1, mods[l], 3, 4), f"l{l}f1")
        w_inT, w_out = get_w(l, "mx", h)
        h, y, r2 = _mixer_fwd(h, y, mods[l], (w_inT, (0,)), (w_out, (0,)), sgus[l], cos, sin,
                              (ngs[l], 2, mods[l], 6, 7), f"l{l}mx")
        h, y, r3, wf2 = _ffn_fwd(h, y, mods[l], 6, getter("f2u"), getter("f2d"),
                                 (ngs[l + 1], 0, mods[l + 1], 0, 1) if l + 1 < 2 else None, f"l{l}f2")
        saved.append((r1, r2, r3))
        weights.append((wf1, w_inT, w_out, wf2))
    def gate_of(l, blk):
        r1, r2, r3 = saved[l]
        o, i_g, coef = {"f2": (r3[5], 8, 0.5), "mx": (r2[-1], 5, 1.0), "f1": (r1[5], 2, 0.5)}[blk]
        return o, mods[l], i_g, coef

    seq = [(l, blk) for l in (1, 0) for blk in ("f2", "mx", "f1")]
    dh, red_final, do, red_g = final_loss_bwd(h, gf, tgt, gate_of(*seq[0]), "final_loss_bwd")
    rn, rg = {}, {}
    for idx, (l, blk) in enumerate(seq):
        r1, r2, r3 = saved[l]
        wf1, w_inT, w_out, wf2 = weights[l]
        nxt = gate_of(*seq[idx + 1]) if idx + 1 < len(seq) else None
        rg[blk] = red_g
        tag = f"l{l}{blk}"

        def on(arrays, l=l, blk=blk):
            return on_block_grads(l, blk, arrays)

        if blk == "f2":
            outs, then = _ffn_bwd(dh, do, r3, ngs[l], 2, mods[l], 6, *wf2, on, nxt, tag)
        elif blk == "mx":
            outs, d_sw, d_svec, then = _mixer_bwd(dh, do, r2, ngs[l], mods[l], (w_inT, (0,)), (w_out, (0,)), sgus[l],
                                                  cos, sin, on, nxt, tag)
        else:
            outs, then = _ffn_bwd(dh, do, r1, ngs[l], 0, mods[l], 0, *wf1, on, nxt, tag)
        dh, rn[blk] = outs[0], outs[1]
        if nxt is not None:
            do, red_g = outs[2], outs[3]
        if blk == "f1":
            mods = mods + on_layer_small(l, dict(sgu_w=d_sw, sgu_vec=d_svec, red_n=(rn["f1"], rn["mx"], rn["f2"]),
                                                 red_g=(rg["f1"], rg["mx"], rg["f2"])),
                                         red_final if l == 0 else None)
            mods = mods + then(mods)
        else:
            mods = mods + then(dh)
    return dh


def _adam_out(w, g, m, v, name):
    shp = w.shape
    two_d = (-1, shp[-1])
    d, mn, vn = adamw(w.reshape(two_d), g.reshape(two_d), m.reshape(two_d), v.reshape(two_d), name)
    return g, d.reshape(shp), mn.reshape(shp), vn.reshape(shp)


def kernel(x, c, ada_w, ada_b, norm_g, ffn1_wg, ffn1_wu, ffn1_wd, ffn2_wg, ffn2_wu, ffn2_wd, w_in, sgu_ln_g, sgu_ln_b, sgu_w, sgu_b, w_out, final_g, loss_target, m_ada_w, m_ada_b, m_norm_g, m_ffn1_wg, m_ffn1_wu, m_ffn1_wd, m_ffn2_wg, m_ffn2_wu, m_ffn2_wd, m_w_in, m_sgu_ln_g, m_sgu_ln_b, m_sgu_w, m_sgu_b, m_w_out, m_final_g, v_ada_w, v_ada_b, v_norm_g, v_ffn1_wg, v_ffn1_wu, v_ffn1_wd, v_ffn2_wg, v_ffn2_wu, v_ffn2_wd, v_w_in, v_sgu_ln_g, v_sgu_ln_b, v_sgu_w, v_sgu_b, v_w_out, v_final_g):
    T, D = x.shape[1], x.shape[2]
    NL = ada_w.shape[0]
    mx, my, mc = _my_place()
    me = 4 * mx + 2 * my + mc
    ci = 2 * mx + my
    c_idx = jnp.reshape(mc, (1,)).astype(jnp.int32)
    place = jnp.stack([ci, mc]).astype(jnp.int32)

    def halves(a):
        n, r, _ = a.shape
        return a.reshape(n, 2, r // 2, D)

    first_group = [halves(jnp.stack([ffn1_wg[0].T, ffn1_wu[0].T], axis=0).astype(BF16))]
    first_handles, first_token = gather_start([first_group], place, "gather_start_first")
    zero = first_token[0, 0]

    ngw = norm_g.shape[2]
    small_in = jnp.concatenate([jnp.pad(c, ((0, 7), (0, 0))),
                                jnp.pad(norm_g.reshape(NL * 3, ngw), ((0, 8 - NL * 3), (0, D - ngw)))], axis=0)
    small_all, _ = gather_small(small_in, first_token, "gather_c_normg")
    c_all = small_all[:, 0, :]
    ng_parts = small_all[0::2, 8:8 + NL * 3, :ngw]
    ngs = jnp.transpose(ng_parts, (1, 0, 2)).reshape(NL, 3, N_CHIP * ngw)

    nmod = ada_w.shape[2]
    ada_b_mine = lax.dynamic_slice_in_dim(ada_b, ci * nmod, nmod, axis=1).reshape(NL, 1, nmod)
    mod_part = ada_fwd(c_all, ada_w, ada_b_mine, "ada_fwd")
    mod_all, _ = gather_small(mod_part.reshape(NL * N_DEV, nmod), first_token, "gather_mod")
    mod_rows = lax.dynamic_index_in_dim(mod_all.reshape(N_DEV, NL, N_DEV, nmod), me, axis=2, keepdims=False)
    mods = jnp.transpose(mod_rows[0::2], (1, 0, 2)).reshape(NL, N_ADA, D)

    sgus = []
    for l in range(NL):
        sgus.append((sgu_ln_g[l].reshape(1, MIX_HALF), sgu_ln_b[l].reshape(1, MIX_HALF), sgu_w[l],
                     jnp.swapaxes(sgu_w[l], 1, 2), jnp.transpose(sgu_b[l])))

    def prep(a):
        return (a + zero).astype(BF16)

    groups = []
    for l in range(NL):
        groups += [[halves(jnp.stack([prep(ffn1_wg[l].T), prep(ffn1_wu[l].T)], axis=0))],
                   [halves(prep(ffn1_wd[l])[None])],
                   [halves(prep(w_in[l].T)[None]), halves(prep(w_out[l])[None])],
                   [halves(jnp.stack([prep(ffn2_wg[l].T), prep(ffn2_wu[l].T)], axis=0))],
                   [halves(prep(ffn2_wd[l])[None])]]
    handles, token = gather_start(groups[1:], mods, "gather_start")
    handles = first_handles + handles
    mods = mods + token[0, 0]
    group_no = {"f1u": 0, "f1d": 1, "mx": 2, "f2u": 3, "f2d": 4}

    def get_w(l, key, after):
        full = gather_wait(handles[len(group_no) * l + group_no[key]], after, f"gather_wait_l{l}{key}")
        full = [a.reshape(a.shape[0], N_CHIP * 2 * a.shape[3], D) for a in full]
        return full[0] if key != "mx" else tuple(full)

    def split(a):
        n, r4, _ = a.shape
        return a.reshape(n, N_CHIP, 2, r4 // N_CHIP // 2, D)

    pending, small_pending, small_tokens = {}, {}, {}

    def on_block_grads(l, blk, bufs):
        tag = f"l{l}{blk}"
        sib, tok1 = sibling_start([split(g) for g in bufs], place, f"rs_sibling_start_{tag}")

        def then(after):
            parts, lands = sibling_wait(sib, after, f"rs_sibling_wait_{tag}")
            psums = [sum_halves(g, ld, c_idx, f"rs_sum_halves_{tag}_{i}") for i, (g, ld) in enumerate(zip(parts, lands))]
            pending[(l, blk)], tok2 = scatter_start(psums, lands[0], f"rs_chips_start_{tag}")
            return tok2[0, 0]

        return tok1[0, 0], then

    def blocks_finish(blocks, after, tag):
        ssums, counts = [], []
        for l, blk in blocks:
            psums, lands2 = scatter_wait(pending.pop((l, blk)), after, f"rs_chips_wait_l{l}{blk}")
            ssums += [sum_chips(p, ld, place, f"rs_sum_chips_l{l}{blk}_{i}") for i, (p, ld) in enumerate(zip(psums, lands2))]
            counts.append(len(psums))
        fins = [f.reshape(f.shape[0], -1, D) for f in sibling_complete(ssums, f"rs_complete_{tag}")]
        out, i = [], 0
        for n in counts:
            out.append(fins[i:i + n])
            i += n
        return out

    def on_layer_small(l, grads, red_final):
        blocks = list(grads["red_n"]) + list(grads["red_g"])
        blocks.append(jnp.pad(grads["sgu_vec"], ((0, 0), (0, D - MIX_HALF))))
        blocks.append(grads["sgu_w"].reshape(-1, D))
        if red_final is not None:
            blocks.append(red_final)
        xs = jnp.concatenate(blocks, axis=0)
        small_pending[l], small_tokens[l] = small_start(xs, place, f"small_start_l{l}")
        return small_tokens[l][0, 0]

    grad_x = _local_step(x[0], loss_target[0], mods, ngs, get_w, sgus, final_g.reshape(1, D),
                         on_block_grads, on_layer_small)

    adam_state = {}

    def adam_big(nm, l, g, w, m, v):
        adam_state[nm] = adamw_layer(w, g, m, v, l, adam_state.get(nm), f"adamw_{nm}_l{l}")

    def adam_block(l, blk, fin):
        if blk == "mx":
            adam_big("w_in", l, fin[0][0].T, w_in, m_w_in, v_w_in)
            adam_big("w_out", l, fin[1][0], w_out, m_w_out, v_w_out)
        else:
            ws = ((ffn1_wg, m_ffn1_wg, v_ffn1_wg), (ffn1_wu, m_ffn1_wu, v_ffn1_wu), (ffn1_wd, m_ffn1_wd, v_ffn1_wd)) \
                if blk == "f1" else \
                ((ffn2_wg, m_ffn2_wg, v_ffn2_wg), (ffn2_wu, m_ffn2_wu, v_ffn2_wu), (ffn2_wd, m_ffn2_wd, v_ffn2_wd))
            pre = "ffn1" if blk == "f1" else "ffn2"
            for k, (nm, tr) in enumerate((("wg", True), ("wu", True), ("wd", False))):
                adam_big(f"{pre}_{nm}", l, fin[0][k].T if tr else fin[0][k], *ws[k])

    done_order = [(l, blk) for l in range(NL - 1, -1, -1) for blk in ("f2", "mx", "f1")]
    for (l, blk), fin in zip(done_order[:-1], blocks_finish(done_order[:-1], small_tokens[0], "early")):
        adam_block(l, blk, fin)
    last_big = adam_state["w_out"][1]

    small_sum, small_all = [], []
    for l in range(NL):
        xs, land = small_wait(small_pending[l], last_big, f"small_wait_l{l}")
        full = lax.dynamic_update_slice(land, xs[None], (me, 0, 0))
        small_all.append(full)
        small_sum.append(sum_slots(full, f"small_sum_l{l}"))
    offs = [8 * i for i in range(8)]
    off_final = offs[7] + SGU_HEADS * ATT_BLOCK * HEAD_LANES // D
    loss = small_sum[0][off_final + 1, 0]
    g_final_g = small_sum[0][off_final, :]
    g_norm_g, g_ada_b, g_lng, g_lnb, g_sb, g_sw, dmod_all = [], [], [], [], [], [], []
    for l in range(NL):
        rn = [small_sum[l][offs[i]:offs[i] + 8] for i in range(3)]
        rg = [small_sum[l][offs[3 + i]:offs[3 + i] + 8] for i in range(3)]
        g_norm_g.append(jnp.stack([rn[i][2] for i in range(3)], axis=0))
        g_ada_b.append(jnp.concatenate([jnp.stack([rn[i][0], rn[i][1], rg[i][0]], axis=0) for i in range(3)],
                                       axis=0).reshape(N_ADA * D))
        sv = small_sum[l][offs[6]:offs[6] + 8, :MIX_HALF]
        g_lng.append(sv[0].reshape(SGU_HEADS, HEAD_LANES))
        g_lnb.append(sv[1].reshape(SGU_HEADS, HEAD_LANES))
        g_sb.append(sv[2].reshape(SGU_HEADS, ATT_BLOCK))
        g_sw.append(small_sum[l][offs[7]:off_final].reshape(sgu_w.shape[1:]))
        rows = []
        for i in range(3):
            an = small_all[l][:, offs[i]:offs[i] + 2]
            ag = small_all[l][:, offs[3 + i]:offs[3 + i] + 1]
            rows += [an[:, 0], an[:, 1], ag[:, 0]]
        dmod_all.append(jnp.stack(rows, axis=1).reshape(N_DEV, N_ADA * D))
    dmod_all = jnp.stack(dmod_all, axis=0)
    dmod_mine = lax.dynamic_slice_in_dim(dmod_all, ci * nmod, nmod, axis=2)
    g_ada_w = ada_bwd(jnp.transpose(c_all), dmod_mine, "ada_bwd")
    g_ada_b = jnp.stack(g_ada_b, axis=0)
    g_norm_g_full = jnp.stack(g_norm_g, axis=0)
    g_norm_g_mine = lax.dynamic_slice_in_dim(g_norm_g_full, ci * ngw, ngw, axis=2)

    small_params = [
        ("ada_w", ada_w, g_ada_w, m_ada_w, v_ada_w),
        ("ada_b", ada_b, g_ada_b, m_ada_b, v_ada_b),
        ("norm_g", norm_g, g_norm_g_mine, m_norm_g, v_norm_g),
        ("sgu_ln_g", sgu_ln_g, jnp.stack(g_lng, axis=0), m_sgu_ln_g, v_sgu_ln_g),
        ("sgu_ln_b", sgu_ln_b, jnp.stack(g_lnb, axis=0), m_sgu_ln_b, v_sgu_ln_b),
        ("sgu_w", sgu_w, jnp.stack(g_sw, axis=0), m_sgu_w, v_sgu_w),
        ("sgu_b", sgu_b, jnp.stack(g_sb, axis=0), m_sgu_b, v_sgu_b),
        ("final_g", final_g.reshape(1, D), g_final_g.reshape(1, D), m_final_g.reshape(1, D), v_final_g.reshape(1, D)),
    ]
    for nm, w, g, m, v in small_params:
        res = _adam_out(w, g, m, v, f"adamw_{nm}")
        adam_state[nm] = tuple(t.reshape(D) for t in res) if nm == "final_g" else res

    l, blk = done_order[-1]
    adam_block(l, blk, blocks_finish([(l, blk)], adam_state["ada_w"][1], "last")[0])

    names = ["ada_w", "ada_b", "norm_g", "ffn1_wg", "ffn1_wu", "ffn1_wd", "ffn2_wg", "ffn2_wu", "ffn2_wd", "w_in",
             "sgu_ln_g", "sgu_ln_b", "sgu_w", "sgu_b", "w_out", "final_g"]
    shapes = [t.shape for t in (ada_w, ada_b, norm_g, ffn1_wg, ffn1_wu, ffn1_wd, ffn2_wg, ffn2_wu, ffn2_wd, w_in,
                                sgu_ln_g, sgu_ln_b, sgu_w, sgu_b, w_out, final_g)]
    return (loss, grad_x[None], *[adam_state[nm][i].reshape(s) for i in range(4) for nm, s in zip(names, shapes)])
```

```python
import math

import jax
import jax.numpy as jnp
from jax import lax
from jax.experimental import pallas as pl
from jax.experimental.pallas import tpu as pltpu

F32 = jnp.float32
BF16 = jnp.bfloat16
EPS = 1e-6
SGU_HEADS = 4
HEAD_LANES = 128
ATT_DH = 64
ATT_BLOCK = 128
MIX_HALF = SGU_HEADS * HEAD_LANES
DILATIONS = (1, 4, 16)
ROPE_THETA = 10000.0
N_ADA = 9
ADAM_LR, ADAM_B1, ADAM_B2, ADAM_EPS, ADAM_WD, ADAM_STEP = 0.001, 0.9, 0.999, 1e-08, 0.01, 10
NEG = -1e30
V7X_VMEM_BYTES = 64 * 1024 * 1024
VMEM_LIMIT = V7X_VMEM_BYTES * 7 // 8
MESH = pl.DeviceIdType.MESH
N_DEV = 8
N_CHIP = 4


def _tile(n, cap, mult):
    if n <= cap:
        return n
    t = (cap // mult) * mult
    while t >= mult:
        if n % t == 0:
            return t
        t -= mult
    raise ValueError((n, cap, mult))


def _params(dims=None):
    return pltpu.CompilerParams(dimension_semantics=dims, vmem_limit_bytes=VMEM_LIMIT)


def _wspec(w, rows, idx, resident=False):
    arr, lead = w
    kw = dict(pipeline_mode=pl.Buffered(1)) if resident else {}
    return pl.BlockSpec((None,) * len(lead) + (rows, arr.shape[-1]), lambda *g: tuple(lead) + (idx(*g), 0), **kw)


def _wrows(w):
    return w[0].shape[-2]


def _nt(a, b):
    return lax.dot_general(a, b, (((1,), (1,)), ((), ())), preferred_element_type=F32)


def _tn(a, b):
    return lax.dot_general(a, b, (((0,), (0,)), ((), ())), preferred_element_type=F32)


def _nn(a, b):
    return jnp.dot(a, b, preferred_element_type=F32)


def _sigmoid(x):
    return 0.5 * jnp.tanh(0.5 * x) + 0.5


_GELU_K = math.sqrt(2.0 / math.pi)
_GELU_C = 0.044715


def _gelu(x):
    t = jnp.tanh(_GELU_K * (x + _GELU_C * x * x * x))
    return 0.5 * x * (1.0 + t)


def _gelu_and_grad(x):
    x2 = x * x
    t = jnp.tanh(_GELU_K * (x + _GELU_C * x * x2))
    g = 0.5 * x * (1.0 + t)
    dg = 0.5 * (1.0 + t) + 0.5 * x * (1.0 - t * t) * (_GELU_K * (1.0 + 3.0 * _GELU_C * x2))
    return g, dg


def normmod_fwd(h, ng, i_n, mod, i_sh, i_sc, name):
    T, D = h.shape
    tm = _tile(T, 512, 8)

    def body(h_ref, ng_ref, mod_ref, y_ref):
        y_ref[...] = _normmod(h_ref[...], ng_ref[i_n:i_n + 1, :], mod_ref[i_sh:i_sh + 1, :],
                              mod_ref[i_sc:i_sc + 1, :]).astype(BF16)

    return pl.pallas_call(
        body, name=name, grid=(T // tm,),
        in_specs=[pl.BlockSpec((tm, D), lambda i: (i, 0)),
                  pl.BlockSpec(ng.shape, lambda i: (0, 0)),
                  pl.BlockSpec(mod.shape, lambda i: (0, 0))],
        out_specs=pl.BlockSpec((tm, D), lambda i: (i, 0)),
        out_shape=jax.ShapeDtypeStruct((T, D), BF16),
        compiler_params=_params(("parallel",)),
    )(h, ng, mod)


def ffn_up(y, wgT, wuT, name):
    T, D = y.shape
    F = _wrows(wgT)
    tm = _tile(T, 512, 16)
    tf = _tile(F, 2816, 256)
    cuts = list(range(0, tf, 768)) + [tf]

    def body(y_ref, wg_ref, wu_ref, a_ref, b_ref, s_ref):
        yv = y_ref[...]
        for c0, c1 in zip(cuts[:-1], cuts[1:]):
            a = _nt(yv, wg_ref[c0:c1, :])
            b = _nt(yv, wu_ref[c0:c1, :])
            a_ref[:, c0:c1] = a.astype(BF16)
            b_ref[:, c0:c1] = b.astype(BF16)
            s_ref[:, c0:c1] = (a * _sigmoid(a) * b).astype(BF16)

    act = jax.ShapeDtypeStruct((T, F), BF16)
    return pl.pallas_call(
        body, name=name, grid=(F // tf, T // tm),
        in_specs=[pl.BlockSpec((tm, D), lambda j, i: (i, 0)),
                  _wspec(wgT, tf, lambda j, i: j, resident=True),
                  _wspec(wuT, tf, lambda j, i: j, resident=True)],
        out_specs=[pl.BlockSpec((tm, tf), lambda j, i: (i, j))] * 3,
        out_shape=[act, act, act],
        compiler_params=_params(("parallel", "parallel")),
    )(y, wgT[0], wuT[0])


def _normmod(x, gn, sh, sc):
    r = lax.rsqrt(jnp.mean(x * x, axis=-1, keepdims=True) + EPS)
    return ((x * r) * gn) * (1.0 + sc) + sh


def resid_matmul(xs, w, h, mod, i_g, coef, name, norm_next=None):
    T, D = h.shape
    kb = xs[0].shape[1]
    assert all(x.shape == (T, kb) for x in xs) and _wrows(w) == kb * len(xs)
    tm = _tile(T, 1024, 16)
    nx = len(xs)
    n_in, n_out, n_shape, n_ops = [], [], [], []
    if norm_next:
        ng_n, i_n, mod_n, i_sh, i_sc = norm_next
        n_in = [pl.BlockSpec(ng_n.shape, lambda i: (0, 0)), pl.BlockSpec(mod_n.shape, lambda i: (0, 0))]
        n_out = [pl.BlockSpec((tm, D), lambda i: (i, 0))]
        n_shape = [jax.ShapeDtypeStruct((T, D), BF16)]
        n_ops = [ng_n, mod_n]

    def body(*refs):
        x_refs, w_refs = refs[:nx], refs[nx:2 * nx]
        h_ref, mod_ref = refs[2 * nx:2 * nx + 2]
        hn_ref, o_ref = refs[2 * nx + 2 + len(n_in):2 * nx + 4 + len(n_in)]
        o = _nn(x_refs[0][...], w_refs[0][...])
        for xr, wr in zip(x_refs[1:], w_refs[1:]):
            o = o + _nn(xr[...], wr[...])
        o_ref[...] = o.astype(BF16)
        hn = h_ref[...] + (coef * mod_ref[i_g:i_g + 1, :]) * o
        hn_ref[...] = hn
        if norm_next:
            ng_ref, modn_ref = refs[2 * nx + 2], refs[2 * nx + 3]
            refs[-1][...] = _normmod(hn, ng_ref[i_n:i_n + 1, :], modn_ref[i_sh:i_sh + 1, :],
                                     modn_ref[i_sc:i_sc + 1, :]).astype(BF16)

    return pl.pallas_call(
        body, name=name, grid=(T // tm,),
        in_specs=([pl.BlockSpec((tm, kb), lambda i: (i, 0))] * nx
                  + [_wspec(w, kb, lambda i, p=p: p, resident=True) for p in range(nx)]
                  + [pl.BlockSpec((tm, D), lambda i: (i, 0)),
                     pl.BlockSpec(mod.shape, lambda i: (0, 0))] + n_in),
        out_specs=[pl.BlockSpec((tm, D), lambda i: (i, 0))] * 2 + n_out,
        out_shape=[jax.ShapeDtypeStruct((T, D), F32), jax.ShapeDtypeStruct((T, D), BF16)] + n_shape,
        compiler_params=_params(("parallel",)),
    )(*xs, *([w[0]] * nx), h, mod, *n_ops)


def _gate_specs(gate, tm, D):
    o, mod, _, _ = gate
    T = o.shape[0]
    return ([pl.BlockSpec((tm, D), lambda i: (i, 0)), pl.BlockSpec(mod.shape, lambda i: (0, 0))],
            [pl.BlockSpec((tm, D), lambda i: (i, 0)), pl.BlockSpec((8, D), lambda i: (0, 0))],
            [jax.ShapeDtypeStruct((T, D), BF16), jax.ShapeDtypeStruct((8, D), F32)],
            [o, mod])


def _gate_emit(d, gate, o_ref, mod_ref, do_ref, red_ref):
    _, _, i_g, coef = gate
    do_ref[...] = (d * (coef * mod_ref[i_g:i_g + 1, :])).astype(BF16)

    @pl.when(pl.program_id(0) == 0)
    def _():
        red_ref[...] = jnp.zeros_like(red_ref)

    red_ref[0:1, :] += coef * jnp.sum(d * o_ref[...].astype(F32), axis=0, keepdims=True)


def ffn_bwd_mid(do, wd, a, b, name):
    T, D = do.shape
    F = _wrows(wd)
    tm = _tile(T, 512, 16)
    tf = _tile(F, 2816, 256)
    cuts = list(range(0, tf, 256)) + [tf]

    def body(do_ref, wd_ref, a_ref, b_ref, da_ref, db_ref):
        dov = do_ref[...]
        for c0, c1 in zip(cuts[:-1], cuts[1:]):
            ds = _nt(dov, wd_ref[c0:c1, :])
            av = a_ref[:, c0:c1].astype(F32)
            bv = b_ref[:, c0:c1].astype(F32)
            sig = _sigmoid(av)
            da_ref[:, c0:c1] = (ds * bv * (sig * (1.0 + av * (1.0 - sig)))).astype(BF16)
            db_ref[:, c0:c1] = (ds * (av * sig)).astype(BF16)

    act = jax.ShapeDtypeStruct((T, F), BF16)
    return pl.pallas_call(
        body, name=name, grid=(F // tf, T // tm),
        in_specs=[pl.BlockSpec((tm, D), lambda j, i: (i, 0)),
                  _wspec(wd, tf, lambda j, i: j, resident=True),
                  pl.BlockSpec((tm, tf), lambda j, i: (i, j)),
                  pl.BlockSpec((tm, tf), lambda j, i: (i, j))],
        out_specs=[pl.BlockSpec((tm, tf), lambda j, i: (i, j))] * 2,
        out_shape=[act, act],
        compiler_params=_params(("parallel", "parallel")),
    )(do, wd[0], a, b)


def dy_normbwd(pairs, h, dhp, ng, i_n, mod, i_sc, name, gate=None):
    T, D = h.shape
    tm = _tile(T, 512, 16)
    npair = len(pairs)
    g_in, g_out, g_shape, g_ops = _gate_specs(gate, tm, D) if gate else ([], [], [], [])

    def body(*refs):
        x_refs, w_refs = refs[:npair], refs[npair:2 * npair]
        h_ref, dhp_ref, ng_ref, mod_ref = refs[2 * npair:2 * npair + 4]
        dh_ref, red_ref = refs[2 * npair + 4 + len(g_in):2 * npair + 6 + len(g_in)]
        dy = _nn(x_refs[0][...], w_refs[0][...])
        for xr, wr in zip(x_refs[1:], w_refs[1:]):
            dy = dy + _nn(xr[...], wr[...])
        x = h_ref[...]
        r = lax.rsqrt(jnp.mean(x * x, axis=-1, keepdims=True) + EPS)
        n = x * r
        gn = ng_ref[i_n:i_n + 1, :]
        dnh = dy * (1.0 + mod_ref[i_sc:i_sc + 1, :])

        @pl.when(pl.program_id(0) == 0)
        def _():
            red_ref[...] = jnp.zeros_like(red_ref)

        red_ref[0:1, :] += jnp.sum(dy, axis=0, keepdims=True)
        red_ref[1:2, :] += jnp.sum(dy * (n * gn), axis=0, keepdims=True)
        red_ref[2:3, :] += jnp.sum(dnh * n, axis=0, keepdims=True)
        dn = dnh * gn
        dh_new = dhp_ref[...] + r * (dn - n * jnp.mean(dn * n, axis=-1, keepdims=True))
        dh_ref[...] = dh_new
        if gate:
            _gate_emit(dh_new, gate, refs[2 * npair + 4], refs[2 * npair + 5], refs[-2], refs[-1])

    in_specs = ([pl.BlockSpec((tm, kb), lambda i, c=c: (i, c)) for (_, c, _, _, kb) in pairs]
                + [_wspec(w, kb, lambda i, r=r: r, resident=True) for (_, _, w, r, kb) in pairs]
                + [pl.BlockSpec((tm, D), lambda i: (i, 0)),
                   pl.BlockSpec((tm, D), lambda i: (i, 0)),
                   pl.BlockSpec(ng.shape, lambda i: (0, 0)),
                   pl.BlockSpec(mod.shape, lambda i: (0, 0))] + g_in)
    return pl.pallas_call(
        body, name=name, grid=(T // tm,), in_specs=in_specs,
        out_specs=[pl.BlockSpec((tm, D), lambda i: (i, 0)), pl.BlockSpec((8, D), lambda i: (0, 0))] + g_out,
        out_shape=[jax.ShapeDtypeStruct((T, D), F32), jax.ShapeDtypeStruct((8, D), F32)] + g_shape,
        compiler_params=_params(("arbitrary",)),
    )(*[p[0] for p in pairs], *[p[2][0] for p in pairs], h, dhp, ng, mod, *g_ops)


def matmul_tn(a, b, buf, slot, row0, name, tmo_cap=1408):
    T, N = b.shape
    ma = a.shape[1]
    tmo = _tile(ma, tmo_cap, 128)
    assert row0 % tmo == 0
    nmo = ma // tmo
    tk = _tile(T, 2048, 16)
    nk = T // tk

    def body(a_ref, b_ref, buf_ref, o_ref, acc_ref):
        k = pl.program_id(1)

        @pl.when(k == 0)
        def _():
            acc_ref[...] = jnp.zeros_like(acc_ref)

        acc_ref[...] += _tn(a_ref[...], b_ref[...])

        @pl.when(k == nk - 1)
        def _():
            o_ref[...] = acc_ref[...].astype(BF16)

    return pl.pallas_call(
        body, name=name, grid=(nmo, nk),
        in_specs=[pl.BlockSpec((tk, tmo), lambda j, k: (k, j)),
                  pl.BlockSpec((tk, N), lambda j, k: (k, 0)),
                  pl.BlockSpec(memory_space=pl.ANY)],
        out_specs=pl.BlockSpec((None, tmo, N), lambda j, k: (slot, row0 // tmo + j, 0)),
        out_shape=jax.ShapeDtypeStruct(buf.shape, BF16),
        scratch_shapes=[pltpu.VMEM((tmo, N), F32)],
        input_output_aliases={2: 0},
        compiler_params=_params(("parallel", "arbitrary")),
    )(a, b, buf)


def matmul_nt(x, w, name):
    T, K = x.shape
    N = _wrows(w)
    tm = _tile(T, 1024, 16)
    tn = _tile(N, 1280, 128)

    def body(x_ref, w_ref, o_ref):
        o_ref[...] = _nt(x_ref[...], w_ref[...]).astype(BF16)

    return pl.pallas_call(
        body, name=name, grid=(N // tn, T // tm),
        in_specs=[pl.BlockSpec((tm, K), lambda j, i: (i, 0)), _wspec(w, tn, lambda j, i: j)],
        out_specs=pl.BlockSpec((tm, tn), lambda j, i: (i, j)),
        out_shape=jax.ShapeDtypeStruct((T, N), BF16),
        compiler_params=_params(("parallel", "parallel")),
    )(x, w[0])


def _sgu_head_fwd(u, v, lng, lnb):
    gu, dgu = _gelu_and_grad(u)
    gv, dgv = _gelu_and_grad(v)
    mu = jnp.mean(gv, axis=-1, keepdims=True)
    xc = gv - mu
    rstd = lax.rsqrt(jnp.mean(xc * xc, axis=-1, keepdims=True) + EPS)
    xhat = xc * rstd
    vn = xhat * lng + lnb
    return gu, dgu, dgv, rstd, xhat, vn


def _tril_mask():
    r = lax.broadcasted_iota(jnp.int32, (ATT_BLOCK, ATT_BLOCK), 0)
    c = lax.broadcasted_iota(jnp.int32, (ATT_BLOCK, ATT_BLOCK), 1)
    return c <= r


def _triu_mask():
    r = lax.broadcasted_iota(jnp.int32, (ATT_BLOCK, ATT_BLOCK), 0)
    c = lax.broadcasted_iota(jnp.int32, (ATT_BLOCK, ATT_BLOCK), 1)
    return r <= c


def sgu_fwd(proj, lng, lnb, w, bcol, name):
    T = proj.shape[0]
    tm = _tile(T, 512, 128)
    nch = tm // ATT_BLOCK

    def body(u_ref, v_ref, lng_ref, lnb_ref, w_ref, b_ref, o_ref):
        tril = _tril_mask()
        for hd in range(SGU_HEADS):
            sl = slice(hd * HEAD_LANES, (hd + 1) * HEAD_LANES)
            u = u_ref[:, sl].astype(F32)
            v = v_ref[:, sl].astype(F32)
            gu, _, _, _, _, vn = _sgu_head_fwd(u, v, lng_ref[:, sl], lnb_ref[:, sl])
            wm = jnp.where(tril, w_ref[hd], 0.0).astype(BF16)
            vnb = vn.astype(BF16)
            bc = b_ref[:, hd:hd + 1]
            for ch in range(nch):
                rs = slice(ch * ATT_BLOCK, (ch + 1) * ATT_BLOCK)
                z = _nn(wm, vnb[rs, :]) + bc
                o_ref[rs, sl] = (gu[rs, :] * z).astype(BF16)

    return pl.pallas_call(
        body, name=name, grid=(T // tm,),
        in_specs=[pl.BlockSpec((tm, MIX_HALF), lambda i: (i, 0)),
                  pl.BlockSpec((tm, MIX_HALF), lambda i: (i, 1)),
                  pl.BlockSpec((1, MIX_HALF), lambda i: (0, 0)),
                  pl.BlockSpec((1, MIX_HALF), lambda i: (0, 0)),
                  pl.BlockSpec(w.shape, lambda i: (0, 0, 0)),
                  pl.BlockSpec(bcol.shape, lambda i: (0, 0))],
        out_specs=pl.BlockSpec((tm, MIX_HALF), lambda i: (i, 0)),
        out_shape=jax.ShapeDtypeStruct((T, MIX_HALF), BF16),
        compiler_params=_params(("parallel",)),
    )(proj, proj, lng, lnb, w, bcol)


def sgu_bwd(proj, dmixed, lng, lnb, w, wt, bcol, name):
    T = proj.shape[0]
    tm = _tile(T, 512, 128)
    nch = tm // ATT_BLOCK
    nsteps = T // tm

    def body(u_ref, v_ref, g_ref, lng_ref, lnb_ref, w_ref, wt_ref, b_ref, duv_ref, dw_ref, dvec_ref, bacc_ref):
        step = pl.program_id(0)

        @pl.when(step == 0)
        def _():
            dw_ref[...] = jnp.zeros_like(dw_ref)
            dvec_ref[...] = jnp.zeros_like(dvec_ref)
            bacc_ref[...] = jnp.zeros_like(bacc_ref)

        tril = _tril_mask()
        triu = _triu_mask()
        for hd in range(SGU_HEADS):
            sl = slice(hd * HEAD_LANES, (hd + 1) * HEAD_LANES)
            u = u_ref[:, sl].astype(F32)
            v = v_ref[:, sl].astype(F32)
            lng_h = lng_ref[:, sl]
            gu, dgu, dgv, rstd, xhat, vn = _sgu_head_fwd(u, v, lng_h, lnb_ref[:, sl])
            wm = jnp.where(tril, w_ref[hd], 0.0).astype(BF16)
            wmt = jnp.where(triu, wt_ref[hd], 0.0).astype(BF16)
            vnb = vn.astype(BF16)
            bc = b_ref[:, hd:hd + 1]
            g = g_ref[:, sl].astype(F32)
            dw_acc = jnp.zeros((ATT_BLOCK, ATT_BLOCK), F32)
            b_acc = jnp.zeros((ATT_BLOCK, HEAD_LANES), F32)
            dvn_parts = []
            for ch in range(nch):
                rs = slice(ch * ATT_BLOCK, (ch + 1) * ATT_BLOCK)
                z = _nn(wm, vnb[rs, :]) + bc
                duv_ref[rs, sl] = (g[rs, :] * z * dgu[rs, :]).astype(BF16)
                dz = g[rs, :] * gu[rs, :]
                dzb = dz.astype(BF16)
                dvn_parts.append(_nn(wmt, dzb))
                dw_acc = dw_acc + _nt(dzb, vnb[rs, :])
                b_acc = b_acc + dz
            dvn = jnp.concatenate(dvn_parts, axis=0)
            dw_ref[hd] += jnp.where(tril, dw_acc, 0.0)
            bacc_ref[hd] += b_acc
            dvec_ref[0:1, sl] += jnp.sum(dvn * xhat, axis=0, keepdims=True)
            dvec_ref[1:2, sl] += jnp.sum(dvn, axis=0, keepdims=True)
            dxh = dvn * lng_h
            dgv_in = rstd * (dxh - jnp.mean(dxh, axis=-1, keepdims=True)
                             - xhat * jnp.mean(dxh * xhat, axis=-1, keepdims=True))
            duv_ref[:, MIX_HALF + hd * HEAD_LANES:MIX_HALF + (hd + 1) * HEAD_LANES] = (dgv_in * dgv).astype(BF16)

        @pl.when(step == nsteps - 1)
        def _():
            for hd in range(SGU_HEADS):
                sl = slice(hd * HEAD_LANES, (hd + 1) * HEAD_LANES)
                dvec_ref[2:3, sl] = jnp.sum(bacc_ref[hd].T, axis=0, keepdims=True)

    return pl.pallas_call(
        body, name=name, grid=(nsteps,),
        in_specs=[pl.BlockSpec((tm, MIX_HALF), lambda i: (i, 0)),
                  pl.BlockSpec((tm, MIX_HALF), lambda i: (i, 1)),
                  pl.BlockSpec((tm, MIX_HALF), lambda i: (i, 0)),
                  pl.BlockSpec((1, MIX_HALF), lambda i: (0, 0)),
                  pl.BlockSpec((1, MIX_HALF), lambda i: (0, 0)),
                  pl.BlockSpec(w.shape, lambda i: (0, 0, 0)),
                  pl.BlockSpec(w.shape, lambda i: (0, 0, 0)),
                  pl.BlockSpec(bcol.shape, lambda i: (0, 0))],
        out_specs=[pl.BlockSpec((tm, 2 * MIX_HALF), lambda i: (i, 0)),
                   pl.BlockSpec(w.shape, lambda i: (0, 0, 0)),
                   pl.BlockSpec((8, MIX_HALF), lambda i: (0, 0))],
        out_shape=[jax.ShapeDtypeStruct((T, 2 * MIX_HALF), BF16),
                   jax.ShapeDtypeStruct(w.shape, F32),
                   jax.ShapeDtypeStruct((8, MIX_HALF), F32)],
        scratch_shapes=[pltpu.VMEM((SGU_HEADS, ATT_BLOCK, HEAD_LANES), F32)],
        compiler_params=_params(("arbitrary",)),
    )(proj, proj, dmixed, lng, lnb, w, wt, bcol)


def _rot_half(t):
    lane = lax.broadcasted_iota(jnp.int32, t.shape, 1)
    first = (lane % ATT_DH) < (ATT_DH // 2)
    return jnp.where(first, -pltpu.roll(t, HEAD_LANES - ATT_DH // 2, 1), pltpu.roll(t, ATT_DH // 2, 1))


LAYOUT_ROWS = 512


def _res_spec(d, tm, W):
    return pl.BlockSpec((d, tm // d, W), lambda i: (0, i, 0))


def _res_shape(d, T, W, dtype):
    return jax.ShapeDtypeStruct((d, T // d, W), dtype)


def _slab_buf(tm, W):
    return pltpu.VMEM((W // HEAD_LANES, tm, HEAD_LANES), F32)


def _lanes(hp):
    return slice(hp * HEAD_LANES, (hp + 1) * HEAD_LANES)


def _to_res(buf, out_ref, d, dtype):
    nslab, tm, _ = buf.shape
    for hp in range(nslab):
        if d == 1:
            out_ref[0, :, _lanes(hp)] = buf[hp].astype(dtype)
        else:
            for r in range(d):
                out_ref[r, :, _lanes(hp)] = buf.at[hp][pl.ds(r, tm // d, stride=d), :].astype(dtype)


def _from_res(in_ref, buf, d):
    nslab, tm, _ = buf.shape
    for hp in range(nslab):
        if d == 1:
            buf[hp] = in_ref[0, :, _lanes(hp)]
        else:
            for r in range(d):
                buf.at[hp][pl.ds(r, tm // d, stride=d), :] = in_ref[r, :, _lanes(hp)]


def rope_fwd(proj, cos, sin, name):
    T = proj.shape[0]
    tm = LAYOUT_ROWS
    scale = 1.0 / math.sqrt(ATT_DH)
    nd = len(DILATIONS)

    def body(q_ref, k_ref, v_ref, cos_ref, sin_ref, *rest):
        outs, buf = rest[:3 * nd], rest[3 * nd]
        c = cos_ref[...]
        s = sin_ref[...]
        for which, src in enumerate((q_ref, k_ref, v_ref)):
            for hp in range(MIX_HALF // HEAD_LANES):
                t = src[:, _lanes(hp)].astype(F32)
                if which == 0:
                    t = scale * (t * c + _rot_half(t) * s)
                elif which == 1:
                    t = t * c + _rot_half(t) * s
                buf[hp] = t
            for di, d in enumerate(DILATIONS):
                _to_res(buf, outs[3 * di + which], d, BF16)

    return pl.pallas_call(
        body, name=name, grid=(T // tm,),
        in_specs=[pl.BlockSpec((tm, MIX_HALF), lambda i: (i, 2)),
                  pl.BlockSpec((tm, MIX_HALF), lambda i: (i, 3)),
                  pl.BlockSpec((tm, MIX_HALF), lambda i: (i, 4)),
                  pl.BlockSpec((tm, HEAD_LANES), lambda i: (i, 0)),
                  pl.BlockSpec((tm, HEAD_LANES), lambda i: (i, 0))],
        out_specs=[_res_spec(d, tm, MIX_HALF) for d in DILATIONS for _ in range(3)],
        out_shape=[_res_shape(d, T, MIX_HALF, BF16) for d in DILATIONS for _ in range(3)],
        scratch_shapes=[_slab_buf(tm, MIX_HALF)],
        compiler_params=_params(("parallel",)),
    )(proj, proj, proj, cos, sin)


def to_residues(x, col, name):
    T = x.shape[0]
    tm = LAYOUT_ROWS

    def body(x_ref, *rest):
        outs, buf = rest[:-1], rest[-1]
        for hp in range(MIX_HALF // HEAD_LANES):
            buf[hp] = x_ref[:, _lanes(hp)].astype(F32)
        for o_ref, d in zip(outs, DILATIONS):
            _to_res(buf, o_ref, d, BF16)

    return pl.pallas_call(
        body, name=name, grid=(T // tm,),
        in_specs=[pl.BlockSpec((tm, MIX_HALF), lambda i: (i, col))],
        out_specs=[_res_spec(d, tm, MIX_HALF) for d in DILATIONS],
        out_shape=[_res_shape(d, T, MIX_HALF, BF16) for d in DILATIONS],
        scratch_shapes=[_slab_buf(tm, MIX_HALF)],
        compiler_params=_params(("parallel",)),
    )(x)


def rope_bwd(dqs, dks, dvs, cos, sin, name):
    T = dqs[0].shape[0] * dqs[0].shape[1]
    tm = LAYOUT_ROWS
    scale = 1.0 / math.sqrt(ATT_DH)
    npat = len(dqs)

    def body(*refs):
        groups = refs[:npat], refs[npat:2 * npat], refs[2 * npat:3 * npat]
        cos_ref, sin_ref, o_ref, buf, acc = refs[3 * npat:]
        c = cos_ref[...]
        s = sin_ref[...]
        for which, g_refs in enumerate(groups):
            _from_res(g_refs[0], acc, DILATIONS[0])
            for g_ref, d in zip(g_refs[1:], DILATIONS[1:]):
                _from_res(g_ref, buf, d)
                acc[...] += buf[...]
            for hp in range(MIX_HALF // HEAD_LANES):
                g = acc[hp]
                if which == 0:
                    g = scale * g
                if which < 2:
                    g = g * c - _rot_half(g * s)
                o_ref[:, which * MIX_HALF + hp * HEAD_LANES:which * MIX_HALF + (hp + 1) * HEAD_LANES] = g.astype(BF16)

    return pl.pallas_call(
        body, name=name, grid=(T // tm,),
        in_specs=([_res_spec(d, tm, MIX_HALF) for _ in range(3) for d in DILATIONS]
                  + [pl.BlockSpec((tm, HEAD_LANES), lambda i: (i, 0))] * 2),
        out_specs=pl.BlockSpec((tm, 3 * MIX_HALF), lambda i: (i, 0)),
        out_shape=jax.ShapeDtypeStruct((T, 3 * MIX_HALF), BF16),
        scratch_shapes=[_slab_buf(tm, MIX_HALF), _slab_buf(tm, MIX_HALF)],
        compiler_params=_params(("parallel",)),
    )(*dqs, *dks, *dvs, cos, sin)


def _band_masks(n):
    r = lax.broadcasted_iota(jnp.int32, (2 * ATT_BLOCK, ATT_BLOCK), 0)
    c = lax.broadcasted_iota(jnp.int32, (2 * ATT_BLOCK, ATT_BLOCK), 1)
    qi = r % ATT_BLOCK
    head = (c < ATT_DH) == (r < ATT_BLOCK)
    return (c >= qi) & (n > 0), c <= qi, head, c[:ATT_BLOCK] < ATT_DH


def _stack_heads(x, head):
    x2 = jnp.concatenate([x, x], axis=0)
    return jnp.where(head, x2, jnp.zeros_like(x2))


def attn_fwd(q, k, v, name):
    d, L, W = q.shape
    nb = L // ATT_BLOCK

    def body(q_ref, kp_ref, kc_ref, vp_ref, vc_ref, o_ref, lse_ref):
        mask_p, mask_c, head, head0 = _band_masks(pl.program_id(1))
        for hp in range(W // HEAD_LANES):
            sl = slice(hp * HEAD_LANES, (hp + 1) * HEAD_LANES)
            kp, kc, vp, vc = kp_ref[0, :, sl], kc_ref[0, :, sl], vp_ref[0, :, sl], vc_ref[0, :, sl]
            qs = _stack_heads(q_ref[0, :, sl], head)
            sp = jnp.where(mask_p, _nt(qs, kp), NEG)
            sc = jnp.where(mask_c, _nt(qs, kc), NEG)
            m = jnp.maximum(jnp.max(sp, axis=1, keepdims=True), jnp.max(sc, axis=1, keepdims=True))
            pp = jnp.exp(sp - m)
            pc = jnp.exp(sc - m)
            den = jnp.sum(pp, axis=1, keepdims=True) + jnp.sum(pc, axis=1, keepdims=True)
            o = (_nn(pp.astype(BF16), vp) + _nn(pc.astype(BF16), vc)) / den
            lse = m + jnp.log(den)
            o_ref[0, :, sl] = jnp.where(head0, o[:ATT_BLOCK], o[ATT_BLOCK:])
            lse_ref[0, :, sl] = jnp.where(head0, lse[:ATT_BLOCK], lse[ATT_BLOCK:])

    cur = pl.BlockSpec((1, ATT_BLOCK, W), lambda r, n: (r, n, 0))
    prev = pl.BlockSpec((1, ATT_BLOCK, W), lambda r, n: (r, jnp.maximum(n - 1, 0), 0))
    out = jax.ShapeDtypeStruct((d, L, W), F32)
    return pl.pallas_call(
        body, name=name, grid=(d, nb),
        in_specs=[cur, prev, cur, prev, cur],
        out_specs=[cur, cur], out_shape=[out, out],
        compiler_params=_params(("parallel", "parallel")),
    )(q, k, k, v, v)


def attn_combine(os_, lses, name):
    T = os_[0].shape[0] * os_[0].shape[1]
    W = os_[0].shape[2]
    tm = LAYOUT_ROWS
    npat = len(os_)

    def body(*refs):
        o_refs, l_refs = refs[:npat], refs[npat:2 * npat]
        out_ref = refs[2 * npat]
        ores, lres = refs[2 * npat + 1:3 * npat + 1], refs[3 * npat + 1:4 * npat + 1]
        bufs = refs[4 * npat + 1:]
        lbufs, obufs, out_buf, lse_buf = bufs[:npat], bufs[npat:2 * npat], bufs[2 * npat], bufs[2 * npat + 1]
        for p, d in enumerate(DILATIONS):
            _from_res(l_refs[p], lbufs[p], d)
            _from_res(o_refs[p], obufs[p], d)
        for hp in range(W // HEAD_LANES):
            ls = [b[hp] for b in lbufs]
            m = ls[0]
            for l in ls[1:]:
                m = jnp.maximum(m, l)
            es = [jnp.exp(l - m) for l in ls]
            z = es[0]
            for e in es[1:]:
                z = z + e
            acc = es[0] * obufs[0][hp]
            for p in range(1, npat):
                acc = acc + es[p] * obufs[p][hp]
            out = acc / z
            out_ref[:, _lanes(hp)] = out.astype(BF16)
            out_buf[hp] = out
            lse_buf[hp] = m + jnp.log(z)
        for p, d in enumerate(DILATIONS):
            _to_res(out_buf, ores[p], d, BF16)
            _to_res(lse_buf, lres[p], d, F32)

    return pl.pallas_call(
        body, name=name, grid=(T // tm,),
        in_specs=[_res_spec(d, tm, W) for _ in range(2) for d in DILATIONS],
        out_specs=([pl.BlockSpec((tm, W), lambda i: (i, 0))] + [_res_spec(d, tm, W) for _ in range(2) for d in DILATIONS]),
        out_shape=([jax.ShapeDtypeStruct((T, W), BF16)] + [_res_shape(d, T, W, BF16) for d in DILATIONS]
                   + [_res_shape(d, T, W, F32) for d in DILATIONS]),
        scratch_shapes=[_slab_buf(tm, W)] * (2 * npat + 2),
        compiler_params=_params(("parallel",)),
    )(*os_, *lses)


def attn_bwd(q, k, v, do, o, lse, name):
    d, L, W = q.shape
    nb = L // ATT_BLOCK

    def body(q_ref, kp_ref, kc_ref, vp_ref, vc_ref, do_ref, o_ref, lse_ref, dq_ref, dk_ref, dv_ref, kkeep, vkeep):
        n = pl.program_id(1)

        @pl.when(n == 0)
        def _():
            kkeep[...] = jnp.zeros_like(kkeep)
            vkeep[...] = jnp.zeros_like(vkeep)

        @pl.when(n < nb)
        def _():
            mask_p, mask_c, head, head0 = _band_masks(n)
            for hp in range(W // HEAD_LANES):
                sl = slice(hp * HEAD_LANES, (hp + 1) * HEAD_LANES)
                kp, kc, vp, vc = kp_ref[0, :, sl], kc_ref[0, :, sl], vp_ref[0, :, sl], vc_ref[0, :, sl]
                dout = do_ref[0, :, sl]
                qs = _stack_heads(q_ref[0, :, sl], head)
                dos = _stack_heads(dout, head)
                lse_v = lse_ref[0, :, sl]
                lse_c = jnp.max(jnp.where(head, jnp.concatenate([lse_v, lse_v], axis=0), NEG), axis=1, keepdims=True)
                delta = jnp.sum(_stack_heads(dout.astype(F32) * o_ref[0, :, sl].astype(F32), head), axis=1, keepdims=True)
                pp = jnp.exp(jnp.where(mask_p, _nt(qs, kp), NEG) - lse_c)
                pc = jnp.exp(jnp.where(mask_c, _nt(qs, kc), NEG) - lse_c)
                dsp = (pp * (_nt(dos, vp) - delta)).astype(BF16)
                dsc = (pc * (_nt(dos, vc) - delta)).astype(BF16)
                dq2 = _nn(dsp, kp) + _nn(dsc, kc)
                dq_ref[0, :, sl] = jnp.where(head0, dq2[:ATT_BLOCK], dq2[ATT_BLOCK:])
                dk_ref[0, :, sl] = kkeep[:, sl] + _tn(dsp, qs)
                dv_ref[0, :, sl] = vkeep[:, sl] + _tn(pp.astype(BF16), dos)
                kkeep[:, sl] = _tn(dsc, qs)
                vkeep[:, sl] = _tn(pc.astype(BF16), dos)

        @pl.when(n == nb)
        def _():
            dk_ref[0] = kkeep[...]
            dv_ref[0] = vkeep[...]

    cur = pl.BlockSpec((1, ATT_BLOCK, W), lambda r, n: (r, jnp.minimum(n, nb - 1), 0))
    prev = pl.BlockSpec((1, ATT_BLOCK, W), lambda r, n: (r, jnp.clip(n - 1, 0, nb - 1), 0))
    out = jax.ShapeDtypeStruct((d, L, W), F32)
    return pl.pallas_call(
        body, name=name, grid=(d, nb + 1),
        in_specs=[cur, prev, cur, prev, cur, cur, cur, cur],
        out_specs=[cur, prev, prev], out_shape=[out, out, out],
        scratch_shapes=[pltpu.VMEM((ATT_BLOCK, W), F32), pltpu.VMEM((ATT_BLOCK, W), F32)],
        compiler_params=_params(("parallel", "arbitrary")),
    )(q, k, k, v, v, do, o, lse)


def final_loss_bwd(h, gf, tgt, gate, name):
    T, D = h.shape
    tm = _tile(T, 512, 16)
    g_in, g_out, g_shape, g_ops = _gate_specs(gate, tm, D)

    def body(h_ref, g_ref, t_ref, o_ref, modg_ref, dh_ref, red_ref, do_ref, redg_ref):
        x = h_ref[...]
        r = lax.rsqrt(jnp.mean(x * x, axis=-1, keepdims=True) + EPS)
        n = x * r
        g = g_ref[...]
        err = n * g - t_ref[...]
        dy = err * (1.0 / D)

        @pl.when(pl.program_id(0) == 0)
        def _():
            red_ref[...] = jnp.zeros_like(red_ref)

        red_ref[0:1, :] += jnp.sum(dy * n, axis=0, keepdims=True)
        red_ref[1:2, :] += jnp.zeros((1, D), F32) + (0.5 / D) * jnp.sum(err * err, keepdims=True)
        dn = dy * g
        dh = r * (dn - n * jnp.mean(dn * n, axis=-1, keepdims=True))
        dh_ref[...] = dh
        _gate_emit(dh, gate, o_ref, modg_ref, do_ref, redg_ref)

    return pl.pallas_call(
        body, name=name, grid=(T // tm,),
        in_specs=[pl.BlockSpec((tm, D), lambda i: (i, 0)),
                  pl.BlockSpec((1, D), lambda i: (0, 0)),
                  pl.BlockSpec((tm, D), lambda i: (i, 0))] + g_in,
        out_specs=[pl.BlockSpec((tm, D), lambda i: (i, 0)), pl.BlockSpec((8, D), lambda i: (0, 0))] + g_out,
        out_shape=[jax.ShapeDtypeStruct((T, D), F32), jax.ShapeDtypeStruct((8, D), F32)] + g_shape,
        compiler_params=_params(("arbitrary",)),
    )(h, gf, tgt, *g_ops)


def ada_fwd(c_all, ada_w, ada_b, name):
    nl, D, N = ada_w.shape

    def body(c_ref, w_ref, b_ref, o_ref):
        c = c_ref[...]
        o_ref[0] = _nn(c * _sigmoid(c), w_ref[0]) + b_ref[0]

    return pl.pallas_call(
        body, name=name, grid=(nl,),
        in_specs=[pl.BlockSpec((N_DEV, D), lambda l: (0, 0)),
                  pl.BlockSpec((1, D, N), lambda l: (l, 0, 0)),
                  pl.BlockSpec((1, 1, N), lambda l: (l, 0, 0))],
        out_specs=pl.BlockSpec((1, N_DEV, N), lambda l: (l, 0, 0)),
        out_shape=jax.ShapeDtypeStruct((nl, N_DEV, N), F32),
        compiler_params=_params(("parallel",)),
    )(c_all, ada_w, ada_b)


def ada_bwd(c_allT, dmod, name):
    nl, _, N = dmod.shape
    D = c_allT.shape[0]

    def body(c_ref, g_ref, o_ref):
        c = c_ref[...]
        ca = c * _sigmoid(c)
        acc = ca[:, 0:1] * g_ref[0, 0:1, :]
        for b in range(1, N_DEV):
            acc = acc + ca[:, b:b + 1] * g_ref[0, b:b + 1, :]
        o_ref[0] = acc

    return pl.pallas_call(
        body, name=name, grid=(nl,),
        in_specs=[pl.BlockSpec((D, N_DEV), lambda l: (0, 0)),
                  pl.BlockSpec((1, N_DEV, N), lambda l: (l, 0, 0))],
        out_specs=pl.BlockSpec((1, D, N), lambda l: (l, 0, 0)),
        out_shape=jax.ShapeDtypeStruct((nl, D, N), F32),
        compiler_params=_params(("parallel",)),
    )(c_allT, dmod)


def adamw(w, g, m, v, name):
    R, C = w.shape
    tr = _tile(R, max(8, (1 << 19) // C // 8 * 8), 8)
    c1 = 1.0 - ADAM_B1 ** ADAM_STEP
    c2 = 1.0 - ADAM_B2 ** ADAM_STEP

    def body(w_ref, g_ref, m_ref, v_ref, d_ref, mo_ref, vo_ref):
        gv = g_ref[...]
        mn = ADAM_B1 * m_ref[...] + (1.0 - ADAM_B1) * gv
        vn = ADAM_B2 * v_ref[...] + (1.0 - ADAM_B2) * (gv * gv)
        mo_ref[...] = mn
        vo_ref[...] = vn
        d_ref[...] = -ADAM_LR * ((mn / c1) / (jnp.sqrt(vn / c2) + ADAM_EPS) + ADAM_WD * w_ref[...])

    blk = pl.BlockSpec((tr, C), lambda i: (i, 0))
    out = jax.ShapeDtypeStruct((R, C), F32)
    return pl.pallas_call(
        body, name=name, grid=(R // tr,),
        in_specs=[blk] * 4, out_specs=[blk] * 3, out_shape=[out] * 3,
        compiler_params=_params(("parallel",)),
    )(w, g, m, v)


def adamw_layer(w, g, m, v, l, prev, name):
    NLw, R, C = w.shape
    tr = _tile(R, max(8, (1 << 19) // C // 8 * 8), 8)
    nrb = R // tr
    c1 = 1.0 - ADAM_B1 ** ADAM_STEP
    c2 = 1.0 - ADAM_B2 ** ADAM_STEP
    w, m, v = (t.reshape(NLw * R, C) for t in (w, m, v))

    def body(w_ref, g_ref, m_ref, v_ref, *rest):
        go_ref, d_ref, mo_ref, vo_ref = rest[-4:]
        gv = g_ref[...]
        mn = ADAM_B1 * m_ref[...] + (1.0 - ADAM_B1) * gv
        vn = ADAM_B2 * v_ref[...] + (1.0 - ADAM_B2) * (gv * gv)
        go_ref[...] = gv
        mo_ref[...] = mn
        vo_ref[...] = vn
        d_ref[...] = -ADAM_LR * ((mn / c1) / (jnp.sqrt(vn / c2) + ADAM_EPS) + ADAM_WD * w_ref[...])

    lay = pl.BlockSpec((tr, C), lambda i: (l * nrb + i, 0))
    out = jax.ShapeDtypeStruct((NLw * R, C), F32)
    n_prev = 0 if prev is None else 4
    return pl.pallas_call(
        body, name=name, grid=(nrb,),
        in_specs=[lay, pl.BlockSpec((tr, C), lambda i: (i, 0)), lay, lay] + [pl.BlockSpec(memory_space=pl.ANY)] * n_prev,
        out_specs=[lay] * 4, out_shape=[out] * 4,
        input_output_aliases={4 + i: i for i in range(n_prev)},
        compiler_params=_params(("parallel",)),
    )(w, g, m, v, *(prev or ()))


def sum_slots(x, name):
    S, R, C = x.shape
    tr = _tile(R, 128, 8)

    def body(x_ref, o_ref):
        acc = x_ref[0]
        for s in range(1, S):
            acc = acc + x_ref[s]
        o_ref[...] = acc

    return pl.pallas_call(
        body, name=name, grid=(R // tr,),
        in_specs=[pl.BlockSpec((S, tr, C), lambda i: (0, i, 0))],
        out_specs=pl.BlockSpec((tr, C), lambda i: (i, 0)),
        out_shape=jax.ShapeDtypeStruct((R, C), F32),
        compiler_params=_params(("parallel",)),
    )(x)


def sum_halves(g, lands, c_idx, name):
    n, ns, _, rh, D = g.shape

    def body(c_ref, g_ref, l_ref, o_ref):
        o_ref[0, 0] = (g_ref[0, 0, 0].astype(F32) + l_ref[0, 0].astype(F32)).astype(BF16)

    return pl.pallas_call(
        body, name=name,
        grid_spec=pltpu.PrefetchScalarGridSpec(
            num_scalar_prefetch=1, grid=(n, ns),
            in_specs=[pl.BlockSpec((1, 1, 1, rh, D), lambda i, j, c: (i, j, c[0], 0, 0)),
                      pl.BlockSpec((1, 1, rh, D), lambda i, j, c: (i, j, 0, 0))],
            out_specs=pl.BlockSpec((1, 1, rh, D), lambda i, j, c: (i, j, 0, 0))),
        out_shape=jax.ShapeDtypeStruct((n, ns, rh, D), BF16),
        compiler_params=_params(("parallel", "parallel")),
    )(c_idx, g, lands)


def sum_chips(p, lands, place, name):
    n, ns, rh, D = p.shape

    def body(c_ref, p_ref, l_ref, o_ref):
        acc = p_ref[0, 0].astype(F32)
        for j in range(N_CHIP - 1):
            acc = acc + l_ref[j, 0].astype(F32)
        o_ref[0, 0] = acc

    return pl.pallas_call(
        body, name=name,
        grid_spec=pltpu.PrefetchScalarGridSpec(
            num_scalar_prefetch=1, grid=(n,),
            in_specs=[pl.BlockSpec((1, 1, rh, D), lambda i, c: (i, c[0], 0, 0)),
                      pl.BlockSpec((N_CHIP - 1, 1, rh, D), lambda i, c: (0, i, 0, 0))],
            out_specs=pl.BlockSpec((1, 1, rh, D), lambda i, c: (i, c[1], 0, 0))),
        out_shape=jax.ShapeDtypeStruct((n, 2, rh, D), F32),
        compiler_params=_params(("parallel",)),
    )(place, p, lands)


def _my_place():
    return lax.axis_index("x"), lax.axis_index("y"), lax.axis_index("c")


def _other_chips(mx, my):
    return [(1 - mx, my), (mx, 1 - my), (1 - mx, 1 - my)]


def gather_small(x, after, name):
    def body(x_ref, after_ref, out_ref, sum_ref, send_sems, recv_sems):
        mx, my, mc = _my_place()
        me = 4 * mx + 2 * my + mc
        out_ref[me] = x_ref[...]
        sends = []
        for k in range(1, N_DEV):
            kx, ky, kc = (k >> 2) & 1, (k >> 1) & 1, k & 1
            peer = (1 - mx if kx else mx, 1 - my if ky else my, 1 - mc if kc else mc)
            cp = pltpu.make_async_remote_copy(
                src_ref=x_ref, dst_ref=out_ref.at[me], send_sem=send_sems.at[k - 1], recv_sem=recv_sems.at[k - 1],
                device_id=peer, device_id_type=MESH)
            cp.start()
            sends.append((cp, 4 * peer[0] + 2 * peer[1] + peer[2], peer))
        for k, (cp, peer_slot, peer) in enumerate(sends):
            pltpu.make_async_remote_copy(
                src_ref=x_ref, dst_ref=out_ref.at[peer_slot], send_sem=send_sems.at[k], recv_sem=recv_sems.at[k],
                device_id=peer, device_id_type=MESH).wait_recv()
        for cp, _, _ in sends:
            cp.wait_send()
        acc = out_ref[0]
        for s in range(1, N_DEV):
            acc = acc + out_ref[s]
        sum_ref[...] = acc

    vmem = pl.BlockSpec(memory_space=pltpu.VMEM)
    return pl.pallas_call(
        body, name=name,
        in_specs=[vmem, pl.BlockSpec(memory_space=pl.ANY)], out_specs=[vmem, vmem],
        out_shape=[jax.ShapeDtypeStruct((N_DEV,) + x.shape, x.dtype), jax.ShapeDtypeStruct(x.shape, x.dtype)],
        scratch_shapes=[pltpu.SemaphoreType.DMA((N_DEV - 1,)), pltpu.SemaphoreType.DMA((N_DEV - 1,))],
        compiler_params=pltpu.CompilerParams(vmem_limit_bytes=VMEM_LIMIT),
    )(x, after)


_HBM =pl.BlockSpec(memory_space=pltpu.HBM)
_SEM = pl.BlockSpec(memory_space=pltpu.SEMAPHORE)
_DATAFLOW = pltpu.SideEffectType.DATAFLOW_SIDE_EFFECTING


def _gather_copies(shard, land, send, recv, base):
    mx, my, mc = _my_place()
    ci = 2 * mx + my
    peers = [((cx, cy, mc), 2 * cx + cy) for cx, cy in _other_chips(mx, my)] + [((mx, my, 1 - mc), ci)]
    out = []
    for q, (dev, src_slot) in enumerate(peers):
        out.append((
            pltpu.make_async_remote_copy(src_ref=shard, dst_ref=land.at[:, ci], send_sem=send.at[base + q],
                                         recv_sem=recv.at[base + q], device_id=dev, device_id_type=MESH),
            pltpu.make_async_remote_copy(src_ref=shard, dst_ref=land.at[:, src_slot], send_sem=send.at[base + q],
                                         recv_sem=recv.at[base + q], device_id=dev, device_id_type=MESH)))
    return out


def gather_start(groups, after, name):
    items = [s for g in groups for s in g]
    ni, ng = len(items), len(groups)

    def body(*refs):
        shards, lands = refs[:ni], refs[ni:2 * ni]
        sems = refs[2 * ni + 1:2 * ni + 1 + 2 * ng]
        token = refs[-1]
        i = 0
        for g, grp in enumerate(groups):
            for p in range(len(grp)):
                for start_cp, _ in _gather_copies(shards[i], lands[i], sems[2 * g], sems[2 * g + 1], 4 * p):
                    start_cp.start()
                i += 1
        token[...] = jnp.zeros_like(token)

    sem_shapes = []
    for grp in groups:
        sem_shapes += [pltpu.SemaphoreType.DMA((4 * len(grp),))] * 2
    land_shapes = [(s.shape[0], N_CHIP) + s.shape[1:] for s in items]
    outs = pl.pallas_call(
        body, name=name,
        in_specs=[_HBM] * (2 * ni) + [pl.BlockSpec(memory_space=pl.ANY)],
        out_specs=[_SEM] * (2 * ng) + [_HBM] * (2 * ni) + [pl.BlockSpec(memory_space=pltpu.VMEM)],
        out_shape=(sem_shapes + [pltpu.HBM(s.shape, s.dtype) for s in items]
                   + [pltpu.HBM(ls, s.dtype) for ls, s in zip(land_shapes, items)]
                   + [jax.ShapeDtypeStruct((8, 128), F32)]),
        input_output_aliases={i: 2 * ng + i for i in range(2 * ni)},
        compiler_params=pltpu.CompilerParams(has_side_effects=_DATAFLOW),
    )(*[pltpu.with_memory_space_constraint(s, pltpu.HBM) for s in items],
      *[pltpu.with_memory_space_constraint(lax.empty(ls, s.dtype), pltpu.HBM) for ls, s in zip(land_shapes, items)],
      after)
    sems, thru, token = outs[:2 * ng], outs[2 * ng:2 * ng + 2 * ni], outs[-1]
    handles, i = [], 0
    for g, grp in enumerate(groups):
        n = len(grp)
        handles.append((sems[2 * g], sems[2 * g + 1], thru[i:i + n], thru[ni + i:ni + i + n]))
        i += n
    return handles, token


def gather_wait(handle, after, name):
    send, recv, shards, lands = handle
    n = len(shards)

    def body(*refs):
        shard_refs, land_refs = refs[:n], refs[n:2 * n]
        send_ref, recv_ref = refs[2 * n], refs[2 * n + 1]
        for p in range(n):
            for start_cp, recv_cp in _gather_copies(shard_refs[p], land_refs[p], send_ref, recv_ref, 4 * p):
                start_cp.wait_send()
                recv_cp.wait_recv()

    outs = pl.pallas_call(
        body, name=name,
        in_specs=[_HBM] * (2 * n) + [_SEM, _SEM, pl.BlockSpec(memory_space=pl.ANY)],
        out_specs=[_HBM] * (2 * n),
        out_shape=[pltpu.HBM(s.shape, s.dtype) for s in shards] + [pltpu.HBM(l.shape, l.dtype) for l in lands],
        input_output_aliases={i: i for i in range(2 * n)},
        compiler_params=pltpu.CompilerParams(has_side_effects=_DATAFLOW),
    )(*shards, *lands, send, recv, after)
    return outs[n:]


def _sibling_copies(gs, lands, send, recv):
    mx, my, mc = _my_place()
    return [pltpu.make_async_remote_copy(
        src_ref=gs[k].at[:, :, 1 - mc], dst_ref=lands[k], send_sem=send.at[k], recv_sem=recv.at[k],
        device_id=(mx, my, 1 - mc), device_id_type=MESH) for k in range(len(gs))]


def sibling_start(gs, after, name):
    K = len(gs)

    def body(*refs):
        ins, lands = refs[:K], refs[K:2 * K]
        send, recv = refs[2 * K + 1], refs[2 * K + 2]
        for cp in _sibling_copies(ins, lands, send, recv):
            cp.start()
        refs[-1][...] = jnp.zeros_like(refs[-1])

    land_shapes = [g.shape[:2] + g.shape[3:] for g in gs]
    outs = pl.pallas_call(
        body, name=name,
        in_specs=[_HBM] * (2 * K) + [pl.BlockSpec(memory_space=pl.ANY)],
        out_specs=[_SEM, _SEM] + [_HBM] * (2 * K) + [pl.BlockSpec(memory_space=pltpu.VMEM)],
        out_shape=([pltpu.SemaphoreType.DMA((K,))] * 2 + [pltpu.HBM(g.shape, g.dtype) for g in gs]
                   + [pltpu.HBM(ls, g.dtype) for ls, g in zip(land_shapes, gs)] + [jax.ShapeDtypeStruct((8, 128), F32)]),
        input_output_aliases={i: 2 + i for i in range(2 * K)},
        compiler_params=pltpu.CompilerParams(has_side_effects=_DATAFLOW),
    )(*[pltpu.with_memory_space_constraint(g, pltpu.HBM) for g in gs],
      *[pltpu.with_memory_space_constraint(lax.empty(ls, g.dtype), pltpu.HBM) for ls, g in zip(land_shapes, gs)],
      after)
    return (outs[0], outs[1], outs[2:2 + K], outs[2 + K:2 + 2 * K]), outs[-1]


def sibling_wait(handle, after, name):
    send, recv, gs, lands = handle
    K = len(gs)

    def body(*refs):
        ins, land_refs = refs[:K], refs[K:2 * K]
        for cp in _sibling_copies(ins, land_refs, refs[2 * K], refs[2 * K + 1]):
            cp.wait_send()
            cp.wait_recv()

    outs = pl.pallas_call(
        body, name=name,
        in_specs=[_HBM] * (2 * K) + [_SEM, _SEM, pl.BlockSpec(memory_space=pl.ANY)],
        out_specs=[_HBM] * (2 * K),
        out_shape=[pltpu.HBM(g.shape, g.dtype) for g in gs] + [pltpu.HBM(l.shape, l.dtype) for l in lands],
        input_output_aliases={i: i for i in range(2 * K)},
        compiler_params=pltpu.CompilerParams(has_side_effects=_DATAFLOW),
    )(*gs, *lands, send, recv, after)
    return outs[:K], outs[K:]


def _small_copies(x, land, send, recv):
    mx, my, mc = _my_place()
    me = 4 * mx + 2 * my + mc
    out = []
    for k in range(1, N_DEV):
        peer = (1 - mx if k & 4 else mx, 1 - my if k & 2 else my, 1 - mc if k & 1 else mc)
        slot = 4 * peer[0] + 2 * peer[1] + peer[2]
        out.append(tuple(pltpu.make_async_remote_copy(
            src_ref=x, dst_ref=land.at[s], send_sem=send.at[k - 1], recv_sem=recv.at[k - 1],
            device_id=peer, device_id_type=MESH) for s in (me, slot)))
    return out


def small_start(x, after, name):
    def body(x_ref, land_ref, after_ref, send, recv, x_thru, land_thru, token):
        for mine, _ in _small_copies(x_ref, land_ref, send, recv):
            mine.start()
        token[...] = jnp.zeros_like(token)

    land_shape = (N_DEV,) + x.shape
    outs = pl.pallas_call(
        body, name=name,
        in_specs=[_HBM, _HBM, pl.BlockSpec(memory_space=pl.ANY)],
        out_specs=[_SEM, _SEM, _HBM, _HBM, pl.BlockSpec(memory_space=pltpu.VMEM)],
        out_shape=[pltpu.SemaphoreType.DMA((N_DEV - 1,))] * 2 + [pltpu.HBM(x.shape, x.dtype), pltpu.HBM(land_shape, x.dtype),
                                                                 jax.ShapeDtypeStruct((8, 128), F32)],
        input_output_aliases={0: 2, 1: 3},
        compiler_params=pltpu.CompilerParams(has_side_effects=_DATAFLOW),
    )(pltpu.with_memory_space_constraint(x, pltpu.HBM),
      pltpu.with_memory_space_constraint(lax.empty(land_shape, x.dtype), pltpu.HBM), after)
    return outs[:4], outs[4]


def small_wait(handle, after, name):
    send, recv, x, land = handle

    def body(x_ref, land_ref, send_ref, recv_ref, after_ref, x_out, land_out):
        for mine, theirs in _small_copies(x_ref, land_ref, send_ref, recv_ref):
            mine.wait_send()
            theirs.wait_recv()

    return pl.pallas_call(
        body, name=name,
        in_specs=[_HBM, _HBM, _SEM, _SEM, pl.BlockSpec(memory_space=pl.ANY)],
        out_specs=[_HBM, _HBM],
        out_shape=[pltpu.HBM(x.shape, x.dtype), pltpu.HBM(land.shape, land.dtype)],
        input_output_aliases={0: 0, 1: 1},
        compiler_params=pltpu.CompilerParams(has_side_effects=_DATAFLOW),
    )(x, land, send, recv, after)


def _scatter_copies(ps, lands, send, recv):
    mx, my, mc = _my_place()
    cps = []
    for j, (cx, cy) in enumerate(_other_chips(mx, my)):
        for k in range(len(ps)):
            cps.append(pltpu.make_async_remote_copy(
                src_ref=ps[k].at[:, 2 * cx + cy], dst_ref=lands[k].at[j],
                send_sem=send.at[k * 3 + j], recv_sem=recv.at[k * 3 + j],
                device_id=(cx, cy, mc), device_id_type=MESH))
    return cps


def scatter_start(ps, after, name):
    K = len(ps)

    def body(*refs):
        ins, lands = refs[:K], refs[K:2 * K]
        send, recv = refs[2 * K + 1], refs[2 * K + 2]
        for cp in _scatter_copies(ins, lands, send, recv):
            cp.start()
        refs[-1][...] = jnp.zeros_like(refs[-1])

    land_shapes = [(N_CHIP - 1, p.shape[0]) + p.shape[2:] for p in ps]
    outs = pl.pallas_call(
        body, name=name,
        in_specs=[_HBM] * (2 * K) + [pl.BlockSpec(memory_space=pl.ANY)],
        out_specs=[_SEM, _SEM] + [_HBM] * (2 * K) + [pl.BlockSpec(memory_space=pltpu.VMEM)],
        out_shape=([pltpu.SemaphoreType.DMA((3 * K,))] * 2 + [pltpu.HBM(p.shape, p.dtype) for p in ps]
                   + [pltpu.HBM(ls, p.dtype) for ls, p in zip(land_shapes, ps)] + [jax.ShapeDtypeStruct((8, 128), F32)]),
        input_output_aliases={i: 2 + i for i in range(2 * K)},
        compiler_params=pltpu.CompilerParams(has_side_effects=_DATAFLOW),
    )(*[pltpu.with_memory_space_constraint(p, pltpu.HBM) for p in ps],
      *[pltpu.with_memory_space_constraint(lax.empty(ls, p.dtype), pltpu.HBM) for ls, p in zip(land_shapes, ps)],
      after)
    return (outs[0], outs[1], outs[2:2 + K], outs[2 + K:2 + 2 * K]), outs[-1]


def scatter_wait(handle, after, name):
    send, recv, ps, lands = handle
    K = len(ps)
    afters = list(after) if isinstance(after, (list, tuple)) else [after]

    def body(*refs):
        ins, land_refs = refs[:K], refs[K:2 * K]
        send_ref, recv_ref = refs[2 * K], refs[2 * K + 1]
        for cp in _scatter_copies(ins, land_refs, send_ref, recv_ref):
            cp.wait_send()
            cp.wait_recv()

    outs = pl.pallas_call(
        body, name=name,
        in_specs=[_HBM] * (2 * K) + [_SEM, _SEM] + [pl.BlockSpec(memory_space=pl.ANY)] * len(afters),
        out_specs=[_HBM] * (2 * K),
        out_shape=[pltpu.HBM(p.shape, p.dtype) for p in ps] + [pltpu.HBM(l.shape, l.dtype) for l in lands],
        input_output_aliases={i: i for i in range(2 * K)},
        compiler_params=pltpu.CompilerParams(has_side_effects=_DATAFLOW),
    )(*ps, *lands, send, recv, *afters)
    return outs[:K], outs[K:]


def sibling_complete(ss, name):
    K = len(ss)

    def body(*refs):
        ins, outs = refs[:K], refs[K:2 * K]
        send, recv = refs[2 * K:]
        mx, my, mc = _my_place()
        cps = []
        for k in range(K):
            cp = pltpu.make_async_remote_copy(
                src_ref=ins[k].at[:, mc], dst_ref=outs[k].at[:, mc], send_sem=send.at[k], recv_sem=recv.at[k],
                device_id=(mx, my, 1 - mc), device_id_type=MESH)
            cp.start()
            cps.append(cp)
        for k in range(K):
            pltpu.make_async_remote_copy(
                src_ref=ins[k].at[:, mc], dst_ref=outs[k].at[:, 1 - mc], send_sem=send.at[k], recv_sem=recv.at[k],
                device_id=(mx, my, 1 - mc), device_id_type=MESH).wait_recv()
        for cp in cps:
            cp.wait_send()

    hbm = pl.BlockSpec(memory_space=pl.ANY)
    return pl.pallas_call(
        body, name=name,
        in_specs=[hbm] * K, out_specs=[hbm] * K,
        out_shape=[jax.ShapeDtypeStruct(s.shape, s.dtype) for s in ss],
        scratch_shapes=[pltpu.SemaphoreType.DMA((K,)), pltpu.SemaphoreType.DMA((K,))],
        input_output_aliases={k: k for k in range(K)},
    )(*ss)


def _rope_tables(T):
    inv = ROPE_THETA ** (-jnp.arange(0, ATT_DH, 2, dtype=F32) / ATT_DH)
    ang = jnp.arange(T, dtype=F32)[:, None] * inv[None, :]
    ang = jnp.concatenate([ang, ang, ang, ang], axis=-1)
    return jnp.cos(ang), jnp.sin(ang)


def _ffn_fwd(h, y, mod, i0, get_up, get_down, norm_next, tag):
    wgu = get_up(h)
    a, b, s = ffn_up(y, (wgu, (0,)), (wgu, (1,)), f"ffn_up_{tag}")
    wd = get_down(s)
    outs = resid_matmul([s], (wd, (0,)), h, mod, i0 + 2, 0.5, f"ffn_down_{tag}", norm_next)
    hn, o = outs[0], outs[1]
    return hn, (outs[2] if norm_next else None), (h, y, a, b, s, o), ((wgu, (0,)), (wgu, (1,)), (wd, (0,)))


def _ffn_bwd(dh, do, res, ng, i_n, mod, i0, wgT, wuT, wd, on_grads, next_gate, tag):
    h, y, a, b, s, o = res
    F = _wrows(wgT)
    da, db = ffn_bwd_mid(do, wd, a, b, f"ffn_bwd_mid_{tag}")
    gbuf = lax.empty((3, F, h.shape[1]), BF16)
    gbuf = matmul_tn(da, y, gbuf, 0, 0, f"dwg_{tag}")
    gbuf = matmul_tn(db, y, gbuf, 1, 0, f"dwu_{tag}")
    gbuf = matmul_tn(s, do, gbuf, 2, 0, f"dwd_{tag}")
    token, then = on_grads([gbuf])
    outs = dy_normbwd([(da, 0, wgT, 0, F), (db, 0, wuT, 0, F)], h, dh, ng, i_n, mod + token, i0 + 1,
                      f"ffn_bwd_dy_{tag}", next_gate)
    return outs, then


def _mixer_fwd(h, y, mod, w_inT, w_out, sgu, cos, sin, norm_next, tag):
    lng, lnb, sw, swt, bcol = sgu
    proj = matmul_nt(y, w_inT, f"proj_{tag}")
    out_a = sgu_fwd(proj, lng, lnb, sw, bcol, f"sgu_fwd_{tag}")
    qkv = rope_fwd(proj, cos, sin, f"rope_fwd_{tag}")
    npat = len(DILATIONS)
    qkv_res = [tuple(qkv[3 * p:3 * p + 3]) for p in range(npat)]
    os_, lses = [], []
    for d, (qd, kd, vd) in zip(DILATIONS, qkv_res):
        o_d, lse_d = attn_fwd(qd, kd, vd, f"attn_fwd_d{d}_{tag}")
        os_.append(o_d)
        lses.append(lse_d)
    comb = attn_combine(os_, lses, f"attn_combine_{tag}")
    out_b, o_res, lse_res = comb[0], comb[1:1 + npat], comb[1 + npat:]
    outs = resid_matmul([out_a, out_b], w_out, h, mod, 5, 1.0, f"mix_out_{tag}", norm_next)
    hn, om = outs[0], outs[1]
    return hn, (outs[2] if norm_next else None), (h, y, proj, out_a, out_b, o_res, lse_res, qkv_res, om)


def _mixer_bwd(dh, dom, res, ng, mod, w_inT, w_out, sgu, cos, sin, on_grads, next_gate, tag):
    lng, lnb, sw, swt, bcol = sgu
    h, y, proj, out_a, out_b, o_res, lse_res, qkv_res, om = res
    D = h.shape[1]
    dmixed = matmul_nt(dom, w_out, f"dmixed_{tag}")
    woutbuf = lax.empty((1, 2 * MIX_HALF, D), BF16)
    woutbuf = matmul_tn(out_a, dom, woutbuf, 0, 0, f"dwout_a_{tag}", tmo_cap=MIX_HALF)
    woutbuf = matmul_tn(out_b, dom, woutbuf, 0, MIX_HALF, f"dwout_b_{tag}", tmo_cap=MIX_HALF)
    d_uv, d_sw, d_svec = sgu_bwd(proj, dmixed, lng, lnb, sw, swt, bcol, f"sgu_bwd_{tag}")
    do_res = to_residues(dmixed, 1, f"dout_res_{tag}")
    dqs, dks, dvs = [], [], []
    for p, (d, (qd, kd, vd)) in enumerate(zip(DILATIONS, qkv_res)):
        dq, dk, dv = attn_bwd(qd, kd, vd, do_res[p], o_res[p], lse_res[p], f"attn_bwd_d{d}_{tag}")
        dqs.append(dq)
        dks.append(dk)
        dvs.append(dv)
    d_qkv = rope_bwd(dqs, dks, dvs, cos, sin, f"rope_bwd_{tag}")
    winbuf = lax.empty((1, 5 * MIX_HALF, D), BF16)
    winbuf = matmul_tn(d_uv, y, winbuf, 0, 0, f"dwin_uv_{tag}", tmo_cap=MIX_HALF)
    winbuf = matmul_tn(d_qkv, y, winbuf, 0, 2 * MIX_HALF, f"dwin_qkv_{tag}", tmo_cap=MIX_HALF)
    token, then = on_grads([winbuf, woutbuf])
    pairs = [(d_uv, 0, w_inT, 0, 2 * MIX_HALF), (d_qkv, 0, w_inT, 1, 2 * MIX_HALF), (d_qkv, 2, w_inT, 4, MIX_HALF)]
    outs = dy_normbwd(pairs, h, dh, ng, 1, mod + token, 4, f"mix_bwd_dy_{tag}", next_gate)
    return outs, d_sw, d_svec, then


def _local_step(x, tgt, mods, ngs, get_w, sgus, gf, on_block_grads, on_layer_small):
    T, D = x.shape
    cos, sin = _rope_tables(T)
    h = x
    saved, weights = [], []
    for l in range(2):
        def getter(blk, l=l):
            return lambda after: get_w(l, blk, after)

        if l == 0:
            y = normmod_fwd(h, ngs[0], 0, mods[0], 0, 1, "normmod_l0f1")
        h, y, r1, wf1 = _ffn_fwd(h, y, mods[l], 0, getter("f1u"), getter("f1d"), (ngs[l], 1, mods[l], 3, 4), f"l{l}f1")
        w_inT, w_out = get_w(l, "mx", h)
        h, y, r2 = _mixer_fwd(h, y, mods[l], (w_inT, (0,)), (w_out, (0,)), sgus[l], cos, sin,
                              (ngs[l], 2, mods[l], 6, 7), f"l{l}mx")
        h, y, r3, wf2 = _ffn_fwd(h, y, mods[l], 6, getter("f2u"), getter("f2d"),
                                 (ngs[l + 1], 0, mods[l + 1], 0, 1) if l + 1 < 2 else None, f"l{l}f2")
        saved.append((r1, r2, r3))
        weights.append((wf1, w_inT, w_out, wf2))
    def gate_of(l, blk):
        r1, r2, r3 = saved[l]
        o, i_g, coef = {"f2": (r3[5], 8, 0.5), "mx": (r2[-1], 5, 1.0), "f1": (r1[5], 2, 0.5)}[blk]
        return o, mods[l], i_g, coef

    seq = [(l, blk) for l in (1, 0) for blk in ("f2", "mx", "f1")]
    dh, red_final, do, red_g = final_loss_bwd(h, gf, tgt, gate_of(*seq[0]), "final_loss_bwd")
    rn, rg = {}, {}
    for idx, (l, blk) in enumerate(seq):
        r1, r2, r3 = saved[l]
        wf1, w_inT, w_out, wf2 = weights[l]
        nxt = gate_of(*seq[idx + 1]) if idx + 1 < len(seq) else None
        rg[blk] = red_g
        tag = f"l{l}{blk}"

        def on(arrays, l=l, blk=blk):
            return on_block_grads(l, blk, arrays)

        if blk == "f2":
            outs, then = _ffn_bwd(dh, do, r3, ngs[l], 2, mods[l], 6, *wf2, on, nxt, tag)
        elif blk == "mx":
            outs, d_sw, d_svec, then = _mixer_bwd(dh, do, r2, ngs[l], mods[l], (w_inT, (0,)), (w_out, (0,)), sgus[l],
                                                  cos, sin, on, nxt, tag)
        else:
            outs, then = _ffn_bwd(dh, do, r1, ngs[l], 0, mods[l], 0, *wf1, on, nxt, tag)
        dh, rn[blk] = outs[0], outs[1]
        if nxt is not None:
            do, red_g = outs[2], outs[3]
        if blk == "f1":
            mods = mods + on_layer_small(l, dict(sgu_w=d_sw, sgu_vec=d_svec, red_n=(rn["f1"], rn["mx"], rn["f2"]),
                                                 red_g=(rg["f1"], rg["mx"], rg["f2"])),
                                         red_final if l == 0 else None)
            mods = mods + then(mods)
        else:
            mods = mods + then(dh)
    return dh


def _adam_out(w, g, m, v, name):
    shp = w.shape
    two_d = (-1, shp[-1])
    d, mn, vn = adamw(w.reshape(two_d), g.reshape(two_d), m.reshape(two_d), v.reshape(two_d), name)
    return g, d.reshape(shp), mn.reshape(shp), vn.reshape(shp)


def kernel(x, c, ada_w, ada_b, norm_g, ffn1_wg, ffn1_wu, ffn1_wd, ffn2_wg, ffn2_wu, ffn2_wd, w_in, sgu_ln_g, sgu_ln_b, sgu_w, sgu_b, w_out, final_g, loss_target, m_ada_w, m_ada_b, m_norm_g, m_ffn1_wg, m_ffn1_wu, m_ffn1_wd, m_ffn2_wg, m_ffn2_wu, m_ffn2_wd, m_w_in, m_sgu_ln_g, m_sgu_ln_b, m_sgu_w, m_sgu_b, m_w_out, m_final_g, v_ada_w, v_ada_b, v_norm_g, v_ffn1_wg, v_ffn1_wu, v_ffn1_wd, v_ffn2_wg, v_ffn2_wu, v_ffn2_wd, v_w_in, v_sgu_ln_g, v_sgu_ln_b, v_sgu_w, v_sgu_b, v_w_out, v_final_g):
    T, D = x.shape[1], x.shape[2]
    NL = ada_w.shape[0]
    mx, my, mc = _my_place()
    me = 4 * mx + 2 * my + mc
    ci = 2 * mx + my
    c_idx = jnp.reshape(mc, (1,)).astype(jnp.int32)
    place = jnp.stack([ci, mc]).astype(jnp.int32)

    ngw = norm_g.shape[2]
    small_in = jnp.concatenate([jnp.pad(c, ((0, 7), (0, 0))),
                                jnp.pad(norm_g.reshape(NL * 3, ngw), ((0, 8 - NL * 3), (0, D - ngw)))], axis=0)
    small_all, _ = gather_small(small_in, place, "gather_c_normg")
    c_all = small_all[:, 0, :]
    ng_parts = small_all[0::2, 8:8 + NL * 3, :ngw]
    ngs = jnp.transpose(ng_parts, (1, 0, 2)).reshape(NL, 3, N_CHIP * ngw)

    nmod = ada_w.shape[2]
    ada_b_mine = lax.dynamic_slice_in_dim(ada_b, ci * nmod, nmod, axis=1).reshape(NL, 1, nmod)
    mod_part = ada_fwd(c_all, ada_w, ada_b_mine, "ada_fwd")
    mod_all, _ = gather_small(mod_part.reshape(NL * N_DEV, nmod), place, "gather_mod")
    mod_rows = lax.dynamic_index_in_dim(mod_all.reshape(N_DEV, NL, N_DEV, nmod), me, axis=2, keepdims=False)
    mods = jnp.transpose(mod_rows[0::2], (1, 0, 2)).reshape(NL, N_ADA, D)

    sgus = []
    for l in range(NL):
        sgus.append((sgu_ln_g[l].reshape(1, MIX_HALF), sgu_ln_b[l].reshape(1, MIX_HALF), sgu_w[l],
                     jnp.swapaxes(sgu_w[l], 1, 2), jnp.transpose(sgu_b[l])))

    def halves(a):
        n, r, _ = a.shape
        return a.reshape(n, 2, r // 2, D)

    first_group = [halves(jnp.stack([ffn1_wg[0].T, ffn1_wu[0].T], axis=0).astype(BF16))]
    first_handles, first_token = gather_start([first_group], mods, "gather_start_first")
    zero = first_token[0, 0]
    mods = mods + zero

    def prep(a):
        return (a + zero).astype(BF16)

    groups = []
    for l in range(NL):
        groups += [[halves(jnp.stack([prep(ffn1_wg[l].T), prep(ffn1_wu[l].T)], axis=0))],
                   [halves(prep(ffn1_wd[l])[None])],
                   [halves(prep(w_in[l].T)[None]), halves(prep(w_out[l])[None])],
                   [halves(jnp.stack([prep(ffn2_wg[l].T), prep(ffn2_wu[l].T)], axis=0))],
                   [halves(prep(ffn2_wd[l])[None])]]
    handles, token = gather_start(groups[1:], mods, "gather_start")
    handles = first_handles + handles
    mods = mods + token[0, 0]
    group_no = {"f1u": 0, "f1d": 1, "mx": 2, "f2u": 3, "f2d": 4}

    def get_w(l, key, after):
        full = gather_wait(handles[len(group_no) * l + group_no[key]], after, f"gather_wait_l{l}{key}")
        full = [a.reshape(a.shape[0], N_CHIP * 2 * a.shape[3], D) for a in full]
        return full[0] if key != "mx" else tuple(full)

    def split(a):
        n, r4, _ = a.shape
        return a.reshape(n, N_CHIP, 2, r4 // N_CHIP // 2, D)

    pending, small_pending, small_tokens = {}, {}, {}

    def on_block_grads(l, blk, bufs):
        tag = f"l{l}{blk}"
        sib, tok1 = sibling_start([split(g) for g in bufs], place, f"rs_sibling_start_{tag}")

        def then(after):
            parts, lands = sibling_wait(sib, after, f"rs_sibling_wait_{tag}")
            psums = [sum_halves(g, ld, c_idx, f"rs_sum_halves_{tag}_{i}") for i, (g, ld) in enumerate(zip(parts, lands))]
            pending[(l, blk)], tok2 = scatter_start(psums, lands[0], f"rs_chips_start_{tag}")
            return tok2[0, 0]

        return tok1[0, 0], then

    def blocks_finish(blocks, after, tag):
        ssums, counts = [], []
        for l, blk in blocks:
            psums, lands2 = scatter_wait(pending.pop((l, blk)), after, f"rs_chips_wait_l{l}{blk}")
            ssums += [sum_chips(p, ld, place, f"rs_sum_chips_l{l}{blk}_{i}") for i, (p, ld) in enumerate(zip(psums, lands2))]
            counts.append(len(psums))
        fins = [f.reshape(f.shape[0], -1, D) for f in sibling_complete(ssums, f"rs_complete_{tag}")]
        out, i = [], 0
        for n in counts:
            out.append(fins[i:i + n])
            i += n
        return out

    def on_layer_small(l, grads, red_final):
        blocks = list(grads["red_n"]) + list(grads["red_g"])
        blocks.append(jnp.pad(grads["sgu_vec"], ((0, 0), (0, D - MIX_HALF))))
        blocks.append(grads["sgu_w"].reshape(-1, D))
        if red_final is not None:
            blocks.append(red_final)
        xs = jnp.concatenate(blocks, axis=0)
        small_pending[l], small_tokens[l] = small_start(xs, place, f"small_start_l{l}")
        return small_tokens[l][0, 0]

    grad_x = _local_step(x[0], loss_target[0], mods, ngs, get_w, sgus, final_g.reshape(1, D),
                         on_block_grads, on_layer_small)

    adam_state = {}

    def adam_big(nm, l, g, w, m, v):
        adam_state[nm] = adamw_layer(w, g, m, v, l, adam_state.get(nm), f"adamw_{nm}_l{l}")

    def adam_block(l, blk, fin):
        if blk == "mx":
            adam_big("w_in", l, fin[0][0].T, w_in, m_w_in, v_w_in)
            adam_big("w_out", l, fin[1][0], w_out, m_w_out, v_w_out)
        else:
            ws = ((ffn1_wg, m_ffn1_wg, v_ffn1_wg), (ffn1_wu, m_ffn1_wu, v_ffn1_wu), (ffn1_wd, m_ffn1_wd, v_ffn1_wd)) \
                if blk == "f1" else \
                ((ffn2_wg, m_ffn2_wg, v_ffn2_wg), (ffn2_wu, m_ffn2_wu, v_ffn2_wu), (ffn2_wd, m_ffn2_wd, v_ffn2_wd))
            pre = "ffn1" if blk == "f1" else "ffn2"
            for k, (nm, tr) in enumerate((("wg", True), ("wu", True), ("wd", False))):
                adam_big(f"{pre}_{nm}", l, fin[0][k].T if tr else fin[0][k], *ws[k])

    done_order = [(l, blk) for l in range(NL - 1, -1, -1) for blk in ("f2", "mx", "f1")]
    for (l, blk), fin in zip(done_order[:-1], blocks_finish(done_order[:-1], small_tokens[0], "early")):
        adam_block(l, blk, fin)
    last_big = adam_state["w_out"][1]

    small_sum, small_all = [], []
    for l in range(NL):
        xs, land = small_wait(small_pending[l], last_big, f"small_wait_l{l}")
        full = lax.dynamic_update_slice(land, xs[None], (me, 0, 0))
        small_all.append(full)
        small_sum.append(sum_slots(full, f"small_sum_l{l}"))
    offs = [8 * i for i in range(8)]
    off_final = offs[7] + SGU_HEADS * ATT_BLOCK * HEAD_LANES // D
    loss = small_sum[0][off_final + 1, 0]
    g_final_g = small_sum[0][off_final, :]
    g_norm_g, g_ada_b, g_lng, g_lnb, g_sb, g_sw, dmod_all = [], [], [], [], [], [], []
    for l in range(NL):
        rn = [small_sum[l][offs[i]:offs[i] + 8] for i in range(3)]
        rg = [small_sum[l][offs[3 + i]:offs[3 + i] + 8] for i in range(3)]
        g_norm_g.append(jnp.stack([rn[i][2] for i in range(3)], axis=0))
        g_ada_b.append(jnp.concatenate([jnp.stack([rn[i][0], rn[i][1], rg[i][0]], axis=0) for i in range(3)],
                                       axis=0).reshape(N_ADA * D))
        sv = small_sum[l][offs[6]:offs[6] + 8, :MIX_HALF]
        g_lng.append(sv[0].reshape(SGU_HEADS, HEAD_LANES))
        g_lnb.append(sv[1].reshape(SGU_HEADS, HEAD_LANES))
        g_sb.append(sv[2].reshape(SGU_HEADS, ATT_BLOCK))
        g_sw.append(small_sum[l][offs[7]:off_final].reshape(sgu_w.shape[1:]))
        rows = []
        for i in range(3):
            an = small_all[l][:, offs[i]:offs[i] + 2]
            ag = small_all[l][:, offs[3 + i]:offs[3 + i] + 1]
            rows += [an[:, 0], an[:, 1], ag[:, 0]]
        dmod_all.append(jnp.stack(rows, axis=1).reshape(N_DEV, N_ADA * D))
    dmod_all = jnp.stack(dmod_all, axis=0)
    dmod_mine = lax.dynamic_slice_in_dim(dmod_all, ci * nmod, nmod, axis=2)
    g_ada_w = ada_bwd(jnp.transpose(c_all), dmod_mine, "ada_bwd")
    g_ada_b = jnp.stack(g_ada_b, axis=0)
    g_norm_g_full = jnp.stack(g_norm_g, axis=0)
    g_norm_g_mine = lax.dynamic_slice_in_dim(g_norm_g_full, ci * ngw, ngw, axis=2)

    small_params = [
        ("ada_w", ada_w, g_ada_w, m_ada_w, v_ada_w),
        ("ada_b", ada_b, g_ada_b, m_ada_b, v_ada_b),
        ("norm_g", norm_g, g_norm_g_mine, m_norm_g, v_norm_g),
        ("sgu_ln_g", sgu_ln_g, jnp.stack(g_lng, axis=0), m_sgu_ln_g, v_sgu_ln_g),
        ("sgu_ln_b", sgu_ln_b, jnp.stack(g_lnb, axis=0), m_sgu_ln_b, v_sgu_ln_b),
        ("sgu_w", sgu_w, jnp.stack(g_sw, axis=0), m_sgu_w, v_sgu_w),
        ("sgu_b", sgu_b, jnp.stack(g_sb, axis=0), m_sgu_b, v_sgu_b),
        ("final_g", final_g.reshape(1, D), g_final_g.reshape(1, D), m_final_g.reshape(1, D), v_final_g.reshape(1, D)),
    ]
    for nm, w, g, m, v in small_params:
        res = _adam_out(w, g, m, v, f"adamw_{nm}")
        adam_state[nm] = tuple(t.reshape(D) for t in res) if nm == "final_g" else res

    l, blk = done_order[-1]
    adam_block(l, blk, blocks_finish([(l, blk)], [st[1] for st in adam_state.values()], "last")[0])

    names = ["ada_w", "ada_b", "norm_g", "ffn1_wg", "ffn1_wu", "ffn1_wd", "ffn2_wg", "ffn2_wu", "ffn2_wd", "w_in",
             "sgu_ln_g", "sgu_ln_b", "sgu_w", "sgu_b", "w_out", "final_g"]
    shapes = [t.shape for t in (ada_w, ada_b, norm_g, ffn1_wg, ffn1_wu, ffn1_wd, ffn2_wg, ffn2_wu, ffn2_wd, w_in,
                                sgu_ln_g, sgu_ln_b, sgu_w, sgu_b, w_out, final_g)]
    return (loss, grad_x[None], *[adam_state[nm][i].reshape(s) for i in range(4) for nm, s in zip(names, shapes)])
```

```python
import math

import jax
import jax.numpy as jnp
from jax import lax
from jax.experimental import pallas as pl
from jax.experimental.pallas import tpu as pltpu

F32 = jnp.float32
BF16 = jnp.bfloat16
EPS = 1e-6
SGU_HEADS = 4
HEAD_LANES = 128
ATT_DH = 64
ATT_BLOCK = 128
MIX_HALF = SGU_HEADS * HEAD_LANES
DILATIONS = (1, 4, 16)
ROPE_THETA = 10000.0
N_ADA = 9
ADAM_LR, ADAM_B1, ADAM_B2, ADAM_EPS, ADAM_WD, ADAM_STEP = 0.001, 0.9, 0.999, 1e-08, 0.01, 10
NEG = -1e30
V7X_VMEM_BYTES = 64 * 1024 * 1024
VMEM_LIMIT = V7X_VMEM_BYTES * 7 // 8
MESH = pl.DeviceIdType.MESH
N_DEV = 8
N_CHIP = 4


def _tile(n, cap, mult):
    if n <= cap:
        return n
    t = (cap // mult) * mult
    while t >= mult:
        if n % t == 0:
            return t
        t -= mult
    raise ValueError((n, cap, mult))


def _params(dims=None):
    return pltpu.CompilerParams(dimension_semantics=dims, vmem_limit_bytes=VMEM_LIMIT)


def _wspec(w, rows, idx, resident=False):
    arr, lead = w
    kw = dict(pipeline_mode=pl.Buffered(1)) if resident else {}
    return pl.BlockSpec((None,) * len(lead) + (rows, arr.shape[-1]), lambda *g: tuple(lead) + (idx(*g), 0), **kw)


def _wrows(w):
    return w[0].shape[-2]


def _nt(a, b):
    return lax.dot_general(a, b, (((1,), (1,)), ((), ())), preferred_element_type=F32)


def _tn(a, b):
    return lax.dot_general(a, b, (((0,), (0,)), ((), ())), preferred_element_type=F32)


def _nn(a, b):
    return jnp.dot(a, b, preferred_element_type=F32)


def _sigmoid(x):
    return 0.5 * jnp.tanh(0.5 * x) + 0.5


_GELU_K = math.sqrt(2.0 / math.pi)
_GELU_C = 0.044715


def _gelu(x):
    t = jnp.tanh(_GELU_K * (x + _GELU_C * x * x * x))
    return 0.5 * x * (1.0 + t)


def _gelu_and_grad(x):
    x2 = x * x
    t = jnp.tanh(_GELU_K * (x + _GELU_C * x * x2))
    g = 0.5 * x * (1.0 + t)
    dg = 0.5 * (1.0 + t) + 0.5 * x * (1.0 - t * t) * (_GELU_K * (1.0 + 3.0 * _GELU_C * x2))
    return g, dg


def normmod_fwd(h, ng, i_n, mod, i_sh, i_sc, name):
    T, D = h.shape
    tm = _tile(T, 512, 8)

    def body(h_ref, ng_ref, mod_ref, y_ref):
        y_ref[...] = _normmod(h_ref[...], ng_ref[i_n:i_n + 1, :], mod_ref[i_sh:i_sh + 1, :],
                              mod_ref[i_sc:i_sc + 1, :]).astype(BF16)

    return pl.pallas_call(
        body, name=name, grid=(T // tm,),
        in_specs=[pl.BlockSpec((tm, D), lambda i: (i, 0)),
                  pl.BlockSpec(ng.shape, lambda i: (0, 0)),
                  pl.BlockSpec(mod.shape, lambda i: (0, 0))],
        out_specs=pl.BlockSpec((tm, D), lambda i: (i, 0)),
        out_shape=jax.ShapeDtypeStruct((T, D), BF16),
        compiler_params=_params(("parallel",)),
    )(h, ng, mod)


def ffn_up(y, wgT, wuT, name):
    T, D = y.shape
    F = _wrows(wgT)
    tm = _tile(T, 512, 16)
    tf = _tile(F, 2816, 256)
    cuts = list(range(0, tf, 768)) + [tf]

    def body(y_ref, wg_ref, wu_ref, a_ref, b_ref, s_ref):
        yv = y_ref[...]
        for c0, c1 in zip(cuts[:-1], cuts[1:]):
            a = _nt(yv, wg_ref[c0:c1, :])
            b = _nt(yv, wu_ref[c0:c1, :])
            a_ref[:, c0:c1] = a.astype(BF16)
            b_ref[:, c0:c1] = b.astype(BF16)
            s_ref[:, c0:c1] = (a * _sigmoid(a) * b).astype(BF16)

    act = jax.ShapeDtypeStruct((T, F), BF16)
    return pl.pallas_call(
        body, name=name, grid=(F // tf, T // tm),
        in_specs=[pl.BlockSpec((tm, D), lambda j, i: (i, 0)),
                  _wspec(wgT, tf, lambda j, i: j, resident=True),
                  _wspec(wuT, tf, lambda j, i: j, resident=True)],
        out_specs=[pl.BlockSpec((tm, tf), lambda j, i: (i, j))] * 3,
        out_shape=[act, act, act],
        compiler_params=_params(("parallel", "parallel")),
    )(y, wgT[0], wuT[0])


def _normmod(x, gn, sh, sc):
    r = lax.rsqrt(jnp.mean(x * x, axis=-1, keepdims=True) + EPS)
    return ((x * r) * gn) * (1.0 + sc) + sh


def resid_matmul(xs, w, h, mod, i_g, coef, name, norm_next=None):
    T, D = h.shape
    kb = xs[0].shape[1]
    assert all(x.shape == (T, kb) for x in xs) and _wrows(w) == kb * len(xs)
    tm = _tile(T, 1024, 16)
    nx = len(xs)
    n_in, n_out, n_shape, n_ops = [], [], [], []
    if norm_next:
        ng_n, i_n, mod_n, i_sh, i_sc = norm_next
        n_in = [pl.BlockSpec(ng_n.shape, lambda i: (0, 0)), pl.BlockSpec(mod_n.shape, lambda i: (0, 0))]
        n_out = [pl.BlockSpec((tm, D), lambda i: (i, 0))]
        n_shape = [jax.ShapeDtypeStruct((T, D), BF16)]
        n_ops = [ng_n, mod_n]

    def body(*refs):
        x_refs, w_refs = refs[:nx], refs[nx:2 * nx]
        h_ref, mod_ref = refs[2 * nx:2 * nx + 2]
        hn_ref, o_ref = refs[2 * nx + 2 + len(n_in):2 * nx + 4 + len(n_in)]
        o = _nn(x_refs[0][...], w_refs[0][...])
        for xr, wr in zip(x_refs[1:], w_refs[1:]):
            o = o + _nn(xr[...], wr[...])
        o_ref[...] = o.astype(BF16)
        hn = h_ref[...] + (coef * mod_ref[i_g:i_g + 1, :]) * o
        hn_ref[...] = hn
        if norm_next:
            ng_ref, modn_ref = refs[2 * nx + 2], refs[2 * nx + 3]
            refs[-1][...] = _normmod(hn, ng_ref[i_n:i_n + 1, :], modn_ref[i_sh:i_sh + 1, :],
                                     modn_ref[i_sc:i_sc + 1, :]).astype(BF16)

    return pl.pallas_call(
        body, name=name, grid=(T // tm,),
        in_specs=([pl.BlockSpec((tm, kb), lambda i: (i, 0))] * nx
                  + [_wspec(w, kb, lambda i, p=p: p, resident=True) for p in range(nx)]
                  + [pl.BlockSpec((tm, D), lambda i: (i, 0)),
                     pl.BlockSpec(mod.shape, lambda i: (0, 0))] + n_in),
        out_specs=[pl.BlockSpec((tm, D), lambda i: (i, 0))] * 2 + n_out,
        out_shape=[jax.ShapeDtypeStruct((T, D), F32), jax.ShapeDtypeStruct((T, D), BF16)] + n_shape,
        compiler_params=_params(("parallel",)),
    )(*xs, *([w[0]] * nx), h, mod, *n_ops)


def _gate_specs(gate, tm, D):
    o, mod, _, _ = gate
    T = o.shape[0]
    return ([pl.BlockSpec((tm, D), lambda i: (i, 0)), pl.BlockSpec(mod.shape, lambda i: (0, 0))],
            [pl.BlockSpec((tm, D), lambda i: (i, 0)), pl.BlockSpec((8, D), lambda i: (0, 0))],
            [jax.ShapeDtypeStruct((T, D), BF16), jax.ShapeDtypeStruct((8, D), F32)],
            [o, mod])


def _gate_emit(d, gate, o_ref, mod_ref, do_ref, red_ref):
    _, _, i_g, coef = gate
    do_ref[...] = (d * (coef * mod_ref[i_g:i_g + 1, :])).astype(BF16)

    @pl.when(pl.program_id(0) == 0)
    def _():
        red_ref[...] = jnp.zeros_like(red_ref)

    red_ref[0:1, :] += coef * jnp.sum(d * o_ref[...].astype(F32), axis=0, keepdims=True)


def ffn_bwd_mid(do, wd, a, b, name):
    T, D = do.shape
    F = _wrows(wd)
    tm = _tile(T, 512, 16)
    tf = _tile(F, 2816, 256)
    cuts = list(range(0, tf, 256)) + [tf]

    def body(do_ref, wd_ref, a_ref, b_ref, da_ref, db_ref):
        dov = do_ref[...]
        for c0, c1 in zip(cuts[:-1], cuts[1:]):
            ds = _nt(dov, wd_ref[c0:c1, :])
            av = a_ref[:, c0:c1].astype(F32)
            bv = b_ref[:, c0:c1].astype(F32)
            sig = _sigmoid(av)
            da_ref[:, c0:c1] = (ds * bv * (sig * (1.0 + av * (1.0 - sig)))).astype(BF16)
            db_ref[:, c0:c1] = (ds * (av * sig)).astype(BF16)

    act = jax.ShapeDtypeStruct((T, F), BF16)
    return pl.pallas_call(
        body, name=name, grid=(F // tf, T // tm),
        in_specs=[pl.BlockSpec((tm, D), lambda j, i: (i, 0)),
                  _wspec(wd, tf, lambda j, i: j, resident=True),
                  pl.BlockSpec((tm, tf), lambda j, i: (i, j)),
                  pl.BlockSpec((tm, tf), lambda j, i: (i, j))],
        out_specs=[pl.BlockSpec((tm, tf), lambda j, i: (i, j))] * 2,
        out_shape=[act, act],
        compiler_params=_params(("parallel", "parallel")),
    )(do, wd[0], a, b)


def dy_normbwd(pairs, h, dhp, ng, i_n, mod, i_sc, name, gate=None):
    T, D = h.shape
    tm = _tile(T, 512, 16)
    npair = len(pairs)
    g_in, g_out, g_shape, g_ops = _gate_specs(gate, tm, D) if gate else ([], [], [], [])

    def body(*refs):
        x_refs, w_refs = refs[:npair], refs[npair:2 * npair]
        h_ref, dhp_ref, ng_ref, mod_ref = refs[2 * npair:2 * npair + 4]
        dh_ref, red_ref = refs[2 * npair + 4 + len(g_in):2 * npair + 6 + len(g_in)]
        dy = _nn(x_refs[0][...], w_refs[0][...])
        for xr, wr in zip(x_refs[1:], w_refs[1:]):
            dy = dy + _nn(xr[...], wr[...])
        x = h_ref[...]
        r = lax.rsqrt(jnp.mean(x * x, axis=-1, keepdims=True) + EPS)
        n = x * r
        gn = ng_ref[i_n:i_n + 1, :]
        dnh = dy * (1.0 + mod_ref[i_sc:i_sc + 1, :])

        @pl.when(pl.program_id(0) == 0)
        def _():
            red_ref[...] = jnp.zeros_like(red_ref)

        red_ref[0:1, :] += jnp.sum(dy, axis=0, keepdims=True)
        red_ref[1:2, :] += jnp.sum(dy * (n * gn), axis=0, keepdims=True)
        red_ref[2:3, :] += jnp.sum(dnh * n, axis=0, keepdims=True)
        dn = dnh * gn
        dh_new = dhp_ref[...] + r * (dn - n * jnp.mean(dn * n, axis=-1, keepdims=True))
        dh_ref[...] = dh_new
        if gate:
            _gate_emit(dh_new, gate, refs[2 * npair + 4], refs[2 * npair + 5], refs[-2], refs[-1])

    in_specs = ([pl.BlockSpec((tm, kb), lambda i, c=c: (i, c)) for (_, c, _, _, kb) in pairs]
                + [_wspec(w, kb, lambda i, r=r: r, resident=True) for (_, _, w, r, kb) in pairs]
                + [pl.BlockSpec((tm, D), lambda i: (i, 0)),
                   pl.BlockSpec((tm, D), lambda i: (i, 0)),
                   pl.BlockSpec(ng.shape, lambda i: (0, 0)),
                   pl.BlockSpec(mod.shape, lambda i: (0, 0))] + g_in)
    return pl.pallas_call(
        body, name=name, grid=(T // tm,), in_specs=in_specs,
        out_specs=[pl.BlockSpec((tm, D), lambda i: (i, 0)), pl.BlockSpec((8, D), lambda i: (0, 0))] + g_out,
        out_shape=[jax.ShapeDtypeStruct((T, D), F32), jax.ShapeDtypeStruct((8, D), F32)] + g_shape,
        compiler_params=_params(("arbitrary",)),
    )(*[p[0] for p in pairs], *[p[2][0] for p in pairs], h, dhp, ng, mod, *g_ops)


def matmul_tn(a, b, buf, slot, row0, name, tmo_cap=1408):
    T, N = b.shape
    ma = a.shape[1]
    tmo = _tile(ma, tmo_cap, 128)
    assert row0 % tmo == 0
    nmo = ma // tmo
    tk = _tile(T, 2048, 16)
    nk = T // tk

    def body(a_ref, b_ref, buf_ref, o_ref, acc_ref):
        k = pl.program_id(1)

        @pl.when(k == 0)
        def _():
            acc_ref[...] = jnp.zeros_like(acc_ref)

        acc_ref[...] += _tn(a_ref[...], b_ref[...])

        @pl.when(k == nk - 1)
        def _():
            o_ref[...] = acc_ref[...].astype(BF16)

    return pl.pallas_call(
        body, name=name, grid=(nmo, nk),
        in_specs=[pl.BlockSpec((tk, tmo), lambda j, k: (k, j)),
                  pl.BlockSpec((tk, N), lambda j, k: (k, 0)),
                  pl.BlockSpec(memory_space=pl.ANY)],
        out_specs=pl.BlockSpec((None, tmo, N), lambda j, k: (slot, row0 // tmo + j, 0)),
        out_shape=jax.ShapeDtypeStruct(buf.shape, BF16),
        scratch_shapes=[pltpu.VMEM((tmo, N), F32)],
        input_output_aliases={2: 0},
        compiler_params=_params(("parallel", "arbitrary")),
    )(a, b, buf)


def matmul_nt(x, w, name):
    T, K = x.shape
    N = _wrows(w)
    tm = _tile(T, 1024, 16)
    tn = _tile(N, 1280, 128)

    def body(x_ref, w_ref, o_ref):
        o_ref[...] = _nt(x_ref[...], w_ref[...]).astype(BF16)

    return pl.pallas_call(
        body, name=name, grid=(N // tn, T // tm),
        in_specs=[pl.BlockSpec((tm, K), lambda j, i: (i, 0)), _wspec(w, tn, lambda j, i: j)],
        out_specs=pl.BlockSpec((tm, tn), lambda j, i: (i, j)),
        out_shape=jax.ShapeDtypeStruct((T, N), BF16),
        compiler_params=_params(("parallel", "parallel")),
    )(x, w[0])


def _sgu_head_fwd(u, v, lng, lnb):
    gu, dgu = _gelu_and_grad(u)
    gv, dgv = _gelu_and_grad(v)
    mu = jnp.mean(gv, axis=-1, keepdims=True)
    xc = gv - mu
    rstd = lax.rsqrt(jnp.mean(xc * xc, axis=-1, keepdims=True) + EPS)
    xhat = xc * rstd
    vn = xhat * lng + lnb
    return gu, dgu, dgv, rstd, xhat, vn


def _tril_mask():
    r = lax.broadcasted_iota(jnp.int32, (ATT_BLOCK, ATT_BLOCK), 0)
    c = lax.broadcasted_iota(jnp.int32, (ATT_BLOCK, ATT_BLOCK), 1)
    return c <= r


def _triu_mask():
    r = lax.broadcasted_iota(jnp.int32, (ATT_BLOCK, ATT_BLOCK), 0)
    c = lax.broadcasted_iota(jnp.int32, (ATT_BLOCK, ATT_BLOCK), 1)
    return r <= c


def sgu_fwd(proj, lng, lnb, w, bcol, name):
    T = proj.shape[0]
    tm = _tile(T, 512, 128)
    nch = tm // ATT_BLOCK

    def body(u_ref, v_ref, lng_ref, lnb_ref, w_ref, b_ref, o_ref):
        tril = _tril_mask()
        for hd in range(SGU_HEADS):
            sl = slice(hd * HEAD_LANES, (hd + 1) * HEAD_LANES)
            u = u_ref[:, sl].astype(F32)
            v = v_ref[:, sl].astype(F32)
            gu, _, _, _, _, vn = _sgu_head_fwd(u, v, lng_ref[:, sl], lnb_ref[:, sl])
            wm = jnp.where(tril, w_ref[hd], 0.0).astype(BF16)
            vnb = vn.astype(BF16)
            bc = b_ref[:, hd:hd + 1]
            for ch in range(nch):
                rs = slice(ch * ATT_BLOCK, (ch + 1) * ATT_BLOCK)
                z = _nn(wm, vnb[rs, :]) + bc
                o_ref[rs, sl] = (gu[rs, :] * z).astype(BF16)

    return pl.pallas_call(
        body, name=name, grid=(T // tm,),
        in_specs=[pl.BlockSpec((tm, MIX_HALF), lambda i: (i, 0)),
                  pl.BlockSpec((tm, MIX_HALF), lambda i: (i, 1)),
                  pl.BlockSpec((1, MIX_HALF), lambda i: (0, 0)),
                  pl.BlockSpec((1, MIX_HALF), lambda i: (0, 0)),
                  pl.BlockSpec(w.shape, lambda i: (0, 0, 0)),
                  pl.BlockSpec(bcol.shape, lambda i: (0, 0))],
        out_specs=pl.BlockSpec((tm, MIX_HALF), lambda i: (i, 0)),
        out_shape=jax.ShapeDtypeStruct((T, MIX_HALF), BF16),
        compiler_params=_params(("parallel",)),
    )(proj, proj, lng, lnb, w, bcol)


def sgu_bwd(proj, dmixed, lng, lnb, w, wt, bcol, name):
    T = proj.shape[0]
    tm = _tile(T, 512, 128)
    nch = tm // ATT_BLOCK
    nsteps = T // tm

    def body(u_ref, v_ref, g_ref, lng_ref, lnb_ref, w_ref, wt_ref, b_ref, duv_ref, dw_ref, dvec_ref, bacc_ref):
        step = pl.program_id(0)

        @pl.when(step == 0)
        def _():
            dw_ref[...] = jnp.zeros_like(dw_ref)
            dvec_ref[...] = jnp.zeros_like(dvec_ref)
            bacc_ref[...] = jnp.zeros_like(bacc_ref)

        tril = _tril_mask()
        triu = _triu_mask()
        for hd in range(SGU_HEADS):
            sl = slice(hd * HEAD_LANES, (hd + 1) * HEAD_LANES)
            u = u_ref[:, sl].astype(F32)
            v = v_ref[:, sl].astype(F32)
            lng_h = lng_ref[:, sl]
            gu, dgu, dgv, rstd, xhat, vn = _sgu_head_fwd(u, v, lng_h, lnb_ref[:, sl])
            wm = jnp.where(tril, w_ref[hd], 0.0).astype(BF16)
            wmt = jnp.where(triu, wt_ref[hd], 0.0).astype(BF16)
            vnb = vn.astype(BF16)
            bc = b_ref[:, hd:hd + 1]
            g = g_ref[:, sl].astype(F32)
            dw_acc = jnp.zeros((ATT_BLOCK, ATT_BLOCK), F32)
            b_acc = jnp.zeros((ATT_BLOCK, HEAD_LANES), F32)
            dvn_parts = []
            for ch in range(nch):
                rs = slice(ch * ATT_BLOCK, (ch + 1) * ATT_BLOCK)
                z = _nn(wm, vnb[rs, :]) + bc
                duv_ref[rs, sl] = (g[rs, :] * z * dgu[rs, :]).astype(BF16)
                dz = g[rs, :] * gu[rs, :]
                dzb = dz.astype(BF16)
                dvn_parts.append(_nn(wmt, dzb))
                dw_acc = dw_acc + _nt(dzb, vnb[rs, :])
                b_acc = b_acc + dz
            dvn = jnp.concatenate(dvn_parts, axis=0)
            dw_ref[hd] += jnp.where(tril, dw_acc, 0.0)
            bacc_ref[hd] += b_acc
            dvec_ref[0:1, sl] += jnp.sum(dvn * xhat, axis=0, keepdims=True)
            dvec_ref[1:2, sl] += jnp.sum(dvn, axis=0, keepdims=True)
            dxh = dvn * lng_h
            dgv_in = rstd * (dxh - jnp.mean(dxh, axis=-1, keepdims=True)
                             - xhat * jnp.mean(dxh * xhat, axis=-1, keepdims=True))
            duv_ref[:, MIX_HALF + hd * HEAD_LANES:MIX_HALF + (hd + 1) * HEAD_LANES] = (dgv_in * dgv).astype(BF16)

        @pl.when(step == nsteps - 1)
        def _():
            for hd in range(SGU_HEADS):
                sl = slice(hd * HEAD_LANES, (hd + 1) * HEAD_LANES)
                dvec_ref[2:3, sl] = jnp.sum(bacc_ref[hd].T, axis=0, keepdims=True)

    return pl.pallas_call(
        body, name=name, grid=(nsteps,),
        in_specs=[pl.BlockSpec((tm, MIX_HALF), lambda i: (i, 0)),
                  pl.BlockSpec((tm, MIX_HALF), lambda i: (i, 1)),
                  pl.BlockSpec((tm, MIX_HALF), lambda i: (i, 0)),
                  pl.BlockSpec((1, MIX_HALF), lambda i: (0, 0)),
                  pl.BlockSpec((1, MIX_HALF), lambda i: (0, 0)),
                  pl.BlockSpec(w.shape, lambda i: (0, 0, 0)),
                  pl.BlockSpec(w.shape, lambda i: (0, 0, 0)),
                  pl.BlockSpec(bcol.shape, lambda i: (0, 0))],
        out_specs=[pl.BlockSpec((tm, 2 * MIX_HALF), lambda i: (i, 0)),
                   pl.BlockSpec(w.shape, lambda i: (0, 0, 0)),
                   pl.BlockSpec((8, MIX_HALF), lambda i: (0, 0))],
        out_shape=[jax.ShapeDtypeStruct((T, 2 * MIX_HALF), BF16),
                   jax.ShapeDtypeStruct(w.shape, F32),
                   jax.ShapeDtypeStruct((8, MIX_HALF), F32)],
        scratch_shapes=[pltpu.VMEM((SGU_HEADS, ATT_BLOCK, HEAD_LANES), F32)],
        compiler_params=_params(("arbitrary",)),
    )(proj, proj, dmixed, lng, lnb, w, wt, bcol)


def _rot_half(t):
    lane = lax.broadcasted_iota(jnp.int32, t.shape, 1)
    first = (lane % ATT_DH) < (ATT_DH // 2)
    return jnp.where(first, -pltpu.roll(t, HEAD_LANES - ATT_DH // 2, 1), pltpu.roll(t, ATT_DH // 2, 1))


LAYOUT_ROWS = 512


def _res_spec(d, tm, W):
    return pl.BlockSpec((d, tm // d, W), lambda i: (0, i, 0))


def _res_shape(d, T, W, dtype):
    return jax.ShapeDtypeStruct((d, T // d, W), dtype)


def _slab_buf(tm, W):
    return pltpu.VMEM((W // HEAD_LANES, tm, HEAD_LANES), F32)


def _lanes(hp):
    return slice(hp * HEAD_LANES, (hp + 1) * HEAD_LANES)


def _to_res(buf, out_ref, d, dtype):
    nslab, tm, _ = buf.shape
    for hp in range(nslab):
        if d == 1:
            out_ref[0, :, _lanes(hp)] = buf[hp].astype(dtype)
        else:
            for r in range(d):
                out_ref[r, :, _lanes(hp)] = buf.at[hp][pl.ds(r, tm // d, stride=d), :].astype(dtype)


def _from_res(in_ref, buf, d):
    nslab, tm, _ = buf.shape
    for hp in range(nslab):
        if d == 1:
            buf[hp] = in_ref[0, :, _lanes(hp)]
        else:
            for r in range(d):
                buf.at[hp][pl.ds(r, tm // d, stride=d), :] = in_ref[r, :, _lanes(hp)]


def rope_fwd(proj, cos, sin, name):
    T = proj.shape[0]
    tm = LAYOUT_ROWS
    scale = 1.0 / math.sqrt(ATT_DH)
    nd = len(DILATIONS)

    def body(q_ref, k_ref, v_ref, cos_ref, sin_ref, *rest):
        outs, buf = rest[:3 * nd], rest[3 * nd]
        c = cos_ref[...]
        s = sin_ref[...]
        for which, src in enumerate((q_ref, k_ref, v_ref)):
            for hp in range(MIX_HALF // HEAD_LANES):
                t = src[:, _lanes(hp)].astype(F32)
                if which == 0:
                    t = scale * (t * c + _rot_half(t) * s)
                elif which == 1:
                    t = t * c + _rot_half(t) * s
                buf[hp] = t
            for di, d in enumerate(DILATIONS):
                _to_res(buf, outs[3 * di + which], d, BF16)

    return pl.pallas_call(
        body, name=name, grid=(T // tm,),
        in_specs=[pl.BlockSpec((tm, MIX_HALF), lambda i: (i, 2)),
                  pl.BlockSpec((tm, MIX_HALF), lambda i: (i, 3)),
                  pl.BlockSpec((tm, MIX_HALF), lambda i: (i, 4)),
                  pl.BlockSpec((tm, HEAD_LANES), lambda i: (i, 0)),
                  pl.BlockSpec((tm, HEAD_LANES), lambda i: (i, 0))],
        out_specs=[_res_spec(d, tm, MIX_HALF) for d in DILATIONS for _ in range(3)],
        out_shape=[_res_shape(d, T, MIX_HALF, BF16) for d in DILATIONS for _ in range(3)],
        scratch_shapes=[_slab_buf(tm, MIX_HALF)],
        compiler_params=_params(("parallel",)),
    )(proj, proj, proj, cos, sin)


def to_residues(x, col, name):
    T = x.shape[0]
    tm = LAYOUT_ROWS

    def body(x_ref, *rest):
        outs, buf = rest[:-1], rest[-1]
        for hp in range(MIX_HALF // HEAD_LANES):
            buf[hp] = x_ref[:, _lanes(hp)].astype(F32)
        for o_ref, d in zip(outs, DILATIONS):
            _to_res(buf, o_ref, d, BF16)

    return pl.pallas_call(
        body, name=name, grid=(T // tm,),
        in_specs=[pl.BlockSpec((tm, MIX_HALF), lambda i: (i, col))],
        out_specs=[_res_spec(d, tm, MIX_HALF) for d in DILATIONS],
        out_shape=[_res_shape(d, T, MIX_HALF, BF16) for d in DILATIONS],
        scratch_shapes=[_slab_buf(tm, MIX_HALF)],
        compiler_params=_params(("parallel",)),
    )(x)


def rope_bwd(dqs, dks, dvs, cos, sin, name):
    T = dqs[0].shape[0] * dqs[0].shape[1]
    tm = LAYOUT_ROWS
    scale = 1.0 / math.sqrt(ATT_DH)
    npat = len(dqs)

    def body(*refs):
        groups = refs[:npat], refs[npat:2 * npat], refs[2 * npat:3 * npat]
        cos_ref, sin_ref, o_ref, buf, acc = refs[3 * npat:]
        c = cos_ref[...]
        s = sin_ref[...]
        for which, g_refs in enumerate(groups):
            _from_res(g_refs[0], acc, DILATIONS[0])
            for g_ref, d in zip(g_refs[1:], DILATIONS[1:]):
                _from_res(g_ref, buf, d)
                acc[...] += buf[...]
            for hp in range(MIX_HALF // HEAD_LANES):
                g = acc[hp]
                if which == 0:
                    g = scale * g
                if which < 2:
                    g = g * c - _rot_half(g * s)
                o_ref[:, which * MIX_HALF + hp * HEAD_LANES:which * MIX_HALF + (hp + 1) * HEAD_LANES] = g.astype(BF16)

    return pl.pallas_call(
        body, name=name, grid=(T // tm,),
        in_specs=([_res_spec(d, tm, MIX_HALF) for _ in range(3) for d in DILATIONS]
                  + [pl.BlockSpec((tm, HEAD_LANES), lambda i: (i, 0))] * 2),
        out_specs=pl.BlockSpec((tm, 3 * MIX_HALF), lambda i: (i, 0)),
        out_shape=jax.ShapeDtypeStruct((T, 3 * MIX_HALF), BF16),
        scratch_shapes=[_slab_buf(tm, MIX_HALF), _slab_buf(tm, MIX_HALF)],
        compiler_params=_params(("parallel",)),
    )(*dqs, *dks, *dvs, cos, sin)


def _band_masks(n):
    r = lax.broadcasted_iota(jnp.int32, (2 * ATT_BLOCK, ATT_BLOCK), 0)
    c = lax.broadcasted_iota(jnp.int32, (2 * ATT_BLOCK, ATT_BLOCK), 1)
    qi = r % ATT_BLOCK
    head = (c < ATT_DH) == (r < ATT_BLOCK)
    return (c >= qi) & (n > 0), c <= qi, head, c[:ATT_BLOCK] < ATT_DH


def _stack_heads(x, head):
    x2 = jnp.concatenate([x, x], axis=0)
    return jnp.where(head, x2, jnp.zeros_like(x2))


def attn_fwd(q, k, v, name):
    d, L, W = q.shape
    nb = L // ATT_BLOCK

    def body(q_ref, kp_ref, kc_ref, vp_ref, vc_ref, o_ref, lse_ref):
        mask_p, mask_c, head, head0 = _band_masks(pl.program_id(1))
        for hp in range(W // HEAD_LANES):
            sl = slice(hp * HEAD_LANES, (hp + 1) * HEAD_LANES)
            kp, kc, vp, vc = kp_ref[0, :, sl], kc_ref[0, :, sl], vp_ref[0, :, sl], vc_ref[0, :, sl]
            qs = _stack_heads(q_ref[0, :, sl], head)
            sp = jnp.where(mask_p, _nt(qs, kp), NEG)
            sc = jnp.where(mask_c, _nt(qs, kc), NEG)
            m = jnp.maximum(jnp.max(sp, axis=1, keepdims=True), jnp.max(sc, axis=1, keepdims=True))
            pp = jnp.exp(sp - m)
            pc = jnp.exp(sc - m)
            den = jnp.sum(pp, axis=1, keepdims=True) + jnp.sum(pc, axis=1, keepdims=True)
            o = (_nn(pp.astype(BF16), vp) + _nn(pc.astype(BF16), vc)) / den
            lse = m + jnp.log(den)
            o_ref[0, :, sl] = jnp.where(head0, o[:ATT_BLOCK], o[ATT_BLOCK:])
            lse_ref[0, :, sl] = jnp.where(head0, lse[:ATT_BLOCK], lse[ATT_BLOCK:])

    cur = pl.BlockSpec((1, ATT_BLOCK, W), lambda r, n: (r, n, 0))
    prev = pl.BlockSpec((1, ATT_BLOCK, W), lambda r, n: (r, jnp.maximum(n - 1, 0), 0))
    out = jax.ShapeDtypeStruct((d, L, W), F32)
    return pl.pallas_call(
        body, name=name, grid=(d, nb),
        in_specs=[cur, prev, cur, prev, cur],
        out_specs=[cur, cur], out_shape=[out, out],
        compiler_params=_params(("parallel", "parallel")),
    )(q, k, k, v, v)


def attn_combine(os_, lses, name):
    T = os_[0].shape[0] * os_[0].shape[1]
    W = os_[0].shape[2]
    tm = LAYOUT_ROWS
    npat = len(os_)

    def body(*refs):
        o_refs, l_refs = refs[:npat], refs[npat:2 * npat]
        out_ref = refs[2 * npat]
        ores, lres = refs[2 * npat + 1:3 * npat + 1], refs[3 * npat + 1:4 * npat + 1]
        bufs = refs[4 * npat + 1:]
        lbufs, obufs, out_buf, lse_buf = bufs[:npat], bufs[npat:2 * npat], bufs[2 * npat], bufs[2 * npat + 1]
        for p, d in enumerate(DILATIONS):
            _from_res(l_refs[p], lbufs[p], d)
            _from_res(o_refs[p], obufs[p], d)
        for hp in range(W // HEAD_LANES):
            ls = [b[hp] for b in lbufs]
            m = ls[0]
            for l in ls[1:]:
                m = jnp.maximum(m, l)
            es = [jnp.exp(l - m) for l in ls]
            z = es[0]
            for e in es[1:]:
                z = z + e
            acc = es[0] * obufs[0][hp]
            for p in range(1, npat):
                acc = acc + es[p] * obufs[p][hp]
            out = acc / z
            out_ref[:, _lanes(hp)] = out.astype(BF16)
            out_buf[hp] = out
            lse_buf[hp] = m + jnp.log(z)
        for p, d in enumerate(DILATIONS):
            _to_res(out_buf, ores[p], d, BF16)
            _to_res(lse_buf, lres[p], d, F32)

    return pl.pallas_call(
        body, name=name, grid=(T // tm,),
        in_specs=[_res_spec(d, tm, W) for _ in range(2) for d in DILATIONS],
        out_specs=([pl.BlockSpec((tm, W), lambda i: (i, 0))] + [_res_spec(d, tm, W) for _ in range(2) for d in DILATIONS]),
        out_shape=([jax.ShapeDtypeStruct((T, W), BF16)] + [_res_shape(d, T, W, BF16) for d in DILATIONS]
                   + [_res_shape(d, T, W, F32) for d in DILATIONS]),
        scratch_shapes=[_slab_buf(tm, W)] * (2 * npat + 2),
        compiler_params=_params(("parallel",)),
    )(*os_, *lses)


def attn_bwd(q, k, v, do, o, lse, name):
    d, L, W = q.shape
    nb = L // ATT_BLOCK

    def body(q_ref, kp_ref, kc_ref, vp_ref, vc_ref, do_ref, o_ref, lse_ref, dq_ref, dk_ref, dv_ref, kkeep, vkeep):
        n = pl.program_id(1)

        @pl.when(n == 0)
        def _():
            kkeep[...] = jnp.zeros_like(kkeep)
            vkeep[...] = jnp.zeros_like(vkeep)

        @pl.when(n < nb)
        def _():
            mask_p, mask_c, head, head0 = _band_masks(n)
            for hp in range(W // HEAD_LANES):
                sl = slice(hp * HEAD_LANES, (hp + 1) * HEAD_LANES)
                kp, kc, vp, vc = kp_ref[0, :, sl], kc_ref[0, :, sl], vp_ref[0, :, sl], vc_ref[0, :, sl]
                dout = do_ref[0, :, sl]
                qs = _stack_heads(q_ref[0, :, sl], head)
                dos = _stack_heads(dout, head)
                lse_v = lse_ref[0, :, sl]
                lse_c = jnp.max(jnp.where(head, jnp.concatenate([lse_v, lse_v], axis=0), NEG), axis=1, keepdims=True)
                delta = jnp.sum(_stack_heads(dout.astype(F32) * o_ref[0, :, sl].astype(F32), head), axis=1, keepdims=True)
                pp = jnp.exp(jnp.where(mask_p, _nt(qs, kp), NEG) - lse_c)
                pc = jnp.exp(jnp.where(mask_c, _nt(qs, kc), NEG) - lse_c)
                dsp = (pp * (_nt(dos, vp) - delta)).astype(BF16)
                dsc = (pc * (_nt(dos, vc) - delta)).astype(BF16)
                dq2 = _nn(dsp, kp) + _nn(dsc, kc)
                dq_ref[0, :, sl] = jnp.where(head0, dq2[:ATT_BLOCK], dq2[ATT_BLOCK:])
                dk_ref[0, :, sl] = kkeep[:, sl] + _tn(dsp, qs)
                dv_ref[0, :, sl] = vkeep[:, sl] + _tn(pp.astype(BF16), dos)
                kkeep[:, sl] = _tn(dsc, qs)
                vkeep[:, sl] = _tn(pc.astype(BF16), dos)

        @pl.when(n == nb)
        def _():
            dk_ref[0] = kkeep[...]
            dv_ref[0] = vkeep[...]

    cur = pl.BlockSpec((1, ATT_BLOCK, W), lambda r, n: (r, jnp.minimum(n, nb - 1), 0))
    prev = pl.BlockSpec((1, ATT_BLOCK, W), lambda r, n: (r, jnp.clip(n - 1, 0, nb - 1), 0))
    out = jax.ShapeDtypeStruct((d, L, W), F32)
    return pl.pallas_call(
        body, name=name, grid=(d, nb + 1),
        in_specs=[cur, prev, cur, prev, cur, cur, cur, cur],
        out_specs=[cur, prev, prev], out_shape=[out, out, out],
        scratch_shapes=[pltpu.VMEM((ATT_BLOCK, W), F32), pltpu.VMEM((ATT_BLOCK, W), F32)],
        compiler_params=_params(("parallel", "arbitrary")),
    )(q, k, k, v, v, do, o, lse)


def final_loss_bwd(h, gf, tgt, gate, name):
    T, D = h.shape
    tm = _tile(T, 512, 16)
    g_in, g_out, g_shape, g_ops = _gate_specs(gate, tm, D)

    def body(h_ref, g_ref, t_ref, o_ref, modg_ref, dh_ref, red_ref, do_ref, redg_ref):
        x = h_ref[...]
        r = lax.rsqrt(jnp.mean(x * x, axis=-1, keepdims=True) + EPS)
        n = x * r
        g = g_ref[...]
        err = n * g - t_ref[...]
        dy = err * (1.0 / D)

        @pl.when(pl.program_id(0) == 0)
        def _():
            red_ref[...] = jnp.zeros_like(red_ref)

        red_ref[0:1, :] += jnp.sum(dy * n, axis=0, keepdims=True)
        red_ref[1:2, :] += jnp.zeros((1, D), F32) + (0.5 / D) * jnp.sum(err * err, keepdims=True)
        dn = dy * g
        dh = r * (dn - n * jnp.mean(dn * n, axis=-1, keepdims=True))
        dh_ref[...] = dh
        _gate_emit(dh, gate, o_ref, modg_ref, do_ref, redg_ref)

    return pl.pallas_call(
        body, name=name, grid=(T // tm,),
        in_specs=[pl.BlockSpec((tm, D), lambda i: (i, 0)),
                  pl.BlockSpec((1, D), lambda i: (0, 0)),
                  pl.BlockSpec((tm, D), lambda i: (i, 0))] + g_in,
        out_specs=[pl.BlockSpec((tm, D), lambda i: (i, 0)), pl.BlockSpec((8, D), lambda i: (0, 0))] + g_out,
        out_shape=[jax.ShapeDtypeStruct((T, D), F32), jax.ShapeDtypeStruct((8, D), F32)] + g_shape,
        compiler_params=_params(("arbitrary",)),
    )(h, gf, tgt, *g_ops)


def ada_fwd(c_all, ada_w, ada_b, name):
    nl, D, N = ada_w.shape

    def body(c_ref, w_ref, b_ref, o_ref):
        c = c_ref[...]
        o_ref[0] = _nn(c * _sigmoid(c), w_ref[0]) + b_ref[0]

    return pl.pallas_call(
        body, name=name, grid=(nl,),
        in_specs=[pl.BlockSpec((N_DEV, D), lambda l: (0, 0)),
                  pl.BlockSpec((1, D, N), lambda l: (l, 0, 0)),
                  pl.BlockSpec((1, 1, N), lambda l: (l, 0, 0))],
        out_specs=pl.BlockSpec((1, N_DEV, N), lambda l: (l, 0, 0)),
        out_shape=jax.ShapeDtypeStruct((nl, N_DEV, N), F32),
        compiler_params=_params(("parallel",)),
    )(c_all, ada_w, ada_b)


def ada_bwd(c_allT, dmod, name):
    nl, _, N = dmod.shape
    D = c_allT.shape[0]

    def body(c_ref, g_ref, o_ref):
        c = c_ref[...]
        ca = c * _sigmoid(c)
        acc = ca[:, 0:1] * g_ref[0, 0:1, :]
        for b in range(1, N_DEV):
            acc = acc + ca[:, b:b + 1] * g_ref[0, b:b + 1, :]
        o_ref[0] = acc

    return pl.pallas_call(
        body, name=name, grid=(nl,),
        in_specs=[pl.BlockSpec((D, N_DEV), lambda l: (0, 0)),
                  pl.BlockSpec((1, N_DEV, N), lambda l: (l, 0, 0))],
        out_specs=pl.BlockSpec((1, D, N), lambda l: (l, 0, 0)),
        out_shape=jax.ShapeDtypeStruct((nl, D, N), F32),
        compiler_params=_params(("parallel",)),
    )(c_allT, dmod)


def adamw(w, g, m, v, name):
    R, C = w.shape
    tr = _tile(R, max(8, (1 << 19) // C // 8 * 8), 8)
    c1 = 1.0 - ADAM_B1 ** ADAM_STEP
    c2 = 1.0 - ADAM_B2 ** ADAM_STEP

    def body(w_ref, g_ref, m_ref, v_ref, d_ref, mo_ref, vo_ref):
        gv = g_ref[...]
        mn = ADAM_B1 * m_ref[...] + (1.0 - ADAM_B1) * gv
        vn = ADAM_B2 * v_ref[...] + (1.0 - ADAM_B2) * (gv * gv)
        mo_ref[...] = mn
        vo_ref[...] = vn
        d_ref[...] = -ADAM_LR * ((mn / c1) / (jnp.sqrt(vn / c2) + ADAM_EPS) + ADAM_WD * w_ref[...])

    blk = pl.BlockSpec((tr, C), lambda i: (i, 0))
    out = jax.ShapeDtypeStruct((R, C), F32)
    return pl.pallas_call(
        body, name=name, grid=(R // tr,),
        in_specs=[blk] * 4, out_specs=[blk] * 3, out_shape=[out] * 3,
        compiler_params=_params(("parallel",)),
    )(w, g, m, v)


def adamw_layer(w, g, m, v, l, prev, name):
    NLw, R, C = w.shape
    tr = _tile(R, max(8, (1 << 19) // C // 8 * 8), 8)
    nrb = R // tr
    c1 = 1.0 - ADAM_B1 ** ADAM_STEP
    c2 = 1.0 - ADAM_B2 ** ADAM_STEP
    w, m, v = (t.reshape(NLw * R, C) for t in (w, m, v))

    def body(w_ref, g_ref, m_ref, v_ref, *rest):
        go_ref, d_ref, mo_ref, vo_ref = rest[-4:]
        gv = g_ref[...]
        mn = ADAM_B1 * m_ref[...] + (1.0 - ADAM_B1) * gv
        vn = ADAM_B2 * v_ref[...] + (1.0 - ADAM_B2) * (gv * gv)
        go_ref[...] = gv
        mo_ref[...] = mn
        vo_ref[...] = vn
        d_ref[...] = -ADAM_LR * ((mn / c1) / (jnp.sqrt(vn / c2) + ADAM_EPS) + ADAM_WD * w_ref[...])

    lay = pl.BlockSpec((tr, C), lambda i: (l * nrb + i, 0))
    out = jax.ShapeDtypeStruct((NLw * R, C), F32)
    n_prev = 0 if prev is None else 4
    return pl.pallas_call(
        body, name=name, grid=(nrb,),
        in_specs=[lay, pl.BlockSpec((tr, C), lambda i: (i, 0)), lay, lay] + [pl.BlockSpec(memory_space=pl.ANY)] * n_prev,
        out_specs=[lay] * 4, out_shape=[out] * 4,
        input_output_aliases={4 + i: i for i in range(n_prev)},
        compiler_params=_params(("parallel",)),
    )(w, g, m, v, *(prev or ()))


def sum_slots(x, name):
    S, R, C = x.shape
    tr = _tile(R, 128, 8)

    def body(x_ref, o_ref):
        acc = x_ref[0]
        for s in range(1, S):
            acc = acc + x_ref[s]
        o_ref[...] = acc

    return pl.pallas_call(
        body, name=name, grid=(R // tr,),
        in_specs=[pl.BlockSpec((S, tr, C), lambda i: (0, i, 0))],
        out_specs=pl.BlockSpec((tr, C), lambda i: (i, 0)),
        out_shape=jax.ShapeDtypeStruct((R, C), F32),
        compiler_params=_params(("parallel",)),
    )(x)


def sum_halves(g, lands, c_idx, name):
    n, ns, _, rh, D = g.shape

    def body(c_ref, g_ref, l_ref, o_ref):
        o_ref[0, 0] = (g_ref[0, 0, 0].astype(F32) + l_ref[0, 0].astype(F32)).astype(BF16)

    return pl.pallas_call(
        body, name=name,
        grid_spec=pltpu.PrefetchScalarGridSpec(
            num_scalar_prefetch=1, grid=(n, ns),
            in_specs=[pl.BlockSpec((1, 1, 1, rh, D), lambda i, j, c: (i, j, c[0], 0, 0)),
                      pl.BlockSpec((1, 1, rh, D), lambda i, j, c: (i, j, 0, 0))],
            out_specs=pl.BlockSpec((1, 1, rh, D), lambda i, j, c: (i, j, 0, 0))),
        out_shape=jax.ShapeDtypeStruct((n, ns, rh, D), BF16),
        compiler_params=_params(("parallel", "parallel")),
    )(c_idx, g, lands)


def sum_chips(p, lands, place, name):
    n, ns, rh, D = p.shape

    def body(c_ref, p_ref, l_ref, o_ref):
        acc = p_ref[0, 0].astype(F32)
        for j in range(N_CHIP - 1):
            acc = acc + l_ref[j, 0].astype(F32)
        o_ref[0, 0] = acc

    return pl.pallas_call(
        body, name=name,
        grid_spec=pltpu.PrefetchScalarGridSpec(
            num_scalar_prefetch=1, grid=(n,),
            in_specs=[pl.BlockSpec((1, 1, rh, D), lambda i, c: (i, c[0], 0, 0)),
                      pl.BlockSpec((N_CHIP - 1, 1, rh, D), lambda i, c: (0, i, 0, 0))],
            out_specs=pl.BlockSpec((1, 1, rh, D), lambda i, c: (i, c[1], 0, 0))),
        out_shape=jax.ShapeDtypeStruct((n, 2, rh, D), F32),
        compiler_params=_params(("parallel",)),
    )(place, p, lands)


def _my_place():
    return lax.axis_index("x"), lax.axis_index("y"), lax.axis_index("c")


def _other_chips(mx, my):
    return [(1 - mx, my), (mx, 1 - my), (1 - mx, 1 - my)]


def gather_small(x, after, name):
    def body(x_ref, after_ref, out_ref, sum_ref, send_sems, recv_sems):
        mx, my, mc = _my_place()
        me = 4 * mx + 2 * my + mc
        out_ref[me] = x_ref[...]
        sends = []
        for k in range(1, N_DEV):
            kx, ky, kc = (k >> 2) & 1, (k >> 1) & 1, k & 1
            peer = (1 - mx if kx else mx, 1 - my if ky else my, 1 - mc if kc else mc)
            cp = pltpu.make_async_remote_copy(
                src_ref=x_ref, dst_ref=out_ref.at[me], send_sem=send_sems.at[k - 1], recv_sem=recv_sems.at[k - 1],
                device_id=peer, device_id_type=MESH)
            cp.start()
            sends.append((cp, 4 * peer[0] + 2 * peer[1] + peer[2], peer))
        for k, (cp, peer_slot, peer) in enumerate(sends):
            pltpu.make_async_remote_copy(
                src_ref=x_ref, dst_ref=out_ref.at[peer_slot], send_sem=send_sems.at[k], recv_sem=recv_sems.at[k],
                device_id=peer, device_id_type=MESH).wait_recv()
        for cp, _, _ in sends:
            cp.wait_send()
        acc = out_ref[0]
        for s in range(1, N_DEV):
            acc = acc + out_ref[s]
        sum_ref[...] = acc

    vmem = pl.BlockSpec(memory_space=pltpu.VMEM)
    return pl.pallas_call(
        body, name=name,
        in_specs=[vmem, pl.BlockSpec(memory_space=pl.ANY)], out_specs=[vmem, vmem],
        out_shape=[jax.ShapeDtypeStruct((N_DEV,) + x.shape, x.dtype), jax.ShapeDtypeStruct(x.shape, x.dtype)],
        scratch_shapes=[pltpu.SemaphoreType.DMA((N_DEV - 1,)), pltpu.SemaphoreType.DMA((N_DEV - 1,))],
        compiler_params=pltpu.CompilerParams(vmem_limit_bytes=VMEM_LIMIT),
    )(x, after)


_HBM =pl.BlockSpec(memory_space=pltpu.HBM)
_SEM = pl.BlockSpec(memory_space=pltpu.SEMAPHORE)
_DATAFLOW = pltpu.SideEffectType.DATAFLOW_SIDE_EFFECTING


def _gather_copies(shard, land, send, recv, base):
    mx, my, mc = _my_place()
    ci = 2 * mx + my
    peers = [((cx, cy, mc), 2 * cx + cy) for cx, cy in _other_chips(mx, my)] + [((mx, my, 1 - mc), ci)]
    out = []
    for q, (dev, src_slot) in enumerate(peers):
        out.append((
            pltpu.make_async_remote_copy(src_ref=shard, dst_ref=land.at[:, ci], send_sem=send.at[base + q],
                                         recv_sem=recv.at[base + q], device_id=dev, device_id_type=MESH),
            pltpu.make_async_remote_copy(src_ref=shard, dst_ref=land.at[:, src_slot], send_sem=send.at[base + q],
                                         recv_sem=recv.at[base + q], device_id=dev, device_id_type=MESH)))
    return out


def gather_start(groups, after, name):
    items = [s for g in groups for s in g]
    ni, ng = len(items), len(groups)

    def body(*refs):
        shards, lands = refs[:ni], refs[ni:2 * ni]
        sems = refs[2 * ni + 1:2 * ni + 1 + 2 * ng]
        token = refs[-1]
        i = 0
        for g, grp in enumerate(groups):
            for p in range(len(grp)):
                for start_cp, _ in _gather_copies(shards[i], lands[i], sems[2 * g], sems[2 * g + 1], 4 * p):
                    start_cp.start()
                i += 1
        token[...] = jnp.zeros_like(token)

    sem_shapes = []
    for grp in groups:
        sem_shapes += [pltpu.SemaphoreType.DMA((4 * len(grp),))] * 2
    land_shapes = [(s.shape[0], N_CHIP) + s.shape[1:] for s in items]
    outs = pl.pallas_call(
        body, name=name,
        in_specs=[_HBM] * (2 * ni) + [pl.BlockSpec(memory_space=pl.ANY)],
        out_specs=[_SEM] * (2 * ng) + [_HBM] * (2 * ni) + [pl.BlockSpec(memory_space=pltpu.VMEM)],
        out_shape=(sem_shapes + [pltpu.HBM(s.shape, s.dtype) for s in items]
                   + [pltpu.HBM(ls, s.dtype) for ls, s in zip(land_shapes, items)]
                   + [jax.ShapeDtypeStruct((8, 128), F32)]),
        input_output_aliases={i: 2 * ng + i for i in range(2 * ni)},
        compiler_params=pltpu.CompilerParams(has_side_effects=_DATAFLOW),
    )(*[pltpu.with_memory_space_constraint(s, pltpu.HBM) for s in items],
      *[pltpu.with_memory_space_constraint(lax.empty(ls, s.dtype), pltpu.HBM) for ls, s in zip(land_shapes, items)],
      after)
    sems, thru, token = outs[:2 * ng], outs[2 * ng:2 * ng + 2 * ni], outs[-1]
    handles, i = [], 0
    for g, grp in enumerate(groups):
        n = len(grp)
        handles.append((sems[2 * g], sems[2 * g + 1], thru[i:i + n], thru[ni + i:ni + i + n]))
        i += n
    return handles, token


def gather_wait(handle, after, name):
    send, recv, shards, lands = handle
    n = len(shards)

    def body(*refs):
        shard_refs, land_refs = refs[:n], refs[n:2 * n]
        send_ref, recv_ref = refs[2 * n], refs[2 * n + 1]
        for p in range(n):
            for start_cp, recv_cp in _gather_copies(shard_refs[p], land_refs[p], send_ref, recv_ref, 4 * p):
                start_cp.wait_send()
                recv_cp.wait_recv()

    outs = pl.pallas_call(
        body, name=name,
        in_specs=[_HBM] * (2 * n) + [_SEM, _SEM, pl.BlockSpec(memory_space=pl.ANY)],
        out_specs=[_HBM] * (2 * n),
        out_shape=[pltpu.HBM(s.shape, s.dtype) for s in shards] + [pltpu.HBM(l.shape, l.dtype) for l in lands],
        input_output_aliases={i: i for i in range(2 * n)},
        compiler_params=pltpu.CompilerParams(has_side_effects=_DATAFLOW),
    )(*shards, *lands, send, recv, after)
    return outs[n:]


def _sibling_copies(gs, lands, send, recv):
    mx, my, mc = _my_place()
    return [pltpu.make_async_remote_copy(
        src_ref=gs[k].at[:, :, 1 - mc], dst_ref=lands[k], send_sem=send.at[k], recv_sem=recv.at[k],
        device_id=(mx, my, 1 - mc), device_id_type=MESH) for k in range(len(gs))]


def sibling_start(gs, after, name):
    K = len(gs)

    def body(*refs):
        ins, lands = refs[:K], refs[K:2 * K]
        send, recv = refs[2 * K + 1], refs[2 * K + 2]
        for cp in _sibling_copies(ins, lands, send, recv):
            cp.start()
        refs[-1][...] = jnp.zeros_like(refs[-1])

    land_shapes = [g.shape[:2] + g.shape[3:] for g in gs]
    outs = pl.pallas_call(
        body, name=name,
        in_specs=[_HBM] * (2 * K) + [pl.BlockSpec(memory_space=pl.ANY)],
        out_specs=[_SEM, _SEM] + [_HBM] * (2 * K) + [pl.BlockSpec(memory_space=pltpu.VMEM)],
        out_shape=([pltpu.SemaphoreType.DMA((K,))] * 2 + [pltpu.HBM(g.shape, g.dtype) for g in gs]
                   + [pltpu.HBM(ls, g.dtype) for ls, g in zip(land_shapes, gs)] + [jax.ShapeDtypeStruct((8, 128), F32)]),
        input_output_aliases={i: 2 + i for i in range(2 * K)},
        compiler_params=pltpu.CompilerParams(has_side_effects=_DATAFLOW),
    )(*[pltpu.with_memory_space_constraint(g, pltpu.HBM) for g in gs],
      *[pltpu.with_memory_space_constraint(lax.empty(ls, g.dtype), pltpu.HBM) for ls, g in zip(land_shapes, gs)],
      after)
    return (outs[0], outs[1], outs[2:2 + K], outs[2 + K:2 + 2 * K]), outs[-1]


def sibling_wait(handle, after, name):
    send, recv, gs, lands = handle
    K = len(gs)

    def body(*refs):
        ins, land_refs = refs[:K], refs[K:2 * K]
        for cp in _sibling_copies(ins, land_refs, refs[2 * K], refs[2 * K + 1]):
            cp.wait_send()
            cp.wait_recv()

    outs = pl.pallas_call(
        body, name=name,
        in_specs=[_HBM] * (2 * K) + [_SEM, _SEM, pl.BlockSpec(memory_space=pl.ANY)],
        out_specs=[_HBM] * (2 * K),
        out_shape=[pltpu.HBM(g.shape, g.dtype) for g in gs] + [pltpu.HBM(l.shape, l.dtype) for l in lands],
        input_output_aliases={i: i for i in range(2 * K)},
        compiler_params=pltpu.CompilerParams(has_side_effects=_DATAFLOW),
    )(*gs, *lands, send, recv, after)
    return outs[:K], outs[K:]


def _small_copies(x, land, send, recv):
    mx, my, mc = _my_place()
    me = 4 * mx + 2 * my + mc
    out = []
    for k in range(1, N_DEV):
        peer = (1 - mx if k & 4 else mx, 1 - my if k & 2 else my, 1 - mc if k & 1 else mc)
        slot = 4 * peer[0] + 2 * peer[1] + peer[2]
        out.append(tuple(pltpu.make_async_remote_copy(
            src_ref=x, dst_ref=land.at[s], send_sem=send.at[k - 1], recv_sem=recv.at[k - 1],
            device_id=peer, device_id_type=MESH) for s in (me, slot)))
    return out


def small_start(x, after, name):
    def body(x_ref, land_ref, after_ref, send, recv, x_thru, land_thru, token):
        for mine, _ in _small_copies(x_ref, land_ref, send, recv):
            mine.start()
        token[...] = jnp.zeros_like(token)

    land_shape = (N_DEV,) + x.shape
    outs = pl.pallas_call(
        body, name=name,
        in_specs=[_HBM, _HBM, pl.BlockSpec(memory_space=pl.ANY)],
        out_specs=[_SEM, _SEM, _HBM, _HBM, pl.BlockSpec(memory_space=pltpu.VMEM)],
        out_shape=[pltpu.SemaphoreType.DMA((N_DEV - 1,))] * 2 + [pltpu.HBM(x.shape, x.dtype), pltpu.HBM(land_shape, x.dtype),
                                                                 jax.ShapeDtypeStruct((8, 128), F32)],
        input_output_aliases={0: 2, 1: 3},
        compiler_params=pltpu.CompilerParams(has_side_effects=_DATAFLOW),
    )(pltpu.with_memory_space_constraint(x, pltpu.HBM),
      pltpu.with_memory_space_constraint(lax.empty(land_shape, x.dtype), pltpu.HBM), after)
    return outs[:4], outs[4]


def small_wait(handle, after, name):
    send, recv, x, land = handle

    def body(x_ref, land_ref, send_ref, recv_ref, after_ref, x_out, land_out):
        for mine, theirs in _small_copies(x_ref, land_ref, send_ref, recv_ref):
            mine.wait_send()
            theirs.wait_recv()

    return pl.pallas_call(
        body, name=name,
        in_specs=[_HBM, _HBM, _SEM, _SEM, pl.BlockSpec(memory_space=pl.ANY)],
        out_specs=[_HBM, _HBM],
        out_shape=[pltpu.HBM(x.shape, x.dtype), pltpu.HBM(land.shape, land.dtype)],
        input_output_aliases={0: 0, 1: 1},
        compiler_params=pltpu.CompilerParams(has_side_effects=_DATAFLOW),
    )(x, land, send, recv, after)


def _scatter_copies(ps, lands, send, recv):
    mx, my, mc = _my_place()
    cps = []
    for j, (cx, cy) in enumerate(_other_chips(mx, my)):
        for k in range(len(ps)):
            cps.append(pltpu.make_async_remote_copy(
                src_ref=ps[k].at[:, 2 * cx + cy], dst_ref=lands[k].at[j],
                send_sem=send.at[k * 3 + j], recv_sem=recv.at[k * 3 + j],
                device_id=(cx, cy, mc), device_id_type=MESH))
    return cps


def scatter_start(ps, after, name):
    K = len(ps)

    def body(*refs):
        ins, lands = refs[:K], refs[K:2 * K]
        send, recv = refs[2 * K + 1], refs[2 * K + 2]
        for cp in _scatter_copies(ins, lands, send, recv):
            cp.start()
        refs[-1][...] = jnp.zeros_like(refs[-1])

    land_shapes = [(N_CHIP - 1, p.shape[0]) + p.shape[2:] for p in ps]
    outs = pl.pallas_call(
        body, name=name,
        in_specs=[_HBM] * (2 * K) + [pl.BlockSpec(memory_space=pl.ANY)],
        out_specs=[_SEM, _SEM] + [_HBM] * (2 * K) + [pl.BlockSpec(memory_space=pltpu.VMEM)],
        out_shape=([pltpu.SemaphoreType.DMA((3 * K,))] * 2 + [pltpu.HBM(p.shape, p.dtype) for p in ps]
                   + [pltpu.HBM(ls, p.dtype) for ls, p in zip(land_shapes, ps)] + [jax.ShapeDtypeStruct((8, 128), F32)]),
        input_output_aliases={i: 2 + i for i in range(2 * K)},
        compiler_params=pltpu.CompilerParams(has_side_effects=_DATAFLOW),
    )(*[pltpu.with_memory_space_constraint(p, pltpu.HBM) for p in ps],
      *[pltpu.with_memory_space_constraint(lax.empty(ls, p.dtype), pltpu.HBM) for ls, p in zip(land_shapes, ps)],
      after)
    return (outs[0], outs[1], outs[2:2 + K], outs[2 + K:2 + 2 * K]), outs[-1]


def scatter_wait(handle, after, name):
    send, recv, ps, lands = handle
    K = len(ps)
    afters = list(after) if isinstance(after, (list, tuple)) else [after]

    def body(*refs):
        ins, land_refs = refs[:K], refs[K:2 * K]
        send_ref, recv_ref = refs[2 * K], refs[2 * K + 1]
        for cp in _scatter_copies(ins, land_refs, send_ref, recv_ref):
            cp.wait_send()
            cp.wait_recv()

    outs = pl.pallas_call(
        body, name=name,
        in_specs=[_HBM] * (2 * K) + [_SEM, _SEM] + [pl.BlockSpec(memory_space=pl.ANY)] * len(afters),
        out_specs=[_HBM] * (2 * K),
        out_shape=[pltpu.HBM(p.shape, p.dtype) for p in ps] + [pltpu.HBM(l.shape, l.dtype) for l in lands],
        input_output_aliases={i: i for i in range(2 * K)},
        compiler_params=pltpu.CompilerParams(has_side_effects=_DATAFLOW),
    )(*ps, *lands, send, recv, *afters)
    return outs[:K], outs[K:]


def sibling_complete(ss, name):
    K = len(ss)

    def body(*refs):
        ins, outs = refs[:K], refs[K:2 * K]
        send, recv = refs[2 * K:]
        mx, my, mc = _my_place()
        cps = []
        for k in range(K):
            cp = pltpu.make_async_remote_copy(
                src_ref=ins[k].at[:, mc], dst_ref=outs[k].at[:, mc], send_sem=send.at[k], recv_sem=recv.at[k],
                device_id=(mx, my, 1 - mc), device_id_type=MESH)
            cp.start()
            cps.append(cp)
        for k in range(K):
            pltpu.make_async_remote_copy(
                src_ref=ins[k].at[:, mc], dst_ref=outs[k].at[:, 1 - mc], send_sem=send.at[k], recv_sem=recv.at[k],
                device_id=(mx, my, 1 - mc), device_id_type=MESH).wait_recv()
        for cp in cps:
            cp.wait_send()

    hbm = pl.BlockSpec(memory_space=pl.ANY)
    return pl.pallas_call(
        body, name=name,
        in_specs=[hbm] * K, out_specs=[hbm] * K,
        out_shape=[jax.ShapeDtypeStruct(s.shape, s.dtype) for s in ss],
        scratch_shapes=[pltpu.SemaphoreType.DMA((K,)), pltpu.SemaphoreType.DMA((K,))],
        input_output_aliases={k: k for k in range(K)},
    )(*ss)


def _rope_tables(T):
    inv = ROPE_THETA ** (-jnp.arange(0, ATT_DH, 2, dtype=F32) / ATT_DH)
    ang = jnp.arange(T, dtype=F32)[:, None] * inv[None, :]
    ang = jnp.concatenate([ang, ang, ang, ang], axis=-1)
    return jnp.cos(ang), jnp.sin(ang)


def _ffn_fwd(h, y, mod, i0, get_up, get_down, norm_next, tag):
    wgu = get_up(h)
    a, b, s = ffn_up(y, (wgu, (0,)), (wgu, (1,)), f"ffn_up_{tag}")
    wd = get_down(s)
    outs = resid_matmul([s], (wd, (0,)), h, mod, i0 + 2, 0.5, f"ffn_down_{tag}", norm_next)
    hn, o = outs[0], outs[1]
    return hn, (outs[2] if norm_next else None), (h, y, a, b, s, o), ((wgu, (0,)), (wgu, (1,)), (wd, (0,)))


def _ffn_bwd(dh, do, res, ng, i_n, mod, i0, wgT, wuT, wd, on_grads, next_gate, tag):
    h, y, a, b, s, o = res
    F = _wrows(wgT)
    da, db = ffn_bwd_mid(do, wd, a, b, f"ffn_bwd_mid_{tag}")
    gbuf = lax.empty((3, F, h.shape[1]), BF16)
    gbuf = matmul_tn(da, y, gbuf, 0, 0, f"dwg_{tag}")
    gbuf = matmul_tn(db, y, gbuf, 1, 0, f"dwu_{tag}")
    gbuf = matmul_tn(s, do, gbuf, 2, 0, f"dwd_{tag}")
    token, then = on_grads([gbuf])
    outs = dy_normbwd([(da, 0, wgT, 0, F), (db, 0, wuT, 0, F)], h, dh, ng, i_n, mod + token, i0 + 1,
                      f"ffn_bwd_dy_{tag}", next_gate)
    return outs, then


def _mixer_fwd(h, y, mod, w_inT, w_out, sgu, cos, sin, norm_next, tag):
    lng, lnb, sw, swt, bcol = sgu
    proj = matmul_nt(y, w_inT, f"proj_{tag}")
    out_a = sgu_fwd(proj, lng, lnb, sw, bcol, f"sgu_fwd_{tag}")
    qkv = rope_fwd(proj, cos, sin, f"rope_fwd_{tag}")
    npat = len(DILATIONS)
    qkv_res = [tuple(qkv[3 * p:3 * p + 3]) for p in range(npat)]
    os_, lses = [], []
    for d, (qd, kd, vd) in zip(DILATIONS, qkv_res):
        o_d, lse_d = attn_fwd(qd, kd, vd, f"attn_fwd_d{d}_{tag}")
        os_.append(o_d)
        lses.append(lse_d)
    comb = attn_combine(os_, lses, f"attn_combine_{tag}")
    out_b, o_res, lse_res = comb[0], comb[1:1 + npat], comb[1 + npat:]
    outs = resid_matmul([out_a, out_b], w_out, h, mod, 5, 1.0, f"mix_out_{tag}", norm_next)
    hn, om = outs[0], outs[1]
    return hn, (outs[2] if norm_next else None), (h, y, proj, out_a, out_b, o_res, lse_res, qkv_res, om)


def _mixer_bwd(dh, dom, res, ng, mod, w_inT, w_out, sgu, cos, sin, on_grads, next_gate, tag):
    lng, lnb, sw, swt, bcol = sgu
    h, y, proj, out_a, out_b, o_res, lse_res, qkv_res, om = res
    D = h.shape[1]
    dmixed = matmul_nt(dom, w_out, f"dmixed_{tag}")
    woutbuf = lax.empty((1, 2 * MIX_HALF, D), BF16)
    woutbuf = matmul_tn(out_a, dom, woutbuf, 0, 0, f"dwout_a_{tag}", tmo_cap=MIX_HALF)
    woutbuf = matmul_tn(out_b, dom, woutbuf, 0, MIX_HALF, f"dwout_b_{tag}", tmo_cap=MIX_HALF)
    d_uv, d_sw, d_svec = sgu_bwd(proj, dmixed, lng, lnb, sw, swt, bcol, f"sgu_bwd_{tag}")
    do_res = to_residues(dmixed, 1, f"dout_res_{tag}")
    dqs, dks, dvs = [], [], []
    for p, (d, (qd, kd, vd)) in enumerate(zip(DILATIONS, qkv_res)):
        dq, dk, dv = attn_bwd(qd, kd, vd, do_res[p], o_res[p], lse_res[p], f"attn_bwd_d{d}_{tag}")
        dqs.append(dq)
        dks.append(dk)
        dvs.append(dv)
    d_qkv = rope_bwd(dqs, dks, dvs, cos, sin, f"rope_bwd_{tag}")
    winbuf = lax.empty((1, 5 * MIX_HALF, D), BF16)
    winbuf = matmul_tn(d_uv, y, winbuf, 0, 0, f"dwin_uv_{tag}", tmo_cap=MIX_HALF)
    winbuf = matmul_tn(d_qkv, y, winbuf, 0, 2 * MIX_HALF, f"dwin_qkv_{tag}", tmo_cap=MIX_HALF)
    token, then = on_grads([winbuf, woutbuf])
    pairs = [(d_uv, 0, w_inT, 0, 2 * MIX_HALF), (d_qkv, 0, w_inT, 1, 2 * MIX_HALF), (d_qkv, 2, w_inT, 4, MIX_HALF)]
    outs = dy_normbwd(pairs, h, dh, ng, 1, mod + token, 4, f"mix_bwd_dy_{tag}", next_gate)
    return outs, d_sw, d_svec, then


def _local_step(x, tgt, mods, ngs, get_w, sgus, gf, on_block_grads, on_layer_small):
    T, D = x.shape
    cos, sin = _rope_tables(T)
    h = x
    saved, weights = [], []
    for l in range(2):
        def getter(blk, l=l):
            return lambda after: get_w(l, blk, after)

        if l == 0:
            y = normmod_fwd(h, ngs[0], 0, mods[0], 0, 1, "normmod_l0f1")
        h, y, r1, wf1 = _ffn_fwd(h, y, mods[l], 0, getter("f1u"), getter("f1d"), (ngs[l], 1, mods[l], 3, 4), f"l{l}f1")
        w_inT, w_out = get_w(l, "mx", h)
        h, y, r2 = _mixer_fwd(h, y, mods[l], (w_inT, (0,)), (w_out, (0,)), sgus[l], cos, sin,
                              (ngs[l], 2, mods[l], 6, 7), f"l{l}mx")
        h, y, r3, wf2 = _ffn_fwd(h, y, mods[l], 6, getter("f2u"), getter("f2d"),
                                 (ngs[l + 1], 0, mods[l + 1], 0, 1) if l + 1 < 2 else None, f"l{l}f2")
        saved.append((r1, r2, r3))
        weights.append((wf1, w_inT, w_out, wf2))
    def gate_of(l, blk):
        r1, r2, r3 = saved[l]
        o, i_g, coef = {"f2": (r3[5], 8, 0.5), "mx": (r2[-1], 5, 1.0), "f1": (r1[5], 2, 0.5)}[blk]
        return o, mods[l], i_g, coef

    seq = [(l, blk) for l in (1, 0) for blk in ("f2", "mx", "f1")]
    dh, red_final, do, red_g = final_loss_bwd(h, gf, tgt, gate_of(*seq[0]), "final_loss_bwd")
    rn, rg = {}, {}
    for idx, (l, blk) in enumerate(seq):
        r1, r2, r3 = saved[l]
        wf1, w_inT, w_out, wf2 = weights[l]
        nxt = gate_of(*seq[idx + 1]) if idx + 1 < len(seq) else None
        rg[blk] = red_g
        tag = f"l{l}{blk}"

        def on(arrays, l=l, blk=blk):
            return on_block_grads(l, blk, arrays)

        if blk == "f2":
            outs, then = _ffn_bwd(dh, do, r3, ngs[l], 2, mods[l], 6, *wf2, on, nxt, tag)
        elif blk == "mx":
            outs, d_sw, d_svec, then = _mixer_bwd(dh, do, r2, ngs[l], mods[l], (w_inT, (0,)), (w_out, (0,)), sgus[l],
                                                  cos, sin, on, nxt, tag)
        else:
            outs, then = _ffn_bwd(dh, do, r1, ngs[l], 0, mods[l], 0, *wf1, on, nxt, tag)
        dh, rn[blk] = outs[0], outs[1]
        if nxt is not None:
            do, red_g = outs[2], outs[3]
        if blk == "f1":
            mods = mods + on_layer_small(l, dict(sgu_w=d_sw, sgu_vec=d_svec, red_n=(rn["f1"], rn["mx"], rn["f2"]),
                                                 red_g=(rg["f1"], rg["mx"], rg["f2"])),
                                         red_final if l == 0 else None)
            mods = mods + then(mods)
        else:
            mods = mods + then(dh)
    return dh


def _adam_out(w, g, m, v, name):
    shp = w.shape
    two_d = (-1, shp[-1])
    d, mn, vn = adamw(w.reshape(two_d), g.reshape(two_d), m.reshape(two_d), v.reshape(two_d), name)
    return g, d.reshape(shp), mn.reshape(shp), vn.reshape(shp)


def kernel(x, c, ada_w, ada_b, norm_g, ffn1_wg, ffn1_wu, ffn1_wd, ffn2_wg, ffn2_wu, ffn2_wd, w_in, sgu_ln_g, sgu_ln_b, sgu_w, sgu_b, w_out, final_g, loss_target, m_ada_w, m_ada_b, m_norm_g, m_ffn1_wg, m_ffn1_wu, m_ffn1_wd, m_ffn2_wg, m_ffn2_wu, m_ffn2_wd, m_w_in, m_sgu_ln_g, m_sgu_ln_b, m_sgu_w, m_sgu_b, m_w_out, m_final_g, v_ada_w, v_ada_b, v_norm_g, v_ffn1_wg, v_ffn1_wu, v_ffn1_wd, v_ffn2_wg, v_ffn2_wu, v_ffn2_wd, v_w_in, v_sgu_ln_g, v_sgu_ln_b, v_sgu_w, v_sgu_b, v_w_out, v_final_g):
    T, D = x.shape[1], x.shape[2]
    NL = ada_w.shape[0]
    mx, my, mc = _my_place()
    me = 4 * mx + 2 * my + mc
    ci = 2 * mx + my
    c_idx = jnp.reshape(mc, (1,)).astype(jnp.int32)
    place = jnp.stack([ci, mc]).astype(jnp.int32)

    ngw = norm_g.shape[2]
    small_in = jnp.concatenate([jnp.pad(c, ((0, 7), (0, 0))),
                                jnp.pad(norm_g.reshape(NL * 3, ngw), ((0, 8 - NL * 3), (0, D - ngw)))], axis=0)
    small_all, _ = gather_small(small_in, place, "gather_c_normg")
    c_all = small_all[:, 0, :]
    ng_parts = small_all[0::2, 8:8 + NL * 3, :ngw]
    ngs = jnp.transpose(ng_parts, (1, 0, 2)).reshape(NL, 3, N_CHIP * ngw)

    nmod = ada_w.shape[2]
    ada_b_mine = lax.dynamic_slice_in_dim(ada_b, ci * nmod, nmod, axis=1).reshape(NL, 1, nmod)
    mod_part = ada_fwd(c_all, ada_w, ada_b_mine, "ada_fwd")
    mod_all, _ = gather_small(mod_part.reshape(NL * N_DEV, nmod), place, "gather_mod")
    mod_rows = lax.dynamic_index_in_dim(mod_all.reshape(N_DEV, NL, N_DEV, nmod), me, axis=2, keepdims=False)
    mods = jnp.transpose(mod_rows[0::2], (1, 0, 2)).reshape(NL, N_ADA, D)

    sgus = []
    for l in range(NL):
        sgus.append((sgu_ln_g[l].reshape(1, MIX_HALF), sgu_ln_b[l].reshape(1, MIX_HALF), sgu_w[l],
                     jnp.swapaxes(sgu_w[l], 1, 2), jnp.transpose(sgu_b[l])))

    def halves(a):
        n, r, _ = a.shape
        return a.reshape(n, 2, r // 2, D)

    first_group = [halves(jnp.stack([ffn1_wg[0].T, ffn1_wu[0].T], axis=0).astype(BF16))]
    first_handles, first_token = gather_start([first_group], mods, "gather_start_first")
    zero = first_token[0, 0]
    mods = mods + zero

    def prep(a):
        return (a + zero).astype(BF16)

    groups = []
    for l in range(NL):
        groups += [[halves(jnp.stack([prep(ffn1_wg[l].T), prep(ffn1_wu[l].T)], axis=0))],
                   [halves(prep(ffn1_wd[l])[None])],
                   [halves(prep(w_in[l].T)[None]), halves(prep(w_out[l])[None])],
                   [halves(jnp.stack([prep(ffn2_wg[l].T), prep(ffn2_wu[l].T)], axis=0))],
                   [halves(prep(ffn2_wd[l])[None])]]
    handles, token = gather_start(groups[1:], mods, "gather_start")
    handles = first_handles + handles
    mods = mods + token[0, 0]
    group_no = {"f1u": 0, "f1d": 1, "mx": 2, "f2u": 3, "f2d": 4}

    def get_w(l, key, after):
        full = gather_wait(handles[len(group_no) * l + group_no[key]], after, f"gather_wait_l{l}{key}")
        full = [a.reshape(a.shape[0], N_CHIP * 2 * a.shape[3], D) for a in full]
        return full[0] if key != "mx" else tuple(full)

    def split(a):
        n, r4, _ = a.shape
        return a.reshape(n, N_CHIP, 2, r4 // N_CHIP // 2, D)

    pending, small_pending, small_tokens = {}, {}, {}

    def on_block_grads(l, blk, bufs):
        tag = f"l{l}{blk}"
        sib, tok1 = sibling_start([split(g) for g in bufs], place, f"rs_sibling_start_{tag}")

        def then(after):
            parts, lands = sibling_wait(sib, after, f"rs_sibling_wait_{tag}")
            psums = [sum_halves(g, ld, c_idx, f"rs_sum_halves_{tag}_{i}") for i, (g, ld) in enumerate(zip(parts, lands))]
            pending[(l, blk)], tok2 = scatter_start(psums, lands[0], f"rs_chips_start_{tag}")
            return tok2[0, 0]

        return tok1[0, 0], then

    def blocks_finish(blocks, after, tag):
        ssums, counts = [], []
        for l, blk in blocks:
            psums, lands2 = scatter_wait(pending.pop((l, blk)), after, f"rs_chips_wait_l{l}{blk}")
            ssums += [sum_chips(p, ld, place, f"rs_sum_chips_l{l}{blk}_{i}") for i, (p, ld) in enumerate(zip(psums, lands2))]
            counts.append(len(psums))
        fins = [f.reshape(f.shape[0], -1, D) for f in sibling_complete(ssums, f"rs_complete_{tag}")]
        out, i = [], 0
        for n in counts:
            out.append(fins[i:i + n])
            i += n
        return out

    def on_layer_small(l, grads, red_final):
        blocks = list(grads["red_n"]) + list(grads["red_g"])
        blocks.append(jnp.pad(grads["sgu_vec"], ((0, 0), (0, D - MIX_HALF))))
        blocks.append(grads["sgu_w"].reshape(-1, D))
        if red_final is not None:
            blocks.append(red_final)
        xs = jnp.concatenate(blocks, axis=0)
        small_pending[l], small_tokens[l] = small_start(xs, place, f"small_start_l{l}")
        return small_tokens[l][0, 0]

    grad_x = _local_step(x[0], loss_target[0], mods, ngs, get_w, sgus, final_g.reshape(1, D),
                         on_block_grads, on_layer_small)

    adam_state = {}

    def adam_big(nm, l, g, w, m, v):
        adam_state[nm] = adamw_layer(w, g, m, v, l, adam_state.get(nm), f"adamw_{nm}_l{l}")

    def adam_block(l, blk, fin):
        if blk == "mx":
            adam_big("w_in", l, fin[0][0].T, w_in, m_w_in, v_w_in)
            adam_big("w_out", l, fin[1][0], w_out, m_w_out, v_w_out)
        else:
            ws = ((ffn1_wg, m_ffn1_wg, v_ffn1_wg), (ffn1_wu, m_ffn1_wu, v_ffn1_wu), (ffn1_wd, m_ffn1_wd, v_ffn1_wd)) \
                if blk == "f1" else \
                ((ffn2_wg, m_ffn2_wg, v_ffn2_wg), (ffn2_wu, m_ffn2_wu, v_ffn2_wu), (ffn2_wd, m_ffn2_wd, v_ffn2_wd))
            pre = "ffn1" if blk == "f1" else "ffn2"
            for k, (nm, tr) in enumerate((("wg", True), ("wu", True), ("wd", False))):
                adam_big(f"{pre}_{nm}", l, fin[0][k], *[jnp.swapaxes(t, 1, 2) if tr else t for t in ws[k]])

    done_order = [(l, blk) for l in range(NL - 1, -1, -1) for blk in ("f2", "mx", "f1")]
    for (l, blk), fin in zip(done_order[:-1], blocks_finish(done_order[:-1], small_tokens[0], "early")):
        adam_block(l, blk, fin)
    last_big = adam_state["w_out"][1]

    small_sum, small_all = [], []
    for l in range(NL):
        xs, land = small_wait(small_pending[l], last_big, f"small_wait_l{l}")
        full = lax.dynamic_update_slice(land, xs[None], (me, 0, 0))
        small_all.append(full)
        small_sum.append(sum_slots(full, f"small_sum_l{l}"))
    offs = [8 * i for i in range(8)]
    off_final = offs[7] + SGU_HEADS * ATT_BLOCK * HEAD_LANES // D
    loss = small_sum[0][off_final + 1, 0]
    g_final_g = small_sum[0][off_final, :]
    g_norm_g, g_ada_b, g_lng, g_lnb, g_sb, g_sw, dmod_all = [], [], [], [], [], [], []
    for l in range(NL):
        rn = [small_sum[l][offs[i]:offs[i] + 8] for i in range(3)]
        rg = [small_sum[l][offs[3 + i]:offs[3 + i] + 8] for i in range(3)]
        g_norm_g.append(jnp.stack([rn[i][2] for i in range(3)], axis=0))
        g_ada_b.append(jnp.concatenate([jnp.stack([rn[i][0], rn[i][1], rg[i][0]], axis=0) for i in range(3)],
                                       axis=0).reshape(N_ADA * D))
        sv = small_sum[l][offs[6]:offs[6] + 8, :MIX_HALF]
        g_lng.append(sv[0].reshape(SGU_HEADS, HEAD_LANES))
        g_lnb.append(sv[1].reshape(SGU_HEADS, HEAD_LANES))
        g_sb.append(sv[2].reshape(SGU_HEADS, ATT_BLOCK))
        g_sw.append(small_sum[l][offs[7]:off_final].reshape(sgu_w.shape[1:]))
        rows = []
        for i in range(3):
            an = small_all[l][:, offs[i]:offs[i] + 2]
            ag = small_all[l][:, offs[3 + i]:offs[3 + i] + 1]
            rows += [an[:, 0], an[:, 1], ag[:, 0]]
        dmod_all.append(jnp.stack(rows, axis=1).reshape(N_DEV, N_ADA * D))
    dmod_all = jnp.stack(dmod_all, axis=0)
    dmod_mine = lax.dynamic_slice_in_dim(dmod_all, ci * nmod, nmod, axis=2)
    g_ada_w = ada_bwd(jnp.transpose(c_all), dmod_mine, "ada_bwd")
    g_ada_b = jnp.stack(g_ada_b, axis=0)
    g_norm_g_full = jnp.stack(g_norm_g, axis=0)
    g_norm_g_mine = lax.dynamic_slice_in_dim(g_norm_g_full, ci * ngw, ngw, axis=2)

    small_params = [
        ("ada_w", ada_w, g_ada_w, m_ada_w, v_ada_w),
        ("ada_b", ada_b, g_ada_b, m_ada_b, v_ada_b),
        ("norm_g", norm_g, g_norm_g_mine, m_norm_g, v_norm_g),
        ("sgu_ln_g", sgu_ln_g, jnp.stack(g_lng, axis=0), m_sgu_ln_g, v_sgu_ln_g),
        ("sgu_ln_b", sgu_ln_b, jnp.stack(g_lnb, axis=0), m_sgu_ln_b, v_sgu_ln_b),
        ("sgu_w", sgu_w, jnp.stack(g_sw, axis=0), m_sgu_w, v_sgu_w),
        ("sgu_b", sgu_b, jnp.stack(g_sb, axis=0), m_sgu_b, v_sgu_b),
        ("final_g", final_g.reshape(1, D), g_final_g.reshape(1, D), m_final_g.reshape(1, D), v_final_g.reshape(1, D)),
    ]
    for nm, w, g, m, v in small_params:
        res = _adam_out(w, g, m, v, f"adamw_{nm}")
        adam_state[nm] = tuple(t.reshape(D) for t in res) if nm == "final_g" else res

    l, blk = done_order[-1]
    adam_block(l, blk, blocks_finish([(l, blk)], [st[1] for st in adam_state.values()], "last")[0])

    names = ["ada_w", "ada_b", "norm_g", "ffn1_wg", "ffn1_wu", "ffn1_wd", "ffn2_wg", "ffn2_wu", "ffn2_wd", "w_in",
             "sgu_ln_g", "sgu_ln_b", "sgu_w", "sgu_b", "w_out", "final_g"]
    shapes = [t.shape for t in (ada_w, ada_b, norm_g, ffn1_wg, ffn1_wu, ffn1_wd, ffn2_wg, ffn2_wu, ffn2_wd, w_in,
                                sgu_ln_g, sgu_ln_b, sgu_w, sgu_b, w_out, final_g)]
    def shaped(nm, t, s):
        if nm in ("ffn1_wg", "ffn1_wu", "ffn2_wg", "ffn2_wu"):
            return jnp.swapaxes(t.reshape(s[0], s[2], s[1]), 1, 2)
        return t.reshape(s)

    return (loss, grad_x[None], *[shaped(nm, adam_state[nm][i], s) for i in range(4) for nm, s in zip(names, shapes)])
```

```python
import math

import jax
import jax.numpy as jnp
from jax import lax
from jax.experimental import pallas as pl
from jax.experimental.pallas import tpu as pltpu

F32 = jnp.float32
BF16 = jnp.bfloat16
EPS = 1e-6
SGU_HEADS = 4
HEAD_LANES = 128
ATT_DH = 64
ATT_BLOCK = 128
MIX_HALF = SGU_HEADS * HEAD_LANES
DILATIONS = (1, 4, 16)
ROPE_THETA = 10000.0
N_ADA = 9
ADAM_LR, ADAM_B1, ADAM_B2, ADAM_EPS, ADAM_WD, ADAM_STEP = 0.001, 0.9, 0.999, 1e-08, 0.01, 10
NEG = -1e30
V7X_VMEM_BYTES = 64 * 1024 * 1024
VMEM_LIMIT = V7X_VMEM_BYTES * 7 // 8
MESH = pl.DeviceIdType.MESH
N_DEV = 8
N_CHIP = 4


def _tile(n, cap, mult):
    if n <= cap:
        return n
    t = (cap // mult) * mult
    while t >= mult:
        if n % t == 0:
            return t
        t -= mult
    raise ValueError((n, cap, mult))


def _params(dims=None):
    return pltpu.CompilerParams(dimension_semantics=dims, vmem_limit_bytes=VMEM_LIMIT)


def _wspec(w, rows, idx, resident=False):
    arr, lead = w
    kw = dict(pipeline_mode=pl.Buffered(1)) if resident else {}
    return pl.BlockSpec((None,) * len(lead) + (rows, arr.shape[-1]), lambda *g: tuple(lead) + (idx(*g), 0), **kw)


def _wrows(w):
    return w[0].shape[-2]


def _nt(a, b):
    return lax.dot_general(a, b, (((1,), (1,)), ((), ())), preferred_element_type=F32)


def _tn(a, b):
    return lax.dot_general(a, b, (((0,), (0,)), ((), ())), preferred_element_type=F32)


def _nn(a, b):
    return jnp.dot(a, b, preferred_element_type=F32)


def _sigmoid(x):
    return 0.5 * jnp.tanh(0.5 * x) + 0.5


_GELU_K = math.sqrt(2.0 / math.pi)
_GELU_C = 0.044715


def _gelu(x):
    t = jnp.tanh(_GELU_K * (x + _GELU_C * x * x * x))
    return 0.5 * x * (1.0 + t)


def _gelu_and_grad(x):
    x2 = x * x
    t = jnp.tanh(_GELU_K * (x + _GELU_C * x * x2))
    g = 0.5 * x * (1.0 + t)
    dg = 0.5 * (1.0 + t) + 0.5 * x * (1.0 - t * t) * (_GELU_K * (1.0 + 3.0 * _GELU_C * x2))
    return g, dg


def normmod_fwd(h, ng, i_n, mod, i_sh, i_sc, name):
    T, D = h.shape
    tm = _tile(T, 512, 8)

    def body(h_ref, ng_ref, mod_ref, y_ref):
        y_ref[...] = _normmod(h_ref[...], ng_ref[i_n:i_n + 1, :], mod_ref[i_sh:i_sh + 1, :],
                              mod_ref[i_sc:i_sc + 1, :]).astype(BF16)

    return pl.pallas_call(
        body, name=name, grid=(T // tm,),
        in_specs=[pl.BlockSpec((tm, D), lambda i: (i, 0)),
                  pl.BlockSpec(ng.shape, lambda i: (0, 0)),
                  pl.BlockSpec(mod.shape, lambda i: (0, 0))],
        out_specs=pl.BlockSpec((tm, D), lambda i: (i, 0)),
        out_shape=jax.ShapeDtypeStruct((T, D), BF16),
        compiler_params=_params(("parallel",)),
    )(h, ng, mod)


def ffn_up(y, wgT, wuT, name):
    T, D = y.shape
    F = _wrows(wgT)
    tm = _tile(T, 512, 16)
    tf = _tile(F, 2816, 256)
    cuts = list(range(0, tf, 768)) + [tf]

    def body(y_ref, wg_ref, wu_ref, p_ref, q_ref, s_ref):
        yv = y_ref[...]
        for c0, c1 in zip(cuts[:-1], cuts[1:]):
            a = _nt(yv, wg_ref[c0:c1, :])
            b = _nt(yv, wu_ref[c0:c1, :])
            sig = _sigmoid(a)
            q = a * sig
            p_ref[:, c0:c1] = (b * (sig + q * (1.0 - sig))).astype(BF16)
            q_ref[:, c0:c1] = q.astype(BF16)
            s_ref[:, c0:c1] = (q * b).astype(BF16)

    act = jax.ShapeDtypeStruct((T, F), BF16)
    return pl.pallas_call(
        body, name=name, grid=(F // tf, T // tm),
        in_specs=[pl.BlockSpec((tm, D), lambda j, i: (i, 0)),
                  _wspec(wgT, tf, lambda j, i: j, resident=True),
                  _wspec(wuT, tf, lambda j, i: j, resident=True)],
        out_specs=[pl.BlockSpec((tm, tf), lambda j, i: (i, j))] * 3,
        out_shape=[act, act, act],
        compiler_params=_params(("parallel", "parallel")),
    )(y, wgT[0], wuT[0])


def _normmod(x, gn, sh, sc):
    r = lax.rsqrt(jnp.mean(x * x, axis=-1, keepdims=True) + EPS)
    return ((x * r) * gn) * (1.0 + sc) + sh


def resid_matmul(xs, w, h, mod, i_g, coef, name, norm_next=None):
    T, D = h.shape
    kb = xs[0].shape[1]
    assert all(x.shape == (T, kb) for x in xs) and _wrows(w) == kb * len(xs)
    tm = _tile(T, 1024, 16)
    nx = len(xs)
    n_in, n_out, n_shape, n_ops = [], [], [], []
    if norm_next:
        ng_n, i_n, mod_n, i_sh, i_sc = norm_next
        n_in = [pl.BlockSpec(ng_n.shape, lambda i: (0, 0)), pl.BlockSpec(mod_n.shape, lambda i: (0, 0))]
        n_out = [pl.BlockSpec((tm, D), lambda i: (i, 0))]
        n_shape = [jax.ShapeDtypeStruct((T, D), BF16)]
        n_ops = [ng_n, mod_n]

    def body(*refs):
        x_refs, w_refs = refs[:nx], refs[nx:2 * nx]
        h_ref, mod_ref = refs[2 * nx:2 * nx + 2]
        hn_ref, o_ref = refs[2 * nx + 2 + len(n_in):2 * nx + 4 + len(n_in)]
        o = _nn(x_refs[0][...], w_refs[0][...])
        for xr, wr in zip(x_refs[1:], w_refs[1:]):
            o = o + _nn(xr[...], wr[...])
        o_ref[...] = o.astype(BF16)
        hn = h_ref[...] + (coef * mod_ref[i_g:i_g + 1, :]) * o
        hn_ref[...] = hn
        if norm_next:
            ng_ref, modn_ref = refs[2 * nx + 2], refs[2 * nx + 3]
            refs[-1][...] = _normmod(hn, ng_ref[i_n:i_n + 1, :], modn_ref[i_sh:i_sh + 1, :],
                                     modn_ref[i_sc:i_sc + 1, :]).astype(BF16)

    return pl.pallas_call(
        body, name=name, grid=(T // tm,),
        in_specs=([pl.BlockSpec((tm, kb), lambda i: (i, 0))] * nx
                  + [_wspec(w, kb, lambda i, p=p: p, resident=True) for p in range(nx)]
                  + [pl.BlockSpec((tm, D), lambda i: (i, 0)),
                     pl.BlockSpec(mod.shape, lambda i: (0, 0))] + n_in),
        out_specs=[pl.BlockSpec((tm, D), lambda i: (i, 0))] * 2 + n_out,
        out_shape=[jax.ShapeDtypeStruct((T, D), F32), jax.ShapeDtypeStruct((T, D), BF16)] + n_shape,
        compiler_params=_params(("parallel",)),
    )(*xs, *([w[0]] * nx), h, mod, *n_ops)


def _gate_specs(gate, tm, D):
    o, mod, _, _ = gate
    T = o.shape[0]
    return ([pl.BlockSpec((tm, D), lambda i: (i, 0)), pl.BlockSpec(mod.shape, lambda i: (0, 0))],
            [pl.BlockSpec((tm, D), lambda i: (i, 0)), pl.BlockSpec((8, D), lambda i: (0, 0))],
            [jax.ShapeDtypeStruct((T, D), BF16), jax.ShapeDtypeStruct((8, D), F32)],
            [o, mod])


def _gate_emit(d, gate, o_ref, mod_ref, do_ref, red_ref):
    _, _, i_g, coef = gate
    do_ref[...] = (d * (coef * mod_ref[i_g:i_g + 1, :])).astype(BF16)

    @pl.when(pl.program_id(0) == 0)
    def _():
        red_ref[...] = jnp.zeros_like(red_ref)

    red_ref[0:1, :] += coef * jnp.sum(d * o_ref[...].astype(F32), axis=0, keepdims=True)


def ffn_bwd_mid(do, wd, p, q, name):
    T, D = do.shape
    F = _wrows(wd)
    tm = _tile(T, 512, 16)
    tf = _tile(F, 2816, 256)
    cuts = list(range(0, tf, 256)) + [tf]

    def body(do_ref, wd_ref, p_ref, q_ref, da_ref, db_ref):
        dov = do_ref[...]
        for c0, c1 in zip(cuts[:-1], cuts[1:]):
            ds = _nt(dov, wd_ref[c0:c1, :])
            da_ref[:, c0:c1] = (ds * p_ref[:, c0:c1].astype(F32)).astype(BF16)
            db_ref[:, c0:c1] = (ds * q_ref[:, c0:c1].astype(F32)).astype(BF16)

    act = jax.ShapeDtypeStruct((T, F), BF16)
    return pl.pallas_call(
        body, name=name, grid=(F // tf, T // tm),
        in_specs=[pl.BlockSpec((tm, D), lambda j, i: (i, 0)),
                  _wspec(wd, tf, lambda j, i: j, resident=True),
                  pl.BlockSpec((tm, tf), lambda j, i: (i, j)),
                  pl.BlockSpec((tm, tf), lambda j, i: (i, j))],
        out_specs=[pl.BlockSpec((tm, tf), lambda j, i: (i, j))] * 2,
        out_shape=[act, act],
        compiler_params=_params(("parallel", "parallel")),
    )(do, wd[0], p, q)


def dy_normbwd(pairs, h, dhp, ng, i_n, mod, i_sc, name, gate=None):
    T, D = h.shape
    tm = _tile(T, 512, 16)
    npair = len(pairs)
    g_in, g_out, g_shape, g_ops = _gate_specs(gate, tm, D) if gate else ([], [], [], [])

    def body(*refs):
        x_refs, w_refs = refs[:npair], refs[npair:2 * npair]
        h_ref, dhp_ref, ng_ref, mod_ref = refs[2 * npair:2 * npair + 4]
        dh_ref, red_ref = refs[2 * npair + 4 + len(g_in):2 * npair + 6 + len(g_in)]
        dy = _nn(x_refs[0][...], w_refs[0][...])
        for xr, wr in zip(x_refs[1:], w_refs[1:]):
            dy = dy + _nn(xr[...], wr[...])
        x = h_ref[...]
        r = lax.rsqrt(jnp.mean(x * x, axis=-1, keepdims=True) + EPS)
        n = x * r
        gn = ng_ref[i_n:i_n + 1, :]
        dnh = dy * (1.0 + mod_ref[i_sc:i_sc + 1, :])

        @pl.when(pl.program_id(0) == 0)
        def _():
            red_ref[...] = jnp.zeros_like(red_ref)

        red_ref[0:1, :] += jnp.sum(dy, axis=0, keepdims=True)
        red_ref[1:2, :] += jnp.sum(dy * (n * gn), axis=0, keepdims=True)
        red_ref[2:3, :] += jnp.sum(dnh * n, axis=0, keepdims=True)
        dn = dnh * gn
        dh_new = dhp_ref[...] + r * (dn - n * jnp.mean(dn * n, axis=-1, keepdims=True))
        dh_ref[...] = dh_new
        if gate:
            _gate_emit(dh_new, gate, refs[2 * npair + 4], refs[2 * npair + 5], refs[-2], refs[-1])

    in_specs = ([pl.BlockSpec((tm, kb), lambda i, c=c: (i, c)) for (_, c, _, _, kb) in pairs]
                + [_wspec(w, kb, lambda i, r=r: r, resident=True) for (_, _, w, r, kb) in pairs]
                + [pl.BlockSpec((tm, D), lambda i: (i, 0)),
                   pl.BlockSpec((tm, D), lambda i: (i, 0)),
                   pl.BlockSpec(ng.shape, lambda i: (0, 0)),
                   pl.BlockSpec(mod.shape, lambda i: (0, 0))] + g_in)
    return pl.pallas_call(
        body, name=name, grid=(T // tm,), in_specs=in_specs,
        out_specs=[pl.BlockSpec((tm, D), lambda i: (i, 0)), pl.BlockSpec((8, D), lambda i: (0, 0))] + g_out,
        out_shape=[jax.ShapeDtypeStruct((T, D), F32), jax.ShapeDtypeStruct((8, D), F32)] + g_shape,
        compiler_params=_params(("arbitrary",)),
    )(*[p[0] for p in pairs], *[p[2][0] for p in pairs], h, dhp, ng, mod, *g_ops)


def matmul_tn(a, b, buf, slot, row0, name, tmo_cap=1408):
    T, N = b.shape
    ma = a.shape[1]
    tmo = _tile(ma, tmo_cap, 128)
    assert row0 % tmo == 0
    nmo = ma // tmo
    tk = _tile(T, 2048, 16)
    nk = T // tk

    def body(a_ref, b_ref, buf_ref, o_ref, acc_ref):
        k = pl.program_id(1)

        @pl.when(k == 0)
        def _():
            acc_ref[...] = jnp.zeros_like(acc_ref)

        acc_ref[...] += _tn(a_ref[...], b_ref[...])

        @pl.when(k == nk - 1)
        def _():
            o_ref[...] = acc_ref[...].astype(BF16)

    return pl.pallas_call(
        body, name=name, grid=(nmo, nk),
        in_specs=[pl.BlockSpec((tk, tmo), lambda j, k: (k, j)),
                  pl.BlockSpec((tk, N), lambda j, k: (k, 0)),
                  pl.BlockSpec(memory_space=pl.ANY)],
        out_specs=pl.BlockSpec((None, tmo, N), lambda j, k: (slot, row0 // tmo + j, 0)),
        out_shape=jax.ShapeDtypeStruct(buf.shape, BF16),
        scratch_shapes=[pltpu.VMEM((tmo, N), F32)],
        input_output_aliases={2: 0},
        compiler_params=_params(("parallel", "arbitrary")),
    )(a, b, buf)


def matmul_nt(x, w, name):
    T, K = x.shape
    N = _wrows(w)
    tm = _tile(T, 1024, 16)
    tn = _tile(N, 1280, 128)

    def body(x_ref, w_ref, o_ref):
        o_ref[...] = _nt(x_ref[...], w_ref[...]).astype(BF16)

    return pl.pallas_call(
        body, name=name, grid=(N // tn, T // tm),
        in_specs=[pl.BlockSpec((tm, K), lambda j, i: (i, 0)), _wspec(w, tn, lambda j, i: j)],
        out_specs=pl.BlockSpec((tm, tn), lambda j, i: (i, j)),
        out_shape=jax.ShapeDtypeStruct((T, N), BF16),
        compiler_params=_params(("parallel", "parallel")),
    )(x, w[0])


def _sgu_head_fwd(u, v, lng, lnb):
    gu, dgu = _gelu_and_grad(u)
    gv, dgv = _gelu_and_grad(v)
    mu = jnp.mean(gv, axis=-1, keepdims=True)
    xc = gv - mu
    rstd = lax.rsqrt(jnp.mean(xc * xc, axis=-1, keepdims=True) + EPS)
    xhat = xc * rstd
    vn = xhat * lng + lnb
    return gu, dgu, dgv, rstd, xhat, vn


def _tril_mask():
    r = lax.broadcasted_iota(jnp.int32, (ATT_BLOCK, ATT_BLOCK), 0)
    c = lax.broadcasted_iota(jnp.int32, (ATT_BLOCK, ATT_BLOCK), 1)
    return c <= r


def _triu_mask():
    r = lax.broadcasted_iota(jnp.int32, (ATT_BLOCK, ATT_BLOCK), 0)
    c = lax.broadcasted_iota(jnp.int32, (ATT_BLOCK, ATT_BLOCK), 1)
    return r <= c


def sgu_fwd(proj, lng, lnb, w, bcol, name):
    T = proj.shape[0]
    tm = _tile(T, 512, 128)
    nch = tm // ATT_BLOCK

    def body(u_ref, v_ref, lng_ref, lnb_ref, w_ref, b_ref, o_ref):
        tril = _tril_mask()
        for hd in range(SGU_HEADS):
            sl = slice(hd * HEAD_LANES, (hd + 1) * HEAD_LANES)
            u = u_ref[:, sl].astype(F32)
            v = v_ref[:, sl].astype(F32)
            gu, _, _, _, _, vn = _sgu_head_fwd(u, v, lng_ref[:, sl], lnb_ref[:, sl])
            wm = jnp.where(tril, w_ref[hd], 0.0).astype(BF16)
            vnb = vn.astype(BF16)
            bc = b_ref[:, hd:hd + 1]
            for ch in range(nch):
                rs = slice(ch * ATT_BLOCK, (ch + 1) * ATT_BLOCK)
                z = _nn(wm, vnb[rs, :]) + bc
                o_ref[rs, sl] = (gu[rs, :] * z).astype(BF16)

    return pl.pallas_call(
        body, name=name, grid=(T // tm,),
        in_specs=[pl.BlockSpec((tm, MIX_HALF), lambda i: (i, 0)),
                  pl.BlockSpec((tm, MIX_HALF), lambda i: (i, 1)),
                  pl.BlockSpec((1, MIX_HALF), lambda i: (0, 0)),
                  pl.BlockSpec((1, MIX_HALF), lambda i: (0, 0)),
                  pl.BlockSpec(w.shape, lambda i: (0, 0, 0)),
                  pl.BlockSpec(bcol.shape, lambda i: (0, 0))],
        out_specs=pl.BlockSpec((tm, MIX_HALF), lambda i: (i, 0)),
        out_shape=jax.ShapeDtypeStruct((T, MIX_HALF), BF16),
        compiler_params=_params(("parallel",)),
    )(proj, proj, lng, lnb, w, bcol)


def sgu_bwd(proj, dmixed, lng, lnb, w, wt, bcol, name):
    T = proj.shape[0]
    tm = _tile(T, 512, 128)
    nch = tm // ATT_BLOCK
    nsteps = T // tm

    def body(u_ref, v_ref, g_ref, lng_ref, lnb_ref, w_ref, wt_ref, b_ref, duv_ref, dw_ref, dvec_ref, bacc_ref):
        step = pl.program_id(0)

        @pl.when(step == 0)
        def _():
            dw_ref[...] = jnp.zeros_like(dw_ref)
            dvec_ref[...] = jnp.zeros_like(dvec_ref)
            bacc_ref[...] = jnp.zeros_like(bacc_ref)

        tril = _tril_mask()
        triu = _triu_mask()
        for hd in range(SGU_HEADS):
            sl = slice(hd * HEAD_LANES, (hd + 1) * HEAD_LANES)
            u = u_ref[:, sl].astype(F32)
            v = v_ref[:, sl].astype(F32)
            lng_h = lng_ref[:, sl]
            gu, dgu, dgv, rstd, xhat, vn = _sgu_head_fwd(u, v, lng_h, lnb_ref[:, sl])
            wm = jnp.where(tril, w_ref[hd], 0.0).astype(BF16)
            wmt = jnp.where(triu, wt_ref[hd], 0.0).astype(BF16)
            vnb = vn.astype(BF16)
            bc = b_ref[:, hd:hd + 1]
            g = g_ref[:, sl].astype(F32)
            dw_acc = jnp.zeros((ATT_BLOCK, ATT_BLOCK), F32)
            b_acc = jnp.zeros((ATT_BLOCK, HEAD_LANES), F32)
            dvn_parts = []
            for ch in range(nch):
                rs = slice(ch * ATT_BLOCK, (ch + 1) * ATT_BLOCK)
                z = _nn(wm, vnb[rs, :]) + bc
                duv_ref[rs, sl] = (g[rs, :] * z * dgu[rs, :]).astype(BF16)
                dz = g[rs, :] * gu[rs, :]
                dzb = dz.astype(BF16)
                dvn_parts.append(_nn(wmt, dzb))
                dw_acc = dw_acc + _nt(dzb, vnb[rs, :])
                b_acc = b_acc + dz
            dvn = jnp.concatenate(dvn_parts, axis=0)
            dw_ref[hd] += jnp.where(tril, dw_acc, 0.0)
            bacc_ref[hd] += b_acc
            dvec_ref[0:1, sl] += jnp.sum(dvn * xhat, axis=0, keepdims=True)
            dvec_ref[1:2, sl] += jnp.sum(dvn, axis=0, keepdims=True)
            dxh = dvn * lng_h
            dgv_in = rstd * (dxh - jnp.mean(dxh, axis=-1, keepdims=True)
                             - xhat * jnp.mean(dxh * xhat, axis=-1, keepdims=True))
            duv_ref[:, MIX_HALF + hd * HEAD_LANES:MIX_HALF + (hd + 1) * HEAD_LANES] = (dgv_in * dgv).astype(BF16)

        @pl.when(step == nsteps - 1)
        def _():
            for hd in range(SGU_HEADS):
                sl = slice(hd * HEAD_LANES, (hd + 1) * HEAD_LANES)
                dvec_ref[2:3, sl] = jnp.sum(bacc_ref[hd].T, axis=0, keepdims=True)

    return pl.pallas_call(
        body, name=name, grid=(nsteps,),
        in_specs=[pl.BlockSpec((tm, MIX_HALF), lambda i: (i, 0)),
                  pl.BlockSpec((tm, MIX_HALF), lambda i: (i, 1)),
                  pl.BlockSpec((tm, MIX_HALF), lambda i: (i, 0)),
                  pl.BlockSpec((1, MIX_HALF), lambda i: (0, 0)),
                  pl.BlockSpec((1, MIX_HALF), lambda i: (0, 0)),
                  pl.BlockSpec(w.shape, lambda i: (0, 0, 0)),
                  pl.BlockSpec(w.shape, lambda i: (0, 0, 0)),
                  pl.BlockSpec(bcol.shape, lambda i: (0, 0))],
        out_specs=[pl.BlockSpec((tm, 2 * MIX_HALF), lambda i: (i, 0)),
                   pl.BlockSpec(w.shape, lambda i: (0, 0, 0)),
                   pl.BlockSpec((8, MIX_HALF), lambda i: (0, 0))],
        out_shape=[jax.ShapeDtypeStruct((T, 2 * MIX_HALF), BF16),
                   jax.ShapeDtypeStruct(w.shape, F32),
                   jax.ShapeDtypeStruct((8, MIX_HALF), F32)],
        scratch_shapes=[pltpu.VMEM((SGU_HEADS, ATT_BLOCK, HEAD_LANES), F32)],
        compiler_params=_params(("arbitrary",)),
    )(proj, proj, dmixed, lng, lnb, w, wt, bcol)


def _rot_half(t):
    lane = lax.broadcasted_iota(jnp.int32, t.shape, 1)
    first = (lane % ATT_DH) < (ATT_DH // 2)
    return jnp.where(first, -pltpu.roll(t, HEAD_LANES - ATT_DH // 2, 1), pltpu.roll(t, ATT_DH // 2, 1))


LAYOUT_ROWS = 512


def _res_spec(d, tm, W):
    return pl.BlockSpec((d, tm // d, W), lambda i: (0, i, 0))


def _res_shape(d, T, W, dtype):
    return jax.ShapeDtypeStruct((d, T // d, W), dtype)


def _slab_buf(tm, W):
    return pltpu.VMEM((W // HEAD_LANES, tm, HEAD_LANES), F32)


def _lanes(hp):
    return slice(hp * HEAD_LANES, (hp + 1) * HEAD_LANES)


def _to_res(buf, out_ref, d, dtype):
    nslab, tm, _ = buf.shape
    for hp in range(nslab):
        if d == 1:
            out_ref[0, :, _lanes(hp)] = buf[hp].astype(dtype)
        else:
            for r in range(d):
                out_ref[r, :, _lanes(hp)] = buf.at[hp][pl.ds(r, tm // d, stride=d), :].astype(dtype)


def _from_res(in_ref, buf, d):
    nslab, tm, _ = buf.shape
    for hp in range(nslab):
        if d == 1:
            buf[hp] = in_ref[0, :, _lanes(hp)]
        else:
            for r in range(d):
                buf.at[hp][pl.ds(r, tm // d, stride=d), :] = in_ref[r, :, _lanes(hp)]


def rope_fwd(proj, cos, sin, name):
    T = proj.shape[0]
    tm = LAYOUT_ROWS
    scale = 1.0 / math.sqrt(ATT_DH)
    nd = len(DILATIONS)

    def body(q_ref, k_ref, v_ref, cos_ref, sin_ref, *rest):
        outs, buf = rest[:3 * nd], rest[3 * nd]
        c = cos_ref[...]
        s = sin_ref[...]
        for which, src in enumerate((q_ref, k_ref, v_ref)):
            for hp in range(MIX_HALF // HEAD_LANES):
                t = src[:, _lanes(hp)].astype(F32)
                if which == 0:
                    t = scale * (t * c + _rot_half(t) * s)
                elif which == 1:
                    t = t * c + _rot_half(t) * s
                buf[hp] = t
            for di, d in enumerate(DILATIONS):
                _to_res(buf, outs[3 * di + which], d, BF16)

    return pl.pallas_call(
        body, name=name, grid=(T // tm,),
        in_specs=[pl.BlockSpec((tm, MIX_HALF), lambda i: (i, 2)),
                  pl.BlockSpec((tm, MIX_HALF), lambda i: (i, 3)),
                  pl.BlockSpec((tm, MIX_HALF), lambda i: (i, 4)),
                  pl.BlockSpec((tm, HEAD_LANES), lambda i: (i, 0)),
                  pl.BlockSpec((tm, HEAD_LANES), lambda i: (i, 0))],
        out_specs=[_res_spec(d, tm, MIX_HALF) for d in DILATIONS for _ in range(3)],
        out_shape=[_res_shape(d, T, MIX_HALF, BF16) for d in DILATIONS for _ in range(3)],
        scratch_shapes=[_slab_buf(tm, MIX_HALF)],
        compiler_params=_params(("parallel",)),
    )(proj, proj, proj, cos, sin)


def to_residues(x, col, name):
    T = x.shape[0]
    tm = LAYOUT_ROWS

    def body(x_ref, *rest):
        outs, buf = rest[:-1], rest[-1]
        for hp in range(MIX_HALF // HEAD_LANES):
            buf[hp] = x_ref[:, _lanes(hp)].astype(F32)
        for o_ref, d in zip(outs, DILATIONS):
            _to_res(buf, o_ref, d, BF16)

    return pl.pallas_call(
        body, name=name, grid=(T // tm,),
        in_specs=[pl.BlockSpec((tm, MIX_HALF), lambda i: (i, col))],
        out_specs=[_res_spec(d, tm, MIX_HALF) for d in DILATIONS],
        out_shape=[_res_shape(d, T, MIX_HALF, BF16) for d in DILATIONS],
        scratch_shapes=[_slab_buf(tm, MIX_HALF)],
        compiler_params=_params(("parallel",)),
    )(x)


def rope_bwd(dqs, dks, dvs, cos, sin, name):
    T = dqs[0].shape[0] * dqs[0].shape[1]
    tm = LAYOUT_ROWS
    scale = 1.0 / math.sqrt(ATT_DH)
    npat = len(dqs)

    def body(*refs):
        groups = refs[:npat], refs[npat:2 * npat], refs[2 * npat:3 * npat]
        cos_ref, sin_ref, o_ref, buf, acc = refs[3 * npat:]
        c = cos_ref[...]
        s = sin_ref[...]
        for which, g_refs in enumerate(groups):
            _from_res(g_refs[0], acc, DILATIONS[0])
            for g_ref, d in zip(g_refs[1:], DILATIONS[1:]):
                _from_res(g_ref, buf, d)
                acc[...] += buf[...]
            for hp in range(MIX_HALF // HEAD_LANES):
                g = acc[hp]
                if which == 0:
                    g = scale * g
                if which < 2:
                    g = g * c - _rot_half(g * s)
                o_ref[:, which * MIX_HALF + hp * HEAD_LANES:which * MIX_HALF + (hp + 1) * HEAD_LANES] = g.astype(BF16)

    return pl.pallas_call(
        body, name=name, grid=(T // tm,),
        in_specs=([_res_spec(d, tm, MIX_HALF) for _ in range(3) for d in DILATIONS]
                  + [pl.BlockSpec((tm, HEAD_LANES), lambda i: (i, 0))] * 2),
        out_specs=pl.BlockSpec((tm, 3 * MIX_HALF), lambda i: (i, 0)),
        out_shape=jax.ShapeDtypeStruct((T, 3 * MIX_HALF), BF16),
        scratch_shapes=[_slab_buf(tm, MIX_HALF), _slab_buf(tm, MIX_HALF)],
        compiler_params=_params(("parallel",)),
    )(*dqs, *dks, *dvs, cos, sin)


def _band_masks(n):
    r = lax.broadcasted_iota(jnp.int32, (2 * ATT_BLOCK, ATT_BLOCK), 0)
    c = lax.broadcasted_iota(jnp.int32, (2 * ATT_BLOCK, ATT_BLOCK), 1)
    qi = r % ATT_BLOCK
    head = (c < ATT_DH) == (r < ATT_BLOCK)
    return (c >= qi) & (n > 0), c <= qi, head, c[:ATT_BLOCK] < ATT_DH


def _stack_heads(x, head):
    x2 = jnp.concatenate([x, x], axis=0)
    return jnp.where(head, x2, jnp.zeros_like(x2))


def attn_fwd(q, k, v, name):
    d, L, W = q.shape
    nb = L // ATT_BLOCK

    def body(q_ref, kp_ref, kc_ref, vp_ref, vc_ref, o_ref, lse_ref):
        mask_p, mask_c, head, head0 = _band_masks(pl.program_id(1))
        for hp in range(W // HEAD_LANES):
            sl = slice(hp * HEAD_LANES, (hp + 1) * HEAD_LANES)
            kp, kc, vp, vc = kp_ref[0, :, sl], kc_ref[0, :, sl], vp_ref[0, :, sl], vc_ref[0, :, sl]
            qs = _stack_heads(q_ref[0, :, sl], head)
            sp = jnp.where(mask_p, _nt(qs, kp), NEG)
            sc = jnp.where(mask_c, _nt(qs, kc), NEG)
            m = jnp.maximum(jnp.max(sp, axis=1, keepdims=True), jnp.max(sc, axis=1, keepdims=True))
            pp = jnp.exp(sp - m)
            pc = jnp.exp(sc - m)
            den = jnp.sum(pp, axis=1, keepdims=True) + jnp.sum(pc, axis=1, keepdims=True)
            o = (_nn(pp.astype(BF16), vp) + _nn(pc.astype(BF16), vc)) / den
            lse = m + jnp.log(den)
            o_ref[0, :, sl] = jnp.where(head0, o[:ATT_BLOCK], o[ATT_BLOCK:])
            lse_ref[0, :, sl] = jnp.where(head0, lse[:ATT_BLOCK], lse[ATT_BLOCK:])

    cur = pl.BlockSpec((1, ATT_BLOCK, W), lambda r, n: (r, n, 0))
    prev = pl.BlockSpec((1, ATT_BLOCK, W), lambda r, n: (r, jnp.maximum(n - 1, 0), 0))
    out = jax.ShapeDtypeStruct((d, L, W), F32)
    return pl.pallas_call(
        body, name=name, grid=(d, nb),
        in_specs=[cur, prev, cur, prev, cur],
        out_specs=[cur, cur], out_shape=[out, out],
        compiler_params=_params(("parallel", "parallel")),
    )(q, k, k, v, v)


def attn_combine(os_, lses, name):
    T = os_[0].shape[0] * os_[0].shape[1]
    W = os_[0].shape[2]
    tm = LAYOUT_ROWS
    npat = len(os_)

    def body(*refs):
        o_refs, l_refs = refs[:npat], refs[npat:2 * npat]
        out_ref = refs[2 * npat]
        ores, lres = refs[2 * npat + 1:3 * npat + 1], refs[3 * npat + 1:4 * npat + 1]
        bufs = refs[4 * npat + 1:]
        lbufs, obufs, out_buf, lse_buf = bufs[:npat], bufs[npat:2 * npat], bufs[2 * npat], bufs[2 * npat + 1]
        for p, d in enumerate(DILATIONS):
            _from_res(l_refs[p], lbufs[p], d)
            _from_res(o_refs[p], obufs[p], d)
        for hp in range(W // HEAD_LANES):
            ls = [b[hp] for b in lbufs]
            m = ls[0]
            for l in ls[1:]:
                m = jnp.maximum(m, l)
            es = [jnp.exp(l - m) for l in ls]
            z = es[0]
            for e in es[1:]:
                z = z + e
            acc = es[0] * obufs[0][hp]
            for p in range(1, npat):
                acc = acc + es[p] * obufs[p][hp]
            out = acc / z
            out_ref[:, _lanes(hp)] = out.astype(BF16)
            out_buf[hp] = out
            lse_buf[hp] = m + jnp.log(z)
        for p, d in enumerate(DILATIONS):
            _to_res(out_buf, ores[p], d, BF16)
            _to_res(lse_buf, lres[p], d, F32)

    return pl.pallas_call(
        body, name=name, grid=(T // tm,),
        in_specs=[_res_spec(d, tm, W) for _ in range(2) for d in DILATIONS],
        out_specs=([pl.BlockSpec((tm, W), lambda i: (i, 0))] + [_res_spec(d, tm, W) for _ in range(2) for d in DILATIONS]),
        out_shape=([jax.ShapeDtypeStruct((T, W), BF16)] + [_res_shape(d, T, W, BF16) for d in DILATIONS]
                   + [_res_shape(d, T, W, F32) for d in DILATIONS]),
        scratch_shapes=[_slab_buf(tm, W)] * (2 * npat + 2),
        compiler_params=_params(("parallel",)),
    )(*os_, *lses)


def attn_bwd(q, k, v, do, o, lse, name):
    d, L, W = q.shape
    nb = L // ATT_BLOCK

    def body(q_ref, kp_ref, kc_ref, vp_ref, vc_ref, do_ref, o_ref, lse_ref, dq_ref, dk_ref, dv_ref, kkeep, vkeep):
        n = pl.program_id(1)

        @pl.when(n == 0)
        def _():
            kkeep[...] = jnp.zeros_like(kkeep)
            vkeep[...] = jnp.zeros_like(vkeep)

        @pl.when(n < nb)
        def _():
            mask_p, mask_c, head, head0 = _band_masks(n)
            for hp in range(W // HEAD_LANES):
                sl = slice(hp * HEAD_LANES, (hp + 1) * HEAD_LANES)
                kp, kc, vp, vc = kp_ref[0, :, sl], kc_ref[0, :, sl], vp_ref[0, :, sl], vc_ref[0, :, sl]
                dout = do_ref[0, :, sl]
                qs = _stack_heads(q_ref[0, :, sl], head)
                dos = _stack_heads(dout, head)
                lse_v = lse_ref[0, :, sl]
                lse_c = jnp.max(jnp.where(head, jnp.concatenate([lse_v, lse_v], axis=0), NEG), axis=1, keepdims=True)
                delta = jnp.sum(_stack_heads(dout.astype(F32) * o_ref[0, :, sl].astype(F32), head), axis=1, keepdims=True)
                pp = jnp.exp(jnp.where(mask_p, _nt(qs, kp), NEG) - lse_c)
                pc = jnp.exp(jnp.where(mask_c, _nt(qs, kc), NEG) - lse_c)
                dsp = (pp * (_nt(dos, vp) - delta)).astype(BF16)
                dsc = (pc * (_nt(dos, vc) - delta)).astype(BF16)
                dq2 = _nn(dsp, kp) + _nn(dsc, kc)
                dq_ref[0, :, sl] = jnp.where(head0, dq2[:ATT_BLOCK], dq2[ATT_BLOCK:])
                dk_ref[0, :, sl] = kkeep[:, sl] + _tn(dsp, qs)
                dv_ref[0, :, sl] = vkeep[:, sl] + _tn(pp.astype(BF16), dos)
                kkeep[:, sl] = _tn(dsc, qs)
                vkeep[:, sl] = _tn(pc.astype(BF16), dos)

        @pl.when(n == nb)
        def _():
            dk_ref[0] = kkeep[...]
            dv_ref[0] = vkeep[...]

    cur = pl.BlockSpec((1, ATT_BLOCK, W), lambda r, n: (r, jnp.minimum(n, nb - 1), 0))
    prev = pl.BlockSpec((1, ATT_BLOCK, W), lambda r, n: (r, jnp.clip(n - 1, 0, nb - 1), 0))
    out = jax.ShapeDtypeStruct((d, L, W), F32)
    return pl.pallas_call(
        body, name=name, grid=(d, nb + 1),
        in_specs=[cur, prev, cur, prev, cur, cur, cur, cur],
        out_specs=[cur, prev, prev], out_shape=[out, out, out],
        scratch_shapes=[pltpu.VMEM((ATT_BLOCK, W), F32), pltpu.VMEM((ATT_BLOCK, W), F32)],
        compiler_params=_params(("parallel", "arbitrary")),
    )(q, k, k, v, v, do, o, lse)


def final_loss_bwd(h, gf, tgt, gate, name):
    T, D = h.shape
    tm = _tile(T, 512, 16)
    g_in, g_out, g_shape, g_ops = _gate_specs(gate, tm, D)

    def body(h_ref, g_ref, t_ref, o_ref, modg_ref, dh_ref, red_ref, do_ref, redg_ref):
        x = h_ref[...]
        r = lax.rsqrt(jnp.mean(x * x, axis=-1, keepdims=True) + EPS)
        n = x * r
        g = g_ref[...]
        err = n * g - t_ref[...]
        dy = err * (1.0 / D)

        @pl.when(pl.program_id(0) == 0)
        def _():
            red_ref[...] = jnp.zeros_like(red_ref)

        red_ref[0:1, :] += jnp.sum(dy * n, axis=0, keepdims=True)
        red_ref[1:2, :] += jnp.zeros((1, D), F32) + (0.5 / D) * jnp.sum(err * err, keepdims=True)
        dn = dy * g
        dh = r * (dn - n * jnp.mean(dn * n, axis=-1, keepdims=True))
        dh_ref[...] = dh
        _gate_emit(dh, gate, o_ref, modg_ref, do_ref, redg_ref)

    return pl.pallas_call(
        body, name=name, grid=(T // tm,),
        in_specs=[pl.BlockSpec((tm, D), lambda i: (i, 0)),
                  pl.BlockSpec((1, D), lambda i: (0, 0)),
                  pl.BlockSpec((tm, D), lambda i: (i, 0))] + g_in,
        out_specs=[pl.BlockSpec((tm, D), lambda i: (i, 0)), pl.BlockSpec((8, D), lambda i: (0, 0))] + g_out,
        out_shape=[jax.ShapeDtypeStruct((T, D), F32), jax.ShapeDtypeStruct((8, D), F32)] + g_shape,
        compiler_params=_params(("arbitrary",)),
    )(h, gf, tgt, *g_ops)


def ada_fwd(c_all, ada_w, ada_b, name):
    nl, D, N = ada_w.shape

    def body(c_ref, w_ref, b_ref, o_ref):
        c = c_ref[...]
        o_ref[0] = _nn(c * _sigmoid(c), w_ref[0]) + b_ref[0]

    return pl.pallas_call(
        body, name=name, grid=(nl,),
        in_specs=[pl.BlockSpec((N_DEV, D), lambda l: (0, 0)),
                  pl.BlockSpec((1, D, N), lambda l: (l, 0, 0)),
                  pl.BlockSpec((1, 1, N), lambda l: (l, 0, 0))],
        out_specs=pl.BlockSpec((1, N_DEV, N), lambda l: (l, 0, 0)),
        out_shape=jax.ShapeDtypeStruct((nl, N_DEV, N), F32),
        compiler_params=_params(("parallel",)),
    )(c_all, ada_w, ada_b)


def ada_bwd(c_allT, dmod, name):
    nl, _, N = dmod.shape
    D = c_allT.shape[0]

    def body(c_ref, g_ref, o_ref):
        c = c_ref[...]
        ca = c * _sigmoid(c)
        acc = ca[:, 0:1] * g_ref[0, 0:1, :]
        for b in range(1, N_DEV):
            acc = acc + ca[:, b:b + 1] * g_ref[0, b:b + 1, :]
        o_ref[0] = acc

    return pl.pallas_call(
        body, name=name, grid=(nl,),
        in_specs=[pl.BlockSpec((D, N_DEV), lambda l: (0, 0)),
                  pl.BlockSpec((1, N_DEV, N), lambda l: (l, 0, 0))],
        out_specs=pl.BlockSpec((1, D, N), lambda l: (l, 0, 0)),
        out_shape=jax.ShapeDtypeStruct((nl, D, N), F32),
        compiler_params=_params(("parallel",)),
    )(c_allT, dmod)


def adamw(w, g, m, v, name):
    R, C = w.shape
    tr = _tile(R, max(8, (1 << 19) // C // 8 * 8), 8)
    c1 = 1.0 - ADAM_B1 ** ADAM_STEP
    c2 = 1.0 - ADAM_B2 ** ADAM_STEP

    def body(w_ref, g_ref, m_ref, v_ref, d_ref, mo_ref, vo_ref):
        gv = g_ref[...]
        mn = ADAM_B1 * m_ref[...] + (1.0 - ADAM_B1) * gv
        vn = ADAM_B2 * v_ref[...] + (1.0 - ADAM_B2) * (gv * gv)
        mo_ref[...] = mn
        vo_ref[...] = vn
        d_ref[...] = -ADAM_LR * ((mn / c1) / (jnp.sqrt(vn / c2) + ADAM_EPS) + ADAM_WD * w_ref[...])

    blk = pl.BlockSpec((tr, C), lambda i: (i, 0))
    out = jax.ShapeDtypeStruct((R, C), F32)
    return pl.pallas_call(
        body, name=name, grid=(R // tr,),
        in_specs=[blk] * 4, out_specs=[blk] * 3, out_shape=[out] * 3,
        compiler_params=_params(("parallel",)),
    )(w, g, m, v)


def adamw_layer(w, g, m, v, l, prev, name):
    NLw, R, C = w.shape
    tr = _tile(R, max(8, (1 << 19) // C // 8 * 8), 8)
    nrb = R // tr
    c1 = 1.0 - ADAM_B1 ** ADAM_STEP
    c2 = 1.0 - ADAM_B2 ** ADAM_STEP
    w, m, v = (t.reshape(NLw * R, C) for t in (w, m, v))

    def body(w_ref, g_ref, m_ref, v_ref, *rest):
        go_ref, d_ref, mo_ref, vo_ref = rest[-4:]
        gv = g_ref[...]
        mn = ADAM_B1 * m_ref[...] + (1.0 - ADAM_B1) * gv
        vn = ADAM_B2 * v_ref[...] + (1.0 - ADAM_B2) * (gv * gv)
        go_ref[...] = gv
        mo_ref[...] = mn
        vo_ref[...] = vn
        d_ref[...] = -ADAM_LR * ((mn / c1) / (jnp.sqrt(vn / c2) + ADAM_EPS) + ADAM_WD * w_ref[...])

    lay = pl.BlockSpec((tr, C), lambda i: (l * nrb + i, 0))
    out = jax.ShapeDtypeStruct((NLw * R, C), F32)
    n_prev = 0 if prev is None else 4
    return pl.pallas_call(
        body, name=name, grid=(nrb,),
        in_specs=[lay, pl.BlockSpec((tr, C), lambda i: (i, 0)), lay, lay] + [pl.BlockSpec(memory_space=pl.ANY)] * n_prev,
        out_specs=[lay] * 4, out_shape=[out] * 4,
        input_output_aliases={4 + i: i for i in range(n_prev)},
        compiler_params=_params(("parallel",)),
    )(w, g, m, v, *(prev or ()))


def sum_slots(x, name):
    S, R, C = x.shape
    tr = _tile(R, 128, 8)

    def body(x_ref, o_ref):
        acc = x_ref[0]
        for s in range(1, S):
            acc = acc + x_ref[s]
        o_ref[...] = acc

    return pl.pallas_call(
        body, name=name, grid=(R // tr,),
        in_specs=[pl.BlockSpec((S, tr, C), lambda i: (0, i, 0))],
        out_specs=pl.BlockSpec((tr, C), lambda i: (i, 0)),
        out_shape=jax.ShapeDtypeStruct((R, C), F32),
        compiler_params=_params(("parallel",)),
    )(x)


def sum_halves(g, lands, c_idx, name):
    n, ns, _, rh, D = g.shape

    def body(c_ref, g_ref, l_ref, o_ref):
        o_ref[0, 0] = (g_ref[0, 0, 0].astype(F32) + l_ref[0, 0].astype(F32)).astype(BF16)

    return pl.pallas_call(
        body, name=name,
        grid_spec=pltpu.PrefetchScalarGridSpec(
            num_scalar_prefetch=1, grid=(n, ns),
            in_specs=[pl.BlockSpec((1, 1, 1, rh, D), lambda i, j, c: (i, j, c[0], 0, 0)),
                      pl.BlockSpec((1, 1, rh, D), lambda i, j, c: (i, j, 0, 0))],
            out_specs=pl.BlockSpec((1, 1, rh, D), lambda i, j, c: (i, j, 0, 0))),
        out_shape=jax.ShapeDtypeStruct((n, ns, rh, D), BF16),
        compiler_params=_params(("parallel", "parallel")),
    )(c_idx, g, lands)


def sum_chips(p, lands, place, name):
    n, ns, rh, D = p.shape

    def body(c_ref, p_ref, l_ref, o_ref):
        acc = p_ref[0, 0].astype(F32)
        for j in range(N_CHIP - 1):
            acc = acc + l_ref[j, 0].astype(F32)
        o_ref[0, 0] = acc

    return pl.pallas_call(
        body, name=name,
        grid_spec=pltpu.PrefetchScalarGridSpec(
            num_scalar_prefetch=1, grid=(n,),
            in_specs=[pl.BlockSpec((1, 1, rh, D), lambda i, c: (i, c[0], 0, 0)),
                      pl.BlockSpec((N_CHIP - 1, 1, rh, D), lambda i, c: (0, i, 0, 0))],
            out_specs=pl.BlockSpec((1, 1, rh, D), lambda i, c: (i, c[1], 0, 0))),
        out_shape=jax.ShapeDtypeStruct((n, 2, rh, D), F32),
        compiler_params=_params(("parallel",)),
    )(place, p, lands)


def _my_place():
    return lax.axis_index("x"), lax.axis_index("y"), lax.axis_index("c")


def _other_chips(mx, my):
    return [(1 - mx, my), (mx, 1 - my), (1 - mx, 1 - my)]


def gather_small(x, after, name):
    def body(x_ref, after_ref, out_ref, sum_ref, send_sems, recv_sems):
        mx, my, mc = _my_place()
        me = 4 * mx + 2 * my + mc
        out_ref[me] = x_ref[...]
        sends = []
        for k in range(1, N_DEV):
            kx, ky, kc = (k >> 2) & 1, (k >> 1) & 1, k & 1
            peer = (1 - mx if kx else mx, 1 - my if ky else my, 1 - mc if kc else mc)
            cp = pltpu.make_async_remote_copy(
                src_ref=x_ref, dst_ref=out_ref.at[me], send_sem=send_sems.at[k - 1], recv_sem=recv_sems.at[k - 1],
                device_id=peer, device_id_type=MESH)
            cp.start()
            sends.append((cp, 4 * peer[0] + 2 * peer[1] + peer[2], peer))
        for k, (cp, peer_slot, peer) in enumerate(sends):
            pltpu.make_async_remote_copy(
                src_ref=x_ref, dst_ref=out_ref.at[peer_slot], send_sem=send_sems.at[k], recv_sem=recv_sems.at[k],
                device_id=peer, device_id_type=MESH).wait_recv()
        for cp, _, _ in sends:
            cp.wait_send()
        acc = out_ref[0]
        for s in range(1, N_DEV):
            acc = acc + out_ref[s]
        sum_ref[...] = acc

    vmem = pl.BlockSpec(memory_space=pltpu.VMEM)
    return pl.pallas_call(
        body, name=name,
        in_specs=[vmem, pl.BlockSpec(memory_space=pl.ANY)], out_specs=[vmem, vmem],
        out_shape=[jax.ShapeDtypeStruct((N_DEV,) + x.shape, x.dtype), jax.ShapeDtypeStruct(x.shape, x.dtype)],
        scratch_shapes=[pltpu.SemaphoreType.DMA((N_DEV - 1,)), pltpu.SemaphoreType.DMA((N_DEV - 1,))],
        compiler_params=pltpu.CompilerParams(vmem_limit_bytes=VMEM_LIMIT),
    )(x, after)


_HBM =pl.BlockSpec(memory_space=pltpu.HBM)
_SEM = pl.BlockSpec(memory_space=pltpu.SEMAPHORE)
_DATAFLOW = pltpu.SideEffectType.DATAFLOW_SIDE_EFFECTING


def _gather_copies(shard, land, send, recv, base):
    mx, my, mc = _my_place()
    ci = 2 * mx + my
    peers = [((cx, cy, mc), 2 * cx + cy) for cx, cy in _other_chips(mx, my)] + [((mx, my, 1 - mc), ci)]
    out = []
    for q, (dev, src_slot) in enumerate(peers):
        out.append((
            pltpu.make_async_remote_copy(src_ref=shard, dst_ref=land.at[:, ci], send_sem=send.at[base + q],
                                         recv_sem=recv.at[base + q], device_id=dev, device_id_type=MESH),
            pltpu.make_async_remote_copy(src_ref=shard, dst_ref=land.at[:, src_slot], send_sem=send.at[base + q],
                                         recv_sem=recv.at[base + q], device_id=dev, device_id_type=MESH)))
    return out


def gather_start(groups, after, name):
    items = [s for g in groups for s in g]
    ni, ng = len(items), len(groups)

    def body(*refs):
        shards, lands = refs[:ni], refs[ni:2 * ni]
        sems = refs[2 * ni + 1:2 * ni + 1 + 2 * ng]
        token = refs[-1]
        i = 0
        for g, grp in enumerate(groups):
            for p in range(len(grp)):
                for start_cp, _ in _gather_copies(shards[i], lands[i], sems[2 * g], sems[2 * g + 1], 4 * p):
                    start_cp.start()
                i += 1
        token[...] = jnp.zeros_like(token)

    sem_shapes = []
    for grp in groups:
        sem_shapes += [pltpu.SemaphoreType.DMA((4 * len(grp),))] * 2
    land_shapes = [(s.shape[0], N_CHIP) + s.shape[1:] for s in items]
    outs = pl.pallas_call(
        body, name=name,
        in_specs=[_HBM] * (2 * ni) + [pl.BlockSpec(memory_space=pl.ANY)],
        out_specs=[_SEM] * (2 * ng) + [_HBM] * (2 * ni) + [pl.BlockSpec(memory_space=pltpu.VMEM)],
        out_shape=(sem_shapes + [pltpu.HBM(s.shape, s.dtype) for s in items]
                   + [pltpu.HBM(ls, s.dtype) for ls, s in zip(land_shapes, items)]
                   + [jax.ShapeDtypeStruct((8, 128), F32)]),
        input_output_aliases={i: 2 * ng + i for i in range(2 * ni)},
        compiler_params=pltpu.CompilerParams(has_side_effects=_DATAFLOW),
    )(*[pltpu.with_memory_space_constraint(s, pltpu.HBM) for s in items],
      *[pltpu.with_memory_space_constraint(lax.empty(ls, s.dtype), pltpu.HBM) for ls, s in zip(land_shapes, items)],
      after)
    sems, thru, token = outs[:2 * ng], outs[2 * ng:2 * ng + 2 * ni], outs[-1]
    handles, i = [], 0
    for g, grp in enumerate(groups):
        n = len(grp)
        handles.append((sems[2 * g], sems[2 * g + 1], thru[i:i + n], thru[ni + i:ni + i + n]))
        i += n
    return handles, token


def gather_wait(handle, after, name):
    send, recv, shards, lands = handle
    n = len(shards)

    def body(*refs):
        shard_refs, land_refs = refs[:n], refs[n:2 * n]
        send_ref, recv_ref = refs[2 * n], refs[2 * n + 1]
        for p in range(n):
            for start_cp, recv_cp in _gather_copies(shard_refs[p], land_refs[p], send_ref, recv_ref, 4 * p):
                start_cp.wait_send()
                recv_cp.wait_recv()

    outs = pl.pallas_call(
        body, name=name,
        in_specs=[_HBM] * (2 * n) + [_SEM, _SEM, pl.BlockSpec(memory_space=pl.ANY)],
        out_specs=[_HBM] * (2 * n),
        out_shape=[pltpu.HBM(s.shape, s.dtype) for s in shards] + [pltpu.HBM(l.shape, l.dtype) for l in lands],
        input_output_aliases={i: i for i in range(2 * n)},
        compiler_params=pltpu.CompilerParams(has_side_effects=_DATAFLOW),
    )(*shards, *lands, send, recv, after)
    return outs[n:]


def _sibling_copies(gs, lands, send, recv):
    mx, my, mc = _my_place()
    return [pltpu.make_async_remote_copy(
        src_ref=gs[k].at[:, :, 1 - mc], dst_ref=lands[k], send_sem=send.at[k], recv_sem=recv.at[k],
        device_id=(mx, my, 1 - mc), device_id_type=MESH) for k in range(len(gs))]


def sibling_start(gs, after, name):
    K = len(gs)

    def body(*refs):
        ins, lands = refs[:K], refs[K:2 * K]
        send, recv = refs[2 * K + 1], refs[2 * K + 2]
        for cp in _sibling_copies(ins, lands, send, recv):
            cp.start()
        refs[-1][...] = jnp.zeros_like(refs[-1])

    land_shapes = [g.shape[:2] + g.shape[3:] for g in gs]
    outs = pl.pallas_call(
        body, name=name,
        in_specs=[_HBM] * (2 * K) + [pl.BlockSpec(memory_space=pl.ANY)],
        out_specs=[_SEM, _SEM] + [_HBM] * (2 * K) + [pl.BlockSpec(memory_space=pltpu.VMEM)],
        out_shape=([pltpu.SemaphoreType.DMA((K,))] * 2 + [pltpu.HBM(g.shape, g.dtype) for g in gs]
                   + [pltpu.HBM(ls, g.dtype) for ls, g in zip(land_shapes, gs)] + [jax.ShapeDtypeStruct((8, 128), F32)]),
        input_output_aliases={i: 2 + i for i in range(2 * K)},
        compiler_params=pltpu.CompilerParams(has_side_effects=_DATAFLOW),
    )(*[pltpu.with_memory_space_constraint(g, pltpu.HBM) for g in gs],
      *[pltpu.with_memory_space_constraint(lax.empty(ls, g.dtype), pltpu.HBM) for ls, g in zip(land_shapes, gs)],
      after)
    return (outs[0], outs[1], outs[2:2 + K], outs[2 + K:2 + 2 * K]), outs[-1]


def sibling_wait(handle, after, name):
    send, recv, gs, lands = handle
    K = len(gs)

    def body(*refs):
        ins, land_refs = refs[:K], refs[K:2 * K]
        for cp in _sibling_copies(ins, land_refs, refs[2 * K], refs[2 * K + 1]):
            cp.wait_send()
            cp.wait_recv()

    outs = pl.pallas_call(
        body, name=name,
        in_specs=[_HBM] * (2 * K) + [_SEM, _SEM, pl.BlockSpec(memory_space=pl.ANY)],
        out_specs=[_HBM] * (2 * K),
        out_shape=[pltpu.HBM(g.shape, g.dtype) for g in gs] + [pltpu.HBM(l.shape, l.dtype) for l in lands],
        input_output_aliases={i: i for i in range(2 * K)},
        compiler_params=pltpu.CompilerParams(has_side_effects=_DATAFLOW),
    )(*gs, *lands, send, recv, after)
    return outs[:K], outs[K:]


def _small_copies(x, land, send, recv):
    mx, my, mc = _my_place()
    me = 4 * mx + 2 * my + mc
    out = []
    for k in range(1, N_DEV):
        peer = (1 - mx if k & 4 else mx, 1 - my if k & 2 else my, 1 - mc if k & 1 else mc)
        slot = 4 * peer[0] + 2 * peer[1] + peer[2]
        out.append(tuple(pltpu.make_async_remote_copy(
            src_ref=x, dst_ref=land.at[s], send_sem=send.at[k - 1], recv_sem=recv.at[k - 1],
            device_id=peer, device_id_type=MESH) for s in (me, slot)))
    return out


def small_start(x, after, name):
    def body(x_ref, land_ref, after_ref, send, recv, x_thru, land_thru, token):
        for mine, _ in _small_copies(x_ref, land_ref, send, recv):
            mine.start()
        token[...] = jnp.zeros_like(token)

    land_shape = (N_DEV,) + x.shape
    outs = pl.pallas_call(
        body, name=name,
        in_specs=[_HBM, _HBM, pl.BlockSpec(memory_space=pl.ANY)],
        out_specs=[_SEM, _SEM, _HBM, _HBM, pl.BlockSpec(memory_space=pltpu.VMEM)],
        out_shape=[pltpu.SemaphoreType.DMA((N_DEV - 1,))] * 2 + [pltpu.HBM(x.shape, x.dtype), pltpu.HBM(land_shape, x.dtype),
                                                                 jax.ShapeDtypeStruct((8, 128), F32)],
        input_output_aliases={0: 2, 1: 3},
        compiler_params=pltpu.CompilerParams(has_side_effects=_DATAFLOW),
    )(pltpu.with_memory_space_constraint(x, pltpu.HBM),
      pltpu.with_memory_space_constraint(lax.empty(land_shape, x.dtype), pltpu.HBM), after)
    return outs[:4], outs[4]


def small_wait(handle, after, name):
    send, recv, x, land = handle

    def body(x_ref, land_ref, send_ref, recv_ref, after_ref, x_out, land_out):
        for mine, theirs in _small_copies(x_ref, land_ref, send_ref, recv_ref):
            mine.wait_send()
            theirs.wait_recv()

    return pl.pallas_call(
        body, name=name,
        in_specs=[_HBM, _HBM, _SEM, _SEM, pl.BlockSpec(memory_space=pl.ANY)],
        out_specs=[_HBM, _HBM],
        out_shape=[pltpu.HBM(x.shape, x.dtype), pltpu.HBM(land.shape, land.dtype)],
        input_output_aliases={0: 0, 1: 1},
        compiler_params=pltpu.CompilerParams(has_side_effects=_DATAFLOW),
    )(x, land, send, recv, after)


def _scatter_copies(ps, lands, send, recv):
    mx, my, mc = _my_place()
    cps = []
    for j, (cx, cy) in enumerate(_other_chips(mx, my)):
        for k in range(len(ps)):
            cps.append(pltpu.make_async_remote_copy(
                src_ref=ps[k].at[:, 2 * cx + cy], dst_ref=lands[k].at[j],
                send_sem=send.at[k * 3 + j], recv_sem=recv.at[k * 3 + j],
                device_id=(cx, cy, mc), device_id_type=MESH))
    return cps


def scatter_start(ps, after, name):
    K = len(ps)

    def body(*refs):
        ins, lands = refs[:K], refs[K:2 * K]
        send, recv = refs[2 * K + 1], refs[2 * K + 2]
        for cp in _scatter_copies(ins, lands, send, recv):
            cp.start()
        refs[-1][...] = jnp.zeros_like(refs[-1])

    land_shapes = [(N_CHIP - 1, p.shape[0]) + p.shape[2:] for p in ps]
    outs = pl.pallas_call(
        body, name=name,
        in_specs=[_HBM] * (2 * K) + [pl.BlockSpec(memory_space=pl.ANY)],
        out_specs=[_SEM, _SEM] + [_HBM] * (2 * K) + [pl.BlockSpec(memory_space=pltpu.VMEM)],
        out_shape=([pltpu.SemaphoreType.DMA((3 * K,))] * 2 + [pltpu.HBM(p.shape, p.dtype) for p in ps]
                   + [pltpu.HBM(ls, p.dtype) for ls, p in zip(land_shapes, ps)] + [jax.ShapeDtypeStruct((8, 128), F32)]),
        input_output_aliases={i: 2 + i for i in range(2 * K)},
        compiler_params=pltpu.CompilerParams(has_side_effects=_DATAFLOW),
    )(*[pltpu.with_memory_space_constraint(p, pltpu.HBM) for p in ps],
      *[pltpu.with_memory_space_constraint(lax.empty(ls, p.dtype), pltpu.HBM) for ls, p in zip(land_shapes, ps)],
      after)
    return (outs[0], outs[1], outs[2:2 + K], outs[2 + K:2 + 2 * K]), outs[-1]


def scatter_wait(handle, after, name):
    send, recv, ps, lands = handle
    K = len(ps)
    afters = list(after) if isinstance(after, (list, tuple)) else [after]

    def body(*refs):
        ins, land_refs = refs[:K], refs[K:2 * K]
        send_ref, recv_ref = refs[2 * K], refs[2 * K + 1]
        for cp in _scatter_copies(ins, land_refs, send_ref, recv_ref):
            cp.wait_send()
            cp.wait_recv()

    outs = pl.pallas_call(
        body, name=name,
        in_specs=[_HBM] * (2 * K) + [_SEM, _SEM] + [pl.BlockSpec(memory_space=pl.ANY)] * len(afters),
        out_specs=[_HBM] * (2 * K),
        out_shape=[pltpu.HBM(p.shape, p.dtype) for p in ps] + [pltpu.HBM(l.shape, l.dtype) for l in lands],
        input_output_aliases={i: i for i in range(2 * K)},
        compiler_params=pltpu.CompilerParams(has_side_effects=_DATAFLOW),
    )(*ps, *lands, send, recv, *afters)
    return outs[:K], outs[K:]


def sibling_complete(ss, name):
    K = len(ss)

    def body(*refs):
        ins, outs = refs[:K], refs[K:2 * K]
        send, recv = refs[2 * K:]
        mx, my, mc = _my_place()
        cps = []
        for k in range(K):
            cp = pltpu.make_async_remote_copy(
                src_ref=ins[k].at[:, mc], dst_ref=outs[k].at[:, mc], send_sem=send.at[k], recv_sem=recv.at[k],
                device_id=(mx, my, 1 - mc), device_id_type=MESH)
            cp.start()
            cps.append(cp)
        for k in range(K):
            pltpu.make_async_remote_copy(
                src_ref=ins[k].at[:, mc], dst_ref=outs[k].at[:, 1 - mc], send_sem=send.at[k], recv_sem=recv.at[k],
                device_id=(mx, my, 1 - mc), device_id_type=MESH).wait_recv()
        for cp in cps:
            cp.wait_send()

    hbm = pl.BlockSpec(memory_space=pl.ANY)
    return pl.pallas_call(
        body, name=name,
        in_specs=[hbm] * K, out_specs=[hbm] * K,
        out_shape=[jax.ShapeDtypeStruct(s.shape, s.dtype) for s in ss],
        scratch_shapes=[pltpu.SemaphoreType.DMA((K,)), pltpu.SemaphoreType.DMA((K,))],
        input_output_aliases={k: k for k in range(K)},
    )(*ss)


def _rope_tables(T):
    inv = ROPE_THETA ** (-jnp.arange(0, ATT_DH, 2, dtype=F32) / ATT_DH)
    ang = jnp.arange(T, dtype=F32)[:, None] * inv[None, :]
    ang = jnp.concatenate([ang, ang, ang, ang], axis=-1)
    return jnp.cos(ang), jnp.sin(ang)


def _ffn_fwd(h, y, mod, i0, get_up, get_down, norm_next, tag):
    wgu = get_up(h)
    a, b, s = ffn_up(y, (wgu, (0,)), (wgu, (1,)), f"ffn_up_{tag}")
    wd = get_down(s)
    outs = resid_matmul([s], (wd, (0,)), h, mod, i0 + 2, 0.5, f"ffn_down_{tag}", norm_next)
    hn, o = outs[0], outs[1]
    return hn, (outs[2] if norm_next else None), (h, y, a, b, s, o), ((wgu, (0,)), (wgu, (1,)), (wd, (0,)))


def _ffn_bwd(dh, do, res, ng, i_n, mod, i0, wgT, wuT, wd, on_grads, next_gate, tag):
    h, y, a, b, s, o = res
    F = _wrows(wgT)
    da, db = ffn_bwd_mid(do, wd, a, b, f"ffn_bwd_mid_{tag}")
    gbuf = lax.empty((3, F, h.shape[1]), BF16)
    gbuf = matmul_tn(da, y, gbuf, 0, 0, f"dwg_{tag}")
    gbuf = matmul_tn(db, y, gbuf, 1, 0, f"dwu_{tag}")
    gbuf = matmul_tn(s, do, gbuf, 2, 0, f"dwd_{tag}")
    token, then = on_grads([gbuf])
    outs = dy_normbwd([(da, 0, wgT, 0, F), (db, 0, wuT, 0, F)], h, dh, ng, i_n, mod + token, i0 + 1,
                      f"ffn_bwd_dy_{tag}", next_gate)
    return outs, then


def _mixer_fwd(h, y, mod, w_inT, w_out, sgu, cos, sin, norm_next, tag):
    lng, lnb, sw, swt, bcol = sgu
    proj = matmul_nt(y, w_inT, f"proj_{tag}")
    out_a = sgu_fwd(proj, lng, lnb, sw, bcol, f"sgu_fwd_{tag}")
    qkv = rope_fwd(proj, cos, sin, f"rope_fwd_{tag}")
    npat = len(DILATIONS)
    qkv_res = [tuple(qkv[3 * p:3 * p + 3]) for p in range(npat)]
    os_, lses = [], []
    for d, (qd, kd, vd) in zip(DILATIONS, qkv_res):
        o_d, lse_d = attn_fwd(qd, kd, vd, f"attn_fwd_d{d}_{tag}")
        os_.append(o_d)
        lses.append(lse_d)
    comb = attn_combine(os_, lses, f"attn_combine_{tag}")
    out_b, o_res, lse_res = comb[0], comb[1:1 + npat], comb[1 + npat:]
    outs = resid_matmul([out_a, out_b], w_out, h, mod, 5, 1.0, f"mix_out_{tag}", norm_next)
    hn, om = outs[0], outs[1]
    return hn, (outs[2] if norm_next else None), (h, y, proj, out_a, out_b, o_res, lse_res, qkv_res, om)


def _mixer_bwd(dh, dom, res, ng, mod, w_inT, w_out, sgu, cos, sin, on_grads, next_gate, tag):
    lng, lnb, sw, swt, bcol = sgu
    h, y, proj, out_a, out_b, o_res, lse_res, qkv_res, om = res
    D = h.shape[1]
    dmixed = matmul_nt(dom, w_out, f"dmixed_{tag}")
    woutbuf = lax.empty((1, 2 * MIX_HALF, D), BF16)
    woutbuf = matmul_tn(out_a, dom, woutbuf, 0, 0, f"dwout_a_{tag}", tmo_cap=MIX_HALF)
    woutbuf = matmul_tn(out_b, dom, woutbuf, 0, MIX_HALF, f"dwout_b_{tag}", tmo_cap=MIX_HALF)
    d_uv, d_sw, d_svec = sgu_bwd(proj, dmixed, lng, lnb, sw, swt, bcol, f"sgu_bwd_{tag}")
    do_res = to_residues(dmixed, 1, f"dout_res_{tag}")
    dqs, dks, dvs = [], [], []
    for p, (d, (qd, kd, vd)) in enumerate(zip(DILATIONS, qkv_res)):
        dq, dk, dv = attn_bwd(qd, kd, vd, do_res[p], o_res[p], lse_res[p], f"attn_bwd_d{d}_{tag}")
        dqs.append(dq)
        dks.append(dk)
        dvs.append(dv)
    d_qkv = rope_bwd(dqs, dks, dvs, cos, sin, f"rope_bwd_{tag}")
    winbuf = lax.empty((1, 5 * MIX_HALF, D), BF16)
    winbuf = matmul_tn(d_uv, y, winbuf, 0, 0, f"dwin_uv_{tag}", tmo_cap=MIX_HALF)
    winbuf = matmul_tn(d_qkv, y, winbuf, 0, 2 * MIX_HALF, f"dwin_qkv_{tag}", tmo_cap=MIX_HALF)
    token, then = on_grads([winbuf, woutbuf])
    pairs = [(d_uv, 0, w_inT, 0, 2 * MIX_HALF), (d_qkv, 0, w_inT, 1, 2 * MIX_HALF), (d_qkv, 2, w_inT, 4, MIX_HALF)]
    outs = dy_normbwd(pairs, h, dh, ng, 1, mod + token, 4, f"mix_bwd_dy_{tag}", next_gate)
    return outs, d_sw, d_svec, then


def _local_step(x, tgt, mods, ngs, get_w, sgus, gf, on_block_grads, on_layer_small):
    T, D = x.shape
    cos, sin = _rope_tables(T)
    h = x
    saved, weights = [], []
    for l in range(2):
        def getter(blk, l=l):
            return lambda after: get_w(l, blk, after)

        if l == 0:
            y = normmod_fwd(h, ngs[0], 0, mods[0], 0, 1, "normmod_l0f1")
        h, y, r1, wf1 = _ffn_fwd(h, y, mods[l], 0, getter("f1u"), getter("f1d"), (ngs[l], 1, mods[l], 3, 4), f"l{l}f1")
        w_inT, w_out = get_w(l, "mx", h)
        h, y, r2 = _mixer_fwd(h, y, mods[l], (w_inT, (0,)), (w_out, (0,)), sgus[l], cos, sin,
                              (ngs[l], 2, mods[l], 6, 7), f"l{l}mx")
        h, y, r3, wf2 = _ffn_fwd(h, y, mods[l], 6, getter("f2u"), getter("f2d"),
                                 (ngs[l + 1], 0, mods[l + 1], 0, 1) if l + 1 < 2 else None, f"l{l}f2")
        saved.append((r1, r2, r3))
        weights.append((wf1, w_inT, w_out, wf2))
    def gate_of(l, blk):
        r1, r2, r3 = saved[l]
        o, i_g, coef = {"f2": (r3[5], 8, 0.5), "mx": (r2[-1], 5, 1.0), "f1": (r1[5], 2, 0.5)}[blk]
        return o, mods[l], i_g, coef

    seq = [(l, blk) for l in (1, 0) for blk in ("f2", "mx", "f1")]
    dh, red_final, do, red_g = final_loss_bwd(h, gf, tgt, gate_of(*seq[0]), "final_loss_bwd")
    rn, rg = {}, {}
    for idx, (l, blk) in enumerate(seq):
        r1, r2, r3 = saved[l]
        wf1, w_inT, w_out, wf2 = weights[l]
        nxt = gate_of(*seq[idx + 1]) if idx + 1 < len(seq) else None
        rg[blk] = red_g
        tag = f"l{l}{blk}"

        def on(arrays, l=l, blk=blk):
            return on_block_grads(l, blk, arrays)

        if blk == "f2":
            outs, then = _ffn_bwd(dh, do, r3, ngs[l], 2, mods[l], 6, *wf2, on, nxt, tag)
        elif blk == "mx":
            outs, d_sw, d_svec, then = _mixer_bwd(dh, do, r2, ngs[l], mods[l], (w_inT, (0,)), (w_out, (0,)), sgus[l],
                                                  cos, sin, on, nxt, tag)
        else:
            outs, then = _ffn_bwd(dh, do, r1, ngs[l], 0, mods[l], 0, *wf1, on, nxt, tag)
        dh, rn[blk] = outs[0], outs[1]
        if nxt is not None:
            do, red_g = outs[2], outs[3]
        if blk == "f1":
            mods = mods + on_layer_small(l, dict(sgu_w=d_sw, sgu_vec=d_svec, red_n=(rn["f1"], rn["mx"], rn["f2"]),
                                                 red_g=(rg["f1"], rg["mx"], rg["f2"])),
                                         red_final if l == 0 else None)
            mods = mods + then(mods)
        else:
            mods = mods + then(dh)
    return dh


def _adam_out(w, g, m, v, name):
    shp = w.shape
    two_d = (-1, shp[-1])
    d, mn, vn = adamw(w.reshape(two_d), g.reshape(two_d), m.reshape(two_d), v.reshape(two_d), name)
    return g, d.reshape(shp), mn.reshape(shp), vn.reshape(shp)


def kernel(x, c, ada_w, ada_b, norm_g, ffn1_wg, ffn1_wu, ffn1_wd, ffn2_wg, ffn2_wu, ffn2_wd, w_in, sgu_ln_g, sgu_ln_b, sgu_w, sgu_b, w_out, final_g, loss_target, m_ada_w, m_ada_b, m_norm_g, m_ffn1_wg, m_ffn1_wu, m_ffn1_wd, m_ffn2_wg, m_ffn2_wu, m_ffn2_wd, m_w_in, m_sgu_ln_g, m_sgu_ln_b, m_sgu_w, m_sgu_b, m_w_out, m_final_g, v_ada_w, v_ada_b, v_norm_g, v_ffn1_wg, v_ffn1_wu, v_ffn1_wd, v_ffn2_wg, v_ffn2_wu, v_ffn2_wd, v_w_in, v_sgu_ln_g, v_sgu_ln_b, v_sgu_w, v_sgu_b, v_w_out, v_final_g):
    T, D = x.shape[1], x.shape[2]
    NL = ada_w.shape[0]
    mx, my, mc = _my_place()
    me = 4 * mx + 2 * my + mc
    ci = 2 * mx + my
    c_idx = jnp.reshape(mc, (1,)).astype(jnp.int32)
    place = jnp.stack([ci, mc]).astype(jnp.int32)

    ngw = norm_g.shape[2]
    small_in = jnp.concatenate([jnp.pad(c, ((0, 7), (0, 0))),
                                jnp.pad(norm_g.reshape(NL * 3, ngw), ((0, 8 - NL * 3), (0, D - ngw)))], axis=0)
    small_all, _ = gather_small(small_in, place, "gather_c_normg")
    c_all = small_all[:, 0, :]
    ng_parts = small_all[0::2, 8:8 + NL * 3, :ngw]
    ngs = jnp.transpose(ng_parts, (1, 0, 2)).reshape(NL, 3, N_CHIP * ngw)

    nmod = ada_w.shape[2]
    ada_b_mine = lax.dynamic_slice_in_dim(ada_b, ci * nmod, nmod, axis=1).reshape(NL, 1, nmod)
    mod_part = ada_fwd(c_all, ada_w, ada_b_mine, "ada_fwd")
    mod_all, _ = gather_small(mod_part.reshape(NL * N_DEV, nmod), place, "gather_mod")
    mod_rows = lax.dynamic_index_in_dim(mod_all.reshape(N_DEV, NL, N_DEV, nmod), me, axis=2, keepdims=False)
    mods = jnp.transpose(mod_rows[0::2], (1, 0, 2)).reshape(NL, N_ADA, D)

    sgus = []
    for l in range(NL):
        sgus.append((sgu_ln_g[l].reshape(1, MIX_HALF), sgu_ln_b[l].reshape(1, MIX_HALF), sgu_w[l],
                     jnp.swapaxes(sgu_w[l], 1, 2), jnp.transpose(sgu_b[l])))

    def halves(a):
        n, r, _ = a.shape
        return a.reshape(n, 2, r // 2, D)

    first_group = [halves(jnp.stack([ffn1_wg[0].T, ffn1_wu[0].T], axis=0).astype(BF16))]
    first_handles, first_token = gather_start([first_group], mods, "gather_start_first")
    zero = first_token[0, 0]
    mods = mods + zero

    def prep(a):
        return (a + zero).astype(BF16)

    groups = []
    for l in range(NL):
        groups += [[halves(jnp.stack([prep(ffn1_wg[l].T), prep(ffn1_wu[l].T)], axis=0))],
                   [halves(prep(ffn1_wd[l])[None])],
                   [halves(prep(w_in[l].T)[None]), halves(prep(w_out[l])[None])],
                   [halves(jnp.stack([prep(ffn2_wg[l].T), prep(ffn2_wu[l].T)], axis=0))],
                   [halves(prep(ffn2_wd[l])[None])]]
    handles, token = gather_start(groups[1:], mods, "gather_start")
    handles = first_handles + handles
    mods = mods + token[0, 0]
    group_no = {"f1u": 0, "f1d": 1, "mx": 2, "f2u": 3, "f2d": 4}

    def get_w(l, key, after):
        full = gather_wait(handles[len(group_no) * l + group_no[key]], after, f"gather_wait_l{l}{key}")
        full = [a.reshape(a.shape[0], N_CHIP * 2 * a.shape[3], D) for a in full]
        return full[0] if key != "mx" else tuple(full)

    def split(a):
        n, r4, _ = a.shape
        return a.reshape(n, N_CHIP, 2, r4 // N_CHIP // 2, D)

    pending, small_pending, small_tokens = {}, {}, {}

    def on_block_grads(l, blk, bufs):
        tag = f"l{l}{blk}"
        sib, tok1 = sibling_start([split(g) for g in bufs], place, f"rs_sibling_start_{tag}")

        def then(after):
            parts, lands = sibling_wait(sib, after, f"rs_sibling_wait_{tag}")
            psums = [sum_halves(g, ld, c_idx, f"rs_sum_halves_{tag}_{i}") for i, (g, ld) in enumerate(zip(parts, lands))]
            pending[(l, blk)], tok2 = scatter_start(psums, lands[0], f"rs_chips_start_{tag}")
            return tok2[0, 0]

        return tok1[0, 0], then

    def blocks_finish(blocks, after, tag):
        ssums, counts = [], []
        for l, blk in blocks:
            psums, lands2 = scatter_wait(pending.pop((l, blk)), after, f"rs_chips_wait_l{l}{blk}")
            ssums += [sum_chips(p, ld, place, f"rs_sum_chips_l{l}{blk}_{i}") for i, (p, ld) in enumerate(zip(psums, lands2))]
            counts.append(len(psums))
        fins = [f.reshape(f.shape[0], -1, D) for f in sibling_complete(ssums, f"rs_complete_{tag}")]
        out, i = [], 0
        for n in counts:
            out.append(fins[i:i + n])
            i += n
        return out

    def on_layer_small(l, grads, red_final):
        blocks = list(grads["red_n"]) + list(grads["red_g"])
        blocks.append(jnp.pad(grads["sgu_vec"], ((0, 0), (0, D - MIX_HALF))))
        blocks.append(grads["sgu_w"].reshape(-1, D))
        if red_final is not None:
            blocks.append(red_final)
        xs = jnp.concatenate(blocks, axis=0)
        small_pending[l], small_tokens[l] = small_start(xs, place, f"small_start_l{l}")
        return small_tokens[l][0, 0]

    grad_x = _local_step(x[0], loss_target[0], mods, ngs, get_w, sgus, final_g.reshape(1, D),
                         on_block_grads, on_layer_small)

    adam_state = {}

    def adam_big(nm, l, g, w, m, v):
        adam_state[nm] = adamw_layer(w, g, m, v, l, adam_state.get(nm), f"adamw_{nm}_l{l}")

    def adam_block(l, blk, fin):
        if blk == "mx":
            adam_big("w_in", l, fin[0][0].T, w_in, m_w_in, v_w_in)
            adam_big("w_out", l, fin[1][0], w_out, m_w_out, v_w_out)
        else:
            ws = ((ffn1_wg, m_ffn1_wg, v_ffn1_wg), (ffn1_wu, m_ffn1_wu, v_ffn1_wu), (ffn1_wd, m_ffn1_wd, v_ffn1_wd)) \
                if blk == "f1" else \
                ((ffn2_wg, m_ffn2_wg, v_ffn2_wg), (ffn2_wu, m_ffn2_wu, v_ffn2_wu), (ffn2_wd, m_ffn2_wd, v_ffn2_wd))
            pre = "ffn1" if blk == "f1" else "ffn2"
            for k, (nm, tr) in enumerate((("wg", True), ("wu", True), ("wd", False))):
                adam_big(f"{pre}_{nm}", l, fin[0][k], *[jnp.swapaxes(t, 1, 2) if tr else t for t in ws[k]])

    done_order = [(l, blk) for l in range(NL - 1, -1, -1) for blk in ("f2", "mx", "f1")]
    for (l, blk), fin in zip(done_order[:-1], blocks_finish(done_order[:-1], small_tokens[0], "early")):
        adam_block(l, blk, fin)
    last_big = adam_state["w_out"][1]

    small_sum, small_all = [], []
    for l in range(NL):
        xs, land = small_wait(small_pending[l], last_big, f"small_wait_l{l}")
        full = lax.dynamic_update_slice(land, xs[None], (me, 0, 0))
        small_all.append(full)
        small_sum.append(sum_slots(full, f"small_sum_l{l}"))
    offs = [8 * i for i in range(8)]
    off_final = offs[7] + SGU_HEADS * ATT_BLOCK * HEAD_LANES // D
    loss = small_sum[0][off_final + 1, 0]
    g_final_g = small_sum[0][off_final, :]
    g_norm_g, g_ada_b, g_lng, g_lnb, g_sb, g_sw, dmod_all = [], [], [], [], [], [], []
    for l in range(NL):
        rn = [small_sum[l][offs[i]:offs[i] + 8] for i in range(3)]
        rg = [small_sum[l][offs[3 + i]:offs[3 + i] + 8] for i in range(3)]
        g_norm_g.append(jnp.stack([rn[i][2] for i in range(3)], axis=0))
        g_ada_b.append(jnp.concatenate([jnp.stack([rn[i][0], rn[i][1], rg[i][0]], axis=0) for i in range(3)],
                                       axis=0).reshape(N_ADA * D))
        sv = small_sum[l][offs[6]:offs[6] + 8, :MIX_HALF]
        g_lng.append(sv[0].reshape(SGU_HEADS, HEAD_LANES))
        g_lnb.append(sv[1].reshape(SGU_HEADS, HEAD_LANES))
        g_sb.append(sv[2].reshape(SGU_HEADS, ATT_BLOCK))
        g_sw.append(small_sum[l][offs[7]:off_final].reshape(sgu_w.shape[1:]))
        rows = []
        for i in range(3):
            an = small_all[l][:, offs[i]:offs[i] + 2]
            ag = small_all[l][:, offs[3 + i]:offs[3 + i] + 1]
            rows += [an[:, 0], an[:, 1], ag[:, 0]]
        dmod_all.append(jnp.stack(rows, axis=1).reshape(N_DEV, N_ADA * D))
    dmod_all = jnp.stack(dmod_all, axis=0)
    dmod_mine = lax.dynamic_slice_in_dim(dmod_all, ci * nmod, nmod, axis=2)
    g_ada_w = ada_bwd(jnp.transpose(c_all), dmod_mine, "ada_bwd")
    g_ada_b = jnp.stack(g_ada_b, axis=0)
    g_norm_g_full = jnp.stack(g_norm_g, axis=0)
    g_norm_g_mine = lax.dynamic_slice_in_dim(g_norm_g_full, ci * ngw, ngw, axis=2)

    small_params = [
        ("ada_w", ada_w, g_ada_w, m_ada_w, v_ada_w),
        ("ada_b", ada_b, g_ada_b, m_ada_b, v_ada_b),
        ("norm_g", norm_g, g_norm_g_mine, m_norm_g, v_norm_g),
        ("sgu_ln_g", sgu_ln_g, jnp.stack(g_lng, axis=0), m_sgu_ln_g, v_sgu_ln_g),
        ("sgu_ln_b", sgu_ln_b, jnp.stack(g_lnb, axis=0), m_sgu_ln_b, v_sgu_ln_b),
        ("sgu_w", sgu_w, jnp.stack(g_sw, axis=0), m_sgu_w, v_sgu_w),
        ("sgu_b", sgu_b, jnp.stack(g_sb, axis=0), m_sgu_b, v_sgu_b),
        ("final_g", final_g.reshape(1, D), g_final_g.reshape(1, D), m_final_g.reshape(1, D), v_final_g.reshape(1, D)),
    ]
    for nm, w, g, m, v in small_params:
        res = _adam_out(w, g, m, v, f"adamw_{nm}")
        adam_state[nm] = tuple(t.reshape(D) for t in res) if nm == "final_g" else res

    l, blk = done_order[-1]
    adam_block(l, blk, blocks_finish([(l, blk)], [st[1] for st in adam_state.values()], "last")[0])

    names = ["ada_w", "ada_b", "norm_g", "ffn1_wg", "ffn1_wu", "ffn1_wd", "ffn2_wg", "ffn2_wu", "ffn2_wd", "w_in",
             "sgu_ln_g", "sgu_ln_b", "sgu_w", "sgu_b", "w_out", "final_g"]
    shapes = [t.shape for t in (ada_w, ada_b, norm_g, ffn1_wg, ffn1_wu, ffn1_wd, ffn2_wg, ffn2_wu, ffn2_wd, w_in,
                                sgu_ln_g, sgu_ln_b, sgu_w, sgu_b, w_out, final_g)]
    def shaped(nm, t, s):
        if nm in ("ffn1_wg", "ffn1_wu", "ffn2_wg", "ffn2_wu"):
            return jnp.swapaxes(t.reshape(s[0], s[2], s[1]), 1, 2)
        return t.reshape(s)

    return (loss, grad_x[None], *[shaped(nm, adam_state[nm][i], s) for i in range(4) for nm, s in zip(names, shapes)])
```

```python
import math

import jax
import jax.numpy as jnp
from jax import lax
from jax.experimental import pallas as pl
from jax.experimental.pallas import tpu as pltpu

F32 = jnp.float32
BF16 = jnp.bfloat16
EPS = 1e-6
SGU_HEADS = 4
HEAD_LANES = 128
ATT_DH = 64
ATT_BLOCK = 128
MIX_HALF = SGU_HEADS * HEAD_LANES
DILATIONS = (1, 4, 16)
ROPE_THETA = 10000.0
N_ADA = 9
ADAM_LR, ADAM_B1, ADAM_B2, ADAM_EPS, ADAM_WD, ADAM_STEP = 0.001, 0.9, 0.999, 1e-08, 0.01, 10
NEG = -1e30
V7X_VMEM_BYTES = 64 * 1024 * 1024
VMEM_LIMIT = V7X_VMEM_BYTES * 7 // 8
MESH = pl.DeviceIdType.MESH
N_DEV = 8
N_CHIP = 4


def _tile(n, cap, mult):
    if n <= cap:
        return n
    t = (cap // mult) * mult
    while t >= mult:
        if n % t == 0:
            return t
        t -= mult
    raise ValueError((n, cap, mult))


def _params(dims=None):
    return pltpu.CompilerParams(dimension_semantics=dims, vmem_limit_bytes=VMEM_LIMIT)


def _wspec(w, rows, idx, resident=False):
    arr, lead = w
    kw = dict(pipeline_mode=pl.Buffered(1)) if resident else {}
    return pl.BlockSpec((None,) * len(lead) + (rows, arr.shape[-1]), lambda *g: tuple(lead) + (idx(*g), 0), **kw)


def _wrows(w):
    return w[0].shape[-2]


def _nt(a, b):
    return lax.dot_general(a, b, (((1,), (1,)), ((), ())), preferred_element_type=F32)


def _tn(a, b):
    return lax.dot_general(a, b, (((0,), (0,)), ((), ())), preferred_element_type=F32)


def _nn(a, b):
    return jnp.dot(a, b, preferred_element_type=F32)


def _sigmoid(x):
    return 0.5 * jnp.tanh(0.5 * x) + 0.5


_GELU_K = math.sqrt(2.0 / math.pi)
_GELU_C = 0.044715


def _gelu(x):
    t = jnp.tanh(_GELU_K * (x + _GELU_C * x * x * x))
    return 0.5 * x * (1.0 + t)


def _gelu_and_grad(x):
    x2 = x * x
    t = jnp.tanh(_GELU_K * (x + _GELU_C * x * x2))
    g = 0.5 * x * (1.0 + t)
    dg = 0.5 * (1.0 + t) + 0.5 * x * (1.0 - t * t) * (_GELU_K * (1.0 + 3.0 * _GELU_C * x2))
    return g, dg


def normmod_fwd(h, ng, i_n, mod, i_sh, i_sc, name):
    T, D = h.shape
    tm = _tile(T, 512, 8)

    def body(h_ref, ng_ref, mod_ref, y_ref):
        y_ref[...] = _normmod(h_ref[...], ng_ref[i_n:i_n + 1, :], mod_ref[i_sh:i_sh + 1, :],
                              mod_ref[i_sc:i_sc + 1, :]).astype(BF16)

    return pl.pallas_call(
        body, name=name, grid=(T // tm,),
        in_specs=[pl.BlockSpec((tm, D), lambda i: (i, 0)),
                  pl.BlockSpec(ng.shape, lambda i: (0, 0)),
                  pl.BlockSpec(mod.shape, lambda i: (0, 0))],
        out_specs=pl.BlockSpec((tm, D), lambda i: (i, 0)),
        out_shape=jax.ShapeDtypeStruct((T, D), BF16),
        compiler_params=_params(("parallel",)),
    )(h, ng, mod)


def ffn_up(y, wgT, wuT, name):
    T, D = y.shape
    F = _wrows(wgT)
    tm = _tile(T, 512, 16)
    tf = _tile(F, 2816, 256)
    cuts = list(range(0, tf, 768)) + [tf]

    def body(y_ref, wg_ref, wu_ref, p_ref, q_ref, s_ref):
        yv = y_ref[...]
        for c0, c1 in zip(cuts[:-1], cuts[1:]):
            a = _nt(yv, wg_ref[c0:c1, :])
            b = _nt(yv, wu_ref[c0:c1, :])
            sig = _sigmoid(a)
            q = a * sig
            p_ref[:, c0:c1] = (b * (sig + q * (1.0 - sig))).astype(BF16)
            q_ref[:, c0:c1] = q.astype(BF16)
            s_ref[:, c0:c1] = (q * b).astype(BF16)

    act = jax.ShapeDtypeStruct((T, F), BF16)
    return pl.pallas_call(
        body, name=name, grid=(F // tf, T // tm),
        in_specs=[pl.BlockSpec((tm, D), lambda j, i: (i, 0)),
                  _wspec(wgT, tf, lambda j, i: j, resident=True),
                  _wspec(wuT, tf, lambda j, i: j, resident=True)],
        out_specs=[pl.BlockSpec((tm, tf), lambda j, i: (i, j))] * 3,
        out_shape=[act, act, act],
        compiler_params=_params(("parallel", "parallel")),
    )(y, wgT[0], wuT[0])


def _normmod(x, gn, sh, sc):
    r = lax.rsqrt(jnp.mean(x * x, axis=-1, keepdims=True) + EPS)
    return ((x * r) * gn) * (1.0 + sc) + sh


def resid_matmul(xs, w, h, mod, i_g, coef, name, norm_next=None):
    T, D = h.shape
    kb = xs[0].shape[1]
    assert all(x.shape == (T, kb) for x in xs) and _wrows(w) == kb * len(xs)
    tm = _tile(T, 1024, 16)
    nx = len(xs)
    n_in, n_out, n_shape, n_ops = [], [], [], []
    if norm_next:
        ng_n, i_n, mod_n, i_sh, i_sc = norm_next
        n_in = [pl.BlockSpec(ng_n.shape, lambda i: (0, 0)), pl.BlockSpec(mod_n.shape, lambda i: (0, 0))]
        n_out = [pl.BlockSpec((tm, D), lambda i: (i, 0))]
        n_shape = [jax.ShapeDtypeStruct((T, D), BF16)]
        n_ops = [ng_n, mod_n]

    def body(*refs):
        x_refs, w_refs = refs[:nx], refs[nx:2 * nx]
        h_ref, mod_ref = refs[2 * nx:2 * nx + 2]
        hn_ref, o_ref = refs[2 * nx + 2 + len(n_in):2 * nx + 4 + len(n_in)]
        o = _nn(x_refs[0][...], w_refs[0][...])
        for xr, wr in zip(x_refs[1:], w_refs[1:]):
            o = o + _nn(xr[...], wr[...])
        o_ref[...] = o.astype(BF16)
        hn = h_ref[...] + (coef * mod_ref[i_g:i_g + 1, :]) * o
        hn_ref[...] = hn
        if norm_next:
            ng_ref, modn_ref = refs[2 * nx + 2], refs[2 * nx + 3]
            refs[-1][...] = _normmod(hn, ng_ref[i_n:i_n + 1, :], modn_ref[i_sh:i_sh + 1, :],
                                     modn_ref[i_sc:i_sc + 1, :]).astype(BF16)

    return pl.pallas_call(
        body, name=name, grid=(T // tm,),
        in_specs=([pl.BlockSpec((tm, kb), lambda i: (i, 0))] * nx
                  + [_wspec(w, kb, lambda i, p=p: p, resident=True) for p in range(nx)]
                  + [pl.BlockSpec((tm, D), lambda i: (i, 0)),
                     pl.BlockSpec(mod.shape, lambda i: (0, 0))] + n_in),
        out_specs=[pl.BlockSpec((tm, D), lambda i: (i, 0))] * 2 + n_out,
        out_shape=[jax.ShapeDtypeStruct((T, D), F32), jax.ShapeDtypeStruct((T, D), BF16)] + n_shape,
        compiler_params=_params(("parallel",)),
    )(*xs, *([w[0]] * nx), h, mod, *n_ops)


def _gate_specs(gate, tm, D):
    o, mod, _, _ = gate
    T = o.shape[0]
    return ([pl.BlockSpec((tm, D), lambda i: (i, 0)), pl.BlockSpec(mod.shape, lambda i: (0, 0))],
            [pl.BlockSpec((tm, D), lambda i: (i, 0)), pl.BlockSpec((8, D), lambda i: (0, 0))],
            [jax.ShapeDtypeStruct((T, D), BF16), jax.ShapeDtypeStruct((8, D), F32)],
            [o, mod])


def _gate_emit(d, gate, o_ref, mod_ref, do_ref, red_ref):
    _, _, i_g, coef = gate
    do_ref[...] = (d * (coef * mod_ref[i_g:i_g + 1, :])).astype(BF16)

    @pl.when(pl.program_id(0) == 0)
    def _():
        red_ref[...] = jnp.zeros_like(red_ref)

    red_ref[0:1, :] += coef * jnp.sum(d * o_ref[...].astype(F32), axis=0, keepdims=True)


def ffn_bwd_mid(do, wd, p, q, name):
    T, D = do.shape
    F = _wrows(wd)
    tm = _tile(T, 512, 16)
    tf = _tile(F, 2816, 256)
    cuts = list(range(0, tf, 256)) + [tf]

    def body(do_ref, wd_ref, p_ref, q_ref, da_ref, db_ref):
        dov = do_ref[...]
        for c0, c1 in zip(cuts[:-1], cuts[1:]):
            ds = _nt(dov, wd_ref[c0:c1, :])
            da_ref[:, c0:c1] = (ds * p_ref[:, c0:c1].astype(F32)).astype(BF16)
            db_ref[:, c0:c1] = (ds * q_ref[:, c0:c1].astype(F32)).astype(BF16)

    act = jax.ShapeDtypeStruct((T, F), BF16)
    return pl.pallas_call(
        body, name=name, grid=(F // tf, T // tm),
        in_specs=[pl.BlockSpec((tm, D), lambda j, i: (i, 0)),
                  _wspec(wd, tf, lambda j, i: j, resident=True),
                  pl.BlockSpec((tm, tf), lambda j, i: (i, j)),
                  pl.BlockSpec((tm, tf), lambda j, i: (i, j))],
        out_specs=[pl.BlockSpec((tm, tf), lambda j, i: (i, j))] * 2,
        out_shape=[act, act],
        compiler_params=_params(("parallel", "parallel")),
    )(do, wd[0], p, q)


def dy_normbwd(pairs, h, dhp, ng, i_n, mod, i_sc, name, gate=None):
    T, D = h.shape
    tm = _tile(T, 512, 16)
    npair = len(pairs)
    g_in, g_out, g_shape, g_ops = _gate_specs(gate, tm, D) if gate else ([], [], [], [])

    def body(*refs):
        x_refs, w_refs = refs[:npair], refs[npair:2 * npair]
        h_ref, dhp_ref, ng_ref, mod_ref = refs[2 * npair:2 * npair + 4]
        dh_ref, red_ref = refs[2 * npair + 4 + len(g_in):2 * npair + 6 + len(g_in)]
        dy = _nn(x_refs[0][...], w_refs[0][...])
        for xr, wr in zip(x_refs[1:], w_refs[1:]):
            dy = dy + _nn(xr[...], wr[...])
        x = h_ref[...]
        r = lax.rsqrt(jnp.mean(x * x, axis=-1, keepdims=True) + EPS)
        n = x * r
        gn = ng_ref[i_n:i_n + 1, :]
        dnh = dy * (1.0 + mod_ref[i_sc:i_sc + 1, :])

        @pl.when(pl.program_id(0) == 0)
        def _():
            red_ref[...] = jnp.zeros_like(red_ref)

        red_ref[0:1, :] += jnp.sum(dy, axis=0, keepdims=True)
        red_ref[1:2, :] += jnp.sum(dy * (n * gn), axis=0, keepdims=True)
        red_ref[2:3, :] += jnp.sum(dnh * n, axis=0, keepdims=True)
        dn = dnh * gn
        dh_new = dhp_ref[...] + r * (dn - n * jnp.mean(dn * n, axis=-1, keepdims=True))
        dh_ref[...] = dh_new
        if gate:
            _gate_emit(dh_new, gate, refs[2 * npair + 4], refs[2 * npair + 5], refs[-2], refs[-1])

    in_specs = ([pl.BlockSpec((tm, kb), lambda i, c=c: (i, c)) for (_, c, _, _, kb) in pairs]
                + [_wspec(w, kb, lambda i, r=r: r, resident=True) for (_, _, w, r, kb) in pairs]
                + [pl.BlockSpec((tm, D), lambda i: (i, 0)),
                   pl.BlockSpec((tm, D), lambda i: (i, 0)),
                   pl.BlockSpec(ng.shape, lambda i: (0, 0)),
                   pl.BlockSpec(mod.shape, lambda i: (0, 0))] + g_in)
    return pl.pallas_call(
        body, name=name, grid=(T // tm,), in_specs=in_specs,
        out_specs=[pl.BlockSpec((tm, D), lambda i: (i, 0)), pl.BlockSpec((8, D), lambda i: (0, 0))] + g_out,
        out_shape=[jax.ShapeDtypeStruct((T, D), F32), jax.ShapeDtypeStruct((8, D), F32)] + g_shape,
        compiler_params=_params(("arbitrary",)),
    )(*[p[0] for p in pairs], *[p[2][0] for p in pairs], h, dhp, ng, mod, *g_ops)


def matmul_tn(a, b, buf, slot, row0, name, tmo_cap=1408):
    T, N = b.shape
    ma = a.shape[1]
    tmo = _tile(ma, tmo_cap, 128)
    assert row0 % tmo == 0
    nmo = ma // tmo
    tk = _tile(T, 2048, 16)
    nk = T // tk

    def body(a_ref, b_ref, buf_ref, o_ref, acc_ref):
        k = pl.program_id(1)

        @pl.when(k == 0)
        def _():
            acc_ref[...] = jnp.zeros_like(acc_ref)

        acc_ref[...] += _tn(a_ref[...], b_ref[...])

        @pl.when(k == nk - 1)
        def _():
            o_ref[...] = acc_ref[...].astype(BF16)

    return pl.pallas_call(
        body, name=name, grid=(nmo, nk),
        in_specs=[pl.BlockSpec((tk, tmo), lambda j, k: (k, j)),
                  pl.BlockSpec((tk, N), lambda j, k: (k, 0)),
                  pl.BlockSpec(memory_space=pl.ANY)],
        out_specs=pl.BlockSpec((None, tmo, N), lambda j, k: (slot, row0 // tmo + j, 0)),
        out_shape=jax.ShapeDtypeStruct(buf.shape, BF16),
        scratch_shapes=[pltpu.VMEM((tmo, N), F32)],
        input_output_aliases={2: 0},
        compiler_params=_params(("parallel", "arbitrary")),
    )(a, b, buf)


def matmul_nt(x, w, name):
    T, K = x.shape
    N = _wrows(w)
    tm = _tile(T, 1024, 16)
    tn = _tile(N, 1280, 128)

    def body(x_ref, w_ref, o_ref):
        o_ref[...] = _nt(x_ref[...], w_ref[...]).astype(BF16)

    return pl.pallas_call(
        body, name=name, grid=(N // tn, T // tm),
        in_specs=[pl.BlockSpec((tm, K), lambda j, i: (i, 0)), _wspec(w, tn, lambda j, i: j)],
        out_specs=pl.BlockSpec((tm, tn), lambda j, i: (i, j)),
        out_shape=jax.ShapeDtypeStruct((T, N), BF16),
        compiler_params=_params(("parallel", "parallel")),
    )(x, w[0])


def _sgu_head_fwd(u, v, lng, lnb):
    gu, dgu = _gelu_and_grad(u)
    gv, dgv = _gelu_and_grad(v)
    mu = jnp.mean(gv, axis=-1, keepdims=True)
    xc = gv - mu
    rstd = lax.rsqrt(jnp.mean(xc * xc, axis=-1, keepdims=True) + EPS)
    xhat = xc * rstd
    vn = xhat * lng + lnb
    return gu, dgu, dgv, rstd, xhat, vn


def _tril_mask():
    r = lax.broadcasted_iota(jnp.int32, (ATT_BLOCK, ATT_BLOCK), 0)
    c = lax.broadcasted_iota(jnp.int32, (ATT_BLOCK, ATT_BLOCK), 1)
    return c <= r


def _triu_mask():
    r = lax.broadcasted_iota(jnp.int32, (ATT_BLOCK, ATT_BLOCK), 0)
    c = lax.broadcasted_iota(jnp.int32, (ATT_BLOCK, ATT_BLOCK), 1)
    return r <= c


def sgu_fwd(proj, lng, lnb, w, bcol, name):
    T = proj.shape[0]
    tm = _tile(T, 512, 128)
    nch = tm // ATT_BLOCK

    def body(u_ref, v_ref, lng_ref, lnb_ref, w_ref, b_ref, o_ref):
        tril = _tril_mask()
        for hd in range(SGU_HEADS):
            sl = slice(hd * HEAD_LANES, (hd + 1) * HEAD_LANES)
            u = u_ref[:, sl].astype(F32)
            v = v_ref[:, sl].astype(F32)
            gu, _, _, _, _, vn = _sgu_head_fwd(u, v, lng_ref[:, sl], lnb_ref[:, sl])
            wm = jnp.where(tril, w_ref[hd], 0.0).astype(BF16)
            vnb = vn.astype(BF16)
            bc = b_ref[:, hd:hd + 1]
            for ch in range(nch):
                rs = slice(ch * ATT_BLOCK, (ch + 1) * ATT_BLOCK)
                z = _nn(wm, vnb[rs, :]) + bc
                o_ref[rs, sl] = (gu[rs, :] * z).astype(BF16)

    return pl.pallas_call(
        body, name=name, grid=(T // tm,),
        in_specs=[pl.BlockSpec((tm, MIX_HALF), lambda i: (i, 0)),
                  pl.BlockSpec((tm, MIX_HALF), lambda i: (i, 1)),
                  pl.BlockSpec((1, MIX_HALF), lambda i: (0, 0)),
                  pl.BlockSpec((1, MIX_HALF), lambda i: (0, 0)),
                  pl.BlockSpec(w.shape, lambda i: (0, 0, 0)),
                  pl.BlockSpec(bcol.shape, lambda i: (0, 0))],
        out_specs=pl.BlockSpec((tm, MIX_HALF), lambda i: (i, 0)),
        out_shape=jax.ShapeDtypeStruct((T, MIX_HALF), BF16),
        compiler_params=_params(("parallel",)),
    )(proj, proj, lng, lnb, w, bcol)


def sgu_bwd(proj, dmixed, lng, lnb, w, wt, bcol, name):
    T = proj.shape[0]
    tm = _tile(T, 512, 128)
    nch = tm // ATT_BLOCK
    nsteps = T // tm

    def body(u_ref, v_ref, g_ref, lng_ref, lnb_ref, w_ref, wt_ref, b_ref, duv_ref, dw_ref, dvec_ref, bacc_ref):
        step = pl.program_id(0)

        @pl.when(step == 0)
        def _():
            dw_ref[...] = jnp.zeros_like(dw_ref)
            dvec_ref[...] = jnp.zeros_like(dvec_ref)
            bacc_ref[...] = jnp.zeros_like(bacc_ref)

        tril = _tril_mask()
        triu = _triu_mask()
        for hd in range(SGU_HEADS):
            sl = slice(hd * HEAD_LANES, (hd + 1) * HEAD_LANES)
            u = u_ref[:, sl].astype(F32)
            v = v_ref[:, sl].astype(F32)
            lng_h = lng_ref[:, sl]
            gu, dgu, dgv, rstd, xhat, vn = _sgu_head_fwd(u, v, lng_h, lnb_ref[:, sl])
            wm = jnp.where(tril, w_ref[hd], 0.0).astype(BF16)
            wmt = jnp.where(triu, wt_ref[hd], 0.0).astype(BF16)
            vnb = vn.astype(BF16)
            bc = b_ref[:, hd:hd + 1]
            g = g_ref[:, sl].astype(F32)
            dw_acc = jnp.zeros((ATT_BLOCK, ATT_BLOCK), F32)
            b_acc = jnp.zeros((ATT_BLOCK, HEAD_LANES), F32)
            dvn_parts = []
            for ch in range(nch):
                rs = slice(ch * ATT_BLOCK, (ch + 1) * ATT_BLOCK)
                z = _nn(wm, vnb[rs, :]) + bc
                duv_ref[rs, sl] = (g[rs, :] * z * dgu[rs, :]).astype(BF16)
                dz = g[rs, :] * gu[rs, :]
                dzb = dz.astype(BF16)
                dvn_parts.append(_nn(wmt, dzb))
                dw_acc = dw_acc + _nt(dzb, vnb[rs, :])
                b_acc = b_acc + dz
            dvn = jnp.concatenate(dvn_parts, axis=0)
            dw_ref[hd] += jnp.where(tril, dw_acc, 0.0)
            bacc_ref[hd] += b_acc
            dvec_ref[0:1, sl] += jnp.sum(dvn * xhat, axis=0, keepdims=True)
            dvec_ref[1:2, sl] += jnp.sum(dvn, axis=0, keepdims=True)
            dxh = dvn * lng_h
            dgv_in = rstd * (dxh - jnp.mean(dxh, axis=-1, keepdims=True)
                             - xhat * jnp.mean(dxh * xhat, axis=-1, keepdims=True))
            duv_ref[:, MIX_HALF + hd * HEAD_LANES:MIX_HALF + (hd + 1) * HEAD_LANES] = (dgv_in * dgv).astype(BF16)

        @pl.when(step == nsteps - 1)
        def _():
            for hd in range(SGU_HEADS):
                sl = slice(hd * HEAD_LANES, (hd + 1) * HEAD_LANES)
                dvec_ref[2:3, sl] = jnp.sum(bacc_ref[hd].T, axis=0, keepdims=True)

    return pl.pallas_call(
        body, name=name, grid=(nsteps,),
        in_specs=[pl.BlockSpec((tm, MIX_HALF), lambda i: (i, 0)),
                  pl.BlockSpec((tm, MIX_HALF), lambda i: (i, 1)),
                  pl.BlockSpec((tm, MIX_HALF), lambda i: (i, 0)),
                  pl.BlockSpec((1, MIX_HALF), lambda i: (0, 0)),
                  pl.BlockSpec((1, MIX_HALF), lambda i: (0, 0)),
                  pl.BlockSpec(w.shape, lambda i: (0, 0, 0)),
                  pl.BlockSpec(w.shape, lambda i: (0, 0, 0)),
                  pl.BlockSpec(bcol.shape, lambda i: (0, 0))],
        out_specs=[pl.BlockSpec((tm, 2 * MIX_HALF), lambda i: (i, 0)),
                   pl.BlockSpec(w.shape, lambda i: (0, 0, 0)),
                   pl.BlockSpec((8, MIX_HALF), lambda i: (0, 0))],
        out_shape=[jax.ShapeDtypeStruct((T, 2 * MIX_HALF), BF16),
                   jax.ShapeDtypeStruct(w.shape, F32),
                   jax.ShapeDtypeStruct((8, MIX_HALF), F32)],
        scratch_shapes=[pltpu.VMEM((SGU_HEADS, ATT_BLOCK, HEAD_LANES), F32)],
        compiler_params=_params(("arbitrary",)),
    )(proj, proj, dmixed, lng, lnb, w, wt, bcol)


def _rot_half(t):
    lane = lax.broadcasted_iota(jnp.int32, t.shape, 1)
    first = (lane % ATT_DH) < (ATT_DH // 2)
    return jnp.where(first, -pltpu.roll(t, HEAD_LANES - ATT_DH // 2, 1), pltpu.roll(t, ATT_DH // 2, 1))


LAYOUT_ROWS = 512


def _res_spec(d, tm, W):
    return pl.BlockSpec((d, tm // d, W), lambda i: (0, i, 0))


def _res_shape(d, T, W, dtype):
    return jax.ShapeDtypeStruct((d, T // d, W), dtype)


def _slab_buf(tm, W):
    return pltpu.VMEM((W // HEAD_LANES, tm, HEAD_LANES), F32)


def _lanes(hp):
    return slice(hp * HEAD_LANES, (hp + 1) * HEAD_LANES)


def _to_res(buf, out_ref, d, dtype):
    nslab, tm, _ = buf.shape
    for hp in range(nslab):
        if d == 1:
            out_ref[0, :, _lanes(hp)] = buf[hp].astype(dtype)
        else:
            for r in range(d):
                out_ref[r, :, _lanes(hp)] = buf.at[hp][pl.ds(r, tm // d, stride=d), :].astype(dtype)


def _from_res(in_ref, buf, d):
    nslab, tm, _ = buf.shape
    for hp in range(nslab):
        if d == 1:
            buf[hp] = in_ref[0, :, _lanes(hp)]
        else:
            for r in range(d):
                buf.at[hp][pl.ds(r, tm // d, stride=d), :] = in_ref[r, :, _lanes(hp)]


def rope_fwd(proj, cos, sin, name):
    T = proj.shape[0]
    tm = LAYOUT_ROWS
    scale = 1.0 / math.sqrt(ATT_DH)
    nd = len(DILATIONS)

    def body(q_ref, k_ref, v_ref, cos_ref, sin_ref, *rest):
        outs, buf = rest[:3 * nd], rest[3 * nd]
        c = cos_ref[...]
        s = sin_ref[...]
        for which, src in enumerate((q_ref, k_ref, v_ref)):
            for hp in range(MIX_HALF // HEAD_LANES):
                t = src[:, _lanes(hp)].astype(F32)
                if which == 0:
                    t = scale * (t * c + _rot_half(t) * s)
                elif which == 1:
                    t = t * c + _rot_half(t) * s
                buf[hp] = t
            for di, d in enumerate(DILATIONS):
                _to_res(buf, outs[3 * di + which], d, BF16)

    return pl.pallas_call(
        body, name=name, grid=(T // tm,),
        in_specs=[pl.BlockSpec((tm, MIX_HALF), lambda i: (i, 2)),
                  pl.BlockSpec((tm, MIX_HALF), lambda i: (i, 3)),
                  pl.BlockSpec((tm, MIX_HALF), lambda i: (i, 4)),
                  pl.BlockSpec((tm, HEAD_LANES), lambda i: (i, 0)),
                  pl.BlockSpec((tm, HEAD_LANES), lambda i: (i, 0))],
        out_specs=[_res_spec(d, tm, MIX_HALF) for d in DILATIONS for _ in range(3)],
        out_shape=[_res_shape(d, T, MIX_HALF, BF16) for d in DILATIONS for _ in range(3)],
        scratch_shapes=[_slab_buf(tm, MIX_HALF)],
        compiler_params=_params(("parallel",)),
    )(proj, proj, proj, cos, sin)


def to_residues(x, col, name):
    T = x.shape[0]
    tm = LAYOUT_ROWS

    def body(x_ref, *rest):
        outs, buf = rest[:-1], rest[-1]
        for hp in range(MIX_HALF // HEAD_LANES):
            buf[hp] = x_ref[:, _lanes(hp)].astype(F32)
        for o_ref, d in zip(outs, DILATIONS):
            _to_res(buf, o_ref, d, BF16)

    return pl.pallas_call(
        body, name=name, grid=(T // tm,),
        in_specs=[pl.BlockSpec((tm, MIX_HALF), lambda i: (i, col))],
        out_specs=[_res_spec(d, tm, MIX_HALF) for d in DILATIONS],
        out_shape=[_res_shape(d, T, MIX_HALF, BF16) for d in DILATIONS],
        scratch_shapes=[_slab_buf(tm, MIX_HALF)],
        compiler_params=_params(("parallel",)),
    )(x)


def rope_bwd(dqs, dks, dvs, cos, sin, name):
    T = dqs[0].shape[0] * dqs[0].shape[1]
    tm = LAYOUT_ROWS
    scale = 1.0 / math.sqrt(ATT_DH)
    npat = len(dqs)

    def body(*refs):
        groups = refs[:npat], refs[npat:2 * npat], refs[2 * npat:3 * npat]
        cos_ref, sin_ref, o_ref, buf, acc = refs[3 * npat:]
        c = cos_ref[...]
        s = sin_ref[...]
        for which, g_refs in enumerate(groups):
            _from_res(g_refs[0], acc, DILATIONS[0])
            for g_ref, d in zip(g_refs[1:], DILATIONS[1:]):
                _from_res(g_ref, buf, d)
                acc[...] += buf[...]
            for hp in range(MIX_HALF // HEAD_LANES):
                g = acc[hp]
                if which == 0:
                    g = scale * g
                if which < 2:
                    g = g * c - _rot_half(g * s)
                o_ref[:, which * MIX_HALF + hp * HEAD_LANES:which * MIX_HALF + (hp + 1) * HEAD_LANES] = g.astype(BF16)

    return pl.pallas_call(
        body, name=name, grid=(T // tm,),
        in_specs=([_res_spec(d, tm, MIX_HALF) for _ in range(3) for d in DILATIONS]
                  + [pl.BlockSpec((tm, HEAD_LANES), lambda i: (i, 0))] * 2),
        out_specs=pl.BlockSpec((tm, 3 * MIX_HALF), lambda i: (i, 0)),
        out_shape=jax.ShapeDtypeStruct((T, 3 * MIX_HALF), BF16),
        scratch_shapes=[_slab_buf(tm, MIX_HALF), _slab_buf(tm, MIX_HALF)],
        compiler_params=_params(("parallel",)),
    )(*dqs, *dks, *dvs, cos, sin)


def _band_masks(n):
    r = lax.broadcasted_iota(jnp.int32, (2 * ATT_BLOCK, ATT_BLOCK), 0)
    c = lax.broadcasted_iota(jnp.int32, (2 * ATT_BLOCK, ATT_BLOCK), 1)
    qi = r % ATT_BLOCK
    head = (c < ATT_DH) == (r < ATT_BLOCK)
    return (c >= qi) & (n > 0), c <= qi, head, c[:ATT_BLOCK] < ATT_DH


def _stack_heads(x, head):
    x2 = jnp.concatenate([x, x], axis=0)
    return jnp.where(head, x2, jnp.zeros_like(x2))


def attn_fwd(q, k, v, name):
    d, L, W = q.shape
    nb = L // ATT_BLOCK

    def body(q_ref, kp_ref, kc_ref, vp_ref, vc_ref, o_ref, lse_ref):
        mask_p, mask_c, head, head0 = _band_masks(pl.program_id(1))
        for hp in range(W // HEAD_LANES):
            sl = slice(hp * HEAD_LANES, (hp + 1) * HEAD_LANES)
            kp, kc, vp, vc = kp_ref[0, :, sl], kc_ref[0, :, sl], vp_ref[0, :, sl], vc_ref[0, :, sl]
            qs = _stack_heads(q_ref[0, :, sl], head)
            sp = jnp.where(mask_p, _nt(qs, kp), NEG)
            sc = jnp.where(mask_c, _nt(qs, kc), NEG)
            m = jnp.maximum(jnp.max(sp, axis=1, keepdims=True), jnp.max(sc, axis=1, keepdims=True))
            pp = jnp.exp(sp - m)
            pc = jnp.exp(sc - m)
            den = jnp.sum(pp, axis=1, keepdims=True) + jnp.sum(pc, axis=1, keepdims=True)
            o = (_nn(pp.astype(BF16), vp) + _nn(pc.astype(BF16), vc)) / den
            lse = m + jnp.log(den)
            o_ref[0, :, sl] = jnp.where(head0, o[:ATT_BLOCK], o[ATT_BLOCK:])
            lse_ref[0, :, sl] = jnp.where(head0, lse[:ATT_BLOCK], lse[ATT_BLOCK:])

    cur = pl.BlockSpec((1, ATT_BLOCK, W), lambda r, n: (r, n, 0))
    prev = pl.BlockSpec((1, ATT_BLOCK, W), lambda r, n: (r, jnp.maximum(n - 1, 0), 0))
    out = jax.ShapeDtypeStruct((d, L, W), F32)
    return pl.pallas_call(
        body, name=name, grid=(d, nb),
        in_specs=[cur, prev, cur, prev, cur],
        out_specs=[cur, cur], out_shape=[out, out],
        compiler_params=_params(("parallel", "parallel")),
    )(q, k, k, v, v)


def attn_combine(os_, lses, name):
    T = os_[0].shape[0] * os_[0].shape[1]
    W = os_[0].shape[2]
    tm = LAYOUT_ROWS
    npat = len(os_)

    def body(*refs):
        o_refs, l_refs = refs[:npat], refs[npat:2 * npat]
        out_ref = refs[2 * npat]
        ores, lres = refs[2 * npat + 1:3 * npat + 1], refs[3 * npat + 1:4 * npat + 1]
        bufs = refs[4 * npat + 1:]
        lbufs, obufs, out_buf, lse_buf = bufs[:npat], bufs[npat:2 * npat], bufs[2 * npat], bufs[2 * npat + 1]
        for p, d in enumerate(DILATIONS):
            _from_res(l_refs[p], lbufs[p], d)
            _from_res(o_refs[p], obufs[p], d)
        for hp in range(W // HEAD_LANES):
            ls = [b[hp] for b in lbufs]
            m = ls[0]
            for l in ls[1:]:
                m = jnp.maximum(m, l)
            es = [jnp.exp(l - m) for l in ls]
            z = es[0]
            for e in es[1:]:
                z = z + e
            acc = es[0] * obufs[0][hp]
            for p in range(1, npat):
                acc = acc + es[p] * obufs[p][hp]
            out = acc / z
            out_ref[:, _lanes(hp)] = out.astype(BF16)
            out_buf[hp] = out
            lse_buf[hp] = m + jnp.log(z)
        for p, d in enumerate(DILATIONS):
            _to_res(out_buf, ores[p], d, BF16)
            _to_res(lse_buf, lres[p], d, F32)

    return pl.pallas_call(
        body, name=name, grid=(T // tm,),
        in_specs=[_res_spec(d, tm, W) for _ in range(2) for d in DILATIONS],
        out_specs=([pl.BlockSpec((tm, W), lambda i: (i, 0))] + [_res_spec(d, tm, W) for _ in range(2) for d in DILATIONS]),
        out_shape=([jax.ShapeDtypeStruct((T, W), BF16)] + [_res_shape(d, T, W, BF16) for d in DILATIONS]
                   + [_res_shape(d, T, W, F32) for d in DILATIONS]),
        scratch_shapes=[_slab_buf(tm, W)] * (2 * npat + 2),
        compiler_params=_params(("parallel",)),
    )(*os_, *lses)


def attn_bwd(q, k, v, do, o, lse, name):
    d, L, W = q.shape
    nb = L // ATT_BLOCK

    def body(q_ref, kp_ref, kc_ref, vp_ref, vc_ref, do_ref, o_ref, lse_ref, dq_ref, dk_ref, dv_ref, kkeep, vkeep):
        n = pl.program_id(1)

        @pl.when(n == 0)
        def _():
            kkeep[...] = jnp.zeros_like(kkeep)
            vkeep[...] = jnp.zeros_like(vkeep)

        @pl.when(n < nb)
        def _():
            mask_p, mask_c, head, head0 = _band_masks(n)
            for hp in range(W // HEAD_LANES):
                sl = slice(hp * HEAD_LANES, (hp + 1) * HEAD_LANES)
                kp, kc, vp, vc = kp_ref[0, :, sl], kc_ref[0, :, sl], vp_ref[0, :, sl], vc_ref[0, :, sl]
                dout = do_ref[0, :, sl]
                qs = _stack_heads(q_ref[0, :, sl], head)
                dos = _stack_heads(dout, head)
                lse_v = lse_ref[0, :, sl]
                lse_c = jnp.max(jnp.where(head, jnp.concatenate([lse_v, lse_v], axis=0), NEG), axis=1, keepdims=True)
                delta = jnp.sum(_stack_heads(dout.astype(F32) * o_ref[0, :, sl].astype(F32), head), axis=1, keepdims=True)
                pp = jnp.exp(jnp.where(mask_p, _nt(qs, kp), NEG) - lse_c)
                pc = jnp.exp(jnp.where(mask_c, _nt(qs, kc), NEG) - lse_c)
                dsp = (pp * (_nt(dos, vp) - delta)).astype(BF16)
                dsc = (pc * (_nt(dos, vc) - delta)).astype(BF16)
                dq2 = _nn(dsp, kp) + _nn(dsc, kc)
                dq_ref[0, :, sl] = jnp.where(head0, dq2[:ATT_BLOCK], dq2[ATT_BLOCK:])
                dk_ref[0, :, sl] = kkeep[:, sl] + _tn(dsp, qs)
                dv_ref[0, :, sl] = vkeep[:, sl] + _tn(pp.astype(BF16), dos)
                kkeep[:, sl] = _tn(dsc, qs)
                vkeep[:, sl] = _tn(pc.astype(BF16), dos)

        @pl.when(n == nb)
        def _():
            dk_ref[0] = kkeep[...]
            dv_ref[0] = vkeep[...]

    cur = pl.BlockSpec((1, ATT_BLOCK, W), lambda r, n: (r, jnp.minimum(n, nb - 1), 0))
    prev = pl.BlockSpec((1, ATT_BLOCK, W), lambda r, n: (r, jnp.clip(n - 1, 0, nb - 1), 0))
    out = jax.ShapeDtypeStruct((d, L, W), F32)
    return pl.pallas_call(
        body, name=name, grid=(d, nb + 1),
        in_specs=[cur, prev, cur, prev, cur, cur, cur, cur],
        out_specs=[cur, prev, prev], out_shape=[out, out, out],
        scratch_shapes=[pltpu.VMEM((ATT_BLOCK, W), F32), pltpu.VMEM((ATT_BLOCK, W), F32)],
        compiler_params=_params(("parallel", "arbitrary")),
    )(q, k, k, v, v, do, o, lse)


def final_loss_bwd(h, gf, tgt, gate, name):
    T, D = h.shape
    tm = _tile(T, 512, 16)
    g_in, g_out, g_shape, g_ops = _gate_specs(gate, tm, D)

    def body(h_ref, g_ref, t_ref, o_ref, modg_ref, dh_ref, red_ref, do_ref, redg_ref):
        x = h_ref[...]
        r = lax.rsqrt(jnp.mean(x * x, axis=-1, keepdims=True) + EPS)
        n = x * r
        g = g_ref[...]
        err = n * g - t_ref[...]
        dy = err * (1.0 / D)

        @pl.when(pl.program_id(0) == 0)
        def _():
            red_ref[...] = jnp.zeros_like(red_ref)

        red_ref[0:1, :] += jnp.sum(dy * n, axis=0, keepdims=True)
        red_ref[1:2, :] += jnp.zeros((1, D), F32) + (0.5 / D) * jnp.sum(err * err, keepdims=True)
        dn = dy * g
        dh = r * (dn - n * jnp.mean(dn * n, axis=-1, keepdims=True))
        dh_ref[...] = dh
        _gate_emit(dh, gate, o_ref, modg_ref, do_ref, redg_ref)

    return pl.pallas_call(
        body, name=name, grid=(T // tm,),
        in_specs=[pl.BlockSpec((tm, D), lambda i: (i, 0)),
                  pl.BlockSpec((1, D), lambda i: (0, 0)),
                  pl.BlockSpec((tm, D), lambda i: (i, 0))] + g_in,
        out_specs=[pl.BlockSpec((tm, D), lambda i: (i, 0)), pl.BlockSpec((8, D), lambda i: (0, 0))] + g_out,
        out_shape=[jax.ShapeDtypeStruct((T, D), F32), jax.ShapeDtypeStruct((8, D), F32)] + g_shape,
        compiler_params=_params(("arbitrary",)),
    )(h, gf, tgt, *g_ops)


def ada_fwd(c_all, ada_w, ada_b, name):
    nl, D, N = ada_w.shape

    def body(c_ref, w_ref, b_ref, o_ref):
        c = c_ref[...]
        o_ref[0] = _nn(c * _sigmoid(c), w_ref[0]) + b_ref[0]

    return pl.pallas_call(
        body, name=name, grid=(nl,),
        in_specs=[pl.BlockSpec((N_DEV, D), lambda l: (0, 0)),
                  pl.BlockSpec((1, D, N), lambda l: (l, 0, 0)),
                  pl.BlockSpec((1, 1, N), lambda l: (l, 0, 0))],
        out_specs=pl.BlockSpec((1, N_DEV, N), lambda l: (l, 0, 0)),
        out_shape=jax.ShapeDtypeStruct((nl, N_DEV, N), F32),
        compiler_params=_params(("parallel",)),
    )(c_all, ada_w, ada_b)


def ada_bwd(c_allT, dmod, name):
    nl, _, N = dmod.shape
    D = c_allT.shape[0]

    def body(c_ref, g_ref, o_ref):
        c = c_ref[...]
        ca = c * _sigmoid(c)
        acc = ca[:, 0:1] * g_ref[0, 0:1, :]
        for b in range(1, N_DEV):
            acc = acc + ca[:, b:b + 1] * g_ref[0, b:b + 1, :]
        o_ref[0] = acc

    return pl.pallas_call(
        body, name=name, grid=(nl,),
        in_specs=[pl.BlockSpec((D, N_DEV), lambda l: (0, 0)),
                  pl.BlockSpec((1, N_DEV, N), lambda l: (l, 0, 0))],
        out_specs=pl.BlockSpec((1, D, N), lambda l: (l, 0, 0)),
        out_shape=jax.ShapeDtypeStruct((nl, D, N), F32),
        compiler_params=_params(("parallel",)),
    )(c_allT, dmod)


def adamw(w, g, m, v, name):
    R, C = w.shape
    tr = _tile(R, max(8, (1 << 19) // C // 8 * 8), 8)
    c1 = 1.0 - ADAM_B1 ** ADAM_STEP
    c2 = 1.0 - ADAM_B2 ** ADAM_STEP

    def body(w_ref, g_ref, m_ref, v_ref, d_ref, mo_ref, vo_ref):
        gv = g_ref[...]
        mn = ADAM_B1 * m_ref[...] + (1.0 - ADAM_B1) * gv
        vn = ADAM_B2 * v_ref[...] + (1.0 - ADAM_B2) * (gv * gv)
        mo_ref[...] = mn
        vo_ref[...] = vn
        d_ref[...] = -ADAM_LR * ((mn / c1) / (jnp.sqrt(vn / c2) + ADAM_EPS) + ADAM_WD * w_ref[...])

    blk = pl.BlockSpec((tr, C), lambda i: (i, 0))
    out = jax.ShapeDtypeStruct((R, C), F32)
    return pl.pallas_call(
        body, name=name, grid=(R // tr,),
        in_specs=[blk] * 4, out_specs=[blk] * 3, out_shape=[out] * 3,
        compiler_params=_params(("parallel",)),
    )(w, g, m, v)


def adamw_layer(w, g, m, v, l, prev, name):
    NLw, R, C = w.shape
    tr = _tile(R, max(8, (1 << 19) // C // 8 * 8), 8)
    nrb = R // tr
    c1 = 1.0 - ADAM_B1 ** ADAM_STEP
    c2 = 1.0 - ADAM_B2 ** ADAM_STEP
    w, m, v = (t.reshape(NLw * R, C) for t in (w, m, v))

    def body(w_ref, g_ref, m_ref, v_ref, *rest):
        go_ref, d_ref, mo_ref, vo_ref = rest[-4:]
        gv = g_ref[...]
        mn = ADAM_B1 * m_ref[...] + (1.0 - ADAM_B1) * gv
        vn = ADAM_B2 * v_ref[...] + (1.0 - ADAM_B2) * (gv * gv)
        go_ref[...] = gv
        mo_ref[...] = mn
        vo_ref[...] = vn
        d_ref[...] = -ADAM_LR * ((mn / c1) / (jnp.sqrt(vn / c2) + ADAM_EPS) + ADAM_WD * w_ref[...])

    lay = pl.BlockSpec((tr, C), lambda i: (l * nrb + i, 0))
    out = jax.ShapeDtypeStruct((NLw * R, C), F32)
    n_prev = 0 if prev is None else 4
    return pl.pallas_call(
        body, name=name, grid=(nrb,),
        in_specs=[lay, pl.BlockSpec((tr, C), lambda i: (i, 0)), lay, lay] + [pl.BlockSpec(memory_space=pl.ANY)] * n_prev,
        out_specs=[lay] * 4, out_shape=[out] * 4,
        input_output_aliases={4 + i: i for i in range(n_prev)},
        compiler_params=_params(("parallel",)),
    )(w, g, m, v, *(prev or ()))


def sum_slots(x, name):
    S, R, C = x.shape
    tr = _tile(R, 128, 8)

    def body(x_ref, o_ref):
        acc = x_ref[0]
        for s in range(1, S):
            acc = acc + x_ref[s]
        o_ref[...] = acc

    return pl.pallas_call(
        body, name=name, grid=(R // tr,),
        in_specs=[pl.BlockSpec((S, tr, C), lambda i: (0, i, 0))],
        out_specs=pl.BlockSpec((tr, C), lambda i: (i, 0)),
        out_shape=jax.ShapeDtypeStruct((R, C), F32),
        compiler_params=_params(("parallel",)),
    )(x)


def sum_halves(g, lands, c_idx, name):
    n, ns, _, rh, D = g.shape

    def body(c_ref, g_ref, l_ref, o_ref):
        o_ref[0, 0] = (g_ref[0, 0, 0].astype(F32) + l_ref[0, 0].astype(F32)).astype(BF16)

    return pl.pallas_call(
        body, name=name,
        grid_spec=pltpu.PrefetchScalarGridSpec(
            num_scalar_prefetch=1, grid=(n, ns),
            in_specs=[pl.BlockSpec((1, 1, 1, rh, D), lambda i, j, c: (i, j, c[0], 0, 0)),
                      pl.BlockSpec((1, 1, rh, D), lambda i, j, c: (i, j, 0, 0))],
            out_specs=pl.BlockSpec((1, 1, rh, D), lambda i, j, c: (i, j, 0, 0))),
        out_shape=jax.ShapeDtypeStruct((n, ns, rh, D), BF16),
        compiler_params=_params(("parallel", "parallel")),
    )(c_idx, g, lands)


def sum_chips(p, lands, place, name):
    n, ns, rh, D = p.shape

    def body(c_ref, p_ref, l_ref, o_ref):
        acc = p_ref[0, 0].astype(F32)
        for j in range(N_CHIP - 1):
            acc = acc + l_ref[j, 0].astype(F32)
        o_ref[0, 0] = acc

    return pl.pallas_call(
        body, name=name,
        grid_spec=pltpu.PrefetchScalarGridSpec(
            num_scalar_prefetch=1, grid=(n,),
            in_specs=[pl.BlockSpec((1, 1, rh, D), lambda i, c: (i, c[0], 0, 0)),
                      pl.BlockSpec((N_CHIP - 1, 1, rh, D), lambda i, c: (0, i, 0, 0))],
            out_specs=pl.BlockSpec((1, 1, rh, D), lambda i, c: (i, c[1], 0, 0))),
        out_shape=jax.ShapeDtypeStruct((n, 2, rh, D), F32),
        compiler_params=_params(("parallel",)),
    )(place, p, lands)


def _my_place():
    return lax.axis_index("x"), lax.axis_index("y"), lax.axis_index("c")


def _other_chips(mx, my):
    return [(1 - mx, my), (mx, 1 - my), (1 - mx, 1 - my)]


def gather_small(x, after, name):
    def body(x_ref, after_ref, out_ref, sum_ref, send_sems, recv_sems):
        mx, my, mc = _my_place()
        me = 4 * mx + 2 * my + mc
        out_ref[me] = x_ref[...]
        sends = []
        for k in range(1, N_DEV):
            kx, ky, kc = (k >> 2) & 1, (k >> 1) & 1, k & 1
            peer = (1 - mx if kx else mx, 1 - my if ky else my, 1 - mc if kc else mc)
            cp = pltpu.make_async_remote_copy(
                src_ref=x_ref, dst_ref=out_ref.at[me], send_sem=send_sems.at[k - 1], recv_sem=recv_sems.at[k - 1],
                device_id=peer, device_id_type=MESH)
            cp.start()
            sends.append((cp, 4 * peer[0] + 2 * peer[1] + peer[2], peer))
        for k, (cp, peer_slot, peer) in enumerate(sends):
            pltpu.make_async_remote_copy(
                src_ref=x_ref, dst_ref=out_ref.at[peer_slot], send_sem=send_sems.at[k], recv_sem=recv_sems.at[k],
                device_id=peer, device_id_type=MESH).wait_recv()
        for cp, _, _ in sends:
            cp.wait_send()
        acc = out_ref[0]
        for s in range(1, N_DEV):
            acc = acc + out_ref[s]
        sum_ref[...] = acc

    vmem = pl.BlockSpec(memory_space=pltpu.VMEM)
    return pl.pallas_call(
        body, name=name,
        in_specs=[vmem, pl.BlockSpec(memory_space=pl.ANY)], out_specs=[vmem, vmem],
        out_shape=[jax.ShapeDtypeStruct((N_DEV,) + x.shape, x.dtype), jax.ShapeDtypeStruct(x.shape, x.dtype)],
        scratch_shapes=[pltpu.SemaphoreType.DMA((N_DEV - 1,)), pltpu.SemaphoreType.DMA((N_DEV - 1,))],
        compiler_params=pltpu.CompilerParams(vmem_limit_bytes=VMEM_LIMIT),
    )(x, after)


_HBM =pl.BlockSpec(memory_space=pltpu.HBM)
_SEM = pl.BlockSpec(memory_space=pltpu.SEMAPHORE)
_DATAFLOW = pltpu.SideEffectType.DATAFLOW_SIDE_EFFECTING


def _gather_copies(shard, land, send, recv, base):
    mx, my, mc = _my_place()
    ci = 2 * mx + my
    peers = [((cx, cy, mc), 2 * cx + cy) for cx, cy in _other_chips(mx, my)] + [((mx, my, 1 - mc), ci)]
    out = []
    for q, (dev, src_slot) in enumerate(peers):
        out.append((
            pltpu.make_async_remote_copy(src_ref=shard, dst_ref=land.at[:, ci], send_sem=send.at[base + q],
                                         recv_sem=recv.at[base + q], device_id=dev, device_id_type=MESH),
            pltpu.make_async_remote_copy(src_ref=shard, dst_ref=land.at[:, src_slot], send_sem=send.at[base + q],
                                         recv_sem=recv.at[base + q], device_id=dev, device_id_type=MESH)))
    return out


def gather_start(groups, after, name):
    items = [s for g in groups for s in g]
    ni, ng = len(items), len(groups)

    def body(*refs):
        shards, lands = refs[:ni], refs[ni:2 * ni]
        sems = refs[2 * ni + 1:2 * ni + 1 + 2 * ng]
        token = refs[-1]
        i = 0
        for g, grp in enumerate(groups):
            for p in range(len(grp)):
                for start_cp, _ in _gather_copies(shards[i], lands[i], sems[2 * g], sems[2 * g + 1], 4 * p):
                    start_cp.start()
                i += 1
        token[...] = jnp.zeros_like(token)

    sem_shapes = []
    for grp in groups:
        sem_shapes += [pltpu.SemaphoreType.DMA((4 * len(grp),))] * 2
    land_shapes = [(s.shape[0], N_CHIP) + s.shape[1:] for s in items]
    outs = pl.pallas_call(
        body, name=name,
        in_specs=[_HBM] * (2 * ni) + [pl.BlockSpec(memory_space=pl.ANY)],
        out_specs=[_SEM] * (2 * ng) + [_HBM] * (2 * ni) + [pl.BlockSpec(memory_space=pltpu.VMEM)],
        out_shape=(sem_shapes + [pltpu.HBM(s.shape, s.dtype) for s in items]
                   + [pltpu.HBM(ls, s.dtype) for ls, s in zip(land_shapes, items)]
                   + [jax.ShapeDtypeStruct((8, 128), F32)]),
        input_output_aliases={i: 2 * ng + i for i in range(2 * ni)},
        compiler_params=pltpu.CompilerParams(has_side_effects=_DATAFLOW),
    )(*[pltpu.with_memory_space_constraint(s, pltpu.HBM) for s in items],
      *[pltpu.with_memory_space_constraint(lax.empty(ls, s.dtype), pltpu.HBM) for ls, s in zip(land_shapes, items)],
      after)
    sems, thru, token = outs[:2 * ng], outs[2 * ng:2 * ng + 2 * ni], outs[-1]
    handles, i = [], 0
    for g, grp in enumerate(groups):
        n = len(grp)
        handles.append((sems[2 * g], sems[2 * g + 1], thru[i:i + n], thru[ni + i:ni + i + n]))
        i += n
    return handles, token


def gather_wait(handle, after, name):
    send, recv, shards, lands = handle
    n = len(shards)

    def body(*refs):
        shard_refs, land_refs = refs[:n], refs[n:2 * n]
        send_ref, recv_ref = refs[2 * n], refs[2 * n + 1]
        for p in range(n):
            for start_cp, recv_cp in _gather_copies(shard_refs[p], land_refs[p], send_ref, recv_ref, 4 * p):
                start_cp.wait_send()
                recv_cp.wait_recv()

    outs = pl.pallas_call(
        body, name=name,
        in_specs=[_HBM] * (2 * n) + [_SEM, _SEM, pl.BlockSpec(memory_space=pl.ANY)],
        out_specs=[_HBM] * (2 * n),
        out_shape=[pltpu.HBM(s.shape, s.dtype) for s in shards] + [pltpu.HBM(l.shape, l.dtype) for l in lands],
        input_output_aliases={i: i for i in range(2 * n)},
        compiler_params=pltpu.CompilerParams(has_side_effects=_DATAFLOW),
    )(*shards, *lands, send, recv, after)
    return outs[n:]


def _first_copies(shard, land, send, recv):
    mx, my, mc = _my_place()
    ci = 2 * mx + my
    out = []
    for q, (cx, cy) in enumerate(_other_chips(mx, my)):
        dev = (cx, cy, mc)
        out.append(tuple(pltpu.make_async_remote_copy(
            src_ref=shard.at[:, mc], dst_ref=land.at[:, slot, mc], send_sem=send.at[q], recv_sem=recv.at[q],
            device_id=dev, device_id_type=MESH) for slot in (ci, 2 * cx + cy)))
    sib = pltpu.make_async_remote_copy(src_ref=shard, dst_ref=land.at[:, ci], send_sem=send.at[3], recv_sem=recv.at[3],
                                       device_id=(mx, my, 1 - mc), device_id_type=MESH)
    return out + [(sib, sib)]


def _forward_copies(land, send, recv):
    mx, my, mc = _my_place()
    out = []
    for q, (cx, cy) in enumerate(_other_chips(mx, my)):
        out.append(tuple(pltpu.make_async_remote_copy(
            src_ref=land.at[:, 2 * cx + cy, hc], dst_ref=land.at[:, 2 * cx + cy, hc], send_sem=send.at[q],
            recv_sem=recv.at[q], device_id=(mx, my, 1 - mc), device_id_type=MESH) for hc in (mc, 1 - mc)))
    return out


def first_start(shard, after, name):
    def body(shard_ref, land_ref, after_ref, send, recv, shard_thru, land_thru, token):
        for mine, _ in _first_copies(shard_ref, land_ref, send, recv):
            mine.start()
        token[...] = jnp.zeros_like(token)

    land_shape = (shard.shape[0], N_CHIP) + shard.shape[1:]
    outs = pl.pallas_call(
        body, name=name,
        in_specs=[_HBM, _HBM, pl.BlockSpec(memory_space=pl.ANY)],
        out_specs=[_SEM, _SEM, _HBM, _HBM, pl.BlockSpec(memory_space=pltpu.VMEM)],
        out_shape=[pltpu.SemaphoreType.DMA((4,))] * 2 + [pltpu.HBM(shard.shape, shard.dtype),
                                                         pltpu.HBM(land_shape, shard.dtype),
                                                         jax.ShapeDtypeStruct((8, 128), F32)],
        input_output_aliases={0: 2, 1: 3},
        compiler_params=pltpu.CompilerParams(has_side_effects=_DATAFLOW),
    )(pltpu.with_memory_space_constraint(shard, pltpu.HBM),
      pltpu.with_memory_space_constraint(lax.empty(land_shape, shard.dtype), pltpu.HBM), after)
    return outs[:4], outs[4]


def first_forward(handle, after, name):
    send, recv, shard, land = handle

    def body(shard_ref, land_ref, send_ref, recv_ref, after_ref, send2, recv2, shard_thru, land_thru):
        firsts = _first_copies(shard_ref, land_ref, send_ref, recv_ref)
        forwards = _forward_copies(land_ref, send2, recv2)
        for q in range(3):
            firsts[q][1].wait_recv()
            forwards[q][0].start()
        firsts[3][1].wait_recv()
        for mine, _ in firsts:
            mine.wait_send()

    outs = pl.pallas_call(
        body, name=name,
        in_specs=[_HBM, _HBM, _SEM, _SEM, pl.BlockSpec(memory_space=pl.ANY)],
        out_specs=[_SEM, _SEM, _HBM, _HBM],
        out_shape=[pltpu.SemaphoreType.DMA((3,))] * 2 + [pltpu.HBM(shard.shape, shard.dtype),
                                                         pltpu.HBM(land.shape, land.dtype)],
        input_output_aliases={0: 2, 1: 3},
        compiler_params=pltpu.CompilerParams(has_side_effects=_DATAFLOW),
    )(shard, land, send, recv, after)
    return outs[0], outs[1], outs[3]


def first_wait(handle, after, name):
    send, recv, land = handle

    def body(land_ref, send_ref, recv_ref, after_ref, land_out):
        for mine, theirs in _forward_copies(land_ref, send_ref, recv_ref):
            mine.wait_send()
            theirs.wait_recv()

    return pl.pallas_call(
        body, name=name,
        in_specs=[_HBM, _SEM, _SEM, pl.BlockSpec(memory_space=pl.ANY)],
        out_specs=[_HBM],
        out_shape=[pltpu.HBM(land.shape, land.dtype)],
        input_output_aliases={0: 0},
        compiler_params=pltpu.CompilerParams(has_side_effects=_DATAFLOW),
    )(land, send, recv, after)[0]


def _sibling_copies(gs, lands, send, recv):
    mx, my, mc = _my_place()
    return [pltpu.make_async_remote_copy(
        src_ref=gs[k].at[:, :, 1 - mc], dst_ref=lands[k], send_sem=send.at[k], recv_sem=recv.at[k],
        device_id=(mx, my, 1 - mc), device_id_type=MESH) for k in range(len(gs))]


def sibling_start(gs, after, name):
    K = len(gs)

    def body(*refs):
        ins, lands = refs[:K], refs[K:2 * K]
        send, recv = refs[2 * K + 1], refs[2 * K + 2]
        for cp in _sibling_copies(ins, lands, send, recv):
            cp.start()
        refs[-1][...] = jnp.zeros_like(refs[-1])

    land_shapes = [g.shape[:2] + g.shape[3:] for g in gs]
    outs = pl.pallas_call(
        body, name=name,
        in_specs=[_HBM] * (2 * K) + [pl.BlockSpec(memory_space=pl.ANY)],
        out_specs=[_SEM, _SEM] + [_HBM] * (2 * K) + [pl.BlockSpec(memory_space=pltpu.VMEM)],
        out_shape=([pltpu.SemaphoreType.DMA((K,))] * 2 + [pltpu.HBM(g.shape, g.dtype) for g in gs]
                   + [pltpu.HBM(ls, g.dtype) for ls, g in zip(land_shapes, gs)] + [jax.ShapeDtypeStruct((8, 128), F32)]),
        input_output_aliases={i: 2 + i for i in range(2 * K)},
        compiler_params=pltpu.CompilerParams(has_side_effects=_DATAFLOW),
    )(*[pltpu.with_memory_space_constraint(g, pltpu.HBM) for g in gs],
      *[pltpu.with_memory_space_constraint(lax.empty(ls, g.dtype), pltpu.HBM) for ls, g in zip(land_shapes, gs)],
      after)
    return (outs[0], outs[1], outs[2:2 + K], outs[2 + K:2 + 2 * K]), outs[-1]


def sibling_wait(handle, after, name):
    send, recv, gs, lands = handle
    K = len(gs)

    def body(*refs):
        ins, land_refs = refs[:K], refs[K:2 * K]
        for cp in _sibling_copies(ins, land_refs, refs[2 * K], refs[2 * K + 1]):
            cp.wait_send()
            cp.wait_recv()

    outs = pl.pallas_call(
        body, name=name,
        in_specs=[_HBM] * (2 * K) + [_SEM, _SEM, pl.BlockSpec(memory_space=pl.ANY)],
        out_specs=[_HBM] * (2 * K),
        out_shape=[pltpu.HBM(g.shape, g.dtype) for g in gs] + [pltpu.HBM(l.shape, l.dtype) for l in lands],
        input_output_aliases={i: i for i in range(2 * K)},
        compiler_params=pltpu.CompilerParams(has_side_effects=_DATAFLOW),
    )(*gs, *lands, send, recv, after)
    return outs[:K], outs[K:]


def _small_copies(x, land, send, recv):
    mx, my, mc = _my_place()
    me = 4 * mx + 2 * my + mc
    out = []
    for k in range(1, N_DEV):
        peer = (1 - mx if k & 4 else mx, 1 - my if k & 2 else my, 1 - mc if k & 1 else mc)
        slot = 4 * peer[0] + 2 * peer[1] + peer[2]
        out.append(tuple(pltpu.make_async_remote_copy(
            src_ref=x, dst_ref=land.at[s], send_sem=send.at[k - 1], recv_sem=recv.at[k - 1],
            device_id=peer, device_id_type=MESH) for s in (me, slot)))
    return out


def small_start(x, after, name):
    def body(x_ref, land_ref, after_ref, send, recv, x_thru, land_thru, token):
        for mine, _ in _small_copies(x_ref, land_ref, send, recv):
            mine.start()
        token[...] = jnp.zeros_like(token)

    land_shape = (N_DEV,) + x.shape
    outs = pl.pallas_call(
        body, name=name,
        in_specs=[_HBM, _HBM, pl.BlockSpec(memory_space=pl.ANY)],
        out_specs=[_SEM, _SEM, _HBM, _HBM, pl.BlockSpec(memory_space=pltpu.VMEM)],
        out_shape=[pltpu.SemaphoreType.DMA((N_DEV - 1,))] * 2 + [pltpu.HBM(x.shape, x.dtype), pltpu.HBM(land_shape, x.dtype),
                                                                 jax.ShapeDtypeStruct((8, 128), F32)],
        input_output_aliases={0: 2, 1: 3},
        compiler_params=pltpu.CompilerParams(has_side_effects=_DATAFLOW),
    )(pltpu.with_memory_space_constraint(x, pltpu.HBM),
      pltpu.with_memory_space_constraint(lax.empty(land_shape, x.dtype), pltpu.HBM), after)
    return outs[:4], outs[4]


def small_wait(handle, after, name):
    send, recv, x, land = handle

    def body(x_ref, land_ref, send_ref, recv_ref, after_ref, x_out, land_out):
        for mine, theirs in _small_copies(x_ref, land_ref, send_ref, recv_ref):
            mine.wait_send()
            theirs.wait_recv()

    return pl.pallas_call(
        body, name=name,
        in_specs=[_HBM, _HBM, _SEM, _SEM, pl.BlockSpec(memory_space=pl.ANY)],
        out_specs=[_HBM, _HBM],
        out_shape=[pltpu.HBM(x.shape, x.dtype), pltpu.HBM(land.shape, land.dtype)],
        input_output_aliases={0: 0, 1: 1},
        compiler_params=pltpu.CompilerParams(has_side_effects=_DATAFLOW),
    )(x, land, send, recv, after)


def _scatter_copies(ps, lands, send, recv):
    mx, my, mc = _my_place()
    cps = []
    for j, (cx, cy) in enumerate(_other_chips(mx, my)):
        for k in range(len(ps)):
            cps.append(pltpu.make_async_remote_copy(
                src_ref=ps[k].at[:, 2 * cx + cy], dst_ref=lands[k].at[j],
                send_sem=send.at[k * 3 + j], recv_sem=recv.at[k * 3 + j],
                device_id=(cx, cy, mc), device_id_type=MESH))
    return cps


def scatter_start(ps, after, name):
    K = len(ps)

    def body(*refs):
        ins, lands = refs[:K], refs[K:2 * K]
        send, recv = refs[2 * K + 1], refs[2 * K + 2]
        for cp in _scatter_copies(ins, lands, send, recv):
            cp.start()
        refs[-1][...] = jnp.zeros_like(refs[-1])

    land_shapes = [(N_CHIP - 1, p.shape[0]) + p.shape[2:] for p in ps]
    outs = pl.pallas_call(
        body, name=name,
        in_specs=[_HBM] * (2 * K) + [pl.BlockSpec(memory_space=pl.ANY)],
        out_specs=[_SEM, _SEM] + [_HBM] * (2 * K) + [pl.BlockSpec(memory_space=pltpu.VMEM)],
        out_shape=([pltpu.SemaphoreType.DMA((3 * K,))] * 2 + [pltpu.HBM(p.shape, p.dtype) for p in ps]
                   + [pltpu.HBM(ls, p.dtype) for ls, p in zip(land_shapes, ps)] + [jax.ShapeDtypeStruct((8, 128), F32)]),
        input_output_aliases={i: 2 + i for i in range(2 * K)},
        compiler_params=pltpu.CompilerParams(has_side_effects=_DATAFLOW),
    )(*[pltpu.with_memory_space_constraint(p, pltpu.HBM) for p in ps],
      *[pltpu.with_memory_space_constraint(lax.empty(ls, p.dtype), pltpu.HBM) for ls, p in zip(land_shapes, ps)],
      after)
    return (outs[0], outs[1], outs[2:2 + K], outs[2 + K:2 + 2 * K]), outs[-1]


def scatter_wait(handle, after, name):
    send, recv, ps, lands = handle
    K = len(ps)
    afters = list(after) if isinstance(after, (list, tuple)) else [after]

    def body(*refs):
        ins, land_refs = refs[:K], refs[K:2 * K]
        send_ref, recv_ref = refs[2 * K], refs[2 * K + 1]
        for cp in _scatter_copies(ins, land_refs, send_ref, recv_ref):
            cp.wait_send()
            cp.wait_recv()

    outs = pl.pallas_call(
        body, name=name,
        in_specs=[_HBM] * (2 * K) + [_SEM, _SEM] + [pl.BlockSpec(memory_space=pl.ANY)] * len(afters),
        out_specs=[_HBM] * (2 * K),
        out_shape=[pltpu.HBM(p.shape, p.dtype) for p in ps] + [pltpu.HBM(l.shape, l.dtype) for l in lands],
        input_output_aliases={i: i for i in range(2 * K)},
        compiler_params=pltpu.CompilerParams(has_side_effects=_DATAFLOW),
    )(*ps, *lands, send, recv, *afters)
    return outs[:K], outs[K:]


def sibling_complete(ss, name):
    K = len(ss)

    def body(*refs):
        ins, outs = refs[:K], refs[K:2 * K]
        send, recv = refs[2 * K:]
        mx, my, mc = _my_place()
        cps = []
        for k in range(K):
            cp = pltpu.make_async_remote_copy(
                src_ref=ins[k].at[:, mc], dst_ref=outs[k].at[:, mc], send_sem=send.at[k], recv_sem=recv.at[k],
                device_id=(mx, my, 1 - mc), device_id_type=MESH)
            cp.start()
            cps.append(cp)
        for k in range(K):
            pltpu.make_async_remote_copy(
                src_ref=ins[k].at[:, mc], dst_ref=outs[k].at[:, 1 - mc], send_sem=send.at[k], recv_sem=recv.at[k],
                device_id=(mx, my, 1 - mc), device_id_type=MESH).wait_recv()
        for cp in cps:
            cp.wait_send()

    hbm = pl.BlockSpec(memory_space=pl.ANY)
    return pl.pallas_call(
        body, name=name,
        in_specs=[hbm] * K, out_specs=[hbm] * K,
        out_shape=[jax.ShapeDtypeStruct(s.shape, s.dtype) for s in ss],
        scratch_shapes=[pltpu.SemaphoreType.DMA((K,)), pltpu.SemaphoreType.DMA((K,))],
        input_output_aliases={k: k for k in range(K)},
    )(*ss)


def _rope_tables(T):
    inv = ROPE_THETA ** (-jnp.arange(0, ATT_DH, 2, dtype=F32) / ATT_DH)
    ang = jnp.arange(T, dtype=F32)[:, None] * inv[None, :]
    ang = jnp.concatenate([ang, ang, ang, ang], axis=-1)
    return jnp.cos(ang), jnp.sin(ang)


def _ffn_fwd(h, y, mod, i0, get_up, get_down, norm_next, tag):
    wgu = get_up(h)
    a, b, s = ffn_up(y, (wgu, (0,)), (wgu, (1,)), f"ffn_up_{tag}")
    wd = get_down(s)
    outs = resid_matmul([s], (wd, (0,)), h, mod, i0 + 2, 0.5, f"ffn_down_{tag}", norm_next)
    hn, o = outs[0], outs[1]
    return hn, (outs[2] if norm_next else None), (h, y, a, b, s, o), ((wgu, (0,)), (wgu, (1,)), (wd, (0,)))


def _ffn_bwd(dh, do, res, ng, i_n, mod, i0, wgT, wuT, wd, on_grads, next_gate, tag):
    h, y, a, b, s, o = res
    F = _wrows(wgT)
    da, db = ffn_bwd_mid(do, wd, a, b, f"ffn_bwd_mid_{tag}")
    gbuf = lax.empty((3, F, h.shape[1]), BF16)
    gbuf = matmul_tn(da, y, gbuf, 0, 0, f"dwg_{tag}")
    gbuf = matmul_tn(db, y, gbuf, 1, 0, f"dwu_{tag}")
    gbuf = matmul_tn(s, do, gbuf, 2, 0, f"dwd_{tag}")
    token, then = on_grads([gbuf])
    outs = dy_normbwd([(da, 0, wgT, 0, F), (db, 0, wuT, 0, F)], h, dh, ng, i_n, mod + token, i0 + 1,
                      f"ffn_bwd_dy_{tag}", next_gate)
    return outs, then


def _mixer_fwd(h, y, mod, w_inT, w_out, sgu, cos, sin, norm_next, tag):
    lng, lnb, sw, swt, bcol = sgu
    proj = matmul_nt(y, w_inT, f"proj_{tag}")
    out_a = sgu_fwd(proj, lng, lnb, sw, bcol, f"sgu_fwd_{tag}")
    qkv = rope_fwd(proj, cos, sin, f"rope_fwd_{tag}")
    npat = len(DILATIONS)
    qkv_res = [tuple(qkv[3 * p:3 * p + 3]) for p in range(npat)]
    os_, lses = [], []
    for d, (qd, kd, vd) in zip(DILATIONS, qkv_res):
        o_d, lse_d = attn_fwd(qd, kd, vd, f"attn_fwd_d{d}_{tag}")
        os_.append(o_d)
        lses.append(lse_d)
    comb = attn_combine(os_, lses, f"attn_combine_{tag}")
    out_b, o_res, lse_res = comb[0], comb[1:1 + npat], comb[1 + npat:]
    outs = resid_matmul([out_a, out_b], w_out, h, mod, 5, 1.0, f"mix_out_{tag}", norm_next)
    hn, om = outs[0], outs[1]
    return hn, (outs[2] if norm_next else None), (h, y, proj, out_a, out_b, o_res, lse_res, qkv_res, om)


def _mixer_bwd(dh, dom, res, ng, mod, w_inT, w_out, sgu, cos, sin, on_grads, next_gate, tag):
    lng, lnb, sw, swt, bcol = sgu
    h, y, proj, out_a, out_b, o_res, lse_res, qkv_res, om = res
    D = h.shape[1]
    dmixed = matmul_nt(dom, w_out, f"dmixed_{tag}")
    woutbuf = lax.empty((1, 2 * MIX_HALF, D), BF16)
    woutbuf = matmul_tn(out_a, dom, woutbuf, 0, 0, f"dwout_a_{tag}", tmo_cap=MIX_HALF)
    woutbuf = matmul_tn(out_b, dom, woutbuf, 0, MIX_HALF, f"dwout_b_{tag}", tmo_cap=MIX_HALF)
    d_uv, d_sw, d_svec = sgu_bwd(proj, dmixed, lng, lnb, sw, swt, bcol, f"sgu_bwd_{tag}")
    do_res = to_residues(dmixed, 1, f"dout_res_{tag}")
    dqs, dks, dvs = [], [], []
    for p, (d, (qd, kd, vd)) in enumerate(zip(DILATIONS, qkv_res)):
        dq, dk, dv = attn_bwd(qd, kd, vd, do_res[p], o_res[p], lse_res[p], f"attn_bwd_d{d}_{tag}")
        dqs.append(dq)
        dks.append(dk)
        dvs.append(dv)
    d_qkv = rope_bwd(dqs, dks, dvs, cos, sin, f"rope_bwd_{tag}")
    winbuf = lax.empty((1, 5 * MIX_HALF, D), BF16)
    winbuf = matmul_tn(d_uv, y, winbuf, 0, 0, f"dwin_uv_{tag}", tmo_cap=MIX_HALF)
    winbuf = matmul_tn(d_qkv, y, winbuf, 0, 2 * MIX_HALF, f"dwin_qkv_{tag}", tmo_cap=MIX_HALF)
    token, then = on_grads([winbuf, woutbuf])
    pairs = [(d_uv, 0, w_inT, 0, 2 * MIX_HALF), (d_qkv, 0, w_inT, 1, 2 * MIX_HALF), (d_qkv, 2, w_inT, 4, MIX_HALF)]
    outs = dy_normbwd(pairs, h, dh, ng, 1, mod + token, 4, f"mix_bwd_dy_{tag}", next_gate)
    return outs, d_sw, d_svec, then


def _local_step(x, tgt, mods, ngs, get_w, sgus, gf, on_block_grads, on_layer_small):
    T, D = x.shape
    cos, sin = _rope_tables(T)
    h = x
    saved, weights = [], []
    for l in range(2):
        def getter(blk, l=l):
            return lambda after: get_w(l, blk, after)

        if l == 0:
            y = normmod_fwd(h, ngs[0], 0, mods[0], 0, 1, "normmod_l0f1")
        h, y, r1, wf1 = _ffn_fwd(h, y, mods[l], 0, getter("f1u"), getter("f1d"), (ngs[l], 1, mods[l], 3, 4), f"l{l}f1")
        w_inT, w_out = get_w(l, "mx", h)
        h, y, r2 = _mixer_fwd(h, y, mods[l], (w_inT, (0,)), (w_out, (0,)), sgus[l], cos, sin,
                              (ngs[l], 2, mods[l], 6, 7), f"l{l}mx")
        h, y, r3, wf2 = _ffn_fwd(h, y, mods[l], 6, getter("f2u"), getter("f2d"),
                                 (ngs[l + 1], 0, mods[l + 1], 0, 1) if l + 1 < 2 else None, f"l{l}f2")
        saved.append((r1, r2, r3))
        weights.append((wf1, w_inT, w_out, wf2))
    def gate_of(l, blk):
        r1, r2, r3 = saved[l]
        o, i_g, coef = {"f2": (r3[5], 8, 0.5), "mx": (r2[-1], 5, 1.0), "f1": (r1[5], 2, 0.5)}[blk]
        return o, mods[l], i_g, coef

    seq = [(l, blk) for l in (1, 0) for blk in ("f2", "mx", "f1")]
    dh, red_final, do, red_g = final_loss_bwd(h, gf, tgt, gate_of(*seq[0]), "final_loss_bwd")
    rn, rg = {}, {}
    for idx, (l, blk) in enumerate(seq):
        r1, r2, r3 = saved[l]
        wf1, w_inT, w_out, wf2 = weights[l]
        nxt = gate_of(*seq[idx + 1]) if idx + 1 < len(seq) else None
        rg[blk] = red_g
        tag = f"l{l}{blk}"

        def on(arrays, l=l, blk=blk):
            return on_block_grads(l, blk, arrays)

        if blk == "f2":
            outs, then = _ffn_bwd(dh, do, r3, ngs[l], 2, mods[l], 6, *wf2, on, nxt, tag)
        elif blk == "mx":
            outs, d_sw, d_svec, then = _mixer_bwd(dh, do, r2, ngs[l], mods[l], (w_inT, (0,)), (w_out, (0,)), sgus[l],
                                                  cos, sin, on, nxt, tag)
        else:
            outs, then = _ffn_bwd(dh, do, r1, ngs[l], 0, mods[l], 0, *wf1, on, nxt, tag)
        dh, rn[blk] = outs[0], outs[1]
        if nxt is not None:
            do, red_g = outs[2], outs[3]
        if blk == "f1":
            mods = mods + on_layer_small(l, dict(sgu_w=d_sw, sgu_vec=d_svec, red_n=(rn["f1"], rn["mx"], rn["f2"]),
                                                 red_g=(rg["f1"], rg["mx"], rg["f2"])),
                                         red_final if l == 0 else None)
            mods = mods + then(mods)
        else:
            mods = mods + then(dh)
    return dh


def _adam_out(w, g, m, v, name):
    shp = w.shape
    two_d = (-1, shp[-1])
    d, mn, vn = adamw(w.reshape(two_d), g.reshape(two_d), m.reshape(two_d), v.reshape(two_d), name)
    return g, d.reshape(shp), mn.reshape(shp), vn.reshape(shp)


def kernel(x, c, ada_w, ada_b, norm_g, ffn1_wg, ffn1_wu, ffn1_wd, ffn2_wg, ffn2_wu, ffn2_wd, w_in, sgu_ln_g, sgu_ln_b, sgu_w, sgu_b, w_out, final_g, loss_target, m_ada_w, m_ada_b, m_norm_g, m_ffn1_wg, m_ffn1_wu, m_ffn1_wd, m_ffn2_wg, m_ffn2_wu, m_ffn2_wd, m_w_in, m_sgu_ln_g, m_sgu_ln_b, m_sgu_w, m_sgu_b, m_w_out, m_final_g, v_ada_w, v_ada_b, v_norm_g, v_ffn1_wg, v_ffn1_wu, v_ffn1_wd, v_ffn2_wg, v_ffn2_wu, v_ffn2_wd, v_w_in, v_sgu_ln_g, v_sgu_ln_b, v_sgu_w, v_sgu_b, v_w_out, v_final_g):
    T, D = x.shape[1], x.shape[2]
    NL = ada_w.shape[0]
    mx, my, mc = _my_place()
    me = 4 * mx + 2 * my + mc
    ci = 2 * mx + my
    c_idx = jnp.reshape(mc, (1,)).astype(jnp.int32)
    place = jnp.stack([ci, mc]).astype(jnp.int32)

    ngw = norm_g.shape[2]
    small_in = jnp.concatenate([jnp.pad(c, ((0, 7), (0, 0))),
                                jnp.pad(norm_g.reshape(NL * 3, ngw), ((0, 8 - NL * 3), (0, D - ngw)))], axis=0)
    small_all, _ = gather_small(small_in, place, "gather_c_normg")
    c_all = small_all[:, 0, :]
    ng_parts = small_all[0::2, 8:8 + NL * 3, :ngw]
    ngs = jnp.transpose(ng_parts, (1, 0, 2)).reshape(NL, 3, N_CHIP * ngw)

    nmod = ada_w.shape[2]
    ada_b_mine = lax.dynamic_slice_in_dim(ada_b, ci * nmod, nmod, axis=1).reshape(NL, 1, nmod)
    mod_part = ada_fwd(c_all, ada_w, ada_b_mine, "ada_fwd")
    mod_all, _ = gather_small(mod_part.reshape(NL * N_DEV, nmod), place, "gather_mod")
    mod_rows = lax.dynamic_index_in_dim(mod_all.reshape(N_DEV, NL, N_DEV, nmod), me, axis=2, keepdims=False)
    mods = jnp.transpose(mod_rows[0::2], (1, 0, 2)).reshape(NL, N_ADA, D)

    sgus = []
    for l in range(NL):
        sgus.append((sgu_ln_g[l].reshape(1, MIX_HALF), sgu_ln_b[l].reshape(1, MIX_HALF), sgu_w[l],
                     jnp.swapaxes(sgu_w[l], 1, 2), jnp.transpose(sgu_b[l])))

    def halves(a):
        n, r, _ = a.shape
        return a.reshape(n, 2, r // 2, D)

    first_group = halves(jnp.stack([ffn1_wg[0].T, ffn1_wu[0].T], axis=0).astype(BF16))
    first_handle, first_token = first_start(first_group, mods, "first_start")
    zero = first_token[0, 0]
    mods = mods + zero

    def prep(a):
        return (a + zero).astype(BF16)

    groups = []
    for l in range(NL):
        groups += [[halves(jnp.stack([prep(ffn1_wg[l].T), prep(ffn1_wu[l].T)], axis=0))],
                   [halves(prep(ffn1_wd[l])[None])],
                   [halves(prep(w_in[l].T)[None]), halves(prep(w_out[l])[None])],
                   [halves(jnp.stack([prep(ffn2_wg[l].T), prep(ffn2_wu[l].T)], axis=0))],
                   [halves(prep(ffn2_wd[l])[None])]]
    handles, token = gather_start(groups[1:], mods, "gather_start")
    handles = [None] + handles
    mods = mods + token[0, 0]
    group_no = {"f1u": 0, "f1d": 1, "mx": 2, "f2u": 3, "f2d": 4}

    def get_w(l, key, after):
        g = len(group_no) * l + group_no[key]
        if g == 0:
            full = [first_wait(first_forward(first_handle, after, "first_forward"), place, "first_wait")]
        else:
            full = gather_wait(handles[g], after, f"gather_wait_l{l}{key}")
        full = [a.reshape(a.shape[0], N_CHIP * 2 * a.shape[3], D) for a in full]
        return full[0] if key != "mx" else tuple(full)

    def split(a):
        n, r4, _ = a.shape
        return a.reshape(n, N_CHIP, 2, r4 // N_CHIP // 2, D)

    pending, small_pending, small_tokens = {}, {}, {}

    def on_block_grads(l, blk, bufs):
        tag = f"l{l}{blk}"
        sib, tok1 = sibling_start([split(g) for g in bufs], place, f"rs_sibling_start_{tag}")

        def then(after):
            parts, lands = sibling_wait(sib, after, f"rs_sibling_wait_{tag}")
            psums = [sum_halves(g, ld, c_idx, f"rs_sum_halves_{tag}_{i}") for i, (g, ld) in enumerate(zip(parts, lands))]
            pending[(l, blk)], tok2 = scatter_start(psums, lands[0], f"rs_chips_start_{tag}")
            return tok2[0, 0]

        return tok1[0, 0], then

    def blocks_finish(blocks, after, tag):
        ssums, counts = [], []
        for l, blk in blocks:
            psums, lands2 = scatter_wait(pending.pop((l, blk)), after, f"rs_chips_wait_l{l}{blk}")
            ssums += [sum_chips(p, ld, place, f"rs_sum_chips_l{l}{blk}_{i}") for i, (p, ld) in enumerate(zip(psums, lands2))]
            counts.append(len(psums))
        fins = [f.reshape(f.shape[0], -1, D) for f in sibling_complete(ssums, f"rs_complete_{tag}")]
        out, i = [], 0
        for n in counts:
            out.append(fins[i:i + n])
            i += n
        return out

    def on_layer_small(l, grads, red_final):
        blocks = list(grads["red_n"]) + list(grads["red_g"])
        blocks.append(jnp.pad(grads["sgu_vec"], ((0, 0), (0, D - MIX_HALF))))
        blocks.append(grads["sgu_w"].reshape(-1, D))
        if red_final is not None:
            blocks.append(red_final)
        xs = jnp.concatenate(blocks, axis=0)
        small_pending[l], small_tokens[l] = small_start(xs, place, f"small_start_l{l}")
        return small_tokens[l][0, 0]

    grad_x = _local_step(x[0], loss_target[0], mods, ngs, get_w, sgus, final_g.reshape(1, D),
                         on_block_grads, on_layer_small)

    adam_state = {}

    def adam_big(nm, l, g, w, m, v):
        adam_state[nm] = adamw_layer(w, g, m, v, l, adam_state.get(nm), f"adamw_{nm}_l{l}")

    def adam_block(l, blk, fin):
        if blk == "mx":
            adam_big("w_in", l, fin[0][0].T, w_in, m_w_in, v_w_in)
            adam_big("w_out", l, fin[1][0], w_out, m_w_out, v_w_out)
        else:
            ws = ((ffn1_wg, m_ffn1_wg, v_ffn1_wg), (ffn1_wu, m_ffn1_wu, v_ffn1_wu), (ffn1_wd, m_ffn1_wd, v_ffn1_wd)) \
                if blk == "f1" else \
                ((ffn2_wg, m_ffn2_wg, v_ffn2_wg), (ffn2_wu, m_ffn2_wu, v_ffn2_wu), (ffn2_wd, m_ffn2_wd, v_ffn2_wd))
            pre = "ffn1" if blk == "f1" else "ffn2"
            for k, (nm, tr) in enumerate((("wg", True), ("wu", True), ("wd", False))):
                adam_big(f"{pre}_{nm}", l, fin[0][k], *[jnp.swapaxes(t, 1, 2) if tr else t for t in ws[k]])

    done_order = [(l, blk) for l in range(NL - 1, -1, -1) for blk in ("f2", "mx", "f1")]
    for (l, blk), fin in zip(done_order[:-1], blocks_finish(done_order[:-1], small_tokens[0], "early")):
        adam_block(l, blk, fin)
    last_big = adam_state["w_out"][1]

    small_sum, small_all = [], []
    for l in range(NL):
        xs, land = small_wait(small_pending[l], last_big, f"small_wait_l{l}")
        full = lax.dynamic_update_slice(land, xs[None], (me, 0, 0))
        small_all.append(full)
        small_sum.append(sum_slots(full, f"small_sum_l{l}"))
    offs = [8 * i for i in range(8)]
    off_final = offs[7] + SGU_HEADS * ATT_BLOCK * HEAD_LANES // D
    loss = small_sum[0][off_final + 1, 0]
    g_final_g = small_sum[0][off_final, :]
    g_norm_g, g_ada_b, g_lng, g_lnb, g_sb, g_sw, dmod_all = [], [], [], [], [], [], []
    for l in range(NL):
        rn = [small_sum[l][offs[i]:offs[i] + 8] for i in range(3)]
        rg = [small_sum[l][offs[3 + i]:offs[3 + i] + 8] for i in range(3)]
        g_norm_g.append(jnp.stack([rn[i][2] for i in range(3)], axis=0))
        g_ada_b.append(jnp.concatenate([jnp.stack([rn[i][0], rn[i][1], rg[i][0]], axis=0) for i in range(3)],
                                       axis=0).reshape(N_ADA * D))
        sv = small_sum[l][offs[6]:offs[6] + 8, :MIX_HALF]
        g_lng.append(sv[0].reshape(SGU_HEADS, HEAD_LANES))
        g_lnb.append(sv[1].reshape(SGU_HEADS, HEAD_LANES))
        g_sb.append(sv[2].reshape(SGU_HEADS, ATT_BLOCK))
        g_sw.append(small_sum[l][offs[7]:off_final].reshape(sgu_w.shape[1:]))
        rows = []
        for i in range(3):
            an = small_all[l][:, offs[i]:offs[i] + 2]
            ag = small_all[l][:, offs[3 + i]:offs[3 + i] + 1]
            rows += [an[:, 0], an[:, 1], ag[:, 0]]
        dmod_all.append(jnp.stack(rows, axis=1).reshape(N_DEV, N_ADA * D))
    dmod_all = jnp.stack(dmod_all, axis=0)
    dmod_mine = lax.dynamic_slice_in_dim(dmod_all, ci * nmod, nmod, axis=2)
    g_ada_w = ada_bwd(jnp.transpose(c_all), dmod_mine, "ada_bwd")
    g_ada_b = jnp.stack(g_ada_b, axis=0)
    g_norm_g_full = jnp.stack(g_norm_g, axis=0)
    g_norm_g_mine = lax.dynamic_slice_in_dim(g_norm_g_full, ci * ngw, ngw, axis=2)

    small_params = [
        ("ada_w", ada_w, g_ada_w, m_ada_w, v_ada_w),
        ("ada_b", ada_b, g_ada_b, m_ada_b, v_ada_b),
        ("norm_g", norm_g, g_norm_g_mine, m_norm_g, v_norm_g),
        ("sgu_ln_g", sgu_ln_g, jnp.stack(g_lng, axis=0), m_sgu_ln_g, v_sgu_ln_g),
        ("sgu_ln_b", sgu_ln_b, jnp.stack(g_lnb, axis=0), m_sgu_ln_b, v_sgu_ln_b),
        ("sgu_w", sgu_w, jnp.stack(g_sw, axis=0), m_sgu_w, v_sgu_w),
        ("sgu_b", sgu_b, jnp.stack(g_sb, axis=0), m_sgu_b, v_sgu_b),
        ("final_g", final_g.reshape(1, D), g_final_g.reshape(1, D), m_final_g.reshape(1, D), v_final_g.reshape(1, D)),
    ]
    for nm, w, g, m, v in small_params:
        res = _adam_out(w, g, m, v, f"adamw_{nm}")
        adam_state[nm] = tuple(t.reshape(D) for t in res) if nm == "final_g" else res

    l, blk = done_order[-1]
    adam_block(l, blk, blocks_finish([(l, blk)], [st[1] for st in adam_state.values()], "last")[0])

    names = ["ada_w", "ada_b", "norm_g", "ffn1_wg", "ffn1_wu", "ffn1_wd", "ffn2_wg", "ffn2_wu", "ffn2_wd", "w_in",
             "sgu_ln_g", "sgu_ln_b", "sgu_w", "sgu_b", "w_out", "final_g"]
    shapes = [t.shape for t in (ada_w, ada_b, norm_g, ffn1_wg, ffn1_wu, ffn1_wd, ffn2_wg, ffn2_wu, ffn2_wd, w_in,
                                sgu_ln_g, sgu_ln_b, sgu_w, sgu_b, w_out, final_g)]
    def shaped(nm, t, s):
        if nm in ("ffn1_wg", "ffn1_wu", "ffn2_wg", "ffn2_wu"):
            return jnp.swapaxes(t.reshape(s[0], s[2], s[1]), 1, 2)
        return t.reshape(s)

    return (loss, grad_x[None], *[shaped(nm, adam_state[nm][i], s) for i in range(4) for nm, s in zip(names, shapes)])
```

```python
import math

import jax
import jax.numpy as jnp
from jax import lax
from jax.experimental import pallas as pl
from jax.experimental.pallas import tpu as pltpu

F32 = jnp.float32
BF16 = jnp.bfloat16
EPS = 1e-6
SGU_HEADS = 4
HEAD_LANES = 128
ATT_DH = 64
ATT_BLOCK = 128
MIX_HALF = SGU_HEADS * HEAD_LANES
DILATIONS = (1, 4, 16)
ROPE_THETA = 10000.0
N_ADA = 9
ADAM_LR, ADAM_B1, ADAM_B2, ADAM_EPS, ADAM_WD, ADAM_STEP = 0.001, 0.9, 0.999, 1e-08, 0.01, 10
NEG = -1e30
V7X_VMEM_BYTES = 64 * 1024 * 1024
VMEM_LIMIT = V7X_VMEM_BYTES * 7 // 8
MESH = pl.DeviceIdType.MESH
N_DEV = 8
N_CHIP = 4


def _tile(n, cap, mult):
    if n <= cap:
        return n
    t = (cap // mult) * mult
    while t >= mult:
        if n % t == 0:
            return t
        t -= mult
    raise ValueError((n, cap, mult))


def _params(dims=None):
    return pltpu.CompilerParams(dimension_semantics=dims, vmem_limit_bytes=VMEM_LIMIT)


def _wspec(w, rows, idx, resident=False):
    arr, lead = w
    kw = dict(pipeline_mode=pl.Buffered(1)) if resident else {}
    return pl.BlockSpec((None,) * len(lead) + (rows, arr.shape[-1]), lambda *g: tuple(lead) + (idx(*g), 0), **kw)


def _wrows(w):
    return w[0].shape[-2]


def _nt(a, b):
    return lax.dot_general(a, b, (((1,), (1,)), ((), ())), preferred_element_type=F32)


def _tn(a, b):
    return lax.dot_general(a, b, (((0,), (0,)), ((), ())), preferred_element_type=F32)


def _nn(a, b):
    return jnp.dot(a, b, preferred_element_type=F32)


def _sigmoid(x):
    return 0.5 * jnp.tanh(0.5 * x) + 0.5


_GELU_K = math.sqrt(2.0 / math.pi)
_GELU_C = 0.044715


def _gelu(x):
    t = jnp.tanh(_GELU_K * (x + _GELU_C * x * x * x))
    return 0.5 * x * (1.0 + t)


def _gelu_and_grad(x):
    x2 = x * x
    t = jnp.tanh(_GELU_K * (x + _GELU_C * x * x2))
    g = 0.5 * x * (1.0 + t)
    dg = 0.5 * (1.0 + t) + 0.5 * x * (1.0 - t * t) * (_GELU_K * (1.0 + 3.0 * _GELU_C * x2))
    return g, dg


def normmod_fwd(h, ng, i_n, mod, i_sh, i_sc, name):
    T, D = h.shape
    tm = _tile(T, 512, 8)

    def body(h_ref, ng_ref, mod_ref, y_ref):
        y_ref[...] = _normmod(h_ref[...], ng_ref[i_n:i_n + 1, :], mod_ref[i_sh:i_sh + 1, :],
                              mod_ref[i_sc:i_sc + 1, :]).astype(BF16)

    return pl.pallas_call(
        body, name=name, grid=(T // tm,),
        in_specs=[pl.BlockSpec((tm, D), lambda i: (i, 0)),
                  pl.BlockSpec(ng.shape, lambda i: (0, 0)),
                  pl.BlockSpec(mod.shape, lambda i: (0, 0))],
        out_specs=pl.BlockSpec((tm, D), lambda i: (i, 0)),
        out_shape=jax.ShapeDtypeStruct((T, D), BF16),
        compiler_params=_params(("parallel",)),
    )(h, ng, mod)


def ffn_up(y, wgT, wuT, name):
    T, D = y.shape
    F = _wrows(wgT)
    tm = _tile(T, 512, 16)
    tf = _tile(F, 2816, 256)
    cuts = list(range(0, tf, 768)) + [tf]

    def body(y_ref, wg_ref, wu_ref, p_ref, q_ref, s_ref):
        yv = y_ref[...]
        for c0, c1 in zip(cuts[:-1], cuts[1:]):
            a = _nt(yv, wg_ref[c0:c1, :])
            b = _nt(yv, wu_ref[c0:c1, :])
            sig = _sigmoid(a)
            q = a * sig
            p_ref[:, c0:c1] = (b * (sig + q * (1.0 - sig))).astype(BF16)
            q_ref[:, c0:c1] = q.astype(BF16)
            s_ref[:, c0:c1] = (q * b).astype(BF16)

    act = jax.ShapeDtypeStruct((T, F), BF16)
    return pl.pallas_call(
        body, name=name, grid=(F // tf, T // tm),
        in_specs=[pl.BlockSpec((tm, D), lambda j, i: (i, 0)),
                  _wspec(wgT, tf, lambda j, i: j, resident=True),
                  _wspec(wuT, tf, lambda j, i: j, resident=True)],
        out_specs=[pl.BlockSpec((tm, tf), lambda j, i: (i, j))] * 3,
        out_shape=[act, act, act],
        compiler_params=_params(("parallel", "parallel")),
    )(y, wgT[0], wuT[0])


def _normmod(x, gn, sh, sc):
    r = lax.rsqrt(jnp.mean(x * x, axis=-1, keepdims=True) + EPS)
    return ((x * r) * gn) * (1.0 + sc) + sh


def resid_matmul(xs, w, h, mod, i_g, coef, name, norm_next=None):
    T, D = h.shape
    kb = xs[0].shape[1]
    assert all(x.shape == (T, kb) for x in xs) and _wrows(w) == kb * len(xs)
    tm = _tile(T, 1024, 16)
    nx = len(xs)
    n_in, n_out, n_shape, n_ops = [], [], [], []
    if norm_next:
        ng_n, i_n, mod_n, i_sh, i_sc = norm_next
        n_in = [pl.BlockSpec(ng_n.shape, lambda i: (0, 0)), pl.BlockSpec(mod_n.shape, lambda i: (0, 0))]
        n_out = [pl.BlockSpec((tm, D), lambda i: (i, 0))]
        n_shape = [jax.ShapeDtypeStruct((T, D), BF16)]
        n_ops = [ng_n, mod_n]

    def body(*refs):
        x_refs, w_refs = refs[:nx], refs[nx:2 * nx]
        h_ref, mod_ref = refs[2 * nx:2 * nx + 2]
        hn_ref, o_ref = refs[2 * nx + 2 + len(n_in):2 * nx + 4 + len(n_in)]
        o = _nn(x_refs[0][...], w_refs[0][...])
        for xr, wr in zip(x_refs[1:], w_refs[1:]):
            o = o + _nn(xr[...], wr[...])
        o_ref[...] = o.astype(BF16)
        hn = h_ref[...] + (coef * mod_ref[i_g:i_g + 1, :]) * o
        hn_ref[...] = hn
        if norm_next:
            ng_ref, modn_ref = refs[2 * nx + 2], refs[2 * nx + 3]
            refs[-1][...] = _normmod(hn, ng_ref[i_n:i_n + 1, :], modn_ref[i_sh:i_sh + 1, :],
                                     modn_ref[i_sc:i_sc + 1, :]).astype(BF16)

    return pl.pallas_call(
        body, name=name, grid=(T // tm,),
        in_specs=([pl.BlockSpec((tm, kb), lambda i: (i, 0))] * nx
                  + [_wspec(w, kb, lambda i, p=p: p, resident=True) for p in range(nx)]
                  + [pl.BlockSpec((tm, D), lambda i: (i, 0)),
                     pl.BlockSpec(mod.shape, lambda i: (0, 0))] + n_in),
        out_specs=[pl.BlockSpec((tm, D), lambda i: (i, 0))] * 2 + n_out,
        out_shape=[jax.ShapeDtypeStruct((T, D), F32), jax.ShapeDtypeStruct((T, D), BF16)] + n_shape,
        compiler_params=_params(("parallel",)),
    )(*xs, *([w[0]] * nx), h, mod, *n_ops)


def _gate_specs(gate, tm, D):
    o, mod, _, _ = gate
    T = o.shape[0]
    return ([pl.BlockSpec((tm, D), lambda i: (i, 0)), pl.BlockSpec(mod.shape, lambda i: (0, 0))],
            [pl.BlockSpec((tm, D), lambda i: (i, 0)), pl.BlockSpec((8, D), lambda i: (0, 0))],
            [jax.ShapeDtypeStruct((T, D), BF16), jax.ShapeDtypeStruct((8, D), F32)],
            [o, mod])


def _gate_emit(d, gate, o_ref, mod_ref, do_ref, red_ref):
    _, _, i_g, coef = gate
    do_ref[...] = (d * (coef * mod_ref[i_g:i_g + 1, :])).astype(BF16)

    @pl.when(pl.program_id(0) == 0)
    def _():
        red_ref[...] = jnp.zeros_like(red_ref)

    red_ref[0:1, :] += coef * jnp.sum(d * o_ref[...].astype(F32), axis=0, keepdims=True)


def ffn_bwd_mid(do, wd, p, q, name):
    T, D = do.shape
    F = _wrows(wd)
    tm = _tile(T, 512, 16)
    tf = _tile(F, 2816, 256)
    cuts = list(range(0, tf, 256)) + [tf]

    def body(do_ref, wd_ref, p_ref, q_ref, da_ref, db_ref):
        dov = do_ref[...]
        for c0, c1 in zip(cuts[:-1], cuts[1:]):
            ds = _nt(dov, wd_ref[c0:c1, :])
            da_ref[:, c0:c1] = (ds * p_ref[:, c0:c1].astype(F32)).astype(BF16)
            db_ref[:, c0:c1] = (ds * q_ref[:, c0:c1].astype(F32)).astype(BF16)

    act = jax.ShapeDtypeStruct((T, F), BF16)
    return pl.pallas_call(
        body, name=name, grid=(F // tf, T // tm),
        in_specs=[pl.BlockSpec((tm, D), lambda j, i: (i, 0)),
                  _wspec(wd, tf, lambda j, i: j, resident=True),
                  pl.BlockSpec((tm, tf), lambda j, i: (i, j)),
                  pl.BlockSpec((tm, tf), lambda j, i: (i, j))],
        out_specs=[pl.BlockSpec((tm, tf), lambda j, i: (i, j))] * 2,
        out_shape=[act, act],
        compiler_params=_params(("parallel", "parallel")),
    )(do, wd[0], p, q)


def dy_normbwd(pairs, h, dhp, ng, i_n, mod, i_sc, name, gate=None):
    T, D = h.shape
    tm = _tile(T, 512, 16)
    npair = len(pairs)
    g_in, g_out, g_shape, g_ops = _gate_specs(gate, tm, D) if gate else ([], [], [], [])

    def body(*refs):
        x_refs, w_refs = refs[:npair], refs[npair:2 * npair]
        h_ref, dhp_ref, ng_ref, mod_ref = refs[2 * npair:2 * npair + 4]
        dh_ref, red_ref = refs[2 * npair + 4 + len(g_in):2 * npair + 6 + len(g_in)]
        dy = _nn(x_refs[0][...], w_refs[0][...])
        for xr, wr in zip(x_refs[1:], w_refs[1:]):
            dy = dy + _nn(xr[...], wr[...])
        x = h_ref[...]
        r = lax.rsqrt(jnp.mean(x * x, axis=-1, keepdims=True) + EPS)
        n = x * r
        gn = ng_ref[i_n:i_n + 1, :]
        dnh = dy * (1.0 + mod_ref[i_sc:i_sc + 1, :])

        @pl.when(pl.program_id(0) == 0)
        def _():
            red_ref[...] = jnp.zeros_like(red_ref)

        red_ref[0:1, :] += jnp.sum(dy, axis=0, keepdims=True)
        red_ref[1:2, :] += jnp.sum(dy * (n * gn), axis=0, keepdims=True)
        red_ref[2:3, :] += jnp.sum(dnh * n, axis=0, keepdims=True)
        dn = dnh * gn
        dh_new = dhp_ref[...] + r * (dn - n * jnp.mean(dn * n, axis=-1, keepdims=True))
        dh_ref[...] = dh_new
        if gate:
            _gate_emit(dh_new, gate, refs[2 * npair + 4], refs[2 * npair + 5], refs[-2], refs[-1])

    in_specs = ([pl.BlockSpec((tm, kb), lambda i, c=c: (i, c)) for (_, c, _, _, kb) in pairs]
                + [_wspec(w, kb, lambda i, r=r: r, resident=True) for (_, _, w, r, kb) in pairs]
                + [pl.BlockSpec((tm, D), lambda i: (i, 0)),
                   pl.BlockSpec((tm, D), lambda i: (i, 0)),
                   pl.BlockSpec(ng.shape, lambda i: (0, 0)),
                   pl.BlockSpec(mod.shape, lambda i: (0, 0))] + g_in)
    return pl.pallas_call(
        body, name=name, grid=(T // tm,), in_specs=in_specs,
        out_specs=[pl.BlockSpec((tm, D), lambda i: (i, 0)), pl.BlockSpec((8, D), lambda i: (0, 0))] + g_out,
        out_shape=[jax.ShapeDtypeStruct((T, D), F32), jax.ShapeDtypeStruct((8, D), F32)] + g_shape,
        compiler_params=_params(("arbitrary",)),
    )(*[p[0] for p in pairs], *[p[2][0] for p in pairs], h, dhp, ng, mod, *g_ops)


def matmul_tn(a, b, buf, slot, row0, name, tmo_cap=1408):
    T, N = b.shape
    ma = a.shape[1]
    tmo = _tile(ma, tmo_cap, 128)
    assert row0 % tmo == 0
    nmo = ma // tmo
    tk = _tile(T, 2048, 16)
    nk = T // tk

    def body(a_ref, b_ref, buf_ref, o_ref, acc_ref):
        k = pl.program_id(1)

        @pl.when(k == 0)
        def _():
            acc_ref[...] = jnp.zeros_like(acc_ref)

        acc_ref[...] += _tn(a_ref[...], b_ref[...])

        @pl.when(k == nk - 1)
        def _():
            o_ref[...] = acc_ref[...].astype(BF16)

    return pl.pallas_call(
        body, name=name, grid=(nmo, nk),
        in_specs=[pl.BlockSpec((tk, tmo), lambda j, k: (k, j)),
                  pl.BlockSpec((tk, N), lambda j, k: (k, 0)),
                  pl.BlockSpec(memory_space=pl.ANY)],
        out_specs=pl.BlockSpec((None, tmo, N), lambda j, k: (slot, row0 // tmo + j, 0)),
        out_shape=jax.ShapeDtypeStruct(buf.shape, BF16),
        scratch_shapes=[pltpu.VMEM((tmo, N), F32)],
        input_output_aliases={2: 0},
        compiler_params=_params(("parallel", "arbitrary")),
    )(a, b, buf)


def matmul_nt(x, w, name):
    T, K = x.shape
    N = _wrows(w)
    tm = _tile(T, 1024, 16)
    tn = _tile(N, 1280, 128)

    def body(x_ref, w_ref, o_ref):
        o_ref[...] = _nt(x_ref[...], w_ref[...]).astype(BF16)

    return pl.pallas_call(
        body, name=name, grid=(N // tn, T // tm),
        in_specs=[pl.BlockSpec((tm, K), lambda j, i: (i, 0)), _wspec(w, tn, lambda j, i: j)],
        out_specs=pl.BlockSpec((tm, tn), lambda j, i: (i, j)),
        out_shape=jax.ShapeDtypeStruct((T, N), BF16),
        compiler_params=_params(("parallel", "parallel")),
    )(x, w[0])


def _sgu_head_fwd(u, v, lng, lnb):
    gu, dgu = _gelu_and_grad(u)
    gv, dgv = _gelu_and_grad(v)
    mu = jnp.mean(gv, axis=-1, keepdims=True)
    xc = gv - mu
    rstd = lax.rsqrt(jnp.mean(xc * xc, axis=-1, keepdims=True) + EPS)
    xhat = xc * rstd
    vn = xhat * lng + lnb
    return gu, dgu, dgv, rstd, xhat, vn


def _tril_mask():
    r = lax.broadcasted_iota(jnp.int32, (ATT_BLOCK, ATT_BLOCK), 0)
    c = lax.broadcasted_iota(jnp.int32, (ATT_BLOCK, ATT_BLOCK), 1)
    return c <= r


def _triu_mask():
    r = lax.broadcasted_iota(jnp.int32, (ATT_BLOCK, ATT_BLOCK), 0)
    c = lax.broadcasted_iota(jnp.int32, (ATT_BLOCK, ATT_BLOCK), 1)
    return r <= c


def sgu_fwd(proj, lng, lnb, w, bcol, name):
    T = proj.shape[0]
    tm = _tile(T, 512, 128)
    nch = tm // ATT_BLOCK

    def body(u_ref, v_ref, lng_ref, lnb_ref, w_ref, b_ref, o_ref):
        tril = _tril_mask()
        for hd in range(SGU_HEADS):
            sl = slice(hd * HEAD_LANES, (hd + 1) * HEAD_LANES)
            u = u_ref[:, sl].astype(F32)
            v = v_ref[:, sl].astype(F32)
            gu, _, _, _, _, vn = _sgu_head_fwd(u, v, lng_ref[:, sl], lnb_ref[:, sl])
            wm = jnp.where(tril, w_ref[hd], 0.0).astype(BF16)
            vnb = vn.astype(BF16)
            bc = b_ref[:, hd:hd + 1]
            for ch in range(nch):
                rs = slice(ch * ATT_BLOCK, (ch + 1) * ATT_BLOCK)
                z = _nn(wm, vnb[rs, :]) + bc
                o_ref[rs, sl] = (gu[rs, :] * z).astype(BF16)

    return pl.pallas_call(
        body, name=name, grid=(T // tm,),
        in_specs=[pl.BlockSpec((tm, MIX_HALF), lambda i: (i, 0)),
                  pl.BlockSpec((tm, MIX_HALF), lambda i: (i, 1)),
                  pl.BlockSpec((1, MIX_HALF), lambda i: (0, 0)),
                  pl.BlockSpec((1, MIX_HALF), lambda i: (0, 0)),
                  pl.BlockSpec(w.shape, lambda i: (0, 0, 0)),
                  pl.BlockSpec(bcol.shape, lambda i: (0, 0))],
        out_specs=pl.BlockSpec((tm, MIX_HALF), lambda i: (i, 0)),
        out_shape=jax.ShapeDtypeStruct((T, MIX_HALF), BF16),
        compiler_params=_params(("parallel",)),
    )(proj, proj, lng, lnb, w, bcol)


def sgu_bwd(proj, dmixed, lng, lnb, w, wt, bcol, name):
    T = proj.shape[0]
    tm = _tile(T, 512, 128)
    nch = tm // ATT_BLOCK
    nsteps = T // tm

    def body(u_ref, v_ref, g_ref, lng_ref, lnb_ref, w_ref, wt_ref, b_ref, duv_ref, dw_ref, dvec_ref, bacc_ref):
        step = pl.program_id(0)

        @pl.when(step == 0)
        def _():
            dw_ref[...] = jnp.zeros_like(dw_ref)
            dvec_ref[...] = jnp.zeros_like(dvec_ref)
            bacc_ref[...] = jnp.zeros_like(bacc_ref)

        tril = _tril_mask()
        triu = _triu_mask()
        for hd in range(SGU_HEADS):
            sl = slice(hd * HEAD_LANES, (hd + 1) * HEAD_LANES)
            u = u_ref[:, sl].astype(F32)
            v = v_ref[:, sl].astype(F32)
            lng_h = lng_ref[:, sl]
            gu, dgu, dgv, rstd, xhat, vn = _sgu_head_fwd(u, v, lng_h, lnb_ref[:, sl])
            wm = jnp.where(tril, w_ref[hd], 0.0).astype(BF16)
            wmt = jnp.where(triu, wt_ref[hd], 0.0).astype(BF16)
            vnb = vn.astype(BF16)
            bc = b_ref[:, hd:hd + 1]
            g = g_ref[:, sl].astype(F32)
            dw_acc = jnp.zeros((ATT_BLOCK, ATT_BLOCK), F32)
            b_acc = jnp.zeros((ATT_BLOCK, HEAD_LANES), F32)
            dvn_parts = []
            for ch in range(nch):
                rs = slice(ch * ATT_BLOCK, (ch + 1) * ATT_BLOCK)
                z = _nn(wm, vnb[rs, :]) + bc
                duv_ref[rs, sl] = (g[rs, :] * z * dgu[rs, :]).astype(BF16)
                dz = g[rs, :] * gu[rs, :]
                dzb = dz.astype(BF16)
                dvn_parts.append(_nn(wmt, dzb))
                dw_acc = dw_acc + _nt(dzb, vnb[rs, :])
                b_acc = b_acc + dz
            dvn = jnp.concatenate(dvn_parts, axis=0)
            dw_ref[hd] += jnp.where(tril, dw_acc, 0.0)
            bacc_ref[hd] += b_acc
            dvec_ref[0:1, sl] += jnp.sum(dvn * xhat, axis=0, keepdims=True)
            dvec_ref[1:2, sl] += jnp.sum(dvn, axis=0, keepdims=True)
            dxh = dvn * lng_h
            dgv_in = rstd * (dxh - jnp.mean(dxh, axis=-1, keepdims=True)
                             - xhat * jnp.mean(dxh * xhat, axis=-1, keepdims=True))
            duv_ref[:, MIX_HALF + hd * HEAD_LANES:MIX_HALF + (hd + 1) * HEAD_LANES] = (dgv_in * dgv).astype(BF16)

        @pl.when(step == nsteps - 1)
        def _():
            for hd in range(SGU_HEADS):
                sl = slice(hd * HEAD_LANES, (hd + 1) * HEAD_LANES)
                dvec_ref[2:3, sl] = jnp.sum(bacc_ref[hd].T, axis=0, keepdims=True)

    return pl.pallas_call(
        body, name=name, grid=(nsteps,),
        in_specs=[pl.BlockSpec((tm, MIX_HALF), lambda i: (i, 0)),
                  pl.BlockSpec((tm, MIX_HALF), lambda i: (i, 1)),
                  pl.BlockSpec((tm, MIX_HALF), lambda i: (i, 0)),
                  pl.BlockSpec((1, MIX_HALF), lambda i: (0, 0)),
                  pl.BlockSpec((1, MIX_HALF), lambda i: (0, 0)),
                  pl.BlockSpec(w.shape, lambda i: (0, 0, 0)),
                  pl.BlockSpec(w.shape, lambda i: (0, 0, 0)),
                  pl.BlockSpec(bcol.shape, lambda i: (0, 0))],
        out_specs=[pl.BlockSpec((tm, 2 * MIX_HALF), lambda i: (i, 0)),
                   pl.BlockSpec(w.shape, lambda i: (0, 0, 0)),
                   pl.BlockSpec((8, MIX_HALF), lambda i: (0, 0))],
        out_shape=[jax.ShapeDtypeStruct((T, 2 * MIX_HALF), BF16),
                   jax.ShapeDtypeStruct(w.shape, F32),
                   jax.ShapeDtypeStruct((8, MIX_HALF), F32)],
        scratch_shapes=[pltpu.VMEM((SGU_HEADS, ATT_BLOCK, HEAD_LANES), F32)],
        compiler_params=_params(("arbitrary",)),
    )(proj, proj, dmixed, lng, lnb, w, wt, bcol)


def _rot_half(t):
    lane = lax.broadcasted_iota(jnp.int32, t.shape, 1)
    first = (lane % ATT_DH) < (ATT_DH // 2)
    return jnp.where(first, -pltpu.roll(t, HEAD_LANES - ATT_DH // 2, 1), pltpu.roll(t, ATT_DH // 2, 1))


LAYOUT_ROWS = 512


def _res_spec(d, tm, W):
    return pl.BlockSpec((d, tm // d, W), lambda i: (0, i, 0))


def _res_shape(d, T, W, dtype):
    return jax.ShapeDtypeStruct((d, T // d, W), dtype)


def _slab_buf(tm, W):
    return pltpu.VMEM((W // HEAD_LANES, tm, HEAD_LANES), F32)


def _lanes(hp):
    return slice(hp * HEAD_LANES, (hp + 1) * HEAD_LANES)


def _to_res(buf, out_ref, d, dtype):
    nslab, tm, _ = buf.shape
    for hp in range(nslab):
        if d == 1:
            out_ref[0, :, _lanes(hp)] = buf[hp].astype(dtype)
        else:
            for r in range(d):
                out_ref[r, :, _lanes(hp)] = buf.at[hp][pl.ds(r, tm // d, stride=d), :].astype(dtype)


def _from_res(in_ref, buf, d):
    nslab, tm, _ = buf.shape
    for hp in range(nslab):
        if d == 1:
            buf[hp] = in_ref[0, :, _lanes(hp)]
        else:
            for r in range(d):
                buf.at[hp][pl.ds(r, tm // d, stride=d), :] = in_ref[r, :, _lanes(hp)]


def rope_fwd(proj, cos, sin, name):
    T = proj.shape[0]
    tm = LAYOUT_ROWS
    scale = 1.0 / math.sqrt(ATT_DH)
    nd = len(DILATIONS)

    def body(q_ref, k_ref, v_ref, cos_ref, sin_ref, *rest):
        outs, buf = rest[:3 * nd], rest[3 * nd]
        c = cos_ref[...]
        s = sin_ref[...]
        for which, src in enumerate((q_ref, k_ref, v_ref)):
            for hp in range(MIX_HALF // HEAD_LANES):
                t = src[:, _lanes(hp)].astype(F32)
                if which == 0:
                    t = scale * (t * c + _rot_half(t) * s)
                elif which == 1:
                    t = t * c + _rot_half(t) * s
                buf[hp] = t
            for di, d in enumerate(DILATIONS):
                _to_res(buf, outs[3 * di + which], d, BF16)

    return pl.pallas_call(
        body, name=name, grid=(T // tm,),
        in_specs=[pl.BlockSpec((tm, MIX_HALF), lambda i: (i, 2)),
                  pl.BlockSpec((tm, MIX_HALF), lambda i: (i, 3)),
                  pl.BlockSpec((tm, MIX_HALF), lambda i: (i, 4)),
                  pl.BlockSpec((tm, HEAD_LANES), lambda i: (i, 0)),
                  pl.BlockSpec((tm, HEAD_LANES), lambda i: (i, 0))],
        out_specs=[_res_spec(d, tm, MIX_HALF) for d in DILATIONS for _ in range(3)],
        out_shape=[_res_shape(d, T, MIX_HALF, BF16) for d in DILATIONS for _ in range(3)],
        scratch_shapes=[_slab_buf(tm, MIX_HALF)],
        compiler_params=_params(("parallel",)),
    )(proj, proj, proj, cos, sin)


def to_residues(x, col, name):
    T = x.shape[0]
    tm = LAYOUT_ROWS

    def body(x_ref, *rest):
        outs, buf = rest[:-1], rest[-1]
        for hp in range(MIX_HALF // HEAD_LANES):
            buf[hp] = x_ref[:, _lanes(hp)].astype(F32)
        for o_ref, d in zip(outs, DILATIONS):
            _to_res(buf, o_ref, d, BF16)

    return pl.pallas_call(
        body, name=name, grid=(T // tm,),
        in_specs=[pl.BlockSpec((tm, MIX_HALF), lambda i: (i, col))],
        out_specs=[_res_spec(d, tm, MIX_HALF) for d in DILATIONS],
        out_shape=[_res_shape(d, T, MIX_HALF, BF16) for d in DILATIONS],
        scratch_shapes=[_slab_buf(tm, MIX_HALF)],
        compiler_params=_params(("parallel",)),
    )(x)


def rope_bwd(dqs, dks, dvs, cos, sin, name):
    T = dqs[0].shape[0] * dqs[0].shape[1]
    tm = LAYOUT_ROWS
    scale = 1.0 / math.sqrt(ATT_DH)
    npat = len(dqs)

    def body(*refs):
        groups = refs[:npat], refs[npat:2 * npat], refs[2 * npat:3 * npat]
        cos_ref, sin_ref, o_ref, buf, acc = refs[3 * npat:]
        c = cos_ref[...]
        s = sin_ref[...]
        for which, g_refs in enumerate(groups):
            _from_res(g_refs[0], acc, DILATIONS[0])
            for g_ref, d in zip(g_refs[1:], DILATIONS[1:]):
                _from_res(g_ref, buf, d)
                acc[...] += buf[...]
            for hp in range(MIX_HALF // HEAD_LANES):
                g = acc[hp]
                if which == 0:
                    g = scale * g
                if which < 2:
                    g = g * c - _rot_half(g * s)
                o_ref[:, which * MIX_HALF + hp * HEAD_LANES:which * MIX_HALF + (hp + 1) * HEAD_LANES] = g.astype(BF16)

    return pl.pallas_call(
        body, name=name, grid=(T // tm,),
        in_specs=([_res_spec(d, tm, MIX_HALF) for _ in range(3) for d in DILATIONS]
                  + [pl.BlockSpec((tm, HEAD_LANES), lambda i: (i, 0))] * 2),
        out_specs=pl.BlockSpec((tm, 3 * MIX_HALF), lambda i: (i, 0)),
        out_shape=jax.ShapeDtypeStruct((T, 3 * MIX_HALF), BF16),
        scratch_shapes=[_slab_buf(tm, MIX_HALF), _slab_buf(tm, MIX_HALF)],
        compiler_params=_params(("parallel",)),
    )(*dqs, *dks, *dvs, cos, sin)


def _band_masks(n):
    r = lax.broadcasted_iota(jnp.int32, (2 * ATT_BLOCK, ATT_BLOCK), 0)
    c = lax.broadcasted_iota(jnp.int32, (2 * ATT_BLOCK, ATT_BLOCK), 1)
    qi = r % ATT_BLOCK
    head = (c < ATT_DH) == (r < ATT_BLOCK)
    return (c >= qi) & (n > 0), c <= qi, head, c[:ATT_BLOCK] < ATT_DH


def _stack_heads(x, head):
    x2 = jnp.concatenate([x, x], axis=0)
    return jnp.where(head, x2, jnp.zeros_like(x2))


def attn_fwd(q, k, v, name):
    d, L, W = q.shape
    nb = L // ATT_BLOCK

    def body(q_ref, kp_ref, kc_ref, vp_ref, vc_ref, o_ref, lse_ref):
        mask_p, mask_c, head, head0 = _band_masks(pl.program_id(1))
        for hp in range(W // HEAD_LANES):
            sl = slice(hp * HEAD_LANES, (hp + 1) * HEAD_LANES)
            kp, kc, vp, vc = kp_ref[0, :, sl], kc_ref[0, :, sl], vp_ref[0, :, sl], vc_ref[0, :, sl]
            qs = _stack_heads(q_ref[0, :, sl], head)
            sp = jnp.where(mask_p, _nt(qs, kp), NEG)
            sc = jnp.where(mask_c, _nt(qs, kc), NEG)
            m = jnp.maximum(jnp.max(sp, axis=1, keepdims=True), jnp.max(sc, axis=1, keepdims=True))
            pp = jnp.exp(sp - m)
            pc = jnp.exp(sc - m)
            den = jnp.sum(pp, axis=1, keepdims=True) + jnp.sum(pc, axis=1, keepdims=True)
            o = (_nn(pp.astype(BF16), vp) + _nn(pc.astype(BF16), vc)) / den
            lse = m + jnp.log(den)
            o_ref[0, :, sl] = jnp.where(head0, o[:ATT_BLOCK], o[ATT_BLOCK:])
            lse_ref[0, :, sl] = jnp.where(head0, lse[:ATT_BLOCK], lse[ATT_BLOCK:])

    cur = pl.BlockSpec((1, ATT_BLOCK, W), lambda r, n: (r, n, 0))
    prev = pl.BlockSpec((1, ATT_BLOCK, W), lambda r, n: (r, jnp.maximum(n - 1, 0), 0))
    out = jax.ShapeDtypeStruct((d, L, W), F32)
    return pl.pallas_call(
        body, name=name, grid=(d, nb),
        in_specs=[cur, prev, cur, prev, cur],
        out_specs=[cur, cur], out_shape=[out, out],
        compiler_params=_params(("parallel", "parallel")),
    )(q, k, k, v, v)


def attn_combine(os_, lses, name):
    T = os_[0].shape[0] * os_[0].shape[1]
    W = os_[0].shape[2]
    tm = LAYOUT_ROWS
    npat = len(os_)

    def body(*refs):
        o_refs, l_refs = refs[:npat], refs[npat:2 * npat]
        out_ref = refs[2 * npat]
        ores, lres = refs[2 * npat + 1:3 * npat + 1], refs[3 * npat + 1:4 * npat + 1]
        bufs = refs[4 * npat + 1:]
        lbufs, obufs, out_buf, lse_buf = bufs[:npat], bufs[npat:2 * npat], bufs[2 * npat], bufs[2 * npat + 1]
        for p, d in enumerate(DILATIONS):
            _from_res(l_refs[p], lbufs[p], d)
            _from_res(o_refs[p], obufs[p], d)
        for hp in range(W // HEAD_LANES):
            ls = [b[hp] for b in lbufs]
            m = ls[0]
            for l in ls[1:]:
                m = jnp.maximum(m, l)
            es = [jnp.exp(l - m) for l in ls]
            z = es[0]
            for e in es[1:]:
                z = z + e
            acc = es[0] * obufs[0][hp]
            for p in range(1, npat):
                acc = acc + es[p] * obufs[p][hp]
            out = acc / z
            out_ref[:, _lanes(hp)] = out.astype(BF16)
            out_buf[hp] = out
            lse_buf[hp] = m + jnp.log(z)
        for p, d in enumerate(DILATIONS):
            _to_res(out_buf, ores[p], d, BF16)
            _to_res(lse_buf, lres[p], d, F32)

    return pl.pallas_call(
        body, name=name, grid=(T // tm,),
        in_specs=[_res_spec(d, tm, W) for _ in range(2) for d in DILATIONS],
        out_specs=([pl.BlockSpec((tm, W), lambda i: (i, 0))] + [_res_spec(d, tm, W) for _ in range(2) for d in DILATIONS]),
        out_shape=([jax.ShapeDtypeStruct((T, W), BF16)] + [_res_shape(d, T, W, BF16) for d in DILATIONS]
                   + [_res_shape(d, T, W, F32) for d in DILATIONS]),
        scratch_shapes=[_slab_buf(tm, W)] * (2 * npat + 2),
        compiler_params=_params(("parallel",)),
    )(*os_, *lses)


def attn_bwd(q, k, v, do, o, lse, name):
    d, L, W = q.shape
    nb = L // ATT_BLOCK

    def body(q_ref, kp_ref, kc_ref, vp_ref, vc_ref, do_ref, o_ref, lse_ref, dq_ref, dk_ref, dv_ref, kkeep, vkeep):
        n = pl.program_id(1)

        @pl.when(n == 0)
        def _():
            kkeep[...] = jnp.zeros_like(kkeep)
            vkeep[...] = jnp.zeros_like(vkeep)

        @pl.when(n < nb)
        def _():
            mask_p, mask_c, head, head0 = _band_masks(n)
            for hp in range(W // HEAD_LANES):
                sl = slice(hp * HEAD_LANES, (hp + 1) * HEAD_LANES)
                kp, kc, vp, vc = kp_ref[0, :, sl], kc_ref[0, :, sl], vp_ref[0, :, sl], vc_ref[0, :, sl]
                dout = do_ref[0, :, sl]
                qs = _stack_heads(q_ref[0, :, sl], head)
                dos = _stack_heads(dout, head)
                lse_v = lse_ref[0, :, sl]
                lse_c = jnp.max(jnp.where(head, jnp.concatenate([lse_v, lse_v], axis=0), NEG), axis=1, keepdims=True)
                delta = jnp.sum(_stack_heads(dout.astype(F32) * o_ref[0, :, sl].astype(F32), head), axis=1, keepdims=True)
                pp = jnp.exp(jnp.where(mask_p, _nt(qs, kp), NEG) - lse_c)
                pc = jnp.exp(jnp.where(mask_c, _nt(qs, kc), NEG) - lse_c)
                dsp = (pp * (_nt(dos, vp) - delta)).astype(BF16)
                dsc = (pc * (_nt(dos, vc) - delta)).astype(BF16)
                dq2 = _nn(dsp, kp) + _nn(dsc, kc)
                dq_ref[0, :, sl] = jnp.where(head0, dq2[:ATT_BLOCK], dq2[ATT_BLOCK:])
                dk_ref[0, :, sl] = kkeep[:, sl] + _tn(dsp, qs)
                dv_ref[0, :, sl] = vkeep[:, sl] + _tn(pp.astype(BF16), dos)
                kkeep[:, sl] = _tn(dsc, qs)
                vkeep[:, sl] = _tn(pc.astype(BF16), dos)

        @pl.when(n == nb)
        def _():
            dk_ref[0] = kkeep[...]
            dv_ref[0] = vkeep[...]

    cur = pl.BlockSpec((1, ATT_BLOCK, W), lambda r, n: (r, jnp.minimum(n, nb - 1), 0))
    prev = pl.BlockSpec((1, ATT_BLOCK, W), lambda r, n: (r, jnp.clip(n - 1, 0, nb - 1), 0))
    out = jax.ShapeDtypeStruct((d, L, W), F32)
    return pl.pallas_call(
        body, name=name, grid=(d, nb + 1),
        in_specs=[cur, prev, cur, prev, cur, cur, cur, cur],
        out_specs=[cur, prev, prev], out_shape=[out, out, out],
        scratch_shapes=[pltpu.VMEM((ATT_BLOCK, W), F32), pltpu.VMEM((ATT_BLOCK, W), F32)],
        compiler_params=_params(("parallel", "arbitrary")),
    )(q, k, k, v, v, do, o, lse)


def final_loss_bwd(h, gf, tgt, gate, name):
    T, D = h.shape
    tm = _tile(T, 512, 16)
    g_in, g_out, g_shape, g_ops = _gate_specs(gate, tm, D)

    def body(h_ref, g_ref, t_ref, o_ref, modg_ref, dh_ref, red_ref, do_ref, redg_ref):
        x = h_ref[...]
        r = lax.rsqrt(jnp.mean(x * x, axis=-1, keepdims=True) + EPS)
        n = x * r
        g = g_ref[...]
        err = n * g - t_ref[...]
        dy = err * (1.0 / D)

        @pl.when(pl.program_id(0) == 0)
        def _():
            red_ref[...] = jnp.zeros_like(red_ref)

        red_ref[0:1, :] += jnp.sum(dy * n, axis=0, keepdims=True)
        red_ref[1:2, :] += jnp.zeros((1, D), F32) + (0.5 / D) * jnp.sum(err * err, keepdims=True)
        dn = dy * g
        dh = r * (dn - n * jnp.mean(dn * n, axis=-1, keepdims=True))
        dh_ref[...] = dh
        _gate_emit(dh, gate, o_ref, modg_ref, do_ref, redg_ref)

    return pl.pallas_call(
        body, name=name, grid=(T // tm,),
        in_specs=[pl.BlockSpec((tm, D), lambda i: (i, 0)),
                  pl.BlockSpec((1, D), lambda i: (0, 0)),
                  pl.BlockSpec((tm, D), lambda i: (i, 0))] + g_in,
        out_specs=[pl.BlockSpec((tm, D), lambda i: (i, 0)), pl.BlockSpec((8, D), lambda i: (0, 0))] + g_out,
        out_shape=[jax.ShapeDtypeStruct((T, D), F32), jax.ShapeDtypeStruct((8, D), F32)] + g_shape,
        compiler_params=_params(("arbitrary",)),
    )(h, gf, tgt, *g_ops)


def ada_fwd(c_all, ada_w, ada_b, name):
    nl, D, N = ada_w.shape

    def body(c_ref, w_ref, b_ref, o_ref):
        c = c_ref[...]
        o_ref[0] = _nn(c * _sigmoid(c), w_ref[0]) + b_ref[0]

    return pl.pallas_call(
        body, name=name, grid=(nl,),
        in_specs=[pl.BlockSpec((N_DEV, D), lambda l: (0, 0)),
                  pl.BlockSpec((1, D, N), lambda l: (l, 0, 0)),
                  pl.BlockSpec((1, 1, N), lambda l: (l, 0, 0))],
        out_specs=pl.BlockSpec((1, N_DEV, N), lambda l: (l, 0, 0)),
        out_shape=jax.ShapeDtypeStruct((nl, N_DEV, N), F32),
        compiler_params=_params(("parallel",)),
    )(c_all, ada_w, ada_b)


def ada_bwd(c_allT, dmod, name):
    nl, _, N = dmod.shape
    D = c_allT.shape[0]

    def body(c_ref, g_ref, o_ref):
        c = c_ref[...]
        ca = c * _sigmoid(c)
        acc = ca[:, 0:1] * g_ref[0, 0:1, :]
        for b in range(1, N_DEV):
            acc = acc + ca[:, b:b + 1] * g_ref[0, b:b + 1, :]
        o_ref[0] = acc

    return pl.pallas_call(
        body, name=name, grid=(nl,),
        in_specs=[pl.BlockSpec((D, N_DEV), lambda l: (0, 0)),
                  pl.BlockSpec((1, N_DEV, N), lambda l: (l, 0, 0))],
        out_specs=pl.BlockSpec((1, D, N), lambda l: (l, 0, 0)),
        out_shape=jax.ShapeDtypeStruct((nl, D, N), F32),
        compiler_params=_params(("parallel",)),
    )(c_allT, dmod)


def adamw(w, g, m, v, name):
    R, C = w.shape
    tr = _tile(R, max(8, (1 << 19) // C // 8 * 8), 8)
    c1 = 1.0 - ADAM_B1 ** ADAM_STEP
    c2 = 1.0 - ADAM_B2 ** ADAM_STEP

    def body(w_ref, g_ref, m_ref, v_ref, d_ref, mo_ref, vo_ref):
        gv = g_ref[...]
        mn = ADAM_B1 * m_ref[...] + (1.0 - ADAM_B1) * gv
        vn = ADAM_B2 * v_ref[...] + (1.0 - ADAM_B2) * (gv * gv)
        mo_ref[...] = mn
        vo_ref[...] = vn
        d_ref[...] = -ADAM_LR * ((mn / c1) / (jnp.sqrt(vn / c2) + ADAM_EPS) + ADAM_WD * w_ref[...])

    blk = pl.BlockSpec((tr, C), lambda i: (i, 0))
    out = jax.ShapeDtypeStruct((R, C), F32)
    return pl.pallas_call(
        body, name=name, grid=(R // tr,),
        in_specs=[blk] * 4, out_specs=[blk] * 3, out_shape=[out] * 3,
        compiler_params=_params(("parallel",)),
    )(w, g, m, v)


def adamw_layer(w, g, m, v, l, prev, name):
    NLw, R, C = w.shape
    tr = _tile(R, max(8, (1 << 19) // C // 8 * 8), 8)
    nrb = R // tr
    c1 = 1.0 - ADAM_B1 ** ADAM_STEP
    c2 = 1.0 - ADAM_B2 ** ADAM_STEP
    w, m, v = (t.reshape(NLw * R, C) for t in (w, m, v))

    def body(w_ref, g_ref, m_ref, v_ref, *rest):
        go_ref, d_ref, mo_ref, vo_ref = rest[-4:]
        gv = g_ref[...]
        mn = ADAM_B1 * m_ref[...] + (1.0 - ADAM_B1) * gv
        vn = ADAM_B2 * v_ref[...] + (1.0 - ADAM_B2) * (gv * gv)
        go_ref[...] = gv
        mo_ref[...] = mn
        vo_ref[...] = vn
        d_ref[...] = -ADAM_LR * ((mn / c1) / (jnp.sqrt(vn / c2) + ADAM_EPS) + ADAM_WD * w_ref[...])

    lay = pl.BlockSpec((tr, C), lambda i: (l * nrb + i, 0))
    out = jax.ShapeDtypeStruct((NLw * R, C), F32)
    n_prev = 0 if prev is None else 4
    return pl.pallas_call(
        body, name=name, grid=(nrb,),
        in_specs=[lay, pl.BlockSpec((tr, C), lambda i: (i, 0)), lay, lay] + [pl.BlockSpec(memory_space=pl.ANY)] * n_prev,
        out_specs=[lay] * 4, out_shape=[out] * 4,
        input_output_aliases={4 + i: i for i in range(n_prev)},
        compiler_params=_params(("parallel",)),
    )(w, g, m, v, *(prev or ()))


def sum_slots(x, name):
    S, R, C = x.shape
    tr = _tile(R, 128, 8)

    def body(x_ref, o_ref):
        acc = x_ref[0]
        for s in range(1, S):
            acc = acc + x_ref[s]
        o_ref[...] = acc

    return pl.pallas_call(
        body, name=name, grid=(R // tr,),
        in_specs=[pl.BlockSpec((S, tr, C), lambda i: (0, i, 0))],
        out_specs=pl.BlockSpec((tr, C), lambda i: (i, 0)),
        out_shape=jax.ShapeDtypeStruct((R, C), F32),
        compiler_params=_params(("parallel",)),
    )(x)


def sum_halves(g, lands, c_idx, name):
    n, ns, _, rh, D = g.shape

    def body(c_ref, g_ref, l_ref, o_ref):
        o_ref[0, 0] = (g_ref[0, 0, 0].astype(F32) + l_ref[0, 0].astype(F32)).astype(BF16)

    return pl.pallas_call(
        body, name=name,
        grid_spec=pltpu.PrefetchScalarGridSpec(
            num_scalar_prefetch=1, grid=(n, ns),
            in_specs=[pl.BlockSpec((1, 1, 1, rh, D), lambda i, j, c: (i, j, c[0], 0, 0)),
                      pl.BlockSpec((1, 1, rh, D), lambda i, j, c: (i, j, 0, 0))],
            out_specs=pl.BlockSpec((1, 1, rh, D), lambda i, j, c: (i, j, 0, 0))),
        out_shape=jax.ShapeDtypeStruct((n, ns, rh, D), BF16),
        compiler_params=_params(("parallel", "parallel")),
    )(c_idx, g, lands)


def sum_chips(p, lands, place, name):
    n, ns, rh, D = p.shape

    def body(c_ref, p_ref, l_ref, o_ref):
        acc = p_ref[0, 0].astype(F32)
        for j in range(N_CHIP - 1):
            acc = acc + l_ref[j, 0].astype(F32)
        o_ref[0, 0] = acc

    return pl.pallas_call(
        body, name=name,
        grid_spec=pltpu.PrefetchScalarGridSpec(
            num_scalar_prefetch=1, grid=(n,),
            in_specs=[pl.BlockSpec((1, 1, rh, D), lambda i, c: (i, c[0], 0, 0)),
                      pl.BlockSpec((N_CHIP - 1, 1, rh, D), lambda i, c: (0, i, 0, 0))],
            out_specs=pl.BlockSpec((1, 1, rh, D), lambda i, c: (i, c[1], 0, 0))),
        out_shape=jax.ShapeDtypeStruct((n, 2, rh, D), F32),
        compiler_params=_params(("parallel",)),
    )(place, p, lands)


def _my_place():
    return lax.axis_index("x"), lax.axis_index("y"), lax.axis_index("c")


def _other_chips(mx, my):
    return [(1 - mx, my), (mx, 1 - my), (1 - mx, 1 - my)]


def gather_small(x, after, name):
    def body(x_ref, after_ref, out_ref, sum_ref, send_sems, recv_sems):
        mx, my, mc = _my_place()
        me = 4 * mx + 2 * my + mc
        out_ref[me] = x_ref[...]
        sends = []
        for k in range(1, N_DEV):
            kx, ky, kc = (k >> 2) & 1, (k >> 1) & 1, k & 1
            peer = (1 - mx if kx else mx, 1 - my if ky else my, 1 - mc if kc else mc)
            cp = pltpu.make_async_remote_copy(
                src_ref=x_ref, dst_ref=out_ref.at[me], send_sem=send_sems.at[k - 1], recv_sem=recv_sems.at[k - 1],
                device_id=peer, device_id_type=MESH)
            cp.start()
            sends.append((cp, 4 * peer[0] + 2 * peer[1] + peer[2], peer))
        for k, (cp, peer_slot, peer) in enumerate(sends):
            pltpu.make_async_remote_copy(
                src_ref=x_ref, dst_ref=out_ref.at[peer_slot], send_sem=send_sems.at[k], recv_sem=recv_sems.at[k],
                device_id=peer, device_id_type=MESH).wait_recv()
        for cp, _, _ in sends:
            cp.wait_send()
        acc = out_ref[0]
        for s in range(1, N_DEV):
            acc = acc + out_ref[s]
        sum_ref[...] = acc

    vmem = pl.BlockSpec(memory_space=pltpu.VMEM)
    return pl.pallas_call(
        body, name=name,
        in_specs=[vmem, pl.BlockSpec(memory_space=pl.ANY)], out_specs=[vmem, vmem],
        out_shape=[jax.ShapeDtypeStruct((N_DEV,) + x.shape, x.dtype), jax.ShapeDtypeStruct(x.shape, x.dtype)],
        scratch_shapes=[pltpu.SemaphoreType.DMA((N_DEV - 1,)), pltpu.SemaphoreType.DMA((N_DEV - 1,))],
        compiler_params=pltpu.CompilerParams(vmem_limit_bytes=VMEM_LIMIT),
    )(x, after)


_HBM =pl.BlockSpec(memory_space=pltpu.HBM)
_SEM = pl.BlockSpec(memory_space=pltpu.SEMAPHORE)
_DATAFLOW = pltpu.SideEffectType.DATAFLOW_SIDE_EFFECTING


def _gather_copies(shard, land, send, recv, base):
    mx, my, mc = _my_place()
    ci = 2 * mx + my
    peers = [((cx, cy, mc), 2 * cx + cy) for cx, cy in _other_chips(mx, my)] + [((mx, my, 1 - mc), ci)]
    out = []
    for q, (dev, src_slot) in enumerate(peers):
        out.append((
            pltpu.make_async_remote_copy(src_ref=shard, dst_ref=land.at[:, ci], send_sem=send.at[base + q],
                                         recv_sem=recv.at[base + q], device_id=dev, device_id_type=MESH),
            pltpu.make_async_remote_copy(src_ref=shard, dst_ref=land.at[:, src_slot], send_sem=send.at[base + q],
                                         recv_sem=recv.at[base + q], device_id=dev, device_id_type=MESH)))
    return out


def gather_start(groups, after, name):
    items = [s for g in groups for s in g]
    ni, ng = len(items), len(groups)

    def body(*refs):
        shards, lands = refs[:ni], refs[ni:2 * ni]
        sems = refs[2 * ni + 1:2 * ni + 1 + 2 * ng]
        token = refs[-1]
        i = 0
        for g, grp in enumerate(groups):
            for p in range(len(grp)):
                for start_cp, _ in _gather_copies(shards[i], lands[i], sems[2 * g], sems[2 * g + 1], 4 * p):
                    start_cp.start()
                i += 1
        token[...] = jnp.zeros_like(token)

    sem_shapes = []
    for grp in groups:
        sem_shapes += [pltpu.SemaphoreType.DMA((4 * len(grp),))] * 2
    land_shapes = [(s.shape[0], N_CHIP) + s.shape[1:] for s in items]
    outs = pl.pallas_call(
        body, name=name,
        in_specs=[_HBM] * (2 * ni) + [pl.BlockSpec(memory_space=pl.ANY)],
        out_specs=[_SEM] * (2 * ng) + [_HBM] * (2 * ni) + [pl.BlockSpec(memory_space=pltpu.VMEM)],
        out_shape=(sem_shapes + [pltpu.HBM(s.shape, s.dtype) for s in items]
                   + [pltpu.HBM(ls, s.dtype) for ls, s in zip(land_shapes, items)]
                   + [jax.ShapeDtypeStruct((8, 128), F32)]),
        input_output_aliases={i: 2 * ng + i for i in range(2 * ni)},
        compiler_params=pltpu.CompilerParams(has_side_effects=_DATAFLOW),
    )(*[pltpu.with_memory_space_constraint(s, pltpu.HBM) for s in items],
      *[pltpu.with_memory_space_constraint(lax.empty(ls, s.dtype), pltpu.HBM) for ls, s in zip(land_shapes, items)],
      after)
    sems, thru, token = outs[:2 * ng], outs[2 * ng:2 * ng + 2 * ni], outs[-1]
    handles, i = [], 0
    for g, grp in enumerate(groups):
        n = len(grp)
        handles.append((sems[2 * g], sems[2 * g + 1], thru[i:i + n], thru[ni + i:ni + i + n]))
        i += n
    return handles, token


def gather_wait(handle, after, name):
    send, recv, shards, lands = handle
    n = len(shards)

    def body(*refs):
        shard_refs, land_refs = refs[:n], refs[n:2 * n]
        send_ref, recv_ref = refs[2 * n], refs[2 * n + 1]
        for p in range(n):
            for start_cp, recv_cp in _gather_copies(shard_refs[p], land_refs[p], send_ref, recv_ref, 4 * p):
                start_cp.wait_send()
                recv_cp.wait_recv()

    outs = pl.pallas_call(
        body, name=name,
        in_specs=[_HBM] * (2 * n) + [_SEM, _SEM, pl.BlockSpec(memory_space=pl.ANY)],
        out_specs=[_HBM] * (2 * n),
        out_shape=[pltpu.HBM(s.shape, s.dtype) for s in shards] + [pltpu.HBM(l.shape, l.dtype) for l in lands],
        input_output_aliases={i: i for i in range(2 * n)},
        compiler_params=pltpu.CompilerParams(has_side_effects=_DATAFLOW),
    )(*shards, *lands, send, recv, after)
    return outs[n:]


def _first_copies(shard, land, send, recv):
    mx, my, mc = _my_place()
    ci = 2 * mx + my
    out = []
    for q, (cx, cy) in enumerate(_other_chips(mx, my)):
        dev = (cx, cy, mc)
        out.append(tuple(pltpu.make_async_remote_copy(
            src_ref=shard.at[:, mc], dst_ref=land.at[:, slot, mc], send_sem=send.at[q], recv_sem=recv.at[q],
            device_id=dev, device_id_type=MESH) for slot in (ci, 2 * cx + cy)))
    sib = pltpu.make_async_remote_copy(src_ref=shard, dst_ref=land.at[:, ci], send_sem=send.at[3], recv_sem=recv.at[3],
                                       device_id=(mx, my, 1 - mc), device_id_type=MESH)
    return out + [(sib, sib)]


def _forward_copies(land, send, recv):
    mx, my, mc = _my_place()
    out = []
    for q, (cx, cy) in enumerate(_other_chips(mx, my)):
        out.append(tuple(pltpu.make_async_remote_copy(
            src_ref=land.at[:, 2 * cx + cy, hc], dst_ref=land.at[:, 2 * cx + cy, hc], send_sem=send.at[q],
            recv_sem=recv.at[q], device_id=(mx, my, 1 - mc), device_id_type=MESH) for hc in (mc, 1 - mc)))
    return out


def first_start(shard, after, name):
    def body(shard_ref, land_ref, after_ref, send, recv, shard_thru, land_thru, token):
        for mine, _ in _first_copies(shard_ref, land_ref, send, recv):
            mine.start()
        token[...] = jnp.zeros_like(token)

    land_shape = (shard.shape[0], N_CHIP) + shard.shape[1:]
    outs = pl.pallas_call(
        body, name=name,
        in_specs=[_HBM, _HBM, pl.BlockSpec(memory_space=pl.ANY)],
        out_specs=[_SEM, _SEM, _HBM, _HBM, pl.BlockSpec(memory_space=pltpu.VMEM)],
        out_shape=[pltpu.SemaphoreType.DMA((4,))] * 2 + [pltpu.HBM(shard.shape, shard.dtype),
                                                         pltpu.HBM(land_shape, shard.dtype),
                                                         jax.ShapeDtypeStruct((8, 128), F32)],
        input_output_aliases={0: 2, 1: 3},
        compiler_params=pltpu.CompilerParams(has_side_effects=_DATAFLOW),
    )(pltpu.with_memory_space_constraint(shard, pltpu.HBM),
      pltpu.with_memory_space_constraint(lax.empty(land_shape, shard.dtype), pltpu.HBM), after)
    return outs[:4], outs[4]


def first_forward(handle, after, name):
    send, recv, shard, land = handle

    def body(shard_ref, land_ref, send_ref, recv_ref, after_ref, send2, recv2, shard_thru, land_thru):
        firsts = _first_copies(shard_ref, land_ref, send_ref, recv_ref)
        forwards = _forward_copies(land_ref, send2, recv2)
        for q in range(3):
            firsts[q][1].wait_recv()
            forwards[q][0].start()
        firsts[3][1].wait_recv()
        for mine, _ in firsts:
            mine.wait_send()

    outs = pl.pallas_call(
        body, name=name,
        in_specs=[_HBM, _HBM, _SEM, _SEM, pl.BlockSpec(memory_space=pl.ANY)],
        out_specs=[_SEM, _SEM, _HBM, _HBM],
        out_shape=[pltpu.SemaphoreType.DMA((3,))] * 2 + [pltpu.HBM(shard.shape, shard.dtype),
                                                         pltpu.HBM(land.shape, land.dtype)],
        input_output_aliases={0: 2, 1: 3},
        compiler_params=pltpu.CompilerParams(has_side_effects=_DATAFLOW),
    )(shard, land, send, recv, after)
    return outs[0], outs[1], outs[3]


def first_wait(handle, after, name):
    send, recv, land = handle

    def body(land_ref, send_ref, recv_ref, after_ref, land_out):
        for mine, theirs in _forward_copies(land_ref, send_ref, recv_ref):
            mine.wait_send()
            theirs.wait_recv()

    return pl.pallas_call(
        body, name=name,
        in_specs=[_HBM, _SEM, _SEM, pl.BlockSpec(memory_space=pl.ANY)],
        out_specs=[_HBM],
        out_shape=[pltpu.HBM(land.shape, land.dtype)],
        input_output_aliases={0: 0},
        compiler_params=pltpu.CompilerParams(has_side_effects=_DATAFLOW),
    )(land, send, recv, after)[0]


def _sibling_copies(gs, lands, send, recv):
    mx, my, mc = _my_place()
    return [pltpu.make_async_remote_copy(
        src_ref=gs[k].at[:, :, 1 - mc], dst_ref=lands[k], send_sem=send.at[k], recv_sem=recv.at[k],
        device_id=(mx, my, 1 - mc), device_id_type=MESH) for k in range(len(gs))]


def sibling_start(gs, after, name):
    K = len(gs)

    def body(*refs):
        ins, lands = refs[:K], refs[K:2 * K]
        send, recv = refs[2 * K + 1], refs[2 * K + 2]
        for cp in _sibling_copies(ins, lands, send, recv):
            cp.start()
        refs[-1][...] = jnp.zeros_like(refs[-1])

    land_shapes = [g.shape[:2] + g.shape[3:] for g in gs]
    outs = pl.pallas_call(
        body, name=name,
        in_specs=[_HBM] * (2 * K) + [pl.BlockSpec(memory_space=pl.ANY)],
        out_specs=[_SEM, _SEM] + [_HBM] * (2 * K) + [pl.BlockSpec(memory_space=pltpu.VMEM)],
        out_shape=([pltpu.SemaphoreType.DMA((K,))] * 2 + [pltpu.HBM(g.shape, g.dtype) for g in gs]
                   + [pltpu.HBM(ls, g.dtype) for ls, g in zip(land_shapes, gs)] + [jax.ShapeDtypeStruct((8, 128), F32)]),
        input_output_aliases={i: 2 + i for i in range(2 * K)},
        compiler_params=pltpu.CompilerParams(has_side_effects=_DATAFLOW),
    )(*[pltpu.with_memory_space_constraint(g, pltpu.HBM) for g in gs],
      *[pltpu.with_memory_space_constraint(lax.empty(ls, g.dtype), pltpu.HBM) for ls, g in zip(land_shapes, gs)],
      after)
    return (outs[0], outs[1], outs[2:2 + K], outs[2 + K:2 + 2 * K]), outs[-1]


def sibling_wait(handle, after, name):
    send, recv, gs, lands = handle
    K = len(gs)

    def body(*refs):
        ins, land_refs = refs[:K], refs[K:2 * K]
        for cp in _sibling_copies(ins, land_refs, refs[2 * K], refs[2 * K + 1]):
            cp.wait_send()
            cp.wait_recv()

    outs = pl.pallas_call(
        body, name=name,
        in_specs=[_HBM] * (2 * K) + [_SEM, _SEM, pl.BlockSpec(memory_space=pl.ANY)],
        out_specs=[_HBM] * (2 * K),
        out_shape=[pltpu.HBM(g.shape, g.dtype) for g in gs] + [pltpu.HBM(l.shape, l.dtype) for l in lands],
        input_output_aliases={i: i for i in range(2 * K)},
        compiler_params=pltpu.CompilerParams(has_side_effects=_DATAFLOW),
    )(*gs, *lands, send, recv, after)
    return outs[:K], outs[K:]


def _small_copies(x, land, send, recv):
    mx, my, mc = _my_place()
    me = 4 * mx + 2 * my + mc
    out = []
    for k in range(1, N_DEV):
        peer = (1 - mx if k & 4 else mx, 1 - my if k & 2 else my, 1 - mc if k & 1 else mc)
        slot = 4 * peer[0] + 2 * peer[1] + peer[2]
        out.append(tuple(pltpu.make_async_remote_copy(
            src_ref=x, dst_ref=land.at[s], send_sem=send.at[k - 1], recv_sem=recv.at[k - 1],
            device_id=peer, device_id_type=MESH) for s in (me, slot)))
    return out


def small_start(x, after, name):
    def body(x_ref, land_ref, after_ref, send, recv, x_thru, land_thru, token):
        for mine, _ in _small_copies(x_ref, land_ref, send, recv):
            mine.start()
        token[...] = jnp.zeros_like(token)

    land_shape = (N_DEV,) + x.shape
    outs = pl.pallas_call(
        body, name=name,
        in_specs=[_HBM, _HBM, pl.BlockSpec(memory_space=pl.ANY)],
        out_specs=[_SEM, _SEM, _HBM, _HBM, pl.BlockSpec(memory_space=pltpu.VMEM)],
        out_shape=[pltpu.SemaphoreType.DMA((N_DEV - 1,))] * 2 + [pltpu.HBM(x.shape, x.dtype), pltpu.HBM(land_shape, x.dtype),
                                                                 jax.ShapeDtypeStruct((8, 128), F32)],
        input_output_aliases={0: 2, 1: 3},
        compiler_params=pltpu.CompilerParams(has_side_effects=_DATAFLOW),
    )(pltpu.with_memory_space_constraint(x, pltpu.HBM),
      pltpu.with_memory_space_constraint(lax.empty(land_shape, x.dtype), pltpu.HBM), after)
    return outs[:4], outs[4]


def small_wait(handle, after, name):
    send, recv, x, land = handle

    def body(x_ref, land_ref, send_ref, recv_ref, after_ref, x_out, land_out):
        for mine, theirs in _small_copies(x_ref, land_ref, send_ref, recv_ref):
            mine.wait_send()
            theirs.wait_recv()

    return pl.pallas_call(
        body, name=name,
        in_specs=[_HBM, _HBM, _SEM, _SEM, pl.BlockSpec(memory_space=pl.ANY)],
        out_specs=[_HBM, _HBM],
        out_shape=[pltpu.HBM(x.shape, x.dtype), pltpu.HBM(land.shape, land.dtype)],
        input_output_aliases={0: 0, 1: 1},
        compiler_params=pltpu.CompilerParams(has_side_effects=_DATAFLOW),
    )(x, land, send, recv, after)


def _scatter_copies(ps, lands, send, recv):
    mx, my, mc = _my_place()
    cps = []
    for j, (cx, cy) in enumerate(_other_chips(mx, my)):
        for k in range(len(ps)):
            cps.append(pltpu.make_async_remote_copy(
                src_ref=ps[k].at[:, 2 * cx + cy], dst_ref=lands[k].at[j],
                send_sem=send.at[k * 3 + j], recv_sem=recv.at[k * 3 + j],
                device_id=(cx, cy, mc), device_id_type=MESH))
    return cps


def scatter_start(ps, after, name):
    K = len(ps)

    def body(*refs):
        ins, lands = refs[:K], refs[K:2 * K]
        send, recv = refs[2 * K + 1], refs[2 * K + 2]
        for cp in _scatter_copies(ins, lands, send, recv):
            cp.start()
        refs[-1][...] = jnp.zeros_like(refs[-1])

    land_shapes = [(N_CHIP - 1, p.shape[0]) + p.shape[2:] for p in ps]
    outs = pl.pallas_call(
        body, name=name,
        in_specs=[_HBM] * (2 * K) + [pl.BlockSpec(memory_space=pl.ANY)],
        out_specs=[_SEM, _SEM] + [_HBM] * (2 * K) + [pl.BlockSpec(memory_space=pltpu.VMEM)],
        out_shape=([pltpu.SemaphoreType.DMA((3 * K,))] * 2 + [pltpu.HBM(p.shape, p.dtype) for p in ps]
                   + [pltpu.HBM(ls, p.dtype) for ls, p in zip(land_shapes, ps)] + [jax.ShapeDtypeStruct((8, 128), F32)]),
        input_output_aliases={i: 2 + i for i in range(2 * K)},
        compiler_params=pltpu.CompilerParams(has_side_effects=_DATAFLOW),
    )(*[pltpu.with_memory_space_constraint(p, pltpu.HBM) for p in ps],
      *[pltpu.with_memory_space_constraint(lax.empty(ls, p.dtype), pltpu.HBM) for ls, p in zip(land_shapes, ps)],
      after)
    return (outs[0], outs[1], outs[2:2 + K], outs[2 + K:2 + 2 * K]), outs[-1]


def scatter_wait(handle, after, name):
    send, recv, ps, lands = handle
    K = len(ps)
    afters = list(after) if isinstance(after, (list, tuple)) else [after]

    def body(*refs):
        ins, land_refs = refs[:K], refs[K:2 * K]
        send_ref, recv_ref = refs[2 * K], refs[2 * K + 1]
        for cp in _scatter_copies(ins, land_refs, send_ref, recv_ref):
            cp.wait_send()
            cp.wait_recv()

    outs = pl.pallas_call(
        body, name=name,
        in_specs=[_HBM] * (2 * K) + [_SEM, _SEM] + [pl.BlockSpec(memory_space=pl.ANY)] * len(afters),
        out_specs=[_HBM] * (2 * K),
        out_shape=[pltpu.HBM(p.shape, p.dtype) for p in ps] + [pltpu.HBM(l.shape, l.dtype) for l in lands],
        input_output_aliases={i: i for i in range(2 * K)},
        compiler_params=pltpu.CompilerParams(has_side_effects=_DATAFLOW),
    )(*ps, *lands, send, recv, *afters)
    return outs[:K], outs[K:]


def sibling_complete(ss, name):
    K = len(ss)

    def body(*refs):
        ins, outs = refs[:K], refs[K:2 * K]
        send, recv = refs[2 * K:]
        mx, my, mc = _my_place()
        cps = []
        for k in range(K):
            cp = pltpu.make_async_remote_copy(
                src_ref=ins[k].at[:, mc], dst_ref=outs[k].at[:, mc], send_sem=send.at[k], recv_sem=recv.at[k],
                device_id=(mx, my, 1 - mc), device_id_type=MESH)
            cp.start()
            cps.append(cp)
        for k in range(K):
            pltpu.make_async_remote_copy(
                src_ref=ins[k].at[:, mc], dst_ref=outs[k].at[:, 1 - mc], send_sem=send.at[k], recv_sem=recv.at[k],
                device_id=(mx, my, 1 - mc), device_id_type=MESH).wait_recv()
        for cp in cps:
            cp.wait_send()

    hbm = pl.BlockSpec(memory_space=pl.ANY)
    return pl.pallas_call(
        body, name=name,
        in_specs=[hbm] * K, out_specs=[hbm] * K,
        out_shape=[jax.ShapeDtypeStruct(s.shape, s.dtype) for s in ss],
        scratch_shapes=[pltpu.SemaphoreType.DMA((K,)), pltpu.SemaphoreType.DMA((K,))],
        input_output_aliases={k: k for k in range(K)},
    )(*ss)


def _rope_tables(T):
    inv = ROPE_THETA ** (-jnp.arange(0, ATT_DH, 2, dtype=F32) / ATT_DH)
    ang = jnp.arange(T, dtype=F32)[:, None] * inv[None, :]
    ang = jnp.concatenate([ang, ang, ang, ang], axis=-1)
    return jnp.cos(ang), jnp.sin(ang)


def _ffn_fwd(h, y, mod, i0, get_up, get_down, norm_next, tag):
    wgu = get_up(y)
    a, b, s = ffn_up(y, (wgu, (0,)), (wgu, (1,)), f"ffn_up_{tag}")
    wd = get_down(s)
    outs = resid_matmul([s], (wd, (0,)), h, mod, i0 + 2, 0.5, f"ffn_down_{tag}", norm_next)
    hn, o = outs[0], outs[1]
    return hn, (outs[2] if norm_next else None), (h, y, a, b, s, o), ((wgu, (0,)), (wgu, (1,)), (wd, (0,)))


def _ffn_bwd(dh, do, res, ng, i_n, mod, i0, wgT, wuT, wd, on_grads, next_gate, tag):
    h, y, a, b, s, o = res
    F = _wrows(wgT)
    da, db = ffn_bwd_mid(do, wd, a, b, f"ffn_bwd_mid_{tag}")
    gbuf = lax.empty((3, F, h.shape[1]), BF16)
    gbuf = matmul_tn(da, y, gbuf, 0, 0, f"dwg_{tag}")
    gbuf = matmul_tn(db, y, gbuf, 1, 0, f"dwu_{tag}")
    gbuf = matmul_tn(s, do, gbuf, 2, 0, f"dwd_{tag}")
    token, then = on_grads([gbuf])
    outs = dy_normbwd([(da, 0, wgT, 0, F), (db, 0, wuT, 0, F)], h, dh, ng, i_n, mod + token, i0 + 1,
                      f"ffn_bwd_dy_{tag}", next_gate)
    return outs, then


def _mixer_fwd(h, y, mod, w_inT, w_out, sgu, cos, sin, norm_next, tag):
    lng, lnb, sw, swt, bcol = sgu
    proj = matmul_nt(y, w_inT, f"proj_{tag}")
    out_a = sgu_fwd(proj, lng, lnb, sw, bcol, f"sgu_fwd_{tag}")
    qkv = rope_fwd(proj, cos, sin, f"rope_fwd_{tag}")
    npat = len(DILATIONS)
    qkv_res = [tuple(qkv[3 * p:3 * p + 3]) for p in range(npat)]
    os_, lses = [], []
    for d, (qd, kd, vd) in zip(DILATIONS, qkv_res):
        o_d, lse_d = attn_fwd(qd, kd, vd, f"attn_fwd_d{d}_{tag}")
        os_.append(o_d)
        lses.append(lse_d)
    comb = attn_combine(os_, lses, f"attn_combine_{tag}")
    out_b, o_res, lse_res = comb[0], comb[1:1 + npat], comb[1 + npat:]
    outs = resid_matmul([out_a, out_b], w_out, h, mod, 5, 1.0, f"mix_out_{tag}", norm_next)
    hn, om = outs[0], outs[1]
    return hn, (outs[2] if norm_next else None), (h, y, proj, out_a, out_b, o_res, lse_res, qkv_res, om)


def _mixer_bwd(dh, dom, res, ng, mod, w_inT, w_out, sgu, cos, sin, on_grads, next_gate, tag):
    lng, lnb, sw, swt, bcol = sgu
    h, y, proj, out_a, out_b, o_res, lse_res, qkv_res, om = res
    D = h.shape[1]
    dmixed = matmul_nt(dom, w_out, f"dmixed_{tag}")
    woutbuf = lax.empty((1, 2 * MIX_HALF, D), BF16)
    woutbuf = matmul_tn(out_a, dom, woutbuf, 0, 0, f"dwout_a_{tag}", tmo_cap=MIX_HALF)
    woutbuf = matmul_tn(out_b, dom, woutbuf, 0, MIX_HALF, f"dwout_b_{tag}", tmo_cap=MIX_HALF)
    d_uv, d_sw, d_svec = sgu_bwd(proj, dmixed, lng, lnb, sw, swt, bcol, f"sgu_bwd_{tag}")
    do_res = to_residues(dmixed, 1, f"dout_res_{tag}")
    dqs, dks, dvs = [], [], []
    for p, (d, (qd, kd, vd)) in enumerate(zip(DILATIONS, qkv_res)):
        dq, dk, dv = attn_bwd(qd, kd, vd, do_res[p], o_res[p], lse_res[p], f"attn_bwd_d{d}_{tag}")
        dqs.append(dq)
        dks.append(dk)
        dvs.append(dv)
    d_qkv = rope_bwd(dqs, dks, dvs, cos, sin, f"rope_bwd_{tag}")
    winbuf = lax.empty((1, 5 * MIX_HALF, D), BF16)
    winbuf = matmul_tn(d_uv, y, winbuf, 0, 0, f"dwin_uv_{tag}", tmo_cap=MIX_HALF)
    winbuf = matmul_tn(d_qkv, y, winbuf, 0, 2 * MIX_HALF, f"dwin_qkv_{tag}", tmo_cap=MIX_HALF)
    token, then = on_grads([winbuf, woutbuf])
    pairs = [(d_uv, 0, w_inT, 0, 2 * MIX_HALF), (d_qkv, 0, w_inT, 1, 2 * MIX_HALF), (d_qkv, 2, w_inT, 4, MIX_HALF)]
    outs = dy_normbwd(pairs, h, dh, ng, 1, mod + token, 4, f"mix_bwd_dy_{tag}", next_gate)
    return outs, d_sw, d_svec, then


def _local_step(x, tgt, mods, ngs, get_w, sgus, gf, on_block_grads, on_layer_small):
    T, D = x.shape
    cos, sin = _rope_tables(T)
    h = x
    saved, weights = [], []
    for l in range(2):
        def getter(blk, l=l):
            return lambda after: get_w(l, blk, after)

        if l == 0:
            y = normmod_fwd(h, ngs[0], 0, mods[0], 0, 1, "normmod_l0f1")
        h, y, r1, wf1 = _ffn_fwd(h, y, mods[l], 0, getter("f1u"), getter("f1d"), (ngs[l], 1, mods[l], 3, 4), f"l{l}f1")
        w_inT, w_out = get_w(l, "mx", h)
        h, y, r2 = _mixer_fwd(h, y, mods[l], (w_inT, (0,)), (w_out, (0,)), sgus[l], cos, sin,
                              (ngs[l], 2, mods[l], 6, 7), f"l{l}mx")
        h, y, r3, wf2 = _ffn_fwd(h, y, mods[l], 6, getter("f2u"), getter("f2d"),
                                 (ngs[l + 1], 0, mods[l + 1], 0, 1) if l + 1 < 2 else None, f"l{l}f2")
        saved.append((r1, r2, r3))
        weights.append((wf1, w_inT, w_out, wf2))
    def gate_of(l, blk):
        r1, r2, r3 = saved[l]
        o, i_g, coef = {"f2": (r3[5], 8, 0.5), "mx": (r2[-1], 5, 1.0), "f1": (r1[5], 2, 0.5)}[blk]
        return o, mods[l], i_g, coef

    seq = [(l, blk) for l in (1, 0) for blk in ("f2", "mx", "f1")]
    dh, red_final, do, red_g = final_loss_bwd(h, gf, tgt, gate_of(*seq[0]), "final_loss_bwd")
    rn, rg = {}, {}
    for idx, (l, blk) in enumerate(seq):
        r1, r2, r3 = saved[l]
        wf1, w_inT, w_out, wf2 = weights[l]
        nxt = gate_of(*seq[idx + 1]) if idx + 1 < len(seq) else None
        rg[blk] = red_g
        tag = f"l{l}{blk}"

        def on(arrays, l=l, blk=blk):
            return on_block_grads(l, blk, arrays)

        if blk == "f2":
            outs, then = _ffn_bwd(dh, do, r3, ngs[l], 2, mods[l], 6, *wf2, on, nxt, tag)
        elif blk == "mx":
            outs, d_sw, d_svec, then = _mixer_bwd(dh, do, r2, ngs[l], mods[l], (w_inT, (0,)), (w_out, (0,)), sgus[l],
                                                  cos, sin, on, nxt, tag)
        else:
            outs, then = _ffn_bwd(dh, do, r1, ngs[l], 0, mods[l], 0, *wf1, on, nxt, tag)
        dh, rn[blk] = outs[0], outs[1]
        if nxt is not None:
            do, red_g = outs[2], outs[3]
        if blk == "f1":
            mods = mods + on_layer_small(l, dict(sgu_w=d_sw, sgu_vec=d_svec, red_n=(rn["f1"], rn["mx"], rn["f2"]),
                                                 red_g=(rg["f1"], rg["mx"], rg["f2"])),
                                         red_final if l == 0 else None)
            mods = mods + then(mods)
        else:
            mods = mods + then(dh)
    return dh


def _adam_out(w, g, m, v, name):
    shp = w.shape
    two_d = (-1, shp[-1])
    d, mn, vn = adamw(w.reshape(two_d), g.reshape(two_d), m.reshape(two_d), v.reshape(two_d), name)
    return g, d.reshape(shp), mn.reshape(shp), vn.reshape(shp)


def kernel(x, c, ada_w, ada_b, norm_g, ffn1_wg, ffn1_wu, ffn1_wd, ffn2_wg, ffn2_wu, ffn2_wd, w_in, sgu_ln_g, sgu_ln_b, sgu_w, sgu_b, w_out, final_g, loss_target, m_ada_w, m_ada_b, m_norm_g, m_ffn1_wg, m_ffn1_wu, m_ffn1_wd, m_ffn2_wg, m_ffn2_wu, m_ffn2_wd, m_w_in, m_sgu_ln_g, m_sgu_ln_b, m_sgu_w, m_sgu_b, m_w_out, m_final_g, v_ada_w, v_ada_b, v_norm_g, v_ffn1_wg, v_ffn1_wu, v_ffn1_wd, v_ffn2_wg, v_ffn2_wu, v_ffn2_wd, v_w_in, v_sgu_ln_g, v_sgu_ln_b, v_sgu_w, v_sgu_b, v_w_out, v_final_g):
    T, D = x.shape[1], x.shape[2]
    NL = ada_w.shape[0]
    mx, my, mc = _my_place()
    me = 4 * mx + 2 * my + mc
    ci = 2 * mx + my
    c_idx = jnp.reshape(mc, (1,)).astype(jnp.int32)
    place = jnp.stack([ci, mc]).astype(jnp.int32)

    ngw = norm_g.shape[2]
    small_in = jnp.concatenate([jnp.pad(c, ((0, 7), (0, 0))),
                                jnp.pad(norm_g.reshape(NL * 3, ngw), ((0, 8 - NL * 3), (0, D - ngw)))], axis=0)
    small_all, _ = gather_small(small_in, place, "gather_c_normg")
    c_all = small_all[:, 0, :]
    ng_parts = small_all[0::2, 8:8 + NL * 3, :ngw]
    ngs = jnp.transpose(ng_parts, (1, 0, 2)).reshape(NL, 3, N_CHIP * ngw)

    nmod = ada_w.shape[2]
    ada_b_mine = lax.dynamic_slice_in_dim(ada_b, ci * nmod, nmod, axis=1).reshape(NL, 1, nmod)
    mod_part = ada_fwd(c_all, ada_w, ada_b_mine, "ada_fwd")
    mod_all, _ = gather_small(mod_part.reshape(NL * N_DEV, nmod), place, "gather_mod")
    mod_rows = lax.dynamic_index_in_dim(mod_all.reshape(N_DEV, NL, N_DEV, nmod), me, axis=2, keepdims=False)
    mods = jnp.transpose(mod_rows[0::2], (1, 0, 2)).reshape(NL, N_ADA, D)

    sgus = []
    for l in range(NL):
        sgus.append((sgu_ln_g[l].reshape(1, MIX_HALF), sgu_ln_b[l].reshape(1, MIX_HALF), sgu_w[l],
                     jnp.swapaxes(sgu_w[l], 1, 2), jnp.transpose(sgu_b[l])))

    def halves(a):
        n, r, _ = a.shape
        return a.reshape(n, 2, r // 2, D)

    first_group = halves(jnp.stack([ffn1_wg[0].T, ffn1_wu[0].T], axis=0).astype(BF16))
    first_handle, first_token = first_start(first_group, mods, "first_start")
    zero = first_token[0, 0]
    mods = mods + zero

    def prep(a):
        return (a + zero).astype(BF16)

    groups = []
    for l in range(NL):
        groups += [[halves(jnp.stack([prep(ffn1_wg[l].T), prep(ffn1_wu[l].T)], axis=0))],
                   [halves(prep(ffn1_wd[l])[None])],
                   [halves(prep(w_in[l].T)[None]), halves(prep(w_out[l])[None])],
                   [halves(jnp.stack([prep(ffn2_wg[l].T), prep(ffn2_wu[l].T)], axis=0))],
                   [halves(prep(ffn2_wd[l])[None])]]
    handles, token = gather_start(groups[1:], mods, "gather_start")
    handles = [None] + handles
    mods = mods + token[0, 0]
    group_no = {"f1u": 0, "f1d": 1, "mx": 2, "f2u": 3, "f2d": 4}

    def get_w(l, key, after):
        g = len(group_no) * l + group_no[key]
        if g == 0:
            full = [first_wait(first_forward(first_handle, after, "first_forward"), place, "first_wait")]
        else:
            full = gather_wait(handles[g], after, f"gather_wait_l{l}{key}")
        full = [a.reshape(a.shape[0], N_CHIP * 2 * a.shape[3], D) for a in full]
        return full[0] if key != "mx" else tuple(full)

    def split(a):
        n, r4, _ = a.shape
        return a.reshape(n, N_CHIP, 2, r4 // N_CHIP // 2, D)

    pending, small_pending, small_tokens = {}, {}, {}

    def on_block_grads(l, blk, bufs):
        tag = f"l{l}{blk}"
        sib, tok1 = sibling_start([split(g) for g in bufs], place, f"rs_sibling_start_{tag}")

        def then(after):
            parts, lands = sibling_wait(sib, after, f"rs_sibling_wait_{tag}")
            psums = [sum_halves(g, ld, c_idx, f"rs_sum_halves_{tag}_{i}") for i, (g, ld) in enumerate(zip(parts, lands))]
            pending[(l, blk)], tok2 = scatter_start(psums, lands[0], f"rs_chips_start_{tag}")
            return tok2[0, 0]

        return tok1[0, 0], then

    def blocks_finish(blocks, after, tag):
        ssums, counts = [], []
        for l, blk in blocks:
            psums, lands2 = scatter_wait(pending.pop((l, blk)), after, f"rs_chips_wait_l{l}{blk}")
            ssums += [sum_chips(p, ld, place, f"rs_sum_chips_l{l}{blk}_{i}") for i, (p, ld) in enumerate(zip(psums, lands2))]
            counts.append(len(psums))
        fins = [f.reshape(f.shape[0], -1, D) for f in sibling_complete(ssums, f"rs_complete_{tag}")]
        out, i = [], 0
        for n in counts:
            out.append(fins[i:i + n])
            i += n
        return out

    def on_layer_small(l, grads, red_final):
        blocks = list(grads["red_n"]) + list(grads["red_g"])
        blocks.append(jnp.pad(grads["sgu_vec"], ((0, 0), (0, D - MIX_HALF))))
        blocks.append(grads["sgu_w"].reshape(-1, D))
        if red_final is not None:
            blocks.append(red_final)
        xs = jnp.concatenate(blocks, axis=0)
        small_pending[l], small_tokens[l] = small_start(xs, place, f"small_start_l{l}")
        return small_tokens[l][0, 0]

    grad_x = _local_step(x[0], loss_target[0], mods, ngs, get_w, sgus, final_g.reshape(1, D),
                         on_block_grads, on_layer_small)

    adam_state = {}

    def adam_big(nm, l, g, w, m, v):
        adam_state[nm] = adamw_layer(w, g, m, v, l, adam_state.get(nm), f"adamw_{nm}_l{l}")

    def adam_block(l, blk, fin):
        if blk == "mx":
            adam_big("w_in", l, fin[0][0].T, w_in, m_w_in, v_w_in)
            adam_big("w_out", l, fin[1][0], w_out, m_w_out, v_w_out)
        else:
            ws = ((ffn1_wg, m_ffn1_wg, v_ffn1_wg), (ffn1_wu, m_ffn1_wu, v_ffn1_wu), (ffn1_wd, m_ffn1_wd, v_ffn1_wd)) \
                if blk == "f1" else \
                ((ffn2_wg, m_ffn2_wg, v_ffn2_wg), (ffn2_wu, m_ffn2_wu, v_ffn2_wu), (ffn2_wd, m_ffn2_wd, v_ffn2_wd))
            pre = "ffn1" if blk == "f1" else "ffn2"
            for k, (nm, tr) in enumerate((("wg", True), ("wu", True), ("wd", False))):
                adam_big(f"{pre}_{nm}", l, fin[0][k], *[jnp.swapaxes(t, 1, 2) if tr else t for t in ws[k]])

    done_order = [(l, blk) for l in range(NL - 1, -1, -1) for blk in ("f2", "mx", "f1")]
    for (l, blk), fin in zip(done_order[:-1], blocks_finish(done_order[:-1], small_tokens[0], "early")):
        adam_block(l, blk, fin)
    last_big = adam_state["w_out"][1]

    small_sum, small_all = [], []
    for l in range(NL):
        xs, land = small_wait(small_pending[l], last_big, f"small_wait_l{l}")
        full = lax.dynamic_update_slice(land, xs[None], (me, 0, 0))
        small_all.append(full)
        small_sum.append(sum_slots(full, f"small_sum_l{l}"))
    offs = [8 * i for i in range(8)]
    off_final = offs[7] + SGU_HEADS * ATT_BLOCK * HEAD_LANES // D
    loss = small_sum[0][off_final + 1, 0]
    g_final_g = small_sum[0][off_final, :]
    g_norm_g, g_ada_b, g_lng, g_lnb, g_sb, g_sw, dmod_all = [], [], [], [], [], [], []
    for l in range(NL):
        rn = [small_sum[l][offs[i]:offs[i] + 8] for i in range(3)]
        rg = [small_sum[l][offs[3 + i]:offs[3 + i] + 8] for i in range(3)]
        g_norm_g.append(jnp.stack([rn[i][2] for i in range(3)], axis=0))
        g_ada_b.append(jnp.concatenate([jnp.stack([rn[i][0], rn[i][1], rg[i][0]], axis=0) for i in range(3)],
                                       axis=0).reshape(N_ADA * D))
        sv = small_sum[l][offs[6]:offs[6] + 8, :MIX_HALF]
        g_lng.append(sv[0].reshape(SGU_HEADS, HEAD_LANES))
        g_lnb.append(sv[1].reshape(SGU_HEADS, HEAD_LANES))
        g_sb.append(sv[2].reshape(SGU_HEADS, ATT_BLOCK))
        g_sw.append(small_sum[l][offs[7]:off_final].reshape(sgu_w.shape[1:]))
        rows = []
        for i in range(3):
            an = small_all[l][:, offs[i]:offs[i] + 2]
            ag = small_all[l][:, offs[3 + i]:offs[3 + i] + 1]
            rows += [an[:, 0], an[:, 1], ag[:, 0]]
        dmod_all.append(jnp.stack(rows, axis=1).reshape(N_DEV, N_ADA * D))
    dmod_all = jnp.stack(dmod_all, axis=0)
    dmod_mine = lax.dynamic_slice_in_dim(dmod_all, ci * nmod, nmod, axis=2)
    g_ada_w = ada_bwd(jnp.transpose(c_all), dmod_mine, "ada_bwd")
    g_ada_b = jnp.stack(g_ada_b, axis=0)
    g_norm_g_full = jnp.stack(g_norm_g, axis=0)
    g_norm_g_mine = lax.dynamic_slice_in_dim(g_norm_g_full, ci * ngw, ngw, axis=2)

    small_params = [
        ("ada_w", ada_w, g_ada_w, m_ada_w, v_ada_w),
        ("ada_b", ada_b, g_ada_b, m_ada_b, v_ada_b),
        ("norm_g", norm_g, g_norm_g_mine, m_norm_g, v_norm_g),
        ("sgu_ln_g", sgu_ln_g, jnp.stack(g_lng, axis=0), m_sgu_ln_g, v_sgu_ln_g),
        ("sgu_ln_b", sgu_ln_b, jnp.stack(g_lnb, axis=0), m_sgu_ln_b, v_sgu_ln_b),
        ("sgu_w", sgu_w, jnp.stack(g_sw, axis=0), m_sgu_w, v_sgu_w),
        ("sgu_b", sgu_b, jnp.stack(g_sb, axis=0), m_sgu_b, v_sgu_b),
        ("final_g", final_g.reshape(1, D), g_final_g.reshape(1, D), m_final_g.reshape(1, D), v_final_g.reshape(1, D)),
    ]
    for nm, w, g, m, v in small_params:
        res = _adam_out(w, g, m, v, f"adamw_{nm}")
        adam_state[nm] = tuple(t.reshape(D) for t in res) if nm == "final_g" else res

    l, blk = done_order[-1]
    adam_block(l, blk, blocks_finish([(l, blk)], [st[1] for st in adam_state.values()], "last")[0])

    names = ["ada_w", "ada_b", "norm_g", "ffn1_wg", "ffn1_wu", "ffn1_wd", "ffn2_wg", "ffn2_wu", "ffn2_wd", "w_in",
             "sgu_ln_g", "sgu_ln_b", "sgu_w", "sgu_b", "w_out", "final_g"]
    shapes = [t.shape for t in (ada_w, ada_b, norm_g, ffn1_wg, ffn1_wu, ffn1_wd, ffn2_wg, ffn2_wu, ffn2_wd, w_in,
                                sgu_ln_g, sgu_ln_b, sgu_w, sgu_b, w_out, final_g)]
    def shaped(nm, t, s):
        if nm in ("ffn1_wg", "ffn1_wu", "ffn2_wg", "ffn2_wu"):
            return jnp.swapaxes(t.reshape(s[0], s[2], s[1]), 1, 2)
        return t.reshape(s)

    return (loss, grad_x[None], *[shaped(nm, adam_state[nm][i], s) for i in range(4) for nm, s in zip(names, shapes)])
```

```python
import math

import jax
import jax.numpy as jnp
from jax import lax
from jax.experimental import pallas as pl
from jax.experimental.pallas import tpu as pltpu

F32 = jnp.float32
BF16 = jnp.bfloat16
EPS = 1e-6
SGU_HEADS = 4
HEAD_LANES = 128
ATT_DH = 64
ATT_BLOCK = 128
MIX_HALF = SGU_HEADS * HEAD_LANES
DILATIONS = (1, 4, 16)
ROPE_THETA = 10000.0
N_ADA = 9
ADAM_LR, ADAM_B1, ADAM_B2, ADAM_EPS, ADAM_WD, ADAM_STEP = 0.001, 0.9, 0.999, 1e-08, 0.01, 10
NEG = -1e30
V7X_VMEM_BYTES = 64 * 1024 * 1024
VMEM_LIMIT = V7X_VMEM_BYTES * 7 // 8
MESH = pl.DeviceIdType.MESH
N_DEV = 8
N_CHIP = 4
_ANY = pl.BlockSpec(memory_space=pl.ANY)


def _tile(n, cap, mult):
    if n <= cap:
        return n
    t = (cap // mult) * mult
    while t >= mult:
        if n % t == 0:
            return t
        t -= mult
    raise ValueError((n, cap, mult))


def _params(dims=None):
    return pltpu.CompilerParams(dimension_semantics=dims, vmem_limit_bytes=VMEM_LIMIT)


def _wspec(w, rows, idx, resident=False):
    arr, lead = w
    kw = dict(pipeline_mode=pl.Buffered(1)) if resident else {}
    return pl.BlockSpec((None,) * len(lead) + (rows, arr.shape[-1]), lambda *g: tuple(lead) + (idx(*g), 0), **kw)


def _wrows(w):
    return w[0].shape[-2]


def _nt(a, b):
    return lax.dot_general(a, b, (((1,), (1,)), ((), ())), preferred_element_type=F32)


def _tn(a, b):
    return lax.dot_general(a, b, (((0,), (0,)), ((), ())), preferred_element_type=F32)


def _nn(a, b):
    return jnp.dot(a, b, preferred_element_type=F32)


def _sigmoid(x):
    return 0.5 * jnp.tanh(0.5 * x) + 0.5


_GELU_K = math.sqrt(2.0 / math.pi)
_GELU_C = 0.044715


def _gelu(x):
    t = jnp.tanh(_GELU_K * (x + _GELU_C * x * x * x))
    return 0.5 * x * (1.0 + t)


def _gelu_and_grad(x):
    x2 = x * x
    t = jnp.tanh(_GELU_K * (x + _GELU_C * x * x2))
    g = 0.5 * x * (1.0 + t)
    dg = 0.5 * (1.0 + t) + 0.5 * x * (1.0 - t * t) * (_GELU_K * (1.0 + 3.0 * _GELU_C * x2))
    return g, dg


def normmod_fwd(h, ng, i_n, mod, i_sh, i_sc, name):
    T, D = h.shape
    tm = _tile(T, 512, 8)

    def body(h_ref, ng_ref, mod_ref, y_ref):
        y_ref[...] = _normmod(h_ref[...], ng_ref[i_n:i_n + 1, :], mod_ref[i_sh:i_sh + 1, :],
                              mod_ref[i_sc:i_sc + 1, :]).astype(BF16)

    return pl.pallas_call(
        body, name=name, grid=(T // tm,),
        in_specs=[pl.BlockSpec((tm, D), lambda i: (i, 0)),
                  pl.BlockSpec(ng.shape, lambda i: (0, 0)),
                  pl.BlockSpec(mod.shape, lambda i: (0, 0))],
        out_specs=pl.BlockSpec((tm, D), lambda i: (i, 0)),
        out_shape=jax.ShapeDtypeStruct((T, D), BF16),
        compiler_params=_params(("parallel",)),
    )(h, ng, mod)


def ffn_up(y, wgT, wuT, name):
    T, D = y.shape
    F = _wrows(wgT)
    tm = _tile(T, 512, 16)
    tf = _tile(F, 2816, 256)
    cuts = list(range(0, tf, 768)) + [tf]

    def body(y_ref, wg_ref, wu_ref, p_ref, q_ref, s_ref):
        yv = y_ref[...]
        for c0, c1 in zip(cuts[:-1], cuts[1:]):
            a = _nt(yv, wg_ref[c0:c1, :])
            b = _nt(yv, wu_ref[c0:c1, :])
            sig = _sigmoid(a)
            q = a * sig
            p_ref[:, c0:c1] = (b * (sig + q * (1.0 - sig))).astype(BF16)
            q_ref[:, c0:c1] = q.astype(BF16)
            s_ref[:, c0:c1] = (q * b).astype(BF16)

    act = jax.ShapeDtypeStruct((T, F), BF16)
    return pl.pallas_call(
        body, name=name, grid=(F // tf, T // tm),
        in_specs=[pl.BlockSpec((tm, D), lambda j, i: (i, 0)),
                  _wspec(wgT, tf, lambda j, i: j, resident=True),
                  _wspec(wuT, tf, lambda j, i: j, resident=True)],
        out_specs=[pl.BlockSpec((tm, tf), lambda j, i: (i, j))] * 3,
        out_shape=[act, act, act],
        compiler_params=_params(("parallel", "parallel")),
    )(y, wgT[0], wuT[0])


def _normmod(x, gn, sh, sc):
    r = lax.rsqrt(jnp.mean(x * x, axis=-1, keepdims=True) + EPS)
    return ((x * r) * gn) * (1.0 + sc) + sh


def resid_matmul(xs, w, h, mod, i_g, coef, name, norm_next=None):
    T, D = h.shape
    kb = xs[0].shape[1]
    assert all(x.shape == (T, kb) for x in xs) and _wrows(w) == kb * len(xs)
    tm = _tile(T, 1024, 16)
    nx = len(xs)
    n_in, n_out, n_shape, n_ops = [], [], [], []
    if norm_next:
        ng_n, i_n, mod_n, i_sh, i_sc = norm_next
        n_in = [pl.BlockSpec(ng_n.shape, lambda i: (0, 0)), pl.BlockSpec(mod_n.shape, lambda i: (0, 0))]
        n_out = [pl.BlockSpec((tm, D), lambda i: (i, 0))]
        n_shape = [jax.ShapeDtypeStruct((T, D), BF16)]
        n_ops = [ng_n, mod_n]

    def body(*refs):
        x_refs, w_refs = refs[:nx], refs[nx:2 * nx]
        h_ref, mod_ref = refs[2 * nx:2 * nx + 2]
        hn_ref, o_ref = refs[2 * nx + 2 + len(n_in):2 * nx + 4 + len(n_in)]
        o = _nn(x_refs[0][...], w_refs[0][...])
        for xr, wr in zip(x_refs[1:], w_refs[1:]):
            o = o + _nn(xr[...], wr[...])
        o_ref[...] = o.astype(BF16)
        hn = h_ref[...] + (coef * mod_ref[i_g:i_g + 1, :]) * o
        hn_ref[...] = hn
        if norm_next:
            ng_ref, modn_ref = refs[2 * nx + 2], refs[2 * nx + 3]
            refs[-1][...] = _normmod(hn, ng_ref[i_n:i_n + 1, :], modn_ref[i_sh:i_sh + 1, :],
                                     modn_ref[i_sc:i_sc + 1, :]).astype(BF16)

    return pl.pallas_call(
        body, name=name, grid=(T // tm,),
        in_specs=([pl.BlockSpec((tm, kb), lambda i: (i, 0))] * nx
                  + [_wspec(w, kb, lambda i, p=p: p, resident=True) for p in range(nx)]
                  + [pl.BlockSpec((tm, D), lambda i: (i, 0)),
                     pl.BlockSpec(mod.shape, lambda i: (0, 0))] + n_in),
        out_specs=[pl.BlockSpec((tm, D), lambda i: (i, 0))] * 2 + n_out,
        out_shape=[jax.ShapeDtypeStruct((T, D), F32), jax.ShapeDtypeStruct((T, D), BF16)] + n_shape,
        compiler_params=_params(("parallel",)),
    )(*xs, *([w[0]] * nx), h, mod, *n_ops)


def _gate_specs(gate, tm, D):
    o, mod, _, _ = gate
    T = o.shape[0]
    return ([pl.BlockSpec((tm, D), lambda i: (i, 0)), pl.BlockSpec(mod.shape, lambda i: (0, 0))],
            [pl.BlockSpec((tm, D), lambda i: (i, 0)), pl.BlockSpec((8, D), lambda i: (0, 0))],
            [jax.ShapeDtypeStruct((T, D), BF16), jax.ShapeDtypeStruct((8, D), F32)],
            [o, mod])


def _gate_emit(d, gate, o_ref, mod_ref, do_ref, red_ref):
    _, _, i_g, coef = gate
    do_ref[...] = (d * (coef * mod_ref[i_g:i_g + 1, :])).astype(BF16)

    @pl.when(pl.program_id(0) == 0)
    def _():
        red_ref[...] = jnp.zeros_like(red_ref)

    red_ref[0:1, :] += coef * jnp.sum(d * o_ref[...].astype(F32), axis=0, keepdims=True)


def ffn_bwd_mid(do, wd, p, q, name, after=()):
    T, D = do.shape
    F = _wrows(wd)
    tm = _tile(T, 512, 16)
    tf = _tile(F, 2816, 256)
    cuts = list(range(0, tf, 256)) + [tf]

    def body(do_ref, wd_ref, p_ref, q_ref, *rest):
        da_ref, db_ref = rest[-2:]
        dov = do_ref[...]
        for c0, c1 in zip(cuts[:-1], cuts[1:]):
            ds = _nt(dov, wd_ref[c0:c1, :])
            da_ref[:, c0:c1] = (ds * p_ref[:, c0:c1].astype(F32)).astype(BF16)
            db_ref[:, c0:c1] = (ds * q_ref[:, c0:c1].astype(F32)).astype(BF16)

    act = jax.ShapeDtypeStruct((T, F), BF16)
    return pl.pallas_call(
        body, name=name, grid=(F // tf, T // tm),
        in_specs=[pl.BlockSpec((tm, D), lambda j, i: (i, 0)),
                  _wspec(wd, tf, lambda j, i: j, resident=True),
                  pl.BlockSpec((tm, tf), lambda j, i: (i, j)),
                  pl.BlockSpec((tm, tf), lambda j, i: (i, j))] + [_ANY] * len(after),
        out_specs=[pl.BlockSpec((tm, tf), lambda j, i: (i, j))] * 2,
        out_shape=[act, act],
        compiler_params=_params(("parallel", "parallel")),
    )(do, wd[0], p, q, *after)


def dy_normbwd(pairs, h, dhp, ng, i_n, mod, i_sc, name, gate=None, after=()):
    T, D = h.shape
    tm = _tile(T, 512, 16)
    npair = len(pairs)
    g_in, g_out, g_shape, g_ops = _gate_specs(gate, tm, D) if gate else ([], [], [], [])
    n_in = 2 * npair + 4 + len(g_in) + len(after)

    def body(*refs):
        x_refs, w_refs = refs[:npair], refs[npair:2 * npair]
        h_ref, dhp_ref, ng_ref, mod_ref = refs[2 * npair:2 * npair + 4]
        dh_ref, red_ref = refs[n_in:n_in + 2]
        dy = _nn(x_refs[0][...], w_refs[0][...])
        for xr, wr in zip(x_refs[1:], w_refs[1:]):
            dy = dy + _nn(xr[...], wr[...])
        x = h_ref[...]
        r = lax.rsqrt(jnp.mean(x * x, axis=-1, keepdims=True) + EPS)
        n = x * r
        gn = ng_ref[i_n:i_n + 1, :]
        dnh = dy * (1.0 + mod_ref[i_sc:i_sc + 1, :])

        @pl.when(pl.program_id(0) == 0)
        def _():
            red_ref[...] = jnp.zeros_like(red_ref)

        red_ref[0:1, :] += jnp.sum(dy, axis=0, keepdims=True)
        red_ref[1:2, :] += jnp.sum(dy * (n * gn), axis=0, keepdims=True)
        red_ref[2:3, :] += jnp.sum(dnh * n, axis=0, keepdims=True)
        dn = dnh * gn
        dh_new = dhp_ref[...] + r * (dn - n * jnp.mean(dn * n, axis=-1, keepdims=True))
        dh_ref[...] = dh_new
        if gate:
            _gate_emit(dh_new, gate, refs[2 * npair + 4], refs[2 * npair + 5], refs[-2], refs[-1])

    in_specs = ([pl.BlockSpec((tm, kb), lambda i, c=c: (i, c)) for (_, c, _, _, kb) in pairs]
                + [_wspec(w, kb, lambda i, r=r: r, resident=True) for (_, _, w, r, kb) in pairs]
                + [pl.BlockSpec((tm, D), lambda i: (i, 0)),
                   pl.BlockSpec((tm, D), lambda i: (i, 0)),
                   pl.BlockSpec(ng.shape, lambda i: (0, 0)),
                   pl.BlockSpec(mod.shape, lambda i: (0, 0))] + g_in + [_ANY] * len(after))
    return pl.pallas_call(
        body, name=name, grid=(T // tm,), in_specs=in_specs,
        out_specs=[pl.BlockSpec((tm, D), lambda i: (i, 0)), pl.BlockSpec((8, D), lambda i: (0, 0))] + g_out,
        out_shape=[jax.ShapeDtypeStruct((T, D), F32), jax.ShapeDtypeStruct((8, D), F32)] + g_shape,
        compiler_params=_params(("arbitrary",)),
    )(*[p[0] for p in pairs], *[p[2][0] for p in pairs], h, dhp, ng, mod, *g_ops, *after)


def matmul_tn(a, b, buf, slot, row0, name, tmo_cap=1408):
    T, N = b.shape
    ma = a.shape[1]
    tmo = _tile(ma, tmo_cap, 128)
    assert row0 % tmo == 0
    nmo = ma // tmo
    tk = _tile(T, 2048, 16)
    nk = T // tk

    def body(a_ref, b_ref, buf_ref, o_ref, acc_ref):
        k = pl.program_id(1)

        @pl.when(k == 0)
        def _():
            acc_ref[...] = jnp.zeros_like(acc_ref)

        acc_ref[...] += _tn(a_ref[...], b_ref[...])

        @pl.when(k == nk - 1)
        def _():
            o_ref[...] = acc_ref[...].astype(BF16)

    return pl.pallas_call(
        body, name=name, grid=(nmo, nk),
        in_specs=[pl.BlockSpec((tk, tmo), lambda j, k: (k, j)),
                  pl.BlockSpec((tk, N), lambda j, k: (k, 0)),
                  pl.BlockSpec(memory_space=pl.ANY)],
        out_specs=pl.BlockSpec((None, tmo, N), lambda j, k: (slot, row0 // tmo + j, 0)),
        out_shape=jax.ShapeDtypeStruct(buf.shape, BF16),
        scratch_shapes=[pltpu.VMEM((tmo, N), F32)],
        input_output_aliases={2: 0},
        compiler_params=_params(("parallel", "arbitrary")),
    )(a, b, buf)


def matmul_nt(x, w, name, after=()):
    T, K = x.shape
    N = _wrows(w)
    tm = _tile(T, 1024, 16)
    tn = _tile(N, 1280, 128)

    def body(x_ref, w_ref, *rest):
        rest[-1][...] = _nt(x_ref[...], w_ref[...]).astype(BF16)

    return pl.pallas_call(
        body, name=name, grid=(N // tn, T // tm),
        in_specs=[pl.BlockSpec((tm, K), lambda j, i: (i, 0)), _wspec(w, tn, lambda j, i: j)] + [_ANY] * len(after),
        out_specs=pl.BlockSpec((tm, tn), lambda j, i: (i, j)),
        out_shape=jax.ShapeDtypeStruct((T, N), BF16),
        compiler_params=_params(("parallel", "parallel")),
    )(x, w[0], *after)


def _sgu_head_fwd(u, v, lng, lnb):
    gu, dgu = _gelu_and_grad(u)
    gv, dgv = _gelu_and_grad(v)
    mu = jnp.mean(gv, axis=-1, keepdims=True)
    xc = gv - mu
    rstd = lax.rsqrt(jnp.mean(xc * xc, axis=-1, keepdims=True) + EPS)
    xhat = xc * rstd
    vn = xhat * lng + lnb
    return gu, dgu, dgv, rstd, xhat, vn


def _tril_mask():
    r = lax.broadcasted_iota(jnp.int32, (ATT_BLOCK, ATT_BLOCK), 0)
    c = lax.broadcasted_iota(jnp.int32, (ATT_BLOCK, ATT_BLOCK), 1)
    return c <= r


def _triu_mask():
    r = lax.broadcasted_iota(jnp.int32, (ATT_BLOCK, ATT_BLOCK), 0)
    c = lax.broadcasted_iota(jnp.int32, (ATT_BLOCK, ATT_BLOCK), 1)
    return r <= c


def sgu_fwd(proj, lng, lnb, w, bcol, name):
    T = proj.shape[0]
    tm = _tile(T, 512, 128)
    nch = tm // ATT_BLOCK

    def body(u_ref, v_ref, lng_ref, lnb_ref, w_ref, b_ref, o_ref):
        tril = _tril_mask()
        for hd in range(SGU_HEADS):
            sl = slice(hd * HEAD_LANES, (hd + 1) * HEAD_LANES)
            u = u_ref[:, sl].astype(F32)
            v = v_ref[:, sl].astype(F32)
            gu, _, _, _, _, vn = _sgu_head_fwd(u, v, lng_ref[:, sl], lnb_ref[:, sl])
            wm = jnp.where(tril, w_ref[hd], 0.0).astype(BF16)
            vnb = vn.astype(BF16)
            bc = b_ref[:, hd:hd + 1]
            for ch in range(nch):
                rs = slice(ch * ATT_BLOCK, (ch + 1) * ATT_BLOCK)
                z = _nn(wm, vnb[rs, :]) + bc
                o_ref[rs, sl] = (gu[rs, :] * z).astype(BF16)

    return pl.pallas_call(
        body, name=name, grid=(T // tm,),
        in_specs=[pl.BlockSpec((tm, MIX_HALF), lambda i: (i, 0)),
                  pl.BlockSpec((tm, MIX_HALF), lambda i: (i, 1)),
                  pl.BlockSpec((1, MIX_HALF), lambda i: (0, 0)),
                  pl.BlockSpec((1, MIX_HALF), lambda i: (0, 0)),
                  pl.BlockSpec(w.shape, lambda i: (0, 0, 0)),
                  pl.BlockSpec(bcol.shape, lambda i: (0, 0))],
        out_specs=pl.BlockSpec((tm, MIX_HALF), lambda i: (i, 0)),
        out_shape=jax.ShapeDtypeStruct((T, MIX_HALF), BF16),
        compiler_params=_params(("parallel",)),
    )(proj, proj, lng, lnb, w, bcol)


def sgu_bwd(proj, dmixed, lng, lnb, w, wt, bcol, name):
    T = proj.shape[0]
    tm = _tile(T, 512, 128)
    nch = tm // ATT_BLOCK
    nsteps = T // tm

    def body(u_ref, v_ref, g_ref, lng_ref, lnb_ref, w_ref, wt_ref, b_ref, duv_ref, dw_ref, dvec_ref, bacc_ref):
        step = pl.program_id(0)

        @pl.when(step == 0)
        def _():
            dw_ref[...] = jnp.zeros_like(dw_ref)
            dvec_ref[...] = jnp.zeros_like(dvec_ref)
            bacc_ref[...] = jnp.zeros_like(bacc_ref)

        tril = _tril_mask()
        triu = _triu_mask()
        for hd in range(SGU_HEADS):
            sl = slice(hd * HEAD_LANES, (hd + 1) * HEAD_LANES)
            u = u_ref[:, sl].astype(F32)
            v = v_ref[:, sl].astype(F32)
            lng_h = lng_ref[:, sl]
            gu, dgu, dgv, rstd, xhat, vn = _sgu_head_fwd(u, v, lng_h, lnb_ref[:, sl])
            wm = jnp.where(tril, w_ref[hd], 0.0).astype(BF16)
            wmt = jnp.where(triu, wt_ref[hd], 0.0).astype(BF16)
            vnb = vn.astype(BF16)
            bc = b_ref[:, hd:hd + 1]
            g = g_ref[:, sl].astype(F32)
            dw_acc = jnp.zeros((ATT_BLOCK, ATT_BLOCK), F32)
            b_acc = jnp.zeros((ATT_BLOCK, HEAD_LANES), F32)
            dvn_parts = []
            for ch in range(nch):
                rs = slice(ch * ATT_BLOCK, (ch + 1) * ATT_BLOCK)
                z = _nn(wm, vnb[rs, :]) + bc
                duv_ref[rs, sl] = (g[rs, :] * z * dgu[rs, :]).astype(BF16)
                dz = g[rs, :] * gu[rs, :]
                dzb = dz.astype(BF16)
                dvn_parts.append(_nn(wmt, dzb))
                dw_acc = dw_acc + _nt(dzb, vnb[rs, :])
                b_acc = b_acc + dz
            dvn = jnp.concatenate(dvn_parts, axis=0)
            dw_ref[hd] += jnp.where(tril, dw_acc, 0.0)
            bacc_ref[hd] += b_acc
            dvec_ref[0:1, sl] += jnp.sum(dvn * xhat, axis=0, keepdims=True)
            dvec_ref[1:2, sl] += jnp.sum(dvn, axis=0, keepdims=True)
            dxh = dvn * lng_h
            dgv_in = rstd * (dxh - jnp.mean(dxh, axis=-1, keepdims=True)
                             - xhat * jnp.mean(dxh * xhat, axis=-1, keepdims=True))
            duv_ref[:, MIX_HALF + hd * HEAD_LANES:MIX_HALF + (hd + 1) * HEAD_LANES] = (dgv_in * dgv).astype(BF16)

        @pl.when(step == nsteps - 1)
        def _():
            for hd in range(SGU_HEADS):
                sl = slice(hd * HEAD_LANES, (hd + 1) * HEAD_LANES)
                dvec_ref[2:3, sl] = jnp.sum(bacc_ref[hd].T, axis=0, keepdims=True)

    return pl.pallas_call(
        body, name=name, grid=(nsteps,),
        in_specs=[pl.BlockSpec((tm, MIX_HALF), lambda i: (i, 0)),
                  pl.BlockSpec((tm, MIX_HALF), lambda i: (i, 1)),
                  pl.BlockSpec((tm, MIX_HALF), lambda i: (i, 0)),
                  pl.BlockSpec((1, MIX_HALF), lambda i: (0, 0)),
                  pl.BlockSpec((1, MIX_HALF), lambda i: (0, 0)),
                  pl.BlockSpec(w.shape, lambda i: (0, 0, 0)),
                  pl.BlockSpec(w.shape, lambda i: (0, 0, 0)),
                  pl.BlockSpec(bcol.shape, lambda i: (0, 0))],
        out_specs=[pl.BlockSpec((tm, 2 * MIX_HALF), lambda i: (i, 0)),
                   pl.BlockSpec(w.shape, lambda i: (0, 0, 0)),
                   pl.BlockSpec((8, MIX_HALF), lambda i: (0, 0))],
        out_shape=[jax.ShapeDtypeStruct((T, 2 * MIX_HALF), BF16),
                   jax.ShapeDtypeStruct(w.shape, F32),
                   jax.ShapeDtypeStruct((8, MIX_HALF), F32)],
        scratch_shapes=[pltpu.VMEM((SGU_HEADS, ATT_BLOCK, HEAD_LANES), F32)],
        compiler_params=_params(("arbitrary",)),
    )(proj, proj, dmixed, lng, lnb, w, wt, bcol)


def _rot_half(t):
    lane = lax.broadcasted_iota(jnp.int32, t.shape, 1)
    first = (lane % ATT_DH) < (ATT_DH // 2)
    return jnp.where(first, -pltpu.roll(t, HEAD_LANES - ATT_DH // 2, 1), pltpu.roll(t, ATT_DH // 2, 1))


LAYOUT_ROWS = 512


def _res_spec(d, tm, W):
    return pl.BlockSpec((d, tm // d, W), lambda i: (0, i, 0))


def _res_shape(d, T, W, dtype):
    return jax.ShapeDtypeStruct((d, T // d, W), dtype)


def _slab_buf(tm, W):
    return pltpu.VMEM((W // HEAD_LANES, tm, HEAD_LANES), F32)


def _lanes(hp):
    return slice(hp * HEAD_LANES, (hp + 1) * HEAD_LANES)


def _to_res(buf, out_ref, d, dtype):
    nslab, tm, _ = buf.shape
    for hp in range(nslab):
        if d == 1:
            out_ref[0, :, _lanes(hp)] = buf[hp].astype(dtype)
        else:
            for r in range(d):
                out_ref[r, :, _lanes(hp)] = buf.at[hp][pl.ds(r, tm // d, stride=d), :].astype(dtype)


def _from_res(in_ref, buf, d):
    nslab, tm, _ = buf.shape
    for hp in range(nslab):
        if d == 1:
            buf[hp] = in_ref[0, :, _lanes(hp)]
        else:
            for r in range(d):
                buf.at[hp][pl.ds(r, tm // d, stride=d), :] = in_ref[r, :, _lanes(hp)]


def rope_fwd(proj, cos, sin, name):
    T = proj.shape[0]
    tm = LAYOUT_ROWS
    scale = 1.0 / math.sqrt(ATT_DH)
    nd = len(DILATIONS)

    def body(q_ref, k_ref, v_ref, cos_ref, sin_ref, *rest):
        outs, buf = rest[:3 * nd], rest[3 * nd]
        c = cos_ref[...]
        s = sin_ref[...]
        for which, src in enumerate((q_ref, k_ref, v_ref)):
            for hp in range(MIX_HALF // HEAD_LANES):
                t = src[:, _lanes(hp)].astype(F32)
                if which == 0:
                    t = scale * (t * c + _rot_half(t) * s)
                elif which == 1:
                    t = t * c + _rot_half(t) * s
                buf[hp] = t
            for di, d in enumerate(DILATIONS):
                _to_res(buf, outs[3 * di + which], d, BF16)

    return pl.pallas_call(
        body, name=name, grid=(T // tm,),
        in_specs=[pl.BlockSpec((tm, MIX_HALF), lambda i: (i, 2)),
                  pl.BlockSpec((tm, MIX_HALF), lambda i: (i, 3)),
                  pl.BlockSpec((tm, MIX_HALF), lambda i: (i, 4)),
                  pl.BlockSpec((tm, HEAD_LANES), lambda i: (i, 0)),
                  pl.BlockSpec((tm, HEAD_LANES), lambda i: (i, 0))],
        out_specs=[_res_spec(d, tm, MIX_HALF) for d in DILATIONS for _ in range(3)],
        out_shape=[_res_shape(d, T, MIX_HALF, BF16) for d in DILATIONS for _ in range(3)],
        scratch_shapes=[_slab_buf(tm, MIX_HALF)],
        compiler_params=_params(("parallel",)),
    )(proj, proj, proj, cos, sin)


def to_residues(x, col, name):
    T = x.shape[0]
    tm = LAYOUT_ROWS

    def body(x_ref, *rest):
        outs, buf = rest[:-1], rest[-1]
        for hp in range(MIX_HALF // HEAD_LANES):
            buf[hp] = x_ref[:, _lanes(hp)].astype(F32)
        for o_ref, d in zip(outs, DILATIONS):
            _to_res(buf, o_ref, d, BF16)

    return pl.pallas_call(
        body, name=name, grid=(T // tm,),
        in_specs=[pl.BlockSpec((tm, MIX_HALF), lambda i: (i, col))],
        out_specs=[_res_spec(d, tm, MIX_HALF) for d in DILATIONS],
        out_shape=[_res_shape(d, T, MIX_HALF, BF16) for d in DILATIONS],
        scratch_shapes=[_slab_buf(tm, MIX_HALF)],
        compiler_params=_params(("parallel",)),
    )(x)


def rope_bwd(dqs, dks, dvs, cos, sin, name):
    T = dqs[0].shape[0] * dqs[0].shape[1]
    tm = LAYOUT_ROWS
    scale = 1.0 / math.sqrt(ATT_DH)
    npat = len(dqs)

    def body(*refs):
        groups = refs[:npat], refs[npat:2 * npat], refs[2 * npat:3 * npat]
        cos_ref, sin_ref, o_ref, buf, acc = refs[3 * npat:]
        c = cos_ref[...]
        s = sin_ref[...]
        for which, g_refs in enumerate(groups):
            _from_res(g_refs[0], acc, DILATIONS[0])
            for g_ref, d in zip(g_refs[1:], DILATIONS[1:]):
                _from_res(g_ref, buf, d)
                acc[...] += buf[...]
            for hp in range(MIX_HALF // HEAD_LANES):
                g = acc[hp]
                if which == 0:
                    g = scale * g
                if which < 2:
                    g = g * c - _rot_half(g * s)
                o_ref[:, which * MIX_HALF + hp * HEAD_LANES:which * MIX_HALF + (hp + 1) * HEAD_LANES] = g.astype(BF16)

    return pl.pallas_call(
        body, name=name, grid=(T // tm,),
        in_specs=([_res_spec(d, tm, MIX_HALF) for _ in range(3) for d in DILATIONS]
                  + [pl.BlockSpec((tm, HEAD_LANES), lambda i: (i, 0))] * 2),
        out_specs=pl.BlockSpec((tm, 3 * MIX_HALF), lambda i: (i, 0)),
        out_shape=jax.ShapeDtypeStruct((T, 3 * MIX_HALF), BF16),
        scratch_shapes=[_slab_buf(tm, MIX_HALF), _slab_buf(tm, MIX_HALF)],
        compiler_params=_params(("parallel",)),
    )(*dqs, *dks, *dvs, cos, sin)


def _band_masks(n):
    r = lax.broadcasted_iota(jnp.int32, (2 * ATT_BLOCK, ATT_BLOCK), 0)
    c = lax.broadcasted_iota(jnp.int32, (2 * ATT_BLOCK, ATT_BLOCK), 1)
    qi = r % ATT_BLOCK
    head = (c < ATT_DH) == (r < ATT_BLOCK)
    return (c >= qi) & (n > 0), c <= qi, head, c[:ATT_BLOCK] < ATT_DH


def _stack_heads(x, head):
    x2 = jnp.concatenate([x, x], axis=0)
    return jnp.where(head, x2, jnp.zeros_like(x2))


def attn_fwd(q, k, v, name):
    d, L, W = q.shape
    nb = L // ATT_BLOCK

    def body(q_ref, kp_ref, kc_ref, vp_ref, vc_ref, o_ref, lse_ref):
        mask_p, mask_c, head, head0 = _band_masks(pl.program_id(1))
        for hp in range(W // HEAD_LANES):
            sl = slice(hp * HEAD_LANES, (hp + 1) * HEAD_LANES)
            kp, kc, vp, vc = kp_ref[0, :, sl], kc_ref[0, :, sl], vp_ref[0, :, sl], vc_ref[0, :, sl]
            qs = _stack_heads(q_ref[0, :, sl], head)
            sp = jnp.where(mask_p, _nt(qs, kp), NEG)
            sc = jnp.where(mask_c, _nt(qs, kc), NEG)
            m = jnp.maximum(jnp.max(sp, axis=1, keepdims=True), jnp.max(sc, axis=1, keepdims=True))
            pp = jnp.exp(sp - m)
            pc = jnp.exp(sc - m)
            den = jnp.sum(pp, axis=1, keepdims=True) + jnp.sum(pc, axis=1, keepdims=True)
            o = (_nn(pp.astype(BF16), vp) + _nn(pc.astype(BF16), vc)) / den
            lse = m + jnp.log(den)
            o_ref[0, :, sl] = jnp.where(head0, o[:ATT_BLOCK], o[ATT_BLOCK:])
            lse_ref[0, :, sl] = jnp.where(head0, lse[:ATT_BLOCK], lse[ATT_BLOCK:])

    cur = pl.BlockSpec((1, ATT_BLOCK, W), lambda r, n: (r, n, 0))
    prev = pl.BlockSpec((1, ATT_BLOCK, W), lambda r, n: (r, jnp.maximum(n - 1, 0), 0))
    out = jax.ShapeDtypeStruct((d, L, W), F32)
    return pl.pallas_call(
        body, name=name, grid=(d, nb),
        in_specs=[cur, prev, cur, prev, cur],
        out_specs=[cur, cur], out_shape=[out, out],
        compiler_params=_params(("parallel", "parallel")),
    )(q, k, k, v, v)


def attn_combine(os_, lses, name):
    T = os_[0].shape[0] * os_[0].shape[1]
    W = os_[0].shape[2]
    tm = LAYOUT_ROWS
    npat = len(os_)

    def body(*refs):
        o_refs, l_refs = refs[:npat], refs[npat:2 * npat]
        out_ref = refs[2 * npat]
        ores, lres = refs[2 * npat + 1:3 * npat + 1], refs[3 * npat + 1:4 * npat + 1]
        bufs = refs[4 * npat + 1:]
        lbufs, obufs, out_buf, lse_buf = bufs[:npat], bufs[npat:2 * npat], bufs[2 * npat], bufs[2 * npat + 1]
        for p, d in enumerate(DILATIONS):
            _from_res(l_refs[p], lbufs[p], d)
            _from_res(o_refs[p], obufs[p], d)
        for hp in range(W // HEAD_LANES):
            ls = [b[hp] for b in lbufs]
            m = ls[0]
            for l in ls[1:]:
                m = jnp.maximum(m, l)
            es = [jnp.exp(l - m) for l in ls]
            z = es[0]
            for e in es[1:]:
                z = z + e
            acc = es[0] * obufs[0][hp]
            for p in range(1, npat):
                acc = acc + es[p] * obufs[p][hp]
            out = acc / z
            out_ref[:, _lanes(hp)] = out.astype(BF16)
            out_buf[hp] = out
            lse_buf[hp] = m + jnp.log(z)
        for p, d in enumerate(DILATIONS):
            _to_res(out_buf, ores[p], d, BF16)
            _to_res(lse_buf, lres[p], d, F32)

    return pl.pallas_call(
        body, name=name, grid=(T // tm,),
        in_specs=[_res_spec(d, tm, W) for _ in range(2) for d in DILATIONS],
        out_specs=([pl.BlockSpec((tm, W), lambda i: (i, 0))] + [_res_spec(d, tm, W) for _ in range(2) for d in DILATIONS]),
        out_shape=([jax.ShapeDtypeStruct((T, W), BF16)] + [_res_shape(d, T, W, BF16) for d in DILATIONS]
                   + [_res_shape(d, T, W, F32) for d in DILATIONS]),
        scratch_shapes=[_slab_buf(tm, W)] * (2 * npat + 2),
        compiler_params=_params(("parallel",)),
    )(*os_, *lses)


def attn_bwd(q, k, v, do, o, lse, name):
    d, L, W = q.shape
    nb = L // ATT_BLOCK

    def body(q_ref, kp_ref, kc_ref, vp_ref, vc_ref, do_ref, o_ref, lse_ref, dq_ref, dk_ref, dv_ref, kkeep, vkeep):
        n = pl.program_id(1)

        @pl.when(n == 0)
        def _():
            kkeep[...] = jnp.zeros_like(kkeep)
            vkeep[...] = jnp.zeros_like(vkeep)

        @pl.when(n < nb)
        def _():
            mask_p, mask_c, head, head0 = _band_masks(n)
            for hp in range(W // HEAD_LANES):
                sl = slice(hp * HEAD_LANES, (hp + 1) * HEAD_LANES)
                kp, kc, vp, vc = kp_ref[0, :, sl], kc_ref[0, :, sl], vp_ref[0, :, sl], vc_ref[0, :, sl]
                dout = do_ref[0, :, sl]
                qs = _stack_heads(q_ref[0, :, sl], head)
                dos = _stack_heads(dout, head)
                lse_v = lse_ref[0, :, sl]
                lse_c = jnp.max(jnp.where(head, jnp.concatenate([lse_v, lse_v], axis=0), NEG), axis=1, keepdims=True)
                delta = jnp.sum(_stack_heads(dout.astype(F32) * o_ref[0, :, sl].astype(F32), head), axis=1, keepdims=True)
                pp = jnp.exp(jnp.where(mask_p, _nt(qs, kp), NEG) - lse_c)
                pc = jnp.exp(jnp.where(mask_c, _nt(qs, kc), NEG) - lse_c)
                dsp = (pp * (_nt(dos, vp) - delta)).astype(BF16)
                dsc = (pc * (_nt(dos, vc) - delta)).astype(BF16)
                dq2 = _nn(dsp, kp) + _nn(dsc, kc)
                dq_ref[0, :, sl] = jnp.where(head0, dq2[:ATT_BLOCK], dq2[ATT_BLOCK:])
                dk_ref[0, :, sl] = kkeep[:, sl] + _tn(dsp, qs)
                dv_ref[0, :, sl] = vkeep[:, sl] + _tn(pp.astype(BF16), dos)
                kkeep[:, sl] = _tn(dsc, qs)
                vkeep[:, sl] = _tn(pc.astype(BF16), dos)

        @pl.when(n == nb)
        def _():
            dk_ref[0] = kkeep[...]
            dv_ref[0] = vkeep[...]

    cur = pl.BlockSpec((1, ATT_BLOCK, W), lambda r, n: (r, jnp.minimum(n, nb - 1), 0))
    prev = pl.BlockSpec((1, ATT_BLOCK, W), lambda r, n: (r, jnp.clip(n - 1, 0, nb - 1), 0))
    out = jax.ShapeDtypeStruct((d, L, W), F32)
    return pl.pallas_call(
        body, name=name, grid=(d, nb + 1),
        in_specs=[cur, prev, cur, prev, cur, cur, cur, cur],
        out_specs=[cur, prev, prev], out_shape=[out, out, out],
        scratch_shapes=[pltpu.VMEM((ATT_BLOCK, W), F32), pltpu.VMEM((ATT_BLOCK, W), F32)],
        compiler_params=_params(("parallel", "arbitrary")),
    )(q, k, k, v, v, do, o, lse)


def final_loss_bwd(h, gf, tgt, gate, name):
    T, D = h.shape
    tm = _tile(T, 512, 16)
    g_in, g_out, g_shape, g_ops = _gate_specs(gate, tm, D)

    def body(h_ref, g_ref, t_ref, o_ref, modg_ref, dh_ref, red_ref, do_ref, redg_ref):
        x = h_ref[...]
        r = lax.rsqrt(jnp.mean(x * x, axis=-1, keepdims=True) + EPS)
        n = x * r
        g = g_ref[...]
        err = n * g - t_ref[...]
        dy = err * (1.0 / D)

        @pl.when(pl.program_id(0) == 0)
        def _():
            red_ref[...] = jnp.zeros_like(red_ref)

        red_ref[0:1, :] += jnp.sum(dy * n, axis=0, keepdims=True)
        red_ref[1:2, :] += jnp.zeros((1, D), F32) + (0.5 / D) * jnp.sum(err * err, keepdims=True)
        dn = dy * g
        dh = r * (dn - n * jnp.mean(dn * n, axis=-1, keepdims=True))
        dh_ref[...] = dh
        _gate_emit(dh, gate, o_ref, modg_ref, do_ref, redg_ref)

    return pl.pallas_call(
        body, name=name, grid=(T // tm,),
        in_specs=[pl.BlockSpec((tm, D), lambda i: (i, 0)),
                  pl.BlockSpec((1, D), lambda i: (0, 0)),
                  pl.BlockSpec((tm, D), lambda i: (i, 0))] + g_in,
        out_specs=[pl.BlockSpec((tm, D), lambda i: (i, 0)), pl.BlockSpec((8, D), lambda i: (0, 0))] + g_out,
        out_shape=[jax.ShapeDtypeStruct((T, D), F32), jax.ShapeDtypeStruct((8, D), F32)] + g_shape,
        compiler_params=_params(("arbitrary",)),
    )(h, gf, tgt, *g_ops)


def ada_fwd(c_all, ada_w, ada_b, name):
    nl, D, N = ada_w.shape

    def body(c_ref, w_ref, b_ref, o_ref):
        c = c_ref[...]
        o_ref[0] = _nn(c * _sigmoid(c), w_ref[0]) + b_ref[0]

    return pl.pallas_call(
        body, name=name, grid=(nl,),
        in_specs=[pl.BlockSpec((N_DEV, D), lambda l: (0, 0)),
                  pl.BlockSpec((1, D, N), lambda l: (l, 0, 0)),
                  pl.BlockSpec((1, 1, N), lambda l: (l, 0, 0))],
        out_specs=pl.BlockSpec((1, N_DEV, N), lambda l: (l, 0, 0)),
        out_shape=jax.ShapeDtypeStruct((nl, N_DEV, N), F32),
        compiler_params=_params(("parallel",)),
    )(c_all, ada_w, ada_b)


def ada_bwd(c_allT, dmod, name):
    nl, _, N = dmod.shape
    D = c_allT.shape[0]

    def body(c_ref, g_ref, o_ref):
        c = c_ref[...]
        ca = c * _sigmoid(c)
        acc = ca[:, 0:1] * g_ref[0, 0:1, :]
        for b in range(1, N_DEV):
            acc = acc + ca[:, b:b + 1] * g_ref[0, b:b + 1, :]
        o_ref[0] = acc

    return pl.pallas_call(
        body, name=name, grid=(nl,),
        in_specs=[pl.BlockSpec((D, N_DEV), lambda l: (0, 0)),
                  pl.BlockSpec((1, N_DEV, N), lambda l: (l, 0, 0))],
        out_specs=pl.BlockSpec((1, D, N), lambda l: (l, 0, 0)),
        out_shape=jax.ShapeDtypeStruct((nl, D, N), F32),
        compiler_params=_params(("parallel",)),
    )(c_allT, dmod)


def adamw(w, g, m, v, name):
    R, C = w.shape
    tr = _tile(R, max(8, (1 << 19) // C // 8 * 8), 8)
    c1 = 1.0 - ADAM_B1 ** ADAM_STEP
    c2 = 1.0 - ADAM_B2 ** ADAM_STEP

    def body(w_ref, g_ref, m_ref, v_ref, d_ref, mo_ref, vo_ref):
        gv = g_ref[...]
        mn = ADAM_B1 * m_ref[...] + (1.0 - ADAM_B1) * gv
        vn = ADAM_B2 * v_ref[...] + (1.0 - ADAM_B2) * (gv * gv)
        mo_ref[...] = mn
        vo_ref[...] = vn
        d_ref[...] = -ADAM_LR * ((mn / c1) / (jnp.sqrt(vn / c2) + ADAM_EPS) + ADAM_WD * w_ref[...])

    blk = pl.BlockSpec((tr, C), lambda i: (i, 0))
    out = jax.ShapeDtypeStruct((R, C), F32)
    return pl.pallas_call(
        body, name=name, grid=(R // tr,),
        in_specs=[blk] * 4, out_specs=[blk] * 3, out_shape=[out] * 3,
        compiler_params=_params(("parallel",)),
    )(w, g, m, v)


def adamw_layer(w, g, m, v, l, prev, name):
    NLw, R, C = w.shape
    tr = _tile(R, max(8, (1 << 19) // C // 8 * 8), 8)
    nrb = R // tr
    c1 = 1.0 - ADAM_B1 ** ADAM_STEP
    c2 = 1.0 - ADAM_B2 ** ADAM_STEP
    w, m, v = (t.reshape(NLw * R, C) for t in (w, m, v))

    def body(w_ref, g_ref, m_ref, v_ref, *rest):
        go_ref, d_ref, mo_ref, vo_ref = rest[-4:]
        gv = g_ref[...]
        mn = ADAM_B1 * m_ref[...] + (1.0 - ADAM_B1) * gv
        vn = ADAM_B2 * v_ref[...] + (1.0 - ADAM_B2) * (gv * gv)
        go_ref[...] = gv
        mo_ref[...] = mn
        vo_ref[...] = vn
        d_ref[...] = -ADAM_LR * ((mn / c1) / (jnp.sqrt(vn / c2) + ADAM_EPS) + ADAM_WD * w_ref[...])

    lay = pl.BlockSpec((tr, C), lambda i: (l * nrb + i, 0))
    out = jax.ShapeDtypeStruct((NLw * R, C), F32)
    n_prev = 0 if prev is None else 4
    return pl.pallas_call(
        body, name=name, grid=(nrb,),
        in_specs=[lay, pl.BlockSpec((tr, C), lambda i: (i, 0)), lay, lay] + [pl.BlockSpec(memory_space=pl.ANY)] * n_prev,
        out_specs=[lay] * 4, out_shape=[out] * 4,
        input_output_aliases={4 + i: i for i in range(n_prev)},
        compiler_params=_params(("parallel",)),
    )(w, g, m, v, *(prev or ()))


def sum_slots(x, name):
    S, R, C = x.shape
    tr = _tile(R, 128, 8)

    def body(x_ref, o_ref):
        acc = x_ref[0]
        for s in range(1, S):
            acc = acc + x_ref[s]
        o_ref[...] = acc

    return pl.pallas_call(
        body, name=name, grid=(R // tr,),
        in_specs=[pl.BlockSpec((S, tr, C), lambda i: (0, i, 0))],
        out_specs=pl.BlockSpec((tr, C), lambda i: (i, 0)),
        out_shape=jax.ShapeDtypeStruct((R, C), F32),
        compiler_params=_params(("parallel",)),
    )(x)


def sum_halves(g, lands, c_idx, name):
    n, ns, _, rh, D = g.shape

    def body(c_ref, g_ref, l_ref, o_ref):
        for j in range(ns):
            o_ref[0, j] = (g_ref[0, j, 0].astype(F32) + l_ref[0, j].astype(F32)).astype(BF16)

    return pl.pallas_call(
        body, name=name,
        grid_spec=pltpu.PrefetchScalarGridSpec(
            num_scalar_prefetch=1, grid=(n,),
            in_specs=[pl.BlockSpec((1, ns, 1, rh, D), lambda i, c: (i, 0, c[0], 0, 0)),
                      pl.BlockSpec((1, ns, rh, D), lambda i, c: (i, 0, 0, 0))],
            out_specs=pl.BlockSpec((1, ns, rh, D), lambda i, c: (i, 0, 0, 0))),
        out_shape=jax.ShapeDtypeStruct((n, ns, rh, D), BF16),
        compiler_params=_params(("parallel",)),
    )(c_idx, g, lands)


def sum_chips(p, lands, place, name):
    n, ns, rh, D = p.shape

    def body(c_ref, p_ref, l_ref, o_ref):
        acc = p_ref[0, 0].astype(F32)
        for j in range(N_CHIP - 1):
            acc = acc + l_ref[j, 0].astype(F32)
        o_ref[0, 0] = acc

    return pl.pallas_call(
        body, name=name,
        grid_spec=pltpu.PrefetchScalarGridSpec(
            num_scalar_prefetch=1, grid=(n,),
            in_specs=[pl.BlockSpec((1, 1, rh, D), lambda i, c: (i, c[0], 0, 0)),
                      pl.BlockSpec((N_CHIP - 1, 1, rh, D), lambda i, c: (0, i, 0, 0))],
            out_specs=pl.BlockSpec((1, 1, rh, D), lambda i, c: (i, c[1], 0, 0))),
        out_shape=jax.ShapeDtypeStruct((n, 2, rh, D), F32),
        compiler_params=_params(("parallel",)),
    )(place, p, lands)


def _my_place():
    return lax.axis_index("x"), lax.axis_index("y"), lax.axis_index("c")


def _other_chips(mx, my):
    return [(1 - mx, my), (mx, 1 - my), (1 - mx, 1 - my)]


def gather_small(x, after, name):
    def body(x_ref, after_ref, out_ref, sum_ref, send_sems, recv_sems):
        mx, my, mc = _my_place()
        me = 4 * mx + 2 * my + mc
        out_ref[me] = x_ref[...]
        sends = []
        for k in range(1, N_DEV):
            kx, ky, kc = (k >> 2) & 1, (k >> 1) & 1, k & 1
            peer = (1 - mx if kx else mx, 1 - my if ky else my, 1 - mc if kc else mc)
            cp = pltpu.make_async_remote_copy(
                src_ref=x_ref, dst_ref=out_ref.at[me], send_sem=send_sems.at[k - 1], recv_sem=recv_sems.at[k - 1],
                device_id=peer, device_id_type=MESH)
            cp.start()
            sends.append((cp, 4 * peer[0] + 2 * peer[1] + peer[2], peer))
        for k, (cp, peer_slot, peer) in enumerate(sends):
            pltpu.make_async_remote_copy(
                src_ref=x_ref, dst_ref=out_ref.at[peer_slot], send_sem=send_sems.at[k], recv_sem=recv_sems.at[k],
                device_id=peer, device_id_type=MESH).wait_recv()
        for cp, _, _ in sends:
            cp.wait_send()
        acc = out_ref[0]
        for s in range(1, N_DEV):
            acc = acc + out_ref[s]
        sum_ref[...] = acc

    vmem = pl.BlockSpec(memory_space=pltpu.VMEM)
    return pl.pallas_call(
        body, name=name,
        in_specs=[vmem, pl.BlockSpec(memory_space=pl.ANY)], out_specs=[vmem, vmem],
        out_shape=[jax.ShapeDtypeStruct((N_DEV,) + x.shape, x.dtype), jax.ShapeDtypeStruct(x.shape, x.dtype)],
        scratch_shapes=[pltpu.SemaphoreType.DMA((N_DEV - 1,)), pltpu.SemaphoreType.DMA((N_DEV - 1,))],
        compiler_params=pltpu.CompilerParams(vmem_limit_bytes=VMEM_LIMIT),
    )(x, after)


_HBM =pl.BlockSpec(memory_space=pltpu.HBM)
_SEM = pl.BlockSpec(memory_space=pltpu.SEMAPHORE)
_DATAFLOW = pltpu.SideEffectType.DATAFLOW_SIDE_EFFECTING


def _gather_copies(shard, land, send, recv, base):
    mx, my, mc = _my_place()
    ci = 2 * mx + my
    peers = [((cx, cy, mc), 2 * cx + cy) for cx, cy in _other_chips(mx, my)] + [((mx, my, 1 - mc), ci)]
    out = []
    for q, (dev, src_slot) in enumerate(peers):
        out.append((
            pltpu.make_async_remote_copy(src_ref=shard, dst_ref=land.at[:, ci], send_sem=send.at[base + q],
                                         recv_sem=recv.at[base + q], device_id=dev, device_id_type=MESH),
            pltpu.make_async_remote_copy(src_ref=shard, dst_ref=land.at[:, src_slot], send_sem=send.at[base + q],
                                         recv_sem=recv.at[base + q], device_id=dev, device_id_type=MESH)))
    return out


def gather_start(groups, after, name):
    items = [s for g in groups for s in g]
    ni, ng = len(items), len(groups)

    def body(*refs):
        shards, lands = refs[:ni], refs[ni:2 * ni]
        sems = refs[2 * ni + 1:2 * ni + 1 + 2 * ng]
        token = refs[-1]
        i = 0
        for g, grp in enumerate(groups):
            for p in range(len(grp)):
                for start_cp, _ in _gather_copies(shards[i], lands[i], sems[2 * g], sems[2 * g + 1], 4 * p):
                    start_cp.start()
                i += 1
        token[...] = jnp.zeros_like(token)

    sem_shapes = []
    for grp in groups:
        sem_shapes += [pltpu.SemaphoreType.DMA((4 * len(grp),))] * 2
    land_shapes = [(s.shape[0], N_CHIP) + s.shape[1:] for s in items]
    outs = pl.pallas_call(
        body, name=name,
        in_specs=[_HBM] * (2 * ni) + [pl.BlockSpec(memory_space=pl.ANY)],
        out_specs=[_SEM] * (2 * ng) + [_HBM] * (2 * ni) + [pl.BlockSpec(memory_space=pltpu.VMEM)],
        out_shape=(sem_shapes + [pltpu.HBM(s.shape, s.dtype) for s in items]
                   + [pltpu.HBM(ls, s.dtype) for ls, s in zip(land_shapes, items)]
                   + [jax.ShapeDtypeStruct((8, 128), F32)]),
        input_output_aliases={i: 2 * ng + i for i in range(2 * ni)},
        compiler_params=pltpu.CompilerParams(has_side_effects=_DATAFLOW),
    )(*[pltpu.with_memory_space_constraint(s, pltpu.HBM) for s in items],
      *[pltpu.with_memory_space_constraint(lax.empty(ls, s.dtype), pltpu.HBM) for ls, s in zip(land_shapes, items)],
      after)
    sems, thru, token = outs[:2 * ng], outs[2 * ng:2 * ng + 2 * ni], outs[-1]
    handles, i = [], 0
    for g, grp in enumerate(groups):
        n = len(grp)
        handles.append((sems[2 * g], sems[2 * g + 1], thru[i:i + n], thru[ni + i:ni + i + n]))
        i += n
    return handles, token


def gather_wait(handle, after, name):
    send, recv, shards, lands = handle
    n = len(shards)

    def body(*refs):
        shard_refs, land_refs = refs[:n], refs[n:2 * n]
        send_ref, recv_ref = refs[2 * n], refs[2 * n + 1]
        for p in range(n):
            for start_cp, recv_cp in _gather_copies(shard_refs[p], land_refs[p], send_ref, recv_ref, 4 * p):
                start_cp.wait_send()
                recv_cp.wait_recv()

    outs = pl.pallas_call(
        body, name=name,
        in_specs=[_HBM] * (2 * n) + [_SEM, _SEM, pl.BlockSpec(memory_space=pl.ANY)],
        out_specs=[_HBM] * (2 * n),
        out_shape=[pltpu.HBM(s.shape, s.dtype) for s in shards] + [pltpu.HBM(l.shape, l.dtype) for l in lands],
        input_output_aliases={i: i for i in range(2 * n)},
        compiler_params=pltpu.CompilerParams(has_side_effects=_DATAFLOW),
    )(*shards, *lands, send, recv, after)
    return outs[n:]


def _first_copies(shard, land, send, recv):
    mx, my, mc = _my_place()
    ci = 2 * mx + my
    out = []
    for q, (cx, cy) in enumerate(_other_chips(mx, my)):
        dev = (cx, cy, mc)
        out.append(tuple(pltpu.make_async_remote_copy(
            src_ref=shard.at[:, mc], dst_ref=land.at[:, slot, mc], send_sem=send.at[q], recv_sem=recv.at[q],
            device_id=dev, device_id_type=MESH) for slot in (ci, 2 * cx + cy)))
    sib = pltpu.make_async_remote_copy(src_ref=shard, dst_ref=land.at[:, ci], send_sem=send.at[3], recv_sem=recv.at[3],
                                       device_id=(mx, my, 1 - mc), device_id_type=MESH)
    return out + [(sib, sib)]


def _forward_copies(land, send, recv):
    mx, my, mc = _my_place()
    out = []
    for q, (cx, cy) in enumerate(_other_chips(mx, my)):
        out.append(tuple(pltpu.make_async_remote_copy(
            src_ref=land.at[:, 2 * cx + cy, hc], dst_ref=land.at[:, 2 * cx + cy, hc], send_sem=send.at[q],
            recv_sem=recv.at[q], device_id=(mx, my, 1 - mc), device_id_type=MESH) for hc in (mc, 1 - mc)))
    return out


def first_start(shard, after, name):
    def body(shard_ref, land_ref, after_ref, send, recv, shard_thru, land_thru, token):
        for mine, _ in _first_copies(shard_ref, land_ref, send, recv):
            mine.start()
        token[...] = jnp.zeros_like(token)

    land_shape = (shard.shape[0], N_CHIP) + shard.shape[1:]
    outs = pl.pallas_call(
        body, name=name,
        in_specs=[_HBM, _HBM, pl.BlockSpec(memory_space=pl.ANY)],
        out_specs=[_SEM, _SEM, _HBM, _HBM, pl.BlockSpec(memory_space=pltpu.VMEM)],
        out_shape=[pltpu.SemaphoreType.DMA((4,))] * 2 + [pltpu.HBM(shard.shape, shard.dtype),
                                                         pltpu.HBM(land_shape, shard.dtype),
                                                         jax.ShapeDtypeStruct((8, 128), F32)],
        input_output_aliases={0: 2, 1: 3},
        compiler_params=pltpu.CompilerParams(has_side_effects=_DATAFLOW),
    )(pltpu.with_memory_space_constraint(shard, pltpu.HBM),
      pltpu.with_memory_space_constraint(lax.empty(land_shape, shard.dtype), pltpu.HBM), after)
    return outs[:4], outs[4]


def first_forward(handle, after, name):
    send, recv, shard, land = handle

    def body(shard_ref, land_ref, send_ref, recv_ref, after_ref, send2, recv2, shard_thru, land_thru):
        firsts = _first_copies(shard_ref, land_ref, send_ref, recv_ref)
        forwards = _forward_copies(land_ref, send2, recv2)
        for q in range(3):
            firsts[q][1].wait_recv()
            forwards[q][0].start()
        firsts[3][1].wait_recv()
        for mine, _ in firsts:
            mine.wait_send()

    outs = pl.pallas_call(
        body, name=name,
        in_specs=[_HBM, _HBM, _SEM, _SEM, pl.BlockSpec(memory_space=pl.ANY)],
        out_specs=[_SEM, _SEM, _HBM, _HBM],
        out_shape=[pltpu.SemaphoreType.DMA((3,))] * 2 + [pltpu.HBM(shard.shape, shard.dtype),
                                                         pltpu.HBM(land.shape, land.dtype)],
        input_output_aliases={0: 2, 1: 3},
        compiler_params=pltpu.CompilerParams(has_side_effects=_DATAFLOW),
    )(shard, land, send, recv, after)
    return outs[0], outs[1], outs[3]


def first_wait(handle, after, name):
    send, recv, land = handle

    def body(land_ref, send_ref, recv_ref, after_ref, land_out):
        for mine, theirs in _forward_copies(land_ref, send_ref, recv_ref):
            mine.wait_send()
            theirs.wait_recv()

    return pl.pallas_call(
        body, name=name,
        in_specs=[_HBM, _SEM, _SEM, pl.BlockSpec(memory_space=pl.ANY)],
        out_specs=[_HBM],
        out_shape=[pltpu.HBM(land.shape, land.dtype)],
        input_output_aliases={0: 0},
        compiler_params=pltpu.CompilerParams(has_side_effects=_DATAFLOW),
    )(land, send, recv, after)[0]


def _sibling_copies(gs, lands, send, recv):
    mx, my, mc = _my_place()
    return [pltpu.make_async_remote_copy(
        src_ref=gs[k].at[:, :, 1 - mc], dst_ref=lands[k], send_sem=send.at[k], recv_sem=recv.at[k],
        device_id=(mx, my, 1 - mc), device_id_type=MESH) for k in range(len(gs))]


def sibling_start(gs, after, name):
    K = len(gs)

    def body(*refs):
        ins, lands = refs[:K], refs[K:2 * K]
        send, recv = refs[2 * K + 1], refs[2 * K + 2]
        for cp in _sibling_copies(ins, lands, send, recv):
            cp.start()
        refs[-1][...] = jnp.zeros_like(refs[-1])

    land_shapes = [g.shape[:2] + g.shape[3:] for g in gs]
    outs = pl.pallas_call(
        body, name=name,
        in_specs=[_HBM] * (2 * K) + [pl.BlockSpec(memory_space=pl.ANY)],
        out_specs=[_SEM, _SEM] + [_HBM] * (2 * K) + [pl.BlockSpec(memory_space=pltpu.VMEM)],
        out_shape=([pltpu.SemaphoreType.DMA((K,))] * 2 + [pltpu.HBM(g.shape, g.dtype) for g in gs]
                   + [pltpu.HBM(ls, g.dtype) for ls, g in zip(land_shapes, gs)] + [jax.ShapeDtypeStruct((8, 128), F32)]),
        input_output_aliases={i: 2 + i for i in range(2 * K)},
        compiler_params=pltpu.CompilerParams(has_side_effects=_DATAFLOW),
    )(*[pltpu.with_memory_space_constraint(g, pltpu.HBM) for g in gs],
      *[pltpu.with_memory_space_constraint(lax.empty(ls, g.dtype), pltpu.HBM) for ls, g in zip(land_shapes, gs)],
      after)
    return (outs[0], outs[1], outs[2:2 + K], outs[2 + K:2 + 2 * K]), outs[-1]


def sibling_wait(handle, after, name):
    send, recv, gs, lands = handle
    K = len(gs)

    def body(*refs):
        ins, land_refs = refs[:K], refs[K:2 * K]
        for cp in _sibling_copies(ins, land_refs, refs[2 * K], refs[2 * K + 1]):
            cp.wait_send()
            cp.wait_recv()

    outs = pl.pallas_call(
        body, name=name,
        in_specs=[_HBM] * (2 * K) + [_SEM, _SEM, pl.BlockSpec(memory_space=pl.ANY)],
        out_specs=[_HBM] * (2 * K),
        out_shape=[pltpu.HBM(g.shape, g.dtype) for g in gs] + [pltpu.HBM(l.shape, l.dtype) for l in lands],
        input_output_aliases={i: i for i in range(2 * K)},
        compiler_params=pltpu.CompilerParams(has_side_effects=_DATAFLOW),
    )(*gs, *lands, send, recv, after)
    return outs[:K], outs[K:]


def _small_copies(x, land, send, recv):
    mx, my, mc = _my_place()
    me = 4 * mx + 2 * my + mc
    out = []
    for k in range(1, N_DEV):
        peer = (1 - mx if k & 4 else mx, 1 - my if k & 2 else my, 1 - mc if k & 1 else mc)
        slot = 4 * peer[0] + 2 * peer[1] + peer[2]
        out.append(tuple(pltpu.make_async_remote_copy(
            src_ref=x, dst_ref=land.at[s], send_sem=send.at[k - 1], recv_sem=recv.at[k - 1],
            device_id=peer, device_id_type=MESH) for s in (me, slot)))
    return out


def small_start(x, after, name):
    def body(x_ref, land_ref, after_ref, send, recv, x_thru, land_thru, token):
        for mine, _ in _small_copies(x_ref, land_ref, send, recv):
            mine.start()
        token[...] = jnp.zeros_like(token)

    land_shape = (N_DEV,) + x.shape
    outs = pl.pallas_call(
        body, name=name,
        in_specs=[_HBM, _HBM, pl.BlockSpec(memory_space=pl.ANY)],
        out_specs=[_SEM, _SEM, _HBM, _HBM, pl.BlockSpec(memory_space=pltpu.VMEM)],
        out_shape=[pltpu.SemaphoreType.DMA((N_DEV - 1,))] * 2 + [pltpu.HBM(x.shape, x.dtype), pltpu.HBM(land_shape, x.dtype),
                                                                 jax.ShapeDtypeStruct((8, 128), F32)],
        input_output_aliases={0: 2, 1: 3},
        compiler_params=pltpu.CompilerParams(has_side_effects=_DATAFLOW),
    )(pltpu.with_memory_space_constraint(x, pltpu.HBM),
      pltpu.with_memory_space_constraint(lax.empty(land_shape, x.dtype), pltpu.HBM), after)
    return outs[:4], outs[4]


def small_wait(handle, after, name):
    send, recv, x, land = handle

    def body(x_ref, land_ref, send_ref, recv_ref, after_ref, x_out, land_out):
        for mine, theirs in _small_copies(x_ref, land_ref, send_ref, recv_ref):
            mine.wait_send()
            theirs.wait_recv()

    return pl.pallas_call(
        body, name=name,
        in_specs=[_HBM, _HBM, _SEM, _SEM, pl.BlockSpec(memory_space=pl.ANY)],
        out_specs=[_HBM, _HBM],
        out_shape=[pltpu.HBM(x.shape, x.dtype), pltpu.HBM(land.shape, land.dtype)],
        input_output_aliases={0: 0, 1: 1},
        compiler_params=pltpu.CompilerParams(has_side_effects=_DATAFLOW),
    )(x, land, send, recv, after)


def _scatter_copies(ps, lands, send, recv):
    mx, my, mc = _my_place()
    cps = []
    for j, (cx, cy) in enumerate(_other_chips(mx, my)):
        for k in range(len(ps)):
            cps.append(pltpu.make_async_remote_copy(
                src_ref=ps[k].at[:, 2 * cx + cy], dst_ref=lands[k].at[j],
                send_sem=send.at[k * 3 + j], recv_sem=recv.at[k * 3 + j],
                device_id=(cx, cy, mc), device_id_type=MESH))
    return cps


def scatter_start(ps, after, name):
    K = len(ps)

    def body(*refs):
        ins, lands = refs[:K], refs[K:2 * K]
        send, recv = refs[2 * K + 1], refs[2 * K + 2]
        for cp in _scatter_copies(ins, lands, send, recv):
            cp.start()
        refs[-1][...] = jnp.zeros_like(refs[-1])

    land_shapes = [(N_CHIP - 1, p.shape[0]) + p.shape[2:] for p in ps]
    outs = pl.pallas_call(
        body, name=name,
        in_specs=[_HBM] * (2 * K) + [pl.BlockSpec(memory_space=pl.ANY)],
        out_specs=[_SEM, _SEM] + [_HBM] * (2 * K) + [pl.BlockSpec(memory_space=pltpu.VMEM)],
        out_shape=([pltpu.SemaphoreType.DMA((3 * K,))] * 2 + [pltpu.HBM(p.shape, p.dtype) for p in ps]
                   + [pltpu.HBM(ls, p.dtype) for ls, p in zip(land_shapes, ps)] + [jax.ShapeDtypeStruct((8, 128), F32)]),
        input_output_aliases={i: 2 + i for i in range(2 * K)},
        compiler_params=pltpu.CompilerParams(has_side_effects=_DATAFLOW),
    )(*[pltpu.with_memory_space_constraint(p, pltpu.HBM) for p in ps],
      *[pltpu.with_memory_space_constraint(lax.empty(ls, p.dtype), pltpu.HBM) for ls, p in zip(land_shapes, ps)],
      after)
    return (outs[0], outs[1], outs[2:2 + K], outs[2 + K:2 + 2 * K]), outs[-1]


def scatter_wait(handle, after, name):
    send, recv, ps, lands = handle
    K = len(ps)
    afters = list(after) if isinstance(after, (list, tuple)) else [after]

    def body(*refs):
        ins, land_refs = refs[:K], refs[K:2 * K]
        send_ref, recv_ref = refs[2 * K], refs[2 * K + 1]
        for cp in _scatter_copies(ins, land_refs, send_ref, recv_ref):
            cp.wait_send()
            cp.wait_recv()

    outs = pl.pallas_call(
        body, name=name,
        in_specs=[_HBM] * (2 * K) + [_SEM, _SEM] + [pl.BlockSpec(memory_space=pl.ANY)] * len(afters),
        out_specs=[_HBM] * (2 * K),
        out_shape=[pltpu.HBM(p.shape, p.dtype) for p in ps] + [pltpu.HBM(l.shape, l.dtype) for l in lands],
        input_output_aliases={i: i for i in range(2 * K)},
        compiler_params=pltpu.CompilerParams(has_side_effects=_DATAFLOW),
    )(*ps, *lands, send, recv, *afters)
    return outs[:K], outs[K:]


def sibling_complete(ss, name):
    K = len(ss)

    def body(*refs):
        ins, outs = refs[:K], refs[K:2 * K]
        send, recv = refs[2 * K:]
        mx, my, mc = _my_place()
        cps = []
        for k in range(K):
            cp = pltpu.make_async_remote_copy(
                src_ref=ins[k].at[:, mc], dst_ref=outs[k].at[:, mc], send_sem=send.at[k], recv_sem=recv.at[k],
                device_id=(mx, my, 1 - mc), device_id_type=MESH)
            cp.start()
            cps.append(cp)
        for k in range(K):
            pltpu.make_async_remote_copy(
                src_ref=ins[k].at[:, mc], dst_ref=outs[k].at[:, 1 - mc], send_sem=send.at[k], recv_sem=recv.at[k],
                device_id=(mx, my, 1 - mc), device_id_type=MESH).wait_recv()
        for cp in cps:
            cp.wait_send()

    hbm = pl.BlockSpec(memory_space=pl.ANY)
    return pl.pallas_call(
        body, name=name,
        in_specs=[hbm] * K, out_specs=[hbm] * K,
        out_shape=[jax.ShapeDtypeStruct(s.shape, s.dtype) for s in ss],
        scratch_shapes=[pltpu.SemaphoreType.DMA((K,)), pltpu.SemaphoreType.DMA((K,))],
        input_output_aliases={k: k for k in range(K)},
    )(*ss)


def _rope_tables(T):
    inv = ROPE_THETA ** (-jnp.arange(0, ATT_DH, 2, dtype=F32) / ATT_DH)
    ang = jnp.arange(T, dtype=F32)[:, None] * inv[None, :]
    ang = jnp.concatenate([ang, ang, ang, ang], axis=-1)
    return jnp.cos(ang), jnp.sin(ang)


def _ffn_fwd(h, y, mod, i0, get_up, get_down, norm_next, tag):
    wgu = get_up(y)
    a, b, s = ffn_up(y, (wgu, (0,)), (wgu, (1,)), f"ffn_up_{tag}")
    wd = get_down(s)
    outs = resid_matmul([s], (wd, (0,)), h, mod, i0 + 2, 0.5, f"ffn_down_{tag}", norm_next)
    hn, o = outs[0], outs[1]
    return hn, (outs[2] if norm_next else None), (h, y, a, b, s, o), ((wgu, (0,)), (wgu, (1,)), (wd, (0,)))


def _ffn_bwd(dh, do, res, ng, i_n, mod, i0, wgT, wuT, wd, on_grads, next_gate, after, tag):
    h, y, a, b, s, o = res
    F = _wrows(wgT)
    da, db = ffn_bwd_mid(do, wd, a, b, f"ffn_bwd_mid_{tag}", after)
    gbuf = lax.empty((3, F, h.shape[1]), BF16)
    gbuf = matmul_tn(da, y, gbuf, 0, 0, f"dwg_{tag}")
    gbuf = matmul_tn(db, y, gbuf, 1, 0, f"dwu_{tag}")
    gbuf = matmul_tn(s, do, gbuf, 2, 0, f"dwd_{tag}")
    token, then = on_grads([gbuf])
    outs = dy_normbwd([(da, 0, wgT, 0, F), (db, 0, wuT, 0, F)], h, dh, ng, i_n, mod, i0 + 1,
                      f"ffn_bwd_dy_{tag}", next_gate, [token])
    return outs, then


def _mixer_fwd(h, y, mod, w_inT, w_out, sgu, cos, sin, norm_next, tag):
    lng, lnb, sw, swt, bcol = sgu
    proj = matmul_nt(y, w_inT, f"proj_{tag}")
    out_a = sgu_fwd(proj, lng, lnb, sw, bcol, f"sgu_fwd_{tag}")
    qkv = rope_fwd(proj, cos, sin, f"rope_fwd_{tag}")
    npat = len(DILATIONS)
    qkv_res = [tuple(qkv[3 * p:3 * p + 3]) for p in range(npat)]
    os_, lses = [], []
    for d, (qd, kd, vd) in zip(DILATIONS, qkv_res):
        o_d, lse_d = attn_fwd(qd, kd, vd, f"attn_fwd_d{d}_{tag}")
        os_.append(o_d)
        lses.append(lse_d)
    comb = attn_combine(os_, lses, f"attn_combine_{tag}")
    out_b, o_res, lse_res = comb[0], comb[1:1 + npat], comb[1 + npat:]
    outs = resid_matmul([out_a, out_b], w_out, h, mod, 5, 1.0, f"mix_out_{tag}", norm_next)
    hn, om = outs[0], outs[1]
    return hn, (outs[2] if norm_next else None), (h, y, proj, out_a, out_b, o_res, lse_res, qkv_res, om)


def _mixer_bwd(dh, dom, res, ng, mod, w_inT, w_out, sgu, cos, sin, on_grads, next_gate, after, tag):
    lng, lnb, sw, swt, bcol = sgu
    h, y, proj, out_a, out_b, o_res, lse_res, qkv_res, om = res
    D = h.shape[1]
    dmixed = matmul_nt(dom, w_out, f"dmixed_{tag}", after)
    woutbuf = lax.empty((1, 2 * MIX_HALF, D), BF16)
    woutbuf = matmul_tn(out_a, dom, woutbuf, 0, 0, f"dwout_a_{tag}", tmo_cap=MIX_HALF)
    woutbuf = matmul_tn(out_b, dom, woutbuf, 0, MIX_HALF, f"dwout_b_{tag}", tmo_cap=MIX_HALF)
    d_uv, d_sw, d_svec = sgu_bwd(proj, dmixed, lng, lnb, sw, swt, bcol, f"sgu_bwd_{tag}")
    do_res = to_residues(dmixed, 1, f"dout_res_{tag}")
    dqs, dks, dvs = [], [], []
    for p, (d, (qd, kd, vd)) in enumerate(zip(DILATIONS, qkv_res)):
        dq, dk, dv = attn_bwd(qd, kd, vd, do_res[p], o_res[p], lse_res[p], f"attn_bwd_d{d}_{tag}")
        dqs.append(dq)
        dks.append(dk)
        dvs.append(dv)
    d_qkv = rope_bwd(dqs, dks, dvs, cos, sin, f"rope_bwd_{tag}")
    winbuf = lax.empty((1, 5 * MIX_HALF, D), BF16)
    winbuf = matmul_tn(d_uv, y, winbuf, 0, 0, f"dwin_uv_{tag}", tmo_cap=MIX_HALF)
    winbuf = matmul_tn(d_qkv, y, winbuf, 0, 2 * MIX_HALF, f"dwin_qkv_{tag}", tmo_cap=MIX_HALF)
    token, then = on_grads([winbuf, woutbuf])
    pairs = [(d_uv, 0, w_inT, 0, 2 * MIX_HALF), (d_qkv, 0, w_inT, 1, 2 * MIX_HALF), (d_qkv, 2, w_inT, 4, MIX_HALF)]
    outs = dy_normbwd(pairs, h, dh, ng, 1, mod, 4, f"mix_bwd_dy_{tag}", next_gate, [token])
    return outs, d_sw, d_svec, then


def _local_step(x, tgt, mods, ngs, get_w, sgus, gf, on_block_grads, on_layer_small):
    T, D = x.shape
    cos, sin = _rope_tables(T)
    h = x
    saved, weights = [], []
    for l in range(2):
        def getter(blk, l=l):
            return lambda after: get_w(l, blk, after)

        if l == 0:
            y = normmod_fwd(h, ngs[0], 0, mods[0], 0, 1, "normmod_l0f1")
        h, y, r1, wf1 = _ffn_fwd(h, y, mods[l], 0, getter("f1u"), getter("f1d"), (ngs[l], 1, mods[l], 3, 4), f"l{l}f1")
        w_inT, w_out = get_w(l, "mx", h)
        h, y, r2 = _mixer_fwd(h, y, mods[l], (w_inT, (0,)), (w_out, (0,)), sgus[l], cos, sin,
                              (ngs[l], 2, mods[l], 6, 7), f"l{l}mx")
        h, y, r3, wf2 = _ffn_fwd(h, y, mods[l], 6, getter("f2u"), getter("f2d"),
                                 (ngs[l + 1], 0, mods[l + 1], 0, 1) if l + 1 < 2 else None, f"l{l}f2")
        saved.append((r1, r2, r3))
        weights.append((wf1, w_inT, w_out, wf2))
    def gate_of(l, blk):
        r1, r2, r3 = saved[l]
        o, i_g, coef = {"f2": (r3[5], 8, 0.5), "mx": (r2[-1], 5, 1.0), "f1": (r1[5], 2, 0.5)}[blk]
        return o, mods[l], i_g, coef

    seq = [(l, blk) for l in (1, 0) for blk in ("f2", "mx", "f1")]
    dh, red_final, do, red_g = final_loss_bwd(h, gf, tgt, gate_of(*seq[0]), "final_loss_bwd")
    rn, rg = {}, {}
    after = []
    for idx, (l, blk) in enumerate(seq):
        r1, r2, r3 = saved[l]
        wf1, w_inT, w_out, wf2 = weights[l]
        nxt = gate_of(*seq[idx + 1]) if idx + 1 < len(seq) else None
        rg[blk] = red_g
        tag = f"l{l}{blk}"

        def on(arrays, l=l, blk=blk):
            return on_block_grads(l, blk, arrays)

        if blk == "f2":
            outs, then = _ffn_bwd(dh, do, r3, ngs[l], 2, mods[l], 6, *wf2, on, nxt, after, tag)
        elif blk == "mx":
            outs, d_sw, d_svec, then = _mixer_bwd(dh, do, r2, ngs[l], mods[l], (w_inT, (0,)), (w_out, (0,)), sgus[l],
                                                  cos, sin, on, nxt, after, tag)
        else:
            outs, then = _ffn_bwd(dh, do, r1, ngs[l], 0, mods[l], 0, *wf1, on, nxt, after, tag)
        dh, rn[blk] = outs[0], outs[1]
        if nxt is not None:
            do, red_g = outs[2], outs[3]
        if blk == "f1":
            small = on_layer_small(l, dict(sgu_w=d_sw, sgu_vec=d_svec, red_n=(rn["f1"], rn["mx"], rn["f2"]),
                                           red_g=(rg["f1"], rg["mx"], rg["f2"])), red_final if l == 0 else None)
            after = [small, then(small)]
        else:
            after = [then(dh)]
    return dh


def _adam_out(w, g, m, v, name):
    shp = w.shape
    two_d = (-1, shp[-1])
    d, mn, vn = adamw(w.reshape(two_d), g.reshape(two_d), m.reshape(two_d), v.reshape(two_d), name)
    return g, d.reshape(shp), mn.reshape(shp), vn.reshape(shp)


def kernel(x, c, ada_w, ada_b, norm_g, ffn1_wg, ffn1_wu, ffn1_wd, ffn2_wg, ffn2_wu, ffn2_wd, w_in, sgu_ln_g, sgu_ln_b, sgu_w, sgu_b, w_out, final_g, loss_target, m_ada_w, m_ada_b, m_norm_g, m_ffn1_wg, m_ffn1_wu, m_ffn1_wd, m_ffn2_wg, m_ffn2_wu, m_ffn2_wd, m_w_in, m_sgu_ln_g, m_sgu_ln_b, m_sgu_w, m_sgu_b, m_w_out, m_final_g, v_ada_w, v_ada_b, v_norm_g, v_ffn1_wg, v_ffn1_wu, v_ffn1_wd, v_ffn2_wg, v_ffn2_wu, v_ffn2_wd, v_w_in, v_sgu_ln_g, v_sgu_ln_b, v_sgu_w, v_sgu_b, v_w_out, v_final_g):
    T, D = x.shape[1], x.shape[2]
    NL = ada_w.shape[0]
    mx, my, mc = _my_place()
    me = 4 * mx + 2 * my + mc
    ci = 2 * mx + my
    c_idx = jnp.reshape(mc, (1,)).astype(jnp.int32)
    place = jnp.stack([ci, mc]).astype(jnp.int32)

    ngw = norm_g.shape[2]
    small_in = jnp.concatenate([jnp.pad(c, ((0, 7), (0, 0))),
                                jnp.pad(norm_g.reshape(NL * 3, ngw), ((0, 8 - NL * 3), (0, D - ngw)))], axis=0)
    small_all, _ = gather_small(small_in, place, "gather_c_normg")
    c_all = small_all[:, 0, :]
    ng_parts = small_all[0::2, 8:8 + NL * 3, :ngw]
    ngs = jnp.transpose(ng_parts, (1, 0, 2)).reshape(NL, 3, N_CHIP * ngw)

    nmod = ada_w.shape[2]
    ada_b_mine = lax.dynamic_slice_in_dim(ada_b, ci * nmod, nmod, axis=1).reshape(NL, 1, nmod)
    mod_part = ada_fwd(c_all, ada_w, ada_b_mine, "ada_fwd")
    mod_all, _ = gather_small(mod_part.reshape(NL * N_DEV, nmod), place, "gather_mod")
    mod_rows = lax.dynamic_index_in_dim(mod_all.reshape(N_DEV, NL, N_DEV, nmod), me, axis=2, keepdims=False)
    mods = jnp.transpose(mod_rows[0::2], (1, 0, 2)).reshape(NL, N_ADA, D)

    sgus = []
    for l in range(NL):
        sgus.append((sgu_ln_g[l].reshape(1, MIX_HALF), sgu_ln_b[l].reshape(1, MIX_HALF), sgu_w[l],
                     jnp.swapaxes(sgu_w[l], 1, 2), jnp.transpose(sgu_b[l])))

    def halves(a):
        n, r, _ = a.shape
        return a.reshape(n, 2, r // 2, D)

    first_group = halves(jnp.stack([ffn1_wg[0].T, ffn1_wu[0].T], axis=0).astype(BF16))
    first_handle, first_token = first_start(first_group, mods, "first_start")
    zero = first_token[0, 0]
    mods = mods + zero

    def prep(a):
        return (a + zero).astype(BF16)

    groups = []
    for l in range(NL):
        groups += [[halves(jnp.stack([prep(ffn1_wg[l].T), prep(ffn1_wu[l].T)], axis=0))],
                   [halves(prep(ffn1_wd[l])[None])],
                   [halves(prep(w_in[l].T)[None]), halves(prep(w_out[l])[None])],
                   [halves(jnp.stack([prep(ffn2_wg[l].T), prep(ffn2_wu[l].T)], axis=0))],
                   [halves(prep(ffn2_wd[l])[None])]]
    handles, token = gather_start(groups[1:], mods, "gather_start")
    handles = [None] + handles
    mods = mods + token[0, 0]
    group_no = {"f1u": 0, "f1d": 1, "mx": 2, "f2u": 3, "f2d": 4}

    def get_w(l, key, after):
        g = len(group_no) * l + group_no[key]
        if g == 0:
            full = [first_wait(first_forward(first_handle, after, "first_forward"), place, "first_wait")]
        else:
            full = gather_wait(handles[g], after, f"gather_wait_l{l}{key}")
        full = [a.reshape(a.shape[0], N_CHIP * 2 * a.shape[3], D) for a in full]
        return full[0] if key != "mx" else tuple(full)

    def split(a):
        n, r4, _ = a.shape
        return a.reshape(n, N_CHIP, 2, r4 // N_CHIP // 2, D)

    pending, small_pending, small_tokens = {}, {}, {}

    def on_block_grads(l, blk, bufs):
        tag = f"l{l}{blk}"
        sib, tok1 = sibling_start([split(g) for g in bufs], place, f"rs_sibling_start_{tag}")

        def then(after):
            parts, lands = sibling_wait(sib, after, f"rs_sibling_wait_{tag}")
            psums = [sum_halves(g, ld, c_idx, f"rs_sum_halves_{tag}_{i}") for i, (g, ld) in enumerate(zip(parts, lands))]
            pending[(l, blk)], tok2 = scatter_start(psums, lands[0], f"rs_chips_start_{tag}")
            return tok2

        return tok1, then

    def blocks_finish(blocks, after, tag):
        ssums, counts = [], []
        for l, blk in blocks:
            psums, lands2 = scatter_wait(pending.pop((l, blk)), after, f"rs_chips_wait_l{l}{blk}")
            ssums += [sum_chips(p, ld, place, f"rs_sum_chips_l{l}{blk}_{i}") for i, (p, ld) in enumerate(zip(psums, lands2))]
            counts.append(len(psums))
        fins = [f.reshape(f.shape[0], -1, D) for f in sibling_complete(ssums, f"rs_complete_{tag}")]
        out, i = [], 0
        for n in counts:
            out.append(fins[i:i + n])
            i += n
        return out

    def on_layer_small(l, grads, red_final):
        blocks = list(grads["red_n"]) + list(grads["red_g"])
        blocks.append(jnp.pad(grads["sgu_vec"], ((0, 0), (0, D - MIX_HALF))))
        blocks.append(grads["sgu_w"].reshape(-1, D))
        if red_final is not None:
            blocks.append(red_final)
        xs = jnp.concatenate(blocks, axis=0)
        small_pending[l], small_tokens[l] = small_start(xs, place, f"small_start_l{l}")
        return small_tokens[l]

    grad_x = _local_step(x[0], loss_target[0], mods, ngs, get_w, sgus, final_g.reshape(1, D),
                         on_block_grads, on_layer_small)

    adam_state = {}

    def adam_big(nm, l, g, w, m, v):
        adam_state[nm] = adamw_layer(w, g, m, v, l, adam_state.get(nm), f"adamw_{nm}_l{l}")

    def adam_block(l, blk, fin):
        if blk == "mx":
            adam_big("w_in", l, fin[0][0].T, w_in, m_w_in, v_w_in)
            adam_big("w_out", l, fin[1][0], w_out, m_w_out, v_w_out)
        else:
            ws = ((ffn1_wg, m_ffn1_wg, v_ffn1_wg), (ffn1_wu, m_ffn1_wu, v_ffn1_wu), (ffn1_wd, m_ffn1_wd, v_ffn1_wd)) \
                if blk == "f1" else \
                ((ffn2_wg, m_ffn2_wg, v_ffn2_wg), (ffn2_wu, m_ffn2_wu, v_ffn2_wu), (ffn2_wd, m_ffn2_wd, v_ffn2_wd))
            pre = "ffn1" if blk == "f1" else "ffn2"
            for k, (nm, tr) in enumerate((("wg", True), ("wu", True), ("wd", False))):
                adam_big(f"{pre}_{nm}", l, fin[0][k], *[jnp.swapaxes(t, 1, 2) if tr else t for t in ws[k]])

    done_order = [(l, blk) for l in range(NL - 1, -1, -1) for blk in ("f2", "mx", "f1")]
    for (l, blk), fin in zip(done_order[:-1], blocks_finish(done_order[:-1], small_tokens[0], "early")):
        adam_block(l, blk, fin)
    last_big = adam_state["w_out"][1]

    small_sum, small_all = [], []
    for l in range(NL):
        xs, land = small_wait(small_pending[l], last_big, f"small_wait_l{l}")
        full = lax.dynamic_update_slice(land, xs[None], (me, 0, 0))
        small_all.append(full)
        small_sum.append(sum_slots(full, f"small_sum_l{l}"))
    offs = [8 * i for i in range(8)]
    off_final = offs[7] + SGU_HEADS * ATT_BLOCK * HEAD_LANES // D
    loss = small_sum[0][off_final + 1, 0]
    g_final_g = small_sum[0][off_final, :]
    g_norm_g, g_ada_b, g_lng, g_lnb, g_sb, g_sw, dmod_all = [], [], [], [], [], [], []
    for l in range(NL):
        rn = [small_sum[l][offs[i]:offs[i] + 8] for i in range(3)]
        rg = [small_sum[l][offs[3 + i]:offs[3 + i] + 8] for i in range(3)]
        g_norm_g.append(jnp.stack([rn[i][2] for i in range(3)], axis=0))
        g_ada_b.append(jnp.concatenate([jnp.stack([rn[i][0], rn[i][1], rg[i][0]], axis=0) for i in range(3)],
                                       axis=0).reshape(N_ADA * D))
        sv = small_sum[l][offs[6]:offs[6] + 8, :MIX_HALF]
        g_lng.append(sv[0].reshape(SGU_HEADS, HEAD_LANES))
        g_lnb.append(sv[1].reshape(SGU_HEADS, HEAD_LANES))
        g_sb.append(sv[2].reshape(SGU_HEADS, ATT_BLOCK))
        g_sw.append(small_sum[l][offs[7]:off_final].reshape(sgu_w.shape[1:]))
        rows = []
        for i in range(3):
            an = small_all[l][:, offs[i]:offs[i] + 2]
            ag = small_all[l][:, offs[3 + i]:offs[3 + i] + 1]
            rows += [an[:, 0], an[:, 1], ag[:, 0]]
        dmod_all.append(jnp.stack(rows, axis=1).reshape(N_DEV, N_ADA * D))
    dmod_all = jnp.stack(dmod_all, axis=0)
    dmod_mine = lax.dynamic_slice_in_dim(dmod_all, ci * nmod, nmod, axis=2)
    g_ada_w = ada_bwd(jnp.transpose(c_all), dmod_mine, "ada_bwd")
    g_ada_b = jnp.stack(g_ada_b, axis=0)
    g_norm_g_full = jnp.stack(g_norm_g, axis=0)
    g_norm_g_mine = lax.dynamic_slice_in_dim(g_norm_g_full, ci * ngw, ngw, axis=2)

    small_params = [
        ("ada_w", ada_w, g_ada_w, m_ada_w, v_ada_w),
        ("ada_b", ada_b, g_ada_b, m_ada_b, v_ada_b),
        ("norm_g", norm_g, g_norm_g_mine, m_norm_g, v_norm_g),
        ("sgu_ln_g", sgu_ln_g, jnp.stack(g_lng, axis=0), m_sgu_ln_g, v_sgu_ln_g),
        ("sgu_ln_b", sgu_ln_b, jnp.stack(g_lnb, axis=0), m_sgu_ln_b, v_sgu_ln_b),
        ("sgu_w", sgu_w, jnp.stack(g_sw, axis=0), m_sgu_w, v_sgu_w),
        ("sgu_b", sgu_b, jnp.stack(g_sb, axis=0), m_sgu_b, v_sgu_b),
        ("final_g", final_g.reshape(1, D), g_final_g.reshape(1, D), m_final_g.reshape(1, D), v_final_g.reshape(1, D)),
    ]
    for nm, w, g, m, v in small_params:
        res = _adam_out(w, g, m, v, f"adamw_{nm}")
        adam_state[nm] = tuple(t.reshape(D) for t in res) if nm == "final_g" else res

    l, blk = done_order[-1]
    adam_block(l, blk, blocks_finish([(l, blk)], [st[1] for st in adam_state.values()], "last")[0])

    names = ["ada_w", "ada_b", "norm_g", "ffn1_wg", "ffn1_wu", "ffn1_wd", "ffn2_wg", "ffn2_wu", "ffn2_wd", "w_in",
             "sgu_ln_g", "sgu_ln_b", "sgu_w", "sgu_b", "w_out", "final_g"]
    shapes = [t.shape for t in (ada_w, ada_b, norm_g, ffn1_wg, ffn1_wu, ffn1_wd, ffn2_wg, ffn2_wu, ffn2_wd, w_in,
                                sgu_ln_g, sgu_ln_b, sgu_w, sgu_b, w_out, final_g)]
    def shaped(nm, t, s):
        if nm in ("ffn1_wg", "ffn1_wu", "ffn2_wg", "ffn2_wu"):
            return jnp.swapaxes(t.reshape(s[0], s[2], s[1]), 1, 2)
        return t.reshape(s)

    return (loss, grad_x[None], *[shaped(nm, adam_state[nm][i], s) for i in range(4) for nm, s in zip(names, shapes)])
```

```python
import math

import jax
import jax.numpy as jnp
from jax import lax
from jax.experimental import pallas as pl
from jax.experimental.pallas import tpu as pltpu

F32 = jnp.float32
BF16 = jnp.bfloat16
EPS = 1e-6
SGU_HEADS = 4
HEAD_LANES = 128
ATT_DH = 64
ATT_BLOCK = 128
MIX_HALF = SGU_HEADS * HEAD_LANES
DILATIONS = (1, 4, 16)
ROPE_THETA = 10000.0
N_ADA = 9
ADAM_LR, ADAM_B1, ADAM_B2, ADAM_EPS, ADAM_WD, ADAM_STEP = 0.001, 0.9, 0.999, 1e-08, 0.01, 10
NEG = -1e30
V7X_VMEM_BYTES = 64 * 1024 * 1024
VMEM_LIMIT = V7X_VMEM_BYTES * 7 // 8
MESH = pl.DeviceIdType.MESH
N_DEV = 8
N_CHIP = 4
_ANY = pl.BlockSpec(memory_space=pl.ANY)


def _tile(n, cap, mult):
    if n <= cap:
        return n
    t = (cap // mult) * mult
    while t >= mult:
        if n % t == 0:
            return t
        t -= mult
    raise ValueError((n, cap, mult))


def _params(dims=None):
    return pltpu.CompilerParams(dimension_semantics=dims, vmem_limit_bytes=VMEM_LIMIT)


def _wspec(w, rows, idx, resident=False):
    arr, lead = w
    kw = dict(pipeline_mode=pl.Buffered(1)) if resident else {}
    return pl.BlockSpec((None,) * len(lead) + (rows, arr.shape[-1]), lambda *g: tuple(lead) + (idx(*g), 0), **kw)


def _wrows(w):
    return w[0].shape[-2]


def _nt(a, b):
    return lax.dot_general(a, b, (((1,), (1,)), ((), ())), preferred_element_type=F32)


def _tn(a, b):
    return lax.dot_general(a, b, (((0,), (0,)), ((), ())), preferred_element_type=F32)


def _nn(a, b):
    return jnp.dot(a, b, preferred_element_type=F32)


def _sigmoid(x):
    return 0.5 * jnp.tanh(0.5 * x) + 0.5


_GELU_K = math.sqrt(2.0 / math.pi)
_GELU_C = 0.044715


def _gelu(x):
    t = jnp.tanh(_GELU_K * (x + _GELU_C * x * x * x))
    return 0.5 * x * (1.0 + t)


def _gelu_and_grad(x):
    x2 = x * x
    t = jnp.tanh(_GELU_K * (x + _GELU_C * x * x2))
    g = 0.5 * x * (1.0 + t)
    dg = 0.5 * (1.0 + t) + 0.5 * x * (1.0 - t * t) * (_GELU_K * (1.0 + 3.0 * _GELU_C * x2))
    return g, dg


def normmod_fwd(h, ng, i_n, mod, i_sh, i_sc, name):
    T, D = h.shape
    tm = _tile(T, 512, 8)

    def body(h_ref, ng_ref, mod_ref, y_ref):
        y_ref[...] = _normmod(h_ref[...], ng_ref[i_n:i_n + 1, :], mod_ref[i_sh:i_sh + 1, :],
                              mod_ref[i_sc:i_sc + 1, :]).astype(BF16)

    return pl.pallas_call(
        body, name=name, grid=(T // tm,),
        in_specs=[pl.BlockSpec((tm, D), lambda i: (i, 0)),
                  pl.BlockSpec(ng.shape, lambda i: (0, 0)),
                  pl.BlockSpec(mod.shape, lambda i: (0, 0))],
        out_specs=pl.BlockSpec((tm, D), lambda i: (i, 0)),
        out_shape=jax.ShapeDtypeStruct((T, D), BF16),
        compiler_params=_params(("parallel",)),
    )(h, ng, mod)


def ffn_up(y, wgT, wuT, name):
    T, D = y.shape
    F = _wrows(wgT)
    tm = _tile(T, 512, 16)
    tf = _tile(F, 2816, 256)
    cuts = list(range(0, tf, 768)) + [tf]

    def body(y_ref, wg_ref, wu_ref, p_ref, q_ref, s_ref):
        yv = y_ref[...]
        for c0, c1 in zip(cuts[:-1], cuts[1:]):
            a = _nt(yv, wg_ref[c0:c1, :])
            b = _nt(yv, wu_ref[c0:c1, :])
            sig = _sigmoid(a)
            q = a * sig
            p_ref[:, c0:c1] = (b * (sig + q * (1.0 - sig))).astype(BF16)
            q_ref[:, c0:c1] = q.astype(BF16)
            s_ref[:, c0:c1] = (q * b).astype(BF16)

    act = jax.ShapeDtypeStruct((T, F), BF16)
    return pl.pallas_call(
        body, name=name, grid=(F // tf, T // tm),
        in_specs=[pl.BlockSpec((tm, D), lambda j, i: (i, 0)),
                  _wspec(wgT, tf, lambda j, i: j, resident=True),
                  _wspec(wuT, tf, lambda j, i: j, resident=True)],
        out_specs=[pl.BlockSpec((tm, tf), lambda j, i: (i, j))] * 3,
        out_shape=[act, act, act],
        compiler_params=_params(("parallel", "parallel")),
    )(y, wgT[0], wuT[0])


def _normmod(x, gn, sh, sc):
    r = lax.rsqrt(jnp.mean(x * x, axis=-1, keepdims=True) + EPS)
    return ((x * r) * gn) * (1.0 + sc) + sh


def resid_matmul(xs, w, h, mod, i_g, coef, name, norm_next=None):
    T, D = h.shape
    kb = xs[0].shape[1]
    assert all(x.shape == (T, kb) for x in xs) and _wrows(w) == kb * len(xs)
    tm = _tile(T, 1024, 16)
    nx = len(xs)
    n_in, n_out, n_shape, n_ops = [], [], [], []
    if norm_next:
        ng_n, i_n, mod_n, i_sh, i_sc = norm_next
        n_in = [pl.BlockSpec(ng_n.shape, lambda i: (0, 0)), pl.BlockSpec(mod_n.shape, lambda i: (0, 0))]
        n_out = [pl.BlockSpec((tm, D), lambda i: (i, 0))]
        n_shape = [jax.ShapeDtypeStruct((T, D), BF16)]
        n_ops = [ng_n, mod_n]

    def body(*refs):
        x_refs, w_refs = refs[:nx], refs[nx:2 * nx]
        h_ref, mod_ref = refs[2 * nx:2 * nx + 2]
        hn_ref, o_ref = refs[2 * nx + 2 + len(n_in):2 * nx + 4 + len(n_in)]
        o = _nn(x_refs[0][...], w_refs[0][...])
        for xr, wr in zip(x_refs[1:], w_refs[1:]):
            o = o + _nn(xr[...], wr[...])
        o_ref[...] = o.astype(BF16)
        hn = h_ref[...] + (coef * mod_ref[i_g:i_g + 1, :]) * o
        hn_ref[...] = hn
        if norm_next:
            ng_ref, modn_ref = refs[2 * nx + 2], refs[2 * nx + 3]
            refs[-1][...] = _normmod(hn, ng_ref[i_n:i_n + 1, :], modn_ref[i_sh:i_sh + 1, :],
                                     modn_ref[i_sc:i_sc + 1, :]).astype(BF16)

    return pl.pallas_call(
        body, name=name, grid=(T // tm,),
        in_specs=([pl.BlockSpec((tm, kb), lambda i: (i, 0))] * nx
                  + [_wspec(w, kb, lambda i, p=p: p, resident=True) for p in range(nx)]
                  + [pl.BlockSpec((tm, D), lambda i: (i, 0)),
                     pl.BlockSpec(mod.shape, lambda i: (0, 0))] + n_in),
        out_specs=[pl.BlockSpec((tm, D), lambda i: (i, 0))] * 2 + n_out,
        out_shape=[jax.ShapeDtypeStruct((T, D), F32), jax.ShapeDtypeStruct((T, D), BF16)] + n_shape,
        compiler_params=_params(("parallel",)),
    )(*xs, *([w[0]] * nx), h, mod, *n_ops)


def _gate_specs(gate, tm, D):
    o, mod, _, _ = gate
    T = o.shape[0]
    return ([pl.BlockSpec((tm, D), lambda i: (i, 0)), pl.BlockSpec(mod.shape, lambda i: (0, 0))],
            [pl.BlockSpec((tm, D), lambda i: (i, 0)), pl.BlockSpec((8, D), lambda i: (0, 0))],
            [jax.ShapeDtypeStruct((T, D), BF16), jax.ShapeDtypeStruct((8, D), F32)],
            [o, mod])


def _gate_emit(d, gate, o_ref, mod_ref, do_ref, red_ref):
    _, _, i_g, coef = gate
    do_ref[...] = (d * (coef * mod_ref[i_g:i_g + 1, :])).astype(BF16)

    @pl.when(pl.program_id(0) == 0)
    def _():
        red_ref[...] = jnp.zeros_like(red_ref)

    red_ref[0:1, :] += coef * jnp.sum(d * o_ref[...].astype(F32), axis=0, keepdims=True)


def ffn_bwd_mid(do, wd, p, q, name, after=()):
    T, D = do.shape
    F = _wrows(wd)
    tm = _tile(T, 512, 16)
    tf = _tile(F, 2816, 256)
    cuts = list(range(0, tf, 256)) + [tf]

    def body(do_ref, wd_ref, p_ref, q_ref, *rest):
        da_ref, db_ref = rest[-2:]
        dov = do_ref[...]
        for c0, c1 in zip(cuts[:-1], cuts[1:]):
            ds = _nt(dov, wd_ref[c0:c1, :])
            da_ref[:, c0:c1] = (ds * p_ref[:, c0:c1].astype(F32)).astype(BF16)
            db_ref[:, c0:c1] = (ds * q_ref[:, c0:c1].astype(F32)).astype(BF16)

    act = jax.ShapeDtypeStruct((T, F), BF16)
    return pl.pallas_call(
        body, name=name, grid=(F // tf, T // tm),
        in_specs=[pl.BlockSpec((tm, D), lambda j, i: (i, 0)),
                  _wspec(wd, tf, lambda j, i: j, resident=True),
                  pl.BlockSpec((tm, tf), lambda j, i: (i, j)),
                  pl.BlockSpec((tm, tf), lambda j, i: (i, j))] + [_ANY] * len(after),
        out_specs=[pl.BlockSpec((tm, tf), lambda j, i: (i, j))] * 2,
        out_shape=[act, act],
        compiler_params=_params(("parallel", "parallel")),
    )(do, wd[0], p, q, *after)


def dy_normbwd(pairs, h, dhp, ng, i_n, mod, i_sc, name, gate=None, after=()):
    T, D = h.shape
    tm = _tile(T, 512, 16)
    npair = len(pairs)
    g_in, g_out, g_shape, g_ops = _gate_specs(gate, tm, D) if gate else ([], [], [], [])
    n_in = 2 * npair + 4 + len(g_in) + len(after)

    def body(*refs):
        x_refs, w_refs = refs[:npair], refs[npair:2 * npair]
        h_ref, dhp_ref, ng_ref, mod_ref = refs[2 * npair:2 * npair + 4]
        dh_ref, red_ref = refs[n_in:n_in + 2]
        dy = _nn(x_refs[0][...], w_refs[0][...])
        for xr, wr in zip(x_refs[1:], w_refs[1:]):
            dy = dy + _nn(xr[...], wr[...])
        x = h_ref[...]
        r = lax.rsqrt(jnp.mean(x * x, axis=-1, keepdims=True) + EPS)
        n = x * r
        gn = ng_ref[i_n:i_n + 1, :]
        dnh = dy * (1.0 + mod_ref[i_sc:i_sc + 1, :])

        @pl.when(pl.program_id(0) == 0)
        def _():
            red_ref[...] = jnp.zeros_like(red_ref)

        red_ref[0:1, :] += jnp.sum(dy, axis=0, keepdims=True)
        red_ref[1:2, :] += jnp.sum(dy * (n * gn), axis=0, keepdims=True)
        red_ref[2:3, :] += jnp.sum(dnh * n, axis=0, keepdims=True)
        dn = dnh * gn
        dh_new = dhp_ref[...] + r * (dn - n * jnp.mean(dn * n, axis=-1, keepdims=True))
        dh_ref[...] = dh_new
        if gate:
            _gate_emit(dh_new, gate, refs[2 * npair + 4], refs[2 * npair + 5], refs[-2], refs[-1])

    in_specs = ([pl.BlockSpec((tm, kb), lambda i, c=c: (i, c)) for (_, c, _, _, kb) in pairs]
                + [_wspec(w, kb, lambda i, r=r: r, resident=True) for (_, _, w, r, kb) in pairs]
                + [pl.BlockSpec((tm, D), lambda i: (i, 0)),
                   pl.BlockSpec((tm, D), lambda i: (i, 0)),
                   pl.BlockSpec(ng.shape, lambda i: (0, 0)),
                   pl.BlockSpec(mod.shape, lambda i: (0, 0))] + g_in + [_ANY] * len(after))
    return pl.pallas_call(
        body, name=name, grid=(T // tm,), in_specs=in_specs,
        out_specs=[pl.BlockSpec((tm, D), lambda i: (i, 0)), pl.BlockSpec((8, D), lambda i: (0, 0))] + g_out,
        out_shape=[jax.ShapeDtypeStruct((T, D), F32), jax.ShapeDtypeStruct((8, D), F32)] + g_shape,
        compiler_params=_params(("arbitrary",)),
    )(*[p[0] for p in pairs], *[p[2][0] for p in pairs], h, dhp, ng, mod, *g_ops, *after)


def matmul_tn(a, b, buf, slot, row0, name, tmo_cap=1408):
    T, N = b.shape
    ma = a.shape[1]
    tmo = _tile(ma, tmo_cap, 128)
    assert row0 % tmo == 0
    nmo = ma // tmo
    tk = _tile(T, 2048, 16)
    nk = T // tk

    def body(a_ref, b_ref, buf_ref, o_ref, acc_ref):
        k = pl.program_id(1)

        @pl.when(k == 0)
        def _():
            acc_ref[...] = jnp.zeros_like(acc_ref)

        acc_ref[...] += _tn(a_ref[...], b_ref[...])

        @pl.when(k == nk - 1)
        def _():
            o_ref[...] = acc_ref[...].astype(BF16)

    return pl.pallas_call(
        body, name=name, grid=(nmo, nk),
        in_specs=[pl.BlockSpec((tk, tmo), lambda j, k: (k, j)),
                  pl.BlockSpec((tk, N), lambda j, k: (k, 0)),
                  pl.BlockSpec(memory_space=pl.ANY)],
        out_specs=pl.BlockSpec((None, tmo, N), lambda j, k: (slot, row0 // tmo + j, 0)),
        out_shape=jax.ShapeDtypeStruct(buf.shape, BF16),
        scratch_shapes=[pltpu.VMEM((tmo, N), F32)],
        input_output_aliases={2: 0},
        compiler_params=_params(("parallel", "arbitrary")),
    )(a, b, buf)


def matmul_nt(x, w, name, after=()):
    T, K = x.shape
    N = _wrows(w)
    tm = _tile(T, 1024, 16)
    tn = _tile(N, 1280, 128)

    def body(x_ref, w_ref, *rest):
        rest[-1][...] = _nt(x_ref[...], w_ref[...]).astype(BF16)

    return pl.pallas_call(
        body, name=name, grid=(N // tn, T // tm),
        in_specs=[pl.BlockSpec((tm, K), lambda j, i: (i, 0)), _wspec(w, tn, lambda j, i: j)] + [_ANY] * len(after),
        out_specs=pl.BlockSpec((tm, tn), lambda j, i: (i, j)),
        out_shape=jax.ShapeDtypeStruct((T, N), BF16),
        compiler_params=_params(("parallel", "parallel")),
    )(x, w[0], *after)


def _sgu_head_fwd(u, v, lng, lnb):
    gu, dgu = _gelu_and_grad(u)
    gv, dgv = _gelu_and_grad(v)
    mu = jnp.mean(gv, axis=-1, keepdims=True)
    xc = gv - mu
    rstd = lax.rsqrt(jnp.mean(xc * xc, axis=-1, keepdims=True) + EPS)
    xhat = xc * rstd
    vn = xhat * lng + lnb
    return gu, dgu, dgv, rstd, xhat, vn


def _tril_mask():
    r = lax.broadcasted_iota(jnp.int32, (ATT_BLOCK, ATT_BLOCK), 0)
    c = lax.broadcasted_iota(jnp.int32, (ATT_BLOCK, ATT_BLOCK), 1)
    return c <= r


def _triu_mask():
    r = lax.broadcasted_iota(jnp.int32, (ATT_BLOCK, ATT_BLOCK), 0)
    c = lax.broadcasted_iota(jnp.int32, (ATT_BLOCK, ATT_BLOCK), 1)
    return r <= c


def sgu_fwd(proj, lng, lnb, w, bcol, name):
    T = proj.shape[0]
    tm = _tile(T, 512, 128)
    nch = tm // ATT_BLOCK

    def body(u_ref, v_ref, lng_ref, lnb_ref, w_ref, b_ref, o_ref):
        tril = _tril_mask()
        for hd in range(SGU_HEADS):
            sl = slice(hd * HEAD_LANES, (hd + 1) * HEAD_LANES)
            u = u_ref[:, sl].astype(F32)
            v = v_ref[:, sl].astype(F32)
            gu, _, _, _, _, vn = _sgu_head_fwd(u, v, lng_ref[:, sl], lnb_ref[:, sl])
            wm = jnp.where(tril, w_ref[hd], 0.0).astype(BF16)
            vnb = vn.astype(BF16)
            bc = b_ref[:, hd:hd + 1]
            for ch in range(nch):
                rs = slice(ch * ATT_BLOCK, (ch + 1) * ATT_BLOCK)
                z = _nn(wm, vnb[rs, :]) + bc
                o_ref[rs, sl] = (gu[rs, :] * z).astype(BF16)

    return pl.pallas_call(
        body, name=name, grid=(T // tm,),
        in_specs=[pl.BlockSpec((tm, MIX_HALF), lambda i: (i, 0)),
                  pl.BlockSpec((tm, MIX_HALF), lambda i: (i, 1)),
                  pl.BlockSpec((1, MIX_HALF), lambda i: (0, 0)),
                  pl.BlockSpec((1, MIX_HALF), lambda i: (0, 0)),
                  pl.BlockSpec(w.shape, lambda i: (0, 0, 0)),
                  pl.BlockSpec(bcol.shape, lambda i: (0, 0))],
        out_specs=pl.BlockSpec((tm, MIX_HALF), lambda i: (i, 0)),
        out_shape=jax.ShapeDtypeStruct((T, MIX_HALF), BF16),
        compiler_params=_params(("parallel",)),
    )(proj, proj, lng, lnb, w, bcol)


def sgu_bwd(proj, dmixed, lng, lnb, w, wt, bcol, name):
    T = proj.shape[0]
    tm = _tile(T, 512, 128)
    nch = tm // ATT_BLOCK
    nsteps = T // tm

    def body(u_ref, v_ref, g_ref, lng_ref, lnb_ref, w_ref, wt_ref, b_ref, duv_ref, dw_ref, dvec_ref, bacc_ref):
        step = pl.program_id(0)

        @pl.when(step == 0)
        def _():
            dw_ref[...] = jnp.zeros_like(dw_ref)
            dvec_ref[...] = jnp.zeros_like(dvec_ref)
            bacc_ref[...] = jnp.zeros_like(bacc_ref)

        tril = _tril_mask()
        triu = _triu_mask()
        for hd in range(SGU_HEADS):
            sl = slice(hd * HEAD_LANES, (hd + 1) * HEAD_LANES)
            u = u_ref[:, sl].astype(F32)
            v = v_ref[:, sl].astype(F32)
            lng_h = lng_ref[:, sl]
            gu, dgu, dgv, rstd, xhat, vn = _sgu_head_fwd(u, v, lng_h, lnb_ref[:, sl])
            wm = jnp.where(tril, w_ref[hd], 0.0).astype(BF16)
            wmt = jnp.where(triu, wt_ref[hd], 0.0).astype(BF16)
            vnb = vn.astype(BF16)
            bc = b_ref[:, hd:hd + 1]
            g = g_ref[:, sl].astype(F32)
            dw_acc = jnp.zeros((ATT_BLOCK, ATT_BLOCK), F32)
            b_acc = jnp.zeros((ATT_BLOCK, HEAD_LANES), F32)
            dvn_parts = []
            for ch in range(nch):
                rs = slice(ch * ATT_BLOCK, (ch + 1) * ATT_BLOCK)
                z = _nn(wm, vnb[rs, :]) + bc
                duv_ref[rs, sl] = (g[rs, :] * z * dgu[rs, :]).astype(BF16)
                dz = g[rs, :] * gu[rs, :]
                dzb = dz.astype(BF16)
                dvn_parts.append(_nn(wmt, dzb))
                dw_acc = dw_acc + _nt(dzb, vnb[rs, :])
                b_acc = b_acc + dz
            dvn = jnp.concatenate(dvn_parts, axis=0)
            dw_ref[hd] += jnp.where(tril, dw_acc, 0.0)
            bacc_ref[hd] += b_acc
            dvec_ref[0:1, sl] += jnp.sum(dvn * xhat, axis=0, keepdims=True)
            dvec_ref[1:2, sl] += jnp.sum(dvn, axis=0, keepdims=True)
            dxh = dvn * lng_h
            dgv_in = rstd * (dxh - jnp.mean(dxh, axis=-1, keepdims=True)
                             - xhat * jnp.mean(dxh * xhat, axis=-1, keepdims=True))
            duv_ref[:, MIX_HALF + hd * HEAD_LANES:MIX_HALF + (hd + 1) * HEAD_LANES] = (dgv_in * dgv).astype(BF16)

        @pl.when(step == nsteps - 1)
        def _():
            for hd in range(SGU_HEADS):
                sl = slice(hd * HEAD_LANES, (hd + 1) * HEAD_LANES)
                dvec_ref[2:3, sl] = jnp.sum(bacc_ref[hd].T, axis=0, keepdims=True)

    return pl.pallas_call(
        body, name=name, grid=(nsteps,),
        in_specs=[pl.BlockSpec((tm, MIX_HALF), lambda i: (i, 0)),
                  pl.BlockSpec((tm, MIX_HALF), lambda i: (i, 1)),
                  pl.BlockSpec((tm, MIX_HALF), lambda i: (i, 0)),
                  pl.BlockSpec((1, MIX_HALF), lambda i: (0, 0)),
                  pl.BlockSpec((1, MIX_HALF), lambda i: (0, 0)),
                  pl.BlockSpec(w.shape, lambda i: (0, 0, 0)),
                  pl.BlockSpec(w.shape, lambda i: (0, 0, 0)),
                  pl.BlockSpec(bcol.shape, lambda i: (0, 0))],
        out_specs=[pl.BlockSpec((tm, 2 * MIX_HALF), lambda i: (i, 0)),
                   pl.BlockSpec(w.shape, lambda i: (0, 0, 0)),
                   pl.BlockSpec((8, MIX_HALF), lambda i: (0, 0))],
        out_shape=[jax.ShapeDtypeStruct((T, 2 * MIX_HALF), BF16),
                   jax.ShapeDtypeStruct(w.shape, F32),
                   jax.ShapeDtypeStruct((8, MIX_HALF), F32)],
        scratch_shapes=[pltpu.VMEM((SGU_HEADS, ATT_BLOCK, HEAD_LANES), F32)],
        compiler_params=_params(("arbitrary",)),
    )(proj, proj, dmixed, lng, lnb, w, wt, bcol)


def _rot_half(t):
    lane = lax.broadcasted_iota(jnp.int32, t.shape, 1)
    first = (lane % ATT_DH) < (ATT_DH // 2)
    return jnp.where(first, -pltpu.roll(t, HEAD_LANES - ATT_DH // 2, 1), pltpu.roll(t, ATT_DH // 2, 1))


LAYOUT_ROWS = 512


def _res_spec(d, tm, W):
    return pl.BlockSpec((d, tm // d, W), lambda i: (0, i, 0))


def _res_shape(d, T, W, dtype):
    return jax.ShapeDtypeStruct((d, T // d, W), dtype)


def _slab_buf(tm, W):
    return pltpu.VMEM((W // HEAD_LANES, tm, HEAD_LANES), F32)


def _lanes(hp):
    return slice(hp * HEAD_LANES, (hp + 1) * HEAD_LANES)


def _to_res(buf, out_ref, d, dtype):
    nslab, tm, _ = buf.shape
    for hp in range(nslab):
        if d == 1:
            out_ref[0, :, _lanes(hp)] = buf[hp].astype(dtype)
        else:
            for r in range(d):
                out_ref[r, :, _lanes(hp)] = buf.at[hp][pl.ds(r, tm // d, stride=d), :].astype(dtype)


def _from_res(in_ref, buf, d):
    nslab, tm, _ = buf.shape
    for hp in range(nslab):
        if d == 1:
            buf[hp] = in_ref[0, :, _lanes(hp)]
        else:
            for r in range(d):
                buf.at[hp][pl.ds(r, tm // d, stride=d), :] = in_ref[r, :, _lanes(hp)]


def rope_fwd(proj, cos, sin, name):
    T = proj.shape[0]
    tm = LAYOUT_ROWS
    scale = 1.0 / math.sqrt(ATT_DH)
    nd = len(DILATIONS)

    def body(q_ref, k_ref, v_ref, cos_ref, sin_ref, *rest):
        outs, buf = rest[:3 * nd], rest[3 * nd]
        c = cos_ref[...]
        s = sin_ref[...]
        for which, src in enumerate((q_ref, k_ref, v_ref)):
            for hp in range(MIX_HALF // HEAD_LANES):
                t = src[:, _lanes(hp)].astype(F32)
                if which == 0:
                    t = scale * (t * c + _rot_half(t) * s)
                elif which == 1:
                    t = t * c + _rot_half(t) * s
                buf[hp] = t
            for di, d in enumerate(DILATIONS):
                _to_res(buf, outs[3 * di + which], d, BF16)

    return pl.pallas_call(
        body, name=name, grid=(T // tm,),
        in_specs=[pl.BlockSpec((tm, MIX_HALF), lambda i: (i, 2)),
                  pl.BlockSpec((tm, MIX_HALF), lambda i: (i, 3)),
                  pl.BlockSpec((tm, MIX_HALF), lambda i: (i, 4)),
                  pl.BlockSpec((tm, HEAD_LANES), lambda i: (i, 0)),
                  pl.BlockSpec((tm, HEAD_LANES), lambda i: (i, 0))],
        out_specs=[_res_spec(d, tm, MIX_HALF) for d in DILATIONS for _ in range(3)],
        out_shape=[_res_shape(d, T, MIX_HALF, BF16) for d in DILATIONS for _ in range(3)],
        scratch_shapes=[_slab_buf(tm, MIX_HALF)],
        compiler_params=_params(("parallel",)),
    )(proj, proj, proj, cos, sin)


def to_residues(x, col, name):
    T = x.shape[0]
    tm = LAYOUT_ROWS

    def body(x_ref, *rest):
        outs, buf = rest[:-1], rest[-1]
        for hp in range(MIX_HALF // HEAD_LANES):
            buf[hp] = x_ref[:, _lanes(hp)].astype(F32)
        for o_ref, d in zip(outs, DILATIONS):
            _to_res(buf, o_ref, d, BF16)

    return pl.pallas_call(
        body, name=name, grid=(T // tm,),
        in_specs=[pl.BlockSpec((tm, MIX_HALF), lambda i: (i, col))],
        out_specs=[_res_spec(d, tm, MIX_HALF) for d in DILATIONS],
        out_shape=[_res_shape(d, T, MIX_HALF, BF16) for d in DILATIONS],
        scratch_shapes=[_slab_buf(tm, MIX_HALF)],
        compiler_params=_params(("parallel",)),
    )(x)


def rope_bwd(dqs, dks, dvs, cos, sin, name):
    T = dqs[0].shape[0] * dqs[0].shape[1]
    tm = LAYOUT_ROWS
    scale = 1.0 / math.sqrt(ATT_DH)
    npat = len(dqs)

    def body(*refs):
        groups = refs[:npat], refs[npat:2 * npat], refs[2 * npat:3 * npat]
        cos_ref, sin_ref, o_ref, buf, acc = refs[3 * npat:]
        c = cos_ref[...]
        s = sin_ref[...]
        for which, g_refs in enumerate(groups):
            _from_res(g_refs[0], acc, DILATIONS[0])
            for g_ref, d in zip(g_refs[1:], DILATIONS[1:]):
                _from_res(g_ref, buf, d)
                acc[...] += buf[...]
            for hp in range(MIX_HALF // HEAD_LANES):
                g = acc[hp]
                if which == 0:
                    g = scale * g
                if which < 2:
                    g = g * c - _rot_half(g * s)
                o_ref[:, which * MIX_HALF + hp * HEAD_LANES:which * MIX_HALF + (hp + 1) * HEAD_LANES] = g.astype(BF16)

    return pl.pallas_call(
        body, name=name, grid=(T // tm,),
        in_specs=([_res_spec(d, tm, MIX_HALF) for _ in range(3) for d in DILATIONS]
                  + [pl.BlockSpec((tm, HEAD_LANES), lambda i: (i, 0))] * 2),
        out_specs=pl.BlockSpec((tm, 3 * MIX_HALF), lambda i: (i, 0)),
        out_shape=jax.ShapeDtypeStruct((T, 3 * MIX_HALF), BF16),
        scratch_shapes=[_slab_buf(tm, MIX_HALF), _slab_buf(tm, MIX_HALF)],
        compiler_params=_params(("parallel",)),
    )(*dqs, *dks, *dvs, cos, sin)


def _band_masks(n):
    r = lax.broadcasted_iota(jnp.int32, (2 * ATT_BLOCK, ATT_BLOCK), 0)
    c = lax.broadcasted_iota(jnp.int32, (2 * ATT_BLOCK, ATT_BLOCK), 1)
    qi = r % ATT_BLOCK
    head = (c < ATT_DH) == (r < ATT_BLOCK)
    return (c >= qi) & (n > 0), c <= qi, head, c[:ATT_BLOCK] < ATT_DH


def _stack_heads(x, head):
    x2 = jnp.concatenate([x, x], axis=0)
    return jnp.where(head, x2, jnp.zeros_like(x2))


def attn_fwd(q, k, v, name):
    d, L, W = q.shape
    nb = L // ATT_BLOCK
    nsub = 2 if nb % 2 == 0 else 1

    def body(q_ref, kp_ref, kc_ref, vp_ref, vc_ref, o_ref, lse_ref):
        step = pl.program_id(1)
        for u in range(nsub):
            rows = slice(u * ATT_BLOCK, (u + 1) * ATT_BLOCK)
            before = slice((u - 1) * ATT_BLOCK, u * ATT_BLOCK)
            mask_p, mask_c, head, head0 = _band_masks(step if u == 0 else 1)
            for hp in range(W // HEAD_LANES):
                sl = slice(hp * HEAD_LANES, (hp + 1) * HEAD_LANES)
                kp, vp = (kp_ref[0, :, sl], vp_ref[0, :, sl]) if u == 0 else (kc_ref[0, before, sl], vc_ref[0, before, sl])
                kc, vc = kc_ref[0, rows, sl], vc_ref[0, rows, sl]
                qs = _stack_heads(q_ref[0, rows, sl], head)
                sp = jnp.where(mask_p, _nt(qs, kp), NEG)
                sc = jnp.where(mask_c, _nt(qs, kc), NEG)
                m = jnp.maximum(jnp.max(sp, axis=1, keepdims=True), jnp.max(sc, axis=1, keepdims=True))
                pp = jnp.exp(sp - m)
                pc = jnp.exp(sc - m)
                den = jnp.sum(pp, axis=1, keepdims=True) + jnp.sum(pc, axis=1, keepdims=True)
                o = (_nn(pp.astype(BF16), vp) + _nn(pc.astype(BF16), vc)) / den
                lse = m + jnp.log(den)
                o_ref[0, rows, sl] = jnp.where(head0, o[:ATT_BLOCK], o[ATT_BLOCK:])
                lse_ref[0, rows, sl] = jnp.where(head0, lse[:ATT_BLOCK], lse[ATT_BLOCK:])

    cur = pl.BlockSpec((1, nsub * ATT_BLOCK, W), lambda r, n: (r, n, 0))
    prev = pl.BlockSpec((1, ATT_BLOCK, W), lambda r, n: (r, jnp.maximum(nsub * n - 1, 0), 0))
    out = jax.ShapeDtypeStruct((d, L, W), F32)
    return pl.pallas_call(
        body, name=name, grid=(d, nb // nsub),
        in_specs=[cur, prev, cur, prev, cur],
        out_specs=[cur, cur], out_shape=[out, out],
        compiler_params=_params(("parallel", "parallel")),
    )(q, k, k, v, v)


def attn_combine(os_, lses, name):
    T = os_[0].shape[0] * os_[0].shape[1]
    W = os_[0].shape[2]
    tm = LAYOUT_ROWS
    npat = len(os_)

    def body(*refs):
        o_refs, l_refs = refs[:npat], refs[npat:2 * npat]
        out_ref = refs[2 * npat]
        ores, lres = refs[2 * npat + 1:3 * npat + 1], refs[3 * npat + 1:4 * npat + 1]
        bufs = refs[4 * npat + 1:]
        lbufs, obufs, out_buf, lse_buf = bufs[:npat], bufs[npat:2 * npat], bufs[2 * npat], bufs[2 * npat + 1]
        for p, d in enumerate(DILATIONS):
            _from_res(l_refs[p], lbufs[p], d)
            _from_res(o_refs[p], obufs[p], d)
        for hp in range(W // HEAD_LANES):
            ls = [b[hp] for b in lbufs]
            m = ls[0]
            for l in ls[1:]:
                m = jnp.maximum(m, l)
            es = [jnp.exp(l - m) for l in ls]
            z = es[0]
            for e in es[1:]:
                z = z + e
            acc = es[0] * obufs[0][hp]
            for p in range(1, npat):
                acc = acc + es[p] * obufs[p][hp]
            out = acc / z
            out_ref[:, _lanes(hp)] = out.astype(BF16)
            out_buf[hp] = out
            lse_buf[hp] = m + jnp.log(z)
        for p, d in enumerate(DILATIONS):
            _to_res(out_buf, ores[p], d, BF16)
            _to_res(lse_buf, lres[p], d, F32)

    return pl.pallas_call(
        body, name=name, grid=(T // tm,),
        in_specs=[_res_spec(d, tm, W) for _ in range(2) for d in DILATIONS],
        out_specs=([pl.BlockSpec((tm, W), lambda i: (i, 0))] + [_res_spec(d, tm, W) for _ in range(2) for d in DILATIONS]),
        out_shape=([jax.ShapeDtypeStruct((T, W), BF16)] + [_res_shape(d, T, W, BF16) for d in DILATIONS]
                   + [_res_shape(d, T, W, F32) for d in DILATIONS]),
        scratch_shapes=[_slab_buf(tm, W)] * (2 * npat + 2),
        compiler_params=_params(("parallel",)),
    )(*os_, *lses)


def attn_bwd(q, k, v, do, o, lse, name):
    d, L, W = q.shape
    nb = L // ATT_BLOCK
    nsub = 2 if nb % 2 == 0 else 1
    nst = nb // nsub
    last = slice((nsub - 1) * ATT_BLOCK, nsub * ATT_BLOCK)

    def body(q_ref, kp_ref, kc_ref, vp_ref, vc_ref, do_ref, o_ref, lse_ref, dq_ref, dk_ref, dv_ref, kkeep, vkeep):
        n = pl.program_id(1)

        @pl.when(n == 0)
        def _():
            kkeep[...] = jnp.zeros_like(kkeep)
            vkeep[...] = jnp.zeros_like(vkeep)

        @pl.when(n < nst)
        def _():
            for hp in range(W // HEAD_LANES):
                sl = slice(hp * HEAD_LANES, (hp + 1) * HEAD_LANES)
                shares = []
                for u in range(nsub):
                    rows = slice(u * ATT_BLOCK, (u + 1) * ATT_BLOCK)
                    before = slice((u - 1) * ATT_BLOCK, u * ATT_BLOCK)
                    mask_p, mask_c, head, head0 = _band_masks(n if u == 0 else 1)
                    kp, vp = (kp_ref[0, :, sl], vp_ref[0, :, sl]) if u == 0 else (kc_ref[0, before, sl], vc_ref[0, before, sl])
                    kc, vc = kc_ref[0, rows, sl], vc_ref[0, rows, sl]
                    dout = do_ref[0, rows, sl]
                    qs = _stack_heads(q_ref[0, rows, sl], head)
                    dos = _stack_heads(dout, head)
                    lse_v = lse_ref[0, rows, sl]
                    lse_c = jnp.max(jnp.where(head, jnp.concatenate([lse_v, lse_v], axis=0), NEG), axis=1, keepdims=True)
                    delta = jnp.sum(_stack_heads(dout.astype(F32) * o_ref[0, rows, sl].astype(F32), head), axis=1,
                                    keepdims=True)
                    pp = jnp.exp(jnp.where(mask_p, _nt(qs, kp), NEG) - lse_c)
                    pc = jnp.exp(jnp.where(mask_c, _nt(qs, kc), NEG) - lse_c)
                    dsp = (pp * (_nt(dos, vp) - delta)).astype(BF16)
                    dsc = (pc * (_nt(dos, vc) - delta)).astype(BF16)
                    dq2 = _nn(dsp, kp) + _nn(dsc, kc)
                    dq_ref[0, rows, sl] = jnp.where(head0, dq2[:ATT_BLOCK], dq2[ATT_BLOCK:])
                    shares.append((_tn(dsp, qs), _tn(pp.astype(BF16), dos), _tn(dsc, qs), _tn(pc.astype(BF16), dos)))
                dk_ref[0, last, sl] = kkeep[last, sl] + shares[0][0]
                dv_ref[0, last, sl] = vkeep[last, sl] + shares[0][1]
                if nsub == 2:
                    dk_ref[0, :ATT_BLOCK, sl] = kkeep[:ATT_BLOCK, sl]
                    dv_ref[0, :ATT_BLOCK, sl] = vkeep[:ATT_BLOCK, sl]
                    kkeep[:ATT_BLOCK, sl] = shares[0][2] + shares[1][0]
                    vkeep[:ATT_BLOCK, sl] = shares[0][3] + shares[1][1]
                kkeep[last, sl] = shares[-1][2]
                vkeep[last, sl] = shares[-1][3]

        @pl.when(n == nst)
        def _():
            dk_ref[0] = kkeep[...]
            dv_ref[0] = vkeep[...]

    rows_per_step = nsub * ATT_BLOCK
    cur = pl.BlockSpec((1, rows_per_step, W), lambda r, n: (r, jnp.minimum(n, nst - 1), 0))
    lag = pl.BlockSpec((1, rows_per_step, W), lambda r, n: (r, jnp.clip(n - 1, 0, nst - 1), 0))
    prev = pl.BlockSpec((1, ATT_BLOCK, W), lambda r, n: (r, jnp.clip(nsub * n - 1, 0, nb - 1), 0))
    out = jax.ShapeDtypeStruct((d, L, W), F32)
    return pl.pallas_call(
        body, name=name, grid=(d, nst + 1),
        in_specs=[cur, prev, cur, prev, cur, cur, cur, cur],
        out_specs=[cur, lag, lag], out_shape=[out, out, out],
        scratch_shapes=[pltpu.VMEM((rows_per_step, W), F32), pltpu.VMEM((rows_per_step, W), F32)],
        compiler_params=_params(("parallel", "arbitrary")),
    )(q, k, k, v, v, do, o, lse)


def final_loss_bwd(h, gf, tgt, gate, name):
    T, D = h.shape
    tm = _tile(T, 512, 16)
    g_in, g_out, g_shape, g_ops = _gate_specs(gate, tm, D)

    def body(h_ref, g_ref, t_ref, o_ref, modg_ref, dh_ref, red_ref, do_ref, redg_ref):
        x = h_ref[...]
        r = lax.rsqrt(jnp.mean(x * x, axis=-1, keepdims=True) + EPS)
        n = x * r
        g = g_ref[...]
        err = n * g - t_ref[...]
        dy = err * (1.0 / D)

        @pl.when(pl.program_id(0) == 0)
        def _():
            red_ref[...] = jnp.zeros_like(red_ref)

        red_ref[0:1, :] += jnp.sum(dy * n, axis=0, keepdims=True)
        red_ref[1:2, :] += jnp.zeros((1, D), F32) + (0.5 / D) * jnp.sum(err * err, keepdims=True)
        dn = dy * g
        dh = r * (dn - n * jnp.mean(dn * n, axis=-1, keepdims=True))
        dh_ref[...] = dh
        _gate_emit(dh, gate, o_ref, modg_ref, do_ref, redg_ref)

    return pl.pallas_call(
        body, name=name, grid=(T // tm,),
        in_specs=[pl.BlockSpec((tm, D), lambda i: (i, 0)),
                  pl.BlockSpec((1, D), lambda i: (0, 0)),
                  pl.BlockSpec((tm, D), lambda i: (i, 0))] + g_in,
        out_specs=[pl.BlockSpec((tm, D), lambda i: (i, 0)), pl.BlockSpec((8, D), lambda i: (0, 0))] + g_out,
        out_shape=[jax.ShapeDtypeStruct((T, D), F32), jax.ShapeDtypeStruct((8, D), F32)] + g_shape,
        compiler_params=_params(("arbitrary",)),
    )(h, gf, tgt, *g_ops)


def ada_fwd(c_all, ada_w, ada_b, name):
    nl, D, N = ada_w.shape

    def body(c_ref, w_ref, b_ref, o_ref):
        c = c_ref[...]
        o_ref[0] = _nn(c * _sigmoid(c), w_ref[0]) + b_ref[0]

    return pl.pallas_call(
        body, name=name, grid=(nl,),
        in_specs=[pl.BlockSpec((N_DEV, D), lambda l: (0, 0)),
                  pl.BlockSpec((1, D, N), lambda l: (l, 0, 0)),
                  pl.BlockSpec((1, 1, N), lambda l: (l, 0, 0))],
        out_specs=pl.BlockSpec((1, N_DEV, N), lambda l: (l, 0, 0)),
        out_shape=jax.ShapeDtypeStruct((nl, N_DEV, N), F32),
        compiler_params=_params(("parallel",)),
    )(c_all, ada_w, ada_b)


def ada_bwd(c_allT, dmod, name):
    nl, _, N = dmod.shape
    D = c_allT.shape[0]

    def body(c_ref, g_ref, o_ref):
        c = c_ref[...]
        ca = c * _sigmoid(c)
        acc = ca[:, 0:1] * g_ref[0, 0:1, :]
        for b in range(1, N_DEV):
            acc = acc + ca[:, b:b + 1] * g_ref[0, b:b + 1, :]
        o_ref[0] = acc

    return pl.pallas_call(
        body, name=name, grid=(nl,),
        in_specs=[pl.BlockSpec((D, N_DEV), lambda l: (0, 0)),
                  pl.BlockSpec((1, N_DEV, N), lambda l: (l, 0, 0))],
        out_specs=pl.BlockSpec((1, D, N), lambda l: (l, 0, 0)),
        out_shape=jax.ShapeDtypeStruct((nl, D, N), F32),
        compiler_params=_params(("parallel",)),
    )(c_allT, dmod)


def adamw(w, g, m, v, name):
    R, C = w.shape
    tr = _tile(R, max(8, (1 << 19) // C // 8 * 8), 8)
    c1 = 1.0 - ADAM_B1 ** ADAM_STEP
    c2 = 1.0 - ADAM_B2 ** ADAM_STEP

    def body(w_ref, g_ref, m_ref, v_ref, d_ref, mo_ref, vo_ref):
        gv = g_ref[...]
        mn = ADAM_B1 * m_ref[...] + (1.0 - ADAM_B1) * gv
        vn = ADAM_B2 * v_ref[...] + (1.0 - ADAM_B2) * (gv * gv)
        mo_ref[...] = mn
        vo_ref[...] = vn
        d_ref[...] = -ADAM_LR * ((mn / c1) / (jnp.sqrt(vn / c2) + ADAM_EPS) + ADAM_WD * w_ref[...])

    blk = pl.BlockSpec((tr, C), lambda i: (i, 0))
    out = jax.ShapeDtypeStruct((R, C), F32)
    return pl.pallas_call(
        body, name=name, grid=(R // tr,),
        in_specs=[blk] * 4, out_specs=[blk] * 3, out_shape=[out] * 3,
        compiler_params=_params(("parallel",)),
    )(w, g, m, v)


def adamw_layer(w, g, m, v, l, prev, name):
    NLw, R, C = w.shape
    tr = _tile(R, max(8, (1 << 19) // C // 8 * 8), 8)
    nrb = R // tr
    c1 = 1.0 - ADAM_B1 ** ADAM_STEP
    c2 = 1.0 - ADAM_B2 ** ADAM_STEP
    w, m, v = (t.reshape(NLw * R, C) for t in (w, m, v))

    def body(w_ref, g_ref, m_ref, v_ref, *rest):
        go_ref, d_ref, mo_ref, vo_ref = rest[-4:]
        gv = g_ref[...]
        mn = ADAM_B1 * m_ref[...] + (1.0 - ADAM_B1) * gv
        vn = ADAM_B2 * v_ref[...] + (1.0 - ADAM_B2) * (gv * gv)
        go_ref[...] = gv
        mo_ref[...] = mn
        vo_ref[...] = vn
        d_ref[...] = -ADAM_LR * ((mn / c1) / (jnp.sqrt(vn / c2) + ADAM_EPS) + ADAM_WD * w_ref[...])

    lay = pl.BlockSpec((tr, C), lambda i: (l * nrb + i, 0))
    out = jax.ShapeDtypeStruct((NLw * R, C), F32)
    n_prev = 0 if prev is None else 4
    return pl.pallas_call(
        body, name=name, grid=(nrb,),
        in_specs=[lay, pl.BlockSpec((tr, C), lambda i: (i, 0)), lay, lay] + [pl.BlockSpec(memory_space=pl.ANY)] * n_prev,
        out_specs=[lay] * 4, out_shape=[out] * 4,
        input_output_aliases={4 + i: i for i in range(n_prev)},
        compiler_params=_params(("parallel",)),
    )(w, g, m, v, *(prev or ()))


def sum_slots(x, name):
    S, R, C = x.shape
    tr = _tile(R, 128, 8)

    def body(x_ref, o_ref):
        acc = x_ref[0]
        for s in range(1, S):
            acc = acc + x_ref[s]
        o_ref[...] = acc

    return pl.pallas_call(
        body, name=name, grid=(R // tr,),
        in_specs=[pl.BlockSpec((S, tr, C), lambda i: (0, i, 0))],
        out_specs=pl.BlockSpec((tr, C), lambda i: (i, 0)),
        out_shape=jax.ShapeDtypeStruct((R, C), F32),
        compiler_params=_params(("parallel",)),
    )(x)


def sum_halves(g, lands, c_idx, name):
    n, ns, _, rh, D = g.shape

    def body(c_ref, g_ref, l_ref, o_ref):
        for j in range(ns):
            o_ref[0, j] = (g_ref[0, j, 0].astype(F32) + l_ref[0, j].astype(F32)).astype(BF16)

    return pl.pallas_call(
        body, name=name,
        grid_spec=pltpu.PrefetchScalarGridSpec(
            num_scalar_prefetch=1, grid=(n,),
            in_specs=[pl.BlockSpec((1, ns, 1, rh, D), lambda i, c: (i, 0, c[0], 0, 0)),
                      pl.BlockSpec((1, ns, rh, D), lambda i, c: (i, 0, 0, 0))],
            out_specs=pl.BlockSpec((1, ns, rh, D), lambda i, c: (i, 0, 0, 0))),
        out_shape=jax.ShapeDtypeStruct((n, ns, rh, D), BF16),
        compiler_params=_params(("parallel",)),
    )(c_idx, g, lands)


def sum_chips(p, lands, place, name):
    n, ns, rh, D = p.shape

    def body(c_ref, p_ref, l_ref, o_ref):
        acc = p_ref[0, 0].astype(F32)
        for j in range(N_CHIP - 1):
            acc = acc + l_ref[j, 0].astype(F32)
        o_ref[0, 0] = acc

    return pl.pallas_call(
        body, name=name,
        grid_spec=pltpu.PrefetchScalarGridSpec(
            num_scalar_prefetch=1, grid=(n,),
            in_specs=[pl.BlockSpec((1, 1, rh, D), lambda i, c: (i, c[0], 0, 0)),
                      pl.BlockSpec((N_CHIP - 1, 1, rh, D), lambda i, c: (0, i, 0, 0))],
            out_specs=pl.BlockSpec((1, 1, rh, D), lambda i, c: (i, c[1], 0, 0))),
        out_shape=jax.ShapeDtypeStruct((n, 2, rh, D), F32),
        compiler_params=_params(("parallel",)),
    )(place, p, lands)


def _my_place():
    return lax.axis_index("x"), lax.axis_index("y"), lax.axis_index("c")


def _other_chips(mx, my):
    return [(1 - mx, my), (mx, 1 - my), (1 - mx, 1 - my)]


def gather_small(x, after, name):
    def body(x_ref, after_ref, out_ref, sum_ref, send_sems, recv_sems):
        mx, my, mc = _my_place()
        me = 4 * mx + 2 * my + mc
        out_ref[me] = x_ref[...]
        sends = []
        for k in range(1, N_DEV):
            kx, ky, kc = (k >> 2) & 1, (k >> 1) & 1, k & 1
            peer = (1 - mx if kx else mx, 1 - my if ky else my, 1 - mc if kc else mc)
            cp = pltpu.make_async_remote_copy(
                src_ref=x_ref, dst_ref=out_ref.at[me], send_sem=send_sems.at[k - 1], recv_sem=recv_sems.at[k - 1],
                device_id=peer, device_id_type=MESH)
            cp.start()
            sends.append((cp, 4 * peer[0] + 2 * peer[1] + peer[2], peer))
        for k, (cp, peer_slot, peer) in enumerate(sends):
            pltpu.make_async_remote_copy(
                src_ref=x_ref, dst_ref=out_ref.at[peer_slot], send_sem=send_sems.at[k], recv_sem=recv_sems.at[k],
                device_id=peer, device_id_type=MESH).wait_recv()
        for cp, _, _ in sends:
            cp.wait_send()
        acc = out_ref[0]
        for s in range(1, N_DEV):
            acc = acc + out_ref[s]
        sum_ref[...] = acc

    vmem = pl.BlockSpec(memory_space=pltpu.VMEM)
    return pl.pallas_call(
        body, name=name,
        in_specs=[vmem, pl.BlockSpec(memory_space=pl.ANY)], out_specs=[vmem, vmem],
        out_shape=[jax.ShapeDtypeStruct((N_DEV,) + x.shape, x.dtype), jax.ShapeDtypeStruct(x.shape, x.dtype)],
        scratch_shapes=[pltpu.SemaphoreType.DMA((N_DEV - 1,)), pltpu.SemaphoreType.DMA((N_DEV - 1,))],
        compiler_params=pltpu.CompilerParams(vmem_limit_bytes=VMEM_LIMIT),
    )(x, after)


_HBM =pl.BlockSpec(memory_space=pltpu.HBM)
_SEM = pl.BlockSpec(memory_space=pltpu.SEMAPHORE)
_DATAFLOW = pltpu.SideEffectType.DATAFLOW_SIDE_EFFECTING


def _gather_copies(shard, land, send, recv, base):
    mx, my, mc = _my_place()
    ci = 2 * mx + my
    peers = [((cx, cy, mc), 2 * cx + cy) for cx, cy in _other_chips(mx, my)] + [((mx, my, 1 - mc), ci)]
    out = []
    for q, (dev, src_slot) in enumerate(peers):
        out.append((
            pltpu.make_async_remote_copy(src_ref=shard, dst_ref=land.at[:, ci], send_sem=send.at[base + q],
                                         recv_sem=recv.at[base + q], device_id=dev, device_id_type=MESH),
            pltpu.make_async_remote_copy(src_ref=shard, dst_ref=land.at[:, src_slot], send_sem=send.at[base + q],
                                         recv_sem=recv.at[base + q], device_id=dev, device_id_type=MESH)))
    return out


def gather_start(groups, after, name):
    items = [s for g in groups for s in g]
    ni, ng = len(items), len(groups)

    def body(*refs):
        shards, lands = refs[:ni], refs[ni:2 * ni]
        sems = refs[2 * ni + 1:2 * ni + 1 + 2 * ng]
        token = refs[-1]
        i = 0
        for g, grp in enumerate(groups):
            for p in range(len(grp)):
                for start_cp, _ in _gather_copies(shards[i], lands[i], sems[2 * g], sems[2 * g + 1], 4 * p):
                    start_cp.start()
                i += 1
        token[...] = jnp.zeros_like(token)

    sem_shapes = []
    for grp in groups:
        sem_shapes += [pltpu.SemaphoreType.DMA((4 * len(grp),))] * 2
    land_shapes = [(s.shape[0], N_CHIP) + s.shape[1:] for s in items]
    outs = pl.pallas_call(
        body, name=name,
        in_specs=[_HBM] * (2 * ni) + [pl.BlockSpec(memory_space=pl.ANY)],
        out_specs=[_SEM] * (2 * ng) + [_HBM] * (2 * ni) + [pl.BlockSpec(memory_space=pltpu.VMEM)],
        out_shape=(sem_shapes + [pltpu.HBM(s.shape, s.dtype) for s in items]
                   + [pltpu.HBM(ls, s.dtype) for ls, s in zip(land_shapes, items)]
                   + [jax.ShapeDtypeStruct((8, 128), F32)]),
        input_output_aliases={i: 2 * ng + i for i in range(2 * ni)},
        compiler_params=pltpu.CompilerParams(has_side_effects=_DATAFLOW),
    )(*[pltpu.with_memory_space_constraint(s, pltpu.HBM) for s in items],
      *[pltpu.with_memory_space_constraint(lax.empty(ls, s.dtype), pltpu.HBM) for ls, s in zip(land_shapes, items)],
      after)
    sems, thru, token = outs[:2 * ng], outs[2 * ng:2 * ng + 2 * ni], outs[-1]
    handles, i = [], 0
    for g, grp in enumerate(groups):
        n = len(grp)
        handles.append((sems[2 * g], sems[2 * g + 1], thru[i:i + n], thru[ni + i:ni + i + n]))
        i += n
    return handles, token


def gather_wait(handle, after, name):
    send, recv, shards, lands = handle
    n = len(shards)

    def body(*refs):
        shard_refs, land_refs = refs[:n], refs[n:2 * n]
        send_ref, recv_ref = refs[2 * n], refs[2 * n + 1]
        for p in range(n):
            for start_cp, recv_cp in _gather_copies(shard_refs[p], land_refs[p], send_ref, recv_ref, 4 * p):
                start_cp.wait_send()
                recv_cp.wait_recv()

    outs = pl.pallas_call(
        body, name=name,
        in_specs=[_HBM] * (2 * n) + [_SEM, _SEM, pl.BlockSpec(memory_space=pl.ANY)],
        out_specs=[_HBM] * (2 * n),
        out_shape=[pltpu.HBM(s.shape, s.dtype) for s in shards] + [pltpu.HBM(l.shape, l.dtype) for l in lands],
        input_output_aliases={i: i for i in range(2 * n)},
        compiler_params=pltpu.CompilerParams(has_side_effects=_DATAFLOW),
    )(*shards, *lands, send, recv, after)
    return outs[n:]


def _first_copies(shard, land, send, recv):
    mx, my, mc = _my_place()
    ci = 2 * mx + my
    out = []
    for q, (cx, cy) in enumerate(_other_chips(mx, my)):
        dev = (cx, cy, mc)
        out.append(tuple(pltpu.make_async_remote_copy(
            src_ref=shard.at[:, mc], dst_ref=land.at[:, slot, mc], send_sem=send.at[q], recv_sem=recv.at[q],
            device_id=dev, device_id_type=MESH) for slot in (ci, 2 * cx + cy)))
    sib = pltpu.make_async_remote_copy(src_ref=shard, dst_ref=land.at[:, ci], send_sem=send.at[3], recv_sem=recv.at[3],
                                       device_id=(mx, my, 1 - mc), device_id_type=MESH)
    return out + [(sib, sib)]


def _forward_copies(land, send, recv):
    mx, my, mc = _my_place()
    out = []
    for q, (cx, cy) in enumerate(_other_chips(mx, my)):
        out.append(tuple(pltpu.make_async_remote_copy(
            src_ref=land.at[:, 2 * cx + cy, hc], dst_ref=land.at[:, 2 * cx + cy, hc], send_sem=send.at[q],
            recv_sem=recv.at[q], device_id=(mx, my, 1 - mc), device_id_type=MESH) for hc in (mc, 1 - mc)))
    return out


def first_start(shard, after, name):
    def body(shard_ref, land_ref, after_ref, send, recv, shard_thru, land_thru, token):
        for mine, _ in _first_copies(shard_ref, land_ref, send, recv):
            mine.start()
        token[...] = jnp.zeros_like(token)

    land_shape = (shard.shape[0], N_CHIP) + shard.shape[1:]
    outs = pl.pallas_call(
        body, name=name,
        in_specs=[_HBM, _HBM, pl.BlockSpec(memory_space=pl.ANY)],
        out_specs=[_SEM, _SEM, _HBM, _HBM, pl.BlockSpec(memory_space=pltpu.VMEM)],
        out_shape=[pltpu.SemaphoreType.DMA((4,))] * 2 + [pltpu.HBM(shard.shape, shard.dtype),
                                                         pltpu.HBM(land_shape, shard.dtype),
                                                         jax.ShapeDtypeStruct((8, 128), F32)],
        input_output_aliases={0: 2, 1: 3},
        compiler_params=pltpu.CompilerParams(has_side_effects=_DATAFLOW),
    )(pltpu.with_memory_space_constraint(shard, pltpu.HBM),
      pltpu.with_memory_space_constraint(lax.empty(land_shape, shard.dtype), pltpu.HBM), after)
    return outs[:4], outs[4]


def first_forward(handle, after, name):
    send, recv, shard, land = handle

    def body(shard_ref, land_ref, send_ref, recv_ref, after_ref, send2, recv2, shard_thru, land_thru):
        firsts = _first_copies(shard_ref, land_ref, send_ref, recv_ref)
        forwards = _forward_copies(land_ref, send2, recv2)
        for q in range(3):
            firsts[q][1].wait_recv()
            forwards[q][0].start()
        firsts[3][1].wait_recv()
        for mine, _ in firsts:
            mine.wait_send()

    outs = pl.pallas_call(
        body, name=name,
        in_specs=[_HBM, _HBM, _SEM, _SEM, pl.BlockSpec(memory_space=pl.ANY)],
        out_specs=[_SEM, _SEM, _HBM, _HBM],
        out_shape=[pltpu.SemaphoreType.DMA((3,))] * 2 + [pltpu.HBM(shard.shape, shard.dtype),
                                                         pltpu.HBM(land.shape, land.dtype)],
        input_output_aliases={0: 2, 1: 3},
        compiler_params=pltpu.CompilerParams(has_side_effects=_DATAFLOW),
    )(shard, land, send, recv, after)
    return outs[0], outs[1], outs[3]


def first_wait(handle, after, name):
    send, recv, land = handle

    def body(land_ref, send_ref, recv_ref, after_ref, land_out):
        for mine, theirs in _forward_copies(land_ref, send_ref, recv_ref):
            mine.wait_send()
            theirs.wait_recv()

    return pl.pallas_call(
        body, name=name,
        in_specs=[_HBM, _SEM, _SEM, pl.BlockSpec(memory_space=pl.ANY)],
        out_specs=[_HBM],
        out_shape=[pltpu.HBM(land.shape, land.dtype)],
        input_output_aliases={0: 0},
        compiler_params=pltpu.CompilerParams(has_side_effects=_DATAFLOW),
    )(land, send, recv, after)[0]


def _sibling_copies(gs, lands, send, recv):
    mx, my, mc = _my_place()
    return [pltpu.make_async_remote_copy(
        src_ref=gs[k].at[:, :, 1 - mc], dst_ref=lands[k], send_sem=send.at[k], recv_sem=recv.at[k],
        device_id=(mx, my, 1 - mc), device_id_type=MESH) for k in range(len(gs))]


def sibling_start(gs, after, name):
    K = len(gs)

    def body(*refs):
        ins, lands = refs[:K], refs[K:2 * K]
        send, recv = refs[2 * K + 1], refs[2 * K + 2]
        for cp in _sibling_copies(ins, lands, send, recv):
            cp.start()
        refs[-1][...] = jnp.zeros_like(refs[-1])

    land_shapes = [g.shape[:2] + g.shape[3:] for g in gs]
    outs = pl.pallas_call(
        body, name=name,
        in_specs=[_HBM] * (2 * K) + [pl.BlockSpec(memory_space=pl.ANY)],
        out_specs=[_SEM, _SEM] + [_HBM] * (2 * K) + [pl.BlockSpec(memory_space=pltpu.VMEM)],
        out_shape=([pltpu.SemaphoreType.DMA((K,))] * 2 + [pltpu.HBM(g.shape, g.dtype) for g in gs]
                   + [pltpu.HBM(ls, g.dtype) for ls, g in zip(land_shapes, gs)] + [jax.ShapeDtypeStruct((8, 128), F32)]),
        input_output_aliases={i: 2 + i for i in range(2 * K)},
        compiler_params=pltpu.CompilerParams(has_side_effects=_DATAFLOW),
    )(*[pltpu.with_memory_space_constraint(g, pltpu.HBM) for g in gs],
      *[pltpu.with_memory_space_constraint(lax.empty(ls, g.dtype), pltpu.HBM) for ls, g in zip(land_shapes, gs)],
      after)
    return (outs[0], outs[1], outs[2:2 + K], outs[2 + K:2 + 2 * K]), outs[-1]


def sibling_wait(handle, after, name):
    send, recv, gs, lands = handle
    K = len(gs)

    def body(*refs):
        ins, land_refs = refs[:K], refs[K:2 * K]
        for cp in _sibling_copies(ins, land_refs, refs[2 * K], refs[2 * K + 1]):
            cp.wait_send()
            cp.wait_recv()

    outs = pl.pallas_call(
        body, name=name,
        in_specs=[_HBM] * (2 * K) + [_SEM, _SEM, pl.BlockSpec(memory_space=pl.ANY)],
        out_specs=[_HBM] * (2 * K),
        out_shape=[pltpu.HBM(g.shape, g.dtype) for g in gs] + [pltpu.HBM(l.shape, l.dtype) for l in lands],
        input_output_aliases={i: i for i in range(2 * K)},
        compiler_params=pltpu.CompilerParams(has_side_effects=_DATAFLOW),
    )(*gs, *lands, send, recv, after)
    return outs[:K], outs[K:]


def _small_copies(x, land, send, recv):
    mx, my, mc = _my_place()
    me = 4 * mx + 2 * my + mc
    out = []
    for k in range(1, N_DEV):
        peer = (1 - mx if k & 4 else mx, 1 - my if k & 2 else my, 1 - mc if k & 1 else mc)
        slot = 4 * peer[0] + 2 * peer[1] + peer[2]
        out.append(tuple(pltpu.make_async_remote_copy(
            src_ref=x, dst_ref=land.at[s], send_sem=send.at[k - 1], recv_sem=recv.at[k - 1],
            device_id=peer, device_id_type=MESH) for s in (me, slot)))
    return out


def small_start(x, after, name):
    def body(x_ref, land_ref, after_ref, send, recv, x_thru, land_thru, token):
        for mine, _ in _small_copies(x_ref, land_ref, send, recv):
            mine.start()
        token[...] = jnp.zeros_like(token)

    land_shape = (N_DEV,) + x.shape
    outs = pl.pallas_call(
        body, name=name,
        in_specs=[_HBM, _HBM, pl.BlockSpec(memory_space=pl.ANY)],
        out_specs=[_SEM, _SEM, _HBM, _HBM, pl.BlockSpec(memory_space=pltpu.VMEM)],
        out_shape=[pltpu.SemaphoreType.DMA((N_DEV - 1,))] * 2 + [pltpu.HBM(x.shape, x.dtype), pltpu.HBM(land_shape, x.dtype),
                                                                 jax.ShapeDtypeStruct((8, 128), F32)],
        input_output_aliases={0: 2, 1: 3},
        compiler_params=pltpu.CompilerParams(has_side_effects=_DATAFLOW),
    )(pltpu.with_memory_space_constraint(x, pltpu.HBM),
      pltpu.with_memory_space_constraint(lax.empty(land_shape, x.dtype), pltpu.HBM), after)
    return outs[:4], outs[4]


def small_wait(handle, after, name):
    send, recv, x, land = handle

    def body(x_ref, land_ref, send_ref, recv_ref, after_ref, x_out, land_out):
        for mine, theirs in _small_copies(x_ref, land_ref, send_ref, recv_ref):
            mine.wait_send()
            theirs.wait_recv()

    return pl.pallas_call(
        body, name=name,
        in_specs=[_HBM, _HBM, _SEM, _SEM, pl.BlockSpec(memory_space=pl.ANY)],
        out_specs=[_HBM, _HBM],
        out_shape=[pltpu.HBM(x.shape, x.dtype), pltpu.HBM(land.shape, land.dtype)],
        input_output_aliases={0: 0, 1: 1},
        compiler_params=pltpu.CompilerParams(has_side_effects=_DATAFLOW),
    )(x, land, send, recv, after)


def _scatter_copies(ps, lands, send, recv):
    mx, my, mc = _my_place()
    cps = []
    for j, (cx, cy) in enumerate(_other_chips(mx, my)):
        for k in range(len(ps)):
            cps.append(pltpu.make_async_remote_copy(
                src_ref=ps[k].at[:, 2 * cx + cy], dst_ref=lands[k].at[j],
                send_sem=send.at[k * 3 + j], recv_sem=recv.at[k * 3 + j],
                device_id=(cx, cy, mc), device_id_type=MESH))
    return cps


def scatter_start(ps, after, name):
    K = len(ps)

    def body(*refs):
        ins, lands = refs[:K], refs[K:2 * K]
        send, recv = refs[2 * K + 1], refs[2 * K + 2]
        for cp in _scatter_copies(ins, lands, send, recv):
            cp.start()
        refs[-1][...] = jnp.zeros_like(refs[-1])

    land_shapes = [(N_CHIP - 1, p.shape[0]) + p.shape[2:] for p in ps]
    outs = pl.pallas_call(
        body, name=name,
        in_specs=[_HBM] * (2 * K) + [pl.BlockSpec(memory_space=pl.ANY)],
        out_specs=[_SEM, _SEM] + [_HBM] * (2 * K) + [pl.BlockSpec(memory_space=pltpu.VMEM)],
        out_shape=([pltpu.SemaphoreType.DMA((3 * K,))] * 2 + [pltpu.HBM(p.shape, p.dtype) for p in ps]
                   + [pltpu.HBM(ls, p.dtype) for ls, p in zip(land_shapes, ps)] + [jax.ShapeDtypeStruct((8, 128), F32)]),
        input_output_aliases={i: 2 + i for i in range(2 * K)},
        compiler_params=pltpu.CompilerParams(has_side_effects=_DATAFLOW),
    )(*[pltpu.with_memory_space_constraint(p, pltpu.HBM) for p in ps],
      *[pltpu.with_memory_space_constraint(lax.empty(ls, p.dtype), pltpu.HBM) for ls, p in zip(land_shapes, ps)],
      after)
    return (outs[0], outs[1], outs[2:2 + K], outs[2 + K:2 + 2 * K]), outs[-1]


def scatter_wait(handle, after, name):
    send, recv, ps, lands = handle
    K = len(ps)
    afters = list(after) if isinstance(after, (list, tuple)) else [after]

    def body(*refs):
        ins, land_refs = refs[:K], refs[K:2 * K]
        send_ref, recv_ref = refs[2 * K], refs[2 * K + 1]
        for cp in _scatter_copies(ins, land_refs, send_ref, recv_ref):
            cp.wait_send()
            cp.wait_recv()

    outs = pl.pallas_call(
        body, name=name,
        in_specs=[_HBM] * (2 * K) + [_SEM, _SEM] + [pl.BlockSpec(memory_space=pl.ANY)] * len(afters),
        out_specs=[_HBM] * (2 * K),
        out_shape=[pltpu.HBM(p.shape, p.dtype) for p in ps] + [pltpu.HBM(l.shape, l.dtype) for l in lands],
        input_output_aliases={i: i for i in range(2 * K)},
        compiler_params=pltpu.CompilerParams(has_side_effects=_DATAFLOW),
    )(*ps, *lands, send, recv, *afters)
    return outs[:K], outs[K:]


def sibling_complete(ss, name):
    K = len(ss)

    def body(*refs):
        ins, outs = refs[:K], refs[K:2 * K]
        send, recv = refs[2 * K:]
        mx, my, mc = _my_place()
        cps = []
        for k in range(K):
            cp = pltpu.make_async_remote_copy(
                src_ref=ins[k].at[:, mc], dst_ref=outs[k].at[:, mc], send_sem=send.at[k], recv_sem=recv.at[k],
                device_id=(mx, my, 1 - mc), device_id_type=MESH)
            cp.start()
            cps.append(cp)
        for k in range(K):
            pltpu.make_async_remote_copy(
                src_ref=ins[k].at[:, mc], dst_ref=outs[k].at[:, 1 - mc], send_sem=send.at[k], recv_sem=recv.at[k],
                device_id=(mx, my, 1 - mc), device_id_type=MESH).wait_recv()
        for cp in cps:
            cp.wait_send()

    hbm = pl.BlockSpec(memory_space=pl.ANY)
    return pl.pallas_call(
        body, name=name,
        in_specs=[hbm] * K, out_specs=[hbm] * K,
        out_shape=[jax.ShapeDtypeStruct(s.shape, s.dtype) for s in ss],
        scratch_shapes=[pltpu.SemaphoreType.DMA((K,)), pltpu.SemaphoreType.DMA((K,))],
        input_output_aliases={k: k for k in range(K)},
    )(*ss)


def _rope_tables(T):
    inv = ROPE_THETA ** (-jnp.arange(0, ATT_DH, 2, dtype=F32) / ATT_DH)
    ang = jnp.arange(T, dtype=F32)[:, None] * inv[None, :]
    ang = jnp.concatenate([ang, ang, ang, ang], axis=-1)
    return jnp.cos(ang), jnp.sin(ang)


def _ffn_fwd(h, y, mod, i0, get_up, get_down, norm_next, tag):
    wgu = get_up(y)
    a, b, s = ffn_up(y, (wgu, (0,)), (wgu, (1,)), f"ffn_up_{tag}")
    wd = get_down(s)
    outs = resid_matmul([s], (wd, (0,)), h, mod, i0 + 2, 0.5, f"ffn_down_{tag}", norm_next)
    hn, o = outs[0], outs[1]
    return hn, (outs[2] if norm_next else None), (h, y, a, b, s, o), ((wgu, (0,)), (wgu, (1,)), (wd, (0,)))


def _ffn_bwd(dh, do, res, ng, i_n, mod, i0, wgT, wuT, wd, on_grads, next_gate, after, tag):
    h, y, a, b, s, o = res
    F = _wrows(wgT)
    da, db = ffn_bwd_mid(do, wd, a, b, f"ffn_bwd_mid_{tag}", after)
    gbuf = lax.empty((3, F, h.shape[1]), BF16)
    gbuf = matmul_tn(da, y, gbuf, 0, 0, f"dwg_{tag}")
    gbuf = matmul_tn(db, y, gbuf, 1, 0, f"dwu_{tag}")
    gbuf = matmul_tn(s, do, gbuf, 2, 0, f"dwd_{tag}")
    token, then = on_grads([gbuf])
    outs = dy_normbwd([(da, 0, wgT, 0, F), (db, 0, wuT, 0, F)], h, dh, ng, i_n, mod, i0 + 1,
                      f"ffn_bwd_dy_{tag}", next_gate, [token])
    return outs, then


def _mixer_fwd(h, y, mod, w_inT, w_out, sgu, cos, sin, norm_next, tag):
    lng, lnb, sw, swt, bcol = sgu
    proj = matmul_nt(y, w_inT, f"proj_{tag}")
    out_a = sgu_fwd(proj, lng, lnb, sw, bcol, f"sgu_fwd_{tag}")
    qkv = rope_fwd(proj, cos, sin, f"rope_fwd_{tag}")
    npat = len(DILATIONS)
    qkv_res = [tuple(qkv[3 * p:3 * p + 3]) for p in range(npat)]
    os_, lses = [], []
    for d, (qd, kd, vd) in zip(DILATIONS, qkv_res):
        o_d, lse_d = attn_fwd(qd, kd, vd, f"attn_fwd_d{d}_{tag}")
        os_.append(o_d)
        lses.append(lse_d)
    comb = attn_combine(os_, lses, f"attn_combine_{tag}")
    out_b, o_res, lse_res = comb[0], comb[1:1 + npat], comb[1 + npat:]
    outs = resid_matmul([out_a, out_b], w_out, h, mod, 5, 1.0, f"mix_out_{tag}", norm_next)
    hn, om = outs[0], outs[1]
    return hn, (outs[2] if norm_next else None), (h, y, proj, out_a, out_b, o_res, lse_res, qkv_res, om)


def _mixer_bwd(dh, dom, res, ng, mod, w_inT, w_out, sgu, cos, sin, on_grads, next_gate, after, tag):
    lng, lnb, sw, swt, bcol = sgu
    h, y, proj, out_a, out_b, o_res, lse_res, qkv_res, om = res
    D = h.shape[1]
    dmixed = matmul_nt(dom, w_out, f"dmixed_{tag}", after)
    woutbuf = lax.empty((1, 2 * MIX_HALF, D), BF16)
    woutbuf = matmul_tn(out_a, dom, woutbuf, 0, 0, f"dwout_a_{tag}", tmo_cap=MIX_HALF)
    woutbuf = matmul_tn(out_b, dom, woutbuf, 0, MIX_HALF, f"dwout_b_{tag}", tmo_cap=MIX_HALF)
    d_uv, d_sw, d_svec = sgu_bwd(proj, dmixed, lng, lnb, sw, swt, bcol, f"sgu_bwd_{tag}")
    do_res = to_residues(dmixed, 1, f"dout_res_{tag}")
    dqs, dks, dvs = [], [], []
    for p, (d, (qd, kd, vd)) in enumerate(zip(DILATIONS, qkv_res)):
        dq, dk, dv = attn_bwd(qd, kd, vd, do_res[p], o_res[p], lse_res[p], f"attn_bwd_d{d}_{tag}")
        dqs.append(dq)
        dks.append(dk)
        dvs.append(dv)
    d_qkv = rope_bwd(dqs, dks, dvs, cos, sin, f"rope_bwd_{tag}")
    winbuf = lax.empty((1, 5 * MIX_HALF, D), BF16)
    winbuf = matmul_tn(d_uv, y, winbuf, 0, 0, f"dwin_uv_{tag}", tmo_cap=MIX_HALF)
    winbuf = matmul_tn(d_qkv, y, winbuf, 0, 2 * MIX_HALF, f"dwin_qkv_{tag}", tmo_cap=MIX_HALF)
    token, then = on_grads([winbuf, woutbuf])
    pairs = [(d_uv, 0, w_inT, 0, 2 * MIX_HALF), (d_qkv, 0, w_inT, 1, 2 * MIX_HALF), (d_qkv, 2, w_inT, 4, MIX_HALF)]
    outs = dy_normbwd(pairs, h, dh, ng, 1, mod, 4, f"mix_bwd_dy_{tag}", next_gate, [token])
    return outs, d_sw, d_svec, then


def _local_step(x, tgt, mods, ngs, get_w, sgus, gf, on_block_grads, on_layer_small):
    T, D = x.shape
    cos, sin = _rope_tables(T)
    h = x
    saved, weights = [], []
    for l in range(2):
        def getter(blk, l=l):
            return lambda after: get_w(l, blk, after)

        if l == 0:
            y = normmod_fwd(h, ngs[0], 0, mods[0], 0, 1, "normmod_l0f1")
        h, y, r1, wf1 = _ffn_fwd(h, y, mods[l], 0, getter("f1u"), getter("f1d"), (ngs[l], 1, mods[l], 3, 4), f"l{l}f1")
        w_inT, w_out = get_w(l, "mx", h)
        h, y, r2 = _mixer_fwd(h, y, mods[l], (w_inT, (0,)), (w_out, (0,)), sgus[l], cos, sin,
                              (ngs[l], 2, mods[l], 6, 7), f"l{l}mx")
        h, y, r3, wf2 = _ffn_fwd(h, y, mods[l], 6, getter("f2u"), getter("f2d"),
                                 (ngs[l + 1], 0, mods[l + 1], 0, 1) if l + 1 < 2 else None, f"l{l}f2")
        saved.append((r1, r2, r3))
        weights.append((wf1, w_inT, w_out, wf2))
    def gate_of(l, blk):
        r1, r2, r3 = saved[l]
        o, i_g, coef = {"f2": (r3[5], 8, 0.5), "mx": (r2[-1], 5, 1.0), "f1": (r1[5], 2, 0.5)}[blk]
        return o, mods[l], i_g, coef

    seq = [(l, blk) for l in (1, 0) for blk in ("f2", "mx", "f1")]
    dh, red_final, do, red_g = final_loss_bwd(h, gf, tgt, gate_of(*seq[0]), "final_loss_bwd")
    rn, rg = {}, {}
    after = []
    for idx, (l, blk) in enumerate(seq):
        r1, r2, r3 = saved[l]
        wf1, w_inT, w_out, wf2 = weights[l]
        nxt = gate_of(*seq[idx + 1]) if idx + 1 < len(seq) else None
        rg[blk] = red_g
        tag = f"l{l}{blk}"

        def on(arrays, l=l, blk=blk):
            return on_block_grads(l, blk, arrays)

        if blk == "f2":
            outs, then = _ffn_bwd(dh, do, r3, ngs[l], 2, mods[l], 6, *wf2, on, nxt, after, tag)
        elif blk == "mx":
            outs, d_sw, d_svec, then = _mixer_bwd(dh, do, r2, ngs[l], mods[l], (w_inT, (0,)), (w_out, (0,)), sgus[l],
                                                  cos, sin, on, nxt, after, tag)
        else:
            outs, then = _ffn_bwd(dh, do, r1, ngs[l], 0, mods[l], 0, *wf1, on, nxt, after, tag)
        dh, rn[blk] = outs[0], outs[1]
        if nxt is not None:
            do, red_g = outs[2], outs[3]
        if blk == "f1":
            small = on_layer_small(l, dict(sgu_w=d_sw, sgu_vec=d_svec, red_n=(rn["f1"], rn["mx"], rn["f2"]),
                                           red_g=(rg["f1"], rg["mx"], rg["f2"])), red_final if l == 0 else None)
            after = [small, then(small)]
        else:
            after = [then(dh)]
    return dh


def _adam_out(w, g, m, v, name):
    shp = w.shape
    two_d = (-1, shp[-1])
    d, mn, vn = adamw(w.reshape(two_d), g.reshape(two_d), m.reshape(two_d), v.reshape(two_d), name)
    return g, d.reshape(shp), mn.reshape(shp), vn.reshape(shp)


def kernel(x, c, ada_w, ada_b, norm_g, ffn1_wg, ffn1_wu, ffn1_wd, ffn2_wg, ffn2_wu, ffn2_wd, w_in, sgu_ln_g, sgu_ln_b, sgu_w, sgu_b, w_out, final_g, loss_target, m_ada_w, m_ada_b, m_norm_g, m_ffn1_wg, m_ffn1_wu, m_ffn1_wd, m_ffn2_wg, m_ffn2_wu, m_ffn2_wd, m_w_in, m_sgu_ln_g, m_sgu_ln_b, m_sgu_w, m_sgu_b, m_w_out, m_final_g, v_ada_w, v_ada_b, v_norm_g, v_ffn1_wg, v_ffn1_wu, v_ffn1_wd, v_ffn2_wg, v_ffn2_wu, v_ffn2_wd, v_w_in, v_sgu_ln_g, v_sgu_ln_b, v_sgu_w, v_sgu_b, v_w_out, v_final_g):
    T, D = x.shape[1], x.shape[2]
    NL = ada_w.shape[0]
    mx, my, mc = _my_place()
    me = 4 * mx + 2 * my + mc
    ci = 2 * mx + my
    c_idx = jnp.reshape(mc, (1,)).astype(jnp.int32)
    place = jnp.stack([ci, mc]).astype(jnp.int32)

    ngw = norm_g.shape[2]
    small_in = jnp.concatenate([jnp.pad(c, ((0, 7), (0, 0))),
                                jnp.pad(norm_g.reshape(NL * 3, ngw), ((0, 8 - NL * 3), (0, D - ngw)))], axis=0)
    small_all, _ = gather_small(small_in, place, "gather_c_normg")
    c_all = small_all[:, 0, :]
    ng_parts = small_all[0::2, 8:8 + NL * 3, :ngw]
    ngs = jnp.transpose(ng_parts, (1, 0, 2)).reshape(NL, 3, N_CHIP * ngw)

    nmod = ada_w.shape[2]
    ada_b_mine = lax.dynamic_slice_in_dim(ada_b, ci * nmod, nmod, axis=1).reshape(NL, 1, nmod)
    mod_part = ada_fwd(c_all, ada_w, ada_b_mine, "ada_fwd")
    mod_all, _ = gather_small(mod_part.reshape(NL * N_DEV, nmod), place, "gather_mod")
    mod_rows = lax.dynamic_index_in_dim(mod_all.reshape(N_DEV, NL, N_DEV, nmod), me, axis=2, keepdims=False)
    mods = jnp.transpose(mod_rows[0::2], (1, 0, 2)).reshape(NL, N_ADA, D)

    sgus = []
    for l in range(NL):
        sgus.append((sgu_ln_g[l].reshape(1, MIX_HALF), sgu_ln_b[l].reshape(1, MIX_HALF), sgu_w[l],
                     jnp.swapaxes(sgu_w[l], 1, 2), jnp.transpose(sgu_b[l])))

    def halves(a):
        n, r, _ = a.shape
        return a.reshape(n, 2, r // 2, D)

    first_group = halves(jnp.stack([ffn1_wg[0].T, ffn1_wu[0].T], axis=0).astype(BF16))
    first_handle, first_token = first_start(first_group, mods, "first_start")
    zero = first_token[0, 0]
    mods = mods + zero

    def prep(a):
        return (a + zero).astype(BF16)

    groups = []
    for l in range(NL):
        groups += [[halves(jnp.stack([prep(ffn1_wg[l].T), prep(ffn1_wu[l].T)], axis=0))],
                   [halves(prep(ffn1_wd[l])[None])],
                   [halves(prep(w_in[l].T)[None]), halves(prep(w_out[l])[None])],
                   [halves(jnp.stack([prep(ffn2_wg[l].T), prep(ffn2_wu[l].T)], axis=0))],
                   [halves(prep(ffn2_wd[l])[None])]]
    handles, token = gather_start(groups[1:], mods, "gather_start")
    handles = [None] + handles
    mods = mods + token[0, 0]
    group_no = {"f1u": 0, "f1d": 1, "mx": 2, "f2u": 3, "f2d": 4}

    def get_w(l, key, after):
        g = len(group_no) * l + group_no[key]
        if g == 0:
            full = [first_wait(first_forward(first_handle, after, "first_forward"), place, "first_wait")]
        else:
            full = gather_wait(handles[g], after, f"gather_wait_l{l}{key}")
        full = [a.reshape(a.shape[0], N_CHIP * 2 * a.shape[3], D) for a in full]
        return full[0] if key != "mx" else tuple(full)

    def split(a):
        n, r4, _ = a.shape
        return a.reshape(n, N_CHIP, 2, r4 // N_CHIP // 2, D)

    pending, small_pending, small_tokens = {}, {}, {}

    def on_block_grads(l, blk, bufs):
        tag = f"l{l}{blk}"
        sib, tok1 = sibling_start([split(g) for g in bufs], place, f"rs_sibling_start_{tag}")

        def then(after):
            parts, lands = sibling_wait(sib, after, f"rs_sibling_wait_{tag}")
            psums = [sum_halves(g, ld, c_idx, f"rs_sum_halves_{tag}_{i}") for i, (g, ld) in enumerate(zip(parts, lands))]
            pending[(l, blk)], tok2 = scatter_start(psums, lands[0], f"rs_chips_start_{tag}")
            return tok2

        return tok1, then

    def blocks_finish(blocks, after, tag):
        ssums, counts = [], []
        for l, blk in blocks:
            psums, lands2 = scatter_wait(pending.pop((l, blk)), after, f"rs_chips_wait_l{l}{blk}")
            ssums += [sum_chips(p, ld, place, f"rs_sum_chips_l{l}{blk}_{i}") for i, (p, ld) in enumerate(zip(psums, lands2))]
            counts.append(len(psums))
        fins = [f.reshape(f.shape[0], -1, D) for f in sibling_complete(ssums, f"rs_complete_{tag}")]
        out, i = [], 0
        for n in counts:
            out.append(fins[i:i + n])
            i += n
        return out

    def on_layer_small(l, grads, red_final):
        blocks = list(grads["red_n"]) + list(grads["red_g"])
        blocks.append(jnp.pad(grads["sgu_vec"], ((0, 0), (0, D - MIX_HALF))))
        blocks.append(grads["sgu_w"].reshape(-1, D))
        if red_final is not None:
            blocks.append(red_final)
        xs = jnp.concatenate(blocks, axis=0)
        small_pending[l], small_tokens[l] = small_start(xs, place, f"small_start_l{l}")
        return small_tokens[l]

    grad_x = _local_step(x[0], loss_target[0], mods, ngs, get_w, sgus, final_g.reshape(1, D),
                         on_block_grads, on_layer_small)

    adam_state = {}

    def adam_big(nm, l, g, w, m, v):
        adam_state[nm] = adamw_layer(w, g, m, v, l, adam_state.get(nm), f"adamw_{nm}_l{l}")

    def adam_block(l, blk, fin):
        if blk == "mx":
            adam_big("w_in", l, fin[0][0].T, w_in, m_w_in, v_w_in)
            adam_big("w_out", l, fin[1][0], w_out, m_w_out, v_w_out)
        else:
            ws = ((ffn1_wg, m_ffn1_wg, v_ffn1_wg), (ffn1_wu, m_ffn1_wu, v_ffn1_wu), (ffn1_wd, m_ffn1_wd, v_ffn1_wd)) \
                if blk == "f1" else \
                ((ffn2_wg, m_ffn2_wg, v_ffn2_wg), (ffn2_wu, m_ffn2_wu, v_ffn2_wu), (ffn2_wd, m_ffn2_wd, v_ffn2_wd))
            pre = "ffn1" if blk == "f1" else "ffn2"
            for k, (nm, tr) in enumerate((("wg", True), ("wu", True), ("wd", False))):
                adam_big(f"{pre}_{nm}", l, fin[0][k], *[jnp.swapaxes(t, 1, 2) if tr else t for t in ws[k]])

    done_order = [(l, blk) for l in range(NL - 1, -1, -1) for blk in ("f2", "mx", "f1")]
    for (l, blk), fin in zip(done_order[:-1], blocks_finish(done_order[:-1], small_tokens[0], "early")):
        adam_block(l, blk, fin)
    last_big = adam_state["w_out"][1]

    small_sum, small_all = [], []
    for l in range(NL):
        xs, land = small_wait(small_pending[l], last_big, f"small_wait_l{l}")
        full = lax.dynamic_update_slice(land, xs[None], (me, 0, 0))
        small_all.append(full)
        small_sum.append(sum_slots(full, f"small_sum_l{l}"))
    offs = [8 * i for i in range(8)]
    off_final = offs[7] + SGU_HEADS * ATT_BLOCK * HEAD_LANES // D
    loss = small_sum[0][off_final + 1, 0]
    g_final_g = small_sum[0][off_final, :]
    g_norm_g, g_ada_b, g_lng, g_lnb, g_sb, g_sw, dmod_all = [], [], [], [], [], [], []
    for l in range(NL):
        rn = [small_sum[l][offs[i]:offs[i] + 8] for i in range(3)]
        rg = [small_sum[l][offs[3 + i]:offs[3 + i] + 8] for i in range(3)]
        g_norm_g.append(jnp.stack([rn[i][2] for i in range(3)], axis=0))
        g_ada_b.append(jnp.concatenate([jnp.stack([rn[i][0], rn[i][1], rg[i][0]], axis=0) for i in range(3)],
                                       axis=0).reshape(N_ADA * D))
        sv = small_sum[l][offs[6]:offs[6] + 8, :MIX_HALF]
        g_lng.append(sv[0].reshape(SGU_HEADS, HEAD_LANES))
        g_lnb.append(sv[1].reshape(SGU_HEADS, HEAD_LANES))
        g_sb.append(sv[2].reshape(SGU_HEADS, ATT_BLOCK))
        g_sw.append(small_sum[l][offs[7]:off_final].reshape(sgu_w.shape[1:]))
        rows = []
        for i in range(3):
            an = small_all[l][:, offs[i]:offs[i] + 2]
            ag = small_all[l][:, offs[3 + i]:offs[3 + i] + 1]
            rows += [an[:, 0], an[:, 1], ag[:, 0]]
        dmod_all.append(jnp.stack(rows, axis=1).reshape(N_DEV, N_ADA * D))
    dmod_all = jnp.stack(dmod_all, axis=0)
    dmod_mine = lax.dynamic_slice_in_dim(dmod_all, ci * nmod, nmod, axis=2)
    g_ada_w = ada_bwd(jnp.transpose(c_all), dmod_mine, "ada_bwd")
    g_ada_b = jnp.stack(g_ada_b, axis=0)
    g_norm_g_full = jnp.stack(g_norm_g, axis=0)
    g_norm_g_mine = lax.dynamic_slice_in_dim(g_norm_g_full, ci * ngw, ngw, axis=2)

    small_params = [
        ("ada_w", ada_w, g_ada_w, m_ada_w, v_ada_w),
        ("ada_b", ada_b, g_ada_b, m_ada_b, v_ada_b),
        ("norm_g", norm_g, g_norm_g_mine, m_norm_g, v_norm_g),
        ("sgu_ln_g", sgu_ln_g, jnp.stack(g_lng, axis=0), m_sgu_ln_g, v_sgu_ln_g),
        ("sgu_ln_b", sgu_ln_b, jnp.stack(g_lnb, axis=0), m_sgu_ln_b, v_sgu_ln_b),
        ("sgu_w", sgu_w, jnp.stack(g_sw, axis=0), m_sgu_w, v_sgu_w),
        ("sgu_b", sgu_b, jnp.stack(g_sb, axis=0), m_sgu_b, v_sgu_b),
        ("final_g", final_g.reshape(1, D), g_final_g.reshape(1, D), m_final_g.reshape(1, D), v_final_g.reshape(1, D)),
    ]
    for nm, w, g, m, v in small_params:
        res = _adam_out(w, g, m, v, f"adamw_{nm}")
        adam_state[nm] = tuple(t.reshape(D) for t in res) if nm == "final_g" else res

    l, blk = done_order[-1]
    adam_block(l, blk, blocks_finish([(l, blk)], [st[1] for st in adam_state.values()], "last")[0])

    names = ["ada_w", "ada_b", "norm_g", "ffn1_wg", "ffn1_wu", "ffn1_wd", "ffn2_wg", "ffn2_wu", "ffn2_wd", "w_in",
             "sgu_ln_g", "sgu_ln_b", "sgu_w", "sgu_b", "w_out", "final_g"]
    shapes = [t.shape for t in (ada_w, ada_b, norm_g, ffn1_wg, ffn1_wu, ffn1_wd, ffn2_wg, ffn2_wu, ffn2_wd, w_in,
                                sgu_ln_g, sgu_ln_b, sgu_w, sgu_b, w_out, final_g)]
    def shaped(nm, t, s):
        if nm in ("ffn1_wg", "ffn1_wu", "ffn2_wg", "ffn2_wu"):
            return jnp.swapaxes(t.reshape(s[0], s[2], s[1]), 1, 2)
        return t.reshape(s)

    return (loss, grad_x[None], *[shaped(nm, adam_state[nm][i], s) for i in range(4) for nm, s in zip(names, shapes)])
```

```python
import math

import jax
import jax.numpy as jnp
from jax import lax
from jax.experimental import pallas as pl
from jax.experimental.pallas import tpu as pltpu

F32 = jnp.float32
BF16 = jnp.bfloat16
EPS = 1e-6
SGU_HEADS = 4
HEAD_LANES = 128
ATT_DH = 64
ATT_BLOCK = 128
MIX_HALF = SGU_HEADS * HEAD_LANES
DILATIONS = (1, 4, 16)
ROPE_THETA = 10000.0
N_ADA = 9
ADAM_LR, ADAM_B1, ADAM_B2, ADAM_EPS, ADAM_WD, ADAM_STEP = 0.001, 0.9, 0.999, 1e-08, 0.01, 10
NEG = -1e30
V7X_VMEM_BYTES = 64 * 1024 * 1024
VMEM_LIMIT = V7X_VMEM_BYTES * 7 // 8
MESH = pl.DeviceIdType.MESH
N_DEV = 8
N_CHIP = 4
_ANY = pl.BlockSpec(memory_space=pl.ANY)


def _tile(n, cap, mult):
    if n <= cap:
        return n
    t = (cap // mult) * mult
    while t >= mult:
        if n % t == 0:
            return t
        t -= mult
    raise ValueError((n, cap, mult))


def _params(dims=None):
    return pltpu.CompilerParams(dimension_semantics=dims, vmem_limit_bytes=VMEM_LIMIT)


def _wspec(w, rows, idx, resident=False):
    arr, lead = w
    kw = dict(pipeline_mode=pl.Buffered(1)) if resident else {}
    return pl.BlockSpec((None,) * len(lead) + (rows, arr.shape[-1]), lambda *g: tuple(lead) + (idx(*g), 0), **kw)


def _wrows(w):
    return w[0].shape[-2]


def _nt(a, b):
    return lax.dot_general(a, b, (((1,), (1,)), ((), ())), preferred_element_type=F32)


def _tn(a, b):
    return lax.dot_general(a, b, (((0,), (0,)), ((), ())), preferred_element_type=F32)


def _nn(a, b):
    return jnp.dot(a, b, preferred_element_type=F32)


def _sigmoid(x):
    return 0.5 * jnp.tanh(0.5 * x) + 0.5


_GELU_K = math.sqrt(2.0 / math.pi)
_GELU_C = 0.044715


def _gelu(x):
    t = jnp.tanh(_GELU_K * (x + _GELU_C * x * x * x))
    return 0.5 * x * (1.0 + t)


def _gelu_and_grad(x):
    x2 = x * x
    t = jnp.tanh(_GELU_K * (x + _GELU_C * x * x2))
    g = 0.5 * x * (1.0 + t)
    dg = 0.5 * (1.0 + t) + 0.5 * x * (1.0 - t * t) * (_GELU_K * (1.0 + 3.0 * _GELU_C * x2))
    return g, dg


def normmod_fwd(h, ng, i_n, mod, i_sh, i_sc, name):
    T, D = h.shape
    tm = _tile(T, 512, 8)

    def body(h_ref, ng_ref, mod_ref, y_ref):
        y_ref[...] = _normmod(h_ref[...], ng_ref[i_n:i_n + 1, :], mod_ref[i_sh:i_sh + 1, :],
                              mod_ref[i_sc:i_sc + 1, :]).astype(BF16)

    return pl.pallas_call(
        body, name=name, grid=(T // tm,),
        in_specs=[pl.BlockSpec((tm, D), lambda i: (i, 0)),
                  pl.BlockSpec(ng.shape, lambda i: (0, 0)),
                  pl.BlockSpec(mod.shape, lambda i: (0, 0))],
        out_specs=pl.BlockSpec((tm, D), lambda i: (i, 0)),
        out_shape=jax.ShapeDtypeStruct((T, D), BF16),
        compiler_params=_params(("parallel",)),
    )(h, ng, mod)


def ffn_up(y, wgT, wuT, name):
    T, D = y.shape
    F = _wrows(wgT)
    tm = _tile(T, 512, 16)
    tf = _tile(F, 2816, 256)
    cuts = list(range(0, tf, 768)) + [tf]

    def body(y_ref, wg_ref, wu_ref, p_ref, q_ref, s_ref):
        yv = y_ref[...]
        for c0, c1 in zip(cuts[:-1], cuts[1:]):
            a = _nt(yv, wg_ref[c0:c1, :])
            b = _nt(yv, wu_ref[c0:c1, :])
            sig = _sigmoid(a)
            q = a * sig
            p_ref[:, c0:c1] = (b * (sig + q * (1.0 - sig))).astype(BF16)
            q_ref[:, c0:c1] = q.astype(BF16)
            s_ref[:, c0:c1] = (q * b).astype(BF16)

    act = jax.ShapeDtypeStruct((T, F), BF16)
    return pl.pallas_call(
        body, name=name, grid=(F // tf, T // tm),
        in_specs=[pl.BlockSpec((tm, D), lambda j, i: (i, 0)),
                  _wspec(wgT, tf, lambda j, i: j, resident=True),
                  _wspec(wuT, tf, lambda j, i: j, resident=True)],
        out_specs=[pl.BlockSpec((tm, tf), lambda j, i: (i, j))] * 3,
        out_shape=[act, act, act],
        compiler_params=_params(("parallel", "parallel")),
    )(y, wgT[0], wuT[0])


def _normmod(x, gn, sh, sc):
    r = lax.rsqrt(jnp.mean(x * x, axis=-1, keepdims=True) + EPS)
    return ((x * r) * gn) * (1.0 + sc) + sh


def resid_matmul(xs, w, h, mod, i_g, coef, name, norm_next=None):
    T, D = h.shape
    kb = xs[0].shape[1]
    assert all(x.shape == (T, kb) for x in xs) and _wrows(w) == kb * len(xs)
    tm = _tile(T, 1024, 16)
    nx = len(xs)
    n_in, n_out, n_shape, n_ops = [], [], [], []
    if norm_next:
        ng_n, i_n, mod_n, i_sh, i_sc = norm_next
        n_in = [pl.BlockSpec(ng_n.shape, lambda i: (0, 0)), pl.BlockSpec(mod_n.shape, lambda i: (0, 0))]
        n_out = [pl.BlockSpec((tm, D), lambda i: (i, 0))]
        n_shape = [jax.ShapeDtypeStruct((T, D), BF16)]
        n_ops = [ng_n, mod_n]

    def body(*refs):
        x_refs, w_refs = refs[:nx], refs[nx:2 * nx]
        h_ref, mod_ref = refs[2 * nx:2 * nx + 2]
        hn_ref, o_ref = refs[2 * nx + 2 + len(n_in):2 * nx + 4 + len(n_in)]
        o = _nn(x_refs[0][...], w_refs[0][...])
        for xr, wr in zip(x_refs[1:], w_refs[1:]):
            o = o + _nn(xr[...], wr[...])
        o_ref[...] = o.astype(BF16)
        hn = h_ref[...] + (coef * mod_ref[i_g:i_g + 1, :]) * o
        hn_ref[...] = hn
        if norm_next:
            ng_ref, modn_ref = refs[2 * nx + 2], refs[2 * nx + 3]
            refs[-1][...] = _normmod(hn, ng_ref[i_n:i_n + 1, :], modn_ref[i_sh:i_sh + 1, :],
                                     modn_ref[i_sc:i_sc + 1, :]).astype(BF16)

    return pl.pallas_call(
        body, name=name, grid=(T // tm,),
        in_specs=([pl.BlockSpec((tm, kb), lambda i: (i, 0))] * nx
                  + [_wspec(w, kb, lambda i, p=p: p, resident=True) for p in range(nx)]
                  + [pl.BlockSpec((tm, D), lambda i: (i, 0)),
                     pl.BlockSpec(mod.shape, lambda i: (0, 0))] + n_in),
        out_specs=[pl.BlockSpec((tm, D), lambda i: (i, 0))] * 2 + n_out,
        out_shape=[jax.ShapeDtypeStruct((T, D), F32), jax.ShapeDtypeStruct((T, D), BF16)] + n_shape,
        compiler_params=_params(("parallel",)),
    )(*xs, *([w[0]] * nx), h, mod, *n_ops)


def _gate_specs(gate, tm, D):
    o, mod, _, _ = gate
    T = o.shape[0]
    return ([pl.BlockSpec((tm, D), lambda i: (i, 0)), pl.BlockSpec(mod.shape, lambda i: (0, 0))],
            [pl.BlockSpec((tm, D), lambda i: (i, 0)), pl.BlockSpec((8, D), lambda i: (0, 0))],
            [jax.ShapeDtypeStruct((T, D), BF16), jax.ShapeDtypeStruct((8, D), F32)],
            [o, mod])


def _gate_emit(d, gate, o_ref, mod_ref, do_ref, red_ref):
    _, _, i_g, coef = gate
    do_ref[...] = (d * (coef * mod_ref[i_g:i_g + 1, :])).astype(BF16)

    @pl.when(pl.program_id(0) == 0)
    def _():
        red_ref[...] = jnp.zeros_like(red_ref)

    red_ref[0:1, :] += coef * jnp.sum(d * o_ref[...].astype(F32), axis=0, keepdims=True)


def ffn_bwd_mid(do, wd, p, q, name, after=()):
    T, D = do.shape
    F = _wrows(wd)
    tm = _tile(T, 512, 16)
    tf = _tile(F, 2816, 256)
    cuts = list(range(0, tf, 256)) + [tf]

    def body(do_ref, wd_ref, p_ref, q_ref, *rest):
        da_ref, db_ref = rest[-2:]
        dov = do_ref[...]
        for c0, c1 in zip(cuts[:-1], cuts[1:]):
            ds = _nt(dov, wd_ref[c0:c1, :])
            da_ref[:, c0:c1] = (ds * p_ref[:, c0:c1].astype(F32)).astype(BF16)
            db_ref[:, c0:c1] = (ds * q_ref[:, c0:c1].astype(F32)).astype(BF16)

    act = jax.ShapeDtypeStruct((T, F), BF16)
    return pl.pallas_call(
        body, name=name, grid=(F // tf, T // tm),
        in_specs=[pl.BlockSpec((tm, D), lambda j, i: (i, 0)),
                  _wspec(wd, tf, lambda j, i: j, resident=True),
                  pl.BlockSpec((tm, tf), lambda j, i: (i, j)),
                  pl.BlockSpec((tm, tf), lambda j, i: (i, j))] + [_ANY] * len(after),
        out_specs=[pl.BlockSpec((tm, tf), lambda j, i: (i, j))] * 2,
        out_shape=[act, act],
        compiler_params=_params(("parallel", "parallel")),
    )(do, wd[0], p, q, *after)


def dy_normbwd(pairs, h, dhp, ng, i_n, mod, i_sc, name, gate=None, after=()):
    T, D = h.shape
    tm = _tile(T, 512, 16)
    npair = len(pairs)
    g_in, g_out, g_shape, g_ops = _gate_specs(gate, tm, D) if gate else ([], [], [], [])
    n_in = 2 * npair + 4 + len(g_in) + len(after)

    def body(*refs):
        x_refs, w_refs = refs[:npair], refs[npair:2 * npair]
        h_ref, dhp_ref, ng_ref, mod_ref = refs[2 * npair:2 * npair + 4]
        dh_ref, red_ref = refs[n_in:n_in + 2]
        dy = _nn(x_refs[0][...], w_refs[0][...])
        for xr, wr in zip(x_refs[1:], w_refs[1:]):
            dy = dy + _nn(xr[...], wr[...])
        x = h_ref[...]
        r = lax.rsqrt(jnp.mean(x * x, axis=-1, keepdims=True) + EPS)
        n = x * r
        gn = ng_ref[i_n:i_n + 1, :]
        dnh = dy * (1.0 + mod_ref[i_sc:i_sc + 1, :])

        @pl.when(pl.program_id(0) == 0)
        def _():
            red_ref[...] = jnp.zeros_like(red_ref)

        red_ref[0:1, :] += jnp.sum(dy, axis=0, keepdims=True)
        red_ref[1:2, :] += jnp.sum(dy * (n * gn), axis=0, keepdims=True)
        red_ref[2:3, :] += jnp.sum(dnh * n, axis=0, keepdims=True)
        dn = dnh * gn
        dh_new = dhp_ref[...] + r * (dn - n * jnp.mean(dn * n, axis=-1, keepdims=True))
        dh_ref[...] = dh_new
        if gate:
            _gate_emit(dh_new, gate, refs[2 * npair + 4], refs[2 * npair + 5], refs[-2], refs[-1])

    in_specs = ([pl.BlockSpec((tm, kb), lambda i, c=c: (i, c)) for (_, c, _, _, kb) in pairs]
                + [_wspec(w, kb, lambda i, r=r: r, resident=True) for (_, _, w, r, kb) in pairs]
                + [pl.BlockSpec((tm, D), lambda i: (i, 0)),
                   pl.BlockSpec((tm, D), lambda i: (i, 0)),
                   pl.BlockSpec(ng.shape, lambda i: (0, 0)),
                   pl.BlockSpec(mod.shape, lambda i: (0, 0))] + g_in + [_ANY] * len(after))
    return pl.pallas_call(
        body, name=name, grid=(T // tm,), in_specs=in_specs,
        out_specs=[pl.BlockSpec((tm, D), lambda i: (i, 0)), pl.BlockSpec((8, D), lambda i: (0, 0))] + g_out,
        out_shape=[jax.ShapeDtypeStruct((T, D), F32), jax.ShapeDtypeStruct((8, D), F32)] + g_shape,
        compiler_params=_params(("arbitrary",)),
    )(*[p[0] for p in pairs], *[p[2][0] for p in pairs], h, dhp, ng, mod, *g_ops, *after)


def matmul_tn(a, b, buf, slot, row0, name, tmo_cap=1408):
    T, N = b.shape
    ma = a.shape[1]
    tmo = _tile(ma, tmo_cap, 128)
    assert row0 % tmo == 0
    nmo = ma // tmo
    tk = _tile(T, 2048, 16)
    nk = T // tk

    def body(a_ref, b_ref, buf_ref, o_ref, acc_ref):
        k = pl.program_id(1)

        @pl.when(k == 0)
        def _():
            acc_ref[...] = jnp.zeros_like(acc_ref)

        acc_ref[...] += _tn(a_ref[...], b_ref[...])

        @pl.when(k == nk - 1)
        def _():
            o_ref[...] = acc_ref[...].astype(BF16)

    return pl.pallas_call(
        body, name=name, grid=(nmo, nk),
        in_specs=[pl.BlockSpec((tk, tmo), lambda j, k: (k, j)),
                  pl.BlockSpec((tk, N), lambda j, k: (k, 0)),
                  pl.BlockSpec(memory_space=pl.ANY)],
        out_specs=pl.BlockSpec((None, tmo, N), lambda j, k: (slot, row0 // tmo + j, 0)),
        out_shape=jax.ShapeDtypeStruct(buf.shape, BF16),
        scratch_shapes=[pltpu.VMEM((tmo, N), F32)],
        input_output_aliases={2: 0},
        compiler_params=_params(("parallel", "arbitrary")),
    )(a, b, buf)


def matmul_nt(x, w, name, after=()):
    T, K = x.shape
    N = _wrows(w)
    tm = _tile(T, 1024, 16)
    tn = _tile(N, 1280, 128)

    def body(x_ref, w_ref, *rest):
        rest[-1][...] = _nt(x_ref[...], w_ref[...]).astype(BF16)

    return pl.pallas_call(
        body, name=name, grid=(N // tn, T // tm),
        in_specs=[pl.BlockSpec((tm, K), lambda j, i: (i, 0)), _wspec(w, tn, lambda j, i: j)] + [_ANY] * len(after),
        out_specs=pl.BlockSpec((tm, tn), lambda j, i: (i, j)),
        out_shape=jax.ShapeDtypeStruct((T, N), BF16),
        compiler_params=_params(("parallel", "parallel")),
    )(x, w[0], *after)


def _sgu_head_fwd(u, v, lng, lnb):
    gu, dgu = _gelu_and_grad(u)
    gv, dgv = _gelu_and_grad(v)
    mu = jnp.mean(gv, axis=-1, keepdims=True)
    xc = gv - mu
    rstd = lax.rsqrt(jnp.mean(xc * xc, axis=-1, keepdims=True) + EPS)
    xhat = xc * rstd
    vn = xhat * lng + lnb
    return gu, dgu, dgv, rstd, xhat, vn


def _tril_mask():
    r = lax.broadcasted_iota(jnp.int32, (ATT_BLOCK, ATT_BLOCK), 0)
    c = lax.broadcasted_iota(jnp.int32, (ATT_BLOCK, ATT_BLOCK), 1)
    return c <= r


def _triu_mask():
    r = lax.broadcasted_iota(jnp.int32, (ATT_BLOCK, ATT_BLOCK), 0)
    c = lax.broadcasted_iota(jnp.int32, (ATT_BLOCK, ATT_BLOCK), 1)
    return r <= c


def sgu_fwd(proj, lng, lnb, w, bcol, name):
    T = proj.shape[0]
    tm = _tile(T, 512, 128)
    nch = tm // ATT_BLOCK

    def body(u_ref, v_ref, lng_ref, lnb_ref, w_ref, b_ref, o_ref):
        tril = _tril_mask()
        for hd in range(SGU_HEADS):
            sl = slice(hd * HEAD_LANES, (hd + 1) * HEAD_LANES)
            u = u_ref[:, sl].astype(F32)
            v = v_ref[:, sl].astype(F32)
            gu, _, _, _, _, vn = _sgu_head_fwd(u, v, lng_ref[:, sl], lnb_ref[:, sl])
            wm = jnp.where(tril, w_ref[hd], 0.0).astype(BF16)
            vnb = vn.astype(BF16)
            bc = b_ref[:, hd:hd + 1]
            for ch in range(nch):
                rs = slice(ch * ATT_BLOCK, (ch + 1) * ATT_BLOCK)
                z = _nn(wm, vnb[rs, :]) + bc
                o_ref[rs, sl] = (gu[rs, :] * z).astype(BF16)

    return pl.pallas_call(
        body, name=name, grid=(T // tm,),
        in_specs=[pl.BlockSpec((tm, MIX_HALF), lambda i: (i, 0)),
                  pl.BlockSpec((tm, MIX_HALF), lambda i: (i, 1)),
                  pl.BlockSpec((1, MIX_HALF), lambda i: (0, 0)),
                  pl.BlockSpec((1, MIX_HALF), lambda i: (0, 0)),
                  pl.BlockSpec(w.shape, lambda i: (0, 0, 0)),
                  pl.BlockSpec(bcol.shape, lambda i: (0, 0))],
        out_specs=pl.BlockSpec((tm, MIX_HALF), lambda i: (i, 0)),
        out_shape=jax.ShapeDtypeStruct((T, MIX_HALF), BF16),
        compiler_params=_params(("parallel",)),
    )(proj, proj, lng, lnb, w, bcol)


def sgu_bwd(proj, dmixed, lng, lnb, w, wt, bcol, name):
    T = proj.shape[0]
    tm = _tile(T, 512, 128)
    nch = tm // ATT_BLOCK
    nsteps = T // tm

    def body(u_ref, v_ref, g_ref, lng_ref, lnb_ref, w_ref, wt_ref, b_ref, duv_ref, dw_ref, dvec_ref, bacc_ref):
        step = pl.program_id(0)

        @pl.when(step == 0)
        def _():
            dw_ref[...] = jnp.zeros_like(dw_ref)
            dvec_ref[...] = jnp.zeros_like(dvec_ref)
            bacc_ref[...] = jnp.zeros_like(bacc_ref)

        tril = _tril_mask()
        triu = _triu_mask()
        for hd in range(SGU_HEADS):
            sl = slice(hd * HEAD_LANES, (hd + 1) * HEAD_LANES)
            u = u_ref[:, sl].astype(F32)
            v = v_ref[:, sl].astype(F32)
            lng_h = lng_ref[:, sl]
            gu, dgu, dgv, rstd, xhat, vn = _sgu_head_fwd(u, v, lng_h, lnb_ref[:, sl])
            wm = jnp.where(tril, w_ref[hd], 0.0).astype(BF16)
            wmt = jnp.where(triu, wt_ref[hd], 0.0).astype(BF16)
            vnb = vn.astype(BF16)
            bc = b_ref[:, hd:hd + 1]
            g = g_ref[:, sl].astype(F32)
            dw_acc = jnp.zeros((ATT_BLOCK, ATT_BLOCK), F32)
            b_acc = jnp.zeros((ATT_BLOCK, HEAD_LANES), F32)
            dvn_parts = []
            for ch in range(nch):
                rs = slice(ch * ATT_BLOCK, (ch + 1) * ATT_BLOCK)
                z = _nn(wm, vnb[rs, :]) + bc
                duv_ref[rs, sl] = (g[rs, :] * z * dgu[rs, :]).astype(BF16)
                dz = g[rs, :] * gu[rs, :]
                dzb = dz.astype(BF16)
                dvn_parts.append(_nn(wmt, dzb))
                dw_acc = dw_acc + _nt(dzb, vnb[rs, :])
                b_acc = b_acc + dz
            dvn = jnp.concatenate(dvn_parts, axis=0)
            dw_ref[hd] += jnp.where(tril, dw_acc, 0.0)
            bacc_ref[hd] += b_acc
            dvec_ref[0:1, sl] += jnp.sum(dvn * xhat, axis=0, keepdims=True)
            dvec_ref[1:2, sl] += jnp.sum(dvn, axis=0, keepdims=True)
            dxh = dvn * lng_h
            dgv_in = rstd * (dxh - jnp.mean(dxh, axis=-1, keepdims=True)
                             - xhat * jnp.mean(dxh * xhat, axis=-1, keepdims=True))
            duv_ref[:, MIX_HALF + hd * HEAD_LANES:MIX_HALF + (hd + 1) * HEAD_LANES] = (dgv_in * dgv).astype(BF16)

        @pl.when(step == nsteps - 1)
        def _():
            for hd in range(SGU_HEADS):
                sl = slice(hd * HEAD_LANES, (hd + 1) * HEAD_LANES)
                dvec_ref[2:3, sl] = jnp.sum(bacc_ref[hd].T, axis=0, keepdims=True)

    return pl.pallas_call(
        body, name=name, grid=(nsteps,),
        in_specs=[pl.BlockSpec((tm, MIX_HALF), lambda i: (i, 0)),
                  pl.BlockSpec((tm, MIX_HALF), lambda i: (i, 1)),
                  pl.BlockSpec((tm, MIX_HALF), lambda i: (i, 0)),
                  pl.BlockSpec((1, MIX_HALF), lambda i: (0, 0)),
                  pl.BlockSpec((1, MIX_HALF), lambda i: (0, 0)),
                  pl.BlockSpec(w.shape, lambda i: (0, 0, 0)),
                  pl.BlockSpec(w.shape, lambda i: (0, 0, 0)),
                  pl.BlockSpec(bcol.shape, lambda i: (0, 0))],
        out_specs=[pl.BlockSpec((tm, 2 * MIX_HALF), lambda i: (i, 0)),
                   pl.BlockSpec(w.shape, lambda i: (0, 0, 0)),
                   pl.BlockSpec((8, MIX_HALF), lambda i: (0, 0))],
        out_shape=[jax.ShapeDtypeStruct((T, 2 * MIX_HALF), BF16),
                   jax.ShapeDtypeStruct(w.shape, F32),
                   jax.ShapeDtypeStruct((8, MIX_HALF), F32)],
        scratch_shapes=[pltpu.VMEM((SGU_HEADS, ATT_BLOCK, HEAD_LANES), F32)],
        compiler_params=_params(("arbitrary",)),
    )(proj, proj, dmixed, lng, lnb, w, wt, bcol)


def _rot_half(t):
    lane = lax.broadcasted_iota(jnp.int32, t.shape, 1)
    first = (lane % ATT_DH) < (ATT_DH // 2)
    return jnp.where(first, -pltpu.roll(t, HEAD_LANES - ATT_DH // 2, 1), pltpu.roll(t, ATT_DH // 2, 1))


LAYOUT_ROWS = 512


def _res_spec(d, tm, W):
    return pl.BlockSpec((d, tm // d, W), lambda i: (0, i, 0))


def _res_shape(d, T, W, dtype):
    return jax.ShapeDtypeStruct((d, T // d, W), dtype)


def _slab_buf(tm, W):
    return pltpu.VMEM((W // HEAD_LANES, tm, HEAD_LANES), F32)


def _lanes(hp):
    return slice(hp * HEAD_LANES, (hp + 1) * HEAD_LANES)


def _to_res(buf, out_ref, d, dtype):
    nslab, tm, _ = buf.shape
    for hp in range(nslab):
        if d == 1:
            out_ref[0, :, _lanes(hp)] = buf[hp].astype(dtype)
        else:
            for r in range(d):
                out_ref[r, :, _lanes(hp)] = buf.at[hp][pl.ds(r, tm // d, stride=d), :].astype(dtype)


def _from_res(in_ref, buf, d):
    nslab, tm, _ = buf.shape
    for hp in range(nslab):
        if d == 1:
            buf[hp] = in_ref[0, :, _lanes(hp)]
        else:
            for r in range(d):
                buf.at[hp][pl.ds(r, tm // d, stride=d), :] = in_ref[r, :, _lanes(hp)]


def rope_fwd(proj, cos, sin, name):
    T = proj.shape[0]
    tm = LAYOUT_ROWS
    scale = 1.0 / math.sqrt(ATT_DH)
    nd = len(DILATIONS)

    def body(q_ref, k_ref, v_ref, cos_ref, sin_ref, *rest):
        outs, buf = rest[:3 * nd], rest[3 * nd]
        c = cos_ref[...]
        s = sin_ref[...]
        for which, src in enumerate((q_ref, k_ref, v_ref)):
            for hp in range(MIX_HALF // HEAD_LANES):
                t = src[:, _lanes(hp)].astype(F32)
                if which == 0:
                    t = scale * (t * c + _rot_half(t) * s)
                elif which == 1:
                    t = t * c + _rot_half(t) * s
                buf[hp] = t
            for di, d in enumerate(DILATIONS):
                _to_res(buf, outs[3 * di + which], d, BF16)

    return pl.pallas_call(
        body, name=name, grid=(T // tm,),
        in_specs=[pl.BlockSpec((tm, MIX_HALF), lambda i: (i, 2)),
                  pl.BlockSpec((tm, MIX_HALF), lambda i: (i, 3)),
                  pl.BlockSpec((tm, MIX_HALF), lambda i: (i, 4)),
                  pl.BlockSpec((tm, HEAD_LANES), lambda i: (i, 0)),
                  pl.BlockSpec((tm, HEAD_LANES), lambda i: (i, 0))],
        out_specs=[_res_spec(d, tm, MIX_HALF) for d in DILATIONS for _ in range(3)],
        out_shape=[_res_shape(d, T, MIX_HALF, BF16) for d in DILATIONS for _ in range(3)],
        scratch_shapes=[_slab_buf(tm, MIX_HALF)],
        compiler_params=_params(("parallel",)),
    )(proj, proj, proj, cos, sin)


def to_residues(x, col, name):
    T = x.shape[0]
    tm = LAYOUT_ROWS

    def body(x_ref, *rest):
        outs, buf = rest[:-1], rest[-1]
        for hp in range(MIX_HALF // HEAD_LANES):
            buf[hp] = x_ref[:, _lanes(hp)].astype(F32)
        for o_ref, d in zip(outs, DILATIONS):
            _to_res(buf, o_ref, d, BF16)

    return pl.pallas_call(
        body, name=name, grid=(T // tm,),
        in_specs=[pl.BlockSpec((tm, MIX_HALF), lambda i: (i, col))],
        out_specs=[_res_spec(d, tm, MIX_HALF) for d in DILATIONS],
        out_shape=[_res_shape(d, T, MIX_HALF, BF16) for d in DILATIONS],
        scratch_shapes=[_slab_buf(tm, MIX_HALF)],
        compiler_params=_params(("parallel",)),
    )(x)


def rope_bwd(dqs, dks, dvs, cos, sin, name):
    T = dqs[0].shape[0] * dqs[0].shape[1]
    tm = LAYOUT_ROWS
    scale = 1.0 / math.sqrt(ATT_DH)
    npat = len(dqs)

    def body(*refs):
        groups = refs[:npat], refs[npat:2 * npat], refs[2 * npat:3 * npat]
        cos_ref, sin_ref, o_ref, buf, acc = refs[3 * npat:]
        c = cos_ref[...]
        s = sin_ref[...]
        for which, g_refs in enumerate(groups):
            _from_res(g_refs[0], acc, DILATIONS[0])
            for g_ref, d in zip(g_refs[1:], DILATIONS[1:]):
                _from_res(g_ref, buf, d)
                acc[...] += buf[...]
            for hp in range(MIX_HALF // HEAD_LANES):
                g = acc[hp]
                if which == 0:
                    g = scale * g
                if which < 2:
                    g = g * c - _rot_half(g * s)
                o_ref[:, which * MIX_HALF + hp * HEAD_LANES:which * MIX_HALF + (hp + 1) * HEAD_LANES] = g.astype(BF16)

    return pl.pallas_call(
        body, name=name, grid=(T // tm,),
        in_specs=([_res_spec(d, tm, MIX_HALF) for _ in range(3) for d in DILATIONS]
                  + [pl.BlockSpec((tm, HEAD_LANES), lambda i: (i, 0))] * 2),
        out_specs=pl.BlockSpec((tm, 3 * MIX_HALF), lambda i: (i, 0)),
        out_shape=jax.ShapeDtypeStruct((T, 3 * MIX_HALF), BF16),
        scratch_shapes=[_slab_buf(tm, MIX_HALF), _slab_buf(tm, MIX_HALF)],
        compiler_params=_params(("parallel",)),
    )(*dqs, *dks, *dvs, cos, sin)


def _band_masks(n):
    r = lax.broadcasted_iota(jnp.int32, (2 * ATT_BLOCK, ATT_BLOCK), 0)
    c = lax.broadcasted_iota(jnp.int32, (2 * ATT_BLOCK, ATT_BLOCK), 1)
    qi = r % ATT_BLOCK
    head = (c < ATT_DH) == (r < ATT_BLOCK)
    return (c >= qi) & (n > 0), c <= qi, head, c[:ATT_BLOCK] < ATT_DH


def _stack_heads(x, head):
    x2 = jnp.concatenate([x, x], axis=0)
    return jnp.where(head, x2, jnp.zeros_like(x2))


def attn_fwd(q, k, v, name):
    d, L, W = q.shape
    nb = L // ATT_BLOCK
    nsub = 2 if nb % 2 == 0 else 1

    def body(q_ref, kp_ref, kc_ref, vp_ref, vc_ref, o_ref, lse_ref):
        step = pl.program_id(1)
        for u in range(nsub):
            rows = slice(u * ATT_BLOCK, (u + 1) * ATT_BLOCK)
            before = slice((u - 1) * ATT_BLOCK, u * ATT_BLOCK)
            mask_p, mask_c, head, head0 = _band_masks(step if u == 0 else 1)
            for hp in range(W // HEAD_LANES):
                sl = slice(hp * HEAD_LANES, (hp + 1) * HEAD_LANES)
                kp, vp = (kp_ref[0, :, sl], vp_ref[0, :, sl]) if u == 0 else (kc_ref[0, before, sl], vc_ref[0, before, sl])
                kc, vc = kc_ref[0, rows, sl], vc_ref[0, rows, sl]
                qs = _stack_heads(q_ref[0, rows, sl], head)
                sp = jnp.where(mask_p, _nt(qs, kp), NEG)
                sc = jnp.where(mask_c, _nt(qs, kc), NEG)
                m = jnp.maximum(jnp.max(sp, axis=1, keepdims=True), jnp.max(sc, axis=1, keepdims=True))
                pp = jnp.exp(sp - m)
                pc = jnp.exp(sc - m)
                den = jnp.sum(pp, axis=1, keepdims=True) + jnp.sum(pc, axis=1, keepdims=True)
                o = (_nn(pp.astype(BF16), vp) + _nn(pc.astype(BF16), vc)) / den
                lse = m + jnp.log(den)
                o_ref[0, rows, sl] = jnp.where(head0, o[:ATT_BLOCK], o[ATT_BLOCK:])
                lse_ref[0, rows, sl] = jnp.where(head0, lse[:ATT_BLOCK], lse[ATT_BLOCK:])

    cur = pl.BlockSpec((1, nsub * ATT_BLOCK, W), lambda r, n: (r, n, 0))
    prev = pl.BlockSpec((1, ATT_BLOCK, W), lambda r, n: (r, jnp.maximum(nsub * n - 1, 0), 0))
    out = jax.ShapeDtypeStruct((d, L, W), F32)
    return pl.pallas_call(
        body, name=name, grid=(d, nb // nsub),
        in_specs=[cur, prev, cur, prev, cur],
        out_specs=[cur, cur], out_shape=[out, out],
        compiler_params=_params(("parallel", "parallel")),
    )(q, k, k, v, v)


def attn_combine(os_, lses, name):
    T = os_[0].shape[0] * os_[0].shape[1]
    W = os_[0].shape[2]
    tm = LAYOUT_ROWS
    npat = len(os_)

    def body(*refs):
        o_refs, l_refs = refs[:npat], refs[npat:2 * npat]
        out_ref = refs[2 * npat]
        ores, lres = refs[2 * npat + 1:3 * npat + 1], refs[3 * npat + 1:4 * npat + 1]
        bufs = refs[4 * npat + 1:]
        lbufs, obufs, out_buf, lse_buf = bufs[:npat], bufs[npat:2 * npat], bufs[2 * npat], bufs[2 * npat + 1]
        for p, d in enumerate(DILATIONS):
            _from_res(l_refs[p], lbufs[p], d)
            _from_res(o_refs[p], obufs[p], d)
        for hp in range(W // HEAD_LANES):
            ls = [b[hp] for b in lbufs]
            m = ls[0]
            for l in ls[1:]:
                m = jnp.maximum(m, l)
            es = [jnp.exp(l - m) for l in ls]
            z = es[0]
            for e in es[1:]:
                z = z + e
            acc = es[0] * obufs[0][hp]
            for p in range(1, npat):
                acc = acc + es[p] * obufs[p][hp]
            out = acc / z
            out_ref[:, _lanes(hp)] = out.astype(BF16)
            out_buf[hp] = out
            lse_buf[hp] = m + jnp.log(z)
        for p, d in enumerate(DILATIONS):
            _to_res(out_buf, ores[p], d, BF16)
            _to_res(lse_buf, lres[p], d, F32)

    return pl.pallas_call(
        body, name=name, grid=(T // tm,),
        in_specs=[_res_spec(d, tm, W) for _ in range(2) for d in DILATIONS],
        out_specs=([pl.BlockSpec((tm, W), lambda i: (i, 0))] + [_res_spec(d, tm, W) for _ in range(2) for d in DILATIONS]),
        out_shape=([jax.ShapeDtypeStruct((T, W), BF16)] + [_res_shape(d, T, W, BF16) for d in DILATIONS]
                   + [_res_shape(d, T, W, F32) for d in DILATIONS]),
        scratch_shapes=[_slab_buf(tm, W)] * (2 * npat + 2),
        compiler_params=_params(("parallel",)),
    )(*os_, *lses)


def attn_bwd(q, k, v, do, o, lse, name):
    d, L, W = q.shape
    nb = L // ATT_BLOCK
    nsub = 2 if nb % 2 == 0 else 1
    per_seq = nb // nsub
    nst = d * per_seq
    nb = d * nb
    last = slice((nsub - 1) * ATT_BLOCK, nsub * ATT_BLOCK)
    q, k, v, do, o, lse = (t.reshape(1, d * L, W) for t in (q, k, v, do, o, lse))

    def body(q_ref, kp_ref, kc_ref, vp_ref, vc_ref, do_ref, o_ref, lse_ref, dq_ref, dk_ref, dv_ref, kkeep, vkeep):
        step = pl.program_id(1)
        n = step % per_seq

        @pl.when(step == 0)
        def _():
            kkeep[...] = jnp.zeros_like(kkeep)
            vkeep[...] = jnp.zeros_like(vkeep)

        @pl.when(step < nst)
        def _():
            for hp in range(W // HEAD_LANES):
                sl = slice(hp * HEAD_LANES, (hp + 1) * HEAD_LANES)
                shares = []
                for u in range(nsub):
                    rows = slice(u * ATT_BLOCK, (u + 1) * ATT_BLOCK)
                    before = slice((u - 1) * ATT_BLOCK, u * ATT_BLOCK)
                    mask_p, mask_c, head, head0 = _band_masks(n if u == 0 else 1)
                    kp, vp = (kp_ref[0, :, sl], vp_ref[0, :, sl]) if u == 0 else (kc_ref[0, before, sl], vc_ref[0, before, sl])
                    kc, vc = kc_ref[0, rows, sl], vc_ref[0, rows, sl]
                    dout = do_ref[0, rows, sl]
                    qs = _stack_heads(q_ref[0, rows, sl], head)
                    dos = _stack_heads(dout, head)
                    lse_v = lse_ref[0, rows, sl]
                    lse_c = jnp.max(jnp.where(head, jnp.concatenate([lse_v, lse_v], axis=0), NEG), axis=1, keepdims=True)
                    delta = jnp.sum(_stack_heads(dout.astype(F32) * o_ref[0, rows, sl].astype(F32), head), axis=1,
                                    keepdims=True)
                    pp = jnp.exp(jnp.where(mask_p, _nt(qs, kp), NEG) - lse_c)
                    pc = jnp.exp(jnp.where(mask_c, _nt(qs, kc), NEG) - lse_c)
                    dsp = (pp * (_nt(dos, vp) - delta)).astype(BF16)
                    dsc = (pc * (_nt(dos, vc) - delta)).astype(BF16)
                    dq2 = _nn(dsp, kp) + _nn(dsc, kc)
                    dq_ref[0, rows, sl] = jnp.where(head0, dq2[:ATT_BLOCK], dq2[ATT_BLOCK:])
                    shares.append((_tn(dsp, qs), _tn(pp.astype(BF16), dos), _tn(dsc, qs), _tn(pc.astype(BF16), dos)))
                dk_ref[0, last, sl] = kkeep[last, sl] + shares[0][0]
                dv_ref[0, last, sl] = vkeep[last, sl] + shares[0][1]
                if nsub == 2:
                    dk_ref[0, :ATT_BLOCK, sl] = kkeep[:ATT_BLOCK, sl]
                    dv_ref[0, :ATT_BLOCK, sl] = vkeep[:ATT_BLOCK, sl]
                    kkeep[:ATT_BLOCK, sl] = shares[0][2] + shares[1][0]
                    vkeep[:ATT_BLOCK, sl] = shares[0][3] + shares[1][1]
                kkeep[last, sl] = shares[-1][2]
                vkeep[last, sl] = shares[-1][3]

        @pl.when(step == nst)
        def _():
            dk_ref[0] = kkeep[...]
            dv_ref[0] = vkeep[...]

    rows_per_step = nsub * ATT_BLOCK
    cur = pl.BlockSpec((1, rows_per_step, W), lambda r, n: (r, jnp.minimum(n, nst - 1), 0))
    lag = pl.BlockSpec((1, rows_per_step, W), lambda r, n: (r, jnp.clip(n - 1, 0, nst - 1), 0))
    prev = pl.BlockSpec((1, ATT_BLOCK, W), lambda r, n: (r, jnp.clip(nsub * n - 1, 0, nb - 1), 0))
    out = jax.ShapeDtypeStruct((1, d * L, W), F32)
    outs = pl.pallas_call(
        body, name=name, grid=(1, nst + 1),
        in_specs=[cur, prev, cur, prev, cur, cur, cur, cur],
        out_specs=[cur, lag, lag], out_shape=[out, out, out],
        scratch_shapes=[pltpu.VMEM((rows_per_step, W), F32), pltpu.VMEM((rows_per_step, W), F32)],
        compiler_params=_params(("parallel", "arbitrary")),
    )(q, k, k, v, v, do, o, lse)
    return [t.reshape(d, L, W) for t in outs]


def final_loss_bwd(h, gf, tgt, gate, name):
    T, D = h.shape
    tm = _tile(T, 512, 16)
    g_in, g_out, g_shape, g_ops = _gate_specs(gate, tm, D)

    def body(h_ref, g_ref, t_ref, o_ref, modg_ref, dh_ref, red_ref, do_ref, redg_ref):
        x = h_ref[...]
        r = lax.rsqrt(jnp.mean(x * x, axis=-1, keepdims=True) + EPS)
        n = x * r
        g = g_ref[...]
        err = n * g - t_ref[...]
        dy = err * (1.0 / D)

        @pl.when(pl.program_id(0) == 0)
        def _():
            red_ref[...] = jnp.zeros_like(red_ref)

        red_ref[0:1, :] += jnp.sum(dy * n, axis=0, keepdims=True)
        red_ref[1:2, :] += jnp.zeros((1, D), F32) + (0.5 / D) * jnp.sum(err * err, keepdims=True)
        dn = dy * g
        dh = r * (dn - n * jnp.mean(dn * n, axis=-1, keepdims=True))
        dh_ref[...] = dh
        _gate_emit(dh, gate, o_ref, modg_ref, do_ref, redg_ref)

    return pl.pallas_call(
        body, name=name, grid=(T // tm,),
        in_specs=[pl.BlockSpec((tm, D), lambda i: (i, 0)),
                  pl.BlockSpec((1, D), lambda i: (0, 0)),
                  pl.BlockSpec((tm, D), lambda i: (i, 0))] + g_in,
        out_specs=[pl.BlockSpec((tm, D), lambda i: (i, 0)), pl.BlockSpec((8, D), lambda i: (0, 0))] + g_out,
        out_shape=[jax.ShapeDtypeStruct((T, D), F32), jax.ShapeDtypeStruct((8, D), F32)] + g_shape,
        compiler_params=_params(("arbitrary",)),
    )(h, gf, tgt, *g_ops)


def ada_fwd(c_all, ada_w, ada_b, name):
    nl, D, N = ada_w.shape

    def body(c_ref, w_ref, b_ref, o_ref):
        c = c_ref[...]
        o_ref[0] = _nn(c * _sigmoid(c), w_ref[0]) + b_ref[0]

    return pl.pallas_call(
        body, name=name, grid=(nl,),
        in_specs=[pl.BlockSpec((N_DEV, D), lambda l: (0, 0)),
                  pl.BlockSpec((1, D, N), lambda l: (l, 0, 0)),
                  pl.BlockSpec((1, 1, N), lambda l: (l, 0, 0))],
        out_specs=pl.BlockSpec((1, N_DEV, N), lambda l: (l, 0, 0)),
        out_shape=jax.ShapeDtypeStruct((nl, N_DEV, N), F32),
        compiler_params=_params(("parallel",)),
    )(c_all, ada_w, ada_b)


def ada_bwd(c_allT, dmod, name):
    nl, _, N = dmod.shape
    D = c_allT.shape[0]

    def body(c_ref, g_ref, o_ref):
        c = c_ref[...]
        ca = c * _sigmoid(c)
        acc = ca[:, 0:1] * g_ref[0, 0:1, :]
        for b in range(1, N_DEV):
            acc = acc + ca[:, b:b + 1] * g_ref[0, b:b + 1, :]
        o_ref[0] = acc

    return pl.pallas_call(
        body, name=name, grid=(nl,),
        in_specs=[pl.BlockSpec((D, N_DEV), lambda l: (0, 0)),
                  pl.BlockSpec((1, N_DEV, N), lambda l: (l, 0, 0))],
        out_specs=pl.BlockSpec((1, D, N), lambda l: (l, 0, 0)),
        out_shape=jax.ShapeDtypeStruct((nl, D, N), F32),
        compiler_params=_params(("parallel",)),
    )(c_allT, dmod)


def adamw(w, g, m, v, name):
    R, C = w.shape
    tr = _tile(R, max(8, (1 << 19) // C // 8 * 8), 8)
    c1 = 1.0 - ADAM_B1 ** ADAM_STEP
    c2 = 1.0 - ADAM_B2 ** ADAM_STEP

    def body(w_ref, g_ref, m_ref, v_ref, d_ref, mo_ref, vo_ref):
        gv = g_ref[...]
        mn = ADAM_B1 * m_ref[...] + (1.0 - ADAM_B1) * gv
        vn = ADAM_B2 * v_ref[...] + (1.0 - ADAM_B2) * (gv * gv)
        mo_ref[...] = mn
        vo_ref[...] = vn
        d_ref[...] = -ADAM_LR * ((mn / c1) / (jnp.sqrt(vn / c2) + ADAM_EPS) + ADAM_WD * w_ref[...])

    blk = pl.BlockSpec((tr, C), lambda i: (i, 0))
    out = jax.ShapeDtypeStruct((R, C), F32)
    return pl.pallas_call(
        body, name=name, grid=(R // tr,),
        in_specs=[blk] * 4, out_specs=[blk] * 3, out_shape=[out] * 3,
        compiler_params=_params(("parallel",)),
    )(w, g, m, v)


def adamw_layer(w, g, m, v, l, prev, name):
    NLw, R, C = w.shape
    tr = _tile(R, max(8, (1 << 19) // C // 8 * 8), 8)
    nrb = R // tr
    c1 = 1.0 - ADAM_B1 ** ADAM_STEP
    c2 = 1.0 - ADAM_B2 ** ADAM_STEP
    w, m, v = (t.reshape(NLw * R, C) for t in (w, m, v))

    def body(w_ref, g_ref, m_ref, v_ref, *rest):
        go_ref, d_ref, mo_ref, vo_ref = rest[-4:]
        gv = g_ref[...]
        mn = ADAM_B1 * m_ref[...] + (1.0 - ADAM_B1) * gv
        vn = ADAM_B2 * v_ref[...] + (1.0 - ADAM_B2) * (gv * gv)
        go_ref[...] = gv
        mo_ref[...] = mn
        vo_ref[...] = vn
        d_ref[...] = -ADAM_LR * ((mn / c1) / (jnp.sqrt(vn / c2) + ADAM_EPS) + ADAM_WD * w_ref[...])

    lay = pl.BlockSpec((tr, C), lambda i: (l * nrb + i, 0))
    out = jax.ShapeDtypeStruct((NLw * R, C), F32)
    n_prev = 0 if prev is None else 4
    return pl.pallas_call(
        body, name=name, grid=(nrb,),
        in_specs=[lay, pl.BlockSpec((tr, C), lambda i: (i, 0)), lay, lay] + [pl.BlockSpec(memory_space=pl.ANY)] * n_prev,
        out_specs=[lay] * 4, out_shape=[out] * 4,
        input_output_aliases={4 + i: i for i in range(n_prev)},
        compiler_params=_params(("parallel",)),
    )(w, g, m, v, *(prev or ()))


def sum_slots(x, name):
    S, R, C = x.shape
    tr = _tile(R, 128, 8)

    def body(x_ref, o_ref):
        acc = x_ref[0]
        for s in range(1, S):
            acc = acc + x_ref[s]
        o_ref[...] = acc

    return pl.pallas_call(
        body, name=name, grid=(R // tr,),
        in_specs=[pl.BlockSpec((S, tr, C), lambda i: (0, i, 0))],
        out_specs=pl.BlockSpec((tr, C), lambda i: (i, 0)),
        out_shape=jax.ShapeDtypeStruct((R, C), F32),
        compiler_params=_params(("parallel",)),
    )(x)


def sum_halves(g, lands, c_idx, name):
    n, ns, _, rh, D = g.shape

    def body(c_ref, g_ref, l_ref, o_ref):
        for j in range(ns):
            o_ref[0, j] = (g_ref[0, j, 0].astype(F32) + l_ref[0, j].astype(F32)).astype(BF16)

    return pl.pallas_call(
        body, name=name,
        grid_spec=pltpu.PrefetchScalarGridSpec(
            num_scalar_prefetch=1, grid=(n,),
            in_specs=[pl.BlockSpec((1, ns, 1, rh, D), lambda i, c: (i, 0, c[0], 0, 0)),
                      pl.BlockSpec((1, ns, rh, D), lambda i, c: (i, 0, 0, 0))],
            out_specs=pl.BlockSpec((1, ns, rh, D), lambda i, c: (i, 0, 0, 0))),
        out_shape=jax.ShapeDtypeStruct((n, ns, rh, D), BF16),
        compiler_params=_params(("parallel",)),
    )(c_idx, g, lands)


def sum_chips(p, lands, place, name):
    n, ns, rh, D = p.shape

    def body(c_ref, p_ref, l_ref, o_ref):
        acc = p_ref[0, 0].astype(F32)
        for j in range(N_CHIP - 1):
            acc = acc + l_ref[j, 0].astype(F32)
        o_ref[0, 0] = acc

    return pl.pallas_call(
        body, name=name,
        grid_spec=pltpu.PrefetchScalarGridSpec(
            num_scalar_prefetch=1, grid=(n,),
            in_specs=[pl.BlockSpec((1, 1, rh, D), lambda i, c: (i, c[0], 0, 0)),
                      pl.BlockSpec((N_CHIP - 1, 1, rh, D), lambda i, c: (0, i, 0, 0))],
            out_specs=pl.BlockSpec((1, 1, rh, D), lambda i, c: (i, c[1], 0, 0))),
        out_shape=jax.ShapeDtypeStruct((n, 2, rh, D), F32),
        compiler_params=_params(("parallel",)),
    )(place, p, lands)


def _my_place():
    return lax.axis_index("x"), lax.axis_index("y"), lax.axis_index("c")


def _other_chips(mx, my):
    return [(1 - mx, my), (mx, 1 - my), (1 - mx, 1 - my)]


def gather_small(x, after, name):
    def body(x_ref, after_ref, out_ref, sum_ref, send_sems, recv_sems):
        mx, my, mc = _my_place()
        me = 4 * mx + 2 * my + mc
        out_ref[me] = x_ref[...]
        sends = []
        for k in range(1, N_DEV):
            kx, ky, kc = (k >> 2) & 1, (k >> 1) & 1, k & 1
            peer = (1 - mx if kx else mx, 1 - my if ky else my, 1 - mc if kc else mc)
            cp = pltpu.make_async_remote_copy(
                src_ref=x_ref, dst_ref=out_ref.at[me], send_sem=send_sems.at[k - 1], recv_sem=recv_sems.at[k - 1],
                device_id=peer, device_id_type=MESH)
            cp.start()
            sends.append((cp, 4 * peer[0] + 2 * peer[1] + peer[2], peer))
        for k, (cp, peer_slot, peer) in enumerate(sends):
            pltpu.make_async_remote_copy(
                src_ref=x_ref, dst_ref=out_ref.at[peer_slot], send_sem=send_sems.at[k], recv_sem=recv_sems.at[k],
                device_id=peer, device_id_type=MESH).wait_recv()
        for cp, _, _ in sends:
            cp.wait_send()
        acc = out_ref[0]
        for s in range(1, N_DEV):
            acc = acc + out_ref[s]
        sum_ref[...] = acc

    vmem = pl.BlockSpec(memory_space=pltpu.VMEM)
    return pl.pallas_call(
        body, name=name,
        in_specs=[vmem, pl.BlockSpec(memory_space=pl.ANY)], out_specs=[vmem, vmem],
        out_shape=[jax.ShapeDtypeStruct((N_DEV,) + x.shape, x.dtype), jax.ShapeDtypeStruct(x.shape, x.dtype)],
        scratch_shapes=[pltpu.SemaphoreType.DMA((N_DEV - 1,)), pltpu.SemaphoreType.DMA((N_DEV - 1,))],
        compiler_params=pltpu.CompilerParams(vmem_limit_bytes=VMEM_LIMIT),
    )(x, after)


_HBM =pl.BlockSpec(memory_space=pltpu.HBM)
_SEM = pl.BlockSpec(memory_space=pltpu.SEMAPHORE)
_DATAFLOW = pltpu.SideEffectType.DATAFLOW_SIDE_EFFECTING


def _gather_copies(shard, land, send, recv, base):
    mx, my, mc = _my_place()
    ci = 2 * mx + my
    peers = [((cx, cy, mc), 2 * cx + cy) for cx, cy in _other_chips(mx, my)] + [((mx, my, 1 - mc), ci)]
    out = []
    for q, (dev, src_slot) in enumerate(peers):
        out.append((
            pltpu.make_async_remote_copy(src_ref=shard, dst_ref=land.at[:, ci], send_sem=send.at[base + q],
                                         recv_sem=recv.at[base + q], device_id=dev, device_id_type=MESH),
            pltpu.make_async_remote_copy(src_ref=shard, dst_ref=land.at[:, src_slot], send_sem=send.at[base + q],
                                         recv_sem=recv.at[base + q], device_id=dev, device_id_type=MESH)))
    return out


def gather_start(groups, after, name):
    items = [s for g in groups for s in g]
    ni, ng = len(items), len(groups)

    def body(*refs):
        shards, lands = refs[:ni], refs[ni:2 * ni]
        sems = refs[2 * ni + 1:2 * ni + 1 + 2 * ng]
        token = refs[-1]
        i = 0
        for g, grp in enumerate(groups):
            for p in range(len(grp)):
                for start_cp, _ in _gather_copies(shards[i], lands[i], sems[2 * g], sems[2 * g + 1], 4 * p):
                    start_cp.start()
                i += 1
        token[...] = jnp.zeros_like(token)

    sem_shapes = []
    for grp in groups:
        sem_shapes += [pltpu.SemaphoreType.DMA((4 * len(grp),))] * 2
    land_shapes = [(s.shape[0], N_CHIP) + s.shape[1:] for s in items]
    outs = pl.pallas_call(
        body, name=name,
        in_specs=[_HBM] * (2 * ni) + [pl.BlockSpec(memory_space=pl.ANY)],
        out_specs=[_SEM] * (2 * ng) + [_HBM] * (2 * ni) + [pl.BlockSpec(memory_space=pltpu.VMEM)],
        out_shape=(sem_shapes + [pltpu.HBM(s.shape, s.dtype) for s in items]
                   + [pltpu.HBM(ls, s.dtype) for ls, s in zip(land_shapes, items)]
                   + [jax.ShapeDtypeStruct((8, 128), F32)]),
        input_output_aliases={i: 2 * ng + i for i in range(2 * ni)},
        compiler_params=pltpu.CompilerParams(has_side_effects=_DATAFLOW),
    )(*[pltpu.with_memory_space_constraint(s, pltpu.HBM) for s in items],
      *[pltpu.with_memory_space_constraint(lax.empty(ls, s.dtype), pltpu.HBM) for ls, s in zip(land_shapes, items)],
      after)
    sems, thru, token = outs[:2 * ng], outs[2 * ng:2 * ng + 2 * ni], outs[-1]
    handles, i = [], 0
    for g, grp in enumerate(groups):
        n = len(grp)
        handles.append((sems[2 * g], sems[2 * g + 1], thru[i:i + n], thru[ni + i:ni + i + n]))
        i += n
    return handles, token


def gather_wait(handle, after, name):
    send, recv, shards, lands = handle
    n = len(shards)

    def body(*refs):
        shard_refs, land_refs = refs[:n], refs[n:2 * n]
        send_ref, recv_ref = refs[2 * n], refs[2 * n + 1]
        for p in range(n):
            for start_cp, recv_cp in _gather_copies(shard_refs[p], land_refs[p], send_ref, recv_ref, 4 * p):
                start_cp.wait_send()
                recv_cp.wait_recv()

    outs = pl.pallas_call(
        body, name=name,
        in_specs=[_HBM] * (2 * n) + [_SEM, _SEM, pl.BlockSpec(memory_space=pl.ANY)],
        out_specs=[_HBM] * (2 * n),
        out_shape=[pltpu.HBM(s.shape, s.dtype) for s in shards] + [pltpu.HBM(l.shape, l.dtype) for l in lands],
        input_output_aliases={i: i for i in range(2 * n)},
        compiler_params=pltpu.CompilerParams(has_side_effects=_DATAFLOW),
    )(*shards, *lands, send, recv, after)
    return outs[n:]


def _first_copies(shard, land, send, recv):
    mx, my, mc = _my_place()
    ci = 2 * mx + my
    out = []
    for q, (cx, cy) in enumerate(_other_chips(mx, my)):
        dev = (cx, cy, mc)
        out.append(tuple(pltpu.make_async_remote_copy(
            src_ref=shard.at[:, mc], dst_ref=land.at[:, slot, mc], send_sem=send.at[q], recv_sem=recv.at[q],
            device_id=dev, device_id_type=MESH) for slot in (ci, 2 * cx + cy)))
    sib = pltpu.make_async_remote_copy(src_ref=shard, dst_ref=land.at[:, ci], send_sem=send.at[3], recv_sem=recv.at[3],
                                       device_id=(mx, my, 1 - mc), device_id_type=MESH)
    return out + [(sib, sib)]


def _forward_copies(land, send, recv):
    mx, my, mc = _my_place()
    out = []
    for q, (cx, cy) in enumerate(_other_chips(mx, my)):
        out.append(tuple(pltpu.make_async_remote_copy(
            src_ref=land.at[:, 2 * cx + cy, hc], dst_ref=land.at[:, 2 * cx + cy, hc], send_sem=send.at[q],
            recv_sem=recv.at[q], device_id=(mx, my, 1 - mc), device_id_type=MESH) for hc in (mc, 1 - mc)))
    return out


def first_start(shard, after, name):
    def body(shard_ref, land_ref, after_ref, send, recv, shard_thru, land_thru, token):
        for mine, _ in _first_copies(shard_ref, land_ref, send, recv):
            mine.start()
        token[...] = jnp.zeros_like(token)

    land_shape = (shard.shape[0], N_CHIP) + shard.shape[1:]
    outs = pl.pallas_call(
        body, name=name,
        in_specs=[_HBM, _HBM, pl.BlockSpec(memory_space=pl.ANY)],
        out_specs=[_SEM, _SEM, _HBM, _HBM, pl.BlockSpec(memory_space=pltpu.VMEM)],
        out_shape=[pltpu.SemaphoreType.DMA((4,))] * 2 + [pltpu.HBM(shard.shape, shard.dtype),
                                                         pltpu.HBM(land_shape, shard.dtype),
                                                         jax.ShapeDtypeStruct((8, 128), F32)],
        input_output_aliases={0: 2, 1: 3},
        compiler_params=pltpu.CompilerParams(has_side_effects=_DATAFLOW),
    )(pltpu.with_memory_space_constraint(shard, pltpu.HBM),
      pltpu.with_memory_space_constraint(lax.empty(land_shape, shard.dtype), pltpu.HBM), after)
    return outs[:4], outs[4]


def first_forward(handle, after, name):
    send, recv, shard, land = handle

    def body(shard_ref, land_ref, send_ref, recv_ref, after_ref, send2, recv2, shard_thru, land_thru):
        firsts = _first_copies(shard_ref, land_ref, send_ref, recv_ref)
        forwards = _forward_copies(land_ref, send2, recv2)
        for q in range(3):
            firsts[q][1].wait_recv()
            forwards[q][0].start()
        firsts[3][1].wait_recv()
        for mine, _ in firsts:
            mine.wait_send()

    outs = pl.pallas_call(
        body, name=name,
        in_specs=[_HBM, _HBM, _SEM, _SEM, pl.BlockSpec(memory_space=pl.ANY)],
        out_specs=[_SEM, _SEM, _HBM, _HBM],
        out_shape=[pltpu.SemaphoreType.DMA((3,))] * 2 + [pltpu.HBM(shard.shape, shard.dtype),
                                                         pltpu.HBM(land.shape, land.dtype)],
        input_output_aliases={0: 2, 1: 3},
        compiler_params=pltpu.CompilerParams(has_side_effects=_DATAFLOW),
    )(shard, land, send, recv, after)
    return outs[0], outs[1], outs[3]


def first_wait(handle, after, name):
    send, recv, land = handle

    def body(land_ref, send_ref, recv_ref, after_ref, land_out):
        for mine, theirs in _forward_copies(land_ref, send_ref, recv_ref):
            mine.wait_send()
            theirs.wait_recv()

    return pl.pallas_call(
        body, name=name,
        in_specs=[_HBM, _SEM, _SEM, pl.BlockSpec(memory_space=pl.ANY)],
        out_specs=[_HBM],
        out_shape=[pltpu.HBM(land.shape, land.dtype)],
        input_output_aliases={0: 0},
        compiler_params=pltpu.CompilerParams(has_side_effects=_DATAFLOW),
    )(land, send, recv, after)[0]


def _sibling_copies(gs, lands, send, recv):
    mx, my, mc = _my_place()
    return [pltpu.make_async_remote_copy(
        src_ref=gs[k].at[:, :, 1 - mc], dst_ref=lands[k], send_sem=send.at[k], recv_sem=recv.at[k],
        device_id=(mx, my, 1 - mc), device_id_type=MESH) for k in range(len(gs))]


def sibling_start(gs, after, name):
    K = len(gs)

    def body(*refs):
        ins, lands = refs[:K], refs[K:2 * K]
        send, recv = refs[2 * K + 1], refs[2 * K + 2]
        for cp in _sibling_copies(ins, lands, send, recv):
            cp.start()
        refs[-1][...] = jnp.zeros_like(refs[-1])

    land_shapes = [g.shape[:2] + g.shape[3:] for g in gs]
    outs = pl.pallas_call(
        body, name=name,
        in_specs=[_HBM] * (2 * K) + [pl.BlockSpec(memory_space=pl.ANY)],
        out_specs=[_SEM, _SEM] + [_HBM] * (2 * K) + [pl.BlockSpec(memory_space=pltpu.VMEM)],
        out_shape=([pltpu.SemaphoreType.DMA((K,))] * 2 + [pltpu.HBM(g.shape, g.dtype) for g in gs]
                   + [pltpu.HBM(ls, g.dtype) for ls, g in zip(land_shapes, gs)] + [jax.ShapeDtypeStruct((8, 128), F32)]),
        input_output_aliases={i: 2 + i for i in range(2 * K)},
        compiler_params=pltpu.CompilerParams(has_side_effects=_DATAFLOW),
    )(*[pltpu.with_memory_space_constraint(g, pltpu.HBM) for g in gs],
      *[pltpu.with_memory_space_constraint(lax.empty(ls, g.dtype), pltpu.HBM) for ls, g in zip(land_shapes, gs)],
      after)
    return (outs[0], outs[1], outs[2:2 + K], outs[2 + K:2 + 2 * K]), outs[-1]


def sibling_wait(handle, after, name):
    send, recv, gs, lands = handle
    K = len(gs)

    def body(*refs):
        ins, land_refs = refs[:K], refs[K:2 * K]
        for cp in _sibling_copies(ins, land_refs, refs[2 * K], refs[2 * K + 1]):
            cp.wait_send()
            cp.wait_recv()

    outs = pl.pallas_call(
        body, name=name,
        in_specs=[_HBM] * (2 * K) + [_SEM, _SEM, pl.BlockSpec(memory_space=pl.ANY)],
        out_specs=[_HBM] * (2 * K),
        out_shape=[pltpu.HBM(g.shape, g.dtype) for g in gs] + [pltpu.HBM(l.shape, l.dtype) for l in lands],
        input_output_aliases={i: i for i in range(2 * K)},
        compiler_params=pltpu.CompilerParams(has_side_effects=_DATAFLOW),
    )(*gs, *lands, send, recv, after)
    return outs[:K], outs[K:]


def _small_copies(x, land, send, recv):
    mx, my, mc = _my_place()
    me = 4 * mx + 2 * my + mc
    out = []
    for k in range(1, N_DEV):
        peer = (1 - mx if k & 4 else mx, 1 - my if k & 2 else my, 1 - mc if k & 1 else mc)
        slot = 4 * peer[0] + 2 * peer[1] + peer[2]
        out.append(tuple(pltpu.make_async_remote_copy(
            src_ref=x, dst_ref=land.at[s], send_sem=send.at[k - 1], recv_sem=recv.at[k - 1],
            device_id=peer, device_id_type=MESH) for s in (me, slot)))
    return out


def small_start(x, after, name):
    def body(x_ref, land_ref, after_ref, send, recv, x_thru, land_thru, token):
        for mine, _ in _small_copies(x_ref, land_ref, send, recv):
            mine.start()
        token[...] = jnp.zeros_like(token)

    land_shape = (N_DEV,) + x.shape
    outs = pl.pallas_call(
        body, name=name,
        in_specs=[_HBM, _HBM, pl.BlockSpec(memory_space=pl.ANY)],
        out_specs=[_SEM, _SEM, _HBM, _HBM, pl.BlockSpec(memory_space=pltpu.VMEM)],
        out_shape=[pltpu.SemaphoreType.DMA((N_DEV - 1,))] * 2 + [pltpu.HBM(x.shape, x.dtype), pltpu.HBM(land_shape, x.dtype),
                                                                 jax.ShapeDtypeStruct((8, 128), F32)],
        input_output_aliases={0: 2, 1: 3},
        compiler_params=pltpu.CompilerParams(has_side_effects=_DATAFLOW),
    )(pltpu.with_memory_space_constraint(x, pltpu.HBM),
      pltpu.with_memory_space_constraint(lax.empty(land_shape, x.dtype), pltpu.HBM), after)
    return outs[:4], outs[4]


def small_wait(handle, after, name):
    send, recv, x, land = handle

    def body(x_ref, land_ref, send_ref, recv_ref, after_ref, x_out, land_out):
        for mine, theirs in _small_copies(x_ref, land_ref, send_ref, recv_ref):
            mine.wait_send()
            theirs.wait_recv()

    return pl.pallas_call(
        body, name=name,
        in_specs=[_HBM, _HBM, _SEM, _SEM, pl.BlockSpec(memory_space=pl.ANY)],
        out_specs=[_HBM, _HBM],
        out_shape=[pltpu.HBM(x.shape, x.dtype), pltpu.HBM(land.shape, land.dtype)],
        input_output_aliases={0: 0, 1: 1},
        compiler_params=pltpu.CompilerParams(has_side_effects=_DATAFLOW),
    )(x, land, send, recv, after)


def _scatter_copies(ps, lands, send, recv):
    mx, my, mc = _my_place()
    cps = []
    for j, (cx, cy) in enumerate(_other_chips(mx, my)):
        for k in range(len(ps)):
            cps.append(pltpu.make_async_remote_copy(
                src_ref=ps[k].at[:, 2 * cx + cy], dst_ref=lands[k].at[j],
                send_sem=send.at[k * 3 + j], recv_sem=recv.at[k * 3 + j],
                device_id=(cx, cy, mc), device_id_type=MESH))
    return cps


def scatter_start(ps, after, name):
    K = len(ps)

    def body(*refs):
        ins, lands = refs[:K], refs[K:2 * K]
        send, recv = refs[2 * K + 1], refs[2 * K + 2]
        for cp in _scatter_copies(ins, lands, send, recv):
            cp.start()
        refs[-1][...] = jnp.zeros_like(refs[-1])

    land_shapes = [(N_CHIP - 1, p.shape[0]) + p.shape[2:] for p in ps]
    outs = pl.pallas_call(
        body, name=name,
        in_specs=[_HBM] * (2 * K) + [pl.BlockSpec(memory_space=pl.ANY)],
        out_specs=[_SEM, _SEM] + [_HBM] * (2 * K) + [pl.BlockSpec(memory_space=pltpu.VMEM)],
        out_shape=([pltpu.SemaphoreType.DMA((3 * K,))] * 2 + [pltpu.HBM(p.shape, p.dtype) for p in ps]
                   + [pltpu.HBM(ls, p.dtype) for ls, p in zip(land_shapes, ps)] + [jax.ShapeDtypeStruct((8, 128), F32)]),
        input_output_aliases={i: 2 + i for i in range(2 * K)},
        compiler_params=pltpu.CompilerParams(has_side_effects=_DATAFLOW),
    )(*[pltpu.with_memory_space_constraint(p, pltpu.HBM) for p in ps],
      *[pltpu.with_memory_space_constraint(lax.empty(ls, p.dtype), pltpu.HBM) for ls, p in zip(land_shapes, ps)],
      after)
    return (outs[0], outs[1], outs[2:2 + K], outs[2 + K:2 + 2 * K]), outs[-1]


def scatter_wait(handle, after, name):
    send, recv, ps, lands = handle
    K = len(ps)
    afters = list(after) if isinstance(after, (list, tuple)) else [after]

    def body(*refs):
        ins, land_refs = refs[:K], refs[K:2 * K]
        send_ref, recv_ref = refs[2 * K], refs[2 * K + 1]
        for cp in _scatter_copies(ins, land_refs, send_ref, recv_ref):
            cp.wait_send()
            cp.wait_recv()

    outs = pl.pallas_call(
        body, name=name,
        in_specs=[_HBM] * (2 * K) + [_SEM, _SEM] + [pl.BlockSpec(memory_space=pl.ANY)] * len(afters),
        out_specs=[_HBM] * (2 * K),
        out_shape=[pltpu.HBM(p.shape, p.dtype) for p in ps] + [pltpu.HBM(l.shape, l.dtype) for l in lands],
        input_output_aliases={i: i for i in range(2 * K)},
        compiler_params=pltpu.CompilerParams(has_side_effects=_DATAFLOW),
    )(*ps, *lands, send, recv, *afters)
    return outs[:K], outs[K:]


def sibling_complete(ss, name):
    K = len(ss)

    def body(*refs):
        ins, outs = refs[:K], refs[K:2 * K]
        send, recv = refs[2 * K:]
        mx, my, mc = _my_place()
        cps = []
        for k in range(K):
            cp = pltpu.make_async_remote_copy(
                src_ref=ins[k].at[:, mc], dst_ref=outs[k].at[:, mc], send_sem=send.at[k], recv_sem=recv.at[k],
                device_id=(mx, my, 1 - mc), device_id_type=MESH)
            cp.start()
            cps.append(cp)
        for k in range(K):
            pltpu.make_async_remote_copy(
                src_ref=ins[k].at[:, mc], dst_ref=outs[k].at[:, 1 - mc], send_sem=send.at[k], recv_sem=recv.at[k],
                device_id=(mx, my, 1 - mc), device_id_type=MESH).wait_recv()
        for cp in cps:
            cp.wait_send()

    hbm = pl.BlockSpec(memory_space=pl.ANY)
    return pl.pallas_call(
        body, name=name,
        in_specs=[hbm] * K, out_specs=[hbm] * K,
        out_shape=[jax.ShapeDtypeStruct(s.shape, s.dtype) for s in ss],
        scratch_shapes=[pltpu.SemaphoreType.DMA((K,)), pltpu.SemaphoreType.DMA((K,))],
        input_output_aliases={k: k for k in range(K)},
    )(*ss)


def _rope_tables(T):
    inv = ROPE_THETA ** (-jnp.arange(0, ATT_DH, 2, dtype=F32) / ATT_DH)
    ang = jnp.arange(T, dtype=F32)[:, None] * inv[None, :]
    ang = jnp.concatenate([ang, ang, ang, ang], axis=-1)
    return jnp.cos(ang), jnp.sin(ang)


def _ffn_fwd(h, y, mod, i0, get_up, get_down, norm_next, tag):
    wgu = get_up(y)
    a, b, s = ffn_up(y, (wgu, (0,)), (wgu, (1,)), f"ffn_up_{tag}")
    wd = get_down(s)
    outs = resid_matmul([s], (wd, (0,)), h, mod, i0 + 2, 0.5, f"ffn_down_{tag}", norm_next)
    hn, o = outs[0], outs[1]
    return hn, (outs[2] if norm_next else None), (h, y, a, b, s, o), ((wgu, (0,)), (wgu, (1,)), (wd, (0,)))


def _ffn_bwd(dh, do, res, ng, i_n, mod, i0, wgT, wuT, wd, on_grads, next_gate, after, tag):
    h, y, a, b, s, o = res
    F = _wrows(wgT)
    da, db = ffn_bwd_mid(do, wd, a, b, f"ffn_bwd_mid_{tag}", after)
    gbuf = lax.empty((3, F, h.shape[1]), BF16)
    gbuf = matmul_tn(da, y, gbuf, 0, 0, f"dwg_{tag}")
    gbuf = matmul_tn(db, y, gbuf, 1, 0, f"dwu_{tag}")
    gbuf = matmul_tn(s, do, gbuf, 2, 0, f"dwd_{tag}")
    token, then = on_grads([gbuf])
    outs = dy_normbwd([(da, 0, wgT, 0, F), (db, 0, wuT, 0, F)], h, dh, ng, i_n, mod, i0 + 1,
                      f"ffn_bwd_dy_{tag}", next_gate, [token])
    return outs, then


def _mixer_fwd(h, y, mod, w_inT, w_out, sgu, cos, sin, norm_next, tag):
    lng, lnb, sw, swt, bcol = sgu
    proj = matmul_nt(y, w_inT, f"proj_{tag}")
    out_a = sgu_fwd(proj, lng, lnb, sw, bcol, f"sgu_fwd_{tag}")
    qkv = rope_fwd(proj, cos, sin, f"rope_fwd_{tag}")
    npat = len(DILATIONS)
    qkv_res = [tuple(qkv[3 * p:3 * p + 3]) for p in range(npat)]
    os_, lses = [], []
    for d, (qd, kd, vd) in zip(DILATIONS, qkv_res):
        o_d, lse_d = attn_fwd(qd, kd, vd, f"attn_fwd_d{d}_{tag}")
        os_.append(o_d)
        lses.append(lse_d)
    comb = attn_combine(os_, lses, f"attn_combine_{tag}")
    out_b, o_res, lse_res = comb[0], comb[1:1 + npat], comb[1 + npat:]
    outs = resid_matmul([out_a, out_b], w_out, h, mod, 5, 1.0, f"mix_out_{tag}", norm_next)
    hn, om = outs[0], outs[1]
    return hn, (outs[2] if norm_next else None), (h, y, proj, out_a, out_b, o_res, lse_res, qkv_res, om)


def _mixer_bwd(dh, dom, res, ng, mod, w_inT, w_out, sgu, cos, sin, on_grads, next_gate, after, tag):
    lng, lnb, sw, swt, bcol = sgu
    h, y, proj, out_a, out_b, o_res, lse_res, qkv_res, om = res
    D = h.shape[1]
    dmixed = matmul_nt(dom, w_out, f"dmixed_{tag}", after)
    woutbuf = lax.empty((1, 2 * MIX_HALF, D), BF16)
    woutbuf = matmul_tn(out_a, dom, woutbuf, 0, 0, f"dwout_a_{tag}", tmo_cap=MIX_HALF)
    woutbuf = matmul_tn(out_b, dom, woutbuf, 0, MIX_HALF, f"dwout_b_{tag}", tmo_cap=MIX_HALF)
    d_uv, d_sw, d_svec = sgu_bwd(proj, dmixed, lng, lnb, sw, swt, bcol, f"sgu_bwd_{tag}")
    do_res = to_residues(dmixed, 1, f"dout_res_{tag}")
    dqs, dks, dvs = [], [], []
    for p, (d, (qd, kd, vd)) in enumerate(zip(DILATIONS, qkv_res)):
        dq, dk, dv = attn_bwd(qd, kd, vd, do_res[p], o_res[p], lse_res[p], f"attn_bwd_d{d}_{tag}")
        dqs.append(dq)
        dks.append(dk)
        dvs.append(dv)
    d_qkv = rope_bwd(dqs, dks, dvs, cos, sin, f"rope_bwd_{tag}")
    winbuf = lax.empty((1, 5 * MIX_HALF, D), BF16)
    winbuf = matmul_tn(d_uv, y, winbuf, 0, 0, f"dwin_uv_{tag}", tmo_cap=MIX_HALF)
    winbuf = matmul_tn(d_qkv, y, winbuf, 0, 2 * MIX_HALF, f"dwin_qkv_{tag}", tmo_cap=MIX_HALF)
    token, then = on_grads([winbuf, woutbuf])
    pairs = [(d_uv, 0, w_inT, 0, 2 * MIX_HALF), (d_qkv, 0, w_inT, 1, 2 * MIX_HALF), (d_qkv, 2, w_inT, 4, MIX_HALF)]
    outs = dy_normbwd(pairs, h, dh, ng, 1, mod, 4, f"mix_bwd_dy_{tag}", next_gate, [token])
    return outs, d_sw, d_svec, then


def _local_step(x, tgt, mods, ngs, get_w, sgus, gf, on_block_grads, on_layer_small):
    T, D = x.shape
    cos, sin = _rope_tables(T)
    h = x
    saved, weights = [], []
    for l in range(2):
        def getter(blk, l=l):
            return lambda after: get_w(l, blk, after)

        if l == 0:
            y = normmod_fwd(h, ngs[0], 0, mods[0], 0, 1, "normmod_l0f1")
        h, y, r1, wf1 = _ffn_fwd(h, y, mods[l], 0, getter("f1u"), getter("f1d"), (ngs[l], 1, mods[l], 3, 4), f"l{l}f1")
        w_inT, w_out = get_w(l, "mx", h)
        h, y, r2 = _mixer_fwd(h, y, mods[l], (w_inT, (0,)), (w_out, (0,)), sgus[l], cos, sin,
                              (ngs[l], 2, mods[l], 6, 7), f"l{l}mx")
        h, y, r3, wf2 = _ffn_fwd(h, y, mods[l], 6, getter("f2u"), getter("f2d"),
                                 (ngs[l + 1], 0, mods[l + 1], 0, 1) if l + 1 < 2 else None, f"l{l}f2")
        saved.append((r1, r2, r3))
        weights.append((wf1, w_inT, w_out, wf2))
    def gate_of(l, blk):
        r1, r2, r3 = saved[l]
        o, i_g, coef = {"f2": (r3[5], 8, 0.5), "mx": (r2[-1], 5, 1.0), "f1": (r1[5], 2, 0.5)}[blk]
        return o, mods[l], i_g, coef

    seq = [(l, blk) for l in (1, 0) for blk in ("f2", "mx", "f1")]
    dh, red_final, do, red_g = final_loss_bwd(h, gf, tgt, gate_of(*seq[0]), "final_loss_bwd")
    rn, rg = {}, {}
    after = []
    for idx, (l, blk) in enumerate(seq):
        r1, r2, r3 = saved[l]
        wf1, w_inT, w_out, wf2 = weights[l]
        nxt = gate_of(*seq[idx + 1]) if idx + 1 < len(seq) else None
        rg[blk] = red_g
        tag = f"l{l}{blk}"

        def on(arrays, l=l, blk=blk):
            return on_block_grads(l, blk, arrays)

        if blk == "f2":
            outs, then = _ffn_bwd(dh, do, r3, ngs[l], 2, mods[l], 6, *wf2, on, nxt, after, tag)
        elif blk == "mx":
            outs, d_sw, d_svec, then = _mixer_bwd(dh, do, r2, ngs[l], mods[l], (w_inT, (0,)), (w_out, (0,)), sgus[l],
                                                  cos, sin, on, nxt, after, tag)
        else:
            outs, then = _ffn_bwd(dh, do, r1, ngs[l], 0, mods[l], 0, *wf1, on, nxt, after, tag)
        dh, rn[blk] = outs[0], outs[1]
        if nxt is not None:
            do, red_g = outs[2], outs[3]
        if blk == "f1":
            small = on_layer_small(l, dict(sgu_w=d_sw, sgu_vec=d_svec, red_n=(rn["f1"], rn["mx"], rn["f2"]),
                                           red_g=(rg["f1"], rg["mx"], rg["f2"])), red_final if l == 0 else None)
            after = [small, then(small)]
        else:
            after = [then(dh)]
    return dh


def _adam_out(w, g, m, v, name):
    shp = w.shape
    two_d = (-1, shp[-1])
    d, mn, vn = adamw(w.reshape(two_d), g.reshape(two_d), m.reshape(two_d), v.reshape(two_d), name)
    return g, d.reshape(shp), mn.reshape(shp), vn.reshape(shp)


def kernel(x, c, ada_w, ada_b, norm_g, ffn1_wg, ffn1_wu, ffn1_wd, ffn2_wg, ffn2_wu, ffn2_wd, w_in, sgu_ln_g, sgu_ln_b, sgu_w, sgu_b, w_out, final_g, loss_target, m_ada_w, m_ada_b, m_norm_g, m_ffn1_wg, m_ffn1_wu, m_ffn1_wd, m_ffn2_wg, m_ffn2_wu, m_ffn2_wd, m_w_in, m_sgu_ln_g, m_sgu_ln_b, m_sgu_w, m_sgu_b, m_w_out, m_final_g, v_ada_w, v_ada_b, v_norm_g, v_ffn1_wg, v_ffn1_wu, v_ffn1_wd, v_ffn2_wg, v_ffn2_wu, v_ffn2_wd, v_w_in, v_sgu_ln_g, v_sgu_ln_b, v_sgu_w, v_sgu_b, v_w_out, v_final_g):
    T, D = x.shape[1], x.shape[2]
    NL = ada_w.shape[0]
    mx, my, mc = _my_place()
    me = 4 * mx + 2 * my + mc
    ci = 2 * mx + my
    c_idx = jnp.reshape(mc, (1,)).astype(jnp.int32)
    place = jnp.stack([ci, mc]).astype(jnp.int32)

    ngw = norm_g.shape[2]
    small_in = jnp.concatenate([jnp.pad(c, ((0, 7), (0, 0))),
                                jnp.pad(norm_g.reshape(NL * 3, ngw), ((0, 8 - NL * 3), (0, D - ngw)))], axis=0)
    small_all, _ = gather_small(small_in, place, "gather_c_normg")
    c_all = small_all[:, 0, :]
    ng_parts = small_all[0::2, 8:8 + NL * 3, :ngw]
    ngs = jnp.transpose(ng_parts, (1, 0, 2)).reshape(NL, 3, N_CHIP * ngw)

    nmod = ada_w.shape[2]
    ada_b_mine = lax.dynamic_slice_in_dim(ada_b, ci * nmod, nmod, axis=1).reshape(NL, 1, nmod)
    mod_part = ada_fwd(c_all, ada_w, ada_b_mine, "ada_fwd")
    mod_all, _ = gather_small(mod_part.reshape(NL * N_DEV, nmod), place, "gather_mod")
    mod_rows = lax.dynamic_index_in_dim(mod_all.reshape(N_DEV, NL, N_DEV, nmod), me, axis=2, keepdims=False)
    mods = jnp.transpose(mod_rows[0::2], (1, 0, 2)).reshape(NL, N_ADA, D)

    sgus = []
    for l in range(NL):
        sgus.append((sgu_ln_g[l].reshape(1, MIX_HALF), sgu_ln_b[l].reshape(1, MIX_HALF), sgu_w[l],
                     jnp.swapaxes(sgu_w[l], 1, 2), jnp.transpose(sgu_b[l])))

    def halves(a):
        n, r, _ = a.shape
        return a.reshape(n, 2, r // 2, D)

    first_group = halves(jnp.stack([ffn1_wg[0].T, ffn1_wu[0].T], axis=0).astype(BF16))
    first_handle, first_token = first_start(first_group, mods, "first_start")
    zero = first_token[0, 0]
    mods = mods + zero

    def prep(a):
        return (a + zero).astype(BF16)

    groups = []
    for l in range(NL):
        groups += [[halves(jnp.stack([prep(ffn1_wg[l].T), prep(ffn1_wu[l].T)], axis=0))],
                   [halves(prep(ffn1_wd[l])[None])],
                   [halves(prep(w_in[l].T)[None]), halves(prep(w_out[l])[None])],
                   [halves(jnp.stack([prep(ffn2_wg[l].T), prep(ffn2_wu[l].T)], axis=0))],
                   [halves(prep(ffn2_wd[l])[None])]]
    handles, token = gather_start(groups[1:], mods, "gather_start")
    handles = [None] + handles
    mods = mods + token[0, 0]
    group_no = {"f1u": 0, "f1d": 1, "mx": 2, "f2u": 3, "f2d": 4}

    def get_w(l, key, after):
        g = len(group_no) * l + group_no[key]
        if g == 0:
            full = [first_wait(first_forward(first_handle, after, "first_forward"), place, "first_wait")]
        else:
            full = gather_wait(handles[g], after, f"gather_wait_l{l}{key}")
        full = [a.reshape(a.shape[0], N_CHIP * 2 * a.shape[3], D) for a in full]
        return full[0] if key != "mx" else tuple(full)

    def split(a):
        n, r4, _ = a.shape
        return a.reshape(n, N_CHIP, 2, r4 // N_CHIP // 2, D)

    pending, small_pending, small_tokens = {}, {}, {}

    def on_block_grads(l, blk, bufs):
        tag = f"l{l}{blk}"
        sib, tok1 = sibling_start([split(g) for g in bufs], place, f"rs_sibling_start_{tag}")

        def then(after):
            parts, lands = sibling_wait(sib, after, f"rs_sibling_wait_{tag}")
            psums = [sum_halves(g, ld, c_idx, f"rs_sum_halves_{tag}_{i}") for i, (g, ld) in enumerate(zip(parts, lands))]
            pending[(l, blk)], tok2 = scatter_start(psums, lands[0], f"rs_chips_start_{tag}")
            return tok2

        return tok1, then

    def blocks_finish(blocks, after, tag):
        ssums, counts = [], []
        for l, blk in blocks:
            psums, lands2 = scatter_wait(pending.pop((l, blk)), after, f"rs_chips_wait_l{l}{blk}")
            ssums += [sum_chips(p, ld, place, f"rs_sum_chips_l{l}{blk}_{i}") for i, (p, ld) in enumerate(zip(psums, lands2))]
            counts.append(len(psums))
        fins = [f.reshape(f.shape[0], -1, D) for f in sibling_complete(ssums, f"rs_complete_{tag}")]
        out, i = [], 0
        for n in counts:
            out.append(fins[i:i + n])
            i += n
        return out

    def on_layer_small(l, grads, red_final):
        blocks = list(grads["red_n"]) + list(grads["red_g"])
        blocks.append(jnp.pad(grads["sgu_vec"], ((0, 0), (0, D - MIX_HALF))))
        blocks.append(grads["sgu_w"].reshape(-1, D))
        if red_final is not None:
            blocks.append(red_final)
        xs = jnp.concatenate(blocks, axis=0)
        small_pending[l], small_tokens[l] = small_start(xs, place, f"small_start_l{l}")
        return small_tokens[l]

    grad_x = _local_step(x[0], loss_target[0], mods, ngs, get_w, sgus, final_g.reshape(1, D),
                         on_block_grads, on_layer_small)

    adam_state = {}

    def adam_big(nm, l, g, w, m, v):
        adam_state[nm] = adamw_layer(w, g, m, v, l, adam_state.get(nm), f"adamw_{nm}_l{l}")

    def adam_block(l, blk, fin):
        if blk == "mx":
            adam_big("w_in", l, fin[0][0].T, w_in, m_w_in, v_w_in)
            adam_big("w_out", l, fin[1][0], w_out, m_w_out, v_w_out)
        else:
            ws = ((ffn1_wg, m_ffn1_wg, v_ffn1_wg), (ffn1_wu, m_ffn1_wu, v_ffn1_wu), (ffn1_wd, m_ffn1_wd, v_ffn1_wd)) \
                if blk == "f1" else \
                ((ffn2_wg, m_ffn2_wg, v_ffn2_wg), (ffn2_wu, m_ffn2_wu, v_ffn2_wu), (ffn2_wd, m_ffn2_wd, v_ffn2_wd))
            pre = "ffn1" if blk == "f1" else "ffn2"
            for k, (nm, tr) in enumerate((("wg", True), ("wu", True), ("wd", False))):
                adam_big(f"{pre}_{nm}", l, fin[0][k], *[jnp.swapaxes(t, 1, 2) if tr else t for t in ws[k]])

    done_order = [(l, blk) for l in range(NL - 1, -1, -1) for blk in ("f2", "mx", "f1")]
    for (l, blk), fin in zip(done_order[:-1], blocks_finish(done_order[:-1], small_tokens[0], "early")):
        adam_block(l, blk, fin)
    last_big = adam_state["w_out"][1]

    small_sum, small_all = [], []
    for l in range(NL):
        xs, land = small_wait(small_pending[l], last_big, f"small_wait_l{l}")
        full = lax.dynamic_update_slice(land, xs[None], (me, 0, 0))
        small_all.append(full)
        small_sum.append(sum_slots(full, f"small_sum_l{l}"))
    offs = [8 * i for i in range(8)]
    off_final = offs[7] + SGU_HEADS * ATT_BLOCK * HEAD_LANES // D
    loss = small_sum[0][off_final + 1, 0]
    g_final_g = small_sum[0][off_final, :]
    g_norm_g, g_ada_b, g_lng, g_lnb, g_sb, g_sw, dmod_all = [], [], [], [], [], [], []
    for l in range(NL):
        rn = [small_sum[l][offs[i]:offs[i] + 8] for i in range(3)]
        rg = [small_sum[l][offs[3 + i]:offs[3 + i] + 8] for i in range(3)]
        g_norm_g.append(jnp.stack([rn[i][2] for i in range(3)], axis=0))
        g_ada_b.append(jnp.concatenate([jnp.stack([rn[i][0], rn[i][1], rg[i][0]], axis=0) for i in range(3)],
                                       axis=0).reshape(N_ADA * D))
        sv = small_sum[l][offs[6]:offs[6] + 8, :MIX_HALF]
        g_lng.append(sv[0].reshape(SGU_HEADS, HEAD_LANES))
        g_lnb.append(sv[1].reshape(SGU_HEADS, HEAD_LANES))
        g_sb.append(sv[2].reshape(SGU_HEADS, ATT_BLOCK))
        g_sw.append(small_sum[l][offs[7]:off_final].reshape(sgu_w.shape[1:]))
        rows = []
        for i in range(3):
            an = small_all[l][:, offs[i]:offs[i] + 2]
            ag = small_all[l][:, offs[3 + i]:offs[3 + i] + 1]
            rows += [an[:, 0], an[:, 1], ag[:, 0]]
        dmod_all.append(jnp.stack(rows, axis=1).reshape(N_DEV, N_ADA * D))
    dmod_all = jnp.stack(dmod_all, axis=0)
    dmod_mine = lax.dynamic_slice_in_dim(dmod_all, ci * nmod, nmod, axis=2)
    g_ada_w = ada_bwd(jnp.transpose(c_all), dmod_mine, "ada_bwd")
    g_ada_b = jnp.stack(g_ada_b, axis=0)
    g_norm_g_full = jnp.stack(g_norm_g, axis=0)
    g_norm_g_mine = lax.dynamic_slice_in_dim(g_norm_g_full, ci * ngw, ngw, axis=2)

    small_params = [
        ("ada_w", ada_w, g_ada_w, m_ada_w, v_ada_w),
        ("ada_b", ada_b, g_ada_b, m_ada_b, v_ada_b),
        ("norm_g", norm_g, g_norm_g_mine, m_norm_g, v_norm_g),
        ("sgu_ln_g", sgu_ln_g, jnp.stack(g_lng, axis=0), m_sgu_ln_g, v_sgu_ln_g),
        ("sgu_ln_b", sgu_ln_b, jnp.stack(g_lnb, axis=0), m_sgu_ln_b, v_sgu_ln_b),
        ("sgu_w", sgu_w, jnp.stack(g_sw, axis=0), m_sgu_w, v_sgu_w),
        ("sgu_b", sgu_b, jnp.stack(g_sb, axis=0), m_sgu_b, v_sgu_b),
        ("final_g", final_g.reshape(1, D), g_final_g.reshape(1, D), m_final_g.reshape(1, D), v_final_g.reshape(1, D)),
    ]
    for nm, w, g, m, v in small_params:
        res = _adam_out(w, g, m, v, f"adamw_{nm}")
        adam_state[nm] = tuple(t.reshape(D) for t in res) if nm == "final_g" else res

    l, blk = done_order[-1]
    adam_block(l, blk, blocks_finish([(l, blk)], [st[1] for st in adam_state.values()], "last")[0])

    names = ["ada_w", "ada_b", "norm_g", "ffn1_wg", "ffn1_wu", "ffn1_wd", "ffn2_wg", "ffn2_wu", "ffn2_wd", "w_in",
             "sgu_ln_g", "sgu_ln_b", "sgu_w", "sgu_b", "w_out", "final_g"]
    shapes = [t.shape for t in (ada_w, ada_b, norm_g, ffn1_wg, ffn1_wu, ffn1_wd, ffn2_wg, ffn2_wu, ffn2_wd, w_in,
                                sgu_ln_g, sgu_ln_b, sgu_w, sgu_b, w_out, final_g)]
    def shaped(nm, t, s):
        if nm in ("ffn1_wg", "ffn1_wu", "ffn2_wg", "ffn2_wu"):
            return jnp.swapaxes(t.reshape(s[0], s[2], s[1]), 1, 2)
        return t.reshape(s)

    return (loss, grad_x[None], *[shaped(nm, adam_state[nm][i], s) for i in range(4) for nm, s in zip(names, shapes)])
```

```python
import math

import jax
import jax.numpy as jnp
from jax import lax
from jax.experimental import pallas as pl
from jax.experimental.pallas import tpu as pltpu

F32 = jnp.float32
BF16 = jnp.bfloat16
EPS = 1e-6
SGU_HEADS = 4
HEAD_LANES = 128
ATT_DH = 64
ATT_BLOCK = 128
MIX_HALF = SGU_HEADS * HEAD_LANES
DILATIONS = (1, 4, 16)
ROPE_THETA = 10000.0
N_ADA = 9
ADAM_LR, ADAM_B1, ADAM_B2, ADAM_EPS, ADAM_WD, ADAM_STEP = 0.001, 0.9, 0.999, 1e-08, 0.01, 10
NEG = -1e30
V7X_VMEM_BYTES = 64 * 1024 * 1024
VMEM_LIMIT = V7X_VMEM_BYTES * 7 // 8
MESH = pl.DeviceIdType.MESH
N_DEV = 8
N_CHIP = 4
_ANY = pl.BlockSpec(memory_space=pl.ANY)


def _tile(n, cap, mult):
    if n <= cap:
        return n
    t = (cap // mult) * mult
    while t >= mult:
        if n % t == 0:
            return t
        t -= mult
    raise ValueError((n, cap, mult))


def _params(dims=None):
    return pltpu.CompilerParams(dimension_semantics=dims, vmem_limit_bytes=VMEM_LIMIT)


def _wspec(w, rows, idx, resident=False):
    arr, lead = w
    kw = dict(pipeline_mode=pl.Buffered(1)) if resident else {}
    return pl.BlockSpec((None,) * len(lead) + (rows, arr.shape[-1]), lambda *g: tuple(lead) + (idx(*g), 0), **kw)


def _wrows(w):
    return w[0].shape[-2]


def _nt(a, b):
    return lax.dot_general(a, b, (((1,), (1,)), ((), ())), preferred_element_type=F32)


def _tn(a, b):
    return lax.dot_general(a, b, (((0,), (0,)), ((), ())), preferred_element_type=F32)


def _nn(a, b):
    return jnp.dot(a, b, preferred_element_type=F32)


def _sigmoid(x):
    return 0.5 * jnp.tanh(0.5 * x) + 0.5


_GELU_K = math.sqrt(2.0 / math.pi)
_GELU_C = 0.044715


def _gelu(x):
    t = jnp.tanh(_GELU_K * (x + _GELU_C * x * x * x))
    return 0.5 * x * (1.0 + t)


def _gelu_and_grad(x):
    x2 = x * x
    t = jnp.tanh(_GELU_K * (x + _GELU_C * x * x2))
    g = 0.5 * x * (1.0 + t)
    dg = 0.5 * (1.0 + t) + 0.5 * x * (1.0 - t * t) * (_GELU_K * (1.0 + 3.0 * _GELU_C * x2))
    return g, dg


def normmod_fwd(h, ng, i_n, mod, i_sh, i_sc, name):
    T, D = h.shape
    tm = _tile(T, 512, 8)

    def body(h_ref, ng_ref, mod_ref, y_ref):
        y_ref[...] = _normmod(h_ref[...], ng_ref[i_n:i_n + 1, :], mod_ref[i_sh:i_sh + 1, :],
                              mod_ref[i_sc:i_sc + 1, :]).astype(BF16)

    return pl.pallas_call(
        body, name=name, grid=(T // tm,),
        in_specs=[pl.BlockSpec((tm, D), lambda i: (i, 0)),
                  pl.BlockSpec(ng.shape, lambda i: (0, 0)),
                  pl.BlockSpec(mod.shape, lambda i: (0, 0))],
        out_specs=pl.BlockSpec((tm, D), lambda i: (i, 0)),
        out_shape=jax.ShapeDtypeStruct((T, D), BF16),
        compiler_params=_params(("parallel",)),
    )(h, ng, mod)


def ffn_up(y, wgT, wuT, name):
    T, D = y.shape
    F = _wrows(wgT)
    tm = _tile(T, 512, 16)
    tf = _tile(F, 2816, 256)
    cuts = list(range(0, tf, 768)) + [tf]

    def body(y_ref, wg_ref, wu_ref, p_ref, q_ref, s_ref):
        yv = y_ref[...]
        for c0, c1 in zip(cuts[:-1], cuts[1:]):
            a = _nt(yv, wg_ref[c0:c1, :])
            b = _nt(yv, wu_ref[c0:c1, :])
            sig = _sigmoid(a)
            q = a * sig
            p_ref[:, c0:c1] = (b * (sig + q * (1.0 - sig))).astype(BF16)
            q_ref[:, c0:c1] = q.astype(BF16)
            s_ref[:, c0:c1] = (q * b).astype(BF16)

    act = jax.ShapeDtypeStruct((T, F), BF16)
    return pl.pallas_call(
        body, name=name, grid=(F // tf, T // tm),
        in_specs=[pl.BlockSpec((tm, D), lambda j, i: (i, 0)),
                  _wspec(wgT, tf, lambda j, i: j, resident=True),
                  _wspec(wuT, tf, lambda j, i: j, resident=True)],
        out_specs=[pl.BlockSpec((tm, tf), lambda j, i: (i, j))] * 3,
        out_shape=[act, act, act],
        compiler_params=_params(("parallel", "parallel")),
    )(y, wgT[0], wuT[0])


def _normmod(x, gn, sh, sc):
    r = lax.rsqrt(jnp.mean(x * x, axis=-1, keepdims=True) + EPS)
    return ((x * r) * gn) * (1.0 + sc) + sh


def resid_matmul(xs, w, h, mod, i_g, coef, name, norm_next=None):
    T, D = h.shape
    kb = xs[0].shape[1]
    assert all(x.shape == (T, kb) for x in xs) and _wrows(w) == kb * len(xs)
    tm = _tile(T, 1024, 16)
    nx = len(xs)
    n_in, n_out, n_shape, n_ops = [], [], [], []
    if norm_next:
        ng_n, i_n, mod_n, i_sh, i_sc = norm_next
        n_in = [pl.BlockSpec(ng_n.shape, lambda i: (0, 0)), pl.BlockSpec(mod_n.shape, lambda i: (0, 0))]
        n_out = [pl.BlockSpec((tm, D), lambda i: (i, 0))]
        n_shape = [jax.ShapeDtypeStruct((T, D), BF16)]
        n_ops = [ng_n, mod_n]

    def body(*refs):
        x_refs, w_refs = refs[:nx], refs[nx:2 * nx]
        h_ref, mod_ref = refs[2 * nx:2 * nx + 2]
        hn_ref, o_ref = refs[2 * nx + 2 + len(n_in):2 * nx + 4 + len(n_in)]
        o = _nn(x_refs[0][...], w_refs[0][...])
        for xr, wr in zip(x_refs[1:], w_refs[1:]):
            o = o + _nn(xr[...], wr[...])
        o_ref[...] = o.astype(BF16)
        hn = h_ref[...] + (coef * mod_ref[i_g:i_g + 1, :]) * o
        hn_ref[...] = hn
        if norm_next:
            ng_ref, modn_ref = refs[2 * nx + 2], refs[2 * nx + 3]
            refs[-1][...] = _normmod(hn, ng_ref[i_n:i_n + 1, :], modn_ref[i_sh:i_sh + 1, :],
                                     modn_ref[i_sc:i_sc + 1, :]).astype(BF16)

    return pl.pallas_call(
        body, name=name, grid=(T // tm,),
        in_specs=([pl.BlockSpec((tm, kb), lambda i: (i, 0))] * nx
                  + [_wspec(w, kb, lambda i, p=p: p, resident=True) for p in range(nx)]
                  + [pl.BlockSpec((tm, D), lambda i: (i, 0)),
                     pl.BlockSpec(mod.shape, lambda i: (0, 0))] + n_in),
        out_specs=[pl.BlockSpec((tm, D), lambda i: (i, 0))] * 2 + n_out,
        out_shape=[jax.ShapeDtypeStruct((T, D), F32), jax.ShapeDtypeStruct((T, D), BF16)] + n_shape,
        compiler_params=_params(("parallel",)),
    )(*xs, *([w[0]] * nx), h, mod, *n_ops)


def _gate_specs(gate, tm, D):
    o, mod, _, _ = gate
    T = o.shape[0]
    return ([pl.BlockSpec((tm, D), lambda i: (i, 0)), pl.BlockSpec(mod.shape, lambda i: (0, 0))],
            [pl.BlockSpec((tm, D), lambda i: (i, 0)), pl.BlockSpec((8, D), lambda i: (0, 0))],
            [jax.ShapeDtypeStruct((T, D), BF16), jax.ShapeDtypeStruct((8, D), F32)],
            [o, mod])


def _gate_emit(d, gate, o_ref, mod_ref, do_ref, red_ref):
    _, _, i_g, coef = gate
    do_ref[...] = (d * (coef * mod_ref[i_g:i_g + 1, :])).astype(BF16)

    @pl.when(pl.program_id(0) == 0)
    def _():
        red_ref[...] = jnp.zeros_like(red_ref)

    red_ref[0:1, :] += coef * jnp.sum(d * o_ref[...].astype(F32), axis=0, keepdims=True)


def ffn_bwd_mid(do, wd, p, q, name, after=()):
    T, D = do.shape
    F = _wrows(wd)
    tm = _tile(T, 512, 16)
    tf = _tile(F, 2816, 256)
    cuts = list(range(0, tf, 256)) + [tf]

    def body(do_ref, wd_ref, p_ref, q_ref, *rest):
        da_ref, db_ref = rest[-2:]
        dov = do_ref[...]
        for c0, c1 in zip(cuts[:-1], cuts[1:]):
            ds = _nt(dov, wd_ref[c0:c1, :])
            da_ref[:, c0:c1] = (ds * p_ref[:, c0:c1].astype(F32)).astype(BF16)
            db_ref[:, c0:c1] = (ds * q_ref[:, c0:c1].astype(F32)).astype(BF16)

    act = jax.ShapeDtypeStruct((T, F), BF16)
    return pl.pallas_call(
        body, name=name, grid=(F // tf, T // tm),
        in_specs=[pl.BlockSpec((tm, D), lambda j, i: (i, 0)),
                  _wspec(wd, tf, lambda j, i: j, resident=True),
                  pl.BlockSpec((tm, tf), lambda j, i: (i, j)),
                  pl.BlockSpec((tm, tf), lambda j, i: (i, j))] + [_ANY] * len(after),
        out_specs=[pl.BlockSpec((tm, tf), lambda j, i: (i, j))] * 2,
        out_shape=[act, act],
        compiler_params=_params(("parallel", "parallel")),
    )(do, wd[0], p, q, *after)


def dy_normbwd(pairs, h, dhp, ng, i_n, mod, i_sc, name, gate=None, after=()):
    T, D = h.shape
    tm = _tile(T, 512, 16)
    npair = len(pairs)
    g_in, g_out, g_shape, g_ops = _gate_specs(gate, tm, D) if gate else ([], [], [], [])
    n_in = 2 * npair + 4 + len(g_in) + len(after)

    def body(*refs):
        x_refs, w_refs = refs[:npair], refs[npair:2 * npair]
        h_ref, dhp_ref, ng_ref, mod_ref = refs[2 * npair:2 * npair + 4]
        dh_ref, red_ref = refs[n_in:n_in + 2]
        dy = _nn(x_refs[0][...], w_refs[0][...])
        for xr, wr in zip(x_refs[1:], w_refs[1:]):
            dy = dy + _nn(xr[...], wr[...])
        x = h_ref[...]
        r = lax.rsqrt(jnp.mean(x * x, axis=-1, keepdims=True) + EPS)
        n = x * r
        gn = ng_ref[i_n:i_n + 1, :]
        sc1 = 1.0 + mod_ref[i_sc:i_sc + 1, :]
        w = sc1 * gn
        dyn = dy * n
        col = jnp.sum(dyn, axis=0, keepdims=True)

        @pl.when(pl.program_id(0) == 0)
        def _():
            red_ref[...] = jnp.zeros_like(red_ref)

        red_ref[0:1, :] += jnp.sum(dy, axis=0, keepdims=True)
        red_ref[1:2, :] += gn * col
        red_ref[2:3, :] += sc1 * col
        dh_new = dhp_ref[...] + r * (dy * w - n * jnp.mean(dyn * w, axis=-1, keepdims=True))
        dh_ref[...] = dh_new
        if gate:
            _gate_emit(dh_new, gate, refs[2 * npair + 4], refs[2 * npair + 5], refs[-2], refs[-1])

    in_specs = ([pl.BlockSpec((tm, kb), lambda i, c=c: (i, c)) for (_, c, _, _, kb) in pairs]
                + [_wspec(w, kb, lambda i, r=r: r, resident=True) for (_, _, w, r, kb) in pairs]
                + [pl.BlockSpec((tm, D), lambda i: (i, 0)),
                   pl.BlockSpec((tm, D), lambda i: (i, 0)),
                   pl.BlockSpec(ng.shape, lambda i: (0, 0)),
                   pl.BlockSpec(mod.shape, lambda i: (0, 0))] + g_in + [_ANY] * len(after))
    return pl.pallas_call(
        body, name=name, grid=(T // tm,), in_specs=in_specs,
        out_specs=[pl.BlockSpec((tm, D), lambda i: (i, 0)), pl.BlockSpec((8, D), lambda i: (0, 0))] + g_out,
        out_shape=[jax.ShapeDtypeStruct((T, D), F32), jax.ShapeDtypeStruct((8, D), F32)] + g_shape,
        compiler_params=_params(("arbitrary",)),
    )(*[p[0] for p in pairs], *[p[2][0] for p in pairs], h, dhp, ng, mod, *g_ops, *after)


def matmul_tn(a, b, buf, slot, row0, name, tmo_cap=1408):
    T, N = b.shape
    ma = a.shape[1]
    tmo = _tile(ma, tmo_cap, 128)
    assert row0 % tmo == 0
    nmo = ma // tmo
    tk = _tile(T, 2048, 16)
    nk = T // tk

    def body(a_ref, b_ref, buf_ref, o_ref, acc_ref):
        k = pl.program_id(1)

        @pl.when(k == 0)
        def _():
            acc_ref[...] = jnp.zeros_like(acc_ref)

        acc_ref[...] += _tn(a_ref[...], b_ref[...])

        @pl.when(k == nk - 1)
        def _():
            o_ref[...] = acc_ref[...].astype(BF16)

    return pl.pallas_call(
        body, name=name, grid=(nmo, nk),
        in_specs=[pl.BlockSpec((tk, tmo), lambda j, k: (k, j)),
                  pl.BlockSpec((tk, N), lambda j, k: (k, 0)),
                  pl.BlockSpec(memory_space=pl.ANY)],
        out_specs=pl.BlockSpec((None, tmo, N), lambda j, k: (slot, row0 // tmo + j, 0)),
        out_shape=jax.ShapeDtypeStruct(buf.shape, BF16),
        scratch_shapes=[pltpu.VMEM((tmo, N), F32)],
        input_output_aliases={2: 0},
        compiler_params=_params(("parallel", "arbitrary")),
    )(a, b, buf)


def matmul_nt(x, w, name, after=()):
    T, K = x.shape
    N = _wrows(w)
    tm = _tile(T, 1024, 16)
    tn = _tile(N, 1280, 128)

    def body(x_ref, w_ref, *rest):
        rest[-1][...] = _nt(x_ref[...], w_ref[...]).astype(BF16)

    return pl.pallas_call(
        body, name=name, grid=(N // tn, T // tm),
        in_specs=[pl.BlockSpec((tm, K), lambda j, i: (i, 0)), _wspec(w, tn, lambda j, i: j)] + [_ANY] * len(after),
        out_specs=pl.BlockSpec((tm, tn), lambda j, i: (i, j)),
        out_shape=jax.ShapeDtypeStruct((T, N), BF16),
        compiler_params=_params(("parallel", "parallel")),
    )(x, w[0], *after)


def _sgu_head_fwd(u, v, lng, lnb):
    gu, dgu = _gelu_and_grad(u)
    gv, dgv = _gelu_and_grad(v)
    mu = jnp.mean(gv, axis=-1, keepdims=True)
    xc = gv - mu
    rstd = lax.rsqrt(jnp.mean(xc * xc, axis=-1, keepdims=True) + EPS)
    xhat = xc * rstd
    vn = xhat * lng + lnb
    return gu, dgu, dgv, rstd, xhat, vn


def _tril_mask():
    r = lax.broadcasted_iota(jnp.int32, (ATT_BLOCK, ATT_BLOCK), 0)
    c = lax.broadcasted_iota(jnp.int32, (ATT_BLOCK, ATT_BLOCK), 1)
    return c <= r


def _triu_mask():
    r = lax.broadcasted_iota(jnp.int32, (ATT_BLOCK, ATT_BLOCK), 0)
    c = lax.broadcasted_iota(jnp.int32, (ATT_BLOCK, ATT_BLOCK), 1)
    return r <= c


def sgu_fwd(proj, lng, lnb, w, bcol, name):
    T = proj.shape[0]
    tm = _tile(T, 512, 128)
    nch = tm // ATT_BLOCK

    def body(u_ref, v_ref, lng_ref, lnb_ref, w_ref, b_ref, o_ref):
        tril = _tril_mask()
        for hd in range(SGU_HEADS):
            sl = slice(hd * HEAD_LANES, (hd + 1) * HEAD_LANES)
            u = u_ref[:, sl].astype(F32)
            v = v_ref[:, sl].astype(F32)
            gu, _, _, _, _, vn = _sgu_head_fwd(u, v, lng_ref[:, sl], lnb_ref[:, sl])
            wm = jnp.where(tril, w_ref[hd], 0.0).astype(BF16)
            vnb = vn.astype(BF16)
            bc = b_ref[:, hd:hd + 1]
            for ch in range(nch):
                rs = slice(ch * ATT_BLOCK, (ch + 1) * ATT_BLOCK)
                z = _nn(wm, vnb[rs, :]) + bc
                o_ref[rs, sl] = (gu[rs, :] * z).astype(BF16)

    return pl.pallas_call(
        body, name=name, grid=(T // tm,),
        in_specs=[pl.BlockSpec((tm, MIX_HALF), lambda i: (i, 0)),
                  pl.BlockSpec((tm, MIX_HALF), lambda i: (i, 1)),
                  pl.BlockSpec((1, MIX_HALF), lambda i: (0, 0)),
                  pl.BlockSpec((1, MIX_HALF), lambda i: (0, 0)),
                  pl.BlockSpec(w.shape, lambda i: (0, 0, 0)),
                  pl.BlockSpec(bcol.shape, lambda i: (0, 0))],
        out_specs=pl.BlockSpec((tm, MIX_HALF), lambda i: (i, 0)),
        out_shape=jax.ShapeDtypeStruct((T, MIX_HALF), BF16),
        compiler_params=_params(("parallel",)),
    )(proj, proj, lng, lnb, w, bcol)


def sgu_bwd(proj, dmixed, lng, lnb, w, wt, bcol, name):
    T = proj.shape[0]
    tm = _tile(T, 512, 128)
    nch = tm // ATT_BLOCK
    nsteps = T // tm

    def body(u_ref, v_ref, g_ref, lng_ref, lnb_ref, w_ref, wt_ref, b_ref, duv_ref, dw_ref, dvec_ref, bacc_ref):
        step = pl.program_id(0)

        @pl.when(step == 0)
        def _():
            dw_ref[...] = jnp.zeros_like(dw_ref)
            dvec_ref[...] = jnp.zeros_like(dvec_ref)
            bacc_ref[...] = jnp.zeros_like(bacc_ref)

        tril = _tril_mask()
        triu = _triu_mask()
        for hd in range(SGU_HEADS):
            sl = slice(hd * HEAD_LANES, (hd + 1) * HEAD_LANES)
            u = u_ref[:, sl].astype(F32)
            v = v_ref[:, sl].astype(F32)
            lng_h = lng_ref[:, sl]
            gu, dgu, dgv, rstd, xhat, vn = _sgu_head_fwd(u, v, lng_h, lnb_ref[:, sl])
            wm = jnp.where(tril, w_ref[hd], 0.0).astype(BF16)
            wmt = jnp.where(triu, wt_ref[hd], 0.0).astype(BF16)
            vnb = vn.astype(BF16)
            bc = b_ref[:, hd:hd + 1]
            g = g_ref[:, sl].astype(F32)
            dw_acc = jnp.zeros((ATT_BLOCK, ATT_BLOCK), F32)
            b_acc = jnp.zeros((ATT_BLOCK, HEAD_LANES), F32)
            dvn_parts = []
            for ch in range(nch):
                rs = slice(ch * ATT_BLOCK, (ch + 1) * ATT_BLOCK)
                z = _nn(wm, vnb[rs, :]) + bc
                duv_ref[rs, sl] = (g[rs, :] * z * dgu[rs, :]).astype(BF16)
                dz = g[rs, :] * gu[rs, :]
                dzb = dz.astype(BF16)
                dvn_parts.append(_nn(wmt, dzb))
                dw_acc = dw_acc + _nt(dzb, vnb[rs, :])
                b_acc = b_acc + dz
            dvn = jnp.concatenate(dvn_parts, axis=0)
            dw_ref[hd] += jnp.where(tril, dw_acc, 0.0)
            bacc_ref[hd] += b_acc
            dvec_ref[0:1, sl] += jnp.sum(dvn * xhat, axis=0, keepdims=True)
            dvec_ref[1:2, sl] += jnp.sum(dvn, axis=0, keepdims=True)
            dxh = dvn * lng_h
            dgv_in = rstd * (dxh - jnp.mean(dxh, axis=-1, keepdims=True)
                             - xhat * jnp.mean(dxh * xhat, axis=-1, keepdims=True))
            duv_ref[:, MIX_HALF + hd * HEAD_LANES:MIX_HALF + (hd + 1) * HEAD_LANES] = (dgv_in * dgv).astype(BF16)

        @pl.when(step == nsteps - 1)
        def _():
            for hd in range(SGU_HEADS):
                sl = slice(hd * HEAD_LANES, (hd + 1) * HEAD_LANES)
                dvec_ref[2:3, sl] = jnp.sum(bacc_ref[hd].T, axis=0, keepdims=True)

    return pl.pallas_call(
        body, name=name, grid=(nsteps,),
        in_specs=[pl.BlockSpec((tm, MIX_HALF), lambda i: (i, 0)),
                  pl.BlockSpec((tm, MIX_HALF), lambda i: (i, 1)),
                  pl.BlockSpec((tm, MIX_HALF), lambda i: (i, 0)),
                  pl.BlockSpec((1, MIX_HALF), lambda i: (0, 0)),
                  pl.BlockSpec((1, MIX_HALF), lambda i: (0, 0)),
                  pl.BlockSpec(w.shape, lambda i: (0, 0, 0)),
                  pl.BlockSpec(w.shape, lambda i: (0, 0, 0)),
                  pl.BlockSpec(bcol.shape, lambda i: (0, 0))],
        out_specs=[pl.BlockSpec((tm, 2 * MIX_HALF), lambda i: (i, 0)),
                   pl.BlockSpec(w.shape, lambda i: (0, 0, 0)),
                   pl.BlockSpec((8, MIX_HALF), lambda i: (0, 0))],
        out_shape=[jax.ShapeDtypeStruct((T, 2 * MIX_HALF), BF16),
                   jax.ShapeDtypeStruct(w.shape, F32),
                   jax.ShapeDtypeStruct((8, MIX_HALF), F32)],
        scratch_shapes=[pltpu.VMEM((SGU_HEADS, ATT_BLOCK, HEAD_LANES), F32)],
        compiler_params=_params(("arbitrary",)),
    )(proj, proj, dmixed, lng, lnb, w, wt, bcol)


def _rot_half(t):
    lane = lax.broadcasted_iota(jnp.int32, t.shape, 1)
    first = (lane % ATT_DH) < (ATT_DH // 2)
    return jnp.where(first, -pltpu.roll(t, HEAD_LANES - ATT_DH // 2, 1), pltpu.roll(t, ATT_DH // 2, 1))


LAYOUT_ROWS = 512


def _res_spec(d, tm, W):
    return pl.BlockSpec((d, tm // d, W), lambda i: (0, i, 0))


def _res_shape(d, T, W, dtype):
    return jax.ShapeDtypeStruct((d, T // d, W), dtype)


def _slab_buf(tm, W):
    return pltpu.VMEM((W // HEAD_LANES, tm, HEAD_LANES), F32)


def _lanes(hp):
    return slice(hp * HEAD_LANES, (hp + 1) * HEAD_LANES)


def _to_res(buf, out_ref, d, dtype):
    nslab, tm, _ = buf.shape
    for hp in range(nslab):
        if d == 1:
            out_ref[0, :, _lanes(hp)] = buf[hp].astype(dtype)
        else:
            for r in range(d):
                out_ref[r, :, _lanes(hp)] = buf.at[hp][pl.ds(r, tm // d, stride=d), :].astype(dtype)


def _from_res(in_ref, buf, d):
    nslab, tm, _ = buf.shape
    for hp in range(nslab):
        if d == 1:
            buf[hp] = in_ref[0, :, _lanes(hp)].astype(F32)
        else:
            for r in range(d):
                buf.at[hp][pl.ds(r, tm // d, stride=d), :] = in_ref[r, :, _lanes(hp)].astype(F32)


def rope_fwd(proj, cos, sin, name):
    T = proj.shape[0]
    tm = LAYOUT_ROWS
    scale = 1.0 / math.sqrt(ATT_DH)
    nd = len(DILATIONS)

    def body(q_ref, k_ref, v_ref, cos_ref, sin_ref, *rest):
        outs, buf = rest[:3 * nd], rest[3 * nd]
        c = cos_ref[...]
        s = sin_ref[...]
        for which, src in enumerate((q_ref, k_ref, v_ref)):
            for hp in range(MIX_HALF // HEAD_LANES):
                t = src[:, _lanes(hp)].astype(F32)
                if which == 0:
                    t = scale * (t * c + _rot_half(t) * s)
                elif which == 1:
                    t = t * c + _rot_half(t) * s
                buf[hp] = t
            for di, d in enumerate(DILATIONS):
                _to_res(buf, outs[3 * di + which], d, BF16)

    return pl.pallas_call(
        body, name=name, grid=(T // tm,),
        in_specs=[pl.BlockSpec((tm, MIX_HALF), lambda i: (i, 2)),
                  pl.BlockSpec((tm, MIX_HALF), lambda i: (i, 3)),
                  pl.BlockSpec((tm, MIX_HALF), lambda i: (i, 4)),
                  pl.BlockSpec((tm, HEAD_LANES), lambda i: (i, 0)),
                  pl.BlockSpec((tm, HEAD_LANES), lambda i: (i, 0))],
        out_specs=[_res_spec(d, tm, MIX_HALF) for d in DILATIONS for _ in range(3)],
        out_shape=[_res_shape(d, T, MIX_HALF, BF16) for d in DILATIONS for _ in range(3)],
        scratch_shapes=[_slab_buf(tm, MIX_HALF)],
        compiler_params=_params(("parallel",)),
    )(proj, proj, proj, cos, sin)


def to_residues(x, col, name):
    T = x.shape[0]
    tm = LAYOUT_ROWS

    def body(x_ref, *rest):
        outs, buf = rest[:-1], rest[-1]
        for hp in range(MIX_HALF // HEAD_LANES):
            buf[hp] = x_ref[:, _lanes(hp)].astype(F32)
        for o_ref, d in zip(outs, DILATIONS):
            _to_res(buf, o_ref, d, BF16)

    return pl.pallas_call(
        body, name=name, grid=(T // tm,),
        in_specs=[pl.BlockSpec((tm, MIX_HALF), lambda i: (i, col))],
        out_specs=[_res_spec(d, tm, MIX_HALF) for d in DILATIONS],
        out_shape=[_res_shape(d, T, MIX_HALF, BF16) for d in DILATIONS],
        scratch_shapes=[_slab_buf(tm, MIX_HALF)],
        compiler_params=_params(("parallel",)),
    )(x)


def rope_bwd(dqs, dks, dvs, cos, sin, name):
    T = dqs[0].shape[0] * dqs[0].shape[1]
    tm = LAYOUT_ROWS
    scale = 1.0 / math.sqrt(ATT_DH)
    npat = len(dqs)

    def body(*refs):
        groups = refs[:npat], refs[npat:2 * npat], refs[2 * npat:3 * npat]
        cos_ref, sin_ref, o_ref, buf, acc = refs[3 * npat:]
        c = cos_ref[...]
        s = sin_ref[...]
        for which, g_refs in enumerate(groups):
            _from_res(g_refs[0], acc, DILATIONS[0])
            for g_ref, d in zip(g_refs[1:], DILATIONS[1:]):
                _from_res(g_ref, buf, d)
                acc[...] += buf[...]
            for hp in range(MIX_HALF // HEAD_LANES):
                g = acc[hp]
                if which == 0:
                    g = scale * g
                if which < 2:
                    g = g * c - _rot_half(g * s)
                o_ref[:, which * MIX_HALF + hp * HEAD_LANES:which * MIX_HALF + (hp + 1) * HEAD_LANES] = g.astype(BF16)

    return pl.pallas_call(
        body, name=name, grid=(T // tm,),
        in_specs=([_res_spec(d, tm, MIX_HALF) for _ in range(3) for d in DILATIONS]
                  + [pl.BlockSpec((tm, HEAD_LANES), lambda i: (i, 0))] * 2),
        out_specs=pl.BlockSpec((tm, 3 * MIX_HALF), lambda i: (i, 0)),
        out_shape=jax.ShapeDtypeStruct((T, 3 * MIX_HALF), BF16),
        scratch_shapes=[_slab_buf(tm, MIX_HALF), _slab_buf(tm, MIX_HALF)],
        compiler_params=_params(("parallel",)),
    )(*dqs, *dks, *dvs, cos, sin)


def _band_masks(n):
    r = lax.broadcasted_iota(jnp.int32, (2 * ATT_BLOCK, ATT_BLOCK), 0)
    c = lax.broadcasted_iota(jnp.int32, (2 * ATT_BLOCK, ATT_BLOCK), 1)
    qi = r % ATT_BLOCK
    head = (c < ATT_DH) == (r < ATT_BLOCK)
    return (c >= qi) & (n > 0), c <= qi, head, c[:ATT_BLOCK] < ATT_DH


def _stack_heads(x, head):
    x2 = jnp.concatenate([x, x], axis=0)
    return jnp.where(head, x2, jnp.zeros_like(x2))


def attn_fwd(q, k, v, name):
    d, L, W = q.shape
    nb = L // ATT_BLOCK
    nsub = 2 if nb % 2 == 0 else 1

    def body(q_ref, kp_ref, kc_ref, vp_ref, vc_ref, o_ref, lse_ref):
        step = pl.program_id(1)
        for u in range(nsub):
            rows = slice(u * ATT_BLOCK, (u + 1) * ATT_BLOCK)
            before = slice((u - 1) * ATT_BLOCK, u * ATT_BLOCK)
            mask_p, mask_c, head, head0 = _band_masks(step if u == 0 else 1)
            for hp in range(W // HEAD_LANES):
                sl = slice(hp * HEAD_LANES, (hp + 1) * HEAD_LANES)
                kp, vp = (kp_ref[0, :, sl], vp_ref[0, :, sl]) if u == 0 else (kc_ref[0, before, sl], vc_ref[0, before, sl])
                kc, vc = kc_ref[0, rows, sl], vc_ref[0, rows, sl]
                qs = _stack_heads(q_ref[0, rows, sl], head)
                sp = jnp.where(mask_p, _nt(qs, kp), NEG)
                sc = jnp.where(mask_c, _nt(qs, kc), NEG)
                m = jnp.maximum(jnp.max(sp, axis=1, keepdims=True), jnp.max(sc, axis=1, keepdims=True))
                pp = jnp.exp(sp - m)
                pc = jnp.exp(sc - m)
                den = jnp.sum(pp, axis=1, keepdims=True) + jnp.sum(pc, axis=1, keepdims=True)
                o = (_nn(pp.astype(BF16), vp) + _nn(pc.astype(BF16), vc)) / den
                lse = m + jnp.log(den)
                o_ref[0, rows, sl] = jnp.where(head0, o[:ATT_BLOCK], o[ATT_BLOCK:]).astype(BF16)
                lse_ref[0, rows, sl] = jnp.where(head0, lse[:ATT_BLOCK], lse[ATT_BLOCK:])

    cur = pl.BlockSpec((1, nsub * ATT_BLOCK, W), lambda r, n: (r, n, 0))
    prev = pl.BlockSpec((1, ATT_BLOCK, W), lambda r, n: (r, jnp.maximum(nsub * n - 1, 0), 0))
    return pl.pallas_call(
        body, name=name, grid=(d, nb // nsub),
        in_specs=[cur, prev, cur, prev, cur],
        out_specs=[cur, cur],
        out_shape=[jax.ShapeDtypeStruct((d, L, W), BF16), jax.ShapeDtypeStruct((d, L, W), F32)],
        compiler_params=_params(("parallel", "parallel")),
    )(q, k, k, v, v)


def attn_combine(os_, lses, name):
    T = os_[0].shape[0] * os_[0].shape[1]
    W = os_[0].shape[2]
    tm = LAYOUT_ROWS
    npat = len(os_)

    def body(*refs):
        o_refs, l_refs = refs[:npat], refs[npat:2 * npat]
        out_ref = refs[2 * npat]
        ores, lres = refs[2 * npat + 1:3 * npat + 1], refs[3 * npat + 1:4 * npat + 1]
        bufs = refs[4 * npat + 1:]
        lbufs, obufs, out_buf, lse_buf = bufs[:npat], bufs[npat:2 * npat], bufs[2 * npat], bufs[2 * npat + 1]
        for p, d in enumerate(DILATIONS):
            _from_res(l_refs[p], lbufs[p], d)
            _from_res(o_refs[p], obufs[p], d)
        for hp in range(W // HEAD_LANES):
            ls = [b[hp] for b in lbufs]
            m = ls[0]
            for l in ls[1:]:
                m = jnp.maximum(m, l)
            es = [jnp.exp(l - m) for l in ls]
            z = es[0]
            for e in es[1:]:
                z = z + e
            acc = es[0] * obufs[0][hp]
            for p in range(1, npat):
                acc = acc + es[p] * obufs[p][hp]
            out = acc / z
            out_ref[:, _lanes(hp)] = out.astype(BF16)
            out_buf[hp] = out
            lse_buf[hp] = m + jnp.log(z)
        for p, d in enumerate(DILATIONS):
            _to_res(out_buf, ores[p], d, BF16)
            _to_res(lse_buf, lres[p], d, F32)

    return pl.pallas_call(
        body, name=name, grid=(T // tm,),
        in_specs=[_res_spec(d, tm, W) for _ in range(2) for d in DILATIONS],
        out_specs=([pl.BlockSpec((tm, W), lambda i: (i, 0))] + [_res_spec(d, tm, W) for _ in range(2) for d in DILATIONS]),
        out_shape=([jax.ShapeDtypeStruct((T, W), BF16)] + [_res_shape(d, T, W, BF16) for d in DILATIONS]
                   + [_res_shape(d, T, W, F32) for d in DILATIONS]),
        scratch_shapes=[_slab_buf(tm, W)] * (2 * npat + 2),
        compiler_params=_params(("parallel",)),
    )(*os_, *lses)


def attn_bwd(q, k, v, do, o, lse, name):
    d, L, W = q.shape
    nb = L // ATT_BLOCK
    nsub = 2 if nb % 2 == 0 else 1
    per_seq = nb // nsub
    nst = d * per_seq
    nb = d * nb
    last = slice((nsub - 1) * ATT_BLOCK, nsub * ATT_BLOCK)
    q, k, v, do, o, lse = (t.reshape(1, d * L, W) for t in (q, k, v, do, o, lse))

    def body(q_ref, kp_ref, kc_ref, vp_ref, vc_ref, do_ref, o_ref, lse_ref, dq_ref, dk_ref, dv_ref, kkeep, vkeep):
        step = pl.program_id(1)
        n = step % per_seq

        @pl.when(step == 0)
        def _():
            kkeep[...] = jnp.zeros_like(kkeep)
            vkeep[...] = jnp.zeros_like(vkeep)

        @pl.when(step < nst)
        def _():
            for hp in range(W // HEAD_LANES):
                sl = slice(hp * HEAD_LANES, (hp + 1) * HEAD_LANES)
                shares = []
                for u in range(nsub):
                    rows = slice(u * ATT_BLOCK, (u + 1) * ATT_BLOCK)
                    before = slice((u - 1) * ATT_BLOCK, u * ATT_BLOCK)
                    mask_p, mask_c, head, head0 = _band_masks(n if u == 0 else 1)
                    kp, vp = (kp_ref[0, :, sl], vp_ref[0, :, sl]) if u == 0 else (kc_ref[0, before, sl], vc_ref[0, before, sl])
                    kc, vc = kc_ref[0, rows, sl], vc_ref[0, rows, sl]
                    dout = do_ref[0, rows, sl]
                    qs = _stack_heads(q_ref[0, rows, sl], head)
                    dos = _stack_heads(dout, head)
                    lse_v = lse_ref[0, rows, sl]
                    lse_c = jnp.max(jnp.where(head, jnp.concatenate([lse_v, lse_v], axis=0), NEG), axis=1, keepdims=True)
                    delta = jnp.sum(_stack_heads(dout.astype(F32) * o_ref[0, rows, sl].astype(F32), head), axis=1,
                                    keepdims=True)
                    pp = jnp.exp(jnp.where(mask_p, _nt(qs, kp), NEG) - lse_c)
                    pc = jnp.exp(jnp.where(mask_c, _nt(qs, kc), NEG) - lse_c)
                    dsp = (pp * (_nt(dos, vp) - delta)).astype(BF16)
                    dsc = (pc * (_nt(dos, vc) - delta)).astype(BF16)
                    dq2 = _nn(dsp, kp) + _nn(dsc, kc)
                    dq_ref[0, rows, sl] = jnp.where(head0, dq2[:ATT_BLOCK], dq2[ATT_BLOCK:]).astype(BF16)
                    shares.append((_tn(dsp, qs), _tn(pp.astype(BF16), dos), _tn(dsc, qs), _tn(pc.astype(BF16), dos)))
                dk_ref[0, last, sl] = (kkeep[last, sl] + shares[0][0]).astype(BF16)
                dv_ref[0, last, sl] = (vkeep[last, sl] + shares[0][1]).astype(BF16)
                if nsub == 2:
                    dk_ref[0, :ATT_BLOCK, sl] = kkeep[:ATT_BLOCK, sl].astype(BF16)
                    dv_ref[0, :ATT_BLOCK, sl] = vkeep[:ATT_BLOCK, sl].astype(BF16)
                    kkeep[:ATT_BLOCK, sl] = shares[0][2] + shares[1][0]
                    vkeep[:ATT_BLOCK, sl] = shares[0][3] + shares[1][1]
                kkeep[last, sl] = shares[-1][2]
                vkeep[last, sl] = shares[-1][3]

        @pl.when(step == nst)
        def _():
            dk_ref[0] = kkeep[...].astype(BF16)
            dv_ref[0] = vkeep[...].astype(BF16)

    rows_per_step = nsub * ATT_BLOCK
    cur = pl.BlockSpec((1, rows_per_step, W), lambda r, n: (r, jnp.minimum(n, nst - 1), 0))
    lag = pl.BlockSpec((1, rows_per_step, W), lambda r, n: (r, jnp.clip(n - 1, 0, nst - 1), 0))
    prev = pl.BlockSpec((1, ATT_BLOCK, W), lambda r, n: (r, jnp.clip(nsub * n - 1, 0, nb - 1), 0))
    out = jax.ShapeDtypeStruct((1, d * L, W), BF16)
    outs = pl.pallas_call(
        body, name=name, grid=(1, nst + 1),
        in_specs=[cur, prev, cur, prev, cur, cur, cur, cur],
        out_specs=[cur, lag, lag], out_shape=[out, out, out],
        scratch_shapes=[pltpu.VMEM((rows_per_step, W), F32), pltpu.VMEM((rows_per_step, W), F32)],
        compiler_params=_params(("parallel", "arbitrary")),
    )(q, k, k, v, v, do, o, lse)
    return [t.reshape(d, L, W) for t in outs]


def final_loss_bwd(h, gf, tgt, gate, name):
    T, D = h.shape
    tm = _tile(T, 512, 16)
    g_in, g_out, g_shape, g_ops = _gate_specs(gate, tm, D)

    def body(h_ref, g_ref, t_ref, o_ref, modg_ref, dh_ref, red_ref, do_ref, redg_ref):
        x = h_ref[...]
        r = lax.rsqrt(jnp.mean(x * x, axis=-1, keepdims=True) + EPS)
        n = x * r
        g = g_ref[...]
        err = n * g - t_ref[...]
        dy = err * (1.0 / D)

        @pl.when(pl.program_id(0) == 0)
        def _():
            red_ref[...] = jnp.zeros_like(red_ref)

        red_ref[0:1, :] += jnp.sum(dy * n, axis=0, keepdims=True)
        red_ref[1:2, :] += jnp.zeros((1, D), F32) + (0.5 / D) * jnp.sum(err * err, keepdims=True)
        dn = dy * g
        dh = r * (dn - n * jnp.mean(dn * n, axis=-1, keepdims=True))
        dh_ref[...] = dh
        _gate_emit(dh, gate, o_ref, modg_ref, do_ref, redg_ref)

    return pl.pallas_call(
        body, name=name, grid=(T // tm,),
        in_specs=[pl.BlockSpec((tm, D), lambda i: (i, 0)),
                  pl.BlockSpec((1, D), lambda i: (0, 0)),
                  pl.BlockSpec((tm, D), lambda i: (i, 0))] + g_in,
        out_specs=[pl.BlockSpec((tm, D), lambda i: (i, 0)), pl.BlockSpec((8, D), lambda i: (0, 0))] + g_out,
        out_shape=[jax.ShapeDtypeStruct((T, D), F32), jax.ShapeDtypeStruct((8, D), F32)] + g_shape,
        compiler_params=_params(("arbitrary",)),
    )(h, gf, tgt, *g_ops)


def ada_fwd(c_all, ada_w, ada_b, name):
    nl, D, N = ada_w.shape

    def body(c_ref, w_ref, b_ref, o_ref):
        c = c_ref[...]
        o_ref[0] = _nn(c * _sigmoid(c), w_ref[0]) + b_ref[0]

    return pl.pallas_call(
        body, name=name, grid=(nl,),
        in_specs=[pl.BlockSpec((N_DEV, D), lambda l: (0, 0)),
                  pl.BlockSpec((1, D, N), lambda l: (l, 0, 0)),
                  pl.BlockSpec((1, 1, N), lambda l: (l, 0, 0))],
        out_specs=pl.BlockSpec((1, N_DEV, N), lambda l: (l, 0, 0)),
        out_shape=jax.ShapeDtypeStruct((nl, N_DEV, N), F32),
        compiler_params=_params(("parallel",)),
    )(c_all, ada_w, ada_b)


def ada_bwd(c_allT, dmod, name):
    nl, _, N = dmod.shape
    D = c_allT.shape[0]

    def body(c_ref, g_ref, o_ref):
        c = c_ref[...]
        ca = c * _sigmoid(c)
        acc = ca[:, 0:1] * g_ref[0, 0:1, :]
        for b in range(1, N_DEV):
            acc = acc + ca[:, b:b + 1] * g_ref[0, b:b + 1, :]
        o_ref[0] = acc

    return pl.pallas_call(
        body, name=name, grid=(nl,),
        in_specs=[pl.BlockSpec((D, N_DEV), lambda l: (0, 0)),
                  pl.BlockSpec((1, N_DEV, N), lambda l: (l, 0, 0))],
        out_specs=pl.BlockSpec((1, D, N), lambda l: (l, 0, 0)),
        out_shape=jax.ShapeDtypeStruct((nl, D, N), F32),
        compiler_params=_params(("parallel",)),
    )(c_allT, dmod)


def adamw(w, g, m, v, name):
    R, C = w.shape
    tr = _tile(R, max(8, (1 << 19) // C // 8 * 8), 8)
    c1 = 1.0 - ADAM_B1 ** ADAM_STEP
    c2 = 1.0 - ADAM_B2 ** ADAM_STEP

    def body(w_ref, g_ref, m_ref, v_ref, d_ref, mo_ref, vo_ref):
        gv = g_ref[...]
        mn = ADAM_B1 * m_ref[...] + (1.0 - ADAM_B1) * gv
        vn = ADAM_B2 * v_ref[...] + (1.0 - ADAM_B2) * (gv * gv)
        mo_ref[...] = mn
        vo_ref[...] = vn
        d_ref[...] = -ADAM_LR * ((mn / c1) / (jnp.sqrt(vn / c2) + ADAM_EPS) + ADAM_WD * w_ref[...])

    blk = pl.BlockSpec((tr, C), lambda i: (i, 0))
    out = jax.ShapeDtypeStruct((R, C), F32)
    return pl.pallas_call(
        body, name=name, grid=(R // tr,),
        in_specs=[blk] * 4, out_specs=[blk] * 3, out_shape=[out] * 3,
        compiler_params=_params(("parallel",)),
    )(w, g, m, v)


def adamw_layer(w, g, m, v, l, prev, name):
    NLw, R, C = w.shape
    tr = _tile(R, max(8, (1 << 19) // C // 8 * 8), 8)
    nrb = R // tr
    c1 = 1.0 - ADAM_B1 ** ADAM_STEP
    c2 = 1.0 - ADAM_B2 ** ADAM_STEP
    w, m, v = (t.reshape(NLw * R, C) for t in (w, m, v))

    def body(w_ref, g_ref, m_ref, v_ref, *rest):
        go_ref, d_ref, mo_ref, vo_ref = rest[-4:]
        gv = g_ref[...]
        mn = ADAM_B1 * m_ref[...] + (1.0 - ADAM_B1) * gv
        vn = ADAM_B2 * v_ref[...] + (1.0 - ADAM_B2) * (gv * gv)
        go_ref[...] = gv
        mo_ref[...] = mn
        vo_ref[...] = vn
        d_ref[...] = -ADAM_LR * ((mn / c1) / (jnp.sqrt(vn / c2) + ADAM_EPS) + ADAM_WD * w_ref[...])

    lay = pl.BlockSpec((tr, C), lambda i: (l * nrb + i, 0))
    out = jax.ShapeDtypeStruct((NLw * R, C), F32)
    n_prev = 0 if prev is None else 4
    return pl.pallas_call(
        body, name=name, grid=(nrb,),
        in_specs=[lay, pl.BlockSpec((tr, C), lambda i: (i, 0)), lay, lay] + [pl.BlockSpec(memory_space=pl.ANY)] * n_prev,
        out_specs=[lay] * 4, out_shape=[out] * 4,
        input_output_aliases={4 + i: i for i in range(n_prev)},
        compiler_params=_params(("parallel",)),
    )(w, g, m, v, *(prev or ()))


def sum_slots(x, name):
    S, R, C = x.shape
    tr = _tile(R, 128, 8)

    def body(x_ref, o_ref):
        acc = x_ref[0]
        for s in range(1, S):
            acc = acc + x_ref[s]
        o_ref[...] = acc

    return pl.pallas_call(
        body, name=name, grid=(R // tr,),
        in_specs=[pl.BlockSpec((S, tr, C), lambda i: (0, i, 0))],
        out_specs=pl.BlockSpec((tr, C), lambda i: (i, 0)),
        out_shape=jax.ShapeDtypeStruct((R, C), F32),
        compiler_params=_params(("parallel",)),
    )(x)


def sum_halves(g, lands, c_idx, name):
    n, ns, _, rh, D = g.shape

    def body(c_ref, g_ref, l_ref, o_ref):
        for j in range(ns):
            o_ref[0, j] = (g_ref[0, j, 0].astype(F32) + l_ref[0, j].astype(F32)).astype(BF16)

    return pl.pallas_call(
        body, name=name,
        grid_spec=pltpu.PrefetchScalarGridSpec(
            num_scalar_prefetch=1, grid=(n,),
            in_specs=[pl.BlockSpec((1, ns, 1, rh, D), lambda i, c: (i, 0, c[0], 0, 0)),
                      pl.BlockSpec((1, ns, rh, D), lambda i, c: (i, 0, 0, 0))],
            out_specs=pl.BlockSpec((1, ns, rh, D), lambda i, c: (i, 0, 0, 0))),
        out_shape=jax.ShapeDtypeStruct((n, ns, rh, D), BF16),
        compiler_params=_params(("parallel",)),
    )(c_idx, g, lands)


def sum_chips(p, lands, place, name):
    n, ns, rh, D = p.shape

    def body(c_ref, p_ref, l_ref, o_ref):
        acc = p_ref[0, 0].astype(F32)
        for j in range(N_CHIP - 1):
            acc = acc + l_ref[j, 0].astype(F32)
        o_ref[0, 0] = acc

    return pl.pallas_call(
        body, name=name,
        grid_spec=pltpu.PrefetchScalarGridSpec(
            num_scalar_prefetch=1, grid=(n,),
            in_specs=[pl.BlockSpec((1, 1, rh, D), lambda i, c: (i, c[0], 0, 0)),
                      pl.BlockSpec((N_CHIP - 1, 1, rh, D), lambda i, c: (0, i, 0, 0))],
            out_specs=pl.BlockSpec((1, 1, rh, D), lambda i, c: (i, c[1], 0, 0))),
        out_shape=jax.ShapeDtypeStruct((n, 2, rh, D), F32),
        compiler_params=_params(("parallel",)),
    )(place, p, lands)


def _my_place():
    return lax.axis_index("x"), lax.axis_index("y"), lax.axis_index("c")


def _other_chips(mx, my):
    return [(1 - mx, my), (mx, 1 - my), (1 - mx, 1 - my)]


def gather_small(x, after, name):
    def body(x_ref, after_ref, out_ref, sum_ref, send_sems, recv_sems):
        mx, my, mc = _my_place()
        me = 4 * mx + 2 * my + mc
        out_ref[me] = x_ref[...]
        sends = []
        for k in range(1, N_DEV):
            kx, ky, kc = (k >> 2) & 1, (k >> 1) & 1, k & 1
            peer = (1 - mx if kx else mx, 1 - my if ky else my, 1 - mc if kc else mc)
            cp = pltpu.make_async_remote_copy(
                src_ref=x_ref, dst_ref=out_ref.at[me], send_sem=send_sems.at[k - 1], recv_sem=recv_sems.at[k - 1],
                device_id=peer, device_id_type=MESH)
            cp.start()
            sends.append((cp, 4 * peer[0] + 2 * peer[1] + peer[2], peer))
        for k, (cp, peer_slot, peer) in enumerate(sends):
            pltpu.make_async_remote_copy(
                src_ref=x_ref, dst_ref=out_ref.at[peer_slot], send_sem=send_sems.at[k], recv_sem=recv_sems.at[k],
                device_id=peer, device_id_type=MESH).wait_recv()
        for cp, _, _ in sends:
            cp.wait_send()
        acc = out_ref[0]
        for s in range(1, N_DEV):
            acc = acc + out_ref[s]
        sum_ref[...] = acc

    vmem = pl.BlockSpec(memory_space=pltpu.VMEM)
    return pl.pallas_call(
        body, name=name,
        in_specs=[vmem, pl.BlockSpec(memory_space=pl.ANY)], out_specs=[vmem, vmem],
        out_shape=[jax.ShapeDtypeStruct((N_DEV,) + x.shape, x.dtype), jax.ShapeDtypeStruct(x.shape, x.dtype)],
        scratch_shapes=[pltpu.SemaphoreType.DMA((N_DEV - 1,)), pltpu.SemaphoreType.DMA((N_DEV - 1,))],
        compiler_params=pltpu.CompilerParams(vmem_limit_bytes=VMEM_LIMIT),
    )(x, after)


_HBM =pl.BlockSpec(memory_space=pltpu.HBM)
_SEM = pl.BlockSpec(memory_space=pltpu.SEMAPHORE)
_DATAFLOW = pltpu.SideEffectType.DATAFLOW_SIDE_EFFECTING


def _gather_copies(shard, land, send, recv, base):
    mx, my, mc = _my_place()
    ci = 2 * mx + my
    peers = [((cx, cy, mc), 2 * cx + cy) for cx, cy in _other_chips(mx, my)] + [((mx, my, 1 - mc), ci)]
    out = []
    for q, (dev, src_slot) in enumerate(peers):
        out.append((
            pltpu.make_async_remote_copy(src_ref=shard, dst_ref=land.at[:, ci], send_sem=send.at[base + q],
                                         recv_sem=recv.at[base + q], device_id=dev, device_id_type=MESH),
            pltpu.make_async_remote_copy(src_ref=shard, dst_ref=land.at[:, src_slot], send_sem=send.at[base + q],
                                         recv_sem=recv.at[base + q], device_id=dev, device_id_type=MESH)))
    return out


def gather_start(groups, after, name):
    items = [s for g in groups for s in g]
    ni, ng = len(items), len(groups)

    def body(*refs):
        shards, lands = refs[:ni], refs[ni:2 * ni]
        sems = refs[2 * ni + 1:2 * ni + 1 + 2 * ng]
        token = refs[-1]
        i = 0
        for g, grp in enumerate(groups):
            for p in range(len(grp)):
                for start_cp, _ in _gather_copies(shards[i], lands[i], sems[2 * g], sems[2 * g + 1], 4 * p):
                    start_cp.start()
                i += 1
        token[...] = jnp.zeros_like(token)

    sem_shapes = []
    for grp in groups:
        sem_shapes += [pltpu.SemaphoreType.DMA((4 * len(grp),))] * 2
    land_shapes = [(s.shape[0], N_CHIP) + s.shape[1:] for s in items]
    outs = pl.pallas_call(
        body, name=name,
        in_specs=[_HBM] * (2 * ni) + [pl.BlockSpec(memory_space=pl.ANY)],
        out_specs=[_SEM] * (2 * ng) + [_HBM] * (2 * ni) + [pl.BlockSpec(memory_space=pltpu.VMEM)],
        out_shape=(sem_shapes + [pltpu.HBM(s.shape, s.dtype) for s in items]
                   + [pltpu.HBM(ls, s.dtype) for ls, s in zip(land_shapes, items)]
                   + [jax.ShapeDtypeStruct((8, 128), F32)]),
        input_output_aliases={i: 2 * ng + i for i in range(2 * ni)},
        compiler_params=pltpu.CompilerParams(has_side_effects=_DATAFLOW),
    )(*[pltpu.with_memory_space_constraint(s, pltpu.HBM) for s in items],
      *[pltpu.with_memory_space_constraint(lax.empty(ls, s.dtype), pltpu.HBM) for ls, s in zip(land_shapes, items)],
      after)
    sems, thru, token = outs[:2 * ng], outs[2 * ng:2 * ng + 2 * ni], outs[-1]
    handles, i = [], 0
    for g, grp in enumerate(groups):
        n = len(grp)
        handles.append((sems[2 * g], sems[2 * g + 1], thru[i:i + n], thru[ni + i:ni + i + n]))
        i += n
    return handles, token


def gather_wait(handle, after, name):
    send, recv, shards, lands = handle
    n = len(shards)

    def body(*refs):
        shard_refs, land_refs = refs[:n], refs[n:2 * n]
        send_ref, recv_ref = refs[2 * n], refs[2 * n + 1]
        for p in range(n):
            for start_cp, recv_cp in _gather_copies(shard_refs[p], land_refs[p], send_ref, recv_ref, 4 * p):
                start_cp.wait_send()
                recv_cp.wait_recv()

    outs = pl.pallas_call(
        body, name=name,
        in_specs=[_HBM] * (2 * n) + [_SEM, _SEM, pl.BlockSpec(memory_space=pl.ANY)],
        out_specs=[_HBM] * (2 * n),
        out_shape=[pltpu.HBM(s.shape, s.dtype) for s in shards] + [pltpu.HBM(l.shape, l.dtype) for l in lands],
        input_output_aliases={i: i for i in range(2 * n)},
        compiler_params=pltpu.CompilerParams(has_side_effects=_DATAFLOW),
    )(*shards, *lands, send, recv, after)
    return outs[n:]


def _first_copies(shard, land, send, recv):
    mx, my, mc = _my_place()
    ci = 2 * mx + my
    out = []
    for q, (cx, cy) in enumerate(_other_chips(mx, my)):
        dev = (cx, cy, mc)
        out.append(tuple(pltpu.make_async_remote_copy(
            src_ref=shard.at[:, mc], dst_ref=land.at[:, slot, mc], send_sem=send.at[q], recv_sem=recv.at[q],
            device_id=dev, device_id_type=MESH) for slot in (ci, 2 * cx + cy)))
    sib = pltpu.make_async_remote_copy(src_ref=shard, dst_ref=land.at[:, ci], send_sem=send.at[3], recv_sem=recv.at[3],
                                       device_id=(mx, my, 1 - mc), device_id_type=MESH)
    return out + [(sib, sib)]


def _forward_copies(land, send, recv):
    mx, my, mc = _my_place()
    out = []
    for q, (cx, cy) in enumerate(_other_chips(mx, my)):
        out.append(tuple(pltpu.make_async_remote_copy(
            src_ref=land.at[:, 2 * cx + cy, hc], dst_ref=land.at[:, 2 * cx + cy, hc], send_sem=send.at[q],
            recv_sem=recv.at[q], device_id=(mx, my, 1 - mc), device_id_type=MESH) for hc in (mc, 1 - mc)))
    return out


def first_start(shard, after, name):
    def body(shard_ref, land_ref, after_ref, send, recv, shard_thru, land_thru, token):
        for mine, _ in _first_copies(shard_ref, land_ref, send, recv):
            mine.start()
        token[...] = jnp.zeros_like(token)

    land_shape = (shard.shape[0], N_CHIP) + shard.shape[1:]
    outs = pl.pallas_call(
        body, name=name,
        in_specs=[_HBM, _HBM, pl.BlockSpec(memory_space=pl.ANY)],
        out_specs=[_SEM, _SEM, _HBM, _HBM, pl.BlockSpec(memory_space=pltpu.VMEM)],
        out_shape=[pltpu.SemaphoreType.DMA((4,))] * 2 + [pltpu.HBM(shard.shape, shard.dtype),
                                                         pltpu.HBM(land_shape, shard.dtype),
                                                         jax.ShapeDtypeStruct((8, 128), F32)],
        input_output_aliases={0: 2, 1: 3},
        compiler_params=pltpu.CompilerParams(has_side_effects=_DATAFLOW),
    )(pltpu.with_memory_space_constraint(shard, pltpu.HBM),
      pltpu.with_memory_space_constraint(lax.empty(land_shape, shard.dtype), pltpu.HBM), after)
    return outs[:4], outs[4]


def first_forward(handle, after, name):
    send, recv, shard, land = handle

    def body(shard_ref, land_ref, send_ref, recv_ref, after_ref, send2, recv2, shard_thru, land_thru):
        firsts = _first_copies(shard_ref, land_ref, send_ref, recv_ref)
        forwards = _forward_copies(land_ref, send2, recv2)
        for q in range(3):
            firsts[q][1].wait_recv()
            forwards[q][0].start()
        firsts[3][1].wait_recv()
        for mine, _ in firsts:
            mine.wait_send()

    outs = pl.pallas_call(
        body, name=name,
        in_specs=[_HBM, _HBM, _SEM, _SEM, pl.BlockSpec(memory_space=pl.ANY)],
        out_specs=[_SEM, _SEM, _HBM, _HBM],
        out_shape=[pltpu.SemaphoreType.DMA((3,))] * 2 + [pltpu.HBM(shard.shape, shard.dtype),
                                                         pltpu.HBM(land.shape, land.dtype)],
        input_output_aliases={0: 2, 1: 3},
        compiler_params=pltpu.CompilerParams(has_side_effects=_DATAFLOW),
    )(shard, land, send, recv, after)
    return outs[0], outs[1], outs[3]


def first_wait(handle, after, name):
    send, recv, land = handle

    def body(land_ref, send_ref, recv_ref, after_ref, land_out):
        for mine, theirs in _forward_copies(land_ref, send_ref, recv_ref):
            mine.wait_send()
            theirs.wait_recv()

    return pl.pallas_call(
        body, name=name,
        in_specs=[_HBM, _SEM, _SEM, pl.BlockSpec(memory_space=pl.ANY)],
        out_specs=[_HBM],
        out_shape=[pltpu.HBM(land.shape, land.dtype)],
        input_output_aliases={0: 0},
        compiler_params=pltpu.CompilerParams(has_side_effects=_DATAFLOW),
    )(land, send, recv, after)[0]


def _sibling_copies(gs, lands, send, recv):
    mx, my, mc = _my_place()
    return [pltpu.make_async_remote_copy(
        src_ref=gs[k].at[:, :, 1 - mc], dst_ref=lands[k], send_sem=send.at[k], recv_sem=recv.at[k],
        device_id=(mx, my, 1 - mc), device_id_type=MESH) for k in range(len(gs))]


def sibling_start(gs, after, name):
    K = len(gs)

    def body(*refs):
        ins, lands = refs[:K], refs[K:2 * K]
        send, recv = refs[2 * K + 1], refs[2 * K + 2]
        for cp in _sibling_copies(ins, lands, send, recv):
            cp.start()
        refs[-1][...] = jnp.zeros_like(refs[-1])

    land_shapes = [g.shape[:2] + g.shape[3:] for g in gs]
    outs = pl.pallas_call(
        body, name=name,
        in_specs=[_HBM] * (2 * K) + [pl.BlockSpec(memory_space=pl.ANY)],
        out_specs=[_SEM, _SEM] + [_HBM] * (2 * K) + [pl.BlockSpec(memory_space=pltpu.VMEM)],
        out_shape=([pltpu.SemaphoreType.DMA((K,))] * 2 + [pltpu.HBM(g.shape, g.dtype) for g in gs]
                   + [pltpu.HBM(ls, g.dtype) for ls, g in zip(land_shapes, gs)] + [jax.ShapeDtypeStruct((8, 128), F32)]),
        input_output_aliases={i: 2 + i for i in range(2 * K)},
        compiler_params=pltpu.CompilerParams(has_side_effects=_DATAFLOW),
    )(*[pltpu.with_memory_space_constraint(g, pltpu.HBM) for g in gs],
      *[pltpu.with_memory_space_constraint(lax.empty(ls, g.dtype), pltpu.HBM) for ls, g in zip(land_shapes, gs)],
      after)
    return (outs[0], outs[1], outs[2:2 + K], outs[2 + K:2 + 2 * K]), outs[-1]


def sibling_wait(handle, after, name):
    send, recv, gs, lands = handle
    K = len(gs)

    def body(*refs):
        ins, land_refs = refs[:K], refs[K:2 * K]
        for cp in _sibling_copies(ins, land_refs, refs[2 * K], refs[2 * K + 1]):
            cp.wait_send()
            cp.wait_recv()

    outs = pl.pallas_call(
        body, name=name,
        in_specs=[_HBM] * (2 * K) + [_SEM, _SEM, pl.BlockSpec(memory_space=pl.ANY)],
        out_specs=[_HBM] * (2 * K),
        out_shape=[pltpu.HBM(g.shape, g.dtype) for g in gs] + [pltpu.HBM(l.shape, l.dtype) for l in lands],
        input_output_aliases={i: i for i in range(2 * K)},
        compiler_params=pltpu.CompilerParams(has_side_effects=_DATAFLOW),
    )(*gs, *lands, send, recv, after)
    return outs[:K], outs[K:]


def _small_copies(x, land, send, recv):
    mx, my, mc = _my_place()
    me = 4 * mx + 2 * my + mc
    out = []
    for k in range(1, N_DEV):
        peer = (1 - mx if k & 4 else mx, 1 - my if k & 2 else my, 1 - mc if k & 1 else mc)
        slot = 4 * peer[0] + 2 * peer[1] + peer[2]
        out.append(tuple(pltpu.make_async_remote_copy(
            src_ref=x, dst_ref=land.at[s], send_sem=send.at[k - 1], recv_sem=recv.at[k - 1],
            device_id=peer, device_id_type=MESH) for s in (me, slot)))
    return out


def small_start(x, after, name):
    def body(x_ref, land_ref, after_ref, send, recv, x_thru, land_thru, token):
        for mine, _ in _small_copies(x_ref, land_ref, send, recv):
            mine.start()
        token[...] = jnp.zeros_like(token)

    land_shape = (N_DEV,) + x.shape
    outs = pl.pallas_call(
        body, name=name,
        in_specs=[_HBM, _HBM, pl.BlockSpec(memory_space=pl.ANY)],
        out_specs=[_SEM, _SEM, _HBM, _HBM, pl.BlockSpec(memory_space=pltpu.VMEM)],
        out_shape=[pltpu.SemaphoreType.DMA((N_DEV - 1,))] * 2 + [pltpu.HBM(x.shape, x.dtype), pltpu.HBM(land_shape, x.dtype),
                                                                 jax.ShapeDtypeStruct((8, 128), F32)],
        input_output_aliases={0: 2, 1: 3},
        compiler_params=pltpu.CompilerParams(has_side_effects=_DATAFLOW),
    )(pltpu.with_memory_space_constraint(x, pltpu.HBM),
      pltpu.with_memory_space_constraint(lax.empty(land_shape, x.dtype), pltpu.HBM), after)
    return outs[:4], outs[4]


def small_wait(handle, after, name):
    send, recv, x, land = handle

    def body(x_ref, land_ref, send_ref, recv_ref, after_ref, x_out, land_out):
        for mine, theirs in _small_copies(x_ref, land_ref, send_ref, recv_ref):
            mine.wait_send()
            theirs.wait_recv()

    return pl.pallas_call(
        body, name=name,
        in_specs=[_HBM, _HBM, _SEM, _SEM, pl.BlockSpec(memory_space=pl.ANY)],
        out_specs=[_HBM, _HBM],
        out_shape=[pltpu.HBM(x.shape, x.dtype), pltpu.HBM(land.shape, land.dtype)],
        input_output_aliases={0: 0, 1: 1},
        compiler_params=pltpu.CompilerParams(has_side_effects=_DATAFLOW),
    )(x, land, send, recv, after)


def _scatter_copies(ps, lands, send, recv):
    mx, my, mc = _my_place()
    cps = []
    for j, (cx, cy) in enumerate(_other_chips(mx, my)):
        for k in range(len(ps)):
            cps.append(pltpu.make_async_remote_copy(
                src_ref=ps[k].at[:, 2 * cx + cy], dst_ref=lands[k].at[j],
                send_sem=send.at[k * 3 + j], recv_sem=recv.at[k * 3 + j],
                device_id=(cx, cy, mc), device_id_type=MESH))
    return cps


def scatter_start(ps, after, name):
    K = len(ps)

    def body(*refs):
        ins, lands = refs[:K], refs[K:2 * K]
        send, recv = refs[2 * K + 1], refs[2 * K + 2]
        for cp in _scatter_copies(ins, lands, send, recv):
            cp.start()
        refs[-1][...] = jnp.zeros_like(refs[-1])

    land_shapes = [(N_CHIP - 1, p.shape[0]) + p.shape[2:] for p in ps]
    outs = pl.pallas_call(
        body, name=name,
        in_specs=[_HBM] * (2 * K) + [pl.BlockSpec(memory_space=pl.ANY)],
        out_specs=[_SEM, _SEM] + [_HBM] * (2 * K) + [pl.BlockSpec(memory_space=pltpu.VMEM)],
        out_shape=([pltpu.SemaphoreType.DMA((3 * K,))] * 2 + [pltpu.HBM(p.shape, p.dtype) for p in ps]
                   + [pltpu.HBM(ls, p.dtype) for ls, p in zip(land_shapes, ps)] + [jax.ShapeDtypeStruct((8, 128), F32)]),
        input_output_aliases={i: 2 + i for i in range(2 * K)},
        compiler_params=pltpu.CompilerParams(has_side_effects=_DATAFLOW),
    )(*[pltpu.with_memory_space_constraint(p, pltpu.HBM) for p in ps],
      *[pltpu.with_memory_space_constraint(lax.empty(ls, p.dtype), pltpu.HBM) for ls, p in zip(land_shapes, ps)],
      after)
    return (outs[0], outs[1], outs[2:2 + K], outs[2 + K:2 + 2 * K]), outs[-1]


def scatter_wait(handle, after, name):
    send, recv, ps, lands = handle
    K = len(ps)
    afters = list(after) if isinstance(after, (list, tuple)) else [after]

    def body(*refs):
        ins, land_refs = refs[:K], refs[K:2 * K]
        send_ref, recv_ref = refs[2 * K], refs[2 * K + 1]
        for cp in _scatter_copies(ins, land_refs, send_ref, recv_ref):
            cp.wait_send()
            cp.wait_recv()

    outs = pl.pallas_call(
        body, name=name,
        in_specs=[_HBM] * (2 * K) + [_SEM, _SEM] + [pl.BlockSpec(memory_space=pl.ANY)] * len(afters),
        out_specs=[_HBM] * (2 * K),
        out_shape=[pltpu.HBM(p.shape, p.dtype) for p in ps] + [pltpu.HBM(l.shape, l.dtype) for l in lands],
        input_output_aliases={i: i for i in range(2 * K)},
        compiler_params=pltpu.CompilerParams(has_side_effects=_DATAFLOW),
    )(*ps, *lands, send, recv, *afters)
    return outs[:K], outs[K:]


def sibling_complete(ss, name):
    K = len(ss)

    def body(*refs):
        ins, outs = refs[:K], refs[K:2 * K]
        send, recv = refs[2 * K:]
        mx, my, mc = _my_place()
        cps = []
        for k in range(K):
            cp = pltpu.make_async_remote_copy(
                src_ref=ins[k].at[:, mc], dst_ref=outs[k].at[:, mc], send_sem=send.at[k], recv_sem=recv.at[k],
                device_id=(mx, my, 1 - mc), device_id_type=MESH)
            cp.start()
            cps.append(cp)
        for k in range(K):
            pltpu.make_async_remote_copy(
                src_ref=ins[k].at[:, mc], dst_ref=outs[k].at[:, 1 - mc], send_sem=send.at[k], recv_sem=recv.at[k],
                device_id=(mx, my, 1 - mc), device_id_type=MESH).wait_recv()
        for cp in cps:
            cp.wait_send()

    hbm = pl.BlockSpec(memory_space=pl.ANY)
    return pl.pallas_call(
        body, name=name,
        in_specs=[hbm] * K, out_specs=[hbm] * K,
        out_shape=[jax.ShapeDtypeStruct(s.shape, s.dtype) for s in ss],
        scratch_shapes=[pltpu.SemaphoreType.DMA((K,)), pltpu.SemaphoreType.DMA((K,))],
        input_output_aliases={k: k for k in range(K)},
    )(*ss)


def _rope_tables(T):
    inv = ROPE_THETA ** (-jnp.arange(0, ATT_DH, 2, dtype=F32) / ATT_DH)
    ang = jnp.arange(T, dtype=F32)[:, None] * inv[None, :]
    ang = jnp.concatenate([ang, ang, ang, ang], axis=-1)
    return jnp.cos(ang), jnp.sin(ang)


def _ffn_fwd(h, y, mod, i0, get_up, get_down, norm_next, tag):
    wgu = get_up(y)
    a, b, s = ffn_up(y, (wgu, (0,)), (wgu, (1,)), f"ffn_up_{tag}")
    wd = get_down(s)
    outs = resid_matmul([s], (wd, (0,)), h, mod, i0 + 2, 0.5, f"ffn_down_{tag}", norm_next)
    hn, o = outs[0], outs[1]
    return hn, (outs[2] if norm_next else None), (h, y, a, b, s, o), ((wgu, (0,)), (wgu, (1,)), (wd, (0,)))


def _ffn_bwd(dh, do, res, ng, i_n, mod, i0, wgT, wuT, wd, on_grads, next_gate, after, tag):
    h, y, a, b, s, o = res
    F = _wrows(wgT)
    da, db = ffn_bwd_mid(do, wd, a, b, f"ffn_bwd_mid_{tag}", after)
    gbuf = lax.empty((3, F, h.shape[1]), BF16)
    gbuf = matmul_tn(da, y, gbuf, 0, 0, f"dwg_{tag}")
    gbuf = matmul_tn(db, y, gbuf, 1, 0, f"dwu_{tag}")
    gbuf = matmul_tn(s, do, gbuf, 2, 0, f"dwd_{tag}")
    token, then = on_grads([gbuf])
    outs = dy_normbwd([(da, 0, wgT, 0, F), (db, 0, wuT, 0, F)], h, dh, ng, i_n, mod, i0 + 1,
                      f"ffn_bwd_dy_{tag}", next_gate, [token])
    return outs, then


def _mixer_fwd(h, y, mod, w_inT, w_out, sgu, cos, sin, norm_next, tag):
    lng, lnb, sw, swt, bcol = sgu
    proj = matmul_nt(y, w_inT, f"proj_{tag}")
    out_a = sgu_fwd(proj, lng, lnb, sw, bcol, f"sgu_fwd_{tag}")
    qkv = rope_fwd(proj, cos, sin, f"rope_fwd_{tag}")
    npat = len(DILATIONS)
    qkv_res = [tuple(qkv[3 * p:3 * p + 3]) for p in range(npat)]
    os_, lses = [], []
    for d, (qd, kd, vd) in zip(DILATIONS, qkv_res):
        o_d, lse_d = attn_fwd(qd, kd, vd, f"attn_fwd_d{d}_{tag}")
        os_.append(o_d)
        lses.append(lse_d)
    comb = attn_combine(os_, lses, f"attn_combine_{tag}")
    out_b, o_res, lse_res = comb[0], comb[1:1 + npat], comb[1 + npat:]
    outs = resid_matmul([out_a, out_b], w_out, h, mod, 5, 1.0, f"mix_out_{tag}", norm_next)
    hn, om = outs[0], outs[1]
    return hn, (outs[2] if norm_next else None), (h, y, proj, out_a, out_b, o_res, lse_res, qkv_res, om)


def _mixer_bwd(dh, dom, res, ng, mod, w_inT, w_out, sgu, cos, sin, on_grads, next_gate, after, tag):
    lng, lnb, sw, swt, bcol = sgu
    h, y, proj, out_a, out_b, o_res, lse_res, qkv_res, om = res
    D = h.shape[1]
    dmixed = matmul_nt(dom, w_out, f"dmixed_{tag}", after)
    woutbuf = lax.empty((1, 2 * MIX_HALF, D), BF16)
    woutbuf = matmul_tn(out_a, dom, woutbuf, 0, 0, f"dwout_a_{tag}", tmo_cap=MIX_HALF)
    woutbuf = matmul_tn(out_b, dom, woutbuf, 0, MIX_HALF, f"dwout_b_{tag}", tmo_cap=MIX_HALF)
    d_uv, d_sw, d_svec = sgu_bwd(proj, dmixed, lng, lnb, sw, swt, bcol, f"sgu_bwd_{tag}")
    do_res = to_residues(dmixed, 1, f"dout_res_{tag}")
    dqs, dks, dvs = [], [], []
    for p, (d, (qd, kd, vd)) in enumerate(zip(DILATIONS, qkv_res)):
        dq, dk, dv = attn_bwd(qd, kd, vd, do_res[p], o_res[p], lse_res[p], f"attn_bwd_d{d}_{tag}")
        dqs.append(dq)
        dks.append(dk)
        dvs.append(dv)
    d_qkv = rope_bwd(dqs, dks, dvs, cos, sin, f"rope_bwd_{tag}")
    winbuf = lax.empty((1, 5 * MIX_HALF, D), BF16)
    winbuf = matmul_tn(d_uv, y, winbuf, 0, 0, f"dwin_uv_{tag}", tmo_cap=MIX_HALF)
    winbuf = matmul_tn(d_qkv, y, winbuf, 0, 2 * MIX_HALF, f"dwin_qkv_{tag}", tmo_cap=MIX_HALF)
    token, then = on_grads([winbuf, woutbuf])
    pairs = [(d_uv, 0, w_inT, 0, 2 * MIX_HALF), (d_qkv, 0, w_inT, 1, 2 * MIX_HALF), (d_qkv, 2, w_inT, 4, MIX_HALF)]
    outs = dy_normbwd(pairs, h, dh, ng, 1, mod, 4, f"mix_bwd_dy_{tag}", next_gate, [token])
    return outs, d_sw, d_svec, then


def _local_step(x, tgt, mods, ngs, get_w, sgus, gf, on_block_grads, on_layer_small):
    T, D = x.shape
    cos, sin = _rope_tables(T)
    h = x
    saved, weights = [], []
    for l in range(2):
        def getter(blk, l=l):
            return lambda after: get_w(l, blk, after)

        if l == 0:
            y = normmod_fwd(h, ngs[0], 0, mods[0], 0, 1, "normmod_l0f1")
        h, y, r1, wf1 = _ffn_fwd(h, y, mods[l], 0, getter("f1u"), getter("f1d"), (ngs[l], 1, mods[l], 3, 4), f"l{l}f1")
        w_inT, w_out = get_w(l, "mx", h)
        h, y, r2 = _mixer_fwd(h, y, mods[l], (w_inT, (0,)), (w_out, (0,)), sgus[l], cos, sin,
                              (ngs[l], 2, mods[l], 6, 7), f"l{l}mx")
        h, y, r3, wf2 = _ffn_fwd(h, y, mods[l], 6, getter("f2u"), getter("f2d"),
                                 (ngs[l + 1], 0, mods[l + 1], 0, 1) if l + 1 < 2 else None, f"l{l}f2")
        saved.append((r1, r2, r3))
        weights.append((wf1, w_inT, w_out, wf2))
    def gate_of(l, blk):
        r1, r2, r3 = saved[l]
        o, i_g, coef = {"f2": (r3[5], 8, 0.5), "mx": (r2[-1], 5, 1.0), "f1": (r1[5], 2, 0.5)}[blk]
        return o, mods[l], i_g, coef

    seq = [(l, blk) for l in (1, 0) for blk in ("f2", "mx", "f1")]
    dh, red_final, do, red_g = final_loss_bwd(h, gf, tgt, gate_of(*seq[0]), "final_loss_bwd")
    rn, rg = {}, {}
    after = []
    for idx, (l, blk) in enumerate(seq):
        r1, r2, r3 = saved[l]
        wf1, w_inT, w_out, wf2 = weights[l]
        nxt = gate_of(*seq[idx + 1]) if idx + 1 < len(seq) else None
        rg[blk] = red_g
        tag = f"l{l}{blk}"

        def on(arrays, l=l, blk=blk):
            return on_block_grads(l, blk, arrays)

        if blk == "f2":
            outs, then = _ffn_bwd(dh, do, r3, ngs[l], 2, mods[l], 6, *wf2, on, nxt, after, tag)
        elif blk == "mx":
            outs, d_sw, d_svec, then = _mixer_bwd(dh, do, r2, ngs[l], mods[l], (w_inT, (0,)), (w_out, (0,)), sgus[l],
                                                  cos, sin, on, nxt, after, tag)
        else:
            outs, then = _ffn_bwd(dh, do, r1, ngs[l], 0, mods[l], 0, *wf1, on, nxt, after, tag)
        dh, rn[blk] = outs[0], outs[1]
        if nxt is not None:
            do, red_g = outs[2], outs[3]
        if blk == "f1":
            small = on_layer_small(l, dict(sgu_w=d_sw, sgu_vec=d_svec, red_n=(rn["f1"], rn["mx"], rn["f2"]),
                                           red_g=(rg["f1"], rg["mx"], rg["f2"])), red_final if l == 0 else None)
            after = [small, then(small)]
        else:
            after = [then(dh)]
    return dh


def _adam_out(w, g, m, v, name):
    shp = w.shape
    two_d = (-1, shp[-1])
    d, mn, vn = adamw(w.reshape(two_d), g.reshape(two_d), m.reshape(two_d), v.reshape(two_d), name)
    return g, d.reshape(shp), mn.reshape(shp), vn.reshape(shp)


def kernel(x, c, ada_w, ada_b, norm_g, ffn1_wg, ffn1_wu, ffn1_wd, ffn2_wg, ffn2_wu, ffn2_wd, w_in, sgu_ln_g, sgu_ln_b, sgu_w, sgu_b, w_out, final_g, loss_target, m_ada_w, m_ada_b, m_norm_g, m_ffn1_wg, m_ffn1_wu, m_ffn1_wd, m_ffn2_wg, m_ffn2_wu, m_ffn2_wd, m_w_in, m_sgu_ln_g, m_sgu_ln_b, m_sgu_w, m_sgu_b, m_w_out, m_final_g, v_ada_w, v_ada_b, v_norm_g, v_ffn1_wg, v_ffn1_wu, v_ffn1_wd, v_ffn2_wg, v_ffn2_wu, v_ffn2_wd, v_w_in, v_sgu_ln_g, v_sgu_ln_b, v_sgu_w, v_sgu_b, v_w_out, v_final_g):
    T, D = x.shape[1], x.shape[2]
    NL = ada_w.shape[0]
    mx, my, mc = _my_place()
    me = 4 * mx + 2 * my + mc
    ci = 2 * mx + my
    c_idx = jnp.reshape(mc, (1,)).astype(jnp.int32)
    place = jnp.stack([ci, mc]).astype(jnp.int32)

    ngw = norm_g.shape[2]
    small_in = jnp.concatenate([jnp.pad(c, ((0, 7), (0, 0))),
                                jnp.pad(norm_g.reshape(NL * 3, ngw), ((0, 8 - NL * 3), (0, D - ngw)))], axis=0)
    small_all, _ = gather_small(small_in, place, "gather_c_normg")
    c_all = small_all[:, 0, :]
    ng_parts = small_all[0::2, 8:8 + NL * 3, :ngw]
    ngs = jnp.transpose(ng_parts, (1, 0, 2)).reshape(NL, 3, N_CHIP * ngw)

    nmod = ada_w.shape[2]
    ada_b_mine = lax.dynamic_slice_in_dim(ada_b, ci * nmod, nmod, axis=1).reshape(NL, 1, nmod)
    mod_part = ada_fwd(c_all, ada_w, ada_b_mine, "ada_fwd")
    mod_all, _ = gather_small(mod_part.reshape(NL * N_DEV, nmod), place, "gather_mod")
    mod_rows = lax.dynamic_index_in_dim(mod_all.reshape(N_DEV, NL, N_DEV, nmod), me, axis=2, keepdims=False)
    mods = jnp.transpose(mod_rows[0::2], (1, 0, 2)).reshape(NL, N_ADA, D)

    sgus = []
    for l in range(NL):
        sgus.append((sgu_ln_g[l].reshape(1, MIX_HALF), sgu_ln_b[l].reshape(1, MIX_HALF), sgu_w[l],
                     jnp.swapaxes(sgu_w[l], 1, 2), jnp.transpose(sgu_b[l])))

    def halves(a):
        n, r, _ = a.shape
        return a.reshape(n, 2, r // 2, D)

    first_group = halves(jnp.stack([ffn1_wg[0].T, ffn1_wu[0].T], axis=0).astype(BF16))
    first_handle, first_token = first_start(first_group, mods, "first_start")
    zero = first_token[0, 0]
    mods = mods + zero

    def prep(a):
        return (a + zero).astype(BF16)

    groups = []
    for l in range(NL):
        groups += [[halves(jnp.stack([prep(ffn1_wg[l].T), prep(ffn1_wu[l].T)], axis=0))],
                   [halves(prep(ffn1_wd[l])[None])],
                   [halves(prep(w_in[l].T)[None]), halves(prep(w_out[l])[None])],
                   [halves(jnp.stack([prep(ffn2_wg[l].T), prep(ffn2_wu[l].T)], axis=0))],
                   [halves(prep(ffn2_wd[l])[None])]]
    handles, token = gather_start(groups[1:], mods, "gather_start")
    handles = [None] + handles
    mods = mods + token[0, 0]
    group_no = {"f1u": 0, "f1d": 1, "mx": 2, "f2u": 3, "f2d": 4}

    def get_w(l, key, after):
        g = len(group_no) * l + group_no[key]
        if g == 0:
            full = [first_wait(first_forward(first_handle, after, "first_forward"), place, "first_wait")]
        else:
            full = gather_wait(handles[g], after, f"gather_wait_l{l}{key}")
        full = [a.reshape(a.shape[0], N_CHIP * 2 * a.shape[3], D) for a in full]
        return full[0] if key != "mx" else tuple(full)

    def split(a):
        n, r4, _ = a.shape
        return a.reshape(n, N_CHIP, 2, r4 // N_CHIP // 2, D)

    pending, small_pending, small_tokens = {}, {}, {}

    def on_block_grads(l, blk, bufs):
        tag = f"l{l}{blk}"
        sib, tok1 = sibling_start([split(g) for g in bufs], place, f"rs_sibling_start_{tag}")

        def then(after):
            parts, lands = sibling_wait(sib, after, f"rs_sibling_wait_{tag}")
            psums = [sum_halves(g, ld, c_idx, f"rs_sum_halves_{tag}_{i}") for i, (g, ld) in enumerate(zip(parts, lands))]
            pending[(l, blk)], tok2 = scatter_start(psums, lands[0], f"rs_chips_start_{tag}")
            return tok2

        return tok1, then

    def blocks_finish(blocks, after, tag):
        ssums, counts = [], []
        for l, blk in blocks:
            psums, lands2 = scatter_wait(pending.pop((l, blk)), after, f"rs_chips_wait_l{l}{blk}")
            ssums += [sum_chips(p, ld, place, f"rs_sum_chips_l{l}{blk}_{i}") for i, (p, ld) in enumerate(zip(psums, lands2))]
            counts.append(len(psums))
        fins = [f.reshape(f.shape[0], -1, D) for f in sibling_complete(ssums, f"rs_complete_{tag}")]
        out, i = [], 0
        for n in counts:
            out.append(fins[i:i + n])
            i += n
        return out

    def on_layer_small(l, grads, red_final):
        blocks = list(grads["red_n"]) + list(grads["red_g"])
        blocks.append(jnp.pad(grads["sgu_vec"], ((0, 0), (0, D - MIX_HALF))))
        blocks.append(grads["sgu_w"].reshape(-1, D))
        if red_final is not None:
            blocks.append(red_final)
        xs = jnp.concatenate(blocks, axis=0)
        small_pending[l], small_tokens[l] = small_start(xs, place, f"small_start_l{l}")
        return small_tokens[l]

    grad_x = _local_step(x[0], loss_target[0], mods, ngs, get_w, sgus, final_g.reshape(1, D),
                         on_block_grads, on_layer_small)

    adam_state = {}

    def adam_big(nm, l, g, w, m, v):
        adam_state[nm] = adamw_layer(w, g, m, v, l, adam_state.get(nm), f"adamw_{nm}_l{l}")

    def adam_block(l, blk, fin):
        if blk == "mx":
            adam_big("w_in", l, fin[0][0].T, w_in, m_w_in, v_w_in)
            adam_big("w_out", l, fin[1][0], w_out, m_w_out, v_w_out)
        else:
            ws = ((ffn1_wg, m_ffn1_wg, v_ffn1_wg), (ffn1_wu, m_ffn1_wu, v_ffn1_wu), (ffn1_wd, m_ffn1_wd, v_ffn1_wd)) \
                if blk == "f1" else \
                ((ffn2_wg, m_ffn2_wg, v_ffn2_wg), (ffn2_wu, m_ffn2_wu, v_ffn2_wu), (ffn2_wd, m_ffn2_wd, v_ffn2_wd))
            pre = "ffn1" if blk == "f1" else "ffn2"
            for k, (nm, tr) in enumerate((("wg", True), ("wu", True), ("wd", False))):
                adam_big(f"{pre}_{nm}", l, fin[0][k], *[jnp.swapaxes(t, 1, 2) if tr else t for t in ws[k]])

    done_order = [(l, blk) for l in range(NL - 1, -1, -1) for blk in ("f2", "mx", "f1")]
    for (l, blk), fin in zip(done_order[:-1], blocks_finish(done_order[:-1], small_tokens[0], "early")):
        adam_block(l, blk, fin)
    last_big = adam_state["w_out"][1]

    small_sum, small_all = [], []
    for l in range(NL):
        xs, land = small_wait(small_pending[l], last_big, f"small_wait_l{l}")
        full = lax.dynamic_update_slice(land, xs[None], (me, 0, 0))
        small_all.append(full)
        small_sum.append(sum_slots(full, f"small_sum_l{l}"))
    offs = [8 * i for i in range(8)]
    off_final = offs[7] + SGU_HEADS * ATT_BLOCK * HEAD_LANES // D
    loss = small_sum[0][off_final + 1, 0]
    g_final_g = small_sum[0][off_final, :]
    g_norm_g, g_ada_b, g_lng, g_lnb, g_sb, g_sw, dmod_all = [], [], [], [], [], [], []
    for l in range(NL):
        rn = [small_sum[l][offs[i]:offs[i] + 8] for i in range(3)]
        rg = [small_sum[l][offs[3 + i]:offs[3 + i] + 8] for i in range(3)]
        g_norm_g.append(jnp.stack([rn[i][2] for i in range(3)], axis=0))
        g_ada_b.append(jnp.concatenate([jnp.stack([rn[i][0], rn[i][1], rg[i][0]], axis=0) for i in range(3)],
                                       axis=0).reshape(N_ADA * D))
        sv = small_sum[l][offs[6]:offs[6] + 8, :MIX_HALF]
        g_lng.append(sv[0].reshape(SGU_HEADS, HEAD_LANES))
        g_lnb.append(sv[1].reshape(SGU_HEADS, HEAD_LANES))
        g_sb.append(sv[2].reshape(SGU_HEADS, ATT_BLOCK))
        g_sw.append(small_sum[l][offs[7]:off_final].reshape(sgu_w.shape[1:]))
        rows = []
        for i in range(3):
            an = small_all[l][:, offs[i]:offs[i] + 2]
            ag = small_all[l][:, offs[3 + i]:offs[3 + i] + 1]
            rows += [an[:, 0], an[:, 1], ag[:, 0]]
        dmod_all.append(jnp.stack(rows, axis=1).reshape(N_DEV, N_ADA * D))
    dmod_all = jnp.stack(dmod_all, axis=0)
    dmod_mine = lax.dynamic_slice_in_dim(dmod_all, ci * nmod, nmod, axis=2)
    g_ada_w = ada_bwd(jnp.transpose(c_all), dmod_mine, "ada_bwd")
    g_ada_b = jnp.stack(g_ada_b, axis=0)
    g_norm_g_full = jnp.stack(g_norm_g, axis=0)
    g_norm_g_mine = lax.dynamic_slice_in_dim(g_norm_g_full, ci * ngw, ngw, axis=2)

    small_params = [
        ("ada_w", ada_w, g_ada_w, m_ada_w, v_ada_w),
        ("ada_b", ada_b, g_ada_b, m_ada_b, v_ada_b),
        ("norm_g", norm_g, g_norm_g_mine, m_norm_g, v_norm_g),
        ("sgu_ln_g", sgu_ln_g, jnp.stack(g_lng, axis=0), m_sgu_ln_g, v_sgu_ln_g),
        ("sgu_ln_b", sgu_ln_b, jnp.stack(g_lnb, axis=0), m_sgu_ln_b, v_sgu_ln_b),
        ("sgu_w", sgu_w, jnp.stack(g_sw, axis=0), m_sgu_w, v_sgu_w),
        ("sgu_b", sgu_b, jnp.stack(g_sb, axis=0), m_sgu_b, v_sgu_b),
        ("final_g", final_g.reshape(1, D), g_final_g.reshape(1, D), m_final_g.reshape(1, D), v_final_g.reshape(1, D)),
    ]
    for nm, w, g, m, v in small_params:
        res = _adam_out(w, g, m, v, f"adamw_{nm}")
        adam_state[nm] = tuple(t.reshape(D) for t in res) if nm == "final_g" else res

    l, blk = done_order[-1]
    adam_block(l, blk, blocks_finish([(l, blk)], [st[1] for st in adam_state.values()], "last")[0])

    names = ["ada_w", "ada_b", "norm_g", "ffn1_wg", "ffn1_wu", "ffn1_wd", "ffn2_wg", "ffn2_wu", "ffn2_wd", "w_in",
             "sgu_ln_g", "sgu_ln_b", "sgu_w", "sgu_b", "w_out", "final_g"]
    shapes = [t.shape for t in (ada_w, ada_b, norm_g, ffn1_wg, ffn1_wu, ffn1_wd, ffn2_wg, ffn2_wu, ffn2_wd, w_in,
                                sgu_ln_g, sgu_ln_b, sgu_w, sgu_b, w_out, final_g)]
    def shaped(nm, t, s):
        if nm in ("ffn1_wg", "ffn1_wu", "ffn2_wg", "ffn2_wu"):
            return jnp.swapaxes(t.reshape(s[0], s[2], s[1]), 1, 2)
        return t.reshape(s)

    return (loss, grad_x[None], *[shaped(nm, adam_state[nm][i], s) for i in range(4) for nm, s in zip(names, shapes)])
```

```python
import math

import jax
import jax.numpy as jnp
from jax import lax
from jax.experimental import pallas as pl
from jax.experimental.pallas import tpu as pltpu

F32 = jnp.float32
BF16 = jnp.bfloat16
EPS = 1e-6
SGU_HEADS = 4
HEAD_LANES = 128
ATT_DH = 64
ATT_BLOCK = 128
MIX_HALF = SGU_HEADS * HEAD_LANES
DILATIONS = (1, 4, 16)
ROPE_THETA = 10000.0
N_ADA = 9
ADAM_LR, ADAM_B1, ADAM_B2, ADAM_EPS, ADAM_WD, ADAM_STEP = 0.001, 0.9, 0.999, 1e-08, 0.01, 10
NEG = -1e30
V7X_VMEM_BYTES = 64 * 1024 * 1024
VMEM_LIMIT = V7X_VMEM_BYTES * 7 // 8
MESH = pl.DeviceIdType.MESH
N_DEV = 8
N_CHIP = 4
_ANY = pl.BlockSpec(memory_space=pl.ANY)


def _tile(n, cap, mult):
    if n <= cap:
        return n
    t = (cap // mult) * mult
    while t >= mult:
        if n % t == 0:
            return t
        t -= mult
    raise ValueError((n, cap, mult))


def _params(dims=None):
    return pltpu.CompilerParams(dimension_semantics=dims, vmem_limit_bytes=VMEM_LIMIT)


def _wspec(w, rows, idx, resident=False):
    arr, lead = w
    kw = dict(pipeline_mode=pl.Buffered(1)) if resident else {}
    return pl.BlockSpec((None,) * len(lead) + (rows, arr.shape[-1]), lambda *g: tuple(lead) + (idx(*g), 0), **kw)


def _wrows(w):
    return w[0].shape[-2]


def _nt(a, b):
    return lax.dot_general(a, b, (((1,), (1,)), ((), ())), preferred_element_type=F32)


def _tn(a, b):
    return lax.dot_general(a, b, (((0,), (0,)), ((), ())), preferred_element_type=F32)


def _nn(a, b):
    return jnp.dot(a, b, preferred_element_type=F32)


def _sigmoid(x):
    return 0.5 * jnp.tanh(0.5 * x) + 0.5


_GELU_K = math.sqrt(2.0 / math.pi)
_GELU_C = 0.044715


def _gelu(x):
    t = jnp.tanh(_GELU_K * (x + _GELU_C * x * x * x))
    return 0.5 * x * (1.0 + t)


def _gelu_and_grad(x):
    x2 = x * x
    t = jnp.tanh(_GELU_K * (x + _GELU_C * x * x2))
    g = 0.5 * x * (1.0 + t)
    dg = 0.5 * (1.0 + t) + 0.5 * x * (1.0 - t * t) * (_GELU_K * (1.0 + 3.0 * _GELU_C * x2))
    return g, dg


def normmod_fwd(h, ng, i_n, mod, i_sh, i_sc, name):
    T, D = h.shape
    tm = _tile(T, 512, 8)

    def body(h_ref, ng_ref, mod_ref, y_ref):
        y_ref[...] = _normmod(h_ref[...], ng_ref[i_n:i_n + 1, :], mod_ref[i_sh:i_sh + 1, :],
                              mod_ref[i_sc:i_sc + 1, :]).astype(BF16)

    return pl.pallas_call(
        body, name=name, grid=(T // tm,),
        in_specs=[pl.BlockSpec((tm, D), lambda i: (i, 0)),
                  pl.BlockSpec(ng.shape, lambda i: (0, 0)),
                  pl.BlockSpec(mod.shape, lambda i: (0, 0))],
        out_specs=pl.BlockSpec((tm, D), lambda i: (i, 0)),
        out_shape=jax.ShapeDtypeStruct((T, D), BF16),
        compiler_params=_params(("parallel",)),
    )(h, ng, mod)


def ffn_up(y, wgT, wuT, name):
    T, D = y.shape
    F = _wrows(wgT)
    tm = _tile(T, 512, 16)
    tf = _tile(F, 2816, 256)
    cuts = list(range(0, tf, 768)) + [tf]

    def body(y_ref, wg_ref, wu_ref, p_ref, q_ref, s_ref):
        yv = y_ref[...]
        for c0, c1 in zip(cuts[:-1], cuts[1:]):
            a = _nt(yv, wg_ref[c0:c1, :])
            b = _nt(yv, wu_ref[c0:c1, :])
            sig = _sigmoid(a)
            q = a * sig
            p_ref[:, c0:c1] = (b * (sig + q * (1.0 - sig))).astype(BF16)
            q_ref[:, c0:c1] = q.astype(BF16)
            s_ref[:, c0:c1] = (q * b).astype(BF16)

    act = jax.ShapeDtypeStruct((T, F), BF16)
    return pl.pallas_call(
        body, name=name, grid=(F // tf, T // tm),
        in_specs=[pl.BlockSpec((tm, D), lambda j, i: (i, 0)),
                  _wspec(wgT, tf, lambda j, i: j, resident=True),
                  _wspec(wuT, tf, lambda j, i: j, resident=True)],
        out_specs=[pl.BlockSpec((tm, tf), lambda j, i: (i, j))] * 3,
        out_shape=[act, act, act],
        compiler_params=_params(("parallel", "parallel")),
    )(y, wgT[0], wuT[0])


def _normmod(x, gn, sh, sc):
    r = lax.rsqrt(jnp.mean(x * x, axis=-1, keepdims=True) + EPS)
    return ((x * r) * gn) * (1.0 + sc) + sh


def resid_matmul(xs, w, h, mod, i_g, coef, name, norm_next=None):
    T, D = h.shape
    kb = xs[0].shape[1]
    assert all(x.shape == (T, kb) for x in xs) and _wrows(w) == kb * len(xs)
    tm = _tile(T, 1024, 16)
    nx = len(xs)
    n_in, n_out, n_shape, n_ops = [], [], [], []
    if norm_next:
        ng_n, i_n, mod_n, i_sh, i_sc = norm_next
        n_in = [pl.BlockSpec(ng_n.shape, lambda i: (0, 0)), pl.BlockSpec(mod_n.shape, lambda i: (0, 0))]
        n_out = [pl.BlockSpec((tm, D), lambda i: (i, 0))]
        n_shape = [jax.ShapeDtypeStruct((T, D), BF16)]
        n_ops = [ng_n, mod_n]

    def body(*refs):
        x_refs, w_refs = refs[:nx], refs[nx:2 * nx]
        h_ref, mod_ref = refs[2 * nx:2 * nx + 2]
        hn_ref, o_ref = refs[2 * nx + 2 + len(n_in):2 * nx + 4 + len(n_in)]
        o = _nn(x_refs[0][...], w_refs[0][...])
        for xr, wr in zip(x_refs[1:], w_refs[1:]):
            o = o + _nn(xr[...], wr[...])
        o_ref[...] = o.astype(BF16)
        hn = h_ref[...] + (coef * mod_ref[i_g:i_g + 1, :]) * o
        hn_ref[...] = hn
        if norm_next:
            ng_ref, modn_ref = refs[2 * nx + 2], refs[2 * nx + 3]
            refs[-1][...] = _normmod(hn, ng_ref[i_n:i_n + 1, :], modn_ref[i_sh:i_sh + 1, :],
                                     modn_ref[i_sc:i_sc + 1, :]).astype(BF16)

    return pl.pallas_call(
        body, name=name, grid=(T // tm,),
        in_specs=([pl.BlockSpec((tm, kb), lambda i: (i, 0))] * nx
                  + [_wspec(w, kb, lambda i, p=p: p, resident=True) for p in range(nx)]
                  + [pl.BlockSpec((tm, D), lambda i: (i, 0)),
                     pl.BlockSpec(mod.shape, lambda i: (0, 0))] + n_in),
        out_specs=[pl.BlockSpec((tm, D), lambda i: (i, 0))] * 2 + n_out,
        out_shape=[jax.ShapeDtypeStruct((T, D), F32), jax.ShapeDtypeStruct((T, D), BF16)] + n_shape,
        compiler_params=_params(("parallel",)),
    )(*xs, *([w[0]] * nx), h, mod, *n_ops)


def _gate_specs(gate, tm, D):
    o, mod, _, _ = gate
    T = o.shape[0]
    return ([pl.BlockSpec((tm, D), lambda i: (i, 0)), pl.BlockSpec(mod.shape, lambda i: (0, 0))],
            [pl.BlockSpec((tm, D), lambda i: (i, 0)), pl.BlockSpec((8, D), lambda i: (0, 0))],
            [jax.ShapeDtypeStruct((T, D), BF16), jax.ShapeDtypeStruct((8, D), F32)],
            [o, mod])


def _gate_emit(d, gate, o_ref, mod_ref, do_ref, red_ref):
    _, _, i_g, coef = gate
    do_ref[...] = (d * (coef * mod_ref[i_g:i_g + 1, :])).astype(BF16)

    @pl.when(pl.program_id(0) == 0)
    def _():
        red_ref[...] = jnp.zeros_like(red_ref)

    red_ref[0:1, :] += coef * jnp.sum(d * o_ref[...].astype(F32), axis=0, keepdims=True)


def ffn_bwd_mid(do, wd, p, q, name, after=()):
    T, D = do.shape
    F = _wrows(wd)
    tm = _tile(T, 512, 16)
    tf = _tile(F, 2816, 256)
    cuts = list(range(0, tf, 256)) + [tf]

    def body(do_ref, wd_ref, p_ref, q_ref, *rest):
        da_ref, db_ref = rest[-2:]
        dov = do_ref[...]
        for c0, c1 in zip(cuts[:-1], cuts[1:]):
            ds = _nt(dov, wd_ref[c0:c1, :])
            da_ref[:, c0:c1] = (ds * p_ref[:, c0:c1].astype(F32)).astype(BF16)
            db_ref[:, c0:c1] = (ds * q_ref[:, c0:c1].astype(F32)).astype(BF16)

    act = jax.ShapeDtypeStruct((T, F), BF16)
    return pl.pallas_call(
        body, name=name, grid=(F // tf, T // tm),
        in_specs=[pl.BlockSpec((tm, D), lambda j, i: (i, 0)),
                  _wspec(wd, tf, lambda j, i: j, resident=True),
                  pl.BlockSpec((tm, tf), lambda j, i: (i, j)),
                  pl.BlockSpec((tm, tf), lambda j, i: (i, j))] + [_ANY] * len(after),
        out_specs=[pl.BlockSpec((tm, tf), lambda j, i: (i, j))] * 2,
        out_shape=[act, act],
        compiler_params=_params(("parallel", "parallel")),
    )(do, wd[0], p, q, *after)


def dy_normbwd(pairs, h, dhp, ng, i_n, mod, i_sc, name, gate=None, after=()):
    T, D = h.shape
    tm = _tile(T, 512, 16)
    npair = len(pairs)
    g_in, g_out, g_shape, g_ops = _gate_specs(gate, tm, D) if gate else ([], [], [], [])
    n_in = 2 * npair + 4 + len(g_in) + len(after)

    def body(*refs):
        x_refs, w_refs = refs[:npair], refs[npair:2 * npair]
        h_ref, dhp_ref, ng_ref, mod_ref = refs[2 * npair:2 * npair + 4]
        dh_ref, red_ref = refs[n_in:n_in + 2]
        dy = _nn(x_refs[0][...], w_refs[0][...])
        for xr, wr in zip(x_refs[1:], w_refs[1:]):
            dy = dy + _nn(xr[...], wr[...])
        x = h_ref[...]
        r = lax.rsqrt(jnp.mean(x * x, axis=-1, keepdims=True) + EPS)
        n = x * r
        gn = ng_ref[i_n:i_n + 1, :]
        sc1 = 1.0 + mod_ref[i_sc:i_sc + 1, :]
        w = sc1 * gn
        dyn = dy * n
        col = jnp.sum(dyn, axis=0, keepdims=True)

        @pl.when(pl.program_id(0) == 0)
        def _():
            red_ref[...] = jnp.zeros_like(red_ref)

        red_ref[0:1, :] += jnp.sum(dy, axis=0, keepdims=True)
        red_ref[1:2, :] += gn * col
        red_ref[2:3, :] += sc1 * col
        dh_new = dhp_ref[...] + r * (dy * w - n * jnp.mean(dyn * w, axis=-1, keepdims=True))
        dh_ref[...] = dh_new
        if gate:
            _gate_emit(dh_new, gate, refs[2 * npair + 4], refs[2 * npair + 5], refs[-2], refs[-1])

    in_specs = ([pl.BlockSpec((tm, kb), lambda i, c=c: (i, c)) for (_, c, _, _, kb) in pairs]
                + [_wspec(w, kb, lambda i, r=r: r, resident=True) for (_, _, w, r, kb) in pairs]
                + [pl.BlockSpec((tm, D), lambda i: (i, 0)),
                   pl.BlockSpec((tm, D), lambda i: (i, 0)),
                   pl.BlockSpec(ng.shape, lambda i: (0, 0)),
                   pl.BlockSpec(mod.shape, lambda i: (0, 0))] + g_in + [_ANY] * len(after))
    return pl.pallas_call(
        body, name=name, grid=(T // tm,), in_specs=in_specs,
        out_specs=[pl.BlockSpec((tm, D), lambda i: (i, 0)), pl.BlockSpec((8, D), lambda i: (0, 0))] + g_out,
        out_shape=[jax.ShapeDtypeStruct((T, D), F32), jax.ShapeDtypeStruct((8, D), F32)] + g_shape,
        compiler_params=_params(("arbitrary",)),
    )(*[p[0] for p in pairs], *[p[2][0] for p in pairs], h, dhp, ng, mod, *g_ops, *after)


def matmul_tn(a, b, buf, slot, row0, name, tmo_cap=1408):
    T, N = b.shape
    ma = a.shape[1]
    tmo = _tile(ma, tmo_cap, 128)
    assert row0 % tmo == 0
    nmo = ma // tmo
    tk = _tile(T, 2048, 16)
    nk = T // tk

    def body(a_ref, b_ref, buf_ref, o_ref, acc_ref):
        k = pl.program_id(1)

        @pl.when(k == 0)
        def _():
            acc_ref[...] = jnp.zeros_like(acc_ref)

        acc_ref[...] += _tn(a_ref[...], b_ref[...])

        @pl.when(k == nk - 1)
        def _():
            o_ref[...] = acc_ref[...].astype(BF16)

    return pl.pallas_call(
        body, name=name, grid=(nmo, nk),
        in_specs=[pl.BlockSpec((tk, tmo), lambda j, k: (k, j)),
                  pl.BlockSpec((tk, N), lambda j, k: (k, 0)),
                  pl.BlockSpec(memory_space=pl.ANY)],
        out_specs=pl.BlockSpec((None, tmo, N), lambda j, k: (slot, row0 // tmo + j, 0)),
        out_shape=jax.ShapeDtypeStruct(buf.shape, BF16),
        scratch_shapes=[pltpu.VMEM((tmo, N), F32)],
        input_output_aliases={2: 0},
        compiler_params=_params(("parallel", "arbitrary")),
    )(a, b, buf)


def matmul_nt(x, w, name, after=()):
    T, K = x.shape
    N = _wrows(w)
    tm = _tile(T, 1024, 16)
    tn = _tile(N, 1280, 128)

    def body(x_ref, w_ref, *rest):
        rest[-1][...] = _nt(x_ref[...], w_ref[...]).astype(BF16)

    return pl.pallas_call(
        body, name=name, grid=(N // tn, T // tm),
        in_specs=[pl.BlockSpec((tm, K), lambda j, i: (i, 0)), _wspec(w, tn, lambda j, i: j)] + [_ANY] * len(after),
        out_specs=pl.BlockSpec((tm, tn), lambda j, i: (i, j)),
        out_shape=jax.ShapeDtypeStruct((T, N), BF16),
        compiler_params=_params(("parallel", "parallel")),
    )(x, w[0], *after)


def _sgu_head_fwd(u, v, lng, lnb):
    gu, dgu = _gelu_and_grad(u)
    gv, dgv = _gelu_and_grad(v)
    mu = jnp.mean(gv, axis=-1, keepdims=True)
    xc = gv - mu
    rstd = lax.rsqrt(jnp.mean(xc * xc, axis=-1, keepdims=True) + EPS)
    xhat = xc * rstd
    vn = xhat * lng + lnb
    return gu, dgu, dgv, rstd, xhat, vn


def _tril_mask():
    r = lax.broadcasted_iota(jnp.int32, (ATT_BLOCK, ATT_BLOCK), 0)
    c = lax.broadcasted_iota(jnp.int32, (ATT_BLOCK, ATT_BLOCK), 1)
    return c <= r


def _triu_mask():
    r = lax.broadcasted_iota(jnp.int32, (ATT_BLOCK, ATT_BLOCK), 0)
    c = lax.broadcasted_iota(jnp.int32, (ATT_BLOCK, ATT_BLOCK), 1)
    return r <= c


def sgu_fwd(proj, lng, lnb, w, bcol, name):
    T = proj.shape[0]
    tm = _tile(T, 512, 128)
    nch = tm // ATT_BLOCK

    def body(u_ref, v_ref, lng_ref, lnb_ref, w_ref, b_ref, o_ref):
        tril = _tril_mask()
        for hd in range(SGU_HEADS):
            sl = slice(hd * HEAD_LANES, (hd + 1) * HEAD_LANES)
            u = u_ref[:, sl].astype(F32)
            v = v_ref[:, sl].astype(F32)
            gu, _, _, _, _, vn = _sgu_head_fwd(u, v, lng_ref[:, sl], lnb_ref[:, sl])
            wm = jnp.where(tril, w_ref[hd], 0.0).astype(BF16)
            vnb = vn.astype(BF16)
            bc = b_ref[:, hd:hd + 1]
            for ch in range(nch):
                rs = slice(ch * ATT_BLOCK, (ch + 1) * ATT_BLOCK)
                z = _nn(wm, vnb[rs, :]) + bc
                o_ref[rs, sl] = (gu[rs, :] * z).astype(BF16)

    return pl.pallas_call(
        body, name=name, grid=(T // tm,),
        in_specs=[pl.BlockSpec((tm, MIX_HALF), lambda i: (i, 0)),
                  pl.BlockSpec((tm, MIX_HALF), lambda i: (i, 1)),
                  pl.BlockSpec((1, MIX_HALF), lambda i: (0, 0)),
                  pl.BlockSpec((1, MIX_HALF), lambda i: (0, 0)),
                  pl.BlockSpec(w.shape, lambda i: (0, 0, 0)),
                  pl.BlockSpec(bcol.shape, lambda i: (0, 0))],
        out_specs=pl.BlockSpec((tm, MIX_HALF), lambda i: (i, 0)),
        out_shape=jax.ShapeDtypeStruct((T, MIX_HALF), BF16),
        compiler_params=_params(("parallel",)),
    )(proj, proj, lng, lnb, w, bcol)


def sgu_bwd(proj, dmixed, lng, lnb, w, wt, bcol, name):
    T = proj.shape[0]
    tm = _tile(T, 512, 128)
    nch = tm // ATT_BLOCK
    nsteps = T // tm

    def body(u_ref, v_ref, g_ref, lng_ref, lnb_ref, w_ref, wt_ref, b_ref, duv_ref, dw_ref, dvec_ref, bacc_ref):
        step = pl.program_id(0)

        @pl.when(step == 0)
        def _():
            dw_ref[...] = jnp.zeros_like(dw_ref)
            dvec_ref[...] = jnp.zeros_like(dvec_ref)
            bacc_ref[...] = jnp.zeros_like(bacc_ref)

        tril = _tril_mask()
        triu = _triu_mask()
        for hd in range(SGU_HEADS):
            sl = slice(hd * HEAD_LANES, (hd + 1) * HEAD_LANES)
            u = u_ref[:, sl].astype(F32)
            v = v_ref[:, sl].astype(F32)
            lng_h = lng_ref[:, sl]
            gu, dgu, dgv, rstd, xhat, vn = _sgu_head_fwd(u, v, lng_h, lnb_ref[:, sl])
            wm = jnp.where(tril, w_ref[hd], 0.0).astype(BF16)
            wmt = jnp.where(triu, wt_ref[hd], 0.0).astype(BF16)
            vnb = vn.astype(BF16)
            bc = b_ref[:, hd:hd + 1]
            g = g_ref[:, sl].astype(F32)
            dw_acc = jnp.zeros((ATT_BLOCK, ATT_BLOCK), F32)
            b_acc = jnp.zeros((ATT_BLOCK, HEAD_LANES), F32)
            dvn_parts = []
            for ch in range(nch):
                rs = slice(ch * ATT_BLOCK, (ch + 1) * ATT_BLOCK)
                z = _nn(wm, vnb[rs, :]) + bc
                duv_ref[rs, sl] = (g[rs, :] * z * dgu[rs, :]).astype(BF16)
                dz = g[rs, :] * gu[rs, :]
                dzb = dz.astype(BF16)
                dvn_parts.append(_nn(wmt, dzb))
                dw_acc = dw_acc + _nt(dzb, vnb[rs, :])
                b_acc = b_acc + dz
            dvn = jnp.concatenate(dvn_parts, axis=0)
            dw_ref[hd] += jnp.where(tril, dw_acc, 0.0)
            bacc_ref[hd] += b_acc
            dvec_ref[0:1, sl] += jnp.sum(dvn * xhat, axis=0, keepdims=True)
            dvec_ref[1:2, sl] += jnp.sum(dvn, axis=0, keepdims=True)
            dxh = dvn * lng_h
            dgv_in = rstd * (dxh - jnp.mean(dxh, axis=-1, keepdims=True)
                             - xhat * jnp.mean(dxh * xhat, axis=-1, keepdims=True))
            duv_ref[:, MIX_HALF + hd * HEAD_LANES:MIX_HALF + (hd + 1) * HEAD_LANES] = (dgv_in * dgv).astype(BF16)

        @pl.when(step == nsteps - 1)
        def _():
            for hd in range(SGU_HEADS):
                sl = slice(hd * HEAD_LANES, (hd + 1) * HEAD_LANES)
                dvec_ref[2:3, sl] = jnp.sum(bacc_ref[hd].T, axis=0, keepdims=True)

    return pl.pallas_call(
        body, name=name, grid=(nsteps,),
        in_specs=[pl.BlockSpec((tm, MIX_HALF), lambda i: (i, 0)),
                  pl.BlockSpec((tm, MIX_HALF), lambda i: (i, 1)),
                  pl.BlockSpec((tm, MIX_HALF), lambda i: (i, 0)),
                  pl.BlockSpec((1, MIX_HALF), lambda i: (0, 0)),
                  pl.BlockSpec((1, MIX_HALF), lambda i: (0, 0)),
                  pl.BlockSpec(w.shape, lambda i: (0, 0, 0)),
                  pl.BlockSpec(w.shape, lambda i: (0, 0, 0)),
                  pl.BlockSpec(bcol.shape, lambda i: (0, 0))],
        out_specs=[pl.BlockSpec((tm, 2 * MIX_HALF), lambda i: (i, 0)),
                   pl.BlockSpec(w.shape, lambda i: (0, 0, 0)),
                   pl.BlockSpec((8, MIX_HALF), lambda i: (0, 0))],
        out_shape=[jax.ShapeDtypeStruct((T, 2 * MIX_HALF), BF16),
                   jax.ShapeDtypeStruct(w.shape, F32),
                   jax.ShapeDtypeStruct((8, MIX_HALF), F32)],
        scratch_shapes=[pltpu.VMEM((SGU_HEADS, ATT_BLOCK, HEAD_LANES), F32)],
        compiler_params=_params(("arbitrary",)),
    )(proj, proj, dmixed, lng, lnb, w, wt, bcol)


def _rot_half(t):
    lane = lax.broadcasted_iota(jnp.int32, t.shape, 1)
    first = (lane % ATT_DH) < (ATT_DH // 2)
    return jnp.where(first, -pltpu.roll(t, HEAD_LANES - ATT_DH // 2, 1), pltpu.roll(t, ATT_DH // 2, 1))


LAYOUT_ROWS = 512


def _res_spec(d, tm, W):
    return pl.BlockSpec((d, tm // d, W), lambda i: (0, i, 0))


def _res_shape(d, T, W, dtype):
    return jax.ShapeDtypeStruct((d, T // d, W), dtype)


def _slab_buf(tm, W):
    return pltpu.VMEM((W // HEAD_LANES, tm, HEAD_LANES), F32)


def _lanes(hp):
    return slice(hp * HEAD_LANES, (hp + 1) * HEAD_LANES)


def _to_res(buf, out_ref, d, dtype):
    nslab, tm, _ = buf.shape
    for hp in range(nslab):
        if d == 1:
            out_ref[0, :, _lanes(hp)] = buf[hp].astype(dtype)
        else:
            for r in range(d):
                out_ref[r, :, _lanes(hp)] = buf.at[hp][pl.ds(r, tm // d, stride=d), :].astype(dtype)


def _from_res(in_ref, buf, d):
    nslab, tm, _ = buf.shape
    for hp in range(nslab):
        if d == 1:
            buf[hp] = in_ref[0, :, _lanes(hp)].astype(F32)
        else:
            for r in range(d):
                buf.at[hp][pl.ds(r, tm // d, stride=d), :] = in_ref[r, :, _lanes(hp)].astype(F32)


def rope_fwd(proj, cos, sin, name):
    T = proj.shape[0]
    tm = LAYOUT_ROWS
    scale = 1.0 / math.sqrt(ATT_DH)
    nd = len(DILATIONS)

    def body(q_ref, k_ref, v_ref, cos_ref, sin_ref, *rest):
        outs, buf = rest[:3 * nd], rest[3 * nd]
        c = cos_ref[...]
        s = sin_ref[...]
        for which, src in enumerate((q_ref, k_ref, v_ref)):
            for hp in range(MIX_HALF // HEAD_LANES):
                t = src[:, _lanes(hp)].astype(F32)
                if which == 0:
                    t = scale * (t * c + _rot_half(t) * s)
                elif which == 1:
                    t = t * c + _rot_half(t) * s
                buf[hp] = t
            for di, d in enumerate(DILATIONS):
                _to_res(buf, outs[3 * di + which], d, BF16)

    return pl.pallas_call(
        body, name=name, grid=(T // tm,),
        in_specs=[pl.BlockSpec((tm, MIX_HALF), lambda i: (i, 2)),
                  pl.BlockSpec((tm, MIX_HALF), lambda i: (i, 3)),
                  pl.BlockSpec((tm, MIX_HALF), lambda i: (i, 4)),
                  pl.BlockSpec((tm, HEAD_LANES), lambda i: (i, 0)),
                  pl.BlockSpec((tm, HEAD_LANES), lambda i: (i, 0))],
        out_specs=[_res_spec(d, tm, MIX_HALF) for d in DILATIONS for _ in range(3)],
        out_shape=[_res_shape(d, T, MIX_HALF, BF16) for d in DILATIONS for _ in range(3)],
        scratch_shapes=[_slab_buf(tm, MIX_HALF)],
        compiler_params=_params(("parallel",)),
    )(proj, proj, proj, cos, sin)


def to_residues(x, col, name):
    T = x.shape[0]
    tm = LAYOUT_ROWS

    def body(x_ref, *rest):
        outs, buf = rest[:-1], rest[-1]
        for hp in range(MIX_HALF // HEAD_LANES):
            buf[hp] = x_ref[:, _lanes(hp)].astype(F32)
        for o_ref, d in zip(outs, DILATIONS):
            _to_res(buf, o_ref, d, BF16)

    return pl.pallas_call(
        body, name=name, grid=(T // tm,),
        in_specs=[pl.BlockSpec((tm, MIX_HALF), lambda i: (i, col))],
        out_specs=[_res_spec(d, tm, MIX_HALF) for d in DILATIONS],
        out_shape=[_res_shape(d, T, MIX_HALF, BF16) for d in DILATIONS],
        scratch_shapes=[_slab_buf(tm, MIX_HALF)],
        compiler_params=_params(("parallel",)),
    )(x)


def rope_bwd(dqs, dks, dvs, cos, sin, name):
    T = dqs[0].shape[0] * dqs[0].shape[1]
    tm = LAYOUT_ROWS
    scale = 1.0 / math.sqrt(ATT_DH)
    npat = len(dqs)

    def body(*refs):
        groups = refs[:npat], refs[npat:2 * npat], refs[2 * npat:3 * npat]
        cos_ref, sin_ref, o_ref, buf, acc = refs[3 * npat:]
        c = cos_ref[...]
        s = sin_ref[...]
        for which, g_refs in enumerate(groups):
            _from_res(g_refs[0], acc, DILATIONS[0])
            for g_ref, d in zip(g_refs[1:], DILATIONS[1:]):
                _from_res(g_ref, buf, d)
                acc[...] += buf[...]
            for hp in range(MIX_HALF // HEAD_LANES):
                g = acc[hp]
                if which == 0:
                    g = scale * g
                if which < 2:
                    g = g * c - _rot_half(g * s)
                o_ref[:, which * MIX_HALF + hp * HEAD_LANES:which * MIX_HALF + (hp + 1) * HEAD_LANES] = g.astype(BF16)

    return pl.pallas_call(
        body, name=name, grid=(T // tm,),
        in_specs=([_res_spec(d, tm, MIX_HALF) for _ in range(3) for d in DILATIONS]
                  + [pl.BlockSpec((tm, HEAD_LANES), lambda i: (i, 0))] * 2),
        out_specs=pl.BlockSpec((tm, 3 * MIX_HALF), lambda i: (i, 0)),
        out_shape=jax.ShapeDtypeStruct((T, 3 * MIX_HALF), BF16),
        scratch_shapes=[_slab_buf(tm, MIX_HALF), _slab_buf(tm, MIX_HALF)],
        compiler_params=_params(("parallel",)),
    )(*dqs, *dks, *dvs, cos, sin)


def _band_masks(n):
    r = lax.broadcasted_iota(jnp.int32, (2 * ATT_BLOCK, ATT_BLOCK), 0)
    c = lax.broadcasted_iota(jnp.int32, (2 * ATT_BLOCK, ATT_BLOCK), 1)
    qi = r % ATT_BLOCK
    head = (c < ATT_DH) == (r < ATT_BLOCK)
    return (c >= qi) & (n > 0), c <= qi, head, c[:ATT_BLOCK] < ATT_DH


def _stack_heads(x, head):
    x2 = jnp.concatenate([x, x], axis=0)
    return jnp.where(head, x2, jnp.zeros_like(x2))


def _blocks_per_step(nb):
    return next(n for n in (4, 2, 1) if nb % n == 0)


def attn_fwd(q, k, v, name):
    d, L, W = q.shape
    per_seq = L // ATT_BLOCK
    nb = d * per_seq
    nsub = _blocks_per_step(nb)
    q, k, v = (t.reshape(1, d * L, W) for t in (q, k, v))

    def body(q_ref, kp_ref, kc_ref, vp_ref, vc_ref, o_ref, lse_ref):
        step = pl.program_id(1)
        for u in range(nsub):
            rows = slice(u * ATT_BLOCK, (u + 1) * ATT_BLOCK)
            before = slice((u - 1) * ATT_BLOCK, u * ATT_BLOCK)
            mask_p, mask_c, head, head0 = _band_masks((nsub * step + u) % per_seq)
            for hp in range(W // HEAD_LANES):
                sl = slice(hp * HEAD_LANES, (hp + 1) * HEAD_LANES)
                kp, vp = (kp_ref[0, :, sl], vp_ref[0, :, sl]) if u == 0 else (kc_ref[0, before, sl], vc_ref[0, before, sl])
                kc, vc = kc_ref[0, rows, sl], vc_ref[0, rows, sl]
                qs = _stack_heads(q_ref[0, rows, sl], head)
                sp = jnp.where(mask_p, _nt(qs, kp), NEG)
                sc = jnp.where(mask_c, _nt(qs, kc), NEG)
                m = jnp.maximum(jnp.max(sp, axis=1, keepdims=True), jnp.max(sc, axis=1, keepdims=True))
                pp = jnp.exp(sp - m)
                pc = jnp.exp(sc - m)
                den = jnp.sum(pp, axis=1, keepdims=True) + jnp.sum(pc, axis=1, keepdims=True)
                o = (_nn(pp.astype(BF16), vp) + _nn(pc.astype(BF16), vc)) / den
                lse = m + jnp.log(den)
                o_ref[0, rows, sl] = jnp.where(head0, o[:ATT_BLOCK], o[ATT_BLOCK:]).astype(BF16)
                lse_ref[0, rows, sl] = jnp.where(head0, lse[:ATT_BLOCK], lse[ATT_BLOCK:])

    cur = pl.BlockSpec((1, nsub * ATT_BLOCK, W), lambda r, n: (r, n, 0))
    prev = pl.BlockSpec((1, ATT_BLOCK, W), lambda r, n: (r, jnp.maximum(nsub * n - 1, 0), 0))
    o, lse = pl.pallas_call(
        body, name=name, grid=(1, nb // nsub),
        in_specs=[cur, prev, cur, prev, cur],
        out_specs=[cur, cur],
        out_shape=[jax.ShapeDtypeStruct((1, d * L, W), BF16), jax.ShapeDtypeStruct((1, d * L, W), F32)],
        compiler_params=_params(("parallel", "parallel")),
    )(q, k, k, v, v)
    return o.reshape(d, L, W), lse.reshape(d, L, W)


def attn_combine(os_, lses, name):
    T = os_[0].shape[0] * os_[0].shape[1]
    W = os_[0].shape[2]
    tm = LAYOUT_ROWS
    npat = len(os_)

    def body(*refs):
        o_refs, l_refs = refs[:npat], refs[npat:2 * npat]
        out_ref = refs[2 * npat]
        ores, lres = refs[2 * npat + 1:3 * npat + 1], refs[3 * npat + 1:4 * npat + 1]
        bufs = refs[4 * npat + 1:]
        lbufs, obufs, out_buf, lse_buf = bufs[:npat], bufs[npat:2 * npat], bufs[2 * npat], bufs[2 * npat + 1]
        for p, d in enumerate(DILATIONS):
            _from_res(l_refs[p], lbufs[p], d)
            _from_res(o_refs[p], obufs[p], d)
        for hp in range(W // HEAD_LANES):
            ls = [b[hp] for b in lbufs]
            m = ls[0]
            for l in ls[1:]:
                m = jnp.maximum(m, l)
            es = [jnp.exp(l - m) for l in ls]
            z = es[0]
            for e in es[1:]:
                z = z + e
            acc = es[0] * obufs[0][hp]
            for p in range(1, npat):
                acc = acc + es[p] * obufs[p][hp]
            out = acc / z
            out_ref[:, _lanes(hp)] = out.astype(BF16)
            out_buf[hp] = out
            lse_buf[hp] = m + jnp.log(z)
        for p, d in enumerate(DILATIONS):
            _to_res(out_buf, ores[p], d, BF16)
            _to_res(lse_buf, lres[p], d, F32)

    return pl.pallas_call(
        body, name=name, grid=(T // tm,),
        in_specs=[_res_spec(d, tm, W) for _ in range(2) for d in DILATIONS],
        out_specs=([pl.BlockSpec((tm, W), lambda i: (i, 0))] + [_res_spec(d, tm, W) for _ in range(2) for d in DILATIONS]),
        out_shape=([jax.ShapeDtypeStruct((T, W), BF16)] + [_res_shape(d, T, W, BF16) for d in DILATIONS]
                   + [_res_shape(d, T, W, F32) for d in DILATIONS]),
        scratch_shapes=[_slab_buf(tm, W)] * (2 * npat + 2),
        compiler_params=_params(("parallel",)),
    )(*os_, *lses)


def attn_bwd(q, k, v, do, o, lse, name):
    d, L, W = q.shape
    per_seq = L // ATT_BLOCK
    nb = d * per_seq
    nsub = _blocks_per_step(nb)
    nst = nb // nsub
    last = slice((nsub - 1) * ATT_BLOCK, nsub * ATT_BLOCK)
    q, k, v, do, o, lse = (t.reshape(1, d * L, W) for t in (q, k, v, do, o, lse))

    def body(q_ref, kp_ref, kc_ref, vp_ref, vc_ref, do_ref, o_ref, lse_ref, dq_ref, dk_ref, dv_ref, kkeep, vkeep):
        step = pl.program_id(1)

        @pl.when(step == 0)
        def _():
            kkeep[...] = jnp.zeros_like(kkeep)
            vkeep[...] = jnp.zeros_like(vkeep)

        @pl.when(step < nst)
        def _():
            for hp in range(W // HEAD_LANES):
                sl = slice(hp * HEAD_LANES, (hp + 1) * HEAD_LANES)
                shares = []
                for u in range(nsub):
                    rows = slice(u * ATT_BLOCK, (u + 1) * ATT_BLOCK)
                    before = slice((u - 1) * ATT_BLOCK, u * ATT_BLOCK)
                    mask_p, mask_c, head, head0 = _band_masks((nsub * step + u) % per_seq)
                    kp, vp = (kp_ref[0, :, sl], vp_ref[0, :, sl]) if u == 0 else (kc_ref[0, before, sl], vc_ref[0, before, sl])
                    kc, vc = kc_ref[0, rows, sl], vc_ref[0, rows, sl]
                    dout = do_ref[0, rows, sl]
                    qs = _stack_heads(q_ref[0, rows, sl], head)
                    dos = _stack_heads(dout, head)
                    lse_v = lse_ref[0, rows, sl]
                    lse_c = jnp.max(jnp.where(head, jnp.concatenate([lse_v, lse_v], axis=0), NEG), axis=1, keepdims=True)
                    delta = jnp.sum(_stack_heads(dout.astype(F32) * o_ref[0, rows, sl].astype(F32), head), axis=1,
                                    keepdims=True)
                    pp = jnp.exp(jnp.where(mask_p, _nt(qs, kp), NEG) - lse_c)
                    pc = jnp.exp(jnp.where(mask_c, _nt(qs, kc), NEG) - lse_c)
                    dsp = (pp * (_nt(dos, vp) - delta)).astype(BF16)
                    dsc = (pc * (_nt(dos, vc) - delta)).astype(BF16)
                    dq2 = _nn(dsp, kp) + _nn(dsc, kc)
                    dq_ref[0, rows, sl] = jnp.where(head0, dq2[:ATT_BLOCK], dq2[ATT_BLOCK:]).astype(BF16)
                    shares.append((_tn(dsp, qs), _tn(pp.astype(BF16), dos), _tn(dsc, qs), _tn(pc.astype(BF16), dos)))
                dk_ref[0, last, sl] = (kkeep[last, sl] + shares[0][0]).astype(BF16)
                dv_ref[0, last, sl] = (vkeep[last, sl] + shares[0][1]).astype(BF16)
                for j in range(nsub - 1):
                    blk = slice(j * ATT_BLOCK, (j + 1) * ATT_BLOCK)
                    dk_ref[0, blk, sl] = kkeep[blk, sl].astype(BF16)
                    dv_ref[0, blk, sl] = vkeep[blk, sl].astype(BF16)
                    kkeep[blk, sl] = shares[j][2] + shares[j + 1][0]
                    vkeep[blk, sl] = shares[j][3] + shares[j + 1][1]
                kkeep[last, sl] = shares[-1][2]
                vkeep[last, sl] = shares[-1][3]

        @pl.when(step == nst)
        def _():
            dk_ref[0] = kkeep[...].astype(BF16)
            dv_ref[0] = vkeep[...].astype(BF16)

    rows_per_step = nsub * ATT_BLOCK
    cur = pl.BlockSpec((1, rows_per_step, W), lambda r, n: (r, jnp.minimum(n, nst - 1), 0))
    lag = pl.BlockSpec((1, rows_per_step, W), lambda r, n: (r, jnp.clip(n - 1, 0, nst - 1), 0))
    prev = pl.BlockSpec((1, ATT_BLOCK, W), lambda r, n: (r, jnp.clip(nsub * n - 1, 0, nb - 1), 0))
    out = jax.ShapeDtypeStruct((1, d * L, W), BF16)
    outs = pl.pallas_call(
        body, name=name, grid=(1, nst + 1),
        in_specs=[cur, prev, cur, prev, cur, cur, cur, cur],
        out_specs=[cur, lag, lag], out_shape=[out, out, out],
        scratch_shapes=[pltpu.VMEM((rows_per_step, W), F32), pltpu.VMEM((rows_per_step, W), F32)],
        compiler_params=_params(("parallel", "arbitrary")),
    )(q, k, k, v, v, do, o, lse)
    return [t.reshape(d, L, W) for t in outs]


def final_loss_bwd(h, gf, tgt, gate, name):
    T, D = h.shape
    tm = _tile(T, 512, 16)
    g_in, g_out, g_shape, g_ops = _gate_specs(gate, tm, D)

    def body(h_ref, g_ref, t_ref, o_ref, modg_ref, dh_ref, red_ref, do_ref, redg_ref):
        x = h_ref[...]
        r = lax.rsqrt(jnp.mean(x * x, axis=-1, keepdims=True) + EPS)
        n = x * r
        g = g_ref[...]
        err = n * g - t_ref[...]
        dy = err * (1.0 / D)

        @pl.when(pl.program_id(0) == 0)
        def _():
            red_ref[...] = jnp.zeros_like(red_ref)

        red_ref[0:1, :] += jnp.sum(dy * n, axis=0, keepdims=True)
        red_ref[1:2, :] += jnp.zeros((1, D), F32) + (0.5 / D) * jnp.sum(err * err, keepdims=True)
        dn = dy * g
        dh = r * (dn - n * jnp.mean(dn * n, axis=-1, keepdims=True))
        dh_ref[...] = dh
        _gate_emit(dh, gate, o_ref, modg_ref, do_ref, redg_ref)

    return pl.pallas_call(
        body, name=name, grid=(T // tm,),
        in_specs=[pl.BlockSpec((tm, D), lambda i: (i, 0)),
                  pl.BlockSpec((1, D), lambda i: (0, 0)),
                  pl.BlockSpec((tm, D), lambda i: (i, 0))] + g_in,
        out_specs=[pl.BlockSpec((tm, D), lambda i: (i, 0)), pl.BlockSpec((8, D), lambda i: (0, 0))] + g_out,
        out_shape=[jax.ShapeDtypeStruct((T, D), F32), jax.ShapeDtypeStruct((8, D), F32)] + g_shape,
        compiler_params=_params(("arbitrary",)),
    )(h, gf, tgt, *g_ops)


def ada_fwd(c_all, ada_w, ada_b, name):
    nl, D, N = ada_w.shape

    def body(c_ref, w_ref, b_ref, o_ref):
        c = c_ref[...]
        o_ref[0] = _nn(c * _sigmoid(c), w_ref[0]) + b_ref[0]

    return pl.pallas_call(
        body, name=name, grid=(nl,),
        in_specs=[pl.BlockSpec((N_DEV, D), lambda l: (0, 0)),
                  pl.BlockSpec((1, D, N), lambda l: (l, 0, 0)),
                  pl.BlockSpec((1, 1, N), lambda l: (l, 0, 0))],
        out_specs=pl.BlockSpec((1, N_DEV, N), lambda l: (l, 0, 0)),
        out_shape=jax.ShapeDtypeStruct((nl, N_DEV, N), F32),
        compiler_params=_params(("parallel",)),
    )(c_all, ada_w, ada_b)


def ada_bwd(c_allT, dmod, name):
    nl, _, N = dmod.shape
    D = c_allT.shape[0]

    def body(c_ref, g_ref, o_ref):
        c = c_ref[...]
        ca = c * _sigmoid(c)
        acc = ca[:, 0:1] * g_ref[0, 0:1, :]
        for b in range(1, N_DEV):
            acc = acc + ca[:, b:b + 1] * g_ref[0, b:b + 1, :]
        o_ref[0] = acc

    return pl.pallas_call(
        body, name=name, grid=(nl,),
        in_specs=[pl.BlockSpec((D, N_DEV), lambda l: (0, 0)),
                  pl.BlockSpec((1, N_DEV, N), lambda l: (l, 0, 0))],
        out_specs=pl.BlockSpec((1, D, N), lambda l: (l, 0, 0)),
        out_shape=jax.ShapeDtypeStruct((nl, D, N), F32),
        compiler_params=_params(("parallel",)),
    )(c_allT, dmod)


def adamw(w, g, m, v, name):
    R, C = w.shape
    tr = _tile(R, max(8, (1 << 19) // C // 8 * 8), 8)
    c1 = 1.0 - ADAM_B1 ** ADAM_STEP
    c2 = 1.0 - ADAM_B2 ** ADAM_STEP

    def body(w_ref, g_ref, m_ref, v_ref, d_ref, mo_ref, vo_ref):
        gv = g_ref[...]
        mn = ADAM_B1 * m_ref[...] + (1.0 - ADAM_B1) * gv
        vn = ADAM_B2 * v_ref[...] + (1.0 - ADAM_B2) * (gv * gv)
        mo_ref[...] = mn
        vo_ref[...] = vn
        d_ref[...] = -ADAM_LR * ((mn / c1) / (jnp.sqrt(vn / c2) + ADAM_EPS) + ADAM_WD * w_ref[...])

    blk = pl.BlockSpec((tr, C), lambda i: (i, 0))
    out = jax.ShapeDtypeStruct((R, C), F32)
    return pl.pallas_call(
        body, name=name, grid=(R // tr,),
        in_specs=[blk] * 4, out_specs=[blk] * 3, out_shape=[out] * 3,
        compiler_params=_params(("parallel",)),
    )(w, g, m, v)


def adamw_layer(w, g, m, v, l, prev, name):
    NLw, R, C = w.shape
    tr = _tile(R, max(8, (1 << 19) // C // 8 * 8), 8)
    nrb = R // tr
    c1 = 1.0 - ADAM_B1 ** ADAM_STEP
    c2 = 1.0 - ADAM_B2 ** ADAM_STEP
    w, m, v = (t.reshape(NLw * R, C) for t in (w, m, v))

    def body(w_ref, g_ref, m_ref, v_ref, *rest):
        go_ref, d_ref, mo_ref, vo_ref = rest[-4:]
        gv = g_ref[...]
        mn = ADAM_B1 * m_ref[...] + (1.0 - ADAM_B1) * gv
        vn = ADAM_B2 * v_ref[...] + (1.0 - ADAM_B2) * (gv * gv)
        go_ref[...] = gv
        mo_ref[...] = mn
        vo_ref[...] = vn
        d_ref[...] = -ADAM_LR * ((mn / c1) / (jnp.sqrt(vn / c2) + ADAM_EPS) + ADAM_WD * w_ref[...])

    lay = pl.BlockSpec((tr, C), lambda i: (l * nrb + i, 0))
    out = jax.ShapeDtypeStruct((NLw * R, C), F32)
    n_prev = 0 if prev is None else 4
    return pl.pallas_call(
        body, name=name, grid=(nrb,),
        in_specs=[lay, pl.BlockSpec((tr, C), lambda i: (i, 0)), lay, lay] + [pl.BlockSpec(memory_space=pl.ANY)] * n_prev,
        out_specs=[lay] * 4, out_shape=[out] * 4,
        input_output_aliases={4 + i: i for i in range(n_prev)},
        compiler_params=_params(("parallel",)),
    )(w, g, m, v, *(prev or ()))


def sum_slots(x, name):
    S, R, C = x.shape
    tr = _tile(R, 128, 8)

    def body(x_ref, o_ref):
        acc = x_ref[0]
        for s in range(1, S):
            acc = acc + x_ref[s]
        o_ref[...] = acc

    return pl.pallas_call(
        body, name=name, grid=(R // tr,),
        in_specs=[pl.BlockSpec((S, tr, C), lambda i: (0, i, 0))],
        out_specs=pl.BlockSpec((tr, C), lambda i: (i, 0)),
        out_shape=jax.ShapeDtypeStruct((R, C), F32),
        compiler_params=_params(("parallel",)),
    )(x)


def sum_halves(g, lands, c_idx, name):
    n, ns, _, rh, D = g.shape

    def body(c_ref, g_ref, l_ref, o_ref):
        for j in range(ns):
            o_ref[0, j] = (g_ref[0, j, 0].astype(F32) + l_ref[0, j].astype(F32)).astype(BF16)

    return pl.pallas_call(
        body, name=name,
        grid_spec=pltpu.PrefetchScalarGridSpec(
            num_scalar_prefetch=1, grid=(n,),
            in_specs=[pl.BlockSpec((1, ns, 1, rh, D), lambda i, c: (i, 0, c[0], 0, 0)),
                      pl.BlockSpec((1, ns, rh, D), lambda i, c: (i, 0, 0, 0))],
            out_specs=pl.BlockSpec((1, ns, rh, D), lambda i, c: (i, 0, 0, 0))),
        out_shape=jax.ShapeDtypeStruct((n, ns, rh, D), BF16),
        compiler_params=_params(("parallel",)),
    )(c_idx, g, lands)


def sum_chips(p, lands, place, name):
    n, ns, rh, D = p.shape

    def body(c_ref, p_ref, l_ref, o_ref):
        acc = p_ref[0, 0].astype(F32)
        for j in range(N_CHIP - 1):
            acc = acc + l_ref[j, 0].astype(F32)
        o_ref[0, 0] = acc

    return pl.pallas_call(
        body, name=name,
        grid_spec=pltpu.PrefetchScalarGridSpec(
            num_scalar_prefetch=1, grid=(n,),
            in_specs=[pl.BlockSpec((1, 1, rh, D), lambda i, c: (i, c[0], 0, 0)),
                      pl.BlockSpec((N_CHIP - 1, 1, rh, D), lambda i, c: (0, i, 0, 0))],
            out_specs=pl.BlockSpec((1, 1, rh, D), lambda i, c: (i, c[1], 0, 0))),
        out_shape=jax.ShapeDtypeStruct((n, 2, rh, D), F32),
        compiler_params=_params(("parallel",)),
    )(place, p, lands)


def _my_place():
    return lax.axis_index("x"), lax.axis_index("y"), lax.axis_index("c")


def _other_chips(mx, my):
    return [(1 - mx, my), (mx, 1 - my), (1 - mx, 1 - my)]


def gather_small(x, after, name):
    def body(x_ref, after_ref, out_ref, sum_ref, send_sems, recv_sems):
        mx, my, mc = _my_place()
        me = 4 * mx + 2 * my + mc
        out_ref[me] = x_ref[...]
        sends = []
        for k in range(1, N_DEV):
            kx, ky, kc = (k >> 2) & 1, (k >> 1) & 1, k & 1
            peer = (1 - mx if kx else mx, 1 - my if ky else my, 1 - mc if kc else mc)
            cp = pltpu.make_async_remote_copy(
                src_ref=x_ref, dst_ref=out_ref.at[me], send_sem=send_sems.at[k - 1], recv_sem=recv_sems.at[k - 1],
                device_id=peer, device_id_type=MESH)
            cp.start()
            sends.append((cp, 4 * peer[0] + 2 * peer[1] + peer[2], peer))
        for k, (cp, peer_slot, peer) in enumerate(sends):
            pltpu.make_async_remote_copy(
                src_ref=x_ref, dst_ref=out_ref.at[peer_slot], send_sem=send_sems.at[k], recv_sem=recv_sems.at[k],
                device_id=peer, device_id_type=MESH).wait_recv()
        for cp, _, _ in sends:
            cp.wait_send()
        acc = out_ref[0]
        for s in range(1, N_DEV):
            acc = acc + out_ref[s]
        sum_ref[...] = acc

    vmem = pl.BlockSpec(memory_space=pltpu.VMEM)
    return pl.pallas_call(
        body, name=name,
        in_specs=[vmem, pl.BlockSpec(memory_space=pl.ANY)], out_specs=[vmem, vmem],
        out_shape=[jax.ShapeDtypeStruct((N_DEV,) + x.shape, x.dtype), jax.ShapeDtypeStruct(x.shape, x.dtype)],
        scratch_shapes=[pltpu.SemaphoreType.DMA((N_DEV - 1,)), pltpu.SemaphoreType.DMA((N_DEV - 1,))],
        compiler_params=pltpu.CompilerParams(vmem_limit_bytes=VMEM_LIMIT),
    )(x, after)


_HBM =pl.BlockSpec(memory_space=pltpu.HBM)
_SEM = pl.BlockSpec(memory_space=pltpu.SEMAPHORE)
_DATAFLOW = pltpu.SideEffectType.DATAFLOW_SIDE_EFFECTING


def _gather_copies(shard, land, send, recv, base):
    mx, my, mc = _my_place()
    ci = 2 * mx + my
    peers = [((cx, cy, mc), 2 * cx + cy) for cx, cy in _other_chips(mx, my)] + [((mx, my, 1 - mc), ci)]
    out = []
    for q, (dev, src_slot) in enumerate(peers):
        out.append((
            pltpu.make_async_remote_copy(src_ref=shard, dst_ref=land.at[:, ci], send_sem=send.at[base + q],
                                         recv_sem=recv.at[base + q], device_id=dev, device_id_type=MESH),
            pltpu.make_async_remote_copy(src_ref=shard, dst_ref=land.at[:, src_slot], send_sem=send.at[base + q],
                                         recv_sem=recv.at[base + q], device_id=dev, device_id_type=MESH)))
    return out


def gather_start(groups, after, name):
    items = [s for g in groups for s in g]
    ni, ng = len(items), len(groups)

    def body(*refs):
        shards, lands = refs[:ni], refs[ni:2 * ni]
        sems = refs[2 * ni + 1:2 * ni + 1 + 2 * ng]
        token = refs[-1]
        i = 0
        for g, grp in enumerate(groups):
            for p in range(len(grp)):
                for start_cp, _ in _gather_copies(shards[i], lands[i], sems[2 * g], sems[2 * g + 1], 4 * p):
                    start_cp.start()
                i += 1
        token[...] = jnp.zeros_like(token)

    sem_shapes = []
    for grp in groups:
        sem_shapes += [pltpu.SemaphoreType.DMA((4 * len(grp),))] * 2
    land_shapes = [(s.shape[0], N_CHIP) + s.shape[1:] for s in items]
    outs = pl.pallas_call(
        body, name=name,
        in_specs=[_HBM] * (2 * ni) + [pl.BlockSpec(memory_space=pl.ANY)],
        out_specs=[_SEM] * (2 * ng) + [_HBM] * (2 * ni) + [pl.BlockSpec(memory_space=pltpu.VMEM)],
        out_shape=(sem_shapes + [pltpu.HBM(s.shape, s.dtype) for s in items]
                   + [pltpu.HBM(ls, s.dtype) for ls, s in zip(land_shapes, items)]
                   + [jax.ShapeDtypeStruct((8, 128), F32)]),
        input_output_aliases={i: 2 * ng + i for i in range(2 * ni)},
        compiler_params=pltpu.CompilerParams(has_side_effects=_DATAFLOW),
    )(*[pltpu.with_memory_space_constraint(s, pltpu.HBM) for s in items],
      *[pltpu.with_memory_space_constraint(lax.empty(ls, s.dtype), pltpu.HBM) for ls, s in zip(land_shapes, items)],
      after)
    sems, thru, token = outs[:2 * ng], outs[2 * ng:2 * ng + 2 * ni], outs[-1]
    handles, i = [], 0
    for g, grp in enumerate(groups):
        n = len(grp)
        handles.append((sems[2 * g], sems[2 * g + 1], thru[i:i + n], thru[ni + i:ni + i + n]))
        i += n
    return handles, token


def gather_wait(handle, after, name):
    send, recv, shards, lands = handle
    n = len(shards)

    def body(*refs):
        shard_refs, land_refs = refs[:n], refs[n:2 * n]
        send_ref, recv_ref = refs[2 * n], refs[2 * n + 1]
        for p in range(n):
            for start_cp, recv_cp in _gather_copies(shard_refs[p], land_refs[p], send_ref, recv_ref, 4 * p):
                start_cp.wait_send()
                recv_cp.wait_recv()

    outs = pl.pallas_call(
        body, name=name,
        in_specs=[_HBM] * (2 * n) + [_SEM, _SEM, pl.BlockSpec(memory_space=pl.ANY)],
        out_specs=[_HBM] * (2 * n),
        out_shape=[pltpu.HBM(s.shape, s.dtype) for s in shards] + [pltpu.HBM(l.shape, l.dtype) for l in lands],
        input_output_aliases={i: i for i in range(2 * n)},
        compiler_params=pltpu.CompilerParams(has_side_effects=_DATAFLOW),
    )(*shards, *lands, send, recv, after)
    return outs[n:]


def _first_copies(shard, land, send, recv):
    mx, my, mc = _my_place()
    ci = 2 * mx + my
    out = []
    for q, (cx, cy) in enumerate(_other_chips(mx, my)):
        dev = (cx, cy, mc)
        out.append(tuple(pltpu.make_async_remote_copy(
            src_ref=shard.at[:, mc], dst_ref=land.at[:, slot, mc], send_sem=send.at[q], recv_sem=recv.at[q],
            device_id=dev, device_id_type=MESH) for slot in (ci, 2 * cx + cy)))
    sib = pltpu.make_async_remote_copy(src_ref=shard, dst_ref=land.at[:, ci], send_sem=send.at[3], recv_sem=recv.at[3],
                                       device_id=(mx, my, 1 - mc), device_id_type=MESH)
    return out + [(sib, sib)]


def _forward_copies(land, send, recv):
    mx, my, mc = _my_place()
    out = []
    for q, (cx, cy) in enumerate(_other_chips(mx, my)):
        out.append(tuple(pltpu.make_async_remote_copy(
            src_ref=land.at[:, 2 * cx + cy, hc], dst_ref=land.at[:, 2 * cx + cy, hc], send_sem=send.at[q],
            recv_sem=recv.at[q], device_id=(mx, my, 1 - mc), device_id_type=MESH) for hc in (mc, 1 - mc)))
    return out


def first_start(shard, after, name):
    def body(shard_ref, land_ref, after_ref, send, recv, shard_thru, land_thru, token):
        for mine, _ in _first_copies(shard_ref, land_ref, send, recv):
            mine.start()
        token[...] = jnp.zeros_like(token)

    land_shape = (shard.shape[0], N_CHIP) + shard.shape[1:]
    outs = pl.pallas_call(
        body, name=name,
        in_specs=[_HBM, _HBM, pl.BlockSpec(memory_space=pl.ANY)],
        out_specs=[_SEM, _SEM, _HBM, _HBM, pl.BlockSpec(memory_space=pltpu.VMEM)],
        out_shape=[pltpu.SemaphoreType.DMA((4,))] * 2 + [pltpu.HBM(shard.shape, shard.dtype),
                                                         pltpu.HBM(land_shape, shard.dtype),
                                                         jax.ShapeDtypeStruct((8, 128), F32)],
        input_output_aliases={0: 2, 1: 3},
        compiler_params=pltpu.CompilerParams(has_side_effects=_DATAFLOW),
    )(pltpu.with_memory_space_constraint(shard, pltpu.HBM),
      pltpu.with_memory_space_constraint(lax.empty(land_shape, shard.dtype), pltpu.HBM), after)
    return outs[:4], outs[4]


def first_forward(handle, after, name):
    send, recv, shard, land = handle

    def body(shard_ref, land_ref, send_ref, recv_ref, after_ref, send2, recv2, shard_thru, land_thru):
        firsts = _first_copies(shard_ref, land_ref, send_ref, recv_ref)
        forwards = _forward_copies(land_ref, send2, recv2)
        for q in range(3):
            firsts[q][1].wait_recv()
            forwards[q][0].start()
        firsts[3][1].wait_recv()
        for mine, _ in firsts:
            mine.wait_send()

    outs = pl.pallas_call(
        body, name=name,
        in_specs=[_HBM, _HBM, _SEM, _SEM, pl.BlockSpec(memory_space=pl.ANY)],
        out_specs=[_SEM, _SEM, _HBM, _HBM],
        out_shape=[pltpu.SemaphoreType.DMA((3,))] * 2 + [pltpu.HBM(shard.shape, shard.dtype),
                                                         pltpu.HBM(land.shape, land.dtype)],
        input_output_aliases={0: 2, 1: 3},
        compiler_params=pltpu.CompilerParams(has_side_effects=_DATAFLOW),
    )(shard, land, send, recv, after)
    return outs[0], outs[1], outs[3]


def first_wait(handle, after, name):
    send, recv, land = handle

    def body(land_ref, send_ref, recv_ref, after_ref, land_out):
        for mine, theirs in _forward_copies(land_ref, send_ref, recv_ref):
            mine.wait_send()
            theirs.wait_recv()

    return pl.pallas_call(
        body, name=name,
        in_specs=[_HBM, _SEM, _SEM, pl.BlockSpec(memory_space=pl.ANY)],
        out_specs=[_HBM],
        out_shape=[pltpu.HBM(land.shape, land.dtype)],
        input_output_aliases={0: 0},
        compiler_params=pltpu.CompilerParams(has_side_effects=_DATAFLOW),
    )(land, send, recv, after)[0]


def _sibling_copies(gs, lands, send, recv):
    mx, my, mc = _my_place()
    return [pltpu.make_async_remote_copy(
        src_ref=gs[k].at[:, :, 1 - mc], dst_ref=lands[k], send_sem=send.at[k], recv_sem=recv.at[k],
        device_id=(mx, my, 1 - mc), device_id_type=MESH) for k in range(len(gs))]


def sibling_start(gs, after, name):
    K = len(gs)

    def body(*refs):
        ins, lands = refs[:K], refs[K:2 * K]
        send, recv = refs[2 * K + 1], refs[2 * K + 2]
        for cp in _sibling_copies(ins, lands, send, recv):
            cp.start()
        refs[-1][...] = jnp.zeros_like(refs[-1])

    land_shapes = [g.shape[:2] + g.shape[3:] for g in gs]
    outs = pl.pallas_call(
        body, name=name,
        in_specs=[_HBM] * (2 * K) + [pl.BlockSpec(memory_space=pl.ANY)],
        out_specs=[_SEM, _SEM] + [_HBM] * (2 * K) + [pl.BlockSpec(memory_space=pltpu.VMEM)],
        out_shape=([pltpu.SemaphoreType.DMA((K,))] * 2 + [pltpu.HBM(g.shape, g.dtype) for g in gs]
                   + [pltpu.HBM(ls, g.dtype) for ls, g in zip(land_shapes, gs)] + [jax.ShapeDtypeStruct((8, 128), F32)]),
        input_output_aliases={i: 2 + i for i in range(2 * K)},
        compiler_params=pltpu.CompilerParams(has_side_effects=_DATAFLOW),
    )(*[pltpu.with_memory_space_constraint(g, pltpu.HBM) for g in gs],
      *[pltpu.with_memory_space_constraint(lax.empty(ls, g.dtype), pltpu.HBM) for ls, g in zip(land_shapes, gs)],
      after)
    return (outs[0], outs[1], outs[2:2 + K], outs[2 + K:2 + 2 * K]), outs[-1]


def sibling_wait(handle, after, name):
    send, recv, gs, lands = handle
    K = len(gs)

    def body(*refs):
        ins, land_refs = refs[:K], refs[K:2 * K]
        for cp in _sibling_copies(ins, land_refs, refs[2 * K], refs[2 * K + 1]):
            cp.wait_send()
            cp.wait_recv()

    outs = pl.pallas_call(
        body, name=name,
        in_specs=[_HBM] * (2 * K) + [_SEM, _SEM, pl.BlockSpec(memory_space=pl.ANY)],
        out_specs=[_HBM] * (2 * K),
        out_shape=[pltpu.HBM(g.shape, g.dtype) for g in gs] + [pltpu.HBM(l.shape, l.dtype) for l in lands],
        input_output_aliases={i: i for i in range(2 * K)},
        compiler_params=pltpu.CompilerParams(has_side_effects=_DATAFLOW),
    )(*gs, *lands, send, recv, after)
    return outs[:K], outs[K:]


def _small_copies(x, land, send, recv):
    mx, my, mc = _my_place()
    me = 4 * mx + 2 * my + mc
    out = []
    for k in range(1, N_DEV):
        peer = (1 - mx if k & 4 else mx, 1 - my if k & 2 else my, 1 - mc if k & 1 else mc)
        slot = 4 * peer[0] + 2 * peer[1] + peer[2]
        out.append(tuple(pltpu.make_async_remote_copy(
            src_ref=x, dst_ref=land.at[s], send_sem=send.at[k - 1], recv_sem=recv.at[k - 1],
            device_id=peer, device_id_type=MESH) for s in (me, slot)))
    return out


def small_start(x, after, name):
    def body(x_ref, land_ref, after_ref, send, recv, x_thru, land_thru, token):
        for mine, _ in _small_copies(x_ref, land_ref, send, recv):
            mine.start()
        token[...] = jnp.zeros_like(token)

    land_shape = (N_DEV,) + x.shape
    outs = pl.pallas_call(
        body, name=name,
        in_specs=[_HBM, _HBM, pl.BlockSpec(memory_space=pl.ANY)],
        out_specs=[_SEM, _SEM, _HBM, _HBM, pl.BlockSpec(memory_space=pltpu.VMEM)],
        out_shape=[pltpu.SemaphoreType.DMA((N_DEV - 1,))] * 2 + [pltpu.HBM(x.shape, x.dtype), pltpu.HBM(land_shape, x.dtype),
                                                                 jax.ShapeDtypeStruct((8, 128), F32)],
        input_output_aliases={0: 2, 1: 3},
        compiler_params=pltpu.CompilerParams(has_side_effects=_DATAFLOW),
    )(pltpu.with_memory_space_constraint(x, pltpu.HBM),
      pltpu.with_memory_space_constraint(lax.empty(land_shape, x.dtype), pltpu.HBM), after)
    return outs[:4], outs[4]


def small_wait(handle, after, name):
    send, recv, x, land = handle

    def body(x_ref, land_ref, send_ref, recv_ref, after_ref, x_out, land_out):
        for mine, theirs in _small_copies(x_ref, land_ref, send_ref, recv_ref):
            mine.wait_send()
            theirs.wait_recv()

    return pl.pallas_call(
        body, name=name,
        in_specs=[_HBM, _HBM, _SEM, _SEM, pl.BlockSpec(memory_space=pl.ANY)],
        out_specs=[_HBM, _HBM],
        out_shape=[pltpu.HBM(x.shape, x.dtype), pltpu.HBM(land.shape, land.dtype)],
        input_output_aliases={0: 0, 1: 1},
        compiler_params=pltpu.CompilerParams(has_side_effects=_DATAFLOW),
    )(x, land, send, recv, after)


def _scatter_copies(ps, lands, send, recv):
    mx, my, mc = _my_place()
    cps = []
    for j, (cx, cy) in enumerate(_other_chips(mx, my)):
        for k in range(len(ps)):
            cps.append(pltpu.make_async_remote_copy(
                src_ref=ps[k].at[:, 2 * cx + cy], dst_ref=lands[k].at[j],
                send_sem=send.at[k * 3 + j], recv_sem=recv.at[k * 3 + j],
                device_id=(cx, cy, mc), device_id_type=MESH))
    return cps


def scatter_start(ps, after, name):
    K = len(ps)

    def body(*refs):
        ins, lands = refs[:K], refs[K:2 * K]
        send, recv = refs[2 * K + 1], refs[2 * K + 2]
        for cp in _scatter_copies(ins, lands, send, recv):
            cp.start()
        refs[-1][...] = jnp.zeros_like(refs[-1])

    land_shapes = [(N_CHIP - 1, p.shape[0]) + p.shape[2:] for p in ps]
    outs = pl.pallas_call(
        body, name=name,
        in_specs=[_HBM] * (2 * K) + [pl.BlockSpec(memory_space=pl.ANY)],
        out_specs=[_SEM, _SEM] + [_HBM] * (2 * K) + [pl.BlockSpec(memory_space=pltpu.VMEM)],
        out_shape=([pltpu.SemaphoreType.DMA((3 * K,))] * 2 + [pltpu.HBM(p.shape, p.dtype) for p in ps]
                   + [pltpu.HBM(ls, p.dtype) for ls, p in zip(land_shapes, ps)] + [jax.ShapeDtypeStruct((8, 128), F32)]),
        input_output_aliases={i: 2 + i for i in range(2 * K)},
        compiler_params=pltpu.CompilerParams(has_side_effects=_DATAFLOW),
    )(*[pltpu.with_memory_space_constraint(p, pltpu.HBM) for p in ps],
      *[pltpu.with_memory_space_constraint(lax.empty(ls, p.dtype), pltpu.HBM) for ls, p in zip(land_shapes, ps)],
      after)
    return (outs[0], outs[1], outs[2:2 + K], outs[2 + K:2 + 2 * K]), outs[-1]


def scatter_wait(handle, after, name):
    send, recv, ps, lands = handle
    K = len(ps)
    afters = list(after) if isinstance(after, (list, tuple)) else [after]

    def body(*refs):
        ins, land_refs = refs[:K], refs[K:2 * K]
        send_ref, recv_ref = refs[2 * K], refs[2 * K + 1]
        for cp in _scatter_copies(ins, land_refs, send_ref, recv_ref):
            cp.wait_send()
            cp.wait_recv()

    outs = pl.pallas_call(
        body, name=name,
        in_specs=[_HBM] * (2 * K) + [_SEM, _SEM] + [pl.BlockSpec(memory_space=pl.ANY)] * len(afters),
        out_specs=[_HBM] * (2 * K),
        out_shape=[pltpu.HBM(p.shape, p.dtype) for p in ps] + [pltpu.HBM(l.shape, l.dtype) for l in lands],
        input_output_aliases={i: i for i in range(2 * K)},
        compiler_params=pltpu.CompilerParams(has_side_effects=_DATAFLOW),
    )(*ps, *lands, send, recv, *afters)
    return outs[:K], outs[K:]


def sibling_complete(ss, name):
    K = len(ss)

    def body(*refs):
        ins, outs = refs[:K], refs[K:2 * K]
        send, recv = refs[2 * K:]
        mx, my, mc = _my_place()
        cps = []
        for k in range(K):
            cp = pltpu.make_async_remote_copy(
                src_ref=ins[k].at[:, mc], dst_ref=outs[k].at[:, mc], send_sem=send.at[k], recv_sem=recv.at[k],
                device_id=(mx, my, 1 - mc), device_id_type=MESH)
            cp.start()
            cps.append(cp)
        for k in range(K):
            pltpu.make_async_remote_copy(
                src_ref=ins[k].at[:, mc], dst_ref=outs[k].at[:, 1 - mc], send_sem=send.at[k], recv_sem=recv.at[k],
                device_id=(mx, my, 1 - mc), device_id_type=MESH).wait_recv()
        for cp in cps:
            cp.wait_send()

    hbm = pl.BlockSpec(memory_space=pl.ANY)
    return pl.pallas_call(
        body, name=name,
        in_specs=[hbm] * K, out_specs=[hbm] * K,
        out_shape=[jax.ShapeDtypeStruct(s.shape, s.dtype) for s in ss],
        scratch_shapes=[pltpu.SemaphoreType.DMA((K,)), pltpu.SemaphoreType.DMA((K,))],
        input_output_aliases={k: k for k in range(K)},
    )(*ss)


def _rope_tables(T):
    inv = ROPE_THETA ** (-jnp.arange(0, ATT_DH, 2, dtype=F32) / ATT_DH)
    ang = jnp.arange(T, dtype=F32)[:, None] * inv[None, :]
    ang = jnp.concatenate([ang, ang, ang, ang], axis=-1)
    return jnp.cos(ang), jnp.sin(ang)


def _ffn_fwd(h, y, mod, i0, get_up, get_down, norm_next, tag):
    wgu = get_up(y)
    a, b, s = ffn_up(y, (wgu, (0,)), (wgu, (1,)), f"ffn_up_{tag}")
    wd = get_down(s)
    outs = resid_matmul([s], (wd, (0,)), h, mod, i0 + 2, 0.5, f"ffn_down_{tag}", norm_next)
    hn, o = outs[0], outs[1]
    return hn, (outs[2] if norm_next else None), (h, y, a, b, s, o), ((wgu, (0,)), (wgu, (1,)), (wd, (0,)))


def _ffn_bwd(dh, do, res, ng, i_n, mod, i0, wgT, wuT, wd, on_grads, next_gate, after, tag):
    h, y, a, b, s, o = res
    F = _wrows(wgT)
    da, db = ffn_bwd_mid(do, wd, a, b, f"ffn_bwd_mid_{tag}", after)
    gbuf = lax.empty((3, F, h.shape[1]), BF16)
    gbuf = matmul_tn(da, y, gbuf, 0, 0, f"dwg_{tag}")
    gbuf = matmul_tn(db, y, gbuf, 1, 0, f"dwu_{tag}")
    gbuf = matmul_tn(s, do, gbuf, 2, 0, f"dwd_{tag}")
    token, then = on_grads([gbuf])
    outs = dy_normbwd([(da, 0, wgT, 0, F), (db, 0, wuT, 0, F)], h, dh, ng, i_n, mod, i0 + 1,
                      f"ffn_bwd_dy_{tag}", next_gate, [token])
    return outs, then


def _mixer_fwd(h, y, mod, w_inT, w_out, sgu, cos, sin, norm_next, tag):
    lng, lnb, sw, swt, bcol = sgu
    proj = matmul_nt(y, w_inT, f"proj_{tag}")
    out_a = sgu_fwd(proj, lng, lnb, sw, bcol, f"sgu_fwd_{tag}")
    qkv = rope_fwd(proj, cos, sin, f"rope_fwd_{tag}")
    npat = len(DILATIONS)
    qkv_res = [tuple(qkv[3 * p:3 * p + 3]) for p in range(npat)]
    os_, lses = [], []
    for d, (qd, kd, vd) in zip(DILATIONS, qkv_res):
        o_d, lse_d = attn_fwd(qd, kd, vd, f"attn_fwd_d{d}_{tag}")
        os_.append(o_d)
        lses.append(lse_d)
    comb = attn_combine(os_, lses, f"attn_combine_{tag}")
    out_b, o_res, lse_res = comb[0], comb[1:1 + npat], comb[1 + npat:]
    outs = resid_matmul([out_a, out_b], w_out, h, mod, 5, 1.0, f"mix_out_{tag}", norm_next)
    hn, om = outs[0], outs[1]
    return hn, (outs[2] if norm_next else None), (h, y, proj, out_a, out_b, o_res, lse_res, qkv_res, om)


def _mixer_bwd(dh, dom, res, ng, mod, w_inT, w_out, sgu, cos, sin, on_grads, next_gate, after, tag):
    lng, lnb, sw, swt, bcol = sgu
    h, y, proj, out_a, out_b, o_res, lse_res, qkv_res, om = res
    D = h.shape[1]
    dmixed = matmul_nt(dom, w_out, f"dmixed_{tag}", after)
    woutbuf = lax.empty((1, 2 * MIX_HALF, D), BF16)
    woutbuf = matmul_tn(out_a, dom, woutbuf, 0, 0, f"dwout_a_{tag}", tmo_cap=MIX_HALF)
    woutbuf = matmul_tn(out_b, dom, woutbuf, 0, MIX_HALF, f"dwout_b_{tag}", tmo_cap=MIX_HALF)
    d_uv, d_sw, d_svec = sgu_bwd(proj, dmixed, lng, lnb, sw, swt, bcol, f"sgu_bwd_{tag}")
    do_res = to_residues(dmixed, 1, f"dout_res_{tag}")
    dqs, dks, dvs = [], [], []
    for p, (d, (qd, kd, vd)) in enumerate(zip(DILATIONS, qkv_res)):
        dq, dk, dv = attn_bwd(qd, kd, vd, do_res[p], o_res[p], lse_res[p], f"attn_bwd_d{d}_{tag}")
        dqs.append(dq)
        dks.append(dk)
        dvs.append(dv)
    d_qkv = rope_bwd(dqs, dks, dvs, cos, sin, f"rope_bwd_{tag}")
    winbuf = lax.empty((1, 5 * MIX_HALF, D), BF16)
    winbuf = matmul_tn(d_uv, y, winbuf, 0, 0, f"dwin_uv_{tag}", tmo_cap=MIX_HALF)
    winbuf = matmul_tn(d_qkv, y, winbuf, 0, 2 * MIX_HALF, f"dwin_qkv_{tag}", tmo_cap=MIX_HALF)
    token, then = on_grads([winbuf, woutbuf])
    pairs = [(d_uv, 0, w_inT, 0, 2 * MIX_HALF), (d_qkv, 0, w_inT, 1, 2 * MIX_HALF), (d_qkv, 2, w_inT, 4, MIX_HALF)]
    outs = dy_normbwd(pairs, h, dh, ng, 1, mod, 4, f"mix_bwd_dy_{tag}", next_gate, [token])
    return outs, d_sw, d_svec, then


def _local_step(x, tgt, mods, ngs, get_w, sgus, gf, on_block_grads, on_layer_small):
    T, D = x.shape
    cos, sin = _rope_tables(T)
    h = x
    saved, weights = [], []
    for l in range(2):
        def getter(blk, l=l):
            return lambda after: get_w(l, blk, after)

        if l == 0:
            y = normmod_fwd(h, ngs[0], 0, mods[0], 0, 1, "normmod_l0f1")
        h, y, r1, wf1 = _ffn_fwd(h, y, mods[l], 0, getter("f1u"), getter("f1d"), (ngs[l], 1, mods[l], 3, 4), f"l{l}f1")
        w_inT, w_out = get_w(l, "mx", h)
        h, y, r2 = _mixer_fwd(h, y, mods[l], (w_inT, (0,)), (w_out, (0,)), sgus[l], cos, sin,
                              (ngs[l], 2, mods[l], 6, 7), f"l{l}mx")
        h, y, r3, wf2 = _ffn_fwd(h, y, mods[l], 6, getter("f2u"), getter("f2d"),
                                 (ngs[l + 1], 0, mods[l + 1], 0, 1) if l + 1 < 2 else None, f"l{l}f2")
        saved.append((r1, r2, r3))
        weights.append((wf1, w_inT, w_out, wf2))
    def gate_of(l, blk):
        r1, r2, r3 = saved[l]
        o, i_g, coef = {"f2": (r3[5], 8, 0.5), "mx": (r2[-1], 5, 1.0), "f1": (r1[5], 2, 0.5)}[blk]
        return o, mods[l], i_g, coef

    seq = [(l, blk) for l in (1, 0) for blk in ("f2", "mx", "f1")]
    dh, red_final, do, red_g = final_loss_bwd(h, gf, tgt, gate_of(*seq[0]), "final_loss_bwd")
    rn, rg = {}, {}
    after = []
    for idx, (l, blk) in enumerate(seq):
        r1, r2, r3 = saved[l]
        wf1, w_inT, w_out, wf2 = weights[l]
        nxt = gate_of(*seq[idx + 1]) if idx + 1 < len(seq) else None
        rg[blk] = red_g
        tag = f"l{l}{blk}"

        def on(arrays, l=l, blk=blk):
            return on_block_grads(l, blk, arrays)

        if blk == "f2":
            outs, then = _ffn_bwd(dh, do, r3, ngs[l], 2, mods[l], 6, *wf2, on, nxt, after, tag)
        elif blk == "mx":
            outs, d_sw, d_svec, then = _mixer_bwd(dh, do, r2, ngs[l], mods[l], (w_inT, (0,)), (w_out, (0,)), sgus[l],
                                                  cos, sin, on, nxt, after, tag)
        else:
            outs, then = _ffn_bwd(dh, do, r1, ngs[l], 0, mods[l], 0, *wf1, on, nxt, after, tag)
        dh, rn[blk] = outs[0], outs[1]
        if nxt is not None:
            do, red_g = outs[2], outs[3]
        if blk == "f1":
            small = on_layer_small(l, dict(sgu_w=d_sw, sgu_vec=d_svec, red_n=(rn["f1"], rn["mx"], rn["f2"]),
                                           red_g=(rg["f1"], rg["mx"], rg["f2"])), red_final if l == 0 else None)
            after = [small, then(small)]
        else:
            after = [then(dh)]
    return dh


def _adam_out(w, g, m, v, name):
    shp = w.shape
    two_d = (-1, shp[-1])
    d, mn, vn = adamw(w.reshape(two_d), g.reshape(two_d), m.reshape(two_d), v.reshape(two_d), name)
    return g, d.reshape(shp), mn.reshape(shp), vn.reshape(shp)


def kernel(x, c, ada_w, ada_b, norm_g, ffn1_wg, ffn1_wu, ffn1_wd, ffn2_wg, ffn2_wu, ffn2_wd, w_in, sgu_ln_g, sgu_ln_b, sgu_w, sgu_b, w_out, final_g, loss_target, m_ada_w, m_ada_b, m_norm_g, m_ffn1_wg, m_ffn1_wu, m_ffn1_wd, m_ffn2_wg, m_ffn2_wu, m_ffn2_wd, m_w_in, m_sgu_ln_g, m_sgu_ln_b, m_sgu_w, m_sgu_b, m_w_out, m_final_g, v_ada_w, v_ada_b, v_norm_g, v_ffn1_wg, v_ffn1_wu, v_ffn1_wd, v_ffn2_wg, v_ffn2_wu, v_ffn2_wd, v_w_in, v_sgu_ln_g, v_sgu_ln_b, v_sgu_w, v_sgu_b, v_w_out, v_final_g):
    T, D = x.shape[1], x.shape[2]
    NL = ada_w.shape[0]
    mx, my, mc = _my_place()
    me = 4 * mx + 2 * my + mc
    ci = 2 * mx + my
    c_idx = jnp.reshape(mc, (1,)).astype(jnp.int32)
    place = jnp.stack([ci, mc]).astype(jnp.int32)

    ngw = norm_g.shape[2]
    small_in = jnp.concatenate([jnp.pad(c, ((0, 7), (0, 0))),
                                jnp.pad(norm_g.reshape(NL * 3, ngw), ((0, 8 - NL * 3), (0, D - ngw)))], axis=0)
    small_all, _ = gather_small(small_in, place, "gather_c_normg")
    c_all = small_all[:, 0, :]
    ng_parts = small_all[0::2, 8:8 + NL * 3, :ngw]
    ngs = jnp.transpose(ng_parts, (1, 0, 2)).reshape(NL, 3, N_CHIP * ngw)

    nmod = ada_w.shape[2]
    ada_b_mine = lax.dynamic_slice_in_dim(ada_b, ci * nmod, nmod, axis=1).reshape(NL, 1, nmod)
    mod_part = ada_fwd(c_all, ada_w, ada_b_mine, "ada_fwd")
    mod_all, _ = gather_small(mod_part.reshape(NL * N_DEV, nmod), place, "gather_mod")
    mod_rows = lax.dynamic_index_in_dim(mod_all.reshape(N_DEV, NL, N_DEV, nmod), me, axis=2, keepdims=False)
    mods = jnp.transpose(mod_rows[0::2], (1, 0, 2)).reshape(NL, N_ADA, D)

    sgus = []
    for l in range(NL):
        sgus.append((sgu_ln_g[l].reshape(1, MIX_HALF), sgu_ln_b[l].reshape(1, MIX_HALF), sgu_w[l],
                     jnp.swapaxes(sgu_w[l], 1, 2), jnp.transpose(sgu_b[l])))

    def halves(a):
        n, r, _ = a.shape
        return a.reshape(n, 2, r // 2, D)

    first_group = halves(jnp.stack([ffn1_wg[0].T, ffn1_wu[0].T], axis=0).astype(BF16))
    first_handle, first_token = first_start(first_group, mods, "first_start")
    zero = first_token[0, 0]
    mods = mods + zero

    def prep(a):
        return (a + zero).astype(BF16)

    groups = []
    for l in range(NL):
        groups += [[halves(jnp.stack([prep(ffn1_wg[l].T), prep(ffn1_wu[l].T)], axis=0))],
                   [halves(prep(ffn1_wd[l])[None])],
                   [halves(prep(w_in[l].T)[None]), halves(prep(w_out[l])[None])],
                   [halves(jnp.stack([prep(ffn2_wg[l].T), prep(ffn2_wu[l].T)], axis=0))],
                   [halves(prep(ffn2_wd[l])[None])]]
    handles, token = gather_start(groups[1:], mods, "gather_start")
    handles = [None] + handles
    mods = mods + token[0, 0]
    group_no = {"f1u": 0, "f1d": 1, "mx": 2, "f2u": 3, "f2d": 4}

    def get_w(l, key, after):
        g = len(group_no) * l + group_no[key]
        if g == 0:
            full = [first_wait(first_forward(first_handle, after, "first_forward"), place, "first_wait")]
        else:
            full = gather_wait(handles[g], after, f"gather_wait_l{l}{key}")
        full = [a.reshape(a.shape[0], N_CHIP * 2 * a.shape[3], D) for a in full]
        return full[0] if key != "mx" else tuple(full)

    def split(a):
        n, r4, _ = a.shape
        return a.reshape(n, N_CHIP, 2, r4 // N_CHIP // 2, D)

    pending, small_pending, small_tokens = {}, {}, {}

    def on_block_grads(l, blk, bufs):
        tag = f"l{l}{blk}"
        sib, tok1 = sibling_start([split(g) for g in bufs], place, f"rs_sibling_start_{tag}")

        def then(after):
            parts, lands = sibling_wait(sib, after, f"rs_sibling_wait_{tag}")
            psums = [sum_halves(g, ld, c_idx, f"rs_sum_halves_{tag}_{i}") for i, (g, ld) in enumerate(zip(parts, lands))]
            pending[(l, blk)], tok2 = scatter_start(psums, lands[0], f"rs_chips_start_{tag}")
            return tok2

        return tok1, then

    def blocks_finish(blocks, after, tag):
        ssums, counts = [], []
        for l, blk in blocks:
            psums, lands2 = scatter_wait(pending.pop((l, blk)), after, f"rs_chips_wait_l{l}{blk}")
            ssums += [sum_chips(p, ld, place, f"rs_sum_chips_l{l}{blk}_{i}") for i, (p, ld) in enumerate(zip(psums, lands2))]
            counts.append(len(psums))
        fins = [f.reshape(f.shape[0], -1, D) for f in sibling_complete(ssums, f"rs_complete_{tag}")]
        out, i = [], 0
        for n in counts:
            out.append(fins[i:i + n])
            i += n
        return out

    def on_layer_small(l, grads, red_final):
        blocks = list(grads["red_n"]) + list(grads["red_g"])
        blocks.append(jnp.pad(grads["sgu_vec"], ((0, 0), (0, D - MIX_HALF))))
        blocks.append(grads["sgu_w"].reshape(-1, D))
        if red_final is not None:
            blocks.append(red_final)
        xs = jnp.concatenate(blocks, axis=0)
        small_pending[l], small_tokens[l] = small_start(xs, place, f"small_start_l{l}")
        return small_tokens[l]

    grad_x = _local_step(x[0], loss_target[0], mods, ngs, get_w, sgus, final_g.reshape(1, D),
                         on_block_grads, on_layer_small)

    adam_state = {}

    def adam_big(nm, l, g, w, m, v):
        adam_state[nm] = adamw_layer(w, g, m, v, l, adam_state.get(nm), f"adamw_{nm}_l{l}")

    def adam_block(l, blk, fin):
        if blk == "mx":
            adam_big("w_in", l, fin[0][0].T, w_in, m_w_in, v_w_in)
            adam_big("w_out", l, fin[1][0], w_out, m_w_out, v_w_out)
        else:
            ws = ((ffn1_wg, m_ffn1_wg, v_ffn1_wg), (ffn1_wu, m_ffn1_wu, v_ffn1_wu), (ffn1_wd, m_ffn1_wd, v_ffn1_wd)) \
                if blk == "f1" else \
                ((ffn2_wg, m_ffn2_wg, v_ffn2_wg), (ffn2_wu, m_ffn2_wu, v_ffn2_wu), (ffn2_wd, m_ffn2_wd, v_ffn2_wd))
            pre = "ffn1" if blk == "f1" else "ffn2"
            for k, (nm, tr) in enumerate((("wg", True), ("wu", True), ("wd", False))):
                adam_big(f"{pre}_{nm}", l, fin[0][k], *[jnp.swapaxes(t, 1, 2) if tr else t for t in ws[k]])

    done_order = [(l, blk) for l in range(NL - 1, -1, -1) for blk in ("f2", "mx", "f1")]
    for (l, blk), fin in zip(done_order[:-1], blocks_finish(done_order[:-1], small_tokens[0], "early")):
        adam_block(l, blk, fin)
    last_big = adam_state["w_out"][1]

    small_sum, small_all = [], []
    for l in range(NL):
        xs, land = small_wait(small_pending[l], last_big, f"small_wait_l{l}")
        full = lax.dynamic_update_slice(land, xs[None], (me, 0, 0))
        small_all.append(full)
        small_sum.append(sum_slots(full, f"small_sum_l{l}"))
    offs = [8 * i for i in range(8)]
    off_final = offs[7] + SGU_HEADS * ATT_BLOCK * HEAD_LANES // D
    loss = small_sum[0][off_final + 1, 0]
    g_final_g = small_sum[0][off_final, :]
    g_norm_g, g_ada_b, g_lng, g_lnb, g_sb, g_sw, dmod_all = [], [], [], [], [], [], []
    for l in range(NL):
        rn = [small_sum[l][offs[i]:offs[i] + 8] for i in range(3)]
        rg = [small_sum[l][offs[3 + i]:offs[3 + i] + 8] for i in range(3)]
        g_norm_g.append(jnp.stack([rn[i][2] for i in range(3)], axis=0))
        g_ada_b.append(jnp.concatenate([jnp.stack([rn[i][0], rn[i][1], rg[i][0]], axis=0) for i in range(3)],
                                       axis=0).reshape(N_ADA * D))
        sv = small_sum[l][offs[6]:offs[6] + 8, :MIX_HALF]
        g_lng.append(sv[0].reshape(SGU_HEADS, HEAD_LANES))
        g_lnb.append(sv[1].reshape(SGU_HEADS, HEAD_LANES))
        g_sb.append(sv[2].reshape(SGU_HEADS, ATT_BLOCK))
        g_sw.append(small_sum[l][offs[7]:off_final].reshape(sgu_w.shape[1:]))
        rows = []
        for i in range(3):
            an = small_all[l][:, offs[i]:offs[i] + 2]
            ag = small_all[l][:, offs[3 + i]:offs[3 + i] + 1]
            rows += [an[:, 0], an[:, 1], ag[:, 0]]
        dmod_all.append(jnp.stack(rows, axis=1).reshape(N_DEV, N_ADA * D))
    dmod_all = jnp.stack(dmod_all, axis=0)
    dmod_mine = lax.dynamic_slice_in_dim(dmod_all, ci * nmod, nmod, axis=2)
    g_ada_w = ada_bwd(jnp.transpose(c_all), dmod_mine, "ada_bwd")
    g_ada_b = jnp.stack(g_ada_b, axis=0)
    g_norm_g_full = jnp.stack(g_norm_g, axis=0)
    g_norm_g_mine = lax.dynamic_slice_in_dim(g_norm_g_full, ci * ngw, ngw, axis=2)

    small_params = [
        ("ada_w", ada_w, g_ada_w, m_ada_w, v_ada_w),
        ("ada_b", ada_b, g_ada_b, m_ada_b, v_ada_b),
        ("norm_g", norm_g, g_norm_g_mine, m_norm_g, v_norm_g),
        ("sgu_ln_g", sgu_ln_g, jnp.stack(g_lng, axis=0), m_sgu_ln_g, v_sgu_ln_g),
        ("sgu_ln_b", sgu_ln_b, jnp.stack(g_lnb, axis=0), m_sgu_ln_b, v_sgu_ln_b),
        ("sgu_w", sgu_w, jnp.stack(g_sw, axis=0), m_sgu_w, v_sgu_w),
        ("sgu_b", sgu_b, jnp.stack(g_sb, axis=0), m_sgu_b, v_sgu_b),
        ("final_g", final_g.reshape(1, D), g_final_g.reshape(1, D), m_final_g.reshape(1, D), v_final_g.reshape(1, D)),
    ]
    for nm, w, g, m, v in small_params:
        res = _adam_out(w, g, m, v, f"adamw_{nm}")
        adam_state[nm] = tuple(t.reshape(D) for t in res) if nm == "final_g" else res

    l, blk = done_order[-1]
    adam_block(l, blk, blocks_finish([(l, blk)], [st[1] for st in adam_state.values()], "last")[0])

    names = ["ada_w", "ada_b", "norm_g", "ffn1_wg", "ffn1_wu", "ffn1_wd", "ffn2_wg", "ffn2_wu", "ffn2_wd", "w_in",
             "sgu_ln_g", "sgu_ln_b", "sgu_w", "sgu_b", "w_out", "final_g"]
    shapes = [t.shape for t in (ada_w, ada_b, norm_g, ffn1_wg, ffn1_wu, ffn1_wd, ffn2_wg, ffn2_wu, ffn2_wd, w_in,
                                sgu_ln_g, sgu_ln_b, sgu_w, sgu_b, w_out, final_g)]
    def shaped(nm, t, s):
        if nm in ("ffn1_wg", "ffn1_wu", "ffn2_wg", "ffn2_wu"):
            return jnp.swapaxes(t.reshape(s[0], s[2], s[1]), 1, 2)
        return t.reshape(s)

    return (loss, grad_x[None], *[shaped(nm, adam_state[nm][i], s) for i in range(4) for nm, s in zip(names, shapes)])
```

```python
import math

import jax
import jax.numpy as jnp
from jax import lax
from jax.experimental import pallas as pl
from jax.experimental.pallas import tpu as pltpu

F32 = jnp.float32
BF16 = jnp.bfloat16
EPS = 1e-6
SGU_HEADS = 4
HEAD_LANES = 128
ATT_DH = 64
ATT_BLOCK = 128
MIX_HALF = SGU_HEADS * HEAD_LANES
DILATIONS = (1, 4, 16)
ROPE_THETA = 10000.0
N_ADA = 9
ADAM_LR, ADAM_B1, ADAM_B2, ADAM_EPS, ADAM_WD, ADAM_STEP = 0.001, 0.9, 0.999, 1e-08, 0.01, 10
NEG = -1e30
V7X_VMEM_BYTES = 64 * 1024 * 1024
VMEM_LIMIT = V7X_VMEM_BYTES * 7 // 8
MESH = pl.DeviceIdType.MESH
N_DEV = 8
N_CHIP = 4
_ANY = pl.BlockSpec(memory_space=pl.ANY)


def _tile(n, cap, mult):
    if n <= cap:
        return n
    t = (cap // mult) * mult
    while t >= mult:
        if n % t == 0:
            return t
        t -= mult
    raise ValueError((n, cap, mult))


def _params(dims=None):
    return pltpu.CompilerParams(dimension_semantics=dims, vmem_limit_bytes=VMEM_LIMIT)


def _wspec(w, rows, idx, resident=False):
    arr, lead = w
    kw = dict(pipeline_mode=pl.Buffered(1)) if resident else {}
    return pl.BlockSpec((None,) * len(lead) + (rows, arr.shape[-1]), lambda *g: tuple(lead) + (idx(*g), 0), **kw)


def _wrows(w):
    return w[0].shape[-2]


def _nt(a, b):
    return lax.dot_general(a, b, (((1,), (1,)), ((), ())), preferred_element_type=F32)


def _tn(a, b):
    return lax.dot_general(a, b, (((0,), (0,)), ((), ())), preferred_element_type=F32)


def _nn(a, b):
    return jnp.dot(a, b, preferred_element_type=F32)


def _sigmoid(x):
    return 0.5 * jnp.tanh(0.5 * x) + 0.5


_GELU_K = math.sqrt(2.0 / math.pi)
_GELU_C = 0.044715


def _gelu(x):
    t = jnp.tanh(_GELU_K * (x + _GELU_C * x * x * x))
    return 0.5 * x * (1.0 + t)


def _gelu_and_grad(x):
    x2 = x * x
    t = jnp.tanh(_GELU_K * (x + _GELU_C * x * x2))
    g = 0.5 * x * (1.0 + t)
    dg = 0.5 * (1.0 + t) + 0.5 * x * (1.0 - t * t) * (_GELU_K * (1.0 + 3.0 * _GELU_C * x2))
    return g, dg


def normmod_fwd(h, ng, i_n, mod, i_sh, i_sc, name):
    T, D = h.shape
    tm = _tile(T, 512, 8)

    def body(h_ref, ng_ref, mod_ref, y_ref):
        y_ref[...] = _normmod(h_ref[...], ng_ref[i_n:i_n + 1, :], mod_ref[i_sh:i_sh + 1, :],
                              mod_ref[i_sc:i_sc + 1, :]).astype(BF16)

    return pl.pallas_call(
        body, name=name, grid=(T // tm,),
        in_specs=[pl.BlockSpec((tm, D), lambda i: (i, 0)),
                  pl.BlockSpec(ng.shape, lambda i: (0, 0)),
                  pl.BlockSpec(mod.shape, lambda i: (0, 0))],
        out_specs=pl.BlockSpec((tm, D), lambda i: (i, 0)),
        out_shape=jax.ShapeDtypeStruct((T, D), BF16),
        compiler_params=_params(("parallel",)),
    )(h, ng, mod)


def ffn_up(y, wgT, wuT, name):
    T, D = y.shape
    F = _wrows(wgT)
    tm = _tile(T, 512, 16)
    tf = _tile(F, 2816, 256)
    cuts = list(range(0, tf, 768)) + [tf]

    def body(y_ref, wg_ref, wu_ref, p_ref, q_ref, s_ref):
        yv = y_ref[...]
        for c0, c1 in zip(cuts[:-1], cuts[1:]):
            a = _nt(yv, wg_ref[c0:c1, :])
            b = _nt(yv, wu_ref[c0:c1, :])
            sig = _sigmoid(a)
            q = a * sig
            p_ref[:, c0:c1] = (b * (sig + q * (1.0 - sig))).astype(BF16)
            q_ref[:, c0:c1] = q.astype(BF16)
            s_ref[:, c0:c1] = (q * b).astype(BF16)

    act = jax.ShapeDtypeStruct((T, F), BF16)
    return pl.pallas_call(
        body, name=name, grid=(F // tf, T // tm),
        in_specs=[pl.BlockSpec((tm, D), lambda j, i: (i, 0)),
                  _wspec(wgT, tf, lambda j, i: j, resident=True),
                  _wspec(wuT, tf, lambda j, i: j, resident=True)],
        out_specs=[pl.BlockSpec((tm, tf), lambda j, i: (i, j))] * 3,
        out_shape=[act, act, act],
        compiler_params=_params(("parallel", "parallel")),
    )(y, wgT[0], wuT[0])


def _normmod(x, gn, sh, sc):
    r = lax.rsqrt(jnp.mean(x * x, axis=-1, keepdims=True) + EPS)
    return ((x * r) * gn) * (1.0 + sc) + sh


def resid_matmul(xs, w, h, mod, i_g, coef, name, norm_next=None):
    T, D = h.shape
    kb = xs[0].shape[1]
    assert all(x.shape == (T, kb) for x in xs) and _wrows(w) == kb * len(xs)
    tm = _tile(T, 1024, 16)
    nx = len(xs)
    n_in, n_out, n_shape, n_ops = [], [], [], []
    if norm_next:
        ng_n, i_n, mod_n, i_sh, i_sc = norm_next
        n_in = [pl.BlockSpec(ng_n.shape, lambda i: (0, 0)), pl.BlockSpec(mod_n.shape, lambda i: (0, 0))]
        n_out = [pl.BlockSpec((tm, D), lambda i: (i, 0))]
        n_shape = [jax.ShapeDtypeStruct((T, D), BF16)]
        n_ops = [ng_n, mod_n]

    def body(*refs):
        x_refs, w_refs = refs[:nx], refs[nx:2 * nx]
        h_ref, mod_ref = refs[2 * nx:2 * nx + 2]
        hn_ref, o_ref = refs[2 * nx + 2 + len(n_in):2 * nx + 4 + len(n_in)]
        o = _nn(x_refs[0][...], w_refs[0][...])
        for xr, wr in zip(x_refs[1:], w_refs[1:]):
            o = o + _nn(xr[...], wr[...])
        o_ref[...] = o.astype(BF16)
        hn = h_ref[...] + (coef * mod_ref[i_g:i_g + 1, :]) * o
        hn_ref[...] = hn
        if norm_next:
            ng_ref, modn_ref = refs[2 * nx + 2], refs[2 * nx + 3]
            refs[-1][...] = _normmod(hn, ng_ref[i_n:i_n + 1, :], modn_ref[i_sh:i_sh + 1, :],
                                     modn_ref[i_sc:i_sc + 1, :]).astype(BF16)

    return pl.pallas_call(
        body, name=name, grid=(T // tm,),
        in_specs=([pl.BlockSpec((tm, kb), lambda i: (i, 0))] * nx
                  + [_wspec(w, kb, lambda i, p=p: p, resident=True) for p in range(nx)]
                  + [pl.BlockSpec((tm, D), lambda i: (i, 0)),
                     pl.BlockSpec(mod.shape, lambda i: (0, 0))] + n_in),
        out_specs=[pl.BlockSpec((tm, D), lambda i: (i, 0))] * 2 + n_out,
        out_shape=[jax.ShapeDtypeStruct((T, D), F32), jax.ShapeDtypeStruct((T, D), BF16)] + n_shape,
        compiler_params=_params(("parallel",)),
    )(*xs, *([w[0]] * nx), h, mod, *n_ops)


def _gate_specs(gate, tm, D):
    o, mod, _, _ = gate
    T = o.shape[0]
    return ([pl.BlockSpec((tm, D), lambda i: (i, 0)), pl.BlockSpec(mod.shape, lambda i: (0, 0))],
            [pl.BlockSpec((tm, D), lambda i: (i, 0)), pl.BlockSpec((8, D), lambda i: (0, 0))],
            [jax.ShapeDtypeStruct((T, D), BF16), jax.ShapeDtypeStruct((8, D), F32)],
            [o, mod])


def _gate_emit(d, gate, o_ref, mod_ref, do_ref, red_ref):
    _, _, i_g, coef = gate
    do_ref[...] = (d * (coef * mod_ref[i_g:i_g + 1, :])).astype(BF16)

    @pl.when(pl.program_id(0) == 0)
    def _():
        red_ref[...] = jnp.zeros_like(red_ref)

    red_ref[0:1, :] += coef * jnp.sum(d * o_ref[...].astype(F32), axis=0, keepdims=True)


def ffn_bwd_mid(do, wd, p, q, name, after=()):
    T, D = do.shape
    F = _wrows(wd)
    tm = _tile(T, 512, 16)
    tf = _tile(F, 2816, 256)
    cuts = list(range(0, tf, 256)) + [tf]

    def body(do_ref, wd_ref, p_ref, q_ref, *rest):
        da_ref, db_ref = rest[-2:]
        dov = do_ref[...]
        for c0, c1 in zip(cuts[:-1], cuts[1:]):
            ds = _nt(dov, wd_ref[c0:c1, :])
            da_ref[:, c0:c1] = (ds * p_ref[:, c0:c1].astype(F32)).astype(BF16)
            db_ref[:, c0:c1] = (ds * q_ref[:, c0:c1].astype(F32)).astype(BF16)

    act = jax.ShapeDtypeStruct((T, F), BF16)
    return pl.pallas_call(
        body, name=name, grid=(F // tf, T // tm),
        in_specs=[pl.BlockSpec((tm, D), lambda j, i: (i, 0)),
                  _wspec(wd, tf, lambda j, i: j, resident=True),
                  pl.BlockSpec((tm, tf), lambda j, i: (i, j)),
                  pl.BlockSpec((tm, tf), lambda j, i: (i, j))] + [_ANY] * len(after),
        out_specs=[pl.BlockSpec((tm, tf), lambda j, i: (i, j))] * 2,
        out_shape=[act, act],
        compiler_params=_params(("parallel", "parallel")),
    )(do, wd[0], p, q, *after)


def dy_normbwd(pairs, h, dhp, ng, i_n, mod, i_sc, name, gate=None, after=()):
    T, D = h.shape
    tm = _tile(T, 512, 16)
    npair = len(pairs)
    g_in, g_out, g_shape, g_ops = _gate_specs(gate, tm, D) if gate else ([], [], [], [])
    n_in = 2 * npair + 4 + len(g_in) + len(after)

    def body(*refs):
        x_refs, w_refs = refs[:npair], refs[npair:2 * npair]
        h_ref, dhp_ref, ng_ref, mod_ref = refs[2 * npair:2 * npair + 4]
        dh_ref, red_ref = refs[n_in:n_in + 2]
        dy = _nn(x_refs[0][...], w_refs[0][...])
        for xr, wr in zip(x_refs[1:], w_refs[1:]):
            dy = dy + _nn(xr[...], wr[...])
        x = h_ref[...]
        r = lax.rsqrt(jnp.mean(x * x, axis=-1, keepdims=True) + EPS)
        n = x * r
        gn = ng_ref[i_n:i_n + 1, :]
        sc1 = 1.0 + mod_ref[i_sc:i_sc + 1, :]
        w = sc1 * gn
        dyn = dy * n
        col = jnp.sum(dyn, axis=0, keepdims=True)

        @pl.when(pl.program_id(0) == 0)
        def _():
            red_ref[...] = jnp.zeros_like(red_ref)

        red_ref[0:1, :] += jnp.sum(dy, axis=0, keepdims=True)
        red_ref[1:2, :] += gn * col
        red_ref[2:3, :] += sc1 * col
        dh_new = dhp_ref[...] + r * (dy * w - n * jnp.mean(dyn * w, axis=-1, keepdims=True))
        dh_ref[...] = dh_new
        if gate:
            _gate_emit(dh_new, gate, refs[2 * npair + 4], refs[2 * npair + 5], refs[-2], refs[-1])

    in_specs = ([pl.BlockSpec((tm, kb), lambda i, c=c: (i, c)) for (_, c, _, _, kb) in pairs]
                + [_wspec(w, kb, lambda i, r=r: r, resident=True) for (_, _, w, r, kb) in pairs]
                + [pl.BlockSpec((tm, D), lambda i: (i, 0)),
                   pl.BlockSpec((tm, D), lambda i: (i, 0)),
                   pl.BlockSpec(ng.shape, lambda i: (0, 0)),
                   pl.BlockSpec(mod.shape, lambda i: (0, 0))] + g_in + [_ANY] * len(after))
    return pl.pallas_call(
        body, name=name, grid=(T // tm,), in_specs=in_specs,
        out_specs=[pl.BlockSpec((tm, D), lambda i: (i, 0)), pl.BlockSpec((8, D), lambda i: (0, 0))] + g_out,
        out_shape=[jax.ShapeDtypeStruct((T, D), F32), jax.ShapeDtypeStruct((8, D), F32)] + g_shape,
        compiler_params=_params(("arbitrary",)),
    )(*[p[0] for p in pairs], *[p[2][0] for p in pairs], h, dhp, ng, mod, *g_ops, *after)


def matmul_tn(a, b, buf, slot, row0, name, tmo_cap=1408):
    T, N = b.shape
    ma = a.shape[1]
    tmo = _tile(ma, tmo_cap, 128)
    assert row0 % tmo == 0
    nmo = ma // tmo
    tk = _tile(T, 2048, 16)
    nk = T // tk

    def body(a_ref, b_ref, buf_ref, o_ref, acc_ref):
        k = pl.program_id(1)

        @pl.when(k == 0)
        def _():
            acc_ref[...] = jnp.zeros_like(acc_ref)

        acc_ref[...] += _tn(a_ref[...], b_ref[...])

        @pl.when(k == nk - 1)
        def _():
            o_ref[...] = acc_ref[...].astype(BF16)

    return pl.pallas_call(
        body, name=name, grid=(nmo, nk),
        in_specs=[pl.BlockSpec((tk, tmo), lambda j, k: (k, j)),
                  pl.BlockSpec((tk, N), lambda j, k: (k, 0)),
                  pl.BlockSpec(memory_space=pl.ANY)],
        out_specs=pl.BlockSpec((None, tmo, N), lambda j, k: (slot, row0 // tmo + j, 0)),
        out_shape=jax.ShapeDtypeStruct(buf.shape, BF16),
        scratch_shapes=[pltpu.VMEM((tmo, N), F32)],
        input_output_aliases={2: 0},
        compiler_params=_params(("parallel", "arbitrary")),
    )(a, b, buf)


def matmul_nt(x, w, name, after=()):
    T, K = x.shape
    N = _wrows(w)
    tm = _tile(T, 1024, 16)
    tn = _tile(N, 1280, 128)

    def body(x_ref, w_ref, *rest):
        rest[-1][...] = _nt(x_ref[...], w_ref[...]).astype(BF16)

    return pl.pallas_call(
        body, name=name, grid=(N // tn, T // tm),
        in_specs=[pl.BlockSpec((tm, K), lambda j, i: (i, 0)), _wspec(w, tn, lambda j, i: j)] + [_ANY] * len(after),
        out_specs=pl.BlockSpec((tm, tn), lambda j, i: (i, j)),
        out_shape=jax.ShapeDtypeStruct((T, N), BF16),
        compiler_params=_params(("parallel", "parallel")),
    )(x, w[0], *after)


def _sgu_head_fwd(u, v, lng, lnb):
    gu, dgu = _gelu_and_grad(u)
    gv, dgv = _gelu_and_grad(v)
    mu = jnp.mean(gv, axis=-1, keepdims=True)
    xc = gv - mu
    rstd = lax.rsqrt(jnp.mean(xc * xc, axis=-1, keepdims=True) + EPS)
    xhat = xc * rstd
    vn = xhat * lng + lnb
    return gu, dgu, dgv, rstd, xhat, vn


def _tril_mask():
    r = lax.broadcasted_iota(jnp.int32, (ATT_BLOCK, ATT_BLOCK), 0)
    c = lax.broadcasted_iota(jnp.int32, (ATT_BLOCK, ATT_BLOCK), 1)
    return c <= r


def _triu_mask():
    r = lax.broadcasted_iota(jnp.int32, (ATT_BLOCK, ATT_BLOCK), 0)
    c = lax.broadcasted_iota(jnp.int32, (ATT_BLOCK, ATT_BLOCK), 1)
    return r <= c


def sgu_fwd(proj, lng, lnb, w, bcol, name):
    T = proj.shape[0]
    tm = _tile(T, 512, 128)
    nch = tm // ATT_BLOCK

    def body(u_ref, v_ref, lng_ref, lnb_ref, w_ref, b_ref, o_ref):
        tril = _tril_mask()
        for hd in range(SGU_HEADS):
            sl = slice(hd * HEAD_LANES, (hd + 1) * HEAD_LANES)
            u = u_ref[:, sl].astype(F32)
            v = v_ref[:, sl].astype(F32)
            gu, _, _, _, _, vn = _sgu_head_fwd(u, v, lng_ref[:, sl], lnb_ref[:, sl])
            wm = jnp.where(tril, w_ref[hd], 0.0).astype(BF16)
            vnb = vn.astype(BF16)
            bc = b_ref[:, hd:hd + 1]
            for ch in range(nch):
                rs = slice(ch * ATT_BLOCK, (ch + 1) * ATT_BLOCK)
                z = _nn(wm, vnb[rs, :]) + bc
                o_ref[rs, sl] = (gu[rs, :] * z).astype(BF16)

    return pl.pallas_call(
        body, name=name, grid=(T // tm,),
        in_specs=[pl.BlockSpec((tm, MIX_HALF), lambda i: (i, 0)),
                  pl.BlockSpec((tm, MIX_HALF), lambda i: (i, 1)),
                  pl.BlockSpec((1, MIX_HALF), lambda i: (0, 0)),
                  pl.BlockSpec((1, MIX_HALF), lambda i: (0, 0)),
                  pl.BlockSpec(w.shape, lambda i: (0, 0, 0)),
                  pl.BlockSpec(bcol.shape, lambda i: (0, 0))],
        out_specs=pl.BlockSpec((tm, MIX_HALF), lambda i: (i, 0)),
        out_shape=jax.ShapeDtypeStruct((T, MIX_HALF), BF16),
        compiler_params=_params(("parallel",)),
    )(proj, proj, lng, lnb, w, bcol)


def sgu_bwd(proj, dmixed, lng, lnb, w, wt, bcol, name):
    T = proj.shape[0]
    tm = _tile(T, 512, 128)
    nch = tm // ATT_BLOCK
    nsteps = T // tm

    def body(u_ref, v_ref, g_ref, lng_ref, lnb_ref, w_ref, wt_ref, b_ref, duv_ref, dw_ref, dvec_ref, bacc_ref):
        step = pl.program_id(0)

        @pl.when(step == 0)
        def _():
            dw_ref[...] = jnp.zeros_like(dw_ref)
            dvec_ref[...] = jnp.zeros_like(dvec_ref)
            bacc_ref[...] = jnp.zeros_like(bacc_ref)

        tril = _tril_mask()
        triu = _triu_mask()
        for hd in range(SGU_HEADS):
            sl = slice(hd * HEAD_LANES, (hd + 1) * HEAD_LANES)
            u = u_ref[:, sl].astype(F32)
            v = v_ref[:, sl].astype(F32)
            lng_h = lng_ref[:, sl]
            gu, dgu, dgv, rstd, xhat, vn = _sgu_head_fwd(u, v, lng_h, lnb_ref[:, sl])
            wm = jnp.where(tril, w_ref[hd], 0.0).astype(BF16)
            wmt = jnp.where(triu, wt_ref[hd], 0.0).astype(BF16)
            vnb = vn.astype(BF16)
            bc = b_ref[:, hd:hd + 1]
            g = g_ref[:, sl].astype(F32)
            dw_acc = jnp.zeros((ATT_BLOCK, ATT_BLOCK), F32)
            b_acc = jnp.zeros((ATT_BLOCK, HEAD_LANES), F32)
            dvn_parts = []
            for ch in range(nch):
                rs = slice(ch * ATT_BLOCK, (ch + 1) * ATT_BLOCK)
                z = _nn(wm, vnb[rs, :]) + bc
                duv_ref[rs, sl] = (g[rs, :] * z * dgu[rs, :]).astype(BF16)
                dz = g[rs, :] * gu[rs, :]
                dzb = dz.astype(BF16)
                dvn_parts.append(_nn(wmt, dzb))
                dw_acc = dw_acc + _nt(dzb, vnb[rs, :])
                b_acc = b_acc + dz
            dvn = jnp.concatenate(dvn_parts, axis=0)
            dw_ref[hd] += jnp.where(tril, dw_acc, 0.0)
            bacc_ref[hd] += b_acc
            dvec_ref[0:1, sl] += jnp.sum(dvn * xhat, axis=0, keepdims=True)
            dvec_ref[1:2, sl] += jnp.sum(dvn, axis=0, keepdims=True)
            dxh = dvn * lng_h
            dgv_in = rstd * (dxh - jnp.mean(dxh, axis=-1, keepdims=True)
                             - xhat * jnp.mean(dxh * xhat, axis=-1, keepdims=True))
            duv_ref[:, MIX_HALF + hd * HEAD_LANES:MIX_HALF + (hd + 1) * HEAD_LANES] = (dgv_in * dgv).astype(BF16)

        @pl.when(step == nsteps - 1)
        def _():
            for hd in range(SGU_HEADS):
                sl = slice(hd * HEAD_LANES, (hd + 1) * HEAD_LANES)
                dvec_ref[2:3, sl] = jnp.sum(bacc_ref[hd].T, axis=0, keepdims=True)

    return pl.pallas_call(
        body, name=name, grid=(nsteps,),
        in_specs=[pl.BlockSpec((tm, MIX_HALF), lambda i: (i, 0)),
                  pl.BlockSpec((tm, MIX_HALF), lambda i: (i, 1)),
                  pl.BlockSpec((tm, MIX_HALF), lambda i: (i, 0)),
                  pl.BlockSpec((1, MIX_HALF), lambda i: (0, 0)),
                  pl.BlockSpec((1, MIX_HALF), lambda i: (0, 0)),
                  pl.BlockSpec(w.shape, lambda i: (0, 0, 0)),
                  pl.BlockSpec(w.shape, lambda i: (0, 0, 0)),
                  pl.BlockSpec(bcol.shape, lambda i: (0, 0))],
        out_specs=[pl.BlockSpec((tm, 2 * MIX_HALF), lambda i: (i, 0)),
                   pl.BlockSpec(w.shape, lambda i: (0, 0, 0)),
                   pl.BlockSpec((8, MIX_HALF), lambda i: (0, 0))],
        out_shape=[jax.ShapeDtypeStruct((T, 2 * MIX_HALF), BF16),
                   jax.ShapeDtypeStruct(w.shape, F32),
                   jax.ShapeDtypeStruct((8, MIX_HALF), F32)],
        scratch_shapes=[pltpu.VMEM((SGU_HEADS, ATT_BLOCK, HEAD_LANES), F32)],
        compiler_params=_params(("arbitrary",)),
    )(proj, proj, dmixed, lng, lnb, w, wt, bcol)


def _rot_half(t):
    lane = lax.broadcasted_iota(jnp.int32, t.shape, 1)
    first = (lane % ATT_DH) < (ATT_DH // 2)
    return jnp.where(first, -pltpu.roll(t, HEAD_LANES - ATT_DH // 2, 1), pltpu.roll(t, ATT_DH // 2, 1))


LAYOUT_ROWS = 512


def _res_spec(d, tm, W):
    return pl.BlockSpec((d, tm // d, W), lambda i: (0, i, 0))


def _res_shape(d, T, W, dtype):
    return jax.ShapeDtypeStruct((d, T // d, W), dtype)


def _slab_buf(tm, W):
    return pltpu.VMEM((W // HEAD_LANES, tm, HEAD_LANES), F32)


def _lanes(hp):
    return slice(hp * HEAD_LANES, (hp + 1) * HEAD_LANES)


def _to_res(buf, out_ref, d, dtype):
    nslab, tm, _ = buf.shape
    for hp in range(nslab):
        if d == 1:
            out_ref[0, :, _lanes(hp)] = buf[hp].astype(dtype)
        else:
            for r in range(d):
                out_ref[r, :, _lanes(hp)] = buf.at[hp][pl.ds(r, tm // d, stride=d), :].astype(dtype)


def _from_res(in_ref, buf, d):
    nslab, tm, _ = buf.shape
    for hp in range(nslab):
        if d == 1:
            buf[hp] = in_ref[0, :, _lanes(hp)].astype(F32)
        else:
            for r in range(d):
                buf.at[hp][pl.ds(r, tm // d, stride=d), :] = in_ref[r, :, _lanes(hp)].astype(F32)


def rope_fwd(proj, cos, sin, name):
    T = proj.shape[0]
    tm = LAYOUT_ROWS
    scale = 1.0 / math.sqrt(ATT_DH)
    nd = len(DILATIONS)

    def body(q_ref, k_ref, v_ref, cos_ref, sin_ref, *rest):
        outs, buf = rest[:3 * nd], rest[3 * nd]
        c = cos_ref[...]
        s = sin_ref[...]
        for which, src in enumerate((q_ref, k_ref, v_ref)):
            for hp in range(MIX_HALF // HEAD_LANES):
                t = src[:, _lanes(hp)].astype(F32)
                if which == 0:
                    t = scale * (t * c + _rot_half(t) * s)
                elif which == 1:
                    t = t * c + _rot_half(t) * s
                buf[hp] = t
            for di, d in enumerate(DILATIONS):
                _to_res(buf, outs[3 * di + which], d, BF16)

    return pl.pallas_call(
        body, name=name, grid=(T // tm,),
        in_specs=[pl.BlockSpec((tm, MIX_HALF), lambda i: (i, 2)),
                  pl.BlockSpec((tm, MIX_HALF), lambda i: (i, 3)),
                  pl.BlockSpec((tm, MIX_HALF), lambda i: (i, 4)),
                  pl.BlockSpec((tm, HEAD_LANES), lambda i: (i, 0)),
                  pl.BlockSpec((tm, HEAD_LANES), lambda i: (i, 0))],
        out_specs=[_res_spec(d, tm, MIX_HALF) for d in DILATIONS for _ in range(3)],
        out_shape=[_res_shape(d, T, MIX_HALF, BF16) for d in DILATIONS for _ in range(3)],
        scratch_shapes=[_slab_buf(tm, MIX_HALF)],
        compiler_params=_params(("parallel",)),
    )(proj, proj, proj, cos, sin)


def to_residues(x, col, name):
    T = x.shape[0]
    tm = LAYOUT_ROWS

    def body(x_ref, *rest):
        outs, buf = rest[:-1], rest[-1]
        for hp in range(MIX_HALF // HEAD_LANES):
            buf[hp] = x_ref[:, _lanes(hp)].astype(F32)
        for o_ref, d in zip(outs, DILATIONS):
            _to_res(buf, o_ref, d, BF16)

    return pl.pallas_call(
        body, name=name, grid=(T // tm,),
        in_specs=[pl.BlockSpec((tm, MIX_HALF), lambda i: (i, col))],
        out_specs=[_res_spec(d, tm, MIX_HALF) for d in DILATIONS],
        out_shape=[_res_shape(d, T, MIX_HALF, BF16) for d in DILATIONS],
        scratch_shapes=[_slab_buf(tm, MIX_HALF)],
        compiler_params=_params(("parallel",)),
    )(x)


def rope_bwd(dqs, dks, dvs, cos, sin, name):
    T = dqs[0].shape[0] * dqs[0].shape[1]
    tm = LAYOUT_ROWS
    scale = 1.0 / math.sqrt(ATT_DH)
    npat = len(dqs)

    def body(*refs):
        groups = refs[:npat], refs[npat:2 * npat], refs[2 * npat:3 * npat]
        cos_ref, sin_ref, o_ref, buf, acc = refs[3 * npat:]
        c = cos_ref[...]
        s = sin_ref[...]
        for which, g_refs in enumerate(groups):
            _from_res(g_refs[0], acc, DILATIONS[0])
            for g_ref, d in zip(g_refs[1:], DILATIONS[1:]):
                _from_res(g_ref, buf, d)
                acc[...] += buf[...]
            for hp in range(MIX_HALF // HEAD_LANES):
                g = acc[hp]
                if which == 0:
                    g = scale * g
                if which < 2:
                    g = g * c - _rot_half(g * s)
                o_ref[:, which * MIX_HALF + hp * HEAD_LANES:which * MIX_HALF + (hp + 1) * HEAD_LANES] = g.astype(BF16)

    return pl.pallas_call(
        body, name=name, grid=(T // tm,),
        in_specs=([_res_spec(d, tm, MIX_HALF) for _ in range(3) for d in DILATIONS]
                  + [pl.BlockSpec((tm, HEAD_LANES), lambda i: (i, 0))] * 2),
        out_specs=pl.BlockSpec((tm, 3 * MIX_HALF), lambda i: (i, 0)),
        out_shape=jax.ShapeDtypeStruct((T, 3 * MIX_HALF), BF16),
        scratch_shapes=[_slab_buf(tm, MIX_HALF), _slab_buf(tm, MIX_HALF)],
        compiler_params=_params(("parallel",)),
    )(*dqs, *dks, *dvs, cos, sin)


def _band_masks(n):
    r = lax.broadcasted_iota(jnp.int32, (2 * ATT_BLOCK, ATT_BLOCK), 0)
    c = lax.broadcasted_iota(jnp.int32, (2 * ATT_BLOCK, ATT_BLOCK), 1)
    qi = r % ATT_BLOCK
    head = (c < ATT_DH) == (r < ATT_BLOCK)
    return (c >= qi) & (n > 0), c <= qi, head, c[:ATT_BLOCK] < ATT_DH


def _stack_heads(x, head):
    x2 = jnp.concatenate([x, x], axis=0)
    return jnp.where(head, x2, jnp.zeros_like(x2))


def _blocks_per_step(nb):
    return next(n for n in (4, 2, 1) if nb % n == 0)


def attn_fwd(q, k, v, name):
    d, L, W = q.shape
    per_seq = L // ATT_BLOCK
    nb = d * per_seq
    nsub = _blocks_per_step(nb)
    q, k, v = (t.reshape(1, d * L, W) for t in (q, k, v))

    def body(q_ref, kp_ref, kc_ref, vp_ref, vc_ref, o_ref, lse_ref):
        step = pl.program_id(1)
        for u in range(nsub):
            rows = slice(u * ATT_BLOCK, (u + 1) * ATT_BLOCK)
            before = slice((u - 1) * ATT_BLOCK, u * ATT_BLOCK)
            mask_p, mask_c, head, head0 = _band_masks((nsub * step + u) % per_seq)
            for hp in range(W // HEAD_LANES):
                sl = slice(hp * HEAD_LANES, (hp + 1) * HEAD_LANES)
                kp, vp = (kp_ref[0, :, sl], vp_ref[0, :, sl]) if u == 0 else (kc_ref[0, before, sl], vc_ref[0, before, sl])
                kc, vc = kc_ref[0, rows, sl], vc_ref[0, rows, sl]
                qs = _stack_heads(q_ref[0, rows, sl], head)
                sp = jnp.where(mask_p, _nt(qs, kp), NEG)
                sc = jnp.where(mask_c, _nt(qs, kc), NEG)
                m = jnp.maximum(jnp.max(sp, axis=1, keepdims=True), jnp.max(sc, axis=1, keepdims=True))
                pp = jnp.exp(sp - m)
                pc = jnp.exp(sc - m)
                den = jnp.sum(pp, axis=1, keepdims=True) + jnp.sum(pc, axis=1, keepdims=True)
                o = (_nn(pp.astype(BF16), vp) + _nn(pc.astype(BF16), vc)) / den
                lse = m + jnp.log(den)
                o_ref[0, rows, sl] = jnp.where(head0, o[:ATT_BLOCK], o[ATT_BLOCK:]).astype(BF16)
                lse_ref[0, rows, sl] = jnp.where(head0, lse[:ATT_BLOCK], lse[ATT_BLOCK:])

    cur = pl.BlockSpec((1, nsub * ATT_BLOCK, W), lambda r, n: (r, n, 0))
    prev = pl.BlockSpec((1, ATT_BLOCK, W), lambda r, n: (r, jnp.maximum(nsub * n - 1, 0), 0))
    o, lse = pl.pallas_call(
        body, name=name, grid=(1, nb // nsub),
        in_specs=[cur, prev, cur, prev, cur],
        out_specs=[cur, cur],
        out_shape=[jax.ShapeDtypeStruct((1, d * L, W), BF16), jax.ShapeDtypeStruct((1, d * L, W), F32)],
        compiler_params=_params(("parallel", "parallel")),
    )(q, k, k, v, v)
    return o.reshape(d, L, W), lse.reshape(d, L, W)


def attn_combine(os_, lses, name):
    T = os_[0].shape[0] * os_[0].shape[1]
    W = os_[0].shape[2]
    tm = LAYOUT_ROWS
    npat = len(os_)

    def body(*refs):
        o_refs, l_refs = refs[:npat], refs[npat:2 * npat]
        out_ref = refs[2 * npat]
        ores, lres = refs[2 * npat + 1:3 * npat + 1], refs[3 * npat + 1:4 * npat + 1]
        bufs = refs[4 * npat + 1:]
        lbufs, obufs, out_buf, lse_buf = bufs[:npat], bufs[npat:2 * npat], bufs[2 * npat], bufs[2 * npat + 1]
        for p, d in enumerate(DILATIONS):
            _from_res(l_refs[p], lbufs[p], d)
            _from_res(o_refs[p], obufs[p], d)
        for hp in range(W // HEAD_LANES):
            ls = [b[hp] for b in lbufs]
            m = ls[0]
            for l in ls[1:]:
                m = jnp.maximum(m, l)
            es = [jnp.exp(l - m) for l in ls]
            z = es[0]
            for e in es[1:]:
                z = z + e
            acc = es[0] * obufs[0][hp]
            for p in range(1, npat):
                acc = acc + es[p] * obufs[p][hp]
            out = acc / z
            out_ref[:, _lanes(hp)] = out.astype(BF16)
            out_buf[hp] = out
            lse_buf[hp] = m + jnp.log(z)
        for p, d in enumerate(DILATIONS):
            _to_res(out_buf, ores[p], d, BF16)
            _to_res(lse_buf, lres[p], d, F32)

    return pl.pallas_call(
        body, name=name, grid=(T // tm,),
        in_specs=[_res_spec(d, tm, W) for _ in range(2) for d in DILATIONS],
        out_specs=([pl.BlockSpec((tm, W), lambda i: (i, 0))] + [_res_spec(d, tm, W) for _ in range(2) for d in DILATIONS]),
        out_shape=([jax.ShapeDtypeStruct((T, W), BF16)] + [_res_shape(d, T, W, BF16) for d in DILATIONS]
                   + [_res_shape(d, T, W, F32) for d in DILATIONS]),
        scratch_shapes=[_slab_buf(tm, W)] * (2 * npat + 2),
        compiler_params=_params(("parallel",)),
    )(*os_, *lses)


def attn_bwd(q, k, v, do, o, lse, name):
    d, L, W = q.shape
    per_seq = L // ATT_BLOCK
    nb = d * per_seq
    nsub = _blocks_per_step(nb)
    nst = nb // nsub
    last = slice((nsub - 1) * ATT_BLOCK, nsub * ATT_BLOCK)
    q, k, v, do, o, lse = (t.reshape(1, d * L, W) for t in (q, k, v, do, o, lse))

    def body(q_ref, kp_ref, kc_ref, vp_ref, vc_ref, do_ref, o_ref, lse_ref, dq_ref, dk_ref, dv_ref, kkeep, vkeep):
        step = pl.program_id(1)

        @pl.when(step == 0)
        def _():
            kkeep[...] = jnp.zeros_like(kkeep)
            vkeep[...] = jnp.zeros_like(vkeep)

        @pl.when(step < nst)
        def _():
            for hp in range(W // HEAD_LANES):
                sl = slice(hp * HEAD_LANES, (hp + 1) * HEAD_LANES)
                shares = []
                for u in range(nsub):
                    rows = slice(u * ATT_BLOCK, (u + 1) * ATT_BLOCK)
                    before = slice((u - 1) * ATT_BLOCK, u * ATT_BLOCK)
                    mask_p, mask_c, head, head0 = _band_masks((nsub * step + u) % per_seq)
                    kp, vp = (kp_ref[0, :, sl], vp_ref[0, :, sl]) if u == 0 else (kc_ref[0, before, sl], vc_ref[0, before, sl])
                    kc, vc = kc_ref[0, rows, sl], vc_ref[0, rows, sl]
                    dout = do_ref[0, rows, sl]
                    qs = _stack_heads(q_ref[0, rows, sl], head)
                    dos = _stack_heads(dout, head)
                    lse_v = lse_ref[0, rows, sl]
                    lse_c = jnp.max(jnp.where(head, jnp.concatenate([lse_v, lse_v], axis=0), NEG), axis=1, keepdims=True)
                    delta = jnp.sum(_stack_heads(dout.astype(F32) * o_ref[0, rows, sl].astype(F32), head), axis=1,
                                    keepdims=True)
                    pp = jnp.exp(jnp.where(mask_p, _nt(qs, kp), NEG) - lse_c)
                    pc = jnp.exp(jnp.where(mask_c, _nt(qs, kc), NEG) - lse_c)
                    dsp = (pp * (_nt(dos, vp) - delta)).astype(BF16)
                    dsc = (pc * (_nt(dos, vc) - delta)).astype(BF16)
                    dq2 = _nn(dsp, kp) + _nn(dsc, kc)
                    dq_ref[0, rows, sl] = jnp.where(head0, dq2[:ATT_BLOCK], dq2[ATT_BLOCK:]).astype(BF16)
                    shares.append((_tn(dsp, qs), _tn(pp.astype(BF16), dos), _tn(dsc, qs), _tn(pc.astype(BF16), dos)))
                dk_ref[0, last, sl] = (kkeep[last, sl] + shares[0][0]).astype(BF16)
                dv_ref[0, last, sl] = (vkeep[last, sl] + shares[0][1]).astype(BF16)
                for j in range(nsub - 1):
                    blk = slice(j * ATT_BLOCK, (j + 1) * ATT_BLOCK)
                    dk_ref[0, blk, sl] = kkeep[blk, sl].astype(BF16)
                    dv_ref[0, blk, sl] = vkeep[blk, sl].astype(BF16)
                    kkeep[blk, sl] = shares[j][2] + shares[j + 1][0]
                    vkeep[blk, sl] = shares[j][3] + shares[j + 1][1]
                kkeep[last, sl] = shares[-1][2]
                vkeep[last, sl] = shares[-1][3]

        @pl.when(step == nst)
        def _():
            dk_ref[0] = kkeep[...].astype(BF16)
            dv_ref[0] = vkeep[...].astype(BF16)

    rows_per_step = nsub * ATT_BLOCK
    cur = pl.BlockSpec((1, rows_per_step, W), lambda r, n: (r, jnp.minimum(n, nst - 1), 0))
    lag = pl.BlockSpec((1, rows_per_step, W), lambda r, n: (r, jnp.clip(n - 1, 0, nst - 1), 0))
    prev = pl.BlockSpec((1, ATT_BLOCK, W), lambda r, n: (r, jnp.clip(nsub * n - 1, 0, nb - 1), 0))
    out = jax.ShapeDtypeStruct((1, d * L, W), BF16)
    outs = pl.pallas_call(
        body, name=name, grid=(1, nst + 1),
        in_specs=[cur, prev, cur, prev, cur, cur, cur, cur],
        out_specs=[cur, lag, lag], out_shape=[out, out, out],
        scratch_shapes=[pltpu.VMEM((rows_per_step, W), F32), pltpu.VMEM((rows_per_step, W), F32)],
        compiler_params=_params(("parallel", "arbitrary")),
    )(q, k, k, v, v, do, o, lse)
    return [t.reshape(d, L, W) for t in outs]


def final_loss_bwd(h, gf, tgt, gate, name):
    T, D = h.shape
    tm = _tile(T, 512, 16)
    g_in, g_out, g_shape, g_ops = _gate_specs(gate, tm, D)

    def body(h_ref, g_ref, t_ref, o_ref, modg_ref, dh_ref, red_ref, do_ref, redg_ref):
        x = h_ref[...]
        r = lax.rsqrt(jnp.mean(x * x, axis=-1, keepdims=True) + EPS)
        n = x * r
        g = g_ref[...]
        err = n * g - t_ref[...]
        dy = err * (1.0 / D)

        @pl.when(pl.program_id(0) == 0)
        def _():
            red_ref[...] = jnp.zeros_like(red_ref)

        red_ref[0:1, :] += jnp.sum(dy * n, axis=0, keepdims=True)
        red_ref[1:2, :] += jnp.zeros((1, D), F32) + (0.5 / D) * jnp.sum(err * err, keepdims=True)
        dn = dy * g
        dh = r * (dn - n * jnp.mean(dn * n, axis=-1, keepdims=True))
        dh_ref[...] = dh
        _gate_emit(dh, gate, o_ref, modg_ref, do_ref, redg_ref)

    return pl.pallas_call(
        body, name=name, grid=(T // tm,),
        in_specs=[pl.BlockSpec((tm, D), lambda i: (i, 0)),
                  pl.BlockSpec((1, D), lambda i: (0, 0)),
                  pl.BlockSpec((tm, D), lambda i: (i, 0))] + g_in,
        out_specs=[pl.BlockSpec((tm, D), lambda i: (i, 0)), pl.BlockSpec((8, D), lambda i: (0, 0))] + g_out,
        out_shape=[jax.ShapeDtypeStruct((T, D), F32), jax.ShapeDtypeStruct((8, D), F32)] + g_shape,
        compiler_params=_params(("arbitrary",)),
    )(h, gf, tgt, *g_ops)


def ada_fwd(c_all, ada_w, ada_b, name):
    nl, D, N = ada_w.shape

    def body(c_ref, w_ref, b_ref, o_ref):
        c = c_ref[...]
        o_ref[0] = _nn(c * _sigmoid(c), w_ref[0]) + b_ref[0]

    return pl.pallas_call(
        body, name=name, grid=(nl,),
        in_specs=[pl.BlockSpec((N_DEV, D), lambda l: (0, 0)),
                  pl.BlockSpec((1, D, N), lambda l: (l, 0, 0)),
                  pl.BlockSpec((1, 1, N), lambda l: (l, 0, 0))],
        out_specs=pl.BlockSpec((1, N_DEV, N), lambda l: (l, 0, 0)),
        out_shape=jax.ShapeDtypeStruct((nl, N_DEV, N), F32),
        compiler_params=_params(("parallel",)),
    )(c_all, ada_w, ada_b)


def ada_bwd(c_allT, dmod, name):
    nl, _, N = dmod.shape
    D = c_allT.shape[0]

    def body(c_ref, g_ref, o_ref):
        c = c_ref[...]
        ca = c * _sigmoid(c)
        acc = ca[:, 0:1] * g_ref[0, 0:1, :]
        for b in range(1, N_DEV):
            acc = acc + ca[:, b:b + 1] * g_ref[0, b:b + 1, :]
        o_ref[0] = acc

    return pl.pallas_call(
        body, name=name, grid=(nl,),
        in_specs=[pl.BlockSpec((D, N_DEV), lambda l: (0, 0)),
                  pl.BlockSpec((1, N_DEV, N), lambda l: (l, 0, 0))],
        out_specs=pl.BlockSpec((1, D, N), lambda l: (l, 0, 0)),
        out_shape=jax.ShapeDtypeStruct((nl, D, N), F32),
        compiler_params=_params(("parallel",)),
    )(c_allT, dmod)


def adamw(w, g, m, v, name):
    R, C = w.shape
    tr = _tile(R, max(8, (1 << 19) // C // 8 * 8), 8)
    c1 = 1.0 - ADAM_B1 ** ADAM_STEP
    c2 = 1.0 - ADAM_B2 ** ADAM_STEP

    def body(w_ref, g_ref, m_ref, v_ref, d_ref, mo_ref, vo_ref):
        gv = g_ref[...]
        mn = ADAM_B1 * m_ref[...] + (1.0 - ADAM_B1) * gv
        vn = ADAM_B2 * v_ref[...] + (1.0 - ADAM_B2) * (gv * gv)
        mo_ref[...] = mn
        vo_ref[...] = vn
        d_ref[...] = -ADAM_LR * ((mn / c1) / (jnp.sqrt(vn / c2) + ADAM_EPS) + ADAM_WD * w_ref[...])

    blk = pl.BlockSpec((tr, C), lambda i: (i, 0))
    out = jax.ShapeDtypeStruct((R, C), F32)
    return pl.pallas_call(
        body, name=name, grid=(R // tr,),
        in_specs=[blk] * 4, out_specs=[blk] * 3, out_shape=[out] * 3,
        compiler_params=_params(("parallel",)),
    )(w, g, m, v)


def adamw_layer(w, g, m, v, l, prev, name):
    NLw, R, C = w.shape
    tr = _tile(R, max(8, (1 << 19) // C // 8 * 8), 8)
    nrb = R // tr
    c1 = 1.0 - ADAM_B1 ** ADAM_STEP
    c2 = 1.0 - ADAM_B2 ** ADAM_STEP
    w, m, v = (t.reshape(NLw * R, C) for t in (w, m, v))

    def body(w_ref, g_ref, m_ref, v_ref, *rest):
        go_ref, d_ref, mo_ref, vo_ref = rest[-4:]
        gv = g_ref[...]
        mn = ADAM_B1 * m_ref[...] + (1.0 - ADAM_B1) * gv
        vn = ADAM_B2 * v_ref[...] + (1.0 - ADAM_B2) * (gv * gv)
        go_ref[...] = gv
        mo_ref[...] = mn
        vo_ref[...] = vn
        d_ref[...] = -ADAM_LR * ((mn / c1) / (jnp.sqrt(vn / c2) + ADAM_EPS) + ADAM_WD * w_ref[...])

    lay = pl.BlockSpec((tr, C), lambda i: (l * nrb + i, 0))
    out = jax.ShapeDtypeStruct((NLw * R, C), F32)
    n_prev = 0 if prev is None else 4
    return pl.pallas_call(
        body, name=name, grid=(nrb,),
        in_specs=[lay, pl.BlockSpec((tr, C), lambda i: (i, 0)), lay, lay] + [pl.BlockSpec(memory_space=pl.ANY)] * n_prev,
        out_specs=[lay] * 4, out_shape=[out] * 4,
        input_output_aliases={4 + i: i for i in range(n_prev)},
        compiler_params=_params(("parallel",)),
    )(w, g, m, v, *(prev or ()))


def sum_slots(x, name):
    S, R, C = x.shape
    tr = _tile(R, 128, 8)

    def body(x_ref, o_ref):
        acc = x_ref[0]
        for s in range(1, S):
            acc = acc + x_ref[s]
        o_ref[...] = acc

    return pl.pallas_call(
        body, name=name, grid=(R // tr,),
        in_specs=[pl.BlockSpec((S, tr, C), lambda i: (0, i, 0))],
        out_specs=pl.BlockSpec((tr, C), lambda i: (i, 0)),
        out_shape=jax.ShapeDtypeStruct((R, C), F32),
        compiler_params=_params(("parallel",)),
    )(x)


def sum_halves(g, lands, c_idx, name):
    n, ns, _, rh, D = g.shape

    def body(c_ref, g_ref, l_ref, o_ref):
        for j in range(ns):
            o_ref[0, j] = (g_ref[0, j, 0].astype(F32) + l_ref[0, j].astype(F32)).astype(BF16)

    return pl.pallas_call(
        body, name=name,
        grid_spec=pltpu.PrefetchScalarGridSpec(
            num_scalar_prefetch=1, grid=(n,),
            in_specs=[pl.BlockSpec((1, ns, 1, rh, D), lambda i, c: (i, 0, c[0], 0, 0)),
                      pl.BlockSpec((1, ns, rh, D), lambda i, c: (i, 0, 0, 0))],
            out_specs=pl.BlockSpec((1, ns, rh, D), lambda i, c: (i, 0, 0, 0))),
        out_shape=jax.ShapeDtypeStruct((n, ns, rh, D), BF16),
        compiler_params=_params(("parallel",)),
    )(c_idx, g, lands)


def sum_chips(p, lands, place, name):
    n, ns, rh, D = p.shape

    def body(c_ref, p_ref, l_ref, o_ref):
        acc = p_ref[0, 0].astype(F32)
        for j in range(N_CHIP - 1):
            acc = acc + l_ref[j, 0].astype(F32)
        o_ref[0, 0] = acc

    return pl.pallas_call(
        body, name=name,
        grid_spec=pltpu.PrefetchScalarGridSpec(
            num_scalar_prefetch=1, grid=(n,),
            in_specs=[pl.BlockSpec((1, 1, rh, D), lambda i, c: (i, c[0], 0, 0)),
                      pl.BlockSpec((N_CHIP - 1, 1, rh, D), lambda i, c: (0, i, 0, 0))],
            out_specs=pl.BlockSpec((1, 1, rh, D), lambda i, c: (i, c[1], 0, 0))),
        out_shape=jax.ShapeDtypeStruct((n, 2, rh, D), F32),
        compiler_params=_params(("parallel",)),
    )(place, p, lands)


def _my_place():
    return lax.axis_index("x"), lax.axis_index("y"), lax.axis_index("c")


def _other_chips(mx, my):
    return [(1 - mx, my), (mx, 1 - my), (1 - mx, 1 - my)]


def gather_small(x, after, name):
    def body(x_ref, after_ref, out_ref, sum_ref, send_sems, recv_sems):
        mx, my, mc = _my_place()
        me = 4 * mx + 2 * my + mc
        out_ref[me] = x_ref[...]
        sends = []
        for k in range(1, N_DEV):
            kx, ky, kc = (k >> 2) & 1, (k >> 1) & 1, k & 1
            peer = (1 - mx if kx else mx, 1 - my if ky else my, 1 - mc if kc else mc)
            cp = pltpu.make_async_remote_copy(
                src_ref=x_ref, dst_ref=out_ref.at[me], send_sem=send_sems.at[k - 1], recv_sem=recv_sems.at[k - 1],
                device_id=peer, device_id_type=MESH)
            cp.start()
            sends.append((cp, 4 * peer[0] + 2 * peer[1] + peer[2], peer))
        for k, (cp, peer_slot, peer) in enumerate(sends):
            pltpu.make_async_remote_copy(
                src_ref=x_ref, dst_ref=out_ref.at[peer_slot], send_sem=send_sems.at[k], recv_sem=recv_sems.at[k],
                device_id=peer, device_id_type=MESH).wait_recv()
        for cp, _, _ in sends:
            cp.wait_send()
        acc = out_ref[0]
        for s in range(1, N_DEV):
            acc = acc + out_ref[s]
        sum_ref[...] = acc

    vmem = pl.BlockSpec(memory_space=pltpu.VMEM)
    return pl.pallas_call(
        body, name=name,
        in_specs=[vmem, pl.BlockSpec(memory_space=pl.ANY)], out_specs=[vmem, vmem],
        out_shape=[jax.ShapeDtypeStruct((N_DEV,) + x.shape, x.dtype), jax.ShapeDtypeStruct(x.shape, x.dtype)],
        scratch_shapes=[pltpu.SemaphoreType.DMA((N_DEV - 1,)), pltpu.SemaphoreType.DMA((N_DEV - 1,))],
        compiler_params=pltpu.CompilerParams(vmem_limit_bytes=VMEM_LIMIT),
    )(x, after)


_HBM =pl.BlockSpec(memory_space=pltpu.HBM)
_SEM = pl.BlockSpec(memory_space=pltpu.SEMAPHORE)
_DATAFLOW = pltpu.SideEffectType.DATAFLOW_SIDE_EFFECTING


def _gather_copies(shard, land, send, recv, base):
    mx, my, mc = _my_place()
    ci = 2 * mx + my
    peers = [((cx, cy, mc), 2 * cx + cy) for cx, cy in _other_chips(mx, my)] + [((mx, my, 1 - mc), ci)]
    out = []
    for q, (dev, src_slot) in enumerate(peers):
        out.append((
            pltpu.make_async_remote_copy(src_ref=shard, dst_ref=land.at[:, ci], send_sem=send.at[base + q],
                                         recv_sem=recv.at[base + q], device_id=dev, device_id_type=MESH),
            pltpu.make_async_remote_copy(src_ref=shard, dst_ref=land.at[:, src_slot], send_sem=send.at[base + q],
                                         recv_sem=recv.at[base + q], device_id=dev, device_id_type=MESH)))
    return out


def gather_start(groups, after, name):
    items = [s for g in groups for s in g]
    ni, ng = len(items), len(groups)

    def body(*refs):
        shards, lands = refs[:ni], refs[ni:2 * ni]
        sems = refs[2 * ni + 1:2 * ni + 1 + 2 * ng]
        token = refs[-1]
        i = 0
        for g, grp in enumerate(groups):
            for p in range(len(grp)):
                for start_cp, _ in _gather_copies(shards[i], lands[i], sems[2 * g], sems[2 * g + 1], 4 * p):
                    start_cp.start()
                i += 1
        token[...] = jnp.zeros_like(token)

    sem_shapes = []
    for grp in groups:
        sem_shapes += [pltpu.SemaphoreType.DMA((4 * len(grp),))] * 2
    land_shapes = [(s.shape[0], N_CHIP) + s.shape[1:] for s in items]
    outs = pl.pallas_call(
        body, name=name,
        in_specs=[_HBM] * (2 * ni) + [pl.BlockSpec(memory_space=pl.ANY)],
        out_specs=[_SEM] * (2 * ng) + [_HBM] * (2 * ni) + [pl.BlockSpec(memory_space=pltpu.VMEM)],
        out_shape=(sem_shapes + [pltpu.HBM(s.shape, s.dtype) for s in items]
                   + [pltpu.HBM(ls, s.dtype) for ls, s in zip(land_shapes, items)]
                   + [jax.ShapeDtypeStruct((8, 128), F32)]),
        input_output_aliases={i: 2 * ng + i for i in range(2 * ni)},
        compiler_params=pltpu.CompilerParams(has_side_effects=_DATAFLOW),
    )(*[pltpu.with_memory_space_constraint(s, pltpu.HBM) for s in items],
      *[pltpu.with_memory_space_constraint(lax.empty(ls, s.dtype), pltpu.HBM) for ls, s in zip(land_shapes, items)],
      after)
    sems, thru, token = outs[:2 * ng], outs[2 * ng:2 * ng + 2 * ni], outs[-1]
    handles, i = [], 0
    for g, grp in enumerate(groups):
        n = len(grp)
        handles.append((sems[2 * g], sems[2 * g + 1], thru[i:i + n], thru[ni + i:ni + i + n]))
        i += n
    return handles, token


def gather_wait(handle, after, name):
    send, recv, shards, lands = handle
    n = len(shards)

    def body(*refs):
        shard_refs, land_refs = refs[:n], refs[n:2 * n]
        send_ref, recv_ref = refs[2 * n], refs[2 * n + 1]
        for p in range(n):
            for start_cp, recv_cp in _gather_copies(shard_refs[p], land_refs[p], send_ref, recv_ref, 4 * p):
                start_cp.wait_send()
                recv_cp.wait_recv()

    outs = pl.pallas_call(
        body, name=name,
        in_specs=[_HBM] * (2 * n) + [_SEM, _SEM, pl.BlockSpec(memory_space=pl.ANY)],
        out_specs=[_HBM] * (2 * n),
        out_shape=[pltpu.HBM(s.shape, s.dtype) for s in shards] + [pltpu.HBM(l.shape, l.dtype) for l in lands],
        input_output_aliases={i: i for i in range(2 * n)},
        compiler_params=pltpu.CompilerParams(has_side_effects=_DATAFLOW),
    )(*shards, *lands, send, recv, after)
    return outs[n:]


def _first_copies(shard, land, send, recv):
    mx, my, mc = _my_place()
    ci = 2 * mx + my
    out = []
    for q, (cx, cy) in enumerate(_other_chips(mx, my)):
        dev = (cx, cy, mc)
        out.append(tuple(pltpu.make_async_remote_copy(
            src_ref=shard.at[:, mc], dst_ref=land.at[:, slot, mc], send_sem=send.at[q], recv_sem=recv.at[q],
            device_id=dev, device_id_type=MESH) for slot in (ci, 2 * cx + cy)))
    sib = pltpu.make_async_remote_copy(src_ref=shard, dst_ref=land.at[:, ci], send_sem=send.at[3], recv_sem=recv.at[3],
                                       device_id=(mx, my, 1 - mc), device_id_type=MESH)
    return out + [(sib, sib)]


def _forward_copies(land, send, recv):
    mx, my, mc = _my_place()
    out = []
    for q, (cx, cy) in enumerate(_other_chips(mx, my)):
        out.append(tuple(pltpu.make_async_remote_copy(
            src_ref=land.at[:, 2 * cx + cy, hc], dst_ref=land.at[:, 2 * cx + cy, hc], send_sem=send.at[q],
            recv_sem=recv.at[q], device_id=(mx, my, 1 - mc), device_id_type=MESH) for hc in (mc, 1 - mc)))
    return out


def first_start(shard, after, name):
    def body(shard_ref, land_ref, after_ref, send, recv, shard_thru, land_thru, token):
        for mine, _ in _first_copies(shard_ref, land_ref, send, recv):
            mine.start()
        token[...] = jnp.zeros_like(token)

    land_shape = (shard.shape[0], N_CHIP) + shard.shape[1:]
    outs = pl.pallas_call(
        body, name=name,
        in_specs=[_HBM, _HBM, pl.BlockSpec(memory_space=pl.ANY)],
        out_specs=[_SEM, _SEM, _HBM, _HBM, pl.BlockSpec(memory_space=pltpu.VMEM)],
        out_shape=[pltpu.SemaphoreType.DMA((4,))] * 2 + [pltpu.HBM(shard.shape, shard.dtype),
                                                         pltpu.HBM(land_shape, shard.dtype),
                                                         jax.ShapeDtypeStruct((8, 128), F32)],
        input_output_aliases={0: 2, 1: 3},
        compiler_params=pltpu.CompilerParams(has_side_effects=_DATAFLOW),
    )(pltpu.with_memory_space_constraint(shard, pltpu.HBM),
      pltpu.with_memory_space_constraint(lax.empty(land_shape, shard.dtype), pltpu.HBM), after)
    return outs[:4], outs[4]


def first_forward(handle, after, name):
    send, recv, shard, land = handle

    def body(shard_ref, land_ref, send_ref, recv_ref, after_ref, send2, recv2, shard_thru, land_thru):
        firsts = _first_copies(shard_ref, land_ref, send_ref, recv_ref)
        forwards = _forward_copies(land_ref, send2, recv2)
        for q in range(3):
            firsts[q][1].wait_recv()
            forwards[q][0].start()
        firsts[3][1].wait_recv()
        for mine, _ in firsts:
            mine.wait_send()

    outs = pl.pallas_call(
        body, name=name,
        in_specs=[_HBM, _HBM, _SEM, _SEM, pl.BlockSpec(memory_space=pl.ANY)],
        out_specs=[_SEM, _SEM, _HBM, _HBM],
        out_shape=[pltpu.SemaphoreType.DMA((3,))] * 2 + [pltpu.HBM(shard.shape, shard.dtype),
                                                         pltpu.HBM(land.shape, land.dtype)],
        input_output_aliases={0: 2, 1: 3},
        compiler_params=pltpu.CompilerParams(has_side_effects=_DATAFLOW),
    )(shard, land, send, recv, after)
    return outs[0], outs[1], outs[3]


def first_wait(handle, after, name):
    send, recv, land = handle

    def body(land_ref, send_ref, recv_ref, after_ref, land_out):
        for mine, theirs in _forward_copies(land_ref, send_ref, recv_ref):
            mine.wait_send()
            theirs.wait_recv()

    return pl.pallas_call(
        body, name=name,
        in_specs=[_HBM, _SEM, _SEM, pl.BlockSpec(memory_space=pl.ANY)],
        out_specs=[_HBM],
        out_shape=[pltpu.HBM(land.shape, land.dtype)],
        input_output_aliases={0: 0},
        compiler_params=pltpu.CompilerParams(has_side_effects=_DATAFLOW),
    )(land, send, recv, after)[0]


def _sibling_copies(gs, lands, send, recv):
    mx, my, mc = _my_place()
    return [pltpu.make_async_remote_copy(
        src_ref=gs[k].at[:, :, 1 - mc], dst_ref=lands[k], send_sem=send.at[k], recv_sem=recv.at[k],
        device_id=(mx, my, 1 - mc), device_id_type=MESH) for k in range(len(gs))]


def sibling_start(gs, after, name):
    K = len(gs)

    def body(*refs):
        ins, lands = refs[:K], refs[K:2 * K]
        send, recv = refs[2 * K + 1], refs[2 * K + 2]
        for cp in _sibling_copies(ins, lands, send, recv):
            cp.start()
        refs[-1][...] = jnp.zeros_like(refs[-1])

    land_shapes = [g.shape[:2] + g.shape[3:] for g in gs]
    outs = pl.pallas_call(
        body, name=name,
        in_specs=[_HBM] * (2 * K) + [pl.BlockSpec(memory_space=pl.ANY)],
        out_specs=[_SEM, _SEM] + [_HBM] * (2 * K) + [pl.BlockSpec(memory_space=pltpu.VMEM)],
        out_shape=([pltpu.SemaphoreType.DMA((K,))] * 2 + [pltpu.HBM(g.shape, g.dtype) for g in gs]
                   + [pltpu.HBM(ls, g.dtype) for ls, g in zip(land_shapes, gs)] + [jax.ShapeDtypeStruct((8, 128), F32)]),
        input_output_aliases={i: 2 + i for i in range(2 * K)},
        compiler_params=pltpu.CompilerParams(has_side_effects=_DATAFLOW),
    )(*[pltpu.with_memory_space_constraint(g, pltpu.HBM) for g in gs],
      *[pltpu.with_memory_space_constraint(lax.empty(ls, g.dtype), pltpu.HBM) for ls, g in zip(land_shapes, gs)],
      after)
    return (outs[0], outs[1], outs[2:2 + K], outs[2 + K:2 + 2 * K]), outs[-1]


def sibling_wait(handle, after, name):
    send, recv, gs, lands = handle
    K = len(gs)

    def body(*refs):
        ins, land_refs = refs[:K], refs[K:2 * K]
        for cp in _sibling_copies(ins, land_refs, refs[2 * K], refs[2 * K + 1]):
            cp.wait_send()
            cp.wait_recv()

    outs = pl.pallas_call(
        body, name=name,
        in_specs=[_HBM] * (2 * K) + [_SEM, _SEM, pl.BlockSpec(memory_space=pl.ANY)],
        out_specs=[_HBM] * (2 * K),
        out_shape=[pltpu.HBM(g.shape, g.dtype) for g in gs] + [pltpu.HBM(l.shape, l.dtype) for l in lands],
        input_output_aliases={i: i for i in range(2 * K)},
        compiler_params=pltpu.CompilerParams(has_side_effects=_DATAFLOW),
    )(*gs, *lands, send, recv, after)
    return outs[:K], outs[K:]


def _small_copies(x, land, send, recv):
    mx, my, mc = _my_place()
    me = 4 * mx + 2 * my + mc
    out = []
    for k in range(1, N_DEV):
        peer = (1 - mx if k & 4 else mx, 1 - my if k & 2 else my, 1 - mc if k & 1 else mc)
        slot = 4 * peer[0] + 2 * peer[1] + peer[2]
        out.append(tuple(pltpu.make_async_remote_copy(
            src_ref=x, dst_ref=land.at[s], send_sem=send.at[k - 1], recv_sem=recv.at[k - 1],
            device_id=peer, device_id_type=MESH) for s in (me, slot)))
    return out


def small_start(x, after, name):
    def body(x_ref, land_ref, after_ref, send, recv, x_thru, land_thru, token):
        for mine, _ in _small_copies(x_ref, land_ref, send, recv):
            mine.start()
        token[...] = jnp.zeros_like(token)

    land_shape = (N_DEV,) + x.shape
    outs = pl.pallas_call(
        body, name=name,
        in_specs=[_HBM, _HBM, pl.BlockSpec(memory_space=pl.ANY)],
        out_specs=[_SEM, _SEM, _HBM, _HBM, pl.BlockSpec(memory_space=pltpu.VMEM)],
        out_shape=[pltpu.SemaphoreType.DMA((N_DEV - 1,))] * 2 + [pltpu.HBM(x.shape, x.dtype), pltpu.HBM(land_shape, x.dtype),
                                                                 jax.ShapeDtypeStruct((8, 128), F32)],
        input_output_aliases={0: 2, 1: 3},
        compiler_params=pltpu.CompilerParams(has_side_effects=_DATAFLOW),
    )(pltpu.with_memory_space_constraint(x, pltpu.HBM),
      pltpu.with_memory_space_constraint(lax.empty(land_shape, x.dtype), pltpu.HBM), after)
    return outs[:4], outs[4]


def small_wait(handle, after, name):
    send, recv, x, land = handle

    def body(x_ref, land_ref, send_ref, recv_ref, after_ref, x_out, land_out):
        for mine, theirs in _small_copies(x_ref, land_ref, send_ref, recv_ref):
            mine.wait_send()
            theirs.wait_recv()

    return pl.pallas_call(
        body, name=name,
        in_specs=[_HBM, _HBM, _SEM, _SEM, pl.BlockSpec(memory_space=pl.ANY)],
        out_specs=[_HBM, _HBM],
        out_shape=[pltpu.HBM(x.shape, x.dtype), pltpu.HBM(land.shape, land.dtype)],
        input_output_aliases={0: 0, 1: 1},
        compiler_params=pltpu.CompilerParams(has_side_effects=_DATAFLOW),
    )(x, land, send, recv, after)


def _scatter_copies(ps, lands, send, recv):
    mx, my, mc = _my_place()
    cps = []
    for j, (cx, cy) in enumerate(_other_chips(mx, my)):
        for k in range(len(ps)):
            cps.append(pltpu.make_async_remote_copy(
                src_ref=ps[k].at[:, 2 * cx + cy], dst_ref=lands[k].at[j],
                send_sem=send.at[k * 3 + j], recv_sem=recv.at[k * 3 + j],
                device_id=(cx, cy, mc), device_id_type=MESH))
    return cps


def scatter_start(ps, after, name):
    K = len(ps)

    def body(*refs):
        ins, lands = refs[:K], refs[K:2 * K]
        send, recv = refs[2 * K + 1], refs[2 * K + 2]
        for cp in _scatter_copies(ins, lands, send, recv):
            cp.start()
        refs[-1][...] = jnp.zeros_like(refs[-1])

    land_shapes = [(N_CHIP - 1, p.shape[0]) + p.shape[2:] for p in ps]
    outs = pl.pallas_call(
        body, name=name,
        in_specs=[_HBM] * (2 * K) + [pl.BlockSpec(memory_space=pl.ANY)],
        out_specs=[_SEM, _SEM] + [_HBM] * (2 * K) + [pl.BlockSpec(memory_space=pltpu.VMEM)],
        out_shape=([pltpu.SemaphoreType.DMA((3 * K,))] * 2 + [pltpu.HBM(p.shape, p.dtype) for p in ps]
                   + [pltpu.HBM(ls, p.dtype) for ls, p in zip(land_shapes, ps)] + [jax.ShapeDtypeStruct((8, 128), F32)]),
        input_output_aliases={i: 2 + i for i in range(2 * K)},
        compiler_params=pltpu.CompilerParams(has_side_effects=_DATAFLOW),
    )(*[pltpu.with_memory_space_constraint(p, pltpu.HBM) for p in ps],
      *[pltpu.with_memory_space_constraint(lax.empty(ls, p.dtype), pltpu.HBM) for ls, p in zip(land_shapes, ps)],
      after)
    return (outs[0], outs[1], outs[2:2 + K], outs[2 + K:2 + 2 * K]), outs[-1]


def scatter_wait(handle, after, name):
    send, recv, ps, lands = handle
    K = len(ps)
    afters = list(after) if isinstance(after, (list, tuple)) else [after]

    def body(*refs):
        ins, land_refs = refs[:K], refs[K:2 * K]
        send_ref, recv_ref = refs[2 * K], refs[2 * K + 1]
        for cp in _scatter_copies(ins, land_refs, send_ref, recv_ref):
            cp.wait_send()
            cp.wait_recv()

    outs = pl.pallas_call(
        body, name=name,
        in_specs=[_HBM] * (2 * K) + [_SEM, _SEM] + [pl.BlockSpec(memory_space=pl.ANY)] * len(afters),
        out_specs=[_HBM] * (2 * K),
        out_shape=[pltpu.HBM(p.shape, p.dtype) for p in ps] + [pltpu.HBM(l.shape, l.dtype) for l in lands],
        input_output_aliases={i: i for i in range(2 * K)},
        compiler_params=pltpu.CompilerParams(has_side_effects=_DATAFLOW),
    )(*ps, *lands, send, recv, *afters)
    return outs[:K], outs[K:]


def sibling_complete(ss, name):
    K = len(ss)

    def body(*refs):
        ins, outs = refs[:K], refs[K:2 * K]
        send, recv = refs[2 * K:]
        mx, my, mc = _my_place()
        cps = []
        for k in range(K):
            cp = pltpu.make_async_remote_copy(
                src_ref=ins[k].at[:, mc], dst_ref=outs[k].at[:, mc], send_sem=send.at[k], recv_sem=recv.at[k],
                device_id=(mx, my, 1 - mc), device_id_type=MESH)
            cp.start()
            cps.append(cp)
        for k in range(K):
            pltpu.make_async_remote_copy(
                src_ref=ins[k].at[:, mc], dst_ref=outs[k].at[:, 1 - mc], send_sem=send.at[k], recv_sem=recv.at[k],
                device_id=(mx, my, 1 - mc), device_id_type=MESH).wait_recv()
        for cp in cps:
            cp.wait_send()

    hbm = pl.BlockSpec(memory_space=pl.ANY)
    return pl.pallas_call(
        body, name=name,
        in_specs=[hbm] * K, out_specs=[hbm] * K,
        out_shape=[jax.ShapeDtypeStruct(s.shape, s.dtype) for s in ss],
        scratch_shapes=[pltpu.SemaphoreType.DMA((K,)), pltpu.SemaphoreType.DMA((K,))],
        input_output_aliases={k: k for k in range(K)},
    )(*ss)


def _complete_copies(ss, send, recv):
    mx, my, mc = _my_place()
    return [tuple(pltpu.make_async_remote_copy(
        src_ref=ss[k].at[:, hc], dst_ref=ss[k].at[:, hc], send_sem=send.at[k], recv_sem=recv.at[k],
        device_id=(mx, my, 1 - mc), device_id_type=MESH) for hc in (mc, 1 - mc)) for k in range(len(ss))]


def complete_start(ss, after, name):
    K = len(ss)

    def body(*refs):
        for mine, _ in _complete_copies(refs[:K], refs[K + 1], refs[K + 2]):
            mine.start()

    outs = pl.pallas_call(
        body, name=name,
        in_specs=[_HBM] * K + [_ANY],
        out_specs=[_SEM, _SEM] + [_HBM] * K,
        out_shape=[pltpu.SemaphoreType.DMA((K,))] * 2 + [pltpu.HBM(s.shape, s.dtype) for s in ss],
        input_output_aliases={k: 2 + k for k in range(K)},
        compiler_params=pltpu.CompilerParams(has_side_effects=_DATAFLOW),
    )(*[pltpu.with_memory_space_constraint(s, pltpu.HBM) for s in ss], after)
    return outs[0], outs[1], outs[2:]


def complete_wait(handle, after, name):
    send, recv, ss = handle
    K = len(ss)
    afters = list(after) if isinstance(after, (list, tuple)) else [after]

    def body(*refs):
        for mine, theirs in _complete_copies(refs[:K], refs[K], refs[K + 1]):
            mine.wait_send()
            theirs.wait_recv()

    return pl.pallas_call(
        body, name=name,
        in_specs=[_HBM] * K + [_SEM, _SEM] + [_ANY] * len(afters),
        out_specs=[_HBM] * K,
        out_shape=[pltpu.HBM(s.shape, s.dtype) for s in ss],
        input_output_aliases={k: k for k in range(K)},
        compiler_params=pltpu.CompilerParams(has_side_effects=_DATAFLOW),
    )(*ss, send, recv, *afters)


def _rope_tables(T):
    inv = ROPE_THETA ** (-jnp.arange(0, ATT_DH, 2, dtype=F32) / ATT_DH)
    ang = jnp.arange(T, dtype=F32)[:, None] * inv[None, :]
    ang = jnp.concatenate([ang, ang, ang, ang], axis=-1)
    return jnp.cos(ang), jnp.sin(ang)


def _ffn_fwd(h, y, mod, i0, get_up, get_down, norm_next, tag):
    wgu = get_up(y)
    a, b, s = ffn_up(y, (wgu, (0,)), (wgu, (1,)), f"ffn_up_{tag}")
    wd = get_down(s)
    outs = resid_matmul([s], (wd, (0,)), h, mod, i0 + 2, 0.5, f"ffn_down_{tag}", norm_next)
    hn, o = outs[0], outs[1]
    return hn, (outs[2] if norm_next else None), (h, y, a, b, s, o), ((wgu, (0,)), (wgu, (1,)), (wd, (0,)))


def _ffn_bwd(dh, do, res, ng, i_n, mod, i0, wgT, wuT, wd, on_grads, next_gate, after, tag):
    h, y, a, b, s, o = res
    F = _wrows(wgT)
    da, db = ffn_bwd_mid(do, wd, a, b, f"ffn_bwd_mid_{tag}", after)
    gbuf = lax.empty((3, F, h.shape[1]), BF16)
    gbuf = matmul_tn(da, y, gbuf, 0, 0, f"dwg_{tag}")
    gbuf = matmul_tn(db, y, gbuf, 1, 0, f"dwu_{tag}")
    gbuf = matmul_tn(s, do, gbuf, 2, 0, f"dwd_{tag}")
    token, then = on_grads([gbuf])
    outs = dy_normbwd([(da, 0, wgT, 0, F), (db, 0, wuT, 0, F)], h, dh, ng, i_n, mod, i0 + 1,
                      f"ffn_bwd_dy_{tag}", next_gate, [token])
    return outs, then


def _mixer_fwd(h, y, mod, w_inT, w_out, sgu, cos, sin, norm_next, tag):
    lng, lnb, sw, swt, bcol = sgu
    proj = matmul_nt(y, w_inT, f"proj_{tag}")
    out_a = sgu_fwd(proj, lng, lnb, sw, bcol, f"sgu_fwd_{tag}")
    qkv = rope_fwd(proj, cos, sin, f"rope_fwd_{tag}")
    npat = len(DILATIONS)
    qkv_res = [tuple(qkv[3 * p:3 * p + 3]) for p in range(npat)]
    os_, lses = [], []
    for d, (qd, kd, vd) in zip(DILATIONS, qkv_res):
        o_d, lse_d = attn_fwd(qd, kd, vd, f"attn_fwd_d{d}_{tag}")
        os_.append(o_d)
        lses.append(lse_d)
    comb = attn_combine(os_, lses, f"attn_combine_{tag}")
    out_b, o_res, lse_res = comb[0], comb[1:1 + npat], comb[1 + npat:]
    outs = resid_matmul([out_a, out_b], w_out, h, mod, 5, 1.0, f"mix_out_{tag}", norm_next)
    hn, om = outs[0], outs[1]
    return hn, (outs[2] if norm_next else None), (h, y, proj, out_a, out_b, o_res, lse_res, qkv_res, om)


def _mixer_bwd(dh, dom, res, ng, mod, w_inT, w_out, sgu, cos, sin, on_grads, next_gate, after, tag):
    lng, lnb, sw, swt, bcol = sgu
    h, y, proj, out_a, out_b, o_res, lse_res, qkv_res, om = res
    D = h.shape[1]
    dmixed = matmul_nt(dom, w_out, f"dmixed_{tag}", after)
    woutbuf = lax.empty((1, 2 * MIX_HALF, D), BF16)
    woutbuf = matmul_tn(out_a, dom, woutbuf, 0, 0, f"dwout_a_{tag}", tmo_cap=MIX_HALF)
    woutbuf = matmul_tn(out_b, dom, woutbuf, 0, MIX_HALF, f"dwout_b_{tag}", tmo_cap=MIX_HALF)
    d_uv, d_sw, d_svec = sgu_bwd(proj, dmixed, lng, lnb, sw, swt, bcol, f"sgu_bwd_{tag}")
    do_res = to_residues(dmixed, 1, f"dout_res_{tag}")
    dqs, dks, dvs = [], [], []
    for p, (d, (qd, kd, vd)) in enumerate(zip(DILATIONS, qkv_res)):
        dq, dk, dv = attn_bwd(qd, kd, vd, do_res[p], o_res[p], lse_res[p], f"attn_bwd_d{d}_{tag}")
        dqs.append(dq)
        dks.append(dk)
        dvs.append(dv)
    d_qkv = rope_bwd(dqs, dks, dvs, cos, sin, f"rope_bwd_{tag}")
    winbuf = lax.empty((1, 5 * MIX_HALF, D), BF16)
    winbuf = matmul_tn(d_uv, y, winbuf, 0, 0, f"dwin_uv_{tag}", tmo_cap=MIX_HALF)
    winbuf = matmul_tn(d_qkv, y, winbuf, 0, 2 * MIX_HALF, f"dwin_qkv_{tag}", tmo_cap=MIX_HALF)
    token, then = on_grads([winbuf, woutbuf])
    pairs = [(d_uv, 0, w_inT, 0, 2 * MIX_HALF), (d_qkv, 0, w_inT, 1, 2 * MIX_HALF), (d_qkv, 2, w_inT, 4, MIX_HALF)]
    outs = dy_normbwd(pairs, h, dh, ng, 1, mod, 4, f"mix_bwd_dy_{tag}", next_gate, [token])
    return outs, d_sw, d_svec, then


def _local_step(x, tgt, mods, ngs, get_w, sgus, gf, on_block_grads, on_layer_small):
    T, D = x.shape
    cos, sin = _rope_tables(T)
    h = x
    saved, weights = [], []
    for l in range(2):
        def getter(blk, l=l):
            return lambda after: get_w(l, blk, after)

        if l == 0:
            y = normmod_fwd(h, ngs[0], 0, mods[0], 0, 1, "normmod_l0f1")
        h, y, r1, wf1 = _ffn_fwd(h, y, mods[l], 0, getter("f1u"), getter("f1d"), (ngs[l], 1, mods[l], 3, 4), f"l{l}f1")
        w_inT, w_out = get_w(l, "mx", h)
        h, y, r2 = _mixer_fwd(h, y, mods[l], (w_inT, (0,)), (w_out, (0,)), sgus[l], cos, sin,
                              (ngs[l], 2, mods[l], 6, 7), f"l{l}mx")
        h, y, r3, wf2 = _ffn_fwd(h, y, mods[l], 6, getter("f2u"), getter("f2d"),
                                 (ngs[l + 1], 0, mods[l + 1], 0, 1) if l + 1 < 2 else None, f"l{l}f2")
        saved.append((r1, r2, r3))
        weights.append((wf1, w_inT, w_out, wf2))
    def gate_of(l, blk):
        r1, r2, r3 = saved[l]
        o, i_g, coef = {"f2": (r3[5], 8, 0.5), "mx": (r2[-1], 5, 1.0), "f1": (r1[5], 2, 0.5)}[blk]
        return o, mods[l], i_g, coef

    seq = [(l, blk) for l in (1, 0) for blk in ("f2", "mx", "f1")]
    dh, red_final, do, red_g = final_loss_bwd(h, gf, tgt, gate_of(*seq[0]), "final_loss_bwd")
    rn, rg = {}, {}
    after = []
    for idx, (l, blk) in enumerate(seq):
        r1, r2, r3 = saved[l]
        wf1, w_inT, w_out, wf2 = weights[l]
        nxt = gate_of(*seq[idx + 1]) if idx + 1 < len(seq) else None
        rg[blk] = red_g
        tag = f"l{l}{blk}"

        def on(arrays, l=l, blk=blk):
            return on_block_grads(l, blk, arrays)

        if blk == "f2":
            outs, then = _ffn_bwd(dh, do, r3, ngs[l], 2, mods[l], 6, *wf2, on, nxt, after, tag)
        elif blk == "mx":
            outs, d_sw, d_svec, then = _mixer_bwd(dh, do, r2, ngs[l], mods[l], (w_inT, (0,)), (w_out, (0,)), sgus[l],
                                                  cos, sin, on, nxt, after, tag)
        else:
            outs, then = _ffn_bwd(dh, do, r1, ngs[l], 0, mods[l], 0, *wf1, on, nxt, after, tag)
        dh, rn[blk] = outs[0], outs[1]
        if nxt is not None:
            do, red_g = outs[2], outs[3]
        if blk == "f1":
            small = on_layer_small(l, dict(sgu_w=d_sw, sgu_vec=d_svec, red_n=(rn["f1"], rn["mx"], rn["f2"]),
                                           red_g=(rg["f1"], rg["mx"], rg["f2"])), red_final if l == 0 else None)
            after = [small, then(small)]
        else:
            after = [then(dh)]
    return dh


def _adam_out(w, g, m, v, name):
    shp = w.shape
    two_d = (-1, shp[-1])
    d, mn, vn = adamw(w.reshape(two_d), g.reshape(two_d), m.reshape(two_d), v.reshape(two_d), name)
    return g, d.reshape(shp), mn.reshape(shp), vn.reshape(shp)


def kernel(x, c, ada_w, ada_b, norm_g, ffn1_wg, ffn1_wu, ffn1_wd, ffn2_wg, ffn2_wu, ffn2_wd, w_in, sgu_ln_g, sgu_ln_b, sgu_w, sgu_b, w_out, final_g, loss_target, m_ada_w, m_ada_b, m_norm_g, m_ffn1_wg, m_ffn1_wu, m_ffn1_wd, m_ffn2_wg, m_ffn2_wu, m_ffn2_wd, m_w_in, m_sgu_ln_g, m_sgu_ln_b, m_sgu_w, m_sgu_b, m_w_out, m_final_g, v_ada_w, v_ada_b, v_norm_g, v_ffn1_wg, v_ffn1_wu, v_ffn1_wd, v_ffn2_wg, v_ffn2_wu, v_ffn2_wd, v_w_in, v_sgu_ln_g, v_sgu_ln_b, v_sgu_w, v_sgu_b, v_w_out, v_final_g):
    T, D = x.shape[1], x.shape[2]
    NL = ada_w.shape[0]
    mx, my, mc = _my_place()
    me = 4 * mx + 2 * my + mc
    ci = 2 * mx + my
    c_idx = jnp.reshape(mc, (1,)).astype(jnp.int32)
    place = jnp.stack([ci, mc]).astype(jnp.int32)

    ngw = norm_g.shape[2]
    small_in = jnp.concatenate([jnp.pad(c, ((0, 7), (0, 0))),
                                jnp.pad(norm_g.reshape(NL * 3, ngw), ((0, 8 - NL * 3), (0, D - ngw)))], axis=0)
    small_all, _ = gather_small(small_in, place, "gather_c_normg")
    c_all = small_all[:, 0, :]
    ng_parts = small_all[0::2, 8:8 + NL * 3, :ngw]
    ngs = jnp.transpose(ng_parts, (1, 0, 2)).reshape(NL, 3, N_CHIP * ngw)

    nmod = ada_w.shape[2]
    ada_b_mine = lax.dynamic_slice_in_dim(ada_b, ci * nmod, nmod, axis=1).reshape(NL, 1, nmod)
    mod_part = ada_fwd(c_all, ada_w, ada_b_mine, "ada_fwd")
    mod_all, _ = gather_small(mod_part.reshape(NL * N_DEV, nmod), place, "gather_mod")
    mod_rows = lax.dynamic_index_in_dim(mod_all.reshape(N_DEV, NL, N_DEV, nmod), me, axis=2, keepdims=False)
    mods = jnp.transpose(mod_rows[0::2], (1, 0, 2)).reshape(NL, N_ADA, D)

    sgus = []
    for l in range(NL):
        sgus.append((sgu_ln_g[l].reshape(1, MIX_HALF), sgu_ln_b[l].reshape(1, MIX_HALF), sgu_w[l],
                     jnp.swapaxes(sgu_w[l], 1, 2), jnp.transpose(sgu_b[l])))

    def halves(a):
        n, r, _ = a.shape
        return a.reshape(n, 2, r // 2, D)

    first_group = halves(jnp.stack([ffn1_wg[0].T, ffn1_wu[0].T], axis=0).astype(BF16))
    first_handle, first_token = first_start(first_group, mods, "first_start")
    zero = first_token[0, 0]
    mods = mods + zero

    def prep(a):
        return (a + zero).astype(BF16)

    groups = []
    for l in range(NL):
        groups += [[halves(jnp.stack([prep(ffn1_wg[l].T), prep(ffn1_wu[l].T)], axis=0))],
                   [halves(prep(ffn1_wd[l])[None])],
                   [halves(prep(w_in[l].T)[None]), halves(prep(w_out[l])[None])],
                   [halves(jnp.stack([prep(ffn2_wg[l].T), prep(ffn2_wu[l].T)], axis=0))],
                   [halves(prep(ffn2_wd[l])[None])]]
    handles, token = gather_start(groups[1:], mods, "gather_start")
    handles = [None] + handles
    mods = mods + token[0, 0]
    group_no = {"f1u": 0, "f1d": 1, "mx": 2, "f2u": 3, "f2d": 4}

    def get_w(l, key, after):
        g = len(group_no) * l + group_no[key]
        if g == 0:
            full = [first_wait(first_forward(first_handle, after, "first_forward"), place, "first_wait")]
        else:
            full = gather_wait(handles[g], after, f"gather_wait_l{l}{key}")
        full = [a.reshape(a.shape[0], N_CHIP * 2 * a.shape[3], D) for a in full]
        return full[0] if key != "mx" else tuple(full)

    def split(a):
        n, r4, _ = a.shape
        return a.reshape(n, N_CHIP, 2, r4 // N_CHIP // 2, D)

    pending, small_pending, small_tokens = {}, {}, {}

    def on_block_grads(l, blk, bufs):
        tag = f"l{l}{blk}"
        sib, tok1 = sibling_start([split(g) for g in bufs], place, f"rs_sibling_start_{tag}")

        def then(after):
            parts, lands = sibling_wait(sib, after, f"rs_sibling_wait_{tag}")
            psums = [sum_halves(g, ld, c_idx, f"rs_sum_halves_{tag}_{i}") for i, (g, ld) in enumerate(zip(parts, lands))]
            pending[(l, blk)], tok2 = scatter_start(psums, lands[0], f"rs_chips_start_{tag}")
            return tok2

        return tok1, then

    def blocks_sums(blocks, after):
        ssums, counts = [], []
        for l, blk in blocks:
            psums, lands2 = scatter_wait(pending.pop((l, blk)), after, f"rs_chips_wait_l{l}{blk}")
            ssums += [sum_chips(p, ld, place, f"rs_sum_chips_l{l}{blk}_{i}") for i, (p, ld) in enumerate(zip(psums, lands2))]
            counts.append(len(psums))
        return ssums, counts

    def per_block(completed, counts):
        fins = [f.reshape(f.shape[0], -1, D) for f in completed]
        out, i = [], 0
        for n in counts:
            out.append(fins[i:i + n])
            i += n
        return out

    def on_layer_small(l, grads, red_final):
        blocks = list(grads["red_n"]) + list(grads["red_g"])
        blocks.append(jnp.pad(grads["sgu_vec"], ((0, 0), (0, D - MIX_HALF))))
        blocks.append(grads["sgu_w"].reshape(-1, D))
        if red_final is not None:
            blocks.append(red_final)
        xs = jnp.concatenate(blocks, axis=0)
        small_pending[l], small_tokens[l] = small_start(xs, place, f"small_start_l{l}")
        return small_tokens[l]

    grad_x = _local_step(x[0], loss_target[0], mods, ngs, get_w, sgus, final_g.reshape(1, D),
                         on_block_grads, on_layer_small)

    adam_state = {}

    def adam_big(nm, l, g, w, m, v):
        adam_state[nm] = adamw_layer(w, g, m, v, l, adam_state.get(nm), f"adamw_{nm}_l{l}")

    def adam_block(l, blk, fin):
        if blk == "mx":
            adam_big("w_in", l, fin[0][0].T, w_in, m_w_in, v_w_in)
            adam_big("w_out", l, fin[1][0], w_out, m_w_out, v_w_out)
        else:
            ws = ((ffn1_wg, m_ffn1_wg, v_ffn1_wg), (ffn1_wu, m_ffn1_wu, v_ffn1_wu), (ffn1_wd, m_ffn1_wd, v_ffn1_wd)) \
                if blk == "f1" else \
                ((ffn2_wg, m_ffn2_wg, v_ffn2_wg), (ffn2_wu, m_ffn2_wu, v_ffn2_wu), (ffn2_wd, m_ffn2_wd, v_ffn2_wd))
            pre = "ffn1" if blk == "f1" else "ffn2"
            for k, (nm, tr) in enumerate((("wg", True), ("wu", True), ("wd", False))):
                adam_big(f"{pre}_{nm}", l, fin[0][k], *[jnp.swapaxes(t, 1, 2) if tr else t for t in ws[k]])

    done_order = [(l, blk) for l in range(NL - 1, -1, -1) for blk in ("f2", "mx", "f1")]
    early_sums, early_counts = blocks_sums(done_order[:-1], small_tokens[0])
    early = complete_start(early_sums, place, "rs_complete_early_start")

    small_sum, small_all = [], []
    for l in range(NL):
        xs, land = small_wait(small_pending[l], early[2][-1], f"small_wait_l{l}")
        full = lax.dynamic_update_slice(land, xs[None], (me, 0, 0))
        small_all.append(full)
        small_sum.append(sum_slots(full, f"small_sum_l{l}"))
    offs = [8 * i for i in range(8)]
    off_final = offs[7] + SGU_HEADS * ATT_BLOCK * HEAD_LANES // D
    loss = small_sum[0][off_final + 1, 0]
    g_final_g = small_sum[0][off_final, :]
    g_norm_g, g_ada_b, g_lng, g_lnb, g_sb, g_sw, dmod_all = [], [], [], [], [], [], []
    for l in range(NL):
        rn = [small_sum[l][offs[i]:offs[i] + 8] for i in range(3)]
        rg = [small_sum[l][offs[3 + i]:offs[3 + i] + 8] for i in range(3)]
        g_norm_g.append(jnp.stack([rn[i][2] for i in range(3)], axis=0))
        g_ada_b.append(jnp.concatenate([jnp.stack([rn[i][0], rn[i][1], rg[i][0]], axis=0) for i in range(3)],
                                       axis=0).reshape(N_ADA * D))
        sv = small_sum[l][offs[6]:offs[6] + 8, :MIX_HALF]
        g_lng.append(sv[0].reshape(SGU_HEADS, HEAD_LANES))
        g_lnb.append(sv[1].reshape(SGU_HEADS, HEAD_LANES))
        g_sb.append(sv[2].reshape(SGU_HEADS, ATT_BLOCK))
        g_sw.append(small_sum[l][offs[7]:off_final].reshape(sgu_w.shape[1:]))
        rows = []
        for i in range(3):
            an = small_all[l][:, offs[i]:offs[i] + 2]
            ag = small_all[l][:, offs[3 + i]:offs[3 + i] + 1]
            rows += [an[:, 0], an[:, 1], ag[:, 0]]
        dmod_all.append(jnp.stack(rows, axis=1).reshape(N_DEV, N_ADA * D))
    dmod_all = jnp.stack(dmod_all, axis=0)
    dmod_mine = lax.dynamic_slice_in_dim(dmod_all, ci * nmod, nmod, axis=2)
    g_ada_w = ada_bwd(jnp.transpose(c_all), dmod_mine, "ada_bwd")
    g_ada_b = jnp.stack(g_ada_b, axis=0)
    g_norm_g_full = jnp.stack(g_norm_g, axis=0)
    g_norm_g_mine = lax.dynamic_slice_in_dim(g_norm_g_full, ci * ngw, ngw, axis=2)

    small_params = [
        ("ada_w", ada_w, g_ada_w, m_ada_w, v_ada_w),
        ("ada_b", ada_b, g_ada_b, m_ada_b, v_ada_b),
        ("norm_g", norm_g, g_norm_g_mine, m_norm_g, v_norm_g),
        ("sgu_ln_g", sgu_ln_g, jnp.stack(g_lng, axis=0), m_sgu_ln_g, v_sgu_ln_g),
        ("sgu_ln_b", sgu_ln_b, jnp.stack(g_lnb, axis=0), m_sgu_ln_b, v_sgu_ln_b),
        ("sgu_w", sgu_w, jnp.stack(g_sw, axis=0), m_sgu_w, v_sgu_w),
        ("sgu_b", sgu_b, jnp.stack(g_sb, axis=0), m_sgu_b, v_sgu_b),
        ("final_g", final_g.reshape(1, D), g_final_g.reshape(1, D), m_final_g.reshape(1, D), v_final_g.reshape(1, D)),
    ]
    for nm, w, g, m, v in small_params:
        res = _adam_out(w, g, m, v, f"adamw_{nm}")
        adam_state[nm] = tuple(t.reshape(D) for t in res) if nm == "final_g" else res

    completed = complete_wait(early, [st[1] for st in adam_state.values()], "rs_complete_early_wait")
    for (l, blk), fin in zip(done_order[:-1], per_block(completed, early_counts)):
        adam_block(l, blk, fin)

    l, blk = done_order[-1]
    last_sums, last_counts = blocks_sums([(l, blk)], [st[1] for st in adam_state.values()])
    adam_block(l, blk, per_block(sibling_complete(last_sums, "rs_complete_last"), last_counts)[0])

    names = ["ada_w", "ada_b", "norm_g", "ffn1_wg", "ffn1_wu", "ffn1_wd", "ffn2_wg", "ffn2_wu", "ffn2_wd", "w_in",
             "sgu_ln_g", "sgu_ln_b", "sgu_w", "sgu_b", "w_out", "final_g"]
    shapes = [t.shape for t in (ada_w, ada_b, norm_g, ffn1_wg, ffn1_wu, ffn1_wd, ffn2_wg, ffn2_wu, ffn2_wd, w_in,
                                sgu_ln_g, sgu_ln_b, sgu_w, sgu_b, w_out, final_g)]
    def shaped(nm, t, s):
        if nm in ("ffn1_wg", "ffn1_wu", "ffn2_wg", "ffn2_wu"):
            return jnp.swapaxes(t.reshape(s[0], s[2], s[1]), 1, 2)
        return t.reshape(s)

    return (loss, grad_x[None], *[shaped(nm, adam_state[nm][i], s) for i in range(4) for nm, s in zip(names, shapes)])
```

```python
import math

import jax
import jax.numpy as jnp
from jax import lax
from jax.experimental import pallas as pl
from jax.experimental.pallas import tpu as pltpu

F32 = jnp.float32
BF16 = jnp.bfloat16
EPS = 1e-6
SGU_HEADS = 4
HEAD_LANES = 128
ATT_DH = 64
ATT_BLOCK = 128
MIX_HALF = SGU_HEADS * HEAD_LANES
DILATIONS = (1, 4, 16)
ROPE_THETA = 10000.0
N_ADA = 9
ADAM_LR, ADAM_B1, ADAM_B2, ADAM_EPS, ADAM_WD, ADAM_STEP = 0.001, 0.9, 0.999, 1e-08, 0.01, 10
NEG = -1e30
V7X_VMEM_BYTES = 64 * 1024 * 1024
VMEM_LIMIT = V7X_VMEM_BYTES * 7 // 8
MESH = pl.DeviceIdType.MESH
N_DEV = 8
N_CHIP = 4
_ANY = pl.BlockSpec(memory_space=pl.ANY)


def _tile(n, cap, mult):
    if n <= cap:
        return n
    t = (cap // mult) * mult
    while t >= mult:
        if n % t == 0:
            return t
        t -= mult
    raise ValueError((n, cap, mult))


def _params(dims=None):
    return pltpu.CompilerParams(dimension_semantics=dims, vmem_limit_bytes=VMEM_LIMIT)


def _wspec(w, rows, idx, resident=False):
    arr, lead = w
    kw = dict(pipeline_mode=pl.Buffered(1)) if resident else {}
    return pl.BlockSpec((None,) * len(lead) + (rows, arr.shape[-1]), lambda *g: tuple(lead) + (idx(*g), 0), **kw)


def _wrows(w):
    return w[0].shape[-2]


def _nt(a, b):
    return lax.dot_general(a, b, (((1,), (1,)), ((), ())), preferred_element_type=F32)


def _tn(a, b):
    return lax.dot_general(a, b, (((0,), (0,)), ((), ())), preferred_element_type=F32)


def _nn(a, b):
    return jnp.dot(a, b, preferred_element_type=F32)


def _sigmoid(x):
    return 0.5 * jnp.tanh(0.5 * x) + 0.5


_GELU_K = math.sqrt(2.0 / math.pi)
_GELU_C = 0.044715


def _gelu_and_grad(x):
    x2 = x * x
    t = jnp.tanh(_GELU_K * (x + _GELU_C * x * x2))
    g = 0.5 * x * (1.0 + t)
    dg = 0.5 * (1.0 + t) + 0.5 * x * (1.0 - t * t) * (_GELU_K * (1.0 + 3.0 * _GELU_C * x2))
    return g, dg


def normmod_fwd(h, ng, i_n, mod, i_sh, i_sc, name):
    T, D = h.shape
    tm = _tile(T, 512, 8)

    def body(h_ref, ng_ref, mod_ref, y_ref):
        y_ref[...] = _normmod(h_ref[...], ng_ref[i_n:i_n + 1, :], mod_ref[i_sh:i_sh + 1, :],
                              mod_ref[i_sc:i_sc + 1, :]).astype(BF16)

    return pl.pallas_call(
        body, name=name, grid=(T // tm,),
        in_specs=[pl.BlockSpec((tm, D), lambda i: (i, 0)),
                  pl.BlockSpec(ng.shape, lambda i: (0, 0)),
                  pl.BlockSpec(mod.shape, lambda i: (0, 0))],
        out_specs=pl.BlockSpec((tm, D), lambda i: (i, 0)),
        out_shape=jax.ShapeDtypeStruct((T, D), BF16),
        compiler_params=_params(("parallel",)),
    )(h, ng, mod)


def ffn_up(y, wgT, wuT, name):
    T, D = y.shape
    F = _wrows(wgT)
    tm = _tile(T, 512, 16)
    tf = _tile(F, 2816, 256)
    cuts = list(range(0, tf, 256)) + [tf]

    def body(y_ref, wg_ref, wu_ref, p_ref, q_ref, s_ref):
        yv = y_ref[...]
        for c0, c1 in zip(cuts[:-1], cuts[1:]):
            a = _nt(yv, wg_ref[c0:c1, :])
            b = _nt(yv, wu_ref[c0:c1, :])
            sig = _sigmoid(a)
            q = a * sig
            p_ref[:, c0:c1] = (b * (sig + q * (1.0 - sig))).astype(BF16)
            q_ref[:, c0:c1] = q.astype(BF16)
            s_ref[:, c0:c1] = (q * b).astype(BF16)

    act = jax.ShapeDtypeStruct((T, F), BF16)
    return pl.pallas_call(
        body, name=name, grid=(F // tf, T // tm),
        in_specs=[pl.BlockSpec((tm, D), lambda j, i: (i, 0)),
                  _wspec(wgT, tf, lambda j, i: j, resident=True),
                  _wspec(wuT, tf, lambda j, i: j, resident=True)],
        out_specs=[pl.BlockSpec((tm, tf), lambda j, i: (i, j))] * 3,
        out_shape=[act, act, act],
        compiler_params=_params(("parallel", "parallel")),
    )(y, wgT[0], wuT[0])


def _normmod(x, gn, sh, sc):
    r = lax.rsqrt(jnp.mean(x * x, axis=-1, keepdims=True) + EPS)
    return ((x * r) * gn) * (1.0 + sc) + sh


def resid_matmul(xs, w, h, mod, i_g, coef, name, norm_next=None):
    T, D = h.shape
    kb = xs[0].shape[1]
    assert all(x.shape == (T, kb) for x in xs) and _wrows(w) == kb * len(xs)
    tm = _tile(T, 1024, 16)
    nx = len(xs)
    n_in, n_out, n_shape, n_ops = [], [], [], []
    if norm_next:
        ng_n, i_n, mod_n, i_sh, i_sc = norm_next
        n_in = [pl.BlockSpec(ng_n.shape, lambda i: (0, 0)), pl.BlockSpec(mod_n.shape, lambda i: (0, 0))]
        n_out = [pl.BlockSpec((tm, D), lambda i: (i, 0))]
        n_shape = [jax.ShapeDtypeStruct((T, D), BF16)]
        n_ops = [ng_n, mod_n]

    def body(*refs):
        x_refs, w_refs = refs[:nx], refs[nx:2 * nx]
        h_ref, mod_ref = refs[2 * nx:2 * nx + 2]
        hn_ref, o_ref = refs[2 * nx + 2 + len(n_in):2 * nx + 4 + len(n_in)]
        o = _nn(x_refs[0][...], w_refs[0][...])
        for xr, wr in zip(x_refs[1:], w_refs[1:]):
            o = o + _nn(xr[...], wr[...])
        o_ref[...] = o.astype(BF16)
        hn = h_ref[...] + (coef * mod_ref[i_g:i_g + 1, :]) * o
        hn_ref[...] = hn
        if norm_next:
            ng_ref, modn_ref = refs[2 * nx + 2], refs[2 * nx + 3]
            refs[-1][...] = _normmod(hn, ng_ref[i_n:i_n + 1, :], modn_ref[i_sh:i_sh + 1, :],
                                     modn_ref[i_sc:i_sc + 1, :]).astype(BF16)

    return pl.pallas_call(
        body, name=name, grid=(T // tm,),
        in_specs=([pl.BlockSpec((tm, kb), lambda i: (i, 0))] * nx
                  + [_wspec(w, kb, lambda i, p=p: p, resident=True) for p in range(nx)]
                  + [pl.BlockSpec((tm, D), lambda i: (i, 0)),
                     pl.BlockSpec(mod.shape, lambda i: (0, 0))] + n_in),
        out_specs=[pl.BlockSpec((tm, D), lambda i: (i, 0))] * 2 + n_out,
        out_shape=[jax.ShapeDtypeStruct((T, D), F32), jax.ShapeDtypeStruct((T, D), BF16)] + n_shape,
        compiler_params=_params(("parallel",)),
    )(*xs, *([w[0]] * nx), h, mod, *n_ops)


def _gate_specs(gate, tm, D):
    o, mod, _, _ = gate
    T = o.shape[0]
    return ([pl.BlockSpec((tm, D), lambda i: (i, 0)), pl.BlockSpec(mod.shape, lambda i: (0, 0))],
            [pl.BlockSpec((tm, D), lambda i: (i, 0)), pl.BlockSpec((8, D), lambda i: (0, 0))],
            [jax.ShapeDtypeStruct((T, D), BF16), jax.ShapeDtypeStruct((8, D), F32)],
            [o, mod])


def _gate_emit(d, gate, o_ref, mod_ref, do_ref, red_ref):
    _, _, i_g, coef = gate
    do_ref[...] = (d * (coef * mod_ref[i_g:i_g + 1, :])).astype(BF16)

    @pl.when(pl.program_id(0) == 0)
    def _():
        red_ref[...] = jnp.zeros_like(red_ref)

    red_ref[0:1, :] += coef * jnp.sum(d * o_ref[...].astype(F32), axis=0, keepdims=True)


def ffn_bwd_mid(do, wd, p, q, name, after=()):
    T, D = do.shape
    F = _wrows(wd)
    tm = _tile(T, 512, 16)
    tf = _tile(F, 2816, 256)
    cuts = list(range(0, tf, 256)) + [tf]

    def body(do_ref, wd_ref, p_ref, q_ref, *rest):
        da_ref, db_ref = rest[-2:]
        dov = do_ref[...]
        for c0, c1 in zip(cuts[:-1], cuts[1:]):
            ds = _nt(dov, wd_ref[c0:c1, :])
            da_ref[:, c0:c1] = (ds * p_ref[:, c0:c1].astype(F32)).astype(BF16)
            db_ref[:, c0:c1] = (ds * q_ref[:, c0:c1].astype(F32)).astype(BF16)

    act = jax.ShapeDtypeStruct((T, F), BF16)
    return pl.pallas_call(
        body, name=name, grid=(F // tf, T // tm),
        in_specs=[pl.BlockSpec((tm, D), lambda j, i: (i, 0)),
                  _wspec(wd, tf, lambda j, i: j, resident=True),
                  pl.BlockSpec((tm, tf), lambda j, i: (i, j)),
                  pl.BlockSpec((tm, tf), lambda j, i: (i, j))] + [_ANY] * len(after),
        out_specs=[pl.BlockSpec((tm, tf), lambda j, i: (i, j))] * 2,
        out_shape=[act, act],
        compiler_params=_params(("parallel", "parallel")),
    )(do, wd[0], p, q, *after)


def dy_normbwd(pairs, h, dhp, ng, i_n, mod, i_sc, name, gate=None, after=()):
    T, D = h.shape
    tm = _tile(T, 512, 16)
    npair = len(pairs)
    g_in, g_out, g_shape, g_ops = _gate_specs(gate, tm, D) if gate else ([], [], [], [])
    n_in = 2 * npair + 4 + len(g_in) + len(after)

    def body(*refs):
        x_refs, w_refs = refs[:npair], refs[npair:2 * npair]
        h_ref, dhp_ref, ng_ref, mod_ref = refs[2 * npair:2 * npair + 4]
        dh_ref, red_ref = refs[n_in:n_in + 2]
        dy = _nn(x_refs[0][...], w_refs[0][...])
        for xr, wr in zip(x_refs[1:], w_refs[1:]):
            dy = dy + _nn(xr[...], wr[...])
        x = h_ref[...]
        r = lax.rsqrt(jnp.mean(x * x, axis=-1, keepdims=True) + EPS)
        n = x * r
        gn = ng_ref[i_n:i_n + 1, :]
        sc1 = 1.0 + mod_ref[i_sc:i_sc + 1, :]
        w = sc1 * gn
        dyn = dy * n
        col = jnp.sum(dyn, axis=0, keepdims=True)

        @pl.when(pl.program_id(0) == 0)
        def _():
            red_ref[...] = jnp.zeros_like(red_ref)

        red_ref[0:1, :] += jnp.sum(dy, axis=0, keepdims=True)
        red_ref[1:2, :] += gn * col
        red_ref[2:3, :] += sc1 * col
        dh_new = dhp_ref[...] + r * (dy * w - n * jnp.mean(dyn * w, axis=-1, keepdims=True))
        dh_ref[...] = dh_new
        if gate:
            _gate_emit(dh_new, gate, refs[2 * npair + 4], refs[2 * npair + 5], refs[-2], refs[-1])

    in_specs = ([pl.BlockSpec((tm, kb), lambda i, c=c: (i, c)) for (_, c, _, _, kb) in pairs]
                + [_wspec(w, kb, lambda i, r=r: r, resident=True) for (_, _, w, r, kb) in pairs]
                + [pl.BlockSpec((tm, D), lambda i: (i, 0)),
                   pl.BlockSpec((tm, D), lambda i: (i, 0)),
                   pl.BlockSpec(ng.shape, lambda i: (0, 0)),
                   pl.BlockSpec(mod.shape, lambda i: (0, 0))] + g_in + [_ANY] * len(after))
    return pl.pallas_call(
        body, name=name, grid=(T // tm,), in_specs=in_specs,
        out_specs=[pl.BlockSpec((tm, D), lambda i: (i, 0)), pl.BlockSpec((8, D), lambda i: (0, 0))] + g_out,
        out_shape=[jax.ShapeDtypeStruct((T, D), F32), jax.ShapeDtypeStruct((8, D), F32)] + g_shape,
        compiler_params=_params(("arbitrary",)),
    )(*[p[0] for p in pairs], *[p[2][0] for p in pairs], h, dhp, ng, mod, *g_ops, *after)


def matmul_tn(a, b, buf, slot, row0, name, tmo_cap=1408):
    T, N = b.shape
    ma = a.shape[1]
    tmo = _tile(ma, tmo_cap, 128)
    assert row0 % tmo == 0
    nmo = ma // tmo
    tk = _tile(T, 2048, 16)
    nk = T // tk

    def body(a_ref, b_ref, buf_ref, o_ref, acc_ref):
        k = pl.program_id(1)

        @pl.when(k == 0)
        def _():
            acc_ref[...] = jnp.zeros_like(acc_ref)

        acc_ref[...] += _tn(a_ref[...], b_ref[...])

        @pl.when(k == nk - 1)
        def _():
            o_ref[...] = acc_ref[...].astype(BF16)

    return pl.pallas_call(
        body, name=name, grid=(nmo, nk),
        in_specs=[pl.BlockSpec((tk, tmo), lambda j, k: (k, j)),
                  pl.BlockSpec((tk, N), lambda j, k: (k, 0)),
                  pl.BlockSpec(memory_space=pl.ANY)],
        out_specs=pl.BlockSpec((None, tmo, N), lambda j, k: (slot, row0 // tmo + j, 0)),
        out_shape=jax.ShapeDtypeStruct(buf.shape, BF16),
        scratch_shapes=[pltpu.VMEM((tmo, N), F32)],
        input_output_aliases={2: 0},
        compiler_params=_params(("parallel", "arbitrary")),
    )(a, b, buf)


def matmul_nt(x, w, name, after=()):
    T, K = x.shape
    N = _wrows(w)
    tm = _tile(T, 1024, 16)
    tn = _tile(N, 1280, 128)

    def body(x_ref, w_ref, *rest):
        rest[-1][...] = _nt(x_ref[...], w_ref[...]).astype(BF16)

    return pl.pallas_call(
        body, name=name, grid=(N // tn, T // tm),
        in_specs=[pl.BlockSpec((tm, K), lambda j, i: (i, 0)), _wspec(w, tn, lambda j, i: j)] + [_ANY] * len(after),
        out_specs=pl.BlockSpec((tm, tn), lambda j, i: (i, j)),
        out_shape=jax.ShapeDtypeStruct((T, N), BF16),
        compiler_params=_params(("parallel", "parallel")),
    )(x, w[0], *after)


def _sgu_head_fwd(u, v, lng, lnb):
    gu, dgu = _gelu_and_grad(u)
    gv, dgv = _gelu_and_grad(v)
    mu = jnp.mean(gv, axis=-1, keepdims=True)
    xc = gv - mu
    rstd = lax.rsqrt(jnp.mean(xc * xc, axis=-1, keepdims=True) + EPS)
    xhat = xc * rstd
    vn = xhat * lng + lnb
    return gu, dgu, dgv, rstd, xhat, vn


def _tril_mask():
    r = lax.broadcasted_iota(jnp.int32, (ATT_BLOCK, ATT_BLOCK), 0)
    c = lax.broadcasted_iota(jnp.int32, (ATT_BLOCK, ATT_BLOCK), 1)
    return c <= r


def _triu_mask():
    r = lax.broadcasted_iota(jnp.int32, (ATT_BLOCK, ATT_BLOCK), 0)
    c = lax.broadcasted_iota(jnp.int32, (ATT_BLOCK, ATT_BLOCK), 1)
    return r <= c


def sgu_fwd(proj, lng, lnb, w, bcol, name):
    T = proj.shape[0]
    tm = _tile(T, 512, 128)
    nch = tm // ATT_BLOCK

    def body(u_ref, v_ref, lng_ref, lnb_ref, w_ref, b_ref, o_ref):
        tril = _tril_mask()
        for hd in range(SGU_HEADS):
            sl = slice(hd * HEAD_LANES, (hd + 1) * HEAD_LANES)
            u = u_ref[:, sl].astype(F32)
            v = v_ref[:, sl].astype(F32)
            gu, _, _, _, _, vn = _sgu_head_fwd(u, v, lng_ref[:, sl], lnb_ref[:, sl])
            wm = jnp.where(tril, w_ref[hd], 0.0).astype(BF16)
            vnb = vn.astype(BF16)
            bc = b_ref[:, hd:hd + 1]
            for ch in range(nch):
                rs = slice(ch * ATT_BLOCK, (ch + 1) * ATT_BLOCK)
                z = _nn(wm, vnb[rs, :]) + bc
                o_ref[rs, sl] = (gu[rs, :] * z).astype(BF16)

    return pl.pallas_call(
        body, name=name, grid=(T // tm,),
        in_specs=[pl.BlockSpec((tm, MIX_HALF), lambda i: (i, 0)),
                  pl.BlockSpec((tm, MIX_HALF), lambda i: (i, 1)),
                  pl.BlockSpec((1, MIX_HALF), lambda i: (0, 0)),
                  pl.BlockSpec((1, MIX_HALF), lambda i: (0, 0)),
                  pl.BlockSpec(w.shape, lambda i: (0, 0, 0)),
                  pl.BlockSpec(bcol.shape, lambda i: (0, 0))],
        out_specs=pl.BlockSpec((tm, MIX_HALF), lambda i: (i, 0)),
        out_shape=jax.ShapeDtypeStruct((T, MIX_HALF), BF16),
        compiler_params=_params(("parallel",)),
    )(proj, proj, lng, lnb, w, bcol)


def sgu_bwd(proj, dmixed, lng, lnb, w, wt, bcol, name):
    T = proj.shape[0]
    tm = _tile(T, 512, 128)
    nch = tm // ATT_BLOCK
    nsteps = T // tm

    def body(u_ref, v_ref, g_ref, lng_ref, lnb_ref, w_ref, wt_ref, b_ref, duv_ref, dw_ref, dvec_ref, bacc_ref):
        step = pl.program_id(0)

        @pl.when(step == 0)
        def _():
            dw_ref[...] = jnp.zeros_like(dw_ref)
            dvec_ref[...] = jnp.zeros_like(dvec_ref)
            bacc_ref[...] = jnp.zeros_like(bacc_ref)

        tril = _tril_mask()
        triu = _triu_mask()
        for hd in range(SGU_HEADS):
            sl = slice(hd * HEAD_LANES, (hd + 1) * HEAD_LANES)
            u = u_ref[:, sl].astype(F32)
            v = v_ref[:, sl].astype(F32)
            lng_h = lng_ref[:, sl]
            gu, dgu, dgv, rstd, xhat, vn = _sgu_head_fwd(u, v, lng_h, lnb_ref[:, sl])
            wm = jnp.where(tril, w_ref[hd], 0.0).astype(BF16)
            wmt = jnp.where(triu, wt_ref[hd], 0.0).astype(BF16)
            vnb = vn.astype(BF16)
            bc = b_ref[:, hd:hd + 1]
            g = g_ref[:, sl].astype(F32)
            dw_acc = jnp.zeros((ATT_BLOCK, ATT_BLOCK), F32)
            b_acc = jnp.zeros((ATT_BLOCK, HEAD_LANES), F32)
            dvn_parts = []
            for ch in range(nch):
                rs = slice(ch * ATT_BLOCK, (ch + 1) * ATT_BLOCK)
                z = _nn(wm, vnb[rs, :]) + bc
                duv_ref[rs, sl] = (g[rs, :] * z * dgu[rs, :]).astype(BF16)
                dz = g[rs, :] * gu[rs, :]
                dzb = dz.astype(BF16)
                dvn_parts.append(_nn(wmt, dzb))
                dw_acc = dw_acc + _nt(dzb, vnb[rs, :])
                b_acc = b_acc + dz
            dvn = jnp.concatenate(dvn_parts, axis=0)
            dw_ref[hd] += jnp.where(tril, dw_acc, 0.0)
            bacc_ref[hd] += b_acc
            dvec_ref[0:1, sl] += jnp.sum(dvn * xhat, axis=0, keepdims=True)
            dvec_ref[1:2, sl] += jnp.sum(dvn, axis=0, keepdims=True)
            dxh = dvn * lng_h
            dgv_in = rstd * (dxh - jnp.mean(dxh, axis=-1, keepdims=True)
                             - xhat * jnp.mean(dxh * xhat, axis=-1, keepdims=True))
            duv_ref[:, MIX_HALF + hd * HEAD_LANES:MIX_HALF + (hd + 1) * HEAD_LANES] = (dgv_in * dgv).astype(BF16)

        @pl.when(step == nsteps - 1)
        def _():
            for hd in range(SGU_HEADS):
                sl = slice(hd * HEAD_LANES, (hd + 1) * HEAD_LANES)
                dvec_ref[2:3, sl] = jnp.sum(bacc_ref[hd].T, axis=0, keepdims=True)

    return pl.pallas_call(
        body, name=name, grid=(nsteps,),
        in_specs=[pl.BlockSpec((tm, MIX_HALF), lambda i: (i, 0)),
                  pl.BlockSpec((tm, MIX_HALF), lambda i: (i, 1)),
                  pl.BlockSpec((tm, MIX_HALF), lambda i: (i, 0)),
                  pl.BlockSpec((1, MIX_HALF), lambda i: (0, 0)),
                  pl.BlockSpec((1, MIX_HALF), lambda i: (0, 0)),
                  pl.BlockSpec(w.shape, lambda i: (0, 0, 0)),
                  pl.BlockSpec(w.shape, lambda i: (0, 0, 0)),
                  pl.BlockSpec(bcol.shape, lambda i: (0, 0))],
        out_specs=[pl.BlockSpec((tm, 2 * MIX_HALF), lambda i: (i, 0)),
                   pl.BlockSpec(w.shape, lambda i: (0, 0, 0)),
                   pl.BlockSpec((8, MIX_HALF), lambda i: (0, 0))],
        out_shape=[jax.ShapeDtypeStruct((T, 2 * MIX_HALF), BF16),
                   jax.ShapeDtypeStruct(w.shape, F32),
                   jax.ShapeDtypeStruct((8, MIX_HALF), F32)],
        scratch_shapes=[pltpu.VMEM((SGU_HEADS, ATT_BLOCK, HEAD_LANES), F32)],
        compiler_params=_params(("arbitrary",)),
    )(proj, proj, dmixed, lng, lnb, w, wt, bcol)


def _rot_half(t):
    lane = lax.broadcasted_iota(jnp.int32, t.shape, 1)
    first = (lane % ATT_DH) < (ATT_DH // 2)
    return jnp.where(first, -pltpu.roll(t, HEAD_LANES - ATT_DH // 2, 1), pltpu.roll(t, ATT_DH // 2, 1))


LAYOUT_ROWS = 512


def _res_spec(d, tm, W):
    return pl.BlockSpec((d, tm // d, W), lambda i: (0, i, 0))


def _res_shape(d, T, W, dtype):
    return jax.ShapeDtypeStruct((d, T // d, W), dtype)


def _slab_buf(tm, W):
    return pltpu.VMEM((W // HEAD_LANES, tm, HEAD_LANES), F32)


def _lanes(hp):
    return slice(hp * HEAD_LANES, (hp + 1) * HEAD_LANES)


def _to_res(buf, out_ref, d, dtype):
    nslab, tm, _ = buf.shape
    for hp in range(nslab):
        if d == 1:
            out_ref[0, :, _lanes(hp)] = buf[hp].astype(dtype)
        else:
            for r in range(d):
                out_ref[r, :, _lanes(hp)] = buf.at[hp][pl.ds(r, tm // d, stride=d), :].astype(dtype)


def _from_res(in_ref, buf, d):
    nslab, tm, _ = buf.shape
    for hp in range(nslab):
        if d == 1:
            buf[hp] = in_ref[0, :, _lanes(hp)].astype(F32)
        else:
            for r in range(d):
                buf.at[hp][pl.ds(r, tm // d, stride=d), :] = in_ref[r, :, _lanes(hp)].astype(F32)


def rope_fwd(proj, cos, sin, name):
    T = proj.shape[0]
    tm = LAYOUT_ROWS
    scale = 1.0 / math.sqrt(ATT_DH)
    nd = len(DILATIONS)

    def body(q_ref, k_ref, v_ref, cos_ref, sin_ref, *rest):
        outs, buf = rest[:3 * nd], rest[3 * nd]
        c = cos_ref[...]
        s = sin_ref[...]
        for which, src in enumerate((q_ref, k_ref, v_ref)):
            for hp in range(MIX_HALF // HEAD_LANES):
                t = src[:, _lanes(hp)].astype(F32)
                if which == 0:
                    t = scale * (t * c + _rot_half(t) * s)
                elif which == 1:
                    t = t * c + _rot_half(t) * s
                buf[hp] = t
            for di, d in enumerate(DILATIONS):
                _to_res(buf, outs[3 * di + which], d, BF16)

    return pl.pallas_call(
        body, name=name, grid=(T // tm,),
        in_specs=[pl.BlockSpec((tm, MIX_HALF), lambda i: (i, 2)),
                  pl.BlockSpec((tm, MIX_HALF), lambda i: (i, 3)),
                  pl.BlockSpec((tm, MIX_HALF), lambda i: (i, 4)),
                  pl.BlockSpec((tm, HEAD_LANES), lambda i: (i, 0)),
                  pl.BlockSpec((tm, HEAD_LANES), lambda i: (i, 0))],
        out_specs=[_res_spec(d, tm, MIX_HALF) for d in DILATIONS for _ in range(3)],
        out_shape=[_res_shape(d, T, MIX_HALF, BF16) for d in DILATIONS for _ in range(3)],
        scratch_shapes=[_slab_buf(tm, MIX_HALF)],
        compiler_params=_params(("parallel",)),
    )(proj, proj, proj, cos, sin)


def to_residues(x, col, name):
    T = x.shape[0]
    tm = LAYOUT_ROWS

    def body(x_ref, *rest):
        outs, buf = rest[:-1], rest[-1]
        for hp in range(MIX_HALF // HEAD_LANES):
            buf[hp] = x_ref[:, _lanes(hp)].astype(F32)
        for o_ref, d in zip(outs, DILATIONS):
            _to_res(buf, o_ref, d, BF16)

    return pl.pallas_call(
        body, name=name, grid=(T // tm,),
        in_specs=[pl.BlockSpec((tm, MIX_HALF), lambda i: (i, col))],
        out_specs=[_res_spec(d, tm, MIX_HALF) for d in DILATIONS],
        out_shape=[_res_shape(d, T, MIX_HALF, BF16) for d in DILATIONS],
        scratch_shapes=[_slab_buf(tm, MIX_HALF)],
        compiler_params=_params(("parallel",)),
    )(x)


def rope_bwd(dqs, dks, dvs, cos, sin, name):
    T = dqs[0].shape[0] * dqs[0].shape[1]
    tm = LAYOUT_ROWS
    scale = 1.0 / math.sqrt(ATT_DH)
    npat = len(dqs)

    def body(*refs):
        groups = refs[:npat], refs[npat:2 * npat], refs[2 * npat:3 * npat]
        cos_ref, sin_ref, o_ref, buf, acc = refs[3 * npat:]
        c = cos_ref[...]
        s = sin_ref[...]
        for which, g_refs in enumerate(groups):
            _from_res(g_refs[0], acc, DILATIONS[0])
            for g_ref, d in zip(g_refs[1:], DILATIONS[1:]):
                _from_res(g_ref, buf, d)
                acc[...] += buf[...]
            for hp in range(MIX_HALF // HEAD_LANES):
                g = acc[hp]
                if which == 0:
                    g = scale * g
                if which < 2:
                    g = g * c - _rot_half(g * s)
                o_ref[:, which * MIX_HALF + hp * HEAD_LANES:which * MIX_HALF + (hp + 1) * HEAD_LANES] = g.astype(BF16)

    return pl.pallas_call(
        body, name=name, grid=(T // tm,),
        in_specs=([_res_spec(d, tm, MIX_HALF) for _ in range(3) for d in DILATIONS]
                  + [pl.BlockSpec((tm, HEAD_LANES), lambda i: (i, 0))] * 2),
        out_specs=pl.BlockSpec((tm, 3 * MIX_HALF), lambda i: (i, 0)),
        out_shape=jax.ShapeDtypeStruct((T, 3 * MIX_HALF), BF16),
        scratch_shapes=[_slab_buf(tm, MIX_HALF), _slab_buf(tm, MIX_HALF)],
        compiler_params=_params(("parallel",)),
    )(*dqs, *dks, *dvs, cos, sin)


def _band_masks(n):
    r = lax.broadcasted_iota(jnp.int32, (2 * ATT_BLOCK, ATT_BLOCK), 0)
    c = lax.broadcasted_iota(jnp.int32, (2 * ATT_BLOCK, ATT_BLOCK), 1)
    qi = r % ATT_BLOCK
    head = (c < ATT_DH) == (r < ATT_BLOCK)
    return (c >= qi) & (n > 0), c <= qi, head, c[:ATT_BLOCK] < ATT_DH


def _stack_heads(x, head):
    x2 = jnp.concatenate([x, x], axis=0)
    return jnp.where(head, x2, jnp.zeros_like(x2))


def attn_fwd(q, k, v, name):
    d, L, W = q.shape
    nb = L // ATT_BLOCK
    nsub = 2 if nb % 2 == 0 else 1

    def body(q_ref, kp_ref, kc_ref, vp_ref, vc_ref, o_ref, lse_ref):
        step = pl.program_id(1)
        for u in range(nsub):
            rows = slice(u * ATT_BLOCK, (u + 1) * ATT_BLOCK)
            before = slice((u - 1) * ATT_BLOCK, u * ATT_BLOCK)
            mask_p, mask_c, head, head0 = _band_masks(step if u == 0 else 1)
            for hp in range(W // HEAD_LANES):
                sl = slice(hp * HEAD_LANES, (hp + 1) * HEAD_LANES)
                kp, vp = (kp_ref[0, :, sl], vp_ref[0, :, sl]) if u == 0 else (kc_ref[0, before, sl], vc_ref[0, before, sl])
                kc, vc = kc_ref[0, rows, sl], vc_ref[0, rows, sl]
                qs = _stack_heads(q_ref[0, rows, sl], head)
                sp = jnp.where(mask_p, _nt(qs, kp), NEG)
                sc = jnp.where(mask_c, _nt(qs, kc), NEG)
                m = jnp.maximum(jnp.max(sp, axis=1, keepdims=True), jnp.max(sc, axis=1, keepdims=True))
                pp = jnp.exp(sp - m)
                pc = jnp.exp(sc - m)
                den = jnp.sum(pp, axis=1, keepdims=True) + jnp.sum(pc, axis=1, keepdims=True)
                o = (_nn(pp.astype(BF16), vp) + _nn(pc.astype(BF16), vc)) / den
                lse = m + jnp.log(den)
                o_ref[0, rows, sl] = jnp.where(head0, o[:ATT_BLOCK], o[ATT_BLOCK:]).astype(BF16)
                lse_ref[0, rows, sl] = jnp.where(head0, lse[:ATT_BLOCK], lse[ATT_BLOCK:])

    cur = pl.BlockSpec((1, nsub * ATT_BLOCK, W), lambda r, n: (r, n, 0))
    prev = pl.BlockSpec((1, ATT_BLOCK, W), lambda r, n: (r, jnp.maximum(nsub * n - 1, 0), 0))
    return pl.pallas_call(
        body, name=name, grid=(d, nb // nsub),
        in_specs=[cur, prev, cur, prev, cur],
        out_specs=[cur, cur],
        out_shape=[jax.ShapeDtypeStruct((d, L, W), BF16), jax.ShapeDtypeStruct((d, L, W), F32)],
        compiler_params=_params(("parallel", "parallel")),
    )(q, k, k, v, v)


def attn_combine(os_, lses, name):
    T = os_[0].shape[0] * os_[0].shape[1]
    W = os_[0].shape[2]
    tm = LAYOUT_ROWS
    npat = len(os_)

    def body(*refs):
        o_refs, l_refs = refs[:npat], refs[npat:2 * npat]
        out_ref = refs[2 * npat]
        ores, lres = refs[2 * npat + 1:3 * npat + 1], refs[3 * npat + 1:4 * npat + 1]
        bufs = refs[4 * npat + 1:]
        lbufs, obufs, out_buf, lse_buf = bufs[:npat], bufs[npat:2 * npat], bufs[2 * npat], bufs[2 * npat + 1]
        for p, d in enumerate(DILATIONS):
            _from_res(l_refs[p], lbufs[p], d)
            _from_res(o_refs[p], obufs[p], d)
        for hp in range(W // HEAD_LANES):
            ls = [b[hp] for b in lbufs]
            m = ls[0]
            for l in ls[1:]:
                m = jnp.maximum(m, l)
            es = [jnp.exp(l - m) for l in ls]
            z = es[0]
            for e in es[1:]:
                z = z + e
            acc = es[0] * obufs[0][hp]
            for p in range(1, npat):
                acc = acc + es[p] * obufs[p][hp]
            out = acc / z
            out_ref[:, _lanes(hp)] = out.astype(BF16)
            out_buf[hp] = out
            lse_buf[hp] = m + jnp.log(z)
        for p, d in enumerate(DILATIONS):
            _to_res(out_buf, ores[p], d, BF16)
            _to_res(lse_buf, lres[p], d, F32)

    return pl.pallas_call(
        body, name=name, grid=(T // tm,),
        in_specs=[_res_spec(d, tm, W) for _ in range(2) for d in DILATIONS],
        out_specs=([pl.BlockSpec((tm, W), lambda i: (i, 0))] + [_res_spec(d, tm, W) for _ in range(2) for d in DILATIONS]),
        out_shape=([jax.ShapeDtypeStruct((T, W), BF16)] + [_res_shape(d, T, W, BF16) for d in DILATIONS]
                   + [_res_shape(d, T, W, F32) for d in DILATIONS]),
        scratch_shapes=[_slab_buf(tm, W)] * (2 * npat + 2),
        compiler_params=_params(("parallel",)),
    )(*os_, *lses)


def attn_bwd(q, k, v, do, o, lse, name):
    d, L, W = q.shape
    nb = L // ATT_BLOCK
    nsub = 2 if nb % 2 == 0 else 1
    per_seq = nb // nsub
    nst = d * per_seq
    nb = d * nb
    last = slice((nsub - 1) * ATT_BLOCK, nsub * ATT_BLOCK)
    q, k, v, do, o, lse = (t.reshape(1, d * L, W) for t in (q, k, v, do, o, lse))

    def body(q_ref, kp_ref, kc_ref, vp_ref, vc_ref, do_ref, o_ref, lse_ref, dq_ref, dk_ref, dv_ref, kkeep, vkeep):
        step = pl.program_id(1)
        n = step % per_seq

        @pl.when(step == 0)
        def _():
            kkeep[...] = jnp.zeros_like(kkeep)
            vkeep[...] = jnp.zeros_like(vkeep)

        @pl.when(step < nst)
        def _():
            for hp in range(W // HEAD_LANES):
                sl = slice(hp * HEAD_LANES, (hp + 1) * HEAD_LANES)
                shares = []
                for u in range(nsub):
                    rows = slice(u * ATT_BLOCK, (u + 1) * ATT_BLOCK)
                    before = slice((u - 1) * ATT_BLOCK, u * ATT_BLOCK)
                    mask_p, mask_c, head, head0 = _band_masks(n if u == 0 else 1)
                    kp, vp = (kp_ref[0, :, sl], vp_ref[0, :, sl]) if u == 0 else (kc_ref[0, before, sl], vc_ref[0, before, sl])
                    kc, vc = kc_ref[0, rows, sl], vc_ref[0, rows, sl]
                    dout = do_ref[0, rows, sl]
                    qs = _stack_heads(q_ref[0, rows, sl], head)
                    dos = _stack_heads(dout, head)
                    lse_v = lse_ref[0, rows, sl]
                    lse_c = jnp.max(jnp.where(head, jnp.concatenate([lse_v, lse_v], axis=0), NEG), axis=1, keepdims=True)
                    delta = jnp.sum(_stack_heads(dout.astype(F32) * o_ref[0, rows, sl].astype(F32), head), axis=1,
                                    keepdims=True)
                    pp = jnp.exp(jnp.where(mask_p, _nt(qs, kp), NEG) - lse_c)
                    pc = jnp.exp(jnp.where(mask_c, _nt(qs, kc), NEG) - lse_c)
                    dsp = (pp * (_nt(dos, vp) - delta)).astype(BF16)
                    dsc = (pc * (_nt(dos, vc) - delta)).astype(BF16)
                    dq2 = _nn(dsp, kp) + _nn(dsc, kc)
                    dq_ref[0, rows, sl] = jnp.where(head0, dq2[:ATT_BLOCK], dq2[ATT_BLOCK:]).astype(BF16)
                    shares.append((_tn(dsp, qs), _tn(pp.astype(BF16), dos), _tn(dsc, qs), _tn(pc.astype(BF16), dos)))
                dk_ref[0, last, sl] = (kkeep[last, sl] + shares[0][0]).astype(BF16)
                dv_ref[0, last, sl] = (vkeep[last, sl] + shares[0][1]).astype(BF16)
                if nsub == 2:
                    dk_ref[0, :ATT_BLOCK, sl] = kkeep[:ATT_BLOCK, sl].astype(BF16)
                    dv_ref[0, :ATT_BLOCK, sl] = vkeep[:ATT_BLOCK, sl].astype(BF16)
                    kkeep[:ATT_BLOCK, sl] = shares[0][2] + shares[1][0]
                    vkeep[:ATT_BLOCK, sl] = shares[0][3] + shares[1][1]
                kkeep[last, sl] = shares[-1][2]
                vkeep[last, sl] = shares[-1][3]

        @pl.when(step == nst)
        def _():
            dk_ref[0] = kkeep[...].astype(BF16)
            dv_ref[0] = vkeep[...].astype(BF16)

    rows_per_step = nsub * ATT_BLOCK
    cur = pl.BlockSpec((1, rows_per_step, W), lambda r, n: (r, jnp.minimum(n, nst - 1), 0))
    lag = pl.BlockSpec((1, rows_per_step, W), lambda r, n: (r, jnp.clip(n - 1, 0, nst - 1), 0))
    prev = pl.BlockSpec((1, ATT_BLOCK, W), lambda r, n: (r, jnp.clip(nsub * n - 1, 0, nb - 1), 0))
    out = jax.ShapeDtypeStruct((1, d * L, W), BF16)
    outs = pl.pallas_call(
        body, name=name, grid=(1, nst + 1),
        in_specs=[cur, prev, cur, prev, cur, cur, cur, cur],
        out_specs=[cur, lag, lag], out_shape=[out, out, out],
        scratch_shapes=[pltpu.VMEM((rows_per_step, W), F32), pltpu.VMEM((rows_per_step, W), F32)],
        compiler_params=_params(("parallel", "arbitrary")),
    )(q, k, k, v, v, do, o, lse)
    return [t.reshape(d, L, W) for t in outs]


def final_loss_bwd(h, gf, tgt, gate, name):
    T, D = h.shape
    tm = _tile(T, 512, 16)
    g_in, g_out, g_shape, g_ops = _gate_specs(gate, tm, D)

    def body(h_ref, g_ref, t_ref, o_ref, modg_ref, dh_ref, red_ref, do_ref, redg_ref):
        x = h_ref[...]
        r = lax.rsqrt(jnp.mean(x * x, axis=-1, keepdims=True) + EPS)
        n = x * r
        g = g_ref[...]
        err = n * g - t_ref[...]
        dy = err * (1.0 / D)

        @pl.when(pl.program_id(0) == 0)
        def _():
            red_ref[...] = jnp.zeros_like(red_ref)

        red_ref[0:1, :] += jnp.sum(dy * n, axis=0, keepdims=True)
        red_ref[1:2, :] += jnp.zeros((1, D), F32) + (0.5 / D) * jnp.sum(err * err, keepdims=True)
        dn = dy * g
        dh = r * (dn - n * jnp.mean(dn * n, axis=-1, keepdims=True))
        dh_ref[...] = dh
        _gate_emit(dh, gate, o_ref, modg_ref, do_ref, redg_ref)

    return pl.pallas_call(
        body, name=name, grid=(T // tm,),
        in_specs=[pl.BlockSpec((tm, D), lambda i: (i, 0)),
                  pl.BlockSpec((1, D), lambda i: (0, 0)),
                  pl.BlockSpec((tm, D), lambda i: (i, 0))] + g_in,
        out_specs=[pl.BlockSpec((tm, D), lambda i: (i, 0)), pl.BlockSpec((8, D), lambda i: (0, 0))] + g_out,
        out_shape=[jax.ShapeDtypeStruct((T, D), F32), jax.ShapeDtypeStruct((8, D), F32)] + g_shape,
        compiler_params=_params(("arbitrary",)),
    )(h, gf, tgt, *g_ops)


def ada_fwd(c_all, ada_w, ada_b, name):
    nl, D, N = ada_w.shape

    def body(c_ref, w_ref, b_ref, o_ref):
        c = c_ref[...]
        o_ref[0] = _nn(c * _sigmoid(c), w_ref[0]) + b_ref[0]

    return pl.pallas_call(
        body, name=name, grid=(nl,),
        in_specs=[pl.BlockSpec((N_DEV, D), lambda l: (0, 0)),
                  pl.BlockSpec((1, D, N), lambda l: (l, 0, 0)),
                  pl.BlockSpec((1, 1, N), lambda l: (l, 0, 0))],
        out_specs=pl.BlockSpec((1, N_DEV, N), lambda l: (l, 0, 0)),
        out_shape=jax.ShapeDtypeStruct((nl, N_DEV, N), F32),
        compiler_params=_params(("parallel",)),
    )(c_all, ada_w, ada_b)


def ada_bwd(c_allT, dmod, name):
    nl, _, N = dmod.shape
    D = c_allT.shape[0]

    def body(c_ref, g_ref, o_ref):
        c = c_ref[...]
        ca = c * _sigmoid(c)
        acc = ca[:, 0:1] * g_ref[0, 0:1, :]
        for b in range(1, N_DEV):
            acc = acc + ca[:, b:b + 1] * g_ref[0, b:b + 1, :]
        o_ref[0] = acc

    return pl.pallas_call(
        body, name=name, grid=(nl,),
        in_specs=[pl.BlockSpec((D, N_DEV), lambda l: (0, 0)),
                  pl.BlockSpec((1, N_DEV, N), lambda l: (l, 0, 0))],
        out_specs=pl.BlockSpec((1, D, N), lambda l: (l, 0, 0)),
        out_shape=jax.ShapeDtypeStruct((nl, D, N), F32),
        compiler_params=_params(("parallel",)),
    )(c_allT, dmod)


def adamw(w, g, m, v, name):
    R, C = w.shape
    tr = _tile(R, max(8, (1 << 19) // C // 8 * 8), 8)
    c1 = 1.0 - ADAM_B1 ** ADAM_STEP
    c2 = 1.0 - ADAM_B2 ** ADAM_STEP

    def body(w_ref, g_ref, m_ref, v_ref, d_ref, mo_ref, vo_ref):
        gv = g_ref[...]
        mn = ADAM_B1 * m_ref[...] + (1.0 - ADAM_B1) * gv
        vn = ADAM_B2 * v_ref[...] + (1.0 - ADAM_B2) * (gv * gv)
        mo_ref[...] = mn
        vo_ref[...] = vn
        d_ref[...] = -ADAM_LR * ((mn / c1) / (jnp.sqrt(vn / c2) + ADAM_EPS) + ADAM_WD * w_ref[...])

    blk = pl.BlockSpec((tr, C), lambda i: (i, 0))
    out = jax.ShapeDtypeStruct((R, C), F32)
    return pl.pallas_call(
        body, name=name, grid=(R // tr,),
        in_specs=[blk] * 4, out_specs=[blk] * 3, out_shape=[out] * 3,
        compiler_params=_params(("parallel",)),
    )(w, g, m, v)


def adamw_layer(w, g, m, v, l, prev, name):
    NLw, R, C = w.shape
    tr = _tile(R, max(8, (1 << 19) // C // 8 * 8), 8)
    nrb = R // tr
    c1 = 1.0 - ADAM_B1 ** ADAM_STEP
    c2 = 1.0 - ADAM_B2 ** ADAM_STEP
    w, m, v = (t.reshape(NLw * R, C) for t in (w, m, v))

    def body(w_ref, g_ref, m_ref, v_ref, *rest):
        go_ref, d_ref, mo_ref, vo_ref = rest[-4:]
        gv = g_ref[...]
        mn = ADAM_B1 * m_ref[...] + (1.0 - ADAM_B1) * gv
        vn = ADAM_B2 * v_ref[...] + (1.0 - ADAM_B2) * (gv * gv)
        go_ref[...] = gv
        mo_ref[...] = mn
        vo_ref[...] = vn
        d_ref[...] = -ADAM_LR * ((mn / c1) / (jnp.sqrt(vn / c2) + ADAM_EPS) + ADAM_WD * w_ref[...])

    lay = pl.BlockSpec((tr, C), lambda i: (l * nrb + i, 0))
    out = jax.ShapeDtypeStruct((NLw * R, C), F32)
    n_prev = 0 if prev is None else 4
    return pl.pallas_call(
        body, name=name, grid=(nrb,),
        in_specs=[lay, pl.BlockSpec((tr, C), lambda i: (i, 0)), lay, lay] + [pl.BlockSpec(memory_space=pl.ANY)] * n_prev,
        out_specs=[lay] * 4, out_shape=[out] * 4,
        input_output_aliases={4 + i: i for i in range(n_prev)},
        compiler_params=_params(("parallel",)),
    )(w, g, m, v, *(prev or ()))


def sum_slots(x, name):
    S, R, C = x.shape
    tr = _tile(R, 128, 8)

    def body(x_ref, o_ref):
        acc = x_ref[0]
        for s in range(1, S):
            acc = acc + x_ref[s]
        o_ref[...] = acc

    return pl.pallas_call(
        body, name=name, grid=(R // tr,),
        in_specs=[pl.BlockSpec((S, tr, C), lambda i: (0, i, 0))],
        out_specs=pl.BlockSpec((tr, C), lambda i: (i, 0)),
        out_shape=jax.ShapeDtypeStruct((R, C), F32),
        compiler_params=_params(("parallel",)),
    )(x)


def sum_halves(g, lands, c_idx, name):
    n, ns, _, rh, D = g.shape

    def body(c_ref, g_ref, l_ref, o_ref):
        for j in range(ns):
            o_ref[0, j] = (g_ref[0, j, 0].astype(F32) + l_ref[0, j].astype(F32)).astype(BF16)

    return pl.pallas_call(
        body, name=name,
        grid_spec=pltpu.PrefetchScalarGridSpec(
            num_scalar_prefetch=1, grid=(n,),
            in_specs=[pl.BlockSpec((1, ns, 1, rh, D), lambda i, c: (i, 0, c[0], 0, 0)),
                      pl.BlockSpec((1, ns, rh, D), lambda i, c: (i, 0, 0, 0))],
            out_specs=pl.BlockSpec((1, ns, rh, D), lambda i, c: (i, 0, 0, 0))),
        out_shape=jax.ShapeDtypeStruct((n, ns, rh, D), BF16),
        compiler_params=_params(("parallel",)),
    )(c_idx, g, lands)


def sum_chips(p, lands, place, name):
    n, ns, rh, D = p.shape

    def body(c_ref, p_ref, l_ref, o_ref):
        acc = p_ref[0, 0].astype(F32)
        for j in range(N_CHIP - 1):
            acc = acc + l_ref[j, 0].astype(F32)
        o_ref[0, 0] = acc

    return pl.pallas_call(
        body, name=name,
        grid_spec=pltpu.PrefetchScalarGridSpec(
            num_scalar_prefetch=1, grid=(n,),
            in_specs=[pl.BlockSpec((1, 1, rh, D), lambda i, c: (i, c[0], 0, 0)),
                      pl.BlockSpec((N_CHIP - 1, 1, rh, D), lambda i, c: (0, i, 0, 0))],
            out_specs=pl.BlockSpec((1, 1, rh, D), lambda i, c: (i, c[1], 0, 0))),
        out_shape=jax.ShapeDtypeStruct((n, 2, rh, D), F32),
        compiler_params=_params(("parallel",)),
    )(place, p, lands)


def _my_place():
    return lax.axis_index("x"), lax.axis_index("y"), lax.axis_index("c")


def _other_chips(mx, my):
    return [(1 - mx, my), (mx, 1 - my), (1 - mx, 1 - my)]


def gather_small(x, after, name):
    def body(x_ref, after_ref, out_ref, sum_ref, send_sems, recv_sems):
        mx, my, mc = _my_place()
        me = 4 * mx + 2 * my + mc
        out_ref[me] = x_ref[...]
        sends = []
        for k in range(1, N_DEV):
            kx, ky, kc = (k >> 2) & 1, (k >> 1) & 1, k & 1
            peer = (1 - mx if kx else mx, 1 - my if ky else my, 1 - mc if kc else mc)
            cp = pltpu.make_async_remote_copy(
                src_ref=x_ref, dst_ref=out_ref.at[me], send_sem=send_sems.at[k - 1], recv_sem=recv_sems.at[k - 1],
                device_id=peer, device_id_type=MESH)
            cp.start()
            sends.append((cp, 4 * peer[0] + 2 * peer[1] + peer[2], peer))
        for k, (cp, peer_slot, peer) in enumerate(sends):
            pltpu.make_async_remote_copy(
                src_ref=x_ref, dst_ref=out_ref.at[peer_slot], send_sem=send_sems.at[k], recv_sem=recv_sems.at[k],
                device_id=peer, device_id_type=MESH).wait_recv()
        for cp, _, _ in sends:
            cp.wait_send()
        acc = out_ref[0]
        for s in range(1, N_DEV):
            acc = acc + out_ref[s]
        sum_ref[...] = acc

    vmem = pl.BlockSpec(memory_space=pltpu.VMEM)
    return pl.pallas_call(
        body, name=name,
        in_specs=[vmem, pl.BlockSpec(memory_space=pl.ANY)], out_specs=[vmem, vmem],
        out_shape=[jax.ShapeDtypeStruct((N_DEV,) + x.shape, x.dtype), jax.ShapeDtypeStruct(x.shape, x.dtype)],
        scratch_shapes=[pltpu.SemaphoreType.DMA((N_DEV - 1,)), pltpu.SemaphoreType.DMA((N_DEV - 1,))],
        compiler_params=pltpu.CompilerParams(vmem_limit_bytes=VMEM_LIMIT),
    )(x, after)


_HBM =pl.BlockSpec(memory_space=pltpu.HBM)
_SEM = pl.BlockSpec(memory_space=pltpu.SEMAPHORE)
_DATAFLOW = pltpu.SideEffectType.DATAFLOW_SIDE_EFFECTING


def _gather_copies(shard, land, send, recv, base):
    mx, my, mc = _my_place()
    ci = 2 * mx + my
    peers = [((cx, cy, mc), 2 * cx + cy) for cx, cy in _other_chips(mx, my)] + [((mx, my, 1 - mc), ci)]
    out = []
    for q, (dev, src_slot) in enumerate(peers):
        out.append((
            pltpu.make_async_remote_copy(src_ref=shard, dst_ref=land.at[:, ci], send_sem=send.at[base + q],
                                         recv_sem=recv.at[base + q], device_id=dev, device_id_type=MESH),
            pltpu.make_async_remote_copy(src_ref=shard, dst_ref=land.at[:, src_slot], send_sem=send.at[base + q],
                                         recv_sem=recv.at[base + q], device_id=dev, device_id_type=MESH)))
    return out


def gather_start(groups, after, name):
    items = [s for g in groups for s in g]
    ni, ng = len(items), len(groups)

    def body(*refs):
        shards, lands = refs[:ni], refs[ni:2 * ni]
        sems = refs[2 * ni + 1:2 * ni + 1 + 2 * ng]
        token = refs[-1]
        i = 0
        for g, grp in enumerate(groups):
            for p in range(len(grp)):
                for start_cp, _ in _gather_copies(shards[i], lands[i], sems[2 * g], sems[2 * g + 1], 4 * p):
                    start_cp.start()
                i += 1
        token[...] = jnp.zeros_like(token)

    sem_shapes = []
    for grp in groups:
        sem_shapes += [pltpu.SemaphoreType.DMA((4 * len(grp),))] * 2
    land_shapes = [(s.shape[0], N_CHIP) + s.shape[1:] for s in items]
    outs = pl.pallas_call(
        body, name=name,
        in_specs=[_HBM] * (2 * ni) + [pl.BlockSpec(memory_space=pl.ANY)],
        out_specs=[_SEM] * (2 * ng) + [_HBM] * (2 * ni) + [pl.BlockSpec(memory_space=pltpu.VMEM)],
        out_shape=(sem_shapes + [pltpu.HBM(s.shape, s.dtype) for s in items]
                   + [pltpu.HBM(ls, s.dtype) for ls, s in zip(land_shapes, items)]
                   + [jax.ShapeDtypeStruct((8, 128), F32)]),
        input_output_aliases={i: 2 * ng + i for i in range(2 * ni)},
        compiler_params=pltpu.CompilerParams(has_side_effects=_DATAFLOW),
    )(*[pltpu.with_memory_space_constraint(s, pltpu.HBM) for s in items],
      *[pltpu.with_memory_space_constraint(lax.empty(ls, s.dtype), pltpu.HBM) for ls, s in zip(land_shapes, items)],
      after)
    sems, thru, token = outs[:2 * ng], outs[2 * ng:2 * ng + 2 * ni], outs[-1]
    handles, i = [], 0
    for g, grp in enumerate(groups):
        n = len(grp)
        handles.append((sems[2 * g], sems[2 * g + 1], thru[i:i + n], thru[ni + i:ni + i + n]))
        i += n
    return handles, token


def gather_wait(handle, after, name):
    send, recv, shards, lands = handle
    n = len(shards)

    def body(*refs):
        shard_refs, land_refs = refs[:n], refs[n:2 * n]
        send_ref, recv_ref = refs[2 * n], refs[2 * n + 1]
        for p in range(n):
            for start_cp, recv_cp in _gather_copies(shard_refs[p], land_refs[p], send_ref, recv_ref, 4 * p):
                start_cp.wait_send()
                recv_cp.wait_recv()

    outs = pl.pallas_call(
        body, name=name,
        in_specs=[_HBM] * (2 * n) + [_SEM, _SEM, pl.BlockSpec(memory_space=pl.ANY)],
        out_specs=[_HBM] * (2 * n),
        out_shape=[pltpu.HBM(s.shape, s.dtype) for s in shards] + [pltpu.HBM(l.shape, l.dtype) for l in lands],
        input_output_aliases={i: i for i in range(2 * n)},
        compiler_params=pltpu.CompilerParams(has_side_effects=_DATAFLOW),
    )(*shards, *lands, send, recv, after)
    return outs[n:]


def _first_copies(shard, land, send, recv):
    mx, my, mc = _my_place()
    ci = 2 * mx + my
    out = []
    for q, (cx, cy) in enumerate(_other_chips(mx, my)):
        dev = (cx, cy, mc)
        out.append(tuple(pltpu.make_async_remote_copy(
            src_ref=shard.at[:, mc], dst_ref=land.at[:, slot, mc], send_sem=send.at[q], recv_sem=recv.at[q],
            device_id=dev, device_id_type=MESH) for slot in (ci, 2 * cx + cy)))
    sib = pltpu.make_async_remote_copy(src_ref=shard, dst_ref=land.at[:, ci], send_sem=send.at[3], recv_sem=recv.at[3],
                                       device_id=(mx, my, 1 - mc), device_id_type=MESH)
    return out + [(sib, sib)]


def _forward_copies(land, send, recv):
    mx, my, mc = _my_place()
    out = []
    for q, (cx, cy) in enumerate(_other_chips(mx, my)):
        out.append(tuple(pltpu.make_async_remote_copy(
            src_ref=land.at[:, 2 * cx + cy, hc], dst_ref=land.at[:, 2 * cx + cy, hc], send_sem=send.at[q],
            recv_sem=recv.at[q], device_id=(mx, my, 1 - mc), device_id_type=MESH) for hc in (mc, 1 - mc)))
    return out


def first_start(shard, after, name):
    def body(shard_ref, land_ref, after_ref, send, recv, shard_thru, land_thru, token):
        for mine, _ in _first_copies(shard_ref, land_ref, send, recv):
            mine.start()
        token[...] = jnp.zeros_like(token)

    land_shape = (shard.shape[0], N_CHIP) + shard.shape[1:]
    outs = pl.pallas_call(
        body, name=name,
        in_specs=[_HBM, _HBM, pl.BlockSpec(memory_space=pl.ANY)],
        out_specs=[_SEM, _SEM, _HBM, _HBM, pl.BlockSpec(memory_space=pltpu.VMEM)],
        out_shape=[pltpu.SemaphoreType.DMA((4,))] * 2 + [pltpu.HBM(shard.shape, shard.dtype),
                                                         pltpu.HBM(land_shape, shard.dtype),
                                                         jax.ShapeDtypeStruct((8, 128), F32)],
        input_output_aliases={0: 2, 1: 3},
        compiler_params=pltpu.CompilerParams(has_side_effects=_DATAFLOW),
    )(pltpu.with_memory_space_constraint(shard, pltpu.HBM),
      pltpu.with_memory_space_constraint(lax.empty(land_shape, shard.dtype), pltpu.HBM), after)
    return outs[:4], outs[4]


def first_forward(handle, after, name):
    send, recv, shard, land = handle

    def body(shard_ref, land_ref, send_ref, recv_ref, after_ref, send2, recv2, shard_thru, land_thru):
        firsts = _first_copies(shard_ref, land_ref, send_ref, recv_ref)
        forwards = _forward_copies(land_ref, send2, recv2)
        for q in range(3):
            firsts[q][1].wait_recv()
            forwards[q][0].start()
        firsts[3][1].wait_recv()
        for mine, _ in firsts:
            mine.wait_send()

    outs = pl.pallas_call(
        body, name=name,
        in_specs=[_HBM, _HBM, _SEM, _SEM, pl.BlockSpec(memory_space=pl.ANY)],
        out_specs=[_SEM, _SEM, _HBM, _HBM],
        out_shape=[pltpu.SemaphoreType.DMA((3,))] * 2 + [pltpu.HBM(shard.shape, shard.dtype),
                                                         pltpu.HBM(land.shape, land.dtype)],
        input_output_aliases={0: 2, 1: 3},
        compiler_params=pltpu.CompilerParams(has_side_effects=_DATAFLOW),
    )(shard, land, send, recv, after)
    return outs[0], outs[1], outs[3]


def first_wait(handle, after, name):
    send, recv, land = handle

    def body(land_ref, send_ref, recv_ref, after_ref, land_out):
        for mine, theirs in _forward_copies(land_ref, send_ref, recv_ref):
            mine.wait_send()
            theirs.wait_recv()

    return pl.pallas_call(
        body, name=name,
        in_specs=[_HBM, _SEM, _SEM, pl.BlockSpec(memory_space=pl.ANY)],
        out_specs=[_HBM],
        out_shape=[pltpu.HBM(land.shape, land.dtype)],
        input_output_aliases={0: 0},
        compiler_params=pltpu.CompilerParams(has_side_effects=_DATAFLOW),
    )(land, send, recv, after)[0]


def _sibling_copies(gs, lands, send, recv):
    mx, my, mc = _my_place()
    return [pltpu.make_async_remote_copy(
        src_ref=gs[k].at[:, :, 1 - mc], dst_ref=lands[k], send_sem=send.at[k], recv_sem=recv.at[k],
        device_id=(mx, my, 1 - mc), device_id_type=MESH) for k in range(len(gs))]


def sibling_start(gs, after, name):
    K = len(gs)

    def body(*refs):
        ins, lands = refs[:K], refs[K:2 * K]
        send, recv = refs[2 * K + 1], refs[2 * K + 2]
        for cp in _sibling_copies(ins, lands, send, recv):
            cp.start()
        refs[-1][...] = jnp.zeros_like(refs[-1])

    land_shapes = [g.shape[:2] + g.shape[3:] for g in gs]
    outs = pl.pallas_call(
        body, name=name,
        in_specs=[_HBM] * (2 * K) + [pl.BlockSpec(memory_space=pl.ANY)],
        out_specs=[_SEM, _SEM] + [_HBM] * (2 * K) + [pl.BlockSpec(memory_space=pltpu.VMEM)],
        out_shape=([pltpu.SemaphoreType.DMA((K,))] * 2 + [pltpu.HBM(g.shape, g.dtype) for g in gs]
                   + [pltpu.HBM(ls, g.dtype) for ls, g in zip(land_shapes, gs)] + [jax.ShapeDtypeStruct((8, 128), F32)]),
        input_output_aliases={i: 2 + i for i in range(2 * K)},
        compiler_params=pltpu.CompilerParams(has_side_effects=_DATAFLOW),
    )(*[pltpu.with_memory_space_constraint(g, pltpu.HBM) for g in gs],
      *[pltpu.with_memory_space_constraint(lax.empty(ls, g.dtype), pltpu.HBM) for ls, g in zip(land_shapes, gs)],
      after)
    return (outs[0], outs[1], outs[2:2 + K], outs[2 + K:2 + 2 * K]), outs[-1]


def sibling_wait(handle, after, name):
    send, recv, gs, lands = handle
    K = len(gs)

    def body(*refs):
        ins, land_refs = refs[:K], refs[K:2 * K]
        for cp in _sibling_copies(ins, land_refs, refs[2 * K], refs[2 * K + 1]):
            cp.wait_send()
            cp.wait_recv()

    outs = pl.pallas_call(
        body, name=name,
        in_specs=[_HBM] * (2 * K) + [_SEM, _SEM, pl.BlockSpec(memory_space=pl.ANY)],
        out_specs=[_HBM] * (2 * K),
        out_shape=[pltpu.HBM(g.shape, g.dtype) for g in gs] + [pltpu.HBM(l.shape, l.dtype) for l in lands],
        input_output_aliases={i: i for i in range(2 * K)},
        compiler_params=pltpu.CompilerParams(has_side_effects=_DATAFLOW),
    )(*gs, *lands, send, recv, after)
    return outs[:K], outs[K:]


def _small_copies(x, land, send, recv):
    mx, my, mc = _my_place()
    me = 4 * mx + 2 * my + mc
    out = []
    for k in range(1, N_DEV):
        peer = (1 - mx if k & 4 else mx, 1 - my if k & 2 else my, 1 - mc if k & 1 else mc)
        slot = 4 * peer[0] + 2 * peer[1] + peer[2]
        out.append(tuple(pltpu.make_async_remote_copy(
            src_ref=x, dst_ref=land.at[s], send_sem=send.at[k - 1], recv_sem=recv.at[k - 1],
            device_id=peer, device_id_type=MESH) for s in (me, slot)))
    return out


def small_start(x, after, name):
    def body(x_ref, land_ref, after_ref, send, recv, x_thru, land_thru, token):
        for mine, _ in _small_copies(x_ref, land_ref, send, recv):
            mine.start()
        token[...] = jnp.zeros_like(token)

    land_shape = (N_DEV,) + x.shape
    outs = pl.pallas_call(
        body, name=name,
        in_specs=[_HBM, _HBM, pl.BlockSpec(memory_space=pl.ANY)],
        out_specs=[_SEM, _SEM, _HBM, _HBM, pl.BlockSpec(memory_space=pltpu.VMEM)],
        out_shape=[pltpu.SemaphoreType.DMA((N_DEV - 1,))] * 2 + [pltpu.HBM(x.shape, x.dtype), pltpu.HBM(land_shape, x.dtype),
                                                                 jax.ShapeDtypeStruct((8, 128), F32)],
        input_output_aliases={0: 2, 1: 3},
        compiler_params=pltpu.CompilerParams(has_side_effects=_DATAFLOW),
    )(pltpu.with_memory_space_constraint(x, pltpu.HBM),
      pltpu.with_memory_space_constraint(lax.empty(land_shape, x.dtype), pltpu.HBM), after)
    return outs[:4], outs[4]


def small_wait(handle, after, name):
    send, recv, x, land = handle

    def body(x_ref, land_ref, send_ref, recv_ref, after_ref, x_out, land_out):
        for mine, theirs in _small_copies(x_ref, land_ref, send_ref, recv_ref):
            mine.wait_send()
            theirs.wait_recv()

    return pl.pallas_call(
        body, name=name,
        in_specs=[_HBM, _HBM, _SEM, _SEM, pl.BlockSpec(memory_space=pl.ANY)],
        out_specs=[_HBM, _HBM],
        out_shape=[pltpu.HBM(x.shape, x.dtype), pltpu.HBM(land.shape, land.dtype)],
        input_output_aliases={0: 0, 1: 1},
        compiler_params=pltpu.CompilerParams(has_side_effects=_DATAFLOW),
    )(x, land, send, recv, after)


def _scatter_copies(ps, lands, send, recv):
    mx, my, mc = _my_place()
    cps = []
    for j, (cx, cy) in enumerate(_other_chips(mx, my)):
        for k in range(len(ps)):
            cps.append(pltpu.make_async_remote_copy(
                src_ref=ps[k].at[:, 2 * cx + cy], dst_ref=lands[k].at[j],
                send_sem=send.at[k * 3 + j], recv_sem=recv.at[k * 3 + j],
                device_id=(cx, cy, mc), device_id_type=MESH))
    return cps


def scatter_start(ps, after, name):
    K = len(ps)

    def body(*refs):
        ins, lands = refs[:K], refs[K:2 * K]
        send, recv = refs[2 * K + 1], refs[2 * K + 2]
        for cp in _scatter_copies(ins, lands, send, recv):
            cp.start()
        refs[-1][...] = jnp.zeros_like(refs[-1])

    land_shapes = [(N_CHIP - 1, p.shape[0]) + p.shape[2:] for p in ps]
    outs = pl.pallas_call(
        body, name=name,
        in_specs=[_HBM] * (2 * K) + [pl.BlockSpec(memory_space=pl.ANY)],
        out_specs=[_SEM, _SEM] + [_HBM] * (2 * K) + [pl.BlockSpec(memory_space=pltpu.VMEM)],
        out_shape=([pltpu.SemaphoreType.DMA((3 * K,))] * 2 + [pltpu.HBM(p.shape, p.dtype) for p in ps]
                   + [pltpu.HBM(ls, p.dtype) for ls, p in zip(land_shapes, ps)] + [jax.ShapeDtypeStruct((8, 128), F32)]),
        input_output_aliases={i: 2 + i for i in range(2 * K)},
        compiler_params=pltpu.CompilerParams(has_side_effects=_DATAFLOW),
    )(*[pltpu.with_memory_space_constraint(p, pltpu.HBM) for p in ps],
      *[pltpu.with_memory_space_constraint(lax.empty(ls, p.dtype), pltpu.HBM) for ls, p in zip(land_shapes, ps)],
      after)
    return (outs[0], outs[1], outs[2:2 + K], outs[2 + K:2 + 2 * K]), outs[-1]


def scatter_wait(handle, after, name):
    send, recv, ps, lands = handle
    K = len(ps)
    afters = list(after) if isinstance(after, (list, tuple)) else [after]

    def body(*refs):
        ins, land_refs = refs[:K], refs[K:2 * K]
        send_ref, recv_ref = refs[2 * K], refs[2 * K + 1]
        for cp in _scatter_copies(ins, land_refs, send_ref, recv_ref):
            cp.wait_send()
            cp.wait_recv()

    outs = pl.pallas_call(
        body, name=name,
        in_specs=[_HBM] * (2 * K) + [_SEM, _SEM] + [pl.BlockSpec(memory_space=pl.ANY)] * len(afters),
        out_specs=[_HBM] * (2 * K),
        out_shape=[pltpu.HBM(p.shape, p.dtype) for p in ps] + [pltpu.HBM(l.shape, l.dtype) for l in lands],
        input_output_aliases={i: i for i in range(2 * K)},
        compiler_params=pltpu.CompilerParams(has_side_effects=_DATAFLOW),
    )(*ps, *lands, send, recv, *afters)
    return outs[:K], outs[K:]


def sibling_complete(ss, name):
    K = len(ss)

    def body(*refs):
        ins, outs = refs[:K], refs[K:2 * K]
        send, recv = refs[2 * K:]
        mx, my, mc = _my_place()
        cps = []
        for k in range(K):
            cp = pltpu.make_async_remote_copy(
                src_ref=ins[k].at[:, mc], dst_ref=outs[k].at[:, mc], send_sem=send.at[k], recv_sem=recv.at[k],
                device_id=(mx, my, 1 - mc), device_id_type=MESH)
            cp.start()
            cps.append(cp)
        for k in range(K):
            pltpu.make_async_remote_copy(
                src_ref=ins[k].at[:, mc], dst_ref=outs[k].at[:, 1 - mc], send_sem=send.at[k], recv_sem=recv.at[k],
                device_id=(mx, my, 1 - mc), device_id_type=MESH).wait_recv()
        for cp in cps:
            cp.wait_send()

    hbm = pl.BlockSpec(memory_space=pl.ANY)
    return pl.pallas_call(
        body, name=name,
        in_specs=[hbm] * K, out_specs=[hbm] * K,
        out_shape=[jax.ShapeDtypeStruct(s.shape, s.dtype) for s in ss],
        scratch_shapes=[pltpu.SemaphoreType.DMA((K,)), pltpu.SemaphoreType.DMA((K,))],
        input_output_aliases={k: k for k in range(K)},
    )(*ss)


def _rope_tables(T):
    inv = ROPE_THETA ** (-jnp.arange(0, ATT_DH, 2, dtype=F32) / ATT_DH)
    ang = jnp.arange(T, dtype=F32)[:, None] * inv[None, :]
    ang = jnp.concatenate([ang, ang, ang, ang], axis=-1)
    return jnp.cos(ang), jnp.sin(ang)


def _ffn_fwd(h, y, mod, i0, get_up, get_down, norm_next, tag):
    wgu = get_up(y)
    a, b, s = ffn_up(y, (wgu, (0,)), (wgu, (1,)), f"ffn_up_{tag}")
    wd = get_down(s)
    outs = resid_matmul([s], (wd, (0,)), h, mod, i0 + 2, 0.5, f"ffn_down_{tag}", norm_next)
    hn, o = outs[0], outs[1]
    return hn, (outs[2] if norm_next else None), (h, y, a, b, s, o), ((wgu, (0,)), (wgu, (1,)), (wd, (0,)))


def _ffn_bwd(dh, do, res, ng, i_n, mod, i0, wgT, wuT, wd, on_grads, next_gate, after, tag):
    h, y, a, b, s, o = res
    F = _wrows(wgT)
    da, db = ffn_bwd_mid(do, wd, a, b, f"ffn_bwd_mid_{tag}", after)
    gbuf = lax.empty((3, F, h.shape[1]), BF16)
    gbuf = matmul_tn(da, y, gbuf, 0, 0, f"dwg_{tag}")
    gbuf = matmul_tn(db, y, gbuf, 1, 0, f"dwu_{tag}")
    gbuf = matmul_tn(s, do, gbuf, 2, 0, f"dwd_{tag}")
    token, then = on_grads([gbuf])
    outs = dy_normbwd([(da, 0, wgT, 0, F), (db, 0, wuT, 0, F)], h, dh, ng, i_n, mod, i0 + 1,
                      f"ffn_bwd_dy_{tag}", next_gate, [token])
    return outs, then


def _mixer_fwd(h, y, mod, w_inT, w_out, sgu, cos, sin, norm_next, tag):
    lng, lnb, sw, swt, bcol = sgu
    proj = matmul_nt(y, w_inT, f"proj_{tag}")
    out_a = sgu_fwd(proj, lng, lnb, sw, bcol, f"sgu_fwd_{tag}")
    qkv = rope_fwd(proj, cos, sin, f"rope_fwd_{tag}")
    npat = len(DILATIONS)
    qkv_res = [tuple(qkv[3 * p:3 * p + 3]) for p in range(npat)]
    os_, lses = [], []
    for d, (qd, kd, vd) in zip(DILATIONS, qkv_res):
        o_d, lse_d = attn_fwd(qd, kd, vd, f"attn_fwd_d{d}_{tag}")
        os_.append(o_d)
        lses.append(lse_d)
    comb = attn_combine(os_, lses, f"attn_combine_{tag}")
    out_b, o_res, lse_res = comb[0], comb[1:1 + npat], comb[1 + npat:]
    outs = resid_matmul([out_a, out_b], w_out, h, mod, 5, 1.0, f"mix_out_{tag}", norm_next)
    hn, om = outs[0], outs[1]
    return hn, (outs[2] if norm_next else None), (h, y, proj, out_a, out_b, o_res, lse_res, qkv_res, om)


def _mixer_bwd(dh, dom, res, ng, mod, w_inT, w_out, sgu, cos, sin, on_grads, next_gate, after, tag):
    lng, lnb, sw, swt, bcol = sgu
    h, y, proj, out_a, out_b, o_res, lse_res, qkv_res, om = res
    D = h.shape[1]
    dmixed = matmul_nt(dom, w_out, f"dmixed_{tag}", after)
    woutbuf = lax.empty((1, 2 * MIX_HALF, D), BF16)
    woutbuf = matmul_tn(out_a, dom, woutbuf, 0, 0, f"dwout_a_{tag}", tmo_cap=MIX_HALF)
    woutbuf = matmul_tn(out_b, dom, woutbuf, 0, MIX_HALF, f"dwout_b_{tag}", tmo_cap=MIX_HALF)
    d_uv, d_sw, d_svec = sgu_bwd(proj, dmixed, lng, lnb, sw, swt, bcol, f"sgu_bwd_{tag}")
    do_res = to_residues(dmixed, 1, f"dout_res_{tag}")
    dqs, dks, dvs = [], [], []
    for p, (d, (qd, kd, vd)) in enumerate(zip(DILATIONS, qkv_res)):
        dq, dk, dv = attn_bwd(qd, kd, vd, do_res[p], o_res[p], lse_res[p], f"attn_bwd_d{d}_{tag}")
        dqs.append(dq)
        dks.append(dk)
        dvs.append(dv)
    d_qkv = rope_bwd(dqs, dks, dvs, cos, sin, f"rope_bwd_{tag}")
    winbuf = lax.empty((1, 5 * MIX_HALF, D), BF16)
    winbuf = matmul_tn(d_uv, y, winbuf, 0, 0, f"dwin_uv_{tag}", tmo_cap=MIX_HALF)
    winbuf = matmul_tn(d_qkv, y, winbuf, 0, 2 * MIX_HALF, f"dwin_qkv_{tag}", tmo_cap=MIX_HALF)
    token, then = on_grads([winbuf, woutbuf])
    pairs = [(d_uv, 0, w_inT, 0, 2 * MIX_HALF), (d_qkv, 0, w_inT, 1, 2 * MIX_HALF), (d_qkv, 2, w_inT, 4, MIX_HALF)]
    outs = dy_normbwd(pairs, h, dh, ng, 1, mod, 4, f"mix_bwd_dy_{tag}", next_gate, [token])
    return outs, d_sw, d_svec, then


def _local_step(x, tgt, mods, ngs, get_w, sgus, gf, on_block_grads, on_layer_small):
    T, D = x.shape
    cos, sin = _rope_tables(T)
    h = x
    saved, weights = [], []
    for l in range(2):
        def getter(blk, l=l):
            return lambda after: get_w(l, blk, after)

        if l == 0:
            y = normmod_fwd(h, ngs[0], 0, mods[0], 0, 1, "normmod_l0f1")
        h, y, r1, wf1 = _ffn_fwd(h, y, mods[l], 0, getter("f1u"), getter("f1d"), (ngs[l], 1, mods[l], 3, 4), f"l{l}f1")
        w_inT, w_out = get_w(l, "mx", h)
        h, y, r2 = _mixer_fwd(h, y, mods[l], (w_inT, (0,)), (w_out, (0,)), sgus[l], cos, sin,
                              (ngs[l], 2, mods[l], 6, 7), f"l{l}mx")
        h, y, r3, wf2 = _ffn_fwd(h, y, mods[l], 6, getter("f2u"), getter("f2d"),
                                 (ngs[l + 1], 0, mods[l + 1], 0, 1) if l + 1 < 2 else None, f"l{l}f2")
        saved.append((r1, r2, r3))
        weights.append((wf1, w_inT, w_out, wf2))
    def gate_of(l, blk):
        r1, r2, r3 = saved[l]
        o, i_g, coef = {"f2": (r3[5], 8, 0.5), "mx": (r2[-1], 5, 1.0), "f1": (r1[5], 2, 0.5)}[blk]
        return o, mods[l], i_g, coef

    seq = [(l, blk) for l in (1, 0) for blk in ("f2", "mx", "f1")]
    dh, red_final, do, red_g = final_loss_bwd(h, gf, tgt, gate_of(*seq[0]), "final_loss_bwd")
    rn, rg = {}, {}
    after = []
    for idx, (l, blk) in enumerate(seq):
        r1, r2, r3 = saved[l]
        wf1, w_inT, w_out, wf2 = weights[l]
        nxt = gate_of(*seq[idx + 1]) if idx + 1 < len(seq) else None
        rg[blk] = red_g
        tag = f"l{l}{blk}"

        def on(arrays, l=l, blk=blk):
            return on_block_grads(l, blk, arrays)

        if blk == "f2":
            outs, then = _ffn_bwd(dh, do, r3, ngs[l], 2, mods[l], 6, *wf2, on, nxt, after, tag)
        elif blk == "mx":
            outs, d_sw, d_svec, then = _mixer_bwd(dh, do, r2, ngs[l], mods[l], (w_inT, (0,)), (w_out, (0,)), sgus[l],
                                                  cos, sin, on, nxt, after, tag)
        else:
            outs, then = _ffn_bwd(dh, do, r1, ngs[l], 0, mods[l], 0, *wf1, on, nxt, after, tag)
        dh, rn[blk] = outs[0], outs[1]
        if nxt is not None:
            do, red_g = outs[2], outs[3]
        if blk == "f1":
            small = on_layer_small(l, dict(sgu_w=d_sw, sgu_vec=d_svec, red_n=(rn["f1"], rn["mx"], rn["f2"]),
                                           red_g=(rg["f1"], rg["mx"], rg["f2"])), red_final if l == 0 else None)
            after = [small, then(small)]
        else:
            after = [then(dh)]
    return dh


def _adam_out(w, g, m, v, name):
    shp = w.shape
    two_d = (-1, shp[-1])
    d, mn, vn = adamw(w.reshape(two_d), g.reshape(two_d), m.reshape(two_d), v.reshape(two_d), name)
    return g, d.reshape(shp), mn.reshape(shp), vn.reshape(shp)


def kernel(x, c, ada_w, ada_b, norm_g, ffn1_wg, ffn1_wu, ffn1_wd, ffn2_wg, ffn2_wu, ffn2_wd, w_in, sgu_ln_g, sgu_ln_b, sgu_w, sgu_b, w_out, final_g, loss_target, m_ada_w, m_ada_b, m_norm_g, m_ffn1_wg, m_ffn1_wu, m_ffn1_wd, m_ffn2_wg, m_ffn2_wu, m_ffn2_wd, m_w_in, m_sgu_ln_g, m_sgu_ln_b, m_sgu_w, m_sgu_b, m_w_out, m_final_g, v_ada_w, v_ada_b, v_norm_g, v_ffn1_wg, v_ffn1_wu, v_ffn1_wd, v_ffn2_wg, v_ffn2_wu, v_ffn2_wd, v_w_in, v_sgu_ln_g, v_sgu_ln_b, v_sgu_w, v_sgu_b, v_w_out, v_final_g):
    T, D = x.shape[1], x.shape[2]
    NL = ada_w.shape[0]
    mx, my, mc = _my_place()
    me = 4 * mx + 2 * my + mc
    ci = 2 * mx + my
    c_idx = jnp.reshape(mc, (1,)).astype(jnp.int32)
    place = jnp.stack([ci, mc]).astype(jnp.int32)

    ngw = norm_g.shape[2]
    small_in = jnp.concatenate([jnp.pad(c, ((0, 7), (0, 0))),
                                jnp.pad(norm_g.reshape(NL * 3, ngw), ((0, 8 - NL * 3), (0, D - ngw)))], axis=0)
    small_all, _ = gather_small(small_in, place, "gather_c_normg")
    c_all = small_all[:, 0, :]
    ng_parts = small_all[0::2, 8:8 + NL * 3, :ngw]
    ngs = jnp.transpose(ng_parts, (1, 0, 2)).reshape(NL, 3, N_CHIP * ngw)

    nmod = ada_w.shape[2]
    ada_b_mine = lax.dynamic_slice_in_dim(ada_b, ci * nmod, nmod, axis=1).reshape(NL, 1, nmod)
    mod_part = ada_fwd(c_all, ada_w, ada_b_mine, "ada_fwd")
    mod_all, _ = gather_small(mod_part.reshape(NL * N_DEV, nmod), place, "gather_mod")
    mod_rows = lax.dynamic_index_in_dim(mod_all.reshape(N_DEV, NL, N_DEV, nmod), me, axis=2, keepdims=False)
    mods = jnp.transpose(mod_rows[0::2], (1, 0, 2)).reshape(NL, N_ADA, D)

    sgus = []
    for l in range(NL):
        sgus.append((sgu_ln_g[l].reshape(1, MIX_HALF), sgu_ln_b[l].reshape(1, MIX_HALF), sgu_w[l],
                     jnp.swapaxes(sgu_w[l], 1, 2), jnp.transpose(sgu_b[l])))

    def halves(a):
        n, r, _ = a.shape
        return a.reshape(n, 2, r // 2, D)

    first_group = halves(jnp.stack([ffn1_wg[0].T, ffn1_wu[0].T], axis=0).astype(BF16))
    first_handle, first_token = first_start(first_group, mods, "first_start")
    zero = first_token[0, 0]
    mods = mods + zero

    def prep(a):
        return (a + zero).astype(BF16)

    groups = []
    for l in range(NL):
        groups += [[halves(jnp.stack([prep(ffn1_wg[l].T), prep(ffn1_wu[l].T)], axis=0))],
                   [halves(prep(ffn1_wd[l])[None])],
                   [halves(prep(w_in[l].T)[None]), halves(prep(w_out[l])[None])],
                   [halves(jnp.stack([prep(ffn2_wg[l].T), prep(ffn2_wu[l].T)], axis=0))],
                   [halves(prep(ffn2_wd[l])[None])]]
    handles, token = gather_start(groups[1:], mods, "gather_start")
    handles = [None] + handles
    mods = mods + token[0, 0]
    group_no = {"f1u": 0, "f1d": 1, "mx": 2, "f2u": 3, "f2d": 4}

    def get_w(l, key, after):
        g = len(group_no) * l + group_no[key]
        if g == 0:
            full = [first_wait(first_forward(first_handle, after, "first_forward"), place, "first_wait")]
        else:
            full = gather_wait(handles[g], after, f"gather_wait_l{l}{key}")
        full = [a.reshape(a.shape[0], N_CHIP * 2 * a.shape[3], D) for a in full]
        return full[0] if key != "mx" else tuple(full)

    def split(a):
        n, r4, _ = a.shape
        return a.reshape(n, N_CHIP, 2, r4 // N_CHIP // 2, D)

    pending, small_pending, small_tokens = {}, {}, {}

    def on_block_grads(l, blk, bufs):
        tag = f"l{l}{blk}"
        sib, tok1 = sibling_start([split(g) for g in bufs], place, f"rs_sibling_start_{tag}")

        def then(after):
            parts, lands = sibling_wait(sib, after, f"rs_sibling_wait_{tag}")
            psums = [sum_halves(g, ld, c_idx, f"rs_sum_halves_{tag}_{i}") for i, (g, ld) in enumerate(zip(parts, lands))]
            pending[(l, blk)], tok2 = scatter_start(psums, lands[0], f"rs_chips_start_{tag}")
            return tok2

        return tok1, then

    def blocks_finish(blocks, after, tag):
        ssums, counts = [], []
        for l, blk in blocks:
            psums, lands2 = scatter_wait(pending.pop((l, blk)), after, f"rs_chips_wait_l{l}{blk}")
            ssums += [sum_chips(p, ld, place, f"rs_sum_chips_l{l}{blk}_{i}") for i, (p, ld) in enumerate(zip(psums, lands2))]
            counts.append(len(psums))
        fins = [f.reshape(f.shape[0], -1, D) for f in sibling_complete(ssums, f"rs_complete_{tag}")]
        out, i = [], 0
        for n in counts:
            out.append(fins[i:i + n])
            i += n
        return out

    def on_layer_small(l, grads, red_final):
        blocks = list(grads["red_n"]) + list(grads["red_g"])
        blocks.append(jnp.pad(grads["sgu_vec"], ((0, 0), (0, D - MIX_HALF))))
        blocks.append(grads["sgu_w"].reshape(-1, D))
        if red_final is not None:
            blocks.append(red_final)
        xs = jnp.concatenate(blocks, axis=0)
        small_pending[l], small_tokens[l] = small_start(xs, place, f"small_start_l{l}")
        return small_tokens[l]

    grad_x = _local_step(x[0], loss_target[0], mods, ngs, get_w, sgus, final_g.reshape(1, D),
                         on_block_grads, on_layer_small)

    adam_state = {}

    def adam_big(nm, l, g, w, m, v):
        adam_state[nm] = adamw_layer(w, g, m, v, l, adam_state.get(nm), f"adamw_{nm}_l{l}")

    def adam_block(l, blk, fin):
        if blk == "mx":
            adam_big("w_in", l, fin[0][0].T, w_in, m_w_in, v_w_in)
            adam_big("w_out", l, fin[1][0], w_out, m_w_out, v_w_out)
        else:
            ws = ((ffn1_wg, m_ffn1_wg, v_ffn1_wg), (ffn1_wu, m_ffn1_wu, v_ffn1_wu), (ffn1_wd, m_ffn1_wd, v_ffn1_wd)) \
                if blk == "f1" else \
                ((ffn2_wg, m_ffn2_wg, v_ffn2_wg), (ffn2_wu, m_ffn2_wu, v_ffn2_wu), (ffn2_wd, m_ffn2_wd, v_ffn2_wd))
            pre = "ffn1" if blk == "f1" else "ffn2"
            for k, (nm, tr) in enumerate((("wg", True), ("wu", True), ("wd", False))):
                adam_big(f"{pre}_{nm}", l, fin[0][k], *[jnp.swapaxes(t, 1, 2) if tr else t for t in ws[k]])

    done_order = [(l, blk) for l in range(NL - 1, -1, -1) for blk in ("f2", "mx", "f1")]
    for (l, blk), fin in zip(done_order[:-1], blocks_finish(done_order[:-1], small_tokens[0], "early")):
        adam_block(l, blk, fin)
    last_big = adam_state["w_out"][1]

    small_sum, small_all = [], []
    for l in range(NL):
        xs, land = small_wait(small_pending[l], last_big, f"small_wait_l{l}")
        full = lax.dynamic_update_slice(land, xs[None], (me, 0, 0))
        small_all.append(full)
        small_sum.append(sum_slots(full, f"small_sum_l{l}"))
    offs = [8 * i for i in range(8)]
    off_final = offs[7] + SGU_HEADS * ATT_BLOCK * HEAD_LANES // D
    loss = small_sum[0][off_final + 1, 0]
    g_final_g = small_sum[0][off_final, :]
    g_norm_g, g_ada_b, g_lng, g_lnb, g_sb, g_sw, dmod_all = [], [], [], [], [], [], []
    for l in range(NL):
        rn = [small_sum[l][offs[i]:offs[i] + 8] for i in range(3)]
        rg = [small_sum[l][offs[3 + i]:offs[3 + i] + 8] for i in range(3)]
        g_norm_g.append(jnp.stack([rn[i][2] for i in range(3)], axis=0))
        g_ada_b.append(jnp.concatenate([jnp.stack([rn[i][0], rn[i][1], rg[i][0]], axis=0) for i in range(3)],
                                       axis=0).reshape(N_ADA * D))
        sv = small_sum[l][offs[6]:offs[6] + 8, :MIX_HALF]
        g_lng.append(sv[0].reshape(SGU_HEADS, HEAD_LANES))
        g_lnb.append(sv[1].reshape(SGU_HEADS, HEAD_LANES))
        g_sb.append(sv[2].reshape(SGU_HEADS, ATT_BLOCK))
        g_sw.append(small_sum[l][offs[7]:off_final].reshape(sgu_w.shape[1:]))
        rows = []
        for i in range(3):
            an = small_all[l][:, offs[i]:offs[i] + 2]
            ag = small_all[l][:, offs[3 + i]:offs[3 + i] + 1]
            rows += [an[:, 0], an[:, 1], ag[:, 0]]
        dmod_all.append(jnp.stack(rows, axis=1).reshape(N_DEV, N_ADA * D))
    dmod_all = jnp.stack(dmod_all, axis=0)
    dmod_mine = lax.dynamic_slice_in_dim(dmod_all, ci * nmod, nmod, axis=2)
    g_ada_w = ada_bwd(jnp.transpose(c_all), dmod_mine, "ada_bwd")
    g_ada_b = jnp.stack(g_ada_b, axis=0)
    g_norm_g_full = jnp.stack(g_norm_g, axis=0)
    g_norm_g_mine = lax.dynamic_slice_in_dim(g_norm_g_full, ci * ngw, ngw, axis=2)

    small_params = [
        ("ada_w", ada_w, g_ada_w, m_ada_w, v_ada_w),
        ("ada_b", ada_b, g_ada_b, m_ada_b, v_ada_b),
        ("norm_g", norm_g, g_norm_g_mine, m_norm_g, v_norm_g),
        ("sgu_ln_g", sgu_ln_g, jnp.stack(g_lng, axis=0), m_sgu_ln_g, v_sgu_ln_g),
        ("sgu_ln_b", sgu_ln_b, jnp.stack(g_lnb, axis=0), m_sgu_ln_b, v_sgu_ln_b),
        ("sgu_w", sgu_w, jnp.stack(g_sw, axis=0), m_sgu_w, v_sgu_w),
        ("sgu_b", sgu_b, jnp.stack(g_sb, axis=0), m_sgu_b, v_sgu_b),
        ("final_g", final_g.reshape(1, D), g_final_g.reshape(1, D), m_final_g.reshape(1, D), v_final_g.reshape(1, D)),
    ]
    for nm, w, g, m, v in small_params:
        res = _adam_out(w, g, m, v, f"adamw_{nm}")
        adam_state[nm] = tuple(t.reshape(D) for t in res) if nm == "final_g" else res

    l, blk = done_order[-1]
    adam_block(l, blk, blocks_finish([(l, blk)], [st[1] for st in adam_state.values()], "last")[0])

    names = ["ada_w", "ada_b", "norm_g", "ffn1_wg", "ffn1_wu", "ffn1_wd", "ffn2_wg", "ffn2_wu", "ffn2_wd", "w_in",
             "sgu_ln_g", "sgu_ln_b", "sgu_w", "sgu_b", "w_out", "final_g"]
    shapes = [t.shape for t in (ada_w, ada_b, norm_g, ffn1_wg, ffn1_wu, ffn1_wd, ffn2_wg, ffn2_wu, ffn2_wd, w_in,
                                sgu_ln_g, sgu_ln_b, sgu_w, sgu_b, w_out, final_g)]
    def shaped(nm, t, s):
        if nm in ("ffn1_wg", "ffn1_wu", "ffn2_wg", "ffn2_wu"):
            return jnp.swapaxes(t.reshape(s[0], s[2], s[1]), 1, 2)
        return t.reshape(s)

    return (loss, grad_x[None], *[shaped(nm, adam_state[nm][i], s) for i in range(4) for nm, s in zip(names, shapes)])
```

```python
import math

import jax
import jax.numpy as jnp
from jax import lax
from jax.experimental import pallas as pl
from jax.experimental.pallas import tpu as pltpu

F32 = jnp.float32
BF16 = jnp.bfloat16
EPS = 1e-6
SGU_HEADS = 4
HEAD_LANES = 128
ATT_DH = 64
ATT_BLOCK = 128
MIX_HALF = SGU_HEADS * HEAD_LANES
DILATIONS = (1, 4, 16)
ROPE_THETA = 10000.0
N_ADA = 9
ADAM_LR, ADAM_B1, ADAM_B2, ADAM_EPS, ADAM_WD, ADAM_STEP = 0.001, 0.9, 0.999, 1e-08, 0.01, 10
NEG = -1e30
V7X_VMEM_BYTES = 64 * 1024 * 1024
VMEM_LIMIT = V7X_VMEM_BYTES * 7 // 8
MESH = pl.DeviceIdType.MESH
N_DEV = 8
N_CHIP = 4
_ANY = pl.BlockSpec(memory_space=pl.ANY)


def _tile(n, cap, mult):
    if n <= cap:
        return n
    t = (cap // mult) * mult
    while t >= mult:
        if n % t == 0:
            return t
        t -= mult
    raise ValueError((n, cap, mult))


def _params(dims=None):
    return pltpu.CompilerParams(dimension_semantics=dims, vmem_limit_bytes=VMEM_LIMIT)


def _wspec(w, rows, idx, resident=False):
    arr, lead = w
    kw = dict(pipeline_mode=pl.Buffered(1)) if resident else {}
    return pl.BlockSpec((None,) * len(lead) + (rows, arr.shape[-1]), lambda *g: tuple(lead) + (idx(*g), 0), **kw)


def _wrows(w):
    return w[0].shape[-2]


def _nt(a, b):
    return lax.dot_general(a, b, (((1,), (1,)), ((), ())), preferred_element_type=F32)


def _tn(a, b):
    return lax.dot_general(a, b, (((0,), (0,)), ((), ())), preferred_element_type=F32)


def _nn(a, b):
    return jnp.dot(a, b, preferred_element_type=F32)


def _sigmoid(x):
    return 0.5 * jnp.tanh(0.5 * x) + 0.5


_GELU_K = math.sqrt(2.0 / math.pi)
_GELU_C = 0.044715


def _gelu_and_grad(x):
    x2 = x * x
    t = jnp.tanh(_GELU_K * (x + _GELU_C * x * x2))
    g = 0.5 * x * (1.0 + t)
    dg = 0.5 * (1.0 + t) + 0.5 * x * (1.0 - t * t) * (_GELU_K * (1.0 + 3.0 * _GELU_C * x2))
    return g, dg


def normmod_fwd(h, ng, i_n, mod, i_sh, i_sc, name):
    T, D = h.shape
    tm = _tile(T, 512, 8)

    def body(h_ref, ng_ref, mod_ref, y_ref):
        y_ref[...] = _normmod(h_ref[...], ng_ref[i_n:i_n + 1, :], mod_ref[i_sh:i_sh + 1, :],
                              mod_ref[i_sc:i_sc + 1, :]).astype(BF16)

    return pl.pallas_call(
        body, name=name, grid=(T // tm,),
        in_specs=[pl.BlockSpec((tm, D), lambda i: (i, 0)),
                  pl.BlockSpec(ng.shape, lambda i: (0, 0)),
                  pl.BlockSpec(mod.shape, lambda i: (0, 0))],
        out_specs=pl.BlockSpec((tm, D), lambda i: (i, 0)),
        out_shape=jax.ShapeDtypeStruct((T, D), BF16),
        compiler_params=_params(("parallel",)),
    )(h, ng, mod)


def ffn_up(y, wgT, wuT, name):
    T, D = y.shape
    F = _wrows(wgT)
    tm = _tile(T, 512, 16)
    tf = _tile(F, 2816, 256)
    cuts = list(range(0, tf, 768)) + [tf]

    def body(y_ref, wg_ref, wu_ref, p_ref, q_ref, s_ref):
        yv = y_ref[...]
        for c0, c1 in zip(cuts[:-1], cuts[1:]):
            a = _nt(yv, wg_ref[c0:c1, :])
            b = _nt(yv, wu_ref[c0:c1, :])
            sig = _sigmoid(a)
            q = a * sig
            p_ref[:, c0:c1] = (b * (sig + q * (1.0 - sig))).astype(BF16)
            q_ref[:, c0:c1] = q.astype(BF16)
            s_ref[:, c0:c1] = (q * b).astype(BF16)

    act = jax.ShapeDtypeStruct((T, F), BF16)
    return pl.pallas_call(
        body, name=name, grid=(F // tf, T // tm),
        in_specs=[pl.BlockSpec((tm, D), lambda j, i: (i, 0)),
                  _wspec(wgT, tf, lambda j, i: j, resident=True),
                  _wspec(wuT, tf, lambda j, i: j, resident=True)],
        out_specs=[pl.BlockSpec((tm, tf), lambda j, i: (i, j))] * 3,
        out_shape=[act, act, act],
        compiler_params=_params(("parallel", "parallel")),
    )(y, wgT[0], wuT[0])


def _normmod(x, gn, sh, sc):
    r = lax.rsqrt(jnp.mean(x * x, axis=-1, keepdims=True) + EPS)
    return ((x * r) * gn) * (1.0 + sc) + sh


def resid_matmul(xs, w, h, mod, i_g, coef, name, norm_next=None):
    T, D = h.shape
    kb = xs[0].shape[1]
    assert all(x.shape == (T, kb) for x in xs) and _wrows(w) == kb * len(xs)
    tm = _tile(T, 1024, 16)
    nx = len(xs)
    n_in, n_out, n_shape, n_ops = [], [], [], []
    if norm_next:
        ng_n, i_n, mod_n, i_sh, i_sc = norm_next
        n_in = [pl.BlockSpec(ng_n.shape, lambda i: (0, 0)), pl.BlockSpec(mod_n.shape, lambda i: (0, 0))]
        n_out = [pl.BlockSpec((tm, D), lambda i: (i, 0))]
        n_shape = [jax.ShapeDtypeStruct((T, D), BF16)]
        n_ops = [ng_n, mod_n]

    def body(*refs):
        x_refs, w_refs = refs[:nx], refs[nx:2 * nx]
        h_ref, mod_ref = refs[2 * nx:2 * nx + 2]
        hn_ref, o_ref = refs[2 * nx + 2 + len(n_in):2 * nx + 4 + len(n_in)]
        o = _nn(x_refs[0][...], w_refs[0][...])
        for xr, wr in zip(x_refs[1:], w_refs[1:]):
            o = o + _nn(xr[...], wr[...])
        o_ref[...] = o.astype(BF16)
        hn = h_ref[...] + (coef * mod_ref[i_g:i_g + 1, :]) * o
        hn_ref[...] = hn
        if norm_next:
            ng_ref, modn_ref = refs[2 * nx + 2], refs[2 * nx + 3]
            refs[-1][...] = _normmod(hn, ng_ref[i_n:i_n + 1, :], modn_ref[i_sh:i_sh + 1, :],
                                     modn_ref[i_sc:i_sc + 1, :]).astype(BF16)

    return pl.pallas_call(
        body, name=name, grid=(T // tm,),
        in_specs=([pl.BlockSpec((tm, kb), lambda i: (i, 0))] * nx
                  + [_wspec(w, kb, lambda i, p=p: p, resident=True) for p in range(nx)]
                  + [pl.BlockSpec((tm, D), lambda i: (i, 0)),
                     pl.BlockSpec(mod.shape, lambda i: (0, 0))] + n_in),
        out_specs=[pl.BlockSpec((tm, D), lambda i: (i, 0))] * 2 + n_out,
        out_shape=[jax.ShapeDtypeStruct((T, D), F32), jax.ShapeDtypeStruct((T, D), BF16)] + n_shape,
        compiler_params=_params(("parallel",)),
    )(*xs, *([w[0]] * nx), h, mod, *n_ops)


def _gate_specs(gate, tm, D):
    o, mod, _, _ = gate
    T = o.shape[0]
    return ([pl.BlockSpec((tm, D), lambda i: (i, 0)), pl.BlockSpec(mod.shape, lambda i: (0, 0))],
            [pl.BlockSpec((tm, D), lambda i: (i, 0)), pl.BlockSpec((8, D), lambda i: (0, 0))],
            [jax.ShapeDtypeStruct((T, D), BF16), jax.ShapeDtypeStruct((8, D), F32)],
            [o, mod])


def _gate_emit(d, gate, o_ref, mod_ref, do_ref, red_ref):
    _, _, i_g, coef = gate
    do_ref[...] = (d * (coef * mod_ref[i_g:i_g + 1, :])).astype(BF16)

    @pl.when(pl.program_id(0) == 0)
    def _():
        red_ref[...] = jnp.zeros_like(red_ref)

    red_ref[0:1, :] += coef * jnp.sum(d * o_ref[...].astype(F32), axis=0, keepdims=True)


def ffn_bwd_mid(do, wd, p, q, name, after=()):
    T, D = do.shape
    F = _wrows(wd)
    tm = _tile(T, 512, 16)
    tf = _tile(F, 2816, 256)
    cuts = list(range(0, tf, 256)) + [tf]

    def body(do_ref, wd_ref, p_ref, q_ref, *rest):
        da_ref, db_ref = rest[-2:]
        dov = do_ref[...]
        for c0, c1 in zip(cuts[:-1], cuts[1:]):
            ds = _nt(dov, wd_ref[c0:c1, :])
            da_ref[:, c0:c1] = (ds * p_ref[:, c0:c1].astype(F32)).astype(BF16)
            db_ref[:, c0:c1] = (ds * q_ref[:, c0:c1].astype(F32)).astype(BF16)

    act = jax.ShapeDtypeStruct((T, F), BF16)
    return pl.pallas_call(
        body, name=name, grid=(F // tf, T // tm),
        in_specs=[pl.BlockSpec((tm, D), lambda j, i: (i, 0)),
                  _wspec(wd, tf, lambda j, i: j, resident=True),
                  pl.BlockSpec((tm, tf), lambda j, i: (i, j)),
                  pl.BlockSpec((tm, tf), lambda j, i: (i, j))] + [_ANY] * len(after),
        out_specs=[pl.BlockSpec((tm, tf), lambda j, i: (i, j))] * 2,
        out_shape=[act, act],
        compiler_params=_params(("parallel", "parallel")),
    )(do, wd[0], p, q, *after)


def dy_normbwd(pairs, h, dhp, ng, i_n, mod, i_sc, name, gate=None, after=()):
    T, D = h.shape
    tm = _tile(T, 512, 16)
    npair = len(pairs)
    g_in, g_out, g_shape, g_ops = _gate_specs(gate, tm, D) if gate else ([], [], [], [])
    n_in = 2 * npair + 4 + len(g_in) + len(after)

    def body(*refs):
        x_refs, w_refs = refs[:npair], refs[npair:2 * npair]
        h_ref, dhp_ref, ng_ref, mod_ref = refs[2 * npair:2 * npair + 4]
        dh_ref, red_ref = refs[n_in:n_in + 2]
        dy = _nn(x_refs[0][...], w_refs[0][...])
        for xr, wr in zip(x_refs[1:], w_refs[1:]):
            dy = dy + _nn(xr[...], wr[...])
        x = h_ref[...]
        r = lax.rsqrt(jnp.mean(x * x, axis=-1, keepdims=True) + EPS)
        n = x * r
        gn = ng_ref[i_n:i_n + 1, :]
        sc1 = 1.0 + mod_ref[i_sc:i_sc + 1, :]
        w = sc1 * gn
        dyn = dy * n
        col = jnp.sum(dyn, axis=0, keepdims=True)

        @pl.when(pl.program_id(0) == 0)
        def _():
            red_ref[...] = jnp.zeros_like(red_ref)

        red_ref[0:1, :] += jnp.sum(dy, axis=0, keepdims=True)
        red_ref[1:2, :] += gn * col
        red_ref[2:3, :] += sc1 * col
        dh_new = dhp_ref[...] + r * (dy * w - n * jnp.mean(dyn * w, axis=-1, keepdims=True))
        dh_ref[...] = dh_new
        if gate:
            _gate_emit(dh_new, gate, refs[2 * npair + 4], refs[2 * npair + 5], refs[-2], refs[-1])

    in_specs = ([pl.BlockSpec((tm, kb), lambda i, c=c: (i, c)) for (_, c, _, _, kb) in pairs]
                + [_wspec(w, kb, lambda i, r=r: r, resident=True) for (_, _, w, r, kb) in pairs]
                + [pl.BlockSpec((tm, D), lambda i: (i, 0)),
                   pl.BlockSpec((tm, D), lambda i: (i, 0)),
                   pl.BlockSpec(ng.shape, lambda i: (0, 0)),
                   pl.BlockSpec(mod.shape, lambda i: (0, 0))] + g_in + [_ANY] * len(after))
    return pl.pallas_call(
        body, name=name, grid=(T // tm,), in_specs=in_specs,
        out_specs=[pl.BlockSpec((tm, D), lambda i: (i, 0)), pl.BlockSpec((8, D), lambda i: (0, 0))] + g_out,
        out_shape=[jax.ShapeDtypeStruct((T, D), F32), jax.ShapeDtypeStruct((8, D), F32)] + g_shape,
        compiler_params=_params(("arbitrary",)),
    )(*[p[0] for p in pairs], *[p[2][0] for p in pairs], h, dhp, ng, mod, *g_ops, *after)


def matmul_tn(a, b, buf, slot, row0, name, tmo_cap=1408):
    T, N = b.shape
    ma = a.shape[1]
    tmo = _tile(ma, tmo_cap, 128)
    assert row0 % tmo == 0
    nmo = ma // tmo
    tk = _tile(T, 2048, 16)
    nk = T // tk

    def body(a_ref, b_ref, buf_ref, o_ref, acc_ref):
        k = pl.program_id(1)

        @pl.when(k == 0)
        def _():
            acc_ref[...] = jnp.zeros_like(acc_ref)

        acc_ref[...] += _tn(a_ref[...], b_ref[...])

        @pl.when(k == nk - 1)
        def _():
            o_ref[...] = acc_ref[...].astype(BF16)

    return pl.pallas_call(
        body, name=name, grid=(nmo, nk),
        in_specs=[pl.BlockSpec((tk, tmo), lambda j, k: (k, j)),
                  pl.BlockSpec((tk, N), lambda j, k: (k, 0)),
                  pl.BlockSpec(memory_space=pl.ANY)],
        out_specs=pl.BlockSpec((None, tmo, N), lambda j, k: (slot, row0 // tmo + j, 0)),
        out_shape=jax.ShapeDtypeStruct(buf.shape, BF16),
        scratch_shapes=[pltpu.VMEM((tmo, N), F32)],
        input_output_aliases={2: 0},
        compiler_params=_params(("parallel", "arbitrary")),
    )(a, b, buf)


def matmul_nt(x, w, name, after=()):
    T, K = x.shape
    N = _wrows(w)
    tm = _tile(T, 1024, 16)
    tn = _tile(N, 1280, 128)

    def body(x_ref, w_ref, *rest):
        rest[-1][...] = _nt(x_ref[...], w_ref[...]).astype(BF16)

    return pl.pallas_call(
        body, name=name, grid=(N // tn, T // tm),
        in_specs=[pl.BlockSpec((tm, K), lambda j, i: (i, 0)), _wspec(w, tn, lambda j, i: j)] + [_ANY] * len(after),
        out_specs=pl.BlockSpec((tm, tn), lambda j, i: (i, j)),
        out_shape=jax.ShapeDtypeStruct((T, N), BF16),
        compiler_params=_params(("parallel", "parallel")),
    )(x, w[0], *after)


def _sgu_head_fwd(u, v, lng, lnb):
    gu, dgu = _gelu_and_grad(u)
    gv, dgv = _gelu_and_grad(v)
    mu = jnp.mean(gv, axis=-1, keepdims=True)
    xc = gv - mu
    rstd = lax.rsqrt(jnp.mean(xc * xc, axis=-1, keepdims=True) + EPS)
    xhat = xc * rstd
    vn = xhat * lng + lnb
    return gu, dgu, dgv, rstd, xhat, vn


def _tril_mask():
    r = lax.broadcasted_iota(jnp.int32, (ATT_BLOCK, ATT_BLOCK), 0)
    c = lax.broadcasted_iota(jnp.int32, (ATT_BLOCK, ATT_BLOCK), 1)
    return c <= r


def _triu_mask():
    r = lax.broadcasted_iota(jnp.int32, (ATT_BLOCK, ATT_BLOCK), 0)
    c = lax.broadcasted_iota(jnp.int32, (ATT_BLOCK, ATT_BLOCK), 1)
    return r <= c


def sgu_fwd(proj, lng, lnb, w, bcol, name):
    T = proj.shape[0]
    tm = _tile(T, 512, 128)
    nch = tm // ATT_BLOCK

    def body(u_ref, v_ref, lng_ref, lnb_ref, w_ref, b_ref, o_ref):
        tril = _tril_mask()
        for hd in range(SGU_HEADS):
            sl = slice(hd * HEAD_LANES, (hd + 1) * HEAD_LANES)
            u = u_ref[:, sl].astype(F32)
            v = v_ref[:, sl].astype(F32)
            gu, _, _, _, _, vn = _sgu_head_fwd(u, v, lng_ref[:, sl], lnb_ref[:, sl])
            wm = jnp.where(tril, w_ref[hd], 0.0).astype(BF16)
            vnb = vn.astype(BF16)
            bc = b_ref[:, hd:hd + 1]
            for ch in range(nch):
                rs = slice(ch * ATT_BLOCK, (ch + 1) * ATT_BLOCK)
                z = _nn(wm, vnb[rs, :]) + bc
                o_ref[rs, sl] = (gu[rs, :] * z).astype(BF16)

    return pl.pallas_call(
        body, name=name, grid=(T // tm,),
        in_specs=[pl.BlockSpec((tm, MIX_HALF), lambda i: (i, 0)),
                  pl.BlockSpec((tm, MIX_HALF), lambda i: (i, 1)),
                  pl.BlockSpec((1, MIX_HALF), lambda i: (0, 0)),
                  pl.BlockSpec((1, MIX_HALF), lambda i: (0, 0)),
                  pl.BlockSpec(w.shape, lambda i: (0, 0, 0)),
                  pl.BlockSpec(bcol.shape, lambda i: (0, 0))],
        out_specs=pl.BlockSpec((tm, MIX_HALF), lambda i: (i, 0)),
        out_shape=jax.ShapeDtypeStruct((T, MIX_HALF), BF16),
        compiler_params=_params(("parallel",)),
    )(proj, proj, lng, lnb, w, bcol)


def sgu_bwd(proj, dmixed, lng, lnb, w, wt, bcol, name):
    T = proj.shape[0]
    tm = _tile(T, 512, 128)
    nch = tm // ATT_BLOCK
    nsteps = T // tm

    def body(u_ref, v_ref, g_ref, lng_ref, lnb_ref, w_ref, wt_ref, b_ref, duv_ref, dw_ref, dvec_ref, bacc_ref):
        step = pl.program_id(0)

        @pl.when(step == 0)
        def _():
            dw_ref[...] = jnp.zeros_like(dw_ref)
            dvec_ref[...] = jnp.zeros_like(dvec_ref)
            bacc_ref[...] = jnp.zeros_like(bacc_ref)

        tril = _tril_mask()
        triu = _triu_mask()
        for hd in range(SGU_HEADS):
            sl = slice(hd * HEAD_LANES, (hd + 1) * HEAD_LANES)
            u = u_ref[:, sl].astype(F32)
            v = v_ref[:, sl].astype(F32)
            lng_h = lng_ref[:, sl]
            gu, dgu, dgv, rstd, xhat, vn = _sgu_head_fwd(u, v, lng_h, lnb_ref[:, sl])
            wm = jnp.where(tril, w_ref[hd], 0.0).astype(BF16)
            wmt = jnp.where(triu, wt_ref[hd], 0.0).astype(BF16)
            vnb = vn.astype(BF16)
            bc = b_ref[:, hd:hd + 1]
            g = g_ref[:, sl].astype(F32)
            dw_acc = jnp.zeros((ATT_BLOCK, ATT_BLOCK), F32)
            b_acc = jnp.zeros((ATT_BLOCK, HEAD_LANES), F32)
            dvn_parts = []
            for ch in range(nch):
                rs = slice(ch * ATT_BLOCK, (ch + 1) * ATT_BLOCK)
                z = _nn(wm, vnb[rs, :]) + bc
                duv_ref[rs, sl] = (g[rs, :] * z * dgu[rs, :]).astype(BF16)
                dz = g[rs, :] * gu[rs, :]
                dzb = dz.astype(BF16)
                dvn_parts.append(_nn(wmt, dzb))
                dw_acc = dw_acc + _nt(dzb, vnb[rs, :])
                b_acc = b_acc + dz
            dvn = jnp.concatenate(dvn_parts, axis=0)
            dw_ref[hd] += jnp.where(tril, dw_acc, 0.0)
            bacc_ref[hd] += b_acc
            dvec_ref[0:1, sl] += jnp.sum(dvn * xhat, axis=0, keepdims=True)
            dvec_ref[1:2, sl] += jnp.sum(dvn, axis=0, keepdims=True)
            dxh = dvn * lng_h
            dgv_in = rstd * (dxh - jnp.mean(dxh, axis=-1, keepdims=True)
                             - xhat * jnp.mean(dxh * xhat, axis=-1, keepdims=True))
            duv_ref[:, MIX_HALF + hd * HEAD_LANES:MIX_HALF + (hd + 1) * HEAD_LANES] = (dgv_in * dgv).astype(BF16)

        @pl.when(step == nsteps - 1)
        def _():
            for hd in range(SGU_HEADS):
                sl = slice(hd * HEAD_LANES, (hd + 1) * HEAD_LANES)
                dvec_ref[2:3, sl] = jnp.sum(bacc_ref[hd].T, axis=0, keepdims=True)

    return pl.pallas_call(
        body, name=name, grid=(nsteps,),
        in_specs=[pl.BlockSpec((tm, MIX_HALF), lambda i: (i, 0)),
                  pl.BlockSpec((tm, MIX_HALF), lambda i: (i, 1)),
                  pl.BlockSpec((tm, MIX_HALF), lambda i: (i, 0)),
                  pl.BlockSpec((1, MIX_HALF), lambda i: (0, 0)),
                  pl.BlockSpec((1, MIX_HALF), lambda i: (0, 0)),
                  pl.BlockSpec(w.shape, lambda i: (0, 0, 0)),
                  pl.BlockSpec(w.shape, lambda i: (0, 0, 0)),
                  pl.BlockSpec(bcol.shape, lambda i: (0, 0))],
        out_specs=[pl.BlockSpec((tm, 2 * MIX_HALF), lambda i: (i, 0)),
                   pl.BlockSpec(w.shape, lambda i: (0, 0, 0)),
                   pl.BlockSpec((8, MIX_HALF), lambda i: (0, 0))],
        out_shape=[jax.ShapeDtypeStruct((T, 2 * MIX_HALF), BF16),
                   jax.ShapeDtypeStruct(w.shape, F32),
                   jax.ShapeDtypeStruct((8, MIX_HALF), F32)],
        scratch_shapes=[pltpu.VMEM((SGU_HEADS, ATT_BLOCK, HEAD_LANES), F32)],
        compiler_params=_params(("arbitrary",)),
    )(proj, proj, dmixed, lng, lnb, w, wt, bcol)


def _rot_half(t):
    lane = lax.broadcasted_iota(jnp.int32, t.shape, 1)
    first = (lane % ATT_DH) < (ATT_DH // 2)
    return jnp.where(first, -pltpu.roll(t, HEAD_LANES - ATT_DH // 2, 1), pltpu.roll(t, ATT_DH // 2, 1))


LAYOUT_ROWS = 512


def _res_spec(d, tm, W):
    return pl.BlockSpec((d, tm // d, W), lambda i: (0, i, 0))


def _res_shape(d, T, W, dtype):
    return jax.ShapeDtypeStruct((d, T // d, W), dtype)


def _slab_buf(tm, W):
    return pltpu.VMEM((W // HEAD_LANES, tm, HEAD_LANES), F32)


def _lanes(hp):
    return slice(hp * HEAD_LANES, (hp + 1) * HEAD_LANES)


def _to_res(buf, out_ref, d, dtype):
    nslab, tm, _ = buf.shape
    for hp in range(nslab):
        if d == 1:
            out_ref[0, :, _lanes(hp)] = buf[hp].astype(dtype)
        else:
            for r in range(d):
                out_ref[r, :, _lanes(hp)] = buf.at[hp][pl.ds(r, tm // d, stride=d), :].astype(dtype)


def _from_res(in_ref, buf, d):
    nslab, tm, _ = buf.shape
    for hp in range(nslab):
        if d == 1:
            buf[hp] = in_ref[0, :, _lanes(hp)].astype(F32)
        else:
            for r in range(d):
                buf.at[hp][pl.ds(r, tm // d, stride=d), :] = in_ref[r, :, _lanes(hp)].astype(F32)


def rope_fwd(proj, cos, sin, name):
    T = proj.shape[0]
    tm = LAYOUT_ROWS
    scale = 1.0 / math.sqrt(ATT_DH)
    nd = len(DILATIONS)

    def body(q_ref, k_ref, v_ref, cos_ref, sin_ref, *rest):
        outs, buf = rest[:3 * nd], rest[3 * nd]
        c = cos_ref[...]
        s = sin_ref[...]
        for which, src in enumerate((q_ref, k_ref, v_ref)):
            for hp in range(MIX_HALF // HEAD_LANES):
                t = src[:, _lanes(hp)].astype(F32)
                if which == 0:
                    t = scale * (t * c + _rot_half(t) * s)
                elif which == 1:
                    t = t * c + _rot_half(t) * s
                buf[hp] = t
            for di, d in enumerate(DILATIONS):
                _to_res(buf, outs[3 * di + which], d, BF16)

    return pl.pallas_call(
        body, name=name, grid=(T // tm,),
        in_specs=[pl.BlockSpec((tm, MIX_HALF), lambda i: (i, 2)),
                  pl.BlockSpec((tm, MIX_HALF), lambda i: (i, 3)),
                  pl.BlockSpec((tm, MIX_HALF), lambda i: (i, 4)),
                  pl.BlockSpec((tm, HEAD_LANES), lambda i: (i, 0)),
                  pl.BlockSpec((tm, HEAD_LANES), lambda i: (i, 0))],
        out_specs=[_res_spec(d, tm, MIX_HALF) for d in DILATIONS for _ in range(3)],
        out_shape=[_res_shape(d, T, MIX_HALF, BF16) for d in DILATIONS for _ in range(3)],
        scratch_shapes=[_slab_buf(tm, MIX_HALF)],
        compiler_params=_params(("parallel",)),
    )(proj, proj, proj, cos, sin)


def to_residues(x, col, name):
    T = x.shape[0]
    tm = LAYOUT_ROWS

    def body(x_ref, *rest):
        outs, buf = rest[:-1], rest[-1]
        for hp in range(MIX_HALF // HEAD_LANES):
            buf[hp] = x_ref[:, _lanes(hp)].astype(F32)
        for o_ref, d in zip(outs, DILATIONS):
            _to_res(buf, o_ref, d, BF16)

    return pl.pallas_call(
        body, name=name, grid=(T // tm,),
        in_specs=[pl.BlockSpec((tm, MIX_HALF), lambda i: (i, col))],
        out_specs=[_res_spec(d, tm, MIX_HALF) for d in DILATIONS],
        out_shape=[_res_shape(d, T, MIX_HALF, BF16) for d in DILATIONS],
        scratch_shapes=[_slab_buf(tm, MIX_HALF)],
        compiler_params=_params(("parallel",)),
    )(x)


def rope_bwd(dqs, dks, dvs, cos, sin, name):
    T = dqs[0].shape[0] * dqs[0].shape[1]
    tm = LAYOUT_ROWS
    scale = 1.0 / math.sqrt(ATT_DH)
    npat = len(dqs)

    def body(*refs):
        groups = refs[:npat], refs[npat:2 * npat], refs[2 * npat:3 * npat]
        cos_ref, sin_ref, o_ref, buf, acc = refs[3 * npat:]
        c = cos_ref[...]
        s = sin_ref[...]
        for which, g_refs in enumerate(groups):
            _from_res(g_refs[0], acc, DILATIONS[0])
            for g_ref, d in zip(g_refs[1:], DILATIONS[1:]):
                _from_res(g_ref, buf, d)
                acc[...] += buf[...]
            for hp in range(MIX_HALF // HEAD_LANES):
                g = acc[hp]
                if which == 0:
                    g = scale * g
                if which < 2:
                    g = g * c - _rot_half(g * s)
                o_ref[:, which * MIX_HALF + hp * HEAD_LANES:which * MIX_HALF + (hp + 1) * HEAD_LANES] = g.astype(BF16)

    return pl.pallas_call(
        body, name=name, grid=(T // tm,),
        in_specs=([_res_spec(d, tm, MIX_HALF) for _ in range(3) for d in DILATIONS]
                  + [pl.BlockSpec((tm, HEAD_LANES), lambda i: (i, 0))] * 2),
        out_specs=pl.BlockSpec((tm, 3 * MIX_HALF), lambda i: (i, 0)),
        out_shape=jax.ShapeDtypeStruct((T, 3 * MIX_HALF), BF16),
        scratch_shapes=[_slab_buf(tm, MIX_HALF), _slab_buf(tm, MIX_HALF)],
        compiler_params=_params(("parallel",)),
    )(*dqs, *dks, *dvs, cos, sin)


def _band_masks(n):
    r = lax.broadcasted_iota(jnp.int32, (2 * ATT_BLOCK, ATT_BLOCK), 0)
    c = lax.broadcasted_iota(jnp.int32, (2 * ATT_BLOCK, ATT_BLOCK), 1)
    qi = r % ATT_BLOCK
    head = (c < ATT_DH) == (r < ATT_BLOCK)
    return (c >= qi) & (n > 0), c <= qi, head, c[:ATT_BLOCK] < ATT_DH


def _stack_heads(x, head):
    x2 = jnp.concatenate([x, x], axis=0)
    return jnp.where(head, x2, jnp.zeros_like(x2))


def attn_fwd(q, k, v, name):
    d, L, W = q.shape
    nb = L // ATT_BLOCK
    nsub = 2 if nb % 2 == 0 else 1

    def body(q_ref, kp_ref, kc_ref, vp_ref, vc_ref, o_ref, lse_ref):
        step = pl.program_id(1)
        for u in range(nsub):
            rows = slice(u * ATT_BLOCK, (u + 1) * ATT_BLOCK)
            before = slice((u - 1) * ATT_BLOCK, u * ATT_BLOCK)
            mask_p, mask_c, head, head0 = _band_masks(step if u == 0 else 1)
            for hp in range(W // HEAD_LANES):
                sl = slice(hp * HEAD_LANES, (hp + 1) * HEAD_LANES)
                kp, vp = (kp_ref[0, :, sl], vp_ref[0, :, sl]) if u == 0 else (kc_ref[0, before, sl], vc_ref[0, before, sl])
                kc, vc = kc_ref[0, rows, sl], vc_ref[0, rows, sl]
                qs = _stack_heads(q_ref[0, rows, sl], head)
                sp = jnp.where(mask_p, _nt(qs, kp), NEG)
                sc = jnp.where(mask_c, _nt(qs, kc), NEG)
                m = jnp.maximum(jnp.max(sp, axis=1, keepdims=True), jnp.max(sc, axis=1, keepdims=True))
                pp = jnp.exp(sp - m)
                pc = jnp.exp(sc - m)
                den = jnp.sum(pp, axis=1, keepdims=True) + jnp.sum(pc, axis=1, keepdims=True)
                o = (_nn(pp.astype(BF16), vp) + _nn(pc.astype(BF16), vc)) / den
                lse = m + jnp.log(den)
                o_ref[0, rows, sl] = jnp.where(head0, o[:ATT_BLOCK], o[ATT_BLOCK:]).astype(BF16)
                lse_ref[0, rows, sl] = jnp.where(head0, lse[:ATT_BLOCK], lse[ATT_BLOCK:])

    cur = pl.BlockSpec((1, nsub * ATT_BLOCK, W), lambda r, n: (r, n, 0))
    prev = pl.BlockSpec((1, ATT_BLOCK, W), lambda r, n: (r, jnp.maximum(nsub * n - 1, 0), 0))
    return pl.pallas_call(
        body, name=name, grid=(d, nb // nsub),
        in_specs=[cur, prev, cur, prev, cur],
        out_specs=[cur, cur],
        out_shape=[jax.ShapeDtypeStruct((d, L, W), BF16), jax.ShapeDtypeStruct((d, L, W), F32)],
        compiler_params=_params(("parallel", "parallel")),
    )(q, k, k, v, v)


def attn_combine(os_, lses, name):
    T = os_[0].shape[0] * os_[0].shape[1]
    W = os_[0].shape[2]
    tm = LAYOUT_ROWS
    npat = len(os_)

    def body(*refs):
        o_refs, l_refs = refs[:npat], refs[npat:2 * npat]
        out_ref = refs[2 * npat]
        ores, lres = refs[2 * npat + 1:3 * npat + 1], refs[3 * npat + 1:4 * npat + 1]
        bufs = refs[4 * npat + 1:]
        lbufs, obufs, out_buf, lse_buf = bufs[:npat], bufs[npat:2 * npat], bufs[2 * npat], bufs[2 * npat + 1]
        for p, d in enumerate(DILATIONS):
            _from_res(l_refs[p], lbufs[p], d)
            _from_res(o_refs[p], obufs[p], d)
        for hp in range(W // HEAD_LANES):
            ls = [b[hp] for b in lbufs]
            m = ls[0]
            for l in ls[1:]:
                m = jnp.maximum(m, l)
            es = [jnp.exp(l - m) for l in ls]
            z = es[0]
            for e in es[1:]:
                z = z + e
            acc = es[0] * obufs[0][hp]
            for p in range(1, npat):
                acc = acc + es[p] * obufs[p][hp]
            out = acc / z
            out_ref[:, _lanes(hp)] = out.astype(BF16)
            out_buf[hp] = out
            lse_buf[hp] = m + jnp.log(z)
        for p, d in enumerate(DILATIONS):
            _to_res(out_buf, ores[p], d, BF16)
            _to_res(lse_buf, lres[p], d, F32)

    return pl.pallas_call(
        body, name=name, grid=(T // tm,),
        in_specs=[_res_spec(d, tm, W) for _ in range(2) for d in DILATIONS],
        out_specs=([pl.BlockSpec((tm, W), lambda i: (i, 0))] + [_res_spec(d, tm, W) for _ in range(2) for d in DILATIONS]),
        out_shape=([jax.ShapeDtypeStruct((T, W), BF16)] + [_res_shape(d, T, W, BF16) for d in DILATIONS]
                   + [_res_shape(d, T, W, F32) for d in DILATIONS]),
        scratch_shapes=[_slab_buf(tm, W)] * (2 * npat + 2),
        compiler_params=_params(("parallel",)),
    )(*os_, *lses)


def attn_bwd(q, k, v, do, o, lse, name):
    d, L, W = q.shape
    nb = L // ATT_BLOCK
    nsub = 2 if nb % 2 == 0 else 1
    per_seq = nb // nsub
    nst = d * per_seq
    nb = d * nb
    last = slice((nsub - 1) * ATT_BLOCK, nsub * ATT_BLOCK)
    q, k, v, do, o, lse = (t.reshape(1, d * L, W) for t in (q, k, v, do, o, lse))

    def body(q_ref, kp_ref, kc_ref, vp_ref, vc_ref, do_ref, o_ref, lse_ref, dq_ref, dk_ref, dv_ref, kkeep, vkeep):
        step = pl.program_id(1)
        n = step % per_seq

        @pl.when(step == 0)
        def _():
            kkeep[...] = jnp.zeros_like(kkeep)
            vkeep[...] = jnp.zeros_like(vkeep)

        @pl.when(step < nst)
        def _():
            for hp in range(W // HEAD_LANES):
                sl = slice(hp * HEAD_LANES, (hp + 1) * HEAD_LANES)
                shares = []
                for u in range(nsub):
                    rows = slice(u * ATT_BLOCK, (u + 1) * ATT_BLOCK)
                    before = slice((u - 1) * ATT_BLOCK, u * ATT_BLOCK)
                    mask_p, mask_c, head, head0 = _band_masks(n if u == 0 else 1)
                    kp, vp = (kp_ref[0, :, sl], vp_ref[0, :, sl]) if u == 0 else (kc_ref[0, before, sl], vc_ref[0, before, sl])
                    kc, vc = kc_ref[0, rows, sl], vc_ref[0, rows, sl]
                    dout = do_ref[0, rows, sl]
                    qs = _stack_heads(q_ref[0, rows, sl], head)
                    dos = _stack_heads(dout, head)
                    lse_v = lse_ref[0, rows, sl]
                    lse_c = jnp.max(jnp.where(head, jnp.concatenate([lse_v, lse_v], axis=0), NEG), axis=1, keepdims=True)
                    delta = jnp.sum(_stack_heads(dout.astype(F32) * o_ref[0, rows, sl].astype(F32), head), axis=1,
                                    keepdims=True)
                    pp = jnp.exp(jnp.where(mask_p, _nt(qs, kp), NEG) - lse_c)
                    pc = jnp.exp(jnp.where(mask_c, _nt(qs, kc), NEG) - lse_c)
                    dsp = (pp * (_nt(dos, vp) - delta)).astype(BF16)
                    dsc = (pc * (_nt(dos, vc) - delta)).astype(BF16)
                    dq2 = _nn(dsp, kp) + _nn(dsc, kc)
                    dq_ref[0, rows, sl] = jnp.where(head0, dq2[:ATT_BLOCK], dq2[ATT_BLOCK:]).astype(BF16)
                    shares.append((_tn(dsp, qs), _tn(pp.astype(BF16), dos), _tn(dsc, qs), _tn(pc.astype(BF16), dos)))
                dk_ref[0, last, sl] = (kkeep[last, sl] + shares[0][0]).astype(BF16)
                dv_ref[0, last, sl] = (vkeep[last, sl] + shares[0][1]).astype(BF16)
                if nsub == 2:
                    dk_ref[0, :ATT_BLOCK, sl] = kkeep[:ATT_BLOCK, sl].astype(BF16)
                    dv_ref[0, :ATT_BLOCK, sl] = vkeep[:ATT_BLOCK, sl].astype(BF16)
                    kkeep[:ATT_BLOCK, sl] = shares[0][2] + shares[1][0]
                    vkeep[:ATT_BLOCK, sl] = shares[0][3] + shares[1][1]
                kkeep[last, sl] = shares[-1][2]
                vkeep[last, sl] = shares[-1][3]

        @pl.when(step == nst)
        def _():
            dk_ref[0] = kkeep[...].astype(BF16)
            dv_ref[0] = vkeep[...].astype(BF16)

    rows_per_step = nsub * ATT_BLOCK
    cur = pl.BlockSpec((1, rows_per_step, W), lambda r, n: (r, jnp.minimum(n, nst - 1), 0))
    lag = pl.BlockSpec((1, rows_per_step, W), lambda r, n: (r, jnp.clip(n - 1, 0, nst - 1), 0))
    prev = pl.BlockSpec((1, ATT_BLOCK, W), lambda r, n: (r, jnp.clip(nsub * n - 1, 0, nb - 1), 0))
    out = jax.ShapeDtypeStruct((1, d * L, W), BF16)
    outs = pl.pallas_call(
        body, name=name, grid=(1, nst + 1),
        in_specs=[cur, prev, cur, prev, cur, cur, cur, cur],
        out_specs=[cur, lag, lag], out_shape=[out, out, out],
        scratch_shapes=[pltpu.VMEM((rows_per_step, W), F32), pltpu.VMEM((rows_per_step, W), F32)],
        compiler_params=_params(("parallel", "arbitrary")),
    )(q, k, k, v, v, do, o, lse)
    return [t.reshape(d, L, W) for t in outs]


def final_loss_bwd(h, gf, tgt, gate, name):
    T, D = h.shape
    tm = _tile(T, 512, 16)
    g_in, g_out, g_shape, g_ops = _gate_specs(gate, tm, D)

    def body(h_ref, g_ref, t_ref, o_ref, modg_ref, dh_ref, red_ref, do_ref, redg_ref):
        x = h_ref[...]
        r = lax.rsqrt(jnp.mean(x * x, axis=-1, keepdims=True) + EPS)
        n = x * r
        g = g_ref[...]
        err = n * g - t_ref[...]
        dy = err * (1.0 / D)

        @pl.when(pl.program_id(0) == 0)
        def _():
            red_ref[...] = jnp.zeros_like(red_ref)

        red_ref[0:1, :] += jnp.sum(dy * n, axis=0, keepdims=True)
        red_ref[1:2, :] += jnp.zeros((1, D), F32) + (0.5 / D) * jnp.sum(err * err, keepdims=True)
        dn = dy * g
        dh = r * (dn - n * jnp.mean(dn * n, axis=-1, keepdims=True))
        dh_ref[...] = dh
        _gate_emit(dh, gate, o_ref, modg_ref, do_ref, redg_ref)

    return pl.pallas_call(
        body, name=name, grid=(T // tm,),
        in_specs=[pl.BlockSpec((tm, D), lambda i: (i, 0)),
                  pl.BlockSpec((1, D), lambda i: (0, 0)),
                  pl.BlockSpec((tm, D), lambda i: (i, 0))] + g_in,
        out_specs=[pl.BlockSpec((tm, D), lambda i: (i, 0)), pl.BlockSpec((8, D), lambda i: (0, 0))] + g_out,
        out_shape=[jax.ShapeDtypeStruct((T, D), F32), jax.ShapeDtypeStruct((8, D), F32)] + g_shape,
        compiler_params=_params(("arbitrary",)),
    )(h, gf, tgt, *g_ops)


def ada_fwd(c_all, ada_w, ada_b, name):
    nl, D, N = ada_w.shape

    def body(c_ref, w_ref, b_ref, o_ref):
        c = c_ref[...]
        o_ref[0] = _nn(c * _sigmoid(c), w_ref[0]) + b_ref[0]

    return pl.pallas_call(
        body, name=name, grid=(nl,),
        in_specs=[pl.BlockSpec((N_DEV, D), lambda l: (0, 0)),
                  pl.BlockSpec((1, D, N), lambda l: (l, 0, 0)),
                  pl.BlockSpec((1, 1, N), lambda l: (l, 0, 0))],
        out_specs=pl.BlockSpec((1, N_DEV, N), lambda l: (l, 0, 0)),
        out_shape=jax.ShapeDtypeStruct((nl, N_DEV, N), F32),
        compiler_params=_params(("parallel",)),
    )(c_all, ada_w, ada_b)


def ada_bwd(c_allT, dmod, name):
    nl, _, N = dmod.shape
    D = c_allT.shape[0]

    def body(c_ref, g_ref, o_ref):
        c = c_ref[...]
        ca = c * _sigmoid(c)
        acc = ca[:, 0:1] * g_ref[0, 0:1, :]
        for b in range(1, N_DEV):
            acc = acc + ca[:, b:b + 1] * g_ref[0, b:b + 1, :]
        o_ref[0] = acc

    return pl.pallas_call(
        body, name=name, grid=(nl,),
        in_specs=[pl.BlockSpec((D, N_DEV), lambda l: (0, 0)),
                  pl.BlockSpec((1, N_DEV, N), lambda l: (l, 0, 0))],
        out_specs=pl.BlockSpec((1, D, N), lambda l: (l, 0, 0)),
        out_shape=jax.ShapeDtypeStruct((nl, D, N), F32),
        compiler_params=_params(("parallel",)),
    )(c_allT, dmod)


def adamw(w, g, m, v, name):
    R, C = w.shape
    tr = _tile(R, max(8, (1 << 19) // C // 8 * 8), 8)
    c1 = 1.0 - ADAM_B1 ** ADAM_STEP
    c2 = 1.0 - ADAM_B2 ** ADAM_STEP

    def body(w_ref, g_ref, m_ref, v_ref, d_ref, mo_ref, vo_ref):
        gv = g_ref[...]
        mn = ADAM_B1 * m_ref[...] + (1.0 - ADAM_B1) * gv
        vn = ADAM_B2 * v_ref[...] + (1.0 - ADAM_B2) * (gv * gv)
        mo_ref[...] = mn
        vo_ref[...] = vn
        d_ref[...] = -ADAM_LR * ((mn / c1) / (jnp.sqrt(vn / c2) + ADAM_EPS) + ADAM_WD * w_ref[...])

    blk = pl.BlockSpec((tr, C), lambda i: (i, 0))
    out = jax.ShapeDtypeStruct((R, C), F32)
    return pl.pallas_call(
        body, name=name, grid=(R // tr,),
        in_specs=[blk] * 4, out_specs=[blk] * 3, out_shape=[out] * 3,
        compiler_params=_params(("parallel",)),
    )(w, g, m, v)


def adamw_layer(w, g, m, v, l, prev, name):
    NLw, R, C = w.shape
    tr = _tile(R, 128, 8)
    nrb = R // tr
    c1 = 1.0 - ADAM_B1 ** ADAM_STEP
    c2 = 1.0 - ADAM_B2 ** ADAM_STEP
    w, m, v = (t.reshape(NLw * R, C) for t in (w, m, v))

    def body(w_ref, g_ref, m_ref, v_ref, *rest):
        go_ref, d_ref, mo_ref, vo_ref = rest[-4:]
        gv = g_ref[...]
        mn = ADAM_B1 * m_ref[...] + (1.0 - ADAM_B1) * gv
        vn = ADAM_B2 * v_ref[...] + (1.0 - ADAM_B2) * (gv * gv)
        go_ref[...] = gv
        mo_ref[...] = mn
        vo_ref[...] = vn
        d_ref[...] = -ADAM_LR * ((mn / c1) / (jnp.sqrt(vn / c2) + ADAM_EPS) + ADAM_WD * w_ref[...])

    lay = pl.BlockSpec((tr, C), lambda i: (l * nrb + i, 0))
    out = jax.ShapeDtypeStruct((NLw * R, C), F32)
    n_prev = 0 if prev is None else 4
    return pl.pallas_call(
        body, name=name, grid=(nrb,),
        in_specs=[lay, pl.BlockSpec((tr, C), lambda i: (i, 0)), lay, lay] + [pl.BlockSpec(memory_space=pl.ANY)] * n_prev,
        out_specs=[lay] * 4, out_shape=[out] * 4,
        input_output_aliases={4 + i: i for i in range(n_prev)},
        compiler_params=_params(("parallel",)),
    )(w, g, m, v, *(prev or ()))


def sum_slots(x, name):
    S, R, C = x.shape
    tr = _tile(R, 128, 8)

    def body(x_ref, o_ref):
        acc = x_ref[0]
        for s in range(1, S):
            acc = acc + x_ref[s]
        o_ref[...] = acc

    return pl.pallas_call(
        body, name=name, grid=(R // tr,),
        in_specs=[pl.BlockSpec((S, tr, C), lambda i: (0, i, 0))],
        out_specs=pl.BlockSpec((tr, C), lambda i: (i, 0)),
        out_shape=jax.ShapeDtypeStruct((R, C), F32),
        compiler_params=_params(("parallel",)),
    )(x)


def sum_halves(g, lands, c_idx, name):
    n, ns, _, rh, D = g.shape

    def body(c_ref, g_ref, l_ref, o_ref):
        for j in range(ns):
            o_ref[0, j] = (g_ref[0, j, 0].astype(F32) + l_ref[0, j].astype(F32)).astype(BF16)

    return pl.pallas_call(
        body, name=name,
        grid_spec=pltpu.PrefetchScalarGridSpec(
            num_scalar_prefetch=1, grid=(n,),
            in_specs=[pl.BlockSpec((1, ns, 1, rh, D), lambda i, c: (i, 0, c[0], 0, 0)),
                      pl.BlockSpec((1, ns, rh, D), lambda i, c: (i, 0, 0, 0))],
            out_specs=pl.BlockSpec((1, ns, rh, D), lambda i, c: (i, 0, 0, 0))),
        out_shape=jax.ShapeDtypeStruct((n, ns, rh, D), BF16),
        compiler_params=_params(("parallel",)),
    )(c_idx, g, lands)


def sum_chips(p, lands, place, name):
    n, ns, rh, D = p.shape

    def body(c_ref, p_ref, l_ref, o_ref):
        acc = p_ref[0, 0].astype(F32)
        for j in range(N_CHIP - 1):
            acc = acc + l_ref[j, 0].astype(F32)
        o_ref[0, 0] = acc

    return pl.pallas_call(
        body, name=name,
        grid_spec=pltpu.PrefetchScalarGridSpec(
            num_scalar_prefetch=1, grid=(n,),
            in_specs=[pl.BlockSpec((1, 1, rh, D), lambda i, c: (i, c[0], 0, 0)),
                      pl.BlockSpec((N_CHIP - 1, 1, rh, D), lambda i, c: (0, i, 0, 0))],
            out_specs=pl.BlockSpec((1, 1, rh, D), lambda i, c: (i, c[1], 0, 0))),
        out_shape=jax.ShapeDtypeStruct((n, 2, rh, D), F32),
        compiler_params=_params(("parallel",)),
    )(place, p, lands)


def _my_place():
    return lax.axis_index("x"), lax.axis_index("y"), lax.axis_index("c")


def _other_chips(mx, my):
    return [(1 - mx, my), (mx, 1 - my), (1 - mx, 1 - my)]


def gather_small(x, after, name):
    def body(x_ref, after_ref, out_ref, sum_ref, send_sems, recv_sems):
        mx, my, mc = _my_place()
        me = 4 * mx + 2 * my + mc
        out_ref[me] = x_ref[...]
        sends = []
        for k in range(1, N_DEV):
            kx, ky, kc = (k >> 2) & 1, (k >> 1) & 1, k & 1
            peer = (1 - mx if kx else mx, 1 - my if ky else my, 1 - mc if kc else mc)
            cp = pltpu.make_async_remote_copy(
                src_ref=x_ref, dst_ref=out_ref.at[me], send_sem=send_sems.at[k - 1], recv_sem=recv_sems.at[k - 1],
                device_id=peer, device_id_type=MESH)
            cp.start()
            sends.append((cp, 4 * peer[0] + 2 * peer[1] + peer[2], peer))
        for k, (cp, peer_slot, peer) in enumerate(sends):
            pltpu.make_async_remote_copy(
                src_ref=x_ref, dst_ref=out_ref.at[peer_slot], send_sem=send_sems.at[k], recv_sem=recv_sems.at[k],
                device_id=peer, device_id_type=MESH).wait_recv()
        for cp, _, _ in sends:
            cp.wait_send()
        acc = out_ref[0]
        for s in range(1, N_DEV):
            acc = acc + out_ref[s]
        sum_ref[...] = acc

    vmem = pl.BlockSpec(memory_space=pltpu.VMEM)
    return pl.pallas_call(
        body, name=name,
        in_specs=[vmem, pl.BlockSpec(memory_space=pl.ANY)], out_specs=[vmem, vmem],
        out_shape=[jax.ShapeDtypeStruct((N_DEV,) + x.shape, x.dtype), jax.ShapeDtypeStruct(x.shape, x.dtype)],
        scratch_shapes=[pltpu.SemaphoreType.DMA((N_DEV - 1,)), pltpu.SemaphoreType.DMA((N_DEV - 1,))],
        compiler_params=pltpu.CompilerParams(vmem_limit_bytes=VMEM_LIMIT),
    )(x, after)


_HBM =pl.BlockSpec(memory_space=pltpu.HBM)
_SEM = pl.BlockSpec(memory_space=pltpu.SEMAPHORE)
_DATAFLOW = pltpu.SideEffectType.DATAFLOW_SIDE_EFFECTING


def _gather_copies(shard, land, send, recv, base):
    mx, my, mc = _my_place()
    ci = 2 * mx + my
    peers = [((cx, cy, mc), 2 * cx + cy) for cx, cy in _other_chips(mx, my)] + [((mx, my, 1 - mc), ci)]
    out = []
    for q, (dev, src_slot) in enumerate(peers):
        out.append((
            pltpu.make_async_remote_copy(src_ref=shard, dst_ref=land.at[:, ci], send_sem=send.at[base + q],
                                         recv_sem=recv.at[base + q], device_id=dev, device_id_type=MESH),
            pltpu.make_async_remote_copy(src_ref=shard, dst_ref=land.at[:, src_slot], send_sem=send.at[base + q],
                                         recv_sem=recv.at[base + q], device_id=dev, device_id_type=MESH)))
    return out


def gather_start(groups, after, name):
    items = [s for g in groups for s in g]
    ni, ng = len(items), len(groups)

    def body(*refs):
        shards, lands = refs[:ni], refs[ni:2 * ni]
        sems = refs[2 * ni + 1:2 * ni + 1 + 2 * ng]
        token = refs[-1]
        i = 0
        for g, grp in enumerate(groups):
            for p in range(len(grp)):
                for start_cp, _ in _gather_copies(shards[i], lands[i], sems[2 * g], sems[2 * g + 1], 4 * p):
                    start_cp.start()
                i += 1
        token[...] = jnp.zeros_like(token)

    sem_shapes = []
    for grp in groups:
        sem_shapes += [pltpu.SemaphoreType.DMA((4 * len(grp),))] * 2
    land_shapes = [(s.shape[0], N_CHIP) + s.shape[1:] for s in items]
    outs = pl.pallas_call(
        body, name=name,
        in_specs=[_HBM] * (2 * ni) + [pl.BlockSpec(memory_space=pl.ANY)],
        out_specs=[_SEM] * (2 * ng) + [_HBM] * (2 * ni) + [pl.BlockSpec(memory_space=pltpu.VMEM)],
        out_shape=(sem_shapes + [pltpu.HBM(s.shape, s.dtype) for s in items]
                   + [pltpu.HBM(ls, s.dtype) for ls, s in zip(land_shapes, items)]
                   + [jax.ShapeDtypeStruct((8, 128), F32)]),
        input_output_aliases={i: 2 * ng + i for i in range(2 * ni)},
        compiler_params=pltpu.CompilerParams(has_side_effects=_DATAFLOW),
    )(*[pltpu.with_memory_space_constraint(s, pltpu.HBM) for s in items],
      *[pltpu.with_memory_space_constraint(lax.empty(ls, s.dtype), pltpu.HBM) for ls, s in zip(land_shapes, items)],
      after)
    sems, thru, token = outs[:2 * ng], outs[2 * ng:2 * ng + 2 * ni], outs[-1]
    handles, i = [], 0
    for g, grp in enumerate(groups):
        n = len(grp)
        handles.append((sems[2 * g], sems[2 * g + 1], thru[i:i + n], thru[ni + i:ni + i + n]))
        i += n
    return handles, token


def gather_wait(handle, after, name):
    send, recv, shards, lands = handle
    n = len(shards)

    def body(*refs):
        shard_refs, land_refs = refs[:n], refs[n:2 * n]
        send_ref, recv_ref = refs[2 * n], refs[2 * n + 1]
        for p in range(n):
            for start_cp, recv_cp in _gather_copies(shard_refs[p], land_refs[p], send_ref, recv_ref, 4 * p):
                start_cp.wait_send()
                recv_cp.wait_recv()

    outs = pl.pallas_call(
        body, name=name,
        in_specs=[_HBM] * (2 * n) + [_SEM, _SEM, pl.BlockSpec(memory_space=pl.ANY)],
        out_specs=[_HBM] * (2 * n),
        out_shape=[pltpu.HBM(s.shape, s.dtype) for s in shards] + [pltpu.HBM(l.shape, l.dtype) for l in lands],
        input_output_aliases={i: i for i in range(2 * n)},
        compiler_params=pltpu.CompilerParams(has_side_effects=_DATAFLOW),
    )(*shards, *lands, send, recv, after)
    return outs[n:]


def _first_copies(shard, land, send, recv):
    mx, my, mc = _my_place()
    ci = 2 * mx + my
    out = []
    for q, (cx, cy) in enumerate(_other_chips(mx, my)):
        dev = (cx, cy, mc)
        out.append(tuple(pltpu.make_async_remote_copy(
            src_ref=shard.at[:, mc], dst_ref=land.at[:, slot, mc], send_sem=send.at[q], recv_sem=recv.at[q],
            device_id=dev, device_id_type=MESH) for slot in (ci, 2 * cx + cy)))
    sib = pltpu.make_async_remote_copy(src_ref=shard, dst_ref=land.at[:, ci], send_sem=send.at[3], recv_sem=recv.at[3],
                                       device_id=(mx, my, 1 - mc), device_id_type=MESH)
    return out + [(sib, sib)]


def _forward_copies(land, send, recv):
    mx, my, mc = _my_place()
    out = []
    for q, (cx, cy) in enumerate(_other_chips(mx, my)):
        out.append(tuple(pltpu.make_async_remote_copy(
            src_ref=land.at[:, 2 * cx + cy, hc], dst_ref=land.at[:, 2 * cx + cy, hc], send_sem=send.at[q],
            recv_sem=recv.at[q], device_id=(mx, my, 1 - mc), device_id_type=MESH) for hc in (mc, 1 - mc)))
    return out


def first_start(shard, after, name):
    def body(shard_ref, land_ref, after_ref, send, recv, shard_thru, land_thru, token):
        for mine, _ in _first_copies(shard_ref, land_ref, send, recv):
            mine.start()
        token[...] = jnp.zeros_like(token)

    land_shape = (shard.shape[0], N_CHIP) + shard.shape[1:]
    outs = pl.pallas_call(
        body, name=name,
        in_specs=[_HBM, _HBM, pl.BlockSpec(memory_space=pl.ANY)],
        out_specs=[_SEM, _SEM, _HBM, _HBM, pl.BlockSpec(memory_space=pltpu.VMEM)],
        out_shape=[pltpu.SemaphoreType.DMA((4,))] * 2 + [pltpu.HBM(shard.shape, shard.dtype),
                                                         pltpu.HBM(land_shape, shard.dtype),
                                                         jax.ShapeDtypeStruct((8, 128), F32)],
        input_output_aliases={0: 2, 1: 3},
        compiler_params=pltpu.CompilerParams(has_side_effects=_DATAFLOW),
    )(pltpu.with_memory_space_constraint(shard, pltpu.HBM),
      pltpu.with_memory_space_constraint(lax.empty(land_shape, shard.dtype), pltpu.HBM), after)
    return outs[:4], outs[4]


def first_forward(handle, after, name):
    send, recv, shard, land = handle

    def body(shard_ref, land_ref, send_ref, recv_ref, after_ref, send2, recv2, shard_thru, land_thru):
        firsts = _first_copies(shard_ref, land_ref, send_ref, recv_ref)
        forwards = _forward_copies(land_ref, send2, recv2)
        for q in range(3):
            firsts[q][1].wait_recv()
            forwards[q][0].start()
        firsts[3][1].wait_recv()
        for mine, _ in firsts:
            mine.wait_send()

    outs = pl.pallas_call(
        body, name=name,
        in_specs=[_HBM, _HBM, _SEM, _SEM, pl.BlockSpec(memory_space=pl.ANY)],
        out_specs=[_SEM, _SEM, _HBM, _HBM],
        out_shape=[pltpu.SemaphoreType.DMA((3,))] * 2 + [pltpu.HBM(shard.shape, shard.dtype),
                                                         pltpu.HBM(land.shape, land.dtype)],
        input_output_aliases={0: 2, 1: 3},
        compiler_params=pltpu.CompilerParams(has_side_effects=_DATAFLOW),
    )(shard, land, send, recv, after)
    return outs[0], outs[1], outs[3]


def first_wait(handle, after, name):
    send, recv, land = handle

    def body(land_ref, send_ref, recv_ref, after_ref, land_out):
        for mine, theirs in _forward_copies(land_ref, send_ref, recv_ref):
            mine.wait_send()
            theirs.wait_recv()

    return pl.pallas_call(
        body, name=name,
        in_specs=[_HBM, _SEM, _SEM, pl.BlockSpec(memory_space=pl.ANY)],
        out_specs=[_HBM],
        out_shape=[pltpu.HBM(land.shape, land.dtype)],
        input_output_aliases={0: 0},
        compiler_params=pltpu.CompilerParams(has_side_effects=_DATAFLOW),
    )(land, send, recv, after)[0]


def _sibling_copies(gs, lands, send, recv):
    mx, my, mc = _my_place()
    return [pltpu.make_async_remote_copy(
        src_ref=gs[k].at[:, :, 1 - mc], dst_ref=lands[k], send_sem=send.at[k], recv_sem=recv.at[k],
        device_id=(mx, my, 1 - mc), device_id_type=MESH) for k in range(len(gs))]


def sibling_start(gs, after, name):
    K = len(gs)

    def body(*refs):
        ins, lands = refs[:K], refs[K:2 * K]
        send, recv = refs[2 * K + 1], refs[2 * K + 2]
        for cp in _sibling_copies(ins, lands, send, recv):
            cp.start()
        refs[-1][...] = jnp.zeros_like(refs[-1])

    land_shapes = [g.shape[:2] + g.shape[3:] for g in gs]
    outs = pl.pallas_call(
        body, name=name,
        in_specs=[_HBM] * (2 * K) + [pl.BlockSpec(memory_space=pl.ANY)],
        out_specs=[_SEM, _SEM] + [_HBM] * (2 * K) + [pl.BlockSpec(memory_space=pltpu.VMEM)],
        out_shape=([pltpu.SemaphoreType.DMA((K,))] * 2 + [pltpu.HBM(g.shape, g.dtype) for g in gs]
                   + [pltpu.HBM(ls, g.dtype) for ls, g in zip(land_shapes, gs)] + [jax.ShapeDtypeStruct((8, 128), F32)]),
        input_output_aliases={i: 2 + i for i in range(2 * K)},
        compiler_params=pltpu.CompilerParams(has_side_effects=_DATAFLOW),
    )(*[pltpu.with_memory_space_constraint(g, pltpu.HBM) for g in gs],
      *[pltpu.with_memory_space_constraint(lax.empty(ls, g.dtype), pltpu.HBM) for ls, g in zip(land_shapes, gs)],
      after)
    return (outs[0], outs[1], outs[2:2 + K], outs[2 + K:2 + 2 * K]), outs[-1]


def sibling_wait(handle, after, name):
    send, recv, gs, lands = handle
    K = len(gs)

    def body(*refs):
        ins, land_refs = refs[:K], refs[K:2 * K]
        for cp in _sibling_copies(ins, land_refs, refs[2 * K], refs[2 * K + 1]):
            cp.wait_send()
            cp.wait_recv()

    outs = pl.pallas_call(
        body, name=name,
        in_specs=[_HBM] * (2 * K) + [_SEM, _SEM, pl.BlockSpec(memory_space=pl.ANY)],
        out_specs=[_HBM] * (2 * K),
        out_shape=[pltpu.HBM(g.shape, g.dtype) for g in gs] + [pltpu.HBM(l.shape, l.dtype) for l in lands],
        input_output_aliases={i: i for i in range(2 * K)},
        compiler_params=pltpu.CompilerParams(has_side_effects=_DATAFLOW),
    )(*gs, *lands, send, recv, after)
    return outs[:K], outs[K:]


def _small_copies(x, land, send, recv):
    mx, my, mc = _my_place()
    me = 4 * mx + 2 * my + mc
    out = []
    for k in range(1, N_DEV):
        peer = (1 - mx if k & 4 else mx, 1 - my if k & 2 else my, 1 - mc if k & 1 else mc)
        slot = 4 * peer[0] + 2 * peer[1] + peer[2]
        out.append(tuple(pltpu.make_async_remote_copy(
            src_ref=x, dst_ref=land.at[s], send_sem=send.at[k - 1], recv_sem=recv.at[k - 1],
            device_id=peer, device_id_type=MESH) for s in (me, slot)))
    return out


def small_start(x, after, name):
    def body(x_ref, land_ref, after_ref, send, recv, x_thru, land_thru, token):
        for mine, _ in _small_copies(x_ref, land_ref, send, recv):
            mine.start()
        token[...] = jnp.zeros_like(token)

    land_shape = (N_DEV,) + x.shape
    outs = pl.pallas_call(
        body, name=name,
        in_specs=[_HBM, _HBM, pl.BlockSpec(memory_space=pl.ANY)],
        out_specs=[_SEM, _SEM, _HBM, _HBM, pl.BlockSpec(memory_space=pltpu.VMEM)],
        out_shape=[pltpu.SemaphoreType.DMA((N_DEV - 1,))] * 2 + [pltpu.HBM(x.shape, x.dtype), pltpu.HBM(land_shape, x.dtype),
                                                                 jax.ShapeDtypeStruct((8, 128), F32)],
        input_output_aliases={0: 2, 1: 3},
        compiler_params=pltpu.CompilerParams(has_side_effects=_DATAFLOW),
    )(pltpu.with_memory_space_constraint(x, pltpu.HBM),
      pltpu.with_memory_space_constraint(lax.empty(land_shape, x.dtype), pltpu.HBM), after)
    return outs[:4], outs[4]


def small_wait(handle, after, name):
    send, recv, x, land = handle

    def body(x_ref, land_ref, send_ref, recv_ref, after_ref, x_out, land_out):
        for mine, theirs in _small_copies(x_ref, land_ref, send_ref, recv_ref):
            mine.wait_send()
            theirs.wait_recv()

    return pl.pallas_call(
        body, name=name,
        in_specs=[_HBM, _HBM, _SEM, _SEM, pl.BlockSpec(memory_space=pl.ANY)],
        out_specs=[_HBM, _HBM],
        out_shape=[pltpu.HBM(x.shape, x.dtype), pltpu.HBM(land.shape, land.dtype)],
        input_output_aliases={0: 0, 1: 1},
        compiler_params=pltpu.CompilerParams(has_side_effects=_DATAFLOW),
    )(x, land, send, recv, after)


def _scatter_copies(ps, lands, send, recv):
    mx, my, mc = _my_place()
    cps = []
    for j, (cx, cy) in enumerate(_other_chips(mx, my)):
        for k in range(len(ps)):
            cps.append(pltpu.make_async_remote_copy(
                src_ref=ps[k].at[:, 2 * cx + cy], dst_ref=lands[k].at[j],
                send_sem=send.at[k * 3 + j], recv_sem=recv.at[k * 3 + j],
                device_id=(cx, cy, mc), device_id_type=MESH))
    return cps


def scatter_start(ps, after, name):
    K = len(ps)

    def body(*refs):
        ins, lands = refs[:K], refs[K:2 * K]
        send, recv = refs[2 * K + 1], refs[2 * K + 2]
        for cp in _scatter_copies(ins, lands, send, recv):
            cp.start()
        refs[-1][...] = jnp.zeros_like(refs[-1])

    land_shapes = [(N_CHIP - 1, p.shape[0]) + p.shape[2:] for p in ps]
    outs = pl.pallas_call(
        body, name=name,
        in_specs=[_HBM] * (2 * K) + [pl.BlockSpec(memory_space=pl.ANY)],
        out_specs=[_SEM, _SEM] + [_HBM] * (2 * K) + [pl.BlockSpec(memory_space=pltpu.VMEM)],
        out_shape=([pltpu.SemaphoreType.DMA((3 * K,))] * 2 + [pltpu.HBM(p.shape, p.dtype) for p in ps]
                   + [pltpu.HBM(ls, p.dtype) for ls, p in zip(land_shapes, ps)] + [jax.ShapeDtypeStruct((8, 128), F32)]),
        input_output_aliases={i: 2 + i for i in range(2 * K)},
        compiler_params=pltpu.CompilerParams(has_side_effects=_DATAFLOW),
    )(*[pltpu.with_memory_space_constraint(p, pltpu.HBM) for p in ps],
      *[pltpu.with_memory_space_constraint(lax.empty(ls, p.dtype), pltpu.HBM) for ls, p in zip(land_shapes, ps)],
      after)
    return (outs[0], outs[1], outs[2:2 + K], outs[2 + K:2 + 2 * K]), outs[-1]


def scatter_wait(handle, after, name):
    send, recv, ps, lands = handle
    K = len(ps)
    afters = list(after) if isinstance(after, (list, tuple)) else [after]

    def body(*refs):
        ins, land_refs = refs[:K], refs[K:2 * K]
        send_ref, recv_ref = refs[2 * K], refs[2 * K + 1]
        for cp in _scatter_copies(ins, land_refs, send_ref, recv_ref):
            cp.wait_send()
            cp.wait_recv()

    outs = pl.pallas_call(
        body, name=name,
        in_specs=[_HBM] * (2 * K) + [_SEM, _SEM] + [pl.BlockSpec(memory_space=pl.ANY)] * len(afters),
        out_specs=[_HBM] * (2 * K),
        out_shape=[pltpu.HBM(p.shape, p.dtype) for p in ps] + [pltpu.HBM(l.shape, l.dtype) for l in lands],
        input_output_aliases={i: i for i in range(2 * K)},
        compiler_params=pltpu.CompilerParams(has_side_effects=_DATAFLOW),
    )(*ps, *lands, send, recv, *afters)
    return outs[:K], outs[K:]


def sibling_complete(ss, name):
    K = len(ss)

    def body(*refs):
        ins, outs = refs[:K], refs[K:2 * K]
        send, recv = refs[2 * K:]
        mx, my, mc = _my_place()
        cps = []
        for k in range(K):
            cp = pltpu.make_async_remote_copy(
                src_ref=ins[k].at[:, mc], dst_ref=outs[k].at[:, mc], send_sem=send.at[k], recv_sem=recv.at[k],
                device_id=(mx, my, 1 - mc), device_id_type=MESH)
            cp.start()
            cps.append(cp)
        for k in range(K):
            pltpu.make_async_remote_copy(
                src_ref=ins[k].at[:, mc], dst_ref=outs[k].at[:, 1 - mc], send_sem=send.at[k], recv_sem=recv.at[k],
                device_id=(mx, my, 1 - mc), device_id_type=MESH).wait_recv()
        for cp in cps:
            cp.wait_send()

    hbm = pl.BlockSpec(memory_space=pl.ANY)
    return pl.pallas_call(
        body, name=name,
        in_specs=[hbm] * K, out_specs=[hbm] * K,
        out_shape=[jax.ShapeDtypeStruct(s.shape, s.dtype) for s in ss],
        scratch_shapes=[pltpu.SemaphoreType.DMA((K,)), pltpu.SemaphoreType.DMA((K,))],
        input_output_aliases={k: k for k in range(K)},
    )(*ss)


def _rope_tables(T):
    inv = ROPE_THETA ** (-jnp.arange(0, ATT_DH, 2, dtype=F32) / ATT_DH)
    ang = jnp.arange(T, dtype=F32)[:, None] * inv[None, :]
    ang = jnp.concatenate([ang, ang, ang, ang], axis=-1)
    return jnp.cos(ang), jnp.sin(ang)


def _ffn_fwd(h, y, mod, i0, get_up, get_down, norm_next, tag):
    wgu = get_up(y)
    a, b, s = ffn_up(y, (wgu, (0,)), (wgu, (1,)), f"ffn_up_{tag}")
    wd = get_down(s)
    outs = resid_matmul([s], (wd, (0,)), h, mod, i0 + 2, 0.5, f"ffn_down_{tag}", norm_next)
    hn, o = outs[0], outs[1]
    return hn, (outs[2] if norm_next else None), (h, y, a, b, s, o), ((wgu, (0,)), (wgu, (1,)), (wd, (0,)))


def _ffn_bwd(dh, do, res, ng, i_n, mod, i0, wgT, wuT, wd, on_grads, next_gate, after, tag):
    h, y, a, b, s, o = res
    F = _wrows(wgT)
    da, db = ffn_bwd_mid(do, wd, a, b, f"ffn_bwd_mid_{tag}", after)
    gbuf = lax.empty((3, F, h.shape[1]), BF16)
    gbuf = matmul_tn(da, y, gbuf, 0, 0, f"dwg_{tag}")
    gbuf = matmul_tn(db, y, gbuf, 1, 0, f"dwu_{tag}")
    gbuf = matmul_tn(s, do, gbuf, 2, 0, f"dwd_{tag}")
    token, then = on_grads([gbuf])
    outs = dy_normbwd([(da, 0, wgT, 0, F), (db, 0, wuT, 0, F)], h, dh, ng, i_n, mod, i0 + 1,
                      f"ffn_bwd_dy_{tag}", next_gate, [token])
    return outs, then


def _mixer_fwd(h, y, mod, w_inT, w_out, sgu, cos, sin, norm_next, tag):
    lng, lnb, sw, swt, bcol = sgu
    proj = matmul_nt(y, w_inT, f"proj_{tag}")
    out_a = sgu_fwd(proj, lng, lnb, sw, bcol, f"sgu_fwd_{tag}")
    qkv = rope_fwd(proj, cos, sin, f"rope_fwd_{tag}")
    npat = len(DILATIONS)
    qkv_res = [tuple(qkv[3 * p:3 * p + 3]) for p in range(npat)]
    os_, lses = [], []
    for d, (qd, kd, vd) in zip(DILATIONS, qkv_res):
        o_d, lse_d = attn_fwd(qd, kd, vd, f"attn_fwd_d{d}_{tag}")
        os_.append(o_d)
        lses.append(lse_d)
    comb = attn_combine(os_, lses, f"attn_combine_{tag}")
    out_b, o_res, lse_res = comb[0], comb[1:1 + npat], comb[1 + npat:]
    outs = resid_matmul([out_a, out_b], w_out, h, mod, 5, 1.0, f"mix_out_{tag}", norm_next)
    hn, om = outs[0], outs[1]
    return hn, (outs[2] if norm_next else None), (h, y, proj, out_a, out_b, o_res, lse_res, qkv_res, om)


def _mixer_bwd(dh, dom, res, ng, mod, w_inT, w_out, sgu, cos, sin, on_grads, next_gate, after, tag):
    lng, lnb, sw, swt, bcol = sgu
    h, y, proj, out_a, out_b, o_res, lse_res, qkv_res, om = res
    D = h.shape[1]
    dmixed = matmul_nt(dom, w_out, f"dmixed_{tag}", after)
    woutbuf = lax.empty((1, 2 * MIX_HALF, D), BF16)
    woutbuf = matmul_tn(out_a, dom, woutbuf, 0, 0, f"dwout_a_{tag}", tmo_cap=MIX_HALF)
    woutbuf = matmul_tn(out_b, dom, woutbuf, 0, MIX_HALF, f"dwout_b_{tag}", tmo_cap=MIX_HALF)
    d_uv, d_sw, d_svec = sgu_bwd(proj, dmixed, lng, lnb, sw, swt, bcol, f"sgu_bwd_{tag}")
    do_res = to_residues(dmixed, 1, f"dout_res_{tag}")
    dqs, dks, dvs = [], [], []
    for p, (d, (qd, kd, vd)) in enumerate(zip(DILATIONS, qkv_res)):
        dq, dk, dv = attn_bwd(qd, kd, vd, do_res[p], o_res[p], lse_res[p], f"attn_bwd_d{d}_{tag}")
        dqs.append(dq)
        dks.append(dk)
        dvs.append(dv)
    d_qkv = rope_bwd(dqs, dks, dvs, cos, sin, f"rope_bwd_{tag}")
    winbuf = lax.empty((1, 5 * MIX_HALF, D), BF16)
    winbuf = matmul_tn(d_uv, y, winbuf, 0, 0, f"dwin_uv_{tag}", tmo_cap=MIX_HALF)
    winbuf = matmul_tn(d_qkv, y, winbuf, 0, 2 * MIX_HALF, f"dwin_qkv_{tag}", tmo_cap=MIX_HALF)
    token, then = on_grads([winbuf, woutbuf])
    pairs = [(d_uv, 0, w_inT, 0, 2 * MIX_HALF), (d_qkv, 0, w_inT, 1, 2 * MIX_HALF), (d_qkv, 2, w_inT, 4, MIX_HALF)]
    outs = dy_normbwd(pairs, h, dh, ng, 1, mod, 4, f"mix_bwd_dy_{tag}", next_gate, [token])
    return outs, d_sw, d_svec, then


def _local_step(x, tgt, mods, ngs, get_w, sgus, gf, on_block_grads, on_layer_small):
    T, D = x.shape
    cos, sin = _rope_tables(T)
    h = x
    saved, weights = [], []
    for l in range(2):
        def getter(blk, l=l):
            return lambda after: get_w(l, blk, after)

        if l == 0:
            y = normmod_fwd(h, ngs[0], 0, mods[0], 0, 1, "normmod_l0f1")
        h, y, r1, wf1 = _ffn_fwd(h, y, mods[l], 0, getter("f1u"), getter("f1d"), (ngs[l], 1, mods[l], 3, 4), f"l{l}f1")
        w_inT, w_out = get_w(l, "mx", h)
        h, y, r2 = _mixer_fwd(h, y, mods[l], (w_inT, (0,)), (w_out, (0,)), sgus[l], cos, sin,
                              (ngs[l], 2, mods[l], 6, 7), f"l{l}mx")
        h, y, r3, wf2 = _ffn_fwd(h, y, mods[l], 6, getter("f2u"), getter("f2d"),
                                 (ngs[l + 1], 0, mods[l + 1], 0, 1) if l + 1 < 2 else None, f"l{l}f2")
        saved.append((r1, r2, r3))
        weights.append((wf1, w_inT, w_out, wf2))
    def gate_of(l, blk):
        r1, r2, r3 = saved[l]
        o, i_g, coef = {"f2": (r3[5], 8, 0.5), "mx": (r2[-1], 5, 1.0), "f1": (r1[5], 2, 0.5)}[blk]
        return o, mods[l], i_g, coef

    seq = [(l, blk) for l in (1, 0) for blk in ("f2", "mx", "f1")]
    dh, red_final, do, red_g = final_loss_bwd(h, gf, tgt, gate_of(*seq[0]), "final_loss_bwd")
    rn, rg = {}, {}
    after = []
    for idx, (l, blk) in enumerate(seq):
        r1, r2, r3 = saved[l]
        wf1, w_inT, w_out, wf2 = weights[l]
        nxt = gate_of(*seq[idx + 1]) if idx + 1 < len(seq) else None
        rg[blk] = red_g
        tag = f"l{l}{blk}"

        def on(arrays, l=l, blk=blk):
            return on_block_grads(l, blk, arrays)

        if blk == "f2":
            outs, then = _ffn_bwd(dh, do, r3, ngs[l], 2, mods[l], 6, *wf2, on, nxt, after, tag)
        elif blk == "mx":
            outs, d_sw, d_svec, then = _mixer_bwd(dh, do, r2, ngs[l], mods[l], (w_inT, (0,)), (w_out, (0,)), sgus[l],
                                                  cos, sin, on, nxt, after, tag)
        else:
            outs, then = _ffn_bwd(dh, do, r1, ngs[l], 0, mods[l], 0, *wf1, on, nxt, after, tag)
        dh, rn[blk] = outs[0], outs[1]
        if nxt is not None:
            do, red_g = outs[2], outs[3]
        if blk == "f1":
            small = on_layer_small(l, dict(sgu_w=d_sw, sgu_vec=d_svec, red_n=(rn["f1"], rn["mx"], rn["f2"]),
                                           red_g=(rg["f1"], rg["mx"], rg["f2"])), red_final if l == 0 else None)
            after = [small, then(small)]
        else:
            after = [then(dh)]
    return dh


def _adam_out(w, g, m, v, name):
    shp = w.shape
    two_d = (-1, shp[-1])
    d, mn, vn = adamw(w.reshape(two_d), g.reshape(two_d), m.reshape(two_d), v.reshape(two_d), name)
    return g, d.reshape(shp), mn.reshape(shp), vn.reshape(shp)


def kernel(x, c, ada_w, ada_b, norm_g, ffn1_wg, ffn1_wu, ffn1_wd, ffn2_wg, ffn2_wu, ffn2_wd, w_in, sgu_ln_g, sgu_ln_b, sgu_w, sgu_b, w_out, final_g, loss_target, m_ada_w, m_ada_b, m_norm_g, m_ffn1_wg, m_ffn1_wu, m_ffn1_wd, m_ffn2_wg, m_ffn2_wu, m_ffn2_wd, m_w_in, m_sgu_ln_g, m_sgu_ln_b, m_sgu_w, m_sgu_b, m_w_out, m_final_g, v_ada_w, v_ada_b, v_norm_g, v_ffn1_wg, v_ffn1_wu, v_ffn1_wd, v_ffn2_wg, v_ffn2_wu, v_ffn2_wd, v_w_in, v_sgu_ln_g, v_sgu_ln_b, v_sgu_w, v_sgu_b, v_w_out, v_final_g):
    T, D = x.shape[1], x.shape[2]
    NL = ada_w.shape[0]
    mx, my, mc = _my_place()
    me = 4 * mx + 2 * my + mc
    ci = 2 * mx + my
    c_idx = jnp.reshape(mc, (1,)).astype(jnp.int32)
    place = jnp.stack([ci, mc]).astype(jnp.int32)

    ngw = norm_g.shape[2]
    small_in = jnp.concatenate([jnp.pad(c, ((0, 7), (0, 0))),
                                jnp.pad(norm_g.reshape(NL * 3, ngw), ((0, 8 - NL * 3), (0, D - ngw)))], axis=0)
    small_all, _ = gather_small(small_in, place, "gather_c_normg")
    c_all = small_all[:, 0, :]
    ng_parts = small_all[0::2, 8:8 + NL * 3, :ngw]
    ngs = jnp.transpose(ng_parts, (1, 0, 2)).reshape(NL, 3, N_CHIP * ngw)

    nmod = ada_w.shape[2]
    ada_b_mine = lax.dynamic_slice_in_dim(ada_b, ci * nmod, nmod, axis=1).reshape(NL, 1, nmod)
    mod_part = ada_fwd(c_all, ada_w, ada_b_mine, "ada_fwd")
    mod_all, _ = gather_small(mod_part.reshape(NL * N_DEV, nmod), place, "gather_mod")
    mod_rows = lax.dynamic_index_in_dim(mod_all.reshape(N_DEV, NL, N_DEV, nmod), me, axis=2, keepdims=False)
    mods = jnp.transpose(mod_rows[0::2], (1, 0, 2)).reshape(NL, N_ADA, D)

    sgus = []
    for l in range(NL):
        sgus.append((sgu_ln_g[l].reshape(1, MIX_HALF), sgu_ln_b[l].reshape(1, MIX_HALF), sgu_w[l],
                     jnp.swapaxes(sgu_w[l], 1, 2), jnp.transpose(sgu_b[l])))

    def halves(a):
        n, r, _ = a.shape
        return a.reshape(n, 2, r // 2, D)

    first_group = halves(jnp.stack([ffn1_wg[0].T, ffn1_wu[0].T], axis=0).astype(BF16))
    first_handle, first_token = first_start(first_group, mods, "first_start")
    zero = first_token[0, 0]
    mods = mods + zero

    def prep(a):
        return (a + zero).astype(BF16)

    groups = []
    for l in range(NL):
        groups += [[halves(jnp.stack([prep(ffn1_wg[l].T), prep(ffn1_wu[l].T)], axis=0))],
                   [halves(prep(ffn1_wd[l])[None])],
                   [halves(prep(w_in[l].T)[None]), halves(prep(w_out[l])[None])],
                   [halves(jnp.stack([prep(ffn2_wg[l].T), prep(ffn2_wu[l].T)], axis=0))],
                   [halves(prep(ffn2_wd[l])[None])]]
    handles, token = gather_start(groups[1:], mods, "gather_start")
    handles = [None] + handles
    mods = mods + token[0, 0]
    group_no = {"f1u": 0, "f1d": 1, "mx": 2, "f2u": 3, "f2d": 4}

    def get_w(l, key, after):
        g = len(group_no) * l + group_no[key]
        if g == 0:
            full = [first_wait(first_forward(first_handle, after, "first_forward"), place, "first_wait")]
        else:
            full = gather_wait(handles[g], after, f"gather_wait_l{l}{key}")
        full = [a.reshape(a.shape[0], N_CHIP * 2 * a.shape[3], D) for a in full]
        return full[0] if key != "mx" else tuple(full)

    def split(a):
        n, r4, _ = a.shape
        return a.reshape(n, N_CHIP, 2, r4 // N_CHIP // 2, D)

    pending, small_pending, small_tokens = {}, {}, {}

    def on_block_grads(l, blk, bufs):
        tag = f"l{l}{blk}"
        sib, tok1 = sibling_start([split(g) for g in bufs], place, f"rs_sibling_start_{tag}")

        def then(after):
            parts, lands = sibling_wait(sib, after, f"rs_sibling_wait_{tag}")
            psums = [sum_halves(g, ld, c_idx, f"rs_sum_halves_{tag}_{i}") for i, (g, ld) in enumerate(zip(parts, lands))]
            pending[(l, blk)], tok2 = scatter_start(psums, lands[0], f"rs_chips_start_{tag}")
            return tok2

        return tok1, then

    def blocks_finish(blocks, after, tag):
        ssums, counts = [], []
        for l, blk in blocks:
            psums, lands2 = scatter_wait(pending.pop((l, blk)), after, f"rs_chips_wait_l{l}{blk}")
            ssums += [sum_chips(p, ld, place, f"rs_sum_chips_l{l}{blk}_{i}") for i, (p, ld) in enumerate(zip(psums, lands2))]
            counts.append(len(psums))
        fins = [f.reshape(f.shape[0], -1, D) for f in sibling_complete(ssums, f"rs_complete_{tag}")]
        out, i = [], 0
        for n in counts:
            out.append(fins[i:i + n])
            i += n
        return out

    def on_layer_small(l, grads, red_final):
        blocks = list(grads["red_n"]) + list(grads["red_g"])
        blocks.append(jnp.pad(grads["sgu_vec"], ((0, 0), (0, D - MIX_HALF))))
        blocks.append(grads["sgu_w"].reshape(-1, D))
        if red_final is not None:
            blocks.append(red_final)
        xs = jnp.concatenate(blocks, axis=0)
        small_pending[l], small_tokens[l] = small_start(xs, place, f"small_start_l{l}")
        return small_tokens[l]

    grad_x = _local_step(x[0], loss_target[0], mods, ngs, get_w, sgus, final_g.reshape(1, D),
                         on_block_grads, on_layer_small)

    adam_state = {}

    def adam_big(nm, l, g, w, m, v):
        adam_state[nm] = adamw_layer(w, g, m, v, l, adam_state.get(nm), f"adamw_{nm}_l{l}")

    def adam_block(l, blk, fin):
        if blk == "mx":
            adam_big("w_in", l, fin[0][0].T, w_in, m_w_in, v_w_in)
            adam_big("w_out", l, fin[1][0], w_out, m_w_out, v_w_out)
        else:
            ws = ((ffn1_wg, m_ffn1_wg, v_ffn1_wg), (ffn1_wu, m_ffn1_wu, v_ffn1_wu), (ffn1_wd, m_ffn1_wd, v_ffn1_wd)) \
                if blk == "f1" else \
                ((ffn2_wg, m_ffn2_wg, v_ffn2_wg), (ffn2_wu, m_ffn2_wu, v_ffn2_wu), (ffn2_wd, m_ffn2_wd, v_ffn2_wd))
            pre = "ffn1" if blk == "f1" else "ffn2"
            for k, (nm, tr) in enumerate((("wg", True), ("wu", True), ("wd", False))):
                adam_big(f"{pre}_{nm}", l, fin[0][k], *[jnp.swapaxes(t, 1, 2) if tr else t for t in ws[k]])

    done_order = [(l, blk) for l in range(NL - 1, -1, -1) for blk in ("f2", "mx", "f1")]
    for (l, blk), fin in zip(done_order[:-1], blocks_finish(done_order[:-1], small_tokens[0], "early")):
        adam_block(l, blk, fin)
    last_big = adam_state["w_out"][1]

    small_sum, small_all = [], []
    for l in range(NL):
        xs, land = small_wait(small_pending[l], last_big, f"small_wait_l{l}")
        full = lax.dynamic_update_slice(land, xs[None], (me, 0, 0))
        small_all.append(full)
        small_sum.append(sum_slots(full, f"small_sum_l{l}"))
    offs = [8 * i for i in range(8)]
    off_final = offs[7] + SGU_HEADS * ATT_BLOCK * HEAD_LANES // D
    loss = small_sum[0][off_final + 1, 0]
    g_final_g = small_sum[0][off_final, :]
    g_norm_g, g_ada_b, g_lng, g_lnb, g_sb, g_sw, dmod_all = [], [], [], [], [], [], []
    for l in range(NL):
        rn = [small_sum[l][offs[i]:offs[i] + 8] for i in range(3)]
        rg = [small_sum[l][offs[3 + i]:offs[3 + i] + 8] for i in range(3)]
        g_norm_g.append(jnp.stack([rn[i][2] for i in range(3)], axis=0))
        g_ada_b.append(jnp.concatenate([jnp.stack([rn[i][0], rn[i][1], rg[i][0]], axis=0) for i in range(3)],
                                       axis=0).reshape(N_ADA * D))
        sv = small_sum[l][offs[6]:offs[6] + 8, :MIX_HALF]
        g_lng.append(sv[0].reshape(SGU_HEADS, HEAD_LANES))
        g_lnb.append(sv[1].reshape(SGU_HEADS, HEAD_LANES))
        g_sb.append(sv[2].reshape(SGU_HEADS, ATT_BLOCK))
        g_sw.append(small_sum[l][offs[7]:off_final].reshape(sgu_w.shape[1:]))
        rows = []
        for i in range(3):
            an = small_all[l][:, offs[i]:offs[i] + 2]
            ag = small_all[l][:, offs[3 + i]:offs[3 + i] + 1]
            rows += [an[:, 0], an[:, 1], ag[:, 0]]
        dmod_all.append(jnp.stack(rows, axis=1).reshape(N_DEV, N_ADA * D))
    dmod_all = jnp.stack(dmod_all, axis=0)
    dmod_mine = lax.dynamic_slice_in_dim(dmod_all, ci * nmod, nmod, axis=2)
    g_ada_w = ada_bwd(jnp.transpose(c_all), dmod_mine, "ada_bwd")
    g_ada_b = jnp.stack(g_ada_b, axis=0)
    g_norm_g_full = jnp.stack(g_norm_g, axis=0)
    g_norm_g_mine = lax.dynamic_slice_in_dim(g_norm_g_full, ci * ngw, ngw, axis=2)

    small_params = [
        ("ada_w", ada_w, g_ada_w, m_ada_w, v_ada_w),
        ("ada_b", ada_b, g_ada_b, m_ada_b, v_ada_b),
        ("norm_g", norm_g, g_norm_g_mine, m_norm_g, v_norm_g),
        ("sgu_ln_g", sgu_ln_g, jnp.stack(g_lng, axis=0), m_sgu_ln_g, v_sgu_ln_g),
        ("sgu_ln_b", sgu_ln_b, jnp.stack(g_lnb, axis=0), m_sgu_ln_b, v_sgu_ln_b),
        ("sgu_w", sgu_w, jnp.stack(g_sw, axis=0), m_sgu_w, v_sgu_w),
        ("sgu_b", sgu_b, jnp.stack(g_sb, axis=0), m_sgu_b, v_sgu_b),
        ("final_g", final_g.reshape(1, D), g_final_g.reshape(1, D), m_final_g.reshape(1, D), v_final_g.reshape(1, D)),
    ]
    for nm, w, g, m, v in small_params:
        res = _adam_out(w, g, m, v, f"adamw_{nm}")
        adam_state[nm] = tuple(t.reshape(D) for t in res) if nm == "final_g" else res

    l, blk = done_order[-1]
    adam_block(l, blk, blocks_finish([(l, blk)], [st[1] for st in adam_state.values()], "last")[0])

    names = ["ada_w", "ada_b", "norm_g", "ffn1_wg", "ffn1_wu", "ffn1_wd", "ffn2_wg", "ffn2_wu", "ffn2_wd", "w_in",
             "sgu_ln_g", "sgu_ln_b", "sgu_w", "sgu_b", "w_out", "final_g"]
    shapes = [t.shape for t in (ada_w, ada_b, norm_g, ffn1_wg, ffn1_wu, ffn1_wd, ffn2_wg, ffn2_wu, ffn2_wd, w_in,
                                sgu_ln_g, sgu_ln_b, sgu_w, sgu_b, w_out, final_g)]
    def shaped(nm, t, s):
        if nm in ("ffn1_wg", "ffn1_wu", "ffn2_wg", "ffn2_wu"):
            return jnp.swapaxes(t.reshape(s[0], s[2], s[1]), 1, 2)
        return t.reshape(s)

    return (loss, grad_x[None], *[shaped(nm, adam_state[nm][i], s) for i in range(4) for nm, s in zip(names, shapes)])
```
